```python
import math
import jax, jax.numpy as jnp
from jax import lax
import numpy as np

D_MODEL = 1024
BATCH = 8
SEQ = 2048
DEPTH = 2

N_MIXERS = 2
N_A = (DEPTH + 1) // 2
N_B = DEPTH // 2
N_SUB = 3
D_FF = 2816
FFN_RES = 0.5
EPS = 1e-6

MLA_HEADS = 16
Q_LORA = 384
KV_LORA = 256
QK_NOPE = 64
QK_ROPE = 32
V_HEAD = 64
ROPE_THETA = 10000.0
Q_BLOCK = 128

DIL_GROUPS = ((128, 1), (512, 4), (2048, 16))
N_GROUPS = 3
DIL_HEADS = 16
DIL_HEAD_DIM = 64
DIL_BLOCK = 128
DIL_WIDTH = DIL_HEADS * DIL_HEAD_DIM

N_BUCKETS = 32
MAX_DISTANCE = 2048

kernel_name = "hybrid_mla_dilated_macaron"


def rmsnorm(x, g):
    xf = x.astype(jnp.float32)
    y = xf * lax.rsqrt(jnp.mean(xf * xf, axis=-1, keepdims=True) + EPS)
    return (y * g.astype(jnp.float32)).astype(x.dtype)


def swiglu(h, w_gate, w_up, w_down):
    return (jax.nn.silu(h @ w_gate) * (h @ w_up)) @ w_down


def rope(x, pos):
    half = x.shape[-1] // 2
    freqs = ROPE_THETA ** (-jnp.arange(half, dtype=jnp.float32) / half)
    ang = pos[:, None] * freqs[None, :]
    cos = jnp.cos(ang)[None, :, None, :]
    sin = jnp.sin(ang)[None, :, None, :]
    x1 = x[..., :half].astype(jnp.float32)
    x2 = x[..., half:].astype(jnp.float32)
    return jnp.concatenate([x1 * cos - x2 * sin, x1 * sin + x2 * cos], axis=-1).astype(x.dtype)


def causal_block_attention(q, k, v, scale):
    B, S, H, dq = q.shape
    nb = S // Q_BLOCK
    qb = q.reshape(B, nb, Q_BLOCK, H, dq).transpose(1, 0, 3, 2, 4)
    kt = k.transpose(0, 2, 1, 3)
    vt = v.transpose(0, 2, 1, 3)
    kpos = jnp.arange(S)

    def one_block(args):
        qi, n = args
        s = jnp.einsum('bhqd,bhkd->bhqk', qi, kt).astype(jnp.float32) * scale
        qpos = n * Q_BLOCK + jnp.arange(Q_BLOCK)
        s = jnp.where(kpos[None, :] <= qpos[:, None], s, -jnp.inf)
        p = jax.nn.softmax(s, axis=-1).astype(vt.dtype)
        return jnp.einsum('bhqk,bhkd->bhqd', p, vt)

    out = lax.map(one_block, (qb, jnp.arange(nb)))
    return out.transpose(1, 0, 3, 2, 4).reshape(B, S, H, v.shape[-1])


def mla(h, w_in, q_norm, w_q_up, kv_norm, w_kv_up, w_o):
    B, S, _ = h.shape
    lat = h @ w_in
    cq = lat[..., :Q_LORA]
    ckv = lat[..., Q_LORA:Q_LORA + KV_LORA]
    k_rope = lat[..., Q_LORA + KV_LORA:][:, :, None, :]
    q = (rmsnorm(cq, q_norm) @ w_q_up).reshape(B, S, MLA_HEADS, QK_NOPE + QK_ROPE)
    kv = (rmsnorm(ckv, kv_norm) @ w_kv_up).reshape(B, S, MLA_HEADS, QK_NOPE + V_HEAD)
    pos = jnp.arange(S, dtype=jnp.float32)
    q = jnp.concatenate([q[..., :QK_NOPE], rope(q[..., QK_NOPE:], pos)], axis=-1)
    k_rope = jnp.broadcast_to(rope(k_rope, pos), (B, S, MLA_HEADS, QK_ROPE))
    k = jnp.concatenate([kv[..., :QK_NOPE], k_rope.astype(kv.dtype)], axis=-1)
    v = kv[..., QK_NOPE:]
    o = causal_block_attention(q, k, v, (QK_NOPE + QK_ROPE) ** -0.5)
    return o.reshape(B, S, MLA_HEADS * V_HEAD) @ w_o


def t5_bucket(dist):
    max_exact = N_BUCKETS // 2
    d = jnp.maximum(dist, 1).astype(jnp.float32)
    large = max_exact + (jnp.log(d / max_exact) / math.log(MAX_DISTANCE / max_exact)
                         * (N_BUCKETS - max_exact)).astype(jnp.int32)
    large = jnp.minimum(large, N_BUCKETS - 1)
    return jnp.where(dist < max_exact, dist, large)


def strided_window_attention(q, k, v, dilation, span, bias_table):
    B, S, H, E = q.shape
    L = S // dilation
    nb = -(-L // DIL_BLOCK)
    Lp = nb * DIL_BLOCK
    qs = q.reshape(B, L, dilation, H, E)
    qb = jnp.pad(qs, ((0, 0), (0, Lp - L), (0, 0), (0, 0), (0, 0))).reshape(B, nb, DIL_BLOCK, dilation, H, E)

    def windows(t):
        tp = jnp.pad(t.reshape(B, L, dilation, H, E),
                     ((0, 0), (DIL_BLOCK, Lp - L), (0, 0), (0, 0), (0, 0)))
        tp = tp.reshape(B, nb + 1, DIL_BLOCK, dilation, H, E)
        return jnp.concatenate([tp[:, :-1], tp[:, 1:]], axis=2)

    kw, vw = windows(k), windows(v)
    s = jnp.einsum('bnqrhe,bnkrhe->bnrhqk', qb, kw).astype(jnp.float32) * (E ** -0.5)
    iq = jnp.arange(DIL_BLOCK)[:, None]
    ik = jnp.arange(2 * DIL_BLOCK)[None, :]
    rel = DIL_BLOCK + iq - ik
    in_window = (rel >= 0) & (rel <= span)
    bucket = t5_bucket(jnp.maximum(rel, 0) * dilation)
    bias = jnp.transpose(bias_table[bucket], (2, 0, 1)).astype(jnp.float32)
    key_m = (jnp.arange(nb)[:, None] - 1) * DIL_BLOCK + jnp.arange(2 * DIL_BLOCK)[None, :]
    valid = in_window[None] & (key_m >= 0)[:, None, :]
    logits = jnp.where(valid[None, :, None, None], s + bias, -jnp.inf)
    lse = jax.nn.logsumexp(logits, axis=-1)
    p = jnp.exp(logits - lse[..., None]).astype(v.dtype)
    o = jnp.einsum('bnrhqk,bnkrhe->bnqrhe', p, vw)
    o = o.reshape(B, Lp, dilation, H, E)[:, :L].reshape(B, S, H, E)
    lse = jnp.transpose(lse, (0, 1, 4, 2, 3)).reshape(B, Lp, dilation, H)[:, :L].reshape(B, S, H)
    return o, lse


def dilated_attention(h, w_in, w_o, rel_bias):
    B, S, _ = h.shape
    proj = (h @ w_in).reshape(B, S, N_GROUPS, 3, DIL_HEADS, DIL_HEAD_DIM)
    outs, lses = [], []
    for g, (window, dilation) in enumerate(DIL_GROUPS):
        o, lse = strided_window_attention(
            proj[:, :, g, 0], proj[:, :, g, 1], proj[:, :, g, 2],
            dilation, window // dilation, rel_bias[:, g * DIL_HEADS:(g + 1) * DIL_HEADS])
        outs.append(o)
        lses.append(lse)
    alpha = jax.nn.softmax(jnp.stack(lses, axis=0), axis=0)
    o = jnp.sum(alpha[..., None] * jnp.stack(outs, axis=0).astype(jnp.float32), axis=0)
    return o.astype(h.dtype).reshape(B, S, DIL_WIDTH) @ w_o


def sandwich(x, fn, pre_g, post_g, shift, scale, gate, res_w):
    hn = rmsnorm(x, pre_g) * (1 + scale[:, None, :]) + shift[:, None, :]
    y = rmsnorm(fn(hn), post_g)
    return x + res_w * gate[:, None, :] * y


def _fwd_setup_inputs(seed: int = 0) -> dict:
    key = jax.random.key(seed)
    ks = jax.random.split(key, 20)
    D = D_MODEL
    nrm = lambda k, shape, fan: jax.random.normal(k, shape, jnp.float32) * fan ** -0.5
    return {
        "x": jax.random.normal(ks[0], (BATCH, SEQ, D), jnp.float32),
        "c": jax.random.normal(ks[1], (BATCH, D), jnp.float32),
        "norm_pre": 1.0 + 0.05 * jax.random.normal(ks[2], (DEPTH, N_SUB, D), jnp.float32),
        "norm_post": 1.0 + 0.05 * jax.random.normal(ks[3], (DEPTH, N_SUB, D), jnp.float32),
        "w_mod": nrm(ks[4], (DEPTH, D, N_SUB * 3 * D), D) * 0.5,
        "b_mod": 0.02 * jax.random.normal(ks[5], (DEPTH, N_SUB * 3 * D), jnp.float32),
        "ffn_w_gate": nrm(ks[6], (DEPTH, 2, D, D_FF), D),
        "ffn_w_up": nrm(ks[7], (DEPTH, 2, D, D_FF), D),
        "ffn_w_down": nrm(ks[8], (DEPTH, 2, D_FF, D), D_FF),
        "mla_w_in": nrm(ks[9], (N_A, D, Q_LORA + KV_LORA + QK_ROPE), D),
        "mla_q_norm": 1.0 + 0.05 * jax.random.normal(ks[10], (N_A, Q_LORA), jnp.float32),
        "mla_w_q_up": nrm(ks[11], (N_A, Q_LORA, MLA_HEADS * (QK_NOPE + QK_ROPE)), Q_LORA),
        "mla_kv_norm": 1.0 + 0.05 * jax.random.normal(ks[12], (N_A, KV_LORA), jnp.float32),
        "mla_w_kv_up": nrm(ks[13], (N_A, KV_LORA, MLA_HEADS * (QK_NOPE + V_HEAD)), KV_LORA),
        "mla_w_o": nrm(ks[14], (N_A, MLA_HEADS * V_HEAD, D), MLA_HEADS * V_HEAD),
        "dil_w_in": nrm(ks[15], (N_B, D, N_GROUPS * 3 * DIL_WIDTH), D),
        "dil_w_o": nrm(ks[16], (N_B, DIL_WIDTH, D), DIL_WIDTH),
        "rel_bias": 0.5 * jax.random.normal(ks[17], (N_BUCKETS, N_GROUPS * DIL_HEADS), jnp.float32),
    }


def _fwd_reference(x, c, norm_pre, norm_post, w_mod, b_mod, ffn_w_gate, ffn_w_up, ffn_w_down,
              mla_w_in, mla_q_norm, mla_w_q_up, mla_kv_norm, mla_w_kv_up, mla_w_o,
              dil_w_in, dil_w_o, rel_bias):
    B = x.shape[0]
    for i in range(DEPTH):
        mod = (jax.nn.silu(c) @ w_mod[i] + b_mod[i]).reshape(B, N_SUB, 3, D_MODEL)

        def ffn_first(h, i=i):
            return swiglu(h, ffn_w_gate[i, 0], ffn_w_up[i, 0], ffn_w_down[i, 0])

        def ffn_second(h, i=i):
            return swiglu(h, ffn_w_gate[i, 1], ffn_w_up[i, 1], ffn_w_down[i, 1])

        if i % N_MIXERS == 0:
            a = i // N_MIXERS
            def mixer(h, a=a):
                return mla(h, mla_w_in[a], mla_q_norm[a], mla_w_q_up[a],
                           mla_kv_norm[a], mla_w_kv_up[a], mla_w_o[a])
        else:
            b = i // N_MIXERS
            def mixer(h, b=b):
                return dilated_attention(h, dil_w_in[b], dil_w_o[b], rel_bias)

        x = sandwich(x, ffn_first, norm_pre[i, 0], norm_post[i, 0],
                     mod[:, 0, 0], mod[:, 0, 1], mod[:, 0, 2], FFN_RES)
        x = sandwich(x, mixer, norm_pre[i, 1], norm_post[i, 1],
                     mod[:, 1, 0], mod[:, 1, 1], mod[:, 1, 2], 1.0)
        x = sandwich(x, ffn_second, norm_pre[i, 2], norm_post[i, 2],
                     mod[:, 2, 0], mod[:, 2, 1], mod[:, 2, 2], FFN_RES)
    return x


import jax as _jax
import jax.numpy as _jnp

TWIN_FORMAT = 'train_step'
FWD_PARAMS = ['x', 'c', 'norm_pre', 'norm_post', 'w_mod', 'b_mod', 'ffn_w_gate', 'ffn_w_up', 'ffn_w_down', 'mla_w_in', 'mla_q_norm', 'mla_w_q_up', 'mla_kv_norm', 'mla_w_kv_up', 'mla_w_o', 'dil_w_in', 'dil_w_o', 'rel_bias']
TWIN_WEIGHTS = ['norm_pre', 'norm_post', 'w_mod', 'b_mod', 'ffn_w_gate', 'ffn_w_up', 'ffn_w_down', 'mla_w_in', 'mla_q_norm', 'mla_w_q_up', 'mla_kv_norm', 'mla_w_kv_up', 'mla_w_o', 'dil_w_in', 'dil_w_o', 'rel_bias']
TWIN_DIFF_INPUT = 'x'
TWIN_INPUTS = ['x', 'c', 'norm_pre', 'norm_post', 'w_mod', 'b_mod', 'ffn_w_gate', 'ffn_w_up', 'ffn_w_down', 'mla_w_in', 'mla_q_norm', 'mla_w_q_up', 'mla_kv_norm', 'mla_w_kv_up', 'mla_w_o', 'dil_w_in', 'dil_w_o', 'rel_bias', 'loss_target', 'm_norm_pre', 'm_norm_post', 'm_w_mod', 'm_b_mod', 'm_ffn_w_gate', 'm_ffn_w_up', 'm_ffn_w_down', 'm_mla_w_in', 'm_mla_q_norm', 'm_mla_w_q_up', 'm_mla_kv_norm', 'm_mla_w_kv_up', 'm_mla_w_o', 'm_dil_w_in', 'm_dil_w_o', 'm_rel_bias', 'v_norm_pre', 'v_norm_post', 'v_w_mod', 'v_b_mod', 'v_ffn_w_gate', 'v_ffn_w_up', 'v_ffn_w_down', 'v_mla_w_in', 'v_mla_q_norm', 'v_mla_w_q_up', 'v_mla_kv_norm', 'v_mla_w_kv_up', 'v_mla_w_o', 'v_dil_w_in', 'v_dil_w_o', 'v_rel_bias']
TWIN_OUTPUTS = ['loss', 'grad_x', 'grad_norm_pre', 'grad_norm_post', 'grad_w_mod', 'grad_b_mod', 'grad_ffn_w_gate', 'grad_ffn_w_up', 'grad_ffn_w_down', 'grad_mla_w_in', 'grad_mla_q_norm', 'grad_mla_w_q_up', 'grad_mla_kv_norm', 'grad_mla_w_kv_up', 'grad_mla_w_o', 'grad_dil_w_in', 'grad_dil_w_o', 'grad_rel_bias', 'delta_norm_pre', 'delta_norm_post', 'delta_w_mod', 'delta_b_mod', 'delta_ffn_w_gate', 'delta_ffn_w_up', 'delta_ffn_w_down', 'delta_mla_w_in', 'delta_mla_q_norm', 'delta_mla_w_q_up', 'delta_mla_kv_norm', 'delta_mla_w_kv_up', 'delta_mla_w_o', 'delta_dil_w_in', 'delta_dil_w_o', 'delta_rel_bias', 'new_m_norm_pre', 'new_m_norm_post', 'new_m_w_mod', 'new_m_b_mod', 'new_m_ffn_w_gate', 'new_m_ffn_w_up', 'new_m_ffn_w_down', 'new_m_mla_w_in', 'new_m_mla_q_norm', 'new_m_mla_w_q_up', 'new_m_mla_kv_norm', 'new_m_mla_w_kv_up', 'new_m_mla_w_o', 'new_m_dil_w_in', 'new_m_dil_w_o', 'new_m_rel_bias', 'new_v_norm_pre', 'new_v_norm_post', 'new_v_w_mod', 'new_v_b_mod', 'new_v_ffn_w_gate', 'new_v_ffn_w_up', 'new_v_ffn_w_down', 'new_v_mla_w_in', 'new_v_mla_q_norm', 'new_v_mla_w_q_up', 'new_v_mla_kv_norm', 'new_v_mla_w_kv_up', 'new_v_mla_w_o', 'new_v_dil_w_in', 'new_v_dil_w_o', 'new_v_rel_bias']
TWIN_LEAF_KINDS = {'loss': 'loss', 'grad_x': 'grad_x', 'grad_norm_pre': 'grad_w', 'grad_norm_post': 'grad_w', 'grad_w_mod': 'grad_w', 'grad_b_mod': 'grad_w', 'grad_ffn_w_gate': 'grad_w', 'grad_ffn_w_up': 'grad_w', 'grad_ffn_w_down': 'grad_w', 'grad_mla_w_in': 'grad_w', 'grad_mla_q_norm': 'grad_w', 'grad_mla_w_q_up': 'grad_w', 'grad_mla_kv_norm': 'grad_w', 'grad_mla_w_kv_up': 'grad_w', 'grad_mla_w_o': 'grad_w', 'grad_dil_w_in': 'grad_w', 'grad_dil_w_o': 'grad_w', 'grad_rel_bias': 'grad_w', 'delta_norm_pre': 'delta_w', 'delta_norm_post': 'delta_w', 'delta_w_mod': 'delta_w', 'delta_b_mod': 'delta_w', 'delta_ffn_w_gate': 'delta_w', 'delta_ffn_w_up': 'delta_w', 'delta_ffn_w_down': 'delta_w', 'delta_mla_w_in': 'delta_w', 'delta_mla_q_norm': 'delta_w', 'delta_mla_w_q_up': 'delta_w', 'delta_mla_kv_norm': 'delta_w', 'delta_mla_w_kv_up': 'delta_w', 'delta_mla_w_o': 'delta_w', 'delta_dil_w_in': 'delta_w', 'delta_dil_w_o': 'delta_w', 'delta_rel_bias': 'delta_w', 'new_m_norm_pre': 'new_m', 'new_m_norm_post': 'new_m', 'new_m_w_mod': 'new_m', 'new_m_b_mod': 'new_m', 'new_m_ffn_w_gate': 'new_m', 'new_m_ffn_w_up': 'new_m', 'new_m_ffn_w_down': 'new_m', 'new_m_mla_w_in': 'new_m', 'new_m_mla_q_norm': 'new_m', 'new_m_mla_w_q_up': 'new_m', 'new_m_mla_kv_norm': 'new_m', 'new_m_mla_w_kv_up': 'new_m', 'new_m_mla_w_o': 'new_m', 'new_m_dil_w_in': 'new_m', 'new_m_dil_w_o': 'new_m', 'new_m_rel_bias': 'new_m', 'new_v_norm_pre': 'new_v', 'new_v_norm_post': 'new_v', 'new_v_w_mod': 'new_v', 'new_v_b_mod': 'new_v', 'new_v_ffn_w_gate': 'new_v', 'new_v_ffn_w_up': 'new_v', 'new_v_ffn_w_down': 'new_v', 'new_v_mla_w_in': 'new_v', 'new_v_mla_q_norm': 'new_v', 'new_v_mla_w_q_up': 'new_v', 'new_v_mla_kv_norm': 'new_v', 'new_v_mla_w_kv_up': 'new_v', 'new_v_mla_w_o': 'new_v', 'new_v_dil_w_in': 'new_v', 'new_v_dil_w_o': 'new_v', 'new_v_rel_bias': 'new_v'}


def _forward(args):
    return _fwd_reference(*[args[k] for k in FWD_PARAMS])


def _output_shape():
    out = _jax.eval_shape(lambda: _forward(_fwd_setup_inputs(0)))
    return out.shape, out.dtype

N_MICROBATCH = 1
ADAM_LR = 0.001
ADAM_B1 = 0.9
ADAM_B2 = 0.999
ADAM_EPS = 1e-08
ADAM_WD = 0.01
ADAM_STEP = 10
PER_EXAMPLE_BATCH_AXIS = {'x': 0, 'c': 0, 'loss_target': 0}
SHARED_INPUTS = []
_WEIGHT_DTYPES = {'norm_pre': _jnp.float32, 'norm_post': _jnp.float32, 'w_mod': _jnp.float32, 'b_mod': _jnp.float32, 'ffn_w_gate': _jnp.float32, 'ffn_w_up': _jnp.float32, 'ffn_w_down': _jnp.float32, 'mla_w_in': _jnp.float32, 'mla_q_norm': _jnp.float32, 'mla_w_q_up': _jnp.float32, 'mla_kv_norm': _jnp.float32, 'mla_w_kv_up': _jnp.float32, 'mla_w_o': _jnp.float32, 'dil_w_in': _jnp.float32, 'dil_w_o': _jnp.float32, 'rel_bias': _jnp.float32}
MOMENT_SCALE = {'norm_pre': 2.712261e-01, 'norm_post': 1.432422e+00, 'w_mod': 1.109271e+00, 'b_mod': 1.873025e+00, 'ffn_w_gate': 4.218479e-02, 'ffn_w_up': 6.169842e-02, 'ffn_w_down': 1.017293e-01, 'mla_w_in': 1.421850e+00, 'mla_q_norm': 8.261709e-02, 'mla_w_q_up': 4.280534e-02, 'mla_kv_norm': 2.258105e+00, 'mla_w_kv_up': 8.487106e-01, 'mla_w_o': 1.164483e+00, 'dil_w_in': 3.224964e-01, 'dil_w_o': 9.308959e-01, 'rel_bias': 1.535482e-01}


def _to_microbatches(a, axis):
    t = _jnp.moveaxis(a, axis, 0)
    t = t.reshape((N_MICROBATCH, t.shape[0] // N_MICROBATCH) + t.shape[1:])
    return _jnp.moveaxis(t, 1, axis + 1)


def setup_inputs(seed: int = 0) -> dict:
    inp = _fwd_setup_inputs(seed)
    key = _jax.random.fold_in(_jax.random.key(seed), 7919)
    shape, _ = _output_shape()
    out = dict(inp)
    out["loss_target"] = _jax.random.normal(_jax.random.fold_in(key, 0), shape, _jnp.float32)
    for i, name in enumerate(TWIN_WEIGHTS):
        w = inp[name].astype(_jnp.float32)
        if MOMENT_SCALE is None:
            s = _jnp.sqrt(_jnp.mean(_jnp.square(w)) + 1e-30)
        else:
            s = MOMENT_SCALE[name]
        km, kv = _jax.random.split(_jax.random.fold_in(key, i + 1))
        out[name] = w
        out["m_" + name] = s * _jax.random.normal(km, w.shape, _jnp.float32)
        out["v_" + name] = (s * s) * _jax.random.uniform(kv, w.shape, _jnp.float32, 0.5, 1.5)
    if N_MICROBATCH > 1:
        for name, axis in PER_EXAMPLE_BATCH_AXIS.items():
            out[name] = _to_microbatches(out[name], axis)
    return {'x': out['x'], 'c': out['c'], 'norm_pre': out['norm_pre'], 'norm_post': out['norm_post'], 'w_mod': out['w_mod'], 'b_mod': out['b_mod'], 'ffn_w_gate': out['ffn_w_gate'], 'ffn_w_up': out['ffn_w_up'], 'ffn_w_down': out['ffn_w_down'], 'mla_w_in': out['mla_w_in'], 'mla_q_norm': out['mla_q_norm'], 'mla_w_q_up': out['mla_w_q_up'], 'mla_kv_norm': out['mla_kv_norm'], 'mla_w_kv_up': out['mla_w_kv_up'], 'mla_w_o': out['mla_w_o'], 'dil_w_in': out['dil_w_in'], 'dil_w_o': out['dil_w_o'], 'rel_bias': out['rel_bias'], 'loss_target': out['loss_target'], 'm_norm_pre': out['m_norm_pre'], 'm_norm_post': out['m_norm_post'], 'm_w_mod': out['m_w_mod'], 'm_b_mod': out['m_b_mod'], 'm_ffn_w_gate': out['m_ffn_w_gate'], 'm_ffn_w_up': out['m_ffn_w_up'], 'm_ffn_w_down': out['m_ffn_w_down'], 'm_mla_w_in': out['m_mla_w_in'], 'm_mla_q_norm': out['m_mla_q_norm'], 'm_mla_w_q_up': out['m_mla_w_q_up'], 'm_mla_kv_norm': out['m_mla_kv_norm'], 'm_mla_w_kv_up': out['m_mla_w_kv_up'], 'm_mla_w_o': out['m_mla_w_o'], 'm_dil_w_in': out['m_dil_w_in'], 'm_dil_w_o': out['m_dil_w_o'], 'm_rel_bias': out['m_rel_bias'], 'v_norm_pre': out['v_norm_pre'], 'v_norm_post': out['v_norm_post'], 'v_w_mod': out['v_w_mod'], 'v_b_mod': out['v_b_mod'], 'v_ffn_w_gate': out['v_ffn_w_gate'], 'v_ffn_w_up': out['v_ffn_w_up'], 'v_ffn_w_down': out['v_ffn_w_down'], 'v_mla_w_in': out['v_mla_w_in'], 'v_mla_q_norm': out['v_mla_q_norm'], 'v_mla_w_q_up': out['v_mla_w_q_up'], 'v_mla_kv_norm': out['v_mla_kv_norm'], 'v_mla_w_kv_up': out['v_mla_w_kv_up'], 'v_mla_w_o': out['v_mla_w_o'], 'v_dil_w_in': out['v_dil_w_in'], 'v_dil_w_o': out['v_dil_w_o'], 'v_rel_bias': out['v_rel_bias']}


def _loss(weights, diff, rest, loss_target):
    with _jax.named_scope("forward"):
        args = {**rest, TWIN_DIFF_INPUT: diff, **{k: w.astype(_WEIGHT_DTYPES[k]) for k, w in weights.items()}}
        y = _forward(args)
    with _jax.named_scope("loss_head"):
        err = _jnp.square(y.astype(_jnp.float32) - loss_target)
        return 0.5 * _jnp.sum(_jnp.mean(err, axis=-1)) if err.ndim else 0.5 * err


def _adamw(w, g, m, v):
    m = ADAM_B1 * m + (1.0 - ADAM_B1) * g
    v = ADAM_B2 * v + (1.0 - ADAM_B2) * _jnp.square(g)
    m_hat = m / (1.0 - ADAM_B1 ** ADAM_STEP)
    v_hat = v / (1.0 - ADAM_B2 ** ADAM_STEP)
    delta = -ADAM_LR * (m_hat / (_jnp.sqrt(v_hat) + ADAM_EPS) + ADAM_WD * w)
    return delta, m, v


def reference(x, c, norm_pre, norm_post, w_mod, b_mod, ffn_w_gate, ffn_w_up, ffn_w_down, mla_w_in, mla_q_norm, mla_w_q_up, mla_kv_norm, mla_w_kv_up, mla_w_o, dil_w_in, dil_w_o, rel_bias, loss_target, m_norm_pre, m_norm_post, m_w_mod, m_b_mod, m_ffn_w_gate, m_ffn_w_up, m_ffn_w_down, m_mla_w_in, m_mla_q_norm, m_mla_w_q_up, m_mla_kv_norm, m_mla_w_kv_up, m_mla_w_o, m_dil_w_in, m_dil_w_o, m_rel_bias, v_norm_pre, v_norm_post, v_w_mod, v_b_mod, v_ffn_w_gate, v_ffn_w_up, v_ffn_w_down, v_mla_w_in, v_mla_q_norm, v_mla_w_q_up, v_mla_kv_norm, v_mla_w_kv_up, v_mla_w_o, v_dil_w_in, v_dil_w_o, v_rel_bias):
    given = dict(x=x, c=c, norm_pre=norm_pre, norm_post=norm_post, w_mod=w_mod, b_mod=b_mod, ffn_w_gate=ffn_w_gate, ffn_w_up=ffn_w_up, ffn_w_down=ffn_w_down, mla_w_in=mla_w_in, mla_q_norm=mla_q_norm, mla_w_q_up=mla_w_q_up, mla_kv_norm=mla_kv_norm, mla_w_kv_up=mla_w_kv_up, mla_w_o=mla_w_o, dil_w_in=dil_w_in, dil_w_o=dil_w_o, rel_bias=rel_bias, loss_target=loss_target, m_norm_pre=m_norm_pre, m_norm_post=m_norm_post, m_w_mod=m_w_mod, m_b_mod=m_b_mod, m_ffn_w_gate=m_ffn_w_gate, m_ffn_w_up=m_ffn_w_up, m_ffn_w_down=m_ffn_w_down, m_mla_w_in=m_mla_w_in, m_mla_q_norm=m_mla_q_norm, m_mla_w_q_up=m_mla_w_q_up, m_mla_kv_norm=m_mla_kv_norm, m_mla_w_kv_up=m_mla_w_kv_up, m_mla_w_o=m_mla_w_o, m_dil_w_in=m_dil_w_in, m_dil_w_o=m_dil_w_o, m_rel_bias=m_rel_bias, v_norm_pre=v_norm_pre, v_norm_post=v_norm_post, v_w_mod=v_w_mod, v_b_mod=v_b_mod, v_ffn_w_gate=v_ffn_w_gate, v_ffn_w_up=v_ffn_w_up, v_ffn_w_down=v_ffn_w_down, v_mla_w_in=v_mla_w_in, v_mla_q_norm=v_mla_q_norm, v_mla_w_q_up=v_mla_w_q_up, v_mla_kv_norm=v_mla_kv_norm, v_mla_w_kv_up=v_mla_w_kv_up, v_mla_w_o=v_mla_w_o, v_dil_w_in=v_dil_w_in, v_dil_w_o=v_dil_w_o, v_rel_bias=v_rel_bias)
    weights = {n: given[n] for n in TWIN_WEIGHTS}
    shared = {n: given[n] for n in SHARED_INPUTS}
    per_example = {n: given[n] for n in ['x', 'c']}
    grad_fn = _jax.value_and_grad(_loss, argnums=(0, 1))

    def one_microbatch(ex, loss_target):
        ex = dict(ex)
        diff = ex.pop(TWIN_DIFF_INPUT)
        return grad_fn(weights, diff, {**shared, **ex}, loss_target)

    if N_MICROBATCH == 1:
        loss, (grad_w, grad_x) = one_microbatch(per_example, given["loss_target"])
    else:
        def body(carry, xs):
            loss_sum, grad_sum = carry
            l_k, (gw_k, gx_k) = one_microbatch(xs[0], xs[1])
            with _jax.named_scope("update"):
                return (loss_sum + l_k, _jax.tree.map(_jnp.add, grad_sum, gw_k)), gx_k

        init = (_jnp.zeros((), _jnp.float32), _jax.tree.map(_jnp.zeros_like, weights))
        (loss, grad_w), grad_x = _jax.lax.scan(body, init, (per_example, given["loss_target"]))
    with _jax.named_scope("update"):
        delta_w, new_m, new_v = {}, {}, {}
        for n in TWIN_WEIGHTS:
            delta_w[n], new_m[n], new_v[n] = _adamw(weights[n], grad_w[n], given["m_" + n], given["v_" + n])
    return (loss, grad_x, *[grad_w[n] for n in TWIN_WEIGHTS], *[delta_w[n] for n in TWIN_WEIGHTS],
            *[new_m[n] for n in TWIN_WEIGHTS], *[new_v[n] for n in TWIN_WEIGHTS])
```

```python
import math

import jax
import jax.numpy as jnp
from jax import lax
from jax.experimental import pallas as pl
from jax.experimental.pallas import tpu as pltpu

F32 = jnp.float32
BF16 = jnp.bfloat16
MESH = pl.DeviceIdType.MESH

SEQ = 2048
D_MODEL = 1024
D_FF = 2816
N_SHARD = 4
F_SHARD = D_FF // N_SHARD
EPS = 1e-6
FFN_RES = 0.5
HEADS = 16
Q_LORA, KV_LORA, QK_NOPE, QK_ROPE, V_HEAD = 384, 256, 64, 32, 64
HALF_ROPE = QK_ROPE // 2
ROPE_THETA = 10000.0
DIL_GROUPS = ((128, 1), (512, 4), (2048, 16))
DIL_BLOCK = 128
N_BUCKETS = 32
MAX_DISTANCE = 2048
ADAM_LR, ADAM_B1, ADAM_B2, ADAM_EPS, ADAM_WD, ADAM_STEP = 0.001, 0.9, 0.999, 1e-08, 0.01, 10

VMEM_LIMIT = 48 * 1024 * 1024
SMALL_ROWS = 40

WEIGHTS = ['norm_pre', 'norm_post', 'w_mod', 'b_mod', 'ffn_w_gate', 'ffn_w_up', 'ffn_w_down', 'mla_w_in',
           'mla_q_norm', 'mla_w_q_up', 'mla_kv_norm', 'mla_w_kv_up', 'mla_w_o', 'dil_w_in', 'dil_w_o', 'rel_bias']


def _cparams(**kw):
    return pltpu.CompilerParams(vmem_limit_bytes=VMEM_LIMIT, **kw)


def _pick(n, cap, mult=128):
    if n <= cap:
        return n
    best = n
    for t in range(mult, cap + 1, mult):
        if n % t == 0:
            best = t
    return best


def _mm(a, b, *, name, ta=False, tb=False, reduce_g=False, bias=None, out_dtype=F32, tm_cap=512, tn_cap=1024,
        g_n=None, b_sel=None, out_shape=None, out_sel=None, out_buf=None):
    a3 = a if a.ndim == 3 else a[None]
    ga = a3.shape[0]
    if b_sel is None:
        b_n = b if b.ndim == 3 else b[None]
        gb = b_n.shape[0]
        b_sel = (lambda g: (g,)) if gb > 1 else (lambda g: (0,))
        g_n = max(ga, gb)
    else:
        b_n = b
    k_dim, m_dim = (a3.shape[1], a3.shape[2]) if ta else (a3.shape[2], a3.shape[1])
    k2, n_dim = (b_n.shape[-1], b_n.shape[-2]) if tb else (b_n.shape[-2], b_n.shape[-1])
    assert k_dim == k2, (a.shape, b.shape)
    tm = _pick(m_dim, tm_cap, 128 if ta else 8)
    tn = _pick(n_dim, tn_cap, 128)
    mt, nt = m_dim // tm, n_dim // tn
    dims = (((0 if ta else 1,), (1 if tb else 0,)), ((), ()))

    if reduce_g:
        grid = (mt, nt, g_n)
        ids = lambda i, j, g: (g, i, j)
    else:
        grid = (g_n, mt, nt)
        ids = lambda g, i, j: (g, i, j)

    def a_map(*p):
        g, i, j = ids(*p)
        g = g if ga > 1 else 0
        return (g, 0, i) if ta else (g, i, 0)

    def b_map(*p):
        g, i, j = ids(*p)
        return (*b_sel(g), j, 0) if tb else (*b_sel(g), 0, j)

    b_lead = (None,) * (b_n.ndim - 2)
    a_spec = pl.BlockSpec((None, k_dim, tm) if ta else (None, tm, k_dim), a_map)
    b_spec = pl.BlockSpec(b_lead + ((tn, k_dim) if tb else (k_dim, tn)), b_map)
    in_specs = [a_spec, b_spec]
    operands = [a3, b_n]
    if bias is not None:
        assert not reduce_g and bias.shape == (g_n, 1, n_dim)
        in_specs.append(pl.BlockSpec((None, 1, tn), lambda g, i, j: (g, 0, j)))
        operands.append(bias)
    aliases = {}
    if out_buf is not None:
        assert tuple(out_buf.shape) == tuple(out_shape) and out_buf.dtype == out_dtype
        in_specs.append(pl.BlockSpec(memory_space=pl.ANY))
        operands.append(out_buf)
        aliases = {len(operands) - 1: 0}

    if reduce_g:
        out_spec = pl.BlockSpec((tm, tn), lambda i, j, g: (i, j))
        out_sds = jax.ShapeDtypeStruct((m_dim, n_dim), F32)
    elif out_shape is not None:
        def o_map(g, i, j):
            lead, rb = out_sel(g, i)
            return (*lead, rb, j)

        out_spec = pl.BlockSpec((None,) * (len(out_shape) - 2) + (tm, tn), o_map)
        out_sds = jax.ShapeDtypeStruct(tuple(out_shape), out_dtype)
    else:
        out_spec = pl.BlockSpec((None, tm, tn), lambda g, i, j: (g, i, j))
        out_sds = jax.ShapeDtypeStruct((g_n, m_dim, n_dim), out_dtype)

    def body(a_ref, b_ref, *rest):
        o_ref = rest[-1]
        r = lax.dot_general(a_ref[...].astype(BF16), b_ref[...].astype(BF16), dims, preferred_element_type=F32)
        if bias is not None:
            r = r + rest[0][...]
        if reduce_g:
            g = pl.program_id(2)

            @pl.when(g == 0)
            def _():
                o_ref[...] = r

            @pl.when(g > 0)
            def _():
                o_ref[...] += r
        else:
            o_ref[...] = r.astype(o_ref.dtype)

    out = pl.pallas_call(body, grid=grid, in_specs=in_specs, out_specs=out_spec, out_shape=out_sds,
                         input_output_aliases=aliases, compiler_params=_cparams(), name=name)(*operands)
    if not reduce_g and out_shape is None and a.ndim == 2 and b.ndim == 2:
        out = out[0]
    return out


def _rows(tm, w):
    return pl.BlockSpec((tm, w), lambda i: (i, 0))


def _vec(w):
    return pl.BlockSpec((1, w), lambda i: (0, 0))


def _rstd(v):
    return lax.rsqrt(jnp.mean(v * v, axis=-1, keepdims=True) + EPS)


def _pre_fwd(x, pg, sc, sh, *, name):
    s_n, w = x.shape
    tm = _pick(s_n, 256, 8)

    def body(x_ref, pg_ref, sc_ref, sh_ref, o_ref):
        xv = x_ref[...]
        n = (xv * _rstd(xv)) * pg_ref[...]
        o_ref[...] = (n * (1.0 + sc_ref[...]) + sh_ref[...]).astype(o_ref.dtype)

    return pl.pallas_call(body, grid=(s_n // tm,), in_specs=[_rows(tm, w), _vec(w), _vec(w), _vec(w)],
                          out_specs=_rows(tm, w), out_shape=jax.ShapeDtypeStruct((s_n, w), BF16),
                          compiler_params=_cparams(), name=name)(x, pg, sc, sh)


def _post_fwd(f, x, qg, gate, res_w, *, name):
    s_n, w = x.shape
    tm = _pick(s_n, 256, 8)

    def body(f_ref, x_ref, qg_ref, gate_ref, o_ref):
        fv = f_ref[...]
        y = (fv * _rstd(fv)) * qg_ref[...]
        o_ref[...] = x_ref[...] + (res_w * gate_ref[...]) * y

    return pl.pallas_call(body, grid=(s_n // tm,), in_specs=[_rows(tm, w), _rows(tm, w), _vec(w), _vec(w)],
                          out_specs=_rows(tm, w), out_shape=jax.ShapeDtypeStruct((s_n, w), F32),
                          compiler_params=_cparams(), name=name)(f, x, qg, gate)


def _post_bwd(dout, f, qg, gate, res_w, *, name):
    s_n, w = f.shape
    tm = _pick(s_n, 256, 8)

    def body(do_ref, f_ref, qg_ref, gate_ref, df_ref, dgate_ref, dqg_ref):
        @pl.when(pl.program_id(0) == 0)
        def _():
            dgate_ref[...] = jnp.zeros_like(dgate_ref)
            dqg_ref[...] = jnp.zeros_like(dqg_ref)

        do = do_ref[...]
        fv = f_ref[...]
        r = _rstd(fv)
        fh = fv * r
        qg_v = qg_ref[...]
        dgate_ref[...] += res_w * jnp.sum(do * (fh * qg_v), axis=0, keepdims=True)
        dy = do * (res_w * gate_ref[...])
        dqg_ref[...] += jnp.sum(dy * fh, axis=0, keepdims=True)
        dfh = dy * qg_v
        df = r * (dfh - fh * jnp.mean(dfh * fh, axis=-1, keepdims=True))
        df_ref[...] = df.astype(df_ref.dtype)

    return pl.pallas_call(
        body, grid=(s_n // tm,), in_specs=[_rows(tm, w), _rows(tm, w), _vec(w), _vec(w)],
        out_specs=[_rows(tm, w), _vec(w), _vec(w)],
        out_shape=[jax.ShapeDtypeStruct((s_n, w), BF16), jax.ShapeDtypeStruct((1, w), F32),
                   jax.ShapeDtypeStruct((1, w), F32)],
        compiler_params=_cparams(), name=name)(dout, f, qg, gate)


def _pre_bwd(dhn, x, dout, pg, sc, *, name):
    s_n, w = x.shape
    tm = _pick(s_n, 256, 8)

    def body(dhn_ref, x_ref, do_ref, pg_ref, sc_ref, dx_ref, dsh_ref, dsc_ref, dpg_ref):
        @pl.when(pl.program_id(0) == 0)
        def _():
            dsh_ref[...] = jnp.zeros_like(dsh_ref)
            dsc_ref[...] = jnp.zeros_like(dsc_ref)
            dpg_ref[...] = jnp.zeros_like(dpg_ref)

        dhn_v = dhn_ref[...]
        xv = x_ref[...]
        r = _rstd(xv)
        xh = xv * r
        pg_v = pg_ref[...]
        dsh_ref[...] += jnp.sum(dhn_v, axis=0, keepdims=True)
        dsc_ref[...] += jnp.sum(dhn_v * (xh * pg_v), axis=0, keepdims=True)
        dn = dhn_v * (1.0 + sc_ref[...])
        dpg_ref[...] += jnp.sum(dn * xh, axis=0, keepdims=True)
        dxh = dn * pg_v
        dx_ref[...] = do_ref[...] + r * (dxh - xh * jnp.mean(dxh * xh, axis=-1, keepdims=True))

    vec = jax.ShapeDtypeStruct((1, w), F32)
    return pl.pallas_call(
        body, grid=(s_n // tm,), in_specs=[_rows(tm, w), _rows(tm, w), _rows(tm, w), _vec(w), _vec(w)],
        out_specs=[_rows(tm, w), _vec(w), _vec(w), _vec(w)],
        out_shape=[jax.ShapeDtypeStruct((s_n, w), F32), vec, vec, vec],
        compiler_params=_cparams(), name=name)(dhn, x, dout, pg, sc)


def _rms_fwd(x, g, *, name):
    s_n, w = x.shape
    tm = _pick(s_n, 512, 8)

    def body(x_ref, g_ref, o_ref):
        xv = x_ref[...]
        o_ref[...] = ((xv * _rstd(xv)) * g_ref[...]).astype(o_ref.dtype)

    return pl.pallas_call(body, grid=(s_n // tm,), in_specs=[_rows(tm, w), _vec(w)], out_specs=_rows(tm, w),
                          out_shape=jax.ShapeDtypeStruct((s_n, w), BF16), compiler_params=_cparams(),
                          name=name)(x, g)


def _rms_bwd(dy, x, g, *, name):
    s_n, w = x.shape
    tm = _pick(s_n, 512, 8)

    def body(dy_ref, x_ref, g_ref, dx_ref, dg_ref):
        @pl.when(pl.program_id(0) == 0)
        def _():
            dg_ref[...] = jnp.zeros_like(dg_ref)

        dy_v = dy_ref[...]
        xv = x_ref[...]
        r = _rstd(xv)
        xh = xv * r
        dg_ref[...] += jnp.sum(dy_v * xh, axis=0, keepdims=True)
        dxh = dy_v * g_ref[...]
        dx_ref[...] = r * (dxh - xh * jnp.mean(dxh * xh, axis=-1, keepdims=True))

    return pl.pallas_call(
        body, grid=(s_n // tm,), in_specs=[_rows(tm, w), _rows(tm, w), _vec(w)],
        out_specs=[_rows(tm, w), _vec(w)],
        out_shape=[jax.ShapeDtypeStruct((s_n, w), F32), jax.ShapeDtypeStruct((1, w), F32)],
        compiler_params=_cparams(), name=name)(dy, x, g)


def _rope(a1, a2, cos, sin, *, name):
    s_n, w = a1.shape
    tm = _pick(s_n, 512, 8)

    def body(a1_ref, a2_ref, c_ref, s_ref, r1_ref, r2_ref):
        u, v, c_v, s_v = a1_ref[...], a2_ref[...], c_ref[...], s_ref[...]
        r1_ref[...] = u * c_v - v * s_v
        r2_ref[...] = u * s_v + v * c_v

    sd = jax.ShapeDtypeStruct((s_n, w), F32)
    return pl.pallas_call(body, grid=(s_n // tm,), in_specs=[_rows(tm, w)] * 4, out_specs=[_rows(tm, w)] * 2,
                          out_shape=[sd, sd], compiler_params=_cparams(), name=name)(a1, a2, cos, sin)


def _silu_bf16(x, *, name):
    def body(x_ref, o_ref):
        xv = x_ref[...]
        o_ref[...] = (xv * jax.nn.sigmoid(xv)).astype(o_ref.dtype)

    return pl.pallas_call(body, out_shape=jax.ShapeDtypeStruct(x.shape, BF16), name=name)(x)


def _loss(y, target, *, name):
    s_n, w = y.shape
    tm = _pick(s_n, 256, 8)

    def body(y_ref, t_ref, dy_ref, l_ref):
        @pl.when(pl.program_id(0) == 0)
        def _():
            l_ref[...] = jnp.zeros_like(l_ref)

        e = y_ref[...] - t_ref[...]
        dy_ref[...] = e * (1.0 / w)
        row = jnp.mean(e * e, axis=-1, keepdims=True)
        l_ref[...] += 0.5 * jnp.sum(row, axis=0, keepdims=True)

    return pl.pallas_call(
        body, grid=(s_n // tm,), in_specs=[_rows(tm, w), _rows(tm, w)],
        out_specs=[_rows(tm, w), pl.BlockSpec((1, 1), lambda i: (0, 0))],
        out_shape=[jax.ShapeDtypeStruct((s_n, w), F32), jax.ShapeDtypeStruct((1, 1), F32)],
        compiler_params=_cparams(), name=name)(y, target)


def _swiglu_fwd(gu, *, name):
    _, s_n, f = gu.shape
    tm = _pick(s_n, 512, 8)

    def body(g_ref, u_ref, a_ref):
        g = g_ref[...]
        a_ref[...] = ((g * jax.nn.sigmoid(g)) * u_ref[...]).astype(a_ref.dtype)

    return pl.pallas_call(
        body, grid=(N_SHARD, s_n // tm),
        in_specs=[pl.BlockSpec((None, tm, f), lambda s, i: (s, i, 0)),
                  pl.BlockSpec((None, tm, f), lambda s, i: (s + N_SHARD, i, 0))],
        out_specs=pl.BlockSpec((None, tm, f), lambda s, i: (s, i, 0)),
        out_shape=jax.ShapeDtypeStruct((N_SHARD, s_n, f), BF16), compiler_params=_cparams(), name=name)(gu, gu)


def _swiglu_bwd(da, gu, *, name):
    _, s_n, f = gu.shape
    tm = _pick(s_n, 512, 8)

    def body(da_ref, g_ref, u_ref, o_ref):
        g = g_ref[...]
        da_v = da_ref[...]
        sig = jax.nn.sigmoid(g)
        dg = da_v * u_ref[...] * (sig * (1.0 + g * (1.0 - sig)))
        du = da_v * (g * sig)
        o_ref[...] = jnp.where(pl.program_id(0) < N_SHARD, dg, du).astype(o_ref.dtype)

    return pl.pallas_call(
        body, grid=(2 * N_SHARD, s_n // tm),
        in_specs=[pl.BlockSpec((None, tm, f), lambda k, i: (k % N_SHARD, i, 0)),
                  pl.BlockSpec((None, tm, f), lambda k, i: (k % N_SHARD, i, 0)),
                  pl.BlockSpec((None, tm, f), lambda k, i: (k % N_SHARD + N_SHARD, i, 0))],
        out_specs=pl.BlockSpec((None, tm, f), lambda k, i: (k, i, 0)),
        out_shape=jax.ShapeDtypeStruct((2 * N_SHARD, s_n, f), BF16), compiler_params=_cparams(),
        name=name)(da, gu, gu)


_NT = (((1,), (1,)), ((), ()))
_TN = (((0,), (0,)), ((), ()))
MLA_TQ = 256


def _causal_mask(i, tq, s_n):
    qpos = i * tq + lax.broadcasted_iota(jnp.int32, (tq, s_n), 0)
    kpos = lax.broadcasted_iota(jnp.int32, (tq, s_n), 1)
    return kpos <= qpos


def _mla_attn_fwd(q, k, v, *, name):
    h_n, s_n, dq = q.shape
    dv = v.shape[-1]
    tq = MLA_TQ
    scale = float(dq) ** -0.5

    def body(q_ref, k_ref, v_ref, o_ref, lse_ref):
        mask = _causal_mask(pl.program_id(1), tq, s_n)
        s = lax.dot_general(q_ref[...], k_ref[...], _NT, preferred_element_type=F32) * scale
        s = jnp.where(mask, s, -jnp.inf)
        m = jnp.max(s, axis=-1, keepdims=True)
        p = jnp.exp(s - m)
        l = jnp.sum(p, axis=-1, keepdims=True)
        o = jnp.dot(p.astype(BF16), v_ref[...], preferred_element_type=F32)
        o_ref[...] = o / l
        lse_ref[...] = m + jnp.log(l)

    return pl.pallas_call(
        body, grid=(h_n, s_n // tq),
        in_specs=[pl.BlockSpec((None, tq, dq), lambda h, i: (h, i, 0)),
                  pl.BlockSpec((None, s_n, dq), lambda h, i: (h, 0, 0)),
                  pl.BlockSpec((None, s_n, dv), lambda h, i: (h, 0, 0))],
        out_specs=[pl.BlockSpec((None, tq, dv), lambda h, i: (h, i, 0)),
                   pl.BlockSpec((None, tq, 1), lambda h, i: (h, i, 0))],
        out_shape=[jax.ShapeDtypeStruct((h_n, s_n, dv), F32), jax.ShapeDtypeStruct((h_n, s_n, 1), F32)],
        compiler_params=_cparams(), name=name)(q, k, v)


def _mla_attn_bwd(q, k, v, o, do, lse, *, name):
    h_n, s_n, dq = q.shape
    dv = v.shape[-1]
    tq = MLA_TQ
    scale = float(dq) ** -0.5

    def body(q_ref, k_ref, v_ref, o_ref, do_ref, lse_ref, dq_ref, dk_ref, dv_ref):
        i = pl.program_id(1)

        @pl.when(i == 0)
        def _():
            dk_ref[...] = jnp.zeros_like(dk_ref)
            dv_ref[...] = jnp.zeros_like(dv_ref)

        mask = _causal_mask(i, tq, s_n)
        qv, kv, vv = q_ref[...], k_ref[...], v_ref[...]
        do_v = do_ref[...]
        s = lax.dot_general(qv, kv, _NT, preferred_element_type=F32) * scale
        p = jnp.where(mask, jnp.exp(s - lse_ref[...]), 0.0)
        dob = do_v.astype(BF16)
        dv_ref[...] += lax.dot_general(p.astype(BF16), dob, _TN, preferred_element_type=F32)
        dp = lax.dot_general(dob, vv, _NT, preferred_element_type=F32)
        delta = jnp.sum(do_v * o_ref[...], axis=-1, keepdims=True)
        dsb = (p * (dp - delta) * scale).astype(BF16)
        dq_ref[...] = jnp.dot(dsb, kv, preferred_element_type=F32)
        dk_ref[...] += lax.dot_general(dsb, qv, _TN, preferred_element_type=F32)

    return pl.pallas_call(
        body, grid=(h_n, s_n // tq),
        in_specs=[pl.BlockSpec((None, tq, dq), lambda h, i: (h, i, 0)),
                  pl.BlockSpec((None, s_n, dq), lambda h, i: (h, 0, 0)),
                  pl.BlockSpec((None, s_n, dv), lambda h, i: (h, 0, 0)),
                  pl.BlockSpec((None, tq, dv), lambda h, i: (h, i, 0)),
                  pl.BlockSpec((None, tq, dv), lambda h, i: (h, i, 0)),
                  pl.BlockSpec((None, tq, 1), lambda h, i: (h, i, 0))],
        out_specs=[pl.BlockSpec((None, tq, dq), lambda h, i: (h, i, 0)),
                   pl.BlockSpec((None, s_n, dq), lambda h, i: (h, 0, 0)),
                   pl.BlockSpec((None, s_n, dv), lambda h, i: (h, 0, 0))],
        out_shape=[jax.ShapeDtypeStruct((h_n, s_n, dq), F32), jax.ShapeDtypeStruct((h_n, s_n, dq), F32),
                   jax.ShapeDtypeStruct((h_n, s_n, dv), F32)],
        compiler_params=_cparams(), name=name)(q, k, v, o, do, lse)


def _head_sum(x, *, name):
    h_n, s_n, w = x.shape
    tm = _pick(s_n, 256, 8)

    def body(x_ref, o_ref):
        o_ref[...] = jnp.sum(x_ref[...], axis=0)

    return pl.pallas_call(body, grid=(s_n // tm,), in_specs=[pl.BlockSpec((h_n, tm, w), lambda i: (0, i, 0))],
                          out_specs=_rows(tm, w), out_shape=jax.ShapeDtypeStruct((s_n, w), F32),
                          compiler_params=_cparams(), name=name)(x)


N_BLK = SEQ // DIL_BLOCK
DIL_SCALE = 64 ** -0.5


def _dil_masks():
    iq = lax.broadcasted_iota(jnp.int32, (DIL_BLOCK, 2 * DIL_BLOCK), 0)
    ik = lax.broadcasted_iota(jnp.int32, (DIL_BLOCK, 2 * DIL_BLOCK), 1)
    rel = DIL_BLOCK + iq - ik
    both = (rel >= 0) & (rel <= DIL_BLOCK)
    iq1 = lax.broadcasted_iota(jnp.int32, (DIL_BLOCK, DIL_BLOCK), 0)
    ik1 = lax.broadcasted_iota(jnp.int32, (DIL_BLOCK, DIL_BLOCK), 1)
    return both, ik1 <= iq1


def _dil_block(j, nb):
    lo = j * DIL_BLOCK
    first = j % nb == 0
    k_lo = lo if first else lo - DIL_BLOCK
    b_lo = DIL_BLOCK if first else 0
    return lo, k_lo, b_lo, first


def _dil_attn_fwd(q, k, v, bias, nb, *, name):
    h_n, s_n, e = q.shape

    def body(q_ref, k_ref, v_ref, b_ref, o_ref, lse_ref):
        m_both, m_first = _dil_masks()
        for j in range(N_BLK):
            lo, k_lo, b_lo, first = _dil_block(j, nb)
            qj = q_ref[lo:lo + DIL_BLOCK, :]
            kk = k_ref[k_lo:lo + DIL_BLOCK, :]
            vv = v_ref[k_lo:lo + DIL_BLOCK, :]
            s = lax.dot_general(qj, kk, _NT, preferred_element_type=F32) * DIL_SCALE + b_ref[:, b_lo:]
            s = jnp.where(m_first if first else m_both, s, -jnp.inf)
            m = jnp.max(s, axis=-1, keepdims=True)
            lse = m + jnp.log(jnp.sum(jnp.exp(s - m), axis=-1, keepdims=True))
            p = jnp.exp(s - lse)
            o_ref[lo:lo + DIL_BLOCK, :] = jnp.dot(p.astype(BF16), vv, preferred_element_type=F32)
            lse_ref[lo:lo + DIL_BLOCK, :] = lse

    head = lambda w: pl.BlockSpec((None, s_n, w), lambda h: (h, 0, 0))
    return pl.pallas_call(
        body, grid=(h_n,),
        in_specs=[head(e), head(e), head(e), pl.BlockSpec((None, DIL_BLOCK, 2 * DIL_BLOCK), lambda h: (h, 0, 0))],
        out_specs=[head(e), head(1)],
        out_shape=[jax.ShapeDtypeStruct((h_n, s_n, e), F32), jax.ShapeDtypeStruct((h_n, s_n, 1), F32)],
        compiler_params=_cparams(), name=name)(q, k, v, bias)


def _dil_attn_bwd(q, k, v, bias, lse, do, dlt, nb, *, name):
    h_n, s_n, e = q.shape

    def body(q_ref, k_ref, v_ref, b_ref, lse_ref, do_ref, dlt_ref, dq_ref, dk_ref, dv_ref, db_ref):
        dk_ref[...] = jnp.zeros_like(dk_ref)
        dv_ref[...] = jnp.zeros_like(dv_ref)
        db_ref[...] = jnp.zeros_like(db_ref)
        m_both, m_first = _dil_masks()
        for j in range(N_BLK):
            lo, k_lo, b_lo, first = _dil_block(j, nb)
            qj = q_ref[lo:lo + DIL_BLOCK, :]
            kk = k_ref[k_lo:lo + DIL_BLOCK, :]
            vv = v_ref[k_lo:lo + DIL_BLOCK, :]
            s = lax.dot_general(qj, kk, _NT, preferred_element_type=F32) * DIL_SCALE + b_ref[:, b_lo:]
            p = jnp.where(m_first if first else m_both, jnp.exp(s - lse_ref[lo:lo + DIL_BLOCK, :]), 0.0)
            dob = do_ref[lo:lo + DIL_BLOCK, :].astype(BF16)
            dv_ref[k_lo:lo + DIL_BLOCK, :] += lax.dot_general(p.astype(BF16), dob, _TN, preferred_element_type=F32)
            dp = lax.dot_general(dob, vv, _NT, preferred_element_type=F32)
            ds = p * (dp - dlt_ref[lo:lo + DIL_BLOCK, :])
            db_ref[:, b_lo:] += ds
            dsb = (ds * DIL_SCALE).astype(BF16)
            dq_ref[lo:lo + DIL_BLOCK, :] = jnp.dot(dsb, kk, preferred_element_type=F32)
            dk_ref[k_lo:lo + DIL_BLOCK, :] += lax.dot_general(dsb, qj, _TN, preferred_element_type=F32)

    head = lambda w: pl.BlockSpec((None, s_n, w), lambda h: (h, 0, 0))
    b_spec = pl.BlockSpec((None, DIL_BLOCK, 2 * DIL_BLOCK), lambda h: (h, 0, 0))
    sd = jax.ShapeDtypeStruct((h_n, s_n, e), F32)
    return pl.pallas_call(
        body, grid=(h_n,),
        in_specs=[head(e), head(e), head(e), b_spec, head(1), head(e), head(1)],
        out_specs=[head(e), head(e), head(e), b_spec],
        out_shape=[sd, sd, sd, jax.ShapeDtypeStruct((h_n, DIL_BLOCK, 2 * DIL_BLOCK), F32)],
        compiler_params=_cparams(), name=name)(q, k, v, bias, lse, do, dlt)


def _group_alpha(l_refs):
    ls = [r[...] for r in l_refs]
    m = jnp.maximum(jnp.maximum(ls[0], ls[1]), ls[2])
    es = [jnp.exp(l - m) for l in ls]
    tot = es[0] + es[1] + es[2]
    return [ex / tot for ex in es]


def _dil_mix_fwd(os_, ls_, *, name):
    h_n, s_n, e = os_[0].shape
    tm = 512

    def body(o0, o1, o2, l0, l1, l2, out_ref):
        al = _group_alpha((l0, l1, l2))
        out_ref[...] = al[0] * o0[...] + al[1] * o1[...] + al[2] * o2[...]

    blk = lambda w: pl.BlockSpec((None, tm, w), lambda h, i: (h, i, 0))
    return pl.pallas_call(body, grid=(h_n, s_n // tm), in_specs=[blk(e)] * 3 + [blk(1)] * 3, out_specs=blk(e),
                          out_shape=jax.ShapeDtypeStruct((h_n, s_n, e), F32), compiler_params=_cparams(),
                          name=name)(*os_, *ls_)


def _dil_mix_bwd(do, os_, ls_, *, name):
    h_n, s_n, e = do.shape
    tm = 512

    def body(do_ref, o0, o1, o2, l0, l1, l2, d0, d1, d2, t0, t1, t2):
        al = _group_alpha((l0, l1, l2))
        do_v = do_ref[...]
        mix = al[0] * o0[...] + al[1] * o1[...] + al[2] * o2[...]
        dbar = jnp.sum(do_v * mix, axis=-1, keepdims=True)
        for a_g, d_ref, t_ref in zip(al, (d0, d1, d2), (t0, t1, t2)):
            d_ref[...] = a_g * do_v
            t_ref[...] = a_g * dbar

    blk = lambda w: pl.BlockSpec((None, tm, w), lambda h, i: (h, i, 0))
    sd_e = jax.ShapeDtypeStruct((h_n, s_n, e), F32)
    sd_1 = jax.ShapeDtypeStruct((h_n, s_n, 1), F32)
    outs = pl.pallas_call(body, grid=(h_n, s_n // tm), in_specs=[blk(e)] * 4 + [blk(1)] * 3,
                          out_specs=[blk(e)] * 3 + [blk(1)] * 3, out_shape=[sd_e] * 3 + [sd_1] * 3,
                          compiler_params=_cparams(), name=name)(do, *os_, *ls_)
    return outs[:3], outs[3:]


def _bias_grad(ds, bucket, *, name):
    h_n = ds.shape[0]

    def body(ds_ref, bk_ref, o_ref):
        ds_v = ds_ref[...]
        bk = bk_ref[...]
        lane = lax.broadcasted_iota(jnp.int32, (1, N_BUCKETS), 1)
        acc = jnp.zeros((1, N_BUCKETS), F32)
        for b in range(N_BUCKETS):
            tot = jnp.sum(jnp.sum(jnp.where(bk == b, ds_v, 0.0), axis=1, keepdims=True), axis=0, keepdims=True)
            acc = acc + jnp.where(lane == b, tot, 0.0)
        o_ref[...] = acc

    return pl.pallas_call(
        body, grid=(h_n,),
        in_specs=[pl.BlockSpec((None, DIL_BLOCK, 2 * DIL_BLOCK), lambda h: (h, 0, 0)),
                  pl.BlockSpec((DIL_BLOCK, 2 * DIL_BLOCK), lambda h: (0, 0))],
        out_specs=pl.BlockSpec((None, 1, N_BUCKETS), lambda h: (h, 0, 0)),
        out_shape=jax.ShapeDtypeStruct((h_n, 1, N_BUCKETS), F32), compiler_params=_cparams(), name=name)(ds, bucket)


def _bias_table(rb, bucket, *, name):
    h_n = rb.shape[0]

    def body(rb_ref, bk_ref, o_ref):
        bk = bk_ref[...]
        row = rb_ref[...]
        acc = jnp.zeros(bk.shape, F32)
        for b in range(N_BUCKETS):
            acc = jnp.where(bk == b, row[:, b:b + 1], acc)
        o_ref[...] = acc

    return pl.pallas_call(
        body, grid=(h_n,),
        in_specs=[pl.BlockSpec((None, 1, N_BUCKETS), lambda h: (h, 0, 0)),
                  pl.BlockSpec((DIL_BLOCK, 2 * DIL_BLOCK), lambda h: (0, 0))],
        out_specs=pl.BlockSpec((None, DIL_BLOCK, 2 * DIL_BLOCK), lambda h: (h, 0, 0)),
        out_shape=jax.ShapeDtypeStruct((h_n, DIL_BLOCK, 2 * DIL_BLOCK), F32), compiler_params=_cparams(),
        name=name)(rb, bucket)


def _row_tile(rows, cols, budget=1 << 20):
    if rows * cols * 4 <= budget or rows % 8:
        return rows
    best = 8
    for t in range(8, rows + 1, 8):
        if rows % t == 0 and t * cols * 4 <= budget:
            best = t
    return best


def _adamw(w, g, m, v, *, name):
    shape = w.shape
    cols = shape[-1]
    rows = math.prod(shape[:-1]) if len(shape) > 1 else 1
    to2 = lambda t: t.reshape(rows, cols)
    tr = _row_tile(rows, cols)
    c1 = 1.0 / (1.0 - ADAM_B1 ** ADAM_STEP)
    c2 = 1.0 / (1.0 - ADAM_B2 ** ADAM_STEP)

    def body(w_ref, g_ref, m_ref, v_ref, d_ref, nm_ref, nv_ref):
        g_v = g_ref[...]
        nm = ADAM_B1 * m_ref[...] + (1.0 - ADAM_B1) * g_v
        nv = ADAM_B2 * v_ref[...] + (1.0 - ADAM_B2) * (g_v * g_v)
        m_hat = nm * c1
        v_hat = nv * c2
        d_ref[...] = -ADAM_LR * (m_hat / (jnp.sqrt(v_hat) + ADAM_EPS) + ADAM_WD * w_ref[...])
        nm_ref[...] = nm
        nv_ref[...] = nv

    blk = pl.BlockSpec((tr, cols), lambda i: (i, 0))
    sd = jax.ShapeDtypeStruct((rows, cols), F32)
    outs = pl.pallas_call(body, grid=(rows // tr,), in_specs=[blk] * 4, out_specs=[blk] * 3, out_shape=[sd] * 3,
                          compiler_params=_cparams(), name=name)(to2(w), to2(g), to2(m), to2(v))
    return tuple(t.reshape(shape) for t in outs)


def _add_half(unit, got, half_idx, *, name):
    rest = unit.shape[2:]
    c = rest[-1]
    r = math.prod(rest[:-1])
    tr = _row_tile(r, c)

    def body(idx_ref, u_ref, g_ref, o_ref):
        o_ref[...] = u_ref[...] + g_ref[...].astype(F32)

    grid_spec = pltpu.PrefetchScalarGridSpec(
        num_scalar_prefetch=1, grid=(N_SHARD, r // tr),
        in_specs=[pl.BlockSpec((None, None, tr, c), lambda s, i, idx: (idx[0], s, i, 0)),
                  pl.BlockSpec((None, tr, c), lambda s, i, idx: (s, i, 0))],
        out_specs=pl.BlockSpec((None, tr, c), lambda s, i, idx: (s, i, 0)))
    out = pl.pallas_call(body, grid_spec=grid_spec, out_shape=jax.ShapeDtypeStruct((N_SHARD, r, c), F32),
                         compiler_params=_cparams(), name=name)(
        half_idx, unit.reshape(2, N_SHARD, r, c), got.reshape(N_SHARD, r, c))
    return out.reshape((N_SHARD,) + rest)


def _add_shards(part, got, shard_idx, *, name):
    rest = part.shape[1:]
    c = rest[-1]
    r = math.prod(rest[:-1])
    tr = _row_tile(r, c)

    def body(idx_ref, p_ref, g_ref, o_ref):
        acc = p_ref[...]
        for k in range(3):
            acc = acc + g_ref[k].astype(F32)
        o_ref[...] = acc

    grid_spec = pltpu.PrefetchScalarGridSpec(
        num_scalar_prefetch=1, grid=(r // tr,),
        in_specs=[pl.BlockSpec((None, tr, c), lambda i, idx: (idx[0], i, 0)),
                  pl.BlockSpec((3, tr, c), lambda i, idx: (0, i, 0))],
        out_specs=pl.BlockSpec((tr, c), lambda i, idx: (i, 0)))
    out = pl.pallas_call(body, grid_spec=grid_spec, out_shape=jax.ShapeDtypeStruct((r, c), F32),
                         compiler_params=_cparams(), name=name)(
        shard_idx, part.reshape(N_SHARD, r, c), got.reshape(3, r, c))
    return out.reshape(rest)


def _sum_devices(x, n_dev, *, name):
    rows = x.shape[0] // n_dev

    def body(x_ref, o_ref):
        acc = x_ref[0:rows, :]
        for d in range(1, n_dev):
            acc = acc + x_ref[d * rows:(d + 1) * rows, :]
        o_ref[...] = acc

    return pl.pallas_call(body, out_shape=jax.ShapeDtypeStruct((rows, x.shape[1]), F32), name=name)(x)


def _my_pos():
    return lax.axis_index("x"), lax.axis_index("y"), lax.axis_index("c")


def _all_gather(x_blk, *, name, in_vmem):
    m_per, n = x_blk.shape

    def body(x_ref, out_ref, send_sems, recv_sems, local_sem):
        x, y, c = _my_pos()
        me, sibling = (x, y, c), (x, y, 1 - c)
        chips = [(1 - x, y), (x, 1 - y), (1 - x, 1 - y)]

        def rows(px, py, pc):
            return out_ref.at[pl.ds((4 * px + 2 * py + pc) * m_per, m_per), :]

        def copy(k, block, to, src=None):
            return pltpu.make_async_remote_copy(
                src_ref=rows(*block) if src is None else src, dst_ref=rows(*block),
                send_sem=send_sems.at[k], recv_sem=recv_sems.at[k], device_id=to, device_id_type=MESH)

        mine = pltpu.make_async_copy(x_ref, rows(*me), local_sem)
        mine.start()
        first = [copy(0, me, sibling, src=x_ref)]
        first += [copy(1 + j, me, (*chip, c), src=x_ref) for j, chip in enumerate(chips)]
        for cp in first:
            cp.start()
        passed = [copy(4 + j, (*chip, c), sibling) for j, chip in enumerate(chips)]
        for j, chip in enumerate(chips):
            copy(1 + j, (*chip, c), me).wait_recv()
            passed[j].start()
        copy(0, sibling, me).wait_recv()
        for j, chip in enumerate(chips):
            copy(4 + j, (*chip, 1 - c), me).wait_recv()
        for cp in first + passed:
            cp.wait_send()
        mine.wait()

    space = pltpu.VMEM if in_vmem else pl.ANY
    return pl.pallas_call(
        body, out_shape=jax.ShapeDtypeStruct((8 * m_per, n), x_blk.dtype),
        in_specs=[pl.BlockSpec(memory_space=space)], out_specs=pl.BlockSpec(memory_space=space),
        scratch_shapes=[pltpu.SemaphoreType.DMA((7,)), pltpu.SemaphoreType.DMA((7,)), pltpu.SemaphoreType.DMA],
        name=name)(x_blk)


_HBM = pl.BlockSpec(memory_space=pl.ANY)


def _gather_weights(fams, *, name):
    n = len(fams)

    def body(*refs):
        ins, outs = refs[:n], refs[n:2 * n]
        send_sems, recv_sems, local_sems = refs[2 * n:]
        x, y, c = _my_pos()
        me, sibling = (x, y, c), (x, y, 1 - c)
        chips = [(1 - x, y), (x, 1 - y), (1 - x, 1 - y)]

        def copy(f, k, block, to, src=None):
            px, py, pc = block
            dst = outs[f].at[2 * px + py, pc]
            return pltpu.make_async_remote_copy(
                src_ref=dst if src is None else src, dst_ref=dst, send_sem=send_sems.at[7 * f + k],
                recv_sem=recv_sems.at[7 * f + k], device_id=to, device_id_type=MESH)

        mine, first, passed = [], [], []
        for f in range(n):
            src = ins[f].at[c]
            mine.append(pltpu.make_async_copy(src, outs[f].at[2 * x + y, c], local_sems.at[f]))
            first.append(copy(f, 0, me, sibling, src=src))
            first += [copy(f, 1 + j, me, (*chip, c), src=src) for j, chip in enumerate(chips)]
        for cp in mine + first:
            cp.start()
        for j, chip in enumerate(chips):
            for f in range(n):
                copy(f, 1 + j, (*chip, c), me).wait_recv()
                passed.append(copy(f, 4 + j, (*chip, c), sibling))
                passed[-1].start()
        for f in range(n):
            copy(f, 0, sibling, me).wait_recv()
        for j, chip in enumerate(chips):
            for f in range(n):
                copy(f, 4 + j, (*chip, 1 - c), me).wait_recv()
        for cp in first + passed:
            cp.wait_send()
        for cp in mine:
            cp.wait()

    return pl.pallas_call(
        body, out_shape=[jax.ShapeDtypeStruct((N_SHARD,) + t.shape, t.dtype) for t in fams],
        in_specs=[_HBM] * n, out_specs=[_HBM] * n,
        scratch_shapes=[pltpu.SemaphoreType.DMA((7 * n,)), pltpu.SemaphoreType.DMA((7 * n,)),
                        pltpu.SemaphoreType.DMA((n,))], name=name)(*fams)


def _swap_halves(units, *, name):
    n = len(units)

    def body(*refs):
        ins, outs = refs[:n], refs[n:2 * n]
        send_sems, recv_sems = refs[2 * n:]
        x, y, c = _my_pos()
        cps = [pltpu.make_async_remote_copy(src_ref=ins[f].at[1 - c], dst_ref=outs[f], send_sem=send_sems.at[f],
                                            recv_sem=recv_sems.at[f], device_id=(x, y, 1 - c), device_id_type=MESH)
               for f in range(n)]
        for cp in cps:
            cp.start()
        for cp in cps:
            cp.wait()

    return pl.pallas_call(
        body, out_shape=[jax.ShapeDtypeStruct(t.shape[1:], t.dtype) for t in units],
        in_specs=[_HBM] * n, out_specs=[_HBM] * n,
        scratch_shapes=[pltpu.SemaphoreType.DMA((n,)), pltpu.SemaphoreType.DMA((n,))], name=name)(*units)


def _send_to_chips(parts, *, name):
    n = len(parts)

    def body(*refs):
        ins, outs = refs[:n], refs[n:2 * n]
        send_sems, recv_sems = refs[2 * n:]
        x, y, c = _my_pos()
        chips = [(1 - x, y), (x, 1 - y), (1 - x, 1 - y)]
        cps = [pltpu.make_async_remote_copy(src_ref=ins[f].at[2 * cx + cy], dst_ref=outs[f].at[k],
                                            send_sem=send_sems.at[3 * f + k], recv_sem=recv_sems.at[3 * f + k],
                                            device_id=(cx, cy, c), device_id_type=MESH)
               for f in range(n) for k, (cx, cy) in enumerate(chips)]
        for cp in cps:
            cp.start()
        for cp in cps:
            cp.wait()

    return pl.pallas_call(
        body, out_shape=[jax.ShapeDtypeStruct((3,) + t.shape[1:], t.dtype) for t in parts],
        in_specs=[_HBM] * n, out_specs=[_HBM] * n,
        scratch_shapes=[pltpu.SemaphoreType.DMA((3 * n,)), pltpu.SemaphoreType.DMA((3 * n,))], name=name)(*parts)


def _pair_gather(halves, *, name):
    n = len(halves)

    def body(*refs):
        ins, outs = refs[:n], refs[n:2 * n]
        send_sems, recv_sems, local_sems = refs[2 * n:]
        x, y, c = _my_pos()
        mine = [pltpu.make_async_copy(ins[f], outs[f].at[c], local_sems.at[f]) for f in range(n)]
        cps = [pltpu.make_async_remote_copy(src_ref=ins[f], dst_ref=outs[f].at[c], send_sem=send_sems.at[f],
                                            recv_sem=recv_sems.at[f], device_id=(x, y, 1 - c), device_id_type=MESH)
               for f in range(n)]
        for cp in mine + cps:
            cp.start()
        for f in range(n):
            pltpu.make_async_remote_copy(src_ref=ins[f], dst_ref=outs[f].at[1 - c], send_sem=send_sems.at[f],
                                         recv_sem=recv_sems.at[f], device_id=(x, y, 1 - c),
                                         device_id_type=MESH).wait_recv()
        for cp in cps:
            cp.wait_send()
        for cp in mine:
            cp.wait()

    return pl.pallas_call(
        body, out_shape=[jax.ShapeDtypeStruct((2,) + t.shape, t.dtype) for t in halves],
        in_specs=[_HBM] * n, out_specs=[_HBM] * n,
        scratch_shapes=[pltpu.SemaphoreType.DMA((n,)), pltpu.SemaphoreType.DMA((n,)), pltpu.SemaphoreType.DMA((n,))],
        name=name)(*halves)


def _to_heads(t, width):
    return t.reshape(t.shape[0], HEADS, width).transpose(1, 0, 2)


def _from_heads(t):
    return t.transpose(1, 0, 2).reshape(t.shape[1], -1)


def _residue_major(t, d):
    h_n, s_n, e = t.shape
    return t.reshape(h_n, s_n // d, d, e).transpose(0, 2, 1, 3).reshape(h_n, s_n, e)


def _token_major(t, d):
    h_n, s_n, e = t.shape
    return t.reshape(h_n, d, s_n // d, e).transpose(0, 2, 1, 3).reshape(h_n, s_n, e)


def _t5_bucket(dist):
    max_exact = N_BUCKETS // 2
    d = jnp.maximum(dist, 1).astype(F32)
    large = max_exact + (jnp.log(d / max_exact) / math.log(MAX_DISTANCE / max_exact)
                         * (N_BUCKETS - max_exact)).astype(jnp.int32)
    large = jnp.minimum(large, N_BUCKETS - 1)
    return jnp.where(dist < max_exact, dist, large)


def _bucket_map(dilation):
    iq = jnp.arange(DIL_BLOCK)[:, None]
    ik = jnp.arange(2 * DIL_BLOCK)[None, :]
    rel = DIL_BLOCK + iq - ik
    return _t5_bucket(jnp.maximum(rel, 0) * dilation).astype(jnp.int32)


def _q_perm(w):
    w3 = w.reshape(w.shape[0], HEADS, QK_NOPE + QK_ROPE)
    return jnp.concatenate([w3[:, :, :QK_NOPE].reshape(w.shape[0], -1),
                            w3[:, :, QK_NOPE:QK_NOPE + HALF_ROPE].reshape(w.shape[0], -1),
                            w3[:, :, QK_NOPE + HALF_ROPE:].reshape(w.shape[0], -1)], axis=1)


def _q_unperm(w):
    n0, n1 = HEADS * QK_NOPE, HEADS * HALF_ROPE
    r = w.shape[0]
    return jnp.concatenate([w[:, :n0].reshape(r, HEADS, QK_NOPE), w[:, n0:n0 + n1].reshape(r, HEADS, HALF_ROPE),
                            w[:, n0 + n1:].reshape(r, HEADS, HALF_ROPE)], axis=2).reshape(r, -1)


def _kv_perm(w):
    w3 = w.reshape(w.shape[0], HEADS, QK_NOPE + V_HEAD)
    return jnp.concatenate([w3[:, :, :QK_NOPE].reshape(w.shape[0], -1), w3[:, :, QK_NOPE:].reshape(w.shape[0], -1)],
                           axis=1)


def _kv_unperm(w):
    n0 = HEADS * QK_NOPE
    r = w.shape[0]
    return jnp.concatenate([w[:, :n0].reshape(r, HEADS, QK_NOPE), w[:, n0:].reshape(r, HEADS, V_HEAD)],
                           axis=2).reshape(r, -1)


def _row(v):
    return v.reshape(1, -1)


def kernel(x, c, norm_pre, norm_post, w_mod, b_mod, ffn_w_gate, ffn_w_up, ffn_w_down, mla_w_in, mla_q_norm, mla_w_q_up, mla_kv_norm, mla_w_kv_up, mla_w_o, dil_w_in, dil_w_o, rel_bias, loss_target, m_norm_pre, m_norm_post, m_w_mod, m_b_mod, m_ffn_w_gate, m_ffn_w_up, m_ffn_w_down, m_mla_w_in, m_mla_q_norm, m_mla_w_q_up, m_mla_kv_norm, m_mla_w_kv_up, m_mla_w_o, m_dil_w_in, m_dil_w_o, m_rel_bias, v_norm_pre, v_norm_post, v_w_mod, v_b_mod, v_ffn_w_gate, v_ffn_w_up, v_ffn_w_down, v_mla_w_in, v_mla_q_norm, v_mla_w_q_up, v_mla_kv_norm, v_mla_w_kv_up, v_mla_w_o, v_dil_w_in, v_dil_w_o, v_rel_bias):
    given = dict(locals())
    ix, iy, ic = _my_pos()
    shard_id = 2 * ix + iy
    dev_id = 4 * ix + 2 * iy + ic
    x2 = x[0]
    target = loss_target[0]
    half_idx = jnp.reshape(ic, (1,)).astype(jnp.int32)
    shard_idx = jnp.reshape(shard_id, (1,)).astype(jnp.int32)

    blk = jnp.zeros((8, D_MODEL), F32)
    blk = blk.at[0].set(c[0])
    blk = blk.at[1:3].set(jnp.pad(norm_pre.reshape(-1), (0, 512)).reshape(2, D_MODEL))
    blk = blk.at[3:5].set(jnp.pad(norm_post.reshape(-1), (0, 512)).reshape(2, D_MODEL))
    got = _all_gather(blk, name="ag_c_norms", in_vmem=True).reshape(N_SHARD, 2, 8, D_MODEL)
    c_all = got[:, :, 0, :].reshape(8, D_MODEL)

    def full_norm(lo):
        t = got[:, 0, lo:lo + 2, :].reshape(N_SHARD, 2 * D_MODEL)[:, :1536].reshape(N_SHARD, 2, 3, 256)
        return t.transpose(1, 2, 0, 3).reshape(2, 3, D_MODEL)

    pre_full, post_full = full_norm(1), full_norm(3)

    silu_c = _silu_bf16(c_all, name="silu_c")
    b_cols = lax.dynamic_slice_in_dim(b_mod, shard_id * 2304, 2304, axis=1).reshape(2, 1, 2304)
    mod_part = _mm(silu_c, w_mod, bias=b_cols, name="mod_mm", tn_cap=768)
    mod_all = _all_gather(mod_part.reshape(16, 2304), name="ag_mod", in_vmem=True)
    mod_all = mod_all.reshape(N_SHARD, 2, 2, 8, 2304)[:, 0]
    mod_mine = lax.dynamic_index_in_dim(mod_all, dev_id, axis=2, keepdims=False)
    mod = mod_mine.transpose(1, 0, 2).reshape(2, 9, D_MODEL)

    fams = [jnp.concatenate([ffn_w_gate, ffn_w_up], axis=1),
            ffn_w_down,
            mla_w_in.reshape(2, 128, -1), mla_w_q_up.reshape(2, 192, -1), mla_w_kv_up.reshape(2, 128, -1),
            mla_w_o.reshape(2, 128, D_MODEL), dil_w_in.reshape(2, 512, -1), dil_w_o.reshape(2, 128, D_MODEL)]
    full = _gather_weights([t.astype(BF16) for t in fams], name="ag_weights")
    w_gu, w_dn = full[0], full[1]
    w_in = full[2].reshape(D_MODEL, -1)
    wq_p = _q_perm(full[3].reshape(N_SHARD, Q_LORA, -1).transpose(1, 0, 2).reshape(Q_LORA, -1))
    wkv_p = _kv_perm(full[4].reshape(N_SHARD, KV_LORA, -1).transpose(1, 0, 2).reshape(KV_LORA, -1))
    w_mo = full[5].reshape(D_MODEL, D_MODEL)
    w_di = full[6].reshape(N_SHARD, D_MODEL, -1)
    w_do = full[7].reshape(D_MODEL, D_MODEL)
    gu_sel = lambda i, h: (lambda g: (g % N_SHARD, i, (g // N_SHARD) * 2 + h))
    dn_sel = lambda i, h: (lambda g: (g, i, h))

    pos = jnp.arange(SEQ, dtype=F32)
    freqs = ROPE_THETA ** (-jnp.arange(HALF_ROPE, dtype=F32) / HALF_ROPE)
    ang = pos[:, None] * freqs[None, :]
    cos_k, sin_k = jnp.cos(ang), jnp.sin(ang)
    cos_q, sin_q = jnp.tile(cos_k, (1, HEADS)), jnp.tile(sin_k, (1, HEADS))

    buckets = [_bucket_map(d) for _, d in DIL_GROUPS]
    biases = [_bias_table(rel_bias[:, g * HEADS:(g + 1) * HEADS].T.reshape(HEADS, 1, N_BUCKETS), bk,
                          name=f"dil_bias_table_g{g}") for g, bk in enumerate(buckets)]

    def sub_params(i, sub):
        return dict(pg=_row(pre_full[i, sub]), qg=_row(post_full[i, sub]), sh=_row(mod[i, 3 * sub]),
                    sc=_row(mod[i, 3 * sub + 1]), gate=_row(mod[i, 3 * sub + 2]))

    def ffn_fwd(xin, i, h, sub):
        p = sub_params(i, sub)
        tag = f"l{i}s{sub}"
        hn = _pre_fwd(xin, p['pg'], p['sc'], p['sh'], name=f"pre_fwd_{tag}")
        gu = _mm(hn, w_gu, g_n=2 * N_SHARD, b_sel=gu_sel(i, h), name=f"ffn_gu_{tag}")
        a = _swiglu_fwd(gu, name=f"swiglu_{tag}")
        f = _mm(a, w_dn, g_n=N_SHARD, b_sel=dn_sel(i, h), reduce_g=True, name=f"ffn_down_{tag}")
        out = _post_fwd(f, xin, p['qg'], p['gate'], FFN_RES, name=f"post_fwd_{tag}")
        return out, dict(x=xin, hn=hn, gu=gu, a=a, f=f, p=p, i=i, h=h, tag=tag)

    def mla_fwd(xin, i, sub):
        p = sub_params(i, sub)
        tag = f"l{i}s{sub}"
        hn = _pre_fwd(xin, p['pg'], p['sc'], p['sh'], name=f"pre_fwd_{tag}")
        lat = _mm(hn, w_in, name="mla_lat")
        cq, ckv = lat[:, :Q_LORA], lat[:, Q_LORA:Q_LORA + KV_LORA]
        k1, k2 = lat[:, Q_LORA + KV_LORA:Q_LORA + KV_LORA + HALF_ROPE], lat[:, Q_LORA + KV_LORA + HALF_ROPE:]
        cqn = _rms_fwd(cq, mla_q_norm, name="mla_qnorm")
        ckvn = _rms_fwd(ckv, mla_kv_norm, name="mla_kvnorm")
        qp = _mm(cqn, wq_p, name="mla_q_up")
        kvp = _mm(ckvn, wkv_p, name="mla_kv_up")
        n0, n1 = HEADS * QK_NOPE, HEADS * HALF_ROPE
        qr1, qr2 = _rope(qp[:, n0:n0 + n1], qp[:, n0 + n1:], cos_q, sin_q, name="rope_q")
        kr1, kr2 = _rope(k1, k2, cos_k, sin_k, name="rope_k")
        q = jnp.concatenate([qp[:, :n0].reshape(SEQ, HEADS, QK_NOPE), qr1.reshape(SEQ, HEADS, HALF_ROPE),
                             qr2.reshape(SEQ, HEADS, HALF_ROPE)], axis=2).transpose(1, 0, 2).astype(BF16)
        kr = jnp.broadcast_to(jnp.concatenate([kr1, kr2], axis=1)[:, None, :], (SEQ, HEADS, QK_ROPE))
        k = jnp.concatenate([kvp[:, :n0].reshape(SEQ, HEADS, QK_NOPE), kr], axis=2).transpose(1, 0, 2).astype(BF16)
        v = _to_heads(kvp[:, n0:], V_HEAD).astype(BF16)
        o, lse = _mla_attn_fwd(q, k, v, name="mla_attn_fwd")
        o_flat = _from_heads(o).astype(BF16)
        f = _mm(o_flat, w_mo, name="mla_out")
        out = _post_fwd(f, xin, p['qg'], p['gate'], 1.0, name=f"post_fwd_{tag}")
        return out, dict(x=xin, hn=hn, cq=cq, ckv=ckv, cqn=cqn, ckvn=ckvn, q=q, k=k, v=v, o=o, lse=lse,
                         o_flat=o_flat, f=f, p=p, tag=tag)

    def dil_fwd(xin, i, sub):
        p = sub_params(i, sub)
        tag = f"l{i}s{sub}"
        hn = _pre_fwd(xin, p['pg'], p['sc'], p['sh'], name=f"pre_fwd_{tag}")
        proj = _mm(hn, w_di, out_dtype=BF16, tn_cap=768, name="dil_proj")
        heads = proj.reshape(N_SHARD, SEQ, 36, 64).transpose(0, 2, 1, 3).reshape(3, 3, HEADS, SEQ, 64)
        qkv, outs, lses = [], [], []
        for g, (window, d) in enumerate(DIL_GROUPS):
            q, k, v = (_residue_major(heads[g, t], d) for t in range(3))
            o, lse = _dil_attn_fwd(q, k, v, biases[g], SEQ // d // DIL_BLOCK, name=f"dil_attn_fwd_g{g}")
            qkv.append((q, k, v))
            outs.append(_token_major(o, d))
            lses.append(_token_major(lse, d))
        mix = _dil_mix_fwd(outs, lses, name="dil_mix_fwd")
        o_flat = _from_heads(mix).astype(BF16)
        f = _mm(o_flat, w_do, name="dil_out")
        out = _post_fwd(f, xin, p['qg'], p['gate'], 1.0, name=f"post_fwd_{tag}")
        return out, dict(x=xin, hn=hn, qkv=qkv, outs=outs, lses=lses, o_flat=o_flat, f=f, p=p, tag=tag)

    xs = x2
    saved = []
    for i in range(2):
        xs, sv = ffn_fwd(xs, i, 0, 0)
        saved.append(sv)
        xs, sv = (mla_fwd if i == 0 else dil_fwd)(xs, i, 1)
        saved.append(sv)
        xs, sv = ffn_fwd(xs, i, 1, 2)
        saved.append(sv)

    dx, loss_part = _loss(xs, target, name="loss")
    loss = lax.psum(loss_part[0, 0], ("x", "y", "c"))

    dmod = [[None] * 9 for _ in range(2)]
    dpre = [[None] * 3 for _ in range(2)]
    dpost = [[None] * 3 for _ in range(2)]
    gu_shape = (2, N_SHARD, 2, 2, D_MODEL, F_SHARD)
    dn_shape = (2, N_SHARD, 2, F_SHARD, D_MODEL)
    bufs = dict(gu=lax.empty(gu_shape, F32), dn=lax.empty(dn_shape, F32))
    row_unit = lambda g, r: ((r % 2, r // 2), 0)

    def close_sub(dhn, dout, sv, i, sub, res_dgate, res_dqg):
        p = sv['p']
        dxs, dsh, dsc, dpg = _pre_bwd(dhn, sv['x'], dout, p['pg'], p['sc'], name=f"pre_bwd_{sv['tag']}")
        dmod[i][3 * sub], dmod[i][3 * sub + 1], dmod[i][3 * sub + 2] = dsh, dsc, res_dgate
        dpre[i][sub], dpost[i][sub] = dpg, res_dqg
        return dxs

    def ffn_bwd(dout, sv, sub):
        i, h, p, tag = sv['i'], sv['h'], sv['p'], sv['tag']
        df, dgate, dqg = _post_bwd(dout, sv['f'], p['qg'], p['gate'], FFN_RES, name=f"post_bwd_{tag}")
        da = _mm(df, w_dn, g_n=N_SHARD, b_sel=dn_sel(i, h), tb=True, name=f"ffn_da_{tag}")
        bufs['dn'] = _mm(sv['a'], df, ta=True, out_shape=dn_shape, out_sel=lambda g, r: ((i, g, h), r),
                         out_buf=bufs['dn'], name=f"ffn_dwd_{tag}")
        dgu = _swiglu_bwd(da, sv['gu'], name=f"swiglu_bwd_{tag}")
        bufs['gu'] = _mm(sv['hn'], dgu, ta=True, out_shape=gu_shape,
                         out_sel=lambda g, r: ((i, g % N_SHARD, g // N_SHARD, h), r), out_buf=bufs['gu'],
                         name=f"ffn_dwgu_{tag}")
        dhn = _mm(dgu, w_gu, g_n=2 * N_SHARD, b_sel=gu_sel(i, h), tb=True, reduce_g=True, name=f"ffn_dhn_{tag}")
        return close_sub(dhn, dout, sv, i, sub, dgate, dqg)

    def mla_bwd(dout, sv, i, sub):
        p, tag = sv['p'], sv['tag']
        df, dgate, dqg = _post_bwd(dout, sv['f'], p['qg'], p['gate'], 1.0, name=f"post_bwd_{tag}")
        u_wo = _mm(sv['o_flat'], df, ta=True, tm_cap=128, out_shape=(2, N_SHARD, 128, D_MODEL), out_sel=row_unit,
                   name="mla_dwo")
        do_flat = _mm(df, w_mo, tb=True, name="mla_do")
        do = _to_heads(do_flat, V_HEAD)
        dq, dk, dv = _mla_attn_bwd(sv['q'], sv['k'], sv['v'], sv['o'], do, sv['lse'], name="mla_attn_bwd")
        dq_t = dq.transpose(1, 0, 2)
        dqr1, dqr2 = _rope(dq_t[:, :, QK_NOPE:QK_NOPE + HALF_ROPE].reshape(SEQ, -1),
                           dq_t[:, :, QK_NOPE + HALF_ROPE:].reshape(SEQ, -1), cos_q, -sin_q, name="rope_q_bwd")
        dqp = jnp.concatenate([dq_t[:, :, :QK_NOPE].reshape(SEQ, -1), dqr1, dqr2], axis=1).astype(BF16)
        dkr = _head_sum(dk[:, :, QK_NOPE:], name="mla_dkr_sum")
        dk1, dk2 = _rope(dkr[:, :HALF_ROPE], dkr[:, HALF_ROPE:], cos_k, -sin_k, name="rope_k_bwd")
        dkvp = jnp.concatenate([_from_heads(dk[:, :, :QK_NOPE]), _from_heads(dv)], axis=1).astype(BF16)
        g_wq = _q_unperm(_mm(sv['cqn'], dqp, ta=True, name="mla_dwq"))
        g_wkv = _kv_unperm(_mm(sv['ckvn'], dkvp, ta=True, name="mla_dwkv"))
        dcqn = _mm(dqp, wq_p, tb=True, name="mla_dcqn")
        dckvn = _mm(dkvp, wkv_p, tb=True, name="mla_dckvn")
        dcq, g_qn = _rms_bwd(dcqn, sv['cq'], mla_q_norm, name="mla_qnorm_bwd")
        dckv, g_kvn = _rms_bwd(dckvn, sv['ckv'], mla_kv_norm, name="mla_kvnorm_bwd")
        dlat = jnp.concatenate([dcq, dckv, dk1, dk2], axis=1).astype(BF16)
        u_win = _mm(sv['hn'], dlat, ta=True, tm_cap=128, out_shape=(2, N_SHARD, 128, dlat.shape[1]),
                    out_sel=row_unit, name="mla_dwin")
        dhn = _mm(dlat, w_in, tb=True, name="mla_dhn")
        col_unit = lambda t: (t.reshape(t.shape[0], N_SHARD, -1).transpose(1, 0, 2)
                              .reshape(N_SHARD, 2, t.shape[0] // 2, -1).transpose(1, 0, 2, 3))
        grads = dict(units=[u_win, col_unit(g_wq), col_unit(g_wkv), u_wo], q_norm=g_qn, kv_norm=g_kvn)
        return close_sub(dhn, dout, sv, i, sub, dgate, dqg), grads

    def dil_bwd(dout, sv, i, sub):
        p, tag = sv['p'], sv['tag']
        df, dgate, dqg = _post_bwd(dout, sv['f'], p['qg'], p['gate'], 1.0, name=f"post_bwd_{tag}")
        u_wo = _mm(sv['o_flat'], df, ta=True, tm_cap=128, out_shape=(2, N_SHARD, 128, D_MODEL), out_sel=row_unit,
                   name="dil_dwo")
        do = _to_heads(_mm(df, w_do, tb=True, name="dil_do"), 64)
        dos, dlts = _dil_mix_bwd(do, sv['outs'], sv['lses'], name="dil_mix_bwd")
        pieces = []
        bias_rows = []
        for g, (window, d) in enumerate(DIL_GROUPS):
            q, k, v = sv['qkv'][g]
            dq, dk, dv, dbias = _dil_attn_bwd(q, k, v, biases[g], _residue_major(sv['lses'][g], d),
                                              _residue_major(dos[g], d), _residue_major(dlts[g], d),
                                              SEQ // d // DIL_BLOCK, name=f"dil_attn_bwd_g{g}")
            pieces += [_token_major(t, d).astype(BF16) for t in (dq, dk, dv)]
            bias_rows.append(_bias_grad(dbias, buckets[g], name=f"dil_bias_grad_g{g}")[:, 0, :])
        dheads = jnp.stack(pieces).reshape(N_SHARD, 36, SEQ, 64).transpose(0, 2, 1, 3).reshape(N_SHARD, SEQ, 2304)
        u_win = _mm(sv['hn'], dheads, ta=True, tn_cap=768, out_shape=(2, N_SHARD, 512, 2304),
                    out_sel=lambda g, r: ((r, g), 0), name="dil_dwin")
        dhn = _mm(dheads, w_di, tb=True, reduce_g=True, name="dil_dhn")
        g_bias = jnp.concatenate(bias_rows, axis=0).T
        grads = dict(units=[u_win, u_wo], rel_bias=g_bias)
        return close_sub(dhn, dout, sv, i, sub, dgate, dqg), grads

    dx = ffn_bwd(dx, saved[5], 2)
    dx, dil_g = dil_bwd(dx, saved[4], 1, 1)
    dx = ffn_bwd(dx, saved[3], 0)
    dx = ffn_bwd(dx, saved[2], 2)
    dx, mla_g = mla_bwd(dx, saved[1], 0, 1)
    dx = ffn_bwd(dx, saved[0], 0)
    grad_x = dx[None]

    pad_row = lambda v: jnp.pad(v.reshape(-1), (0, (-v.size) % D_MODEL)).reshape(-1, D_MODEL)
    small = jnp.concatenate(
        [jnp.concatenate([dmod[i][r] for i in range(2) for r in range(9)], axis=0),
         jnp.concatenate([dpre[i][s] for i in range(2) for s in range(3)], axis=0),
         jnp.concatenate([dpost[i][s] for i in range(2) for s in range(3)], axis=0),
         pad_row(mla_g['q_norm']), pad_row(mla_g['kv_norm']), pad_row(dil_g['rel_bias'])], axis=0)
    small = jnp.pad(small, ((0, SMALL_ROWS - small.shape[0]), (0, 0)))
    small_all = _all_gather(small, name="ag_small_grads", in_vmem=True)
    small_sum = _sum_devices(small_all, 8, name="sum_small_grads")
    g_b_mod = small_sum[0:18].reshape(2, 9 * D_MODEL)
    my_cols = lambda t: lax.dynamic_slice_in_dim(t, shard_id * 256, 256, axis=2)
    g_norm_pre = my_cols(small_sum[18:24].reshape(2, 3, D_MODEL))
    g_norm_post = my_cols(small_sum[24:30].reshape(2, 3, D_MODEL))
    g_q_norm = small_sum[30, :Q_LORA].reshape(1, Q_LORA)
    g_kv_norm = small_sum[31, :KV_LORA].reshape(1, KV_LORA)
    g_rel_bias = small_sum[32:34].reshape(-1)[:N_BUCKETS * 48].reshape(N_BUCKETS, 48)
    dmod_all = small_all.reshape(8, SMALL_ROWS, D_MODEL)[:, 0:18].reshape(8, 2, 9 * D_MODEL)
    dmod_cols = lax.dynamic_slice_in_dim(dmod_all, shard_id * 2304, 2304, axis=2).transpose(1, 0, 2)
    g_w_mod = _mm(silu_c, dmod_cols.astype(BF16), ta=True, tn_cap=768, name="w_mod_grad")

    units = [bufs['gu'], bufs['dn'], *mla_g['units'], *dil_g['units']]
    got_a = _swap_halves(units, name="rs_sibling")
    parts = [_add_half(u, g, half_idx, name=f"rs_add_half_{k}") for k, (u, g) in enumerate(zip(units, got_a))]
    got_b = _send_to_chips(parts, name="rs_chips")
    reds = [_add_shards(p, g, shard_idx, name=f"rs_add_shards_{k}") for k, (p, g) in enumerate(zip(parts, got_b))]
    fin = _pair_gather(reds, name="rs_pair_gather")
    reduced = dict(ffn_w_gate=fin[0][:, 0], ffn_w_up=fin[0][:, 1], ffn_w_down=fin[1])
    for n, t in zip(['mla_w_in', 'mla_w_q_up', 'mla_w_kv_up', 'mla_w_o', 'dil_w_in', 'dil_w_o'], fin[2:]):
        reduced[n] = t.reshape(given[n].shape)

    grads = dict(norm_pre=g_norm_pre, norm_post=g_norm_post, w_mod=g_w_mod, b_mod=g_b_mod, mla_q_norm=g_q_norm,
                 mla_kv_norm=g_kv_norm, rel_bias=g_rel_bias, **reduced)

    deltas, new_m, new_v = {}, {}, {}
    for n in WEIGHTS:
        deltas[n], new_m[n], new_v[n] = _adamw(given[n], grads[n], given["m_" + n], given["v_" + n],
                                               name=f"adamw_{n}")
    return (loss, grad_x, *[grads[n] for n in WEIGHTS], *[deltas[n] for n in WEIGHTS],
            *[new_m[n] for n in WEIGHTS], *[new_v[n] for n in WEIGHTS])
```

```python
import math

import jax
import jax.numpy as jnp
from jax import lax
from jax.experimental import pallas as pl
from jax.experimental.pallas import tpu as pltpu

F32 = jnp.float32
BF16 = jnp.bfloat16
MESH = pl.DeviceIdType.MESH

SEQ = 2048
D_MODEL = 1024
D_FF = 2816
N_SHARD = 4
F_SHARD = D_FF // N_SHARD
EPS = 1e-6
FFN_RES = 0.5
HEADS = 16
Q_LORA, KV_LORA, QK_NOPE, QK_ROPE, V_HEAD = 384, 256, 64, 32, 64
HALF_ROPE = QK_ROPE // 2
ROPE_THETA = 10000.0
DIL_GROUPS = ((128, 1), (512, 4), (2048, 16))
DIL_BLOCK = 128
N_BUCKETS = 32
MAX_DISTANCE = 2048
ADAM_LR, ADAM_B1, ADAM_B2, ADAM_EPS, ADAM_WD, ADAM_STEP = 0.001, 0.9, 0.999, 1e-08, 0.01, 10

VMEM_LIMIT = 48 * 1024 * 1024
SMALL_ROWS = 40

WEIGHTS = ['norm_pre', 'norm_post', 'w_mod', 'b_mod', 'ffn_w_gate', 'ffn_w_up', 'ffn_w_down', 'mla_w_in',
           'mla_q_norm', 'mla_w_q_up', 'mla_kv_norm', 'mla_w_kv_up', 'mla_w_o', 'dil_w_in', 'dil_w_o', 'rel_bias']


def _cparams(**kw):
    return pltpu.CompilerParams(vmem_limit_bytes=VMEM_LIMIT, **kw)


def _pick(n, cap, mult=128):
    if n <= cap:
        return n
    best = n
    for t in range(mult, cap + 1, mult):
        if n % t == 0:
            best = t
    return best


def _mm(a, b, *, name, ta=False, tb=False, reduce_g=False, bias=None, out_dtype=F32, tm_cap=512, tn_cap=1024,
        g_n=None, b_sel=None, out_shape=None, out_sel=None, out_buf=None):
    a3 = a if a.ndim == 3 else a[None]
    ga = a3.shape[0]
    if b_sel is None:
        b_n = b if b.ndim == 3 else b[None]
        gb = b_n.shape[0]
        b_sel = (lambda g: (g,)) if gb > 1 else (lambda g: (0,))
        g_n = max(ga, gb)
    else:
        b_n = b
    k_dim, m_dim = (a3.shape[1], a3.shape[2]) if ta else (a3.shape[2], a3.shape[1])
    k2, n_dim = (b_n.shape[-1], b_n.shape[-2]) if tb else (b_n.shape[-2], b_n.shape[-1])
    assert k_dim == k2, (a.shape, b.shape)
    tm = _pick(m_dim, tm_cap, 128 if ta else 8)
    tn = _pick(n_dim, tn_cap, 128)
    mt, nt = m_dim // tm, n_dim // tn
    dims = (((0 if ta else 1,), (1 if tb else 0,)), ((), ()))

    if reduce_g:
        grid = (mt, nt, g_n)
        ids = lambda i, j, g: (g, i, j)
    else:
        grid = (g_n, mt, nt)
        ids = lambda g, i, j: (g, i, j)

    def a_map(*p):
        g, i, j = ids(*p)
        g = g if ga > 1 else 0
        return (g, 0, i) if ta else (g, i, 0)

    def b_map(*p):
        g, i, j = ids(*p)
        return (*b_sel(g), j, 0) if tb else (*b_sel(g), 0, j)

    b_lead = (None,) * (b_n.ndim - 2)
    a_spec = pl.BlockSpec((None, k_dim, tm) if ta else (None, tm, k_dim), a_map)
    b_spec = pl.BlockSpec(b_lead + ((tn, k_dim) if tb else (k_dim, tn)), b_map)
    in_specs = [a_spec, b_spec]
    operands = [a3, b_n]
    if bias is not None:
        assert not reduce_g and bias.shape == (g_n, 1, n_dim)
        in_specs.append(pl.BlockSpec((None, 1, tn), lambda g, i, j: (g, 0, j)))
        operands.append(bias)
    aliases = {}
    if out_buf is not None:
        assert tuple(out_buf.shape) == tuple(out_shape) and out_buf.dtype == out_dtype
        in_specs.append(pl.BlockSpec(memory_space=pl.ANY))
        operands.append(out_buf)
        aliases = {len(operands) - 1: 0}

    if reduce_g:
        out_spec = pl.BlockSpec((tm, tn), lambda i, j, g: (i, j))
        out_sds = jax.ShapeDtypeStruct((m_dim, n_dim), F32)
    elif out_shape is not None:
        def o_map(g, i, j):
            lead, rb = out_sel(g, i)
            return (*lead, rb, j)

        out_spec = pl.BlockSpec((None,) * (len(out_shape) - 2) + (tm, tn), o_map)
        out_sds = jax.ShapeDtypeStruct(tuple(out_shape), out_dtype)
    else:
        out_spec = pl.BlockSpec((None, tm, tn), lambda g, i, j: (g, i, j))
        out_sds = jax.ShapeDtypeStruct((g_n, m_dim, n_dim), out_dtype)

    def body(a_ref, b_ref, *rest):
        o_ref = rest[-1]
        r = lax.dot_general(a_ref[...].astype(BF16), b_ref[...].astype(BF16), dims, preferred_element_type=F32)
        if bias is not None:
            r = r + rest[0][...]
        if reduce_g:
            g = pl.program_id(2)

            @pl.when(g == 0)
            def _():
                o_ref[...] = r

            @pl.when(g > 0)
            def _():
                o_ref[...] += r
        else:
            o_ref[...] = r.astype(o_ref.dtype)

    out = pl.pallas_call(body, grid=grid, in_specs=in_specs, out_specs=out_spec, out_shape=out_sds,
                         input_output_aliases=aliases, compiler_params=_cparams(), name=name)(*operands)
    if not reduce_g and out_shape is None and a.ndim == 2 and b.ndim == 2:
        out = out[0]
    return out


def _rows(tm, w):
    return pl.BlockSpec((tm, w), lambda i: (i, 0))


def _vec(w):
    return pl.BlockSpec((1, w), lambda i: (0, 0))


def _rstd(v):
    return lax.rsqrt(jnp.mean(v * v, axis=-1, keepdims=True) + EPS)


def _pre_fwd(x, pg, sc, sh, *, name):
    s_n, w = x.shape
    tm = _pick(s_n, 256, 8)

    def body(x_ref, pg_ref, sc_ref, sh_ref, o_ref):
        xv = x_ref[...]
        n = (xv * _rstd(xv)) * pg_ref[...]
        o_ref[...] = (n * (1.0 + sc_ref[...]) + sh_ref[...]).astype(o_ref.dtype)

    return pl.pallas_call(body, grid=(s_n // tm,), in_specs=[_rows(tm, w), _vec(w), _vec(w), _vec(w)],
                          out_specs=_rows(tm, w), out_shape=jax.ShapeDtypeStruct((s_n, w), BF16),
                          compiler_params=_cparams(), name=name)(x, pg, sc, sh)


def _post_fwd(f, x, qg, gate, res_w, *, name):
    s_n, w = x.shape
    tm = _pick(s_n, 256, 8)

    def body(f_ref, x_ref, qg_ref, gate_ref, o_ref):
        fv = f_ref[...]
        y = (fv * _rstd(fv)) * qg_ref[...]
        o_ref[...] = x_ref[...] + (res_w * gate_ref[...]) * y

    return pl.pallas_call(body, grid=(s_n // tm,), in_specs=[_rows(tm, w), _rows(tm, w), _vec(w), _vec(w)],
                          out_specs=_rows(tm, w), out_shape=jax.ShapeDtypeStruct((s_n, w), F32),
                          compiler_params=_cparams(), name=name)(f, x, qg, gate)


def _post_bwd(dout, f, qg, gate, res_w, *, name):
    s_n, w = f.shape
    tm = _pick(s_n, 256, 8)

    def body(do_ref, f_ref, qg_ref, gate_ref, df_ref, dgate_ref, dqg_ref):
        @pl.when(pl.program_id(0) == 0)
        def _():
            dgate_ref[...] = jnp.zeros_like(dgate_ref)
            dqg_ref[...] = jnp.zeros_like(dqg_ref)

        do = do_ref[...]
        fv = f_ref[...]
        r = _rstd(fv)
        fh = fv * r
        qg_v = qg_ref[...]
        dgate_ref[...] += res_w * jnp.sum(do * (fh * qg_v), axis=0, keepdims=True)
        dy = do * (res_w * gate_ref[...])
        dqg_ref[...] += jnp.sum(dy * fh, axis=0, keepdims=True)
        dfh = dy * qg_v
        df = r * (dfh - fh * jnp.mean(dfh * fh, axis=-1, keepdims=True))
        df_ref[...] = df.astype(df_ref.dtype)

    return pl.pallas_call(
        body, grid=(s_n // tm,), in_specs=[_rows(tm, w), _rows(tm, w), _vec(w), _vec(w)],
        out_specs=[_rows(tm, w), _vec(w), _vec(w)],
        out_shape=[jax.ShapeDtypeStruct((s_n, w), BF16), jax.ShapeDtypeStruct((1, w), F32),
                   jax.ShapeDtypeStruct((1, w), F32)],
        compiler_params=_cparams(), name=name)(dout, f, qg, gate)


def _pre_bwd(dhn, x, dout, pg, sc, *, name):
    s_n, w = x.shape
    tm = _pick(s_n, 256, 8)

    def body(dhn_ref, x_ref, do_ref, pg_ref, sc_ref, dx_ref, dsh_ref, dsc_ref, dpg_ref):
        @pl.when(pl.program_id(0) == 0)
        def _():
            dsh_ref[...] = jnp.zeros_like(dsh_ref)
            dsc_ref[...] = jnp.zeros_like(dsc_ref)
            dpg_ref[...] = jnp.zeros_like(dpg_ref)

        dhn_v = dhn_ref[...]
        xv = x_ref[...]
        r = _rstd(xv)
        xh = xv * r
        pg_v = pg_ref[...]
        dsh_ref[...] += jnp.sum(dhn_v, axis=0, keepdims=True)
        dsc_ref[...] += jnp.sum(dhn_v * (xh * pg_v), axis=0, keepdims=True)
        dn = dhn_v * (1.0 + sc_ref[...])
        dpg_ref[...] += jnp.sum(dn * xh, axis=0, keepdims=True)
        dxh = dn * pg_v
        dx_ref[...] = do_ref[...] + r * (dxh - xh * jnp.mean(dxh * xh, axis=-1, keepdims=True))

    vec = jax.ShapeDtypeStruct((1, w), F32)
    return pl.pallas_call(
        body, grid=(s_n // tm,), in_specs=[_rows(tm, w), _rows(tm, w), _rows(tm, w), _vec(w), _vec(w)],
        out_specs=[_rows(tm, w), _vec(w), _vec(w), _vec(w)],
        out_shape=[jax.ShapeDtypeStruct((s_n, w), F32), vec, vec, vec],
        compiler_params=_cparams(), name=name)(dhn, x, dout, pg, sc)


def _rms_fwd(x, g, *, name):
    s_n, w = x.shape
    tm = _pick(s_n, 512, 8)

    def body(x_ref, g_ref, o_ref):
        xv = x_ref[...]
        o_ref[...] = ((xv * _rstd(xv)) * g_ref[...]).astype(o_ref.dtype)

    return pl.pallas_call(body, grid=(s_n // tm,), in_specs=[_rows(tm, w), _vec(w)], out_specs=_rows(tm, w),
                          out_shape=jax.ShapeDtypeStruct((s_n, w), BF16), compiler_params=_cparams(),
                          name=name)(x, g)


def _rms_bwd(dy, x, g, *, name):
    s_n, w = x.shape
    tm = _pick(s_n, 512, 8)

    def body(dy_ref, x_ref, g_ref, dx_ref, dg_ref):
        @pl.when(pl.program_id(0) == 0)
        def _():
            dg_ref[...] = jnp.zeros_like(dg_ref)

        dy_v = dy_ref[...]
        xv = x_ref[...]
        r = _rstd(xv)
        xh = xv * r
        dg_ref[...] += jnp.sum(dy_v * xh, axis=0, keepdims=True)
        dxh = dy_v * g_ref[...]
        dx_ref[...] = r * (dxh - xh * jnp.mean(dxh * xh, axis=-1, keepdims=True))

    return pl.pallas_call(
        body, grid=(s_n // tm,), in_specs=[_rows(tm, w), _rows(tm, w), _vec(w)],
        out_specs=[_rows(tm, w), _vec(w)],
        out_shape=[jax.ShapeDtypeStruct((s_n, w), F32), jax.ShapeDtypeStruct((1, w), F32)],
        compiler_params=_cparams(), name=name)(dy, x, g)


def _rope(a1, a2, cos, sin, *, name):
    s_n, w = a1.shape
    tm = _pick(s_n, 512, 8)

    def body(a1_ref, a2_ref, c_ref, s_ref, r1_ref, r2_ref):
        u, v, c_v, s_v = a1_ref[...], a2_ref[...], c_ref[...], s_ref[...]
        r1_ref[...] = u * c_v - v * s_v
        r2_ref[...] = u * s_v + v * c_v

    sd = jax.ShapeDtypeStruct((s_n, w), F32)
    return pl.pallas_call(body, grid=(s_n // tm,), in_specs=[_rows(tm, w)] * 4, out_specs=[_rows(tm, w)] * 2,
                          out_shape=[sd, sd], compiler_params=_cparams(), name=name)(a1, a2, cos, sin)


def _silu_bf16(x, *, name):
    def body(x_ref, o_ref):
        xv = x_ref[...]
        o_ref[...] = (xv * jax.nn.sigmoid(xv)).astype(o_ref.dtype)

    return pl.pallas_call(body, out_shape=jax.ShapeDtypeStruct(x.shape, BF16), name=name)(x)


def _loss(y, target, *, name):
    s_n, w = y.shape
    tm = _pick(s_n, 256, 8)

    def body(y_ref, t_ref, dy_ref, l_ref):
        @pl.when(pl.program_id(0) == 0)
        def _():
            l_ref[...] = jnp.zeros_like(l_ref)

        e = y_ref[...] - t_ref[...]
        dy_ref[...] = e * (1.0 / w)
        row = jnp.mean(e * e, axis=-1, keepdims=True)
        l_ref[...] += 0.5 * jnp.sum(row, axis=0, keepdims=True)

    return pl.pallas_call(
        body, grid=(s_n // tm,), in_specs=[_rows(tm, w), _rows(tm, w)],
        out_specs=[_rows(tm, w), pl.BlockSpec((1, 1), lambda i: (0, 0))],
        out_shape=[jax.ShapeDtypeStruct((s_n, w), F32), jax.ShapeDtypeStruct((1, 1), F32)],
        compiler_params=_cparams(), name=name)(y, target)


FFN_TM = 512


def _ffn_up(hn, w_gu, i, h, *, name):
    s_n, d = hn.shape
    f = w_gu.shape[-1]
    tm = _pick(s_n, FFN_TM, 8)

    def body(hn_ref, wg_ref, wu_ref, gu_ref, a_ref):
        xv = hn_ref[...]
        g = jnp.dot(xv, wg_ref[...], preferred_element_type=F32)
        u = jnp.dot(xv, wu_ref[...], preferred_element_type=F32)
        gu_ref[0] = g.astype(BF16)
        gu_ref[1] = u.astype(BF16)
        a_ref[...] = ((g * jax.nn.sigmoid(g)) * u).astype(BF16)

    w_blk = lambda t: pl.BlockSpec((None, None, None, d, f), lambda s, m: (s, i, 2 * t + h, 0, 0))
    return pl.pallas_call(
        body, grid=(N_SHARD, s_n // tm),
        in_specs=[pl.BlockSpec((tm, d), lambda s, m: (m, 0)), w_blk(0), w_blk(1)],
        out_specs=[pl.BlockSpec((None, 2, tm, f), lambda s, m: (s, 0, m, 0)),
                   pl.BlockSpec((None, tm, f), lambda s, m: (s, m, 0))],
        out_shape=[jax.ShapeDtypeStruct((N_SHARD, 2, s_n, f), BF16), jax.ShapeDtypeStruct((N_SHARD, s_n, f), BF16)],
        compiler_params=_cparams(), name=name)(hn, w_gu, w_gu)


def _ffn_dgu(df, w_dn, gu, i, h, *, name):
    s_n, d = df.shape
    f = w_dn.shape[-2]
    tm = _pick(s_n, FFN_TM, 8)

    def body(df_ref, wd_ref, gu_ref, o_ref):
        da = lax.dot_general(df_ref[...], wd_ref[...], (((1,), (1,)), ((), ())), preferred_element_type=F32)
        g = gu_ref[0].astype(F32)
        u = gu_ref[1].astype(F32)
        sig = jax.nn.sigmoid(g)
        o_ref[0] = (da * u * (sig * (1.0 + g * (1.0 - sig)))).astype(BF16)
        o_ref[1] = (da * (g * sig)).astype(BF16)

    gu_blk = pl.BlockSpec((None, 2, tm, f), lambda s, m: (s, 0, m, 0))
    return pl.pallas_call(
        body, grid=(N_SHARD, s_n // tm),
        in_specs=[pl.BlockSpec((tm, d), lambda s, m: (m, 0)),
                  pl.BlockSpec((None, None, None, f, d), lambda s, m: (s, i, h, 0, 0)), gu_blk],
        out_specs=gu_blk, out_shape=jax.ShapeDtypeStruct((N_SHARD, 2, s_n, f), BF16),
        compiler_params=_cparams(), name=name)(df, w_dn, gu)


_NT = (((1,), (1,)), ((), ()))
_TN = (((0,), (0,)), ((), ()))
MLA_TQ = 256


def _causal_mask(i, tq, s_n):
    qpos = i * tq + lax.broadcasted_iota(jnp.int32, (tq, s_n), 0)
    kpos = lax.broadcasted_iota(jnp.int32, (tq, s_n), 1)
    return kpos <= qpos


def _mla_attn_fwd(q, k, v, *, name):
    h_n, s_n, dq = q.shape
    dv = v.shape[-1]
    tq = MLA_TQ
    scale = float(dq) ** -0.5

    def body(q_ref, k_ref, v_ref, o_ref, lse_ref):
        i = pl.program_id(1)
        for e in range(1, s_n // tq + 1):
            @pl.when(i == e - 1)
            def _(ext=e * tq):
                mask = _causal_mask(i, tq, ext)
                s = lax.dot_general(q_ref[...], k_ref[0:ext, :], _NT, preferred_element_type=F32) * scale
                s = jnp.where(mask, s, -jnp.inf)
                m = jnp.max(s, axis=-1, keepdims=True)
                p = jnp.exp(s - m)
                l = jnp.sum(p, axis=-1, keepdims=True)
                o = jnp.dot(p.astype(BF16), v_ref[0:ext, :], preferred_element_type=F32)
                o_ref[...] = o / l
                lse_ref[...] = m + jnp.log(l)

    return pl.pallas_call(
        body, grid=(h_n, s_n // tq),
        in_specs=[pl.BlockSpec((None, tq, dq), lambda h, i: (h, i, 0)),
                  pl.BlockSpec((None, s_n, dq), lambda h, i: (h, 0, 0)),
                  pl.BlockSpec((None, s_n, dv), lambda h, i: (h, 0, 0))],
        out_specs=[pl.BlockSpec((None, tq, dv), lambda h, i: (h, i, 0)),
                   pl.BlockSpec((None, tq, 1), lambda h, i: (h, i, 0))],
        out_shape=[jax.ShapeDtypeStruct((h_n, s_n, dv), F32), jax.ShapeDtypeStruct((h_n, s_n, 1), F32)],
        compiler_params=_cparams(), name=name)(q, k, v)


def _mla_attn_bwd(q, k, v, o, do, lse, *, name):
    h_n, s_n, dq = q.shape
    dv = v.shape[-1]
    tq = MLA_TQ
    scale = float(dq) ** -0.5

    def body(q_ref, k_ref, v_ref, o_ref, do_ref, lse_ref, dq_ref, dk_ref, dv_ref):
        i = pl.program_id(1)

        @pl.when(i == 0)
        def _():
            dk_ref[...] = jnp.zeros_like(dk_ref)
            dv_ref[...] = jnp.zeros_like(dv_ref)

        for e in range(1, s_n // tq + 1):
            @pl.when(i == e - 1)
            def _(ext=e * tq):
                mask = _causal_mask(i, tq, ext)
                qv, kv, vv = q_ref[...], k_ref[0:ext, :], v_ref[0:ext, :]
                do_v = do_ref[...]
                s = lax.dot_general(qv, kv, _NT, preferred_element_type=F32) * scale
                p = jnp.where(mask, jnp.exp(s - lse_ref[...]), 0.0)
                dob = do_v.astype(BF16)
                dv_ref[0:ext, :] += lax.dot_general(p.astype(BF16), dob, _TN, preferred_element_type=F32)
                dp = lax.dot_general(dob, vv, _NT, preferred_element_type=F32)
                delta = jnp.sum(do_v * o_ref[...], axis=-1, keepdims=True)
                dsb = (p * (dp - delta) * scale).astype(BF16)
                dq_ref[...] = jnp.dot(dsb, kv, preferred_element_type=F32)
                dk_ref[0:ext, :] += lax.dot_general(dsb, qv, _TN, preferred_element_type=F32)

    return pl.pallas_call(
        body, grid=(h_n, s_n // tq),
        in_specs=[pl.BlockSpec((None, tq, dq), lambda h, i: (h, i, 0)),
                  pl.BlockSpec((None, s_n, dq), lambda h, i: (h, 0, 0)),
                  pl.BlockSpec((None, s_n, dv), lambda h, i: (h, 0, 0)),
                  pl.BlockSpec((None, tq, dv), lambda h, i: (h, i, 0)),
                  pl.BlockSpec((None, tq, dv), lambda h, i: (h, i, 0)),
                  pl.BlockSpec((None, tq, 1), lambda h, i: (h, i, 0))],
        out_specs=[pl.BlockSpec((None, tq, dq), lambda h, i: (h, i, 0)),
                   pl.BlockSpec((None, s_n, dq), lambda h, i: (h, 0, 0)),
                   pl.BlockSpec((None, s_n, dv), lambda h, i: (h, 0, 0))],
        out_shape=[jax.ShapeDtypeStruct((h_n, s_n, dq), F32), jax.ShapeDtypeStruct((h_n, s_n, dq), F32),
                   jax.ShapeDtypeStruct((h_n, s_n, dv), F32)],
        compiler_params=_cparams(), name=name)(q, k, v, o, do, lse)


def _head_sum(x, *, name):
    h_n, s_n, w = x.shape
    tm = _pick(s_n, 256, 8)

    def body(x_ref, o_ref):
        o_ref[...] = jnp.sum(x_ref[...], axis=0)

    return pl.pallas_call(body, grid=(s_n // tm,), in_specs=[pl.BlockSpec((h_n, tm, w), lambda i: (0, i, 0))],
                          out_specs=_rows(tm, w), out_shape=jax.ShapeDtypeStruct((s_n, w), F32),
                          compiler_params=_cparams(), name=name)(x)


N_BLK = SEQ // DIL_BLOCK
DIL_SCALE = 64 ** -0.5


def _dil_masks():
    iq = lax.broadcasted_iota(jnp.int32, (DIL_BLOCK, 2 * DIL_BLOCK), 0)
    ik = lax.broadcasted_iota(jnp.int32, (DIL_BLOCK, 2 * DIL_BLOCK), 1)
    rel = DIL_BLOCK + iq - ik
    both = (rel >= 0) & (rel <= DIL_BLOCK)
    iq1 = lax.broadcasted_iota(jnp.int32, (DIL_BLOCK, DIL_BLOCK), 0)
    ik1 = lax.broadcasted_iota(jnp.int32, (DIL_BLOCK, DIL_BLOCK), 1)
    return both, ik1 <= iq1


def _dil_block(j, nb):
    lo = j * DIL_BLOCK
    first = j % nb == 0
    k_lo = lo if first else lo - DIL_BLOCK
    b_lo = DIL_BLOCK if first else 0
    return lo, k_lo, b_lo, first


def _dil_attn_fwd(q, k, v, bias, nb, *, name):
    h_n, s_n, e = q.shape

    def body(q_ref, k_ref, v_ref, b_ref, o_ref, lse_ref):
        m_both, m_first = _dil_masks()
        for j in range(N_BLK):
            lo, k_lo, b_lo, first = _dil_block(j, nb)
            qj = q_ref[lo:lo + DIL_BLOCK, :]
            kk = k_ref[k_lo:lo + DIL_BLOCK, :]
            vv = v_ref[k_lo:lo + DIL_BLOCK, :]
            s = lax.dot_general(qj, kk, _NT, preferred_element_type=F32) * DIL_SCALE + b_ref[:, b_lo:]
            s = jnp.where(m_first if first else m_both, s, -jnp.inf)
            m = jnp.max(s, axis=-1, keepdims=True)
            lse = m + jnp.log(jnp.sum(jnp.exp(s - m), axis=-1, keepdims=True))
            p = jnp.exp(s - lse)
            o_ref[lo:lo + DIL_BLOCK, :] = jnp.dot(p.astype(BF16), vv, preferred_element_type=F32)
            lse_ref[lo:lo + DIL_BLOCK, :] = lse

    head = lambda w: pl.BlockSpec((None, s_n, w), lambda h: (h, 0, 0))
    return pl.pallas_call(
        body, grid=(h_n,),
        in_specs=[head(e), head(e), head(e), pl.BlockSpec((None, DIL_BLOCK, 2 * DIL_BLOCK), lambda h: (h, 0, 0))],
        out_specs=[head(e), head(1)],
        out_shape=[jax.ShapeDtypeStruct((h_n, s_n, e), F32), jax.ShapeDtypeStruct((h_n, s_n, 1), F32)],
        compiler_params=_cparams(), name=name)(q, k, v, bias)


def _dil_attn_bwd(q, k, v, bias, lse, do, dlt, nb, *, name):
    h_n, s_n, e = q.shape

    def body(q_ref, k_ref, v_ref, b_ref, lse_ref, do_ref, dlt_ref, dq_ref, dk_ref, dv_ref, db_ref):
        dk_ref[...] = jnp.zeros_like(dk_ref)
        dv_ref[...] = jnp.zeros_like(dv_ref)
        db_ref[...] = jnp.zeros_like(db_ref)
        m_both, m_first = _dil_masks()
        for j in range(N_BLK):
            lo, k_lo, b_lo, first = _dil_block(j, nb)
            qj = q_ref[lo:lo + DIL_BLOCK, :]
            kk = k_ref[k_lo:lo + DIL_BLOCK, :]
            vv = v_ref[k_lo:lo + DIL_BLOCK, :]
            s = lax.dot_general(qj, kk, _NT, preferred_element_type=F32) * DIL_SCALE + b_ref[:, b_lo:]
            p = jnp.where(m_first if first else m_both, jnp.exp(s - lse_ref[lo:lo + DIL_BLOCK, :]), 0.0)
            dob = do_ref[lo:lo + DIL_BLOCK, :].astype(BF16)
            dv_ref[k_lo:lo + DIL_BLOCK, :] += lax.dot_general(p.astype(BF16), dob, _TN, preferred_element_type=F32)
            dp = lax.dot_general(dob, vv, _NT, preferred_element_type=F32)
            ds = p * (dp - dlt_ref[lo:lo + DIL_BLOCK, :])
            db_ref[:, b_lo:] += ds
            dsb = (ds * DIL_SCALE).astype(BF16)
            dq_ref[lo:lo + DIL_BLOCK, :] = jnp.dot(dsb, kk, preferred_element_type=F32)
            dk_ref[k_lo:lo + DIL_BLOCK, :] += lax.dot_general(dsb, qj, _TN, preferred_element_type=F32)

    head = lambda w: pl.BlockSpec((None, s_n, w), lambda h: (h, 0, 0))
    b_spec = pl.BlockSpec((None, DIL_BLOCK, 2 * DIL_BLOCK), lambda h: (h, 0, 0))
    sd = jax.ShapeDtypeStruct((h_n, s_n, e), F32)
    return pl.pallas_call(
        body, grid=(h_n,),
        in_specs=[head(e), head(e), head(e), b_spec, head(1), head(e), head(1)],
        out_specs=[head(e), head(e), head(e), b_spec],
        out_shape=[sd, sd, sd, jax.ShapeDtypeStruct((h_n, DIL_BLOCK, 2 * DIL_BLOCK), F32)],
        compiler_params=_cparams(), name=name)(q, k, v, bias, lse, do, dlt)


def _group_alpha(l_refs):
    ls = [r[...] for r in l_refs]
    m = jnp.maximum(jnp.maximum(ls[0], ls[1]), ls[2])
    es = [jnp.exp(l - m) for l in ls]
    tot = es[0] + es[1] + es[2]
    return [ex / tot for ex in es]


def _dil_mix_fwd(os_, ls_, *, name):
    h_n, s_n, e = os_[0].shape
    tm = 512

    def body(o0, o1, o2, l0, l1, l2, out_ref):
        al = _group_alpha((l0, l1, l2))
        out_ref[...] = al[0] * o0[...] + al[1] * o1[...] + al[2] * o2[...]

    blk = lambda w: pl.BlockSpec((None, tm, w), lambda h, i: (h, i, 0))
    return pl.pallas_call(body, grid=(h_n, s_n // tm), in_specs=[blk(e)] * 3 + [blk(1)] * 3, out_specs=blk(e),
                          out_shape=jax.ShapeDtypeStruct((h_n, s_n, e), F32), compiler_params=_cparams(),
                          name=name)(*os_, *ls_)


def _dil_mix_bwd(do, os_, ls_, *, name):
    h_n, s_n, e = do.shape
    tm = 512

    def body(do_ref, o0, o1, o2, l0, l1, l2, d0, d1, d2, t0, t1, t2):
        al = _group_alpha((l0, l1, l2))
        do_v = do_ref[...]
        mix = al[0] * o0[...] + al[1] * o1[...] + al[2] * o2[...]
        dbar = jnp.sum(do_v * mix, axis=-1, keepdims=True)
        for a_g, d_ref, t_ref in zip(al, (d0, d1, d2), (t0, t1, t2)):
            d_ref[...] = a_g * do_v
            t_ref[...] = a_g * dbar

    blk = lambda w: pl.BlockSpec((None, tm, w), lambda h, i: (h, i, 0))
    sd_e = jax.ShapeDtypeStruct((h_n, s_n, e), F32)
    sd_1 = jax.ShapeDtypeStruct((h_n, s_n, 1), F32)
    outs = pl.pallas_call(body, grid=(h_n, s_n // tm), in_specs=[blk(e)] * 4 + [blk(1)] * 3,
                          out_specs=[blk(e)] * 3 + [blk(1)] * 3, out_shape=[sd_e] * 3 + [sd_1] * 3,
                          compiler_params=_cparams(), name=name)(do, *os_, *ls_)
    return outs[:3], outs[3:]


def _bias_grad(ds, bucket, *, name):
    h_n = ds.shape[0]

    def body(ds_ref, bk_ref, o_ref):
        ds_v = ds_ref[...]
        bk = bk_ref[...]
        lane = lax.broadcasted_iota(jnp.int32, (1, N_BUCKETS), 1)
        acc = jnp.zeros((1, N_BUCKETS), F32)
        for b in range(N_BUCKETS):
            tot = jnp.sum(jnp.sum(jnp.where(bk == b, ds_v, 0.0), axis=1, keepdims=True), axis=0, keepdims=True)
            acc = acc + jnp.where(lane == b, tot, 0.0)
        o_ref[...] = acc

    return pl.pallas_call(
        body, grid=(h_n,),
        in_specs=[pl.BlockSpec((None, DIL_BLOCK, 2 * DIL_BLOCK), lambda h: (h, 0, 0)),
                  pl.BlockSpec((DIL_BLOCK, 2 * DIL_BLOCK), lambda h: (0, 0))],
        out_specs=pl.BlockSpec((None, 1, N_BUCKETS), lambda h: (h, 0, 0)),
        out_shape=jax.ShapeDtypeStruct((h_n, 1, N_BUCKETS), F32), compiler_params=_cparams(), name=name)(ds, bucket)


def _bias_table(rb, bucket, *, name):
    h_n = rb.shape[0]

    def body(rb_ref, bk_ref, o_ref):
        bk = bk_ref[...]
        row = rb_ref[...]
        acc = jnp.zeros(bk.shape, F32)
        for b in range(N_BUCKETS):
            acc = jnp.where(bk == b, row[:, b:b + 1], acc)
        o_ref[...] = acc

    return pl.pallas_call(
        body, grid=(h_n,),
        in_specs=[pl.BlockSpec((None, 1, N_BUCKETS), lambda h: (h, 0, 0)),
                  pl.BlockSpec((DIL_BLOCK, 2 * DIL_BLOCK), lambda h: (0, 0))],
        out_specs=pl.BlockSpec((None, DIL_BLOCK, 2 * DIL_BLOCK), lambda h: (h, 0, 0)),
        out_shape=jax.ShapeDtypeStruct((h_n, DIL_BLOCK, 2 * DIL_BLOCK), F32), compiler_params=_cparams(),
        name=name)(rb, bucket)


def _row_tile(rows, cols, budget=1 << 20):
    if rows * cols * 4 <= budget or rows % 8:
        return rows
    best = 8
    for t in range(8, rows + 1, 8):
        if rows % t == 0 and t * cols * 4 <= budget:
            best = t
    return best


def _adamw(w, g, m, v, *, name):
    shape = w.shape
    cols = shape[-1]
    rows = math.prod(shape[:-1]) if len(shape) > 1 else 1
    to2 = lambda t: t.reshape(rows, cols)
    tr = _row_tile(rows, cols)
    c1 = 1.0 / (1.0 - ADAM_B1 ** ADAM_STEP)
    c2 = 1.0 / (1.0 - ADAM_B2 ** ADAM_STEP)

    def body(w_ref, g_ref, m_ref, v_ref, d_ref, nm_ref, nv_ref):
        g_v = g_ref[...]
        nm = ADAM_B1 * m_ref[...] + (1.0 - ADAM_B1) * g_v
        nv = ADAM_B2 * v_ref[...] + (1.0 - ADAM_B2) * (g_v * g_v)
        m_hat = nm * c1
        v_hat = nv * c2
        d_ref[...] = -ADAM_LR * (m_hat / (jnp.sqrt(v_hat) + ADAM_EPS) + ADAM_WD * w_ref[...])
        nm_ref[...] = nm
        nv_ref[...] = nv

    blk = pl.BlockSpec((tr, cols), lambda i: (i, 0))
    sd = jax.ShapeDtypeStruct((rows, cols), F32)
    outs = pl.pallas_call(body, grid=(rows // tr,), in_specs=[blk] * 4, out_specs=[blk] * 3, out_shape=[sd] * 3,
                          compiler_params=_cparams(), name=name)(to2(w), to2(g), to2(m), to2(v))
    return tuple(t.reshape(shape) for t in outs)


def _add_half(unit, got, half_idx, *, name):
    rest = unit.shape[2:]
    c = rest[-1]
    r = math.prod(rest[:-1])
    tr = _row_tile(r, c)

    def body(idx_ref, u_ref, g_ref, o_ref, w_ref):
        tot = u_ref[...] + g_ref[...].astype(F32)
        o_ref[...] = tot
        w_ref[...] = tot.astype(BF16)

    blk = pl.BlockSpec((None, tr, c), lambda s, i, idx: (s, i, 0))
    grid_spec = pltpu.PrefetchScalarGridSpec(
        num_scalar_prefetch=1, grid=(N_SHARD, r // tr),
        in_specs=[pl.BlockSpec((None, None, tr, c), lambda s, i, idx: (idx[0], s, i, 0)), blk],
        out_specs=[blk, blk])
    out, wire = pl.pallas_call(
        body, grid_spec=grid_spec,
        out_shape=[jax.ShapeDtypeStruct((N_SHARD, r, c), F32), jax.ShapeDtypeStruct((N_SHARD, r, c), BF16)],
        compiler_params=_cparams(), name=name)(half_idx, unit.reshape(2, N_SHARD, r, c), got.reshape(N_SHARD, r, c))
    return out.reshape((N_SHARD,) + rest), wire.reshape((N_SHARD,) + rest)


def _add_shards(part, got, shard_idx, *, name):
    rest = part.shape[1:]
    c = rest[-1]
    r = math.prod(rest[:-1])
    tr = _row_tile(r, c)

    def body(idx_ref, p_ref, g_ref, o_ref):
        acc = p_ref[...]
        for k in range(3):
            acc = acc + g_ref[k].astype(F32)
        o_ref[...] = acc

    grid_spec = pltpu.PrefetchScalarGridSpec(
        num_scalar_prefetch=1, grid=(r // tr,),
        in_specs=[pl.BlockSpec((None, tr, c), lambda i, idx: (idx[0], i, 0)),
                  pl.BlockSpec((3, tr, c), lambda i, idx: (0, i, 0))],
        out_specs=pl.BlockSpec((tr, c), lambda i, idx: (i, 0)))
    out = pl.pallas_call(body, grid_spec=grid_spec, out_shape=jax.ShapeDtypeStruct((r, c), F32),
                         compiler_params=_cparams(), name=name)(
        shard_idx, part.reshape(N_SHARD, r, c), got.reshape(3, r, c))
    return out.reshape(rest)


def _sum_devices(x, n_dev, *, name):
    rows = x.shape[0] // n_dev

    def body(x_ref, o_ref):
        acc = x_ref[0:rows, :]
        for d in range(1, n_dev):
            acc = acc + x_ref[d * rows:(d + 1) * rows, :]
        o_ref[...] = acc

    return pl.pallas_call(body, out_shape=jax.ShapeDtypeStruct((rows, x.shape[1]), F32), name=name)(x)


def _my_pos():
    return lax.axis_index("x"), lax.axis_index("y"), lax.axis_index("c")


def _all_gather(x_blk, *, name, in_vmem):
    m_per, n = x_blk.shape

    def body(x_ref, out_ref, send_sems, recv_sems, local_sem):
        x, y, c = _my_pos()
        me, sibling = (x, y, c), (x, y, 1 - c)
        chips = [(1 - x, y), (x, 1 - y), (1 - x, 1 - y)]

        def rows(px, py, pc):
            return out_ref.at[pl.ds((4 * px + 2 * py + pc) * m_per, m_per), :]

        def copy(k, block, to, src=None):
            return pltpu.make_async_remote_copy(
                src_ref=rows(*block) if src is None else src, dst_ref=rows(*block),
                send_sem=send_sems.at[k], recv_sem=recv_sems.at[k], device_id=to, device_id_type=MESH)

        mine = pltpu.make_async_copy(x_ref, rows(*me), local_sem)
        mine.start()
        first = [copy(0, me, sibling, src=x_ref)]
        first += [copy(1 + j, me, (*chip, c), src=x_ref) for j, chip in enumerate(chips)]
        for cp in first:
            cp.start()
        passed = [copy(4 + j, (*chip, c), sibling) for j, chip in enumerate(chips)]
        for j, chip in enumerate(chips):
            copy(1 + j, (*chip, c), me).wait_recv()
            passed[j].start()
        copy(0, sibling, me).wait_recv()
        for j, chip in enumerate(chips):
            copy(4 + j, (*chip, 1 - c), me).wait_recv()
        for cp in first + passed:
            cp.wait_send()
        mine.wait()

    space = pltpu.VMEM if in_vmem else pl.ANY
    return pl.pallas_call(
        body, out_shape=jax.ShapeDtypeStruct((8 * m_per, n), x_blk.dtype),
        in_specs=[pl.BlockSpec(memory_space=space)], out_specs=pl.BlockSpec(memory_space=space),
        scratch_shapes=[pltpu.SemaphoreType.DMA((7,)), pltpu.SemaphoreType.DMA((7,)), pltpu.SemaphoreType.DMA],
        name=name)(x_blk)


_HBM = pl.BlockSpec(memory_space=pl.ANY)


def _gather_weights(fams, *, name):
    n = len(fams)

    def body(*refs):
        ins, outs = refs[:n], refs[n:2 * n]
        send_sems, recv_sems = refs[2 * n:]
        x, y, c = _my_pos()
        me, sibling = (x, y, c), (x, y, 1 - c)
        chips = [(1 - x, y), (x, 1 - y), (1 - x, 1 - y)]

        def copy(f, k, block, to, src=None):
            px, py, pc = block
            dst = outs[f].at[2 * px + py, pc]
            return pltpu.make_async_remote_copy(
                src_ref=dst if src is None else src, dst_ref=dst, send_sem=send_sems.at[7 * f + k],
                recv_sem=recv_sems.at[7 * f + k], device_id=to, device_id_type=MESH)

        first, passed = [], []
        for f in range(n):
            src = ins[f].at[c]
            first.append(copy(f, 0, me, sibling, src=src))
            first += [copy(f, 1 + j, me, (*chip, c), src=src) for j, chip in enumerate(chips)]
        for cp in first:
            cp.start()
        for j, chip in enumerate(chips):
            for f in range(n):
                copy(f, 1 + j, (*chip, c), me).wait_recv()
                passed.append(copy(f, 4 + j, (*chip, c), sibling))
                passed[-1].start()
        for f in range(n):
            copy(f, 0, sibling, me).wait_recv()
        for j, chip in enumerate(chips):
            for f in range(n):
                copy(f, 4 + j, (*chip, 1 - c), me).wait_recv()
        for cp in first + passed:
            cp.wait_send()

    outs = pl.pallas_call(
        body, out_shape=[jax.ShapeDtypeStruct((N_SHARD,) + t.shape, t.dtype) for t in fams],
        in_specs=[_HBM] * n, out_specs=[_HBM] * n,
        scratch_shapes=[pltpu.SemaphoreType.DMA((7 * n,)), pltpu.SemaphoreType.DMA((7 * n,))], name=name)(*fams)
    x, y, c = _my_pos()
    place = lambda o, t: lax.dynamic_update_slice(
        o, lax.dynamic_index_in_dim(t, c, 0, keepdims=True)[None], (2 * x + y, c) + (0,) * (t.ndim - 1))
    return [place(o, t) for o, t in zip(outs, fams)]


def _swap_halves(units, *, name):
    n = len(units)

    def body(*refs):
        ins, outs = refs[:n], refs[n:2 * n]
        send_sems, recv_sems = refs[2 * n:]
        x, y, c = _my_pos()
        cps = [pltpu.make_async_remote_copy(src_ref=ins[f].at[1 - c], dst_ref=outs[f], send_sem=send_sems.at[f],
                                            recv_sem=recv_sems.at[f], device_id=(x, y, 1 - c), device_id_type=MESH)
               for f in range(n)]
        for cp in cps:
            cp.start()
        for cp in cps:
            cp.wait()

    return pl.pallas_call(
        body, out_shape=[jax.ShapeDtypeStruct(t.shape[1:], t.dtype) for t in units],
        in_specs=[_HBM] * n, out_specs=[_HBM] * n,
        scratch_shapes=[pltpu.SemaphoreType.DMA((n,)), pltpu.SemaphoreType.DMA((n,))], name=name)(*units)


def _send_to_chips(parts, *, name):
    n = len(parts)

    def body(*refs):
        ins, outs = refs[:n], refs[n:2 * n]
        send_sems, recv_sems = refs[2 * n:]
        x, y, c = _my_pos()
        chips = [(1 - x, y), (x, 1 - y), (1 - x, 1 - y)]
        cps = [pltpu.make_async_remote_copy(src_ref=ins[f].at[2 * cx + cy], dst_ref=outs[f].at[k],
                                            send_sem=send_sems.at[3 * f + k], recv_sem=recv_sems.at[3 * f + k],
                                            device_id=(cx, cy, c), device_id_type=MESH)
               for f in range(n) for k, (cx, cy) in enumerate(chips)]
        for cp in cps:
            cp.start()
        for cp in cps:
            cp.wait()

    return pl.pallas_call(
        body, out_shape=[jax.ShapeDtypeStruct((3,) + t.shape[1:], t.dtype) for t in parts],
        in_specs=[_HBM] * n, out_specs=[_HBM] * n,
        scratch_shapes=[pltpu.SemaphoreType.DMA((3 * n,)), pltpu.SemaphoreType.DMA((3 * n,))], name=name)(*parts)


def _pair_gather(halves, *, name):
    n = len(halves)

    def body(*refs):
        ins, outs = refs[:n], refs[n:2 * n]
        send_sems, recv_sems = refs[2 * n:]
        x, y, c = _my_pos()
        cps = [pltpu.make_async_remote_copy(src_ref=ins[f], dst_ref=outs[f].at[c], send_sem=send_sems.at[f],
                                            recv_sem=recv_sems.at[f], device_id=(x, y, 1 - c), device_id_type=MESH)
               for f in range(n)]
        for cp in cps:
            cp.start()
        for f in range(n):
            pltpu.make_async_remote_copy(src_ref=ins[f], dst_ref=outs[f].at[1 - c], send_sem=send_sems.at[f],
                                         recv_sem=recv_sems.at[f], device_id=(x, y, 1 - c),
                                         device_id_type=MESH).wait_recv()
        for cp in cps:
            cp.wait_send()

    outs = pl.pallas_call(
        body, out_shape=[jax.ShapeDtypeStruct((2,) + t.shape, t.dtype) for t in halves],
        in_specs=[_HBM] * n, out_specs=[_HBM] * n,
        scratch_shapes=[pltpu.SemaphoreType.DMA((n,)), pltpu.SemaphoreType.DMA((n,))], name=name)(*halves)
    c = lax.axis_index("c")
    return [lax.dynamic_update_index_in_dim(o, t, c, 0) for o, t in zip(outs, halves)]


def _to_heads(t, width):
    return t.reshape(t.shape[0], HEADS, width).transpose(1, 0, 2)


def _from_heads(t):
    return t.transpose(1, 0, 2).reshape(t.shape[1], -1)


def _residue_major(t, d):
    h_n, s_n, e = t.shape
    return t.reshape(h_n, s_n // d, d, e).transpose(0, 2, 1, 3).reshape(h_n, s_n, e)


def _token_major(t, d):
    h_n, s_n, e = t.shape
    return t.reshape(h_n, d, s_n // d, e).transpose(0, 2, 1, 3).reshape(h_n, s_n, e)


def _t5_bucket(dist):
    max_exact = N_BUCKETS // 2
    d = jnp.maximum(dist, 1).astype(F32)
    large = max_exact + (jnp.log(d / max_exact) / math.log(MAX_DISTANCE / max_exact)
                         * (N_BUCKETS - max_exact)).astype(jnp.int32)
    large = jnp.minimum(large, N_BUCKETS - 1)
    return jnp.where(dist < max_exact, dist, large)


def _bucket_map(dilation):
    iq = jnp.arange(DIL_BLOCK)[:, None]
    ik = jnp.arange(2 * DIL_BLOCK)[None, :]
    rel = DIL_BLOCK + iq - ik
    return _t5_bucket(jnp.maximum(rel, 0) * dilation).astype(jnp.int32)


def _q_perm(w):
    w3 = w.reshape(w.shape[0], HEADS, QK_NOPE + QK_ROPE)
    return jnp.concatenate([w3[:, :, :QK_NOPE].reshape(w.shape[0], -1),
                            w3[:, :, QK_NOPE:QK_NOPE + HALF_ROPE].reshape(w.shape[0], -1),
                            w3[:, :, QK_NOPE + HALF_ROPE:].reshape(w.shape[0], -1)], axis=1)


def _q_unperm(w):
    n0, n1 = HEADS * QK_NOPE, HEADS * HALF_ROPE
    r = w.shape[0]
    return jnp.concatenate([w[:, :n0].reshape(r, HEADS, QK_NOPE), w[:, n0:n0 + n1].reshape(r, HEADS, HALF_ROPE),
                            w[:, n0 + n1:].reshape(r, HEADS, HALF_ROPE)], axis=2).reshape(r, -1)


def _kv_perm(w):
    w3 = w.reshape(w.shape[0], HEADS, QK_NOPE + V_HEAD)
    return jnp.concatenate([w3[:, :, :QK_NOPE].reshape(w.shape[0], -1), w3[:, :, QK_NOPE:].reshape(w.shape[0], -1)],
                           axis=1)


def _kv_unperm(w):
    n0 = HEADS * QK_NOPE
    r = w.shape[0]
    return jnp.concatenate([w[:, :n0].reshape(r, HEADS, QK_NOPE), w[:, n0:].reshape(r, HEADS, V_HEAD)],
                           axis=2).reshape(r, -1)


def _row(v):
    return v.reshape(1, -1)


def kernel(x, c, norm_pre, norm_post, w_mod, b_mod, ffn_w_gate, ffn_w_up, ffn_w_down, mla_w_in, mla_q_norm, mla_w_q_up, mla_kv_norm, mla_w_kv_up, mla_w_o, dil_w_in, dil_w_o, rel_bias, loss_target, m_norm_pre, m_norm_post, m_w_mod, m_b_mod, m_ffn_w_gate, m_ffn_w_up, m_ffn_w_down, m_mla_w_in, m_mla_q_norm, m_mla_w_q_up, m_mla_kv_norm, m_mla_w_kv_up, m_mla_w_o, m_dil_w_in, m_dil_w_o, m_rel_bias, v_norm_pre, v_norm_post, v_w_mod, v_b_mod, v_ffn_w_gate, v_ffn_w_up, v_ffn_w_down, v_mla_w_in, v_mla_q_norm, v_mla_w_q_up, v_mla_kv_norm, v_mla_w_kv_up, v_mla_w_o, v_dil_w_in, v_dil_w_o, v_rel_bias):
    given = dict(locals())
    ix, iy, ic = _my_pos()
    shard_id = 2 * ix + iy
    dev_id = 4 * ix + 2 * iy + ic
    x2 = x[0]
    target = loss_target[0]
    half_idx = jnp.reshape(ic, (1,)).astype(jnp.int32)
    shard_idx = jnp.reshape(shard_id, (1,)).astype(jnp.int32)

    blk = jnp.zeros((8, D_MODEL), F32)
    blk = blk.at[0].set(c[0])
    blk = blk.at[1:3].set(jnp.pad(norm_pre.reshape(-1), (0, 512)).reshape(2, D_MODEL))
    blk = blk.at[3:5].set(jnp.pad(norm_post.reshape(-1), (0, 512)).reshape(2, D_MODEL))
    got = _all_gather(blk, name="ag_c_norms", in_vmem=True).reshape(N_SHARD, 2, 8, D_MODEL)
    c_all = got[:, :, 0, :].reshape(8, D_MODEL)

    def full_norm(lo):
        t = got[:, 0, lo:lo + 2, :].reshape(N_SHARD, 2 * D_MODEL)[:, :1536].reshape(N_SHARD, 2, 3, 256)
        return t.transpose(1, 2, 0, 3).reshape(2, 3, D_MODEL)

    pre_full, post_full = full_norm(1), full_norm(3)

    silu_c = _silu_bf16(c_all, name="silu_c")
    b_cols = lax.dynamic_slice_in_dim(b_mod, shard_id * 2304, 2304, axis=1).reshape(2, 1, 2304)
    mod_part = _mm(silu_c, w_mod, bias=b_cols, name="mod_mm", tn_cap=768)
    mod_all = _all_gather(mod_part.reshape(16, 2304), name="ag_mod", in_vmem=True)
    mod_all = mod_all.reshape(N_SHARD, 2, 2, 8, 2304)[:, 0]
    mod_mine = lax.dynamic_index_in_dim(mod_all, dev_id, axis=2, keepdims=False)
    mod = mod_mine.transpose(1, 0, 2).reshape(2, 9, D_MODEL)

    fams = [jnp.concatenate([ffn_w_gate, ffn_w_up], axis=1),
            ffn_w_down,
            mla_w_in.reshape(2, 128, -1), mla_w_q_up.reshape(2, 192, -1), mla_w_kv_up.reshape(2, 128, -1),
            mla_w_o.reshape(2, 128, D_MODEL), dil_w_in.reshape(2, 512, -1), dil_w_o.reshape(2, 128, D_MODEL)]
    full = _gather_weights([t.astype(BF16) for t in fams], name="ag_weights")
    w_gu, w_dn = full[0], full[1]
    w_in = full[2].reshape(D_MODEL, -1)
    wq_p = _q_perm(full[3].reshape(N_SHARD, Q_LORA, -1).transpose(1, 0, 2).reshape(Q_LORA, -1))
    wkv_p = _kv_perm(full[4].reshape(N_SHARD, KV_LORA, -1).transpose(1, 0, 2).reshape(KV_LORA, -1))
    w_mo = full[5].reshape(D_MODEL, D_MODEL)
    w_di = full[6].reshape(N_SHARD, D_MODEL, -1)
    w_do = full[7].reshape(D_MODEL, D_MODEL)
    gu_sel = lambda i, h: (lambda g: (g // 2, i, (g % 2) * 2 + h))
    dn_sel = lambda i, h: (lambda g: (g, i, h))

    pos = jnp.arange(SEQ, dtype=F32)
    freqs = ROPE_THETA ** (-jnp.arange(HALF_ROPE, dtype=F32) / HALF_ROPE)
    ang = pos[:, None] * freqs[None, :]
    cos_k, sin_k = jnp.cos(ang), jnp.sin(ang)
    cos_q, sin_q = jnp.tile(cos_k, (1, HEADS)), jnp.tile(sin_k, (1, HEADS))

    buckets = [_bucket_map(d) for _, d in DIL_GROUPS]
    biases = [_bias_table(rel_bias[:, g * HEADS:(g + 1) * HEADS].T.reshape(HEADS, 1, N_BUCKETS), bk,
                          name=f"dil_bias_table_g{g}") for g, bk in enumerate(buckets)]

    def sub_params(i, sub):
        return dict(pg=_row(pre_full[i, sub]), qg=_row(post_full[i, sub]), sh=_row(mod[i, 3 * sub]),
                    sc=_row(mod[i, 3 * sub + 1]), gate=_row(mod[i, 3 * sub + 2]))

    def ffn_fwd(xin, i, h, sub):
        p = sub_params(i, sub)
        tag = f"l{i}s{sub}"
        hn = _pre_fwd(xin, p['pg'], p['sc'], p['sh'], name=f"pre_fwd_{tag}")
        gu, a = _ffn_up(hn, w_gu, i, h, name=f"ffn_up_{tag}")
        f = _mm(a, w_dn, g_n=N_SHARD, b_sel=dn_sel(i, h), reduce_g=True, name=f"ffn_down_{tag}")
        out = _post_fwd(f, xin, p['qg'], p['gate'], FFN_RES, name=f"post_fwd_{tag}")
        return out, dict(x=xin, hn=hn, gu=gu, a=a, f=f, p=p, i=i, h=h, tag=tag)

    def mla_fwd(xin, i, sub):
        p = sub_params(i, sub)
        tag = f"l{i}s{sub}"
        hn = _pre_fwd(xin, p['pg'], p['sc'], p['sh'], name=f"pre_fwd_{tag}")
        lat = _mm(hn, w_in, name="mla_lat")
        cq, ckv = lat[:, :Q_LORA], lat[:, Q_LORA:Q_LORA + KV_LORA]
        k1, k2 = lat[:, Q_LORA + KV_LORA:Q_LORA + KV_LORA + HALF_ROPE], lat[:, Q_LORA + KV_LORA + HALF_ROPE:]
        cqn = _rms_fwd(cq, mla_q_norm, name="mla_qnorm")
        ckvn = _rms_fwd(ckv, mla_kv_norm, name="mla_kvnorm")
        qp = _mm(cqn, wq_p, name="mla_q_up")
        kvp = _mm(ckvn, wkv_p, name="mla_kv_up")
        n0, n1 = HEADS * QK_NOPE, HEADS * HALF_ROPE
        qr1, qr2 = _rope(qp[:, n0:n0 + n1], qp[:, n0 + n1:], cos_q, sin_q, name="rope_q")
        kr1, kr2 = _rope(k1, k2, cos_k, sin_k, name="rope_k")
        q = jnp.concatenate([qp[:, :n0].reshape(SEQ, HEADS, QK_NOPE), qr1.reshape(SEQ, HEADS, HALF_ROPE),
                             qr2.reshape(SEQ, HEADS, HALF_ROPE)], axis=2).transpose(1, 0, 2).astype(BF16)
        kr = jnp.broadcast_to(jnp.concatenate([kr1, kr2], axis=1)[:, None, :], (SEQ, HEADS, QK_ROPE))
        k = jnp.concatenate([kvp[:, :n0].reshape(SEQ, HEADS, QK_NOPE), kr], axis=2).transpose(1, 0, 2).astype(BF16)
        v = _to_heads(kvp[:, n0:], V_HEAD).astype(BF16)
        o, lse = _mla_attn_fwd(q, k, v, name="mla_attn_fwd")
        o_flat = _from_heads(o).astype(BF16)
        f = _mm(o_flat, w_mo, name="mla_out")
        out = _post_fwd(f, xin, p['qg'], p['gate'], 1.0, name=f"post_fwd_{tag}")
        return out, dict(x=xin, hn=hn, cq=cq, ckv=ckv, cqn=cqn, ckvn=ckvn, q=q, k=k, v=v, o=o, lse=lse,
                         o_flat=o_flat, f=f, p=p, tag=tag)

    def dil_fwd(xin, i, sub):
        p = sub_params(i, sub)
        tag = f"l{i}s{sub}"
        hn = _pre_fwd(xin, p['pg'], p['sc'], p['sh'], name=f"pre_fwd_{tag}")
        proj = _mm(hn, w_di, out_dtype=BF16, tn_cap=768, name="dil_proj")
        heads = proj.reshape(N_SHARD, SEQ, 36, 64).transpose(0, 2, 1, 3).reshape(3, 3, HEADS, SEQ, 64)
        qkv, outs, lses = [], [], []
        for g, (window, d) in enumerate(DIL_GROUPS):
            q, k, v = (_residue_major(heads[g, t], d) for t in range(3))
            o, lse = _dil_attn_fwd(q, k, v, biases[g], SEQ // d // DIL_BLOCK, name=f"dil_attn_fwd_g{g}")
            qkv.append((q, k, v))
            outs.append(_token_major(o, d))
            lses.append(_token_major(lse, d))
        mix = _dil_mix_fwd(outs, lses, name="dil_mix_fwd")
        o_flat = _from_heads(mix).astype(BF16)
        f = _mm(o_flat, w_do, name="dil_out")
        out = _post_fwd(f, xin, p['qg'], p['gate'], 1.0, name=f"post_fwd_{tag}")
        return out, dict(x=xin, hn=hn, qkv=qkv, outs=outs, lses=lses, o_flat=o_flat, f=f, p=p, tag=tag)

    xs = x2
    saved = []
    for i in range(2):
        xs, sv = ffn_fwd(xs, i, 0, 0)
        saved.append(sv)
        xs, sv = (mla_fwd if i == 0 else dil_fwd)(xs, i, 1)
        saved.append(sv)
        xs, sv = ffn_fwd(xs, i, 1, 2)
        saved.append(sv)

    dx, loss_part = _loss(xs, target, name="loss")
    loss = lax.psum(loss_part[0, 0], ("x", "y", "c"))

    dmod = [[None] * 9 for _ in range(2)]
    dpre = [[None] * 3 for _ in range(2)]
    dpost = [[None] * 3 for _ in range(2)]
    gu_shape = (2, N_SHARD, 2, 2, D_MODEL, F_SHARD)
    dn_shape = (2, N_SHARD, 2, F_SHARD, D_MODEL)
    bufs = dict(gu=lax.empty(gu_shape, F32), dn=lax.empty(dn_shape, F32))
    row_unit = lambda g, r: ((r % 2, r // 2), 0)

    def close_sub(dhn, dout, sv, i, sub, res_dgate, res_dqg):
        p = sv['p']
        dxs, dsh, dsc, dpg = _pre_bwd(dhn, sv['x'], dout, p['pg'], p['sc'], name=f"pre_bwd_{sv['tag']}")
        dmod[i][3 * sub], dmod[i][3 * sub + 1], dmod[i][3 * sub + 2] = dsh, dsc, res_dgate
        dpre[i][sub], dpost[i][sub] = dpg, res_dqg
        return dxs

    def ffn_bwd(dout, sv, sub):
        i, h, p, tag = sv['i'], sv['h'], sv['p'], sv['tag']
        df, dgate, dqg = _post_bwd(dout, sv['f'], p['qg'], p['gate'], FFN_RES, name=f"post_bwd_{tag}")
        bufs['dn'] = _mm(sv['a'], df, ta=True, out_shape=dn_shape, out_sel=lambda g, r: ((i, g, h), r),
                         out_buf=bufs['dn'], name=f"ffn_dwd_{tag}")
        dgu = _ffn_dgu(df, w_dn, sv['gu'], i, h, name=f"ffn_dgu_{tag}")
        dgu = dgu.reshape(2 * N_SHARD, SEQ, F_SHARD)
        bufs['gu'] = _mm(sv['hn'], dgu, ta=True, out_shape=gu_shape,
                         out_sel=lambda g, r: ((i, g // 2, g % 2, h), r), out_buf=bufs['gu'],
                         name=f"ffn_dwgu_{tag}")
        dhn = _mm(dgu, w_gu, g_n=2 * N_SHARD, b_sel=gu_sel(i, h), tb=True, reduce_g=True, name=f"ffn_dhn_{tag}")
        return close_sub(dhn, dout, sv, i, sub, dgate, dqg)

    def mla_bwd(dout, sv, i, sub):
        p, tag = sv['p'], sv['tag']
        df, dgate, dqg = _post_bwd(dout, sv['f'], p['qg'], p['gate'], 1.0, name=f"post_bwd_{tag}")
        u_wo = _mm(sv['o_flat'], df, ta=True, tm_cap=128, out_shape=(2, N_SHARD, 128, D_MODEL), out_sel=row_unit,
                   name="mla_dwo")
        do_flat = _mm(df, w_mo, tb=True, name="mla_do")
        do = _to_heads(do_flat, V_HEAD)
        dq, dk, dv = _mla_attn_bwd(sv['q'], sv['k'], sv['v'], sv['o'], do, sv['lse'], name="mla_attn_bwd")
        dq_t = dq.transpose(1, 0, 2)
        dqr1, dqr2 = _rope(dq_t[:, :, QK_NOPE:QK_NOPE + HALF_ROPE].reshape(SEQ, -1),
                           dq_t[:, :, QK_NOPE + HALF_ROPE:].reshape(SEQ, -1), cos_q, -sin_q, name="rope_q_bwd")
        dqp = jnp.concatenate([dq_t[:, :, :QK_NOPE].reshape(SEQ, -1), dqr1, dqr2], axis=1).astype(BF16)
        dkr = _head_sum(dk[:, :, QK_NOPE:], name="mla_dkr_sum")
        dk1, dk2 = _rope(dkr[:, :HALF_ROPE], dkr[:, HALF_ROPE:], cos_k, -sin_k, name="rope_k_bwd")
        dkvp = jnp.concatenate([_from_heads(dk[:, :, :QK_NOPE]), _from_heads(dv)], axis=1).astype(BF16)
        g_wq = _q_unperm(_mm(sv['cqn'], dqp, ta=True, name="mla_dwq"))
        g_wkv = _kv_unperm(_mm(sv['ckvn'], dkvp, ta=True, name="mla_dwkv"))
        dcqn = _mm(dqp, wq_p, tb=True, name="mla_dcqn")
        dckvn = _mm(dkvp, wkv_p, tb=True, name="mla_dckvn")
        dcq, g_qn = _rms_bwd(dcqn, sv['cq'], mla_q_norm, name="mla_qnorm_bwd")
        dckv, g_kvn = _rms_bwd(dckvn, sv['ckv'], mla_kv_norm, name="mla_kvnorm_bwd")
        dlat = jnp.concatenate([dcq, dckv, dk1, dk2], axis=1).astype(BF16)
        u_win = _mm(sv['hn'], dlat, ta=True, tm_cap=128, out_shape=(2, N_SHARD, 128, dlat.shape[1]),
                    out_sel=row_unit, name="mla_dwin")
        dhn = _mm(dlat, w_in, tb=True, name="mla_dhn")
        col_unit = lambda t: (t.reshape(t.shape[0], N_SHARD, -1).transpose(1, 0, 2)
                              .reshape(N_SHARD, 2, t.shape[0] // 2, -1).transpose(1, 0, 2, 3))
        grads = dict(units=[u_win, col_unit(g_wq), col_unit(g_wkv), u_wo], q_norm=g_qn, kv_norm=g_kvn)
        return close_sub(dhn, dout, sv, i, sub, dgate, dqg), grads

    def dil_bwd(dout, sv, i, sub):
        p, tag = sv['p'], sv['tag']
        df, dgate, dqg = _post_bwd(dout, sv['f'], p['qg'], p['gate'], 1.0, name=f"post_bwd_{tag}")
        u_wo = _mm(sv['o_flat'], df, ta=True, tm_cap=128, out_shape=(2, N_SHARD, 128, D_MODEL), out_sel=row_unit,
                   name="dil_dwo")
        do = _to_heads(_mm(df, w_do, tb=True, name="dil_do"), 64)
        dos, dlts = _dil_mix_bwd(do, sv['outs'], sv['lses'], name="dil_mix_bwd")
        pieces = []
        bias_rows = []
        for g, (window, d) in enumerate(DIL_GROUPS):
            q, k, v = sv['qkv'][g]
            dq, dk, dv, dbias = _dil_attn_bwd(q, k, v, biases[g], _residue_major(sv['lses'][g], d),
                                              _residue_major(dos[g], d), _residue_major(dlts[g], d),
                                              SEQ // d // DIL_BLOCK, name=f"dil_attn_bwd_g{g}")
            pieces += [_token_major(t, d).astype(BF16) for t in (dq, dk, dv)]
            bias_rows.append(_bias_grad(dbias, buckets[g], name=f"dil_bias_grad_g{g}")[:, 0, :])
        dheads = jnp.stack(pieces).reshape(N_SHARD, 36, SEQ, 64).transpose(0, 2, 1, 3).reshape(N_SHARD, SEQ, 2304)
        u_win = _mm(sv['hn'], dheads, ta=True, tn_cap=768, out_shape=(2, N_SHARD, 512, 2304),
                    out_sel=lambda g, r: ((r, g), 0), name="dil_dwin")
        dhn = _mm(dheads, w_di, tb=True, reduce_g=True, name="dil_dhn")
        g_bias = jnp.concatenate(bias_rows, axis=0).T
        grads = dict(units=[u_win, u_wo], rel_bias=g_bias)
        return close_sub(dhn, dout, sv, i, sub, dgate, dqg), grads

    dx = ffn_bwd(dx, saved[5], 2)
    dx, dil_g = dil_bwd(dx, saved[4], 1, 1)
    dx = ffn_bwd(dx, saved[3], 0)
    dx = ffn_bwd(dx, saved[2], 2)
    dx, mla_g = mla_bwd(dx, saved[1], 0, 1)
    dx = ffn_bwd(dx, saved[0], 0)
    grad_x = dx[None]

    pad_row = lambda v: jnp.pad(v.reshape(-1), (0, (-v.size) % D_MODEL)).reshape(-1, D_MODEL)
    small = jnp.concatenate(
        [jnp.concatenate([dmod[i][r] for i in range(2) for r in range(9)], axis=0),
         jnp.concatenate([dpre[i][s] for i in range(2) for s in range(3)], axis=0),
         jnp.concatenate([dpost[i][s] for i in range(2) for s in range(3)], axis=0),
         pad_row(mla_g['q_norm']), pad_row(mla_g['kv_norm']), pad_row(dil_g['rel_bias'])], axis=0)
    small = jnp.pad(small, ((0, SMALL_ROWS - small.shape[0]), (0, 0)))
    small_all = _all_gather(small, name="ag_small_grads", in_vmem=True)
    small_sum = _sum_devices(small_all, 8, name="sum_small_grads")
    g_b_mod = small_sum[0:18].reshape(2, 9 * D_MODEL)
    my_cols = lambda t: lax.dynamic_slice_in_dim(t, shard_id * 256, 256, axis=2)
    g_norm_pre = my_cols(small_sum[18:24].reshape(2, 3, D_MODEL))
    g_norm_post = my_cols(small_sum[24:30].reshape(2, 3, D_MODEL))
    g_q_norm = small_sum[30, :Q_LORA].reshape(1, Q_LORA)
    g_kv_norm = small_sum[31, :KV_LORA].reshape(1, KV_LORA)
    g_rel_bias = small_sum[32:34].reshape(-1)[:N_BUCKETS * 48].reshape(N_BUCKETS, 48)
    dmod_all = small_all.reshape(8, SMALL_ROWS, D_MODEL)[:, 0:18].reshape(8, 2, 9 * D_MODEL)
    dmod_cols = lax.dynamic_slice_in_dim(dmod_all, shard_id * 2304, 2304, axis=2).transpose(1, 0, 2)
    g_w_mod = _mm(silu_c, dmod_cols.astype(BF16), ta=True, tn_cap=768, name="w_mod_grad")

    units = [bufs['gu'], bufs['dn'], *mla_g['units'], *dil_g['units']]
    got_a = _swap_halves(units, name="rs_sibling")
    parts = [_add_half(u, g, half_idx, name=f"rs_add_half_{k}") for k, (u, g) in enumerate(zip(units, got_a))]
    got_b = _send_to_chips([w for _, w in parts], name="rs_chips")
    reds = [_add_shards(p, g, shard_idx, name=f"rs_add_shards_{k}")
            for k, ((p, _), g) in enumerate(zip(parts, got_b))]
    fin = _pair_gather(reds, name="rs_pair_gather")
    reduced = dict(ffn_w_gate=fin[0][:, 0], ffn_w_up=fin[0][:, 1], ffn_w_down=fin[1])
    for n, t in zip(['mla_w_in', 'mla_w_q_up', 'mla_w_kv_up', 'mla_w_o', 'dil_w_in', 'dil_w_o'], fin[2:]):
        reduced[n] = t.reshape(given[n].shape)

    grads = dict(norm_pre=g_norm_pre, norm_post=g_norm_post, w_mod=g_w_mod, b_mod=g_b_mod, mla_q_norm=g_q_norm,
                 mla_kv_norm=g_kv_norm, rel_bias=g_rel_bias, **reduced)

    deltas, new_m, new_v = {}, {}, {}
    for n in WEIGHTS:
        deltas[n], new_m[n], new_v[n] = _adamw(given[n], grads[n], given["m_" + n], given["v_" + n],
                                               name=f"adamw_{n}")
    return (loss, grad_x, *[grads[n] for n in WEIGHTS], *[deltas[n] for n in WEIGHTS],
            *[new_m[n] for n in WEIGHTS], *[new_v[n] for n in WEIGHTS])
```

```python
import math

import jax
import jax.numpy as jnp
from jax import lax
from jax.experimental import pallas as pl
from jax.experimental.pallas import tpu as pltpu

F32 = jnp.float32
BF16 = jnp.bfloat16
MESH = pl.DeviceIdType.MESH

SEQ = 2048
D_MODEL = 1024
D_FF = 2816
N_SHARD = 4
F_SHARD = D_FF // N_SHARD
EPS = 1e-6
FFN_RES = 0.5
HEADS = 16
Q_LORA, KV_LORA, QK_NOPE, QK_ROPE, V_HEAD = 384, 256, 64, 32, 64
HALF_ROPE = QK_ROPE // 2
ROPE_THETA = 10000.0
DIL_GROUPS = ((128, 1), (512, 4), (2048, 16))
DIL_BLOCK = 128
N_BUCKETS = 32
MAX_DISTANCE = 2048
ADAM_LR, ADAM_B1, ADAM_B2, ADAM_EPS, ADAM_WD, ADAM_STEP = 0.001, 0.9, 0.999, 1e-08, 0.01, 10

VMEM_LIMIT = 48 * 1024 * 1024
SMALL_ROWS = 40

WEIGHTS = ['norm_pre', 'norm_post', 'w_mod', 'b_mod', 'ffn_w_gate', 'ffn_w_up', 'ffn_w_down', 'mla_w_in',
           'mla_q_norm', 'mla_w_q_up', 'mla_kv_norm', 'mla_w_kv_up', 'mla_w_o', 'dil_w_in', 'dil_w_o', 'rel_bias']


def _cparams(**kw):
    return pltpu.CompilerParams(vmem_limit_bytes=VMEM_LIMIT, **kw)


def _pick(n, cap, mult=128):
    if n <= cap:
        return n
    best = n
    for t in range(mult, cap + 1, mult):
        if n % t == 0:
            best = t
    return best


def _mm(a, b, *, name, ta=False, tb=False, reduce_g=False, bias=None, out_dtype=F32, tm_cap=512, tn_cap=1024,
        g_n=None, b_sel=None, out_shape=None, out_sel=None, out_buf=None):
    a3 = a if a.ndim == 3 else a[None]
    ga = a3.shape[0]
    if b_sel is None:
        b_n = b if b.ndim == 3 else b[None]
        gb = b_n.shape[0]
        b_sel = (lambda g: (g,)) if gb > 1 else (lambda g: (0,))
        g_n = max(ga, gb)
    else:
        b_n = b
    k_dim, m_dim = (a3.shape[1], a3.shape[2]) if ta else (a3.shape[2], a3.shape[1])
    k2, n_dim = (b_n.shape[-1], b_n.shape[-2]) if tb else (b_n.shape[-2], b_n.shape[-1])
    assert k_dim == k2, (a.shape, b.shape)
    tm = _pick(m_dim, tm_cap, 128 if ta else 8)
    tn = _pick(n_dim, tn_cap, 128)
    mt, nt = m_dim // tm, n_dim // tn
    dims = (((0 if ta else 1,), (1 if tb else 0,)), ((), ()))

    if reduce_g:
        grid = (mt, nt, g_n)
        ids = lambda i, j, g: (g, i, j)
    else:
        grid = (g_n, mt, nt)
        ids = lambda g, i, j: (g, i, j)

    def a_map(*p):
        g, i, j = ids(*p)
        g = g if ga > 1 else 0
        return (g, 0, i) if ta else (g, i, 0)

    def b_map(*p):
        g, i, j = ids(*p)
        return (*b_sel(g), j, 0) if tb else (*b_sel(g), 0, j)

    b_lead = (None,) * (b_n.ndim - 2)
    a_spec = pl.BlockSpec((None, k_dim, tm) if ta else (None, tm, k_dim), a_map)
    b_spec = pl.BlockSpec(b_lead + ((tn, k_dim) if tb else (k_dim, tn)), b_map)
    in_specs = [a_spec, b_spec]
    operands = [a3, b_n]
    if bias is not None:
        assert not reduce_g and bias.shape == (g_n, 1, n_dim)
        in_specs.append(pl.BlockSpec((None, 1, tn), lambda g, i, j: (g, 0, j)))
        operands.append(bias)
    aliases = {}
    if out_buf is not None:
        assert tuple(out_buf.shape) == tuple(out_shape) and out_buf.dtype == out_dtype
        in_specs.append(pl.BlockSpec(memory_space=pl.ANY))
        operands.append(out_buf)
        aliases = {len(operands) - 1: 0}

    if reduce_g:
        out_spec = pl.BlockSpec((tm, tn), lambda i, j, g: (i, j))
        out_sds = jax.ShapeDtypeStruct((m_dim, n_dim), F32)
    elif out_shape is not None:
        def o_map(g, i, j):
            lead, rb = out_sel(g, i)
            return (*lead, rb, j)

        out_spec = pl.BlockSpec((None,) * (len(out_shape) - 2) + (tm, tn), o_map)
        out_sds = jax.ShapeDtypeStruct(tuple(out_shape), out_dtype)
    else:
        out_spec = pl.BlockSpec((None, tm, tn), lambda g, i, j: (g, i, j))
        out_sds = jax.ShapeDtypeStruct((g_n, m_dim, n_dim), out_dtype)

    def body(a_ref, b_ref, *rest):
        o_ref = rest[-1]
        r = lax.dot_general(a_ref[...].astype(BF16), b_ref[...].astype(BF16), dims, preferred_element_type=F32)
        if bias is not None:
            r = r + rest[0][...]
        if reduce_g:
            g = pl.program_id(2)

            @pl.when(g == 0)
            def _():
                o_ref[...] = r

            @pl.when(g > 0)
            def _():
                o_ref[...] += r
        else:
            o_ref[...] = r.astype(o_ref.dtype)

    out = pl.pallas_call(body, grid=grid, in_specs=in_specs, out_specs=out_spec, out_shape=out_sds,
                         input_output_aliases=aliases, compiler_params=_cparams(), name=name)(*operands)
    if not reduce_g and out_shape is None and a.ndim == 2 and b.ndim == 2:
        out = out[0]
    return out


def _rows(tm, w):
    return pl.BlockSpec((tm, w), lambda i: (i, 0))


def _vec(w):
    return pl.BlockSpec((1, w), lambda i: (0, 0))


def _rstd(v):
    return lax.rsqrt(jnp.mean(v * v, axis=-1, keepdims=True) + EPS)


def _pre_fwd(x, pg, sc, sh, *, name):
    s_n, w = x.shape
    tm = _pick(s_n, 256, 8)

    def body(x_ref, pg_ref, sc_ref, sh_ref, o_ref):
        xv = x_ref[...]
        n = (xv * _rstd(xv)) * pg_ref[...]
        o_ref[...] = (n * (1.0 + sc_ref[...]) + sh_ref[...]).astype(o_ref.dtype)

    return pl.pallas_call(body, grid=(s_n // tm,), in_specs=[_rows(tm, w), _vec(w), _vec(w), _vec(w)],
                          out_specs=_rows(tm, w), out_shape=jax.ShapeDtypeStruct((s_n, w), BF16),
                          compiler_params=_cparams(), name=name)(x, pg, sc, sh)


def _post_fwd(f, x, qg, gate, res_w, *, name):
    s_n, w = x.shape
    tm = _pick(s_n, 256, 8)

    def body(f_ref, x_ref, qg_ref, gate_ref, o_ref):
        fv = f_ref[...]
        y = (fv * _rstd(fv)) * qg_ref[...]
        o_ref[...] = x_ref[...] + (res_w * gate_ref[...]) * y

    return pl.pallas_call(body, grid=(s_n // tm,), in_specs=[_rows(tm, w), _rows(tm, w), _vec(w), _vec(w)],
                          out_specs=_rows(tm, w), out_shape=jax.ShapeDtypeStruct((s_n, w), F32),
                          compiler_params=_cparams(), name=name)(f, x, qg, gate)


def _post_bwd(dout, f, qg, gate, res_w, *, name):
    s_n, w = f.shape
    tm = _pick(s_n, 256, 8)

    def body(do_ref, f_ref, qg_ref, gate_ref, df_ref, dgate_ref, dqg_ref):
        @pl.when(pl.program_id(0) == 0)
        def _():
            dgate_ref[...] = jnp.zeros_like(dgate_ref)
            dqg_ref[...] = jnp.zeros_like(dqg_ref)

        do = do_ref[...]
        fv = f_ref[...]
        r = _rstd(fv)
        fh = fv * r
        qg_v = qg_ref[...]
        dgate_ref[...] += res_w * jnp.sum(do * (fh * qg_v), axis=0, keepdims=True)
        dy = do * (res_w * gate_ref[...])
        dqg_ref[...] += jnp.sum(dy * fh, axis=0, keepdims=True)
        dfh = dy * qg_v
        df = r * (dfh - fh * jnp.mean(dfh * fh, axis=-1, keepdims=True))
        df_ref[...] = df.astype(df_ref.dtype)

    return pl.pallas_call(
        body, grid=(s_n // tm,), in_specs=[_rows(tm, w), _rows(tm, w), _vec(w), _vec(w)],
        out_specs=[_rows(tm, w), _vec(w), _vec(w)],
        out_shape=[jax.ShapeDtypeStruct((s_n, w), BF16), jax.ShapeDtypeStruct((1, w), F32),
                   jax.ShapeDtypeStruct((1, w), F32)],
        compiler_params=_cparams(), name=name)(dout, f, qg, gate)


def _pre_bwd(dhn, x, dout, pg, sc, *, name):
    s_n, w = x.shape
    tm = _pick(s_n, 256, 8)

    def body(dhn_ref, x_ref, do_ref, pg_ref, sc_ref, dx_ref, dsh_ref, dsc_ref, dpg_ref):
        @pl.when(pl.program_id(0) == 0)
        def _():
            dsh_ref[...] = jnp.zeros_like(dsh_ref)
            dsc_ref[...] = jnp.zeros_like(dsc_ref)
            dpg_ref[...] = jnp.zeros_like(dpg_ref)

        dhn_v = dhn_ref[...]
        xv = x_ref[...]
        r = _rstd(xv)
        xh = xv * r
        pg_v = pg_ref[...]
        dsh_ref[...] += jnp.sum(dhn_v, axis=0, keepdims=True)
        dsc_ref[...] += jnp.sum(dhn_v * (xh * pg_v), axis=0, keepdims=True)
        dn = dhn_v * (1.0 + sc_ref[...])
        dpg_ref[...] += jnp.sum(dn * xh, axis=0, keepdims=True)
        dxh = dn * pg_v
        dx_ref[...] = do_ref[...] + r * (dxh - xh * jnp.mean(dxh * xh, axis=-1, keepdims=True))

    vec = jax.ShapeDtypeStruct((1, w), F32)
    return pl.pallas_call(
        body, grid=(s_n // tm,), in_specs=[_rows(tm, w), _rows(tm, w), _rows(tm, w), _vec(w), _vec(w)],
        out_specs=[_rows(tm, w), _vec(w), _vec(w), _vec(w)],
        out_shape=[jax.ShapeDtypeStruct((s_n, w), F32), vec, vec, vec],
        compiler_params=_cparams(), name=name)(dhn, x, dout, pg, sc)


def _rms_fwd(x, g, *, name):
    s_n, w = x.shape
    tm = _pick(s_n, 512, 8)

    def body(x_ref, g_ref, o_ref):
        xv = x_ref[...]
        o_ref[...] = ((xv * _rstd(xv)) * g_ref[...]).astype(o_ref.dtype)

    return pl.pallas_call(body, grid=(s_n // tm,), in_specs=[_rows(tm, w), _vec(w)], out_specs=_rows(tm, w),
                          out_shape=jax.ShapeDtypeStruct((s_n, w), BF16), compiler_params=_cparams(),
                          name=name)(x, g)


def _rms_bwd(dy, x, g, *, name):
    s_n, w = x.shape
    tm = _pick(s_n, 512, 8)

    def body(dy_ref, x_ref, g_ref, dx_ref, dg_ref):
        @pl.when(pl.program_id(0) == 0)
        def _():
            dg_ref[...] = jnp.zeros_like(dg_ref)

        dy_v = dy_ref[...]
        xv = x_ref[...]
        r = _rstd(xv)
        xh = xv * r
        dg_ref[...] += jnp.sum(dy_v * xh, axis=0, keepdims=True)
        dxh = dy_v * g_ref[...]
        dx_ref[...] = r * (dxh - xh * jnp.mean(dxh * xh, axis=-1, keepdims=True))

    return pl.pallas_call(
        body, grid=(s_n // tm,), in_specs=[_rows(tm, w), _rows(tm, w), _vec(w)],
        out_specs=[_rows(tm, w), _vec(w)],
        out_shape=[jax.ShapeDtypeStruct((s_n, w), F32), jax.ShapeDtypeStruct((1, w), F32)],
        compiler_params=_cparams(), name=name)(dy, x, g)


def _rope(a1, a2, cos, sin, *, name):
    s_n, w = a1.shape
    tm = _pick(s_n, 512, 8)

    def body(a1_ref, a2_ref, c_ref, s_ref, r1_ref, r2_ref):
        u, v, c_v, s_v = a1_ref[...], a2_ref[...], c_ref[...], s_ref[...]
        r1_ref[...] = u * c_v - v * s_v
        r2_ref[...] = u * s_v + v * c_v

    sd = jax.ShapeDtypeStruct((s_n, w), F32)
    return pl.pallas_call(body, grid=(s_n // tm,), in_specs=[_rows(tm, w)] * 4, out_specs=[_rows(tm, w)] * 2,
                          out_shape=[sd, sd], compiler_params=_cparams(), name=name)(a1, a2, cos, sin)


def _silu_bf16(x, *, name):
    def body(x_ref, o_ref):
        xv = x_ref[...]
        o_ref[...] = (xv * jax.nn.sigmoid(xv)).astype(o_ref.dtype)

    return pl.pallas_call(body, out_shape=jax.ShapeDtypeStruct(x.shape, BF16), name=name)(x)


def _loss(y, target, *, name):
    s_n, w = y.shape
    tm = _pick(s_n, 256, 8)

    def body(y_ref, t_ref, dy_ref, l_ref):
        @pl.when(pl.program_id(0) == 0)
        def _():
            l_ref[...] = jnp.zeros_like(l_ref)

        e = y_ref[...] - t_ref[...]
        dy_ref[...] = e * (1.0 / w)
        row = jnp.mean(e * e, axis=-1, keepdims=True)
        l_ref[...] += 0.5 * jnp.sum(row, axis=0, keepdims=True)

    return pl.pallas_call(
        body, grid=(s_n // tm,), in_specs=[_rows(tm, w), _rows(tm, w)],
        out_specs=[_rows(tm, w), pl.BlockSpec((1, 1), lambda i: (0, 0))],
        out_shape=[jax.ShapeDtypeStruct((s_n, w), F32), jax.ShapeDtypeStruct((1, 1), F32)],
        compiler_params=_cparams(), name=name)(y, target)


FFN_TM = 512


def _ffn_up(hn, w_gu, *, name):
    s_n, d = hn.shape
    f = w_gu.shape[-1]
    tm = _pick(s_n, FFN_TM, 8)

    def body(hn_ref, wg_ref, wu_ref, gu_ref, a_ref):
        xv = hn_ref[...]
        g = jnp.dot(xv, wg_ref[...], preferred_element_type=F32)
        u = jnp.dot(xv, wu_ref[...], preferred_element_type=F32)
        gu_ref[0] = g.astype(BF16)
        gu_ref[1] = u.astype(BF16)
        a_ref[...] = ((g * jax.nn.sigmoid(g)) * u).astype(BF16)

    w_blk = lambda t: pl.BlockSpec((None, None, d, f), lambda s, m: (s, t, 0, 0))
    return pl.pallas_call(
        body, grid=(N_SHARD, s_n // tm),
        in_specs=[pl.BlockSpec((tm, d), lambda s, m: (m, 0)), w_blk(0), w_blk(1)],
        out_specs=[pl.BlockSpec((None, 2, tm, f), lambda s, m: (s, 0, m, 0)),
                   pl.BlockSpec((None, tm, f), lambda s, m: (s, m, 0))],
        out_shape=[jax.ShapeDtypeStruct((N_SHARD, 2, s_n, f), BF16), jax.ShapeDtypeStruct((N_SHARD, s_n, f), BF16)],
        compiler_params=_cparams(), name=name)(hn, w_gu, w_gu)


def _ffn_dgu(df, w_dn, gu, *, name):
    s_n, d = df.shape
    f = w_dn.shape[-2]
    tm = _pick(s_n, FFN_TM, 8)

    def body(df_ref, wd_ref, gu_ref, o_ref):
        da = lax.dot_general(df_ref[...], wd_ref[...], (((1,), (1,)), ((), ())), preferred_element_type=F32)
        g = gu_ref[0].astype(F32)
        u = gu_ref[1].astype(F32)
        sig = jax.nn.sigmoid(g)
        o_ref[0] = (da * u * (sig * (1.0 + g * (1.0 - sig)))).astype(BF16)
        o_ref[1] = (da * (g * sig)).astype(BF16)

    gu_blk = pl.BlockSpec((None, 2, tm, f), lambda s, m: (s, 0, m, 0))
    return pl.pallas_call(
        body, grid=(N_SHARD, s_n // tm),
        in_specs=[pl.BlockSpec((tm, d), lambda s, m: (m, 0)),
                  pl.BlockSpec((None, f, d), lambda s, m: (s, 0, 0)), gu_blk],
        out_specs=gu_blk, out_shape=jax.ShapeDtypeStruct((N_SHARD, 2, s_n, f), BF16),
        compiler_params=_cparams(), name=name)(df, w_dn, gu)


_NT = (((1,), (1,)), ((), ()))
_TN = (((0,), (0,)), ((), ()))
MLA_TQ = 256


def _causal_mask(i, tq, s_n):
    qpos = i * tq + lax.broadcasted_iota(jnp.int32, (tq, s_n), 0)
    kpos = lax.broadcasted_iota(jnp.int32, (tq, s_n), 1)
    return kpos <= qpos


def _mla_attn_fwd(q, k, v, *, name):
    h_n, s_n, dq = q.shape
    dv = v.shape[-1]
    tq = MLA_TQ
    scale = float(dq) ** -0.5

    def body(q_ref, k_ref, v_ref, o_ref, lse_ref):
        i = pl.program_id(1)
        for e in range(1, s_n // tq + 1):
            @pl.when(i == e - 1)
            def _(ext=e * tq):
                mask = _causal_mask(i, tq, ext)
                s = lax.dot_general(q_ref[...], k_ref[0:ext, :], _NT, preferred_element_type=F32) * scale
                s = jnp.where(mask, s, -jnp.inf)
                m = jnp.max(s, axis=-1, keepdims=True)
                p = jnp.exp(s - m)
                l = jnp.sum(p, axis=-1, keepdims=True)
                o = jnp.dot(p.astype(BF16), v_ref[0:ext, :], preferred_element_type=F32)
                o_ref[...] = o / l
                lse_ref[...] = m + jnp.log(l)

    return pl.pallas_call(
        body, grid=(h_n, s_n // tq),
        in_specs=[pl.BlockSpec((None, tq, dq), lambda h, i: (h, i, 0)),
                  pl.BlockSpec((None, s_n, dq), lambda h, i: (h, 0, 0)),
                  pl.BlockSpec((None, s_n, dv), lambda h, i: (h, 0, 0))],
        out_specs=[pl.BlockSpec((None, tq, dv), lambda h, i: (h, i, 0)),
                   pl.BlockSpec((None, tq, 1), lambda h, i: (h, i, 0))],
        out_shape=[jax.ShapeDtypeStruct((h_n, s_n, dv), F32), jax.ShapeDtypeStruct((h_n, s_n, 1), F32)],
        compiler_params=_cparams(), name=name)(q, k, v)


def _mla_attn_bwd(q, k, v, o, do, lse, *, name):
    h_n, s_n, dq = q.shape
    dv = v.shape[-1]
    tq = MLA_TQ
    scale = float(dq) ** -0.5

    def body(q_ref, k_ref, v_ref, o_ref, do_ref, lse_ref, dq_ref, dk_ref, dv_ref):
        i = pl.program_id(1)

        @pl.when(i == 0)
        def _():
            dk_ref[...] = jnp.zeros_like(dk_ref)
            dv_ref[...] = jnp.zeros_like(dv_ref)

        for e in range(1, s_n // tq + 1):
            @pl.when(i == e - 1)
            def _(ext=e * tq):
                mask = _causal_mask(i, tq, ext)
                qv, kv, vv = q_ref[...], k_ref[0:ext, :], v_ref[0:ext, :]
                do_v = do_ref[...]
                s = lax.dot_general(qv, kv, _NT, preferred_element_type=F32) * scale
                p = jnp.where(mask, jnp.exp(s - lse_ref[...]), 0.0)
                dob = do_v.astype(BF16)
                dv_ref[0:ext, :] += lax.dot_general(p.astype(BF16), dob, _TN, preferred_element_type=F32)
                dp = lax.dot_general(dob, vv, _NT, preferred_element_type=F32)
                delta = jnp.sum(do_v * o_ref[...], axis=-1, keepdims=True)
                dsb = (p * (dp - delta) * scale).astype(BF16)
                dq_ref[...] = jnp.dot(dsb, kv, preferred_element_type=F32)
                dk_ref[0:ext, :] += lax.dot_general(dsb, qv, _TN, preferred_element_type=F32)

    return pl.pallas_call(
        body, grid=(h_n, s_n // tq),
        in_specs=[pl.BlockSpec((None, tq, dq), lambda h, i: (h, i, 0)),
                  pl.BlockSpec((None, s_n, dq), lambda h, i: (h, 0, 0)),
                  pl.BlockSpec((None, s_n, dv), lambda h, i: (h, 0, 0)),
                  pl.BlockSpec((None, tq, dv), lambda h, i: (h, i, 0)),
                  pl.BlockSpec((None, tq, dv), lambda h, i: (h, i, 0)),
                  pl.BlockSpec((None, tq, 1), lambda h, i: (h, i, 0))],
        out_specs=[pl.BlockSpec((None, tq, dq), lambda h, i: (h, i, 0)),
                   pl.BlockSpec((None, s_n, dq), lambda h, i: (h, 0, 0)),
                   pl.BlockSpec((None, s_n, dv), lambda h, i: (h, 0, 0))],
        out_shape=[jax.ShapeDtypeStruct((h_n, s_n, dq), F32), jax.ShapeDtypeStruct((h_n, s_n, dq), F32),
                   jax.ShapeDtypeStruct((h_n, s_n, dv), F32)],
        compiler_params=_cparams(), name=name)(q, k, v, o, do, lse)


def _head_sum(x, *, name):
    h_n, s_n, w = x.shape
    tm = _pick(s_n, 256, 8)

    def body(x_ref, o_ref):
        o_ref[...] = jnp.sum(x_ref[...], axis=0)

    return pl.pallas_call(body, grid=(s_n // tm,), in_specs=[pl.BlockSpec((h_n, tm, w), lambda i: (0, i, 0))],
                          out_specs=_rows(tm, w), out_shape=jax.ShapeDtypeStruct((s_n, w), F32),
                          compiler_params=_cparams(), name=name)(x)


N_BLK = SEQ // DIL_BLOCK
DIL_SCALE = 64 ** -0.5


def _dil_masks():
    iq = lax.broadcasted_iota(jnp.int32, (DIL_BLOCK, 2 * DIL_BLOCK), 0)
    ik = lax.broadcasted_iota(jnp.int32, (DIL_BLOCK, 2 * DIL_BLOCK), 1)
    rel = DIL_BLOCK + iq - ik
    both = (rel >= 0) & (rel <= DIL_BLOCK)
    iq1 = lax.broadcasted_iota(jnp.int32, (DIL_BLOCK, DIL_BLOCK), 0)
    ik1 = lax.broadcasted_iota(jnp.int32, (DIL_BLOCK, DIL_BLOCK), 1)
    return both, ik1 <= iq1


def _dil_block(j, nb):
    lo = j * DIL_BLOCK
    first = j % nb == 0
    k_lo = lo if first else lo - DIL_BLOCK
    b_lo = DIL_BLOCK if first else 0
    return lo, k_lo, b_lo, first


def _dil_attn_fwd(q, k, v, bias, nb, *, name):
    h_n, s_n, e = q.shape

    def body(q_ref, k_ref, v_ref, b_ref, o_ref, lse_ref):
        m_both, m_first = _dil_masks()
        for j in range(N_BLK):
            lo, k_lo, b_lo, first = _dil_block(j, nb)
            qj = q_ref[lo:lo + DIL_BLOCK, :]
            kk = k_ref[k_lo:lo + DIL_BLOCK, :]
            vv = v_ref[k_lo:lo + DIL_BLOCK, :]
            s = lax.dot_general(qj, kk, _NT, preferred_element_type=F32) * DIL_SCALE + b_ref[:, b_lo:]
            s = jnp.where(m_first if first else m_both, s, -jnp.inf)
            m = jnp.max(s, axis=-1, keepdims=True)
            lse = m + jnp.log(jnp.sum(jnp.exp(s - m), axis=-1, keepdims=True))
            p = jnp.exp(s - lse)
            o_ref[lo:lo + DIL_BLOCK, :] = jnp.dot(p.astype(BF16), vv, preferred_element_type=F32)
            lse_ref[lo:lo + DIL_BLOCK, :] = lse

    head = lambda w: pl.BlockSpec((None, s_n, w), lambda h: (h, 0, 0))
    return pl.pallas_call(
        body, grid=(h_n,),
        in_specs=[head(e), head(e), head(e), pl.BlockSpec((None, DIL_BLOCK, 2 * DIL_BLOCK), lambda h: (h, 0, 0))],
        out_specs=[head(e), head(1)],
        out_shape=[jax.ShapeDtypeStruct((h_n, s_n, e), F32), jax.ShapeDtypeStruct((h_n, s_n, 1), F32)],
        compiler_params=_cparams(), name=name)(q, k, v, bias)


def _dil_attn_bwd(q, k, v, bias, lse, do, dlt, nb, *, name):
    h_n, s_n, e = q.shape

    def body(q_ref, k_ref, v_ref, b_ref, lse_ref, do_ref, dlt_ref, dq_ref, dk_ref, dv_ref, db_ref):
        dk_ref[...] = jnp.zeros_like(dk_ref)
        dv_ref[...] = jnp.zeros_like(dv_ref)
        db_ref[...] = jnp.zeros_like(db_ref)
        m_both, m_first = _dil_masks()
        for j in range(N_BLK):
            lo, k_lo, b_lo, first = _dil_block(j, nb)
            qj = q_ref[lo:lo + DIL_BLOCK, :]
            kk = k_ref[k_lo:lo + DIL_BLOCK, :]
            vv = v_ref[k_lo:lo + DIL_BLOCK, :]
            s = lax.dot_general(qj, kk, _NT, preferred_element_type=F32) * DIL_SCALE + b_ref[:, b_lo:]
            p = jnp.where(m_first if first else m_both, jnp.exp(s - lse_ref[lo:lo + DIL_BLOCK, :]), 0.0)
            dob = do_ref[lo:lo + DIL_BLOCK, :].astype(BF16)
            dv_ref[k_lo:lo + DIL_BLOCK, :] += lax.dot_general(p.astype(BF16), dob, _TN, preferred_element_type=F32)
            dp = lax.dot_general(dob, vv, _NT, preferred_element_type=F32)
            ds = p * (dp - dlt_ref[lo:lo + DIL_BLOCK, :])
            db_ref[:, b_lo:] += ds
            dsb = (ds * DIL_SCALE).astype(BF16)
            dq_ref[lo:lo + DIL_BLOCK, :] = jnp.dot(dsb, kk, preferred_element_type=F32)
            dk_ref[k_lo:lo + DIL_BLOCK, :] += lax.dot_general(dsb, qj, _TN, preferred_element_type=F32)

    head = lambda w: pl.BlockSpec((None, s_n, w), lambda h: (h, 0, 0))
    b_spec = pl.BlockSpec((None, DIL_BLOCK, 2 * DIL_BLOCK), lambda h: (h, 0, 0))
    sd = jax.ShapeDtypeStruct((h_n, s_n, e), F32)
    return pl.pallas_call(
        body, grid=(h_n,),
        in_specs=[head(e), head(e), head(e), b_spec, head(1), head(e), head(1)],
        out_specs=[head(e), head(e), head(e), b_spec],
        out_shape=[sd, sd, sd, jax.ShapeDtypeStruct((h_n, DIL_BLOCK, 2 * DIL_BLOCK), F32)],
        compiler_params=_cparams(), name=name)(q, k, v, bias, lse, do, dlt)


def _group_alpha(l_refs):
    ls = [r[...] for r in l_refs]
    m = jnp.maximum(jnp.maximum(ls[0], ls[1]), ls[2])
    es = [jnp.exp(l - m) for l in ls]
    tot = es[0] + es[1] + es[2]
    return [ex / tot for ex in es]


def _dil_mix_fwd(os_, ls_, *, name):
    h_n, s_n, e = os_[0].shape
    tm = 512

    def body(o0, o1, o2, l0, l1, l2, out_ref):
        al = _group_alpha((l0, l1, l2))
        out_ref[...] = al[0] * o0[...] + al[1] * o1[...] + al[2] * o2[...]

    blk = lambda w: pl.BlockSpec((None, tm, w), lambda h, i: (h, i, 0))
    return pl.pallas_call(body, grid=(h_n, s_n // tm), in_specs=[blk(e)] * 3 + [blk(1)] * 3, out_specs=blk(e),
                          out_shape=jax.ShapeDtypeStruct((h_n, s_n, e), F32), compiler_params=_cparams(),
                          name=name)(*os_, *ls_)


def _dil_mix_bwd(do, os_, ls_, *, name):
    h_n, s_n, e = do.shape
    tm = 512

    def body(do_ref, o0, o1, o2, l0, l1, l2, d0, d1, d2, t0, t1, t2):
        al = _group_alpha((l0, l1, l2))
        do_v = do_ref[...]
        mix = al[0] * o0[...] + al[1] * o1[...] + al[2] * o2[...]
        dbar = jnp.sum(do_v * mix, axis=-1, keepdims=True)
        for a_g, d_ref, t_ref in zip(al, (d0, d1, d2), (t0, t1, t2)):
            d_ref[...] = a_g * do_v
            t_ref[...] = a_g * dbar

    blk = lambda w: pl.BlockSpec((None, tm, w), lambda h, i: (h, i, 0))
    sd_e = jax.ShapeDtypeStruct((h_n, s_n, e), F32)
    sd_1 = jax.ShapeDtypeStruct((h_n, s_n, 1), F32)
    outs = pl.pallas_call(body, grid=(h_n, s_n // tm), in_specs=[blk(e)] * 4 + [blk(1)] * 3,
                          out_specs=[blk(e)] * 3 + [blk(1)] * 3, out_shape=[sd_e] * 3 + [sd_1] * 3,
                          compiler_params=_cparams(), name=name)(do, *os_, *ls_)
    return outs[:3], outs[3:]


def _bias_grad(ds, bucket, *, name):
    h_n = ds.shape[0]

    def body(ds_ref, bk_ref, o_ref):
        ds_v = ds_ref[...]
        bk = bk_ref[...]
        lane = lax.broadcasted_iota(jnp.int32, (1, N_BUCKETS), 1)
        acc = jnp.zeros((1, N_BUCKETS), F32)
        for b in range(N_BUCKETS):
            tot = jnp.sum(jnp.sum(jnp.where(bk == b, ds_v, 0.0), axis=1, keepdims=True), axis=0, keepdims=True)
            acc = acc + jnp.where(lane == b, tot, 0.0)
        o_ref[...] = acc

    return pl.pallas_call(
        body, grid=(h_n,),
        in_specs=[pl.BlockSpec((None, DIL_BLOCK, 2 * DIL_BLOCK), lambda h: (h, 0, 0)),
                  pl.BlockSpec((DIL_BLOCK, 2 * DIL_BLOCK), lambda h: (0, 0))],
        out_specs=pl.BlockSpec((None, 1, N_BUCKETS), lambda h: (h, 0, 0)),
        out_shape=jax.ShapeDtypeStruct((h_n, 1, N_BUCKETS), F32), compiler_params=_cparams(), name=name)(ds, bucket)


def _bias_table(rb, bucket, *, name):
    h_n = rb.shape[0]

    def body(rb_ref, bk_ref, o_ref):
        bk = bk_ref[...]
        row = rb_ref[...]
        acc = jnp.zeros(bk.shape, F32)
        for b in range(N_BUCKETS):
            acc = jnp.where(bk == b, row[:, b:b + 1], acc)
        o_ref[...] = acc

    return pl.pallas_call(
        body, grid=(h_n,),
        in_specs=[pl.BlockSpec((None, 1, N_BUCKETS), lambda h: (h, 0, 0)),
                  pl.BlockSpec((DIL_BLOCK, 2 * DIL_BLOCK), lambda h: (0, 0))],
        out_specs=pl.BlockSpec((None, DIL_BLOCK, 2 * DIL_BLOCK), lambda h: (h, 0, 0)),
        out_shape=jax.ShapeDtypeStruct((h_n, DIL_BLOCK, 2 * DIL_BLOCK), F32), compiler_params=_cparams(),
        name=name)(rb, bucket)


def _row_tile(rows, cols, budget=1 << 20):
    if rows * cols * 4 <= budget or rows % 8:
        return rows
    best = 8
    for t in range(8, rows + 1, 8):
        if rows % t == 0 and t * cols * 4 <= budget:
            best = t
    return best


def _adamw(w, g, m, v, *, name):
    shape = w.shape
    cols = shape[-1]
    rows = math.prod(shape[:-1]) if len(shape) > 1 else 1
    to2 = lambda t: t.reshape(rows, cols)
    tr = _row_tile(rows, cols)
    c1 = 1.0 / (1.0 - ADAM_B1 ** ADAM_STEP)
    c2 = 1.0 / (1.0 - ADAM_B2 ** ADAM_STEP)

    def body(w_ref, g_ref, m_ref, v_ref, d_ref, nm_ref, nv_ref):
        g_v = g_ref[...]
        nm = ADAM_B1 * m_ref[...] + (1.0 - ADAM_B1) * g_v
        nv = ADAM_B2 * v_ref[...] + (1.0 - ADAM_B2) * (g_v * g_v)
        m_hat = nm * c1
        v_hat = nv * c2
        d_ref[...] = -ADAM_LR * (m_hat / (jnp.sqrt(v_hat) + ADAM_EPS) + ADAM_WD * w_ref[...])
        nm_ref[...] = nm
        nv_ref[...] = nv

    blk = pl.BlockSpec((tr, cols), lambda i: (i, 0))
    sd = jax.ShapeDtypeStruct((rows, cols), F32)
    outs = pl.pallas_call(body, grid=(rows // tr,), in_specs=[blk] * 4, out_specs=[blk] * 3, out_shape=[sd] * 3,
                          compiler_params=_cparams(), name=name)(to2(w), to2(g), to2(m), to2(v))
    return tuple(t.reshape(shape) for t in outs)


def _add_half(unit, got, half_idx, *, name):
    rest = unit.shape[2:]
    c = rest[-1]
    r = math.prod(rest[:-1])
    tr = _row_tile(r, c)

    def body(idx_ref, u_ref, g_ref, o_ref, w_ref):
        tot = u_ref[...] + g_ref[...].astype(F32)
        o_ref[...] = tot
        w_ref[...] = tot.astype(BF16)

    blk = pl.BlockSpec((None, tr, c), lambda s, i, idx: (s, i, 0))
    grid_spec = pltpu.PrefetchScalarGridSpec(
        num_scalar_prefetch=1, grid=(N_SHARD, r // tr),
        in_specs=[pl.BlockSpec((None, None, tr, c), lambda s, i, idx: (idx[0], s, i, 0)), blk],
        out_specs=[blk, blk])
    out, wire = pl.pallas_call(
        body, grid_spec=grid_spec,
        out_shape=[jax.ShapeDtypeStruct((N_SHARD, r, c), F32), jax.ShapeDtypeStruct((N_SHARD, r, c), BF16)],
        compiler_params=_cparams(), name=name)(half_idx, unit.reshape(2, N_SHARD, r, c), got.reshape(N_SHARD, r, c))
    return out.reshape((N_SHARD,) + rest), wire.reshape((N_SHARD,) + rest)


def _add_shards(part, got, shard_idx, *, name):
    rest = part.shape[1:]
    c = rest[-1]
    r = math.prod(rest[:-1])
    tr = _row_tile(r, c)

    def body(idx_ref, p_ref, g_ref, o_ref):
        acc = p_ref[...]
        for k in range(3):
            acc = acc + g_ref[k].astype(F32)
        o_ref[...] = acc

    grid_spec = pltpu.PrefetchScalarGridSpec(
        num_scalar_prefetch=1, grid=(r // tr,),
        in_specs=[pl.BlockSpec((None, tr, c), lambda i, idx: (idx[0], i, 0)),
                  pl.BlockSpec((3, tr, c), lambda i, idx: (0, i, 0))],
        out_specs=pl.BlockSpec((tr, c), lambda i, idx: (i, 0)))
    out = pl.pallas_call(body, grid_spec=grid_spec, out_shape=jax.ShapeDtypeStruct((r, c), F32),
                         compiler_params=_cparams(), name=name)(
        shard_idx, part.reshape(N_SHARD, r, c), got.reshape(3, r, c))
    return out.reshape(rest)


def _sum_devices(x, n_dev, *, name):
    rows = x.shape[0] // n_dev

    def body(x_ref, o_ref):
        acc = x_ref[0:rows, :]
        for d in range(1, n_dev):
            acc = acc + x_ref[d * rows:(d + 1) * rows, :]
        o_ref[...] = acc

    return pl.pallas_call(body, out_shape=jax.ShapeDtypeStruct((rows, x.shape[1]), F32), name=name)(x)


def _my_pos():
    return lax.axis_index("x"), lax.axis_index("y"), lax.axis_index("c")


def _all_gather(x_blk, *, name, in_vmem):
    m_per, n = x_blk.shape

    def body(x_ref, out_ref, send_sems, recv_sems, local_sem):
        x, y, c = _my_pos()
        me, sibling = (x, y, c), (x, y, 1 - c)
        chips = [(1 - x, y), (x, 1 - y), (1 - x, 1 - y)]

        def rows(px, py, pc):
            return out_ref.at[pl.ds((4 * px + 2 * py + pc) * m_per, m_per), :]

        def copy(k, block, to, src=None):
            return pltpu.make_async_remote_copy(
                src_ref=rows(*block) if src is None else src, dst_ref=rows(*block),
                send_sem=send_sems.at[k], recv_sem=recv_sems.at[k], device_id=to, device_id_type=MESH)

        mine = pltpu.make_async_copy(x_ref, rows(*me), local_sem)
        mine.start()
        first = [copy(0, me, sibling, src=x_ref)]
        first += [copy(1 + j, me, (*chip, c), src=x_ref) for j, chip in enumerate(chips)]
        for cp in first:
            cp.start()
        passed = [copy(4 + j, (*chip, c), sibling) for j, chip in enumerate(chips)]
        for j, chip in enumerate(chips):
            copy(1 + j, (*chip, c), me).wait_recv()
            passed[j].start()
        copy(0, sibling, me).wait_recv()
        for j, chip in enumerate(chips):
            copy(4 + j, (*chip, 1 - c), me).wait_recv()
        for cp in first + passed:
            cp.wait_send()
        mine.wait()

    space = pltpu.VMEM if in_vmem else pl.ANY
    return pl.pallas_call(
        body, out_shape=jax.ShapeDtypeStruct((8 * m_per, n), x_blk.dtype),
        in_specs=[pl.BlockSpec(memory_space=space)], out_specs=pl.BlockSpec(memory_space=space),
        scratch_shapes=[pltpu.SemaphoreType.DMA((7,)), pltpu.SemaphoreType.DMA((7,)), pltpu.SemaphoreType.DMA],
        name=name)(x_blk)


_HBM = pl.BlockSpec(memory_space=pl.ANY)


def _gather_weights(fams, *, name):
    n = len(fams)

    def body(*refs):
        ins, outs = refs[:n], refs[n:2 * n]
        send_sems, recv_sems = refs[2 * n:]
        x, y, c = _my_pos()
        me, sibling = (x, y, c), (x, y, 1 - c)
        chips = [(1 - x, y), (x, 1 - y), (1 - x, 1 - y)]

        def copy(f, k, block, to, src=None):
            px, py, pc = block
            dst = outs[f].at[2 * px + py, pc]
            return pltpu.make_async_remote_copy(
                src_ref=dst if src is None else src, dst_ref=dst, send_sem=send_sems.at[7 * f + k],
                recv_sem=recv_sems.at[7 * f + k], device_id=to, device_id_type=MESH)

        first, passed = [], []
        for f in range(n):
            src = ins[f].at[c]
            first.append(copy(f, 0, me, sibling, src=src))
            first += [copy(f, 1 + j, me, (*chip, c), src=src) for j, chip in enumerate(chips)]
        for cp in first:
            cp.start()
        for j, chip in enumerate(chips):
            for f in range(n):
                copy(f, 1 + j, (*chip, c), me).wait_recv()
                passed.append(copy(f, 4 + j, (*chip, c), sibling))
                passed[-1].start()
        for f in range(n):
            copy(f, 0, sibling, me).wait_recv()
        for j, chip in enumerate(chips):
            for f in range(n):
                copy(f, 4 + j, (*chip, 1 - c), me).wait_recv()
        for cp in first + passed:
            cp.wait_send()

    outs = pl.pallas_call(
        body, out_shape=[jax.ShapeDtypeStruct((N_SHARD,) + t.shape, t.dtype) for t in fams],
        in_specs=[_HBM] * n, out_specs=[_HBM] * n,
        scratch_shapes=[pltpu.SemaphoreType.DMA((7 * n,)), pltpu.SemaphoreType.DMA((7 * n,))], name=name)(*fams)
    return [_place_own(o, t) for o, t in zip(outs, fams)]


def _swap_halves(units, *, name):
    n = len(units)

    def body(*refs):
        ins, outs = refs[:n], refs[n:2 * n]
        send_sems, recv_sems = refs[2 * n:]
        x, y, c = _my_pos()
        cps = [pltpu.make_async_remote_copy(src_ref=ins[f].at[1 - c], dst_ref=outs[f], send_sem=send_sems.at[f],
                                            recv_sem=recv_sems.at[f], device_id=(x, y, 1 - c), device_id_type=MESH)
               for f in range(n)]
        for cp in cps:
            cp.start()
        for cp in cps:
            cp.wait()

    return pl.pallas_call(
        body, out_shape=[jax.ShapeDtypeStruct(t.shape[1:], t.dtype) for t in units],
        in_specs=[_HBM] * n, out_specs=[_HBM] * n,
        scratch_shapes=[pltpu.SemaphoreType.DMA((n,)), pltpu.SemaphoreType.DMA((n,))], name=name)(*units)


def _send_to_chips(parts, *, name):
    n = len(parts)

    def body(*refs):
        ins, outs = refs[:n], refs[n:2 * n]
        send_sems, recv_sems = refs[2 * n:]
        x, y, c = _my_pos()
        chips = [(1 - x, y), (x, 1 - y), (1 - x, 1 - y)]
        cps = [pltpu.make_async_remote_copy(src_ref=ins[f].at[2 * cx + cy], dst_ref=outs[f].at[k],
                                            send_sem=send_sems.at[3 * f + k], recv_sem=recv_sems.at[3 * f + k],
                                            device_id=(cx, cy, c), device_id_type=MESH)
               for f in range(n) for k, (cx, cy) in enumerate(chips)]
        for cp in cps:
            cp.start()
        for cp in cps:
            cp.wait()

    return pl.pallas_call(
        body, out_shape=[jax.ShapeDtypeStruct((3,) + t.shape[1:], t.dtype) for t in parts],
        in_specs=[_HBM] * n, out_specs=[_HBM] * n,
        scratch_shapes=[pltpu.SemaphoreType.DMA((3 * n,)), pltpu.SemaphoreType.DMA((3 * n,))], name=name)(*parts)


def _pair_gather(halves, *, name):
    n = len(halves)

    def body(*refs):
        ins, outs = refs[:n], refs[n:2 * n]
        send_sems, recv_sems = refs[2 * n:]
        x, y, c = _my_pos()
        cps = [pltpu.make_async_remote_copy(src_ref=ins[f], dst_ref=outs[f].at[c], send_sem=send_sems.at[f],
                                            recv_sem=recv_sems.at[f], device_id=(x, y, 1 - c), device_id_type=MESH)
               for f in range(n)]
        for cp in cps:
            cp.start()
        for f in range(n):
            pltpu.make_async_remote_copy(src_ref=ins[f], dst_ref=outs[f].at[1 - c], send_sem=send_sems.at[f],
                                         recv_sem=recv_sems.at[f], device_id=(x, y, 1 - c),
                                         device_id_type=MESH).wait_recv()
        for cp in cps:
            cp.wait_send()

    outs = pl.pallas_call(
        body, out_shape=[jax.ShapeDtypeStruct((2,) + t.shape, t.dtype) for t in halves],
        in_specs=[_HBM] * n, out_specs=[_HBM] * n,
        scratch_shapes=[pltpu.SemaphoreType.DMA((n,)), pltpu.SemaphoreType.DMA((n,))], name=name)(*halves)
    c = lax.axis_index("c")
    return [lax.dynamic_update_index_in_dim(o, t, c, 0) for o, t in zip(outs, halves)]


_HBM_ONLY = pl.BlockSpec(memory_space=pltpu.HBM)
_SEMS = pl.BlockSpec(memory_space=pltpu.SEMAPHORE)
_EFFECT = pltpu.SideEffectType.DATAFLOW_SIDE_EFFECTING


def _copies_start(srcs, lands, plan, n_copies, *, name):
    n, m = len(srcs), len(lands)

    def body(*refs):
        src_refs, land_refs = refs[:n], refs[n:n + m]
        send_sems, recv_sems, token = refs[n + m], refs[n + m + 1], refs[-1]
        for k, (src, dst, peer) in enumerate(plan(src_refs, land_refs)):
            pltpu.make_async_remote_copy(src_ref=src, dst_ref=dst, send_sem=send_sems.at[k], recv_sem=recv_sems.at[k],
                                         device_id=peer, device_id_type=MESH).start()
        token[...] = jnp.zeros_like(token)

    bufs = [pltpu.with_memory_space_constraint(t, pltpu.HBM) for t in (*srcs, *lands)]
    outs = pl.pallas_call(
        body, name=name,
        out_shape=(pltpu.SemaphoreType.DMA((n_copies,)), pltpu.SemaphoreType.DMA((n_copies,)),
                   *[pltpu.HBM(t.shape, t.dtype) for t in bufs], jax.ShapeDtypeStruct((8, 128), F32)),
        in_specs=[_HBM_ONLY] * (n + m),
        out_specs=(_SEMS, _SEMS, *[_HBM_ONLY] * (n + m), pl.BlockSpec(memory_space=pltpu.VMEM)),
        input_output_aliases={k: 2 + k for k in range(n + m)},
        compiler_params=pltpu.CompilerParams(has_side_effects=_EFFECT))(*bufs)
    return outs[0], outs[1], list(outs[2:2 + n + m]), outs[-1]


def _copies_wait(send_sems, recv_sems, thru, n_src, plan, after, *, name):
    nm = len(thru)

    def body(*refs):
        t_refs, send, recv = refs[:nm], refs[nm], refs[nm + 1]
        for k, (src, dst, peer) in enumerate(plan(t_refs[:n_src], t_refs[n_src:])):
            cp = pltpu.make_async_remote_copy(src_ref=src, dst_ref=dst, send_sem=send.at[k], recv_sem=recv.at[k],
                                              device_id=peer, device_id_type=MESH)
            cp.wait_send()
            cp.wait_recv()

    outs = pl.pallas_call(
        body, name=name, out_shape=tuple(pltpu.HBM(t.shape, t.dtype) for t in thru),
        in_specs=[_HBM_ONLY] * nm + [_SEMS, _SEMS, pl.BlockSpec(memory_space=pl.ANY)],
        out_specs=tuple([_HBM_ONLY] * nm), input_output_aliases={k: k for k in range(nm)},
        compiler_params=pltpu.CompilerParams(has_side_effects=_EFFECT))(*thru, send_sems, recv_sems, after)
    return list(outs)


_RELATIONS = [(dx, dy, dc) for dx in (0, 1) for dy in (0, 1) for dc in (0, 1)][1:]


def _gather_plan(src_refs, land_refs):
    x, y, c = _my_pos()
    flip = lambda v, d: 1 - v if d else v
    return [(s_ref.at[c], l_ref.at[2 * x + y, c], (flip(x, dx), flip(y, dy), flip(c, dc)))
            for s_ref, l_ref in zip(src_refs, land_refs) for dx, dy, dc in _RELATIONS]


def _sibling_plan(src_refs, land_refs):
    x, y, c = _my_pos()
    return [(s_ref.at[1 - c], l_ref, (x, y, 1 - c)) for s_ref, l_ref in zip(src_refs, land_refs)]


def _chips_plan(src_refs, land_refs):
    x, y, c = _my_pos()
    chips = [(1 - x, y), (x, 1 - y), (1 - x, 1 - y)]
    return [(s_ref.at[2 * cx + cy], l_ref.at[k], (cx, cy, c))
            for s_ref, l_ref in zip(src_refs, land_refs) for k, (cx, cy) in enumerate(chips)]


def _place_own(gathered, fam):
    x, y, c = _my_pos()
    own = lax.dynamic_index_in_dim(fam, c, 0, keepdims=True)[None]
    return lax.dynamic_update_slice(gathered, own, (2 * x + y, c) + (0,) * (fam.ndim - 1))


def _to_heads(t, width):
    return t.reshape(t.shape[0], HEADS, width).transpose(1, 0, 2)


def _from_heads(t):
    return t.transpose(1, 0, 2).reshape(t.shape[1], -1)


def _residue_major(t, d):
    h_n, s_n, e = t.shape
    return t.reshape(h_n, s_n // d, d, e).transpose(0, 2, 1, 3).reshape(h_n, s_n, e)


def _token_major(t, d):
    h_n, s_n, e = t.shape
    return t.reshape(h_n, d, s_n // d, e).transpose(0, 2, 1, 3).reshape(h_n, s_n, e)


def _t5_bucket(dist):
    max_exact = N_BUCKETS // 2
    d = jnp.maximum(dist, 1).astype(F32)
    large = max_exact + (jnp.log(d / max_exact) / math.log(MAX_DISTANCE / max_exact)
                         * (N_BUCKETS - max_exact)).astype(jnp.int32)
    large = jnp.minimum(large, N_BUCKETS - 1)
    return jnp.where(dist < max_exact, dist, large)


def _bucket_map(dilation):
    iq = jnp.arange(DIL_BLOCK)[:, None]
    ik = jnp.arange(2 * DIL_BLOCK)[None, :]
    rel = DIL_BLOCK + iq - ik
    return _t5_bucket(jnp.maximum(rel, 0) * dilation).astype(jnp.int32)


def _q_perm(w):
    w3 = w.reshape(w.shape[0], HEADS, QK_NOPE + QK_ROPE)
    return jnp.concatenate([w3[:, :, :QK_NOPE].reshape(w.shape[0], -1),
                            w3[:, :, QK_NOPE:QK_NOPE + HALF_ROPE].reshape(w.shape[0], -1),
                            w3[:, :, QK_NOPE + HALF_ROPE:].reshape(w.shape[0], -1)], axis=1)


def _q_unperm(w):
    n0, n1 = HEADS * QK_NOPE, HEADS * HALF_ROPE
    r = w.shape[0]
    return jnp.concatenate([w[:, :n0].reshape(r, HEADS, QK_NOPE), w[:, n0:n0 + n1].reshape(r, HEADS, HALF_ROPE),
                            w[:, n0 + n1:].reshape(r, HEADS, HALF_ROPE)], axis=2).reshape(r, -1)


def _kv_perm(w):
    w3 = w.reshape(w.shape[0], HEADS, QK_NOPE + V_HEAD)
    return jnp.concatenate([w3[:, :, :QK_NOPE].reshape(w.shape[0], -1), w3[:, :, QK_NOPE:].reshape(w.shape[0], -1)],
                           axis=1)


def _kv_unperm(w):
    n0 = HEADS * QK_NOPE
    r = w.shape[0]
    return jnp.concatenate([w[:, :n0].reshape(r, HEADS, QK_NOPE), w[:, n0:].reshape(r, HEADS, V_HEAD)],
                           axis=2).reshape(r, -1)


def _row(v):
    return v.reshape(1, -1)


def kernel(x, c, norm_pre, norm_post, w_mod, b_mod, ffn_w_gate, ffn_w_up, ffn_w_down, mla_w_in, mla_q_norm, mla_w_q_up, mla_kv_norm, mla_w_kv_up, mla_w_o, dil_w_in, dil_w_o, rel_bias, loss_target, m_norm_pre, m_norm_post, m_w_mod, m_b_mod, m_ffn_w_gate, m_ffn_w_up, m_ffn_w_down, m_mla_w_in, m_mla_q_norm, m_mla_w_q_up, m_mla_kv_norm, m_mla_w_kv_up, m_mla_w_o, m_dil_w_in, m_dil_w_o, m_rel_bias, v_norm_pre, v_norm_post, v_w_mod, v_b_mod, v_ffn_w_gate, v_ffn_w_up, v_ffn_w_down, v_mla_w_in, v_mla_q_norm, v_mla_w_q_up, v_mla_kv_norm, v_mla_w_kv_up, v_mla_w_o, v_dil_w_in, v_dil_w_o, v_rel_bias):
    given = dict(locals())
    ix, iy, ic = _my_pos()
    shard_id = 2 * ix + iy
    dev_id = 4 * ix + 2 * iy + ic
    x2 = x[0]
    target = loss_target[0]
    half_idx = jnp.reshape(ic, (1,)).astype(jnp.int32)
    shard_idx = jnp.reshape(shard_id, (1,)).astype(jnp.int32)

    blk = jnp.zeros((8, D_MODEL), F32)
    blk = blk.at[0].set(c[0])
    blk = blk.at[1:3].set(jnp.pad(norm_pre.reshape(-1), (0, 512)).reshape(2, D_MODEL))
    blk = blk.at[3:5].set(jnp.pad(norm_post.reshape(-1), (0, 512)).reshape(2, D_MODEL))
    got = _all_gather(blk, name="ag_c_norms", in_vmem=True).reshape(N_SHARD, 2, 8, D_MODEL)
    c_all = got[:, :, 0, :].reshape(8, D_MODEL)

    def full_norm(lo):
        t = got[:, 0, lo:lo + 2, :].reshape(N_SHARD, 2 * D_MODEL)[:, :1536].reshape(N_SHARD, 2, 3, 256)
        return t.transpose(1, 2, 0, 3).reshape(2, 3, D_MODEL)

    pre_full, post_full = full_norm(1), full_norm(3)

    silu_c = _silu_bf16(c_all, name="silu_c")
    b_cols = lax.dynamic_slice_in_dim(b_mod, shard_id * 2304, 2304, axis=1).reshape(2, 1, 2304)
    mod_part = _mm(silu_c, w_mod, bias=b_cols, name="mod_mm", tn_cap=768)
    mod_all = _all_gather(mod_part.reshape(16, 2304), name="ag_mod", in_vmem=True)
    mod_all = mod_all.reshape(N_SHARD, 2, 2, 8, 2304)[:, 0]
    mod_mine = lax.dynamic_index_in_dim(mod_all, dev_id, axis=2, keepdims=False)
    mod = mod_mine.transpose(1, 0, 2).reshape(2, 9, D_MODEL)

    bf = lambda t: t.astype(BF16)
    ffn_fam = lambda i, h: [bf(jnp.stack([ffn_w_gate[i, h], ffn_w_up[i, h]])),
                            bf(ffn_w_down[i, h].reshape(2, F_SHARD // 2, D_MODEL))]
    mla_fam = [bf(mla_w_in.reshape(2, 128, -1)), bf(mla_w_q_up.reshape(2, 192, -1)),
               bf(mla_w_kv_up.reshape(2, 128, -1)), bf(mla_w_o.reshape(2, 128, D_MODEL))]
    dil_fam = [bf(dil_w_in.reshape(2, 512, -1)), bf(dil_w_o.reshape(2, 128, D_MODEL))]
    later_fams = [ffn_fam(0, 1), ffn_fam(1, 0) + dil_fam, ffn_fam(1, 1)]
    full, later_fams = lax.optimization_barrier(
        (_gather_weights(ffn_fam(0, 0) + mla_fam, name="ag_weights_first"), later_fams))

    def gather_later(fams, tag):
        lands = [lax.empty((N_SHARD,) + t.shape, t.dtype) for t in fams]
        send, recv, thru, token = _copies_start(fams, lands, _gather_plan, 7 * len(fams), name=f"ag_start_{tag}")
        return dict(send=send, recv=recv, thru=thru, token=token, n=len(fams), tag=tag)

    def arrive(st, after):
        thru = _copies_wait(st['send'], st['recv'], st['thru'], st['n'], _gather_plan, after,
                            name=f"ag_wait_{st['tag']}")
        return [_place_own(o, t) for t, o in zip(thru[:st['n']], thru[st['n']:])]

    in_flight = [gather_later(fams, tag) for fams, tag in zip(later_fams, ("l0s2", "l1s01", "l1s2"))]
    started = sum(st['token'][0, 0] for st in in_flight)
    ffn_w = {(0, 0): (full[0], full[1].reshape(N_SHARD, F_SHARD, D_MODEL))}
    w_in = full[2].reshape(D_MODEL, -1)
    wq_p = _q_perm(full[3].reshape(N_SHARD, Q_LORA, -1).transpose(1, 0, 2).reshape(Q_LORA, -1))
    wkv_p = _kv_perm(full[4].reshape(N_SHARD, KV_LORA, -1).transpose(1, 0, 2).reshape(KV_LORA, -1))
    w_mo = full[5].reshape(D_MODEL, D_MODEL)
    dil_w = {}

    pos = jnp.arange(SEQ, dtype=F32)
    freqs = ROPE_THETA ** (-jnp.arange(HALF_ROPE, dtype=F32) / HALF_ROPE)
    ang = pos[:, None] * freqs[None, :]
    cos_k, sin_k = jnp.cos(ang), jnp.sin(ang)
    cos_q, sin_q = jnp.tile(cos_k, (1, HEADS)), jnp.tile(sin_k, (1, HEADS))

    buckets = [_bucket_map(d) for _, d in DIL_GROUPS]
    biases = [_bias_table(rel_bias[:, g * HEADS:(g + 1) * HEADS].T.reshape(HEADS, 1, N_BUCKETS), bk,
                          name=f"dil_bias_table_g{g}") for g, bk in enumerate(buckets)]

    def sub_params(i, sub):
        return dict(pg=_row(pre_full[i, sub]), qg=_row(post_full[i, sub]), sh=_row(mod[i, 3 * sub]),
                    sc=_row(mod[i, 3 * sub + 1]), gate=_row(mod[i, 3 * sub + 2]))

    def ffn_fwd(xin, i, h, sub, tie=None):
        p = sub_params(i, sub)
        if tie is not None:
            p['sh'] = p['sh'] + tie
        tag = f"l{i}s{sub}"
        w_gu, w_dn = ffn_w[i, h]
        hn = _pre_fwd(xin, p['pg'], p['sc'], p['sh'], name=f"pre_fwd_{tag}")
        gu, a = _ffn_up(hn, w_gu, name=f"ffn_up_{tag}")
        f = _mm(a, w_dn, reduce_g=True, name=f"ffn_down_{tag}")
        out = _post_fwd(f, xin, p['qg'], p['gate'], FFN_RES, name=f"post_fwd_{tag}")
        return out, dict(x=xin, hn=hn, gu=gu, a=a, f=f, p=p, i=i, h=h, tag=tag)

    def mla_fwd(xin, i, sub):
        p = sub_params(i, sub)
        tag = f"l{i}s{sub}"
        hn = _pre_fwd(xin, p['pg'], p['sc'], p['sh'], name=f"pre_fwd_{tag}")
        lat = _mm(hn, w_in, name="mla_lat")
        cq, ckv = lat[:, :Q_LORA], lat[:, Q_LORA:Q_LORA + KV_LORA]
        k1, k2 = lat[:, Q_LORA + KV_LORA:Q_LORA + KV_LORA + HALF_ROPE], lat[:, Q_LORA + KV_LORA + HALF_ROPE:]
        cqn = _rms_fwd(cq, mla_q_norm, name="mla_qnorm")
        ckvn = _rms_fwd(ckv, mla_kv_norm, name="mla_kvnorm")
        qp = _mm(cqn, wq_p, name="mla_q_up")
        kvp = _mm(ckvn, wkv_p, name="mla_kv_up")
        n0, n1 = HEADS * QK_NOPE, HEADS * HALF_ROPE
        qr1, qr2 = _rope(qp[:, n0:n0 + n1], qp[:, n0 + n1:], cos_q, sin_q, name="rope_q")
        kr1, kr2 = _rope(k1, k2, cos_k, sin_k, name="rope_k")
        q = jnp.concatenate([qp[:, :n0].reshape(SEQ, HEADS, QK_NOPE), qr1.reshape(SEQ, HEADS, HALF_ROPE),
                             qr2.reshape(SEQ, HEADS, HALF_ROPE)], axis=2).transpose(1, 0, 2).astype(BF16)
        kr = jnp.broadcast_to(jnp.concatenate([kr1, kr2], axis=1)[:, None, :], (SEQ, HEADS, QK_ROPE))
        k = jnp.concatenate([kvp[:, :n0].reshape(SEQ, HEADS, QK_NOPE), kr], axis=2).transpose(1, 0, 2).astype(BF16)
        v = _to_heads(kvp[:, n0:], V_HEAD).astype(BF16)
        o, lse = _mla_attn_fwd(q, k, v, name="mla_attn_fwd")
        o_flat = _from_heads(o).astype(BF16)
        f = _mm(o_flat, w_mo, name="mla_out")
        out = _post_fwd(f, xin, p['qg'], p['gate'], 1.0, name=f"post_fwd_{tag}")
        return out, dict(x=xin, hn=hn, cq=cq, ckv=ckv, cqn=cqn, ckvn=ckvn, q=q, k=k, v=v, o=o, lse=lse,
                         o_flat=o_flat, f=f, p=p, tag=tag)

    def dil_fwd(xin, i, sub):
        p = sub_params(i, sub)
        tag = f"l{i}s{sub}"
        hn = _pre_fwd(xin, p['pg'], p['sc'], p['sh'], name=f"pre_fwd_{tag}")
        proj = _mm(hn, dil_w['in'], out_dtype=BF16, tn_cap=768, name="dil_proj")
        heads = proj.reshape(N_SHARD, SEQ, 36, 64).transpose(0, 2, 1, 3).reshape(3, 3, HEADS, SEQ, 64)
        qkv, outs, lses = [], [], []
        for g, (window, d) in enumerate(DIL_GROUPS):
            q, k, v = (_residue_major(heads[g, t], d) for t in range(3))
            o, lse = _dil_attn_fwd(q, k, v, biases[g], SEQ // d // DIL_BLOCK, name=f"dil_attn_fwd_g{g}")
            qkv.append((q, k, v))
            outs.append(_token_major(o, d))
            lses.append(_token_major(lse, d))
        mix = _dil_mix_fwd(outs, lses, name="dil_mix_fwd")
        o_flat = _from_heads(mix).astype(BF16)
        f = _mm(o_flat, dil_w['out'], name="dil_out")
        out = _post_fwd(f, xin, p['qg'], p['gate'], 1.0, name=f"post_fwd_{tag}")
        return out, dict(x=xin, hn=hn, qkv=qkv, outs=outs, lses=lses, o_flat=o_flat, f=f, p=p, tag=tag)

    def set_ffn(i, h, w_gu, w_dn):
        ffn_w[i, h] = (w_gu, w_dn.reshape(N_SHARD, F_SHARD, D_MODEL))

    saved = [None] * 6
    xs, saved[0] = ffn_fwd(x2, 0, 0, 0, tie=started)
    xs, saved[1] = mla_fwd(xs, 0, 1)
    set_ffn(0, 1, *arrive(in_flight[0], xs))
    xs, saved[2] = ffn_fwd(xs, 0, 1, 2)
    got = arrive(in_flight[1], xs)
    set_ffn(1, 0, got[0], got[1])
    dil_w['in'], dil_w['out'] = got[2].reshape(N_SHARD, D_MODEL, -1), got[3].reshape(D_MODEL, D_MODEL)
    xs, saved[3] = ffn_fwd(xs, 1, 0, 0)
    xs, saved[4] = dil_fwd(xs, 1, 1)
    set_ffn(1, 1, *arrive(in_flight[2], xs))
    xs, saved[5] = ffn_fwd(xs, 1, 1, 2)

    dx, loss_part = _loss(xs, target, name="loss")
    loss = lax.psum(loss_part[0, 0], ("x", "y", "c"))

    dmod = [[None] * 9 for _ in range(2)]
    dpre = [[None] * 3 for _ in range(2)]
    dpost = [[None] * 3 for _ in range(2)]
    gu_shape = (2, N_SHARD, 2, D_MODEL, F_SHARD)
    dn_shape = (2, N_SHARD, F_SHARD, D_MODEL)
    bufs = [dict(gu=lax.empty(gu_shape, F32), dn=lax.empty(dn_shape, F32)) for _ in range(2)]
    row_unit = lambda g, r: ((r % 2, r // 2), 0)

    def close_sub(dhn, dout, sv, i, sub, res_dgate, res_dqg):
        p = sv['p']
        dxs, dsh, dsc, dpg = _pre_bwd(dhn, sv['x'], dout, p['pg'], p['sc'], name=f"pre_bwd_{sv['tag']}")
        dmod[i][3 * sub], dmod[i][3 * sub + 1], dmod[i][3 * sub + 2] = dsh, dsc, res_dgate
        dpre[i][sub], dpost[i][sub] = dpg, res_dqg
        return dxs

    def ffn_bwd(dout, sv, sub, tie=0.0):
        i, h, p, tag = sv['i'], sv['h'], sv['p'], sv['tag']
        w_gu, w_dn = ffn_w[i, h]
        df, dgate, dqg = _post_bwd(dout, sv['f'], p['qg'] + tie, p['gate'], FFN_RES, name=f"post_bwd_{tag}")
        bufs[i]['dn'] = _mm(sv['a'], df, ta=True, out_shape=dn_shape, out_sel=lambda g, r: ((h, g), r),
                            out_buf=bufs[i]['dn'], name=f"ffn_dwd_{tag}")
        dgu = _ffn_dgu(df, w_dn, sv['gu'], name=f"ffn_dgu_{tag}")
        dgu = dgu.reshape(2 * N_SHARD, SEQ, F_SHARD)
        bufs[i]['gu'] = _mm(sv['hn'], dgu, ta=True, out_shape=gu_shape,
                            out_sel=lambda g, r: ((h, g // 2, g % 2), r), out_buf=bufs[i]['gu'],
                            name=f"ffn_dwgu_{tag}")
        dhn = _mm(dgu, w_gu.reshape(2 * N_SHARD, D_MODEL, F_SHARD), tb=True, reduce_g=True, name=f"ffn_dhn_{tag}")
        return close_sub(dhn, dout, sv, i, sub, dgate, dqg)

    def mla_bwd(dout, sv, i, sub, tie=0.0):
        p, tag = sv['p'], sv['tag']
        df, dgate, dqg = _post_bwd(dout, sv['f'], p['qg'] + tie, p['gate'], 1.0, name=f"post_bwd_{tag}")
        u_wo = _mm(sv['o_flat'], df, ta=True, tm_cap=128, out_shape=(2, N_SHARD, 128, D_MODEL), out_sel=row_unit,
                   name="mla_dwo")
        do_flat = _mm(df, w_mo, tb=True, name="mla_do")
        do = _to_heads(do_flat, V_HEAD)
        dq, dk, dv = _mla_attn_bwd(sv['q'], sv['k'], sv['v'], sv['o'], do, sv['lse'], name="mla_attn_bwd")
        dq_t = dq.transpose(1, 0, 2)
        dqr1, dqr2 = _rope(dq_t[:, :, QK_NOPE:QK_NOPE + HALF_ROPE].reshape(SEQ, -1),
                           dq_t[:, :, QK_NOPE + HALF_ROPE:].reshape(SEQ, -1), cos_q, -sin_q, name="rope_q_bwd")
        dqp = jnp.concatenate([dq_t[:, :, :QK_NOPE].reshape(SEQ, -1), dqr1, dqr2], axis=1).astype(BF16)
        dkr = _head_sum(dk[:, :, QK_NOPE:], name="mla_dkr_sum")
        dk1, dk2 = _rope(dkr[:, :HALF_ROPE], dkr[:, HALF_ROPE:], cos_k, -sin_k, name="rope_k_bwd")
        dkvp = jnp.concatenate([_from_heads(dk[:, :, :QK_NOPE]), _from_heads(dv)], axis=1).astype(BF16)
        g_wq = _q_unperm(_mm(sv['cqn'], dqp, ta=True, name="mla_dwq"))
        g_wkv = _kv_unperm(_mm(sv['ckvn'], dkvp, ta=True, name="mla_dwkv"))
        dcqn = _mm(dqp, wq_p, tb=True, name="mla_dcqn")
        dckvn = _mm(dkvp, wkv_p, tb=True, name="mla_dckvn")
        dcq, g_qn = _rms_bwd(dcqn, sv['cq'], mla_q_norm, name="mla_qnorm_bwd")
        dckv, g_kvn = _rms_bwd(dckvn, sv['ckv'], mla_kv_norm, name="mla_kvnorm_bwd")
        dlat = jnp.concatenate([dcq, dckv, dk1, dk2], axis=1).astype(BF16)
        u_win = _mm(sv['hn'], dlat, ta=True, tm_cap=128, out_shape=(2, N_SHARD, 128, dlat.shape[1]),
                    out_sel=row_unit, name="mla_dwin")
        dhn = _mm(dlat, w_in, tb=True, name="mla_dhn")
        col_unit = lambda t: (t.reshape(t.shape[0], N_SHARD, -1).transpose(1, 0, 2)
                              .reshape(N_SHARD, 2, t.shape[0] // 2, -1).transpose(1, 0, 2, 3))
        grads = dict(units=[u_win, col_unit(g_wq), col_unit(g_wkv), u_wo], q_norm=g_qn, kv_norm=g_kvn)
        return close_sub(dhn, dout, sv, i, sub, dgate, dqg), grads

    def dil_bwd(dout, sv, i, sub):
        p, tag = sv['p'], sv['tag']
        df, dgate, dqg = _post_bwd(dout, sv['f'], p['qg'], p['gate'], 1.0, name=f"post_bwd_{tag}")
        u_wo = _mm(sv['o_flat'], df, ta=True, tm_cap=128, out_shape=(2, N_SHARD, 128, D_MODEL), out_sel=row_unit,
                   name="dil_dwo")
        do = _to_heads(_mm(df, dil_w['out'], tb=True, name="dil_do"), 64)
        dos, dlts = _dil_mix_bwd(do, sv['outs'], sv['lses'], name="dil_mix_bwd")
        pieces = []
        bias_rows = []
        for g, (window, d) in enumerate(DIL_GROUPS):
            q, k, v = sv['qkv'][g]
            dq, dk, dv, dbias = _dil_attn_bwd(q, k, v, biases[g], _residue_major(sv['lses'][g], d),
                                              _residue_major(dos[g], d), _residue_major(dlts[g], d),
                                              SEQ // d // DIL_BLOCK, name=f"dil_attn_bwd_g{g}")
            pieces += [_token_major(t, d).astype(BF16) for t in (dq, dk, dv)]
            bias_rows.append(_bias_grad(dbias, buckets[g], name=f"dil_bias_grad_g{g}")[:, 0, :])
        dheads = jnp.stack(pieces).reshape(N_SHARD, 36, SEQ, 64).transpose(0, 2, 1, 3).reshape(N_SHARD, SEQ, 2304)
        u_win = _mm(sv['hn'], dheads, ta=True, tn_cap=768, out_shape=(2, N_SHARD, 512, 2304),
                    out_sel=lambda g, r: ((r, g), 0), name="dil_dwin")
        dhn = _mm(dheads, dil_w['in'], tb=True, reduce_g=True, name="dil_dhn")
        g_bias = jnp.concatenate(bias_rows, axis=0).T
        grads = dict(units=[u_win, u_wo], rel_bias=g_bias)
        return close_sub(dhn, dout, sv, i, sub, dgate, dqg), grads

    dx = ffn_bwd(dx, saved[5], 2)
    dx, dil_g = dil_bwd(dx, saved[4], 1, 1)
    dx = ffn_bwd(dx, saved[3], 0)
    units1 = [bufs[1]['gu'], bufs[1]['dn'], *dil_g['units']]
    n1 = len(units1)
    send, recv, thru, token = _copies_start(units1, [lax.empty(u.shape[1:], F32) for u in units1], _sibling_plan, n1,
                                            name="rs1_sibling_start")
    dx = ffn_bwd(dx, saved[2], 2, tie=token[0, 0])
    thru = _copies_wait(send, recv, thru, n1, _sibling_plan, dx, name="rs1_sibling_wait")
    parts1 = [_add_half(u, g, half_idx, name=f"rs1_add_half_{k}") for k, (u, g) in enumerate(zip(thru[:n1], thru[n1:]))]
    send, recv, thru, token = _copies_start([w for _, w in parts1],
                                            [lax.empty((3,) + w.shape[1:], BF16) for _, w in parts1], _chips_plan,
                                            3 * n1, name="rs1_chips_start")
    dx, mla_g = mla_bwd(dx, saved[1], 0, 1, tie=token[0, 0])
    thru = _copies_wait(send, recv, thru, n1, _chips_plan, dx, name="rs1_chips_wait")
    reds1 = [_add_shards(p, g, shard_idx, name=f"rs1_add_shards_{k}")
             for k, ((p, _), g) in enumerate(zip(parts1, thru[n1:]))]
    dx = ffn_bwd(dx, saved[0], 0)
    grad_x = dx[None]

    pad_row = lambda v: jnp.pad(v.reshape(-1), (0, (-v.size) % D_MODEL)).reshape(-1, D_MODEL)
    small = jnp.concatenate(
        [jnp.concatenate([dmod[i][r] for i in range(2) for r in range(9)], axis=0),
         jnp.concatenate([dpre[i][s] for i in range(2) for s in range(3)], axis=0),
         jnp.concatenate([dpost[i][s] for i in range(2) for s in range(3)], axis=0),
         pad_row(mla_g['q_norm']), pad_row(mla_g['kv_norm']), pad_row(dil_g['rel_bias'])], axis=0)
    small = jnp.pad(small, ((0, SMALL_ROWS - small.shape[0]), (0, 0)))
    small_all = _all_gather(small, name="ag_small_grads", in_vmem=True)
    small_sum = _sum_devices(small_all, 8, name="sum_small_grads")
    g_b_mod = small_sum[0:18].reshape(2, 9 * D_MODEL)
    my_cols = lambda t: lax.dynamic_slice_in_dim(t, shard_id * 256, 256, axis=2)
    g_norm_pre = my_cols(small_sum[18:24].reshape(2, 3, D_MODEL))
    g_norm_post = my_cols(small_sum[24:30].reshape(2, 3, D_MODEL))
    g_q_norm = small_sum[30, :Q_LORA].reshape(1, Q_LORA)
    g_kv_norm = small_sum[31, :KV_LORA].reshape(1, KV_LORA)
    g_rel_bias = small_sum[32:34].reshape(-1)[:N_BUCKETS * 48].reshape(N_BUCKETS, 48)
    dmod_all = small_all.reshape(8, SMALL_ROWS, D_MODEL)[:, 0:18].reshape(8, 2, 9 * D_MODEL)
    dmod_cols = lax.dynamic_slice_in_dim(dmod_all, shard_id * 2304, 2304, axis=2).transpose(1, 0, 2)
    g_w_mod = _mm(silu_c, dmod_cols.astype(BF16), ta=True, tn_cap=768, name="w_mod_grad")

    units0 = [bufs[0]['gu'], bufs[0]['dn'], *mla_g['units']]
    got_a = _swap_halves(units0, name="rs0_sibling")
    parts0 = [_add_half(u, g, half_idx, name=f"rs0_add_half_{k}") for k, (u, g) in enumerate(zip(units0, got_a))]
    got_b = _send_to_chips([w for _, w in parts0], name="rs0_chips")
    reds0 = [_add_shards(p, g, shard_idx, name=f"rs0_add_shards_{k}")
             for k, ((p, _), g) in enumerate(zip(parts0, got_b))]
    fin = _pair_gather(reds0 + reds1, name="rs_pair_gather")
    gu_fin, dn_fin = (fin[0], fin[6]), (fin[1], fin[7])
    reduced = dict(ffn_w_gate=jnp.stack([t[:, 0] for t in gu_fin]), ffn_w_up=jnp.stack([t[:, 1] for t in gu_fin]),
                   ffn_w_down=jnp.stack(dn_fin))
    for n, t in zip(['mla_w_in', 'mla_w_q_up', 'mla_w_kv_up', 'mla_w_o', 'dil_w_in', 'dil_w_o'], fin[2:6] + fin[8:]):
        reduced[n] = t.reshape(given[n].shape)

    grads = dict(norm_pre=g_norm_pre, norm_post=g_norm_post, w_mod=g_w_mod, b_mod=g_b_mod, mla_q_norm=g_q_norm,
                 mla_kv_norm=g_kv_norm, rel_bias=g_rel_bias, **reduced)

    deltas, new_m, new_v = {}, {}, {}
    for n in WEIGHTS:
        deltas[n], new_m[n], new_v[n] = _adamw(given[n], grads[n], given["m_" + n], given["v_" + n],
                                               name=f"adamw_{n}")
    return (loss, grad_x, *[grads[n] for n in WEIGHTS], *[deltas[n] for n in WEIGHTS],
            *[new_m[n] for n in WEIGHTS], *[new_v[n] for n in WEIGHTS])
```

```python
import math

import jax
import jax.numpy as jnp
from jax import lax
from jax.experimental import pallas as pl
from jax.experimental.pallas import tpu as pltpu

F32 = jnp.float32
BF16 = jnp.bfloat16
MESH = pl.DeviceIdType.MESH

SEQ = 2048
D_MODEL = 1024
D_FF = 2816
N_SHARD = 4
F_SHARD = D_FF // N_SHARD
EPS = 1e-6
FFN_RES = 0.5
HEADS = 16
Q_LORA, KV_LORA, QK_NOPE, QK_ROPE, V_HEAD = 384, 256, 64, 32, 64
HALF_ROPE = QK_ROPE // 2
ROPE_THETA = 10000.0
DIL_GROUPS = ((128, 1), (512, 4), (2048, 16))
DIL_BLOCK = 128
N_BUCKETS = 32
MAX_DISTANCE = 2048
ADAM_LR, ADAM_B1, ADAM_B2, ADAM_EPS, ADAM_WD, ADAM_STEP = 0.001, 0.9, 0.999, 1e-08, 0.01, 10

VMEM_LIMIT = 48 * 1024 * 1024
SMALL_ROWS = 40

WEIGHTS = ['norm_pre', 'norm_post', 'w_mod', 'b_mod', 'ffn_w_gate', 'ffn_w_up', 'ffn_w_down', 'mla_w_in',
           'mla_q_norm', 'mla_w_q_up', 'mla_kv_norm', 'mla_w_kv_up', 'mla_w_o', 'dil_w_in', 'dil_w_o', 'rel_bias']


def _cparams(**kw):
    return pltpu.CompilerParams(vmem_limit_bytes=VMEM_LIMIT, **kw)


def _pick(n, cap, mult=128):
    if n <= cap:
        return n
    best = n
    for t in range(mult, cap + 1, mult):
        if n % t == 0:
            best = t
    return best


def _mm(a, b, *, name, ta=False, tb=False, reduce_g=False, bias=None, out_dtype=F32, tm_cap=512, tn_cap=1024,
        g_n=None, b_sel=None, out_shape=None, out_sel=None, out_buf=None):
    a3 = a if a.ndim == 3 else a[None]
    ga = a3.shape[0]
    if b_sel is None:
        b_n = b if b.ndim == 3 else b[None]
        gb = b_n.shape[0]
        b_sel = (lambda g: (g,)) if gb > 1 else (lambda g: (0,))
        g_n = max(ga, gb)
    else:
        b_n = b
    k_dim, m_dim = (a3.shape[1], a3.shape[2]) if ta else (a3.shape[2], a3.shape[1])
    k2, n_dim = (b_n.shape[-1], b_n.shape[-2]) if tb else (b_n.shape[-2], b_n.shape[-1])
    assert k_dim == k2, (a.shape, b.shape)
    tm = _pick(m_dim, tm_cap, 128 if ta else 8)
    tn = _pick(n_dim, tn_cap, 128)
    mt, nt = m_dim // tm, n_dim // tn
    dims = (((0 if ta else 1,), (1 if tb else 0,)), ((), ()))

    if reduce_g:
        grid = (mt, nt, g_n)
        ids = lambda i, j, g: (g, i, j)
    else:
        grid = (g_n, mt, nt)
        ids = lambda g, i, j: (g, i, j)

    def a_map(*p):
        g, i, j = ids(*p)
        g = g if ga > 1 else 0
        return (g, 0, i) if ta else (g, i, 0)

    def b_map(*p):
        g, i, j = ids(*p)
        return (*b_sel(g), j, 0) if tb else (*b_sel(g), 0, j)

    b_lead = (None,) * (b_n.ndim - 2)
    a_spec = pl.BlockSpec((None, k_dim, tm) if ta else (None, tm, k_dim), a_map)
    b_spec = pl.BlockSpec(b_lead + ((tn, k_dim) if tb else (k_dim, tn)), b_map)
    in_specs = [a_spec, b_spec]
    operands = [a3, b_n]
    if bias is not None:
        assert not reduce_g and bias.shape == (g_n, 1, n_dim)
        in_specs.append(pl.BlockSpec((None, 1, tn), lambda g, i, j: (g, 0, j)))
        operands.append(bias)
    aliases = {}
    if out_buf is not None:
        assert tuple(out_buf.shape) == tuple(out_shape) and out_buf.dtype == out_dtype
        in_specs.append(pl.BlockSpec(memory_space=pl.ANY))
        operands.append(out_buf)
        aliases = {len(operands) - 1: 0}

    if reduce_g:
        out_spec = pl.BlockSpec((tm, tn), lambda i, j, g: (i, j))
        out_sds = jax.ShapeDtypeStruct((m_dim, n_dim), F32)
    elif out_shape is not None:
        def o_map(g, i, j):
            lead, rb = out_sel(g, i)
            return (*lead, rb, j)

        out_spec = pl.BlockSpec((None,) * (len(out_shape) - 2) + (tm, tn), o_map)
        out_sds = jax.ShapeDtypeStruct(tuple(out_shape), out_dtype)
    else:
        out_spec = pl.BlockSpec((None, tm, tn), lambda g, i, j: (g, i, j))
        out_sds = jax.ShapeDtypeStruct((g_n, m_dim, n_dim), out_dtype)

    def body(a_ref, b_ref, *rest):
        o_ref = rest[-1]
        r = lax.dot_general(a_ref[...].astype(BF16), b_ref[...].astype(BF16), dims, preferred_element_type=F32)
        if bias is not None:
            r = r + rest[0][...]
        if reduce_g:
            g = pl.program_id(2)

            @pl.when(g == 0)
            def _():
                o_ref[...] = r

            @pl.when(g > 0)
            def _():
                o_ref[...] += r
        else:
            o_ref[...] = r.astype(o_ref.dtype)

    out = pl.pallas_call(body, grid=grid, in_specs=in_specs, out_specs=out_spec, out_shape=out_sds,
                         input_output_aliases=aliases, compiler_params=_cparams(), name=name)(*operands)
    if not reduce_g and out_shape is None and a.ndim == 2 and b.ndim == 2:
        out = out[0]
    return out


def _rows(tm, w):
    return pl.BlockSpec((tm, w), lambda i: (i, 0))


def _vec(w):
    return pl.BlockSpec((1, w), lambda i: (0, 0))


def _rstd(v):
    return lax.rsqrt(jnp.mean(v * v, axis=-1, keepdims=True) + EPS)


def _pre_fwd(x, pg, sc, sh, *, name):
    s_n, w = x.shape
    tm = _pick(s_n, 256, 8)

    def body(x_ref, pg_ref, sc_ref, sh_ref, o_ref):
        xv = x_ref[...]
        n = (xv * _rstd(xv)) * pg_ref[...]
        o_ref[...] = (n * (1.0 + sc_ref[...]) + sh_ref[...]).astype(o_ref.dtype)

    return pl.pallas_call(body, grid=(s_n // tm,), in_specs=[_rows(tm, w), _vec(w), _vec(w), _vec(w)],
                          out_specs=_rows(tm, w), out_shape=jax.ShapeDtypeStruct((s_n, w), BF16),
                          compiler_params=_cparams(), name=name)(x, pg, sc, sh)


def _post_fwd(f, x, qg, gate, res_w, *, name):
    s_n, w = x.shape
    tm = _pick(s_n, 256, 8)

    def body(f_ref, x_ref, qg_ref, gate_ref, o_ref):
        fv = f_ref[...]
        y = (fv * _rstd(fv)) * qg_ref[...]
        o_ref[...] = x_ref[...] + (res_w * gate_ref[...]) * y

    return pl.pallas_call(body, grid=(s_n // tm,), in_specs=[_rows(tm, w), _rows(tm, w), _vec(w), _vec(w)],
                          out_specs=_rows(tm, w), out_shape=jax.ShapeDtypeStruct((s_n, w), F32),
                          compiler_params=_cparams(), name=name)(f, x, qg, gate)


def _post_bwd(dout, f, qg, gate, res_w, *, name):
    s_n, w = f.shape
    tm = _pick(s_n, 256, 8)

    def body(do_ref, f_ref, qg_ref, gate_ref, df_ref, dgate_ref, dqg_ref):
        @pl.when(pl.program_id(0) == 0)
        def _():
            dgate_ref[...] = jnp.zeros_like(dgate_ref)
            dqg_ref[...] = jnp.zeros_like(dqg_ref)

        do = do_ref[...]
        fv = f_ref[...]
        r = _rstd(fv)
        fh = fv * r
        qg_v = qg_ref[...]
        dgate_ref[...] += res_w * jnp.sum(do * (fh * qg_v), axis=0, keepdims=True)
        dy = do * (res_w * gate_ref[...])
        dqg_ref[...] += jnp.sum(dy * fh, axis=0, keepdims=True)
        dfh = dy * qg_v
        df = r * (dfh - fh * jnp.mean(dfh * fh, axis=-1, keepdims=True))
        df_ref[...] = df.astype(df_ref.dtype)

    return pl.pallas_call(
        body, grid=(s_n // tm,), in_specs=[_rows(tm, w), _rows(tm, w), _vec(w), _vec(w)],
        out_specs=[_rows(tm, w), _vec(w), _vec(w)],
        out_shape=[jax.ShapeDtypeStruct((s_n, w), BF16), jax.ShapeDtypeStruct((1, w), F32),
                   jax.ShapeDtypeStruct((1, w), F32)],
        compiler_params=_cparams(), name=name)(dout, f, qg, gate)


def _pre_bwd(dhn, x, dout, pg, sc, *, name):
    s_n, w = x.shape
    tm = _pick(s_n, 256, 8)

    def body(dhn_ref, x_ref, do_ref, pg_ref, sc_ref, dx_ref, dsh_ref, dsc_ref, dpg_ref):
        @pl.when(pl.program_id(0) == 0)
        def _():
            dsh_ref[...] = jnp.zeros_like(dsh_ref)
            dsc_ref[...] = jnp.zeros_like(dsc_ref)
            dpg_ref[...] = jnp.zeros_like(dpg_ref)

        dhn_v = dhn_ref[...]
        xv = x_ref[...]
        r = _rstd(xv)
        xh = xv * r
        pg_v = pg_ref[...]
        dsh_ref[...] += jnp.sum(dhn_v, axis=0, keepdims=True)
        dsc_ref[...] += jnp.sum(dhn_v * (xh * pg_v), axis=0, keepdims=True)
        dn = dhn_v * (1.0 + sc_ref[...])
        dpg_ref[...] += jnp.sum(dn * xh, axis=0, keepdims=True)
        dxh = dn * pg_v
        dx_ref[...] = do_ref[...] + r * (dxh - xh * jnp.mean(dxh * xh, axis=-1, keepdims=True))

    vec = jax.ShapeDtypeStruct((1, w), F32)
    return pl.pallas_call(
        body, grid=(s_n // tm,), in_specs=[_rows(tm, w), _rows(tm, w), _rows(tm, w), _vec(w), _vec(w)],
        out_specs=[_rows(tm, w), _vec(w), _vec(w), _vec(w)],
        out_shape=[jax.ShapeDtypeStruct((s_n, w), F32), vec, vec, vec],
        compiler_params=_cparams(), name=name)(dhn, x, dout, pg, sc)


def _rms_fwd(x, g, *, name):
    s_n, w = x.shape
    tm = _pick(s_n, 512, 8)

    def body(x_ref, g_ref, o_ref):
        xv = x_ref[...]
        o_ref[...] = ((xv * _rstd(xv)) * g_ref[...]).astype(o_ref.dtype)

    return pl.pallas_call(body, grid=(s_n // tm,), in_specs=[_rows(tm, w), _vec(w)], out_specs=_rows(tm, w),
                          out_shape=jax.ShapeDtypeStruct((s_n, w), BF16), compiler_params=_cparams(),
                          name=name)(x, g)


def _rms_bwd(dy, x, g, *, name):
    s_n, w = x.shape
    tm = _pick(s_n, 512, 8)

    def body(dy_ref, x_ref, g_ref, dx_ref, dg_ref):
        @pl.when(pl.program_id(0) == 0)
        def _():
            dg_ref[...] = jnp.zeros_like(dg_ref)

        dy_v = dy_ref[...]
        xv = x_ref[...]
        r = _rstd(xv)
        xh = xv * r
        dg_ref[...] += jnp.sum(dy_v * xh, axis=0, keepdims=True)
        dxh = dy_v * g_ref[...]
        dx_ref[...] = r * (dxh - xh * jnp.mean(dxh * xh, axis=-1, keepdims=True))

    return pl.pallas_call(
        body, grid=(s_n // tm,), in_specs=[_rows(tm, w), _rows(tm, w), _vec(w)],
        out_specs=[_rows(tm, w), _vec(w)],
        out_shape=[jax.ShapeDtypeStruct((s_n, w), F32), jax.ShapeDtypeStruct((1, w), F32)],
        compiler_params=_cparams(), name=name)(dy, x, g)


def _rope(a1, a2, cos, sin, *, name):
    s_n, w = a1.shape
    tm = _pick(s_n, 512, 8)

    def body(a1_ref, a2_ref, c_ref, s_ref, r1_ref, r2_ref):
        u, v, c_v, s_v = a1_ref[...], a2_ref[...], c_ref[...], s_ref[...]
        r1_ref[...] = u * c_v - v * s_v
        r2_ref[...] = u * s_v + v * c_v

    sd = jax.ShapeDtypeStruct((s_n, w), F32)
    return pl.pallas_call(body, grid=(s_n // tm,), in_specs=[_rows(tm, w)] * 4, out_specs=[_rows(tm, w)] * 2,
                          out_shape=[sd, sd], compiler_params=_cparams(), name=name)(a1, a2, cos, sin)


def _silu_bf16(x, *, name):
    def body(x_ref, o_ref):
        xv = x_ref[...]
        o_ref[...] = (xv * jax.nn.sigmoid(xv)).astype(o_ref.dtype)

    return pl.pallas_call(body, out_shape=jax.ShapeDtypeStruct(x.shape, BF16), name=name)(x)


def _loss(y, target, *, name):
    s_n, w = y.shape
    tm = _pick(s_n, 256, 8)

    def body(y_ref, t_ref, dy_ref, l_ref):
        @pl.when(pl.program_id(0) == 0)
        def _():
            l_ref[...] = jnp.zeros_like(l_ref)

        e = y_ref[...] - t_ref[...]
        dy_ref[...] = e * (1.0 / w)
        row = jnp.mean(e * e, axis=-1, keepdims=True)
        l_ref[...] += 0.5 * jnp.sum(row, axis=0, keepdims=True)

    return pl.pallas_call(
        body, grid=(s_n // tm,), in_specs=[_rows(tm, w), _rows(tm, w)],
        out_specs=[_rows(tm, w), pl.BlockSpec((1, 1), lambda i: (0, 0))],
        out_shape=[jax.ShapeDtypeStruct((s_n, w), F32), jax.ShapeDtypeStruct((1, 1), F32)],
        compiler_params=_cparams(), name=name)(y, target)


FFN_TM = 512


def _ffn_up(hn, w_gu, *, name):
    s_n, d = hn.shape
    f = w_gu.shape[-1]
    tm = _pick(s_n, FFN_TM, 8)

    def body(hn_ref, wg_ref, wu_ref, gu_ref, a_ref):
        xv = hn_ref[...]
        g = jnp.dot(xv, wg_ref[...], preferred_element_type=F32)
        u = jnp.dot(xv, wu_ref[...], preferred_element_type=F32)
        gu_ref[0] = g.astype(BF16)
        gu_ref[1] = u.astype(BF16)
        a_ref[...] = ((g * jax.nn.sigmoid(g)) * u).astype(BF16)

    w_blk = lambda t: pl.BlockSpec((None, None, d, f), lambda s, m: (s, t, 0, 0))
    return pl.pallas_call(
        body, grid=(N_SHARD, s_n // tm),
        in_specs=[pl.BlockSpec((tm, d), lambda s, m: (m, 0)), w_blk(0), w_blk(1)],
        out_specs=[pl.BlockSpec((None, 2, tm, f), lambda s, m: (s, 0, m, 0)),
                   pl.BlockSpec((None, tm, f), lambda s, m: (s, m, 0))],
        out_shape=[jax.ShapeDtypeStruct((N_SHARD, 2, s_n, f), BF16), jax.ShapeDtypeStruct((N_SHARD, s_n, f), BF16)],
        compiler_params=_cparams(), name=name)(hn, w_gu, w_gu)


def _ffn_dgu(df, w_dn, gu, *, name):
    s_n, d = df.shape
    f = w_dn.shape[-2]
    tm = _pick(s_n, FFN_TM, 8)

    def body(df_ref, wd_ref, gu_ref, o_ref):
        da = lax.dot_general(df_ref[...], wd_ref[...], (((1,), (1,)), ((), ())), preferred_element_type=F32)
        g = gu_ref[0].astype(F32)
        u = gu_ref[1].astype(F32)
        sig = jax.nn.sigmoid(g)
        o_ref[0] = (da * u * (sig * (1.0 + g * (1.0 - sig)))).astype(BF16)
        o_ref[1] = (da * (g * sig)).astype(BF16)

    gu_blk = pl.BlockSpec((None, 2, tm, f), lambda s, m: (s, 0, m, 0))
    return pl.pallas_call(
        body, grid=(N_SHARD, s_n // tm),
        in_specs=[pl.BlockSpec((tm, d), lambda s, m: (m, 0)),
                  pl.BlockSpec((None, f, d), lambda s, m: (s, 0, 0)), gu_blk],
        out_specs=gu_blk, out_shape=jax.ShapeDtypeStruct((N_SHARD, 2, s_n, f), BF16),
        compiler_params=_cparams(), name=name)(df, w_dn, gu)


_NT = (((1,), (1,)), ((), ()))
_TN = (((0,), (0,)), ((), ()))
MLA_TQ = 256


def _causal_mask(i, tq, s_n):
    qpos = i * tq + lax.broadcasted_iota(jnp.int32, (tq, s_n), 0)
    kpos = lax.broadcasted_iota(jnp.int32, (tq, s_n), 1)
    return kpos <= qpos


def _mla_attn_fwd(q, k, v, *, name):
    h_n, s_n, dq = q.shape
    dv = v.shape[-1]
    tq = MLA_TQ
    scale = float(dq) ** -0.5

    def body(q_ref, k_ref, v_ref, o_ref, lse_ref):
        i = pl.program_id(1)
        for e in range(1, s_n // tq + 1):
            @pl.when(i == e - 1)
            def _(ext=e * tq):
                mask = _causal_mask(i, tq, ext)
                s = lax.dot_general(q_ref[...], k_ref[0:ext, :], _NT, preferred_element_type=F32) * scale
                s = jnp.where(mask, s, -jnp.inf)
                m = jnp.max(s, axis=-1, keepdims=True)
                p = jnp.exp(s - m)
                l = jnp.sum(p, axis=-1, keepdims=True)
                o = jnp.dot(p.astype(BF16), v_ref[0:ext, :], preferred_element_type=F32)
                o_ref[...] = o / l
                lse_ref[...] = m + jnp.log(l)

    return pl.pallas_call(
        body, grid=(h_n, s_n // tq),
        in_specs=[pl.BlockSpec((None, tq, dq), lambda h, i: (h, i, 0)),
                  pl.BlockSpec((None, s_n, dq), lambda h, i: (h, 0, 0)),
                  pl.BlockSpec((None, s_n, dv), lambda h, i: (h, 0, 0))],
        out_specs=[pl.BlockSpec((None, tq, dv), lambda h, i: (h, i, 0)),
                   pl.BlockSpec((None, tq, 1), lambda h, i: (h, i, 0))],
        out_shape=[jax.ShapeDtypeStruct((h_n, s_n, dv), F32), jax.ShapeDtypeStruct((h_n, s_n, 1), F32)],
        compiler_params=_cparams(), name=name)(q, k, v)


def _mla_attn_bwd(q, k, v, o, do, lse, *, name):
    h_n, s_n, dq = q.shape
    dv = v.shape[-1]
    tq = MLA_TQ
    scale = float(dq) ** -0.5

    def body(q_ref, k_ref, v_ref, o_ref, do_ref, lse_ref, dq_ref, dk_ref, dv_ref):
        i = pl.program_id(1)

        @pl.when(i == 0)
        def _():
            dk_ref[...] = jnp.zeros_like(dk_ref)
            dv_ref[...] = jnp.zeros_like(dv_ref)

        for e in range(1, s_n // tq + 1):
            @pl.when(i == e - 1)
            def _(ext=e * tq):
                mask = _causal_mask(i, tq, ext)
                qv, kv, vv = q_ref[...], k_ref[0:ext, :], v_ref[0:ext, :]
                do_v = do_ref[...]
                s = lax.dot_general(qv, kv, _NT, preferred_element_type=F32) * scale
                p = jnp.where(mask, jnp.exp(s - lse_ref[...]), 0.0)
                dob = do_v.astype(BF16)
                dv_ref[0:ext, :] += lax.dot_general(p.astype(BF16), dob, _TN, preferred_element_type=F32)
                dp = lax.dot_general(dob, vv, _NT, preferred_element_type=F32)
                delta = jnp.sum(do_v * o_ref[...], axis=-1, keepdims=True)
                dsb = (p * (dp - delta) * scale).astype(BF16)
                dq_ref[...] = jnp.dot(dsb, kv, preferred_element_type=F32)
                dk_ref[0:ext, :] += lax.dot_general(dsb, qv, _TN, preferred_element_type=F32)

    return pl.pallas_call(
        body, grid=(h_n, s_n // tq),
        in_specs=[pl.BlockSpec((None, tq, dq), lambda h, i: (h, i, 0)),
                  pl.BlockSpec((None, s_n, dq), lambda h, i: (h, 0, 0)),
                  pl.BlockSpec((None, s_n, dv), lambda h, i: (h, 0, 0)),
                  pl.BlockSpec((None, tq, dv), lambda h, i: (h, i, 0)),
                  pl.BlockSpec((None, tq, dv), lambda h, i: (h, i, 0)),
                  pl.BlockSpec((None, tq, 1), lambda h, i: (h, i, 0))],
        out_specs=[pl.BlockSpec((None, tq, dq), lambda h, i: (h, i, 0)),
                   pl.BlockSpec((None, s_n, dq), lambda h, i: (h, 0, 0)),
                   pl.BlockSpec((None, s_n, dv), lambda h, i: (h, 0, 0))],
        out_shape=[jax.ShapeDtypeStruct((h_n, s_n, dq), F32), jax.ShapeDtypeStruct((h_n, s_n, dq), F32),
                   jax.ShapeDtypeStruct((h_n, s_n, dv), F32)],
        compiler_params=_cparams(), name=name)(q, k, v, o, do, lse)


def _head_sum(x, *, name):
    h_n, s_n, w = x.shape
    tm = _pick(s_n, 256, 8)

    def body(x_ref, o_ref):
        o_ref[...] = jnp.sum(x_ref[...], axis=0)

    return pl.pallas_call(body, grid=(s_n // tm,), in_specs=[pl.BlockSpec((h_n, tm, w), lambda i: (0, i, 0))],
                          out_specs=_rows(tm, w), out_shape=jax.ShapeDtypeStruct((s_n, w), F32),
                          compiler_params=_cparams(), name=name)(x)


N_BLK = SEQ // DIL_BLOCK
DIL_SCALE = 64 ** -0.5


def _dil_masks():
    iq = lax.broadcasted_iota(jnp.int32, (DIL_BLOCK, 2 * DIL_BLOCK), 0)
    ik = lax.broadcasted_iota(jnp.int32, (DIL_BLOCK, 2 * DIL_BLOCK), 1)
    rel = DIL_BLOCK + iq - ik
    both = (rel >= 0) & (rel <= DIL_BLOCK)
    iq1 = lax.broadcasted_iota(jnp.int32, (DIL_BLOCK, DIL_BLOCK), 0)
    ik1 = lax.broadcasted_iota(jnp.int32, (DIL_BLOCK, DIL_BLOCK), 1)
    return both, ik1 <= iq1


def _dil_block(j, nb):
    lo = j * DIL_BLOCK
    first = j % nb == 0
    k_lo = lo if first else lo - DIL_BLOCK
    b_lo = DIL_BLOCK if first else 0
    return lo, k_lo, b_lo, first


def _dil_attn_fwd(q, k, v, bias, nb, *, name):
    h_n, s_n, e = q.shape

    def body(q_ref, k_ref, v_ref, b_ref, o_ref, lse_ref):
        m_both, m_first = _dil_masks()
        for j in range(N_BLK):
            lo, k_lo, b_lo, first = _dil_block(j, nb)
            qj = q_ref[lo:lo + DIL_BLOCK, :]
            kk = k_ref[k_lo:lo + DIL_BLOCK, :]
            vv = v_ref[k_lo:lo + DIL_BLOCK, :]
            s = lax.dot_general(qj, kk, _NT, preferred_element_type=F32) * DIL_SCALE + b_ref[:, b_lo:]
            s = jnp.where(m_first if first else m_both, s, -jnp.inf)
            m = jnp.max(s, axis=-1, keepdims=True)
            lse = m + jnp.log(jnp.sum(jnp.exp(s - m), axis=-1, keepdims=True))
            p = jnp.exp(s - lse)
            o_ref[lo:lo + DIL_BLOCK, :] = jnp.dot(p.astype(BF16), vv, preferred_element_type=F32)
            lse_ref[lo:lo + DIL_BLOCK, :] = lse

    head = lambda w: pl.BlockSpec((None, s_n, w), lambda h: (h, 0, 0))
    return pl.pallas_call(
        body, grid=(h_n,),
        in_specs=[head(e), head(e), head(e), pl.BlockSpec((None, DIL_BLOCK, 2 * DIL_BLOCK), lambda h: (h, 0, 0))],
        out_specs=[head(e), head(1)],
        out_shape=[jax.ShapeDtypeStruct((h_n, s_n, e), F32), jax.ShapeDtypeStruct((h_n, s_n, 1), F32)],
        compiler_params=_cparams(), name=name)(q, k, v, bias)


def _dil_attn_bwd(q, k, v, bias, lse, do, dlt, nb, *, name):
    h_n, s_n, e = q.shape

    def body(q_ref, k_ref, v_ref, b_ref, lse_ref, do_ref, dlt_ref, dq_ref, dk_ref, dv_ref, db_ref):
        dk_ref[...] = jnp.zeros_like(dk_ref)
        dv_ref[...] = jnp.zeros_like(dv_ref)
        db_ref[...] = jnp.zeros_like(db_ref)
        m_both, m_first = _dil_masks()
        for j in range(N_BLK):
            lo, k_lo, b_lo, first = _dil_block(j, nb)
            qj = q_ref[lo:lo + DIL_BLOCK, :]
            kk = k_ref[k_lo:lo + DIL_BLOCK, :]
            vv = v_ref[k_lo:lo + DIL_BLOCK, :]
            s = lax.dot_general(qj, kk, _NT, preferred_element_type=F32) * DIL_SCALE + b_ref[:, b_lo:]
            p = jnp.where(m_first if first else m_both, jnp.exp(s - lse_ref[lo:lo + DIL_BLOCK, :]), 0.0)
            dob = do_ref[lo:lo + DIL_BLOCK, :].astype(BF16)
            dv_ref[k_lo:lo + DIL_BLOCK, :] += lax.dot_general(p.astype(BF16), dob, _TN, preferred_element_type=F32)
            dp = lax.dot_general(dob, vv, _NT, preferred_element_type=F32)
            ds = p * (dp - dlt_ref[lo:lo + DIL_BLOCK, :])
            db_ref[:, b_lo:] += ds
            dsb = (ds * DIL_SCALE).astype(BF16)
            dq_ref[lo:lo + DIL_BLOCK, :] = jnp.dot(dsb, kk, preferred_element_type=F32)
            dk_ref[k_lo:lo + DIL_BLOCK, :] += lax.dot_general(dsb, qj, _TN, preferred_element_type=F32)

    head = lambda w: pl.BlockSpec((None, s_n, w), lambda h: (h, 0, 0))
    b_spec = pl.BlockSpec((None, DIL_BLOCK, 2 * DIL_BLOCK), lambda h: (h, 0, 0))
    sd = jax.ShapeDtypeStruct((h_n, s_n, e), F32)
    return pl.pallas_call(
        body, grid=(h_n,),
        in_specs=[head(e), head(e), head(e), b_spec, head(1), head(e), head(1)],
        out_specs=[head(e), head(e), head(e), b_spec],
        out_shape=[sd, sd, sd, jax.ShapeDtypeStruct((h_n, DIL_BLOCK, 2 * DIL_BLOCK), F32)],
        compiler_params=_cparams(), name=name)(q, k, v, bias, lse, do, dlt)


def _group_alpha(l_refs):
    ls = [r[...] for r in l_refs]
    m = jnp.maximum(jnp.maximum(ls[0], ls[1]), ls[2])
    es = [jnp.exp(l - m) for l in ls]
    tot = es[0] + es[1] + es[2]
    return [ex / tot for ex in es]


def _dil_mix_fwd(os_, ls_, *, name):
    h_n, s_n, e = os_[0].shape
    tm = 512

    def body(o0, o1, o2, l0, l1, l2, out_ref):
        al = _group_alpha((l0, l1, l2))
        out_ref[...] = al[0] * o0[...] + al[1] * o1[...] + al[2] * o2[...]

    blk = lambda w: pl.BlockSpec((None, tm, w), lambda h, i: (h, i, 0))
    return pl.pallas_call(body, grid=(h_n, s_n // tm), in_specs=[blk(e)] * 3 + [blk(1)] * 3, out_specs=blk(e),
                          out_shape=jax.ShapeDtypeStruct((h_n, s_n, e), F32), compiler_params=_cparams(),
                          name=name)(*os_, *ls_)


def _dil_mix_bwd(do, os_, ls_, *, name):
    h_n, s_n, e = do.shape
    tm = 512

    def body(do_ref, o0, o1, o2, l0, l1, l2, d0, d1, d2, t0, t1, t2):
        al = _group_alpha((l0, l1, l2))
        do_v = do_ref[...]
        mix = al[0] * o0[...] + al[1] * o1[...] + al[2] * o2[...]
        dbar = jnp.sum(do_v * mix, axis=-1, keepdims=True)
        for a_g, d_ref, t_ref in zip(al, (d0, d1, d2), (t0, t1, t2)):
            d_ref[...] = a_g * do_v
            t_ref[...] = a_g * dbar

    blk = lambda w: pl.BlockSpec((None, tm, w), lambda h, i: (h, i, 0))
    sd_e = jax.ShapeDtypeStruct((h_n, s_n, e), F32)
    sd_1 = jax.ShapeDtypeStruct((h_n, s_n, 1), F32)
    outs = pl.pallas_call(body, grid=(h_n, s_n // tm), in_specs=[blk(e)] * 4 + [blk(1)] * 3,
                          out_specs=[blk(e)] * 3 + [blk(1)] * 3, out_shape=[sd_e] * 3 + [sd_1] * 3,
                          compiler_params=_cparams(), name=name)(do, *os_, *ls_)
    return outs[:3], outs[3:]


def _bias_grad(ds, bucket, *, name):
    h_n = ds.shape[0]

    def body(ds_ref, bk_ref, o_ref):
        ds_v = ds_ref[...]
        bk = bk_ref[...]
        lane = lax.broadcasted_iota(jnp.int32, (1, N_BUCKETS), 1)
        acc = jnp.zeros((1, N_BUCKETS), F32)
        for b in range(N_BUCKETS):
            tot = jnp.sum(jnp.sum(jnp.where(bk == b, ds_v, 0.0), axis=1, keepdims=True), axis=0, keepdims=True)
            acc = acc + jnp.where(lane == b, tot, 0.0)
        o_ref[...] = acc

    return pl.pallas_call(
        body, grid=(h_n,),
        in_specs=[pl.BlockSpec((None, DIL_BLOCK, 2 * DIL_BLOCK), lambda h: (h, 0, 0)),
                  pl.BlockSpec((DIL_BLOCK, 2 * DIL_BLOCK), lambda h: (0, 0))],
        out_specs=pl.BlockSpec((None, 1, N_BUCKETS), lambda h: (h, 0, 0)),
        out_shape=jax.ShapeDtypeStruct((h_n, 1, N_BUCKETS), F32), compiler_params=_cparams(), name=name)(ds, bucket)


def _bias_table(rb, bucket, *, name):
    h_n = rb.shape[0]

    def body(rb_ref, bk_ref, o_ref):
        bk = bk_ref[...]
        row = rb_ref[...]
        acc = jnp.zeros(bk.shape, F32)
        for b in range(N_BUCKETS):
            acc = jnp.where(bk == b, row[:, b:b + 1], acc)
        o_ref[...] = acc

    return pl.pallas_call(
        body, grid=(h_n,),
        in_specs=[pl.BlockSpec((None, 1, N_BUCKETS), lambda h: (h, 0, 0)),
                  pl.BlockSpec((DIL_BLOCK, 2 * DIL_BLOCK), lambda h: (0, 0))],
        out_specs=pl.BlockSpec((None, DIL_BLOCK, 2 * DIL_BLOCK), lambda h: (h, 0, 0)),
        out_shape=jax.ShapeDtypeStruct((h_n, DIL_BLOCK, 2 * DIL_BLOCK), F32), compiler_params=_cparams(),
        name=name)(rb, bucket)


def _row_tile(rows, cols, budget=1 << 20):
    if rows * cols * 4 <= budget or rows % 8:
        return rows
    best = 8
    for t in range(8, rows + 1, 8):
        if rows % t == 0 and t * cols * 4 <= budget:
            best = t
    return best


def _adamw(w, g, m, v, *, name):
    shape = w.shape
    cols = shape[-1]
    rows = math.prod(shape[:-1]) if len(shape) > 1 else 1
    to2 = lambda t: t.reshape(rows, cols)
    tr = _row_tile(rows, cols)
    c1 = 1.0 / (1.0 - ADAM_B1 ** ADAM_STEP)
    c2 = 1.0 / (1.0 - ADAM_B2 ** ADAM_STEP)

    def body(w_ref, g_ref, m_ref, v_ref, d_ref, nm_ref, nv_ref):
        g_v = g_ref[...]
        nm = ADAM_B1 * m_ref[...] + (1.0 - ADAM_B1) * g_v
        nv = ADAM_B2 * v_ref[...] + (1.0 - ADAM_B2) * (g_v * g_v)
        m_hat = nm * c1
        v_hat = nv * c2
        d_ref[...] = -ADAM_LR * (m_hat / (jnp.sqrt(v_hat) + ADAM_EPS) + ADAM_WD * w_ref[...])
        nm_ref[...] = nm
        nv_ref[...] = nv

    blk = pl.BlockSpec((tr, cols), lambda i: (i, 0))
    sd = jax.ShapeDtypeStruct((rows, cols), F32)
    outs = pl.pallas_call(body, grid=(rows // tr,), in_specs=[blk] * 4, out_specs=[blk] * 3, out_shape=[sd] * 3,
                          compiler_params=_cparams(), name=name)(to2(w), to2(g), to2(m), to2(v))
    return tuple(t.reshape(shape) for t in outs)


def _add_half(unit, got, half_idx, *, name):
    rest = unit.shape[2:]
    c = rest[-1]
    r = math.prod(rest[:-1])
    tr = _row_tile(r, c)

    def body(idx_ref, u_ref, g_ref, o_ref, w_ref):
        tot = u_ref[...] + g_ref[...].astype(F32)
        o_ref[...] = tot
        w_ref[...] = tot.astype(BF16)

    blk = pl.BlockSpec((None, tr, c), lambda s, i, idx: (s, i, 0))
    grid_spec = pltpu.PrefetchScalarGridSpec(
        num_scalar_prefetch=1, grid=(N_SHARD, r // tr),
        in_specs=[pl.BlockSpec((None, None, tr, c), lambda s, i, idx: (idx[0], s, i, 0)), blk],
        out_specs=[blk, blk])
    out, wire = pl.pallas_call(
        body, grid_spec=grid_spec,
        out_shape=[jax.ShapeDtypeStruct((N_SHARD, r, c), F32), jax.ShapeDtypeStruct((N_SHARD, r, c), BF16)],
        compiler_params=_cparams(), name=name)(half_idx, unit.reshape(2, N_SHARD, r, c), got.reshape(N_SHARD, r, c))
    return out.reshape((N_SHARD,) + rest), wire.reshape((N_SHARD,) + rest)


def _add_shards(part, got, shard_idx, *, name):
    rest = part.shape[1:]
    c = rest[-1]
    r = math.prod(rest[:-1])
    tr = _row_tile(r, c)

    def body(idx_ref, p_ref, g_ref, o_ref):
        acc = p_ref[...]
        for k in range(3):
            acc = acc + g_ref[k].astype(F32)
        o_ref[...] = acc

    grid_spec = pltpu.PrefetchScalarGridSpec(
        num_scalar_prefetch=1, grid=(r // tr,),
        in_specs=[pl.BlockSpec((None, tr, c), lambda i, idx: (idx[0], i, 0)),
                  pl.BlockSpec((3, tr, c), lambda i, idx: (0, i, 0))],
        out_specs=pl.BlockSpec((tr, c), lambda i, idx: (i, 0)))
    out = pl.pallas_call(body, grid_spec=grid_spec, out_shape=jax.ShapeDtypeStruct((r, c), F32),
                         compiler_params=_cparams(), name=name)(
        shard_idx, part.reshape(N_SHARD, r, c), got.reshape(3, r, c))
    return out.reshape(rest)


def _sum_devices(x, n_dev, *, name):
    rows = x.shape[0] // n_dev

    def body(x_ref, o_ref):
        acc = x_ref[0:rows, :]
        for d in range(1, n_dev):
            acc = acc + x_ref[d * rows:(d + 1) * rows, :]
        o_ref[...] = acc

    return pl.pallas_call(body, out_shape=jax.ShapeDtypeStruct((rows, x.shape[1]), F32), name=name)(x)


def _my_pos():
    return lax.axis_index("x"), lax.axis_index("y"), lax.axis_index("c")


def _all_gather(x_blk, *, name, in_vmem):
    m_per, n = x_blk.shape

    def body(x_ref, out_ref, send_sems, recv_sems, local_sem):
        x, y, c = _my_pos()
        me, sibling = (x, y, c), (x, y, 1 - c)
        chips = [(1 - x, y), (x, 1 - y), (1 - x, 1 - y)]

        def rows(px, py, pc):
            return out_ref.at[pl.ds((4 * px + 2 * py + pc) * m_per, m_per), :]

        def copy(k, block, to, src=None):
            return pltpu.make_async_remote_copy(
                src_ref=rows(*block) if src is None else src, dst_ref=rows(*block),
                send_sem=send_sems.at[k], recv_sem=recv_sems.at[k], device_id=to, device_id_type=MESH)

        mine = pltpu.make_async_copy(x_ref, rows(*me), local_sem)
        mine.start()
        first = [copy(0, me, sibling, src=x_ref)]
        first += [copy(1 + j, me, (*chip, c), src=x_ref) for j, chip in enumerate(chips)]
        for cp in first:
            cp.start()
        passed = [copy(4 + j, (*chip, c), sibling) for j, chip in enumerate(chips)]
        for j, chip in enumerate(chips):
            copy(1 + j, (*chip, c), me).wait_recv()
            passed[j].start()
        copy(0, sibling, me).wait_recv()
        for j, chip in enumerate(chips):
            copy(4 + j, (*chip, 1 - c), me).wait_recv()
        for cp in first + passed:
            cp.wait_send()
        mine.wait()

    space = pltpu.VMEM if in_vmem else pl.ANY
    return pl.pallas_call(
        body, out_shape=jax.ShapeDtypeStruct((8 * m_per, n), x_blk.dtype),
        in_specs=[pl.BlockSpec(memory_space=space)], out_specs=pl.BlockSpec(memory_space=space),
        scratch_shapes=[pltpu.SemaphoreType.DMA((7,)), pltpu.SemaphoreType.DMA((7,)), pltpu.SemaphoreType.DMA],
        name=name)(x_blk)


_HBM = pl.BlockSpec(memory_space=pl.ANY)


def _gather_weights(fams, *, name):
    n = len(fams)

    def body(*refs):
        ins, outs = refs[:n], refs[n:2 * n]
        send_sems, recv_sems = refs[2 * n:]
        x, y, c = _my_pos()
        me, sibling = (x, y, c), (x, y, 1 - c)
        chips = [(1 - x, y), (x, 1 - y), (1 - x, 1 - y)]

        def copy(f, k, block, to, src=None):
            px, py, pc = block
            dst = outs[f].at[2 * px + py, pc]
            return pltpu.make_async_remote_copy(
                src_ref=dst if src is None else src, dst_ref=dst, send_sem=send_sems.at[7 * f + k],
                recv_sem=recv_sems.at[7 * f + k], device_id=to, device_id_type=MESH)

        first, passed = [], []
        for f in range(n):
            src = ins[f].at[c]
            first.append(copy(f, 0, me, sibling, src=src))
            first += [copy(f, 1 + j, me, (*chip, c), src=src) for j, chip in enumerate(chips)]
        for cp in first:
            cp.start()
        for j, chip in enumerate(chips):
            for f in range(n):
                copy(f, 1 + j, (*chip, c), me).wait_recv()
                passed.append(copy(f, 4 + j, (*chip, c), sibling))
                passed[-1].start()
        for f in range(n):
            copy(f, 0, sibling, me).wait_recv()
        for j, chip in enumerate(chips):
            for f in range(n):
                copy(f, 4 + j, (*chip, 1 - c), me).wait_recv()
        for cp in first + passed:
            cp.wait_send()

    outs = pl.pallas_call(
        body, out_shape=[jax.ShapeDtypeStruct((N_SHARD,) + t.shape, t.dtype) for t in fams],
        in_specs=[_HBM] * n, out_specs=[_HBM] * n,
        scratch_shapes=[pltpu.SemaphoreType.DMA((7 * n,)), pltpu.SemaphoreType.DMA((7 * n,))], name=name)(*fams)
    return [_place_own(o, t) for o, t in zip(outs, fams)]


def _swap_halves(units, *, name):
    n = len(units)

    def body(*refs):
        ins, outs = refs[:n], refs[n:2 * n]
        send_sems, recv_sems = refs[2 * n:]
        x, y, c = _my_pos()
        cps = [pltpu.make_async_remote_copy(src_ref=ins[f].at[1 - c], dst_ref=outs[f], send_sem=send_sems.at[f],
                                            recv_sem=recv_sems.at[f], device_id=(x, y, 1 - c), device_id_type=MESH)
               for f in range(n)]
        for cp in cps:
            cp.start()
        for cp in cps:
            cp.wait()

    return pl.pallas_call(
        body, out_shape=[jax.ShapeDtypeStruct(t.shape[1:], t.dtype) for t in units],
        in_specs=[_HBM] * n, out_specs=[_HBM] * n,
        scratch_shapes=[pltpu.SemaphoreType.DMA((n,)), pltpu.SemaphoreType.DMA((n,))], name=name)(*units)


def _send_to_chips(parts, *, name):
    n = len(parts)

    def body(*refs):
        ins, outs = refs[:n], refs[n:2 * n]
        send_sems, recv_sems = refs[2 * n:]
        x, y, c = _my_pos()
        chips = [(1 - x, y), (x, 1 - y), (1 - x, 1 - y)]
        cps = [pltpu.make_async_remote_copy(src_ref=ins[f].at[2 * cx + cy], dst_ref=outs[f].at[k],
                                            send_sem=send_sems.at[3 * f + k], recv_sem=recv_sems.at[3 * f + k],
                                            device_id=(cx, cy, c), device_id_type=MESH)
               for f in range(n) for k, (cx, cy) in enumerate(chips)]
        for cp in cps:
            cp.start()
        for cp in cps:
            cp.wait()

    return pl.pallas_call(
        body, out_shape=[jax.ShapeDtypeStruct((3,) + t.shape[1:], t.dtype) for t in parts],
        in_specs=[_HBM] * n, out_specs=[_HBM] * n,
        scratch_shapes=[pltpu.SemaphoreType.DMA((3 * n,)), pltpu.SemaphoreType.DMA((3 * n,))], name=name)(*parts)


def _pair_gather(halves, *, name):
    n = len(halves)

    def body(*refs):
        ins, outs = refs[:n], refs[n:2 * n]
        send_sems, recv_sems = refs[2 * n:]
        x, y, c = _my_pos()
        cps = [pltpu.make_async_remote_copy(src_ref=ins[f], dst_ref=outs[f].at[c], send_sem=send_sems.at[f],
                                            recv_sem=recv_sems.at[f], device_id=(x, y, 1 - c), device_id_type=MESH)
               for f in range(n)]
        for cp in cps:
            cp.start()
        for f in range(n):
            pltpu.make_async_remote_copy(src_ref=ins[f], dst_ref=outs[f].at[1 - c], send_sem=send_sems.at[f],
                                         recv_sem=recv_sems.at[f], device_id=(x, y, 1 - c),
                                         device_id_type=MESH).wait_recv()
        for cp in cps:
            cp.wait_send()

    outs = pl.pallas_call(
        body, out_shape=[jax.ShapeDtypeStruct((2,) + t.shape, t.dtype) for t in halves],
        in_specs=[_HBM] * n, out_specs=[_HBM] * n,
        scratch_shapes=[pltpu.SemaphoreType.DMA((n,)), pltpu.SemaphoreType.DMA((n,))], name=name)(*halves)
    c = lax.axis_index("c")
    return [lax.dynamic_update_index_in_dim(o, t, c, 0) for o, t in zip(outs, halves)]


_HBM_ONLY = pl.BlockSpec(memory_space=pltpu.HBM)
_SEMS = pl.BlockSpec(memory_space=pltpu.SEMAPHORE)
_EFFECT = pltpu.SideEffectType.DATAFLOW_SIDE_EFFECTING


def _copies_start(srcs, lands, plan, n_copies, *, name):
    n, m = len(srcs), len(lands)

    def body(*refs):
        src_refs, land_refs = refs[:n], refs[n:n + m]
        send_sems, recv_sems, token = refs[n + m], refs[n + m + 1], refs[-1]
        for k, (src, dst, peer) in enumerate(plan(src_refs, land_refs)):
            pltpu.make_async_remote_copy(src_ref=src, dst_ref=dst, send_sem=send_sems.at[k], recv_sem=recv_sems.at[k],
                                         device_id=peer, device_id_type=MESH).start()
        token[...] = jnp.zeros_like(token)

    bufs = [pltpu.with_memory_space_constraint(t, pltpu.HBM) for t in (*srcs, *lands)]
    outs = pl.pallas_call(
        body, name=name,
        out_shape=(pltpu.SemaphoreType.DMA((n_copies,)), pltpu.SemaphoreType.DMA((n_copies,)),
                   *[pltpu.HBM(t.shape, t.dtype) for t in bufs], jax.ShapeDtypeStruct((8, 128), F32)),
        in_specs=[_HBM_ONLY] * (n + m),
        out_specs=(_SEMS, _SEMS, *[_HBM_ONLY] * (n + m), pl.BlockSpec(memory_space=pltpu.VMEM)),
        input_output_aliases={k: 2 + k for k in range(n + m)},
        compiler_params=pltpu.CompilerParams(has_side_effects=_EFFECT))(*bufs)
    return outs[0], outs[1], list(outs[2:2 + n + m]), outs[-1]


def _copies_wait(send_sems, recv_sems, thru, n_src, plan, after, *, name):
    nm = len(thru)

    def body(*refs):
        t_refs, send, recv = refs[:nm], refs[nm], refs[nm + 1]
        for k, (src, dst, peer) in enumerate(plan(t_refs[:n_src], t_refs[n_src:])):
            cp = pltpu.make_async_remote_copy(src_ref=src, dst_ref=dst, send_sem=send.at[k], recv_sem=recv.at[k],
                                              device_id=peer, device_id_type=MESH)
            cp.wait_send()
            cp.wait_recv()

    outs = pl.pallas_call(
        body, name=name, out_shape=tuple(pltpu.HBM(t.shape, t.dtype) for t in thru),
        in_specs=[_HBM_ONLY] * nm + [_SEMS, _SEMS, pl.BlockSpec(memory_space=pl.ANY)],
        out_specs=tuple([_HBM_ONLY] * nm), input_output_aliases={k: k for k in range(nm)},
        compiler_params=pltpu.CompilerParams(has_side_effects=_EFFECT))(*thru, send_sems, recv_sems, after)
    return list(outs)


_RELATIONS = [(dx, dy, dc) for dx in (0, 1) for dy in (0, 1) for dc in (0, 1)][1:]


def _gather_plan(src_refs, land_refs):
    x, y, c = _my_pos()
    flip = lambda v, d: 1 - v if d else v
    return [(s_ref.at[c], l_ref.at[2 * x + y, c], (flip(x, dx), flip(y, dy), flip(c, dc)))
            for s_ref, l_ref in zip(src_refs, land_refs) for dx, dy, dc in _RELATIONS]


def _sibling_plan(src_refs, land_refs):
    x, y, c = _my_pos()
    return [(s_ref.at[1 - c], l_ref, (x, y, 1 - c)) for s_ref, l_ref in zip(src_refs, land_refs)]


def _chips_plan(src_refs, land_refs):
    x, y, c = _my_pos()
    chips = [(1 - x, y), (x, 1 - y), (1 - x, 1 - y)]
    return [(s_ref.at[2 * cx + cy], l_ref.at[k], (cx, cy, c))
            for s_ref, l_ref in zip(src_refs, land_refs) for k, (cx, cy) in enumerate(chips)]


def _place_own(gathered, fam):
    x, y, c = _my_pos()
    own = lax.dynamic_index_in_dim(fam, c, 0, keepdims=True)[None]
    return lax.dynamic_update_slice(gathered, own, (2 * x + y, c) + (0,) * (fam.ndim - 1))


def _to_heads(t, width):
    return t.reshape(t.shape[0], HEADS, width).transpose(1, 0, 2)


def _from_heads(t):
    return t.transpose(1, 0, 2).reshape(t.shape[1], -1)


def _residue_major(t, d):
    h_n, s_n, e = t.shape
    return t.reshape(h_n, s_n // d, d, e).transpose(0, 2, 1, 3).reshape(h_n, s_n, e)


def _token_major(t, d):
    h_n, s_n, e = t.shape
    return t.reshape(h_n, d, s_n // d, e).transpose(0, 2, 1, 3).reshape(h_n, s_n, e)


def _t5_bucket(dist):
    max_exact = N_BUCKETS // 2
    d = jnp.maximum(dist, 1).astype(F32)
    large = max_exact + (jnp.log(d / max_exact) / math.log(MAX_DISTANCE / max_exact)
                         * (N_BUCKETS - max_exact)).astype(jnp.int32)
    large = jnp.minimum(large, N_BUCKETS - 1)
    return jnp.where(dist < max_exact, dist, large)


def _bucket_map(dilation):
    iq = jnp.arange(DIL_BLOCK)[:, None]
    ik = jnp.arange(2 * DIL_BLOCK)[None, :]
    rel = DIL_BLOCK + iq - ik
    return _t5_bucket(jnp.maximum(rel, 0) * dilation).astype(jnp.int32)


def _q_perm(w):
    w3 = w.reshape(w.shape[0], HEADS, QK_NOPE + QK_ROPE)
    return jnp.concatenate([w3[:, :, :QK_NOPE].reshape(w.shape[0], -1),
                            w3[:, :, QK_NOPE:QK_NOPE + HALF_ROPE].reshape(w.shape[0], -1),
                            w3[:, :, QK_NOPE + HALF_ROPE:].reshape(w.shape[0], -1)], axis=1)


def _q_unperm(w):
    n0, n1 = HEADS * QK_NOPE, HEADS * HALF_ROPE
    r = w.shape[0]
    return jnp.concatenate([w[:, :n0].reshape(r, HEADS, QK_NOPE), w[:, n0:n0 + n1].reshape(r, HEADS, HALF_ROPE),
                            w[:, n0 + n1:].reshape(r, HEADS, HALF_ROPE)], axis=2).reshape(r, -1)


def _kv_perm(w):
    w3 = w.reshape(w.shape[0], HEADS, QK_NOPE + V_HEAD)
    return jnp.concatenate([w3[:, :, :QK_NOPE].reshape(w.shape[0], -1), w3[:, :, QK_NOPE:].reshape(w.shape[0], -1)],
                           axis=1)


def _kv_unperm(w):
    n0 = HEADS * QK_NOPE
    r = w.shape[0]
    return jnp.concatenate([w[:, :n0].reshape(r, HEADS, QK_NOPE), w[:, n0:].reshape(r, HEADS, V_HEAD)],
                           axis=2).reshape(r, -1)


def _row(v):
    return v.reshape(1, -1)


def kernel(x, c, norm_pre, norm_post, w_mod, b_mod, ffn_w_gate, ffn_w_up, ffn_w_down, mla_w_in, mla_q_norm, mla_w_q_up, mla_kv_norm, mla_w_kv_up, mla_w_o, dil_w_in, dil_w_o, rel_bias, loss_target, m_norm_pre, m_norm_post, m_w_mod, m_b_mod, m_ffn_w_gate, m_ffn_w_up, m_ffn_w_down, m_mla_w_in, m_mla_q_norm, m_mla_w_q_up, m_mla_kv_norm, m_mla_w_kv_up, m_mla_w_o, m_dil_w_in, m_dil_w_o, m_rel_bias, v_norm_pre, v_norm_post, v_w_mod, v_b_mod, v_ffn_w_gate, v_ffn_w_up, v_ffn_w_down, v_mla_w_in, v_mla_q_norm, v_mla_w_q_up, v_mla_kv_norm, v_mla_w_kv_up, v_mla_w_o, v_dil_w_in, v_dil_w_o, v_rel_bias):
    given = dict(locals())
    ix, iy, ic = _my_pos()
    shard_id = 2 * ix + iy
    dev_id = 4 * ix + 2 * iy + ic
    x2 = x[0]
    target = loss_target[0]
    half_idx = jnp.reshape(ic, (1,)).astype(jnp.int32)
    shard_idx = jnp.reshape(shard_id, (1,)).astype(jnp.int32)

    blk = jnp.zeros((8, D_MODEL), F32)
    blk = blk.at[0].set(c[0])
    blk = blk.at[1:3].set(jnp.pad(norm_pre.reshape(-1), (0, 512)).reshape(2, D_MODEL))
    blk = blk.at[3:5].set(jnp.pad(norm_post.reshape(-1), (0, 512)).reshape(2, D_MODEL))
    got = _all_gather(blk, name="ag_c_norms", in_vmem=True).reshape(N_SHARD, 2, 8, D_MODEL)
    c_all = got[:, :, 0, :].reshape(8, D_MODEL)

    def full_norm(lo):
        t = got[:, 0, lo:lo + 2, :].reshape(N_SHARD, 2 * D_MODEL)[:, :1536].reshape(N_SHARD, 2, 3, 256)
        return t.transpose(1, 2, 0, 3).reshape(2, 3, D_MODEL)

    pre_full, post_full = full_norm(1), full_norm(3)

    silu_c = _silu_bf16(c_all, name="silu_c")
    b_cols = lax.dynamic_slice_in_dim(b_mod, shard_id * 2304, 2304, axis=1).reshape(2, 1, 2304)
    mod_part = _mm(silu_c, w_mod, bias=b_cols, name="mod_mm", tn_cap=768)
    mod_all = _all_gather(mod_part.reshape(16, 2304), name="ag_mod", in_vmem=True)
    mod_all = mod_all.reshape(N_SHARD, 2, 2, 8, 2304)[:, 0]
    mod_mine = lax.dynamic_index_in_dim(mod_all, dev_id, axis=2, keepdims=False)
    mod = mod_mine.transpose(1, 0, 2).reshape(2, 9, D_MODEL)

    bf = lambda t: t.astype(BF16)
    ffn_fam = lambda i, h: [bf(jnp.stack([ffn_w_gate[i, h], ffn_w_up[i, h]])),
                            bf(ffn_w_down[i, h].reshape(2, F_SHARD // 2, D_MODEL))]
    mla_fam = [bf(mla_w_in.reshape(2, 128, -1)), bf(mla_w_q_up.reshape(2, 192, -1)),
               bf(mla_w_kv_up.reshape(2, 128, -1)), bf(mla_w_o.reshape(2, 128, D_MODEL))]
    dil_fam = [bf(dil_w_in.reshape(2, 512, -1)), bf(dil_w_o.reshape(2, 128, D_MODEL))]
    later_fams = [ffn_fam(0, 1), ffn_fam(1, 0) + dil_fam, ffn_fam(1, 1)]
    full, later_fams, mod = lax.optimization_barrier(
        (_gather_weights(ffn_fam(0, 0) + mla_fam, name="ag_weights_first"), later_fams, mod))

    def gather_later(fams, tag):
        lands = [lax.empty((N_SHARD,) + t.shape, t.dtype) for t in fams]
        send, recv, thru, token = _copies_start(fams, lands, _gather_plan, 7 * len(fams), name=f"ag_start_{tag}")
        return dict(send=send, recv=recv, thru=thru, token=token, n=len(fams), tag=tag)

    def arrive(st, after):
        thru = _copies_wait(st['send'], st['recv'], st['thru'], st['n'], _gather_plan, after,
                            name=f"ag_wait_{st['tag']}")
        return [_place_own(o, t) for t, o in zip(thru[:st['n']], thru[st['n']:])]

    in_flight = [gather_later(fams, tag) for fams, tag in zip(later_fams, ("l0s2", "l1s01", "l1s2"))]
    started = sum(st['token'][0, 0] for st in in_flight)
    ffn_w = {(0, 0): (full[0], full[1].reshape(N_SHARD, F_SHARD, D_MODEL))}
    w_in = full[2].reshape(D_MODEL, -1)
    wq_p = _q_perm(full[3].reshape(N_SHARD, Q_LORA, -1).transpose(1, 0, 2).reshape(Q_LORA, -1))
    wkv_p = _kv_perm(full[4].reshape(N_SHARD, KV_LORA, -1).transpose(1, 0, 2).reshape(KV_LORA, -1))
    w_mo = full[5].reshape(D_MODEL, D_MODEL)
    dil_w = {}

    pos = jnp.arange(SEQ, dtype=F32)
    freqs = ROPE_THETA ** (-jnp.arange(HALF_ROPE, dtype=F32) / HALF_ROPE)
    ang = pos[:, None] * freqs[None, :]
    cos_k, sin_k = jnp.cos(ang), jnp.sin(ang)
    cos_q, sin_q = jnp.tile(cos_k, (1, HEADS)), jnp.tile(sin_k, (1, HEADS))

    buckets = [_bucket_map(d) for _, d in DIL_GROUPS]
    biases = [_bias_table(rel_bias[:, g * HEADS:(g + 1) * HEADS].T.reshape(HEADS, 1, N_BUCKETS), bk,
                          name=f"dil_bias_table_g{g}") for g, bk in enumerate(buckets)]

    def sub_params(i, sub):
        return dict(pg=_row(pre_full[i, sub]), qg=_row(post_full[i, sub]), sh=_row(mod[i, 3 * sub]),
                    sc=_row(mod[i, 3 * sub + 1]), gate=_row(mod[i, 3 * sub + 2]))

    def ffn_fwd(xin, i, h, sub, tie=None):
        p = sub_params(i, sub)
        if tie is not None:
            p['sh'] = p['sh'] + tie
        tag = f"l{i}s{sub}"
        w_gu, w_dn = ffn_w[i, h]
        hn = _pre_fwd(xin, p['pg'], p['sc'], p['sh'], name=f"pre_fwd_{tag}")
        gu, a = _ffn_up(hn, w_gu, name=f"ffn_up_{tag}")
        f = _mm(a, w_dn, reduce_g=True, name=f"ffn_down_{tag}")
        out = _post_fwd(f, xin, p['qg'], p['gate'], FFN_RES, name=f"post_fwd_{tag}")
        return out, dict(x=xin, hn=hn, gu=gu, a=a, f=f, p=p, i=i, h=h, tag=tag)

    def mla_fwd(xin, i, sub):
        p = sub_params(i, sub)
        tag = f"l{i}s{sub}"
        hn = _pre_fwd(xin, p['pg'], p['sc'], p['sh'], name=f"pre_fwd_{tag}")
        lat = _mm(hn, w_in, name="mla_lat")
        cq, ckv = lat[:, :Q_LORA], lat[:, Q_LORA:Q_LORA + KV_LORA]
        k1, k2 = lat[:, Q_LORA + KV_LORA:Q_LORA + KV_LORA + HALF_ROPE], lat[:, Q_LORA + KV_LORA + HALF_ROPE:]
        cqn = _rms_fwd(cq, mla_q_norm, name="mla_qnorm")
        ckvn = _rms_fwd(ckv, mla_kv_norm, name="mla_kvnorm")
        qp = _mm(cqn, wq_p, name="mla_q_up")
        kvp = _mm(ckvn, wkv_p, name="mla_kv_up")
        n0, n1 = HEADS * QK_NOPE, HEADS * HALF_ROPE
        qr1, qr2 = _rope(qp[:, n0:n0 + n1], qp[:, n0 + n1:], cos_q, sin_q, name="rope_q")
        kr1, kr2 = _rope(k1, k2, cos_k, sin_k, name="rope_k")
        q = jnp.concatenate([qp[:, :n0].reshape(SEQ, HEADS, QK_NOPE), qr1.reshape(SEQ, HEADS, HALF_ROPE),
                             qr2.reshape(SEQ, HEADS, HALF_ROPE)], axis=2).transpose(1, 0, 2).astype(BF16)
        kr = jnp.broadcast_to(jnp.concatenate([kr1, kr2], axis=1)[:, None, :], (SEQ, HEADS, QK_ROPE))
        k = jnp.concatenate([kvp[:, :n0].reshape(SEQ, HEADS, QK_NOPE), kr], axis=2).transpose(1, 0, 2).astype(BF16)
        v = _to_heads(kvp[:, n0:], V_HEAD).astype(BF16)
        o, lse = _mla_attn_fwd(q, k, v, name="mla_attn_fwd")
        o_flat = _from_heads(o).astype(BF16)
        f = _mm(o_flat, w_mo, name="mla_out")
        out = _post_fwd(f, xin, p['qg'], p['gate'], 1.0, name=f"post_fwd_{tag}")
        return out, dict(x=xin, hn=hn, cq=cq, ckv=ckv, cqn=cqn, ckvn=ckvn, q=q, k=k, v=v, o=o, lse=lse,
                         o_flat=o_flat, f=f, p=p, tag=tag)

    def dil_fwd(xin, i, sub):
        p = sub_params(i, sub)
        tag = f"l{i}s{sub}"
        hn = _pre_fwd(xin, p['pg'], p['sc'], p['sh'], name=f"pre_fwd_{tag}")
        proj = _mm(hn, dil_w['in'], out_dtype=BF16, tn_cap=768, name="dil_proj")
        heads = proj.reshape(N_SHARD, SEQ, 36, 64).transpose(0, 2, 1, 3).reshape(3, 3, HEADS, SEQ, 64)
        qkv, outs, lses = [], [], []
        for g, (window, d) in enumerate(DIL_GROUPS):
            q, k, v = (_residue_major(heads[g, t], d) for t in range(3))
            o, lse = _dil_attn_fwd(q, k, v, biases[g], SEQ // d // DIL_BLOCK, name=f"dil_attn_fwd_g{g}")
            qkv.append((q, k, v))
            outs.append(_token_major(o, d))
            lses.append(_token_major(lse, d))
        mix = _dil_mix_fwd(outs, lses, name="dil_mix_fwd")
        o_flat = _from_heads(mix).astype(BF16)
        f = _mm(o_flat, dil_w['out'], name="dil_out")
        out = _post_fwd(f, xin, p['qg'], p['gate'], 1.0, name=f"post_fwd_{tag}")
        return out, dict(x=xin, hn=hn, qkv=qkv, outs=outs, lses=lses, o_flat=o_flat, f=f, p=p, tag=tag)

    def set_ffn(i, h, w_gu, w_dn):
        ffn_w[i, h] = (w_gu, w_dn.reshape(N_SHARD, F_SHARD, D_MODEL))

    saved = [None] * 6
    xs, saved[0] = ffn_fwd(x2, 0, 0, 0, tie=started)
    xs, saved[1] = mla_fwd(xs, 0, 1)
    set_ffn(0, 1, *arrive(in_flight[0], xs))
    xs, saved[2] = ffn_fwd(xs, 0, 1, 2)
    got = arrive(in_flight[1], xs)
    set_ffn(1, 0, got[0], got[1])
    dil_w['in'], dil_w['out'] = got[2].reshape(N_SHARD, D_MODEL, -1), got[3].reshape(D_MODEL, D_MODEL)
    xs, saved[3] = ffn_fwd(xs, 1, 0, 0)
    xs, saved[4] = dil_fwd(xs, 1, 1)
    set_ffn(1, 1, *arrive(in_flight[2], xs))
    xs, saved[5] = ffn_fwd(xs, 1, 1, 2)

    dx, loss_part = _loss(xs, target, name="loss")
    loss = lax.psum(loss_part[0, 0], ("x", "y", "c"))

    dmod = [[None] * 9 for _ in range(2)]
    dpre = [[None] * 3 for _ in range(2)]
    dpost = [[None] * 3 for _ in range(2)]
    gu_shape = (2, N_SHARD, 2, D_MODEL, F_SHARD)
    dn_shape = (2, N_SHARD, F_SHARD, D_MODEL)
    bufs = [dict(gu=lax.empty(gu_shape, F32), dn=lax.empty(dn_shape, F32)) for _ in range(2)]
    row_unit = lambda g, r: ((r % 2, r // 2), 0)

    def close_sub(dhn, dout, sv, i, sub, res_dgate, res_dqg):
        p = sv['p']
        dxs, dsh, dsc, dpg = _pre_bwd(dhn, sv['x'], dout, p['pg'], p['sc'], name=f"pre_bwd_{sv['tag']}")
        dmod[i][3 * sub], dmod[i][3 * sub + 1], dmod[i][3 * sub + 2] = dsh, dsc, res_dgate
        dpre[i][sub], dpost[i][sub] = dpg, res_dqg
        return dxs

    def ffn_bwd(dout, sv, sub, tie=0.0):
        i, h, p, tag = sv['i'], sv['h'], sv['p'], sv['tag']
        w_gu, w_dn = ffn_w[i, h]
        df, dgate, dqg = _post_bwd(dout, sv['f'], p['qg'] + tie, p['gate'], FFN_RES, name=f"post_bwd_{tag}")
        bufs[i]['dn'] = _mm(sv['a'], df, ta=True, out_shape=dn_shape, out_sel=lambda g, r: ((h, g), r),
                            out_buf=bufs[i]['dn'], name=f"ffn_dwd_{tag}")
        dgu = _ffn_dgu(df, w_dn, sv['gu'], name=f"ffn_dgu_{tag}")
        dgu = dgu.reshape(2 * N_SHARD, SEQ, F_SHARD)
        bufs[i]['gu'] = _mm(sv['hn'], dgu, ta=True, out_shape=gu_shape,
                            out_sel=lambda g, r: ((h, g // 2, g % 2), r), out_buf=bufs[i]['gu'],
                            name=f"ffn_dwgu_{tag}")
        dhn = _mm(dgu, w_gu.reshape(2 * N_SHARD, D_MODEL, F_SHARD), tb=True, reduce_g=True, name=f"ffn_dhn_{tag}")
        return close_sub(dhn, dout, sv, i, sub, dgate, dqg)

    def mla_bwd(dout, sv, i, sub, tie=0.0):
        p, tag = sv['p'], sv['tag']
        df, dgate, dqg = _post_bwd(dout, sv['f'], p['qg'] + tie, p['gate'], 1.0, name=f"post_bwd_{tag}")
        u_wo = _mm(sv['o_flat'], df, ta=True, tm_cap=128, out_shape=(2, N_SHARD, 128, D_MODEL), out_sel=row_unit,
                   name="mla_dwo")
        do_flat = _mm(df, w_mo, tb=True, name="mla_do")
        do = _to_heads(do_flat, V_HEAD)
        dq, dk, dv = _mla_attn_bwd(sv['q'], sv['k'], sv['v'], sv['o'], do, sv['lse'], name="mla_attn_bwd")
        dq_t = dq.transpose(1, 0, 2)
        dqr1, dqr2 = _rope(dq_t[:, :, QK_NOPE:QK_NOPE + HALF_ROPE].reshape(SEQ, -1),
                           dq_t[:, :, QK_NOPE + HALF_ROPE:].reshape(SEQ, -1), cos_q, -sin_q, name="rope_q_bwd")
        dqp = jnp.concatenate([dq_t[:, :, :QK_NOPE].reshape(SEQ, -1), dqr1, dqr2], axis=1).astype(BF16)
        dkr = _head_sum(dk[:, :, QK_NOPE:], name="mla_dkr_sum")
        dk1, dk2 = _rope(dkr[:, :HALF_ROPE], dkr[:, HALF_ROPE:], cos_k, -sin_k, name="rope_k_bwd")
        dkvp = jnp.concatenate([_from_heads(dk[:, :, :QK_NOPE]), _from_heads(dv)], axis=1).astype(BF16)
        g_wq = _q_unperm(_mm(sv['cqn'], dqp, ta=True, name="mla_dwq"))
        g_wkv = _kv_unperm(_mm(sv['ckvn'], dkvp, ta=True, name="mla_dwkv"))
        dcqn = _mm(dqp, wq_p, tb=True, name="mla_dcqn")
        dckvn = _mm(dkvp, wkv_p, tb=True, name="mla_dckvn")
        dcq, g_qn = _rms_bwd(dcqn, sv['cq'], mla_q_norm, name="mla_qnorm_bwd")
        dckv, g_kvn = _rms_bwd(dckvn, sv['ckv'], mla_kv_norm, name="mla_kvnorm_bwd")
        dlat = jnp.concatenate([dcq, dckv, dk1, dk2], axis=1).astype(BF16)
        u_win = _mm(sv['hn'], dlat, ta=True, tm_cap=128, out_shape=(2, N_SHARD, 128, dlat.shape[1]),
                    out_sel=row_unit, name="mla_dwin")
        dhn = _mm(dlat, w_in, tb=True, name="mla_dhn")
        col_unit = lambda t: (t.reshape(t.shape[0], N_SHARD, -1).transpose(1, 0, 2)
                              .reshape(N_SHARD, 2, t.shape[0] // 2, -1).transpose(1, 0, 2, 3))
        grads = dict(units=[u_win, col_unit(g_wq), col_unit(g_wkv), u_wo], q_norm=g_qn, kv_norm=g_kvn)
        return close_sub(dhn, dout, sv, i, sub, dgate, dqg), grads

    def dil_bwd(dout, sv, i, sub):
        p, tag = sv['p'], sv['tag']
        df, dgate, dqg = _post_bwd(dout, sv['f'], p['qg'], p['gate'], 1.0, name=f"post_bwd_{tag}")
        u_wo = _mm(sv['o_flat'], df, ta=True, tm_cap=128, out_shape=(2, N_SHARD, 128, D_MODEL), out_sel=row_unit,
                   name="dil_dwo")
        do = _to_heads(_mm(df, dil_w['out'], tb=True, name="dil_do"), 64)
        dos, dlts = _dil_mix_bwd(do, sv['outs'], sv['lses'], name="dil_mix_bwd")
        pieces = []
        bias_rows = []
        for g, (window, d) in enumerate(DIL_GROUPS):
            q, k, v = sv['qkv'][g]
            dq, dk, dv, dbias = _dil_attn_bwd(q, k, v, biases[g], _residue_major(sv['lses'][g], d),
                                              _residue_major(dos[g], d), _residue_major(dlts[g], d),
                                              SEQ // d // DIL_BLOCK, name=f"dil_attn_bwd_g{g}")
            pieces += [_token_major(t, d).astype(BF16) for t in (dq, dk, dv)]
            bias_rows.append(_bias_grad(dbias, buckets[g], name=f"dil_bias_grad_g{g}")[:, 0, :])
        dheads = jnp.stack(pieces).reshape(N_SHARD, 36, SEQ, 64).transpose(0, 2, 1, 3).reshape(N_SHARD, SEQ, 2304)
        u_win = _mm(sv['hn'], dheads, ta=True, tn_cap=768, out_shape=(2, N_SHARD, 512, 2304),
                    out_sel=lambda g, r: ((r, g), 0), name="dil_dwin")
        dhn = _mm(dheads, dil_w['in'], tb=True, reduce_g=True, name="dil_dhn")
        g_bias = jnp.concatenate(bias_rows, axis=0).T
        grads = dict(units=[u_win, u_wo], rel_bias=g_bias)
        return close_sub(dhn, dout, sv, i, sub, dgate, dqg), grads

    dx = ffn_bwd(dx, saved[5], 2)
    dx, dil_g = dil_bwd(dx, saved[4], 1, 1)
    dx = ffn_bwd(dx, saved[3], 0)
    units1 = [bufs[1]['gu'], bufs[1]['dn'], *dil_g['units']]
    n1 = len(units1)
    send, recv, thru, token = _copies_start(units1, [lax.empty(u.shape[1:], F32) for u in units1], _sibling_plan, n1,
                                            name="rs1_sibling_start")
    dx = ffn_bwd(dx, saved[2], 2, tie=token[0, 0])
    thru = _copies_wait(send, recv, thru, n1, _sibling_plan, dx, name="rs1_sibling_wait")
    parts1 = [_add_half(u, g, half_idx, name=f"rs1_add_half_{k}") for k, (u, g) in enumerate(zip(thru[:n1], thru[n1:]))]
    send, recv, thru, token = _copies_start([w for _, w in parts1],
                                            [lax.empty((3,) + w.shape[1:], BF16) for _, w in parts1], _chips_plan,
                                            3 * n1, name="rs1_chips_start")
    dx, mla_g = mla_bwd(dx, saved[1], 0, 1, tie=token[0, 0])
    thru = _copies_wait(send, recv, thru, n1, _chips_plan, dx, name="rs1_chips_wait")
    reds1 = [_add_shards(p, g, shard_idx, name=f"rs1_add_shards_{k}")
             for k, ((p, _), g) in enumerate(zip(parts1, thru[n1:]))]
    dx = ffn_bwd(dx, saved[0], 0)
    grad_x = dx[None]

    pad_row = lambda v: jnp.pad(v.reshape(-1), (0, (-v.size) % D_MODEL)).reshape(-1, D_MODEL)
    small = jnp.concatenate(
        [jnp.concatenate([dmod[i][r] for i in range(2) for r in range(9)], axis=0),
         jnp.concatenate([dpre[i][s] for i in range(2) for s in range(3)], axis=0),
         jnp.concatenate([dpost[i][s] for i in range(2) for s in range(3)], axis=0),
         pad_row(mla_g['q_norm']), pad_row(mla_g['kv_norm']), pad_row(dil_g['rel_bias'])], axis=0)
    small = jnp.pad(small, ((0, SMALL_ROWS - small.shape[0]), (0, 0)))
    small_all = _all_gather(small, name="ag_small_grads", in_vmem=True)
    small_sum = _sum_devices(small_all, 8, name="sum_small_grads")
    g_b_mod = small_sum[0:18].reshape(2, 9 * D_MODEL)
    my_cols = lambda t: lax.dynamic_slice_in_dim(t, shard_id * 256, 256, axis=2)
    g_norm_pre = my_cols(small_sum[18:24].reshape(2, 3, D_MODEL))
    g_norm_post = my_cols(small_sum[24:30].reshape(2, 3, D_MODEL))
    g_q_norm = small_sum[30, :Q_LORA].reshape(1, Q_LORA)
    g_kv_norm = small_sum[31, :KV_LORA].reshape(1, KV_LORA)
    g_rel_bias = small_sum[32:34].reshape(-1)[:N_BUCKETS * 48].reshape(N_BUCKETS, 48)
    dmod_all = small_all.reshape(8, SMALL_ROWS, D_MODEL)[:, 0:18].reshape(8, 2, 9 * D_MODEL)
    dmod_cols = lax.dynamic_slice_in_dim(dmod_all, shard_id * 2304, 2304, axis=2).transpose(1, 0, 2)
    g_w_mod = _mm(silu_c, dmod_cols.astype(BF16), ta=True, tn_cap=768, name="w_mod_grad")

    units0 = [bufs[0]['gu'], bufs[0]['dn'], *mla_g['units']]
    got_a = _swap_halves(units0, name="rs0_sibling")
    parts0 = [_add_half(u, g, half_idx, name=f"rs0_add_half_{k}") for k, (u, g) in enumerate(zip(units0, got_a))]
    got_b = _send_to_chips([w for _, w in parts0], name="rs0_chips")
    reds0 = [_add_shards(p, g, shard_idx, name=f"rs0_add_shards_{k}")
             for k, ((p, _), g) in enumerate(zip(parts0, got_b))]
    fin = _pair_gather(reds0 + reds1, name="rs_pair_gather")
    gu_fin, dn_fin = (fin[0], fin[6]), (fin[1], fin[7])
    reduced = dict(ffn_w_gate=jnp.stack([t[:, 0] for t in gu_fin]), ffn_w_up=jnp.stack([t[:, 1] for t in gu_fin]),
                   ffn_w_down=jnp.stack(dn_fin))
    for n, t in zip(['mla_w_in', 'mla_w_q_up', 'mla_w_kv_up', 'mla_w_o', 'dil_w_in', 'dil_w_o'], fin[2:6] + fin[8:]):
        reduced[n] = t.reshape(given[n].shape)

    grads = dict(norm_pre=g_norm_pre, norm_post=g_norm_post, w_mod=g_w_mod, b_mod=g_b_mod, mla_q_norm=g_q_norm,
                 mla_kv_norm=g_kv_norm, rel_bias=g_rel_bias, **reduced)

    deltas, new_m, new_v = {}, {}, {}
    for n in WEIGHTS:
        deltas[n], new_m[n], new_v[n] = _adamw(given[n], grads[n], given["m_" + n], given["v_" + n],
                                               name=f"adamw_{n}")
    return (loss, grad_x, *[grads[n] for n in WEIGHTS], *[deltas[n] for n in WEIGHTS],
            *[new_m[n] for n in WEIGHTS], *[new_v[n] for n in WEIGHTS])
```

```python
import math

import jax
import jax.numpy as jnp
from jax import lax
from jax.experimental import pallas as pl
from jax.experimental.pallas import tpu as pltpu

F32 = jnp.float32
BF16 = jnp.bfloat16
MESH = pl.DeviceIdType.MESH

SEQ = 2048
D_MODEL = 1024
D_FF = 2816
N_SHARD = 4
F_SHARD = D_FF // N_SHARD
EPS = 1e-6
FFN_RES = 0.5
HEADS = 16
Q_LORA, KV_LORA, QK_NOPE, QK_ROPE, V_HEAD = 384, 256, 64, 32, 64
HALF_ROPE = QK_ROPE // 2
ROPE_THETA = 10000.0
DIL_GROUPS = ((128, 1), (512, 4), (2048, 16))
DIL_BLOCK = 128
N_BUCKETS = 32
MAX_DISTANCE = 2048
ADAM_LR, ADAM_B1, ADAM_B2, ADAM_EPS, ADAM_WD, ADAM_STEP = 0.001, 0.9, 0.999, 1e-08, 0.01, 10

VMEM_LIMIT = 48 * 1024 * 1024
SMALL_ROWS = 40

WEIGHTS = ['norm_pre', 'norm_post', 'w_mod', 'b_mod', 'ffn_w_gate', 'ffn_w_up', 'ffn_w_down', 'mla_w_in',
           'mla_q_norm', 'mla_w_q_up', 'mla_kv_norm', 'mla_w_kv_up', 'mla_w_o', 'dil_w_in', 'dil_w_o', 'rel_bias']


def _cparams(**kw):
    return pltpu.CompilerParams(vmem_limit_bytes=VMEM_LIMIT, **kw)


def _pick(n, cap, mult=128):
    if n <= cap:
        return n
    best = n
    for t in range(mult, cap + 1, mult):
        if n % t == 0:
            best = t
    return best


def _mm(a, b, *, name, ta=False, tb=False, reduce_g=False, bias=None, out_dtype=F32, tm_cap=512, tn_cap=1024,
        g_n=None, b_sel=None, out_shape=None, out_sel=None, out_buf=None):
    a3 = a if a.ndim == 3 else a[None]
    ga = a3.shape[0]
    if b_sel is None:
        b_n = b if b.ndim == 3 else b[None]
        gb = b_n.shape[0]
        b_sel = (lambda g: (g,)) if gb > 1 else (lambda g: (0,))
        g_n = max(ga, gb)
    else:
        b_n = b
    k_dim, m_dim = (a3.shape[1], a3.shape[2]) if ta else (a3.shape[2], a3.shape[1])
    k2, n_dim = (b_n.shape[-1], b_n.shape[-2]) if tb else (b_n.shape[-2], b_n.shape[-1])
    assert k_dim == k2, (a.shape, b.shape)
    tm = _pick(m_dim, tm_cap, 128 if ta else 8)
    tn = _pick(n_dim, tn_cap, 128)
    mt, nt = m_dim // tm, n_dim // tn
    dims = (((0 if ta else 1,), (1 if tb else 0,)), ((), ()))

    if reduce_g:
        grid = (mt, nt, g_n)
        ids = lambda i, j, g: (g, i, j)
    else:
        grid = (g_n, mt, nt)
        ids = lambda g, i, j: (g, i, j)

    def a_map(*p):
        g, i, j = ids(*p)
        g = g if ga > 1 else 0
        return (g, 0, i) if ta else (g, i, 0)

    def b_map(*p):
        g, i, j = ids(*p)
        return (*b_sel(g), j, 0) if tb else (*b_sel(g), 0, j)

    b_lead = (None,) * (b_n.ndim - 2)
    a_spec = pl.BlockSpec((None, k_dim, tm) if ta else (None, tm, k_dim), a_map)
    b_spec = pl.BlockSpec(b_lead + ((tn, k_dim) if tb else (k_dim, tn)), b_map)
    in_specs = [a_spec, b_spec]
    operands = [a3, b_n]
    if bias is not None:
        assert not reduce_g and bias.shape == (g_n, 1, n_dim)
        in_specs.append(pl.BlockSpec((None, 1, tn), lambda g, i, j: (g, 0, j)))
        operands.append(bias)
    aliases = {}
    if out_buf is not None:
        assert tuple(out_buf.shape) == tuple(out_shape) and out_buf.dtype == out_dtype
        in_specs.append(pl.BlockSpec(memory_space=pl.ANY))
        operands.append(out_buf)
        aliases = {len(operands) - 1: 0}

    if reduce_g:
        out_spec = pl.BlockSpec((tm, tn), lambda i, j, g: (i, j))
        out_sds = jax.ShapeDtypeStruct((m_dim, n_dim), F32)
    elif out_shape is not None:
        def o_map(g, i, j):
            lead, rb = out_sel(g, i)
            return (*lead, rb, j)

        out_spec = pl.BlockSpec((None,) * (len(out_shape) - 2) + (tm, tn), o_map)
        out_sds = jax.ShapeDtypeStruct(tuple(out_shape), out_dtype)
    else:
        out_spec = pl.BlockSpec((None, tm, tn), lambda g, i, j: (g, i, j))
        out_sds = jax.ShapeDtypeStruct((g_n, m_dim, n_dim), out_dtype)

    def body(a_ref, b_ref, *rest):
        o_ref = rest[-1]
        r = lax.dot_general(a_ref[...].astype(BF16), b_ref[...].astype(BF16), dims, preferred_element_type=F32)
        if bias is not None:
            r = r + rest[0][...]
        if reduce_g:
            g = pl.program_id(2)

            @pl.when(g == 0)
            def _():
                o_ref[...] = r

            @pl.when(g > 0)
            def _():
                o_ref[...] += r
        else:
            o_ref[...] = r.astype(o_ref.dtype)

    out = pl.pallas_call(body, grid=grid, in_specs=in_specs, out_specs=out_spec, out_shape=out_sds,
                         input_output_aliases=aliases, compiler_params=_cparams(), name=name)(*operands)
    if not reduce_g and out_shape is None and a.ndim == 2 and b.ndim == 2:
        out = out[0]
    return out


def _rows(tm, w):
    return pl.BlockSpec((tm, w), lambda i: (i, 0))


def _vec(w):
    return pl.BlockSpec((1, w), lambda i: (0, 0))


def _rstd(v):
    return lax.rsqrt(jnp.mean(v * v, axis=-1, keepdims=True) + EPS)


def _pre_fwd(x, pg, sc, sh, *, name):
    s_n, w = x.shape
    tm = _pick(s_n, 256, 8)

    def body(x_ref, pg_ref, sc_ref, sh_ref, o_ref):
        xv = x_ref[...]
        n = (xv * _rstd(xv)) * pg_ref[...]
        o_ref[...] = (n * (1.0 + sc_ref[...]) + sh_ref[...]).astype(o_ref.dtype)

    return pl.pallas_call(body, grid=(s_n // tm,), in_specs=[_rows(tm, w), _vec(w), _vec(w), _vec(w)],
                          out_specs=_rows(tm, w), out_shape=jax.ShapeDtypeStruct((s_n, w), BF16),
                          compiler_params=_cparams(), name=name)(x, pg, sc, sh)


def _post_fwd(f, x, qg, gate, res_w, *, name):
    s_n, w = x.shape
    tm = _pick(s_n, 256, 8)

    def body(f_ref, x_ref, qg_ref, gate_ref, o_ref):
        fv = f_ref[...]
        y = (fv * _rstd(fv)) * qg_ref[...]
        o_ref[...] = x_ref[...] + (res_w * gate_ref[...]) * y

    return pl.pallas_call(body, grid=(s_n // tm,), in_specs=[_rows(tm, w), _rows(tm, w), _vec(w), _vec(w)],
                          out_specs=_rows(tm, w), out_shape=jax.ShapeDtypeStruct((s_n, w), F32),
                          compiler_params=_cparams(), name=name)(f, x, qg, gate)


def _post_bwd(dout, f, qg, gate, res_w, *, name):
    s_n, w = f.shape
    tm = _pick(s_n, 256, 8)

    def body(do_ref, f_ref, qg_ref, gate_ref, df_ref, dgate_ref, dqg_ref):
        @pl.when(pl.program_id(0) == 0)
        def _():
            dgate_ref[...] = jnp.zeros_like(dgate_ref)
            dqg_ref[...] = jnp.zeros_like(dqg_ref)

        do = do_ref[...]
        fv = f_ref[...]
        r = _rstd(fv)
        fh = fv * r
        qg_v = qg_ref[...]
        dgate_ref[...] += res_w * jnp.sum(do * (fh * qg_v), axis=0, keepdims=True)
        dy = do * (res_w * gate_ref[...])
        dqg_ref[...] += jnp.sum(dy * fh, axis=0, keepdims=True)
        dfh = dy * qg_v
        df = r * (dfh - fh * jnp.mean(dfh * fh, axis=-1, keepdims=True))
        df_ref[...] = df.astype(df_ref.dtype)

    return pl.pallas_call(
        body, grid=(s_n // tm,), in_specs=[_rows(tm, w), _rows(tm, w), _vec(w), _vec(w)],
        out_specs=[_rows(tm, w), _vec(w), _vec(w)],
        out_shape=[jax.ShapeDtypeStruct((s_n, w), BF16), jax.ShapeDtypeStruct((1, w), F32),
                   jax.ShapeDtypeStruct((1, w), F32)],
        compiler_params=_cparams(), name=name)(dout, f, qg, gate)


def _pre_bwd(dhn, x, dout, pg, sc, *, name):
    s_n, w = x.shape
    tm = _pick(s_n, 256, 8)

    def body(dhn_ref, x_ref, do_ref, pg_ref, sc_ref, dx_ref, dsh_ref, dsc_ref, dpg_ref):
        @pl.when(pl.program_id(0) == 0)
        def _():
            dsh_ref[...] = jnp.zeros_like(dsh_ref)
            dsc_ref[...] = jnp.zeros_like(dsc_ref)
            dpg_ref[...] = jnp.zeros_like(dpg_ref)

        dhn_v = dhn_ref[...]
        xv = x_ref[...]
        r = _rstd(xv)
        xh = xv * r
        pg_v = pg_ref[...]
        dsh_ref[...] += jnp.sum(dhn_v, axis=0, keepdims=True)
        dsc_ref[...] += jnp.sum(dhn_v * (xh * pg_v), axis=0, keepdims=True)
        dn = dhn_v * (1.0 + sc_ref[...])
        dpg_ref[...] += jnp.sum(dn * xh, axis=0, keepdims=True)
        dxh = dn * pg_v
        dx_ref[...] = do_ref[...] + r * (dxh - xh * jnp.mean(dxh * xh, axis=-1, keepdims=True))

    vec = jax.ShapeDtypeStruct((1, w), F32)
    return pl.pallas_call(
        body, grid=(s_n // tm,), in_specs=[_rows(tm, w), _rows(tm, w), _rows(tm, w), _vec(w), _vec(w)],
        out_specs=[_rows(tm, w), _vec(w), _vec(w), _vec(w)],
        out_shape=[jax.ShapeDtypeStruct((s_n, w), F32), vec, vec, vec],
        compiler_params=_cparams(), name=name)(dhn, x, dout, pg, sc)


def _rms_fwd(x, g, *, name):
    s_n, w = x.shape
    tm = _pick(s_n, 512, 8)

    def body(x_ref, g_ref, o_ref):
        xv = x_ref[...]
        o_ref[...] = ((xv * _rstd(xv)) * g_ref[...]).astype(o_ref.dtype)

    return pl.pallas_call(body, grid=(s_n // tm,), in_specs=[_rows(tm, w), _vec(w)], out_specs=_rows(tm, w),
                          out_shape=jax.ShapeDtypeStruct((s_n, w), BF16), compiler_params=_cparams(),
                          name=name)(x, g)


def _rms_bwd(dy, x, g, *, name):
    s_n, w = x.shape
    tm = _pick(s_n, 512, 8)

    def body(dy_ref, x_ref, g_ref, dx_ref, dg_ref):
        @pl.when(pl.program_id(0) == 0)
        def _():
            dg_ref[...] = jnp.zeros_like(dg_ref)

        dy_v = dy_ref[...]
        xv = x_ref[...]
        r = _rstd(xv)
        xh = xv * r
        dg_ref[...] += jnp.sum(dy_v * xh, axis=0, keepdims=True)
        dxh = dy_v * g_ref[...]
        dx_ref[...] = r * (dxh - xh * jnp.mean(dxh * xh, axis=-1, keepdims=True))

    return pl.pallas_call(
        body, grid=(s_n // tm,), in_specs=[_rows(tm, w), _rows(tm, w), _vec(w)],
        out_specs=[_rows(tm, w), _vec(w)],
        out_shape=[jax.ShapeDtypeStruct((s_n, w), F32), jax.ShapeDtypeStruct((1, w), F32)],
        compiler_params=_cparams(), name=name)(dy, x, g)


def _rope(a1, a2, cos, sin, *, name):
    s_n, w = a1.shape
    tm = _pick(s_n, 512, 8)

    def body(a1_ref, a2_ref, c_ref, s_ref, r1_ref, r2_ref):
        u, v, c_v, s_v = a1_ref[...], a2_ref[...], c_ref[...], s_ref[...]
        r1_ref[...] = u * c_v - v * s_v
        r2_ref[...] = u * s_v + v * c_v

    sd = jax.ShapeDtypeStruct((s_n, w), F32)
    return pl.pallas_call(body, grid=(s_n // tm,), in_specs=[_rows(tm, w)] * 4, out_specs=[_rows(tm, w)] * 2,
                          out_shape=[sd, sd], compiler_params=_cparams(), name=name)(a1, a2, cos, sin)


def _silu_bf16(x, *, name):
    def body(x_ref, o_ref):
        xv = x_ref[...]
        o_ref[...] = (xv * jax.nn.sigmoid(xv)).astype(o_ref.dtype)

    return pl.pallas_call(body, out_shape=jax.ShapeDtypeStruct(x.shape, BF16), name=name)(x)


def _loss(y, target, *, name):
    s_n, w = y.shape
    tm = _pick(s_n, 256, 8)

    def body(y_ref, t_ref, dy_ref, l_ref):
        @pl.when(pl.program_id(0) == 0)
        def _():
            l_ref[...] = jnp.zeros_like(l_ref)

        e = y_ref[...] - t_ref[...]
        dy_ref[...] = e * (1.0 / w)
        row = jnp.mean(e * e, axis=-1, keepdims=True)
        l_ref[...] += 0.5 * jnp.sum(row, axis=0, keepdims=True)

    return pl.pallas_call(
        body, grid=(s_n // tm,), in_specs=[_rows(tm, w), _rows(tm, w)],
        out_specs=[_rows(tm, w), pl.BlockSpec((1, 1), lambda i: (0, 0))],
        out_shape=[jax.ShapeDtypeStruct((s_n, w), F32), jax.ShapeDtypeStruct((1, 1), F32)],
        compiler_params=_cparams(), name=name)(y, target)


FFN_TM = 512


def _ffn_up(hn, w_gu, *, name):
    s_n, d = hn.shape
    f = w_gu.shape[-1]
    tm = _pick(s_n, FFN_TM, 8)

    def body(hn_ref, wg_ref, wu_ref, gu_ref, a_ref):
        xv = hn_ref[...]
        g = jnp.dot(xv, wg_ref[...], preferred_element_type=F32)
        u = jnp.dot(xv, wu_ref[...], preferred_element_type=F32)
        gu_ref[0] = g.astype(BF16)
        gu_ref[1] = u.astype(BF16)
        a_ref[...] = ((g * jax.nn.sigmoid(g)) * u).astype(BF16)

    w_blk = lambda t: pl.BlockSpec((None, None, d, f), lambda s, m: (s, t, 0, 0))
    return pl.pallas_call(
        body, grid=(N_SHARD, s_n // tm),
        in_specs=[pl.BlockSpec((tm, d), lambda s, m: (m, 0)), w_blk(0), w_blk(1)],
        out_specs=[pl.BlockSpec((None, 2, tm, f), lambda s, m: (s, 0, m, 0)),
                   pl.BlockSpec((None, tm, f), lambda s, m: (s, m, 0))],
        out_shape=[jax.ShapeDtypeStruct((N_SHARD, 2, s_n, f), BF16), jax.ShapeDtypeStruct((N_SHARD, s_n, f), BF16)],
        compiler_params=_cparams(), name=name)(hn, w_gu, w_gu)


def _ffn_dgu(df, w_dn, gu, *, name):
    s_n, d = df.shape
    f = w_dn.shape[-2]
    tm = _pick(s_n, FFN_TM, 8)

    def body(df_ref, wd_ref, gu_ref, o_ref):
        da = lax.dot_general(df_ref[...], wd_ref[...], (((1,), (1,)), ((), ())), preferred_element_type=F32)
        g = gu_ref[0].astype(F32)
        u = gu_ref[1].astype(F32)
        sig = jax.nn.sigmoid(g)
        o_ref[0] = (da * u * (sig * (1.0 + g * (1.0 - sig)))).astype(BF16)
        o_ref[1] = (da * (g * sig)).astype(BF16)

    gu_blk = pl.BlockSpec((None, 2, tm, f), lambda s, m: (s, 0, m, 0))
    return pl.pallas_call(
        body, grid=(N_SHARD, s_n // tm),
        in_specs=[pl.BlockSpec((tm, d), lambda s, m: (m, 0)),
                  pl.BlockSpec((None, f, d), lambda s, m: (s, 0, 0)), gu_blk],
        out_specs=gu_blk, out_shape=jax.ShapeDtypeStruct((N_SHARD, 2, s_n, f), BF16),
        compiler_params=_cparams(), name=name)(df, w_dn, gu)


_NT = (((1,), (1,)), ((), ()))
_TN = (((0,), (0,)), ((), ()))
MLA_TQ = 256


def _causal_mask(i, tq, s_n):
    qpos = i * tq + lax.broadcasted_iota(jnp.int32, (tq, s_n), 0)
    kpos = lax.broadcasted_iota(jnp.int32, (tq, s_n), 1)
    return kpos <= qpos


def _mla_attn_fwd(q, k, v, *, name):
    h_n, s_n, dq = q.shape
    dv = v.shape[-1]
    tq = MLA_TQ
    scale = float(dq) ** -0.5

    def body(q_ref, k_ref, v_ref, o_ref, lse_ref):
        i = pl.program_id(1)
        for e in range(1, s_n // tq + 1):
            @pl.when(i == e - 1)
            def _(ext=e * tq):
                mask = _causal_mask(i, tq, ext)
                s = lax.dot_general(q_ref[...], k_ref[0:ext, :], _NT, preferred_element_type=F32) * scale
                s = jnp.where(mask, s, -jnp.inf)
                m = jnp.max(s, axis=-1, keepdims=True)
                p = jnp.exp(s - m)
                l = jnp.sum(p, axis=-1, keepdims=True)
                o = jnp.dot(p.astype(BF16), v_ref[0:ext, :], preferred_element_type=F32)
                o_ref[...] = o / l
                lse_ref[...] = m + jnp.log(l)

    return pl.pallas_call(
        body, grid=(h_n, s_n // tq),
        in_specs=[pl.BlockSpec((None, tq, dq), lambda h, i: (h, i, 0)),
                  pl.BlockSpec((None, s_n, dq), lambda h, i: (h, 0, 0)),
                  pl.BlockSpec((None, s_n, dv), lambda h, i: (h, 0, 0))],
        out_specs=[pl.BlockSpec((None, tq, dv), lambda h, i: (h, i, 0)),
                   pl.BlockSpec((None, tq, 1), lambda h, i: (h, i, 0))],
        out_shape=[jax.ShapeDtypeStruct((h_n, s_n, dv), F32), jax.ShapeDtypeStruct((h_n, s_n, 1), F32)],
        compiler_params=_cparams(), name=name)(q, k, v)


def _mla_attn_bwd(q, k, v, o, do, lse, *, name):
    h_n, s_n, dq = q.shape
    dv = v.shape[-1]
    tq = MLA_TQ
    scale = float(dq) ** -0.5

    def body(q_ref, k_ref, v_ref, o_ref, do_ref, lse_ref, dq_ref, dk_ref, dv_ref):
        i = pl.program_id(1)

        @pl.when(i == 0)
        def _():
            dk_ref[...] = jnp.zeros_like(dk_ref)
            dv_ref[...] = jnp.zeros_like(dv_ref)

        for e in range(1, s_n // tq + 1):
            @pl.when(i == e - 1)
            def _(ext=e * tq):
                mask = _causal_mask(i, tq, ext)
                qv, kv, vv = q_ref[...], k_ref[0:ext, :], v_ref[0:ext, :]
                do_v = do_ref[...]
                s = lax.dot_general(qv, kv, _NT, preferred_element_type=F32) * scale
                p = jnp.where(mask, jnp.exp(s - lse_ref[...]), 0.0)
                dob = do_v.astype(BF16)
                dv_ref[0:ext, :] += lax.dot_general(p.astype(BF16), dob, _TN, preferred_element_type=F32)
                dp = lax.dot_general(dob, vv, _NT, preferred_element_type=F32)
                delta = jnp.sum(do_v * o_ref[...], axis=-1, keepdims=True)
                dsb = (p * (dp - delta) * scale).astype(BF16)
                dq_ref[...] = jnp.dot(dsb, kv, preferred_element_type=F32)
                dk_ref[0:ext, :] += lax.dot_general(dsb, qv, _TN, preferred_element_type=F32)

    return pl.pallas_call(
        body, grid=(h_n, s_n // tq),
        in_specs=[pl.BlockSpec((None, tq, dq), lambda h, i: (h, i, 0)),
                  pl.BlockSpec((None, s_n, dq), lambda h, i: (h, 0, 0)),
                  pl.BlockSpec((None, s_n, dv), lambda h, i: (h, 0, 0)),
                  pl.BlockSpec((None, tq, dv), lambda h, i: (h, i, 0)),
                  pl.BlockSpec((None, tq, dv), lambda h, i: (h, i, 0)),
                  pl.BlockSpec((None, tq, 1), lambda h, i: (h, i, 0))],
        out_specs=[pl.BlockSpec((None, tq, dq), lambda h, i: (h, i, 0)),
                   pl.BlockSpec((None, s_n, dq), lambda h, i: (h, 0, 0)),
                   pl.BlockSpec((None, s_n, dv), lambda h, i: (h, 0, 0))],
        out_shape=[jax.ShapeDtypeStruct((h_n, s_n, dq), F32), jax.ShapeDtypeStruct((h_n, s_n, dq), F32),
                   jax.ShapeDtypeStruct((h_n, s_n, dv), F32)],
        compiler_params=_cparams(), name=name)(q, k, v, o, do, lse)


def _head_sum(x, *, name):
    h_n, s_n, w = x.shape
    tm = _pick(s_n, 256, 8)

    def body(x_ref, o_ref):
        o_ref[...] = jnp.sum(x_ref[...], axis=0)

    return pl.pallas_call(body, grid=(s_n // tm,), in_specs=[pl.BlockSpec((h_n, tm, w), lambda i: (0, i, 0))],
                          out_specs=_rows(tm, w), out_shape=jax.ShapeDtypeStruct((s_n, w), F32),
                          compiler_params=_cparams(), name=name)(x)


N_BLK = SEQ // DIL_BLOCK
DIL_SCALE = 64 ** -0.5


def _dil_masks():
    iq = lax.broadcasted_iota(jnp.int32, (DIL_BLOCK, 2 * DIL_BLOCK), 0)
    ik = lax.broadcasted_iota(jnp.int32, (DIL_BLOCK, 2 * DIL_BLOCK), 1)
    rel = DIL_BLOCK + iq - ik
    both = (rel >= 0) & (rel <= DIL_BLOCK)
    iq1 = lax.broadcasted_iota(jnp.int32, (DIL_BLOCK, DIL_BLOCK), 0)
    ik1 = lax.broadcasted_iota(jnp.int32, (DIL_BLOCK, DIL_BLOCK), 1)
    return both, ik1 <= iq1


def _dil_block(j, nb):
    lo = j * DIL_BLOCK
    first = j % nb == 0
    k_lo = lo if first else lo - DIL_BLOCK
    b_lo = DIL_BLOCK if first else 0
    return lo, k_lo, b_lo, first


def _dil_attn_fwd(q, k, v, bias, nb, *, name):
    h_n, s_n, e = q.shape

    def body(q_ref, k_ref, v_ref, b_ref, o_ref, lse_ref):
        m_both, m_first = _dil_masks()
        for j in range(N_BLK):
            lo, k_lo, b_lo, first = _dil_block(j, nb)
            qj = q_ref[lo:lo + DIL_BLOCK, :]
            kk = k_ref[k_lo:lo + DIL_BLOCK, :]
            vv = v_ref[k_lo:lo + DIL_BLOCK, :]
            s = lax.dot_general(qj, kk, _NT, preferred_element_type=F32) * DIL_SCALE + b_ref[:, b_lo:]
            s = jnp.where(m_first if first else m_both, s, -jnp.inf)
            m = jnp.max(s, axis=-1, keepdims=True)
            lse = m + jnp.log(jnp.sum(jnp.exp(s - m), axis=-1, keepdims=True))
            p = jnp.exp(s - lse)
            o_ref[lo:lo + DIL_BLOCK, :] = jnp.dot(p.astype(BF16), vv, preferred_element_type=F32)
            lse_ref[lo:lo + DIL_BLOCK, :] = lse

    head = lambda w: pl.BlockSpec((None, s_n, w), lambda h: (h, 0, 0))
    return pl.pallas_call(
        body, grid=(h_n,),
        in_specs=[head(e), head(e), head(e), pl.BlockSpec((None, DIL_BLOCK, 2 * DIL_BLOCK), lambda h: (h, 0, 0))],
        out_specs=[head(e), head(1)],
        out_shape=[jax.ShapeDtypeStruct((h_n, s_n, e), F32), jax.ShapeDtypeStruct((h_n, s_n, 1), F32)],
        compiler_params=_cparams(), name=name)(q, k, v, bias)


def _dil_attn_bwd(q, k, v, bias, lse, do, dlt, nb, *, name):
    h_n, s_n, e = q.shape

    def body(q_ref, k_ref, v_ref, b_ref, lse_ref, do_ref, dlt_ref, dq_ref, dk_ref, dv_ref, db_ref):
        dk_ref[...] = jnp.zeros_like(dk_ref)
        dv_ref[...] = jnp.zeros_like(dv_ref)
        db_ref[...] = jnp.zeros_like(db_ref)
        m_both, m_first = _dil_masks()
        for j in range(N_BLK):
            lo, k_lo, b_lo, first = _dil_block(j, nb)
            qj = q_ref[lo:lo + DIL_BLOCK, :]
            kk = k_ref[k_lo:lo + DIL_BLOCK, :]
            vv = v_ref[k_lo:lo + DIL_BLOCK, :]
            s = lax.dot_general(qj, kk, _NT, preferred_element_type=F32) * DIL_SCALE + b_ref[:, b_lo:]
            p = jnp.where(m_first if first else m_both, jnp.exp(s - lse_ref[lo:lo + DIL_BLOCK, :]), 0.0)
            dob = do_ref[lo:lo + DIL_BLOCK, :].astype(BF16)
            dv_ref[k_lo:lo + DIL_BLOCK, :] += lax.dot_general(p.astype(BF16), dob, _TN, preferred_element_type=F32)
            dp = lax.dot_general(dob, vv, _NT, preferred_element_type=F32)
            ds = p * (dp - dlt_ref[lo:lo + DIL_BLOCK, :])
            db_ref[:, b_lo:] += ds
            dsb = (ds * DIL_SCALE).astype(BF16)
            dq_ref[lo:lo + DIL_BLOCK, :] = jnp.dot(dsb, kk, preferred_element_type=F32)
            dk_ref[k_lo:lo + DIL_BLOCK, :] += lax.dot_general(dsb, qj, _TN, preferred_element_type=F32)

    head = lambda w: pl.BlockSpec((None, s_n, w), lambda h: (h, 0, 0))
    b_spec = pl.BlockSpec((None, DIL_BLOCK, 2 * DIL_BLOCK), lambda h: (h, 0, 0))
    sd = jax.ShapeDtypeStruct((h_n, s_n, e), F32)
    return pl.pallas_call(
        body, grid=(h_n,),
        in_specs=[head(e), head(e), head(e), b_spec, head(1), head(e), head(1)],
        out_specs=[head(e), head(e), head(e), b_spec],
        out_shape=[sd, sd, sd, jax.ShapeDtypeStruct((h_n, DIL_BLOCK, 2 * DIL_BLOCK), F32)],
        compiler_params=_cparams(), name=name)(q, k, v, bias, lse, do, dlt)


def _proj_heads(x, w, *, name):
    s_n, k = x.shape
    n = w.shape[-1]
    tm, tn, e = 512, 768, 64
    per_blk, n_blk = tn // e, n // tn

    def body(x_ref, w_ref, o_ref):
        r = jnp.dot(x_ref[...], w_ref[...], preferred_element_type=F32)
        for j in range(per_blk):
            o_ref[j] = r[:, e * j:e * (j + 1)].astype(BF16)

    return pl.pallas_call(
        body, grid=(w.shape[0], n_blk, s_n // tm),
        in_specs=[pl.BlockSpec((tm, k), lambda s, b, m: (m, 0)), pl.BlockSpec((None, k, tn), lambda s, b, m: (s, 0, b))],
        out_specs=pl.BlockSpec((per_blk, tm, e), lambda s, b, m: (s * n_blk + b, m, 0)),
        out_shape=jax.ShapeDtypeStruct((w.shape[0] * n // e, s_n, e), BF16), compiler_params=_cparams(),
        name=name)(x, w)


def _heads_cat(d_ref):
    return jnp.concatenate([d_ref[j] for j in range(d_ref.shape[0])], axis=1)


def _proj_heads_dw(x, dh, *, name):
    s_n, k = x.shape
    tn, e = 768, 64
    per_blk = tn // e
    n_blk = dh.shape[0] // N_SHARD // per_blk
    n = n_blk * tn

    def body(x_ref, d_ref, o_ref):
        o_ref[...] = lax.dot_general(x_ref[...], _heads_cat(d_ref), _TN, preferred_element_type=F32)

    return pl.pallas_call(
        body, grid=(N_SHARD, n_blk, 2),
        in_specs=[pl.BlockSpec((s_n, k // 2), lambda s, b, r: (0, r)),
                  pl.BlockSpec((per_blk, s_n, e), lambda s, b, r: (s * n_blk + b, 0, 0))],
        out_specs=pl.BlockSpec((None, None, k // 2, tn), lambda s, b, r: (r, s, 0, b)),
        out_shape=jax.ShapeDtypeStruct((2, N_SHARD, k // 2, n), F32), compiler_params=_cparams(), name=name)(x, dh)


def _proj_heads_dx(dh, w, *, name):
    k, n = w.shape[1:]
    s_n = dh.shape[1]
    tm, tn, e = 512, 768, 64
    per_blk, n_blk = tn // e, n // tn

    def body(d_ref, w_ref, o_ref):
        r = lax.dot_general(_heads_cat(d_ref), w_ref[...], _NT, preferred_element_type=F32)
        g = pl.program_id(1)

        @pl.when(g == 0)
        def _():
            o_ref[...] = r

        @pl.when(g > 0)
        def _():
            o_ref[...] += r

    return pl.pallas_call(
        body, grid=(s_n // tm, N_SHARD * n_blk),
        in_specs=[pl.BlockSpec((per_blk, tm, e), lambda m, g: (g, m, 0)),
                  pl.BlockSpec((None, k, tn), lambda m, g: (g // n_blk, 0, g % n_blk))],
        out_specs=pl.BlockSpec((tm, k), lambda m, g: (m, 0)),
        out_shape=jax.ShapeDtypeStruct((s_n, k), F32), compiler_params=_cparams(), name=name)(dh, w)


def _group_alpha(l_refs):
    ls = [r[...] for r in l_refs]
    m = jnp.maximum(jnp.maximum(ls[0], ls[1]), ls[2])
    es = [jnp.exp(l - m) for l in ls]
    tot = es[0] + es[1] + es[2]
    return [ex / tot for ex in es]


def _dil_mix_fwd(os_, ls_, *, name):
    h_n, s_n, e = os_[0].shape
    tm = 512

    def body(o0, o1, o2, l0, l1, l2, out_ref):
        al = _group_alpha((l0, l1, l2))
        out_ref[...] = al[0] * o0[...] + al[1] * o1[...] + al[2] * o2[...]

    blk = lambda w: pl.BlockSpec((None, tm, w), lambda h, i: (h, i, 0))
    return pl.pallas_call(body, grid=(h_n, s_n // tm), in_specs=[blk(e)] * 3 + [blk(1)] * 3, out_specs=blk(e),
                          out_shape=jax.ShapeDtypeStruct((h_n, s_n, e), F32), compiler_params=_cparams(),
                          name=name)(*os_, *ls_)


def _dil_mix_bwd(do, os_, ls_, *, name):
    h_n, s_n, e = do.shape
    tm = 512

    def body(do_ref, o0, o1, o2, l0, l1, l2, d0, d1, d2, t0, t1, t2):
        al = _group_alpha((l0, l1, l2))
        do_v = do_ref[...]
        mix = al[0] * o0[...] + al[1] * o1[...] + al[2] * o2[...]
        dbar = jnp.sum(do_v * mix, axis=-1, keepdims=True)
        for a_g, d_ref, t_ref in zip(al, (d0, d1, d2), (t0, t1, t2)):
            d_ref[...] = a_g * do_v
            t_ref[...] = a_g * dbar

    blk = lambda w: pl.BlockSpec((None, tm, w), lambda h, i: (h, i, 0))
    sd_e = jax.ShapeDtypeStruct((h_n, s_n, e), F32)
    sd_1 = jax.ShapeDtypeStruct((h_n, s_n, 1), F32)
    outs = pl.pallas_call(body, grid=(h_n, s_n // tm), in_specs=[blk(e)] * 4 + [blk(1)] * 3,
                          out_specs=[blk(e)] * 3 + [blk(1)] * 3, out_shape=[sd_e] * 3 + [sd_1] * 3,
                          compiler_params=_cparams(), name=name)(do, *os_, *ls_)
    return outs[:3], outs[3:]


def _bias_grad(ds, bucket, *, name):
    h_n = ds.shape[0]

    def body(ds_ref, bk_ref, o_ref):
        ds_v = ds_ref[...]
        bk = bk_ref[...]
        lane = lax.broadcasted_iota(jnp.int32, (1, N_BUCKETS), 1)
        acc = jnp.zeros((1, N_BUCKETS), F32)
        for b in range(N_BUCKETS):
            tot = jnp.sum(jnp.sum(jnp.where(bk == b, ds_v, 0.0), axis=1, keepdims=True), axis=0, keepdims=True)
            acc = acc + jnp.where(lane == b, tot, 0.0)
        o_ref[...] = acc

    return pl.pallas_call(
        body, grid=(h_n,),
        in_specs=[pl.BlockSpec((None, DIL_BLOCK, 2 * DIL_BLOCK), lambda h: (h, 0, 0)),
                  pl.BlockSpec((DIL_BLOCK, 2 * DIL_BLOCK), lambda h: (0, 0))],
        out_specs=pl.BlockSpec((None, 1, N_BUCKETS), lambda h: (h, 0, 0)),
        out_shape=jax.ShapeDtypeStruct((h_n, 1, N_BUCKETS), F32), compiler_params=_cparams(), name=name)(ds, bucket)


def _bias_table(rb, bucket, *, name):
    h_n = rb.shape[0]

    def body(rb_ref, bk_ref, o_ref):
        bk = bk_ref[...]
        row = rb_ref[...]
        acc = jnp.zeros(bk.shape, F32)
        for b in range(N_BUCKETS):
            acc = jnp.where(bk == b, row[:, b:b + 1], acc)
        o_ref[...] = acc

    return pl.pallas_call(
        body, grid=(h_n,),
        in_specs=[pl.BlockSpec((None, 1, N_BUCKETS), lambda h: (h, 0, 0)),
                  pl.BlockSpec((DIL_BLOCK, 2 * DIL_BLOCK), lambda h: (0, 0))],
        out_specs=pl.BlockSpec((None, DIL_BLOCK, 2 * DIL_BLOCK), lambda h: (h, 0, 0)),
        out_shape=jax.ShapeDtypeStruct((h_n, DIL_BLOCK, 2 * DIL_BLOCK), F32), compiler_params=_cparams(),
        name=name)(rb, bucket)


def _row_tile(rows, cols, budget=1 << 20):
    if rows * cols * 4 <= budget or rows % 8:
        return rows
    best = 8
    for t in range(8, rows + 1, 8):
        if rows % t == 0 and t * cols * 4 <= budget:
            best = t
    return best


def _adamw(w, g, m, v, *, name):
    shape = w.shape
    cols = shape[-1]
    rows = math.prod(shape[:-1]) if len(shape) > 1 else 1
    to2 = lambda t: t.reshape(rows, cols)
    tr = _row_tile(rows, cols)
    c1 = 1.0 / (1.0 - ADAM_B1 ** ADAM_STEP)
    c2 = 1.0 / (1.0 - ADAM_B2 ** ADAM_STEP)

    def body(w_ref, g_ref, m_ref, v_ref, d_ref, nm_ref, nv_ref):
        g_v = g_ref[...]
        nm = ADAM_B1 * m_ref[...] + (1.0 - ADAM_B1) * g_v
        nv = ADAM_B2 * v_ref[...] + (1.0 - ADAM_B2) * (g_v * g_v)
        m_hat = nm * c1
        v_hat = nv * c2
        d_ref[...] = -ADAM_LR * (m_hat / (jnp.sqrt(v_hat) + ADAM_EPS) + ADAM_WD * w_ref[...])
        nm_ref[...] = nm
        nv_ref[...] = nv

    blk = pl.BlockSpec((tr, cols), lambda i: (i, 0))
    sd = jax.ShapeDtypeStruct((rows, cols), F32)
    outs = pl.pallas_call(body, grid=(rows // tr,), in_specs=[blk] * 4, out_specs=[blk] * 3, out_shape=[sd] * 3,
                          compiler_params=_cparams(), name=name)(to2(w), to2(g), to2(m), to2(v))
    return tuple(t.reshape(shape) for t in outs)


def _add_half(unit, got, half_idx, *, name):
    rest = unit.shape[2:]
    c = rest[-1]
    r = math.prod(rest[:-1])
    tr = _row_tile(r, c)

    def body(idx_ref, u_ref, g_ref, o_ref, w_ref):
        tot = u_ref[...] + g_ref[...].astype(F32)
        o_ref[...] = tot
        w_ref[...] = tot.astype(BF16)

    blk = pl.BlockSpec((None, tr, c), lambda s, i, idx: (s, i, 0))
    grid_spec = pltpu.PrefetchScalarGridSpec(
        num_scalar_prefetch=1, grid=(N_SHARD, r // tr),
        in_specs=[pl.BlockSpec((None, None, tr, c), lambda s, i, idx: (idx[0], s, i, 0)), blk],
        out_specs=[blk, blk])
    out, wire = pl.pallas_call(
        body, grid_spec=grid_spec,
        out_shape=[jax.ShapeDtypeStruct((N_SHARD, r, c), F32), jax.ShapeDtypeStruct((N_SHARD, r, c), BF16)],
        compiler_params=_cparams(), name=name)(half_idx, unit.reshape(2, N_SHARD, r, c), got.reshape(N_SHARD, r, c))
    return out.reshape((N_SHARD,) + rest), wire.reshape((N_SHARD,) + rest)


def _add_shards(part, got, shard_idx, *, name):
    rest = part.shape[1:]
    c = rest[-1]
    r = math.prod(rest[:-1])
    tr = _row_tile(r, c)

    def body(idx_ref, p_ref, g_ref, o_ref):
        acc = p_ref[...]
        for k in range(3):
            acc = acc + g_ref[k].astype(F32)
        o_ref[...] = acc

    grid_spec = pltpu.PrefetchScalarGridSpec(
        num_scalar_prefetch=1, grid=(r // tr,),
        in_specs=[pl.BlockSpec((None, tr, c), lambda i, idx: (idx[0], i, 0)),
                  pl.BlockSpec((3, tr, c), lambda i, idx: (0, i, 0))],
        out_specs=pl.BlockSpec((tr, c), lambda i, idx: (i, 0)))
    out = pl.pallas_call(body, grid_spec=grid_spec, out_shape=jax.ShapeDtypeStruct((r, c), F32),
                         compiler_params=_cparams(), name=name)(
        shard_idx, part.reshape(N_SHARD, r, c), got.reshape(3, r, c))
    return out.reshape(rest)


def _sum_devices(x, n_dev, *, name):
    rows = x.shape[0] // n_dev

    def body(x_ref, o_ref):
        acc = x_ref[0:rows, :]
        for d in range(1, n_dev):
            acc = acc + x_ref[d * rows:(d + 1) * rows, :]
        o_ref[...] = acc

    return pl.pallas_call(body, out_shape=jax.ShapeDtypeStruct((rows, x.shape[1]), F32), name=name)(x)


def _my_pos():
    return lax.axis_index("x"), lax.axis_index("y"), lax.axis_index("c")


def _all_gather(x_blk, *, name, in_vmem):
    m_per, n = x_blk.shape

    def body(x_ref, out_ref, send_sems, recv_sems, local_sem):
        x, y, c = _my_pos()
        me, sibling = (x, y, c), (x, y, 1 - c)
        chips = [(1 - x, y), (x, 1 - y), (1 - x, 1 - y)]

        def rows(px, py, pc):
            return out_ref.at[pl.ds((4 * px + 2 * py + pc) * m_per, m_per), :]

        def copy(k, block, to, src=None):
            return pltpu.make_async_remote_copy(
                src_ref=rows(*block) if src is None else src, dst_ref=rows(*block),
                send_sem=send_sems.at[k], recv_sem=recv_sems.at[k], device_id=to, device_id_type=MESH)

        mine = pltpu.make_async_copy(x_ref, rows(*me), local_sem)
        mine.start()
        first = [copy(0, me, sibling, src=x_ref)]
        first += [copy(1 + j, me, (*chip, c), src=x_ref) for j, chip in enumerate(chips)]
        for cp in first:
            cp.start()
        passed = [copy(4 + j, (*chip, c), sibling) for j, chip in enumerate(chips)]
        for j, chip in enumerate(chips):
            copy(1 + j, (*chip, c), me).wait_recv()
            passed[j].start()
        copy(0, sibling, me).wait_recv()
        for j, chip in enumerate(chips):
            copy(4 + j, (*chip, 1 - c), me).wait_recv()
        for cp in first + passed:
            cp.wait_send()
        mine.wait()

    space = pltpu.VMEM if in_vmem else pl.ANY
    return pl.pallas_call(
        body, out_shape=jax.ShapeDtypeStruct((8 * m_per, n), x_blk.dtype),
        in_specs=[pl.BlockSpec(memory_space=space)], out_specs=pl.BlockSpec(memory_space=space),
        scratch_shapes=[pltpu.SemaphoreType.DMA((7,)), pltpu.SemaphoreType.DMA((7,)), pltpu.SemaphoreType.DMA],
        name=name)(x_blk)


_HBM = pl.BlockSpec(memory_space=pl.ANY)


def _gather_weights(fams, *, name):
    n = len(fams)

    def body(*refs):
        ins, outs = refs[:n], refs[n:2 * n]
        send_sems, recv_sems = refs[2 * n:]
        x, y, c = _my_pos()
        me, sibling = (x, y, c), (x, y, 1 - c)
        chips = [(1 - x, y), (x, 1 - y), (1 - x, 1 - y)]

        def copy(f, k, block, to, src=None):
            px, py, pc = block
            dst = outs[f].at[2 * px + py, pc]
            return pltpu.make_async_remote_copy(
                src_ref=dst if src is None else src, dst_ref=dst, send_sem=send_sems.at[7 * f + k],
                recv_sem=recv_sems.at[7 * f + k], device_id=to, device_id_type=MESH)

        first, passed = [], []
        for f in range(n):
            src = ins[f].at[c]
            first.append(copy(f, 0, me, sibling, src=src))
            first += [copy(f, 1 + j, me, (*chip, c), src=src) for j, chip in enumerate(chips)]
        for cp in first:
            cp.start()
        for j, chip in enumerate(chips):
            for f in range(n):
                copy(f, 1 + j, (*chip, c), me).wait_recv()
                passed.append(copy(f, 4 + j, (*chip, c), sibling))
                passed[-1].start()
        for f in range(n):
            copy(f, 0, sibling, me).wait_recv()
        for j, chip in enumerate(chips):
            for f in range(n):
                copy(f, 4 + j, (*chip, 1 - c), me).wait_recv()
        for cp in first + passed:
            cp.wait_send()

    outs = pl.pallas_call(
        body, out_shape=[jax.ShapeDtypeStruct((N_SHARD,) + t.shape, t.dtype) for t in fams],
        in_specs=[_HBM] * n, out_specs=[_HBM] * n,
        scratch_shapes=[pltpu.SemaphoreType.DMA((7 * n,)), pltpu.SemaphoreType.DMA((7 * n,))], name=name)(*fams)
    return [_place_own(o, t) for o, t in zip(outs, fams)]


def _swap_halves(units, *, name):
    n = len(units)

    def body(*refs):
        ins, outs = refs[:n], refs[n:2 * n]
        send_sems, recv_sems = refs[2 * n:]
        x, y, c = _my_pos()
        cps = [pltpu.make_async_remote_copy(src_ref=ins[f].at[1 - c], dst_ref=outs[f], send_sem=send_sems.at[f],
                                            recv_sem=recv_sems.at[f], device_id=(x, y, 1 - c), device_id_type=MESH)
               for f in range(n)]
        for cp in cps:
            cp.start()
        for cp in cps:
            cp.wait()

    return pl.pallas_call(
        body, out_shape=[jax.ShapeDtypeStruct(t.shape[1:], t.dtype) for t in units],
        in_specs=[_HBM] * n, out_specs=[_HBM] * n,
        scratch_shapes=[pltpu.SemaphoreType.DMA((n,)), pltpu.SemaphoreType.DMA((n,))], name=name)(*units)


def _send_to_chips(parts, *, name):
    n = len(parts)

    def body(*refs):
        ins, outs = refs[:n], refs[n:2 * n]
        send_sems, recv_sems = refs[2 * n:]
        x, y, c = _my_pos()
        chips = [(1 - x, y), (x, 1 - y), (1 - x, 1 - y)]
        cps = [pltpu.make_async_remote_copy(src_ref=ins[f].at[2 * cx + cy], dst_ref=outs[f].at[k],
                                            send_sem=send_sems.at[3 * f + k], recv_sem=recv_sems.at[3 * f + k],
                                            device_id=(cx, cy, c), device_id_type=MESH)
               for f in range(n) for k, (cx, cy) in enumerate(chips)]
        for cp in cps:
            cp.start()
        for cp in cps:
            cp.wait()

    return pl.pallas_call(
        body, out_shape=[jax.ShapeDtypeStruct((3,) + t.shape[1:], t.dtype) for t in parts],
        in_specs=[_HBM] * n, out_specs=[_HBM] * n,
        scratch_shapes=[pltpu.SemaphoreType.DMA((3 * n,)), pltpu.SemaphoreType.DMA((3 * n,))], name=name)(*parts)


def _pair_gather(halves, *, name):
    n = len(halves)

    def body(*refs):
        ins, outs = refs[:n], refs[n:2 * n]
        send_sems, recv_sems = refs[2 * n:]
        x, y, c = _my_pos()
        cps = [pltpu.make_async_remote_copy(src_ref=ins[f], dst_ref=outs[f].at[c], send_sem=send_sems.at[f],
                                            recv_sem=recv_sems.at[f], device_id=(x, y, 1 - c), device_id_type=MESH)
               for f in range(n)]
        for cp in cps:
            cp.start()
        for f in range(n):
            pltpu.make_async_remote_copy(src_ref=ins[f], dst_ref=outs[f].at[1 - c], send_sem=send_sems.at[f],
                                         recv_sem=recv_sems.at[f], device_id=(x, y, 1 - c),
                                         device_id_type=MESH).wait_recv()
        for cp in cps:
            cp.wait_send()

    outs = pl.pallas_call(
        body, out_shape=[jax.ShapeDtypeStruct((2,) + t.shape, t.dtype) for t in halves],
        in_specs=[_HBM] * n, out_specs=[_HBM] * n,
        scratch_shapes=[pltpu.SemaphoreType.DMA((n,)), pltpu.SemaphoreType.DMA((n,))], name=name)(*halves)
    c = lax.axis_index("c")
    return [lax.dynamic_update_index_in_dim(o, t, c, 0) for o, t in zip(outs, halves)]


_HBM_ONLY = pl.BlockSpec(memory_space=pltpu.HBM)
_SEMS = pl.BlockSpec(memory_space=pltpu.SEMAPHORE)
_EFFECT = pltpu.SideEffectType.DATAFLOW_SIDE_EFFECTING


def _copies_start(srcs, lands, plan, n_copies, *, name):
    n, m = len(srcs), len(lands)

    def body(*refs):
        src_refs, land_refs = refs[:n], refs[n:n + m]
        send_sems, recv_sems, token = refs[n + m], refs[n + m + 1], refs[-1]
        for k, (src, dst, peer) in enumerate(plan(src_refs, land_refs)):
            pltpu.make_async_remote_copy(src_ref=src, dst_ref=dst, send_sem=send_sems.at[k], recv_sem=recv_sems.at[k],
                                         device_id=peer, device_id_type=MESH).start()
        token[...] = jnp.zeros_like(token)

    bufs = [pltpu.with_memory_space_constraint(t, pltpu.HBM) for t in (*srcs, *lands)]
    outs = pl.pallas_call(
        body, name=name,
        out_shape=(pltpu.SemaphoreType.DMA((n_copies,)), pltpu.SemaphoreType.DMA((n_copies,)),
                   *[pltpu.HBM(t.shape, t.dtype) for t in bufs], jax.ShapeDtypeStruct((8, 128), F32)),
        in_specs=[_HBM_ONLY] * (n + m),
        out_specs=(_SEMS, _SEMS, *[_HBM_ONLY] * (n + m), pl.BlockSpec(memory_space=pltpu.VMEM)),
        input_output_aliases={k: 2 + k for k in range(n + m)},
        compiler_params=pltpu.CompilerParams(has_side_effects=_EFFECT))(*bufs)
    return outs[0], outs[1], list(outs[2:2 + n + m]), outs[-1]


def _copies_wait(send_sems, recv_sems, thru, n_src, plan, after, *, name):
    nm = len(thru)

    def body(*refs):
        t_refs, send, recv = refs[:nm], refs[nm], refs[nm + 1]
        for k, (src, dst, peer) in enumerate(plan(t_refs[:n_src], t_refs[n_src:])):
            cp = pltpu.make_async_remote_copy(src_ref=src, dst_ref=dst, send_sem=send.at[k], recv_sem=recv.at[k],
                                              device_id=peer, device_id_type=MESH)
            cp.wait_send()
            cp.wait_recv()

    outs = pl.pallas_call(
        body, name=name, out_shape=tuple(pltpu.HBM(t.shape, t.dtype) for t in thru),
        in_specs=[_HBM_ONLY] * nm + [_SEMS, _SEMS, pl.BlockSpec(memory_space=pl.ANY)],
        out_specs=tuple([_HBM_ONLY] * nm), input_output_aliases={k: k for k in range(nm)},
        compiler_params=pltpu.CompilerParams(has_side_effects=_EFFECT))(*thru, send_sems, recv_sems, after)
    return list(outs)


_RELATIONS = [(dx, dy, dc) for dx in (0, 1) for dy in (0, 1) for dc in (0, 1)][1:]


def _gather_plan(src_refs, land_refs):
    x, y, c = _my_pos()
    flip = lambda v, d: 1 - v if d else v
    return [(s_ref.at[c], l_ref.at[2 * x + y, c], (flip(x, dx), flip(y, dy), flip(c, dc)))
            for s_ref, l_ref in zip(src_refs, land_refs) for dx, dy, dc in _RELATIONS]


def _sibling_plan(src_refs, land_refs):
    x, y, c = _my_pos()
    return [(s_ref.at[1 - c], l_ref, (x, y, 1 - c)) for s_ref, l_ref in zip(src_refs, land_refs)]


def _chips_plan(src_refs, land_refs):
    x, y, c = _my_pos()
    chips = [(1 - x, y), (x, 1 - y), (1 - x, 1 - y)]
    return [(s_ref.at[2 * cx + cy], l_ref.at[k], (cx, cy, c))
            for s_ref, l_ref in zip(src_refs, land_refs) for k, (cx, cy) in enumerate(chips)]


def _place_own(gathered, fam):
    x, y, c = _my_pos()
    own = lax.dynamic_index_in_dim(fam, c, 0, keepdims=True)[None]
    return lax.dynamic_update_slice(gathered, own, (2 * x + y, c) + (0,) * (fam.ndim - 1))


def _to_heads(t, width):
    return t.reshape(t.shape[0], HEADS, width).transpose(1, 0, 2)


def _from_heads(t):
    return t.transpose(1, 0, 2).reshape(t.shape[1], -1)


def _residue_major(t, d):
    h_n, s_n, e = t.shape
    return t.reshape(h_n, s_n // d, d, e).transpose(0, 2, 1, 3).reshape(h_n, s_n, e)


def _token_major(t, d):
    h_n, s_n, e = t.shape
    return t.reshape(h_n, d, s_n // d, e).transpose(0, 2, 1, 3).reshape(h_n, s_n, e)


def _t5_bucket(dist):
    max_exact = N_BUCKETS // 2
    d = jnp.maximum(dist, 1).astype(F32)
    large = max_exact + (jnp.log(d / max_exact) / math.log(MAX_DISTANCE / max_exact)
                         * (N_BUCKETS - max_exact)).astype(jnp.int32)
    large = jnp.minimum(large, N_BUCKETS - 1)
    return jnp.where(dist < max_exact, dist, large)


def _bucket_map(dilation):
    iq = jnp.arange(DIL_BLOCK)[:, None]
    ik = jnp.arange(2 * DIL_BLOCK)[None, :]
    rel = DIL_BLOCK + iq - ik
    return _t5_bucket(jnp.maximum(rel, 0) * dilation).astype(jnp.int32)


def _q_perm(w):
    w3 = w.reshape(w.shape[0], HEADS, QK_NOPE + QK_ROPE)
    return jnp.concatenate([w3[:, :, :QK_NOPE].reshape(w.shape[0], -1),
                            w3[:, :, QK_NOPE:QK_NOPE + HALF_ROPE].reshape(w.shape[0], -1),
                            w3[:, :, QK_NOPE + HALF_ROPE:].reshape(w.shape[0], -1)], axis=1)


def _q_unperm(w):
    n0, n1 = HEADS * QK_NOPE, HEADS * HALF_ROPE
    r = w.shape[0]
    return jnp.concatenate([w[:, :n0].reshape(r, HEADS, QK_NOPE), w[:, n0:n0 + n1].reshape(r, HEADS, HALF_ROPE),
                            w[:, n0 + n1:].reshape(r, HEADS, HALF_ROPE)], axis=2).reshape(r, -1)


def _kv_perm(w):
    w3 = w.reshape(w.shape[0], HEADS, QK_NOPE + V_HEAD)
    return jnp.concatenate([w3[:, :, :QK_NOPE].reshape(w.shape[0], -1), w3[:, :, QK_NOPE:].reshape(w.shape[0], -1)],
                           axis=1)


def _kv_unperm(w):
    n0 = HEADS * QK_NOPE
    r = w.shape[0]
    return jnp.concatenate([w[:, :n0].reshape(r, HEADS, QK_NOPE), w[:, n0:].reshape(r, HEADS, V_HEAD)],
                           axis=2).reshape(r, -1)


def _row(v):
    return v.reshape(1, -1)


def kernel(x, c, norm_pre, norm_post, w_mod, b_mod, ffn_w_gate, ffn_w_up, ffn_w_down, mla_w_in, mla_q_norm, mla_w_q_up, mla_kv_norm, mla_w_kv_up, mla_w_o, dil_w_in, dil_w_o, rel_bias, loss_target, m_norm_pre, m_norm_post, m_w_mod, m_b_mod, m_ffn_w_gate, m_ffn_w_up, m_ffn_w_down, m_mla_w_in, m_mla_q_norm, m_mla_w_q_up, m_mla_kv_norm, m_mla_w_kv_up, m_mla_w_o, m_dil_w_in, m_dil_w_o, m_rel_bias, v_norm_pre, v_norm_post, v_w_mod, v_b_mod, v_ffn_w_gate, v_ffn_w_up, v_ffn_w_down, v_mla_w_in, v_mla_q_norm, v_mla_w_q_up, v_mla_kv_norm, v_mla_w_kv_up, v_mla_w_o, v_dil_w_in, v_dil_w_o, v_rel_bias):
    given = dict(locals())
    ix, iy, ic = _my_pos()
    shard_id = 2 * ix + iy
    dev_id = 4 * ix + 2 * iy + ic
    x2 = x[0]
    target = loss_target[0]
    half_idx = jnp.reshape(ic, (1,)).astype(jnp.int32)
    shard_idx = jnp.reshape(shard_id, (1,)).astype(jnp.int32)

    blk = jnp.zeros((8, D_MODEL), F32)
    blk = blk.at[0].set(c[0])
    blk = blk.at[1:3].set(jnp.pad(norm_pre.reshape(-1), (0, 512)).reshape(2, D_MODEL))
    blk = blk.at[3:5].set(jnp.pad(norm_post.reshape(-1), (0, 512)).reshape(2, D_MODEL))
    got = _all_gather(blk, name="ag_c_norms", in_vmem=True).reshape(N_SHARD, 2, 8, D_MODEL)
    c_all = got[:, :, 0, :].reshape(8, D_MODEL)

    def full_norm(lo):
        t = got[:, 0, lo:lo + 2, :].reshape(N_SHARD, 2 * D_MODEL)[:, :1536].reshape(N_SHARD, 2, 3, 256)
        return t.transpose(1, 2, 0, 3).reshape(2, 3, D_MODEL)

    pre_full, post_full = full_norm(1), full_norm(3)

    silu_c = _silu_bf16(c_all, name="silu_c")
    b_cols = lax.dynamic_slice_in_dim(b_mod, shard_id * 2304, 2304, axis=1).reshape(2, 1, 2304)
    mod_part = _mm(silu_c, w_mod, bias=b_cols, name="mod_mm", tn_cap=768)
    mod_all = _all_gather(mod_part.reshape(16, 2304), name="ag_mod", in_vmem=True)
    mod_all = mod_all.reshape(N_SHARD, 2, 2, 8, 2304)[:, 0]
    mod_mine = lax.dynamic_index_in_dim(mod_all, dev_id, axis=2, keepdims=False)
    mod = mod_mine.transpose(1, 0, 2).reshape(2, 9, D_MODEL)

    bf = lambda t: t.astype(BF16)
    ffn_fam = lambda i, h: [bf(jnp.stack([ffn_w_gate[i, h], ffn_w_up[i, h]])),
                            bf(ffn_w_down[i, h].reshape(2, F_SHARD // 2, D_MODEL))]
    mla_fam = [bf(mla_w_in.reshape(2, 128, -1)), bf(mla_w_q_up.reshape(2, 192, -1)),
               bf(mla_w_kv_up.reshape(2, 128, -1)), bf(mla_w_o.reshape(2, 128, D_MODEL))]
    dil_fam = [bf(dil_w_in.reshape(2, 512, -1)), bf(dil_w_o.reshape(2, 128, D_MODEL))]
    later_fams = [ffn_fam(1, 0) + dil_fam, ffn_fam(1, 1)]
    full, later_fams, mod = lax.optimization_barrier(
        (_gather_weights(ffn_fam(0, 0) + mla_fam + ffn_fam(0, 1), name="ag_weights_first"), later_fams, mod))

    def gather_later(fams, tag):
        lands = [lax.empty((N_SHARD,) + t.shape, t.dtype) for t in fams]
        send, recv, thru, token = _copies_start(fams, lands, _gather_plan, 7 * len(fams), name=f"ag_start_{tag}")
        return dict(send=send, recv=recv, thru=thru, token=token, n=len(fams), tag=tag)

    def arrive(st, after):
        thru = _copies_wait(st['send'], st['recv'], st['thru'], st['n'], _gather_plan, after,
                            name=f"ag_wait_{st['tag']}")
        return [_place_own(o, t) for t, o in zip(thru[:st['n']], thru[st['n']:])]

    in_flight = gather_later(later_fams[0], "l1s01")
    as_ffn = lambda w_gu, w_dn: (w_gu, w_dn.reshape(N_SHARD, F_SHARD, D_MODEL))
    ffn_w = {(0, 0): as_ffn(full[0], full[1]), (0, 1): as_ffn(full[6], full[7])}
    w_in = full[2].reshape(D_MODEL, -1)
    wq_p = _q_perm(full[3].reshape(N_SHARD, Q_LORA, -1).transpose(1, 0, 2).reshape(Q_LORA, -1))
    wkv_p = _kv_perm(full[4].reshape(N_SHARD, KV_LORA, -1).transpose(1, 0, 2).reshape(KV_LORA, -1))
    w_mo = full[5].reshape(D_MODEL, D_MODEL)
    dil_w = {}

    pos = jnp.arange(SEQ, dtype=F32)
    freqs = ROPE_THETA ** (-jnp.arange(HALF_ROPE, dtype=F32) / HALF_ROPE)
    ang = pos[:, None] * freqs[None, :]
    cos_k, sin_k = jnp.cos(ang), jnp.sin(ang)
    cos_q, sin_q = jnp.tile(cos_k, (1, HEADS)), jnp.tile(sin_k, (1, HEADS))

    buckets = [_bucket_map(d) for _, d in DIL_GROUPS]
    biases = [_bias_table(rel_bias[:, g * HEADS:(g + 1) * HEADS].T.reshape(HEADS, 1, N_BUCKETS), bk,
                          name=f"dil_bias_table_g{g}") for g, bk in enumerate(buckets)]

    def sub_params(i, sub):
        return dict(pg=_row(pre_full[i, sub]), qg=_row(post_full[i, sub]), sh=_row(mod[i, 3 * sub]),
                    sc=_row(mod[i, 3 * sub + 1]), gate=_row(mod[i, 3 * sub + 2]))

    def ffn_fwd(xin, i, h, sub, tie=None):
        p = sub_params(i, sub)
        if tie is not None:
            p['sh'] = p['sh'] + tie
        tag = f"l{i}s{sub}"
        w_gu, w_dn = ffn_w[i, h]
        hn = _pre_fwd(xin, p['pg'], p['sc'], p['sh'], name=f"pre_fwd_{tag}")
        gu, a = _ffn_up(hn, w_gu, name=f"ffn_up_{tag}")
        f = _mm(a, w_dn, reduce_g=True, name=f"ffn_down_{tag}")
        out = _post_fwd(f, xin, p['qg'], p['gate'], FFN_RES, name=f"post_fwd_{tag}")
        return out, dict(x=xin, hn=hn, gu=gu, a=a, f=f, p=p, i=i, h=h, tag=tag)

    def mla_fwd(xin, i, sub):
        p = sub_params(i, sub)
        tag = f"l{i}s{sub}"
        hn = _pre_fwd(xin, p['pg'], p['sc'], p['sh'], name=f"pre_fwd_{tag}")
        lat = _mm(hn, w_in, name="mla_lat")
        cq, ckv = lat[:, :Q_LORA], lat[:, Q_LORA:Q_LORA + KV_LORA]
        k1, k2 = lat[:, Q_LORA + KV_LORA:Q_LORA + KV_LORA + HALF_ROPE], lat[:, Q_LORA + KV_LORA + HALF_ROPE:]
        cqn = _rms_fwd(cq, mla_q_norm, name="mla_qnorm")
        ckvn = _rms_fwd(ckv, mla_kv_norm, name="mla_kvnorm")
        qp = _mm(cqn, wq_p, name="mla_q_up")
        kvp = _mm(ckvn, wkv_p, name="mla_kv_up")
        n0, n1 = HEADS * QK_NOPE, HEADS * HALF_ROPE
        qr1, qr2 = _rope(qp[:, n0:n0 + n1], qp[:, n0 + n1:], cos_q, sin_q, name="rope_q")
        kr1, kr2 = _rope(k1, k2, cos_k, sin_k, name="rope_k")
        q = jnp.concatenate([qp[:, :n0].reshape(SEQ, HEADS, QK_NOPE), qr1.reshape(SEQ, HEADS, HALF_ROPE),
                             qr2.reshape(SEQ, HEADS, HALF_ROPE)], axis=2).transpose(1, 0, 2).astype(BF16)
        kr = jnp.broadcast_to(jnp.concatenate([kr1, kr2], axis=1)[:, None, :], (SEQ, HEADS, QK_ROPE))
        k = jnp.concatenate([kvp[:, :n0].reshape(SEQ, HEADS, QK_NOPE), kr], axis=2).transpose(1, 0, 2).astype(BF16)
        v = _to_heads(kvp[:, n0:], V_HEAD).astype(BF16)
        o, lse = _mla_attn_fwd(q, k, v, name="mla_attn_fwd")
        o_flat = _from_heads(o).astype(BF16)
        f = _mm(o_flat, w_mo, name="mla_out")
        out = _post_fwd(f, xin, p['qg'], p['gate'], 1.0, name=f"post_fwd_{tag}")
        return out, dict(x=xin, hn=hn, cq=cq, ckv=ckv, cqn=cqn, ckvn=ckvn, q=q, k=k, v=v, o=o, lse=lse,
                         o_flat=o_flat, f=f, p=p, tag=tag)

    def dil_fwd(xin, i, sub):
        p = sub_params(i, sub)
        tag = f"l{i}s{sub}"
        hn = _pre_fwd(xin, p['pg'], p['sc'], p['sh'], name=f"pre_fwd_{tag}")
        heads = _proj_heads(hn, dil_w['in'], name="dil_proj").reshape(3, 3, HEADS, SEQ, 64)
        qkv, outs, lses = [], [], []
        for g, (window, d) in enumerate(DIL_GROUPS):
            q, k, v = (_residue_major(heads[g, t], d) for t in range(3))
            o, lse = _dil_attn_fwd(q, k, v, biases[g], SEQ // d // DIL_BLOCK, name=f"dil_attn_fwd_g{g}")
            qkv.append((q, k, v))
            outs.append(_token_major(o, d))
            lses.append(_token_major(lse, d))
        mix = _dil_mix_fwd(outs, lses, name="dil_mix_fwd")
        o_flat = _from_heads(mix).astype(BF16)
        f = _mm(o_flat, dil_w['out'], name="dil_out")
        out = _post_fwd(f, xin, p['qg'], p['gate'], 1.0, name=f"post_fwd_{tag}")
        return out, dict(x=xin, hn=hn, qkv=qkv, outs=outs, lses=lses, o_flat=o_flat, f=f, p=p, tag=tag)

    saved = [None] * 6
    xs, saved[0] = ffn_fwd(x2, 0, 0, 0, tie=in_flight['token'][0, 0])
    xs, saved[1] = mla_fwd(xs, 0, 1)
    xs, saved[2] = ffn_fwd(xs, 0, 1, 2)
    got, last_fams = lax.optimization_barrier((arrive(in_flight, xs), later_fams[1]))
    ffn_w[1, 0] = as_ffn(got[0], got[1])
    dil_w['in'], dil_w['out'] = got[2].reshape(N_SHARD, D_MODEL, -1), got[3].reshape(D_MODEL, D_MODEL)
    in_flight = gather_later(last_fams, "l1s2")
    xs, saved[3] = ffn_fwd(xs, 1, 0, 0, tie=in_flight['token'][0, 0])
    xs, saved[4] = dil_fwd(xs, 1, 1)
    ffn_w[1, 1] = as_ffn(*arrive(in_flight, xs))
    xs, saved[5] = ffn_fwd(xs, 1, 1, 2)

    dx, loss_part = _loss(xs, target, name="loss")
    loss = lax.psum(loss_part[0, 0], ("x", "y", "c"))

    dmod = [[None] * 9 for _ in range(2)]
    dpre = [[None] * 3 for _ in range(2)]
    dpost = [[None] * 3 for _ in range(2)]
    gu_shape = (2, N_SHARD, 2, F_SHARD, D_MODEL)
    dn_shape = (2, N_SHARD, F_SHARD, D_MODEL)
    bufs = [dict(gu=lax.empty(gu_shape, F32), dn=lax.empty(dn_shape, F32)) for _ in range(2)]
    row_unit = lambda g, r: ((r % 2, r // 2), 0)

    def close_sub(dhn, dout, sv, i, sub, res_dgate, res_dqg):
        p = sv['p']
        dxs, dsh, dsc, dpg = _pre_bwd(dhn, sv['x'], dout, p['pg'], p['sc'], name=f"pre_bwd_{sv['tag']}")
        dmod[i][3 * sub], dmod[i][3 * sub + 1], dmod[i][3 * sub + 2] = dsh, dsc, res_dgate
        dpre[i][sub], dpost[i][sub] = dpg, res_dqg
        return dxs

    def ffn_bwd(dout, sv, sub, tie=0.0):
        i, h, p, tag = sv['i'], sv['h'], sv['p'], sv['tag']
        w_gu, w_dn = ffn_w[i, h]
        df, dgate, dqg = _post_bwd(dout, sv['f'], p['qg'] + tie, p['gate'], FFN_RES, name=f"post_bwd_{tag}")
        bufs[i]['dn'] = _mm(sv['a'], df, ta=True, out_shape=dn_shape, out_sel=lambda g, r: ((h, g), r),
                            out_buf=bufs[i]['dn'], name=f"ffn_dwd_{tag}")
        dgu = _ffn_dgu(df, w_dn, sv['gu'], name=f"ffn_dgu_{tag}")
        dgu = dgu.reshape(2 * N_SHARD, SEQ, F_SHARD)
        bufs[i]['gu'] = _mm(dgu, sv['hn'], ta=True, out_shape=gu_shape,
                            out_sel=lambda g, r: ((h, g // 2, g % 2), r), out_buf=bufs[i]['gu'],
                            name=f"ffn_dwgu_{tag}")
        dhn = _mm(dgu, w_gu.reshape(2 * N_SHARD, D_MODEL, F_SHARD), tb=True, reduce_g=True, name=f"ffn_dhn_{tag}")
        return close_sub(dhn, dout, sv, i, sub, dgate, dqg)

    def mla_bwd(dout, sv, i, sub, tie=0.0):
        p, tag = sv['p'], sv['tag']
        df, dgate, dqg = _post_bwd(dout, sv['f'], p['qg'] + tie, p['gate'], 1.0, name=f"post_bwd_{tag}")
        u_wo = _mm(sv['o_flat'], df, ta=True, tm_cap=128, out_shape=(2, N_SHARD, 128, D_MODEL), out_sel=row_unit,
                   name="mla_dwo")
        do_flat = _mm(df, w_mo, tb=True, name="mla_do")
        do = _to_heads(do_flat, V_HEAD)
        dq, dk, dv = _mla_attn_bwd(sv['q'], sv['k'], sv['v'], sv['o'], do, sv['lse'], name="mla_attn_bwd")
        dq_t = dq.transpose(1, 0, 2)
        dqr1, dqr2 = _rope(dq_t[:, :, QK_NOPE:QK_NOPE + HALF_ROPE].reshape(SEQ, -1),
                           dq_t[:, :, QK_NOPE + HALF_ROPE:].reshape(SEQ, -1), cos_q, -sin_q, name="rope_q_bwd")
        dqp = jnp.concatenate([dq_t[:, :, :QK_NOPE].reshape(SEQ, -1), dqr1, dqr2], axis=1).astype(BF16)
        dkr = _head_sum(dk[:, :, QK_NOPE:], name="mla_dkr_sum")
        dk1, dk2 = _rope(dkr[:, :HALF_ROPE], dkr[:, HALF_ROPE:], cos_k, -sin_k, name="rope_k_bwd")
        dkvp = jnp.concatenate([_from_heads(dk[:, :, :QK_NOPE]), _from_heads(dv)], axis=1).astype(BF16)
        g_wq = _q_unperm(_mm(sv['cqn'], dqp, ta=True, name="mla_dwq"))
        g_wkv = _kv_unperm(_mm(sv['ckvn'], dkvp, ta=True, name="mla_dwkv"))
        dcqn = _mm(dqp, wq_p, tb=True, name="mla_dcqn")
        dckvn = _mm(dkvp, wkv_p, tb=True, name="mla_dckvn")
        dcq, g_qn = _rms_bwd(dcqn, sv['cq'], mla_q_norm, name="mla_qnorm_bwd")
        dckv, g_kvn = _rms_bwd(dckvn, sv['ckv'], mla_kv_norm, name="mla_kvnorm_bwd")
        dlat = jnp.concatenate([dcq, dckv, dk1, dk2], axis=1).astype(BF16)
        u_win = _mm(sv['hn'], dlat, ta=True, tm_cap=128, out_shape=(2, N_SHARD, 128, dlat.shape[1]),
                    out_sel=row_unit, name="mla_dwin")
        dhn = _mm(dlat, w_in, tb=True, name="mla_dhn")
        col_unit = lambda t: (t.reshape(t.shape[0], N_SHARD, -1).transpose(1, 0, 2)
                              .reshape(N_SHARD, 2, t.shape[0] // 2, -1).transpose(1, 0, 2, 3))
        grads = dict(units=[u_win, col_unit(g_wq), col_unit(g_wkv), u_wo], q_norm=g_qn, kv_norm=g_kvn)
        return close_sub(dhn, dout, sv, i, sub, dgate, dqg), grads

    def dil_bwd(dout, sv, i, sub):
        p, tag = sv['p'], sv['tag']
        df, dgate, dqg = _post_bwd(dout, sv['f'], p['qg'], p['gate'], 1.0, name=f"post_bwd_{tag}")
        u_wo = _mm(sv['o_flat'], df, ta=True, tm_cap=128, out_shape=(2, N_SHARD, 128, D_MODEL), out_sel=row_unit,
                   name="dil_dwo")
        do = _to_heads(_mm(df, dil_w['out'], tb=True, name="dil_do"), 64)
        dos, dlts = _dil_mix_bwd(do, sv['outs'], sv['lses'], name="dil_mix_bwd")
        pieces = []
        bias_rows = []
        for g, (window, d) in enumerate(DIL_GROUPS):
            q, k, v = sv['qkv'][g]
            dq, dk, dv, dbias = _dil_attn_bwd(q, k, v, biases[g], _residue_major(sv['lses'][g], d),
                                              _residue_major(dos[g], d), _residue_major(dlts[g], d),
                                              SEQ // d // DIL_BLOCK, name=f"dil_attn_bwd_g{g}")
            pieces += [_token_major(t, d).astype(BF16) for t in (dq, dk, dv)]
            bias_rows.append(_bias_grad(dbias, buckets[g], name=f"dil_bias_grad_g{g}")[:, 0, :])
        dheads = jnp.stack(pieces).reshape(9 * HEADS, SEQ, 64)
        u_win = _proj_heads_dw(sv['hn'], dheads, name="dil_dwin")
        dhn = _proj_heads_dx(dheads, dil_w['in'], name="dil_dhn")
        g_bias = jnp.concatenate(bias_rows, axis=0).T
        grads = dict(units=[u_win, u_wo], rel_bias=g_bias)
        return close_sub(dhn, dout, sv, i, sub, dgate, dqg), grads

    dx = ffn_bwd(dx, saved[5], 2)
    dx, dil_g = dil_bwd(dx, saved[4], 1, 1)
    dx = ffn_bwd(dx, saved[3], 0)
    units1 = [bufs[1]['gu'], bufs[1]['dn'], *dil_g['units']]
    n1 = len(units1)
    send, recv, thru, token = _copies_start(units1, [lax.empty(u.shape[1:], F32) for u in units1], _sibling_plan, n1,
                                            name="rs1_sibling_start")
    dx = ffn_bwd(dx, saved[2], 2, tie=token[0, 0])
    thru = _copies_wait(send, recv, thru, n1, _sibling_plan, dx, name="rs1_sibling_wait")
    parts1 = [_add_half(u, g, half_idx, name=f"rs1_add_half_{k}") for k, (u, g) in enumerate(zip(thru[:n1], thru[n1:]))]
    send, recv, thru, token = _copies_start([w for _, w in parts1],
                                            [lax.empty((3,) + w.shape[1:], BF16) for _, w in parts1], _chips_plan,
                                            3 * n1, name="rs1_chips_start")
    dx, mla_g = mla_bwd(dx, saved[1], 0, 1, tie=token[0, 0])
    thru = _copies_wait(send, recv, thru, n1, _chips_plan, dx, name="rs1_chips_wait")
    reds1 = [_add_shards(p, g, shard_idx, name=f"rs1_add_shards_{k}")
             for k, ((p, _), g) in enumerate(zip(parts1, thru[n1:]))]
    dx = ffn_bwd(dx, saved[0], 0)
    grad_x = dx[None]

    pad_row = lambda v: jnp.pad(v.reshape(-1), (0, (-v.size) % D_MODEL)).reshape(-1, D_MODEL)
    small = jnp.concatenate(
        [jnp.concatenate([dmod[i][r] for i in range(2) for r in range(9)], axis=0),
         jnp.concatenate([dpre[i][s] for i in range(2) for s in range(3)], axis=0),
         jnp.concatenate([dpost[i][s] for i in range(2) for s in range(3)], axis=0),
         pad_row(mla_g['q_norm']), pad_row(mla_g['kv_norm']), pad_row(dil_g['rel_bias'])], axis=0)
    small = jnp.pad(small, ((0, SMALL_ROWS - small.shape[0]), (0, 0)))
    small_all = _all_gather(small, name="ag_small_grads", in_vmem=True)
    small_sum = _sum_devices(small_all, 8, name="sum_small_grads")
    g_b_mod = small_sum[0:18].reshape(2, 9 * D_MODEL)
    my_cols = lambda t: lax.dynamic_slice_in_dim(t, shard_id * 256, 256, axis=2)
    g_norm_pre = my_cols(small_sum[18:24].reshape(2, 3, D_MODEL))
    g_norm_post = my_cols(small_sum[24:30].reshape(2, 3, D_MODEL))
    g_q_norm = small_sum[30, :Q_LORA].reshape(1, Q_LORA)
    g_kv_norm = small_sum[31, :KV_LORA].reshape(1, KV_LORA)
    g_rel_bias = small_sum[32:34].reshape(-1)[:N_BUCKETS * 48].reshape(N_BUCKETS, 48)
    dmod_all = small_all.reshape(8, SMALL_ROWS, D_MODEL)[:, 0:18].reshape(8, 2, 9 * D_MODEL)
    dmod_cols = lax.dynamic_slice_in_dim(dmod_all, shard_id * 2304, 2304, axis=2).transpose(1, 0, 2)
    g_w_mod = _mm(silu_c, dmod_cols.astype(BF16), ta=True, tn_cap=768, name="w_mod_grad")

    units0 = [bufs[0]['gu'], bufs[0]['dn'], *mla_g['units']]
    got_a = _swap_halves(units0, name="rs0_sibling")
    parts0 = [_add_half(u, g, half_idx, name=f"rs0_add_half_{k}") for k, (u, g) in enumerate(zip(units0, got_a))]
    got_b = _send_to_chips([w for _, w in parts0], name="rs0_chips")
    reds0 = [_add_shards(p, g, shard_idx, name=f"rs0_add_shards_{k}")
             for k, ((p, _), g) in enumerate(zip(parts0, got_b))]
    fin = _pair_gather(reds0 + reds1, name="rs_pair_gather")
    gu_fin, dn_fin = (fin[0], fin[6]), (fin[1], fin[7])
    swap = lambda t: jnp.swapaxes(t, 2, 3)
    reduced = dict(ffn_w_gate=swap(jnp.stack([t[:, 0] for t in gu_fin])),
                   ffn_w_up=swap(jnp.stack([t[:, 1] for t in gu_fin])), ffn_w_down=jnp.stack(dn_fin))
    for n, t in zip(['mla_w_in', 'mla_w_q_up', 'mla_w_kv_up', 'mla_w_o', 'dil_w_in', 'dil_w_o'], fin[2:6] + fin[8:]):
        reduced[n] = t.reshape(given[n].shape)

    grads = dict(norm_pre=g_norm_pre, norm_post=g_norm_post, w_mod=g_w_mod, b_mod=g_b_mod, mla_q_norm=g_q_norm,
                 mla_kv_norm=g_kv_norm, rel_bias=g_rel_bias, **reduced)

    deltas, new_m, new_v = {}, {}, {}
    for n in WEIGHTS:
        view = swap if n in ('ffn_w_gate', 'ffn_w_up') else (lambda t: t)
        outs = _adamw(view(given[n]), view(grads[n]), view(given["m_" + n]), view(given["v_" + n]), name=f"adamw_{n}")
        deltas[n], new_m[n], new_v[n] = (view(t) for t in outs)
    return (loss, grad_x, *[grads[n] for n in WEIGHTS], *[deltas[n] for n in WEIGHTS],
            *[new_m[n] for n in WEIGHTS], *[new_v[n] for n in WEIGHTS])
```

```python
import math

import jax
import jax.numpy as jnp
from jax import lax
from jax.experimental import pallas as pl
from jax.experimental.pallas import tpu as pltpu

F32 = jnp.float32
BF16 = jnp.bfloat16
MESH = pl.DeviceIdType.MESH

SEQ = 2048
D_MODEL = 1024
D_FF = 2816
N_SHARD = 4
F_SHARD = D_FF // N_SHARD
EPS = 1e-6
FFN_RES = 0.5
HEADS = 16
Q_LORA, KV_LORA, QK_NOPE, QK_ROPE, V_HEAD = 384, 256, 64, 32, 64
HALF_ROPE = QK_ROPE // 2
ROPE_THETA = 10000.0
DIL_GROUPS = ((128, 1), (512, 4), (2048, 16))
DIL_BLOCK = 128
N_BUCKETS = 32
MAX_DISTANCE = 2048
ADAM_LR, ADAM_B1, ADAM_B2, ADAM_EPS, ADAM_WD, ADAM_STEP = 0.001, 0.9, 0.999, 1e-08, 0.01, 10

VMEM_LIMIT = 48 * 1024 * 1024
SMALL_ROWS = 40

WEIGHTS = ['norm_pre', 'norm_post', 'w_mod', 'b_mod', 'ffn_w_gate', 'ffn_w_up', 'ffn_w_down', 'mla_w_in',
           'mla_q_norm', 'mla_w_q_up', 'mla_kv_norm', 'mla_w_kv_up', 'mla_w_o', 'dil_w_in', 'dil_w_o', 'rel_bias']


def _cparams(**kw):
    return pltpu.CompilerParams(vmem_limit_bytes=VMEM_LIMIT, **kw)


def _pick(n, cap, mult=128):
    if n <= cap:
        return n
    best = n
    for t in range(mult, cap + 1, mult):
        if n % t == 0:
            best = t
    return best


def _mm(a, b, *, name, ta=False, tb=False, reduce_g=False, bias=None, out_dtype=F32, tm_cap=512, tn_cap=1024,
        g_n=None, b_sel=None, out_shape=None, out_sel=None, out_buf=None):
    a3 = a if a.ndim == 3 else a[None]
    ga = a3.shape[0]
    if b_sel is None:
        b_n = b if b.ndim == 3 else b[None]
        gb = b_n.shape[0]
        b_sel = (lambda g: (g,)) if gb > 1 else (lambda g: (0,))
        g_n = max(ga, gb)
    else:
        b_n = b
    k_dim, m_dim = (a3.shape[1], a3.shape[2]) if ta else (a3.shape[2], a3.shape[1])
    k2, n_dim = (b_n.shape[-1], b_n.shape[-2]) if tb else (b_n.shape[-2], b_n.shape[-1])
    assert k_dim == k2, (a.shape, b.shape)
    tm = _pick(m_dim, tm_cap, 128 if ta else 8)
    tn = _pick(n_dim, tn_cap, 128)
    mt, nt = m_dim // tm, n_dim // tn
    dims = (((0 if ta else 1,), (1 if tb else 0,)), ((), ()))

    if reduce_g:
        grid = (mt, nt, g_n)
        ids = lambda i, j, g: (g, i, j)
    else:
        grid = (g_n, mt, nt)
        ids = lambda g, i, j: (g, i, j)

    def a_map(*p):
        g, i, j = ids(*p)
        g = g if ga > 1 else 0
        return (g, 0, i) if ta else (g, i, 0)

    def b_map(*p):
        g, i, j = ids(*p)
        return (*b_sel(g), j, 0) if tb else (*b_sel(g), 0, j)

    b_lead = (None,) * (b_n.ndim - 2)
    a_spec = pl.BlockSpec((None, k_dim, tm) if ta else (None, tm, k_dim), a_map)
    b_spec = pl.BlockSpec(b_lead + ((tn, k_dim) if tb else (k_dim, tn)), b_map)
    in_specs = [a_spec, b_spec]
    operands = [a3, b_n]
    if bias is not None:
        assert not reduce_g and bias.shape == (g_n, 1, n_dim)
        in_specs.append(pl.BlockSpec((None, 1, tn), lambda g, i, j: (g, 0, j)))
        operands.append(bias)
    aliases = {}
    if out_buf is not None:
        assert tuple(out_buf.shape) == tuple(out_shape) and out_buf.dtype == out_dtype
        in_specs.append(pl.BlockSpec(memory_space=pl.ANY))
        operands.append(out_buf)
        aliases = {len(operands) - 1: 0}

    if reduce_g:
        out_spec = pl.BlockSpec((tm, tn), lambda i, j, g: (i, j))
        out_sds = jax.ShapeDtypeStruct((m_dim, n_dim), F32)
    elif out_shape is not None:
        def o_map(g, i, j):
            lead, rb, cb = out_sel(g, i, j)
            return (*lead, rb, cb)

        out_spec = pl.BlockSpec((None,) * (len(out_shape) - 2) + (tm, tn), o_map)
        out_sds = jax.ShapeDtypeStruct(tuple(out_shape), out_dtype)
    else:
        out_spec = pl.BlockSpec((None, tm, tn), lambda g, i, j: (g, i, j))
        out_sds = jax.ShapeDtypeStruct((g_n, m_dim, n_dim), out_dtype)

    def body(a_ref, b_ref, *rest):
        o_ref = rest[-1]
        r = lax.dot_general(a_ref[...].astype(BF16), b_ref[...].astype(BF16), dims, preferred_element_type=F32)
        if bias is not None:
            r = r + rest[0][...]
        if reduce_g:
            g = pl.program_id(2)

            @pl.when(g == 0)
            def _():
                o_ref[...] = r

            @pl.when(g > 0)
            def _():
                o_ref[...] += r
        else:
            o_ref[...] = r.astype(o_ref.dtype)

    out = pl.pallas_call(body, grid=grid, in_specs=in_specs, out_specs=out_spec, out_shape=out_sds,
                         input_output_aliases=aliases, compiler_params=_cparams(), name=name)(*operands)
    if not reduce_g and out_shape is None and a.ndim == 2 and b.ndim == 2:
        out = out[0]
    return out


def _rows(tm, w):
    return pl.BlockSpec((tm, w), lambda i: (i, 0))


def _vec(w):
    return pl.BlockSpec((1, w), lambda i: (0, 0))


def _rstd(v):
    return lax.rsqrt(jnp.mean(v * v, axis=-1, keepdims=True) + EPS)


def _pre_fwd(x, pg, sc, sh, *, name):
    s_n, w = x.shape
    tm = _pick(s_n, 256, 8)

    def body(x_ref, pg_ref, sc_ref, sh_ref, o_ref):
        xv = x_ref[...]
        n = (xv * _rstd(xv)) * pg_ref[...]
        o_ref[...] = (n * (1.0 + sc_ref[...]) + sh_ref[...]).astype(o_ref.dtype)

    return pl.pallas_call(body, grid=(s_n // tm,), in_specs=[_rows(tm, w), _vec(w), _vec(w), _vec(w)],
                          out_specs=_rows(tm, w), out_shape=jax.ShapeDtypeStruct((s_n, w), BF16),
                          compiler_params=_cparams(), name=name)(x, pg, sc, sh)


def _post_fwd(f, x, qg, gate, res_w, *, name):
    s_n, w = x.shape
    tm = _pick(s_n, 256, 8)

    def body(f_ref, x_ref, qg_ref, gate_ref, o_ref):
        fv = f_ref[...]
        y = (fv * _rstd(fv)) * qg_ref[...]
        o_ref[...] = x_ref[...] + (res_w * gate_ref[...]) * y

    return pl.pallas_call(body, grid=(s_n // tm,), in_specs=[_rows(tm, w), _rows(tm, w), _vec(w), _vec(w)],
                          out_specs=_rows(tm, w), out_shape=jax.ShapeDtypeStruct((s_n, w), F32),
                          compiler_params=_cparams(), name=name)(f, x, qg, gate)


def _post_bwd(dout, f, qg, gate, res_w, *, name):
    s_n, w = f.shape
    tm = _pick(s_n, 256, 8)

    def body(do_ref, f_ref, qg_ref, gate_ref, df_ref, dgate_ref, dqg_ref):
        @pl.when(pl.program_id(0) == 0)
        def _():
            dgate_ref[...] = jnp.zeros_like(dgate_ref)
            dqg_ref[...] = jnp.zeros_like(dqg_ref)

        do = do_ref[...]
        fv = f_ref[...]
        r = _rstd(fv)
        fh = fv * r
        qg_v = qg_ref[...]
        dgate_ref[...] += res_w * jnp.sum(do * (fh * qg_v), axis=0, keepdims=True)
        dy = do * (res_w * gate_ref[...])
        dqg_ref[...] += jnp.sum(dy * fh, axis=0, keepdims=True)
        dfh = dy * qg_v
        df = r * (dfh - fh * jnp.mean(dfh * fh, axis=-1, keepdims=True))
        df_ref[...] = df.astype(df_ref.dtype)

    return pl.pallas_call(
        body, grid=(s_n // tm,), in_specs=[_rows(tm, w), _rows(tm, w), _vec(w), _vec(w)],
        out_specs=[_rows(tm, w), _vec(w), _vec(w)],
        out_shape=[jax.ShapeDtypeStruct((s_n, w), BF16), jax.ShapeDtypeStruct((1, w), F32),
                   jax.ShapeDtypeStruct((1, w), F32)],
        compiler_params=_cparams(), name=name)(dout, f, qg, gate)


def _pre_bwd(dhn, x, dout, pg, sc, *, name):
    s_n, w = x.shape
    tm = _pick(s_n, 256, 8)

    def body(dhn_ref, x_ref, do_ref, pg_ref, sc_ref, dx_ref, dsh_ref, dsc_ref, dpg_ref):
        @pl.when(pl.program_id(0) == 0)
        def _():
            dsh_ref[...] = jnp.zeros_like(dsh_ref)
            dsc_ref[...] = jnp.zeros_like(dsc_ref)
            dpg_ref[...] = jnp.zeros_like(dpg_ref)

        dhn_v = dhn_ref[...]
        xv = x_ref[...]
        r = _rstd(xv)
        xh = xv * r
        pg_v = pg_ref[...]
        dsh_ref[...] += jnp.sum(dhn_v, axis=0, keepdims=True)
        dsc_ref[...] += jnp.sum(dhn_v * (xh * pg_v), axis=0, keepdims=True)
        dn = dhn_v * (1.0 + sc_ref[...])
        dpg_ref[...] += jnp.sum(dn * xh, axis=0, keepdims=True)
        dxh = dn * pg_v
        dx_ref[...] = do_ref[...] + r * (dxh - xh * jnp.mean(dxh * xh, axis=-1, keepdims=True))

    vec = jax.ShapeDtypeStruct((1, w), F32)
    return pl.pallas_call(
        body, grid=(s_n // tm,), in_specs=[_rows(tm, w), _rows(tm, w), _rows(tm, w), _vec(w), _vec(w)],
        out_specs=[_rows(tm, w), _vec(w), _vec(w), _vec(w)],
        out_shape=[jax.ShapeDtypeStruct((s_n, w), F32), vec, vec, vec],
        compiler_params=_cparams(), name=name)(dhn, x, dout, pg, sc)


def _rms_fwd(x, g, *, name):
    s_n, w = x.shape
    tm = _pick(s_n, 512, 8)

    def body(x_ref, g_ref, o_ref):
        xv = x_ref[...]
        o_ref[...] = ((xv * _rstd(xv)) * g_ref[...]).astype(o_ref.dtype)

    return pl.pallas_call(body, grid=(s_n // tm,), in_specs=[_rows(tm, w), _vec(w)], out_specs=_rows(tm, w),
                          out_shape=jax.ShapeDtypeStruct((s_n, w), BF16), compiler_params=_cparams(),
                          name=name)(x, g)


def _rms_bwd(dy, x, g, *, name):
    s_n, w = x.shape
    tm = _pick(s_n, 512, 8)

    def body(dy_ref, x_ref, g_ref, dx_ref, dg_ref):
        @pl.when(pl.program_id(0) == 0)
        def _():
            dg_ref[...] = jnp.zeros_like(dg_ref)

        dy_v = dy_ref[...]
        xv = x_ref[...]
        r = _rstd(xv)
        xh = xv * r
        dg_ref[...] += jnp.sum(dy_v * xh, axis=0, keepdims=True)
        dxh = dy_v * g_ref[...]
        dx_ref[...] = r * (dxh - xh * jnp.mean(dxh * xh, axis=-1, keepdims=True))

    return pl.pallas_call(
        body, grid=(s_n // tm,), in_specs=[_rows(tm, w), _rows(tm, w), _vec(w)],
        out_specs=[_rows(tm, w), _vec(w)],
        out_shape=[jax.ShapeDtypeStruct((s_n, w), F32), jax.ShapeDtypeStruct((1, w), F32)],
        compiler_params=_cparams(), name=name)(dy, x, g)


def _rope(a1, a2, cos, sin, *, name):
    s_n, w = a1.shape
    tm = _pick(s_n, 512, 8)

    def body(a1_ref, a2_ref, c_ref, s_ref, r1_ref, r2_ref):
        u, v, c_v, s_v = a1_ref[...], a2_ref[...], c_ref[...], s_ref[...]
        r1_ref[...] = u * c_v - v * s_v
        r2_ref[...] = u * s_v + v * c_v

    sd = jax.ShapeDtypeStruct((s_n, w), F32)
    return pl.pallas_call(body, grid=(s_n // tm,), in_specs=[_rows(tm, w)] * 4, out_specs=[_rows(tm, w)] * 2,
                          out_shape=[sd, sd], compiler_params=_cparams(), name=name)(a1, a2, cos, sin)


def _silu_bf16(x, *, name):
    def body(x_ref, o_ref):
        xv = x_ref[...]
        o_ref[...] = (xv * jax.nn.sigmoid(xv)).astype(o_ref.dtype)

    return pl.pallas_call(body, out_shape=jax.ShapeDtypeStruct(x.shape, BF16), name=name)(x)


def _loss(y, target, *, name):
    s_n, w = y.shape
    tm = _pick(s_n, 256, 8)

    def body(y_ref, t_ref, dy_ref, l_ref):
        @pl.when(pl.program_id(0) == 0)
        def _():
            l_ref[...] = jnp.zeros_like(l_ref)

        e = y_ref[...] - t_ref[...]
        dy_ref[...] = e * (1.0 / w)
        row = jnp.mean(e * e, axis=-1, keepdims=True)
        l_ref[...] += 0.5 * jnp.sum(row, axis=0, keepdims=True)

    return pl.pallas_call(
        body, grid=(s_n // tm,), in_specs=[_rows(tm, w), _rows(tm, w)],
        out_specs=[_rows(tm, w), pl.BlockSpec((1, 1), lambda i: (0, 0))],
        out_shape=[jax.ShapeDtypeStruct((s_n, w), F32), jax.ShapeDtypeStruct((1, 1), F32)],
        compiler_params=_cparams(), name=name)(y, target)


FFN_TM = 512


def _ffn_up(hn, w_gu, *, name):
    s_n, d = hn.shape
    f = w_gu.shape[-1]
    tm = _pick(s_n, FFN_TM, 8)

    def body(hn_ref, wg_ref, wu_ref, gu_ref, a_ref):
        xv = hn_ref[...]
        g = jnp.dot(xv, wg_ref[...], preferred_element_type=F32)
        u = jnp.dot(xv, wu_ref[...], preferred_element_type=F32)
        gu_ref[0] = g.astype(BF16)
        gu_ref[1] = u.astype(BF16)
        a_ref[...] = ((g * jax.nn.sigmoid(g)) * u).astype(BF16)

    w_blk = lambda t: pl.BlockSpec((None, None, d, f), lambda s, m: (s, t, 0, 0))
    return pl.pallas_call(
        body, grid=(N_SHARD, s_n // tm),
        in_specs=[pl.BlockSpec((tm, d), lambda s, m: (m, 0)), w_blk(0), w_blk(1)],
        out_specs=[pl.BlockSpec((None, 2, tm, f), lambda s, m: (s, 0, m, 0)),
                   pl.BlockSpec((None, tm, f), lambda s, m: (s, m, 0))],
        out_shape=[jax.ShapeDtypeStruct((N_SHARD, 2, s_n, f), BF16), jax.ShapeDtypeStruct((N_SHARD, s_n, f), BF16)],
        compiler_params=_cparams(), name=name)(hn, w_gu, w_gu)


def _ffn_dgu(df, w_dn, gu, *, name):
    s_n, d = df.shape
    f = w_dn.shape[-2]
    tm = _pick(s_n, FFN_TM, 8)

    def body(df_ref, wd_ref, gu_ref, o_ref):
        da = lax.dot_general(df_ref[...], wd_ref[...], (((1,), (1,)), ((), ())), preferred_element_type=F32)
        g = gu_ref[0].astype(F32)
        u = gu_ref[1].astype(F32)
        sig = jax.nn.sigmoid(g)
        o_ref[0] = (da * u * (sig * (1.0 + g * (1.0 - sig)))).astype(BF16)
        o_ref[1] = (da * (g * sig)).astype(BF16)

    gu_blk = pl.BlockSpec((None, 2, tm, f), lambda s, m: (s, 0, m, 0))
    return pl.pallas_call(
        body, grid=(N_SHARD, s_n // tm),
        in_specs=[pl.BlockSpec((tm, d), lambda s, m: (m, 0)),
                  pl.BlockSpec((None, f, d), lambda s, m: (s, 0, 0)), gu_blk],
        out_specs=gu_blk, out_shape=jax.ShapeDtypeStruct((N_SHARD, 2, s_n, f), BF16),
        compiler_params=_cparams(), name=name)(df, w_dn, gu)


_NT = (((1,), (1,)), ((), ()))
_TN = (((0,), (0,)), ((), ()))
MLA_TQ = 256


def _causal_mask(i, tq, s_n):
    qpos = i * tq + lax.broadcasted_iota(jnp.int32, (tq, s_n), 0)
    kpos = lax.broadcasted_iota(jnp.int32, (tq, s_n), 1)
    return kpos <= qpos


def _mla_attn_fwd(q, k, v, *, name):
    h_n, s_n, dq = q.shape
    dv = v.shape[-1]
    tq = MLA_TQ
    scale = float(dq) ** -0.5

    def body(q_ref, k_ref, v_ref, o_ref, lse_ref):
        i = pl.program_id(1)
        for e in range(1, s_n // tq + 1):
            @pl.when(i == e - 1)
            def _(ext=e * tq):
                mask = _causal_mask(i, tq, ext)
                s = lax.dot_general(q_ref[...], k_ref[0:ext, :], _NT, preferred_element_type=F32) * scale
                s = jnp.where(mask, s, -jnp.inf)
                m = jnp.max(s, axis=-1, keepdims=True)
                p = jnp.exp(s - m)
                l = jnp.sum(p, axis=-1, keepdims=True)
                o = jnp.dot(p.astype(BF16), v_ref[0:ext, :], preferred_element_type=F32)
                o_ref[...] = o / l
                lse_ref[...] = m + jnp.log(l)

    return pl.pallas_call(
        body, grid=(h_n, s_n // tq),
        in_specs=[pl.BlockSpec((None, tq, dq), lambda h, i: (h, i, 0)),
                  pl.BlockSpec((None, s_n, dq), lambda h, i: (h, 0, 0)),
                  pl.BlockSpec((None, s_n, dv), lambda h, i: (h, 0, 0))],
        out_specs=[pl.BlockSpec((None, tq, dv), lambda h, i: (h, i, 0)),
                   pl.BlockSpec((None, tq, 1), lambda h, i: (h, i, 0))],
        out_shape=[jax.ShapeDtypeStruct((h_n, s_n, dv), F32), jax.ShapeDtypeStruct((h_n, s_n, 1), F32)],
        compiler_params=_cparams(), name=name)(q, k, v)


def _mla_attn_bwd(q, k, v, o, do, lse, *, name):
    h_n, s_n, dq = q.shape
    dv = v.shape[-1]
    tq = MLA_TQ
    scale = float(dq) ** -0.5

    def body(q_ref, k_ref, v_ref, o_ref, do_ref, lse_ref, dq_ref, dk_ref, dv_ref):
        i = pl.program_id(1)

        @pl.when(i == 0)
        def _():
            dk_ref[...] = jnp.zeros_like(dk_ref)
            dv_ref[...] = jnp.zeros_like(dv_ref)

        for e in range(1, s_n // tq + 1):
            @pl.when(i == e - 1)
            def _(ext=e * tq):
                mask = _causal_mask(i, tq, ext)
                qv, kv, vv = q_ref[...], k_ref[0:ext, :], v_ref[0:ext, :]
                do_v = do_ref[...]
                s = lax.dot_general(qv, kv, _NT, preferred_element_type=F32) * scale
                p = jnp.where(mask, jnp.exp(s - lse_ref[...]), 0.0)
                dob = do_v.astype(BF16)
                dv_ref[0:ext, :] += lax.dot_general(p.astype(BF16), dob, _TN, preferred_element_type=F32)
                dp = lax.dot_general(dob, vv, _NT, preferred_element_type=F32)
                delta = jnp.sum(do_v * o_ref[...], axis=-1, keepdims=True)
                dsb = (p * (dp - delta) * scale).astype(BF16)
                dq_ref[...] = jnp.dot(dsb, kv, preferred_element_type=F32)
                dk_ref[0:ext, :] += lax.dot_general(dsb, qv, _TN, preferred_element_type=F32)

    return pl.pallas_call(
        body, grid=(h_n, s_n // tq),
        in_specs=[pl.BlockSpec((None, tq, dq), lambda h, i: (h, i, 0)),
                  pl.BlockSpec((None, s_n, dq), lambda h, i: (h, 0, 0)),
                  pl.BlockSpec((None, s_n, dv), lambda h, i: (h, 0, 0)),
                  pl.BlockSpec((None, tq, dv), lambda h, i: (h, i, 0)),
                  pl.BlockSpec((None, tq, dv), lambda h, i: (h, i, 0)),
                  pl.BlockSpec((None, tq, 1), lambda h, i: (h, i, 0))],
        out_specs=[pl.BlockSpec((None, tq, dq), lambda h, i: (h, i, 0)),
                   pl.BlockSpec((None, s_n, dq), lambda h, i: (h, 0, 0)),
                   pl.BlockSpec((None, s_n, dv), lambda h, i: (h, 0, 0))],
        out_shape=[jax.ShapeDtypeStruct((h_n, s_n, dq), F32), jax.ShapeDtypeStruct((h_n, s_n, dq), F32),
                   jax.ShapeDtypeStruct((h_n, s_n, dv), F32)],
        compiler_params=_cparams(), name=name)(q, k, v, o, do, lse)


def _head_sum(x, *, name):
    h_n, s_n, w = x.shape
    tm = _pick(s_n, 256, 8)

    def body(x_ref, o_ref):
        o_ref[...] = jnp.sum(x_ref[...], axis=0)

    return pl.pallas_call(body, grid=(s_n // tm,), in_specs=[pl.BlockSpec((h_n, tm, w), lambda i: (0, i, 0))],
                          out_specs=_rows(tm, w), out_shape=jax.ShapeDtypeStruct((s_n, w), F32),
                          compiler_params=_cparams(), name=name)(x)


N_BLK = SEQ // DIL_BLOCK
DIL_SCALE = 64 ** -0.5


def _dil_masks():
    iq = lax.broadcasted_iota(jnp.int32, (DIL_BLOCK, 2 * DIL_BLOCK), 0)
    ik = lax.broadcasted_iota(jnp.int32, (DIL_BLOCK, 2 * DIL_BLOCK), 1)
    rel = DIL_BLOCK + iq - ik
    both = (rel >= 0) & (rel <= DIL_BLOCK)
    iq1 = lax.broadcasted_iota(jnp.int32, (DIL_BLOCK, DIL_BLOCK), 0)
    ik1 = lax.broadcasted_iota(jnp.int32, (DIL_BLOCK, DIL_BLOCK), 1)
    return both, ik1 <= iq1


def _dil_block(j, nb):
    lo = j * DIL_BLOCK
    first = j % nb == 0
    k_lo = lo if first else lo - DIL_BLOCK
    b_lo = DIL_BLOCK if first else 0
    return lo, k_lo, b_lo, first


def _dil_attn_fwd(q, k, v, bias, nb, *, name):
    h_n, s_n, e = q.shape

    def body(q_ref, k_ref, v_ref, b_ref, o_ref, lse_ref):
        m_both, m_first = _dil_masks()
        for j in range(N_BLK):
            lo, k_lo, b_lo, first = _dil_block(j, nb)
            qj = q_ref[lo:lo + DIL_BLOCK, :]
            kk = k_ref[k_lo:lo + DIL_BLOCK, :]
            vv = v_ref[k_lo:lo + DIL_BLOCK, :]
            s = lax.dot_general(qj, kk, _NT, preferred_element_type=F32) * DIL_SCALE + b_ref[:, b_lo:]
            s = jnp.where(m_first if first else m_both, s, -jnp.inf)
            m = jnp.max(s, axis=-1, keepdims=True)
            lse = m + jnp.log(jnp.sum(jnp.exp(s - m), axis=-1, keepdims=True))
            p = jnp.exp(s - lse)
            o_ref[lo:lo + DIL_BLOCK, :] = jnp.dot(p.astype(BF16), vv, preferred_element_type=F32)
            lse_ref[lo:lo + DIL_BLOCK, :] = lse

    head = lambda w: pl.BlockSpec((None, s_n, w), lambda h: (h, 0, 0))
    return pl.pallas_call(
        body, grid=(h_n,),
        in_specs=[head(e), head(e), head(e), pl.BlockSpec((None, DIL_BLOCK, 2 * DIL_BLOCK), lambda h: (h, 0, 0))],
        out_specs=[head(e), head(1)],
        out_shape=[jax.ShapeDtypeStruct((h_n, s_n, e), F32), jax.ShapeDtypeStruct((h_n, s_n, 1), F32)],
        compiler_params=_cparams(), name=name)(q, k, v, bias)


def _dil_attn_bwd(q, k, v, bias, lse, do, dlt, nb, *, name):
    h_n, s_n, e = q.shape

    def body(q_ref, k_ref, v_ref, b_ref, lse_ref, do_ref, dlt_ref, dq_ref, dk_ref, dv_ref, db_ref):
        dk_ref[...] = jnp.zeros_like(dk_ref)
        dv_ref[...] = jnp.zeros_like(dv_ref)
        db_ref[...] = jnp.zeros_like(db_ref)
        m_both, m_first = _dil_masks()
        for j in range(N_BLK):
            lo, k_lo, b_lo, first = _dil_block(j, nb)
            qj = q_ref[lo:lo + DIL_BLOCK, :]
            kk = k_ref[k_lo:lo + DIL_BLOCK, :]
            vv = v_ref[k_lo:lo + DIL_BLOCK, :]
            s = lax.dot_general(qj, kk, _NT, preferred_element_type=F32) * DIL_SCALE + b_ref[:, b_lo:]
            p = jnp.where(m_first if first else m_both, jnp.exp(s - lse_ref[lo:lo + DIL_BLOCK, :]), 0.0)
            dob = do_ref[lo:lo + DIL_BLOCK, :].astype(BF16)
            dv_ref[k_lo:lo + DIL_BLOCK, :] += lax.dot_general(p.astype(BF16), dob, _TN, preferred_element_type=F32)
            dp = lax.dot_general(dob, vv, _NT, preferred_element_type=F32)
            ds = p * (dp - dlt_ref[lo:lo + DIL_BLOCK, :])
            db_ref[:, b_lo:] += ds
            dsb = (ds * DIL_SCALE).astype(BF16)
            dq_ref[lo:lo + DIL_BLOCK, :] = jnp.dot(dsb, kk, preferred_element_type=F32)
            dk_ref[k_lo:lo + DIL_BLOCK, :] += lax.dot_general(dsb, qj, _TN, preferred_element_type=F32)

    head = lambda w: pl.BlockSpec((None, s_n, w), lambda h: (h, 0, 0))
    b_spec = pl.BlockSpec((None, DIL_BLOCK, 2 * DIL_BLOCK), lambda h: (h, 0, 0))
    sd = jax.ShapeDtypeStruct((h_n, s_n, e), F32)
    return pl.pallas_call(
        body, grid=(h_n,),
        in_specs=[head(e), head(e), head(e), b_spec, head(1), head(e), head(1)],
        out_specs=[head(e), head(e), head(e), b_spec],
        out_shape=[sd, sd, sd, jax.ShapeDtypeStruct((h_n, DIL_BLOCK, 2 * DIL_BLOCK), F32)],
        compiler_params=_cparams(), name=name)(q, k, v, bias, lse, do, dlt)


def _proj_heads(x, w, *, name):
    s_n, k = x.shape
    n = w.shape[-1]
    tm, tn, e = 512, 768, 64
    per_blk, n_blk = tn // e, n // tn

    def body(x_ref, w_ref, o_ref):
        r = jnp.dot(x_ref[...], w_ref[...], preferred_element_type=F32)
        for j in range(per_blk):
            o_ref[j] = r[:, e * j:e * (j + 1)].astype(BF16)

    return pl.pallas_call(
        body, grid=(w.shape[0], n_blk, s_n // tm),
        in_specs=[pl.BlockSpec((tm, k), lambda s, b, m: (m, 0)), pl.BlockSpec((None, k, tn), lambda s, b, m: (s, 0, b))],
        out_specs=pl.BlockSpec((per_blk, tm, e), lambda s, b, m: (s * n_blk + b, m, 0)),
        out_shape=jax.ShapeDtypeStruct((w.shape[0] * n // e, s_n, e), BF16), compiler_params=_cparams(),
        name=name)(x, w)


def _heads_cat(d_ref):
    return jnp.concatenate([d_ref[j] for j in range(d_ref.shape[0])], axis=1)


def _proj_heads_dw(x, dh, *, name):
    s_n, k = x.shape
    tn, e = 768, 64
    per_blk = tn // e
    n_blk = dh.shape[0] // N_SHARD // per_blk
    n = n_blk * tn

    def body(x_ref, d_ref, o_ref):
        o_ref[...] = lax.dot_general(x_ref[...], _heads_cat(d_ref), _TN, preferred_element_type=F32)

    return pl.pallas_call(
        body, grid=(N_SHARD, n_blk, 2),
        in_specs=[pl.BlockSpec((s_n, k // 2), lambda s, b, r: (0, r)),
                  pl.BlockSpec((per_blk, s_n, e), lambda s, b, r: (s * n_blk + b, 0, 0))],
        out_specs=pl.BlockSpec((None, None, k // 2, tn), lambda s, b, r: (r, s, 0, b)),
        out_shape=jax.ShapeDtypeStruct((2, N_SHARD, k // 2, n), F32), compiler_params=_cparams(), name=name)(x, dh)


def _proj_heads_dx(dh, w, *, name):
    k, n = w.shape[1:]
    s_n = dh.shape[1]
    tm, tn, e = 512, 768, 64
    per_blk, n_blk = tn // e, n // tn

    def body(d_ref, w_ref, o_ref):
        r = lax.dot_general(_heads_cat(d_ref), w_ref[...], _NT, preferred_element_type=F32)
        g = pl.program_id(1)

        @pl.when(g == 0)
        def _():
            o_ref[...] = r

        @pl.when(g > 0)
        def _():
            o_ref[...] += r

    return pl.pallas_call(
        body, grid=(s_n // tm, N_SHARD * n_blk),
        in_specs=[pl.BlockSpec((per_blk, tm, e), lambda m, g: (g, m, 0)),
                  pl.BlockSpec((None, k, tn), lambda m, g: (g // n_blk, 0, g % n_blk))],
        out_specs=pl.BlockSpec((tm, k), lambda m, g: (m, 0)),
        out_shape=jax.ShapeDtypeStruct((s_n, k), F32), compiler_params=_cparams(), name=name)(dh, w)


def _group_alpha(l_refs):
    ls = [r[...] for r in l_refs]
    m = jnp.maximum(jnp.maximum(ls[0], ls[1]), ls[2])
    es = [jnp.exp(l - m) for l in ls]
    tot = es[0] + es[1] + es[2]
    return [ex / tot for ex in es]


def _dil_mix_fwd(os_, ls_, *, name):
    h_n, s_n, e = os_[0].shape
    tm = 512

    def body(o0, o1, o2, l0, l1, l2, out_ref):
        al = _group_alpha((l0, l1, l2))
        out_ref[...] = al[0] * o0[...] + al[1] * o1[...] + al[2] * o2[...]

    blk = lambda w: pl.BlockSpec((None, tm, w), lambda h, i: (h, i, 0))
    return pl.pallas_call(body, grid=(h_n, s_n // tm), in_specs=[blk(e)] * 3 + [blk(1)] * 3, out_specs=blk(e),
                          out_shape=jax.ShapeDtypeStruct((h_n, s_n, e), F32), compiler_params=_cparams(),
                          name=name)(*os_, *ls_)


def _dil_mix_bwd(do, os_, ls_, *, name):
    h_n, s_n, e = do.shape
    tm = 512

    def body(do_ref, o0, o1, o2, l0, l1, l2, d0, d1, d2, t0, t1, t2):
        al = _group_alpha((l0, l1, l2))
        do_v = do_ref[...]
        mix = al[0] * o0[...] + al[1] * o1[...] + al[2] * o2[...]
        dbar = jnp.sum(do_v * mix, axis=-1, keepdims=True)
        for a_g, d_ref, t_ref in zip(al, (d0, d1, d2), (t0, t1, t2)):
            d_ref[...] = a_g * do_v
            t_ref[...] = a_g * dbar

    blk = lambda w: pl.BlockSpec((None, tm, w), lambda h, i: (h, i, 0))
    sd_e = jax.ShapeDtypeStruct((h_n, s_n, e), F32)
    sd_1 = jax.ShapeDtypeStruct((h_n, s_n, 1), F32)
    outs = pl.pallas_call(body, grid=(h_n, s_n // tm), in_specs=[blk(e)] * 4 + [blk(1)] * 3,
                          out_specs=[blk(e)] * 3 + [blk(1)] * 3, out_shape=[sd_e] * 3 + [sd_1] * 3,
                          compiler_params=_cparams(), name=name)(do, *os_, *ls_)
    return outs[:3], outs[3:]


def _bias_grad(ds, bucket, *, name):
    h_n = ds.shape[0]

    def body(ds_ref, bk_ref, o_ref):
        ds_v = ds_ref[...]
        bk = bk_ref[...]
        lane = lax.broadcasted_iota(jnp.int32, (1, N_BUCKETS), 1)
        acc = jnp.zeros((1, N_BUCKETS), F32)
        for b in range(N_BUCKETS):
            tot = jnp.sum(jnp.sum(jnp.where(bk == b, ds_v, 0.0), axis=1, keepdims=True), axis=0, keepdims=True)
            acc = acc + jnp.where(lane == b, tot, 0.0)
        o_ref[...] = acc

    return pl.pallas_call(
        body, grid=(h_n,),
        in_specs=[pl.BlockSpec((None, DIL_BLOCK, 2 * DIL_BLOCK), lambda h: (h, 0, 0)),
                  pl.BlockSpec((DIL_BLOCK, 2 * DIL_BLOCK), lambda h: (0, 0))],
        out_specs=pl.BlockSpec((None, 1, N_BUCKETS), lambda h: (h, 0, 0)),
        out_shape=jax.ShapeDtypeStruct((h_n, 1, N_BUCKETS), F32), compiler_params=_cparams(), name=name)(ds, bucket)


def _bias_table(rb, bucket, *, name):
    h_n = rb.shape[0]

    def body(rb_ref, bk_ref, o_ref):
        bk = bk_ref[...]
        row = rb_ref[...]
        acc = jnp.zeros(bk.shape, F32)
        for b in range(N_BUCKETS):
            acc = jnp.where(bk == b, row[:, b:b + 1], acc)
        o_ref[...] = acc

    return pl.pallas_call(
        body, grid=(h_n,),
        in_specs=[pl.BlockSpec((None, 1, N_BUCKETS), lambda h: (h, 0, 0)),
                  pl.BlockSpec((DIL_BLOCK, 2 * DIL_BLOCK), lambda h: (0, 0))],
        out_specs=pl.BlockSpec((None, DIL_BLOCK, 2 * DIL_BLOCK), lambda h: (h, 0, 0)),
        out_shape=jax.ShapeDtypeStruct((h_n, DIL_BLOCK, 2 * DIL_BLOCK), F32), compiler_params=_cparams(),
        name=name)(rb, bucket)


def _row_tile(rows, cols, budget=1 << 20):
    if rows * cols * 4 <= budget or rows % 8:
        return rows
    best = 8
    for t in range(8, rows + 1, 8):
        if rows % t == 0 and t * cols * 4 <= budget:
            best = t
    return best


def _adamw(w, g, m, v, *, name):
    shape = w.shape
    cols = shape[-1]
    rows = math.prod(shape[:-1]) if len(shape) > 1 else 1
    to2 = lambda t: t.reshape(rows, cols)
    tr = _row_tile(rows, cols)
    c1 = 1.0 / (1.0 - ADAM_B1 ** ADAM_STEP)
    c2 = 1.0 / (1.0 - ADAM_B2 ** ADAM_STEP)

    def body(w_ref, g_ref, m_ref, v_ref, d_ref, nm_ref, nv_ref):
        g_v = g_ref[...]
        nm = ADAM_B1 * m_ref[...] + (1.0 - ADAM_B1) * g_v
        nv = ADAM_B2 * v_ref[...] + (1.0 - ADAM_B2) * (g_v * g_v)
        m_hat = nm * c1
        v_hat = nv * c2
        d_ref[...] = -ADAM_LR * (m_hat / (jnp.sqrt(v_hat) + ADAM_EPS) + ADAM_WD * w_ref[...])
        nm_ref[...] = nm
        nv_ref[...] = nv

    blk = pl.BlockSpec((tr, cols), lambda i: (i, 0))
    sd = jax.ShapeDtypeStruct((rows, cols), F32)
    outs = pl.pallas_call(body, grid=(rows // tr,), in_specs=[blk] * 4, out_specs=[blk] * 3, out_shape=[sd] * 3,
                          compiler_params=_cparams(), name=name)(to2(w), to2(g), to2(m), to2(v))
    return tuple(t.reshape(shape) for t in outs)


def _add_half(unit, got, half_idx, *, name):
    rest = unit.shape[2:]
    c = rest[-1]
    r = math.prod(rest[:-1])
    tr = _row_tile(r, c)

    def body(idx_ref, u_ref, g_ref, o_ref, w_ref):
        tot = u_ref[...] + g_ref[...].astype(F32)
        o_ref[...] = tot
        w_ref[...] = tot.astype(BF16)

    blk = pl.BlockSpec((None, tr, c), lambda s, i, idx: (s, i, 0))
    grid_spec = pltpu.PrefetchScalarGridSpec(
        num_scalar_prefetch=1, grid=(N_SHARD, r // tr),
        in_specs=[pl.BlockSpec((None, None, tr, c), lambda s, i, idx: (idx[0], s, i, 0)), blk],
        out_specs=[blk, blk])
    out, wire = pl.pallas_call(
        body, grid_spec=grid_spec,
        out_shape=[jax.ShapeDtypeStruct((N_SHARD, r, c), F32), jax.ShapeDtypeStruct((N_SHARD, r, c), BF16)],
        compiler_params=_cparams(), name=name)(half_idx, unit.reshape(2, N_SHARD, r, c), got.reshape(N_SHARD, r, c))
    return out.reshape((N_SHARD,) + rest), wire.reshape((N_SHARD,) + rest)


def _add_shards(part, got, shard_idx, *, name):
    rest = part.shape[1:]
    c = rest[-1]
    r = math.prod(rest[:-1])
    tr = _row_tile(r, c)

    def body(idx_ref, p_ref, g_ref, o_ref):
        acc = p_ref[...]
        for k in range(3):
            acc = acc + g_ref[k].astype(F32)
        o_ref[...] = acc

    grid_spec = pltpu.PrefetchScalarGridSpec(
        num_scalar_prefetch=1, grid=(r // tr,),
        in_specs=[pl.BlockSpec((None, tr, c), lambda i, idx: (idx[0], i, 0)),
                  pl.BlockSpec((3, tr, c), lambda i, idx: (0, i, 0))],
        out_specs=pl.BlockSpec((tr, c), lambda i, idx: (i, 0)))
    out = pl.pallas_call(body, grid_spec=grid_spec, out_shape=jax.ShapeDtypeStruct((r, c), F32),
                         compiler_params=_cparams(), name=name)(
        shard_idx, part.reshape(N_SHARD, r, c), got.reshape(3, r, c))
    return out.reshape(rest)


def _sum_devices(x, n_dev, *, name):
    rows = x.shape[0] // n_dev

    def body(x_ref, o_ref):
        acc = x_ref[0:rows, :]
        for d in range(1, n_dev):
            acc = acc + x_ref[d * rows:(d + 1) * rows, :]
        o_ref[...] = acc

    return pl.pallas_call(body, out_shape=jax.ShapeDtypeStruct((rows, x.shape[1]), F32), name=name)(x)


def _my_pos():
    return lax.axis_index("x"), lax.axis_index("y"), lax.axis_index("c")


def _all_gather(x_blk, *, name, in_vmem):
    m_per, n = x_blk.shape

    def body(x_ref, out_ref, send_sems, recv_sems, local_sem):
        x, y, c = _my_pos()
        me, sibling = (x, y, c), (x, y, 1 - c)
        chips = [(1 - x, y), (x, 1 - y), (1 - x, 1 - y)]

        def rows(px, py, pc):
            return out_ref.at[pl.ds((4 * px + 2 * py + pc) * m_per, m_per), :]

        def copy(k, block, to, src=None):
            return pltpu.make_async_remote_copy(
                src_ref=rows(*block) if src is None else src, dst_ref=rows(*block),
                send_sem=send_sems.at[k], recv_sem=recv_sems.at[k], device_id=to, device_id_type=MESH)

        mine = pltpu.make_async_copy(x_ref, rows(*me), local_sem)
        mine.start()
        first = [copy(0, me, sibling, src=x_ref)]
        first += [copy(1 + j, me, (*chip, c), src=x_ref) for j, chip in enumerate(chips)]
        for cp in first:
            cp.start()
        passed = [copy(4 + j, (*chip, c), sibling) for j, chip in enumerate(chips)]
        for j, chip in enumerate(chips):
            copy(1 + j, (*chip, c), me).wait_recv()
            passed[j].start()
        copy(0, sibling, me).wait_recv()
        for j, chip in enumerate(chips):
            copy(4 + j, (*chip, 1 - c), me).wait_recv()
        for cp in first + passed:
            cp.wait_send()
        mine.wait()

    space = pltpu.VMEM if in_vmem else pl.ANY
    return pl.pallas_call(
        body, out_shape=jax.ShapeDtypeStruct((8 * m_per, n), x_blk.dtype),
        in_specs=[pl.BlockSpec(memory_space=space)], out_specs=pl.BlockSpec(memory_space=space),
        scratch_shapes=[pltpu.SemaphoreType.DMA((7,)), pltpu.SemaphoreType.DMA((7,)), pltpu.SemaphoreType.DMA],
        name=name)(x_blk)


_HBM = pl.BlockSpec(memory_space=pl.ANY)


def _gather_weights(fams, *, name):
    n = len(fams)

    def body(*refs):
        ins, outs = refs[:n], refs[n:2 * n]
        send_sems, recv_sems = refs[2 * n:]
        x, y, c = _my_pos()
        me, sibling = (x, y, c), (x, y, 1 - c)
        chips = [(1 - x, y), (x, 1 - y), (1 - x, 1 - y)]

        def copy(f, k, block, to, src=None):
            px, py, pc = block
            dst = outs[f].at[2 * px + py, pc]
            return pltpu.make_async_remote_copy(
                src_ref=dst if src is None else src, dst_ref=dst, send_sem=send_sems.at[7 * f + k],
                recv_sem=recv_sems.at[7 * f + k], device_id=to, device_id_type=MESH)

        first, passed = [], []
        for f in range(n):
            src = ins[f].at[c]
            first.append(copy(f, 0, me, sibling, src=src))
            first += [copy(f, 1 + j, me, (*chip, c), src=src) for j, chip in enumerate(chips)]
        for cp in first:
            cp.start()
        for j, chip in enumerate(chips):
            for f in range(n):
                copy(f, 1 + j, (*chip, c), me).wait_recv()
                passed.append(copy(f, 4 + j, (*chip, c), sibling))
                passed[-1].start()
        for f in range(n):
            copy(f, 0, sibling, me).wait_recv()
        for j, chip in enumerate(chips):
            for f in range(n):
                copy(f, 4 + j, (*chip, 1 - c), me).wait_recv()
        for cp in first + passed:
            cp.wait_send()

    outs = pl.pallas_call(
        body, out_shape=[jax.ShapeDtypeStruct((N_SHARD,) + t.shape, t.dtype) for t in fams],
        in_specs=[_HBM] * n, out_specs=[_HBM] * n,
        scratch_shapes=[pltpu.SemaphoreType.DMA((7 * n,)), pltpu.SemaphoreType.DMA((7 * n,))], name=name)(*fams)
    return [_place_own(o, t) for o, t in zip(outs, fams)]


def _swap_halves(units, *, name):
    n = len(units)

    def body(*refs):
        ins, outs = refs[:n], refs[n:2 * n]
        send_sems, recv_sems = refs[2 * n:]
        x, y, c = _my_pos()
        cps = [pltpu.make_async_remote_copy(src_ref=ins[f].at[1 - c], dst_ref=outs[f], send_sem=send_sems.at[f],
                                            recv_sem=recv_sems.at[f], device_id=(x, y, 1 - c), device_id_type=MESH)
               for f in range(n)]
        for cp in cps:
            cp.start()
        for cp in cps:
            cp.wait()

    return pl.pallas_call(
        body, out_shape=[jax.ShapeDtypeStruct(t.shape[1:], t.dtype) for t in units],
        in_specs=[_HBM] * n, out_specs=[_HBM] * n,
        scratch_shapes=[pltpu.SemaphoreType.DMA((n,)), pltpu.SemaphoreType.DMA((n,))], name=name)(*units)


def _send_to_chips(parts, *, name):
    n = len(parts)

    def body(*refs):
        ins, outs = refs[:n], refs[n:2 * n]
        send_sems, recv_sems = refs[2 * n:]
        x, y, c = _my_pos()
        chips = [(1 - x, y), (x, 1 - y), (1 - x, 1 - y)]
        cps = [pltpu.make_async_remote_copy(src_ref=ins[f].at[2 * cx + cy], dst_ref=outs[f].at[k],
                                            send_sem=send_sems.at[3 * f + k], recv_sem=recv_sems.at[3 * f + k],
                                            device_id=(cx, cy, c), device_id_type=MESH)
               for f in range(n) for k, (cx, cy) in enumerate(chips)]
        for cp in cps:
            cp.start()
        for cp in cps:
            cp.wait()

    return pl.pallas_call(
        body, out_shape=[jax.ShapeDtypeStruct((3,) + t.shape[1:], t.dtype) for t in parts],
        in_specs=[_HBM] * n, out_specs=[_HBM] * n,
        scratch_shapes=[pltpu.SemaphoreType.DMA((3 * n,)), pltpu.SemaphoreType.DMA((3 * n,))], name=name)(*parts)


def _pair_gather(halves, *, name):
    n = len(halves)

    def body(*refs):
        ins, outs = refs[:n], refs[n:2 * n]
        send_sems, recv_sems = refs[2 * n:]
        x, y, c = _my_pos()
        cps = [pltpu.make_async_remote_copy(src_ref=ins[f], dst_ref=outs[f].at[c], send_sem=send_sems.at[f],
                                            recv_sem=recv_sems.at[f], device_id=(x, y, 1 - c), device_id_type=MESH)
               for f in range(n)]
        for cp in cps:
            cp.start()
        for f in range(n):
            pltpu.make_async_remote_copy(src_ref=ins[f], dst_ref=outs[f].at[1 - c], send_sem=send_sems.at[f],
                                         recv_sem=recv_sems.at[f], device_id=(x, y, 1 - c),
                                         device_id_type=MESH).wait_recv()
        for cp in cps:
            cp.wait_send()

    outs = pl.pallas_call(
        body, out_shape=[jax.ShapeDtypeStruct((2,) + t.shape, t.dtype) for t in halves],
        in_specs=[_HBM] * n, out_specs=[_HBM] * n,
        scratch_shapes=[pltpu.SemaphoreType.DMA((n,)), pltpu.SemaphoreType.DMA((n,))], name=name)(*halves)
    c = lax.axis_index("c")
    return [lax.dynamic_update_index_in_dim(o, t, c, 0) for o, t in zip(outs, halves)]


_HBM_ONLY = pl.BlockSpec(memory_space=pltpu.HBM)
_SEMS = pl.BlockSpec(memory_space=pltpu.SEMAPHORE)
_EFFECT = pltpu.SideEffectType.DATAFLOW_SIDE_EFFECTING


def _copies_start(srcs, lands, plan, n_copies, *, name):
    n, m = len(srcs), len(lands)

    def body(*refs):
        src_refs, land_refs = refs[:n], refs[n:n + m]
        send_sems, recv_sems, token = refs[n + m], refs[n + m + 1], refs[-1]
        for k, (src, dst, peer) in enumerate(plan(src_refs, land_refs)):
            pltpu.make_async_remote_copy(src_ref=src, dst_ref=dst, send_sem=send_sems.at[k], recv_sem=recv_sems.at[k],
                                         device_id=peer, device_id_type=MESH).start()
        token[...] = jnp.zeros_like(token)

    bufs = [pltpu.with_memory_space_constraint(t, pltpu.HBM) for t in (*srcs, *lands)]
    outs = pl.pallas_call(
        body, name=name,
        out_shape=(pltpu.SemaphoreType.DMA((n_copies,)), pltpu.SemaphoreType.DMA((n_copies,)),
                   *[pltpu.HBM(t.shape, t.dtype) for t in bufs], jax.ShapeDtypeStruct((8, 128), F32)),
        in_specs=[_HBM_ONLY] * (n + m),
        out_specs=(_SEMS, _SEMS, *[_HBM_ONLY] * (n + m), pl.BlockSpec(memory_space=pltpu.VMEM)),
        input_output_aliases={k: 2 + k for k in range(n + m)},
        compiler_params=pltpu.CompilerParams(has_side_effects=_EFFECT))(*bufs)
    return outs[0], outs[1], list(outs[2:2 + n + m]), outs[-1]


def _copies_wait(send_sems, recv_sems, thru, n_src, plan, after, *, name):
    nm = len(thru)

    def body(*refs):
        t_refs, send, recv = refs[:nm], refs[nm], refs[nm + 1]
        for k, (src, dst, peer) in enumerate(plan(t_refs[:n_src], t_refs[n_src:])):
            cp = pltpu.make_async_remote_copy(src_ref=src, dst_ref=dst, send_sem=send.at[k], recv_sem=recv.at[k],
                                              device_id=peer, device_id_type=MESH)
            cp.wait_send()
            cp.wait_recv()

    outs = pl.pallas_call(
        body, name=name, out_shape=tuple(pltpu.HBM(t.shape, t.dtype) for t in thru),
        in_specs=[_HBM_ONLY] * nm + [_SEMS, _SEMS, pl.BlockSpec(memory_space=pl.ANY)],
        out_specs=tuple([_HBM_ONLY] * nm), input_output_aliases={k: k for k in range(nm)},
        compiler_params=pltpu.CompilerParams(has_side_effects=_EFFECT))(*thru, send_sems, recv_sems, after)
    return list(outs)


_RELATIONS = [(dx, dy, dc) for dx in (0, 1) for dy in (0, 1) for dc in (0, 1)][1:]


def _gather_plan(src_refs, land_refs):
    x, y, c = _my_pos()
    flip = lambda v, d: 1 - v if d else v
    return [(s_ref.at[c], l_ref.at[2 * x + y, c], (flip(x, dx), flip(y, dy), flip(c, dc)))
            for s_ref, l_ref in zip(src_refs, land_refs) for dx, dy, dc in _RELATIONS]


def _sibling_plan(src_refs, land_refs):
    x, y, c = _my_pos()
    return [(s_ref.at[1 - c], l_ref, (x, y, 1 - c)) for s_ref, l_ref in zip(src_refs, land_refs)]


def _chips_plan(src_refs, land_refs):
    x, y, c = _my_pos()
    chips = [(1 - x, y), (x, 1 - y), (1 - x, 1 - y)]
    return [(s_ref.at[2 * cx + cy], l_ref.at[k], (cx, cy, c))
            for s_ref, l_ref in zip(src_refs, land_refs) for k, (cx, cy) in enumerate(chips)]


def _place_own(gathered, fam):
    x, y, c = _my_pos()
    own = lax.dynamic_index_in_dim(fam, c, 0, keepdims=True)[None]
    return lax.dynamic_update_slice(gathered, own, (2 * x + y, c) + (0,) * (fam.ndim - 1))


def _to_heads(t, width):
    return t.reshape(t.shape[0], HEADS, width).transpose(1, 0, 2)


def _from_heads(t):
    return t.transpose(1, 0, 2).reshape(t.shape[1], -1)


def _residue_major(t, d):
    h_n, s_n, e = t.shape
    return t.reshape(h_n, s_n // d, d, e).transpose(0, 2, 1, 3).reshape(h_n, s_n, e)


def _token_major(t, d):
    h_n, s_n, e = t.shape
    return t.reshape(h_n, d, s_n // d, e).transpose(0, 2, 1, 3).reshape(h_n, s_n, e)


def _t5_bucket(dist):
    max_exact = N_BUCKETS // 2
    d = jnp.maximum(dist, 1).astype(F32)
    large = max_exact + (jnp.log(d / max_exact) / math.log(MAX_DISTANCE / max_exact)
                         * (N_BUCKETS - max_exact)).astype(jnp.int32)
    large = jnp.minimum(large, N_BUCKETS - 1)
    return jnp.where(dist < max_exact, dist, large)


def _bucket_map(dilation):
    iq = jnp.arange(DIL_BLOCK)[:, None]
    ik = jnp.arange(2 * DIL_BLOCK)[None, :]
    rel = DIL_BLOCK + iq - ik
    return _t5_bucket(jnp.maximum(rel, 0) * dilation).astype(jnp.int32)


def _q_perm(w):
    w3 = w.reshape(w.shape[0], HEADS, QK_NOPE + QK_ROPE)
    return jnp.concatenate([w3[:, :, :QK_NOPE].reshape(w.shape[0], -1),
                            w3[:, :, QK_NOPE:QK_NOPE + HALF_ROPE].reshape(w.shape[0], -1),
                            w3[:, :, QK_NOPE + HALF_ROPE:].reshape(w.shape[0], -1)], axis=1)


def _q_unperm(w):
    n0, n1 = HEADS * QK_NOPE, HEADS * HALF_ROPE
    r = w.shape[0]
    return jnp.concatenate([w[:, :n0].reshape(r, HEADS, QK_NOPE), w[:, n0:n0 + n1].reshape(r, HEADS, HALF_ROPE),
                            w[:, n0 + n1:].reshape(r, HEADS, HALF_ROPE)], axis=2).reshape(r, -1)


def _kv_perm(w):
    w3 = w.reshape(w.shape[0], HEADS, QK_NOPE + V_HEAD)
    return jnp.concatenate([w3[:, :, :QK_NOPE].reshape(w.shape[0], -1), w3[:, :, QK_NOPE:].reshape(w.shape[0], -1)],
                           axis=1)


def _kv_unperm(w):
    n0 = HEADS * QK_NOPE
    r = w.shape[0]
    return jnp.concatenate([w[:, :n0].reshape(r, HEADS, QK_NOPE), w[:, n0:].reshape(r, HEADS, V_HEAD)],
                           axis=2).reshape(r, -1)


def _row(v):
    return v.reshape(1, -1)


def kernel(x, c, norm_pre, norm_post, w_mod, b_mod, ffn_w_gate, ffn_w_up, ffn_w_down, mla_w_in, mla_q_norm, mla_w_q_up, mla_kv_norm, mla_w_kv_up, mla_w_o, dil_w_in, dil_w_o, rel_bias, loss_target, m_norm_pre, m_norm_post, m_w_mod, m_b_mod, m_ffn_w_gate, m_ffn_w_up, m_ffn_w_down, m_mla_w_in, m_mla_q_norm, m_mla_w_q_up, m_mla_kv_norm, m_mla_w_kv_up, m_mla_w_o, m_dil_w_in, m_dil_w_o, m_rel_bias, v_norm_pre, v_norm_post, v_w_mod, v_b_mod, v_ffn_w_gate, v_ffn_w_up, v_ffn_w_down, v_mla_w_in, v_mla_q_norm, v_mla_w_q_up, v_mla_kv_norm, v_mla_w_kv_up, v_mla_w_o, v_dil_w_in, v_dil_w_o, v_rel_bias):
    given = dict(locals())
    ix, iy, ic = _my_pos()
    shard_id = 2 * ix + iy
    dev_id = 4 * ix + 2 * iy + ic
    x2 = x[0]
    target = loss_target[0]
    half_idx = jnp.reshape(ic, (1,)).astype(jnp.int32)
    shard_idx = jnp.reshape(shard_id, (1,)).astype(jnp.int32)

    blk = jnp.zeros((8, D_MODEL), F32)
    blk = blk.at[0].set(c[0])
    blk = blk.at[1:3].set(jnp.pad(norm_pre.reshape(-1), (0, 512)).reshape(2, D_MODEL))
    blk = blk.at[3:5].set(jnp.pad(norm_post.reshape(-1), (0, 512)).reshape(2, D_MODEL))
    got = _all_gather(blk, name="ag_c_norms", in_vmem=True).reshape(N_SHARD, 2, 8, D_MODEL)
    c_all = got[:, :, 0, :].reshape(8, D_MODEL)

    def full_norm(lo):
        t = got[:, 0, lo:lo + 2, :].reshape(N_SHARD, 2 * D_MODEL)[:, :1536].reshape(N_SHARD, 2, 3, 256)
        return t.transpose(1, 2, 0, 3).reshape(2, 3, D_MODEL)

    pre_full, post_full = full_norm(1), full_norm(3)

    silu_c = _silu_bf16(c_all, name="silu_c")
    b_cols = lax.dynamic_slice_in_dim(b_mod, shard_id * 2304, 2304, axis=1).reshape(2, 1, 2304)
    mod_part = _mm(silu_c, w_mod, bias=b_cols, name="mod_mm", tn_cap=768)
    mod_all = _all_gather(mod_part.reshape(16, 2304), name="ag_mod", in_vmem=True)
    mod_all = mod_all.reshape(N_SHARD, 2, 2, 8, 2304)[:, 0]
    mod_mine = lax.dynamic_index_in_dim(mod_all, dev_id, axis=2, keepdims=False)
    mod = mod_mine.transpose(1, 0, 2).reshape(2, 9, D_MODEL)

    bf = lambda t: t.astype(BF16)
    ffn_fam = lambda i, h: [bf(jnp.stack([ffn_w_gate[i, h], ffn_w_up[i, h]])),
                            bf(ffn_w_down[i, h].reshape(2, F_SHARD // 2, D_MODEL))]
    mla_fam = [bf(mla_w_in.reshape(2, 128, -1)), bf(mla_w_q_up.reshape(2, 192, -1)),
               bf(mla_w_kv_up.reshape(2, 128, -1)), bf(mla_w_o.reshape(2, 128, D_MODEL))]
    dil_fam = [bf(dil_w_in.reshape(2, 512, -1)), bf(dil_w_o.reshape(2, 128, D_MODEL))]
    later_fams = [ffn_fam(1, 0) + dil_fam, ffn_fam(1, 1)]
    full, later_fams, mod = lax.optimization_barrier(
        (_gather_weights(ffn_fam(0, 0) + mla_fam + ffn_fam(0, 1), name="ag_weights_first"), later_fams, mod))

    def gather_later(fams, tag):
        lands = [lax.empty((N_SHARD,) + t.shape, t.dtype) for t in fams]
        send, recv, thru, token = _copies_start(fams, lands, _gather_plan, 7 * len(fams), name=f"ag_start_{tag}")
        return dict(send=send, recv=recv, thru=thru, token=token, n=len(fams), tag=tag)

    def arrive(st, after):
        thru = _copies_wait(st['send'], st['recv'], st['thru'], st['n'], _gather_plan, after,
                            name=f"ag_wait_{st['tag']}")
        return [_place_own(o, t) for t, o in zip(thru[:st['n']], thru[st['n']:])]

    in_flight = gather_later(later_fams[0], "l1s01")
    as_ffn = lambda w_gu, w_dn: (w_gu, w_dn.reshape(N_SHARD, F_SHARD, D_MODEL))
    ffn_w = {(0, 0): as_ffn(full[0], full[1]), (0, 1): as_ffn(full[6], full[7])}
    w_in = full[2].reshape(D_MODEL, -1)
    wq_p = _q_perm(full[3].reshape(N_SHARD, Q_LORA, -1).transpose(1, 0, 2).reshape(Q_LORA, -1))
    wkv_p = _kv_perm(full[4].reshape(N_SHARD, KV_LORA, -1).transpose(1, 0, 2).reshape(KV_LORA, -1))
    w_mo = full[5].reshape(D_MODEL, D_MODEL)
    dil_w = {}

    pos = jnp.arange(SEQ, dtype=F32)
    freqs = ROPE_THETA ** (-jnp.arange(HALF_ROPE, dtype=F32) / HALF_ROPE)
    ang = pos[:, None] * freqs[None, :]
    cos_k, sin_k = jnp.cos(ang), jnp.sin(ang)
    cos_q, sin_q = jnp.tile(cos_k, (1, HEADS)), jnp.tile(sin_k, (1, HEADS))

    buckets = [_bucket_map(d) for _, d in DIL_GROUPS]
    biases = [_bias_table(rel_bias[:, g * HEADS:(g + 1) * HEADS].T.reshape(HEADS, 1, N_BUCKETS), bk,
                          name=f"dil_bias_table_g{g}") for g, bk in enumerate(buckets)]

    def sub_params(i, sub):
        return dict(pg=_row(pre_full[i, sub]), qg=_row(post_full[i, sub]), sh=_row(mod[i, 3 * sub]),
                    sc=_row(mod[i, 3 * sub + 1]), gate=_row(mod[i, 3 * sub + 2]))

    def ffn_fwd(xin, i, h, sub, tie=None):
        p = sub_params(i, sub)
        if tie is not None:
            p['sh'] = p['sh'] + tie
        tag = f"l{i}s{sub}"
        w_gu, w_dn = ffn_w[i, h]
        hn = _pre_fwd(xin, p['pg'], p['sc'], p['sh'], name=f"pre_fwd_{tag}")
        gu, a = _ffn_up(hn, w_gu, name=f"ffn_up_{tag}")
        f = _mm(a, w_dn, reduce_g=True, name=f"ffn_down_{tag}")
        out = _post_fwd(f, xin, p['qg'], p['gate'], FFN_RES, name=f"post_fwd_{tag}")
        return out, dict(x=xin, hn=hn, gu=gu, a=a, f=f, p=p, i=i, h=h, tag=tag)

    def mla_fwd(xin, i, sub):
        p = sub_params(i, sub)
        tag = f"l{i}s{sub}"
        hn = _pre_fwd(xin, p['pg'], p['sc'], p['sh'], name=f"pre_fwd_{tag}")
        lat = _mm(hn, w_in, name="mla_lat")
        cq, ckv = lat[:, :Q_LORA], lat[:, Q_LORA:Q_LORA + KV_LORA]
        k1, k2 = lat[:, Q_LORA + KV_LORA:Q_LORA + KV_LORA + HALF_ROPE], lat[:, Q_LORA + KV_LORA + HALF_ROPE:]
        cqn = _rms_fwd(cq, mla_q_norm, name="mla_qnorm")
        ckvn = _rms_fwd(ckv, mla_kv_norm, name="mla_kvnorm")
        qp = _mm(cqn, wq_p, name="mla_q_up")
        kvp = _mm(ckvn, wkv_p, name="mla_kv_up")
        n0, n1 = HEADS * QK_NOPE, HEADS * HALF_ROPE
        qr1, qr2 = _rope(qp[:, n0:n0 + n1], qp[:, n0 + n1:], cos_q, sin_q, name="rope_q")
        kr1, kr2 = _rope(k1, k2, cos_k, sin_k, name="rope_k")
        q = jnp.concatenate([qp[:, :n0].reshape(SEQ, HEADS, QK_NOPE), qr1.reshape(SEQ, HEADS, HALF_ROPE),
                             qr2.reshape(SEQ, HEADS, HALF_ROPE)], axis=2).transpose(1, 0, 2).astype(BF16)
        kr = jnp.broadcast_to(jnp.concatenate([kr1, kr2], axis=1)[:, None, :], (SEQ, HEADS, QK_ROPE))
        k = jnp.concatenate([kvp[:, :n0].reshape(SEQ, HEADS, QK_NOPE), kr], axis=2).transpose(1, 0, 2).astype(BF16)
        v = _to_heads(kvp[:, n0:], V_HEAD).astype(BF16)
        o, lse = _mla_attn_fwd(q, k, v, name="mla_attn_fwd")
        o_flat = _from_heads(o).astype(BF16)
        f = _mm(o_flat, w_mo, name="mla_out")
        out = _post_fwd(f, xin, p['qg'], p['gate'], 1.0, name=f"post_fwd_{tag}")
        return out, dict(x=xin, hn=hn, cq=cq, ckv=ckv, cqn=cqn, ckvn=ckvn, q=q, k=k, v=v, o=o, lse=lse,
                         o_flat=o_flat, f=f, p=p, tag=tag)

    def dil_fwd(xin, i, sub):
        p = sub_params(i, sub)
        tag = f"l{i}s{sub}"
        hn = _pre_fwd(xin, p['pg'], p['sc'], p['sh'], name=f"pre_fwd_{tag}")
        heads = _proj_heads(hn, dil_w['in'], name="dil_proj").reshape(3, 3, HEADS, SEQ, 64)
        qkv, outs, lses = [], [], []
        for g, (window, d) in enumerate(DIL_GROUPS):
            q, k, v = (_residue_major(heads[g, t], d) for t in range(3))
            o, lse = _dil_attn_fwd(q, k, v, biases[g], SEQ // d // DIL_BLOCK, name=f"dil_attn_fwd_g{g}")
            qkv.append((q, k, v))
            outs.append(_token_major(o, d))
            lses.append(_token_major(lse, d))
        mix = _dil_mix_fwd(outs, lses, name="dil_mix_fwd")
        o_flat = _from_heads(mix).astype(BF16)
        f = _mm(o_flat, dil_w['out'], name="dil_out")
        out = _post_fwd(f, xin, p['qg'], p['gate'], 1.0, name=f"post_fwd_{tag}")
        return out, dict(x=xin, hn=hn, qkv=qkv, outs=outs, lses=lses, o_flat=o_flat, f=f, p=p, tag=tag)

    saved = [None] * 6
    xs, saved[0] = ffn_fwd(x2, 0, 0, 0, tie=in_flight['token'][0, 0])
    xs, saved[1] = mla_fwd(xs, 0, 1)
    xs, saved[2] = ffn_fwd(xs, 0, 1, 2)
    got, last_fams = lax.optimization_barrier((arrive(in_flight, xs), later_fams[1]))
    ffn_w[1, 0] = as_ffn(got[0], got[1])
    dil_w['in'], dil_w['out'] = got[2].reshape(N_SHARD, D_MODEL, -1), got[3].reshape(D_MODEL, D_MODEL)
    in_flight = gather_later(last_fams, "l1s2")
    xs, saved[3] = ffn_fwd(xs, 1, 0, 0, tie=in_flight['token'][0, 0])
    xs, saved[4] = dil_fwd(xs, 1, 1)
    ffn_w[1, 1] = as_ffn(*arrive(in_flight, xs))
    xs, saved[5] = ffn_fwd(xs, 1, 1, 2)

    dx, loss_part = _loss(xs, target, name="loss")

    dmod = [[None] * 9 for _ in range(2)]
    dpre = [[None] * 3 for _ in range(2)]
    dpost = [[None] * 3 for _ in range(2)]
    ffn_units = {}
    row_unit = lambda g, r, j: ((r % 2, r // 2), 0, j)

    def close_sub(dhn, dout, sv, i, sub, res_dgate, res_dqg):
        p = sv['p']
        dxs, dsh, dsc, dpg = _pre_bwd(dhn, sv['x'], dout, p['pg'], p['sc'], name=f"pre_bwd_{sv['tag']}")
        dmod[i][3 * sub], dmod[i][3 * sub + 1], dmod[i][3 * sub + 2] = dsh, dsc, res_dgate
        dpre[i][sub], dpost[i][sub] = dpg, res_dqg
        return dxs

    def ffn_bwd(dout, sv, sub, tie=0.0):
        i, h, p, tag = sv['i'], sv['h'], sv['p'], sv['tag']
        w_gu, w_dn = ffn_w[i, h]
        df, dgate, dqg = _post_bwd(dout, sv['f'], p['qg'] + tie, p['gate'], FFN_RES, name=f"post_bwd_{tag}")
        u_dn = _mm(sv['a'], df, ta=True, tn_cap=D_MODEL // 2, out_shape=(2, N_SHARD, F_SHARD, D_MODEL // 2),
                   out_sel=lambda g, r, j: ((j, g), r, 0), name=f"ffn_dwd_{tag}")
        dgu = _ffn_dgu(df, w_dn, sv['gu'], name=f"ffn_dgu_{tag}")
        dgu = dgu.reshape(2 * N_SHARD, SEQ, F_SHARD)
        u_gu = _mm(dgu, sv['hn'], ta=True, out_shape=(2, N_SHARD, F_SHARD, D_MODEL),
                   out_sel=lambda g, r, j: ((g % 2, g // 2), r, j), name=f"ffn_dwgu_{tag}")
        ffn_units[i, h] = [u_gu, u_dn]
        dhn = _mm(dgu, w_gu.reshape(2 * N_SHARD, D_MODEL, F_SHARD), tb=True, reduce_g=True, name=f"ffn_dhn_{tag}")
        return close_sub(dhn, dout, sv, i, sub, dgate, dqg)

    def mla_bwd(dout, sv, i, sub, tie=0.0):
        p, tag = sv['p'], sv['tag']
        df, dgate, dqg = _post_bwd(dout, sv['f'], p['qg'] + tie, p['gate'], 1.0, name=f"post_bwd_{tag}")
        u_wo = _mm(sv['o_flat'], df, ta=True, tm_cap=128, out_shape=(2, N_SHARD, 128, D_MODEL), out_sel=row_unit,
                   name="mla_dwo")
        do_flat = _mm(df, w_mo, tb=True, name="mla_do")
        do = _to_heads(do_flat, V_HEAD)
        dq, dk, dv = _mla_attn_bwd(sv['q'], sv['k'], sv['v'], sv['o'], do, sv['lse'], name="mla_attn_bwd")
        dq_t = dq.transpose(1, 0, 2)
        dqr1, dqr2 = _rope(dq_t[:, :, QK_NOPE:QK_NOPE + HALF_ROPE].reshape(SEQ, -1),
                           dq_t[:, :, QK_NOPE + HALF_ROPE:].reshape(SEQ, -1), cos_q, -sin_q, name="rope_q_bwd")
        dqp = jnp.concatenate([dq_t[:, :, :QK_NOPE].reshape(SEQ, -1), dqr1, dqr2], axis=1).astype(BF16)
        dkr = _head_sum(dk[:, :, QK_NOPE:], name="mla_dkr_sum")
        dk1, dk2 = _rope(dkr[:, :HALF_ROPE], dkr[:, HALF_ROPE:], cos_k, -sin_k, name="rope_k_bwd")
        dkvp = jnp.concatenate([_from_heads(dk[:, :, :QK_NOPE]), _from_heads(dv)], axis=1).astype(BF16)
        g_wq = _q_unperm(_mm(sv['cqn'], dqp, ta=True, name="mla_dwq"))
        g_wkv = _kv_unperm(_mm(sv['ckvn'], dkvp, ta=True, name="mla_dwkv"))
        dcqn = _mm(dqp, wq_p, tb=True, name="mla_dcqn")
        dckvn = _mm(dkvp, wkv_p, tb=True, name="mla_dckvn")
        dcq, g_qn = _rms_bwd(dcqn, sv['cq'], mla_q_norm, name="mla_qnorm_bwd")
        dckv, g_kvn = _rms_bwd(dckvn, sv['ckv'], mla_kv_norm, name="mla_kvnorm_bwd")
        dlat = jnp.concatenate([dcq, dckv, dk1, dk2], axis=1).astype(BF16)
        u_win = _mm(sv['hn'], dlat, ta=True, tm_cap=128, out_shape=(2, N_SHARD, 128, dlat.shape[1]),
                    out_sel=row_unit, name="mla_dwin")
        dhn = _mm(dlat, w_in, tb=True, name="mla_dhn")
        col_unit = lambda t: (t.reshape(t.shape[0], N_SHARD, -1).transpose(1, 0, 2)
                              .reshape(N_SHARD, 2, t.shape[0] // 2, -1).transpose(1, 0, 2, 3))
        grads = dict(units=[u_win, col_unit(g_wq), col_unit(g_wkv), u_wo], q_norm=g_qn, kv_norm=g_kvn)
        return close_sub(dhn, dout, sv, i, sub, dgate, dqg), grads

    def dil_bwd(dout, sv, i, sub):
        p, tag = sv['p'], sv['tag']
        df, dgate, dqg = _post_bwd(dout, sv['f'], p['qg'], p['gate'], 1.0, name=f"post_bwd_{tag}")
        u_wo = _mm(sv['o_flat'], df, ta=True, tm_cap=128, out_shape=(2, N_SHARD, 128, D_MODEL), out_sel=row_unit,
                   name="dil_dwo")
        do = _to_heads(_mm(df, dil_w['out'], tb=True, name="dil_do"), 64)
        dos, dlts = _dil_mix_bwd(do, sv['outs'], sv['lses'], name="dil_mix_bwd")
        pieces = []
        bias_rows = []
        for g, (window, d) in enumerate(DIL_GROUPS):
            q, k, v = sv['qkv'][g]
            dq, dk, dv, dbias = _dil_attn_bwd(q, k, v, biases[g], _residue_major(sv['lses'][g], d),
                                              _residue_major(dos[g], d), _residue_major(dlts[g], d),
                                              SEQ // d // DIL_BLOCK, name=f"dil_attn_bwd_g{g}")
            pieces += [_token_major(t, d).astype(BF16) for t in (dq, dk, dv)]
            bias_rows.append(_bias_grad(dbias, buckets[g], name=f"dil_bias_grad_g{g}")[:, 0, :])
        dheads = jnp.stack(pieces).reshape(9 * HEADS, SEQ, 64)
        u_win = _proj_heads_dw(sv['hn'], dheads, name="dil_dwin")
        dhn = _proj_heads_dx(dheads, dil_w['in'], name="dil_dhn")
        g_bias = jnp.concatenate(bias_rows, axis=0).T
        grads = dict(units=[u_win, u_wo], rel_bias=g_bias)
        return close_sub(dhn, dout, sv, i, sub, dgate, dqg), grads

    def to_sibling(units, tag):
        n = len(units)
        send, recv, thru, token = _copies_start(units, [lax.empty(u.shape[1:], F32) for u in units], _sibling_plan, n,
                                                name=f"rs{tag}_sibling_start")
        return dict(send=send, recv=recv, thru=thru, n=n, tag=tag), token[0, 0]

    def from_sibling(st, after):
        n, tag = st['n'], st['tag']
        thru = _copies_wait(st['send'], st['recv'], st['thru'], n, _sibling_plan, after, name=f"rs{tag}_sibling_wait")
        return [_add_half(u, g, half_idx, name=f"rs{tag}_add_half_{k}") for k, (u, g) in enumerate(zip(thru[:n], thru[n:]))]

    def to_chips(parts, tag):
        n = len(parts)
        send, recv, thru, token = _copies_start([w for _, w in parts],
                                                [lax.empty((3,) + w.shape[1:], BF16) for _, w in parts], _chips_plan,
                                                3 * n, name=f"rs{tag}_chips_start")
        return dict(send=send, recv=recv, thru=thru, n=n, tag=tag, parts=parts), token[0, 0]

    def from_chips(st, after):
        n, tag = st['n'], st['tag']
        thru = _copies_wait(st['send'], st['recv'], st['thru'], n, _chips_plan, after, name=f"rs{tag}_chips_wait")
        return [_add_shards(p, g, shard_idx, name=f"rs{tag}_add_shards_{k}")
                for k, ((p, _), g) in enumerate(zip(st['parts'], thru[n:]))]

    dx = ffn_bwd(dx, saved[5], 2)
    dx, dil_g = dil_bwd(dx, saved[4], 1, 1)
    dx = ffn_bwd(dx, saved[3], 0)
    st1, tok = to_sibling([*ffn_units[1, 1], *dil_g['units'], *ffn_units[1, 0]], "1")
    dx = ffn_bwd(dx, saved[2], 2, tie=tok)
    st1, tok1 = to_chips(from_sibling(st1, dx), "1")
    st2, tok2 = to_sibling(ffn_units[0, 1], "2")
    dx, mla_g = mla_bwd(dx, saved[1], 0, 1, tie=tok1 + tok2)
    reds1 = from_chips(st1, dx)
    st2, tok = to_chips(from_sibling(st2, dx), "2")
    dx = ffn_bwd(dx, saved[0], 0, tie=tok)
    reds2 = from_chips(st2, dx)
    grad_x = dx[None]

    pad_row = lambda v: jnp.pad(v.reshape(-1), (0, (-v.size) % D_MODEL)).reshape(-1, D_MODEL)
    small = jnp.concatenate(
        [jnp.concatenate([dmod[i][r] for i in range(2) for r in range(9)], axis=0),
         jnp.concatenate([dpre[i][s] for i in range(2) for s in range(3)], axis=0),
         jnp.concatenate([dpost[i][s] for i in range(2) for s in range(3)], axis=0),
         pad_row(mla_g['q_norm']), pad_row(mla_g['kv_norm']), pad_row(dil_g['rel_bias']), pad_row(loss_part)], axis=0)
    small = jnp.pad(small, ((0, SMALL_ROWS - small.shape[0]), (0, 0)))
    small_all = _all_gather(small, name="ag_small_grads", in_vmem=True)
    small_sum = _sum_devices(small_all, 8, name="sum_small_grads")
    g_b_mod = small_sum[0:18].reshape(2, 9 * D_MODEL)
    my_cols = lambda t: lax.dynamic_slice_in_dim(t, shard_id * 256, 256, axis=2)
    g_norm_pre = my_cols(small_sum[18:24].reshape(2, 3, D_MODEL))
    g_norm_post = my_cols(small_sum[24:30].reshape(2, 3, D_MODEL))
    g_q_norm = small_sum[30, :Q_LORA].reshape(1, Q_LORA)
    g_kv_norm = small_sum[31, :KV_LORA].reshape(1, KV_LORA)
    g_rel_bias = small_sum[32:34].reshape(-1)[:N_BUCKETS * 48].reshape(N_BUCKETS, 48)
    loss = small_sum[34, 0]
    dmod_all = small_all.reshape(8, SMALL_ROWS, D_MODEL)[:, 0:18].reshape(8, 2, 9 * D_MODEL)
    dmod_cols = lax.dynamic_slice_in_dim(dmod_all, shard_id * 2304, 2304, axis=2).transpose(1, 0, 2)
    g_w_mod = _mm(silu_c, dmod_cols.astype(BF16), ta=True, tn_cap=768, name="w_mod_grad")

    units0 = [*mla_g['units'], *ffn_units[0, 0]]
    got_a = _swap_halves(units0, name="rs0_sibling")
    parts0 = [_add_half(u, g, half_idx, name=f"rs0_add_half_{k}") for k, (u, g) in enumerate(zip(units0, got_a))]
    got_b = _send_to_chips([w for _, w in parts0], name="rs0_chips")
    reds0 = [_add_shards(p, g, shard_idx, name=f"rs0_add_shards_{k}")
             for k, ((p, _), g) in enumerate(zip(parts0, got_b))]
    fin = _pair_gather(reds1 + reds2 + reds0, name="rs_pair_gather")
    ffn_fin = {(1, 1): fin[0:2], (1, 0): fin[4:6], (0, 1): fin[6:8], (0, 0): fin[12:14]}
    swap = lambda t: jnp.swapaxes(t, 2, 3)
    per_ffn = lambda pick: jnp.stack([jnp.stack([pick(*ffn_fin[i, h]) for h in range(2)]) for i in range(2)])
    reduced = dict(ffn_w_gate=swap(per_ffn(lambda gu, dn: gu[0])), ffn_w_up=swap(per_ffn(lambda gu, dn: gu[1])),
                   ffn_w_down=per_ffn(lambda gu, dn: jnp.concatenate([dn[0], dn[1]], axis=1)))
    for n, t in zip(['dil_w_in', 'dil_w_o', 'mla_w_in', 'mla_w_q_up', 'mla_w_kv_up', 'mla_w_o'], fin[2:4] + fin[8:12]):
        reduced[n] = t.reshape(given[n].shape)

    grads = dict(norm_pre=g_norm_pre, norm_post=g_norm_post, w_mod=g_w_mod, b_mod=g_b_mod, mla_q_norm=g_q_norm,
                 mla_kv_norm=g_kv_norm, rel_bias=g_rel_bias, **reduced)

    deltas, new_m, new_v = {}, {}, {}
    for n in WEIGHTS:
        view = swap if n in ('ffn_w_gate', 'ffn_w_up') else (lambda t: t)
        outs = _adamw(view(given[n]), view(grads[n]), view(given["m_" + n]), view(given["v_" + n]), name=f"adamw_{n}")
        deltas[n], new_m[n], new_v[n] = (view(t) for t in outs)
    return (loss, grad_x, *[grads[n] for n in WEIGHTS], *[deltas[n] for n in WEIGHTS],
            *[new_m[n] for n in WEIGHTS], *[new_v[n] for n in WEIGHTS])
```

```python
import math

import jax
import jax.numpy as jnp
from jax import lax
from jax.experimental import pallas as pl
from jax.experimental.pallas import tpu as pltpu

F32 = jnp.float32
BF16 = jnp.bfloat16
MESH = pl.DeviceIdType.MESH

SEQ = 2048
D_MODEL = 1024
D_FF = 2816
N_SHARD = 4
F_SHARD = D_FF // N_SHARD
EPS = 1e-6
FFN_RES = 0.5
HEADS = 16
Q_LORA, KV_LORA, QK_NOPE, QK_ROPE, V_HEAD = 384, 256, 64, 32, 64
HALF_ROPE = QK_ROPE // 2
ROPE_THETA = 10000.0
DIL_GROUPS = ((128, 1), (512, 4), (2048, 16))
DIL_BLOCK = 128
N_BUCKETS = 32
MAX_DISTANCE = 2048
ADAM_LR, ADAM_B1, ADAM_B2, ADAM_EPS, ADAM_WD, ADAM_STEP = 0.001, 0.9, 0.999, 1e-08, 0.01, 10

VMEM_LIMIT = 48 * 1024 * 1024
SMALL_ROWS = 40

WEIGHTS = ['norm_pre', 'norm_post', 'w_mod', 'b_mod', 'ffn_w_gate', 'ffn_w_up', 'ffn_w_down', 'mla_w_in',
           'mla_q_norm', 'mla_w_q_up', 'mla_kv_norm', 'mla_w_kv_up', 'mla_w_o', 'dil_w_in', 'dil_w_o', 'rel_bias']


def _cparams(**kw):
    return pltpu.CompilerParams(vmem_limit_bytes=VMEM_LIMIT, **kw)


def _pick(n, cap, mult=128):
    if n <= cap:
        return n
    best = n
    for t in range(mult, cap + 1, mult):
        if n % t == 0:
            best = t
    return best


def _mm(a, b, *, name, ta=False, tb=False, reduce_g=False, bias=None, out_dtype=F32, tm_cap=512, tn_cap=1024,
        g_n=None, b_sel=None, out_shape=None, out_sel=None, out_buf=None):
    a3 = a if a.ndim == 3 else a[None]
    ga = a3.shape[0]
    if b_sel is None:
        b_n = b if b.ndim == 3 else b[None]
        gb = b_n.shape[0]
        b_sel = (lambda g: (g,)) if gb > 1 else (lambda g: (0,))
        g_n = max(ga, gb)
    else:
        b_n = b
    k_dim, m_dim = (a3.shape[1], a3.shape[2]) if ta else (a3.shape[2], a3.shape[1])
    k2, n_dim = (b_n.shape[-1], b_n.shape[-2]) if tb else (b_n.shape[-2], b_n.shape[-1])
    assert k_dim == k2, (a.shape, b.shape)
    tm = _pick(m_dim, tm_cap, 128 if ta else 8)
    tn = _pick(n_dim, tn_cap, 128)
    mt, nt = m_dim // tm, n_dim // tn
    dims = (((0 if ta else 1,), (1 if tb else 0,)), ((), ()))

    if reduce_g:
        grid = (mt, nt, g_n)
        ids = lambda i, j, g: (g, i, j)
    else:
        grid = (g_n, mt, nt)
        ids = lambda g, i, j: (g, i, j)

    def a_map(*p):
        g, i, j = ids(*p)
        g = g if ga > 1 else 0
        return (g, 0, i) if ta else (g, i, 0)

    def b_map(*p):
        g, i, j = ids(*p)
        return (*b_sel(g), j, 0) if tb else (*b_sel(g), 0, j)

    b_lead = (None,) * (b_n.ndim - 2)
    a_spec = pl.BlockSpec((None, k_dim, tm) if ta else (None, tm, k_dim), a_map)
    b_spec = pl.BlockSpec(b_lead + ((tn, k_dim) if tb else (k_dim, tn)), b_map)
    in_specs = [a_spec, b_spec]
    operands = [a3, b_n]
    if bias is not None:
        assert not reduce_g and bias.shape == (g_n, 1, n_dim)
        in_specs.append(pl.BlockSpec((None, 1, tn), lambda g, i, j: (g, 0, j)))
        operands.append(bias)
    aliases = {}
    if out_buf is not None:
        assert tuple(out_buf.shape) == tuple(out_shape) and out_buf.dtype == out_dtype
        in_specs.append(pl.BlockSpec(memory_space=pl.ANY))
        operands.append(out_buf)
        aliases = {len(operands) - 1: 0}

    if reduce_g:
        out_spec = pl.BlockSpec((tm, tn), lambda i, j, g: (i, j))
        out_sds = jax.ShapeDtypeStruct((m_dim, n_dim), F32)
    elif out_shape is not None:
        def o_map(g, i, j):
            lead, rb, cb = out_sel(g, i, j)
            return (*lead, rb, cb)

        out_spec = pl.BlockSpec((None,) * (len(out_shape) - 2) + (tm, tn), o_map)
        out_sds = jax.ShapeDtypeStruct(tuple(out_shape), out_dtype)
    else:
        out_spec = pl.BlockSpec((None, tm, tn), lambda g, i, j: (g, i, j))
        out_sds = jax.ShapeDtypeStruct((g_n, m_dim, n_dim), out_dtype)

    def body(a_ref, b_ref, *rest):
        o_ref = rest[-1]
        r = lax.dot_general(a_ref[...].astype(BF16), b_ref[...].astype(BF16), dims, preferred_element_type=F32)
        if bias is not None:
            r = r + rest[0][...]
        if reduce_g:
            g = pl.program_id(2)

            @pl.when(g == 0)
            def _():
                o_ref[...] = r

            @pl.when(g > 0)
            def _():
                o_ref[...] += r
        else:
            o_ref[...] = r.astype(o_ref.dtype)

    out = pl.pallas_call(body, grid=grid, in_specs=in_specs, out_specs=out_spec, out_shape=out_sds,
                         input_output_aliases=aliases, compiler_params=_cparams(), name=name)(*operands)
    if not reduce_g and out_shape is None and a.ndim == 2 and b.ndim == 2:
        out = out[0]
    return out


def _rows(tm, w):
    return pl.BlockSpec((tm, w), lambda i: (i, 0))


def _vec(w):
    return pl.BlockSpec((1, w), lambda i: (0, 0))


def _rstd(v):
    return lax.rsqrt(jnp.mean(v * v, axis=-1, keepdims=True) + EPS)


def _pre_fwd(x, pg, sc, sh, *, name):
    s_n, w = x.shape
    tm = _pick(s_n, 256, 8)

    def body(x_ref, pg_ref, sc_ref, sh_ref, o_ref):
        xv = x_ref[...]
        n = (xv * _rstd(xv)) * pg_ref[...]
        o_ref[...] = (n * (1.0 + sc_ref[...]) + sh_ref[...]).astype(o_ref.dtype)

    return pl.pallas_call(body, grid=(s_n // tm,), in_specs=[_rows(tm, w), _vec(w), _vec(w), _vec(w)],
                          out_specs=_rows(tm, w), out_shape=jax.ShapeDtypeStruct((s_n, w), BF16),
                          compiler_params=_cparams(), name=name)(x, pg, sc, sh)


def _post_fwd(f, x, qg, gate, res_w, *, name):
    s_n, w = x.shape
    tm = _pick(s_n, 256, 8)

    def body(f_ref, x_ref, qg_ref, gate_ref, o_ref):
        fv = f_ref[...]
        y = (fv * _rstd(fv)) * qg_ref[...]
        o_ref[...] = x_ref[...] + (res_w * gate_ref[...]) * y

    return pl.pallas_call(body, grid=(s_n // tm,), in_specs=[_rows(tm, w), _rows(tm, w), _vec(w), _vec(w)],
                          out_specs=_rows(tm, w), out_shape=jax.ShapeDtypeStruct((s_n, w), F32),
                          compiler_params=_cparams(), name=name)(f, x, qg, gate)


def _post_bwd(dout, f, qg, gate, res_w, *, name):
    s_n, w = f.shape
    tm = _pick(s_n, 256, 8)

    def body(do_ref, f_ref, qg_ref, gate_ref, df_ref, dgate_ref, dqg_ref):
        @pl.when(pl.program_id(0) == 0)
        def _():
            dgate_ref[...] = jnp.zeros_like(dgate_ref)
            dqg_ref[...] = jnp.zeros_like(dqg_ref)

        do = do_ref[...]
        fv = f_ref[...]
        r = _rstd(fv)
        fh = fv * r
        qg_v = qg_ref[...]
        dgate_ref[...] += res_w * jnp.sum(do * (fh * qg_v), axis=0, keepdims=True)
        dy = do * (res_w * gate_ref[...])
        dqg_ref[...] += jnp.sum(dy * fh, axis=0, keepdims=True)
        dfh = dy * qg_v
        df = r * (dfh - fh * jnp.mean(dfh * fh, axis=-1, keepdims=True))
        df_ref[...] = df.astype(df_ref.dtype)

    return pl.pallas_call(
        body, grid=(s_n // tm,), in_specs=[_rows(tm, w), _rows(tm, w), _vec(w), _vec(w)],
        out_specs=[_rows(tm, w), _vec(w), _vec(w)],
        out_shape=[jax.ShapeDtypeStruct((s_n, w), BF16), jax.ShapeDtypeStruct((1, w), F32),
                   jax.ShapeDtypeStruct((1, w), F32)],
        compiler_params=_cparams(), name=name)(dout, f, qg, gate)


def _pre_bwd(dhn, x, dout, pg, sc, *, name):
    s_n, w = x.shape
    tm = _pick(s_n, 256, 8)

    def body(dhn_ref, x_ref, do_ref, pg_ref, sc_ref, dx_ref, dsh_ref, dsc_ref, dpg_ref):
        @pl.when(pl.program_id(0) == 0)
        def _():
            dsh_ref[...] = jnp.zeros_like(dsh_ref)
            dsc_ref[...] = jnp.zeros_like(dsc_ref)
            dpg_ref[...] = jnp.zeros_like(dpg_ref)

        dhn_v = dhn_ref[...]
        xv = x_ref[...]
        r = _rstd(xv)
        xh = xv * r
        pg_v = pg_ref[...]
        dsh_ref[...] += jnp.sum(dhn_v, axis=0, keepdims=True)
        dsc_ref[...] += jnp.sum(dhn_v * (xh * pg_v), axis=0, keepdims=True)
        dn = dhn_v * (1.0 + sc_ref[...])
        dpg_ref[...] += jnp.sum(dn * xh, axis=0, keepdims=True)
        dxh = dn * pg_v
        dx_ref[...] = do_ref[...] + r * (dxh - xh * jnp.mean(dxh * xh, axis=-1, keepdims=True))

    vec = jax.ShapeDtypeStruct((1, w), F32)
    return pl.pallas_call(
        body, grid=(s_n // tm,), in_specs=[_rows(tm, w), _rows(tm, w), _rows(tm, w), _vec(w), _vec(w)],
        out_specs=[_rows(tm, w), _vec(w), _vec(w), _vec(w)],
        out_shape=[jax.ShapeDtypeStruct((s_n, w), F32), vec, vec, vec],
        compiler_params=_cparams(), name=name)(dhn, x, dout, pg, sc)


def _rms_fwd(x, g, *, name):
    s_n, w = x.shape
    tm = _pick(s_n, 512, 8)

    def body(x_ref, g_ref, o_ref):
        xv = x_ref[...]
        o_ref[...] = ((xv * _rstd(xv)) * g_ref[...]).astype(o_ref.dtype)

    return pl.pallas_call(body, grid=(s_n // tm,), in_specs=[_rows(tm, w), _vec(w)], out_specs=_rows(tm, w),
                          out_shape=jax.ShapeDtypeStruct((s_n, w), BF16), compiler_params=_cparams(),
                          name=name)(x, g)


def _rms_bwd(dy, x, g, *, name):
    s_n, w = x.shape
    tm = _pick(s_n, 512, 8)

    def body(dy_ref, x_ref, g_ref, dx_ref, dg_ref):
        @pl.when(pl.program_id(0) == 0)
        def _():
            dg_ref[...] = jnp.zeros_like(dg_ref)

        dy_v = dy_ref[...]
        xv = x_ref[...]
        r = _rstd(xv)
        xh = xv * r
        dg_ref[...] += jnp.sum(dy_v * xh, axis=0, keepdims=True)
        dxh = dy_v * g_ref[...]
        dx_ref[...] = r * (dxh - xh * jnp.mean(dxh * xh, axis=-1, keepdims=True))

    return pl.pallas_call(
        body, grid=(s_n // tm,), in_specs=[_rows(tm, w), _rows(tm, w), _vec(w)],
        out_specs=[_rows(tm, w), _vec(w)],
        out_shape=[jax.ShapeDtypeStruct((s_n, w), F32), jax.ShapeDtypeStruct((1, w), F32)],
        compiler_params=_cparams(), name=name)(dy, x, g)


def _rope(a1, a2, cos, sin, *, name):
    s_n, w = a1.shape
    tm = _pick(s_n, 512, 8)

    def body(a1_ref, a2_ref, c_ref, s_ref, r1_ref, r2_ref):
        u, v, c_v, s_v = a1_ref[...], a2_ref[...], c_ref[...], s_ref[...]
        r1_ref[...] = u * c_v - v * s_v
        r2_ref[...] = u * s_v + v * c_v

    sd = jax.ShapeDtypeStruct((s_n, w), F32)
    return pl.pallas_call(body, grid=(s_n // tm,), in_specs=[_rows(tm, w)] * 4, out_specs=[_rows(tm, w)] * 2,
                          out_shape=[sd, sd], compiler_params=_cparams(), name=name)(a1, a2, cos, sin)


def _silu_bf16(x, *, name):
    def body(x_ref, o_ref):
        xv = x_ref[...]
        o_ref[...] = (xv * jax.nn.sigmoid(xv)).astype(o_ref.dtype)

    return pl.pallas_call(body, out_shape=jax.ShapeDtypeStruct(x.shape, BF16), name=name)(x)


def _loss(y, target, *, name):
    s_n, w = y.shape
    tm = _pick(s_n, 256, 8)

    def body(y_ref, t_ref, dy_ref, l_ref):
        @pl.when(pl.program_id(0) == 0)
        def _():
            l_ref[...] = jnp.zeros_like(l_ref)

        e = y_ref[...] - t_ref[...]
        dy_ref[...] = e * (1.0 / w)
        row = jnp.mean(e * e, axis=-1, keepdims=True)
        l_ref[...] += 0.5 * jnp.sum(row, axis=0, keepdims=True)

    return pl.pallas_call(
        body, grid=(s_n // tm,), in_specs=[_rows(tm, w), _rows(tm, w)],
        out_specs=[_rows(tm, w), pl.BlockSpec((1, 1), lambda i: (0, 0))],
        out_shape=[jax.ShapeDtypeStruct((s_n, w), F32), jax.ShapeDtypeStruct((1, 1), F32)],
        compiler_params=_cparams(), name=name)(y, target)


FFN_TM = 512


def _ffn_up(hn, w_gu, *, name):
    s_n, d = hn.shape
    f = w_gu.shape[-1]
    tm = _pick(s_n, FFN_TM, 8)

    def body(hn_ref, wg_ref, wu_ref, gu_ref, a_ref):
        xv = hn_ref[...]
        g = jnp.dot(xv, wg_ref[...], preferred_element_type=F32)
        u = jnp.dot(xv, wu_ref[...], preferred_element_type=F32)
        gu_ref[0] = g.astype(BF16)
        gu_ref[1] = u.astype(BF16)
        a_ref[...] = ((g * jax.nn.sigmoid(g)) * u).astype(BF16)

    w_blk = lambda t: pl.BlockSpec((None, None, d, f), lambda s, m: (s, t, 0, 0))
    return pl.pallas_call(
        body, grid=(N_SHARD, s_n // tm),
        in_specs=[pl.BlockSpec((tm, d), lambda s, m: (m, 0)), w_blk(0), w_blk(1)],
        out_specs=[pl.BlockSpec((None, 2, tm, f), lambda s, m: (s, 0, m, 0)),
                   pl.BlockSpec((None, tm, f), lambda s, m: (s, m, 0))],
        out_shape=[jax.ShapeDtypeStruct((N_SHARD, 2, s_n, f), BF16), jax.ShapeDtypeStruct((N_SHARD, s_n, f), BF16)],
        compiler_params=_cparams(), name=name)(hn, w_gu, w_gu)


def _ffn_dgu(df, w_dn, gu, *, name):
    s_n, d = df.shape
    f = w_dn.shape[-2]
    tm = _pick(s_n, FFN_TM, 8)

    def body(df_ref, wd_ref, gu_ref, o_ref):
        da = lax.dot_general(df_ref[...], wd_ref[...], (((1,), (1,)), ((), ())), preferred_element_type=F32)
        g = gu_ref[0].astype(F32)
        u = gu_ref[1].astype(F32)
        sig = jax.nn.sigmoid(g)
        o_ref[0] = (da * u * (sig * (1.0 + g * (1.0 - sig)))).astype(BF16)
        o_ref[1] = (da * (g * sig)).astype(BF16)

    gu_blk = pl.BlockSpec((None, 2, tm, f), lambda s, m: (s, 0, m, 0))
    return pl.pallas_call(
        body, grid=(N_SHARD, s_n // tm),
        in_specs=[pl.BlockSpec((tm, d), lambda s, m: (m, 0)),
                  pl.BlockSpec((None, f, d), lambda s, m: (s, 0, 0)), gu_blk],
        out_specs=gu_blk, out_shape=jax.ShapeDtypeStruct((N_SHARD, 2, s_n, f), BF16),
        compiler_params=_cparams(), name=name)(df, w_dn, gu)


_NT = (((1,), (1,)), ((), ()))
_TN = (((0,), (0,)), ((), ()))
MLA_TQ = 256


def _causal_mask(i, tq, s_n):
    qpos = i * tq + lax.broadcasted_iota(jnp.int32, (tq, s_n), 0)
    kpos = lax.broadcasted_iota(jnp.int32, (tq, s_n), 1)
    return kpos <= qpos


def _mla_attn_fwd(q, k, v, *, name):
    h_n, s_n, dq = q.shape
    dv = v.shape[-1]
    tq = MLA_TQ
    scale = float(dq) ** -0.5

    def body(q_ref, k_ref, v_ref, o_ref, lse_ref):
        i = pl.program_id(1)
        for e in range(1, s_n // tq + 1):
            @pl.when(i == e - 1)
            def _(ext=e * tq):
                mask = _causal_mask(i, tq, ext)
                s = lax.dot_general(q_ref[...], k_ref[0:ext, :], _NT, preferred_element_type=F32) * scale
                s = jnp.where(mask, s, -jnp.inf)
                m = jnp.max(s, axis=-1, keepdims=True)
                p = jnp.exp(s - m)
                l = jnp.sum(p, axis=-1, keepdims=True)
                o = jnp.dot(p.astype(BF16), v_ref[0:ext, :], preferred_element_type=F32)
                o_ref[...] = o / l
                lse_ref[...] = m + jnp.log(l)

    return pl.pallas_call(
        body, grid=(h_n, s_n // tq),
        in_specs=[pl.BlockSpec((None, tq, dq), lambda h, i: (h, i, 0)),
                  pl.BlockSpec((None, s_n, dq), lambda h, i: (h, 0, 0)),
                  pl.BlockSpec((None, s_n, dv), lambda h, i: (h, 0, 0))],
        out_specs=[pl.BlockSpec((None, tq, dv), lambda h, i: (h, i, 0)),
                   pl.BlockSpec((None, tq, 1), lambda h, i: (h, i, 0))],
        out_shape=[jax.ShapeDtypeStruct((h_n, s_n, dv), F32), jax.ShapeDtypeStruct((h_n, s_n, 1), F32)],
        compiler_params=_cparams(), name=name)(q, k, v)


def _mla_attn_bwd(q, k, v, o, do, lse, *, name):
    h_n, s_n, dq = q.shape
    dv = v.shape[-1]
    tq = MLA_TQ
    scale = float(dq) ** -0.5

    def body(q_ref, k_ref, v_ref, o_ref, do_ref, lse_ref, dq_ref, dk_ref, dv_ref):
        i = pl.program_id(1)

        @pl.when(i == 0)
        def _():
            dk_ref[...] = jnp.zeros_like(dk_ref)
            dv_ref[...] = jnp.zeros_like(dv_ref)

        for e in range(1, s_n // tq + 1):
            @pl.when(i == e - 1)
            def _(ext=e * tq):
                mask = _causal_mask(i, tq, ext)
                qv, kv, vv = q_ref[...], k_ref[0:ext, :], v_ref[0:ext, :]
                do_v = do_ref[...]
                s = lax.dot_general(qv, kv, _NT, preferred_element_type=F32) * scale
                p = jnp.where(mask, jnp.exp(s - lse_ref[...]), 0.0)
                dob = do_v.astype(BF16)
                dv_ref[0:ext, :] += lax.dot_general(p.astype(BF16), dob, _TN, preferred_element_type=F32)
                dp = lax.dot_general(dob, vv, _NT, preferred_element_type=F32)
                delta = jnp.sum(do_v * o_ref[...], axis=-1, keepdims=True)
                dsb = (p * (dp - delta) * scale).astype(BF16)
                dq_ref[...] = jnp.dot(dsb, kv, preferred_element_type=F32)
                dk_ref[0:ext, :] += lax.dot_general(dsb, qv, _TN, preferred_element_type=F32)

    return pl.pallas_call(
        body, grid=(h_n, s_n // tq),
        in_specs=[pl.BlockSpec((None, tq, dq), lambda h, i: (h, i, 0)),
                  pl.BlockSpec((None, s_n, dq), lambda h, i: (h, 0, 0)),
                  pl.BlockSpec((None, s_n, dv), lambda h, i: (h, 0, 0)),
                  pl.BlockSpec((None, tq, dv), lambda h, i: (h, i, 0)),
                  pl.BlockSpec((None, tq, dv), lambda h, i: (h, i, 0)),
                  pl.BlockSpec((None, tq, 1), lambda h, i: (h, i, 0))],
        out_specs=[pl.BlockSpec((None, tq, dq), lambda h, i: (h, i, 0)),
                   pl.BlockSpec((None, s_n, dq), lambda h, i: (h, 0, 0)),
                   pl.BlockSpec((None, s_n, dv), lambda h, i: (h, 0, 0))],
        out_shape=[jax.ShapeDtypeStruct((h_n, s_n, dq), F32), jax.ShapeDtypeStruct((h_n, s_n, dq), F32),
                   jax.ShapeDtypeStruct((h_n, s_n, dv), F32)],
        compiler_params=_cparams(), name=name)(q, k, v, o, do, lse)


def _head_sum(x, *, name):
    h_n, s_n, w = x.shape
    tm = _pick(s_n, 256, 8)

    def body(x_ref, o_ref):
        o_ref[...] = jnp.sum(x_ref[...], axis=0)

    return pl.pallas_call(body, grid=(s_n // tm,), in_specs=[pl.BlockSpec((h_n, tm, w), lambda i: (0, i, 0))],
                          out_specs=_rows(tm, w), out_shape=jax.ShapeDtypeStruct((s_n, w), F32),
                          compiler_params=_cparams(), name=name)(x)


N_BLK = SEQ // DIL_BLOCK
DIL_SCALE = 64 ** -0.5


def _dil_masks():
    iq = lax.broadcasted_iota(jnp.int32, (DIL_BLOCK, 2 * DIL_BLOCK), 0)
    ik = lax.broadcasted_iota(jnp.int32, (DIL_BLOCK, 2 * DIL_BLOCK), 1)
    rel = DIL_BLOCK + iq - ik
    both = (rel >= 0) & (rel <= DIL_BLOCK)
    iq1 = lax.broadcasted_iota(jnp.int32, (DIL_BLOCK, DIL_BLOCK), 0)
    ik1 = lax.broadcasted_iota(jnp.int32, (DIL_BLOCK, DIL_BLOCK), 1)
    return both, ik1 <= iq1


def _dil_block(j, d):
    nb = SEQ // d // DIL_BLOCK
    r, n = divmod(j, nb)
    first = n == 0
    rows = lambda start, size: pl.ds(start, size) if d == 1 else pl.ds(start, size, stride=d)
    q_rows = rows(n * DIL_BLOCK * d + r, DIL_BLOCK)
    k_rows = q_rows if first else rows((n - 1) * DIL_BLOCK * d + r, 2 * DIL_BLOCK)
    return q_rows, k_rows, (DIL_BLOCK if first else 0), first


def _dil_head_specs(s_n, e, g):
    return [pl.BlockSpec((None, s_n, e), lambda h, t=t: (g * 3 * HEADS + t * HEADS + h, 0, 0)) for t in range(3)]


def _dil_attn_fwd(heads, bias, g, d, *, name):
    _, s_n, e = heads.shape

    def body(q_ref, k_ref, v_ref, b_ref, o_ref, lse_ref):
        m_both, m_first = _dil_masks()
        for j in range(N_BLK):
            q_rows, k_rows, b_lo, first = _dil_block(j, d)
            qj = q_ref[q_rows, :].astype(BF16)
            kk = k_ref[k_rows, :].astype(BF16)
            vv = v_ref[k_rows, :].astype(BF16)
            s = lax.dot_general(qj, kk, _NT, preferred_element_type=F32) * DIL_SCALE + b_ref[:, b_lo:]
            s = jnp.where(m_first if first else m_both, s, -jnp.inf)
            m = jnp.max(s, axis=-1, keepdims=True)
            lse = m + jnp.log(jnp.sum(jnp.exp(s - m), axis=-1, keepdims=True))
            p = jnp.exp(s - lse)
            o_ref[q_rows, :] = jnp.dot(p.astype(BF16), vv, preferred_element_type=F32)
            lse_ref[q_rows, :] = lse

    head = lambda w: pl.BlockSpec((None, s_n, w), lambda h: (h, 0, 0))
    return pl.pallas_call(
        body, grid=(HEADS,),
        in_specs=_dil_head_specs(s_n, e, g) + [pl.BlockSpec((None, DIL_BLOCK, 2 * DIL_BLOCK), lambda h: (h, 0, 0))],
        out_specs=[head(e), head(1)],
        out_shape=[jax.ShapeDtypeStruct((HEADS, s_n, e), F32), jax.ShapeDtypeStruct((HEADS, s_n, 1), F32)],
        compiler_params=_cparams(), name=name)(heads, heads, heads, bias)


def _dil_attn_bwd(heads, bias, lse, do, dlt, g, d, *, name):
    _, s_n, e = heads.shape

    def body(q_ref, k_ref, v_ref, b_ref, lse_ref, do_ref, dlt_ref, dq_ref, dk_ref, dv_ref, db_ref):
        dk_ref[...] = jnp.zeros_like(dk_ref)
        dv_ref[...] = jnp.zeros_like(dv_ref)
        db_ref[...] = jnp.zeros_like(db_ref)
        m_both, m_first = _dil_masks()
        for j in range(N_BLK):
            q_rows, k_rows, b_lo, first = _dil_block(j, d)
            qj = q_ref[q_rows, :].astype(BF16)
            kk = k_ref[k_rows, :].astype(BF16)
            vv = v_ref[k_rows, :].astype(BF16)
            s = lax.dot_general(qj, kk, _NT, preferred_element_type=F32) * DIL_SCALE + b_ref[:, b_lo:]
            p = jnp.where(m_first if first else m_both, jnp.exp(s - lse_ref[q_rows, :]), 0.0)
            dob = do_ref[q_rows, :].astype(BF16)
            dv_ref[k_rows, :] += lax.dot_general(p.astype(BF16), dob, _TN, preferred_element_type=F32)
            dp = lax.dot_general(dob, vv, _NT, preferred_element_type=F32)
            ds = p * (dp - dlt_ref[q_rows, :])
            db_ref[:, b_lo:] += ds
            dsb = (ds * DIL_SCALE).astype(BF16)
            dq_ref[q_rows, :] = jnp.dot(dsb, kk, preferred_element_type=F32)
            dk_ref[k_rows, :] += lax.dot_general(dsb, qj, _TN, preferred_element_type=F32)

    head = lambda w: pl.BlockSpec((None, s_n, w), lambda h: (h, 0, 0))
    b_spec = pl.BlockSpec((None, DIL_BLOCK, 2 * DIL_BLOCK), lambda h: (h, 0, 0))
    sd = jax.ShapeDtypeStruct((HEADS, s_n, e), F32)
    return pl.pallas_call(
        body, grid=(HEADS,),
        in_specs=_dil_head_specs(s_n, e, g) + [b_spec, head(1), head(e), head(1)],
        out_specs=[head(e), head(e), head(e), b_spec],
        out_shape=[sd, sd, sd, jax.ShapeDtypeStruct((HEADS, DIL_BLOCK, 2 * DIL_BLOCK), F32)],
        compiler_params=_cparams(), name=name)(heads, heads, heads, bias, lse, do, dlt)


def _proj_heads(x, w, *, name):
    s_n, k = x.shape
    n = w.shape[-1]
    tm, tn, e = 512, 768, 64
    per_blk, n_blk = tn // e, n // tn

    def body(x_ref, w_ref, o_ref):
        r = jnp.dot(x_ref[...], w_ref[...], preferred_element_type=F32)
        for j in range(per_blk):
            o_ref[j] = r[:, e * j:e * (j + 1)]

    return pl.pallas_call(
        body, grid=(w.shape[0], n_blk, s_n // tm),
        in_specs=[pl.BlockSpec((tm, k), lambda s, b, m: (m, 0)), pl.BlockSpec((None, k, tn), lambda s, b, m: (s, 0, b))],
        out_specs=pl.BlockSpec((per_blk, tm, e), lambda s, b, m: (s * n_blk + b, m, 0)),
        out_shape=jax.ShapeDtypeStruct((w.shape[0] * n // e, s_n, e), F32), compiler_params=_cparams(),
        name=name)(x, w)


def _heads_cat(d_ref):
    return jnp.concatenate([d_ref[j] for j in range(d_ref.shape[0])], axis=1)


def _proj_heads_dw(x, dh, *, name):
    s_n, k = x.shape
    tn, e = 768, 64
    per_blk = tn // e
    n_blk = dh.shape[0] // N_SHARD // per_blk
    n = n_blk * tn

    def body(x_ref, d_ref, o_ref):
        o_ref[...] = lax.dot_general(x_ref[...], _heads_cat(d_ref), _TN, preferred_element_type=F32)

    return pl.pallas_call(
        body, grid=(N_SHARD, n_blk, 2),
        in_specs=[pl.BlockSpec((s_n, k // 2), lambda s, b, r: (0, r)),
                  pl.BlockSpec((per_blk, s_n, e), lambda s, b, r: (s * n_blk + b, 0, 0))],
        out_specs=pl.BlockSpec((None, None, k // 2, tn), lambda s, b, r: (r, s, 0, b)),
        out_shape=jax.ShapeDtypeStruct((2, N_SHARD, k // 2, n), F32), compiler_params=_cparams(), name=name)(x, dh)


def _proj_heads_dx(dh, w, *, name):
    k, n = w.shape[1:]
    s_n = dh.shape[1]
    tm, tn, e = 512, 768, 64
    per_blk, n_blk = tn // e, n // tn

    def body(d_ref, w_ref, o_ref):
        r = lax.dot_general(_heads_cat(d_ref), w_ref[...], _NT, preferred_element_type=F32)
        g = pl.program_id(1)

        @pl.when(g == 0)
        def _():
            o_ref[...] = r

        @pl.when(g > 0)
        def _():
            o_ref[...] += r

    return pl.pallas_call(
        body, grid=(s_n // tm, N_SHARD * n_blk),
        in_specs=[pl.BlockSpec((per_blk, tm, e), lambda m, g: (g, m, 0)),
                  pl.BlockSpec((None, k, tn), lambda m, g: (g // n_blk, 0, g % n_blk))],
        out_specs=pl.BlockSpec((tm, k), lambda m, g: (m, 0)),
        out_shape=jax.ShapeDtypeStruct((s_n, k), F32), compiler_params=_cparams(), name=name)(dh, w)


def _group_alpha(l_refs):
    ls = [r[...] for r in l_refs]
    m = jnp.maximum(jnp.maximum(ls[0], ls[1]), ls[2])
    es = [jnp.exp(l - m) for l in ls]
    tot = es[0] + es[1] + es[2]
    return [ex / tot for ex in es]


def _dil_mix_fwd(os_, ls_, *, name):
    h_n, s_n, e = os_[0].shape
    tm = 512

    def body(o0, o1, o2, l0, l1, l2, out_ref):
        al = _group_alpha((l0, l1, l2))
        out_ref[...] = al[0] * o0[...] + al[1] * o1[...] + al[2] * o2[...]

    blk = lambda w: pl.BlockSpec((None, tm, w), lambda h, i: (h, i, 0))
    return pl.pallas_call(body, grid=(h_n, s_n // tm), in_specs=[blk(e)] * 3 + [blk(1)] * 3, out_specs=blk(e),
                          out_shape=jax.ShapeDtypeStruct((h_n, s_n, e), F32), compiler_params=_cparams(),
                          name=name)(*os_, *ls_)


def _dil_mix_bwd(do, os_, ls_, *, name):
    h_n, s_n, e = do.shape
    tm = 512

    def body(do_ref, o0, o1, o2, l0, l1, l2, d0, d1, d2, t0, t1, t2):
        al = _group_alpha((l0, l1, l2))
        do_v = do_ref[...]
        mix = al[0] * o0[...] + al[1] * o1[...] + al[2] * o2[...]
        dbar = jnp.sum(do_v * mix, axis=-1, keepdims=True)
        for a_g, d_ref, t_ref in zip(al, (d0, d1, d2), (t0, t1, t2)):
            d_ref[...] = a_g * do_v
            t_ref[...] = a_g * dbar

    blk = lambda w: pl.BlockSpec((None, tm, w), lambda h, i: (h, i, 0))
    sd_e = jax.ShapeDtypeStruct((h_n, s_n, e), F32)
    sd_1 = jax.ShapeDtypeStruct((h_n, s_n, 1), F32)
    outs = pl.pallas_call(body, grid=(h_n, s_n // tm), in_specs=[blk(e)] * 4 + [blk(1)] * 3,
                          out_specs=[blk(e)] * 3 + [blk(1)] * 3, out_shape=[sd_e] * 3 + [sd_1] * 3,
                          compiler_params=_cparams(), name=name)(do, *os_, *ls_)
    return outs[:3], outs[3:]


def _bias_grad(ds, bucket, *, name):
    h_n = ds.shape[0]

    def body(ds_ref, bk_ref, o_ref):
        ds_v = ds_ref[...]
        bk = bk_ref[...]
        lane = lax.broadcasted_iota(jnp.int32, (1, N_BUCKETS), 1)
        acc = jnp.zeros((1, N_BUCKETS), F32)
        for b in range(N_BUCKETS):
            tot = jnp.sum(jnp.sum(jnp.where(bk == b, ds_v, 0.0), axis=1, keepdims=True), axis=0, keepdims=True)
            acc = acc + jnp.where(lane == b, tot, 0.0)
        o_ref[...] = acc

    return pl.pallas_call(
        body, grid=(h_n,),
        in_specs=[pl.BlockSpec((None, DIL_BLOCK, 2 * DIL_BLOCK), lambda h: (h, 0, 0)),
                  pl.BlockSpec((DIL_BLOCK, 2 * DIL_BLOCK), lambda h: (0, 0))],
        out_specs=pl.BlockSpec((None, 1, N_BUCKETS), lambda h: (h, 0, 0)),
        out_shape=jax.ShapeDtypeStruct((h_n, 1, N_BUCKETS), F32), compiler_params=_cparams(), name=name)(ds, bucket)


def _bias_table(rb, bucket, *, name):
    h_n = rb.shape[0]

    def body(rb_ref, bk_ref, o_ref):
        bk = bk_ref[...]
        row = rb_ref[...]
        acc = jnp.zeros(bk.shape, F32)
        for b in range(N_BUCKETS):
            acc = jnp.where(bk == b, row[:, b:b + 1], acc)
        o_ref[...] = acc

    return pl.pallas_call(
        body, grid=(h_n,),
        in_specs=[pl.BlockSpec((None, 1, N_BUCKETS), lambda h: (h, 0, 0)),
                  pl.BlockSpec((DIL_BLOCK, 2 * DIL_BLOCK), lambda h: (0, 0))],
        out_specs=pl.BlockSpec((None, DIL_BLOCK, 2 * DIL_BLOCK), lambda h: (h, 0, 0)),
        out_shape=jax.ShapeDtypeStruct((h_n, DIL_BLOCK, 2 * DIL_BLOCK), F32), compiler_params=_cparams(),
        name=name)(rb, bucket)


def _row_tile(rows, cols, budget=1 << 20):
    if rows * cols * 4 <= budget or rows % 8:
        return rows
    best = 8
    for t in range(8, rows + 1, 8):
        if rows % t == 0 and t * cols * 4 <= budget:
            best = t
    return best


def _adamw(w, g, m, v, *, name):
    shape = w.shape
    cols = shape[-1]
    rows = math.prod(shape[:-1]) if len(shape) > 1 else 1
    to2 = lambda t: t.reshape(rows, cols)
    tr = _row_tile(rows, cols)
    c1 = 1.0 / (1.0 - ADAM_B1 ** ADAM_STEP)
    c2 = 1.0 / (1.0 - ADAM_B2 ** ADAM_STEP)

    def body(w_ref, g_ref, m_ref, v_ref, d_ref, nm_ref, nv_ref):
        g_v = g_ref[...]
        nm = ADAM_B1 * m_ref[...] + (1.0 - ADAM_B1) * g_v
        nv = ADAM_B2 * v_ref[...] + (1.0 - ADAM_B2) * (g_v * g_v)
        m_hat = nm * c1
        v_hat = nv * c2
        d_ref[...] = -ADAM_LR * (m_hat / (jnp.sqrt(v_hat) + ADAM_EPS) + ADAM_WD * w_ref[...])
        nm_ref[...] = nm
        nv_ref[...] = nv

    blk = pl.BlockSpec((tr, cols), lambda i: (i, 0))
    sd = jax.ShapeDtypeStruct((rows, cols), F32)
    outs = pl.pallas_call(body, grid=(rows // tr,), in_specs=[blk] * 4, out_specs=[blk] * 3, out_shape=[sd] * 3,
                          compiler_params=_cparams(), name=name)(to2(w), to2(g), to2(m), to2(v))
    return tuple(t.reshape(shape) for t in outs)


def _add_half(unit, got, half_idx, *, name):
    rest = unit.shape[2:]
    c = rest[-1]
    r = math.prod(rest[:-1])
    tr = _row_tile(r, c)

    def body(idx_ref, u_ref, g_ref, o_ref, w_ref):
        tot = u_ref[...] + g_ref[...].astype(F32)
        o_ref[...] = tot
        w_ref[...] = tot.astype(BF16)

    blk = pl.BlockSpec((None, tr, c), lambda s, i, idx: (s, i, 0))
    grid_spec = pltpu.PrefetchScalarGridSpec(
        num_scalar_prefetch=1, grid=(N_SHARD, r // tr),
        in_specs=[pl.BlockSpec((None, None, tr, c), lambda s, i, idx: (idx[0], s, i, 0)), blk],
        out_specs=[blk, blk])
    out, wire = pl.pallas_call(
        body, grid_spec=grid_spec,
        out_shape=[jax.ShapeDtypeStruct((N_SHARD, r, c), F32), jax.ShapeDtypeStruct((N_SHARD, r, c), BF16)],
        compiler_params=_cparams(), name=name)(half_idx, unit.reshape(2, N_SHARD, r, c), got.reshape(N_SHARD, r, c))
    return out.reshape((N_SHARD,) + rest), wire.reshape((N_SHARD,) + rest)


def _add_shards(part, got, shard_idx, *, name):
    rest = part.shape[1:]
    c = rest[-1]
    r = math.prod(rest[:-1])
    tr = _row_tile(r, c)

    def body(idx_ref, p_ref, g_ref, o_ref):
        acc = p_ref[...]
        for k in range(3):
            acc = acc + g_ref[k].astype(F32)
        o_ref[...] = acc

    grid_spec = pltpu.PrefetchScalarGridSpec(
        num_scalar_prefetch=1, grid=(r // tr,),
        in_specs=[pl.BlockSpec((None, tr, c), lambda i, idx: (idx[0], i, 0)),
                  pl.BlockSpec((3, tr, c), lambda i, idx: (0, i, 0))],
        out_specs=pl.BlockSpec((tr, c), lambda i, idx: (i, 0)))
    out = pl.pallas_call(body, grid_spec=grid_spec, out_shape=jax.ShapeDtypeStruct((r, c), F32),
                         compiler_params=_cparams(), name=name)(
        shard_idx, part.reshape(N_SHARD, r, c), got.reshape(3, r, c))
    return out.reshape(rest)


def _sum_devices(x, n_dev, *, name):
    rows = x.shape[0] // n_dev

    def body(x_ref, o_ref):
        acc = x_ref[0:rows, :]
        for d in range(1, n_dev):
            acc = acc + x_ref[d * rows:(d + 1) * rows, :]
        o_ref[...] = acc

    return pl.pallas_call(body, out_shape=jax.ShapeDtypeStruct((rows, x.shape[1]), F32), name=name)(x)


def _my_pos():
    return lax.axis_index("x"), lax.axis_index("y"), lax.axis_index("c")


def _all_gather(x_blk, *, name, in_vmem):
    m_per, n = x_blk.shape

    def body(x_ref, out_ref, send_sems, recv_sems, local_sem):
        x, y, c = _my_pos()
        me, sibling = (x, y, c), (x, y, 1 - c)
        chips = [(1 - x, y), (x, 1 - y), (1 - x, 1 - y)]

        def rows(px, py, pc):
            return out_ref.at[pl.ds((4 * px + 2 * py + pc) * m_per, m_per), :]

        def copy(k, block, to, src=None):
            return pltpu.make_async_remote_copy(
                src_ref=rows(*block) if src is None else src, dst_ref=rows(*block),
                send_sem=send_sems.at[k], recv_sem=recv_sems.at[k], device_id=to, device_id_type=MESH)

        mine = pltpu.make_async_copy(x_ref, rows(*me), local_sem)
        mine.start()
        first = [copy(0, me, sibling, src=x_ref)]
        first += [copy(1 + j, me, (*chip, c), src=x_ref) for j, chip in enumerate(chips)]
        for cp in first:
            cp.start()
        passed = [copy(4 + j, (*chip, c), sibling) for j, chip in enumerate(chips)]
        for j, chip in enumerate(chips):
            copy(1 + j, (*chip, c), me).wait_recv()
            passed[j].start()
        copy(0, sibling, me).wait_recv()
        for j, chip in enumerate(chips):
            copy(4 + j, (*chip, 1 - c), me).wait_recv()
        for cp in first + passed:
            cp.wait_send()
        mine.wait()

    space = pltpu.VMEM if in_vmem else pl.ANY
    return pl.pallas_call(
        body, out_shape=jax.ShapeDtypeStruct((8 * m_per, n), x_blk.dtype),
        in_specs=[pl.BlockSpec(memory_space=space)], out_specs=pl.BlockSpec(memory_space=space),
        scratch_shapes=[pltpu.SemaphoreType.DMA((7,)), pltpu.SemaphoreType.DMA((7,)), pltpu.SemaphoreType.DMA],
        name=name)(x_blk)


_HBM = pl.BlockSpec(memory_space=pl.ANY)


def _gather_weights(fams, *, name):
    n = len(fams)

    def body(*refs):
        ins, outs = refs[:n], refs[n:2 * n]
        send_sems, recv_sems = refs[2 * n:]
        x, y, c = _my_pos()
        me, sibling = (x, y, c), (x, y, 1 - c)
        chips = [(1 - x, y), (x, 1 - y), (1 - x, 1 - y)]

        def copy(f, k, block, to, src=None):
            px, py, pc = block
            dst = outs[f].at[2 * px + py, pc]
            return pltpu.make_async_remote_copy(
                src_ref=dst if src is None else src, dst_ref=dst, send_sem=send_sems.at[7 * f + k],
                recv_sem=recv_sems.at[7 * f + k], device_id=to, device_id_type=MESH)

        first, passed = [], []
        for f in range(n):
            src = ins[f].at[c]
            first.append(copy(f, 0, me, sibling, src=src))
            first += [copy(f, 1 + j, me, (*chip, c), src=src) for j, chip in enumerate(chips)]
        for cp in first:
            cp.start()
        for j, chip in enumerate(chips):
            for f in range(n):
                copy(f, 1 + j, (*chip, c), me).wait_recv()
                passed.append(copy(f, 4 + j, (*chip, c), sibling))
                passed[-1].start()
        for f in range(n):
            copy(f, 0, sibling, me).wait_recv()
        for j, chip in enumerate(chips):
            for f in range(n):
                copy(f, 4 + j, (*chip, 1 - c), me).wait_recv()
        for cp in first + passed:
            cp.wait_send()

    outs = pl.pallas_call(
        body, out_shape=[jax.ShapeDtypeStruct((N_SHARD,) + t.shape, t.dtype) for t in fams],
        in_specs=[_HBM] * n, out_specs=[_HBM] * n,
        scratch_shapes=[pltpu.SemaphoreType.DMA((7 * n,)), pltpu.SemaphoreType.DMA((7 * n,))], name=name)(*fams)
    return [_place_own(o, t) for o, t in zip(outs, fams)]


def _swap_halves(units, *, name):
    n = len(units)

    def body(*refs):
        ins, outs = refs[:n], refs[n:2 * n]
        send_sems, recv_sems = refs[2 * n:]
        x, y, c = _my_pos()
        cps = [pltpu.make_async_remote_copy(src_ref=ins[f].at[1 - c], dst_ref=outs[f], send_sem=send_sems.at[f],
                                            recv_sem=recv_sems.at[f], device_id=(x, y, 1 - c), device_id_type=MESH)
               for f in range(n)]
        for cp in cps:
            cp.start()
        for cp in cps:
            cp.wait()

    return pl.pallas_call(
        body, out_shape=[jax.ShapeDtypeStruct(t.shape[1:], t.dtype) for t in units],
        in_specs=[_HBM] * n, out_specs=[_HBM] * n,
        scratch_shapes=[pltpu.SemaphoreType.DMA((n,)), pltpu.SemaphoreType.DMA((n,))], name=name)(*units)


def _send_to_chips(parts, *, name):
    n = len(parts)

    def body(*refs):
        ins, outs = refs[:n], refs[n:2 * n]
        send_sems, recv_sems = refs[2 * n:]
        x, y, c = _my_pos()
        chips = [(1 - x, y), (x, 1 - y), (1 - x, 1 - y)]
        cps = [pltpu.make_async_remote_copy(src_ref=ins[f].at[2 * cx + cy], dst_ref=outs[f].at[k],
                                            send_sem=send_sems.at[3 * f + k], recv_sem=recv_sems.at[3 * f + k],
                                            device_id=(cx, cy, c), device_id_type=MESH)
               for f in range(n) for k, (cx, cy) in enumerate(chips)]
        for cp in cps:
            cp.start()
        for cp in cps:
            cp.wait()

    return pl.pallas_call(
        body, out_shape=[jax.ShapeDtypeStruct((3,) + t.shape[1:], t.dtype) for t in parts],
        in_specs=[_HBM] * n, out_specs=[_HBM] * n,
        scratch_shapes=[pltpu.SemaphoreType.DMA((3 * n,)), pltpu.SemaphoreType.DMA((3 * n,))], name=name)(*parts)


def _pair_gather(halves, *, name):
    n = len(halves)

    def body(*refs):
        ins, outs = refs[:n], refs[n:2 * n]
        send_sems, recv_sems = refs[2 * n:]
        x, y, c = _my_pos()
        cps = [pltpu.make_async_remote_copy(src_ref=ins[f], dst_ref=outs[f].at[c], send_sem=send_sems.at[f],
                                            recv_sem=recv_sems.at[f], device_id=(x, y, 1 - c), device_id_type=MESH)
               for f in range(n)]
        for cp in cps:
            cp.start()
        for f in range(n):
            pltpu.make_async_remote_copy(src_ref=ins[f], dst_ref=outs[f].at[1 - c], send_sem=send_sems.at[f],
                                         recv_sem=recv_sems.at[f], device_id=(x, y, 1 - c),
                                         device_id_type=MESH).wait_recv()
        for cp in cps:
            cp.wait_send()

    outs = pl.pallas_call(
        body, out_shape=[jax.ShapeDtypeStruct((2,) + t.shape, t.dtype) for t in halves],
        in_specs=[_HBM] * n, out_specs=[_HBM] * n,
        scratch_shapes=[pltpu.SemaphoreType.DMA((n,)), pltpu.SemaphoreType.DMA((n,))], name=name)(*halves)
    c = lax.axis_index("c")
    return [lax.dynamic_update_index_in_dim(o, t, c, 0) for o, t in zip(outs, halves)]


_HBM_ONLY = pl.BlockSpec(memory_space=pltpu.HBM)
_SEMS = pl.BlockSpec(memory_space=pltpu.SEMAPHORE)
_EFFECT = pltpu.SideEffectType.DATAFLOW_SIDE_EFFECTING


def _copies_start(srcs, lands, plan, n_copies, *, name):
    n, m = len(srcs), len(lands)

    def body(*refs):
        src_refs, land_refs = refs[:n], refs[n:n + m]
        send_sems, recv_sems, token = refs[n + m], refs[n + m + 1], refs[-1]
        for k, (src, dst, peer) in enumerate(plan(src_refs, land_refs)):
            pltpu.make_async_remote_copy(src_ref=src, dst_ref=dst, send_sem=send_sems.at[k], recv_sem=recv_sems.at[k],
                                         device_id=peer, device_id_type=MESH).start()
        token[...] = jnp.zeros_like(token)

    bufs = [pltpu.with_memory_space_constraint(t, pltpu.HBM) for t in (*srcs, *lands)]
    outs = pl.pallas_call(
        body, name=name,
        out_shape=(pltpu.SemaphoreType.DMA((n_copies,)), pltpu.SemaphoreType.DMA((n_copies,)),
                   *[pltpu.HBM(t.shape, t.dtype) for t in bufs], jax.ShapeDtypeStruct((8, 128), F32)),
        in_specs=[_HBM_ONLY] * (n + m),
        out_specs=(_SEMS, _SEMS, *[_HBM_ONLY] * (n + m), pl.BlockSpec(memory_space=pltpu.VMEM)),
        input_output_aliases={k: 2 + k for k in range(n + m)},
        compiler_params=pltpu.CompilerParams(has_side_effects=_EFFECT))(*bufs)
    return outs[0], outs[1], list(outs[2:2 + n + m]), outs[-1]


def _copies_wait(send_sems, recv_sems, thru, n_src, plan, after, *, name):
    nm = len(thru)

    def body(*refs):
        t_refs, send, recv = refs[:nm], refs[nm], refs[nm + 1]
        for k, (src, dst, peer) in enumerate(plan(t_refs[:n_src], t_refs[n_src:])):
            cp = pltpu.make_async_remote_copy(src_ref=src, dst_ref=dst, send_sem=send.at[k], recv_sem=recv.at[k],
                                              device_id=peer, device_id_type=MESH)
            cp.wait_send()
            cp.wait_recv()

    outs = pl.pallas_call(
        body, name=name, out_shape=tuple(pltpu.HBM(t.shape, t.dtype) for t in thru),
        in_specs=[_HBM_ONLY] * nm + [_SEMS, _SEMS, pl.BlockSpec(memory_space=pl.ANY)],
        out_specs=tuple([_HBM_ONLY] * nm), input_output_aliases={k: k for k in range(nm)},
        compiler_params=pltpu.CompilerParams(has_side_effects=_EFFECT))(*thru, send_sems, recv_sems, after)
    return list(outs)


_RELATIONS = [(dx, dy, dc) for dx in (0, 1) for dy in (0, 1) for dc in (0, 1)][1:]


def _gather_plan(src_refs, land_refs):
    x, y, c = _my_pos()
    flip = lambda v, d: 1 - v if d else v
    return [(s_ref.at[c], l_ref.at[2 * x + y, c], (flip(x, dx), flip(y, dy), flip(c, dc)))
            for s_ref, l_ref in zip(src_refs, land_refs) for dx, dy, dc in _RELATIONS]


def _sibling_plan(src_refs, land_refs):
    x, y, c = _my_pos()
    return [(s_ref.at[1 - c], l_ref, (x, y, 1 - c)) for s_ref, l_ref in zip(src_refs, land_refs)]


def _chips_plan(src_refs, land_refs):
    x, y, c = _my_pos()
    chips = [(1 - x, y), (x, 1 - y), (1 - x, 1 - y)]
    return [(s_ref.at[2 * cx + cy], l_ref.at[k], (cx, cy, c))
            for s_ref, l_ref in zip(src_refs, land_refs) for k, (cx, cy) in enumerate(chips)]


def _place_own(gathered, fam):
    x, y, c = _my_pos()
    own = lax.dynamic_index_in_dim(fam, c, 0, keepdims=True)[None]
    return lax.dynamic_update_slice(gathered, own, (2 * x + y, c) + (0,) * (fam.ndim - 1))


def _to_heads(t, width):
    return t.reshape(t.shape[0], HEADS, width).transpose(1, 0, 2)


def _from_heads(t):
    return t.transpose(1, 0, 2).reshape(t.shape[1], -1)


def _t5_bucket(dist):
    max_exact = N_BUCKETS // 2
    d = jnp.maximum(dist, 1).astype(F32)
    large = max_exact + (jnp.log(d / max_exact) / math.log(MAX_DISTANCE / max_exact)
                         * (N_BUCKETS - max_exact)).astype(jnp.int32)
    large = jnp.minimum(large, N_BUCKETS - 1)
    return jnp.where(dist < max_exact, dist, large)


def _bucket_map(dilation):
    iq = jnp.arange(DIL_BLOCK)[:, None]
    ik = jnp.arange(2 * DIL_BLOCK)[None, :]
    rel = DIL_BLOCK + iq - ik
    return _t5_bucket(jnp.maximum(rel, 0) * dilation).astype(jnp.int32)


def _q_perm(w):
    w3 = w.reshape(w.shape[0], HEADS, QK_NOPE + QK_ROPE)
    return jnp.concatenate([w3[:, :, :QK_NOPE].reshape(w.shape[0], -1),
                            w3[:, :, QK_NOPE:QK_NOPE + HALF_ROPE].reshape(w.shape[0], -1),
                            w3[:, :, QK_NOPE + HALF_ROPE:].reshape(w.shape[0], -1)], axis=1)


def _q_unperm(w):
    n0, n1 = HEADS * QK_NOPE, HEADS * HALF_ROPE
    r = w.shape[0]
    return jnp.concatenate([w[:, :n0].reshape(r, HEADS, QK_NOPE), w[:, n0:n0 + n1].reshape(r, HEADS, HALF_ROPE),
                            w[:, n0 + n1:].reshape(r, HEADS, HALF_ROPE)], axis=2).reshape(r, -1)


def _kv_perm(w):
    w3 = w.reshape(w.shape[0], HEADS, QK_NOPE + V_HEAD)
    return jnp.concatenate([w3[:, :, :QK_NOPE].reshape(w.shape[0], -1), w3[:, :, QK_NOPE:].reshape(w.shape[0], -1)],
                           axis=1)


def _kv_unperm(w):
    n0 = HEADS * QK_NOPE
    r = w.shape[0]
    return jnp.concatenate([w[:, :n0].reshape(r, HEADS, QK_NOPE), w[:, n0:].reshape(r, HEADS, V_HEAD)],
                           axis=2).reshape(r, -1)


def _row(v):
    return v.reshape(1, -1)


def kernel(x, c, norm_pre, norm_post, w_mod, b_mod, ffn_w_gate, ffn_w_up, ffn_w_down, mla_w_in, mla_q_norm, mla_w_q_up, mla_kv_norm, mla_w_kv_up, mla_w_o, dil_w_in, dil_w_o, rel_bias, loss_target, m_norm_pre, m_norm_post, m_w_mod, m_b_mod, m_ffn_w_gate, m_ffn_w_up, m_ffn_w_down, m_mla_w_in, m_mla_q_norm, m_mla_w_q_up, m_mla_kv_norm, m_mla_w_kv_up, m_mla_w_o, m_dil_w_in, m_dil_w_o, m_rel_bias, v_norm_pre, v_norm_post, v_w_mod, v_b_mod, v_ffn_w_gate, v_ffn_w_up, v_ffn_w_down, v_mla_w_in, v_mla_q_norm, v_mla_w_q_up, v_mla_kv_norm, v_mla_w_kv_up, v_mla_w_o, v_dil_w_in, v_dil_w_o, v_rel_bias):
    given = dict(locals())
    ix, iy, ic = _my_pos()
    shard_id = 2 * ix + iy
    dev_id = 4 * ix + 2 * iy + ic
    x2 = x[0]
    target = loss_target[0]
    half_idx = jnp.reshape(ic, (1,)).astype(jnp.int32)
    shard_idx = jnp.reshape(shard_id, (1,)).astype(jnp.int32)

    blk = jnp.zeros((8, D_MODEL), F32)
    blk = blk.at[0].set(c[0])
    blk = blk.at[1:3].set(jnp.pad(norm_pre.reshape(-1), (0, 512)).reshape(2, D_MODEL))
    blk = blk.at[3:5].set(jnp.pad(norm_post.reshape(-1), (0, 512)).reshape(2, D_MODEL))
    got = _all_gather(blk, name="ag_c_norms", in_vmem=True).reshape(N_SHARD, 2, 8, D_MODEL)
    c_all = got[:, :, 0, :].reshape(8, D_MODEL)

    def full_norm(lo):
        t = got[:, 0, lo:lo + 2, :].reshape(N_SHARD, 2 * D_MODEL)[:, :1536].reshape(N_SHARD, 2, 3, 256)
        return t.transpose(1, 2, 0, 3).reshape(2, 3, D_MODEL)

    pre_full, post_full = full_norm(1), full_norm(3)

    silu_c = _silu_bf16(c_all, name="silu_c")
    b_cols = lax.dynamic_slice_in_dim(b_mod, shard_id * 2304, 2304, axis=1).reshape(2, 1, 2304)
    mod_part = _mm(silu_c, w_mod, bias=b_cols, name="mod_mm", tn_cap=768)
    mod_all = _all_gather(mod_part.reshape(16, 2304), name="ag_mod", in_vmem=True)
    mod_all = mod_all.reshape(N_SHARD, 2, 2, 8, 2304)[:, 0]
    mod_mine = lax.dynamic_index_in_dim(mod_all, dev_id, axis=2, keepdims=False)
    mod = mod_mine.transpose(1, 0, 2).reshape(2, 9, D_MODEL)

    bf = lambda t: t.astype(BF16)
    ffn_fam = lambda i, h: [bf(jnp.stack([ffn_w_gate[i, h], ffn_w_up[i, h]])),
                            bf(ffn_w_down[i, h].reshape(2, F_SHARD // 2, D_MODEL))]
    mla_fam = [bf(mla_w_in.reshape(2, 128, -1)), bf(mla_w_q_up.reshape(2, 192, -1)),
               bf(mla_w_kv_up.reshape(2, 128, -1)), bf(mla_w_o.reshape(2, 128, D_MODEL))]
    dil_fam = [bf(dil_w_in.reshape(2, 512, -1)), bf(dil_w_o.reshape(2, 128, D_MODEL))]
    later_fams = [ffn_fam(1, 0) + dil_fam, ffn_fam(1, 1)]
    full, later_fams, mod = lax.optimization_barrier(
        (_gather_weights(ffn_fam(0, 0) + mla_fam + ffn_fam(0, 1), name="ag_weights_first"), later_fams, mod))

    def gather_later(fams, tag):
        lands = [lax.empty((N_SHARD,) + t.shape, t.dtype) for t in fams]
        send, recv, thru, token = _copies_start(fams, lands, _gather_plan, 7 * len(fams), name=f"ag_start_{tag}")
        return dict(send=send, recv=recv, thru=thru, token=token, n=len(fams), tag=tag)

    def arrive(st, after):
        thru = _copies_wait(st['send'], st['recv'], st['thru'], st['n'], _gather_plan, after,
                            name=f"ag_wait_{st['tag']}")
        return [_place_own(o, t) for t, o in zip(thru[:st['n']], thru[st['n']:])]

    in_flight = gather_later(later_fams[0], "l1s01")
    as_ffn = lambda w_gu, w_dn: (w_gu, w_dn.reshape(N_SHARD, F_SHARD, D_MODEL))
    ffn_w = {(0, 0): as_ffn(full[0], full[1]), (0, 1): as_ffn(full[6], full[7])}
    w_in = full[2].reshape(D_MODEL, -1)
    wq_p = _q_perm(full[3].reshape(N_SHARD, Q_LORA, -1).transpose(1, 0, 2).reshape(Q_LORA, -1))
    wkv_p = _kv_perm(full[4].reshape(N_SHARD, KV_LORA, -1).transpose(1, 0, 2).reshape(KV_LORA, -1))
    w_mo = full[5].reshape(D_MODEL, D_MODEL)
    dil_w = {}

    pos = jnp.arange(SEQ, dtype=F32)
    freqs = ROPE_THETA ** (-jnp.arange(HALF_ROPE, dtype=F32) / HALF_ROPE)
    ang = pos[:, None] * freqs[None, :]
    cos_k, sin_k = jnp.cos(ang), jnp.sin(ang)
    cos_q, sin_q = jnp.tile(cos_k, (1, HEADS)), jnp.tile(sin_k, (1, HEADS))

    buckets = [_bucket_map(d) for _, d in DIL_GROUPS]
    biases = [_bias_table(rel_bias[:, g * HEADS:(g + 1) * HEADS].T.reshape(HEADS, 1, N_BUCKETS), bk,
                          name=f"dil_bias_table_g{g}") for g, bk in enumerate(buckets)]

    def sub_params(i, sub):
        return dict(pg=_row(pre_full[i, sub]), qg=_row(post_full[i, sub]), sh=_row(mod[i, 3 * sub]),
                    sc=_row(mod[i, 3 * sub + 1]), gate=_row(mod[i, 3 * sub + 2]))

    def ffn_fwd(xin, i, h, sub, tie=None):
        p = sub_params(i, sub)
        if tie is not None:
            p['sh'] = p['sh'] + tie
        tag = f"l{i}s{sub}"
        w_gu, w_dn = ffn_w[i, h]
        hn = _pre_fwd(xin, p['pg'], p['sc'], p['sh'], name=f"pre_fwd_{tag}")
        gu, a = _ffn_up(hn, w_gu, name=f"ffn_up_{tag}")
        f = _mm(a, w_dn, reduce_g=True, name=f"ffn_down_{tag}")
        out = _post_fwd(f, xin, p['qg'], p['gate'], FFN_RES, name=f"post_fwd_{tag}")
        return out, dict(x=xin, hn=hn, gu=gu, a=a, f=f, p=p, i=i, h=h, tag=tag)

    def mla_fwd(xin, i, sub):
        p = sub_params(i, sub)
        tag = f"l{i}s{sub}"
        hn = _pre_fwd(xin, p['pg'], p['sc'], p['sh'], name=f"pre_fwd_{tag}")
        lat = _mm(hn, w_in, name="mla_lat")
        cq, ckv = lat[:, :Q_LORA], lat[:, Q_LORA:Q_LORA + KV_LORA]
        k1, k2 = lat[:, Q_LORA + KV_LORA:Q_LORA + KV_LORA + HALF_ROPE], lat[:, Q_LORA + KV_LORA + HALF_ROPE:]
        cqn = _rms_fwd(cq, mla_q_norm, name="mla_qnorm")
        ckvn = _rms_fwd(ckv, mla_kv_norm, name="mla_kvnorm")
        qp = _mm(cqn, wq_p, name="mla_q_up")
        kvp = _mm(ckvn, wkv_p, name="mla_kv_up")
        n0, n1 = HEADS * QK_NOPE, HEADS * HALF_ROPE
        qr1, qr2 = _rope(qp[:, n0:n0 + n1], qp[:, n0 + n1:], cos_q, sin_q, name="rope_q")
        kr1, kr2 = _rope(k1, k2, cos_k, sin_k, name="rope_k")
        q = jnp.concatenate([qp[:, :n0].reshape(SEQ, HEADS, QK_NOPE), qr1.reshape(SEQ, HEADS, HALF_ROPE),
                             qr2.reshape(SEQ, HEADS, HALF_ROPE)], axis=2).transpose(1, 0, 2).astype(BF16)
        kr = jnp.broadcast_to(jnp.concatenate([kr1, kr2], axis=1)[:, None, :], (SEQ, HEADS, QK_ROPE))
        k = jnp.concatenate([kvp[:, :n0].reshape(SEQ, HEADS, QK_NOPE), kr], axis=2).transpose(1, 0, 2).astype(BF16)
        v = _to_heads(kvp[:, n0:], V_HEAD).astype(BF16)
        o, lse = _mla_attn_fwd(q, k, v, name="mla_attn_fwd")
        o_flat = _from_heads(o).astype(BF16)
        f = _mm(o_flat, w_mo, name="mla_out")
        out = _post_fwd(f, xin, p['qg'], p['gate'], 1.0, name=f"post_fwd_{tag}")
        return out, dict(x=xin, hn=hn, cq=cq, ckv=ckv, cqn=cqn, ckvn=ckvn, q=q, k=k, v=v, o=o, lse=lse,
                         o_flat=o_flat, f=f, p=p, tag=tag)

    def dil_fwd(xin, i, sub):
        p = sub_params(i, sub)
        tag = f"l{i}s{sub}"
        hn = _pre_fwd(xin, p['pg'], p['sc'], p['sh'], name=f"pre_fwd_{tag}")
        heads = _proj_heads(hn, dil_w['in'], name="dil_proj")
        outs, lses = [], []
        for g, (window, d) in enumerate(DIL_GROUPS):
            o, lse = _dil_attn_fwd(heads, biases[g], g, d, name=f"dil_attn_fwd_g{g}")
            outs.append(o)
            lses.append(lse)
        mix = _dil_mix_fwd(outs, lses, name="dil_mix_fwd")
        o_flat = _from_heads(mix).astype(BF16)
        f = _mm(o_flat, dil_w['out'], name="dil_out")
        out = _post_fwd(f, xin, p['qg'], p['gate'], 1.0, name=f"post_fwd_{tag}")
        return out, dict(x=xin, hn=hn, heads=heads, outs=outs, lses=lses, o_flat=o_flat, f=f, p=p, tag=tag)

    saved = [None] * 6
    xs, saved[0] = ffn_fwd(x2, 0, 0, 0, tie=in_flight['token'][0, 0])
    xs, saved[1] = mla_fwd(xs, 0, 1)
    xs, saved[2] = ffn_fwd(xs, 0, 1, 2)
    got, last_fams = lax.optimization_barrier((arrive(in_flight, xs), later_fams[1]))
    ffn_w[1, 0] = as_ffn(got[0], got[1])
    dil_w['in'], dil_w['out'] = got[2].reshape(N_SHARD, D_MODEL, -1), got[3].reshape(D_MODEL, D_MODEL)
    in_flight = gather_later(last_fams, "l1s2")
    xs, saved[3] = ffn_fwd(xs, 1, 0, 0, tie=in_flight['token'][0, 0])
    xs, saved[4] = dil_fwd(xs, 1, 1)
    ffn_w[1, 1] = as_ffn(*arrive(in_flight, xs))
    xs, saved[5] = ffn_fwd(xs, 1, 1, 2)

    dx, loss_part = _loss(xs, target, name="loss")

    dmod = [[None] * 9 for _ in range(2)]
    dpre = [[None] * 3 for _ in range(2)]
    dpost = [[None] * 3 for _ in range(2)]
    ffn_units = {}
    row_unit = lambda g, r, j: ((r % 2, r // 2), 0, j)

    def close_sub(dhn, dout, sv, i, sub, res_dgate, res_dqg):
        p = sv['p']
        dxs, dsh, dsc, dpg = _pre_bwd(dhn, sv['x'], dout, p['pg'], p['sc'], name=f"pre_bwd_{sv['tag']}")
        dmod[i][3 * sub], dmod[i][3 * sub + 1], dmod[i][3 * sub + 2] = dsh, dsc, res_dgate
        dpre[i][sub], dpost[i][sub] = dpg, res_dqg
        return dxs

    def ffn_bwd(dout, sv, sub, tie=0.0):
        i, h, p, tag = sv['i'], sv['h'], sv['p'], sv['tag']
        w_gu, w_dn = ffn_w[i, h]
        df, dgate, dqg = _post_bwd(dout, sv['f'], p['qg'] + tie, p['gate'], FFN_RES, name=f"post_bwd_{tag}")
        u_dn = _mm(sv['a'], df, ta=True, tn_cap=D_MODEL // 2, out_shape=(2, N_SHARD, F_SHARD, D_MODEL // 2),
                   out_sel=lambda g, r, j: ((j, g), r, 0), name=f"ffn_dwd_{tag}")
        dgu = _ffn_dgu(df, w_dn, sv['gu'], name=f"ffn_dgu_{tag}")
        dgu = dgu.reshape(2 * N_SHARD, SEQ, F_SHARD)
        u_gu = _mm(dgu, sv['hn'], ta=True, out_shape=(2, N_SHARD, F_SHARD, D_MODEL),
                   out_sel=lambda g, r, j: ((g % 2, g // 2), r, j), name=f"ffn_dwgu_{tag}")
        ffn_units[i, h] = [u_gu, u_dn]
        dhn = _mm(dgu, w_gu.reshape(2 * N_SHARD, D_MODEL, F_SHARD), tb=True, reduce_g=True, name=f"ffn_dhn_{tag}")
        return close_sub(dhn, dout, sv, i, sub, dgate, dqg)

    def mla_bwd(dout, sv, i, sub, tie=0.0):
        p, tag = sv['p'], sv['tag']
        df, dgate, dqg = _post_bwd(dout, sv['f'], p['qg'] + tie, p['gate'], 1.0, name=f"post_bwd_{tag}")
        u_wo = _mm(sv['o_flat'], df, ta=True, tm_cap=128, out_shape=(2, N_SHARD, 128, D_MODEL), out_sel=row_unit,
                   name="mla_dwo")
        do_flat = _mm(df, w_mo, tb=True, name="mla_do")
        do = _to_heads(do_flat, V_HEAD)
        dq, dk, dv = _mla_attn_bwd(sv['q'], sv['k'], sv['v'], sv['o'], do, sv['lse'], name="mla_attn_bwd")
        dq_t = dq.transpose(1, 0, 2)
        dqr1, dqr2 = _rope(dq_t[:, :, QK_NOPE:QK_NOPE + HALF_ROPE].reshape(SEQ, -1),
                           dq_t[:, :, QK_NOPE + HALF_ROPE:].reshape(SEQ, -1), cos_q, -sin_q, name="rope_q_bwd")
        dqp = jnp.concatenate([dq_t[:, :, :QK_NOPE].reshape(SEQ, -1), dqr1, dqr2], axis=1).astype(BF16)
        dkr = _head_sum(dk[:, :, QK_NOPE:], name="mla_dkr_sum")
        dk1, dk2 = _rope(dkr[:, :HALF_ROPE], dkr[:, HALF_ROPE:], cos_k, -sin_k, name="rope_k_bwd")
        dkvp = jnp.concatenate([_from_heads(dk[:, :, :QK_NOPE]), _from_heads(dv)], axis=1).astype(BF16)
        g_wq = _q_unperm(_mm(sv['cqn'], dqp, ta=True, name="mla_dwq"))
        g_wkv = _kv_unperm(_mm(sv['ckvn'], dkvp, ta=True, name="mla_dwkv"))
        dcqn = _mm(dqp, wq_p, tb=True, name="mla_dcqn")
        dckvn = _mm(dkvp, wkv_p, tb=True, name="mla_dckvn")
        dcq, g_qn = _rms_bwd(dcqn, sv['cq'], mla_q_norm, name="mla_qnorm_bwd")
        dckv, g_kvn = _rms_bwd(dckvn, sv['ckv'], mla_kv_norm, name="mla_kvnorm_bwd")
        dlat = jnp.concatenate([dcq, dckv, dk1, dk2], axis=1).astype(BF16)
        u_win = _mm(sv['hn'], dlat, ta=True, tm_cap=128, out_shape=(2, N_SHARD, 128, dlat.shape[1]),
                    out_sel=row_unit, name="mla_dwin")
        dhn = _mm(dlat, w_in, tb=True, name="mla_dhn")
        col_unit = lambda t: (t.reshape(t.shape[0], N_SHARD, -1).transpose(1, 0, 2)
                              .reshape(N_SHARD, 2, t.shape[0] // 2, -1).transpose(1, 0, 2, 3))
        grads = dict(units=[u_win, col_unit(g_wq), col_unit(g_wkv), u_wo], q_norm=g_qn, kv_norm=g_kvn)
        return close_sub(dhn, dout, sv, i, sub, dgate, dqg), grads

    def dil_bwd(dout, sv, i, sub):
        p, tag = sv['p'], sv['tag']
        df, dgate, dqg = _post_bwd(dout, sv['f'], p['qg'], p['gate'], 1.0, name=f"post_bwd_{tag}")
        u_wo = _mm(sv['o_flat'], df, ta=True, tm_cap=128, out_shape=(2, N_SHARD, 128, D_MODEL), out_sel=row_unit,
                   name="dil_dwo")
        do = _to_heads(_mm(df, dil_w['out'], tb=True, name="dil_do"), 64)
        dos, dlts = _dil_mix_bwd(do, sv['outs'], sv['lses'], name="dil_mix_bwd")
        pieces = []
        bias_rows = []
        for g, (window, d) in enumerate(DIL_GROUPS):
            dq, dk, dv, dbias = _dil_attn_bwd(sv['heads'], biases[g], sv['lses'][g], dos[g], dlts[g], g, d,
                                              name=f"dil_attn_bwd_g{g}")
            pieces += [dq, dk, dv]
            bias_rows.append(_bias_grad(dbias, buckets[g], name=f"dil_bias_grad_g{g}")[:, 0, :])
        dheads = jnp.concatenate(pieces).astype(BF16)
        u_win = _proj_heads_dw(sv['hn'], dheads, name="dil_dwin")
        dhn = _proj_heads_dx(dheads, dil_w['in'], name="dil_dhn")
        g_bias = jnp.concatenate(bias_rows, axis=0).T
        grads = dict(units=[u_win, u_wo], rel_bias=g_bias)
        return close_sub(dhn, dout, sv, i, sub, dgate, dqg), grads

    def to_sibling(units, tag):
        n = len(units)
        send, recv, thru, token = _copies_start(units, [lax.empty(u.shape[1:], F32) for u in units], _sibling_plan, n,
                                                name=f"rs{tag}_sibling_start")
        return dict(send=send, recv=recv, thru=thru, n=n, tag=tag), token[0, 0]

    def from_sibling(st, after):
        n, tag = st['n'], st['tag']
        thru = _copies_wait(st['send'], st['recv'], st['thru'], n, _sibling_plan, after, name=f"rs{tag}_sibling_wait")
        return [_add_half(u, g, half_idx, name=f"rs{tag}_add_half_{k}") for k, (u, g) in enumerate(zip(thru[:n], thru[n:]))]

    def to_chips(parts, tag):
        n = len(parts)
        send, recv, thru, token = _copies_start([w for _, w in parts],
                                                [lax.empty((3,) + w.shape[1:], BF16) for _, w in parts], _chips_plan,
                                                3 * n, name=f"rs{tag}_chips_start")
        return dict(send=send, recv=recv, thru=thru, n=n, tag=tag, parts=parts), token[0, 0]

    def from_chips(st, after):
        n, tag = st['n'], st['tag']
        thru = _copies_wait(st['send'], st['recv'], st['thru'], n, _chips_plan, after, name=f"rs{tag}_chips_wait")
        return [_add_shards(p, g, shard_idx, name=f"rs{tag}_add_shards_{k}")
                for k, ((p, _), g) in enumerate(zip(st['parts'], thru[n:]))]

    dx = ffn_bwd(dx, saved[5], 2)
    dx, dil_g = dil_bwd(dx, saved[4], 1, 1)
    dx = ffn_bwd(dx, saved[3], 0)
    st1, tok = to_sibling([*ffn_units[1, 1], *dil_g['units'], *ffn_units[1, 0]], "1")
    dx = ffn_bwd(dx, saved[2], 2, tie=tok)
    st1, tok1 = to_chips(from_sibling(st1, dx), "1")
    st2, tok2 = to_sibling(ffn_units[0, 1], "2")
    dx, mla_g = mla_bwd(dx, saved[1], 0, 1, tie=tok1 + tok2)
    reds1 = from_chips(st1, dx)
    st2, tok = to_chips(from_sibling(st2, dx), "2")
    dx = ffn_bwd(dx, saved[0], 0, tie=tok)
    reds2 = from_chips(st2, dx)
    grad_x = dx[None]

    pad_row = lambda v: jnp.pad(v.reshape(-1), (0, (-v.size) % D_MODEL)).reshape(-1, D_MODEL)
    small = jnp.concatenate(
        [jnp.concatenate([dmod[i][r] for i in range(2) for r in range(9)], axis=0),
         jnp.concatenate([dpre[i][s] for i in range(2) for s in range(3)], axis=0),
         jnp.concatenate([dpost[i][s] for i in range(2) for s in range(3)], axis=0),
         pad_row(mla_g['q_norm']), pad_row(mla_g['kv_norm']), pad_row(dil_g['rel_bias']), pad_row(loss_part)], axis=0)
    small = jnp.pad(small, ((0, SMALL_ROWS - small.shape[0]), (0, 0)))
    small_all = _all_gather(small, name="ag_small_grads", in_vmem=True)
    small_sum = _sum_devices(small_all, 8, name="sum_small_grads")
    g_b_mod = small_sum[0:18].reshape(2, 9 * D_MODEL)
    my_cols = lambda t: lax.dynamic_slice_in_dim(t, shard_id * 256, 256, axis=2)
    g_norm_pre = my_cols(small_sum[18:24].reshape(2, 3, D_MODEL))
    g_norm_post = my_cols(small_sum[24:30].reshape(2, 3, D_MODEL))
    g_q_norm = small_sum[30, :Q_LORA].reshape(1, Q_LORA)
    g_kv_norm = small_sum[31, :KV_LORA].reshape(1, KV_LORA)
    g_rel_bias = small_sum[32:34].reshape(-1)[:N_BUCKETS * 48].reshape(N_BUCKETS, 48)
    loss = small_sum[34, 0]
    dmod_all = small_all.reshape(8, SMALL_ROWS, D_MODEL)[:, 0:18].reshape(8, 2, 9 * D_MODEL)
    dmod_cols = lax.dynamic_slice_in_dim(dmod_all, shard_id * 2304, 2304, axis=2).transpose(1, 0, 2)
    g_w_mod = _mm(silu_c, dmod_cols.astype(BF16), ta=True, tn_cap=768, name="w_mod_grad")

    units0 = [*mla_g['units'], *ffn_units[0, 0]]
    got_a = _swap_halves(units0, name="rs0_sibling")
    parts0 = [_add_half(u, g, half_idx, name=f"rs0_add_half_{k}") for k, (u, g) in enumerate(zip(units0, got_a))]
    got_b = _send_to_chips([w for _, w in parts0], name="rs0_chips")
    reds0 = [_add_shards(p, g, shard_idx, name=f"rs0_add_shards_{k}")
             for k, ((p, _), g) in enumerate(zip(parts0, got_b))]
    fin = _pair_gather(reds1 + reds2 + reds0, name="rs_pair_gather")
    ffn_fin = {(1, 1): fin[0:2], (1, 0): fin[4:6], (0, 1): fin[6:8], (0, 0): fin[12:14]}
    swap = lambda t: jnp.swapaxes(t, 2, 3)
    per_ffn = lambda pick: jnp.stack([jnp.stack([pick(*ffn_fin[i, h]) for h in range(2)]) for i in range(2)])
    reduced = dict(ffn_w_gate=swap(per_ffn(lambda gu, dn: gu[0])), ffn_w_up=swap(per_ffn(lambda gu, dn: gu[1])),
                   ffn_w_down=per_ffn(lambda gu, dn: jnp.concatenate([dn[0], dn[1]], axis=1)))
    for n, t in zip(['dil_w_in', 'dil_w_o', 'mla_w_in', 'mla_w_q_up', 'mla_w_kv_up', 'mla_w_o'], fin[2:4] + fin[8:12]):
        reduced[n] = t.reshape(given[n].shape)

    grads = dict(norm_pre=g_norm_pre, norm_post=g_norm_post, w_mod=g_w_mod, b_mod=g_b_mod, mla_q_norm=g_q_norm,
                 mla_kv_norm=g_kv_norm, rel_bias=g_rel_bias, **reduced)

    deltas, new_m, new_v = {}, {}, {}
    for n in WEIGHTS:
        view = swap if n in ('ffn_w_gate', 'ffn_w_up') else (lambda t: t)
        outs = _adamw(view(given[n]), view(grads[n]), view(given["m_" + n]), view(given["v_" + n]), name=f"adamw_{n}")
        deltas[n], new_m[n], new_v[n] = (view(t) for t in outs)
    return (loss, grad_x, *[grads[n] for n in WEIGHTS], *[deltas[n] for n in WEIGHTS],
            *[new_m[n] for n in WEIGHTS], *[new_v[n] for n in WEIGHTS])
```

```python
import math

import jax
import jax.numpy as jnp
from jax import lax
from jax.experimental import pallas as pl
from jax.experimental.pallas import tpu as pltpu

F32 = jnp.float32
BF16 = jnp.bfloat16
MESH = pl.DeviceIdType.MESH

SEQ = 2048
D_MODEL = 1024
D_FF = 2816
N_SHARD = 4
F_SHARD = D_FF // N_SHARD
EPS = 1e-6
FFN_RES = 0.5
HEADS = 16
Q_LORA, KV_LORA, QK_NOPE, QK_ROPE, V_HEAD = 384, 256, 64, 32, 64
HALF_ROPE = QK_ROPE // 2
ROPE_THETA = 10000.0
DIL_GROUPS = ((128, 1), (512, 4), (2048, 16))
DIL_BLOCK = 128
N_BUCKETS = 32
MAX_DISTANCE = 2048
ADAM_LR, ADAM_B1, ADAM_B2, ADAM_EPS, ADAM_WD, ADAM_STEP = 0.001, 0.9, 0.999, 1e-08, 0.01, 10

VMEM_LIMIT = 48 * 1024 * 1024
SMALL_ROWS = 40

WEIGHTS = ['norm_pre', 'norm_post', 'w_mod', 'b_mod', 'ffn_w_gate', 'ffn_w_up', 'ffn_w_down', 'mla_w_in',
           'mla_q_norm', 'mla_w_q_up', 'mla_kv_norm', 'mla_w_kv_up', 'mla_w_o', 'dil_w_in', 'dil_w_o', 'rel_bias']


def _cparams(**kw):
    return pltpu.CompilerParams(vmem_limit_bytes=VMEM_LIMIT, **kw)


def _pick(n, cap, mult=128):
    if n <= cap:
        return n
    best = n
    for t in range(mult, cap + 1, mult):
        if n % t == 0:
            best = t
    return best


def _mm(a, b, *, name, ta=False, tb=False, reduce_g=False, bias=None, out_dtype=F32, tm_cap=512, tn_cap=1024,
        g_n=None, b_sel=None, out_shape=None, out_sel=None, out_buf=None):
    a3 = a if a.ndim == 3 else a[None]
    ga = a3.shape[0]
    if b_sel is None:
        b_n = b if b.ndim == 3 else b[None]
        gb = b_n.shape[0]
        b_sel = (lambda g: (g,)) if gb > 1 else (lambda g: (0,))
        g_n = max(ga, gb)
    else:
        b_n = b
    k_dim, m_dim = (a3.shape[1], a3.shape[2]) if ta else (a3.shape[2], a3.shape[1])
    k2, n_dim = (b_n.shape[-1], b_n.shape[-2]) if tb else (b_n.shape[-2], b_n.shape[-1])
    assert k_dim == k2, (a.shape, b.shape)
    tm = _pick(m_dim, tm_cap, 128 if ta else 8)
    tn = _pick(n_dim, tn_cap, 128)
    mt, nt = m_dim // tm, n_dim // tn
    dims = (((0 if ta else 1,), (1 if tb else 0,)), ((), ()))

    if reduce_g:
        grid = (mt, nt, g_n)
        ids = lambda i, j, g: (g, i, j)
    else:
        grid = (g_n, mt, nt)
        ids = lambda g, i, j: (g, i, j)

    def a_map(*p):
        g, i, j = ids(*p)
        g = g if ga > 1 else 0
        return (g, 0, i) if ta else (g, i, 0)

    def b_map(*p):
        g, i, j = ids(*p)
        return (*b_sel(g), j, 0) if tb else (*b_sel(g), 0, j)

    b_lead = (None,) * (b_n.ndim - 2)
    a_spec = pl.BlockSpec((None, k_dim, tm) if ta else (None, tm, k_dim), a_map)
    b_spec = pl.BlockSpec(b_lead + ((tn, k_dim) if tb else (k_dim, tn)), b_map)
    in_specs = [a_spec, b_spec]
    operands = [a3, b_n]
    if bias is not None:
        assert not reduce_g and bias.shape == (g_n, 1, n_dim)
        in_specs.append(pl.BlockSpec((None, 1, tn), lambda g, i, j: (g, 0, j)))
        operands.append(bias)
    aliases = {}
    if out_buf is not None:
        assert tuple(out_buf.shape) == tuple(out_shape) and out_buf.dtype == out_dtype
        in_specs.append(pl.BlockSpec(memory_space=pl.ANY))
        operands.append(out_buf)
        aliases = {len(operands) - 1: 0}

    if reduce_g:
        out_spec = pl.BlockSpec((tm, tn), lambda i, j, g: (i, j))
        out_sds = jax.ShapeDtypeStruct((m_dim, n_dim), F32)
    elif out_shape is not None:
        def o_map(g, i, j):
            lead, rb, cb = out_sel(g, i, j)
            return (*lead, rb, cb)

        out_spec = pl.BlockSpec((None,) * (len(out_shape) - 2) + (tm, tn), o_map)
        out_sds = jax.ShapeDtypeStruct(tuple(out_shape), out_dtype)
    else:
        out_spec = pl.BlockSpec((None, tm, tn), lambda g, i, j: (g, i, j))
        out_sds = jax.ShapeDtypeStruct((g_n, m_dim, n_dim), out_dtype)

    def body(a_ref, b_ref, *rest):
        o_ref = rest[-1]
        r = lax.dot_general(a_ref[...].astype(BF16), b_ref[...].astype(BF16), dims, preferred_element_type=F32)
        if bias is not None:
            r = r + rest[0][...]
        if reduce_g:
            g = pl.program_id(2)

            @pl.when(g == 0)
            def _():
                o_ref[...] = r

            @pl.when(g > 0)
            def _():
                o_ref[...] += r
        else:
            o_ref[...] = r.astype(o_ref.dtype)

    out = pl.pallas_call(body, grid=grid, in_specs=in_specs, out_specs=out_spec, out_shape=out_sds,
                         input_output_aliases=aliases, compiler_params=_cparams(), name=name)(*operands)
    if not reduce_g and out_shape is None and a.ndim == 2 and b.ndim == 2:
        out = out[0]
    return out


def _rows(tm, w):
    return pl.BlockSpec((tm, w), lambda i: (i, 0))


def _vec(w):
    return pl.BlockSpec((1, w), lambda i: (0, 0))


def _rstd(v):
    return lax.rsqrt(jnp.mean(v * v, axis=-1, keepdims=True) + EPS)


def _pre_fwd(x, pg, sc, sh, *, name):
    s_n, w = x.shape
    tm = _pick(s_n, 256, 8)

    def body(x_ref, pg_ref, sc_ref, sh_ref, o_ref):
        xv = x_ref[...]
        n = (xv * _rstd(xv)) * pg_ref[...]
        o_ref[...] = (n * (1.0 + sc_ref[...]) + sh_ref[...]).astype(o_ref.dtype)

    return pl.pallas_call(body, grid=(s_n // tm,), in_specs=[_rows(tm, w), _vec(w), _vec(w), _vec(w)],
                          out_specs=_rows(tm, w), out_shape=jax.ShapeDtypeStruct((s_n, w), BF16),
                          compiler_params=_cparams(), name=name)(x, pg, sc, sh)


def _post_fwd(f, x, qg, gate, res_w, *, name):
    s_n, w = x.shape
    tm = _pick(s_n, 256, 8)

    def body(f_ref, x_ref, qg_ref, gate_ref, o_ref):
        fv = f_ref[...]
        y = (fv * _rstd(fv)) * qg_ref[...]
        o_ref[...] = x_ref[...] + (res_w * gate_ref[...]) * y

    return pl.pallas_call(body, grid=(s_n // tm,), in_specs=[_rows(tm, w), _rows(tm, w), _vec(w), _vec(w)],
                          out_specs=_rows(tm, w), out_shape=jax.ShapeDtypeStruct((s_n, w), F32),
                          compiler_params=_cparams(), name=name)(f, x, qg, gate)


def _post_bwd(dout, f, qg, gate, res_w, *, name):
    s_n, w = f.shape
    tm = _pick(s_n, 256, 8)

    def body(do_ref, f_ref, qg_ref, gate_ref, df_ref, dgate_ref, dqg_ref):
        @pl.when(pl.program_id(0) == 0)
        def _():
            dgate_ref[...] = jnp.zeros_like(dgate_ref)
            dqg_ref[...] = jnp.zeros_like(dqg_ref)

        do = do_ref[...]
        fv = f_ref[...]
        r = _rstd(fv)
        fh = fv * r
        qg_v = qg_ref[...]
        dgate_ref[...] += res_w * jnp.sum(do * (fh * qg_v), axis=0, keepdims=True)
        dy = do * (res_w * gate_ref[...])
        dqg_ref[...] += jnp.sum(dy * fh, axis=0, keepdims=True)
        dfh = dy * qg_v
        df = r * (dfh - fh * jnp.mean(dfh * fh, axis=-1, keepdims=True))
        df_ref[...] = df.astype(df_ref.dtype)

    return pl.pallas_call(
        body, grid=(s_n // tm,), in_specs=[_rows(tm, w), _rows(tm, w), _vec(w), _vec(w)],
        out_specs=[_rows(tm, w), _vec(w), _vec(w)],
        out_shape=[jax.ShapeDtypeStruct((s_n, w), BF16), jax.ShapeDtypeStruct((1, w), F32),
                   jax.ShapeDtypeStruct((1, w), F32)],
        compiler_params=_cparams(), name=name)(dout, f, qg, gate)


def _pre_bwd(dhn, x, dout, pg, sc, *, name):
    s_n, w = x.shape
    tm = _pick(s_n, 256, 8)

    def body(dhn_ref, x_ref, do_ref, pg_ref, sc_ref, dx_ref, dsh_ref, dsc_ref, dpg_ref):
        @pl.when(pl.program_id(0) == 0)
        def _():
            dsh_ref[...] = jnp.zeros_like(dsh_ref)
            dsc_ref[...] = jnp.zeros_like(dsc_ref)
            dpg_ref[...] = jnp.zeros_like(dpg_ref)

        dhn_v = dhn_ref[...]
        xv = x_ref[...]
        r = _rstd(xv)
        xh = xv * r
        pg_v = pg_ref[...]
        dsh_ref[...] += jnp.sum(dhn_v, axis=0, keepdims=True)
        dsc_ref[...] += jnp.sum(dhn_v * (xh * pg_v), axis=0, keepdims=True)
        dn = dhn_v * (1.0 + sc_ref[...])
        dpg_ref[...] += jnp.sum(dn * xh, axis=0, keepdims=True)
        dxh = dn * pg_v
        dx_ref[...] = do_ref[...] + r * (dxh - xh * jnp.mean(dxh * xh, axis=-1, keepdims=True))

    vec = jax.ShapeDtypeStruct((1, w), F32)
    return pl.pallas_call(
        body, grid=(s_n // tm,), in_specs=[_rows(tm, w), _rows(tm, w), _rows(tm, w), _vec(w), _vec(w)],
        out_specs=[_rows(tm, w), _vec(w), _vec(w), _vec(w)],
        out_shape=[jax.ShapeDtypeStruct((s_n, w), F32), vec, vec, vec],
        compiler_params=_cparams(), name=name)(dhn, x, dout, pg, sc)


def _rms_fwd(x, g, *, name):
    s_n, w = x.shape
    tm = _pick(s_n, 512, 8)

    def body(x_ref, g_ref, o_ref):
        xv = x_ref[...]
        o_ref[...] = ((xv * _rstd(xv)) * g_ref[...]).astype(o_ref.dtype)

    return pl.pallas_call(body, grid=(s_n // tm,), in_specs=[_rows(tm, w), _vec(w)], out_specs=_rows(tm, w),
                          out_shape=jax.ShapeDtypeStruct((s_n, w), BF16), compiler_params=_cparams(),
                          name=name)(x, g)


def _rms_bwd(dy, x, g, *, name):
    s_n, w = x.shape
    tm = _pick(s_n, 512, 8)

    def body(dy_ref, x_ref, g_ref, dx_ref, dg_ref):
        @pl.when(pl.program_id(0) == 0)
        def _():
            dg_ref[...] = jnp.zeros_like(dg_ref)

        dy_v = dy_ref[...]
        xv = x_ref[...]
        r = _rstd(xv)
        xh = xv * r
        dg_ref[...] += jnp.sum(dy_v * xh, axis=0, keepdims=True)
        dxh = dy_v * g_ref[...]
        dx_ref[...] = r * (dxh - xh * jnp.mean(dxh * xh, axis=-1, keepdims=True))

    return pl.pallas_call(
        body, grid=(s_n // tm,), in_specs=[_rows(tm, w), _rows(tm, w), _vec(w)],
        out_specs=[_rows(tm, w), _vec(w)],
        out_shape=[jax.ShapeDtypeStruct((s_n, w), F32), jax.ShapeDtypeStruct((1, w), F32)],
        compiler_params=_cparams(), name=name)(dy, x, g)


def _rope(a1, a2, cos, sin, *, name):
    s_n, w = a1.shape
    tm = _pick(s_n, 512, 8)

    def body(a1_ref, a2_ref, c_ref, s_ref, r1_ref, r2_ref):
        u, v, c_v, s_v = a1_ref[...], a2_ref[...], c_ref[...], s_ref[...]
        r1_ref[...] = u * c_v - v * s_v
        r2_ref[...] = u * s_v + v * c_v

    sd = jax.ShapeDtypeStruct((s_n, w), F32)
    return pl.pallas_call(body, grid=(s_n // tm,), in_specs=[_rows(tm, w)] * 4, out_specs=[_rows(tm, w)] * 2,
                          out_shape=[sd, sd], compiler_params=_cparams(), name=name)(a1, a2, cos, sin)


def _silu_bf16(x, *, name):
    def body(x_ref, o_ref):
        xv = x_ref[...]
        o_ref[...] = (xv * jax.nn.sigmoid(xv)).astype(o_ref.dtype)

    return pl.pallas_call(body, out_shape=jax.ShapeDtypeStruct(x.shape, BF16), name=name)(x)


def _loss(y, target, *, name):
    s_n, w = y.shape
    tm = _pick(s_n, 256, 8)

    def body(y_ref, t_ref, dy_ref, l_ref):
        @pl.when(pl.program_id(0) == 0)
        def _():
            l_ref[...] = jnp.zeros_like(l_ref)

        e = y_ref[...] - t_ref[...]
        dy_ref[...] = e * (1.0 / w)
        row = jnp.mean(e * e, axis=-1, keepdims=True)
        l_ref[...] += 0.5 * jnp.sum(row, axis=0, keepdims=True)

    return pl.pallas_call(
        body, grid=(s_n // tm,), in_specs=[_rows(tm, w), _rows(tm, w)],
        out_specs=[_rows(tm, w), pl.BlockSpec((1, 1), lambda i: (0, 0))],
        out_shape=[jax.ShapeDtypeStruct((s_n, w), F32), jax.ShapeDtypeStruct((1, 1), F32)],
        compiler_params=_cparams(), name=name)(y, target)


FFN_TM = 512


def _ffn_up(hn, w_gu, *, name):
    s_n, d = hn.shape
    f = w_gu.shape[-1]
    tm = _pick(s_n, FFN_TM, 8)

    def body(hn_ref, wg_ref, wu_ref, gu_ref, a_ref):
        xv = hn_ref[...]
        g = jnp.dot(xv, wg_ref[...], preferred_element_type=F32)
        u = jnp.dot(xv, wu_ref[...], preferred_element_type=F32)
        gu_ref[0] = g.astype(BF16)
        gu_ref[1] = u.astype(BF16)
        a_ref[...] = ((g * jax.nn.sigmoid(g)) * u).astype(BF16)

    w_blk = lambda t: pl.BlockSpec((None, None, d, f), lambda s, m: (s, t, 0, 0))
    return pl.pallas_call(
        body, grid=(N_SHARD, s_n // tm),
        in_specs=[pl.BlockSpec((tm, d), lambda s, m: (m, 0)), w_blk(0), w_blk(1)],
        out_specs=[pl.BlockSpec((None, 2, tm, f), lambda s, m: (s, 0, m, 0)),
                   pl.BlockSpec((None, tm, f), lambda s, m: (s, m, 0))],
        out_shape=[jax.ShapeDtypeStruct((N_SHARD, 2, s_n, f), BF16), jax.ShapeDtypeStruct((N_SHARD, s_n, f), BF16)],
        compiler_params=_cparams(), name=name)(hn, w_gu, w_gu)


def _ffn_down(a, w_dn, x, qg, gate, res_w, *, name):
    g_n, s_n, f = a.shape
    d = w_dn.shape[-1]
    tm = _pick(s_n, FFN_TM, 8)

    def body(a_ref, w_ref, x_ref, qg_ref, gate_ref, f_ref, o_ref):
        g = pl.program_id(1)
        r = jnp.dot(a_ref[...], w_ref[...], preferred_element_type=F32)

        @pl.when(g == 0)
        def _():
            f_ref[...] = r

        @pl.when(g > 0)
        def _():
            f_ref[...] += r

        @pl.when(g == g_n - 1)
        def _():
            fv = f_ref[...]
            y = (fv * _rstd(fv)) * qg_ref[...]
            o_ref[...] = x_ref[...] + (res_w * gate_ref[...]) * y

    row = pl.BlockSpec((tm, d), lambda m, g: (m, 0))
    vec = pl.BlockSpec((1, d), lambda m, g: (0, 0))
    sd = jax.ShapeDtypeStruct((s_n, d), F32)
    return pl.pallas_call(
        body, grid=(s_n // tm, g_n),
        in_specs=[pl.BlockSpec((None, tm, f), lambda m, g: (g, m, 0)), pl.BlockSpec((None, f, d), lambda m, g: (g, 0, 0)),
                  row, vec, vec],
        out_specs=[row, row], out_shape=[sd, sd], compiler_params=_cparams(), name=name)(a, w_dn, x, qg, gate)


def _ffn_dhn(dgu, w_gu, x, dout, pg, sc, *, name):
    g_n, s_n, f = dgu.shape
    d = w_gu.shape[-2]
    tm = _pick(s_n, FFN_TM, 8)

    def body(a_ref, w_ref, x_ref, do_ref, pg_ref, sc_ref, dx_ref, dsh_ref, dsc_ref, dpg_ref, acc_ref):
        m, g = pl.program_id(0), pl.program_id(1)
        r = lax.dot_general(a_ref[...], w_ref[...], (((1,), (1,)), ((), ())), preferred_element_type=F32)

        @pl.when(g == 0)
        def _():
            acc_ref[...] = r

        @pl.when(g > 0)
        def _():
            acc_ref[...] += r

        @pl.when((m == 0) & (g == 0))
        def _():
            dsh_ref[...] = jnp.zeros_like(dsh_ref)
            dsc_ref[...] = jnp.zeros_like(dsc_ref)
            dpg_ref[...] = jnp.zeros_like(dpg_ref)

        @pl.when(g == g_n - 1)
        def _():
            dhn_v = acc_ref[...]
            xv = x_ref[...]
            rs = _rstd(xv)
            xh = xv * rs
            pg_v = pg_ref[...]
            dsh_ref[...] += jnp.sum(dhn_v, axis=0, keepdims=True)
            dsc_ref[...] += jnp.sum(dhn_v * (xh * pg_v), axis=0, keepdims=True)
            dn = dhn_v * (1.0 + sc_ref[...])
            dpg_ref[...] += jnp.sum(dn * xh, axis=0, keepdims=True)
            dxh = dn * pg_v
            dx_ref[...] = do_ref[...] + rs * (dxh - xh * jnp.mean(dxh * xh, axis=-1, keepdims=True))

    row = pl.BlockSpec((tm, d), lambda m, g: (m, 0))
    vec = pl.BlockSpec((1, d), lambda m, g: (0, 0))
    vsd = jax.ShapeDtypeStruct((1, d), F32)
    return pl.pallas_call(
        body, grid=(s_n // tm, g_n),
        in_specs=[pl.BlockSpec((None, tm, f), lambda m, g: (g, m, 0)), pl.BlockSpec((None, d, f), lambda m, g: (g, 0, 0)),
                  row, row, vec, vec],
        out_specs=[row, vec, vec, vec], out_shape=[jax.ShapeDtypeStruct((s_n, d), F32), vsd, vsd, vsd],
        scratch_shapes=[pltpu.VMEM((tm, d), F32)], compiler_params=_cparams(), name=name)(dgu, w_gu, x, dout, pg, sc)


def _ffn_dgu(df, w_dn, gu, *, name):
    s_n, d = df.shape
    f = w_dn.shape[-2]
    tm = _pick(s_n, FFN_TM, 8)

    def body(df_ref, wd_ref, gu_ref, o_ref):
        da = lax.dot_general(df_ref[...], wd_ref[...], (((1,), (1,)), ((), ())), preferred_element_type=F32)
        g = gu_ref[0].astype(F32)
        u = gu_ref[1].astype(F32)
        sig = jax.nn.sigmoid(g)
        o_ref[0] = (da * u * (sig * (1.0 + g * (1.0 - sig)))).astype(BF16)
        o_ref[1] = (da * (g * sig)).astype(BF16)

    gu_blk = pl.BlockSpec((None, 2, tm, f), lambda s, m: (s, 0, m, 0))
    return pl.pallas_call(
        body, grid=(N_SHARD, s_n // tm),
        in_specs=[pl.BlockSpec((tm, d), lambda s, m: (m, 0)),
                  pl.BlockSpec((None, f, d), lambda s, m: (s, 0, 0)), gu_blk],
        out_specs=gu_blk, out_shape=jax.ShapeDtypeStruct((N_SHARD, 2, s_n, f), BF16),
        compiler_params=_cparams(), name=name)(df, w_dn, gu)


_NT = (((1,), (1,)), ((), ()))
_TN = (((0,), (0,)), ((), ()))
MLA_TQ = 256


def _causal_mask(i, tq, s_n):
    qpos = i * tq + lax.broadcasted_iota(jnp.int32, (tq, s_n), 0)
    kpos = lax.broadcasted_iota(jnp.int32, (tq, s_n), 1)
    return kpos <= qpos


def _mla_attn_fwd(q, k, v, *, name):
    h_n, s_n, dq = q.shape
    dv = v.shape[-1]
    tq = MLA_TQ
    scale = float(dq) ** -0.5

    def body(q_ref, k_ref, v_ref, o_ref, lse_ref):
        i = pl.program_id(1)
        for e in range(1, s_n // tq + 1):
            @pl.when(i == e - 1)
            def _(ext=e * tq):
                mask = _causal_mask(i, tq, ext)
                s = lax.dot_general(q_ref[...], k_ref[0:ext, :], _NT, preferred_element_type=F32) * scale
                s = jnp.where(mask, s, -jnp.inf)
                m = jnp.max(s, axis=-1, keepdims=True)
                p = jnp.exp(s - m)
                l = jnp.sum(p, axis=-1, keepdims=True)
                o = jnp.dot(p.astype(BF16), v_ref[0:ext, :], preferred_element_type=F32)
                o_ref[...] = o / l
                lse_ref[...] = m + jnp.log(l)

    return pl.pallas_call(
        body, grid=(h_n, s_n // tq),
        in_specs=[pl.BlockSpec((None, tq, dq), lambda h, i: (h, i, 0)),
                  pl.BlockSpec((None, s_n, dq), lambda h, i: (h, 0, 0)),
                  pl.BlockSpec((None, s_n, dv), lambda h, i: (h, 0, 0))],
        out_specs=[pl.BlockSpec((None, tq, dv), lambda h, i: (h, i, 0)),
                   pl.BlockSpec((None, tq, 1), lambda h, i: (h, i, 0))],
        out_shape=[jax.ShapeDtypeStruct((h_n, s_n, dv), F32), jax.ShapeDtypeStruct((h_n, s_n, 1), F32)],
        compiler_params=_cparams(), name=name)(q, k, v)


def _mla_attn_bwd(q, k, v, o, do, lse, *, name):
    h_n, s_n, dq = q.shape
    dv = v.shape[-1]
    tq = MLA_TQ
    scale = float(dq) ** -0.5

    def body(q_ref, k_ref, v_ref, o_ref, do_ref, lse_ref, dq_ref, dk_ref, dv_ref):
        i = pl.program_id(1)

        @pl.when(i == 0)
        def _():
            dk_ref[...] = jnp.zeros_like(dk_ref)
            dv_ref[...] = jnp.zeros_like(dv_ref)

        for e in range(1, s_n // tq + 1):
            @pl.when(i == e - 1)
            def _(ext=e * tq):
                mask = _causal_mask(i, tq, ext)
                qv, kv, vv = q_ref[...], k_ref[0:ext, :], v_ref[0:ext, :]
                do_v = do_ref[...]
                s = lax.dot_general(qv, kv, _NT, preferred_element_type=F32) * scale
                p = jnp.where(mask, jnp.exp(s - lse_ref[...]), 0.0)
                dob = do_v.astype(BF16)
                dv_ref[0:ext, :] += lax.dot_general(p.astype(BF16), dob, _TN, preferred_element_type=F32)
                dp = lax.dot_general(dob, vv, _NT, preferred_element_type=F32)
                delta = jnp.sum(do_v * o_ref[...], axis=-1, keepdims=True)
                dsb = (p * (dp - delta) * scale).astype(BF16)
                dq_ref[...] = jnp.dot(dsb, kv, preferred_element_type=F32)
                dk_ref[0:ext, :] += lax.dot_general(dsb, qv, _TN, preferred_element_type=F32)

    return pl.pallas_call(
        body, grid=(h_n, s_n // tq),
        in_specs=[pl.BlockSpec((None, tq, dq), lambda h, i: (h, i, 0)),
                  pl.BlockSpec((None, s_n, dq), lambda h, i: (h, 0, 0)),
                  pl.BlockSpec((None, s_n, dv), lambda h, i: (h, 0, 0)),
                  pl.BlockSpec((None, tq, dv), lambda h, i: (h, i, 0)),
                  pl.BlockSpec((None, tq, dv), lambda h, i: (h, i, 0)),
                  pl.BlockSpec((None, tq, 1), lambda h, i: (h, i, 0))],
        out_specs=[pl.BlockSpec((None, tq, dq), lambda h, i: (h, i, 0)),
                   pl.BlockSpec((None, s_n, dq), lambda h, i: (h, 0, 0)),
                   pl.BlockSpec((None, s_n, dv), lambda h, i: (h, 0, 0))],
        out_shape=[jax.ShapeDtypeStruct((h_n, s_n, dq), F32), jax.ShapeDtypeStruct((h_n, s_n, dq), F32),
                   jax.ShapeDtypeStruct((h_n, s_n, dv), F32)],
        compiler_params=_cparams(), name=name)(q, k, v, o, do, lse)


def _head_sum(x, *, name):
    h_n, s_n, w = x.shape
    tm = _pick(s_n, 256, 8)

    def body(x_ref, o_ref):
        o_ref[...] = jnp.sum(x_ref[...], axis=0)

    return pl.pallas_call(body, grid=(s_n // tm,), in_specs=[pl.BlockSpec((h_n, tm, w), lambda i: (0, i, 0))],
                          out_specs=_rows(tm, w), out_shape=jax.ShapeDtypeStruct((s_n, w), F32),
                          compiler_params=_cparams(), name=name)(x)


N_BLK = SEQ // DIL_BLOCK
DIL_SCALE = 64 ** -0.5


def _dil_masks():
    iq = lax.broadcasted_iota(jnp.int32, (DIL_BLOCK, 2 * DIL_BLOCK), 0)
    ik = lax.broadcasted_iota(jnp.int32, (DIL_BLOCK, 2 * DIL_BLOCK), 1)
    rel = DIL_BLOCK + iq - ik
    both = (rel >= 0) & (rel <= DIL_BLOCK)
    iq1 = lax.broadcasted_iota(jnp.int32, (DIL_BLOCK, DIL_BLOCK), 0)
    ik1 = lax.broadcasted_iota(jnp.int32, (DIL_BLOCK, DIL_BLOCK), 1)
    return both, ik1 <= iq1


def _dil_block(j, d):
    nb = SEQ // d // DIL_BLOCK
    r, n = divmod(j, nb)
    first = n == 0
    rows = lambda start, size: pl.ds(start, size) if d == 1 else pl.ds(start, size, stride=d)
    q_rows = rows(n * DIL_BLOCK * d + r, DIL_BLOCK)
    k_rows = q_rows if first else rows((n - 1) * DIL_BLOCK * d + r, 2 * DIL_BLOCK)
    return q_rows, k_rows, (DIL_BLOCK if first else 0), first


def _dil_head_specs(s_n, e, g):
    return [pl.BlockSpec((None, s_n, e), lambda h, t=t: (g * 3 * HEADS + t * HEADS + h, 0, 0)) for t in range(3)]


def _dil_attn_fwd(heads, bias, g, d, *, name):
    _, s_n, e = heads.shape

    def body(q_ref, k_ref, v_ref, b_ref, o_ref, lse_ref):
        m_both, m_first = _dil_masks()
        for j in range(N_BLK):
            q_rows, k_rows, b_lo, first = _dil_block(j, d)
            qj = q_ref[q_rows, :].astype(BF16)
            kk = k_ref[k_rows, :].astype(BF16)
            vv = v_ref[k_rows, :].astype(BF16)
            s = lax.dot_general(qj, kk, _NT, preferred_element_type=F32) * DIL_SCALE + b_ref[:, b_lo:]
            s = jnp.where(m_first if first else m_both, s, -jnp.inf)
            m = jnp.max(s, axis=-1, keepdims=True)
            lse = m + jnp.log(jnp.sum(jnp.exp(s - m), axis=-1, keepdims=True))
            p = jnp.exp(s - lse)
            o_ref[q_rows, :] = jnp.dot(p.astype(BF16), vv, preferred_element_type=F32)
            lse_ref[q_rows, :] = lse

    head = lambda w: pl.BlockSpec((None, s_n, w), lambda h: (h, 0, 0))
    return pl.pallas_call(
        body, grid=(HEADS,),
        in_specs=_dil_head_specs(s_n, e, g) + [pl.BlockSpec((None, DIL_BLOCK, 2 * DIL_BLOCK), lambda h: (h, 0, 0))],
        out_specs=[head(e), head(1)],
        out_shape=[jax.ShapeDtypeStruct((HEADS, s_n, e), F32), jax.ShapeDtypeStruct((HEADS, s_n, 1), F32)],
        compiler_params=_cparams(), name=name)(heads, heads, heads, bias)


def _dil_attn_bwd(heads, bias, lse, do, dlt, g, d, *, name):
    _, s_n, e = heads.shape

    def body(q_ref, k_ref, v_ref, b_ref, lse_ref, do_ref, dlt_ref, dq_ref, dk_ref, dv_ref, db_ref):
        dk_ref[...] = jnp.zeros_like(dk_ref)
        dv_ref[...] = jnp.zeros_like(dv_ref)
        db_ref[...] = jnp.zeros_like(db_ref)
        m_both, m_first = _dil_masks()
        for j in range(N_BLK):
            q_rows, k_rows, b_lo, first = _dil_block(j, d)
            qj = q_ref[q_rows, :].astype(BF16)
            kk = k_ref[k_rows, :].astype(BF16)
            vv = v_ref[k_rows, :].astype(BF16)
            s = lax.dot_general(qj, kk, _NT, preferred_element_type=F32) * DIL_SCALE + b_ref[:, b_lo:]
            p = jnp.where(m_first if first else m_both, jnp.exp(s - lse_ref[q_rows, :]), 0.0)
            dob = do_ref[q_rows, :].astype(BF16)
            dv_ref[k_rows, :] += lax.dot_general(p.astype(BF16), dob, _TN, preferred_element_type=F32)
            dp = lax.dot_general(dob, vv, _NT, preferred_element_type=F32)
            ds = p * (dp - dlt_ref[q_rows, :])
            db_ref[:, b_lo:] += ds
            dsb = (ds * DIL_SCALE).astype(BF16)
            dq_ref[q_rows, :] = jnp.dot(dsb, kk, preferred_element_type=F32)
            dk_ref[k_rows, :] += lax.dot_general(dsb, qj, _TN, preferred_element_type=F32)

    head = lambda w: pl.BlockSpec((None, s_n, w), lambda h: (h, 0, 0))
    b_spec = pl.BlockSpec((None, DIL_BLOCK, 2 * DIL_BLOCK), lambda h: (h, 0, 0))
    sd = jax.ShapeDtypeStruct((HEADS, s_n, e), F32)
    return pl.pallas_call(
        body, grid=(HEADS,),
        in_specs=_dil_head_specs(s_n, e, g) + [b_spec, head(1), head(e), head(1)],
        out_specs=[head(e), head(e), head(e), b_spec],
        out_shape=[sd, sd, sd, jax.ShapeDtypeStruct((HEADS, DIL_BLOCK, 2 * DIL_BLOCK), F32)],
        compiler_params=_cparams(), name=name)(heads, heads, heads, bias, lse, do, dlt)


def _proj_heads(x, w, *, name):
    s_n, k = x.shape
    n = w.shape[-1]
    tm, tn, e = 512, 768, 64
    per_blk, n_blk = tn // e, n // tn

    def body(x_ref, w_ref, o_ref):
        r = jnp.dot(x_ref[...], w_ref[...], preferred_element_type=F32)
        for j in range(per_blk):
            o_ref[j] = r[:, e * j:e * (j + 1)]

    return pl.pallas_call(
        body, grid=(w.shape[0], n_blk, s_n // tm),
        in_specs=[pl.BlockSpec((tm, k), lambda s, b, m: (m, 0)), pl.BlockSpec((None, k, tn), lambda s, b, m: (s, 0, b))],
        out_specs=pl.BlockSpec((per_blk, tm, e), lambda s, b, m: (s * n_blk + b, m, 0)),
        out_shape=jax.ShapeDtypeStruct((w.shape[0] * n // e, s_n, e), F32), compiler_params=_cparams(),
        name=name)(x, w)


def _heads_cat(d_ref):
    return jnp.concatenate([d_ref[j] for j in range(d_ref.shape[0])], axis=1)


def _proj_heads_dw(x, dh, *, name):
    s_n, k = x.shape
    tn, e = 768, 64
    per_blk = tn // e
    n_blk = dh.shape[0] // N_SHARD // per_blk
    n = n_blk * tn

    def body(x_ref, d_ref, o_ref):
        o_ref[...] = lax.dot_general(x_ref[...], _heads_cat(d_ref), _TN, preferred_element_type=F32)

    return pl.pallas_call(
        body, grid=(N_SHARD, n_blk, 2),
        in_specs=[pl.BlockSpec((s_n, k // 2), lambda s, b, r: (0, r)),
                  pl.BlockSpec((per_blk, s_n, e), lambda s, b, r: (s * n_blk + b, 0, 0))],
        out_specs=pl.BlockSpec((None, None, k // 2, tn), lambda s, b, r: (r, s, 0, b)),
        out_shape=jax.ShapeDtypeStruct((2, N_SHARD, k // 2, n), F32), compiler_params=_cparams(), name=name)(x, dh)


def _proj_heads_dx(dh, w, *, name):
    k, n = w.shape[1:]
    s_n = dh.shape[1]
    tm, tn, e = 512, 768, 64
    per_blk, n_blk = tn // e, n // tn

    def body(d_ref, w_ref, o_ref):
        r = lax.dot_general(_heads_cat(d_ref), w_ref[...], _NT, preferred_element_type=F32)
        g = pl.program_id(1)

        @pl.when(g == 0)
        def _():
            o_ref[...] = r

        @pl.when(g > 0)
        def _():
            o_ref[...] += r

    return pl.pallas_call(
        body, grid=(s_n // tm, N_SHARD * n_blk),
        in_specs=[pl.BlockSpec((per_blk, tm, e), lambda m, g: (g, m, 0)),
                  pl.BlockSpec((None, k, tn), lambda m, g: (g // n_blk, 0, g % n_blk))],
        out_specs=pl.BlockSpec((tm, k), lambda m, g: (m, 0)),
        out_shape=jax.ShapeDtypeStruct((s_n, k), F32), compiler_params=_cparams(), name=name)(dh, w)


def _group_alpha(l_refs):
    ls = [r[...] for r in l_refs]
    m = jnp.maximum(jnp.maximum(ls[0], ls[1]), ls[2])
    es = [jnp.exp(l - m) for l in ls]
    tot = es[0] + es[1] + es[2]
    return [ex / tot for ex in es]


def _dil_mix_fwd(os_, ls_, *, name):
    h_n, s_n, e = os_[0].shape
    tm = 512

    def body(o0, o1, o2, l0, l1, l2, out_ref):
        al = _group_alpha((l0, l1, l2))
        out_ref[...] = al[0] * o0[...] + al[1] * o1[...] + al[2] * o2[...]

    blk = lambda w: pl.BlockSpec((None, tm, w), lambda h, i: (h, i, 0))
    return pl.pallas_call(body, grid=(h_n, s_n // tm), in_specs=[blk(e)] * 3 + [blk(1)] * 3, out_specs=blk(e),
                          out_shape=jax.ShapeDtypeStruct((h_n, s_n, e), F32), compiler_params=_cparams(),
                          name=name)(*os_, *ls_)


def _dil_mix_bwd(do, os_, ls_, *, name):
    h_n, s_n, e = do.shape
    tm = 512

    def body(do_ref, o0, o1, o2, l0, l1, l2, d0, d1, d2, t0, t1, t2):
        al = _group_alpha((l0, l1, l2))
        do_v = do_ref[...]
        mix = al[0] * o0[...] + al[1] * o1[...] + al[2] * o2[...]
        dbar = jnp.sum(do_v * mix, axis=-1, keepdims=True)
        for a_g, d_ref, t_ref in zip(al, (d0, d1, d2), (t0, t1, t2)):
            d_ref[...] = a_g * do_v
            t_ref[...] = a_g * dbar

    blk = lambda w: pl.BlockSpec((None, tm, w), lambda h, i: (h, i, 0))
    sd_e = jax.ShapeDtypeStruct((h_n, s_n, e), F32)
    sd_1 = jax.ShapeDtypeStruct((h_n, s_n, 1), F32)
    outs = pl.pallas_call(body, grid=(h_n, s_n // tm), in_specs=[blk(e)] * 4 + [blk(1)] * 3,
                          out_specs=[blk(e)] * 3 + [blk(1)] * 3, out_shape=[sd_e] * 3 + [sd_1] * 3,
                          compiler_params=_cparams(), name=name)(do, *os_, *ls_)
    return outs[:3], outs[3:]


def _bias_grad(ds, bucket, *, name):
    h_n = ds.shape[0]

    def body(ds_ref, bk_ref, o_ref):
        ds_v = ds_ref[...]
        bk = bk_ref[...]
        lane = lax.broadcasted_iota(jnp.int32, (1, N_BUCKETS), 1)
        acc = jnp.zeros((1, N_BUCKETS), F32)
        for b in range(N_BUCKETS):
            tot = jnp.sum(jnp.sum(jnp.where(bk == b, ds_v, 0.0), axis=1, keepdims=True), axis=0, keepdims=True)
            acc = acc + jnp.where(lane == b, tot, 0.0)
        o_ref[...] = acc

    return pl.pallas_call(
        body, grid=(h_n,),
        in_specs=[pl.BlockSpec((None, DIL_BLOCK, 2 * DIL_BLOCK), lambda h: (h, 0, 0)),
                  pl.BlockSpec((DIL_BLOCK, 2 * DIL_BLOCK), lambda h: (0, 0))],
        out_specs=pl.BlockSpec((None, 1, N_BUCKETS), lambda h: (h, 0, 0)),
        out_shape=jax.ShapeDtypeStruct((h_n, 1, N_BUCKETS), F32), compiler_params=_cparams(), name=name)(ds, bucket)


def _bias_table(rb, bucket, *, name):
    h_n = rb.shape[0]

    def body(rb_ref, bk_ref, o_ref):
        bk = bk_ref[...]
        row = rb_ref[...]
        acc = jnp.zeros(bk.shape, F32)
        for b in range(N_BUCKETS):
            acc = jnp.where(bk == b, row[:, b:b + 1], acc)
        o_ref[...] = acc

    return pl.pallas_call(
        body, grid=(h_n,),
        in_specs=[pl.BlockSpec((None, 1, N_BUCKETS), lambda h: (h, 0, 0)),
                  pl.BlockSpec((DIL_BLOCK, 2 * DIL_BLOCK), lambda h: (0, 0))],
        out_specs=pl.BlockSpec((None, DIL_BLOCK, 2 * DIL_BLOCK), lambda h: (h, 0, 0)),
        out_shape=jax.ShapeDtypeStruct((h_n, DIL_BLOCK, 2 * DIL_BLOCK), F32), compiler_params=_cparams(),
        name=name)(rb, bucket)


def _row_tile(rows, cols, budget=2 << 20):
    if rows * cols * 4 <= budget or rows % 8:
        return rows
    best = 8
    for t in range(8, rows + 1, 8):
        if rows % t == 0 and t * cols * 4 <= budget:
            best = t
    return best


def _adamw(w, g, m, v, *, name):
    shape = w.shape
    cols = shape[-1]
    rows = math.prod(shape[:-1]) if len(shape) > 1 else 1
    to2 = lambda t: t.reshape(rows, cols)
    tr = _row_tile(rows, cols)
    c1 = 1.0 / (1.0 - ADAM_B1 ** ADAM_STEP)
    c2 = 1.0 / (1.0 - ADAM_B2 ** ADAM_STEP)

    def body(w_ref, g_ref, m_ref, v_ref, d_ref, nm_ref, nv_ref):
        g_v = g_ref[...]
        nm = ADAM_B1 * m_ref[...] + (1.0 - ADAM_B1) * g_v
        nv = ADAM_B2 * v_ref[...] + (1.0 - ADAM_B2) * (g_v * g_v)
        m_hat = nm * c1
        v_hat = nv * c2
        d_ref[...] = -ADAM_LR * (m_hat / (jnp.sqrt(v_hat) + ADAM_EPS) + ADAM_WD * w_ref[...])
        nm_ref[...] = nm
        nv_ref[...] = nv

    blk = pl.BlockSpec((tr, cols), lambda i: (i, 0))
    sd = jax.ShapeDtypeStruct((rows, cols), F32)
    outs = pl.pallas_call(body, grid=(rows // tr,), in_specs=[blk] * 4, out_specs=[blk] * 3, out_shape=[sd] * 3,
                          compiler_params=_cparams(), name=name)(to2(w), to2(g), to2(m), to2(v))
    return tuple(t.reshape(shape) for t in outs)


def _add_half(unit, got, half_idx, *, name):
    rest = unit.shape[2:]
    c = rest[-1]
    r = math.prod(rest[:-1])
    tr = _row_tile(r, c)

    def body(idx_ref, u_ref, g_ref, o_ref, w_ref):
        tot = u_ref[...] + g_ref[...].astype(F32)
        o_ref[...] = tot
        w_ref[...] = tot.astype(BF16)

    blk = pl.BlockSpec((None, tr, c), lambda s, i, idx: (s, i, 0))
    grid_spec = pltpu.PrefetchScalarGridSpec(
        num_scalar_prefetch=1, grid=(N_SHARD, r // tr),
        in_specs=[pl.BlockSpec((None, None, tr, c), lambda s, i, idx: (idx[0], s, i, 0)), blk],
        out_specs=[blk, blk])
    out, wire = pl.pallas_call(
        body, grid_spec=grid_spec,
        out_shape=[jax.ShapeDtypeStruct((N_SHARD, r, c), F32), jax.ShapeDtypeStruct((N_SHARD, r, c), BF16)],
        compiler_params=_cparams(), name=name)(half_idx, unit.reshape(2, N_SHARD, r, c), got.reshape(N_SHARD, r, c))
    return out.reshape((N_SHARD,) + rest), wire.reshape((N_SHARD,) + rest)


def _add_shards(part, got, shard_idx, *, name):
    rest = part.shape[1:]
    c = rest[-1]
    r = math.prod(rest[:-1])
    tr = _row_tile(r, c)

    def body(idx_ref, p_ref, g_ref, o_ref):
        acc = p_ref[...]
        for k in range(3):
            acc = acc + g_ref[k].astype(F32)
        o_ref[...] = acc

    grid_spec = pltpu.PrefetchScalarGridSpec(
        num_scalar_prefetch=1, grid=(r // tr,),
        in_specs=[pl.BlockSpec((None, tr, c), lambda i, idx: (idx[0], i, 0)),
                  pl.BlockSpec((3, tr, c), lambda i, idx: (0, i, 0))],
        out_specs=pl.BlockSpec((tr, c), lambda i, idx: (i, 0)))
    out = pl.pallas_call(body, grid_spec=grid_spec, out_shape=jax.ShapeDtypeStruct((r, c), F32),
                         compiler_params=_cparams(), name=name)(
        shard_idx, part.reshape(N_SHARD, r, c), got.reshape(3, r, c))
    return out.reshape(rest)


def _sum_devices(x, n_dev, *, name):
    rows = x.shape[0] // n_dev

    def body(x_ref, o_ref):
        acc = x_ref[0:rows, :]
        for d in range(1, n_dev):
            acc = acc + x_ref[d * rows:(d + 1) * rows, :]
        o_ref[...] = acc

    return pl.pallas_call(body, out_shape=jax.ShapeDtypeStruct((rows, x.shape[1]), F32), name=name)(x)


def _my_pos():
    return lax.axis_index("x"), lax.axis_index("y"), lax.axis_index("c")


def _all_gather(x_blk, *, name, in_vmem):
    m_per, n = x_blk.shape

    def body(x_ref, out_ref, send_sems, recv_sems, local_sem):
        x, y, c = _my_pos()
        me, sibling = (x, y, c), (x, y, 1 - c)
        chips = [(1 - x, y), (x, 1 - y), (1 - x, 1 - y)]

        def rows(px, py, pc):
            return out_ref.at[pl.ds((4 * px + 2 * py + pc) * m_per, m_per), :]

        def copy(k, block, to, src=None):
            return pltpu.make_async_remote_copy(
                src_ref=rows(*block) if src is None else src, dst_ref=rows(*block),
                send_sem=send_sems.at[k], recv_sem=recv_sems.at[k], device_id=to, device_id_type=MESH)

        mine = pltpu.make_async_copy(x_ref, rows(*me), local_sem)
        mine.start()
        first = [copy(0, me, sibling, src=x_ref)]
        first += [copy(1 + j, me, (*chip, c), src=x_ref) for j, chip in enumerate(chips)]
        for cp in first:
            cp.start()
        passed = [copy(4 + j, (*chip, c), sibling) for j, chip in enumerate(chips)]
        for j, chip in enumerate(chips):
            copy(1 + j, (*chip, c), me).wait_recv()
            passed[j].start()
        copy(0, sibling, me).wait_recv()
        for j, chip in enumerate(chips):
            copy(4 + j, (*chip, 1 - c), me).wait_recv()
        for cp in first + passed:
            cp.wait_send()
        mine.wait()

    space = pltpu.VMEM if in_vmem else pl.ANY
    return pl.pallas_call(
        body, out_shape=jax.ShapeDtypeStruct((8 * m_per, n), x_blk.dtype),
        in_specs=[pl.BlockSpec(memory_space=space)], out_specs=pl.BlockSpec(memory_space=space),
        scratch_shapes=[pltpu.SemaphoreType.DMA((7,)), pltpu.SemaphoreType.DMA((7,)), pltpu.SemaphoreType.DMA],
        name=name)(x_blk)


_HBM = pl.BlockSpec(memory_space=pl.ANY)


def _gather_weights(fams, *, name):
    n = len(fams)

    def body(*refs):
        ins, outs = refs[:n], refs[n:2 * n]
        send_sems, recv_sems = refs[2 * n:]
        x, y, c = _my_pos()
        me, sibling = (x, y, c), (x, y, 1 - c)
        chips = [(1 - x, y), (x, 1 - y), (1 - x, 1 - y)]

        def copy(f, k, block, to, src=None):
            px, py, pc = block
            dst = outs[f].at[2 * px + py, pc]
            return pltpu.make_async_remote_copy(
                src_ref=dst if src is None else src, dst_ref=dst, send_sem=send_sems.at[7 * f + k],
                recv_sem=recv_sems.at[7 * f + k], device_id=to, device_id_type=MESH)

        first, passed = [], []
        for f in range(n):
            src = ins[f].at[c]
            first.append(copy(f, 0, me, sibling, src=src))
            first += [copy(f, 1 + j, me, (*chip, c), src=src) for j, chip in enumerate(chips)]
        for cp in first:
            cp.start()
        for j, chip in enumerate(chips):
            for f in range(n):
                copy(f, 1 + j, (*chip, c), me).wait_recv()
                passed.append(copy(f, 4 + j, (*chip, c), sibling))
                passed[-1].start()
        for f in range(n):
            copy(f, 0, sibling, me).wait_recv()
        for j, chip in enumerate(chips):
            for f in range(n):
                copy(f, 4 + j, (*chip, 1 - c), me).wait_recv()
        for cp in first + passed:
            cp.wait_send()

    outs = pl.pallas_call(
        body, out_shape=[jax.ShapeDtypeStruct((N_SHARD,) + t.shape, t.dtype) for t in fams],
        in_specs=[_HBM] * n, out_specs=[_HBM] * n,
        scratch_shapes=[pltpu.SemaphoreType.DMA((7 * n,)), pltpu.SemaphoreType.DMA((7 * n,))], name=name)(*fams)
    return [_place_own(o, t) for o, t in zip(outs, fams)]


def _swap_halves(units, *, name):
    n = len(units)

    def body(*refs):
        ins, outs = refs[:n], refs[n:2 * n]
        send_sems, recv_sems = refs[2 * n:]
        x, y, c = _my_pos()
        cps = [pltpu.make_async_remote_copy(src_ref=ins[f].at[1 - c], dst_ref=outs[f], send_sem=send_sems.at[f],
                                            recv_sem=recv_sems.at[f], device_id=(x, y, 1 - c), device_id_type=MESH)
               for f in range(n)]
        for cp in cps:
            cp.start()
        for cp in cps:
            cp.wait()

    return pl.pallas_call(
        body, out_shape=[jax.ShapeDtypeStruct(t.shape[1:], t.dtype) for t in units],
        in_specs=[_HBM] * n, out_specs=[_HBM] * n,
        scratch_shapes=[pltpu.SemaphoreType.DMA((n,)), pltpu.SemaphoreType.DMA((n,))], name=name)(*units)


def _send_to_chips(parts, *, name):
    n = len(parts)

    def body(*refs):
        ins, outs = refs[:n], refs[n:2 * n]
        send_sems, recv_sems = refs[2 * n:]
        x, y, c = _my_pos()
        chips = [(1 - x, y), (x, 1 - y), (1 - x, 1 - y)]
        cps = [pltpu.make_async_remote_copy(src_ref=ins[f].at[2 * cx + cy], dst_ref=outs[f].at[k],
                                            send_sem=send_sems.at[3 * f + k], recv_sem=recv_sems.at[3 * f + k],
                                            device_id=(cx, cy, c), device_id_type=MESH)
               for f in range(n) for k, (cx, cy) in enumerate(chips)]
        for cp in cps:
            cp.start()
        for cp in cps:
            cp.wait()

    return pl.pallas_call(
        body, out_shape=[jax.ShapeDtypeStruct((3,) + t.shape[1:], t.dtype) for t in parts],
        in_specs=[_HBM] * n, out_specs=[_HBM] * n,
        scratch_shapes=[pltpu.SemaphoreType.DMA((3 * n,)), pltpu.SemaphoreType.DMA((3 * n,))], name=name)(*parts)


def _pair_gather(halves, *, name):
    n = len(halves)

    def body(*refs):
        ins, outs = refs[:n], refs[n:2 * n]
        send_sems, recv_sems = refs[2 * n:]
        x, y, c = _my_pos()
        cps = [pltpu.make_async_remote_copy(src_ref=ins[f], dst_ref=outs[f].at[c], send_sem=send_sems.at[f],
                                            recv_sem=recv_sems.at[f], device_id=(x, y, 1 - c), device_id_type=MESH)
               for f in range(n)]
        for cp in cps:
            cp.start()
        for f in range(n):
            pltpu.make_async_remote_copy(src_ref=ins[f], dst_ref=outs[f].at[1 - c], send_sem=send_sems.at[f],
                                         recv_sem=recv_sems.at[f], device_id=(x, y, 1 - c),
                                         device_id_type=MESH).wait_recv()
        for cp in cps:
            cp.wait_send()

    outs = pl.pallas_call(
        body, out_shape=[jax.ShapeDtypeStruct((2,) + t.shape, t.dtype) for t in halves],
        in_specs=[_HBM] * n, out_specs=[_HBM] * n,
        scratch_shapes=[pltpu.SemaphoreType.DMA((n,)), pltpu.SemaphoreType.DMA((n,))], name=name)(*halves)
    c = lax.axis_index("c")
    return [lax.dynamic_update_index_in_dim(o, t, c, 0) for o, t in zip(outs, halves)]


_HBM_ONLY = pl.BlockSpec(memory_space=pltpu.HBM)
_SEMS = pl.BlockSpec(memory_space=pltpu.SEMAPHORE)
_EFFECT = pltpu.SideEffectType.DATAFLOW_SIDE_EFFECTING


def _copies_start(srcs, lands, plan, n_copies, *, name):
    n, m = len(srcs), len(lands)

    def body(*refs):
        src_refs, land_refs = refs[:n], refs[n:n + m]
        send_sems, recv_sems, token = refs[n + m], refs[n + m + 1], refs[-1]
        for k, (src, dst, peer) in enumerate(plan(src_refs, land_refs)):
            pltpu.make_async_remote_copy(src_ref=src, dst_ref=dst, send_sem=send_sems.at[k], recv_sem=recv_sems.at[k],
                                         device_id=peer, device_id_type=MESH).start()
        token[...] = jnp.zeros_like(token)

    bufs = [pltpu.with_memory_space_constraint(t, pltpu.HBM) for t in (*srcs, *lands)]
    outs = pl.pallas_call(
        body, name=name,
        out_shape=(pltpu.SemaphoreType.DMA((n_copies,)), pltpu.SemaphoreType.DMA((n_copies,)),
                   *[pltpu.HBM(t.shape, t.dtype) for t in bufs], jax.ShapeDtypeStruct((8, 128), F32)),
        in_specs=[_HBM_ONLY] * (n + m),
        out_specs=(_SEMS, _SEMS, *[_HBM_ONLY] * (n + m), pl.BlockSpec(memory_space=pltpu.VMEM)),
        input_output_aliases={k: 2 + k for k in range(n + m)},
        compiler_params=pltpu.CompilerParams(has_side_effects=_EFFECT))(*bufs)
    return outs[0], outs[1], list(outs[2:2 + n + m]), outs[-1]


def _copies_wait(send_sems, recv_sems, thru, n_src, plan, after, *, name):
    nm = len(thru)

    def body(*refs):
        t_refs, send, recv = refs[:nm], refs[nm], refs[nm + 1]
        for k, (src, dst, peer) in enumerate(plan(t_refs[:n_src], t_refs[n_src:])):
            cp = pltpu.make_async_remote_copy(src_ref=src, dst_ref=dst, send_sem=send.at[k], recv_sem=recv.at[k],
                                              device_id=peer, device_id_type=MESH)
            cp.wait_send()
            cp.wait_recv()

    outs = pl.pallas_call(
        body, name=name, out_shape=tuple(pltpu.HBM(t.shape, t.dtype) for t in thru),
        in_specs=[_HBM_ONLY] * nm + [_SEMS, _SEMS, pl.BlockSpec(memory_space=pl.ANY)],
        out_specs=tuple([_HBM_ONLY] * nm), input_output_aliases={k: k for k in range(nm)},
        compiler_params=pltpu.CompilerParams(has_side_effects=_EFFECT))(*thru, send_sems, recv_sems, after)
    return list(outs)


_RELATIONS = [(dx, dy, dc) for dx in (0, 1) for dy in (0, 1) for dc in (0, 1)][1:]


def _gather_plan(src_refs, land_refs):
    x, y, c = _my_pos()
    flip = lambda v, d: 1 - v if d else v
    return [(s_ref.at[c], l_ref.at[2 * x + y, c], (flip(x, dx), flip(y, dy), flip(c, dc)))
            for s_ref, l_ref in zip(src_refs, land_refs) for dx, dy, dc in _RELATIONS]


def _sibling_plan(src_refs, land_refs):
    x, y, c = _my_pos()
    return [(s_ref.at[1 - c], l_ref, (x, y, 1 - c)) for s_ref, l_ref in zip(src_refs, land_refs)]


def _chips_plan(src_refs, land_refs):
    x, y, c = _my_pos()
    chips = [(1 - x, y), (x, 1 - y), (1 - x, 1 - y)]
    return [(s_ref.at[2 * cx + cy], l_ref.at[k], (cx, cy, c))
            for s_ref, l_ref in zip(src_refs, land_refs) for k, (cx, cy) in enumerate(chips)]


def _place_own(gathered, fam):
    x, y, c = _my_pos()
    own = lax.dynamic_index_in_dim(fam, c, 0, keepdims=True)[None]
    return lax.dynamic_update_slice(gathered, own, (2 * x + y, c) + (0,) * (fam.ndim - 1))


def _to_heads(t, width):
    return t.reshape(t.shape[0], HEADS, width).transpose(1, 0, 2)


def _from_heads(t):
    return t.transpose(1, 0, 2).reshape(t.shape[1], -1)


def _t5_bucket(dist):
    max_exact = N_BUCKETS // 2
    d = jnp.maximum(dist, 1).astype(F32)
    large = max_exact + (jnp.log(d / max_exact) / math.log(MAX_DISTANCE / max_exact)
                         * (N_BUCKETS - max_exact)).astype(jnp.int32)
    large = jnp.minimum(large, N_BUCKETS - 1)
    return jnp.where(dist < max_exact, dist, large)


def _bucket_map(dilation):
    iq = jnp.arange(DIL_BLOCK)[:, None]
    ik = jnp.arange(2 * DIL_BLOCK)[None, :]
    rel = DIL_BLOCK + iq - ik
    return _t5_bucket(jnp.maximum(rel, 0) * dilation).astype(jnp.int32)


def _q_perm(w):
    w3 = w.reshape(w.shape[0], HEADS, QK_NOPE + QK_ROPE)
    return jnp.concatenate([w3[:, :, :QK_NOPE].reshape(w.shape[0], -1),
                            w3[:, :, QK_NOPE:QK_NOPE + HALF_ROPE].reshape(w.shape[0], -1),
                            w3[:, :, QK_NOPE + HALF_ROPE:].reshape(w.shape[0], -1)], axis=1)


def _q_unperm(w):
    n0, n1 = HEADS * QK_NOPE, HEADS * HALF_ROPE
    r = w.shape[0]
    return jnp.concatenate([w[:, :n0].reshape(r, HEADS, QK_NOPE), w[:, n0:n0 + n1].reshape(r, HEADS, HALF_ROPE),
                            w[:, n0 + n1:].reshape(r, HEADS, HALF_ROPE)], axis=2).reshape(r, -1)


def _kv_perm(w):
    w3 = w.reshape(w.shape[0], HEADS, QK_NOPE + V_HEAD)
    return jnp.concatenate([w3[:, :, :QK_NOPE].reshape(w.shape[0], -1), w3[:, :, QK_NOPE:].reshape(w.shape[0], -1)],
                           axis=1)


def _kv_unperm(w):
    n0 = HEADS * QK_NOPE
    r = w.shape[0]
    return jnp.concatenate([w[:, :n0].reshape(r, HEADS, QK_NOPE), w[:, n0:].reshape(r, HEADS, V_HEAD)],
                           axis=2).reshape(r, -1)


def _row(v):
    return v.reshape(1, -1)


def kernel(x, c, norm_pre, norm_post, w_mod, b_mod, ffn_w_gate, ffn_w_up, ffn_w_down, mla_w_in, mla_q_norm, mla_w_q_up, mla_kv_norm, mla_w_kv_up, mla_w_o, dil_w_in, dil_w_o, rel_bias, loss_target, m_norm_pre, m_norm_post, m_w_mod, m_b_mod, m_ffn_w_gate, m_ffn_w_up, m_ffn_w_down, m_mla_w_in, m_mla_q_norm, m_mla_w_q_up, m_mla_kv_norm, m_mla_w_kv_up, m_mla_w_o, m_dil_w_in, m_dil_w_o, m_rel_bias, v_norm_pre, v_norm_post, v_w_mod, v_b_mod, v_ffn_w_gate, v_ffn_w_up, v_ffn_w_down, v_mla_w_in, v_mla_q_norm, v_mla_w_q_up, v_mla_kv_norm, v_mla_w_kv_up, v_mla_w_o, v_dil_w_in, v_dil_w_o, v_rel_bias):
    given = dict(locals())
    ix, iy, ic = _my_pos()
    shard_id = 2 * ix + iy
    dev_id = 4 * ix + 2 * iy + ic
    x2 = x[0]
    target = loss_target[0]
    half_idx = jnp.reshape(ic, (1,)).astype(jnp.int32)
    shard_idx = jnp.reshape(shard_id, (1,)).astype(jnp.int32)

    blk = jnp.zeros((8, D_MODEL), F32)
    blk = blk.at[0].set(c[0])
    blk = blk.at[1:3].set(jnp.pad(norm_pre.reshape(-1), (0, 512)).reshape(2, D_MODEL))
    blk = blk.at[3:5].set(jnp.pad(norm_post.reshape(-1), (0, 512)).reshape(2, D_MODEL))
    got = _all_gather(blk, name="ag_c_norms", in_vmem=True).reshape(N_SHARD, 2, 8, D_MODEL)
    c_all = got[:, :, 0, :].reshape(8, D_MODEL)

    def full_norm(lo):
        t = got[:, 0, lo:lo + 2, :].reshape(N_SHARD, 2 * D_MODEL)[:, :1536].reshape(N_SHARD, 2, 3, 256)
        return t.transpose(1, 2, 0, 3).reshape(2, 3, D_MODEL)

    pre_full, post_full = full_norm(1), full_norm(3)

    silu_c = _silu_bf16(c_all, name="silu_c")
    b_cols = lax.dynamic_slice_in_dim(b_mod, shard_id * 2304, 2304, axis=1).reshape(2, 1, 2304)
    mod_part = _mm(silu_c, w_mod, bias=b_cols, name="mod_mm", tn_cap=768)
    mod_all = _all_gather(mod_part.reshape(16, 2304), name="ag_mod", in_vmem=True)
    mod_all = mod_all.reshape(N_SHARD, 2, 2, 8, 2304)[:, 0]
    mod_mine = lax.dynamic_index_in_dim(mod_all, dev_id, axis=2, keepdims=False)
    mod = mod_mine.transpose(1, 0, 2).reshape(2, 9, D_MODEL)

    bf = lambda t: t.astype(BF16)
    ffn_fam = lambda i, h: [bf(jnp.stack([ffn_w_gate[i, h], ffn_w_up[i, h]])),
                            bf(ffn_w_down[i, h].reshape(2, F_SHARD // 2, D_MODEL))]
    mla_fam = [bf(mla_w_in.reshape(2, 128, -1)), bf(mla_w_q_up.reshape(2, 192, -1)),
               bf(mla_w_kv_up.reshape(2, 128, -1)), bf(mla_w_o.reshape(2, 128, D_MODEL))]
    dil_fam = [bf(dil_w_in.reshape(2, 512, -1)), bf(dil_w_o.reshape(2, 128, D_MODEL))]
    later_fams = [ffn_fam(1, 0) + dil_fam, ffn_fam(1, 1)]
    full, later_fams, mod = lax.optimization_barrier(
        (_gather_weights(ffn_fam(0, 0) + mla_fam + ffn_fam(0, 1), name="ag_weights_first"), later_fams, mod))

    def gather_later(fams, tag):
        lands = [lax.empty((N_SHARD,) + t.shape, t.dtype) for t in fams]
        send, recv, thru, token = _copies_start(fams, lands, _gather_plan, 7 * len(fams), name=f"ag_start_{tag}")
        return dict(send=send, recv=recv, thru=thru, token=token, n=len(fams), tag=tag)

    def arrive(st, after):
        thru = _copies_wait(st['send'], st['recv'], st['thru'], st['n'], _gather_plan, after,
                            name=f"ag_wait_{st['tag']}")
        return [_place_own(o, t) for t, o in zip(thru[:st['n']], thru[st['n']:])]

    in_flight = gather_later(later_fams[0], "l1s01")
    as_ffn = lambda w_gu, w_dn: (w_gu, w_dn.reshape(N_SHARD, F_SHARD, D_MODEL))
    ffn_w = {(0, 0): as_ffn(full[0], full[1]), (0, 1): as_ffn(full[6], full[7])}
    w_in = full[2].reshape(D_MODEL, -1)
    wq_p = _q_perm(full[3].reshape(N_SHARD, Q_LORA, -1).transpose(1, 0, 2).reshape(Q_LORA, -1))
    wkv_p = _kv_perm(full[4].reshape(N_SHARD, KV_LORA, -1).transpose(1, 0, 2).reshape(KV_LORA, -1))
    w_mo = full[5].reshape(D_MODEL, D_MODEL)
    dil_w = {}

    pos = jnp.arange(SEQ, dtype=F32)
    freqs = ROPE_THETA ** (-jnp.arange(HALF_ROPE, dtype=F32) / HALF_ROPE)
    ang = pos[:, None] * freqs[None, :]
    cos_k, sin_k = jnp.cos(ang), jnp.sin(ang)
    cos_q, sin_q = jnp.tile(cos_k, (1, HEADS)), jnp.tile(sin_k, (1, HEADS))

    buckets = [_bucket_map(d) for _, d in DIL_GROUPS]
    biases = [_bias_table(rel_bias[:, g * HEADS:(g + 1) * HEADS].T.reshape(HEADS, 1, N_BUCKETS), bk,
                          name=f"dil_bias_table_g{g}") for g, bk in enumerate(buckets)]

    def sub_params(i, sub):
        return dict(pg=_row(pre_full[i, sub]), qg=_row(post_full[i, sub]), sh=_row(mod[i, 3 * sub]),
                    sc=_row(mod[i, 3 * sub + 1]), gate=_row(mod[i, 3 * sub + 2]))

    def ffn_fwd(xin, i, h, sub, tie=None):
        p = sub_params(i, sub)
        if tie is not None:
            p['sh'] = p['sh'] + tie
        tag = f"l{i}s{sub}"
        w_gu, w_dn = ffn_w[i, h]
        hn = _pre_fwd(xin, p['pg'], p['sc'], p['sh'], name=f"pre_fwd_{tag}")
        gu, a = _ffn_up(hn, w_gu, name=f"ffn_up_{tag}")
        f, out = _ffn_down(a, w_dn, xin, p['qg'], p['gate'], FFN_RES, name=f"ffn_down_{tag}")
        return out, dict(x=xin, hn=hn, gu=gu, a=a, f=f, p=p, i=i, h=h, tag=tag)

    def mla_fwd(xin, i, sub):
        p = sub_params(i, sub)
        tag = f"l{i}s{sub}"
        hn = _pre_fwd(xin, p['pg'], p['sc'], p['sh'], name=f"pre_fwd_{tag}")
        lat = _mm(hn, w_in, name="mla_lat")
        cq, ckv = lat[:, :Q_LORA], lat[:, Q_LORA:Q_LORA + KV_LORA]
        k1, k2 = lat[:, Q_LORA + KV_LORA:Q_LORA + KV_LORA + HALF_ROPE], lat[:, Q_LORA + KV_LORA + HALF_ROPE:]
        cqn = _rms_fwd(cq, mla_q_norm, name="mla_qnorm")
        ckvn = _rms_fwd(ckv, mla_kv_norm, name="mla_kvnorm")
        qp = _mm(cqn, wq_p, name="mla_q_up")
        kvp = _mm(ckvn, wkv_p, name="mla_kv_up")
        n0, n1 = HEADS * QK_NOPE, HEADS * HALF_ROPE
        qr1, qr2 = _rope(qp[:, n0:n0 + n1], qp[:, n0 + n1:], cos_q, sin_q, name="rope_q")
        kr1, kr2 = _rope(k1, k2, cos_k, sin_k, name="rope_k")
        q = jnp.concatenate([qp[:, :n0].reshape(SEQ, HEADS, QK_NOPE), qr1.reshape(SEQ, HEADS, HALF_ROPE),
                             qr2.reshape(SEQ, HEADS, HALF_ROPE)], axis=2).transpose(1, 0, 2).astype(BF16)
        kr = jnp.broadcast_to(jnp.concatenate([kr1, kr2], axis=1)[:, None, :], (SEQ, HEADS, QK_ROPE))
        k = jnp.concatenate([kvp[:, :n0].reshape(SEQ, HEADS, QK_NOPE), kr], axis=2).transpose(1, 0, 2).astype(BF16)
        v = _to_heads(kvp[:, n0:], V_HEAD).astype(BF16)
        o, lse = _mla_attn_fwd(q, k, v, name="mla_attn_fwd")
        o_flat = _from_heads(o).astype(BF16)
        f = _mm(o_flat, w_mo, name="mla_out")
        out = _post_fwd(f, xin, p['qg'], p['gate'], 1.0, name=f"post_fwd_{tag}")
        return out, dict(x=xin, hn=hn, cq=cq, ckv=ckv, cqn=cqn, ckvn=ckvn, q=q, k=k, v=v, o=o, lse=lse,
                         o_flat=o_flat, f=f, p=p, tag=tag)

    def dil_fwd(xin, i, sub):
        p = sub_params(i, sub)
        tag = f"l{i}s{sub}"
        hn = _pre_fwd(xin, p['pg'], p['sc'], p['sh'], name=f"pre_fwd_{tag}")
        heads = _proj_heads(hn, dil_w['in'], name="dil_proj")
        outs, lses = [], []
        for g, (window, d) in enumerate(DIL_GROUPS):
            o, lse = _dil_attn_fwd(heads, biases[g], g, d, name=f"dil_attn_fwd_g{g}")
            outs.append(o)
            lses.append(lse)
        mix = _dil_mix_fwd(outs, lses, name="dil_mix_fwd")
        o_flat = _from_heads(mix).astype(BF16)
        f = _mm(o_flat, dil_w['out'], name="dil_out")
        out = _post_fwd(f, xin, p['qg'], p['gate'], 1.0, name=f"post_fwd_{tag}")
        return out, dict(x=xin, hn=hn, heads=heads, outs=outs, lses=lses, o_flat=o_flat, f=f, p=p, tag=tag)

    saved = [None] * 6
    xs, saved[0] = ffn_fwd(x2, 0, 0, 0, tie=in_flight['token'][0, 0])
    xs, saved[1] = mla_fwd(xs, 0, 1)
    xs, saved[2] = ffn_fwd(xs, 0, 1, 2)
    got, last_fams = lax.optimization_barrier((arrive(in_flight, xs), later_fams[1]))
    ffn_w[1, 0] = as_ffn(got[0], got[1])
    dil_w['in'], dil_w['out'] = got[2].reshape(N_SHARD, D_MODEL, -1), got[3].reshape(D_MODEL, D_MODEL)
    in_flight = gather_later(last_fams, "l1s2")
    xs, saved[3] = ffn_fwd(xs, 1, 0, 0, tie=in_flight['token'][0, 0])
    xs, saved[4] = dil_fwd(xs, 1, 1)
    ffn_w[1, 1] = as_ffn(*arrive(in_flight, xs))
    xs, saved[5] = ffn_fwd(xs, 1, 1, 2)

    dx, loss_part = _loss(xs, target, name="loss")

    dmod = [[None] * 9 for _ in range(2)]
    dpre = [[None] * 3 for _ in range(2)]
    dpost = [[None] * 3 for _ in range(2)]
    ffn_units = {}
    row_unit = lambda g, r, j: ((r % 2, r // 2), 0, j)

    def close_sub(dhn, dout, sv, i, sub, res_dgate, res_dqg):
        p = sv['p']
        dxs, dsh, dsc, dpg = _pre_bwd(dhn, sv['x'], dout, p['pg'], p['sc'], name=f"pre_bwd_{sv['tag']}")
        dmod[i][3 * sub], dmod[i][3 * sub + 1], dmod[i][3 * sub + 2] = dsh, dsc, res_dgate
        dpre[i][sub], dpost[i][sub] = dpg, res_dqg
        return dxs

    def ffn_bwd(dout, sv, sub, tie=0.0):
        i, h, p, tag = sv['i'], sv['h'], sv['p'], sv['tag']
        w_gu, w_dn = ffn_w[i, h]
        df, dgate, dqg = _post_bwd(dout, sv['f'], p['qg'] + tie, p['gate'], FFN_RES, name=f"post_bwd_{tag}")
        u_dn = _mm(sv['a'], df, ta=True, tn_cap=D_MODEL // 2, out_shape=(2, N_SHARD, F_SHARD, D_MODEL // 2),
                   out_sel=lambda g, r, j: ((j, g), r, 0), name=f"ffn_dwd_{tag}")
        dgu = _ffn_dgu(df, w_dn, sv['gu'], name=f"ffn_dgu_{tag}")
        dgu = dgu.reshape(2 * N_SHARD, SEQ, F_SHARD)
        u_gu = _mm(dgu, sv['hn'], ta=True, out_shape=(2, N_SHARD, F_SHARD, D_MODEL),
                   out_sel=lambda g, r, j: ((g % 2, g // 2), r, j), name=f"ffn_dwgu_{tag}")
        ffn_units[i, h] = [u_gu, u_dn]
        dxs, dsh, dsc, dpg = _ffn_dhn(dgu, w_gu.reshape(2 * N_SHARD, D_MODEL, F_SHARD), sv['x'], dout, p['pg'], p['sc'],
                                      name=f"ffn_dhn_{tag}")
        dmod[i][3 * sub], dmod[i][3 * sub + 1], dmod[i][3 * sub + 2] = dsh, dsc, dgate
        dpre[i][sub], dpost[i][sub] = dpg, dqg
        return dxs

    def mla_bwd(dout, sv, i, sub, tie=0.0):
        p, tag = sv['p'], sv['tag']
        df, dgate, dqg = _post_bwd(dout, sv['f'], p['qg'] + tie, p['gate'], 1.0, name=f"post_bwd_{tag}")
        u_wo = _mm(sv['o_flat'], df, ta=True, tm_cap=128, out_shape=(2, N_SHARD, 128, D_MODEL), out_sel=row_unit,
                   name="mla_dwo")
        do_flat = _mm(df, w_mo, tb=True, name="mla_do")
        do = _to_heads(do_flat, V_HEAD)
        dq, dk, dv = _mla_attn_bwd(sv['q'], sv['k'], sv['v'], sv['o'], do, sv['lse'], name="mla_attn_bwd")
        dq_t = dq.transpose(1, 0, 2)
        dqr1, dqr2 = _rope(dq_t[:, :, QK_NOPE:QK_NOPE + HALF_ROPE].reshape(SEQ, -1),
                           dq_t[:, :, QK_NOPE + HALF_ROPE:].reshape(SEQ, -1), cos_q, -sin_q, name="rope_q_bwd")
        dqp = jnp.concatenate([dq_t[:, :, :QK_NOPE].reshape(SEQ, -1), dqr1, dqr2], axis=1).astype(BF16)
        dkr = _head_sum(dk[:, :, QK_NOPE:], name="mla_dkr_sum")
        dk1, dk2 = _rope(dkr[:, :HALF_ROPE], dkr[:, HALF_ROPE:], cos_k, -sin_k, name="rope_k_bwd")
        dkvp = jnp.concatenate([_from_heads(dk[:, :, :QK_NOPE]), _from_heads(dv)], axis=1).astype(BF16)
        g_wq = _q_unperm(_mm(sv['cqn'], dqp, ta=True, name="mla_dwq"))
        g_wkv = _kv_unperm(_mm(sv['ckvn'], dkvp, ta=True, name="mla_dwkv"))
        dcqn = _mm(dqp, wq_p, tb=True, name="mla_dcqn")
        dckvn = _mm(dkvp, wkv_p, tb=True, name="mla_dckvn")
        dcq, g_qn = _rms_bwd(dcqn, sv['cq'], mla_q_norm, name="mla_qnorm_bwd")
        dckv, g_kvn = _rms_bwd(dckvn, sv['ckv'], mla_kv_norm, name="mla_kvnorm_bwd")
        dlat = jnp.concatenate([dcq, dckv, dk1, dk2], axis=1).astype(BF16)
        u_win = _mm(sv['hn'], dlat, ta=True, tm_cap=128, out_shape=(2, N_SHARD, 128, dlat.shape[1]),
                    out_sel=row_unit, name="mla_dwin")
        dhn = _mm(dlat, w_in, tb=True, name="mla_dhn")
        col_unit = lambda t: (t.reshape(t.shape[0], N_SHARD, -1).transpose(1, 0, 2)
                              .reshape(N_SHARD, 2, t.shape[0] // 2, -1).transpose(1, 0, 2, 3))
        grads = dict(units=[u_win, col_unit(g_wq), col_unit(g_wkv), u_wo], q_norm=g_qn, kv_norm=g_kvn)
        return close_sub(dhn, dout, sv, i, sub, dgate, dqg), grads

    def dil_bwd(dout, sv, i, sub):
        p, tag = sv['p'], sv['tag']
        df, dgate, dqg = _post_bwd(dout, sv['f'], p['qg'], p['gate'], 1.0, name=f"post_bwd_{tag}")
        u_wo = _mm(sv['o_flat'], df, ta=True, tm_cap=128, out_shape=(2, N_SHARD, 128, D_MODEL), out_sel=row_unit,
                   name="dil_dwo")
        do = _to_heads(_mm(df, dil_w['out'], tb=True, name="dil_do"), 64)
        dos, dlts = _dil_mix_bwd(do, sv['outs'], sv['lses'], name="dil_mix_bwd")
        pieces = []
        bias_rows = []
        for g, (window, d) in enumerate(DIL_GROUPS):
            dq, dk, dv, dbias = _dil_attn_bwd(sv['heads'], biases[g], sv['lses'][g], dos[g], dlts[g], g, d,
                                              name=f"dil_attn_bwd_g{g}")
            pieces += [dq, dk, dv]
            bias_rows.append(_bias_grad(dbias, buckets[g], name=f"dil_bias_grad_g{g}")[:, 0, :])
        dheads = jnp.concatenate(pieces).astype(BF16)
        u_win = _proj_heads_dw(sv['hn'], dheads, name="dil_dwin")
        dhn = _proj_heads_dx(dheads, dil_w['in'], name="dil_dhn")
        g_bias = jnp.concatenate(bias_rows, axis=0).T
        grads = dict(units=[u_win, u_wo], rel_bias=g_bias)
        return close_sub(dhn, dout, sv, i, sub, dgate, dqg), grads

    def to_sibling(units, tag):
        n = len(units)
        send, recv, thru, token = _copies_start(units, [lax.empty(u.shape[1:], F32) for u in units], _sibling_plan, n,
                                                name=f"rs{tag}_sibling_start")
        return dict(send=send, recv=recv, thru=thru, n=n, tag=tag), token[0, 0]

    def from_sibling(st, after):
        n, tag = st['n'], st['tag']
        thru = _copies_wait(st['send'], st['recv'], st['thru'], n, _sibling_plan, after, name=f"rs{tag}_sibling_wait")
        return [_add_half(u, g, half_idx, name=f"rs{tag}_add_half_{k}") for k, (u, g) in enumerate(zip(thru[:n], thru[n:]))]

    def to_chips(parts, tag):
        n = len(parts)
        send, recv, thru, token = _copies_start([w for _, w in parts],
                                                [lax.empty((3,) + w.shape[1:], BF16) for _, w in parts], _chips_plan,
                                                3 * n, name=f"rs{tag}_chips_start")
        return dict(send=send, recv=recv, thru=thru, n=n, tag=tag, parts=parts), token[0, 0]

    def from_chips(st, after):
        n, tag = st['n'], st['tag']
        thru = _copies_wait(st['send'], st['recv'], st['thru'], n, _chips_plan, after, name=f"rs{tag}_chips_wait")
        return [_add_shards(p, g, shard_idx, name=f"rs{tag}_add_shards_{k}")
                for k, ((p, _), g) in enumerate(zip(st['parts'], thru[n:]))]

    dx = ffn_bwd(dx, saved[5], 2)
    dx, dil_g = dil_bwd(dx, saved[4], 1, 1)
    dx = ffn_bwd(dx, saved[3], 0)
    st1, tok = to_sibling([*ffn_units[1, 1], *dil_g['units'], *ffn_units[1, 0]], "1")
    dx = ffn_bwd(dx, saved[2], 2, tie=tok)
    st1, tok1 = to_chips(from_sibling(st1, dx), "1")
    st2, tok2 = to_sibling(ffn_units[0, 1], "2")
    dx, mla_g = mla_bwd(dx, saved[1], 0, 1, tie=tok1 + tok2)
    reds1 = from_chips(st1, dx)
    st2, tok = to_chips(from_sibling(st2, dx), "2")
    dx = ffn_bwd(dx, saved[0], 0, tie=tok)
    reds2 = from_chips(st2, dx)
    grad_x = dx[None]

    pad_row = lambda v: jnp.pad(v.reshape(-1), (0, (-v.size) % D_MODEL)).reshape(-1, D_MODEL)
    small = jnp.concatenate(
        [jnp.concatenate([dmod[i][r] for i in range(2) for r in range(9)], axis=0),
         jnp.concatenate([dpre[i][s] for i in range(2) for s in range(3)], axis=0),
         jnp.concatenate([dpost[i][s] for i in range(2) for s in range(3)], axis=0),
         pad_row(mla_g['q_norm']), pad_row(mla_g['kv_norm']), pad_row(dil_g['rel_bias']), pad_row(loss_part)], axis=0)
    small = jnp.pad(small, ((0, SMALL_ROWS - small.shape[0]), (0, 0)))
    small_all = _all_gather(small, name="ag_small_grads", in_vmem=True)
    small_sum = _sum_devices(small_all, 8, name="sum_small_grads")
    g_b_mod = small_sum[0:18].reshape(2, 9 * D_MODEL)
    my_cols = lambda t: lax.dynamic_slice_in_dim(t, shard_id * 256, 256, axis=2)
    g_norm_pre = my_cols(small_sum[18:24].reshape(2, 3, D_MODEL))
    g_norm_post = my_cols(small_sum[24:30].reshape(2, 3, D_MODEL))
    g_q_norm = small_sum[30, :Q_LORA].reshape(1, Q_LORA)
    g_kv_norm = small_sum[31, :KV_LORA].reshape(1, KV_LORA)
    g_rel_bias = small_sum[32:34].reshape(-1)[:N_BUCKETS * 48].reshape(N_BUCKETS, 48)
    loss = small_sum[34, 0]
    dmod_all = small_all.reshape(8, SMALL_ROWS, D_MODEL)[:, 0:18].reshape(8, 2, 9 * D_MODEL)
    dmod_cols = lax.dynamic_slice_in_dim(dmod_all, shard_id * 2304, 2304, axis=2).transpose(1, 0, 2)
    g_w_mod = _mm(silu_c, dmod_cols.astype(BF16), ta=True, tn_cap=768, name="w_mod_grad")

    units0 = [*mla_g['units'], *ffn_units[0, 0]]
    got_a = _swap_halves(units0, name="rs0_sibling")
    parts0 = [_add_half(u, g, half_idx, name=f"rs0_add_half_{k}") for k, (u, g) in enumerate(zip(units0, got_a))]
    got_b = _send_to_chips([w for _, w in parts0], name="rs0_chips")
    reds0 = [_add_shards(p, g, shard_idx, name=f"rs0_add_shards_{k}")
             for k, ((p, _), g) in enumerate(zip(parts0, got_b))]
    fin = _pair_gather(reds1 + reds2 + reds0, name="rs_pair_gather")
    ffn_fin = {(1, 1): fin[0:2], (1, 0): fin[4:6], (0, 1): fin[6:8], (0, 0): fin[12:14]}
    swap = lambda t: jnp.swapaxes(t, 2, 3)
    per_ffn = lambda pick: jnp.stack([jnp.stack([pick(*ffn_fin[i, h]) for h in range(2)]) for i in range(2)])
    reduced = dict(ffn_w_gate=swap(per_ffn(lambda gu, dn: gu[0])), ffn_w_up=swap(per_ffn(lambda gu, dn: gu[1])),
                   ffn_w_down=per_ffn(lambda gu, dn: jnp.concatenate([dn[0], dn[1]], axis=1)))
    for n, t in zip(['dil_w_in', 'dil_w_o', 'mla_w_in', 'mla_w_q_up', 'mla_w_kv_up', 'mla_w_o'], fin[2:4] + fin[8:12]):
        reduced[n] = t.reshape(given[n].shape)

    grads = dict(norm_pre=g_norm_pre, norm_post=g_norm_post, w_mod=g_w_mod, b_mod=g_b_mod, mla_q_norm=g_q_norm,
                 mla_kv_norm=g_kv_norm, rel_bias=g_rel_bias, **reduced)

    deltas, new_m, new_v = {}, {}, {}
    for n in WEIGHTS:
        view = swap if n in ('ffn_w_gate', 'ffn_w_up') else (lambda t: t)
        outs = _adamw(view(given[n]), view(grads[n]), view(given["m_" + n]), view(given["v_" + n]), name=f"adamw_{n}")
        deltas[n], new_m[n], new_v[n] = (view(t) for t in outs)
    return (loss, grad_x, *[grads[n] for n in WEIGHTS], *[deltas[n] for n in WEIGHTS],
            *[new_m[n] for n in WEIGHTS], *[new_v[n] for n in WEIGHTS])
```

```python
import math

import jax
import jax.numpy as jnp
from jax import lax
from jax.experimental import pallas as pl
from jax.experimental.pallas import tpu as pltpu

F32 = jnp.float32
BF16 = jnp.bfloat16
MESH = pl.DeviceIdType.MESH

SEQ = 2048
D_MODEL = 1024
D_FF = 2816
N_SHARD = 4
F_SHARD = D_FF // N_SHARD
EPS = 1e-6
FFN_RES = 0.5
HEADS = 16
Q_LORA, KV_LORA, QK_NOPE, QK_ROPE, V_HEAD = 384, 256, 64, 32, 64
HALF_ROPE = QK_ROPE // 2
ROPE_THETA = 10000.0
DIL_GROUPS = ((128, 1), (512, 4), (2048, 16))
DIL_BLOCK = 128
N_BUCKETS = 32
MAX_DISTANCE = 2048
ADAM_LR, ADAM_B1, ADAM_B2, ADAM_EPS, ADAM_WD, ADAM_STEP = 0.001, 0.9, 0.999, 1e-08, 0.01, 10

VMEM_LIMIT = 48 * 1024 * 1024
SMALL_ROWS = 40

WEIGHTS = ['norm_pre', 'norm_post', 'w_mod', 'b_mod', 'ffn_w_gate', 'ffn_w_up', 'ffn_w_down', 'mla_w_in',
           'mla_q_norm', 'mla_w_q_up', 'mla_kv_norm', 'mla_w_kv_up', 'mla_w_o', 'dil_w_in', 'dil_w_o', 'rel_bias']


def _cparams(**kw):
    return pltpu.CompilerParams(vmem_limit_bytes=VMEM_LIMIT, **kw)


def _pick(n, cap, mult=128):
    if n <= cap:
        return n
    best = n
    for t in range(mult, cap + 1, mult):
        if n % t == 0:
            best = t
    return best


def _mm(a, b, *, name, ta=False, tb=False, reduce_g=False, bias=None, out_dtype=F32, tm_cap=512, tn_cap=1024,
        g_n=None, b_sel=None, out_shape=None, out_sel=None, out_buf=None):
    a3 = a if a.ndim == 3 else a[None]
    ga = a3.shape[0]
    if b_sel is None:
        b_n = b if b.ndim == 3 else b[None]
        gb = b_n.shape[0]
        b_sel = (lambda g: (g,)) if gb > 1 else (lambda g: (0,))
        g_n = max(ga, gb)
    else:
        b_n = b
    k_dim, m_dim = (a3.shape[1], a3.shape[2]) if ta else (a3.shape[2], a3.shape[1])
    k2, n_dim = (b_n.shape[-1], b_n.shape[-2]) if tb else (b_n.shape[-2], b_n.shape[-1])
    assert k_dim == k2, (a.shape, b.shape)
    tm = _pick(m_dim, tm_cap, 128 if ta else 8)
    tn = _pick(n_dim, tn_cap, 128)
    mt, nt = m_dim // tm, n_dim // tn
    dims = (((0 if ta else 1,), (1 if tb else 0,)), ((), ()))

    if reduce_g:
        grid = (mt, nt, g_n)
        ids = lambda i, j, g: (g, i, j)
    else:
        grid = (g_n, mt, nt)
        ids = lambda g, i, j: (g, i, j)

    def a_map(*p):
        g, i, j = ids(*p)
        g = g if ga > 1 else 0
        return (g, 0, i) if ta else (g, i, 0)

    def b_map(*p):
        g, i, j = ids(*p)
        return (*b_sel(g), j, 0) if tb else (*b_sel(g), 0, j)

    b_lead = (None,) * (b_n.ndim - 2)
    a_spec = pl.BlockSpec((None, k_dim, tm) if ta else (None, tm, k_dim), a_map)
    b_spec = pl.BlockSpec(b_lead + ((tn, k_dim) if tb else (k_dim, tn)), b_map)
    in_specs = [a_spec, b_spec]
    operands = [a3, b_n]
    if bias is not None:
        assert not reduce_g and bias.shape == (g_n, 1, n_dim)
        in_specs.append(pl.BlockSpec((None, 1, tn), lambda g, i, j: (g, 0, j)))
        operands.append(bias)
    aliases = {}
    if out_buf is not None:
        assert tuple(out_buf.shape) == tuple(out_shape) and out_buf.dtype == out_dtype
        in_specs.append(pl.BlockSpec(memory_space=pl.ANY))
        operands.append(out_buf)
        aliases = {len(operands) - 1: 0}

    if reduce_g:
        out_spec = pl.BlockSpec((tm, tn), lambda i, j, g: (i, j))
        out_sds = jax.ShapeDtypeStruct((m_dim, n_dim), F32)
    elif out_shape is not None:
        def o_map(g, i, j):
            lead, rb, cb = out_sel(g, i, j)
            return (*lead, rb, cb)

        out_spec = pl.BlockSpec((None,) * (len(out_shape) - 2) + (tm, tn), o_map)
        out_sds = jax.ShapeDtypeStruct(tuple(out_shape), out_dtype)
    else:
        out_spec = pl.BlockSpec((None, tm, tn), lambda g, i, j: (g, i, j))
        out_sds = jax.ShapeDtypeStruct((g_n, m_dim, n_dim), out_dtype)

    def body(a_ref, b_ref, *rest):
        o_ref = rest[-1]
        r = lax.dot_general(a_ref[...].astype(BF16), b_ref[...].astype(BF16), dims, preferred_element_type=F32)
        if bias is not None:
            r = r + rest[0][...]
        if reduce_g:
            g = pl.program_id(2)

            @pl.when(g == 0)
            def _():
                o_ref[...] = r

            @pl.when(g > 0)
            def _():
                o_ref[...] += r
        else:
            o_ref[...] = r.astype(o_ref.dtype)

    out = pl.pallas_call(body, grid=grid, in_specs=in_specs, out_specs=out_spec, out_shape=out_sds,
                         input_output_aliases=aliases, compiler_params=_cparams(), name=name)(*operands)
    if not reduce_g and out_shape is None and a.ndim == 2 and b.ndim == 2:
        out = out[0]
    return out


def _rows(tm, w):
    return pl.BlockSpec((tm, w), lambda i: (i, 0))


def _vec(w):
    return pl.BlockSpec((1, w), lambda i: (0, 0))


def _rstd(v):
    return lax.rsqrt(jnp.mean(v * v, axis=-1, keepdims=True) + EPS)


def _pre_fwd(x, pg, sc, sh, *, name):
    s_n, w = x.shape
    tm = _pick(s_n, 512, 8)

    def body(x_ref, pg_ref, sc_ref, sh_ref, o_ref):
        xv = x_ref[...]
        n = (xv * _rstd(xv)) * pg_ref[...]
        o_ref[...] = (n * (1.0 + sc_ref[...]) + sh_ref[...]).astype(o_ref.dtype)

    return pl.pallas_call(body, grid=(s_n // tm,), in_specs=[_rows(tm, w), _vec(w), _vec(w), _vec(w)],
                          out_specs=_rows(tm, w), out_shape=jax.ShapeDtypeStruct((s_n, w), BF16),
                          compiler_params=_cparams(), name=name)(x, pg, sc, sh)


def _post_fwd(f, x, qg, gate, res_w, *, name):
    s_n, w = x.shape
    tm = _pick(s_n, 512, 8)

    def body(f_ref, x_ref, qg_ref, gate_ref, o_ref):
        fv = f_ref[...]
        y = (fv * _rstd(fv)) * qg_ref[...]
        o_ref[...] = x_ref[...] + (res_w * gate_ref[...]) * y

    return pl.pallas_call(body, grid=(s_n // tm,), in_specs=[_rows(tm, w), _rows(tm, w), _vec(w), _vec(w)],
                          out_specs=_rows(tm, w), out_shape=jax.ShapeDtypeStruct((s_n, w), F32),
                          compiler_params=_cparams(), name=name)(f, x, qg, gate)


def _post_bwd(dout, f, qg, gate, res_w, *, name):
    s_n, w = f.shape
    tm = _pick(s_n, 512, 8)

    def body(do_ref, f_ref, qg_ref, gate_ref, df_ref, dgate_ref, dqg_ref):
        @pl.when(pl.program_id(0) == 0)
        def _():
            dgate_ref[...] = jnp.zeros_like(dgate_ref)
            dqg_ref[...] = jnp.zeros_like(dqg_ref)

        do = do_ref[...]
        fv = f_ref[...]
        r = _rstd(fv)
        fh = fv * r
        qg_v = qg_ref[...]
        dgate_ref[...] += res_w * jnp.sum(do * (fh * qg_v), axis=0, keepdims=True)
        dy = do * (res_w * gate_ref[...])
        dqg_ref[...] += jnp.sum(dy * fh, axis=0, keepdims=True)
        dfh = dy * qg_v
        df = r * (dfh - fh * jnp.mean(dfh * fh, axis=-1, keepdims=True))
        df_ref[...] = df.astype(df_ref.dtype)

    return pl.pallas_call(
        body, grid=(s_n // tm,), in_specs=[_rows(tm, w), _rows(tm, w), _vec(w), _vec(w)],
        out_specs=[_rows(tm, w), _vec(w), _vec(w)],
        out_shape=[jax.ShapeDtypeStruct((s_n, w), BF16), jax.ShapeDtypeStruct((1, w), F32),
                   jax.ShapeDtypeStruct((1, w), F32)],
        compiler_params=_cparams(), name=name)(dout, f, qg, gate)


def _pre_bwd(dhn, x, dout, pg, sc, *, name):
    s_n, w = x.shape
    tm = _pick(s_n, 512, 8)

    def body(dhn_ref, x_ref, do_ref, pg_ref, sc_ref, dx_ref, dsh_ref, dsc_ref, dpg_ref):
        @pl.when(pl.program_id(0) == 0)
        def _():
            dsh_ref[...] = jnp.zeros_like(dsh_ref)
            dsc_ref[...] = jnp.zeros_like(dsc_ref)
            dpg_ref[...] = jnp.zeros_like(dpg_ref)

        dhn_v = dhn_ref[...]
        xv = x_ref[...]
        r = _rstd(xv)
        xh = xv * r
        pg_v = pg_ref[...]
        dsh_ref[...] += jnp.sum(dhn_v, axis=0, keepdims=True)
        dsc_ref[...] += jnp.sum(dhn_v * (xh * pg_v), axis=0, keepdims=True)
        dn = dhn_v * (1.0 + sc_ref[...])
        dpg_ref[...] += jnp.sum(dn * xh, axis=0, keepdims=True)
        dxh = dn * pg_v
        dx_ref[...] = do_ref[...] + r * (dxh - xh * jnp.mean(dxh * xh, axis=-1, keepdims=True))

    vec = jax.ShapeDtypeStruct((1, w), F32)
    return pl.pallas_call(
        body, grid=(s_n // tm,), in_specs=[_rows(tm, w), _rows(tm, w), _rows(tm, w), _vec(w), _vec(w)],
        out_specs=[_rows(tm, w), _vec(w), _vec(w), _vec(w)],
        out_shape=[jax.ShapeDtypeStruct((s_n, w), F32), vec, vec, vec],
        compiler_params=_cparams(), name=name)(dhn, x, dout, pg, sc)


def _rms_fwd(x, g, *, name):
    s_n, w = x.shape
    tm = _pick(s_n, 512, 8)

    def body(x_ref, g_ref, o_ref):
        xv = x_ref[...]
        o_ref[...] = ((xv * _rstd(xv)) * g_ref[...]).astype(o_ref.dtype)

    return pl.pallas_call(body, grid=(s_n // tm,), in_specs=[_rows(tm, w), _vec(w)], out_specs=_rows(tm, w),
                          out_shape=jax.ShapeDtypeStruct((s_n, w), BF16), compiler_params=_cparams(),
                          name=name)(x, g)


def _rms_bwd(dy, x, g, *, name):
    s_n, w = x.shape
    tm = _pick(s_n, 512, 8)

    def body(dy_ref, x_ref, g_ref, dx_ref, dg_ref):
        @pl.when(pl.program_id(0) == 0)
        def _():
            dg_ref[...] = jnp.zeros_like(dg_ref)

        dy_v = dy_ref[...]
        xv = x_ref[...]
        r = _rstd(xv)
        xh = xv * r
        dg_ref[...] += jnp.sum(dy_v * xh, axis=0, keepdims=True)
        dxh = dy_v * g_ref[...]
        dx_ref[...] = r * (dxh - xh * jnp.mean(dxh * xh, axis=-1, keepdims=True))

    return pl.pallas_call(
        body, grid=(s_n // tm,), in_specs=[_rows(tm, w), _rows(tm, w), _vec(w)],
        out_specs=[_rows(tm, w), _vec(w)],
        out_shape=[jax.ShapeDtypeStruct((s_n, w), F32), jax.ShapeDtypeStruct((1, w), F32)],
        compiler_params=_cparams(), name=name)(dy, x, g)


def _rope(a1, a2, cos, sin, *, name):
    s_n, w = a1.shape
    tm = _pick(s_n, 512, 8)

    def body(a1_ref, a2_ref, c_ref, s_ref, r1_ref, r2_ref):
        u, v, c_v, s_v = a1_ref[...], a2_ref[...], c_ref[...], s_ref[...]
        r1_ref[...] = u * c_v - v * s_v
        r2_ref[...] = u * s_v + v * c_v

    sd = jax.ShapeDtypeStruct((s_n, w), F32)
    return pl.pallas_call(body, grid=(s_n // tm,), in_specs=[_rows(tm, w)] * 4, out_specs=[_rows(tm, w)] * 2,
                          out_shape=[sd, sd], compiler_params=_cparams(), name=name)(a1, a2, cos, sin)


def _silu_bf16(x, *, name):
    def body(x_ref, o_ref):
        xv = x_ref[...]
        o_ref[...] = (xv * jax.nn.sigmoid(xv)).astype(o_ref.dtype)

    return pl.pallas_call(body, out_shape=jax.ShapeDtypeStruct(x.shape, BF16), name=name)(x)


def _loss(y, target, *, name):
    s_n, w = y.shape
    tm = _pick(s_n, 512, 8)

    def body(y_ref, t_ref, dy_ref, l_ref):
        @pl.when(pl.program_id(0) == 0)
        def _():
            l_ref[...] = jnp.zeros_like(l_ref)

        e = y_ref[...] - t_ref[...]
        dy_ref[...] = e * (1.0 / w)
        row = jnp.mean(e * e, axis=-1, keepdims=True)
        l_ref[...] += 0.5 * jnp.sum(row, axis=0, keepdims=True)

    return pl.pallas_call(
        body, grid=(s_n // tm,), in_specs=[_rows(tm, w), _rows(tm, w)],
        out_specs=[_rows(tm, w), pl.BlockSpec((1, 1), lambda i: (0, 0))],
        out_shape=[jax.ShapeDtypeStruct((s_n, w), F32), jax.ShapeDtypeStruct((1, 1), F32)],
        compiler_params=_cparams(), name=name)(y, target)


FFN_TM = 512


def _ffn_up(hn, w_gu, *, name):
    s_n, d = hn.shape
    f = w_gu.shape[-1]
    tm = _pick(s_n, FFN_TM, 8)

    def body(hn_ref, wg_ref, wu_ref, gu_ref, a_ref):
        xv = hn_ref[...]
        g = jnp.dot(xv, wg_ref[...], preferred_element_type=F32)
        u = jnp.dot(xv, wu_ref[...], preferred_element_type=F32)
        gu_ref[0] = g.astype(BF16)
        gu_ref[1] = u.astype(BF16)
        a_ref[...] = ((g * jax.nn.sigmoid(g)) * u).astype(BF16)

    w_blk = lambda t: pl.BlockSpec((None, None, d, f), lambda s, m: (s, t, 0, 0))
    return pl.pallas_call(
        body, grid=(N_SHARD, s_n // tm),
        in_specs=[pl.BlockSpec((tm, d), lambda s, m: (m, 0)), w_blk(0), w_blk(1)],
        out_specs=[pl.BlockSpec((None, 2, tm, f), lambda s, m: (s, 0, m, 0)),
                   pl.BlockSpec((None, tm, f), lambda s, m: (s, m, 0))],
        out_shape=[jax.ShapeDtypeStruct((N_SHARD, 2, s_n, f), BF16), jax.ShapeDtypeStruct((N_SHARD, s_n, f), BF16)],
        compiler_params=_cparams(), name=name)(hn, w_gu, w_gu)


def _ffn_down(a, w_dn, x, qg, gate, res_w, *, name):
    g_n, s_n, f = a.shape
    d = w_dn.shape[-1]
    tm = _pick(s_n, FFN_TM, 8)

    def body(a_ref, w_ref, x_ref, qg_ref, gate_ref, f_ref, o_ref):
        g = pl.program_id(1)
        r = jnp.dot(a_ref[...], w_ref[...], preferred_element_type=F32)

        @pl.when(g == 0)
        def _():
            f_ref[...] = r

        @pl.when(g > 0)
        def _():
            f_ref[...] += r

        @pl.when(g == g_n - 1)
        def _():
            fv = f_ref[...]
            y = (fv * _rstd(fv)) * qg_ref[...]
            o_ref[...] = x_ref[...] + (res_w * gate_ref[...]) * y

    row = pl.BlockSpec((tm, d), lambda m, g: (m, 0))
    vec = pl.BlockSpec((1, d), lambda m, g: (0, 0))
    sd = jax.ShapeDtypeStruct((s_n, d), F32)
    return pl.pallas_call(
        body, grid=(s_n // tm, g_n),
        in_specs=[pl.BlockSpec((None, tm, f), lambda m, g: (g, m, 0)), pl.BlockSpec((None, f, d), lambda m, g: (g, 0, 0)),
                  row, vec, vec],
        out_specs=[row, row], out_shape=[sd, sd], compiler_params=_cparams(), name=name)(a, w_dn, x, qg, gate)


def _ffn_dhn(dgu, w_gu, x, dout, pg, sc, *, name):
    g_n, s_n, f = dgu.shape
    d = w_gu.shape[-2]
    tm = _pick(s_n, FFN_TM, 8)

    def body(a_ref, w_ref, x_ref, do_ref, pg_ref, sc_ref, dx_ref, dsh_ref, dsc_ref, dpg_ref, acc_ref):
        m, g = pl.program_id(0), pl.program_id(1)
        r = lax.dot_general(a_ref[...], w_ref[...], (((1,), (1,)), ((), ())), preferred_element_type=F32)

        @pl.when(g == 0)
        def _():
            acc_ref[...] = r

        @pl.when(g > 0)
        def _():
            acc_ref[...] += r

        @pl.when((m == 0) & (g == 0))
        def _():
            dsh_ref[...] = jnp.zeros_like(dsh_ref)
            dsc_ref[...] = jnp.zeros_like(dsc_ref)
            dpg_ref[...] = jnp.zeros_like(dpg_ref)

        @pl.when(g == g_n - 1)
        def _():
            dhn_v = acc_ref[...]
            xv = x_ref[...]
            rs = _rstd(xv)
            xh = xv * rs
            pg_v = pg_ref[...]
            dsh_ref[...] += jnp.sum(dhn_v, axis=0, keepdims=True)
            dsc_ref[...] += jnp.sum(dhn_v * (xh * pg_v), axis=0, keepdims=True)
            dn = dhn_v * (1.0 + sc_ref[...])
            dpg_ref[...] += jnp.sum(dn * xh, axis=0, keepdims=True)
            dxh = dn * pg_v
            dx_ref[...] = do_ref[...] + rs * (dxh - xh * jnp.mean(dxh * xh, axis=-1, keepdims=True))

    row = pl.BlockSpec((tm, d), lambda m, g: (m, 0))
    vec = pl.BlockSpec((1, d), lambda m, g: (0, 0))
    vsd = jax.ShapeDtypeStruct((1, d), F32)
    return pl.pallas_call(
        body, grid=(s_n // tm, g_n),
        in_specs=[pl.BlockSpec((None, tm, f), lambda m, g: (g, m, 0)), pl.BlockSpec((None, d, f), lambda m, g: (g, 0, 0)),
                  row, row, vec, vec],
        out_specs=[row, vec, vec, vec], out_shape=[jax.ShapeDtypeStruct((s_n, d), F32), vsd, vsd, vsd],
        scratch_shapes=[pltpu.VMEM((tm, d), F32)], compiler_params=_cparams(), name=name)(dgu, w_gu, x, dout, pg, sc)


def _ffn_dgu(df, w_dn, gu, *, name):
    s_n, d = df.shape
    f = w_dn.shape[-2]
    tm = _pick(s_n, FFN_TM, 8)

    def body(df_ref, wd_ref, gu_ref, o_ref):
        da = lax.dot_general(df_ref[...], wd_ref[...], (((1,), (1,)), ((), ())), preferred_element_type=F32)
        g = gu_ref[0].astype(F32)
        u = gu_ref[1].astype(F32)
        sig = jax.nn.sigmoid(g)
        o_ref[0] = (da * u * (sig * (1.0 + g * (1.0 - sig)))).astype(BF16)
        o_ref[1] = (da * (g * sig)).astype(BF16)

    gu_blk = pl.BlockSpec((None, 2, tm, f), lambda s, m: (s, 0, m, 0))
    return pl.pallas_call(
        body, grid=(N_SHARD, s_n // tm),
        in_specs=[pl.BlockSpec((tm, d), lambda s, m: (m, 0)),
                  pl.BlockSpec((None, f, d), lambda s, m: (s, 0, 0)), gu_blk],
        out_specs=gu_blk, out_shape=jax.ShapeDtypeStruct((N_SHARD, 2, s_n, f), BF16),
        compiler_params=_cparams(), name=name)(df, w_dn, gu)


_NT = (((1,), (1,)), ((), ()))
_TN = (((0,), (0,)), ((), ()))
MLA_TQ = 256


def _causal_mask(i, tq, s_n):
    qpos = i * tq + lax.broadcasted_iota(jnp.int32, (tq, s_n), 0)
    kpos = lax.broadcasted_iota(jnp.int32, (tq, s_n), 1)
    return kpos <= qpos


def _mla_attn_fwd(q, k, v, *, name):
    h_n, s_n, dq = q.shape
    dv = v.shape[-1]
    tq = MLA_TQ
    scale = float(dq) ** -0.5

    def body(q_ref, k_ref, v_ref, o_ref, lse_ref):
        i = pl.program_id(1)
        for e in range(1, s_n // tq + 1):
            @pl.when(i == e - 1)
            def _(ext=e * tq):
                mask = _causal_mask(i, tq, ext)
                s = lax.dot_general(q_ref[...], k_ref[0:ext, :], _NT, preferred_element_type=F32) * scale
                s = jnp.where(mask, s, -jnp.inf)
                m = jnp.max(s, axis=-1, keepdims=True)
                p = jnp.exp(s - m)
                l = jnp.sum(p, axis=-1, keepdims=True)
                o = jnp.dot(p.astype(BF16), v_ref[0:ext, :], preferred_element_type=F32)
                o_ref[...] = o / l
                lse_ref[...] = m + jnp.log(l)

    return pl.pallas_call(
        body, grid=(h_n, s_n // tq),
        in_specs=[pl.BlockSpec((None, tq, dq), lambda h, i: (h, i, 0)),
                  pl.BlockSpec((None, s_n, dq), lambda h, i: (h, 0, 0)),
                  pl.BlockSpec((None, s_n, dv), lambda h, i: (h, 0, 0))],
        out_specs=[pl.BlockSpec((None, tq, dv), lambda h, i: (h, i, 0)),
                   pl.BlockSpec((None, tq, 1), lambda h, i: (h, i, 0))],
        out_shape=[jax.ShapeDtypeStruct((h_n, s_n, dv), F32), jax.ShapeDtypeStruct((h_n, s_n, 1), F32)],
        compiler_params=_cparams(), name=name)(q, k, v)


def _mla_attn_bwd(q, k, v, o, do, lse, *, name):
    h_n, s_n, dq = q.shape
    dv = v.shape[-1]
    tq = MLA_TQ
    scale = float(dq) ** -0.5

    def body(q_ref, k_ref, v_ref, o_ref, do_ref, lse_ref, dq_ref, dk_ref, dv_ref):
        i = pl.program_id(1)

        @pl.when(i == 0)
        def _():
            dk_ref[...] = jnp.zeros_like(dk_ref)
            dv_ref[...] = jnp.zeros_like(dv_ref)

        for e in range(1, s_n // tq + 1):
            @pl.when(i == e - 1)
            def _(ext=e * tq):
                mask = _causal_mask(i, tq, ext)
                qv, kv, vv = q_ref[...], k_ref[0:ext, :], v_ref[0:ext, :]
                do_v = do_ref[...]
                s = lax.dot_general(qv, kv, _NT, preferred_element_type=F32) * scale
                p = jnp.where(mask, jnp.exp(s - lse_ref[...]), 0.0)
                dob = do_v.astype(BF16)
                dv_ref[0:ext, :] += lax.dot_general(p.astype(BF16), dob, _TN, preferred_element_type=F32)
                dp = lax.dot_general(dob, vv, _NT, preferred_element_type=F32)
                delta = jnp.sum(do_v * o_ref[...], axis=-1, keepdims=True)
                dsb = (p * (dp - delta) * scale).astype(BF16)
                dq_ref[...] = jnp.dot(dsb, kv, preferred_element_type=F32)
                dk_ref[0:ext, :] += lax.dot_general(dsb, qv, _TN, preferred_element_type=F32)

    return pl.pallas_call(
        body, grid=(h_n, s_n // tq),
        in_specs=[pl.BlockSpec((None, tq, dq), lambda h, i: (h, i, 0)),
                  pl.BlockSpec((None, s_n, dq), lambda h, i: (h, 0, 0)),
                  pl.BlockSpec((None, s_n, dv), lambda h, i: (h, 0, 0)),
                  pl.BlockSpec((None, tq, dv), lambda h, i: (h, i, 0)),
                  pl.BlockSpec((None, tq, dv), lambda h, i: (h, i, 0)),
                  pl.BlockSpec((None, tq, 1), lambda h, i: (h, i, 0))],
        out_specs=[pl.BlockSpec((None, tq, dq), lambda h, i: (h, i, 0)),
                   pl.BlockSpec((None, s_n, dq), lambda h, i: (h, 0, 0)),
                   pl.BlockSpec((None, s_n, dv), lambda h, i: (h, 0, 0))],
        out_shape=[jax.ShapeDtypeStruct((h_n, s_n, dq), F32), jax.ShapeDtypeStruct((h_n, s_n, dq), F32),
                   jax.ShapeDtypeStruct((h_n, s_n, dv), F32)],
        compiler_params=_cparams(), name=name)(q, k, v, o, do, lse)


def _head_sum(x, *, name):
    h_n, s_n, w = x.shape
    tm = _pick(s_n, 512, 8)

    def body(x_ref, o_ref):
        o_ref[...] = jnp.sum(x_ref[...], axis=0)

    return pl.pallas_call(body, grid=(s_n // tm,), in_specs=[pl.BlockSpec((h_n, tm, w), lambda i: (0, i, 0))],
                          out_specs=_rows(tm, w), out_shape=jax.ShapeDtypeStruct((s_n, w), F32),
                          compiler_params=_cparams(), name=name)(x)


N_BLK = SEQ // DIL_BLOCK
DIL_SCALE = 64 ** -0.5


def _dil_masks():
    iq = lax.broadcasted_iota(jnp.int32, (DIL_BLOCK, 2 * DIL_BLOCK), 0)
    ik = lax.broadcasted_iota(jnp.int32, (DIL_BLOCK, 2 * DIL_BLOCK), 1)
    rel = DIL_BLOCK + iq - ik
    both = (rel >= 0) & (rel <= DIL_BLOCK)
    iq1 = lax.broadcasted_iota(jnp.int32, (DIL_BLOCK, DIL_BLOCK), 0)
    ik1 = lax.broadcasted_iota(jnp.int32, (DIL_BLOCK, DIL_BLOCK), 1)
    return both, ik1 <= iq1


def _dil_block(j, d):
    nb = SEQ // d // DIL_BLOCK
    r, n = divmod(j, nb)
    first = n == 0
    rows = lambda start, size: pl.ds(start, size) if d == 1 else pl.ds(start, size, stride=d)
    q_rows = rows(n * DIL_BLOCK * d + r, DIL_BLOCK)
    k_rows = q_rows if first else rows((n - 1) * DIL_BLOCK * d + r, 2 * DIL_BLOCK)
    return q_rows, k_rows, (DIL_BLOCK if first else 0), first


def _dil_head_specs(s_n, e, g):
    return [pl.BlockSpec((None, s_n, e), lambda h, t=t: (g * 3 * HEADS + t * HEADS + h, 0, 0)) for t in range(3)]


def _dil_attn_fwd(heads, bias, g, d, *, name):
    _, s_n, e = heads.shape

    def body(q_ref, k_ref, v_ref, b_ref, o_ref, lse_ref):
        m_both, m_first = _dil_masks()
        for j in range(N_BLK):
            q_rows, k_rows, b_lo, first = _dil_block(j, d)
            qj = q_ref[q_rows, :].astype(BF16)
            kk = k_ref[k_rows, :].astype(BF16)
            vv = v_ref[k_rows, :].astype(BF16)
            s = lax.dot_general(qj, kk, _NT, preferred_element_type=F32) * DIL_SCALE + b_ref[:, b_lo:]
            s = jnp.where(m_first if first else m_both, s, -jnp.inf)
            m = jnp.max(s, axis=-1, keepdims=True)
            lse = m + jnp.log(jnp.sum(jnp.exp(s - m), axis=-1, keepdims=True))
            p = jnp.exp(s - lse)
            o_ref[q_rows, :] = jnp.dot(p.astype(BF16), vv, preferred_element_type=F32)
            lse_ref[q_rows, :] = lse

    head = lambda w: pl.BlockSpec((None, s_n, w), lambda h: (h, 0, 0))
    return pl.pallas_call(
        body, grid=(HEADS,),
        in_specs=_dil_head_specs(s_n, e, g) + [pl.BlockSpec((None, DIL_BLOCK, 2 * DIL_BLOCK), lambda h: (h, 0, 0))],
        out_specs=[head(e), head(1)],
        out_shape=[jax.ShapeDtypeStruct((HEADS, s_n, e), F32), jax.ShapeDtypeStruct((HEADS, s_n, 1), F32)],
        compiler_params=_cparams(), name=name)(heads, heads, heads, bias)


def _dil_attn_bwd(heads, bias, lse, do, dlt, g, d, *, name):
    _, s_n, e = heads.shape

    def body(q_ref, k_ref, v_ref, b_ref, lse_ref, do_ref, dlt_ref, dq_ref, dk_ref, dv_ref, db_ref):
        dk_ref[...] = jnp.zeros_like(dk_ref)
        dv_ref[...] = jnp.zeros_like(dv_ref)
        db_ref[...] = jnp.zeros_like(db_ref)
        m_both, m_first = _dil_masks()
        for j in range(N_BLK):
            q_rows, k_rows, b_lo, first = _dil_block(j, d)
            qj = q_ref[q_rows, :].astype(BF16)
            kk = k_ref[k_rows, :].astype(BF16)
            vv = v_ref[k_rows, :].astype(BF16)
            s = lax.dot_general(qj, kk, _NT, preferred_element_type=F32) * DIL_SCALE + b_ref[:, b_lo:]
            p = jnp.where(m_first if first else m_both, jnp.exp(s - lse_ref[q_rows, :]), 0.0)
            dob = do_ref[q_rows, :].astype(BF16)
            dv_ref[k_rows, :] += lax.dot_general(p.astype(BF16), dob, _TN, preferred_element_type=F32)
            dp = lax.dot_general(dob, vv, _NT, preferred_element_type=F32)
            ds = p * (dp - dlt_ref[q_rows, :])
            db_ref[:, b_lo:] += ds
            dsb = (ds * DIL_SCALE).astype(BF16)
            dq_ref[q_rows, :] = jnp.dot(dsb, kk, preferred_element_type=F32)
            dk_ref[k_rows, :] += lax.dot_general(dsb, qj, _TN, preferred_element_type=F32)

    head = lambda w: pl.BlockSpec((None, s_n, w), lambda h: (h, 0, 0))
    b_spec = pl.BlockSpec((None, DIL_BLOCK, 2 * DIL_BLOCK), lambda h: (h, 0, 0))
    sd = jax.ShapeDtypeStruct((HEADS, s_n, e), F32)
    return pl.pallas_call(
        body, grid=(HEADS,),
        in_specs=_dil_head_specs(s_n, e, g) + [b_spec, head(1), head(e), head(1)],
        out_specs=[head(e), head(e), head(e), b_spec],
        out_shape=[sd, sd, sd, jax.ShapeDtypeStruct((HEADS, DIL_BLOCK, 2 * DIL_BLOCK), F32)],
        compiler_params=_cparams(), name=name)(heads, heads, heads, bias, lse, do, dlt)


def _proj_heads(x, w, *, name):
    s_n, k = x.shape
    n = w.shape[-1]
    tm, tn, e = 512, 768, 64
    per_blk, n_blk = tn // e, n // tn

    def body(x_ref, w_ref, o_ref):
        r = jnp.dot(x_ref[...], w_ref[...], preferred_element_type=F32)
        for j in range(per_blk):
            o_ref[j] = r[:, e * j:e * (j + 1)]

    return pl.pallas_call(
        body, grid=(w.shape[0], n_blk, s_n // tm),
        in_specs=[pl.BlockSpec((tm, k), lambda s, b, m: (m, 0)), pl.BlockSpec((None, k, tn), lambda s, b, m: (s, 0, b))],
        out_specs=pl.BlockSpec((per_blk, tm, e), lambda s, b, m: (s * n_blk + b, m, 0)),
        out_shape=jax.ShapeDtypeStruct((w.shape[0] * n // e, s_n, e), F32), compiler_params=_cparams(),
        name=name)(x, w)


def _heads_cat(d_ref):
    return jnp.concatenate([d_ref[j] for j in range(d_ref.shape[0])], axis=1)


def _proj_heads_dw(x, dh, *, name):
    s_n, k = x.shape
    tn, e = 768, 64
    per_blk = tn // e
    n_blk = dh.shape[0] // N_SHARD // per_blk
    n = n_blk * tn

    def body(x_ref, d_ref, o_ref):
        o_ref[...] = lax.dot_general(x_ref[...], _heads_cat(d_ref), _TN, preferred_element_type=F32)

    return pl.pallas_call(
        body, grid=(N_SHARD, n_blk, 2),
        in_specs=[pl.BlockSpec((s_n, k // 2), lambda s, b, r: (0, r)),
                  pl.BlockSpec((per_blk, s_n, e), lambda s, b, r: (s * n_blk + b, 0, 0))],
        out_specs=pl.BlockSpec((None, None, k // 2, tn), lambda s, b, r: (r, s, 0, b)),
        out_shape=jax.ShapeDtypeStruct((2, N_SHARD, k // 2, n), F32), compiler_params=_cparams(), name=name)(x, dh)


def _proj_heads_dx(dh, w, *, name):
    k, n = w.shape[1:]
    s_n = dh.shape[1]
    tm, tn, e = 512, 768, 64
    per_blk, n_blk = tn // e, n // tn

    def body(d_ref, w_ref, o_ref):
        r = lax.dot_general(_heads_cat(d_ref), w_ref[...], _NT, preferred_element_type=F32)
        g = pl.program_id(1)

        @pl.when(g == 0)
        def _():
            o_ref[...] = r

        @pl.when(g > 0)
        def _():
            o_ref[...] += r

    return pl.pallas_call(
        body, grid=(s_n // tm, N_SHARD * n_blk),
        in_specs=[pl.BlockSpec((per_blk, tm, e), lambda m, g: (g, m, 0)),
                  pl.BlockSpec((None, k, tn), lambda m, g: (g // n_blk, 0, g % n_blk))],
        out_specs=pl.BlockSpec((tm, k), lambda m, g: (m, 0)),
        out_shape=jax.ShapeDtypeStruct((s_n, k), F32), compiler_params=_cparams(), name=name)(dh, w)


def _group_alpha(l_refs):
    ls = [r[...] for r in l_refs]
    m = jnp.maximum(jnp.maximum(ls[0], ls[1]), ls[2])
    es = [jnp.exp(l - m) for l in ls]
    tot = es[0] + es[1] + es[2]
    return [ex / tot for ex in es]


def _dil_mix_fwd(os_, ls_, *, name):
    h_n, s_n, e = os_[0].shape
    tm = 512

    def body(o0, o1, o2, l0, l1, l2, out_ref):
        al = _group_alpha((l0, l1, l2))
        out_ref[...] = al[0] * o0[...] + al[1] * o1[...] + al[2] * o2[...]

    blk = lambda w: pl.BlockSpec((None, tm, w), lambda h, i: (h, i, 0))
    return pl.pallas_call(body, grid=(h_n, s_n // tm), in_specs=[blk(e)] * 3 + [blk(1)] * 3, out_specs=blk(e),
                          out_shape=jax.ShapeDtypeStruct((h_n, s_n, e), F32), compiler_params=_cparams(),
                          name=name)(*os_, *ls_)


def _dil_mix_bwd(do, os_, ls_, *, name):
    h_n, s_n, e = do.shape
    tm = 512

    def body(do_ref, o0, o1, o2, l0, l1, l2, d0, d1, d2, t0, t1, t2):
        al = _group_alpha((l0, l1, l2))
        do_v = do_ref[...]
        mix = al[0] * o0[...] + al[1] * o1[...] + al[2] * o2[...]
        dbar = jnp.sum(do_v * mix, axis=-1, keepdims=True)
        for a_g, d_ref, t_ref in zip(al, (d0, d1, d2), (t0, t1, t2)):
            d_ref[...] = a_g * do_v
            t_ref[...] = a_g * dbar

    blk = lambda w: pl.BlockSpec((None, tm, w), lambda h, i: (h, i, 0))
    sd_e = jax.ShapeDtypeStruct((h_n, s_n, e), F32)
    sd_1 = jax.ShapeDtypeStruct((h_n, s_n, 1), F32)
    outs = pl.pallas_call(body, grid=(h_n, s_n // tm), in_specs=[blk(e)] * 4 + [blk(1)] * 3,
                          out_specs=[blk(e)] * 3 + [blk(1)] * 3, out_shape=[sd_e] * 3 + [sd_1] * 3,
                          compiler_params=_cparams(), name=name)(do, *os_, *ls_)
    return outs[:3], outs[3:]


def _bias_grad(ds, bucket, *, name):
    h_n = ds.shape[0]

    def body(ds_ref, bk_ref, o_ref):
        ds_v = ds_ref[...]
        bk = bk_ref[...]
        lane = lax.broadcasted_iota(jnp.int32, (1, N_BUCKETS), 1)
        acc = jnp.zeros((1, N_BUCKETS), F32)
        for b in range(N_BUCKETS):
            tot = jnp.sum(jnp.sum(jnp.where(bk == b, ds_v, 0.0), axis=1, keepdims=True), axis=0, keepdims=True)
            acc = acc + jnp.where(lane == b, tot, 0.0)
        o_ref[...] = acc

    return pl.pallas_call(
        body, grid=(h_n,),
        in_specs=[pl.BlockSpec((None, DIL_BLOCK, 2 * DIL_BLOCK), lambda h: (h, 0, 0)),
                  pl.BlockSpec((DIL_BLOCK, 2 * DIL_BLOCK), lambda h: (0, 0))],
        out_specs=pl.BlockSpec((None, 1, N_BUCKETS), lambda h: (h, 0, 0)),
        out_shape=jax.ShapeDtypeStruct((h_n, 1, N_BUCKETS), F32), compiler_params=_cparams(), name=name)(ds, bucket)


def _bias_table(rb, bucket, *, name):
    h_n = rb.shape[0]

    def body(rb_ref, bk_ref, o_ref):
        bk = bk_ref[...]
        row = rb_ref[...]
        acc = jnp.zeros(bk.shape, F32)
        for b in range(N_BUCKETS):
            acc = jnp.where(bk == b, row[:, b:b + 1], acc)
        o_ref[...] = acc

    return pl.pallas_call(
        body, grid=(h_n,),
        in_specs=[pl.BlockSpec((None, 1, N_BUCKETS), lambda h: (h, 0, 0)),
                  pl.BlockSpec((DIL_BLOCK, 2 * DIL_BLOCK), lambda h: (0, 0))],
        out_specs=pl.BlockSpec((None, DIL_BLOCK, 2 * DIL_BLOCK), lambda h: (h, 0, 0)),
        out_shape=jax.ShapeDtypeStruct((h_n, DIL_BLOCK, 2 * DIL_BLOCK), F32), compiler_params=_cparams(),
        name=name)(rb, bucket)


def _row_tile(rows, cols, budget=2 << 20):
    if rows * cols * 4 <= budget or rows % 8:
        return rows
    best = 8
    for t in range(8, rows + 1, 8):
        if rows % t == 0 and t * cols * 4 <= budget:
            best = t
    return best


def _adamw(w, g, m, v, *, name):
    shape = w.shape
    cols = shape[-1]
    rows = math.prod(shape[:-1]) if len(shape) > 1 else 1
    to2 = lambda t: t.reshape(rows, cols)
    tr = _row_tile(rows, cols)
    c1 = 1.0 / (1.0 - ADAM_B1 ** ADAM_STEP)
    c2 = 1.0 / (1.0 - ADAM_B2 ** ADAM_STEP)

    def body(w_ref, g_ref, m_ref, v_ref, d_ref, nm_ref, nv_ref):
        g_v = g_ref[...]
        nm = ADAM_B1 * m_ref[...] + (1.0 - ADAM_B1) * g_v
        nv = ADAM_B2 * v_ref[...] + (1.0 - ADAM_B2) * (g_v * g_v)
        m_hat = nm * c1
        v_hat = nv * c2
        d_ref[...] = -ADAM_LR * (m_hat / (jnp.sqrt(v_hat) + ADAM_EPS) + ADAM_WD * w_ref[...])
        nm_ref[...] = nm
        nv_ref[...] = nv

    blk = pl.BlockSpec((tr, cols), lambda i: (i, 0))
    sd = jax.ShapeDtypeStruct((rows, cols), F32)
    outs = pl.pallas_call(body, grid=(rows // tr,), in_specs=[blk] * 4, out_specs=[blk] * 3, out_shape=[sd] * 3,
                          compiler_params=_cparams(), name=name)(to2(w), to2(g), to2(m), to2(v))
    return tuple(t.reshape(shape) for t in outs)


def _add_half(unit, got, half_idx, *, name):
    rest = unit.shape[2:]
    c = rest[-1]
    r = math.prod(rest[:-1])
    tr = _row_tile(r, c)

    def body(idx_ref, u_ref, g_ref, o_ref, w_ref):
        tot = u_ref[...] + g_ref[...].astype(F32)
        o_ref[...] = tot
        w_ref[...] = tot.astype(BF16)

    blk = pl.BlockSpec((None, tr, c), lambda s, i, idx: (s, i, 0))
    grid_spec = pltpu.PrefetchScalarGridSpec(
        num_scalar_prefetch=1, grid=(N_SHARD, r // tr),
        in_specs=[pl.BlockSpec((None, None, tr, c), lambda s, i, idx: (idx[0], s, i, 0)), blk],
        out_specs=[blk, blk])
    out, wire = pl.pallas_call(
        body, grid_spec=grid_spec,
        out_shape=[jax.ShapeDtypeStruct((N_SHARD, r, c), F32), jax.ShapeDtypeStruct((N_SHARD, r, c), BF16)],
        compiler_params=_cparams(), name=name)(half_idx, unit.reshape(2, N_SHARD, r, c), got.reshape(N_SHARD, r, c))
    return out.reshape((N_SHARD,) + rest), wire.reshape((N_SHARD,) + rest)


def _add_shards(part, got, shard_idx, *, name):
    rest = part.shape[1:]
    c = rest[-1]
    r = math.prod(rest[:-1])
    tr = _row_tile(r, c)

    def body(idx_ref, p_ref, g_ref, o_ref):
        acc = p_ref[...]
        for k in range(3):
            acc = acc + g_ref[k].astype(F32)
        o_ref[...] = acc

    grid_spec = pltpu.PrefetchScalarGridSpec(
        num_scalar_prefetch=1, grid=(r // tr,),
        in_specs=[pl.BlockSpec((None, tr, c), lambda i, idx: (idx[0], i, 0)),
                  pl.BlockSpec((3, tr, c), lambda i, idx: (0, i, 0))],
        out_specs=pl.BlockSpec((tr, c), lambda i, idx: (i, 0)))
    out = pl.pallas_call(body, grid_spec=grid_spec, out_shape=jax.ShapeDtypeStruct((r, c), F32),
                         compiler_params=_cparams(), name=name)(
        shard_idx, part.reshape(N_SHARD, r, c), got.reshape(3, r, c))
    return out.reshape(rest)


def _sum_devices(x, n_dev, *, name):
    rows = x.shape[0] // n_dev

    def body(x_ref, o_ref):
        acc = x_ref[0:rows, :]
        for d in range(1, n_dev):
            acc = acc + x_ref[d * rows:(d + 1) * rows, :]
        o_ref[...] = acc

    return pl.pallas_call(body, out_shape=jax.ShapeDtypeStruct((rows, x.shape[1]), F32), name=name)(x)


def _my_pos():
    return lax.axis_index("x"), lax.axis_index("y"), lax.axis_index("c")


def _all_gather(x_blk, *, name, in_vmem):
    m_per, n = x_blk.shape

    def body(x_ref, out_ref, send_sems, recv_sems, local_sem):
        x, y, c = _my_pos()
        me, sibling = (x, y, c), (x, y, 1 - c)
        chips = [(1 - x, y), (x, 1 - y), (1 - x, 1 - y)]

        def rows(px, py, pc):
            return out_ref.at[pl.ds((4 * px + 2 * py + pc) * m_per, m_per), :]

        def copy(k, block, to, src=None):
            return pltpu.make_async_remote_copy(
                src_ref=rows(*block) if src is None else src, dst_ref=rows(*block),
                send_sem=send_sems.at[k], recv_sem=recv_sems.at[k], device_id=to, device_id_type=MESH)

        mine = pltpu.make_async_copy(x_ref, rows(*me), local_sem)
        mine.start()
        first = [copy(0, me, sibling, src=x_ref)]
        first += [copy(1 + j, me, (*chip, c), src=x_ref) for j, chip in enumerate(chips)]
        for cp in first:
            cp.start()
        passed = [copy(4 + j, (*chip, c), sibling) for j, chip in enumerate(chips)]
        for j, chip in enumerate(chips):
            copy(1 + j, (*chip, c), me).wait_recv()
            passed[j].start()
        copy(0, sibling, me).wait_recv()
        for j, chip in enumerate(chips):
            copy(4 + j, (*chip, 1 - c), me).wait_recv()
        for cp in first + passed:
            cp.wait_send()
        mine.wait()

    space = pltpu.VMEM if in_vmem else pl.ANY
    return pl.pallas_call(
        body, out_shape=jax.ShapeDtypeStruct((8 * m_per, n), x_blk.dtype),
        in_specs=[pl.BlockSpec(memory_space=space)], out_specs=pl.BlockSpec(memory_space=space),
        scratch_shapes=[pltpu.SemaphoreType.DMA((7,)), pltpu.SemaphoreType.DMA((7,)), pltpu.SemaphoreType.DMA],
        name=name)(x_blk)


_HBM = pl.BlockSpec(memory_space=pl.ANY)


def _gather_weights(fams, *, name):
    n = len(fams)

    def body(*refs):
        ins, outs = refs[:n], refs[n:2 * n]
        send_sems, recv_sems = refs[2 * n:]
        x, y, c = _my_pos()
        me, sibling = (x, y, c), (x, y, 1 - c)
        chips = [(1 - x, y), (x, 1 - y), (1 - x, 1 - y)]

        def copy(f, k, block, to, src=None):
            px, py, pc = block
            dst = outs[f].at[2 * px + py, pc]
            return pltpu.make_async_remote_copy(
                src_ref=dst if src is None else src, dst_ref=dst, send_sem=send_sems.at[7 * f + k],
                recv_sem=recv_sems.at[7 * f + k], device_id=to, device_id_type=MESH)

        first, passed = [], []
        for f in range(n):
            src = ins[f].at[c]
            first.append(copy(f, 0, me, sibling, src=src))
            first += [copy(f, 1 + j, me, (*chip, c), src=src) for j, chip in enumerate(chips)]
        for cp in first:
            cp.start()
        for j, chip in enumerate(chips):
            for f in range(n):
                copy(f, 1 + j, (*chip, c), me).wait_recv()
                passed.append(copy(f, 4 + j, (*chip, c), sibling))
                passed[-1].start()
        for f in range(n):
            copy(f, 0, sibling, me).wait_recv()
        for j, chip in enumerate(chips):
            for f in range(n):
                copy(f, 4 + j, (*chip, 1 - c), me).wait_recv()
        for cp in first + passed:
            cp.wait_send()

    outs = pl.pallas_call(
        body, out_shape=[jax.ShapeDtypeStruct((N_SHARD,) + t.shape, t.dtype) for t in fams],
        in_specs=[_HBM] * n, out_specs=[_HBM] * n,
        scratch_shapes=[pltpu.SemaphoreType.DMA((7 * n,)), pltpu.SemaphoreType.DMA((7 * n,))], name=name)(*fams)
    return [_place_own(o, t) for o, t in zip(outs, fams)]


def _swap_halves(units, *, name):
    n = len(units)

    def body(*refs):
        ins, outs = refs[:n], refs[n:2 * n]
        send_sems, recv_sems = refs[2 * n:]
        x, y, c = _my_pos()
        cps = [pltpu.make_async_remote_copy(src_ref=ins[f].at[1 - c], dst_ref=outs[f], send_sem=send_sems.at[f],
                                            recv_sem=recv_sems.at[f], device_id=(x, y, 1 - c), device_id_type=MESH)
               for f in range(n)]
        for cp in cps:
            cp.start()
        for cp in cps:
            cp.wait()

    return pl.pallas_call(
        body, out_shape=[jax.ShapeDtypeStruct(t.shape[1:], t.dtype) for t in units],
        in_specs=[_HBM] * n, out_specs=[_HBM] * n,
        scratch_shapes=[pltpu.SemaphoreType.DMA((n,)), pltpu.SemaphoreType.DMA((n,))], name=name)(*units)


def _send_to_chips(parts, *, name):
    n = len(parts)

    def body(*refs):
        ins, outs = refs[:n], refs[n:2 * n]
        send_sems, recv_sems = refs[2 * n:]
        x, y, c = _my_pos()
        chips = [(1 - x, y), (x, 1 - y), (1 - x, 1 - y)]
        cps = [pltpu.make_async_remote_copy(src_ref=ins[f].at[2 * cx + cy], dst_ref=outs[f].at[k],
                                            send_sem=send_sems.at[3 * f + k], recv_sem=recv_sems.at[3 * f + k],
                                            device_id=(cx, cy, c), device_id_type=MESH)
               for f in range(n) for k, (cx, cy) in enumerate(chips)]
        for cp in cps:
            cp.start()
        for cp in cps:
            cp.wait()

    return pl.pallas_call(
        body, out_shape=[jax.ShapeDtypeStruct((3,) + t.shape[1:], t.dtype) for t in parts],
        in_specs=[_HBM] * n, out_specs=[_HBM] * n,
        scratch_shapes=[pltpu.SemaphoreType.DMA((3 * n,)), pltpu.SemaphoreType.DMA((3 * n,))], name=name)(*parts)


def _pair_gather(halves, *, name):
    n = len(halves)

    def body(*refs):
        ins, outs = refs[:n], refs[n:2 * n]
        send_sems, recv_sems = refs[2 * n:]
        x, y, c = _my_pos()
        cps = [pltpu.make_async_remote_copy(src_ref=ins[f], dst_ref=outs[f].at[c], send_sem=send_sems.at[f],
                                            recv_sem=recv_sems.at[f], device_id=(x, y, 1 - c), device_id_type=MESH)
               for f in range(n)]
        for cp in cps:
            cp.start()
        for f in range(n):
            pltpu.make_async_remote_copy(src_ref=ins[f], dst_ref=outs[f].at[1 - c], send_sem=send_sems.at[f],
                                         recv_sem=recv_sems.at[f], device_id=(x, y, 1 - c),
                                         device_id_type=MESH).wait_recv()
        for cp in cps:
            cp.wait_send()

    outs = pl.pallas_call(
        body, out_shape=[jax.ShapeDtypeStruct((2,) + t.shape, t.dtype) for t in halves],
        in_specs=[_HBM] * n, out_specs=[_HBM] * n,
        scratch_shapes=[pltpu.SemaphoreType.DMA((n,)), pltpu.SemaphoreType.DMA((n,))], name=name)(*halves)
    c = lax.axis_index("c")
    return [lax.dynamic_update_index_in_dim(o, t, c, 0) for o, t in zip(outs, halves)]


_HBM_ONLY = pl.BlockSpec(memory_space=pltpu.HBM)
_SEMS = pl.BlockSpec(memory_space=pltpu.SEMAPHORE)
_EFFECT = pltpu.SideEffectType.DATAFLOW_SIDE_EFFECTING


def _copies_start(srcs, lands, plan, n_copies, *, name):
    n, m = len(srcs), len(lands)

    def body(*refs):
        src_refs, land_refs = refs[:n], refs[n:n + m]
        send_sems, recv_sems, token = refs[n + m], refs[n + m + 1], refs[-1]
        for k, (src, dst, peer) in enumerate(plan(src_refs, land_refs)):
            pltpu.make_async_remote_copy(src_ref=src, dst_ref=dst, send_sem=send_sems.at[k], recv_sem=recv_sems.at[k],
                                         device_id=peer, device_id_type=MESH).start()
        token[...] = jnp.zeros_like(token)

    bufs = [pltpu.with_memory_space_constraint(t, pltpu.HBM) for t in (*srcs, *lands)]
    outs = pl.pallas_call(
        body, name=name,
        out_shape=(pltpu.SemaphoreType.DMA((n_copies,)), pltpu.SemaphoreType.DMA((n_copies,)),
                   *[pltpu.HBM(t.shape, t.dtype) for t in bufs], jax.ShapeDtypeStruct((8, 128), F32)),
        in_specs=[_HBM_ONLY] * (n + m),
        out_specs=(_SEMS, _SEMS, *[_HBM_ONLY] * (n + m), pl.BlockSpec(memory_space=pltpu.VMEM)),
        input_output_aliases={k: 2 + k for k in range(n + m)},
        compiler_params=pltpu.CompilerParams(has_side_effects=_EFFECT))(*bufs)
    return outs[0], outs[1], list(outs[2:2 + n + m]), outs[-1]


def _copies_wait(send_sems, recv_sems, thru, n_src, plan, after, *, name):
    nm = len(thru)

    def body(*refs):
        t_refs, send, recv = refs[:nm], refs[nm], refs[nm + 1]
        for k, (src, dst, peer) in enumerate(plan(t_refs[:n_src], t_refs[n_src:])):
            cp = pltpu.make_async_remote_copy(src_ref=src, dst_ref=dst, send_sem=send.at[k], recv_sem=recv.at[k],
                                              device_id=peer, device_id_type=MESH)
            cp.wait_send()
            cp.wait_recv()

    outs = pl.pallas_call(
        body, name=name, out_shape=tuple(pltpu.HBM(t.shape, t.dtype) for t in thru),
        in_specs=[_HBM_ONLY] * nm + [_SEMS, _SEMS, pl.BlockSpec(memory_space=pl.ANY)],
        out_specs=tuple([_HBM_ONLY] * nm), input_output_aliases={k: k for k in range(nm)},
        compiler_params=pltpu.CompilerParams(has_side_effects=_EFFECT))(*thru, send_sems, recv_sems, after)
    return list(outs)


_RELATIONS = [(dx, dy, dc) for dx in (0, 1) for dy in (0, 1) for dc in (0, 1)][1:]


def _gather_plan(src_refs, land_refs):
    x, y, c = _my_pos()
    flip = lambda v, d: 1 - v if d else v
    return [(s_ref.at[c], l_ref.at[2 * x + y, c], (flip(x, dx), flip(y, dy), flip(c, dc)))
            for s_ref, l_ref in zip(src_refs, land_refs) for dx, dy, dc in _RELATIONS]


def _sibling_plan(src_refs, land_refs):
    x, y, c = _my_pos()
    return [(s_ref.at[1 - c], l_ref, (x, y, 1 - c)) for s_ref, l_ref in zip(src_refs, land_refs)]


def _chips_plan(src_refs, land_refs):
    x, y, c = _my_pos()
    chips = [(1 - x, y), (x, 1 - y), (1 - x, 1 - y)]
    return [(s_ref.at[2 * cx + cy], l_ref.at[k], (cx, cy, c))
            for s_ref, l_ref in zip(src_refs, land_refs) for k, (cx, cy) in enumerate(chips)]


def _place_own(gathered, fam):
    x, y, c = _my_pos()
    own = lax.dynamic_index_in_dim(fam, c, 0, keepdims=True)[None]
    return lax.dynamic_update_slice(gathered, own, (2 * x + y, c) + (0,) * (fam.ndim - 1))


def _to_heads(t, width):
    return t.reshape(t.shape[0], HEADS, width).transpose(1, 0, 2)


def _from_heads(t):
    return t.transpose(1, 0, 2).reshape(t.shape[1], -1)


def _t5_bucket(dist):
    max_exact = N_BUCKETS // 2
    d = jnp.maximum(dist, 1).astype(F32)
    large = max_exact + (jnp.log(d / max_exact) / math.log(MAX_DISTANCE / max_exact)
                         * (N_BUCKETS - max_exact)).astype(jnp.int32)
    large = jnp.minimum(large, N_BUCKETS - 1)
    return jnp.where(dist < max_exact, dist, large)


def _bucket_map(dilation):
    iq = jnp.arange(DIL_BLOCK)[:, None]
    ik = jnp.arange(2 * DIL_BLOCK)[None, :]
    rel = DIL_BLOCK + iq - ik
    return _t5_bucket(jnp.maximum(rel, 0) * dilation).astype(jnp.int32)


def _q_perm(w):
    w3 = w.reshape(w.shape[0], HEADS, QK_NOPE + QK_ROPE)
    return jnp.concatenate([w3[:, :, :QK_NOPE].reshape(w.shape[0], -1),
                            w3[:, :, QK_NOPE:QK_NOPE + HALF_ROPE].reshape(w.shape[0], -1),
                            w3[:, :, QK_NOPE + HALF_ROPE:].reshape(w.shape[0], -1)], axis=1)


def _q_unperm(w):
    n0, n1 = HEADS * QK_NOPE, HEADS * HALF_ROPE
    r = w.shape[0]
    return jnp.concatenate([w[:, :n0].reshape(r, HEADS, QK_NOPE), w[:, n0:n0 + n1].reshape(r, HEADS, HALF_ROPE),
                            w[:, n0 + n1:].reshape(r, HEADS, HALF_ROPE)], axis=2).reshape(r, -1)


def _kv_perm(w):
    w3 = w.reshape(w.shape[0], HEADS, QK_NOPE + V_HEAD)
    return jnp.concatenate([w3[:, :, :QK_NOPE].reshape(w.shape[0], -1), w3[:, :, QK_NOPE:].reshape(w.shape[0], -1)],
                           axis=1)


def _kv_unperm(w):
    n0 = HEADS * QK_NOPE
    r = w.shape[0]
    return jnp.concatenate([w[:, :n0].reshape(r, HEADS, QK_NOPE), w[:, n0:].reshape(r, HEADS, V_HEAD)],
                           axis=2).reshape(r, -1)


def _row(v):
    return v.reshape(1, -1)


def kernel(x, c, norm_pre, norm_post, w_mod, b_mod, ffn_w_gate, ffn_w_up, ffn_w_down, mla_w_in, mla_q_norm, mla_w_q_up, mla_kv_norm, mla_w_kv_up, mla_w_o, dil_w_in, dil_w_o, rel_bias, loss_target, m_norm_pre, m_norm_post, m_w_mod, m_b_mod, m_ffn_w_gate, m_ffn_w_up, m_ffn_w_down, m_mla_w_in, m_mla_q_norm, m_mla_w_q_up, m_mla_kv_norm, m_mla_w_kv_up, m_mla_w_o, m_dil_w_in, m_dil_w_o, m_rel_bias, v_norm_pre, v_norm_post, v_w_mod, v_b_mod, v_ffn_w_gate, v_ffn_w_up, v_ffn_w_down, v_mla_w_in, v_mla_q_norm, v_mla_w_q_up, v_mla_kv_norm, v_mla_w_kv_up, v_mla_w_o, v_dil_w_in, v_dil_w_o, v_rel_bias):
    given = dict(locals())
    ix, iy, ic = _my_pos()
    shard_id = 2 * ix + iy
    dev_id = 4 * ix + 2 * iy + ic
    x2 = x[0]
    target = loss_target[0]
    half_idx = jnp.reshape(ic, (1,)).astype(jnp.int32)
    shard_idx = jnp.reshape(shard_id, (1,)).astype(jnp.int32)

    blk = jnp.zeros((8, D_MODEL), F32)
    blk = blk.at[0].set(c[0])
    blk = blk.at[1:3].set(jnp.pad(norm_pre.reshape(-1), (0, 512)).reshape(2, D_MODEL))
    blk = blk.at[3:5].set(jnp.pad(norm_post.reshape(-1), (0, 512)).reshape(2, D_MODEL))
    got = _all_gather(blk, name="ag_c_norms", in_vmem=True).reshape(N_SHARD, 2, 8, D_MODEL)
    c_all = got[:, :, 0, :].reshape(8, D_MODEL)

    def full_norm(lo):
        t = got[:, 0, lo:lo + 2, :].reshape(N_SHARD, 2 * D_MODEL)[:, :1536].reshape(N_SHARD, 2, 3, 256)
        return t.transpose(1, 2, 0, 3).reshape(2, 3, D_MODEL)

    pre_full, post_full = full_norm(1), full_norm(3)

    silu_c = _silu_bf16(c_all, name="silu_c")
    b_cols = lax.dynamic_slice_in_dim(b_mod, shard_id * 2304, 2304, axis=1).reshape(2, 1, 2304)
    mod_part = _mm(silu_c, w_mod, bias=b_cols, name="mod_mm", tn_cap=768)
    mod_all = _all_gather(mod_part.reshape(16, 2304), name="ag_mod", in_vmem=True)
    mod_all = mod_all.reshape(N_SHARD, 2, 2, 8, 2304)[:, 0]
    mod_mine = lax.dynamic_index_in_dim(mod_all, dev_id, axis=2, keepdims=False)
    mod = mod_mine.transpose(1, 0, 2).reshape(2, 9, D_MODEL)

    bf = lambda t: t.astype(BF16)
    ffn_fam = lambda i, h: [bf(jnp.stack([ffn_w_gate[i, h], ffn_w_up[i, h]])),
                            bf(ffn_w_down[i, h].reshape(2, F_SHARD // 2, D_MODEL))]
    mla_fam = [bf(mla_w_in.reshape(2, 128, -1)), bf(mla_w_q_up.reshape(2, 192, -1)),
               bf(mla_w_kv_up.reshape(2, 128, -1)), bf(mla_w_o.reshape(2, 128, D_MODEL))]
    dil_fam = [bf(dil_w_in.reshape(2, 512, -1)), bf(dil_w_o.reshape(2, 128, D_MODEL))]
    later_fams = [ffn_fam(0, 1), ffn_fam(1, 0) + dil_fam, ffn_fam(1, 1)]
    full, later_fams, mod = lax.optimization_barrier(
        (_gather_weights(ffn_fam(0, 0) + mla_fam, name="ag_weights_first"), later_fams, mod))

    def gather_later(fams, tag):
        lands = [lax.empty((N_SHARD,) + t.shape, t.dtype) for t in fams]
        send, recv, thru, token = _copies_start(fams, lands, _gather_plan, 7 * len(fams), name=f"ag_start_{tag}")
        return dict(send=send, recv=recv, thru=thru, token=token, n=len(fams), tag=tag)

    def arrive(st, after):
        thru = _copies_wait(st['send'], st['recv'], st['thru'], st['n'], _gather_plan, after,
                            name=f"ag_wait_{st['tag']}")
        return [_place_own(o, t) for t, o in zip(thru[:st['n']], thru[st['n']:])]

    flight_a = gather_later(later_fams[0], "l0s2")
    _, next_fams = lax.optimization_barrier((flight_a['token'], later_fams[1]))
    flight_b = gather_later(next_fams, "l1s01")
    as_ffn = lambda w_gu, w_dn: (w_gu, w_dn.reshape(N_SHARD, F_SHARD, D_MODEL))
    ffn_w = {(0, 0): as_ffn(full[0], full[1])}
    w_in = full[2].reshape(D_MODEL, -1)
    wq_p = _q_perm(full[3].reshape(N_SHARD, Q_LORA, -1).transpose(1, 0, 2).reshape(Q_LORA, -1))
    wkv_p = _kv_perm(full[4].reshape(N_SHARD, KV_LORA, -1).transpose(1, 0, 2).reshape(KV_LORA, -1))
    w_mo = full[5].reshape(D_MODEL, D_MODEL)
    dil_w = {}

    pos = jnp.arange(SEQ, dtype=F32)
    freqs = ROPE_THETA ** (-jnp.arange(HALF_ROPE, dtype=F32) / HALF_ROPE)
    ang = pos[:, None] * freqs[None, :]
    cos_k, sin_k = jnp.cos(ang), jnp.sin(ang)
    cos_q, sin_q = jnp.tile(cos_k, (1, HEADS)), jnp.tile(sin_k, (1, HEADS))

    buckets = [_bucket_map(d) for _, d in DIL_GROUPS]
    biases = [_bias_table(rel_bias[:, g * HEADS:(g + 1) * HEADS].T.reshape(HEADS, 1, N_BUCKETS), bk,
                          name=f"dil_bias_table_g{g}") for g, bk in enumerate(buckets)]

    def sub_params(i, sub):
        return dict(pg=_row(pre_full[i, sub]), qg=_row(post_full[i, sub]), sh=_row(mod[i, 3 * sub]),
                    sc=_row(mod[i, 3 * sub + 1]), gate=_row(mod[i, 3 * sub + 2]))

    def ffn_fwd(xin, i, h, sub, tie=None):
        p = sub_params(i, sub)
        if tie is not None:
            p['sh'] = p['sh'] + tie
        tag = f"l{i}s{sub}"
        w_gu, w_dn = ffn_w[i, h]
        hn = _pre_fwd(xin, p['pg'], p['sc'], p['sh'], name=f"pre_fwd_{tag}")
        gu, a = _ffn_up(hn, w_gu, name=f"ffn_up_{tag}")
        f, out = _ffn_down(a, w_dn, xin, p['qg'], p['gate'], FFN_RES, name=f"ffn_down_{tag}")
        return out, dict(x=xin, hn=hn, gu=gu, a=a, f=f, p=p, i=i, h=h, tag=tag)

    def mla_fwd(xin, i, sub):
        p = sub_params(i, sub)
        tag = f"l{i}s{sub}"
        hn = _pre_fwd(xin, p['pg'], p['sc'], p['sh'], name=f"pre_fwd_{tag}")
        lat = _mm(hn, w_in, name="mla_lat")
        cq, ckv = lat[:, :Q_LORA], lat[:, Q_LORA:Q_LORA + KV_LORA]
        k1, k2 = lat[:, Q_LORA + KV_LORA:Q_LORA + KV_LORA + HALF_ROPE], lat[:, Q_LORA + KV_LORA + HALF_ROPE:]
        cqn = _rms_fwd(cq, mla_q_norm, name="mla_qnorm")
        ckvn = _rms_fwd(ckv, mla_kv_norm, name="mla_kvnorm")
        qp = _mm(cqn, wq_p, name="mla_q_up")
        kvp = _mm(ckvn, wkv_p, name="mla_kv_up")
        n0, n1 = HEADS * QK_NOPE, HEADS * HALF_ROPE
        qr1, qr2 = _rope(qp[:, n0:n0 + n1], qp[:, n0 + n1:], cos_q, sin_q, name="rope_q")
        kr1, kr2 = _rope(k1, k2, cos_k, sin_k, name="rope_k")
        q = jnp.concatenate([qp[:, :n0].reshape(SEQ, HEADS, QK_NOPE), qr1.reshape(SEQ, HEADS, HALF_ROPE),
                             qr2.reshape(SEQ, HEADS, HALF_ROPE)], axis=2).transpose(1, 0, 2).astype(BF16)
        kr = jnp.broadcast_to(jnp.concatenate([kr1, kr2], axis=1)[:, None, :], (SEQ, HEADS, QK_ROPE))
        k = jnp.concatenate([kvp[:, :n0].reshape(SEQ, HEADS, QK_NOPE), kr], axis=2).transpose(1, 0, 2).astype(BF16)
        v = _to_heads(kvp[:, n0:], V_HEAD).astype(BF16)
        o, lse = _mla_attn_fwd(q, k, v, name="mla_attn_fwd")
        o_flat = _from_heads(o).astype(BF16)
        f = _mm(o_flat, w_mo, name="mla_out")
        out = _post_fwd(f, xin, p['qg'], p['gate'], 1.0, name=f"post_fwd_{tag}")
        return out, dict(x=xin, hn=hn, cq=cq, ckv=ckv, cqn=cqn, ckvn=ckvn, q=q, k=k, v=v, o=o, lse=lse,
                         o_flat=o_flat, f=f, p=p, tag=tag)

    def dil_fwd(xin, i, sub):
        p = sub_params(i, sub)
        tag = f"l{i}s{sub}"
        hn = _pre_fwd(xin, p['pg'], p['sc'], p['sh'], name=f"pre_fwd_{tag}")
        heads = _proj_heads(hn, dil_w['in'], name="dil_proj")
        outs, lses = [], []
        for g, (window, d) in enumerate(DIL_GROUPS):
            o, lse = _dil_attn_fwd(heads, biases[g], g, d, name=f"dil_attn_fwd_g{g}")
            outs.append(o)
            lses.append(lse)
        mix = _dil_mix_fwd(outs, lses, name="dil_mix_fwd")
        o_flat = _from_heads(mix).astype(BF16)
        f = _mm(o_flat, dil_w['out'], name="dil_out")
        out = _post_fwd(f, xin, p['qg'], p['gate'], 1.0, name=f"post_fwd_{tag}")
        return out, dict(x=xin, hn=hn, heads=heads, outs=outs, lses=lses, o_flat=o_flat, f=f, p=p, tag=tag)

    saved = [None] * 6
    xs, saved[0] = ffn_fwd(x2, 0, 0, 0, tie=flight_a['token'][0, 0] + flight_b['token'][0, 0])
    xs, saved[1] = mla_fwd(xs, 0, 1)
    ffn_w[0, 1] = as_ffn(*arrive(flight_a, xs))
    xs, saved[2] = ffn_fwd(xs, 0, 1, 2)
    got, last_fams = lax.optimization_barrier((arrive(flight_b, xs), later_fams[2]))
    ffn_w[1, 0] = as_ffn(got[0], got[1])
    dil_w['in'], dil_w['out'] = got[2].reshape(N_SHARD, D_MODEL, -1), got[3].reshape(D_MODEL, D_MODEL)
    in_flight = gather_later(last_fams, "l1s2")
    xs, saved[3] = ffn_fwd(xs, 1, 0, 0, tie=in_flight['token'][0, 0])
    xs, saved[4] = dil_fwd(xs, 1, 1)
    ffn_w[1, 1] = as_ffn(*arrive(in_flight, xs))
    xs, saved[5] = ffn_fwd(xs, 1, 1, 2)

    dx, loss_part = _loss(xs, target, name="loss")

    dmod = [[None] * 9 for _ in range(2)]
    dpre = [[None] * 3 for _ in range(2)]
    dpost = [[None] * 3 for _ in range(2)]
    ffn_units = {}
    row_unit = lambda g, r, j: ((r % 2, r // 2), 0, j)

    def close_sub(dhn, dout, sv, i, sub, res_dgate, res_dqg):
        p = sv['p']
        dxs, dsh, dsc, dpg = _pre_bwd(dhn, sv['x'], dout, p['pg'], p['sc'], name=f"pre_bwd_{sv['tag']}")
        dmod[i][3 * sub], dmod[i][3 * sub + 1], dmod[i][3 * sub + 2] = dsh, dsc, res_dgate
        dpre[i][sub], dpost[i][sub] = dpg, res_dqg
        return dxs

    def ffn_bwd(dout, sv, sub, tie=0.0):
        i, h, p, tag = sv['i'], sv['h'], sv['p'], sv['tag']
        w_gu, w_dn = ffn_w[i, h]
        df, dgate, dqg = _post_bwd(dout, sv['f'], p['qg'] + tie, p['gate'], FFN_RES, name=f"post_bwd_{tag}")
        u_dn = _mm(sv['a'], df, ta=True, tn_cap=D_MODEL // 2, out_shape=(2, N_SHARD, F_SHARD, D_MODEL // 2),
                   out_sel=lambda g, r, j: ((j, g), r, 0), name=f"ffn_dwd_{tag}")
        dgu = _ffn_dgu(df, w_dn, sv['gu'], name=f"ffn_dgu_{tag}")
        dgu = dgu.reshape(2 * N_SHARD, SEQ, F_SHARD)
        u_gu = _mm(dgu, sv['hn'], ta=True, out_shape=(2, N_SHARD, F_SHARD, D_MODEL),
                   out_sel=lambda g, r, j: ((g % 2, g // 2), r, j), name=f"ffn_dwgu_{tag}")
        ffn_units[i, h] = [u_gu, u_dn]
        dxs, dsh, dsc, dpg = _ffn_dhn(dgu, w_gu.reshape(2 * N_SHARD, D_MODEL, F_SHARD), sv['x'], dout, p['pg'], p['sc'],
                                      name=f"ffn_dhn_{tag}")
        dmod[i][3 * sub], dmod[i][3 * sub + 1], dmod[i][3 * sub + 2] = dsh, dsc, dgate
        dpre[i][sub], dpost[i][sub] = dpg, dqg
        return dxs

    def mla_bwd(dout, sv, i, sub, tie=0.0):
        p, tag = sv['p'], sv['tag']
        df, dgate, dqg = _post_bwd(dout, sv['f'], p['qg'] + tie, p['gate'], 1.0, name=f"post_bwd_{tag}")
        u_wo = _mm(sv['o_flat'], df, ta=True, tm_cap=128, out_shape=(2, N_SHARD, 128, D_MODEL), out_sel=row_unit,
                   name="mla_dwo")
        do_flat = _mm(df, w_mo, tb=True, name="mla_do")
        do = _to_heads(do_flat, V_HEAD)
        dq, dk, dv = _mla_attn_bwd(sv['q'], sv['k'], sv['v'], sv['o'], do, sv['lse'], name="mla_attn_bwd")
        dq_t = dq.transpose(1, 0, 2)
        dqr1, dqr2 = _rope(dq_t[:, :, QK_NOPE:QK_NOPE + HALF_ROPE].reshape(SEQ, -1),
                           dq_t[:, :, QK_NOPE + HALF_ROPE:].reshape(SEQ, -1), cos_q, -sin_q, name="rope_q_bwd")
        dqp = jnp.concatenate([dq_t[:, :, :QK_NOPE].reshape(SEQ, -1), dqr1, dqr2], axis=1).astype(BF16)
        dkr = _head_sum(dk[:, :, QK_NOPE:], name="mla_dkr_sum")
        dk1, dk2 = _rope(dkr[:, :HALF_ROPE], dkr[:, HALF_ROPE:], cos_k, -sin_k, name="rope_k_bwd")
        dkvp = jnp.concatenate([_from_heads(dk[:, :, :QK_NOPE]), _from_heads(dv)], axis=1).astype(BF16)
        g_wq = _q_unperm(_mm(sv['cqn'], dqp, ta=True, name="mla_dwq"))
        g_wkv = _kv_unperm(_mm(sv['ckvn'], dkvp, ta=True, name="mla_dwkv"))
        dcqn = _mm(dqp, wq_p, tb=True, name="mla_dcqn")
        dckvn = _mm(dkvp, wkv_p, tb=True, name="mla_dckvn")
        dcq, g_qn = _rms_bwd(dcqn, sv['cq'], mla_q_norm, name="mla_qnorm_bwd")
        dckv, g_kvn = _rms_bwd(dckvn, sv['ckv'], mla_kv_norm, name="mla_kvnorm_bwd")
        dlat = jnp.concatenate([dcq, dckv, dk1, dk2], axis=1).astype(BF16)
        u_win = _mm(sv['hn'], dlat, ta=True, tm_cap=128, out_shape=(2, N_SHARD, 128, dlat.shape[1]),
                    out_sel=row_unit, name="mla_dwin")
        dhn = _mm(dlat, w_in, tb=True, name="mla_dhn")
        col_unit = lambda t: (t.reshape(t.shape[0], N_SHARD, -1).transpose(1, 0, 2)
                              .reshape(N_SHARD, 2, t.shape[0] // 2, -1).transpose(1, 0, 2, 3))
        grads = dict(units=[u_win, col_unit(g_wq), col_unit(g_wkv), u_wo], q_norm=g_qn, kv_norm=g_kvn)
        return close_sub(dhn, dout, sv, i, sub, dgate, dqg), grads

    def dil_bwd(dout, sv, i, sub):
        p, tag = sv['p'], sv['tag']
        df, dgate, dqg = _post_bwd(dout, sv['f'], p['qg'], p['gate'], 1.0, name=f"post_bwd_{tag}")
        u_wo = _mm(sv['o_flat'], df, ta=True, tm_cap=128, out_shape=(2, N_SHARD, 128, D_MODEL), out_sel=row_unit,
                   name="dil_dwo")
        do = _to_heads(_mm(df, dil_w['out'], tb=True, name="dil_do"), 64)
        dos, dlts = _dil_mix_bwd(do, sv['outs'], sv['lses'], name="dil_mix_bwd")
        pieces = []
        bias_rows = []
        for g, (window, d) in enumerate(DIL_GROUPS):
            dq, dk, dv, dbias = _dil_attn_bwd(sv['heads'], biases[g], sv['lses'][g], dos[g], dlts[g], g, d,
                                              name=f"dil_attn_bwd_g{g}")
            pieces += [dq, dk, dv]
            bias_rows.append(_bias_grad(dbias, buckets[g], name=f"dil_bias_grad_g{g}")[:, 0, :])
        dheads = jnp.concatenate(pieces).astype(BF16)
        u_win = _proj_heads_dw(sv['hn'], dheads, name="dil_dwin")
        dhn = _proj_heads_dx(dheads, dil_w['in'], name="dil_dhn")
        g_bias = jnp.concatenate(bias_rows, axis=0).T
        grads = dict(units=[u_win, u_wo], rel_bias=g_bias)
        return close_sub(dhn, dout, sv, i, sub, dgate, dqg), grads

    def to_sibling(units, tag):
        n = len(units)
        send, recv, thru, token = _copies_start(units, [lax.empty(u.shape[1:], F32) for u in units], _sibling_plan, n,
                                                name=f"rs{tag}_sibling_start")
        return dict(send=send, recv=recv, thru=thru, n=n, tag=tag), token[0, 0]

    def from_sibling(st, after):
        n, tag = st['n'], st['tag']
        thru = _copies_wait(st['send'], st['recv'], st['thru'], n, _sibling_plan, after, name=f"rs{tag}_sibling_wait")
        return [_add_half(u, g, half_idx, name=f"rs{tag}_add_half_{k}") for k, (u, g) in enumerate(zip(thru[:n], thru[n:]))]

    def to_chips(parts, tag):
        n = len(parts)
        send, recv, thru, token = _copies_start([w for _, w in parts],
                                                [lax.empty((3,) + w.shape[1:], BF16) for _, w in parts], _chips_plan,
                                                3 * n, name=f"rs{tag}_chips_start")
        return dict(send=send, recv=recv, thru=thru, n=n, tag=tag, parts=parts), token[0, 0]

    def from_chips(st, after):
        n, tag = st['n'], st['tag']
        thru = _copies_wait(st['send'], st['recv'], st['thru'], n, _chips_plan, after, name=f"rs{tag}_chips_wait")
        return [_add_shards(p, g, shard_idx, name=f"rs{tag}_add_shards_{k}")
                for k, ((p, _), g) in enumerate(zip(st['parts'], thru[n:]))]

    dx = ffn_bwd(dx, saved[5], 2)
    dx, dil_g = dil_bwd(dx, saved[4], 1, 1)
    dx = ffn_bwd(dx, saved[3], 0)
    st1, tok = to_sibling([*ffn_units[1, 1], *dil_g['units'], *ffn_units[1, 0]], "1")
    dx = ffn_bwd(dx, saved[2], 2, tie=tok)
    st1, tok1 = to_chips(from_sibling(st1, dx), "1")
    st2, tok2 = to_sibling(ffn_units[0, 1], "2")
    dx, mla_g = mla_bwd(dx, saved[1], 0, 1, tie=tok1 + tok2)
    reds1 = from_chips(st1, dx)
    st2, tok = to_chips(from_sibling(st2, dx), "2")
    dx = ffn_bwd(dx, saved[0], 0, tie=tok)
    reds2 = from_chips(st2, dx)
    grad_x = dx[None]

    pad_row = lambda v: jnp.pad(v.reshape(-1), (0, (-v.size) % D_MODEL)).reshape(-1, D_MODEL)
    small = jnp.concatenate(
        [jnp.concatenate([dmod[i][r] for i in range(2) for r in range(9)], axis=0),
         jnp.concatenate([dpre[i][s] for i in range(2) for s in range(3)], axis=0),
         jnp.concatenate([dpost[i][s] for i in range(2) for s in range(3)], axis=0),
         pad_row(mla_g['q_norm']), pad_row(mla_g['kv_norm']), pad_row(dil_g['rel_bias']), pad_row(loss_part)], axis=0)
    small = jnp.pad(small, ((0, SMALL_ROWS - small.shape[0]), (0, 0)))
    small_all = _all_gather(small, name="ag_small_grads", in_vmem=True)
    small_sum = _sum_devices(small_all, 8, name="sum_small_grads")
    g_b_mod = small_sum[0:18].reshape(2, 9 * D_MODEL)
    my_cols = lambda t: lax.dynamic_slice_in_dim(t, shard_id * 256, 256, axis=2)
    g_norm_pre = my_cols(small_sum[18:24].reshape(2, 3, D_MODEL))
    g_norm_post = my_cols(small_sum[24:30].reshape(2, 3, D_MODEL))
    g_q_norm = small_sum[30, :Q_LORA].reshape(1, Q_LORA)
    g_kv_norm = small_sum[31, :KV_LORA].reshape(1, KV_LORA)
    g_rel_bias = small_sum[32:34].reshape(-1)[:N_BUCKETS * 48].reshape(N_BUCKETS, 48)
    loss = small_sum[34, 0]
    dmod_all = small_all.reshape(8, SMALL_ROWS, D_MODEL)[:, 0:18].reshape(8, 2, 9 * D_MODEL)
    dmod_cols = lax.dynamic_slice_in_dim(dmod_all, shard_id * 2304, 2304, axis=2).transpose(1, 0, 2)
    g_w_mod = _mm(silu_c, dmod_cols.astype(BF16), ta=True, tn_cap=768, name="w_mod_grad")

    units0 = [*mla_g['units'], *ffn_units[0, 0]]
    got_a = _swap_halves(units0, name="rs0_sibling")
    parts0 = [_add_half(u, g, half_idx, name=f"rs0_add_half_{k}") for k, (u, g) in enumerate(zip(units0, got_a))]
    got_b = _send_to_chips([w for _, w in parts0], name="rs0_chips")
    reds0 = [_add_shards(p, g, shard_idx, name=f"rs0_add_shards_{k}")
             for k, ((p, _), g) in enumerate(zip(parts0, got_b))]
    fin = _pair_gather(reds1 + reds2 + reds0, name="rs_pair_gather")
    ffn_fin = {(1, 1): fin[0:2], (1, 0): fin[4:6], (0, 1): fin[6:8], (0, 0): fin[12:14]}
    swap = lambda t: jnp.swapaxes(t, 2, 3)
    per_ffn = lambda pick: jnp.stack([jnp.stack([pick(*ffn_fin[i, h]) for h in range(2)]) for i in range(2)])
    reduced = dict(ffn_w_gate=swap(per_ffn(lambda gu, dn: gu[0])), ffn_w_up=swap(per_ffn(lambda gu, dn: gu[1])),
                   ffn_w_down=per_ffn(lambda gu, dn: jnp.concatenate([dn[0], dn[1]], axis=1)))
    for n, t in zip(['dil_w_in', 'dil_w_o', 'mla_w_in', 'mla_w_q_up', 'mla_w_kv_up', 'mla_w_o'], fin[2:4] + fin[8:12]):
        reduced[n] = t.reshape(given[n].shape)

    grads = dict(norm_pre=g_norm_pre, norm_post=g_norm_post, w_mod=g_w_mod, b_mod=g_b_mod, mla_q_norm=g_q_norm,
                 mla_kv_norm=g_kv_norm, rel_bias=g_rel_bias, **reduced)

    deltas, new_m, new_v = {}, {}, {}
    for n in WEIGHTS:
        view = swap if n in ('ffn_w_gate', 'ffn_w_up') else (lambda t: t)
        outs = _adamw(view(given[n]), view(grads[n]), view(given["m_" + n]), view(given["v_" + n]), name=f"adamw_{n}")
        deltas[n], new_m[n], new_v[n] = (view(t) for t in outs)
    return (loss, grad_x, *[grads[n] for n in WEIGHTS], *[deltas[n] for n in WEIGHTS],
            *[new_m[n] for n in WEIGHTS], *[new_v[n] for n in WEIGHTS])
```

```python
import math

import jax
import jax.numpy as jnp
from jax import lax
from jax.experimental import pallas as pl
from jax.experimental.pallas import tpu as pltpu

F32 = jnp.float32
BF16 = jnp.bfloat16
MESH = pl.DeviceIdType.MESH

SEQ = 2048
D_MODEL = 1024
D_FF = 2816
N_SHARD = 4
F_SHARD = D_FF // N_SHARD
EPS = 1e-6
FFN_RES = 0.5
HEADS = 16
Q_LORA, KV_LORA, QK_NOPE, QK_ROPE, V_HEAD = 384, 256, 64, 32, 64
HALF_ROPE = QK_ROPE // 2
ROPE_THETA = 10000.0
DIL_GROUPS = ((128, 1), (512, 4), (2048, 16))
DIL_BLOCK = 128
N_BUCKETS = 32
MAX_DISTANCE = 2048
ADAM_LR, ADAM_B1, ADAM_B2, ADAM_EPS, ADAM_WD, ADAM_STEP = 0.001, 0.9, 0.999, 1e-08, 0.01, 10

VMEM_LIMIT = 48 * 1024 * 1024
SMALL_ROWS = 40

WEIGHTS = ['norm_pre', 'norm_post', 'w_mod', 'b_mod', 'ffn_w_gate', 'ffn_w_up', 'ffn_w_down', 'mla_w_in',
           'mla_q_norm', 'mla_w_q_up', 'mla_kv_norm', 'mla_w_kv_up', 'mla_w_o', 'dil_w_in', 'dil_w_o', 'rel_bias']


def _cparams(**kw):
    return pltpu.CompilerParams(vmem_limit_bytes=VMEM_LIMIT, **kw)


def _pick(n, cap, mult=128):
    if n <= cap:
        return n
    best = n
    for t in range(mult, cap + 1, mult):
        if n % t == 0:
            best = t
    return best


def _mm(a, b, *, name, ta=False, tb=False, reduce_g=False, bias=None, out_dtype=F32, tm_cap=512, tn_cap=1024,
        g_n=None, b_sel=None, out_shape=None, out_sel=None, out_buf=None):
    a3 = a if a.ndim == 3 else a[None]
    ga = a3.shape[0]
    if b_sel is None:
        b_n = b if b.ndim == 3 else b[None]
        gb = b_n.shape[0]
        b_sel = (lambda g: (g,)) if gb > 1 else (lambda g: (0,))
        g_n = max(ga, gb)
    else:
        b_n = b
    k_dim, m_dim = (a3.shape[1], a3.shape[2]) if ta else (a3.shape[2], a3.shape[1])
    k2, n_dim = (b_n.shape[-1], b_n.shape[-2]) if tb else (b_n.shape[-2], b_n.shape[-1])
    assert k_dim == k2, (a.shape, b.shape)
    tm = _pick(m_dim, tm_cap, 128 if ta else 8)
    tn = _pick(n_dim, tn_cap, 128)
    mt, nt = m_dim // tm, n_dim // tn
    dims = (((0 if ta else 1,), (1 if tb else 0,)), ((), ()))

    if reduce_g:
        grid = (mt, nt, g_n)
        ids = lambda i, j, g: (g, i, j)
    else:
        grid = (g_n, mt, nt)
        ids = lambda g, i, j: (g, i, j)

    def a_map(*p):
        g, i, j = ids(*p)
        g = g if ga > 1 else 0
        return (g, 0, i) if ta else (g, i, 0)

    def b_map(*p):
        g, i, j = ids(*p)
        return (*b_sel(g), j, 0) if tb else (*b_sel(g), 0, j)

    b_lead = (None,) * (b_n.ndim - 2)
    a_spec = pl.BlockSpec((None, k_dim, tm) if ta else (None, tm, k_dim), a_map)
    b_spec = pl.BlockSpec(b_lead + ((tn, k_dim) if tb else (k_dim, tn)), b_map)
    in_specs = [a_spec, b_spec]
    operands = [a3, b_n]
    if bias is not None:
        assert not reduce_g and bias.shape == (g_n, 1, n_dim)
        in_specs.append(pl.BlockSpec((None, 1, tn), lambda g, i, j: (g, 0, j)))
        operands.append(bias)
    aliases = {}
    if out_buf is not None:
        assert tuple(out_buf.shape) == tuple(out_shape) and out_buf.dtype == out_dtype
        in_specs.append(pl.BlockSpec(memory_space=pl.ANY))
        operands.append(out_buf)
        aliases = {len(operands) - 1: 0}

    if reduce_g:
        out_spec = pl.BlockSpec((tm, tn), lambda i, j, g: (i, j))
        out_sds = jax.ShapeDtypeStruct((m_dim, n_dim), F32)
    elif out_shape is not None:
        def o_map(g, i, j):
            lead, rb, cb = out_sel(g, i, j)
            return (*lead, rb, cb)

        out_spec = pl.BlockSpec((None,) * (len(out_shape) - 2) + (tm, tn), o_map)
        out_sds = jax.ShapeDtypeStruct(tuple(out_shape), out_dtype)
    else:
        out_spec = pl.BlockSpec((None, tm, tn), lambda g, i, j: (g, i, j))
        out_sds = jax.ShapeDtypeStruct((g_n, m_dim, n_dim), out_dtype)

    def body(a_ref, b_ref, *rest):
        o_ref = rest[-1]
        r = lax.dot_general(a_ref[...].astype(BF16), b_ref[...].astype(BF16), dims, preferred_element_type=F32)
        if bias is not None:
            r = r + rest[0][...]
        if reduce_g:
            g = pl.program_id(2)

            @pl.when(g == 0)
            def _():
                o_ref[...] = r

            @pl.when(g > 0)
            def _():
                o_ref[...] += r
        else:
            o_ref[...] = r.astype(o_ref.dtype)

    out = pl.pallas_call(body, grid=grid, in_specs=in_specs, out_specs=out_spec, out_shape=out_sds,
                         input_output_aliases=aliases, compiler_params=_cparams(), name=name)(*operands)
    if not reduce_g and out_shape is None and a.ndim == 2 and b.ndim == 2:
        out = out[0]
    return out


def _rows(tm, w):
    return pl.BlockSpec((tm, w), lambda i: (i, 0))


def _vec(w):
    return pl.BlockSpec((1, w), lambda i: (0, 0))


def _rstd(v):
    return lax.rsqrt(jnp.mean(v * v, axis=-1, keepdims=True) + EPS)


def _pre_fwd(x, pg, sc, sh, *, name):
    s_n, w = x.shape
    tm = _pick(s_n, 512, 8)

    def body(x_ref, pg_ref, sc_ref, sh_ref, o_ref):
        xv = x_ref[...]
        n = (xv * _rstd(xv)) * pg_ref[...]
        o_ref[...] = (n * (1.0 + sc_ref[...]) + sh_ref[...]).astype(o_ref.dtype)

    return pl.pallas_call(body, grid=(s_n // tm,), in_specs=[_rows(tm, w), _vec(w), _vec(w), _vec(w)],
                          out_specs=_rows(tm, w), out_shape=jax.ShapeDtypeStruct((s_n, w), BF16),
                          compiler_params=_cparams(), name=name)(x, pg, sc, sh)


def _post_fwd(f, x, qg, gate, res_w, *, name):
    s_n, w = x.shape
    tm = _pick(s_n, 512, 8)

    def body(f_ref, x_ref, qg_ref, gate_ref, o_ref):
        fv = f_ref[...]
        y = (fv * _rstd(fv)) * qg_ref[...]
        o_ref[...] = x_ref[...] + (res_w * gate_ref[...]) * y

    return pl.pallas_call(body, grid=(s_n // tm,), in_specs=[_rows(tm, w), _rows(tm, w), _vec(w), _vec(w)],
                          out_specs=_rows(tm, w), out_shape=jax.ShapeDtypeStruct((s_n, w), F32),
                          compiler_params=_cparams(), name=name)(f, x, qg, gate)


def _post_bwd(dout, f, qg, gate, res_w, *, name):
    s_n, w = f.shape
    tm = _pick(s_n, 512, 8)

    def body(do_ref, f_ref, qg_ref, gate_ref, df_ref, dgate_ref, dqg_ref):
        @pl.when(pl.program_id(0) == 0)
        def _():
            dgate_ref[...] = jnp.zeros_like(dgate_ref)
            dqg_ref[...] = jnp.zeros_like(dqg_ref)

        do = do_ref[...]
        fv = f_ref[...]
        r = _rstd(fv)
        fh = fv * r
        qg_v = qg_ref[...]
        dgate_ref[...] += res_w * jnp.sum(do * (fh * qg_v), axis=0, keepdims=True)
        dy = do * (res_w * gate_ref[...])
        dqg_ref[...] += jnp.sum(dy * fh, axis=0, keepdims=True)
        dfh = dy * qg_v
        df = r * (dfh - fh * jnp.mean(dfh * fh, axis=-1, keepdims=True))
        df_ref[...] = df.astype(df_ref.dtype)

    return pl.pallas_call(
        body, grid=(s_n // tm,), in_specs=[_rows(tm, w), _rows(tm, w), _vec(w), _vec(w)],
        out_specs=[_rows(tm, w), _vec(w), _vec(w)],
        out_shape=[jax.ShapeDtypeStruct((s_n, w), BF16), jax.ShapeDtypeStruct((1, w), F32),
                   jax.ShapeDtypeStruct((1, w), F32)],
        compiler_params=_cparams(), name=name)(dout, f, qg, gate)


def _pre_bwd(dhn, x, dout, pg, sc, *, name):
    s_n, w = x.shape
    tm = _pick(s_n, 512, 8)

    def body(dhn_ref, x_ref, do_ref, pg_ref, sc_ref, dx_ref, dsh_ref, dsc_ref, dpg_ref):
        @pl.when(pl.program_id(0) == 0)
        def _():
            dsh_ref[...] = jnp.zeros_like(dsh_ref)
            dsc_ref[...] = jnp.zeros_like(dsc_ref)
            dpg_ref[...] = jnp.zeros_like(dpg_ref)

        dhn_v = dhn_ref[...]
        xv = x_ref[...]
        r = _rstd(xv)
        xh = xv * r
        pg_v = pg_ref[...]
        dsh_ref[...] += jnp.sum(dhn_v, axis=0, keepdims=True)
        dsc_ref[...] += jnp.sum(dhn_v * (xh * pg_v), axis=0, keepdims=True)
        dn = dhn_v * (1.0 + sc_ref[...])
        dpg_ref[...] += jnp.sum(dn * xh, axis=0, keepdims=True)
        dxh = dn * pg_v
        dx_ref[...] = do_ref[...] + r * (dxh - xh * jnp.mean(dxh * xh, axis=-1, keepdims=True))

    vec = jax.ShapeDtypeStruct((1, w), F32)
    return pl.pallas_call(
        body, grid=(s_n // tm,), in_specs=[_rows(tm, w), _rows(tm, w), _rows(tm, w), _vec(w), _vec(w)],
        out_specs=[_rows(tm, w), _vec(w), _vec(w), _vec(w)],
        out_shape=[jax.ShapeDtypeStruct((s_n, w), F32), vec, vec, vec],
        compiler_params=_cparams(), name=name)(dhn, x, dout, pg, sc)


def _rms_fwd(x, g, *, name):
    s_n, w = x.shape
    tm = _pick(s_n, 512, 8)

    def body(x_ref, g_ref, o_ref):
        xv = x_ref[...]
        o_ref[...] = ((xv * _rstd(xv)) * g_ref[...]).astype(o_ref.dtype)

    return pl.pallas_call(body, grid=(s_n // tm,), in_specs=[_rows(tm, w), _vec(w)], out_specs=_rows(tm, w),
                          out_shape=jax.ShapeDtypeStruct((s_n, w), BF16), compiler_params=_cparams(),
                          name=name)(x, g)


def _rms_bwd(dy, x, g, *, name):
    s_n, w = x.shape
    tm = _pick(s_n, 512, 8)

    def body(dy_ref, x_ref, g_ref, dx_ref, dg_ref):
        @pl.when(pl.program_id(0) == 0)
        def _():
            dg_ref[...] = jnp.zeros_like(dg_ref)

        dy_v = dy_ref[...]
        xv = x_ref[...]
        r = _rstd(xv)
        xh = xv * r
        dg_ref[...] += jnp.sum(dy_v * xh, axis=0, keepdims=True)
        dxh = dy_v * g_ref[...]
        dx_ref[...] = r * (dxh - xh * jnp.mean(dxh * xh, axis=-1, keepdims=True))

    return pl.pallas_call(
        body, grid=(s_n // tm,), in_specs=[_rows(tm, w), _rows(tm, w), _vec(w)],
        out_specs=[_rows(tm, w), _vec(w)],
        out_shape=[jax.ShapeDtypeStruct((s_n, w), F32), jax.ShapeDtypeStruct((1, w), F32)],
        compiler_params=_cparams(), name=name)(dy, x, g)


def _rope(a1, a2, cos, sin, *, name):
    s_n, w = a1.shape
    tm = _pick(s_n, 512, 8)

    def body(a1_ref, a2_ref, c_ref, s_ref, r1_ref, r2_ref):
        u, v, c_v, s_v = a1_ref[...], a2_ref[...], c_ref[...], s_ref[...]
        r1_ref[...] = u * c_v - v * s_v
        r2_ref[...] = u * s_v + v * c_v

    sd = jax.ShapeDtypeStruct((s_n, w), F32)
    return pl.pallas_call(body, grid=(s_n // tm,), in_specs=[_rows(tm, w)] * 4, out_specs=[_rows(tm, w)] * 2,
                          out_shape=[sd, sd], compiler_params=_cparams(), name=name)(a1, a2, cos, sin)


def _silu_bf16(x, *, name):
    def body(x_ref, o_ref):
        xv = x_ref[...]
        o_ref[...] = (xv * jax.nn.sigmoid(xv)).astype(o_ref.dtype)

    return pl.pallas_call(body, out_shape=jax.ShapeDtypeStruct(x.shape, BF16), name=name)(x)


def _loss(y, target, *, name):
    s_n, w = y.shape
    tm = _pick(s_n, 512, 8)

    def body(y_ref, t_ref, dy_ref, l_ref):
        @pl.when(pl.program_id(0) == 0)
        def _():
            l_ref[...] = jnp.zeros_like(l_ref)

        e = y_ref[...] - t_ref[...]
        dy_ref[...] = e * (1.0 / w)
        row = jnp.mean(e * e, axis=-1, keepdims=True)
        l_ref[...] += 0.5 * jnp.sum(row, axis=0, keepdims=True)

    return pl.pallas_call(
        body, grid=(s_n // tm,), in_specs=[_rows(tm, w), _rows(tm, w)],
        out_specs=[_rows(tm, w), pl.BlockSpec((1, 1), lambda i: (0, 0))],
        out_shape=[jax.ShapeDtypeStruct((s_n, w), F32), jax.ShapeDtypeStruct((1, 1), F32)],
        compiler_params=_cparams(), name=name)(y, target)


FFN_TM = 512


def _ffn_up(hn, w_gu, *, name):
    s_n, d = hn.shape
    f = w_gu.shape[-1]
    tm = _pick(s_n, FFN_TM, 8)

    def body(hn_ref, wg_ref, wu_ref, gu_ref, a_ref):
        xv = hn_ref[...]
        g = jnp.dot(xv, wg_ref[...], preferred_element_type=F32)
        u = jnp.dot(xv, wu_ref[...], preferred_element_type=F32)
        gu_ref[0] = g.astype(BF16)
        gu_ref[1] = u.astype(BF16)
        a_ref[...] = ((g * jax.nn.sigmoid(g)) * u).astype(BF16)

    w_blk = lambda t: pl.BlockSpec((None, None, d, f), lambda s, m: (s, t, 0, 0))
    return pl.pallas_call(
        body, grid=(N_SHARD, s_n // tm),
        in_specs=[pl.BlockSpec((tm, d), lambda s, m: (m, 0)), w_blk(0), w_blk(1)],
        out_specs=[pl.BlockSpec((None, 2, tm, f), lambda s, m: (s, 0, m, 0)),
                   pl.BlockSpec((None, tm, f), lambda s, m: (s, m, 0))],
        out_shape=[jax.ShapeDtypeStruct((N_SHARD, 2, s_n, f), BF16), jax.ShapeDtypeStruct((N_SHARD, s_n, f), BF16)],
        compiler_params=_cparams(), name=name)(hn, w_gu, w_gu)


def _ffn_down(a, w_dn, x, qg, gate, res_w, *, name):
    g_n, s_n, f = a.shape
    d = w_dn.shape[-1]
    tm = _pick(s_n, FFN_TM, 8)

    def body(a_ref, w_ref, x_ref, qg_ref, gate_ref, f_ref, o_ref):
        g = pl.program_id(1)
        r = jnp.dot(a_ref[...], w_ref[...], preferred_element_type=F32)

        @pl.when(g == 0)
        def _():
            f_ref[...] = r

        @pl.when(g > 0)
        def _():
            f_ref[...] += r

        @pl.when(g == g_n - 1)
        def _():
            fv = f_ref[...]
            y = (fv * _rstd(fv)) * qg_ref[...]
            o_ref[...] = x_ref[...] + (res_w * gate_ref[...]) * y

    row = pl.BlockSpec((tm, d), lambda m, g: (m, 0))
    vec = pl.BlockSpec((1, d), lambda m, g: (0, 0))
    sd = jax.ShapeDtypeStruct((s_n, d), F32)
    return pl.pallas_call(
        body, grid=(s_n // tm, g_n),
        in_specs=[pl.BlockSpec((None, tm, f), lambda m, g: (g, m, 0)), pl.BlockSpec((None, f, d), lambda m, g: (g, 0, 0)),
                  row, vec, vec],
        out_specs=[row, row], out_shape=[sd, sd], compiler_params=_cparams(), name=name)(a, w_dn, x, qg, gate)


def _ffn_dhn(dgu, w_gu, x, dout, pg, sc, *, name):
    g_n, s_n, f = dgu.shape
    d = w_gu.shape[-2]
    tm = _pick(s_n, FFN_TM, 8)

    def body(a_ref, w_ref, x_ref, do_ref, pg_ref, sc_ref, dx_ref, dsh_ref, dsc_ref, dpg_ref, acc_ref):
        m, g = pl.program_id(0), pl.program_id(1)
        r = lax.dot_general(a_ref[...], w_ref[...], (((1,), (1,)), ((), ())), preferred_element_type=F32)

        @pl.when(g == 0)
        def _():
            acc_ref[...] = r

        @pl.when(g > 0)
        def _():
            acc_ref[...] += r

        @pl.when((m == 0) & (g == 0))
        def _():
            dsh_ref[...] = jnp.zeros_like(dsh_ref)
            dsc_ref[...] = jnp.zeros_like(dsc_ref)
            dpg_ref[...] = jnp.zeros_like(dpg_ref)

        @pl.when(g == g_n - 1)
        def _():
            dhn_v = acc_ref[...]
            xv = x_ref[...]
            rs = _rstd(xv)
            xh = xv * rs
            pg_v = pg_ref[...]
            dsh_ref[...] += jnp.sum(dhn_v, axis=0, keepdims=True)
            dsc_ref[...] += jnp.sum(dhn_v * (xh * pg_v), axis=0, keepdims=True)
            dn = dhn_v * (1.0 + sc_ref[...])
            dpg_ref[...] += jnp.sum(dn * xh, axis=0, keepdims=True)
            dxh = dn * pg_v
            dx_ref[...] = do_ref[...] + rs * (dxh - xh * jnp.mean(dxh * xh, axis=-1, keepdims=True))

    row = pl.BlockSpec((tm, d), lambda m, g: (m, 0))
    vec = pl.BlockSpec((1, d), lambda m, g: (0, 0))
    vsd = jax.ShapeDtypeStruct((1, d), F32)
    return pl.pallas_call(
        body, grid=(s_n // tm, g_n),
        in_specs=[pl.BlockSpec((None, tm, f), lambda m, g: (g, m, 0)), pl.BlockSpec((None, d, f), lambda m, g: (g, 0, 0)),
                  row, row, vec, vec],
        out_specs=[row, vec, vec, vec], out_shape=[jax.ShapeDtypeStruct((s_n, d), F32), vsd, vsd, vsd],
        scratch_shapes=[pltpu.VMEM((tm, d), F32)], compiler_params=_cparams(), name=name)(dgu, w_gu, x, dout, pg, sc)


def _ffn_dgu(df, w_dn, gu, *, name):
    s_n, d = df.shape
    f = w_dn.shape[-2]
    tm = _pick(s_n, FFN_TM, 8)

    def body(df_ref, wd_ref, gu_ref, o_ref):
        da = lax.dot_general(df_ref[...], wd_ref[...], (((1,), (1,)), ((), ())), preferred_element_type=F32)
        g = gu_ref[0].astype(F32)
        u = gu_ref[1].astype(F32)
        sig = jax.nn.sigmoid(g)
        o_ref[0] = (da * u * (sig * (1.0 + g * (1.0 - sig)))).astype(BF16)
        o_ref[1] = (da * (g * sig)).astype(BF16)

    gu_blk = pl.BlockSpec((None, 2, tm, f), lambda s, m: (s, 0, m, 0))
    return pl.pallas_call(
        body, grid=(N_SHARD, s_n // tm),
        in_specs=[pl.BlockSpec((tm, d), lambda s, m: (m, 0)),
                  pl.BlockSpec((None, f, d), lambda s, m: (s, 0, 0)), gu_blk],
        out_specs=gu_blk, out_shape=jax.ShapeDtypeStruct((N_SHARD, 2, s_n, f), BF16),
        compiler_params=_cparams(), name=name)(df, w_dn, gu)


_NT = (((1,), (1,)), ((), ()))
_TN = (((0,), (0,)), ((), ()))
MLA_TQ = 256


def _causal_mask(i, tq, s_n):
    qpos = i * tq + lax.broadcasted_iota(jnp.int32, (tq, s_n), 0)
    kpos = lax.broadcasted_iota(jnp.int32, (tq, s_n), 1)
    return kpos <= qpos


def _mla_attn_fwd(q, k, v, *, name):
    h_n, s_n, dq = q.shape
    dv = v.shape[-1]
    tq = MLA_TQ
    scale = float(dq) ** -0.5

    def body(q_ref, k_ref, v_ref, o_ref, lse_ref):
        i = pl.program_id(1)
        for e in range(1, s_n // tq + 1):
            @pl.when(i == e - 1)
            def _(ext=e * tq):
                mask = _causal_mask(i, tq, ext)
                s = lax.dot_general(q_ref[...], k_ref[0:ext, :], _NT, preferred_element_type=F32) * scale
                s = jnp.where(mask, s, -jnp.inf)
                m = jnp.max(s, axis=-1, keepdims=True)
                p = jnp.exp(s - m)
                l = jnp.sum(p, axis=-1, keepdims=True)
                o = jnp.dot(p.astype(BF16), v_ref[0:ext, :], preferred_element_type=F32)
                o_ref[...] = o / l
                lse_ref[...] = m + jnp.log(l)

    return pl.pallas_call(
        body, grid=(h_n, s_n // tq),
        in_specs=[pl.BlockSpec((None, tq, dq), lambda h, i: (h, i, 0)),
                  pl.BlockSpec((None, s_n, dq), lambda h, i: (h, 0, 0)),
                  pl.BlockSpec((None, s_n, dv), lambda h, i: (h, 0, 0))],
        out_specs=[pl.BlockSpec((None, tq, dv), lambda h, i: (h, i, 0)),
                   pl.BlockSpec((None, tq, 1), lambda h, i: (h, i, 0))],
        out_shape=[jax.ShapeDtypeStruct((h_n, s_n, dv), F32), jax.ShapeDtypeStruct((h_n, s_n, 1), F32)],
        compiler_params=_cparams(), name=name)(q, k, v)


def _mla_attn_bwd(q, k, v, o, do, lse, *, name):
    h_n, s_n, dq = q.shape
    dv = v.shape[-1]
    tq = MLA_TQ
    scale = float(dq) ** -0.5

    def body(q_ref, k_ref, v_ref, o_ref, do_ref, lse_ref, dq_ref, dk_ref, dv_ref):
        i = pl.program_id(1)

        @pl.when(i == 0)
        def _():
            dk_ref[...] = jnp.zeros_like(dk_ref)
            dv_ref[...] = jnp.zeros_like(dv_ref)

        for e in range(1, s_n // tq + 1):
            @pl.when(i == e - 1)
            def _(ext=e * tq):
                mask = _causal_mask(i, tq, ext)
                qv, kv, vv = q_ref[...], k_ref[0:ext, :], v_ref[0:ext, :]
                do_v = do_ref[...]
                s = lax.dot_general(qv, kv, _NT, preferred_element_type=F32) * scale
                p = jnp.where(mask, jnp.exp(s - lse_ref[...]), 0.0)
                dob = do_v.astype(BF16)
                dv_ref[0:ext, :] += lax.dot_general(p.astype(BF16), dob, _TN, preferred_element_type=F32)
                dp = lax.dot_general(dob, vv, _NT, preferred_element_type=F32)
                delta = jnp.sum(do_v * o_ref[...], axis=-1, keepdims=True)
                dsb = (p * (dp - delta) * scale).astype(BF16)
                dq_ref[...] = jnp.dot(dsb, kv, preferred_element_type=F32)
                dk_ref[0:ext, :] += lax.dot_general(dsb, qv, _TN, preferred_element_type=F32)

    return pl.pallas_call(
        body, grid=(h_n, s_n // tq),
        in_specs=[pl.BlockSpec((None, tq, dq), lambda h, i: (h, i, 0)),
                  pl.BlockSpec((None, s_n, dq), lambda h, i: (h, 0, 0)),
                  pl.BlockSpec((None, s_n, dv), lambda h, i: (h, 0, 0)),
                  pl.BlockSpec((None, tq, dv), lambda h, i: (h, i, 0)),
                  pl.BlockSpec((None, tq, dv), lambda h, i: (h, i, 0)),
                  pl.BlockSpec((None, tq, 1), lambda h, i: (h, i, 0))],
        out_specs=[pl.BlockSpec((None, tq, dq), lambda h, i: (h, i, 0)),
                   pl.BlockSpec((None, s_n, dq), lambda h, i: (h, 0, 0)),
                   pl.BlockSpec((None, s_n, dv), lambda h, i: (h, 0, 0))],
        out_shape=[jax.ShapeDtypeStruct((h_n, s_n, dq), F32), jax.ShapeDtypeStruct((h_n, s_n, dq), F32),
                   jax.ShapeDtypeStruct((h_n, s_n, dv), F32)],
        compiler_params=_cparams(), name=name)(q, k, v, o, do, lse)


def _head_sum(x, *, name):
    h_n, s_n, w = x.shape
    tm = _pick(s_n, 512, 8)

    def body(x_ref, o_ref):
        o_ref[...] = jnp.sum(x_ref[...], axis=0)

    return pl.pallas_call(body, grid=(s_n // tm,), in_specs=[pl.BlockSpec((h_n, tm, w), lambda i: (0, i, 0))],
                          out_specs=_rows(tm, w), out_shape=jax.ShapeDtypeStruct((s_n, w), F32),
                          compiler_params=_cparams(), name=name)(x)


N_BLK = SEQ // DIL_BLOCK
DIL_SCALE = 64 ** -0.5


def _dil_masks():
    iq = lax.broadcasted_iota(jnp.int32, (DIL_BLOCK, 2 * DIL_BLOCK), 0)
    ik = lax.broadcasted_iota(jnp.int32, (DIL_BLOCK, 2 * DIL_BLOCK), 1)
    rel = DIL_BLOCK + iq - ik
    both = (rel >= 0) & (rel <= DIL_BLOCK)
    iq1 = lax.broadcasted_iota(jnp.int32, (DIL_BLOCK, DIL_BLOCK), 0)
    ik1 = lax.broadcasted_iota(jnp.int32, (DIL_BLOCK, DIL_BLOCK), 1)
    return both, ik1 <= iq1


def _dil_block(j, d):
    nb = SEQ // d // DIL_BLOCK
    r, n = divmod(j, nb)
    first = n == 0
    rows = lambda start, size: pl.ds(start, size) if d == 1 else pl.ds(start, size, stride=d)
    q_rows = rows(n * DIL_BLOCK * d + r, DIL_BLOCK)
    k_rows = q_rows if first else rows((n - 1) * DIL_BLOCK * d + r, 2 * DIL_BLOCK)
    return q_rows, k_rows, (DIL_BLOCK if first else 0), first


def _dil_head_specs(s_n, e, g):
    return [pl.BlockSpec((None, s_n, e), lambda h, t=t: (g * 3 * HEADS + t * HEADS + h, 0, 0)) for t in range(3)]


def _dil_attn_fwd(heads, bias, g, d, *, name):
    _, s_n, e = heads.shape

    def body(q_ref, k_ref, v_ref, b_ref, o_ref, lse_ref):
        m_both, m_first = _dil_masks()
        for j in range(N_BLK):
            q_rows, k_rows, b_lo, first = _dil_block(j, d)
            qj = q_ref[q_rows, :].astype(BF16)
            kk = k_ref[k_rows, :].astype(BF16)
            vv = v_ref[k_rows, :].astype(BF16)
            s = lax.dot_general(qj, kk, _NT, preferred_element_type=F32) * DIL_SCALE + b_ref[:, b_lo:]
            s = jnp.where(m_first if first else m_both, s, -jnp.inf)
            m = jnp.max(s, axis=-1, keepdims=True)
            lse = m + jnp.log(jnp.sum(jnp.exp(s - m), axis=-1, keepdims=True))
            p = jnp.exp(s - lse)
            o_ref[q_rows, :] = jnp.dot(p.astype(BF16), vv, preferred_element_type=F32)
            lse_ref[q_rows, :] = lse

    head = lambda w: pl.BlockSpec((None, s_n, w), lambda h: (h, 0, 0))
    return pl.pallas_call(
        body, grid=(HEADS,),
        in_specs=_dil_head_specs(s_n, e, g) + [pl.BlockSpec((None, DIL_BLOCK, 2 * DIL_BLOCK), lambda h: (h, 0, 0))],
        out_specs=[head(e), head(1)],
        out_shape=[jax.ShapeDtypeStruct((HEADS, s_n, e), F32), jax.ShapeDtypeStruct((HEADS, s_n, 1), F32)],
        compiler_params=_cparams(), name=name)(heads, heads, heads, bias)


def _dil_attn_bwd(heads, bias, lse, do, dlt, g, d, *, name):
    _, s_n, e = heads.shape

    def body(q_ref, k_ref, v_ref, b_ref, lse_ref, do_ref, dlt_ref, dq_ref, dk_ref, dv_ref, db_ref):
        db_ref[...] = jnp.zeros_like(db_ref)
        m_both, m_first = _dil_masks()
        nb = s_n // d // DIL_BLOCK
        own_v = own_k = own_rows = None
        for j in range(N_BLK):
            q_rows, k_rows, b_lo, first = _dil_block(j, d)
            qj = q_ref[q_rows, :].astype(BF16)
            kk = k_ref[k_rows, :].astype(BF16)
            vv = v_ref[k_rows, :].astype(BF16)
            s = lax.dot_general(qj, kk, _NT, preferred_element_type=F32) * DIL_SCALE + b_ref[:, b_lo:]
            p = jnp.where(m_first if first else m_both, jnp.exp(s - lse_ref[q_rows, :]), 0.0)
            dob = do_ref[q_rows, :].astype(BF16)
            dvv = lax.dot_general(p.astype(BF16), dob, _TN, preferred_element_type=F32)
            dp = lax.dot_general(dob, vv, _NT, preferred_element_type=F32)
            ds = p * (dp - dlt_ref[q_rows, :])
            db_ref[:, b_lo:] += ds
            dsb = (ds * DIL_SCALE).astype(BF16)
            dq_ref[q_rows, :] = jnp.dot(dsb, kk, preferred_element_type=F32)
            dkk = lax.dot_general(dsb, qj, _TN, preferred_element_type=F32)
            if not first:
                dv_ref[own_rows, :] = own_v + dvv[:DIL_BLOCK]
                dk_ref[own_rows, :] = own_k + dkk[:DIL_BLOCK]
                dvv, dkk = dvv[DIL_BLOCK:], dkk[DIL_BLOCK:]
            own_v, own_k, own_rows = dvv, dkk, q_rows
            if j % nb == nb - 1:
                dv_ref[own_rows, :] = own_v
                dk_ref[own_rows, :] = own_k

    head = lambda w: pl.BlockSpec((None, s_n, w), lambda h: (h, 0, 0))
    b_spec = pl.BlockSpec((None, DIL_BLOCK, 2 * DIL_BLOCK), lambda h: (h, 0, 0))
    sd = jax.ShapeDtypeStruct((HEADS, s_n, e), F32)
    return pl.pallas_call(
        body, grid=(HEADS,),
        in_specs=_dil_head_specs(s_n, e, g) + [b_spec, head(1), head(e), head(1)],
        out_specs=[head(e), head(e), head(e), b_spec],
        out_shape=[sd, sd, sd, jax.ShapeDtypeStruct((HEADS, DIL_BLOCK, 2 * DIL_BLOCK), F32)],
        compiler_params=_cparams(), name=name)(heads, heads, heads, bias, lse, do, dlt)


def _proj_heads(x, w, *, name):
    s_n, k = x.shape
    n = w.shape[-1]
    tm, tn, e = 512, 768, 64
    per_blk, n_blk = tn // e, n // tn

    def body(x_ref, w_ref, o_ref):
        r = jnp.dot(x_ref[...], w_ref[...], preferred_element_type=F32)
        for j in range(per_blk):
            o_ref[j] = r[:, e * j:e * (j + 1)]

    return pl.pallas_call(
        body, grid=(w.shape[0], n_blk, s_n // tm),
        in_specs=[pl.BlockSpec((tm, k), lambda s, b, m: (m, 0)), pl.BlockSpec((None, k, tn), lambda s, b, m: (s, 0, b))],
        out_specs=pl.BlockSpec((per_blk, tm, e), lambda s, b, m: (s * n_blk + b, m, 0)),
        out_shape=jax.ShapeDtypeStruct((w.shape[0] * n // e, s_n, e), F32), compiler_params=_cparams(),
        name=name)(x, w)


def _heads_cat(d_ref):
    return jnp.concatenate([d_ref[j] for j in range(d_ref.shape[0])], axis=1)


def _proj_heads_dw(x, dh, *, name):
    s_n, k = x.shape
    tn, e = 768, 64
    per_blk = tn // e
    n_blk = dh.shape[0] // N_SHARD // per_blk
    n = n_blk * tn

    def body(x_ref, d_ref, o_ref):
        o_ref[...] = lax.dot_general(x_ref[...], _heads_cat(d_ref), _TN, preferred_element_type=F32)

    return pl.pallas_call(
        body, grid=(N_SHARD, n_blk, 2),
        in_specs=[pl.BlockSpec((s_n, k // 2), lambda s, b, r: (0, r)),
                  pl.BlockSpec((per_blk, s_n, e), lambda s, b, r: (s * n_blk + b, 0, 0))],
        out_specs=pl.BlockSpec((None, None, k // 2, tn), lambda s, b, r: (r, s, 0, b)),
        out_shape=jax.ShapeDtypeStruct((2, N_SHARD, k // 2, n), F32), compiler_params=_cparams(), name=name)(x, dh)


def _proj_heads_dx(dh, w, *, name):
    k, n = w.shape[1:]
    s_n = dh.shape[1]
    tm, tn, e = 512, 768, 64
    per_blk, n_blk = tn // e, n // tn

    def body(d_ref, w_ref, o_ref):
        r = lax.dot_general(_heads_cat(d_ref), w_ref[...], _NT, preferred_element_type=F32)
        g = pl.program_id(1)

        @pl.when(g == 0)
        def _():
            o_ref[...] = r

        @pl.when(g > 0)
        def _():
            o_ref[...] += r

    return pl.pallas_call(
        body, grid=(s_n // tm, N_SHARD * n_blk),
        in_specs=[pl.BlockSpec((per_blk, tm, e), lambda m, g: (g, m, 0)),
                  pl.BlockSpec((None, k, tn), lambda m, g: (g // n_blk, 0, g % n_blk))],
        out_specs=pl.BlockSpec((tm, k), lambda m, g: (m, 0)),
        out_shape=jax.ShapeDtypeStruct((s_n, k), F32), compiler_params=_cparams(), name=name)(dh, w)


def _group_alpha(l_refs):
    ls = [r[...] for r in l_refs]
    m = jnp.maximum(jnp.maximum(ls[0], ls[1]), ls[2])
    es = [jnp.exp(l - m) for l in ls]
    tot = es[0] + es[1] + es[2]
    return [ex / tot for ex in es]


def _dil_mix_fwd(os_, ls_, *, name):
    h_n, s_n, e = os_[0].shape
    tm = 512

    def body(o0, o1, o2, l0, l1, l2, out_ref):
        al = _group_alpha((l0, l1, l2))
        out_ref[...] = al[0] * o0[...] + al[1] * o1[...] + al[2] * o2[...]

    blk = lambda w: pl.BlockSpec((None, tm, w), lambda h, i: (h, i, 0))
    return pl.pallas_call(body, grid=(h_n, s_n // tm), in_specs=[blk(e)] * 3 + [blk(1)] * 3, out_specs=blk(e),
                          out_shape=jax.ShapeDtypeStruct((h_n, s_n, e), F32), compiler_params=_cparams(),
                          name=name)(*os_, *ls_)


def _dil_mix_bwd(do, os_, ls_, *, name):
    h_n, s_n, e = do.shape
    tm = 512

    def body(do_ref, o0, o1, o2, l0, l1, l2, d0, d1, d2, t0, t1, t2):
        al = _group_alpha((l0, l1, l2))
        do_v = do_ref[...]
        mix = al[0] * o0[...] + al[1] * o1[...] + al[2] * o2[...]
        dbar = jnp.sum(do_v * mix, axis=-1, keepdims=True)
        for a_g, d_ref, t_ref in zip(al, (d0, d1, d2), (t0, t1, t2)):
            d_ref[...] = a_g * do_v
            t_ref[...] = a_g * dbar

    blk = lambda w: pl.BlockSpec((None, tm, w), lambda h, i: (h, i, 0))
    sd_e = jax.ShapeDtypeStruct((h_n, s_n, e), F32)
    sd_1 = jax.ShapeDtypeStruct((h_n, s_n, 1), F32)
    outs = pl.pallas_call(body, grid=(h_n, s_n // tm), in_specs=[blk(e)] * 4 + [blk(1)] * 3,
                          out_specs=[blk(e)] * 3 + [blk(1)] * 3, out_shape=[sd_e] * 3 + [sd_1] * 3,
                          compiler_params=_cparams(), name=name)(do, *os_, *ls_)
    return outs[:3], outs[3:]


def _bias_grad(ds, bucket, *, name):
    h_n = ds.shape[0]

    def body(ds_ref, bk_ref, o_ref):
        ds_v = ds_ref[...]
        bk = bk_ref[...]
        lane = lax.broadcasted_iota(jnp.int32, (1, N_BUCKETS), 1)
        acc = jnp.zeros((1, N_BUCKETS), F32)
        for b in range(N_BUCKETS):
            tot = jnp.sum(jnp.sum(jnp.where(bk == b, ds_v, 0.0), axis=1, keepdims=True), axis=0, keepdims=True)
            acc = acc + jnp.where(lane == b, tot, 0.0)
        o_ref[...] = acc

    return pl.pallas_call(
        body, grid=(h_n,),
        in_specs=[pl.BlockSpec((None, DIL_BLOCK, 2 * DIL_BLOCK), lambda h: (h, 0, 0)),
                  pl.BlockSpec((DIL_BLOCK, 2 * DIL_BLOCK), lambda h: (0, 0))],
        out_specs=pl.BlockSpec((None, 1, N_BUCKETS), lambda h: (h, 0, 0)),
        out_shape=jax.ShapeDtypeStruct((h_n, 1, N_BUCKETS), F32), compiler_params=_cparams(), name=name)(ds, bucket)


def _bias_table(rb, bucket, *, name):
    h_n = rb.shape[0]

    def body(rb_ref, bk_ref, o_ref):
        bk = bk_ref[...]
        row = rb_ref[...]
        acc = jnp.zeros(bk.shape, F32)
        for b in range(N_BUCKETS):
            acc = jnp.where(bk == b, row[:, b:b + 1], acc)
        o_ref[...] = acc

    return pl.pallas_call(
        body, grid=(h_n,),
        in_specs=[pl.BlockSpec((None, 1, N_BUCKETS), lambda h: (h, 0, 0)),
                  pl.BlockSpec((DIL_BLOCK, 2 * DIL_BLOCK), lambda h: (0, 0))],
        out_specs=pl.BlockSpec((None, DIL_BLOCK, 2 * DIL_BLOCK), lambda h: (h, 0, 0)),
        out_shape=jax.ShapeDtypeStruct((h_n, DIL_BLOCK, 2 * DIL_BLOCK), F32), compiler_params=_cparams(),
        name=name)(rb, bucket)


def _row_tile(rows, cols, budget=2 << 20):
    if rows * cols * 4 <= budget or rows % 8:
        return rows
    best = 8
    for t in range(8, rows + 1, 8):
        if rows % t == 0 and t * cols * 4 <= budget:
            best = t
    return best


def _adamw(w, g, m, v, *, name):
    shape = w.shape
    cols = shape[-1]
    rows = math.prod(shape[:-1]) if len(shape) > 1 else 1
    to2 = lambda t: t.reshape(rows, cols)
    tr = _row_tile(rows, cols)
    c1 = 1.0 / (1.0 - ADAM_B1 ** ADAM_STEP)
    c2 = 1.0 / (1.0 - ADAM_B2 ** ADAM_STEP)

    def body(w_ref, g_ref, m_ref, v_ref, d_ref, nm_ref, nv_ref):
        g_v = g_ref[...]
        nm = ADAM_B1 * m_ref[...] + (1.0 - ADAM_B1) * g_v
        nv = ADAM_B2 * v_ref[...] + (1.0 - ADAM_B2) * (g_v * g_v)
        m_hat = nm * c1
        v_hat = nv * c2
        d_ref[...] = -ADAM_LR * (m_hat / (jnp.sqrt(v_hat) + ADAM_EPS) + ADAM_WD * w_ref[...])
        nm_ref[...] = nm
        nv_ref[...] = nv

    blk = pl.BlockSpec((tr, cols), lambda i: (i, 0))
    sd = jax.ShapeDtypeStruct((rows, cols), F32)
    outs = pl.pallas_call(body, grid=(rows // tr,), in_specs=[blk] * 4, out_specs=[blk] * 3, out_shape=[sd] * 3,
                          compiler_params=_cparams(), name=name)(to2(w), to2(g), to2(m), to2(v))
    return tuple(t.reshape(shape) for t in outs)


def _add_half(unit, got, half_idx, *, name):
    rest = unit.shape[2:]
    c = rest[-1]
    r = math.prod(rest[:-1])
    tr = _row_tile(r, c)

    def body(idx_ref, u_ref, g_ref, o_ref, w_ref):
        tot = u_ref[...] + g_ref[...].astype(F32)
        o_ref[...] = tot
        w_ref[...] = tot.astype(BF16)

    blk = pl.BlockSpec((None, tr, c), lambda s, i, idx: (s, i, 0))
    grid_spec = pltpu.PrefetchScalarGridSpec(
        num_scalar_prefetch=1, grid=(N_SHARD, r // tr),
        in_specs=[pl.BlockSpec((None, None, tr, c), lambda s, i, idx: (idx[0], s, i, 0)), blk],
        out_specs=[blk, blk])
    out, wire = pl.pallas_call(
        body, grid_spec=grid_spec,
        out_shape=[jax.ShapeDtypeStruct((N_SHARD, r, c), F32), jax.ShapeDtypeStruct((N_SHARD, r, c), BF16)],
        compiler_params=_cparams(), name=name)(half_idx, unit.reshape(2, N_SHARD, r, c), got.reshape(N_SHARD, r, c))
    return out.reshape((N_SHARD,) + rest), wire.reshape((N_SHARD,) + rest)


def _add_shards(part, got, shard_idx, *, name):
    rest = part.shape[1:]
    c = rest[-1]
    r = math.prod(rest[:-1])
    tr = _row_tile(r, c)

    def body(idx_ref, p_ref, g_ref, o_ref):
        acc = p_ref[...]
        for k in range(3):
            acc = acc + g_ref[k].astype(F32)
        o_ref[...] = acc

    grid_spec = pltpu.PrefetchScalarGridSpec(
        num_scalar_prefetch=1, grid=(r // tr,),
        in_specs=[pl.BlockSpec((None, tr, c), lambda i, idx: (idx[0], i, 0)),
                  pl.BlockSpec((3, tr, c), lambda i, idx: (0, i, 0))],
        out_specs=pl.BlockSpec((tr, c), lambda i, idx: (i, 0)))
    out = pl.pallas_call(body, grid_spec=grid_spec, out_shape=jax.ShapeDtypeStruct((r, c), F32),
                         compiler_params=_cparams(), name=name)(
        shard_idx, part.reshape(N_SHARD, r, c), got.reshape(3, r, c))
    return out.reshape(rest)


def _sum_devices(x, n_dev, *, name):
    rows = x.shape[0] // n_dev

    def body(x_ref, o_ref):
        acc = x_ref[0:rows, :]
        for d in range(1, n_dev):
            acc = acc + x_ref[d * rows:(d + 1) * rows, :]
        o_ref[...] = acc

    return pl.pallas_call(body, out_shape=jax.ShapeDtypeStruct((rows, x.shape[1]), F32), name=name)(x)


def _my_pos():
    return lax.axis_index("x"), lax.axis_index("y"), lax.axis_index("c")


def _all_gather(x_blk, *, name, in_vmem):
    m_per, n = x_blk.shape

    def body(x_ref, out_ref, send_sems, recv_sems, local_sem):
        x, y, c = _my_pos()
        me, sibling = (x, y, c), (x, y, 1 - c)
        chips = [(1 - x, y), (x, 1 - y), (1 - x, 1 - y)]

        def rows(px, py, pc):
            return out_ref.at[pl.ds((4 * px + 2 * py + pc) * m_per, m_per), :]

        def copy(k, block, to, src=None):
            return pltpu.make_async_remote_copy(
                src_ref=rows(*block) if src is None else src, dst_ref=rows(*block),
                send_sem=send_sems.at[k], recv_sem=recv_sems.at[k], device_id=to, device_id_type=MESH)

        mine = pltpu.make_async_copy(x_ref, rows(*me), local_sem)
        mine.start()
        first = [copy(0, me, sibling, src=x_ref)]
        first += [copy(1 + j, me, (*chip, c), src=x_ref) for j, chip in enumerate(chips)]
        for cp in first:
            cp.start()
        passed = [copy(4 + j, (*chip, c), sibling) for j, chip in enumerate(chips)]
        for j, chip in enumerate(chips):
            copy(1 + j, (*chip, c), me).wait_recv()
            passed[j].start()
        copy(0, sibling, me).wait_recv()
        for j, chip in enumerate(chips):
            copy(4 + j, (*chip, 1 - c), me).wait_recv()
        for cp in first + passed:
            cp.wait_send()
        mine.wait()

    space = pltpu.VMEM if in_vmem else pl.ANY
    return pl.pallas_call(
        body, out_shape=jax.ShapeDtypeStruct((8 * m_per, n), x_blk.dtype),
        in_specs=[pl.BlockSpec(memory_space=space)], out_specs=pl.BlockSpec(memory_space=space),
        scratch_shapes=[pltpu.SemaphoreType.DMA((7,)), pltpu.SemaphoreType.DMA((7,)), pltpu.SemaphoreType.DMA],
        name=name)(x_blk)


_HBM = pl.BlockSpec(memory_space=pl.ANY)


def _gather_weights(fams, *, name):
    n = len(fams)

    def body(*refs):
        ins, outs = refs[:n], refs[n:2 * n]
        send_sems, recv_sems = refs[2 * n:]
        x, y, c = _my_pos()
        me, sibling = (x, y, c), (x, y, 1 - c)
        chips = [(1 - x, y), (x, 1 - y), (1 - x, 1 - y)]

        def copy(f, k, block, to, src=None):
            px, py, pc = block
            dst = outs[f].at[2 * px + py, pc]
            return pltpu.make_async_remote_copy(
                src_ref=dst if src is None else src, dst_ref=dst, send_sem=send_sems.at[7 * f + k],
                recv_sem=recv_sems.at[7 * f + k], device_id=to, device_id_type=MESH)

        first, passed = [], []
        for f in range(n):
            src = ins[f].at[c]
            first.append(copy(f, 0, me, sibling, src=src))
            first += [copy(f, 1 + j, me, (*chip, c), src=src) for j, chip in enumerate(chips)]
        for cp in first:
            cp.start()
        for j, chip in enumerate(chips):
            for f in range(n):
                copy(f, 1 + j, (*chip, c), me).wait_recv()
                passed.append(copy(f, 4 + j, (*chip, c), sibling))
                passed[-1].start()
        for f in range(n):
            copy(f, 0, sibling, me).wait_recv()
        for j, chip in enumerate(chips):
            for f in range(n):
                copy(f, 4 + j, (*chip, 1 - c), me).wait_recv()
        for cp in first + passed:
            cp.wait_send()

    outs = pl.pallas_call(
        body, out_shape=[jax.ShapeDtypeStruct((N_SHARD,) + t.shape, t.dtype) for t in fams],
        in_specs=[_HBM] * n, out_specs=[_HBM] * n,
        scratch_shapes=[pltpu.SemaphoreType.DMA((7 * n,)), pltpu.SemaphoreType.DMA((7 * n,))], name=name)(*fams)
    return [_place_own(o, t) for o, t in zip(outs, fams)]


def _swap_halves(units, *, name):
    n = len(units)

    def body(*refs):
        ins, outs = refs[:n], refs[n:2 * n]
        send_sems, recv_sems = refs[2 * n:]
        x, y, c = _my_pos()
        cps = [pltpu.make_async_remote_copy(src_ref=ins[f].at[1 - c], dst_ref=outs[f], send_sem=send_sems.at[f],
                                            recv_sem=recv_sems.at[f], device_id=(x, y, 1 - c), device_id_type=MESH)
               for f in range(n)]
        for cp in cps:
            cp.start()
        for cp in cps:
            cp.wait()

    return pl.pallas_call(
        body, out_shape=[jax.ShapeDtypeStruct(t.shape[1:], t.dtype) for t in units],
        in_specs=[_HBM] * n, out_specs=[_HBM] * n,
        scratch_shapes=[pltpu.SemaphoreType.DMA((n,)), pltpu.SemaphoreType.DMA((n,))], name=name)(*units)


def _send_to_chips(parts, *, name):
    n = len(parts)

    def body(*refs):
        ins, outs = refs[:n], refs[n:2 * n]
        send_sems, recv_sems = refs[2 * n:]
        x, y, c = _my_pos()
        chips = [(1 - x, y), (x, 1 - y), (1 - x, 1 - y)]
        cps = [pltpu.make_async_remote_copy(src_ref=ins[f].at[2 * cx + cy], dst_ref=outs[f].at[k],
                                            send_sem=send_sems.at[3 * f + k], recv_sem=recv_sems.at[3 * f + k],
                                            device_id=(cx, cy, c), device_id_type=MESH)
               for f in range(n) for k, (cx, cy) in enumerate(chips)]
        for cp in cps:
            cp.start()
        for cp in cps:
            cp.wait()

    return pl.pallas_call(
        body, out_shape=[jax.ShapeDtypeStruct((3,) + t.shape[1:], t.dtype) for t in parts],
        in_specs=[_HBM] * n, out_specs=[_HBM] * n,
        scratch_shapes=[pltpu.SemaphoreType.DMA((3 * n,)), pltpu.SemaphoreType.DMA((3 * n,))], name=name)(*parts)


def _pair_gather(halves, *, name):
    n = len(halves)

    def body(*refs):
        ins, outs = refs[:n], refs[n:2 * n]
        send_sems, recv_sems = refs[2 * n:]
        x, y, c = _my_pos()
        cps = [pltpu.make_async_remote_copy(src_ref=ins[f], dst_ref=outs[f].at[c], send_sem=send_sems.at[f],
                                            recv_sem=recv_sems.at[f], device_id=(x, y, 1 - c), device_id_type=MESH)
               for f in range(n)]
        for cp in cps:
            cp.start()
        for f in range(n):
            pltpu.make_async_remote_copy(src_ref=ins[f], dst_ref=outs[f].at[1 - c], send_sem=send_sems.at[f],
                                         recv_sem=recv_sems.at[f], device_id=(x, y, 1 - c),
                                         device_id_type=MESH).wait_recv()
        for cp in cps:
            cp.wait_send()

    outs = pl.pallas_call(
        body, out_shape=[jax.ShapeDtypeStruct((2,) + t.shape, t.dtype) for t in halves],
        in_specs=[_HBM] * n, out_specs=[_HBM] * n,
        scratch_shapes=[pltpu.SemaphoreType.DMA((n,)), pltpu.SemaphoreType.DMA((n,))], name=name)(*halves)
    c = lax.axis_index("c")
    return [lax.dynamic_update_index_in_dim(o, t, c, 0) for o, t in zip(outs, halves)]


_HBM_ONLY = pl.BlockSpec(memory_space=pltpu.HBM)
_SEMS = pl.BlockSpec(memory_space=pltpu.SEMAPHORE)
_EFFECT = pltpu.SideEffectType.DATAFLOW_SIDE_EFFECTING


def _copies_start(srcs, lands, plan, n_copies, *, name):
    n, m = len(srcs), len(lands)

    def body(*refs):
        src_refs, land_refs = refs[:n], refs[n:n + m]
        send_sems, recv_sems, token = refs[n + m], refs[n + m + 1], refs[-1]
        for k, (src, dst, peer) in enumerate(plan(src_refs, land_refs)):
            pltpu.make_async_remote_copy(src_ref=src, dst_ref=dst, send_sem=send_sems.at[k], recv_sem=recv_sems.at[k],
                                         device_id=peer, device_id_type=MESH).start()
        token[...] = jnp.zeros_like(token)

    bufs = [pltpu.with_memory_space_constraint(t, pltpu.HBM) for t in (*srcs, *lands)]
    outs = pl.pallas_call(
        body, name=name,
        out_shape=(pltpu.SemaphoreType.DMA((n_copies,)), pltpu.SemaphoreType.DMA((n_copies,)),
                   *[pltpu.HBM(t.shape, t.dtype) for t in bufs], jax.ShapeDtypeStruct((8, 128), F32)),
        in_specs=[_HBM_ONLY] * (n + m),
        out_specs=(_SEMS, _SEMS, *[_HBM_ONLY] * (n + m), pl.BlockSpec(memory_space=pltpu.VMEM)),
        input_output_aliases={k: 2 + k for k in range(n + m)},
        compiler_params=pltpu.CompilerParams(has_side_effects=_EFFECT))(*bufs)
    return outs[0], outs[1], list(outs[2:2 + n + m]), outs[-1]


def _copies_wait(send_sems, recv_sems, thru, n_src, plan, after, *, name):
    nm = len(thru)

    def body(*refs):
        t_refs, send, recv = refs[:nm], refs[nm], refs[nm + 1]
        for k, (src, dst, peer) in enumerate(plan(t_refs[:n_src], t_refs[n_src:])):
            cp = pltpu.make_async_remote_copy(src_ref=src, dst_ref=dst, send_sem=send.at[k], recv_sem=recv.at[k],
                                              device_id=peer, device_id_type=MESH)
            cp.wait_send()
            cp.wait_recv()

    outs = pl.pallas_call(
        body, name=name, out_shape=tuple(pltpu.HBM(t.shape, t.dtype) for t in thru),
        in_specs=[_HBM_ONLY] * nm + [_SEMS, _SEMS, pl.BlockSpec(memory_space=pl.ANY)],
        out_specs=tuple([_HBM_ONLY] * nm), input_output_aliases={k: k for k in range(nm)},
        compiler_params=pltpu.CompilerParams(has_side_effects=_EFFECT))(*thru, send_sems, recv_sems, after)
    return list(outs)


_RELATIONS = [(dx, dy, dc) for dx in (0, 1) for dy in (0, 1) for dc in (0, 1)][1:]


def _gather_plan(src_refs, land_refs):
    x, y, c = _my_pos()
    flip = lambda v, d: 1 - v if d else v
    return [(s_ref.at[c], l_ref.at[2 * x + y, c], (flip(x, dx), flip(y, dy), flip(c, dc)))
            for s_ref, l_ref in zip(src_refs, land_refs) for dx, dy, dc in _RELATIONS]


def _gather_chips_plan(src_refs, land_refs):
    x, y, c = _my_pos()
    peers = [(x, y, 1 - c), (1 - x, y, c), (x, 1 - y, c), (1 - x, 1 - y, c)]
    return [(s_ref.at[c], l_ref.at[2 * x + y, c], peer) for s_ref, l_ref in zip(src_refs, land_refs) for peer in peers]


def _gather_pass_plan(src_refs, land_refs):
    x, y, c = _my_pos()
    chips = [(1 - x, y), (x, 1 - y), (1 - x, 1 - y)]
    return [(l_ref.at[2 * cx + cy, c], l_ref.at[2 * cx + cy, c], (x, y, 1 - c))
            for l_ref in land_refs for cx, cy in chips]


def _sibling_plan(src_refs, land_refs):
    x, y, c = _my_pos()
    return [(s_ref.at[1 - c], l_ref, (x, y, 1 - c)) for s_ref, l_ref in zip(src_refs, land_refs)]


def _chips_plan(src_refs, land_refs):
    x, y, c = _my_pos()
    chips = [(1 - x, y), (x, 1 - y), (1 - x, 1 - y)]
    return [(s_ref.at[2 * cx + cy], l_ref.at[k], (cx, cy, c))
            for s_ref, l_ref in zip(src_refs, land_refs) for k, (cx, cy) in enumerate(chips)]


def _place_own(gathered, fam):
    x, y, c = _my_pos()
    own = lax.dynamic_index_in_dim(fam, c, 0, keepdims=True)[None]
    return lax.dynamic_update_slice(gathered, own, (2 * x + y, c) + (0,) * (fam.ndim - 1))


def _to_heads(t, width):
    return t.reshape(t.shape[0], HEADS, width).transpose(1, 0, 2)


def _from_heads(t):
    return t.transpose(1, 0, 2).reshape(t.shape[1], -1)


def _t5_bucket(dist):
    max_exact = N_BUCKETS // 2
    d = jnp.maximum(dist, 1).astype(F32)
    large = max_exact + (jnp.log(d / max_exact) / math.log(MAX_DISTANCE / max_exact)
                         * (N_BUCKETS - max_exact)).astype(jnp.int32)
    large = jnp.minimum(large, N_BUCKETS - 1)
    return jnp.where(dist < max_exact, dist, large)


def _bucket_map(dilation):
    iq = jnp.arange(DIL_BLOCK)[:, None]
    ik = jnp.arange(2 * DIL_BLOCK)[None, :]
    rel = DIL_BLOCK + iq - ik
    return _t5_bucket(jnp.maximum(rel, 0) * dilation).astype(jnp.int32)


def _q_perm(w):
    w3 = w.reshape(w.shape[0], HEADS, QK_NOPE + QK_ROPE)
    return jnp.concatenate([w3[:, :, :QK_NOPE].reshape(w.shape[0], -1),
                            w3[:, :, QK_NOPE:QK_NOPE + HALF_ROPE].reshape(w.shape[0], -1),
                            w3[:, :, QK_NOPE + HALF_ROPE:].reshape(w.shape[0], -1)], axis=1)


def _q_unperm(w):
    n0, n1 = HEADS * QK_NOPE, HEADS * HALF_ROPE
    r = w.shape[0]
    return jnp.concatenate([w[:, :n0].reshape(r, HEADS, QK_NOPE), w[:, n0:n0 + n1].reshape(r, HEADS, HALF_ROPE),
                            w[:, n0 + n1:].reshape(r, HEADS, HALF_ROPE)], axis=2).reshape(r, -1)


def _kv_perm(w):
    w3 = w.reshape(w.shape[0], HEADS, QK_NOPE + V_HEAD)
    return jnp.concatenate([w3[:, :, :QK_NOPE].reshape(w.shape[0], -1), w3[:, :, QK_NOPE:].reshape(w.shape[0], -1)],
                           axis=1)


def _kv_unperm(w):
    n0 = HEADS * QK_NOPE
    r = w.shape[0]
    return jnp.concatenate([w[:, :n0].reshape(r, HEADS, QK_NOPE), w[:, n0:].reshape(r, HEADS, V_HEAD)],
                           axis=2).reshape(r, -1)


def _row(v):
    return v.reshape(1, -1)


def kernel(x, c, norm_pre, norm_post, w_mod, b_mod, ffn_w_gate, ffn_w_up, ffn_w_down, mla_w_in, mla_q_norm, mla_w_q_up, mla_kv_norm, mla_w_kv_up, mla_w_o, dil_w_in, dil_w_o, rel_bias, loss_target, m_norm_pre, m_norm_post, m_w_mod, m_b_mod, m_ffn_w_gate, m_ffn_w_up, m_ffn_w_down, m_mla_w_in, m_mla_q_norm, m_mla_w_q_up, m_mla_kv_norm, m_mla_w_kv_up, m_mla_w_o, m_dil_w_in, m_dil_w_o, m_rel_bias, v_norm_pre, v_norm_post, v_w_mod, v_b_mod, v_ffn_w_gate, v_ffn_w_up, v_ffn_w_down, v_mla_w_in, v_mla_q_norm, v_mla_w_q_up, v_mla_kv_norm, v_mla_w_kv_up, v_mla_w_o, v_dil_w_in, v_dil_w_o, v_rel_bias):
    given = dict(locals())
    ix, iy, ic = _my_pos()
    shard_id = 2 * ix + iy
    dev_id = 4 * ix + 2 * iy + ic
    x2 = x[0]
    target = loss_target[0]
    half_idx = jnp.reshape(ic, (1,)).astype(jnp.int32)
    shard_idx = jnp.reshape(shard_id, (1,)).astype(jnp.int32)

    blk = jnp.zeros((8, D_MODEL), F32)
    blk = blk.at[0].set(c[0])
    blk = blk.at[1:3].set(jnp.pad(norm_pre.reshape(-1), (0, 512)).reshape(2, D_MODEL))
    blk = blk.at[3:5].set(jnp.pad(norm_post.reshape(-1), (0, 512)).reshape(2, D_MODEL))
    got = _all_gather(blk, name="ag_c_norms", in_vmem=True).reshape(N_SHARD, 2, 8, D_MODEL)
    c_all = got[:, :, 0, :].reshape(8, D_MODEL)

    def full_norm(lo):
        t = got[:, 0, lo:lo + 2, :].reshape(N_SHARD, 2 * D_MODEL)[:, :1536].reshape(N_SHARD, 2, 3, 256)
        return t.transpose(1, 2, 0, 3).reshape(2, 3, D_MODEL)

    pre_full, post_full = full_norm(1), full_norm(3)

    silu_c = _silu_bf16(c_all, name="silu_c")
    b_cols = lax.dynamic_slice_in_dim(b_mod, shard_id * 2304, 2304, axis=1).reshape(2, 1, 2304)
    mod_part = _mm(silu_c, w_mod, bias=b_cols, name="mod_mm", tn_cap=768)
    mod_all = _all_gather(mod_part.reshape(16, 2304), name="ag_mod", in_vmem=True)
    mod_all = mod_all.reshape(N_SHARD, 2, 2, 8, 2304)[:, 0]
    mod_mine = lax.dynamic_index_in_dim(mod_all, dev_id, axis=2, keepdims=False)
    mod = mod_mine.transpose(1, 0, 2).reshape(2, 9, D_MODEL)

    bf = lambda t: t.astype(BF16)
    ffn_fam = lambda i, h: [bf(jnp.stack([ffn_w_gate[i, h], ffn_w_up[i, h]])),
                            bf(ffn_w_down[i, h].reshape(2, F_SHARD // 2, D_MODEL))]
    mla_fam = [bf(mla_w_in.reshape(2, 128, -1)), bf(mla_w_q_up.reshape(2, 192, -1)),
               bf(mla_w_kv_up.reshape(2, 128, -1)), bf(mla_w_o.reshape(2, 128, D_MODEL))]
    dil_fam = [bf(dil_w_in.reshape(2, 512, -1)), bf(dil_w_o.reshape(2, 128, D_MODEL))]
    later_fams = [ffn_fam(0, 1), ffn_fam(1, 0) + dil_fam, ffn_fam(1, 1)]
    full, later_fams, mod = lax.optimization_barrier(
        (_gather_weights(ffn_fam(0, 0) + mla_fam, name="ag_weights_first"), later_fams, mod))

    def gather_later(fams, tag):
        lands = [lax.empty((N_SHARD,) + t.shape, t.dtype) for t in fams]
        send, recv, thru, token = _copies_start(fams, lands, _gather_plan, 7 * len(fams), name=f"ag_start_{tag}")
        return dict(send=send, recv=recv, thru=thru, token=token, n=len(fams), tag=tag)

    def arrive(st, after):
        thru = _copies_wait(st['send'], st['recv'], st['thru'], st['n'], _gather_plan, after,
                            name=f"ag_wait_{st['tag']}")
        return [_place_own(o, t) for t, o in zip(thru[:st['n']], thru[st['n']:])]

    def gather_chips(fams, tag):
        lands = [lax.empty((N_SHARD,) + t.shape, t.dtype) for t in fams]
        send, recv, thru, token = _copies_start(fams, lands, _gather_chips_plan, 4 * len(fams), name=f"ag_start_{tag}")
        return dict(send=send, recv=recv, thru=thru, token=token, n=len(fams), tag=tag)

    def pass_on(st, after):
        n, tag = st['n'], st['tag']
        thru = _copies_wait(st['send'], st['recv'], st['thru'], n, _gather_chips_plan, after, name=f"ag_mid_{tag}")
        send, recv, lands, token = _copies_start([], thru[n:], _gather_pass_plan, 3 * n, name=f"ag_pass_{tag}")
        return dict(send=send, recv=recv, thru=lands, fams=thru[:n], tag=tag), token[0, 0]

    def arrive_passed(st, after):
        lands = _copies_wait(st['send'], st['recv'], st['thru'], 0, _gather_pass_plan, after, name=f"ag_wait_{st['tag']}")
        return [_place_own(o, t) for t, o in zip(st['fams'], lands)]

    flight_a = gather_later(later_fams[0], "l0s2")
    _, next_fams = lax.optimization_barrier((flight_a['token'], later_fams[1]))
    flight_b = gather_chips(next_fams, "l1s01")
    as_ffn = lambda w_gu, w_dn: (w_gu, w_dn.reshape(N_SHARD, F_SHARD, D_MODEL))
    ffn_w = {(0, 0): as_ffn(full[0], full[1])}
    w_in = full[2].reshape(D_MODEL, -1)
    wq_p = _q_perm(full[3].reshape(N_SHARD, Q_LORA, -1).transpose(1, 0, 2).reshape(Q_LORA, -1))
    wkv_p = _kv_perm(full[4].reshape(N_SHARD, KV_LORA, -1).transpose(1, 0, 2).reshape(KV_LORA, -1))
    w_mo = full[5].reshape(D_MODEL, D_MODEL)
    dil_w = {}

    pos = jnp.arange(SEQ, dtype=F32)
    freqs = ROPE_THETA ** (-jnp.arange(HALF_ROPE, dtype=F32) / HALF_ROPE)
    ang = pos[:, None] * freqs[None, :]
    cos_k, sin_k = jnp.cos(ang), jnp.sin(ang)
    cos_q, sin_q = jnp.tile(cos_k, (1, HEADS)), jnp.tile(sin_k, (1, HEADS))

    buckets = [_bucket_map(d) for _, d in DIL_GROUPS]
    biases = [_bias_table(rel_bias[:, g * HEADS:(g + 1) * HEADS].T.reshape(HEADS, 1, N_BUCKETS), bk,
                          name=f"dil_bias_table_g{g}") for g, bk in enumerate(buckets)]

    def sub_params(i, sub):
        return dict(pg=_row(pre_full[i, sub]), qg=_row(post_full[i, sub]), sh=_row(mod[i, 3 * sub]),
                    sc=_row(mod[i, 3 * sub + 1]), gate=_row(mod[i, 3 * sub + 2]))

    def ffn_fwd(xin, i, h, sub, tie=None, mid=None):
        p = sub_params(i, sub)
        if tie is not None:
            p['sh'] = p['sh'] + tie
        tag = f"l{i}s{sub}"
        w_gu, w_dn = ffn_w[i, h]
        hn = _pre_fwd(xin, p['pg'], p['sc'], p['sh'], name=f"pre_fwd_{tag}")
        gu, a = _ffn_up(hn, w_gu, name=f"ffn_up_{tag}")
        if mid is not None:
            p['qg'] = p['qg'] + mid(a)
        f, out = _ffn_down(a, w_dn, xin, p['qg'], p['gate'], FFN_RES, name=f"ffn_down_{tag}")
        return out, dict(x=xin, hn=hn, gu=gu, a=a, f=f, p=p, i=i, h=h, tag=tag)

    def mla_fwd(xin, i, sub):
        p = sub_params(i, sub)
        tag = f"l{i}s{sub}"
        hn = _pre_fwd(xin, p['pg'], p['sc'], p['sh'], name=f"pre_fwd_{tag}")
        lat = _mm(hn, w_in, name="mla_lat")
        cq, ckv = lat[:, :Q_LORA], lat[:, Q_LORA:Q_LORA + KV_LORA]
        k1, k2 = lat[:, Q_LORA + KV_LORA:Q_LORA + KV_LORA + HALF_ROPE], lat[:, Q_LORA + KV_LORA + HALF_ROPE:]
        cqn = _rms_fwd(cq, mla_q_norm, name="mla_qnorm")
        ckvn = _rms_fwd(ckv, mla_kv_norm, name="mla_kvnorm")
        qp = _mm(cqn, wq_p, name="mla_q_up")
        kvp = _mm(ckvn, wkv_p, name="mla_kv_up")
        n0, n1 = HEADS * QK_NOPE, HEADS * HALF_ROPE
        qr1, qr2 = _rope(qp[:, n0:n0 + n1], qp[:, n0 + n1:], cos_q, sin_q, name="rope_q")
        kr1, kr2 = _rope(k1, k2, cos_k, sin_k, name="rope_k")
        q = jnp.concatenate([qp[:, :n0].reshape(SEQ, HEADS, QK_NOPE), qr1.reshape(SEQ, HEADS, HALF_ROPE),
                             qr2.reshape(SEQ, HEADS, HALF_ROPE)], axis=2).transpose(1, 0, 2).astype(BF16)
        kr = jnp.broadcast_to(jnp.concatenate([kr1, kr2], axis=1)[:, None, :], (SEQ, HEADS, QK_ROPE))
        k = jnp.concatenate([kvp[:, :n0].reshape(SEQ, HEADS, QK_NOPE), kr], axis=2).transpose(1, 0, 2).astype(BF16)
        v = _to_heads(kvp[:, n0:], V_HEAD).astype(BF16)
        o, lse = _mla_attn_fwd(q, k, v, name="mla_attn_fwd")
        o_flat = _from_heads(o).astype(BF16)
        f = _mm(o_flat, w_mo, name="mla_out")
        out = _post_fwd(f, xin, p['qg'], p['gate'], 1.0, name=f"post_fwd_{tag}")
        return out, dict(x=xin, hn=hn, cq=cq, ckv=ckv, cqn=cqn, ckvn=ckvn, q=q, k=k, v=v, o=o, lse=lse,
                         o_flat=o_flat, f=f, p=p, tag=tag)

    def dil_fwd(xin, i, sub):
        p = sub_params(i, sub)
        tag = f"l{i}s{sub}"
        hn = _pre_fwd(xin, p['pg'], p['sc'], p['sh'], name=f"pre_fwd_{tag}")
        heads = _proj_heads(hn, dil_w['in'], name="dil_proj")
        outs, lses = [], []
        for g, (window, d) in enumerate(DIL_GROUPS):
            o, lse = _dil_attn_fwd(heads, biases[g], g, d, name=f"dil_attn_fwd_g{g}")
            outs.append(o)
            lses.append(lse)
        mix = _dil_mix_fwd(outs, lses, name="dil_mix_fwd")
        o_flat = _from_heads(mix).astype(BF16)
        f = _mm(o_flat, dil_w['out'], name="dil_out")
        out = _post_fwd(f, xin, p['qg'], p['gate'], 1.0, name=f"post_fwd_{tag}")
        return out, dict(x=xin, hn=hn, heads=heads, outs=outs, lses=lses, o_flat=o_flat, f=f, p=p, tag=tag)

    saved = [None] * 6
    xs, saved[0] = ffn_fwd(x2, 0, 0, 0, tie=flight_a['token'][0, 0] + flight_b['token'][0, 0])
    xs, saved[1] = mla_fwd(xs, 0, 1)
    ffn_w[0, 1] = as_ffn(*arrive(flight_a, xs))
    passed = {}

    def second_step(after):
        passed['st'], tok = pass_on(flight_b, after)
        return tok

    xs, saved[2] = ffn_fwd(xs, 0, 1, 2, mid=second_step)
    got, last_fams = lax.optimization_barrier((arrive_passed(passed['st'], xs), later_fams[2]))
    ffn_w[1, 0] = as_ffn(got[0], got[1])
    dil_w['in'], dil_w['out'] = got[2].reshape(N_SHARD, D_MODEL, -1), got[3].reshape(D_MODEL, D_MODEL)
    in_flight = gather_later(last_fams, "l1s2")
    xs, saved[3] = ffn_fwd(xs, 1, 0, 0, tie=in_flight['token'][0, 0])
    xs, saved[4] = dil_fwd(xs, 1, 1)
    ffn_w[1, 1] = as_ffn(*arrive(in_flight, xs))
    xs, saved[5] = ffn_fwd(xs, 1, 1, 2)

    dx, loss_part = _loss(xs, target, name="loss")

    dmod = [[None] * 9 for _ in range(2)]
    dpre = [[None] * 3 for _ in range(2)]
    dpost = [[None] * 3 for _ in range(2)]
    ffn_units = {}
    row_unit = lambda g, r, j: ((r % 2, r // 2), 0, j)

    def close_sub(dhn, dout, sv, i, sub, res_dgate, res_dqg):
        p = sv['p']
        dxs, dsh, dsc, dpg = _pre_bwd(dhn, sv['x'], dout, p['pg'], p['sc'], name=f"pre_bwd_{sv['tag']}")
        dmod[i][3 * sub], dmod[i][3 * sub + 1], dmod[i][3 * sub + 2] = dsh, dsc, res_dgate
        dpre[i][sub], dpost[i][sub] = dpg, res_dqg
        return dxs

    def ffn_bwd(dout, sv, sub, tie=0.0):
        i, h, p, tag = sv['i'], sv['h'], sv['p'], sv['tag']
        w_gu, w_dn = ffn_w[i, h]
        df, dgate, dqg = _post_bwd(dout, sv['f'], p['qg'] + tie, p['gate'], FFN_RES, name=f"post_bwd_{tag}")
        u_dn = _mm(sv['a'], df, ta=True, tn_cap=D_MODEL // 2, out_shape=(2, N_SHARD, F_SHARD, D_MODEL // 2),
                   out_sel=lambda g, r, j: ((j, g), r, 0), name=f"ffn_dwd_{tag}")
        dgu = _ffn_dgu(df, w_dn, sv['gu'], name=f"ffn_dgu_{tag}")
        dgu = dgu.reshape(2 * N_SHARD, SEQ, F_SHARD)
        u_gu = _mm(dgu, sv['hn'], ta=True, out_shape=(2, N_SHARD, F_SHARD, D_MODEL),
                   out_sel=lambda g, r, j: ((g % 2, g // 2), r, j), name=f"ffn_dwgu_{tag}")
        ffn_units[i, h] = [u_gu, u_dn]
        dxs, dsh, dsc, dpg = _ffn_dhn(dgu, w_gu.reshape(2 * N_SHARD, D_MODEL, F_SHARD), sv['x'], dout, p['pg'], p['sc'],
                                      name=f"ffn_dhn_{tag}")
        dmod[i][3 * sub], dmod[i][3 * sub + 1], dmod[i][3 * sub + 2] = dsh, dsc, dgate
        dpre[i][sub], dpost[i][sub] = dpg, dqg
        return dxs

    def mla_bwd(dout, sv, i, sub, tie=0.0):
        p, tag = sv['p'], sv['tag']
        df, dgate, dqg = _post_bwd(dout, sv['f'], p['qg'] + tie, p['gate'], 1.0, name=f"post_bwd_{tag}")
        u_wo = _mm(sv['o_flat'], df, ta=True, tm_cap=128, out_shape=(2, N_SHARD, 128, D_MODEL), out_sel=row_unit,
                   name="mla_dwo")
        do_flat = _mm(df, w_mo, tb=True, name="mla_do")
        do = _to_heads(do_flat, V_HEAD)
        dq, dk, dv = _mla_attn_bwd(sv['q'], sv['k'], sv['v'], sv['o'], do, sv['lse'], name="mla_attn_bwd")
        dq_t = dq.transpose(1, 0, 2)
        dqr1, dqr2 = _rope(dq_t[:, :, QK_NOPE:QK_NOPE + HALF_ROPE].reshape(SEQ, -1),
                           dq_t[:, :, QK_NOPE + HALF_ROPE:].reshape(SEQ, -1), cos_q, -sin_q, name="rope_q_bwd")
        dqp = jnp.concatenate([dq_t[:, :, :QK_NOPE].reshape(SEQ, -1), dqr1, dqr2], axis=1).astype(BF16)
        dkr = _head_sum(dk[:, :, QK_NOPE:], name="mla_dkr_sum")
        dk1, dk2 = _rope(dkr[:, :HALF_ROPE], dkr[:, HALF_ROPE:], cos_k, -sin_k, name="rope_k_bwd")
        dkvp = jnp.concatenate([_from_heads(dk[:, :, :QK_NOPE]), _from_heads(dv)], axis=1).astype(BF16)
        g_wq = _q_unperm(_mm(sv['cqn'], dqp, ta=True, name="mla_dwq"))
        g_wkv = _kv_unperm(_mm(sv['ckvn'], dkvp, ta=True, name="mla_dwkv"))
        dcqn = _mm(dqp, wq_p, tb=True, name="mla_dcqn")
        dckvn = _mm(dkvp, wkv_p, tb=True, name="mla_dckvn")
        dcq, g_qn = _rms_bwd(dcqn, sv['cq'], mla_q_norm, name="mla_qnorm_bwd")
        dckv, g_kvn = _rms_bwd(dckvn, sv['ckv'], mla_kv_norm, name="mla_kvnorm_bwd")
        dlat = jnp.concatenate([dcq, dckv, dk1, dk2], axis=1).astype(BF16)
        u_win = _mm(sv['hn'], dlat, ta=True, tm_cap=128, out_shape=(2, N_SHARD, 128, dlat.shape[1]),
                    out_sel=row_unit, name="mla_dwin")
        dhn = _mm(dlat, w_in, tb=True, name="mla_dhn")
        col_unit = lambda t: (t.reshape(t.shape[0], N_SHARD, -1).transpose(1, 0, 2)
                              .reshape(N_SHARD, 2, t.shape[0] // 2, -1).transpose(1, 0, 2, 3))
        grads = dict(units=[u_win, col_unit(g_wq), col_unit(g_wkv), u_wo], q_norm=g_qn, kv_norm=g_kvn)
        return close_sub(dhn, dout, sv, i, sub, dgate, dqg), grads

    def dil_bwd(dout, sv, i, sub):
        p, tag = sv['p'], sv['tag']
        df, dgate, dqg = _post_bwd(dout, sv['f'], p['qg'], p['gate'], 1.0, name=f"post_bwd_{tag}")
        u_wo = _mm(sv['o_flat'], df, ta=True, tm_cap=128, out_shape=(2, N_SHARD, 128, D_MODEL), out_sel=row_unit,
                   name="dil_dwo")
        do = _to_heads(_mm(df, dil_w['out'], tb=True, name="dil_do"), 64)
        dos, dlts = _dil_mix_bwd(do, sv['outs'], sv['lses'], name="dil_mix_bwd")
        pieces = []
        bias_rows = []
        for g, (window, d) in enumerate(DIL_GROUPS):
            dq, dk, dv, dbias = _dil_attn_bwd(sv['heads'], biases[g], sv['lses'][g], dos[g], dlts[g], g, d,
                                              name=f"dil_attn_bwd_g{g}")
            pieces += [dq, dk, dv]
            bias_rows.append(_bias_grad(dbias, buckets[g], name=f"dil_bias_grad_g{g}")[:, 0, :])
        dheads = jnp.concatenate(pieces).astype(BF16)
        u_win = _proj_heads_dw(sv['hn'], dheads, name="dil_dwin")
        dhn = _proj_heads_dx(dheads, dil_w['in'], name="dil_dhn")
        g_bias = jnp.concatenate(bias_rows, axis=0).T
        grads = dict(units=[u_win, u_wo], rel_bias=g_bias)
        return close_sub(dhn, dout, sv, i, sub, dgate, dqg), grads

    def to_sibling(units, tag):
        n = len(units)
        send, recv, thru, token = _copies_start(units, [lax.empty(u.shape[1:], F32) for u in units], _sibling_plan, n,
                                                name=f"rs{tag}_sibling_start")
        return dict(send=send, recv=recv, thru=thru, n=n, tag=tag), token[0, 0]

    def from_sibling(st, after):
        n, tag = st['n'], st['tag']
        thru = _copies_wait(st['send'], st['recv'], st['thru'], n, _sibling_plan, after, name=f"rs{tag}_sibling_wait")
        return [_add_half(u, g, half_idx, name=f"rs{tag}_add_half_{k}") for k, (u, g) in enumerate(zip(thru[:n], thru[n:]))]

    def to_chips(parts, tag):
        n = len(parts)
        send, recv, thru, token = _copies_start([w for _, w in parts],
                                                [lax.empty((3,) + w.shape[1:], BF16) for _, w in parts], _chips_plan,
                                                3 * n, name=f"rs{tag}_chips_start")
        return dict(send=send, recv=recv, thru=thru, n=n, tag=tag, parts=parts), token[0, 0]

    def from_chips(st, after):
        n, tag = st['n'], st['tag']
        thru = _copies_wait(st['send'], st['recv'], st['thru'], n, _chips_plan, after, name=f"rs{tag}_chips_wait")
        return [_add_shards(p, g, shard_idx, name=f"rs{tag}_add_shards_{k}")
                for k, ((p, _), g) in enumerate(zip(st['parts'], thru[n:]))]

    dx = ffn_bwd(dx, saved[5], 2)
    dx, dil_g = dil_bwd(dx, saved[4], 1, 1)
    dx = ffn_bwd(dx, saved[3], 0)
    st1, tok = to_sibling([*ffn_units[1, 1], *dil_g['units'], *ffn_units[1, 0]], "1")
    dx = ffn_bwd(dx, saved[2], 2, tie=tok)
    st1, tok1 = to_chips(from_sibling(st1, dx), "1")
    st2, tok2 = to_sibling(ffn_units[0, 1], "2")
    dx, mla_g = mla_bwd(dx, saved[1], 0, 1, tie=tok1 + tok2)
    reds1 = from_chips(st1, dx)
    st2, tok = to_chips(from_sibling(st2, dx), "2")
    dx = ffn_bwd(dx, saved[0], 0, tie=tok)
    reds2 = from_chips(st2, dx)
    grad_x = dx[None]

    pad_row = lambda v: jnp.pad(v.reshape(-1), (0, (-v.size) % D_MODEL)).reshape(-1, D_MODEL)
    small = jnp.concatenate(
        [jnp.concatenate([dmod[i][r] for i in range(2) for r in range(9)], axis=0),
         jnp.concatenate([dpre[i][s] for i in range(2) for s in range(3)], axis=0),
         jnp.concatenate([dpost[i][s] for i in range(2) for s in range(3)], axis=0),
         pad_row(mla_g['q_norm']), pad_row(mla_g['kv_norm']), pad_row(dil_g['rel_bias']), pad_row(loss_part)], axis=0)
    small = jnp.pad(small, ((0, SMALL_ROWS - small.shape[0]), (0, 0)))
    small_all = _all_gather(small, name="ag_small_grads", in_vmem=True)
    small_sum = _sum_devices(small_all, 8, name="sum_small_grads")
    g_b_mod = small_sum[0:18].reshape(2, 9 * D_MODEL)
    my_cols = lambda t: lax.dynamic_slice_in_dim(t, shard_id * 256, 256, axis=2)
    g_norm_pre = my_cols(small_sum[18:24].reshape(2, 3, D_MODEL))
    g_norm_post = my_cols(small_sum[24:30].reshape(2, 3, D_MODEL))
    g_q_norm = small_sum[30, :Q_LORA].reshape(1, Q_LORA)
    g_kv_norm = small_sum[31, :KV_LORA].reshape(1, KV_LORA)
    g_rel_bias = small_sum[32:34].reshape(-1)[:N_BUCKETS * 48].reshape(N_BUCKETS, 48)
    loss = small_sum[34, 0]
    dmod_all = small_all.reshape(8, SMALL_ROWS, D_MODEL)[:, 0:18].reshape(8, 2, 9 * D_MODEL)
    dmod_cols = lax.dynamic_slice_in_dim(dmod_all, shard_id * 2304, 2304, axis=2).transpose(1, 0, 2)
    g_w_mod = _mm(silu_c, dmod_cols.astype(BF16), ta=True, tn_cap=768, name="w_mod_grad")

    units0 = [*mla_g['units'], *ffn_units[0, 0]]
    got_a = _swap_halves(units0, name="rs0_sibling")
    parts0 = [_add_half(u, g, half_idx, name=f"rs0_add_half_{k}") for k, (u, g) in enumerate(zip(units0, got_a))]
    got_b = _send_to_chips([w for _, w in parts0], name="rs0_chips")
    reds0 = [_add_shards(p, g, shard_idx, name=f"rs0_add_shards_{k}")
             for k, ((p, _), g) in enumerate(zip(parts0, got_b))]
    fin = _pair_gather(reds1 + reds2 + reds0, name="rs_pair_gather")
    ffn_fin = {(1, 1): fin[0:2], (1, 0): fin[4:6], (0, 1): fin[6:8], (0, 0): fin[12:14]}
    swap = lambda t: jnp.swapaxes(t, 2, 3)
    per_ffn = lambda pick: jnp.stack([jnp.stack([pick(*ffn_fin[i, h]) for h in range(2)]) for i in range(2)])
    reduced = dict(ffn_w_gate=swap(per_ffn(lambda gu, dn: gu[0])), ffn_w_up=swap(per_ffn(lambda gu, dn: gu[1])),
                   ffn_w_down=per_ffn(lambda gu, dn: jnp.concatenate([dn[0], dn[1]], axis=1)))
    for n, t in zip(['dil_w_in', 'dil_w_o', 'mla_w_in', 'mla_w_q_up', 'mla_w_kv_up', 'mla_w_o'], fin[2:4] + fin[8:12]):
        reduced[n] = t.reshape(given[n].shape)

    grads = dict(norm_pre=g_norm_pre, norm_post=g_norm_post, w_mod=g_w_mod, b_mod=g_b_mod, mla_q_norm=g_q_norm,
                 mla_kv_norm=g_kv_norm, rel_bias=g_rel_bias, **reduced)

    deltas, new_m, new_v = {}, {}, {}
    for n in WEIGHTS:
        view = swap if n in ('ffn_w_gate', 'ffn_w_up') else (lambda t: t)
        outs = _adamw(view(given[n]), view(grads[n]), view(given["m_" + n]), view(given["v_" + n]), name=f"adamw_{n}")
        deltas[n], new_m[n], new_v[n] = (view(t) for t in outs)
    return (loss, grad_x, *[grads[n] for n in WEIGHTS], *[deltas[n] for n in WEIGHTS],
            *[new_m[n] for n in WEIGHTS], *[new_v[n] for n in WEIGHTS])
```

```python
import math

import jax
import jax.numpy as jnp
from jax import lax
from jax.experimental import pallas as pl
from jax.experimental.pallas import tpu as pltpu

F32 = jnp.float32
BF16 = jnp.bfloat16
MESH = pl.DeviceIdType.MESH

SEQ = 2048
D_MODEL = 1024
D_FF = 2816
N_SHARD = 4
F_SHARD = D_FF // N_SHARD
EPS = 1e-6
FFN_RES = 0.5
HEADS = 16
Q_LORA, KV_LORA, QK_NOPE, QK_ROPE, V_HEAD = 384, 256, 64, 32, 64
HALF_ROPE = QK_ROPE // 2
ROPE_THETA = 10000.0
DIL_GROUPS = ((128, 1), (512, 4), (2048, 16))
DIL_BLOCK = 128
N_BUCKETS = 32
MAX_DISTANCE = 2048
ADAM_LR, ADAM_B1, ADAM_B2, ADAM_EPS, ADAM_WD, ADAM_STEP = 0.001, 0.9, 0.999, 1e-08, 0.01, 10

VMEM_LIMIT = 48 * 1024 * 1024
SMALL_ROWS = 40

WEIGHTS = ['norm_pre', 'norm_post', 'w_mod', 'b_mod', 'ffn_w_gate', 'ffn_w_up', 'ffn_w_down', 'mla_w_in',
           'mla_q_norm', 'mla_w_q_up', 'mla_kv_norm', 'mla_w_kv_up', 'mla_w_o', 'dil_w_in', 'dil_w_o', 'rel_bias']


def _cparams(**kw):
    return pltpu.CompilerParams(vmem_limit_bytes=VMEM_LIMIT, **kw)


def _pick(n, cap, mult=128):
    if n <= cap:
        return n
    best = n
    for t in range(mult, cap + 1, mult):
        if n % t == 0:
            best = t
    return best


def _mm(a, b, *, name, ta=False, tb=False, reduce_g=False, bias=None, out_dtype=F32, tm_cap=512, tn_cap=1024,
        g_n=None, b_sel=None, out_shape=None, out_sel=None, out_buf=None):
    a3 = a if a.ndim == 3 else a[None]
    ga = a3.shape[0]
    if b_sel is None:
        b_n = b if b.ndim == 3 else b[None]
        gb = b_n.shape[0]
        b_sel = (lambda g: (g,)) if gb > 1 else (lambda g: (0,))
        g_n = max(ga, gb)
    else:
        b_n = b
    k_dim, m_dim = (a3.shape[1], a3.shape[2]) if ta else (a3.shape[2], a3.shape[1])
    k2, n_dim = (b_n.shape[-1], b_n.shape[-2]) if tb else (b_n.shape[-2], b_n.shape[-1])
    assert k_dim == k2, (a.shape, b.shape)
    tm = _pick(m_dim, tm_cap, 128 if ta else 8)
    tn = _pick(n_dim, tn_cap, 128)
    mt, nt = m_dim // tm, n_dim // tn
    dims = (((0 if ta else 1,), (1 if tb else 0,)), ((), ()))

    if reduce_g:
        grid = (mt, nt, g_n)
        ids = lambda i, j, g: (g, i, j)
    else:
        grid = (g_n, mt, nt)
        ids = lambda g, i, j: (g, i, j)

    def a_map(*p):
        g, i, j = ids(*p)
        g = g if ga > 1 else 0
        return (g, 0, i) if ta else (g, i, 0)

    def b_map(*p):
        g, i, j = ids(*p)
        return (*b_sel(g), j, 0) if tb else (*b_sel(g), 0, j)

    b_lead = (None,) * (b_n.ndim - 2)
    a_spec = pl.BlockSpec((None, k_dim, tm) if ta else (None, tm, k_dim), a_map)
    b_spec = pl.BlockSpec(b_lead + ((tn, k_dim) if tb else (k_dim, tn)), b_map)
    in_specs = [a_spec, b_spec]
    operands = [a3, b_n]
    if bias is not None:
        assert not reduce_g and bias.shape == (g_n, 1, n_dim)
        in_specs.append(pl.BlockSpec((None, 1, tn), lambda g, i, j: (g, 0, j)))
        operands.append(bias)
    aliases = {}
    if out_buf is not None:
        assert tuple(out_buf.shape) == tuple(out_shape) and out_buf.dtype == out_dtype
        in_specs.append(pl.BlockSpec(memory_space=pl.ANY))
        operands.append(out_buf)
        aliases = {len(operands) - 1: 0}

    if reduce_g:
        out_spec = pl.BlockSpec((tm, tn), lambda i, j, g: (i, j))
        out_sds = jax.ShapeDtypeStruct((m_dim, n_dim), F32)
    elif out_shape is not None:
        def o_map(g, i, j):
            lead, rb, cb = out_sel(g, i, j)
            return (*lead, rb, cb)

        out_spec = pl.BlockSpec((None,) * (len(out_shape) - 2) + (tm, tn), o_map)
        out_sds = jax.ShapeDtypeStruct(tuple(out_shape), out_dtype)
    else:
        out_spec = pl.BlockSpec((None, tm, tn), lambda g, i, j: (g, i, j))
        out_sds = jax.ShapeDtypeStruct((g_n, m_dim, n_dim), out_dtype)

    def body(a_ref, b_ref, *rest):
        o_ref = rest[-1]
        r = lax.dot_general(a_ref[...].astype(BF16), b_ref[...].astype(BF16), dims, preferred_element_type=F32)
        if bias is not None:
            r = r + rest[0][...]
        if reduce_g:
            g = pl.program_id(2)

            @pl.when(g == 0)
            def _():
                o_ref[...] = r

            @pl.when(g > 0)
            def _():
                o_ref[...] += r
        else:
            o_ref[...] = r.astype(o_ref.dtype)

    out = pl.pallas_call(body, grid=grid, in_specs=in_specs, out_specs=out_spec, out_shape=out_sds,
                         input_output_aliases=aliases, compiler_params=_cparams(), name=name)(*operands)
    if not reduce_g and out_shape is None and a.ndim == 2 and b.ndim == 2:
        out = out[0]
    return out


def _rows(tm, w):
    return pl.BlockSpec((tm, w), lambda i: (i, 0))


def _vec(w):
    return pl.BlockSpec((1, w), lambda i: (0, 0))


def _rstd(v):
    return lax.rsqrt(jnp.mean(v * v, axis=-1, keepdims=True) + EPS)


def _pre_fwd(x, pg, sc, sh, *, name):
    s_n, w = x.shape
    tm = _pick(s_n, 512, 8)

    def body(x_ref, pg_ref, sc_ref, sh_ref, o_ref):
        xv = x_ref[...]
        n = (xv * _rstd(xv)) * pg_ref[...]
        o_ref[...] = (n * (1.0 + sc_ref[...]) + sh_ref[...]).astype(o_ref.dtype)

    return pl.pallas_call(body, grid=(s_n // tm,), in_specs=[_rows(tm, w), _vec(w), _vec(w), _vec(w)],
                          out_specs=_rows(tm, w), out_shape=jax.ShapeDtypeStruct((s_n, w), BF16),
                          compiler_params=_cparams(), name=name)(x, pg, sc, sh)


def _post_fwd(f, x, qg, gate, res_w, *, name):
    s_n, w = x.shape
    tm = _pick(s_n, 512, 8)

    def body(f_ref, x_ref, qg_ref, gate_ref, o_ref):
        fv = f_ref[...]
        y = (fv * _rstd(fv)) * qg_ref[...]
        o_ref[...] = x_ref[...] + (res_w * gate_ref[...]) * y

    return pl.pallas_call(body, grid=(s_n // tm,), in_specs=[_rows(tm, w), _rows(tm, w), _vec(w), _vec(w)],
                          out_specs=_rows(tm, w), out_shape=jax.ShapeDtypeStruct((s_n, w), F32),
                          compiler_params=_cparams(), name=name)(f, x, qg, gate)


def _post_bwd(dout, f, qg, gate, res_w, *, name):
    s_n, w = f.shape
    tm = _pick(s_n, 512, 8)

    def body(do_ref, f_ref, qg_ref, gate_ref, df_ref, dgate_ref, dqg_ref):
        @pl.when(pl.program_id(0) == 0)
        def _():
            dgate_ref[...] = jnp.zeros_like(dgate_ref)
            dqg_ref[...] = jnp.zeros_like(dqg_ref)

        do = do_ref[...]
        fv = f_ref[...]
        r = _rstd(fv)
        fh = fv * r
        qg_v = qg_ref[...]
        dgate_ref[...] += res_w * jnp.sum(do * (fh * qg_v), axis=0, keepdims=True)
        dy = do * (res_w * gate_ref[...])
        dqg_ref[...] += jnp.sum(dy * fh, axis=0, keepdims=True)
        dfh = dy * qg_v
        df = r * (dfh - fh * jnp.mean(dfh * fh, axis=-1, keepdims=True))
        df_ref[...] = df.astype(df_ref.dtype)

    return pl.pallas_call(
        body, grid=(s_n // tm,), in_specs=[_rows(tm, w), _rows(tm, w), _vec(w), _vec(w)],
        out_specs=[_rows(tm, w), _vec(w), _vec(w)],
        out_shape=[jax.ShapeDtypeStruct((s_n, w), BF16), jax.ShapeDtypeStruct((1, w), F32),
                   jax.ShapeDtypeStruct((1, w), F32)],
        compiler_params=_cparams(), name=name)(dout, f, qg, gate)


def _pre_bwd(dhn, x, dout, pg, sc, *, name):
    s_n, w = x.shape
    tm = _pick(s_n, 512, 8)

    def body(dhn_ref, x_ref, do_ref, pg_ref, sc_ref, dx_ref, dsh_ref, dsc_ref, dpg_ref):
        @pl.when(pl.program_id(0) == 0)
        def _():
            dsh_ref[...] = jnp.zeros_like(dsh_ref)
            dsc_ref[...] = jnp.zeros_like(dsc_ref)
            dpg_ref[...] = jnp.zeros_like(dpg_ref)

        dhn_v = dhn_ref[...]
        xv = x_ref[...]
        r = _rstd(xv)
        xh = xv * r
        pg_v = pg_ref[...]
        dsh_ref[...] += jnp.sum(dhn_v, axis=0, keepdims=True)
        dsc_ref[...] += jnp.sum(dhn_v * (xh * pg_v), axis=0, keepdims=True)
        dn = dhn_v * (1.0 + sc_ref[...])
        dpg_ref[...] += jnp.sum(dn * xh, axis=0, keepdims=True)
        dxh = dn * pg_v
        dx_ref[...] = do_ref[...] + r * (dxh - xh * jnp.mean(dxh * xh, axis=-1, keepdims=True))

    vec = jax.ShapeDtypeStruct((1, w), F32)
    return pl.pallas_call(
        body, grid=(s_n // tm,), in_specs=[_rows(tm, w), _rows(tm, w), _rows(tm, w), _vec(w), _vec(w)],
        out_specs=[_rows(tm, w), _vec(w), _vec(w), _vec(w)],
        out_shape=[jax.ShapeDtypeStruct((s_n, w), F32), vec, vec, vec],
        compiler_params=_cparams(), name=name)(dhn, x, dout, pg, sc)


def _rms_fwd(x, g, *, name):
    s_n, w = x.shape
    tm = _pick(s_n, 512, 8)

    def body(x_ref, g_ref, o_ref):
        xv = x_ref[...]
        o_ref[...] = ((xv * _rstd(xv)) * g_ref[...]).astype(o_ref.dtype)

    return pl.pallas_call(body, grid=(s_n // tm,), in_specs=[_rows(tm, w), _vec(w)], out_specs=_rows(tm, w),
                          out_shape=jax.ShapeDtypeStruct((s_n, w), BF16), compiler_params=_cparams(),
                          name=name)(x, g)


def _rms_bwd(dy, x, g, *, name):
    s_n, w = x.shape
    tm = _pick(s_n, 512, 8)

    def body(dy_ref, x_ref, g_ref, dx_ref, dg_ref):
        @pl.when(pl.program_id(0) == 0)
        def _():
            dg_ref[...] = jnp.zeros_like(dg_ref)

        dy_v = dy_ref[...]
        xv = x_ref[...]
        r = _rstd(xv)
        xh = xv * r
        dg_ref[...] += jnp.sum(dy_v * xh, axis=0, keepdims=True)
        dxh = dy_v * g_ref[...]
        dx_ref[...] = r * (dxh - xh * jnp.mean(dxh * xh, axis=-1, keepdims=True))

    return pl.pallas_call(
        body, grid=(s_n // tm,), in_specs=[_rows(tm, w), _rows(tm, w), _vec(w)],
        out_specs=[_rows(tm, w), _vec(w)],
        out_shape=[jax.ShapeDtypeStruct((s_n, w), F32), jax.ShapeDtypeStruct((1, w), F32)],
        compiler_params=_cparams(), name=name)(dy, x, g)


def _rope(a1, a2, cos, sin, *, name):
    s_n, w = a1.shape
    tm = _pick(s_n, 512, 8)

    def body(a1_ref, a2_ref, c_ref, s_ref, r1_ref, r2_ref):
        u, v, c_v, s_v = a1_ref[...], a2_ref[...], c_ref[...], s_ref[...]
        r1_ref[...] = u * c_v - v * s_v
        r2_ref[...] = u * s_v + v * c_v

    sd = jax.ShapeDtypeStruct((s_n, w), F32)
    return pl.pallas_call(body, grid=(s_n // tm,), in_specs=[_rows(tm, w)] * 4, out_specs=[_rows(tm, w)] * 2,
                          out_shape=[sd, sd], compiler_params=_cparams(), name=name)(a1, a2, cos, sin)


def _silu_bf16(x, *, name):
    def body(x_ref, o_ref):
        xv = x_ref[...]
        o_ref[...] = (xv * jax.nn.sigmoid(xv)).astype(o_ref.dtype)

    return pl.pallas_call(body, out_shape=jax.ShapeDtypeStruct(x.shape, BF16), name=name)(x)


def _loss(y, target, *, name):
    s_n, w = y.shape
    tm = _pick(s_n, 512, 8)

    def body(y_ref, t_ref, dy_ref, l_ref):
        @pl.when(pl.program_id(0) == 0)
        def _():
            l_ref[...] = jnp.zeros_like(l_ref)

        e = y_ref[...] - t_ref[...]
        dy_ref[...] = e * (1.0 / w)
        row = jnp.mean(e * e, axis=-1, keepdims=True)
        l_ref[...] += 0.5 * jnp.sum(row, axis=0, keepdims=True)

    return pl.pallas_call(
        body, grid=(s_n // tm,), in_specs=[_rows(tm, w), _rows(tm, w)],
        out_specs=[_rows(tm, w), pl.BlockSpec((1, 1), lambda i: (0, 0))],
        out_shape=[jax.ShapeDtypeStruct((s_n, w), F32), jax.ShapeDtypeStruct((1, 1), F32)],
        compiler_params=_cparams(), name=name)(y, target)


FFN_TM = 512


def _ffn_up(hn, w_gu, *, name):
    s_n, d = hn.shape
    f = w_gu.shape[-1]
    tm = _pick(s_n, FFN_TM, 8)

    def body(hn_ref, wg_ref, wu_ref, gu_ref, a_ref):
        xv = hn_ref[...]
        g = jnp.dot(xv, wg_ref[...], preferred_element_type=F32)
        u = jnp.dot(xv, wu_ref[...], preferred_element_type=F32)
        gu_ref[0] = g.astype(BF16)
        gu_ref[1] = u.astype(BF16)
        a_ref[...] = ((g * jax.nn.sigmoid(g)) * u).astype(BF16)

    w_blk = lambda t: pl.BlockSpec((None, None, d, f), lambda s, m: (s, t, 0, 0))
    return pl.pallas_call(
        body, grid=(N_SHARD, s_n // tm),
        in_specs=[pl.BlockSpec((tm, d), lambda s, m: (m, 0)), w_blk(0), w_blk(1)],
        out_specs=[pl.BlockSpec((None, 2, tm, f), lambda s, m: (s, 0, m, 0)),
                   pl.BlockSpec((None, tm, f), lambda s, m: (s, m, 0))],
        out_shape=[jax.ShapeDtypeStruct((N_SHARD, 2, s_n, f), BF16), jax.ShapeDtypeStruct((N_SHARD, s_n, f), BF16)],
        compiler_params=_cparams(), name=name)(hn, w_gu, w_gu)


def _ffn_down(a, w_dn, x, qg, gate, res_w, *, name):
    g_n, s_n, f = a.shape
    d = w_dn.shape[-1]
    tm = _pick(s_n, FFN_TM, 8)

    def body(a_ref, w_ref, x_ref, qg_ref, gate_ref, f_ref, o_ref):
        g = pl.program_id(1)
        r = jnp.dot(a_ref[...], w_ref[...], preferred_element_type=F32)

        @pl.when(g == 0)
        def _():
            f_ref[...] = r

        @pl.when(g > 0)
        def _():
            f_ref[...] += r

        @pl.when(g == g_n - 1)
        def _():
            fv = f_ref[...]
            y = (fv * _rstd(fv)) * qg_ref[...]
            o_ref[...] = x_ref[...] + (res_w * gate_ref[...]) * y

    row = pl.BlockSpec((tm, d), lambda m, g: (m, 0))
    vec = pl.BlockSpec((1, d), lambda m, g: (0, 0))
    sd = jax.ShapeDtypeStruct((s_n, d), F32)
    return pl.pallas_call(
        body, grid=(s_n // tm, g_n),
        in_specs=[pl.BlockSpec((None, tm, f), lambda m, g: (g, m, 0)), pl.BlockSpec((None, f, d), lambda m, g: (g, 0, 0)),
                  row, vec, vec],
        out_specs=[row, row], out_shape=[sd, sd], compiler_params=_cparams(), name=name)(a, w_dn, x, qg, gate)


def _ffn_dhn(dgu, w_gu, x, dout, pg, sc, *, name):
    g_n, s_n, f = dgu.shape
    d = w_gu.shape[-2]
    tm = _pick(s_n, FFN_TM, 8)

    def body(a_ref, w_ref, x_ref, do_ref, pg_ref, sc_ref, dx_ref, dsh_ref, dsc_ref, dpg_ref, acc_ref):
        m, g = pl.program_id(0), pl.program_id(1)
        r = lax.dot_general(a_ref[...], w_ref[...], (((1,), (1,)), ((), ())), preferred_element_type=F32)

        @pl.when(g == 0)
        def _():
            acc_ref[...] = r

        @pl.when(g > 0)
        def _():
            acc_ref[...] += r

        @pl.when((m == 0) & (g == 0))
        def _():
            dsh_ref[...] = jnp.zeros_like(dsh_ref)
            dsc_ref[...] = jnp.zeros_like(dsc_ref)
            dpg_ref[...] = jnp.zeros_like(dpg_ref)

        @pl.when(g == g_n - 1)
        def _():
            dhn_v = acc_ref[...]
            xv = x_ref[...]
            rs = _rstd(xv)
            xh = xv * rs
            pg_v = pg_ref[...]
            dsh_ref[...] += jnp.sum(dhn_v, axis=0, keepdims=True)
            dsc_ref[...] += jnp.sum(dhn_v * (xh * pg_v), axis=0, keepdims=True)
            dn = dhn_v * (1.0 + sc_ref[...])
            dpg_ref[...] += jnp.sum(dn * xh, axis=0, keepdims=True)
            dxh = dn * pg_v
            dx_ref[...] = do_ref[...] + rs * (dxh - xh * jnp.mean(dxh * xh, axis=-1, keepdims=True))

    row = pl.BlockSpec((tm, d), lambda m, g: (m, 0))
    vec = pl.BlockSpec((1, d), lambda m, g: (0, 0))
    vsd = jax.ShapeDtypeStruct((1, d), F32)
    return pl.pallas_call(
        body, grid=(s_n // tm, g_n),
        in_specs=[pl.BlockSpec((None, tm, f), lambda m, g: (g, m, 0)), pl.BlockSpec((None, d, f), lambda m, g: (g, 0, 0)),
                  row, row, vec, vec],
        out_specs=[row, vec, vec, vec], out_shape=[jax.ShapeDtypeStruct((s_n, d), F32), vsd, vsd, vsd],
        scratch_shapes=[pltpu.VMEM((tm, d), F32)], compiler_params=_cparams(), name=name)(dgu, w_gu, x, dout, pg, sc)


def _ffn_dgu(df, w_dn, gu, *, name):
    s_n, d = df.shape
    f = w_dn.shape[-2]
    tm = _pick(s_n, FFN_TM, 8)

    def body(df_ref, wd_ref, gu_ref, o_ref):
        da = lax.dot_general(df_ref[...], wd_ref[...], (((1,), (1,)), ((), ())), preferred_element_type=F32)
        g = gu_ref[0].astype(F32)
        u = gu_ref[1].astype(F32)
        sig = jax.nn.sigmoid(g)
        o_ref[0] = (da * u * (sig * (1.0 + g * (1.0 - sig)))).astype(BF16)
        o_ref[1] = (da * (g * sig)).astype(BF16)

    gu_blk = pl.BlockSpec((None, 2, tm, f), lambda s, m: (s, 0, m, 0))
    return pl.pallas_call(
        body, grid=(N_SHARD, s_n // tm),
        in_specs=[pl.BlockSpec((tm, d), lambda s, m: (m, 0)),
                  pl.BlockSpec((None, f, d), lambda s, m: (s, 0, 0)), gu_blk],
        out_specs=gu_blk, out_shape=jax.ShapeDtypeStruct((N_SHARD, 2, s_n, f), BF16),
        compiler_params=_cparams(), name=name)(df, w_dn, gu)


_NT = (((1,), (1,)), ((), ()))
_TN = (((0,), (0,)), ((), ()))
MLA_TQ = 256


def _causal_mask(i, tq, s_n):
    qpos = i * tq + lax.broadcasted_iota(jnp.int32, (tq, s_n), 0)
    kpos = lax.broadcasted_iota(jnp.int32, (tq, s_n), 1)
    return kpos <= qpos


def _mla_attn_fwd(q, k, v, *, name):
    h_n, s_n, dq = q.shape
    dv = v.shape[-1]
    tq = MLA_TQ
    scale = float(dq) ** -0.5

    def body(q_ref, k_ref, v_ref, o_ref, lse_ref):
        i = pl.program_id(1)
        for e in range(1, s_n // tq + 1):
            @pl.when(i == e - 1)
            def _(ext=e * tq):
                mask = _causal_mask(i, tq, ext)
                s = lax.dot_general(q_ref[...], k_ref[0:ext, :], _NT, preferred_element_type=F32) * scale
                s = jnp.where(mask, s, -jnp.inf)
                m = jnp.max(s, axis=-1, keepdims=True)
                p = jnp.exp(s - m)
                l = jnp.sum(p, axis=-1, keepdims=True)
                o = jnp.dot(p.astype(BF16), v_ref[0:ext, :], preferred_element_type=F32)
                o_ref[...] = o / l
                lse_ref[...] = m + jnp.log(l)

    return pl.pallas_call(
        body, grid=(h_n, s_n // tq),
        in_specs=[pl.BlockSpec((None, tq, dq), lambda h, i: (h, i, 0)),
                  pl.BlockSpec((None, s_n, dq), lambda h, i: (h, 0, 0)),
                  pl.BlockSpec((None, s_n, dv), lambda h, i: (h, 0, 0))],
        out_specs=[pl.BlockSpec((None, tq, dv), lambda h, i: (h, i, 0)),
                   pl.BlockSpec((None, tq, 1), lambda h, i: (h, i, 0))],
        out_shape=[jax.ShapeDtypeStruct((h_n, s_n, dv), F32), jax.ShapeDtypeStruct((h_n, s_n, 1), F32)],
        compiler_params=_cparams(), name=name)(q, k, v)


def _mla_attn_bwd(q, k, v, o, do, lse, *, name):
    h_n, s_n, dq = q.shape
    dv = v.shape[-1]
    tq = MLA_TQ
    scale = float(dq) ** -0.5

    def body(q_ref, k_ref, v_ref, o_ref, do_ref, lse_ref, dq_ref, dk_ref, dv_ref):
        i = pl.program_id(1)

        @pl.when(i == 0)
        def _():
            dk_ref[...] = jnp.zeros_like(dk_ref)
            dv_ref[...] = jnp.zeros_like(dv_ref)

        for e in range(1, s_n // tq + 1):
            @pl.when(i == e - 1)
            def _(ext=e * tq):
                mask = _causal_mask(i, tq, ext)
                qv, kv, vv = q_ref[...], k_ref[0:ext, :], v_ref[0:ext, :]
                do_v = do_ref[...]
                s = lax.dot_general(qv, kv, _NT, preferred_element_type=F32) * scale
                p = jnp.where(mask, jnp.exp(s - lse_ref[...]), 0.0)
                dob = do_v.astype(BF16)
                dv_ref[0:ext, :] += lax.dot_general(p.astype(BF16), dob, _TN, preferred_element_type=F32)
                dp = lax.dot_general(dob, vv, _NT, preferred_element_type=F32)
                delta = jnp.sum(do_v * o_ref[...], axis=-1, keepdims=True)
                dsb = (p * (dp - delta) * scale).astype(BF16)
                dq_ref[...] = jnp.dot(dsb, kv, preferred_element_type=F32)
                dk_ref[0:ext, :] += lax.dot_general(dsb, qv, _TN, preferred_element_type=F32)

    return pl.pallas_call(
        body, grid=(h_n, s_n // tq),
        in_specs=[pl.BlockSpec((None, tq, dq), lambda h, i: (h, i, 0)),
                  pl.BlockSpec((None, s_n, dq), lambda h, i: (h, 0, 0)),
                  pl.BlockSpec((None, s_n, dv), lambda h, i: (h, 0, 0)),
                  pl.BlockSpec((None, tq, dv), lambda h, i: (h, i, 0)),
                  pl.BlockSpec((None, tq, dv), lambda h, i: (h, i, 0)),
                  pl.BlockSpec((None, tq, 1), lambda h, i: (h, i, 0))],
        out_specs=[pl.BlockSpec((None, tq, dq), lambda h, i: (h, i, 0)),
                   pl.BlockSpec((None, s_n, dq), lambda h, i: (h, 0, 0)),
                   pl.BlockSpec((None, s_n, dv), lambda h, i: (h, 0, 0))],
        out_shape=[jax.ShapeDtypeStruct((h_n, s_n, dq), F32), jax.ShapeDtypeStruct((h_n, s_n, dq), F32),
                   jax.ShapeDtypeStruct((h_n, s_n, dv), F32)],
        compiler_params=_cparams(), name=name)(q, k, v, o, do, lse)


def _head_sum(x, *, name):
    h_n, s_n, w = x.shape
    tm = _pick(s_n, 512, 8)

    def body(x_ref, o_ref):
        o_ref[...] = jnp.sum(x_ref[...], axis=0)

    return pl.pallas_call(body, grid=(s_n // tm,), in_specs=[pl.BlockSpec((h_n, tm, w), lambda i: (0, i, 0))],
                          out_specs=_rows(tm, w), out_shape=jax.ShapeDtypeStruct((s_n, w), F32),
                          compiler_params=_cparams(), name=name)(x)


N_BLK = SEQ // DIL_BLOCK
DIL_SCALE = 64 ** -0.5


def _dil_masks():
    iq = lax.broadcasted_iota(jnp.int32, (DIL_BLOCK, 2 * DIL_BLOCK), 0)
    ik = lax.broadcasted_iota(jnp.int32, (DIL_BLOCK, 2 * DIL_BLOCK), 1)
    rel = DIL_BLOCK + iq - ik
    both = (rel >= 0) & (rel <= DIL_BLOCK)
    iq1 = lax.broadcasted_iota(jnp.int32, (DIL_BLOCK, DIL_BLOCK), 0)
    ik1 = lax.broadcasted_iota(jnp.int32, (DIL_BLOCK, DIL_BLOCK), 1)
    return both, ik1 <= iq1


def _dil_block(j, d):
    nb = SEQ // d // DIL_BLOCK
    r, n = divmod(j, nb)
    first = n == 0
    rows = lambda start, size: pl.ds(start, size) if d == 1 else pl.ds(start, size, stride=d)
    q_rows = rows(n * DIL_BLOCK * d + r, DIL_BLOCK)
    k_rows = q_rows if first else rows((n - 1) * DIL_BLOCK * d + r, 2 * DIL_BLOCK)
    return q_rows, k_rows, (DIL_BLOCK if first else 0), first


def _dil_head_specs(s_n, e, g):
    return [pl.BlockSpec((None, s_n, e), lambda h, t=t: (g * 3 * HEADS + t * HEADS + h, 0, 0)) for t in range(3)]


def _dil_attn_fwd(heads, bias, g, d, *, name):
    _, s_n, e = heads.shape

    def body(q_ref, k_ref, v_ref, b_ref, o_ref, lse_ref):
        m_both, m_first = _dil_masks()
        for j in range(N_BLK):
            q_rows, k_rows, b_lo, first = _dil_block(j, d)
            qj = q_ref[q_rows, :].astype(BF16)
            kk = k_ref[k_rows, :].astype(BF16)
            vv = v_ref[k_rows, :].astype(BF16)
            s = lax.dot_general(qj, kk, _NT, preferred_element_type=F32) * DIL_SCALE + b_ref[:, b_lo:]
            s = jnp.where(m_first if first else m_both, s, -jnp.inf)
            m = jnp.max(s, axis=-1, keepdims=True)
            lse = m + jnp.log(jnp.sum(jnp.exp(s - m), axis=-1, keepdims=True))
            p = jnp.exp(s - lse)
            o_ref[q_rows, :] = jnp.dot(p.astype(BF16), vv, preferred_element_type=F32)
            lse_ref[q_rows, :] = lse

    head = lambda w: pl.BlockSpec((None, s_n, w), lambda h: (h, 0, 0))
    return pl.pallas_call(
        body, grid=(HEADS,),
        in_specs=_dil_head_specs(s_n, e, g) + [pl.BlockSpec((None, DIL_BLOCK, 2 * DIL_BLOCK), lambda h: (h, 0, 0))],
        out_specs=[head(e), head(1)],
        out_shape=[jax.ShapeDtypeStruct((HEADS, s_n, e), F32), jax.ShapeDtypeStruct((HEADS, s_n, 1), F32)],
        compiler_params=_cparams(), name=name)(heads, heads, heads, bias)


def _dil_attn_bwd(heads, bias, lse, do, dlt, g, d, *, name):
    _, s_n, e = heads.shape

    def body(q_ref, k_ref, v_ref, b_ref, lse_ref, do_ref, dlt_ref, dq_ref, dk_ref, dv_ref, db_ref):
        db_ref[...] = jnp.zeros_like(db_ref)
        m_both, m_first = _dil_masks()
        nb = s_n // d // DIL_BLOCK
        own_v = own_k = own_rows = None
        for j in range(N_BLK):
            q_rows, k_rows, b_lo, first = _dil_block(j, d)
            qj = q_ref[q_rows, :].astype(BF16)
            kk = k_ref[k_rows, :].astype(BF16)
            vv = v_ref[k_rows, :].astype(BF16)
            s = lax.dot_general(qj, kk, _NT, preferred_element_type=F32) * DIL_SCALE + b_ref[:, b_lo:]
            p = jnp.where(m_first if first else m_both, jnp.exp(s - lse_ref[q_rows, :]), 0.0)
            dob = do_ref[q_rows, :].astype(BF16)
            dvv = lax.dot_general(p.astype(BF16), dob, _TN, preferred_element_type=F32)
            dp = lax.dot_general(dob, vv, _NT, preferred_element_type=F32)
            ds = p * (dp - dlt_ref[q_rows, :])
            db_ref[:, b_lo:] += ds
            dsb = (ds * DIL_SCALE).astype(BF16)
            dq_ref[q_rows, :] = jnp.dot(dsb, kk, preferred_element_type=F32)
            dkk = lax.dot_general(dsb, qj, _TN, preferred_element_type=F32)
            if not first:
                dv_ref[own_rows, :] = own_v + dvv[:DIL_BLOCK]
                dk_ref[own_rows, :] = own_k + dkk[:DIL_BLOCK]
                dvv, dkk = dvv[DIL_BLOCK:], dkk[DIL_BLOCK:]
            own_v, own_k, own_rows = dvv, dkk, q_rows
            if j % nb == nb - 1:
                dv_ref[own_rows, :] = own_v
                dk_ref[own_rows, :] = own_k

    head = lambda w: pl.BlockSpec((None, s_n, w), lambda h: (h, 0, 0))
    b_spec = pl.BlockSpec((None, DIL_BLOCK, 2 * DIL_BLOCK), lambda h: (h, 0, 0))
    sd = jax.ShapeDtypeStruct((HEADS, s_n, e), F32)
    return pl.pallas_call(
        body, grid=(HEADS,),
        in_specs=_dil_head_specs(s_n, e, g) + [b_spec, head(1), head(e), head(1)],
        out_specs=[head(e), head(e), head(e), b_spec],
        out_shape=[sd, sd, sd, jax.ShapeDtypeStruct((HEADS, DIL_BLOCK, 2 * DIL_BLOCK), F32)],
        compiler_params=_cparams(), name=name)(heads, heads, heads, bias, lse, do, dlt)


def _proj_heads(x, w, *, name):
    s_n, k = x.shape
    n = w.shape[-1]
    tm, tn, e = 512, 768, 64
    per_blk, n_blk = tn // e, n // tn

    def body(x_ref, w_ref, o_ref):
        r = jnp.dot(x_ref[...], w_ref[...], preferred_element_type=F32)
        for j in range(per_blk):
            o_ref[j] = r[:, e * j:e * (j + 1)]

    return pl.pallas_call(
        body, grid=(w.shape[0], n_blk, s_n // tm),
        in_specs=[pl.BlockSpec((tm, k), lambda s, b, m: (m, 0)), pl.BlockSpec((None, k, tn), lambda s, b, m: (s, 0, b))],
        out_specs=pl.BlockSpec((per_blk, tm, e), lambda s, b, m: (s * n_blk + b, m, 0)),
        out_shape=jax.ShapeDtypeStruct((w.shape[0] * n // e, s_n, e), F32), compiler_params=_cparams(),
        name=name)(x, w)


def _heads_cat(d_ref):
    return jnp.concatenate([d_ref[j] for j in range(d_ref.shape[0])], axis=1)


def _proj_heads_dw(x, dh, *, name):
    s_n, k = x.shape
    tn, e = 768, 64
    per_blk = tn // e
    n_blk = dh.shape[0] // N_SHARD // per_blk
    n = n_blk * tn

    def body(x_ref, d_ref, o_ref):
        o_ref[...] = lax.dot_general(x_ref[...], _heads_cat(d_ref), _TN, preferred_element_type=F32)

    return pl.pallas_call(
        body, grid=(N_SHARD, n_blk, 2),
        in_specs=[pl.BlockSpec((s_n, k // 2), lambda s, b, r: (0, r)),
                  pl.BlockSpec((per_blk, s_n, e), lambda s, b, r: (s * n_blk + b, 0, 0))],
        out_specs=pl.BlockSpec((None, None, k // 2, tn), lambda s, b, r: (r, s, 0, b)),
        out_shape=jax.ShapeDtypeStruct((2, N_SHARD, k // 2, n), F32), compiler_params=_cparams(), name=name)(x, dh)


def _proj_heads_dx(dh, w, *, name):
    k, n = w.shape[1:]
    s_n = dh.shape[1]
    tm, tn, e = 512, 768, 64
    per_blk, n_blk = tn // e, n // tn

    def body(d_ref, w_ref, o_ref):
        r = lax.dot_general(_heads_cat(d_ref), w_ref[...], _NT, preferred_element_type=F32)
        g = pl.program_id(1)

        @pl.when(g == 0)
        def _():
            o_ref[...] = r

        @pl.when(g > 0)
        def _():
            o_ref[...] += r

    return pl.pallas_call(
        body, grid=(s_n // tm, N_SHARD * n_blk),
        in_specs=[pl.BlockSpec((per_blk, tm, e), lambda m, g: (g, m, 0)),
                  pl.BlockSpec((None, k, tn), lambda m, g: (g // n_blk, 0, g % n_blk))],
        out_specs=pl.BlockSpec((tm, k), lambda m, g: (m, 0)),
        out_shape=jax.ShapeDtypeStruct((s_n, k), F32), compiler_params=_cparams(), name=name)(dh, w)


def _group_alpha(ls):
    m = jnp.maximum(jnp.maximum(ls[0], ls[1]), ls[2])
    es = [jnp.exp(l - m) for l in ls]
    tot = es[0] + es[1] + es[2]
    return [ex / tot for ex in es]


def _dil_mix_fwd(os_, ls_, *, name):
    h_n, s_n, e = os_[0].shape
    tm = 512

    def body(o0, o1, o2, l0, l1, l2, out_ref):
        for hh in range(2):
            al = _group_alpha([l[hh] for l in (l0, l1, l2)])
            mix = al[0] * o0[hh] + al[1] * o1[hh] + al[2] * o2[hh]
            out_ref[:, hh * e:(hh + 1) * e] = mix.astype(out_ref.dtype)

    blk = lambda w: pl.BlockSpec((2, tm, w), lambda h, i: (h, i, 0))
    return pl.pallas_call(body, grid=(h_n // 2, s_n // tm), in_specs=[blk(e)] * 3 + [blk(1)] * 3,
                          out_specs=pl.BlockSpec((tm, 2 * e), lambda h, i: (i, h)),
                          out_shape=jax.ShapeDtypeStruct((s_n, h_n * e), BF16), compiler_params=_cparams(),
                          name=name)(*os_, *ls_)


def _dil_mix_bwd(do_flat, os_, ls_, *, name):
    h_n, s_n, e = os_[0].shape
    tm = 512

    def body(do_ref, o0, o1, o2, l0, l1, l2, d0, d1, d2, t0, t1, t2):
        for hh in range(2):
            al = _group_alpha([l[hh] for l in (l0, l1, l2)])
            do_v = do_ref[:, hh * e:(hh + 1) * e]
            mix = al[0] * o0[hh] + al[1] * o1[hh] + al[2] * o2[hh]
            dbar = jnp.sum(do_v * mix, axis=-1, keepdims=True)
            for a_g, d_ref, t_ref in zip(al, (d0, d1, d2), (t0, t1, t2)):
                d_ref[hh] = a_g * do_v
                t_ref[hh] = a_g * dbar

    blk = lambda w: pl.BlockSpec((2, tm, w), lambda h, i: (h, i, 0))
    sd_e = jax.ShapeDtypeStruct((h_n, s_n, e), F32)
    sd_1 = jax.ShapeDtypeStruct((h_n, s_n, 1), F32)
    outs = pl.pallas_call(body, grid=(h_n // 2, s_n // tm),
                          in_specs=[pl.BlockSpec((tm, 2 * e), lambda h, i: (i, h))] + [blk(e)] * 3 + [blk(1)] * 3,
                          out_specs=[blk(e)] * 3 + [blk(1)] * 3, out_shape=[sd_e] * 3 + [sd_1] * 3,
                          compiler_params=_cparams(), name=name)(do_flat, *os_, *ls_)
    return outs[:3], outs[3:]


def _bias_grad(ds, bucket, *, name):
    h_n = ds.shape[0]

    def body(ds_ref, bk_ref, o_ref):
        ds_v = ds_ref[...]
        bk = bk_ref[...]
        lane = lax.broadcasted_iota(jnp.int32, (1, N_BUCKETS), 1)
        acc = jnp.zeros((1, N_BUCKETS), F32)
        for b in range(N_BUCKETS):
            tot = jnp.sum(jnp.sum(jnp.where(bk == b, ds_v, 0.0), axis=1, keepdims=True), axis=0, keepdims=True)
            acc = acc + jnp.where(lane == b, tot, 0.0)
        o_ref[...] = acc

    return pl.pallas_call(
        body, grid=(h_n,),
        in_specs=[pl.BlockSpec((None, DIL_BLOCK, 2 * DIL_BLOCK), lambda h: (h, 0, 0)),
                  pl.BlockSpec((DIL_BLOCK, 2 * DIL_BLOCK), lambda h: (0, 0))],
        out_specs=pl.BlockSpec((None, 1, N_BUCKETS), lambda h: (h, 0, 0)),
        out_shape=jax.ShapeDtypeStruct((h_n, 1, N_BUCKETS), F32), compiler_params=_cparams(), name=name)(ds, bucket)


def _bias_table(rb, bucket, *, name):
    h_n = rb.shape[0]

    def body(rb_ref, bk_ref, o_ref):
        bk = bk_ref[...]
        row = rb_ref[...]
        acc = jnp.zeros(bk.shape, F32)
        for b in range(N_BUCKETS):
            acc = jnp.where(bk == b, row[:, b:b + 1], acc)
        o_ref[...] = acc

    return pl.pallas_call(
        body, grid=(h_n,),
        in_specs=[pl.BlockSpec((None, 1, N_BUCKETS), lambda h: (h, 0, 0)),
                  pl.BlockSpec((DIL_BLOCK, 2 * DIL_BLOCK), lambda h: (0, 0))],
        out_specs=pl.BlockSpec((None, DIL_BLOCK, 2 * DIL_BLOCK), lambda h: (h, 0, 0)),
        out_shape=jax.ShapeDtypeStruct((h_n, DIL_BLOCK, 2 * DIL_BLOCK), F32), compiler_params=_cparams(),
        name=name)(rb, bucket)


def _row_tile(rows, cols, budget=2 << 20):
    if rows * cols * 4 <= budget or rows % 8:
        return rows
    best = 8
    for t in range(8, rows + 1, 8):
        if rows % t == 0 and t * cols * 4 <= budget:
            best = t
    return best


def _adamw(w, g, m, v, *, name):
    shape = w.shape
    cols = shape[-1]
    rows = math.prod(shape[:-1]) if len(shape) > 1 else 1
    to2 = lambda t: t.reshape(rows, cols)
    tr = _row_tile(rows, cols)
    c1 = 1.0 / (1.0 - ADAM_B1 ** ADAM_STEP)
    c2 = 1.0 / (1.0 - ADAM_B2 ** ADAM_STEP)

    def body(w_ref, g_ref, m_ref, v_ref, d_ref, nm_ref, nv_ref):
        g_v = g_ref[...]
        nm = ADAM_B1 * m_ref[...] + (1.0 - ADAM_B1) * g_v
        nv = ADAM_B2 * v_ref[...] + (1.0 - ADAM_B2) * (g_v * g_v)
        m_hat = nm * c1
        v_hat = nv * c2
        d_ref[...] = -ADAM_LR * (m_hat / (jnp.sqrt(v_hat) + ADAM_EPS) + ADAM_WD * w_ref[...])
        nm_ref[...] = nm
        nv_ref[...] = nv

    blk = pl.BlockSpec((tr, cols), lambda i: (i, 0))
    sd = jax.ShapeDtypeStruct((rows, cols), F32)
    outs = pl.pallas_call(body, grid=(rows // tr,), in_specs=[blk] * 4, out_specs=[blk] * 3, out_shape=[sd] * 3,
                          compiler_params=_cparams(), name=name)(to2(w), to2(g), to2(m), to2(v))
    return tuple(t.reshape(shape) for t in outs)


def _add_half(unit, got, half_idx, *, name):
    rest = unit.shape[2:]
    c = rest[-1]
    r = math.prod(rest[:-1])
    tr = _row_tile(r, c)

    def body(idx_ref, u_ref, g_ref, o_ref, w_ref):
        tot = u_ref[...] + g_ref[...].astype(F32)
        o_ref[...] = tot
        w_ref[...] = tot.astype(BF16)

    blk = pl.BlockSpec((None, tr, c), lambda s, i, idx: (s, i, 0))
    grid_spec = pltpu.PrefetchScalarGridSpec(
        num_scalar_prefetch=1, grid=(N_SHARD, r // tr),
        in_specs=[pl.BlockSpec((None, None, tr, c), lambda s, i, idx: (idx[0], s, i, 0)), blk],
        out_specs=[blk, blk])
    out, wire = pl.pallas_call(
        body, grid_spec=grid_spec,
        out_shape=[jax.ShapeDtypeStruct((N_SHARD, r, c), F32), jax.ShapeDtypeStruct((N_SHARD, r, c), BF16)],
        compiler_params=_cparams(), name=name)(half_idx, unit.reshape(2, N_SHARD, r, c), got.reshape(N_SHARD, r, c))
    return out.reshape((N_SHARD,) + rest), wire.reshape((N_SHARD,) + rest)


def _add_shards(part, got, shard_idx, *, name):
    rest = part.shape[1:]
    c = rest[-1]
    r = math.prod(rest[:-1])
    tr = _row_tile(r, c)

    def body(idx_ref, p_ref, g_ref, o_ref):
        acc = p_ref[...]
        for k in range(3):
            acc = acc + g_ref[k].astype(F32)
        o_ref[...] = acc

    grid_spec = pltpu.PrefetchScalarGridSpec(
        num_scalar_prefetch=1, grid=(r // tr,),
        in_specs=[pl.BlockSpec((None, tr, c), lambda i, idx: (idx[0], i, 0)),
                  pl.BlockSpec((3, tr, c), lambda i, idx: (0, i, 0))],
        out_specs=pl.BlockSpec((tr, c), lambda i, idx: (i, 0)))
    out = pl.pallas_call(body, grid_spec=grid_spec, out_shape=jax.ShapeDtypeStruct((r, c), F32),
                         compiler_params=_cparams(), name=name)(
        shard_idx, part.reshape(N_SHARD, r, c), got.reshape(3, r, c))
    return out.reshape(rest)


def _sum_devices(x, n_dev, *, name):
    rows = x.shape[0] // n_dev

    def body(x_ref, o_ref):
        acc = x_ref[0:rows, :]
        for d in range(1, n_dev):
            acc = acc + x_ref[d * rows:(d + 1) * rows, :]
        o_ref[...] = acc

    return pl.pallas_call(body, out_shape=jax.ShapeDtypeStruct((rows, x.shape[1]), F32), name=name)(x)


def _my_pos():
    return lax.axis_index("x"), lax.axis_index("y"), lax.axis_index("c")


def _all_gather(x_blk, *, name, in_vmem):
    m_per, n = x_blk.shape

    def body(x_ref, out_ref, send_sems, recv_sems, local_sem):
        x, y, c = _my_pos()
        me, sibling = (x, y, c), (x, y, 1 - c)
        chips = [(1 - x, y), (x, 1 - y), (1 - x, 1 - y)]

        def rows(px, py, pc):
            return out_ref.at[pl.ds((4 * px + 2 * py + pc) * m_per, m_per), :]

        def copy(k, block, to, src=None):
            return pltpu.make_async_remote_copy(
                src_ref=rows(*block) if src is None else src, dst_ref=rows(*block),
                send_sem=send_sems.at[k], recv_sem=recv_sems.at[k], device_id=to, device_id_type=MESH)

        mine = pltpu.make_async_copy(x_ref, rows(*me), local_sem)
        mine.start()
        first = [copy(0, me, sibling, src=x_ref)]
        first += [copy(1 + j, me, (*chip, c), src=x_ref) for j, chip in enumerate(chips)]
        for cp in first:
            cp.start()
        passed = [copy(4 + j, (*chip, c), sibling) for j, chip in enumerate(chips)]
        for j, chip in enumerate(chips):
            copy(1 + j, (*chip, c), me).wait_recv()
            passed[j].start()
        copy(0, sibling, me).wait_recv()
        for j, chip in enumerate(chips):
            copy(4 + j, (*chip, 1 - c), me).wait_recv()
        for cp in first + passed:
            cp.wait_send()
        mine.wait()

    space = pltpu.VMEM if in_vmem else pl.ANY
    return pl.pallas_call(
        body, out_shape=jax.ShapeDtypeStruct((8 * m_per, n), x_blk.dtype),
        in_specs=[pl.BlockSpec(memory_space=space)], out_specs=pl.BlockSpec(memory_space=space),
        scratch_shapes=[pltpu.SemaphoreType.DMA((7,)), pltpu.SemaphoreType.DMA((7,)), pltpu.SemaphoreType.DMA],
        name=name)(x_blk)


_HBM = pl.BlockSpec(memory_space=pl.ANY)


def _gather_weights(fams, *, name):
    n = len(fams)

    def body(*refs):
        ins, outs = refs[:n], refs[n:2 * n]
        send_sems, recv_sems = refs[2 * n:]
        x, y, c = _my_pos()
        me, sibling = (x, y, c), (x, y, 1 - c)
        chips = [(1 - x, y), (x, 1 - y), (1 - x, 1 - y)]

        def copy(f, k, block, to, src=None):
            px, py, pc = block
            dst = outs[f].at[2 * px + py, pc]
            return pltpu.make_async_remote_copy(
                src_ref=dst if src is None else src, dst_ref=dst, send_sem=send_sems.at[7 * f + k],
                recv_sem=recv_sems.at[7 * f + k], device_id=to, device_id_type=MESH)

        first, passed = [], []
        for f in range(n):
            src = ins[f].at[c]
            first.append(copy(f, 0, me, sibling, src=src))
            first += [copy(f, 1 + j, me, (*chip, c), src=src) for j, chip in enumerate(chips)]
        for cp in first:
            cp.start()
        for j, chip in enumerate(chips):
            for f in range(n):
                copy(f, 1 + j, (*chip, c), me).wait_recv()
                passed.append(copy(f, 4 + j, (*chip, c), sibling))
                passed[-1].start()
        for f in range(n):
            copy(f, 0, sibling, me).wait_recv()
        for j, chip in enumerate(chips):
            for f in range(n):
                copy(f, 4 + j, (*chip, 1 - c), me).wait_recv()
        for cp in first + passed:
            cp.wait_send()

    outs = pl.pallas_call(
        body, out_shape=[jax.ShapeDtypeStruct((N_SHARD,) + t.shape, t.dtype) for t in fams],
        in_specs=[_HBM] * n, out_specs=[_HBM] * n,
        scratch_shapes=[pltpu.SemaphoreType.DMA((7 * n,)), pltpu.SemaphoreType.DMA((7 * n,))], name=name)(*fams)
    return [_place_own(o, t) for o, t in zip(outs, fams)]


def _swap_halves(units, *, name):
    n = len(units)

    def body(*refs):
        ins, outs = refs[:n], refs[n:2 * n]
        send_sems, recv_sems = refs[2 * n:]
        x, y, c = _my_pos()
        cps = [pltpu.make_async_remote_copy(src_ref=ins[f].at[1 - c], dst_ref=outs[f], send_sem=send_sems.at[f],
                                            recv_sem=recv_sems.at[f], device_id=(x, y, 1 - c), device_id_type=MESH)
               for f in range(n)]
        for cp in cps:
            cp.start()
        for cp in cps:
            cp.wait()

    return pl.pallas_call(
        body, out_shape=[jax.ShapeDtypeStruct(t.shape[1:], t.dtype) for t in units],
        in_specs=[_HBM] * n, out_specs=[_HBM] * n,
        scratch_shapes=[pltpu.SemaphoreType.DMA((n,)), pltpu.SemaphoreType.DMA((n,))], name=name)(*units)


def _send_to_chips(parts, *, name):
    n = len(parts)

    def body(*refs):
        ins, outs = refs[:n], refs[n:2 * n]
        send_sems, recv_sems = refs[2 * n:]
        x, y, c = _my_pos()
        chips = [(1 - x, y), (x, 1 - y), (1 - x, 1 - y)]
        cps = [pltpu.make_async_remote_copy(src_ref=ins[f].at[2 * cx + cy], dst_ref=outs[f].at[k],
                                            send_sem=send_sems.at[3 * f + k], recv_sem=recv_sems.at[3 * f + k],
                                            device_id=(cx, cy, c), device_id_type=MESH)
               for f in range(n) for k, (cx, cy) in enumerate(chips)]
        for cp in cps:
            cp.start()
        for cp in cps:
            cp.wait()

    return pl.pallas_call(
        body, out_shape=[jax.ShapeDtypeStruct((3,) + t.shape[1:], t.dtype) for t in parts],
        in_specs=[_HBM] * n, out_specs=[_HBM] * n,
        scratch_shapes=[pltpu.SemaphoreType.DMA((3 * n,)), pltpu.SemaphoreType.DMA((3 * n,))], name=name)(*parts)


def _pair_gather(halves, *, name):
    n = len(halves)

    def body(*refs):
        ins, outs = refs[:n], refs[n:2 * n]
        send_sems, recv_sems = refs[2 * n:]
        x, y, c = _my_pos()
        cps = [pltpu.make_async_remote_copy(src_ref=ins[f], dst_ref=outs[f].at[c], send_sem=send_sems.at[f],
                                            recv_sem=recv_sems.at[f], device_id=(x, y, 1 - c), device_id_type=MESH)
               for f in range(n)]
        for cp in cps:
            cp.start()
        for f in range(n):
            pltpu.make_async_remote_copy(src_ref=ins[f], dst_ref=outs[f].at[1 - c], send_sem=send_sems.at[f],
                                         recv_sem=recv_sems.at[f], device_id=(x, y, 1 - c),
                                         device_id_type=MESH).wait_recv()
        for cp in cps:
            cp.wait_send()

    outs = pl.pallas_call(
        body, out_shape=[jax.ShapeDtypeStruct((2,) + t.shape, t.dtype) for t in halves],
        in_specs=[_HBM] * n, out_specs=[_HBM] * n,
        scratch_shapes=[pltpu.SemaphoreType.DMA((n,)), pltpu.SemaphoreType.DMA((n,))], name=name)(*halves)
    c = lax.axis_index("c")
    return [lax.dynamic_update_index_in_dim(o, t, c, 0) for o, t in zip(outs, halves)]


_HBM_ONLY = pl.BlockSpec(memory_space=pltpu.HBM)
_SEMS = pl.BlockSpec(memory_space=pltpu.SEMAPHORE)
_EFFECT = pltpu.SideEffectType.DATAFLOW_SIDE_EFFECTING


def _copies_start(srcs, lands, plan, n_copies, *, name):
    n, m = len(srcs), len(lands)

    def body(*refs):
        src_refs, land_refs = refs[:n], refs[n:n + m]
        send_sems, recv_sems, token = refs[n + m], refs[n + m + 1], refs[-1]
        for k, (src, dst, peer) in enumerate(plan(src_refs, land_refs)):
            pltpu.make_async_remote_copy(src_ref=src, dst_ref=dst, send_sem=send_sems.at[k], recv_sem=recv_sems.at[k],
                                         device_id=peer, device_id_type=MESH).start()
        token[...] = jnp.zeros_like(token)

    bufs = [pltpu.with_memory_space_constraint(t, pltpu.HBM) for t in (*srcs, *lands)]
    outs = pl.pallas_call(
        body, name=name,
        out_shape=(pltpu.SemaphoreType.DMA((n_copies,)), pltpu.SemaphoreType.DMA((n_copies,)),
                   *[pltpu.HBM(t.shape, t.dtype) for t in bufs], jax.ShapeDtypeStruct((8, 128), F32)),
        in_specs=[_HBM_ONLY] * (n + m),
        out_specs=(_SEMS, _SEMS, *[_HBM_ONLY] * (n + m), pl.BlockSpec(memory_space=pltpu.VMEM)),
        input_output_aliases={k: 2 + k for k in range(n + m)},
        compiler_params=pltpu.CompilerParams(has_side_effects=_EFFECT))(*bufs)
    return outs[0], outs[1], list(outs[2:2 + n + m]), outs[-1]


def _copies_wait(send_sems, recv_sems, thru, n_src, plan, after, *, name):
    nm = len(thru)

    def body(*refs):
        t_refs, send, recv = refs[:nm], refs[nm], refs[nm + 1]
        for k, (src, dst, peer) in enumerate(plan(t_refs[:n_src], t_refs[n_src:])):
            cp = pltpu.make_async_remote_copy(src_ref=src, dst_ref=dst, send_sem=send.at[k], recv_sem=recv.at[k],
                                              device_id=peer, device_id_type=MESH)
            cp.wait_send()
            cp.wait_recv()

    outs = pl.pallas_call(
        body, name=name, out_shape=tuple(pltpu.HBM(t.shape, t.dtype) for t in thru),
        in_specs=[_HBM_ONLY] * nm + [_SEMS, _SEMS, pl.BlockSpec(memory_space=pl.ANY)],
        out_specs=tuple([_HBM_ONLY] * nm), input_output_aliases={k: k for k in range(nm)},
        compiler_params=pltpu.CompilerParams(has_side_effects=_EFFECT))(*thru, send_sems, recv_sems, after)
    return list(outs)


_RELATIONS = [(dx, dy, dc) for dx in (0, 1) for dy in (0, 1) for dc in (0, 1)][1:]


def _gather_plan(src_refs, land_refs):
    x, y, c = _my_pos()
    flip = lambda v, d: 1 - v if d else v
    return [(s_ref.at[c], l_ref.at[2 * x + y, c], (flip(x, dx), flip(y, dy), flip(c, dc)))
            for s_ref, l_ref in zip(src_refs, land_refs) for dx, dy, dc in _RELATIONS]


def _gather_chips_plan(src_refs, land_refs):
    x, y, c = _my_pos()
    peers = [(x, y, 1 - c), (1 - x, y, c), (x, 1 - y, c), (1 - x, 1 - y, c)]
    return [(s_ref.at[c], l_ref.at[2 * x + y, c], peer) for s_ref, l_ref in zip(src_refs, land_refs) for peer in peers]


def _gather_pass_plan(src_refs, land_refs):
    x, y, c = _my_pos()
    chips = [(1 - x, y), (x, 1 - y), (1 - x, 1 - y)]
    return [(l_ref.at[2 * cx + cy, c], l_ref.at[2 * cx + cy, c], (x, y, 1 - c))
            for l_ref in land_refs for cx, cy in chips]


def _sibling_plan(src_refs, land_refs):
    x, y, c = _my_pos()
    return [(s_ref.at[1 - c], l_ref, (x, y, 1 - c)) for s_ref, l_ref in zip(src_refs, land_refs)]


def _chips_plan(src_refs, land_refs):
    x, y, c = _my_pos()
    chips = [(1 - x, y), (x, 1 - y), (1 - x, 1 - y)]
    return [(s_ref.at[2 * cx + cy], l_ref.at[k], (cx, cy, c))
            for s_ref, l_ref in zip(src_refs, land_refs) for k, (cx, cy) in enumerate(chips)]


def _place_own(gathered, fam):
    x, y, c = _my_pos()
    own = lax.dynamic_index_in_dim(fam, c, 0, keepdims=True)[None]
    return lax.dynamic_update_slice(gathered, own, (2 * x + y, c) + (0,) * (fam.ndim - 1))


def _to_heads(t, width):
    return t.reshape(t.shape[0], HEADS, width).transpose(1, 0, 2)


def _from_heads(t):
    return t.transpose(1, 0, 2).reshape(t.shape[1], -1)


def _t5_bucket(dist):
    max_exact = N_BUCKETS // 2
    d = jnp.maximum(dist, 1).astype(F32)
    large = max_exact + (jnp.log(d / max_exact) / math.log(MAX_DISTANCE / max_exact)
                         * (N_BUCKETS - max_exact)).astype(jnp.int32)
    large = jnp.minimum(large, N_BUCKETS - 1)
    return jnp.where(dist < max_exact, dist, large)


def _bucket_map(dilation):
    iq = jnp.arange(DIL_BLOCK)[:, None]
    ik = jnp.arange(2 * DIL_BLOCK)[None, :]
    rel = DIL_BLOCK + iq - ik
    return _t5_bucket(jnp.maximum(rel, 0) * dilation).astype(jnp.int32)


def _q_perm(w):
    w3 = w.reshape(w.shape[0], HEADS, QK_NOPE + QK_ROPE)
    return jnp.concatenate([w3[:, :, :QK_NOPE].reshape(w.shape[0], -1),
                            w3[:, :, QK_NOPE:QK_NOPE + HALF_ROPE].reshape(w.shape[0], -1),
                            w3[:, :, QK_NOPE + HALF_ROPE:].reshape(w.shape[0], -1)], axis=1)


def _q_unperm(w):
    n0, n1 = HEADS * QK_NOPE, HEADS * HALF_ROPE
    r = w.shape[0]
    return jnp.concatenate([w[:, :n0].reshape(r, HEADS, QK_NOPE), w[:, n0:n0 + n1].reshape(r, HEADS, HALF_ROPE),
                            w[:, n0 + n1:].reshape(r, HEADS, HALF_ROPE)], axis=2).reshape(r, -1)


def _kv_perm(w):
    w3 = w.reshape(w.shape[0], HEADS, QK_NOPE + V_HEAD)
    return jnp.concatenate([w3[:, :, :QK_NOPE].reshape(w.shape[0], -1), w3[:, :, QK_NOPE:].reshape(w.shape[0], -1)],
                           axis=1)


def _kv_unperm(w):
    n0 = HEADS * QK_NOPE
    r = w.shape[0]
    return jnp.concatenate([w[:, :n0].reshape(r, HEADS, QK_NOPE), w[:, n0:].reshape(r, HEADS, V_HEAD)],
                           axis=2).reshape(r, -1)


def _row(v):
    return v.reshape(1, -1)


def kernel(x, c, norm_pre, norm_post, w_mod, b_mod, ffn_w_gate, ffn_w_up, ffn_w_down, mla_w_in, mla_q_norm, mla_w_q_up, mla_kv_norm, mla_w_kv_up, mla_w_o, dil_w_in, dil_w_o, rel_bias, loss_target, m_norm_pre, m_norm_post, m_w_mod, m_b_mod, m_ffn_w_gate, m_ffn_w_up, m_ffn_w_down, m_mla_w_in, m_mla_q_norm, m_mla_w_q_up, m_mla_kv_norm, m_mla_w_kv_up, m_mla_w_o, m_dil_w_in, m_dil_w_o, m_rel_bias, v_norm_pre, v_norm_post, v_w_mod, v_b_mod, v_ffn_w_gate, v_ffn_w_up, v_ffn_w_down, v_mla_w_in, v_mla_q_norm, v_mla_w_q_up, v_mla_kv_norm, v_mla_w_kv_up, v_mla_w_o, v_dil_w_in, v_dil_w_o, v_rel_bias):
    given = dict(locals())
    ix, iy, ic = _my_pos()
    shard_id = 2 * ix + iy
    dev_id = 4 * ix + 2 * iy + ic
    x2 = x[0]
    target = loss_target[0]
    half_idx = jnp.reshape(ic, (1,)).astype(jnp.int32)
    shard_idx = jnp.reshape(shard_id, (1,)).astype(jnp.int32)

    blk = jnp.zeros((8, D_MODEL), F32)
    blk = blk.at[0].set(c[0])
    blk = blk.at[1:3].set(jnp.pad(norm_pre.reshape(-1), (0, 512)).reshape(2, D_MODEL))
    blk = blk.at[3:5].set(jnp.pad(norm_post.reshape(-1), (0, 512)).reshape(2, D_MODEL))
    got = _all_gather(blk, name="ag_c_norms", in_vmem=True).reshape(N_SHARD, 2, 8, D_MODEL)
    c_all = got[:, :, 0, :].reshape(8, D_MODEL)

    def full_norm(lo):
        t = got[:, 0, lo:lo + 2, :].reshape(N_SHARD, 2 * D_MODEL)[:, :1536].reshape(N_SHARD, 2, 3, 256)
        return t.transpose(1, 2, 0, 3).reshape(2, 3, D_MODEL)

    pre_full, post_full = full_norm(1), full_norm(3)

    silu_c = _silu_bf16(c_all, name="silu_c")
    b_cols = lax.dynamic_slice_in_dim(b_mod, shard_id * 2304, 2304, axis=1).reshape(2, 1, 2304)
    mod_part = _mm(silu_c, w_mod, bias=b_cols, name="mod_mm", tn_cap=768)
    mod_all = _all_gather(mod_part.reshape(16, 2304), name="ag_mod", in_vmem=True)
    mod_all = mod_all.reshape(N_SHARD, 2, 2, 8, 2304)[:, 0]
    mod_mine = lax.dynamic_index_in_dim(mod_all, dev_id, axis=2, keepdims=False)
    mod = mod_mine.transpose(1, 0, 2).reshape(2, 9, D_MODEL)

    bf = lambda t: t.astype(BF16)
    ffn_fam = lambda i, h: [bf(jnp.stack([ffn_w_gate[i, h], ffn_w_up[i, h]])),
                            bf(ffn_w_down[i, h].reshape(2, F_SHARD // 2, D_MODEL))]
    mla_fam = [bf(mla_w_in.reshape(2, 128, -1)), bf(mla_w_q_up.reshape(2, 192, -1)),
               bf(mla_w_kv_up.reshape(2, 128, -1)), bf(mla_w_o.reshape(2, 128, D_MODEL))]
    dil_fam = [bf(dil_w_in.reshape(2, 512, -1)), bf(dil_w_o.reshape(2, 128, D_MODEL))]
    later_fams = [ffn_fam(0, 1), ffn_fam(1, 0) + dil_fam, ffn_fam(1, 1)]
    full, later_fams, mod = lax.optimization_barrier(
        (_gather_weights(ffn_fam(0, 0) + mla_fam, name="ag_weights_first"), later_fams, mod))

    def gather_later(fams, tag):
        lands = [lax.empty((N_SHARD,) + t.shape, t.dtype) for t in fams]
        send, recv, thru, token = _copies_start(fams, lands, _gather_plan, 7 * len(fams), name=f"ag_start_{tag}")
        return dict(send=send, recv=recv, thru=thru, token=token, n=len(fams), tag=tag)

    def arrive(st, after):
        thru = _copies_wait(st['send'], st['recv'], st['thru'], st['n'], _gather_plan, after,
                            name=f"ag_wait_{st['tag']}")
        return [_place_own(o, t) for t, o in zip(thru[:st['n']], thru[st['n']:])]

    def gather_chips(fams, tag):
        lands = [lax.empty((N_SHARD,) + t.shape, t.dtype) for t in fams]
        send, recv, thru, token = _copies_start(fams, lands, _gather_chips_plan, 4 * len(fams), name=f"ag_start_{tag}")
        return dict(send=send, recv=recv, thru=thru, token=token, n=len(fams), tag=tag)

    def pass_on(st, after):
        n, tag = st['n'], st['tag']
        thru = _copies_wait(st['send'], st['recv'], st['thru'], n, _gather_chips_plan, after, name=f"ag_mid_{tag}")
        send, recv, lands, token = _copies_start([], thru[n:], _gather_pass_plan, 3 * n, name=f"ag_pass_{tag}")
        return dict(send=send, recv=recv, thru=lands, fams=thru[:n], tag=tag), token[0, 0]

    def arrive_passed(st, after):
        lands = _copies_wait(st['send'], st['recv'], st['thru'], 0, _gather_pass_plan, after, name=f"ag_wait_{st['tag']}")
        return [_place_own(o, t) for t, o in zip(st['fams'], lands)]

    flight_a = gather_later(later_fams[0], "l0s2")
    _, next_fams = lax.optimization_barrier((flight_a['token'], later_fams[1]))
    flight_b = gather_chips(next_fams, "l1s01")
    as_ffn = lambda w_gu, w_dn: (w_gu, w_dn.reshape(N_SHARD, F_SHARD, D_MODEL))
    ffn_w = {(0, 0): as_ffn(full[0], full[1])}
    w_in = full[2].reshape(D_MODEL, -1)
    wq_p = _q_perm(full[3].reshape(N_SHARD, Q_LORA, -1).transpose(1, 0, 2).reshape(Q_LORA, -1))
    wkv_p = _kv_perm(full[4].reshape(N_SHARD, KV_LORA, -1).transpose(1, 0, 2).reshape(KV_LORA, -1))
    w_mo = full[5].reshape(D_MODEL, D_MODEL)
    dil_w = {}

    pos = jnp.arange(SEQ, dtype=F32)
    freqs = ROPE_THETA ** (-jnp.arange(HALF_ROPE, dtype=F32) / HALF_ROPE)
    ang = pos[:, None] * freqs[None, :]
    cos_k, sin_k = jnp.cos(ang), jnp.sin(ang)
    cos_q, sin_q = jnp.tile(cos_k, (1, HEADS)), jnp.tile(sin_k, (1, HEADS))

    buckets = [_bucket_map(d) for _, d in DIL_GROUPS]
    biases = [_bias_table(rel_bias[:, g * HEADS:(g + 1) * HEADS].T.reshape(HEADS, 1, N_BUCKETS), bk,
                          name=f"dil_bias_table_g{g}") for g, bk in enumerate(buckets)]

    def sub_params(i, sub):
        return dict(pg=_row(pre_full[i, sub]), qg=_row(post_full[i, sub]), sh=_row(mod[i, 3 * sub]),
                    sc=_row(mod[i, 3 * sub + 1]), gate=_row(mod[i, 3 * sub + 2]))

    def ffn_fwd(xin, i, h, sub, tie=None, mid=None):
        p = sub_params(i, sub)
        if tie is not None:
            p['sh'] = p['sh'] + tie
        tag = f"l{i}s{sub}"
        w_gu, w_dn = ffn_w[i, h]
        hn = _pre_fwd(xin, p['pg'], p['sc'], p['sh'], name=f"pre_fwd_{tag}")
        gu, a = _ffn_up(hn, w_gu, name=f"ffn_up_{tag}")
        if mid is not None:
            p['qg'] = p['qg'] + mid(a)
        f, out = _ffn_down(a, w_dn, xin, p['qg'], p['gate'], FFN_RES, name=f"ffn_down_{tag}")
        return out, dict(x=xin, hn=hn, gu=gu, a=a, f=f, p=p, i=i, h=h, tag=tag)

    def mla_fwd(xin, i, sub):
        p = sub_params(i, sub)
        tag = f"l{i}s{sub}"
        hn = _pre_fwd(xin, p['pg'], p['sc'], p['sh'], name=f"pre_fwd_{tag}")
        lat = _mm(hn, w_in, name="mla_lat")
        cq, ckv = lat[:, :Q_LORA], lat[:, Q_LORA:Q_LORA + KV_LORA]
        k1, k2 = lat[:, Q_LORA + KV_LORA:Q_LORA + KV_LORA + HALF_ROPE], lat[:, Q_LORA + KV_LORA + HALF_ROPE:]
        cqn = _rms_fwd(cq, mla_q_norm, name="mla_qnorm")
        ckvn = _rms_fwd(ckv, mla_kv_norm, name="mla_kvnorm")
        qp = _mm(cqn, wq_p, name="mla_q_up")
        kvp = _mm(ckvn, wkv_p, name="mla_kv_up")
        n0, n1 = HEADS * QK_NOPE, HEADS * HALF_ROPE
        qr1, qr2 = _rope(qp[:, n0:n0 + n1], qp[:, n0 + n1:], cos_q, sin_q, name="rope_q")
        kr1, kr2 = _rope(k1, k2, cos_k, sin_k, name="rope_k")
        q = jnp.concatenate([qp[:, :n0].reshape(SEQ, HEADS, QK_NOPE), qr1.reshape(SEQ, HEADS, HALF_ROPE),
                             qr2.reshape(SEQ, HEADS, HALF_ROPE)], axis=2).transpose(1, 0, 2).astype(BF16)
        kr = jnp.broadcast_to(jnp.concatenate([kr1, kr2], axis=1)[:, None, :], (SEQ, HEADS, QK_ROPE))
        k = jnp.concatenate([kvp[:, :n0].reshape(SEQ, HEADS, QK_NOPE), kr], axis=2).transpose(1, 0, 2).astype(BF16)
        v = _to_heads(kvp[:, n0:], V_HEAD).astype(BF16)
        o, lse = _mla_attn_fwd(q, k, v, name="mla_attn_fwd")
        o_flat = _from_heads(o).astype(BF16)
        f = _mm(o_flat, w_mo, name="mla_out")
        out = _post_fwd(f, xin, p['qg'], p['gate'], 1.0, name=f"post_fwd_{tag}")
        return out, dict(x=xin, hn=hn, cq=cq, ckv=ckv, cqn=cqn, ckvn=ckvn, q=q, k=k, v=v, o=o, lse=lse,
                         o_flat=o_flat, f=f, p=p, tag=tag)

    def dil_fwd(xin, i, sub):
        p = sub_params(i, sub)
        tag = f"l{i}s{sub}"
        hn = _pre_fwd(xin, p['pg'], p['sc'], p['sh'], name=f"pre_fwd_{tag}")
        heads = _proj_heads(hn, dil_w['in'], name="dil_proj")
        outs, lses = [], []
        for g, (window, d) in enumerate(DIL_GROUPS):
            o, lse = _dil_attn_fwd(heads, biases[g], g, d, name=f"dil_attn_fwd_g{g}")
            outs.append(o)
            lses.append(lse)
        o_flat = _dil_mix_fwd(outs, lses, name="dil_mix_fwd")
        f = _mm(o_flat, dil_w['out'], name="dil_out")
        out = _post_fwd(f, xin, p['qg'], p['gate'], 1.0, name=f"post_fwd_{tag}")
        return out, dict(x=xin, hn=hn, heads=heads, outs=outs, lses=lses, o_flat=o_flat, f=f, p=p, tag=tag)

    saved = [None] * 6
    xs, saved[0] = ffn_fwd(x2, 0, 0, 0, tie=flight_a['token'][0, 0] + flight_b['token'][0, 0])
    xs, saved[1] = mla_fwd(xs, 0, 1)
    ffn_w[0, 1] = as_ffn(*arrive(flight_a, xs))
    passed = {}

    def second_step(after):
        passed['st'], tok = pass_on(flight_b, after)
        return tok

    xs, saved[2] = ffn_fwd(xs, 0, 1, 2, mid=second_step)
    got, last_fams = lax.optimization_barrier((arrive_passed(passed['st'], xs), later_fams[2]))
    ffn_w[1, 0] = as_ffn(got[0], got[1])
    dil_w['in'], dil_w['out'] = got[2].reshape(N_SHARD, D_MODEL, -1), got[3].reshape(D_MODEL, D_MODEL)
    in_flight = gather_later(last_fams, "l1s2")
    xs, saved[3] = ffn_fwd(xs, 1, 0, 0, tie=in_flight['token'][0, 0])
    xs, saved[4] = dil_fwd(xs, 1, 1)
    ffn_w[1, 1] = as_ffn(*arrive(in_flight, xs))
    xs, saved[5] = ffn_fwd(xs, 1, 1, 2)

    dx, loss_part = _loss(xs, target, name="loss")

    dmod = [[None] * 9 for _ in range(2)]
    dpre = [[None] * 3 for _ in range(2)]
    dpost = [[None] * 3 for _ in range(2)]
    ffn_units = {}
    row_unit = lambda g, r, j: ((r % 2, r // 2), 0, j)

    def close_sub(dhn, dout, sv, i, sub, res_dgate, res_dqg):
        p = sv['p']
        dxs, dsh, dsc, dpg = _pre_bwd(dhn, sv['x'], dout, p['pg'], p['sc'], name=f"pre_bwd_{sv['tag']}")
        dmod[i][3 * sub], dmod[i][3 * sub + 1], dmod[i][3 * sub + 2] = dsh, dsc, res_dgate
        dpre[i][sub], dpost[i][sub] = dpg, res_dqg
        return dxs

    def ffn_bwd(dout, sv, sub, tie=0.0, mid=None):
        i, h, p, tag = sv['i'], sv['h'], sv['p'], sv['tag']
        w_gu, w_dn = ffn_w[i, h]
        df, dgate, dqg = _post_bwd(dout, sv['f'], p['qg'] + tie, p['gate'], FFN_RES, name=f"post_bwd_{tag}")
        u_dn = _mm(sv['a'], df, ta=True, tn_cap=D_MODEL // 2, out_shape=(2, N_SHARD, F_SHARD, D_MODEL // 2),
                   out_sel=lambda g, r, j: ((j, g), r, 0), name=f"ffn_dwd_{tag}")
        dgu = _ffn_dgu(df, w_dn, sv['gu'], name=f"ffn_dgu_{tag}")
        if mid is not None:
            p = dict(p, pg=p['pg'] + mid(dgu))
        dgu = dgu.reshape(2 * N_SHARD, SEQ, F_SHARD)
        u_gu = _mm(dgu, sv['hn'], ta=True, out_shape=(2, N_SHARD, F_SHARD, D_MODEL),
                   out_sel=lambda g, r, j: ((g % 2, g // 2), r, j), name=f"ffn_dwgu_{tag}")
        ffn_units[i, h] = [u_gu, u_dn]
        dxs, dsh, dsc, dpg = _ffn_dhn(dgu, w_gu.reshape(2 * N_SHARD, D_MODEL, F_SHARD), sv['x'], dout, p['pg'], p['sc'],
                                      name=f"ffn_dhn_{tag}")
        dmod[i][3 * sub], dmod[i][3 * sub + 1], dmod[i][3 * sub + 2] = dsh, dsc, dgate
        dpre[i][sub], dpost[i][sub] = dpg, dqg
        return dxs

    def mla_bwd(dout, sv, i, sub, tie=0.0):
        p, tag = sv['p'], sv['tag']
        df, dgate, dqg = _post_bwd(dout, sv['f'], p['qg'] + tie, p['gate'], 1.0, name=f"post_bwd_{tag}")
        u_wo = _mm(sv['o_flat'], df, ta=True, tm_cap=128, out_shape=(2, N_SHARD, 128, D_MODEL), out_sel=row_unit,
                   name="mla_dwo")
        do_flat = _mm(df, w_mo, tb=True, name="mla_do")
        do = _to_heads(do_flat, V_HEAD)
        dq, dk, dv = _mla_attn_bwd(sv['q'], sv['k'], sv['v'], sv['o'], do, sv['lse'], name="mla_attn_bwd")
        dq_t = dq.transpose(1, 0, 2)
        dqr1, dqr2 = _rope(dq_t[:, :, QK_NOPE:QK_NOPE + HALF_ROPE].reshape(SEQ, -1),
                           dq_t[:, :, QK_NOPE + HALF_ROPE:].reshape(SEQ, -1), cos_q, -sin_q, name="rope_q_bwd")
        dqp = jnp.concatenate([dq_t[:, :, :QK_NOPE].reshape(SEQ, -1), dqr1, dqr2], axis=1).astype(BF16)
        dkr = _head_sum(dk[:, :, QK_NOPE:], name="mla_dkr_sum")
        dk1, dk2 = _rope(dkr[:, :HALF_ROPE], dkr[:, HALF_ROPE:], cos_k, -sin_k, name="rope_k_bwd")
        dkvp = jnp.concatenate([_from_heads(dk[:, :, :QK_NOPE]), _from_heads(dv)], axis=1).astype(BF16)
        g_wq = _q_unperm(_mm(sv['cqn'], dqp, ta=True, name="mla_dwq"))
        g_wkv = _kv_unperm(_mm(sv['ckvn'], dkvp, ta=True, name="mla_dwkv"))
        dcqn = _mm(dqp, wq_p, tb=True, name="mla_dcqn")
        dckvn = _mm(dkvp, wkv_p, tb=True, name="mla_dckvn")
        dcq, g_qn = _rms_bwd(dcqn, sv['cq'], mla_q_norm, name="mla_qnorm_bwd")
        dckv, g_kvn = _rms_bwd(dckvn, sv['ckv'], mla_kv_norm, name="mla_kvnorm_bwd")
        dlat = jnp.concatenate([dcq, dckv, dk1, dk2], axis=1).astype(BF16)
        u_win = _mm(sv['hn'], dlat, ta=True, tm_cap=128, out_shape=(2, N_SHARD, 128, dlat.shape[1]),
                    out_sel=row_unit, name="mla_dwin")
        dhn = _mm(dlat, w_in, tb=True, name="mla_dhn")
        col_unit = lambda t: (t.reshape(t.shape[0], N_SHARD, -1).transpose(1, 0, 2)
                              .reshape(N_SHARD, 2, t.shape[0] // 2, -1).transpose(1, 0, 2, 3))
        grads = dict(units=[u_win, col_unit(g_wq), col_unit(g_wkv), u_wo], q_norm=g_qn, kv_norm=g_kvn)
        return close_sub(dhn, dout, sv, i, sub, dgate, dqg), grads

    def dil_bwd(dout, sv, i, sub):
        p, tag = sv['p'], sv['tag']
        df, dgate, dqg = _post_bwd(dout, sv['f'], p['qg'], p['gate'], 1.0, name=f"post_bwd_{tag}")
        u_wo = _mm(sv['o_flat'], df, ta=True, tm_cap=128, out_shape=(2, N_SHARD, 128, D_MODEL), out_sel=row_unit,
                   name="dil_dwo")
        dos, dlts = _dil_mix_bwd(_mm(df, dil_w['out'], tb=True, name="dil_do"), sv['outs'], sv['lses'],
                                 name="dil_mix_bwd")
        pieces = []
        bias_rows = []
        for g, (window, d) in enumerate(DIL_GROUPS):
            dq, dk, dv, dbias = _dil_attn_bwd(sv['heads'], biases[g], sv['lses'][g], dos[g], dlts[g], g, d,
                                              name=f"dil_attn_bwd_g{g}")
            pieces += [dq, dk, dv]
            bias_rows.append(_bias_grad(dbias, buckets[g], name=f"dil_bias_grad_g{g}")[:, 0, :])
        dheads = jnp.concatenate(pieces).astype(BF16)
        u_win = _proj_heads_dw(sv['hn'], dheads, name="dil_dwin")
        dhn = _proj_heads_dx(dheads, dil_w['in'], name="dil_dhn")
        g_bias = jnp.concatenate(bias_rows, axis=0).T
        grads = dict(units=[u_win, u_wo], rel_bias=g_bias)
        return close_sub(dhn, dout, sv, i, sub, dgate, dqg), grads

    def to_sibling(units, tag):
        n = len(units)
        send, recv, thru, token = _copies_start(units, [lax.empty(u.shape[1:], F32) for u in units], _sibling_plan, n,
                                                name=f"rs{tag}_sibling_start")
        return dict(send=send, recv=recv, thru=thru, n=n, tag=tag), token[0, 0]

    def from_sibling(st, after):
        n, tag = st['n'], st['tag']
        thru = _copies_wait(st['send'], st['recv'], st['thru'], n, _sibling_plan, after, name=f"rs{tag}_sibling_wait")
        return [_add_half(u, g, half_idx, name=f"rs{tag}_add_half_{k}") for k, (u, g) in enumerate(zip(thru[:n], thru[n:]))]

    def to_chips(parts, tag):
        n = len(parts)
        send, recv, thru, token = _copies_start([w for _, w in parts],
                                                [lax.empty((3,) + w.shape[1:], BF16) for _, w in parts], _chips_plan,
                                                3 * n, name=f"rs{tag}_chips_start")
        return dict(send=send, recv=recv, thru=thru, n=n, tag=tag, parts=parts), token[0, 0]

    def from_chips(st, after):
        n, tag = st['n'], st['tag']
        thru = _copies_wait(st['send'], st['recv'], st['thru'], n, _chips_plan, after, name=f"rs{tag}_chips_wait")
        return [_add_shards(p, g, shard_idx, name=f"rs{tag}_add_shards_{k}")
                for k, ((p, _), g) in enumerate(zip(st['parts'], thru[n:]))]

    dx = ffn_bwd(dx, saved[5], 2)
    dx, dil_g = dil_bwd(dx, saved[4], 1, 1)
    dx = ffn_bwd(dx, saved[3], 0)
    st1, tok = to_sibling([*ffn_units[1, 1], *dil_g['units'], *ffn_units[1, 0]], "1")
    dx = ffn_bwd(dx, saved[2], 2, tie=tok)
    st1, tok1 = to_chips(from_sibling(st1, dx), "1")
    st2, tok2 = to_sibling(ffn_units[0, 1], "2")
    dx, mla_g = mla_bwd(dx, saved[1], 0, 1, tie=tok1 + tok2)
    reds1 = from_chips(st1, dx)
    st2, tok = to_chips(from_sibling(st2, dx), "2")
    st3, tok3 = to_sibling(mla_g['units'], "3")
    onward = {}

    def mixer_to_chips(after):
        onward['st'], t = to_chips(from_sibling(st3, after), "3")
        return t

    dx = ffn_bwd(dx, saved[0], 0, tie=tok + tok3, mid=mixer_to_chips)
    reds2 = from_chips(st2, dx)
    reds3 = from_chips(onward['st'], dx)
    grad_x = dx[None]

    pad_row = lambda v: jnp.pad(v.reshape(-1), (0, (-v.size) % D_MODEL)).reshape(-1, D_MODEL)
    small = jnp.concatenate(
        [jnp.concatenate([dmod[i][r] for i in range(2) for r in range(9)], axis=0),
         jnp.concatenate([dpre[i][s] for i in range(2) for s in range(3)], axis=0),
         jnp.concatenate([dpost[i][s] for i in range(2) for s in range(3)], axis=0),
         pad_row(mla_g['q_norm']), pad_row(mla_g['kv_norm']), pad_row(dil_g['rel_bias']), pad_row(loss_part)], axis=0)
    small = jnp.pad(small, ((0, SMALL_ROWS - small.shape[0]), (0, 0)))
    small_all = _all_gather(small, name="ag_small_grads", in_vmem=True)
    small_sum = _sum_devices(small_all, 8, name="sum_small_grads")
    g_b_mod = small_sum[0:18].reshape(2, 9 * D_MODEL)
    my_cols = lambda t: lax.dynamic_slice_in_dim(t, shard_id * 256, 256, axis=2)
    g_norm_pre = my_cols(small_sum[18:24].reshape(2, 3, D_MODEL))
    g_norm_post = my_cols(small_sum[24:30].reshape(2, 3, D_MODEL))
    g_q_norm = small_sum[30, :Q_LORA].reshape(1, Q_LORA)
    g_kv_norm = small_sum[31, :KV_LORA].reshape(1, KV_LORA)
    g_rel_bias = small_sum[32:34].reshape(-1)[:N_BUCKETS * 48].reshape(N_BUCKETS, 48)
    loss = small_sum[34, 0]
    dmod_all = small_all.reshape(8, SMALL_ROWS, D_MODEL)[:, 0:18].reshape(8, 2, 9 * D_MODEL)
    dmod_cols = lax.dynamic_slice_in_dim(dmod_all, shard_id * 2304, 2304, axis=2).transpose(1, 0, 2)
    g_w_mod = _mm(silu_c, dmod_cols.astype(BF16), ta=True, tn_cap=768, name="w_mod_grad")

    units0 = ffn_units[0, 0]
    got_a = _swap_halves(units0, name="rs0_sibling")
    parts0 = [_add_half(u, g, half_idx, name=f"rs0_add_half_{k}") for k, (u, g) in enumerate(zip(units0, got_a))]
    got_b = _send_to_chips([w for _, w in parts0], name="rs0_chips")
    reds0 = [_add_shards(p, g, shard_idx, name=f"rs0_add_shards_{k}")
             for k, ((p, _), g) in enumerate(zip(parts0, got_b))]
    fin = _pair_gather(reds1 + reds2 + reds3 + reds0, name="rs_pair_gather")
    ffn_fin = {(1, 1): fin[0:2], (1, 0): fin[4:6], (0, 1): fin[6:8], (0, 0): fin[12:14]}
    swap = lambda t: jnp.swapaxes(t, 2, 3)
    per_ffn = lambda pick: jnp.stack([jnp.stack([pick(*ffn_fin[i, h]) for h in range(2)]) for i in range(2)])
    reduced = dict(ffn_w_gate=swap(per_ffn(lambda gu, dn: gu[0])), ffn_w_up=swap(per_ffn(lambda gu, dn: gu[1])),
                   ffn_w_down=per_ffn(lambda gu, dn: jnp.concatenate([dn[0], dn[1]], axis=1)))
    for n, t in zip(['dil_w_in', 'dil_w_o', 'mla_w_in', 'mla_w_q_up', 'mla_w_kv_up', 'mla_w_o'], fin[2:4] + fin[8:12]):
        reduced[n] = t.reshape(given[n].shape)

    grads = dict(norm_pre=g_norm_pre, norm_post=g_norm_post, w_mod=g_w_mod, b_mod=g_b_mod, mla_q_norm=g_q_norm,
                 mla_kv_norm=g_kv_norm, rel_bias=g_rel_bias, **reduced)

    deltas, new_m, new_v = {}, {}, {}
    for n in WEIGHTS:
        view = swap if n in ('ffn_w_gate', 'ffn_w_up') else (lambda t: t)
        outs = _adamw(view(given[n]), view(grads[n]), view(given["m_" + n]), view(given["v_" + n]), name=f"adamw_{n}")
        deltas[n], new_m[n], new_v[n] = (view(t) for t in outs)
    return (loss, grad_x, *[grads[n] for n in WEIGHTS], *[deltas[n] for n in WEIGHTS],
            *[new_m[n] for n in WEIGHTS], *[new_v[n] for n in WEIGHTS])
```

```python
import math

import jax
import jax.numpy as jnp
from jax import lax
from jax.experimental import pallas as pl
from jax.experimental.pallas import tpu as pltpu

F32 = jnp.float32
BF16 = jnp.bfloat16
MESH = pl.DeviceIdType.MESH

SEQ = 2048
D_MODEL = 1024
D_FF = 2816
N_SHARD = 4
F_SHARD = D_FF // N_SHARD
EPS = 1e-6
FFN_RES = 0.5
HEADS = 16
Q_LORA, KV_LORA, QK_NOPE, QK_ROPE, V_HEAD = 384, 256, 64, 32, 64
HALF_ROPE = QK_ROPE // 2
ROPE_THETA = 10000.0
DIL_GROUPS = ((128, 1), (512, 4), (2048, 16))
DIL_BLOCK = 128
N_BUCKETS = 32
MAX_DISTANCE = 2048
ADAM_LR, ADAM_B1, ADAM_B2, ADAM_EPS, ADAM_WD, ADAM_STEP = 0.001, 0.9, 0.999, 1e-08, 0.01, 10

VMEM_LIMIT = 48 * 1024 * 1024
SMALL_ROWS = 40

WEIGHTS = ['norm_pre', 'norm_post', 'w_mod', 'b_mod', 'ffn_w_gate', 'ffn_w_up', 'ffn_w_down', 'mla_w_in',
           'mla_q_norm', 'mla_w_q_up', 'mla_kv_norm', 'mla_w_kv_up', 'mla_w_o', 'dil_w_in', 'dil_w_o', 'rel_bias']


def _cparams(**kw):
    return pltpu.CompilerParams(vmem_limit_bytes=VMEM_LIMIT, **kw)


def _pick(n, cap, mult=128):
    if n <= cap:
        return n
    best = n
    for t in range(mult, cap + 1, mult):
        if n % t == 0:
            best = t
    return best


def _mm(a, b, *, name, ta=False, tb=False, reduce_g=False, bias=None, out_dtype=F32, tm_cap=512, tn_cap=1024,
        g_n=None, b_sel=None, out_shape=None, out_sel=None, out_buf=None):
    a3 = a if a.ndim == 3 else a[None]
    ga = a3.shape[0]
    if b_sel is None:
        b_n = b if b.ndim == 3 else b[None]
        gb = b_n.shape[0]
        b_sel = (lambda g: (g,)) if gb > 1 else (lambda g: (0,))
        g_n = max(ga, gb)
    else:
        b_n = b
    k_dim, m_dim = (a3.shape[1], a3.shape[2]) if ta else (a3.shape[2], a3.shape[1])
    k2, n_dim = (b_n.shape[-1], b_n.shape[-2]) if tb else (b_n.shape[-2], b_n.shape[-1])
    assert k_dim == k2, (a.shape, b.shape)
    tm = _pick(m_dim, tm_cap, 128 if ta else 8)
    tn = _pick(n_dim, tn_cap, 128)
    mt, nt = m_dim // tm, n_dim // tn
    dims = (((0 if ta else 1,), (1 if tb else 0,)), ((), ()))

    if reduce_g:
        grid = (mt, nt, g_n)
        ids = lambda i, j, g: (g, i, j)
    else:
        grid = (g_n, mt, nt)
        ids = lambda g, i, j: (g, i, j)

    def a_map(*p):
        g, i, j = ids(*p)
        g = g if ga > 1 else 0
        return (g, 0, i) if ta else (g, i, 0)

    def b_map(*p):
        g, i, j = ids(*p)
        return (*b_sel(g), j, 0) if tb else (*b_sel(g), 0, j)

    b_lead = (None,) * (b_n.ndim - 2)
    a_spec = pl.BlockSpec((None, k_dim, tm) if ta else (None, tm, k_dim), a_map)
    b_spec = pl.BlockSpec(b_lead + ((tn, k_dim) if tb else (k_dim, tn)), b_map)
    in_specs = [a_spec, b_spec]
    operands = [a3, b_n]
    if bias is not None:
        assert not reduce_g and bias.shape == (g_n, 1, n_dim)
        in_specs.append(pl.BlockSpec((None, 1, tn), lambda g, i, j: (g, 0, j)))
        operands.append(bias)
    aliases = {}
    if out_buf is not None:
        assert tuple(out_buf.shape) == tuple(out_shape) and out_buf.dtype == out_dtype
        in_specs.append(pl.BlockSpec(memory_space=pl.ANY))
        operands.append(out_buf)
        aliases = {len(operands) - 1: 0}

    if reduce_g:
        out_spec = pl.BlockSpec((tm, tn), lambda i, j, g: (i, j))
        out_sds = jax.ShapeDtypeStruct((m_dim, n_dim), F32)
    elif out_shape is not None:
        def o_map(g, i, j):
            lead, rb, cb = out_sel(g, i, j)
            return (*lead, rb, cb)

        out_spec = pl.BlockSpec((None,) * (len(out_shape) - 2) + (tm, tn), o_map)
        out_sds = jax.ShapeDtypeStruct(tuple(out_shape), out_dtype)
    else:
        out_spec = pl.BlockSpec((None, tm, tn), lambda g, i, j: (g, i, j))
        out_sds = jax.ShapeDtypeStruct((g_n, m_dim, n_dim), out_dtype)

    def body(a_ref, b_ref, *rest):
        o_ref = rest[-1]
        r = lax.dot_general(a_ref[...].astype(BF16), b_ref[...].astype(BF16), dims, preferred_element_type=F32)
        if bias is not None:
            r = r + rest[0][...]
        if reduce_g:
            g = pl.program_id(2)

            @pl.when(g == 0)
            def _():
                o_ref[...] = r

            @pl.when(g > 0)
            def _():
                o_ref[...] += r
        else:
            o_ref[...] = r.astype(o_ref.dtype)

    out = pl.pallas_call(body, grid=grid, in_specs=in_specs, out_specs=out_spec, out_shape=out_sds,
                         input_output_aliases=aliases, compiler_params=_cparams(), name=name)(*operands)
    if not reduce_g and out_shape is None and a.ndim == 2 and b.ndim == 2:
        out = out[0]
    return out


def _rows(tm, w):
    return pl.BlockSpec((tm, w), lambda i: (i, 0))


def _vec(w):
    return pl.BlockSpec((1, w), lambda i: (0, 0))


def _rstd(v):
    return lax.rsqrt(jnp.mean(v * v, axis=-1, keepdims=True) + EPS)


def _pre_fwd(x, pg, sc, sh, *, name):
    s_n, w = x.shape
    tm = _pick(s_n, 512, 8)

    def body(x_ref, pg_ref, sc_ref, sh_ref, o_ref):
        xv = x_ref[...]
        n = (xv * _rstd(xv)) * pg_ref[...]
        o_ref[...] = (n * (1.0 + sc_ref[...]) + sh_ref[...]).astype(o_ref.dtype)

    return pl.pallas_call(body, grid=(s_n // tm,), in_specs=[_rows(tm, w), _vec(w), _vec(w), _vec(w)],
                          out_specs=_rows(tm, w), out_shape=jax.ShapeDtypeStruct((s_n, w), BF16),
                          compiler_params=_cparams(), name=name)(x, pg, sc, sh)


def _post_fwd(f, x, qg, gate, res_w, *, name):
    s_n, w = x.shape
    tm = _pick(s_n, 512, 8)

    def body(f_ref, x_ref, qg_ref, gate_ref, o_ref):
        fv = f_ref[...]
        y = (fv * _rstd(fv)) * qg_ref[...]
        o_ref[...] = x_ref[...] + (res_w * gate_ref[...]) * y

    return pl.pallas_call(body, grid=(s_n // tm,), in_specs=[_rows(tm, w), _rows(tm, w), _vec(w), _vec(w)],
                          out_specs=_rows(tm, w), out_shape=jax.ShapeDtypeStruct((s_n, w), F32),
                          compiler_params=_cparams(), name=name)(f, x, qg, gate)


def _post_bwd(dout, f, qg, gate, res_w, *, name):
    s_n, w = f.shape
    tm = _pick(s_n, 512, 8)

    def body(do_ref, f_ref, qg_ref, gate_ref, df_ref, dgate_ref, dqg_ref):
        @pl.when(pl.program_id(0) == 0)
        def _():
            dgate_ref[...] = jnp.zeros_like(dgate_ref)
            dqg_ref[...] = jnp.zeros_like(dqg_ref)

        do = do_ref[...]
        fv = f_ref[...]
        r = _rstd(fv)
        fh = fv * r
        qg_v = qg_ref[...]
        dgate_ref[...] += res_w * jnp.sum(do * (fh * qg_v), axis=0, keepdims=True)
        dy = do * (res_w * gate_ref[...])
        dqg_ref[...] += jnp.sum(dy * fh, axis=0, keepdims=True)
        dfh = dy * qg_v
        df = r * (dfh - fh * jnp.mean(dfh * fh, axis=-1, keepdims=True))
        df_ref[...] = df.astype(df_ref.dtype)

    return pl.pallas_call(
        body, grid=(s_n // tm,), in_specs=[_rows(tm, w), _rows(tm, w), _vec(w), _vec(w)],
        out_specs=[_rows(tm, w), _vec(w), _vec(w)],
        out_shape=[jax.ShapeDtypeStruct((s_n, w), BF16), jax.ShapeDtypeStruct((1, w), F32),
                   jax.ShapeDtypeStruct((1, w), F32)],
        compiler_params=_cparams(), name=name)(dout, f, qg, gate)


def _pre_bwd(dhn, x, dout, pg, sc, *, name):
    s_n, w = x.shape
    tm = _pick(s_n, 512, 8)

    def body(dhn_ref, x_ref, do_ref, pg_ref, sc_ref, dx_ref, dsh_ref, dsc_ref, dpg_ref):
        @pl.when(pl.program_id(0) == 0)
        def _():
            dsh_ref[...] = jnp.zeros_like(dsh_ref)
            dsc_ref[...] = jnp.zeros_like(dsc_ref)
            dpg_ref[...] = jnp.zeros_like(dpg_ref)

        dhn_v = dhn_ref[...]
        xv = x_ref[...]
        r = _rstd(xv)
        xh = xv * r
        pg_v = pg_ref[...]
        dsh_ref[...] += jnp.sum(dhn_v, axis=0, keepdims=True)
        dsc_ref[...] += jnp.sum(dhn_v * (xh * pg_v), axis=0, keepdims=True)
        dn = dhn_v * (1.0 + sc_ref[...])
        dpg_ref[...] += jnp.sum(dn * xh, axis=0, keepdims=True)
        dxh = dn * pg_v
        dx_ref[...] = do_ref[...] + r * (dxh - xh * jnp.mean(dxh * xh, axis=-1, keepdims=True))

    vec = jax.ShapeDtypeStruct((1, w), F32)
    return pl.pallas_call(
        body, grid=(s_n // tm,), in_specs=[_rows(tm, w), _rows(tm, w), _rows(tm, w), _vec(w), _vec(w)],
        out_specs=[_rows(tm, w), _vec(w), _vec(w), _vec(w)],
        out_shape=[jax.ShapeDtypeStruct((s_n, w), F32), vec, vec, vec],
        compiler_params=_cparams(), name=name)(dhn, x, dout, pg, sc)


def _rms_fwd(x, g, *, name):
    s_n, w = x.shape
    tm = _pick(s_n, 512, 8)

    def body(x_ref, g_ref, o_ref):
        xv = x_ref[...]
        o_ref[...] = ((xv * _rstd(xv)) * g_ref[...]).astype(o_ref.dtype)

    return pl.pallas_call(body, grid=(s_n // tm,), in_specs=[_rows(tm, w), _vec(w)], out_specs=_rows(tm, w),
                          out_shape=jax.ShapeDtypeStruct((s_n, w), BF16), compiler_params=_cparams(),
                          name=name)(x, g)


def _rms_bwd(dy, x, g, *, name):
    s_n, w = x.shape
    tm = _pick(s_n, 512, 8)

    def body(dy_ref, x_ref, g_ref, dx_ref, dg_ref):
        @pl.when(pl.program_id(0) == 0)
        def _():
            dg_ref[...] = jnp.zeros_like(dg_ref)

        dy_v = dy_ref[...]
        xv = x_ref[...]
        r = _rstd(xv)
        xh = xv * r
        dg_ref[...] += jnp.sum(dy_v * xh, axis=0, keepdims=True)
        dxh = dy_v * g_ref[...]
        dx_ref[...] = r * (dxh - xh * jnp.mean(dxh * xh, axis=-1, keepdims=True))

    return pl.pallas_call(
        body, grid=(s_n // tm,), in_specs=[_rows(tm, w), _rows(tm, w), _vec(w)],
        out_specs=[_rows(tm, w), _vec(w)],
        out_shape=[jax.ShapeDtypeStruct((s_n, w), F32), jax.ShapeDtypeStruct((1, w), F32)],
        compiler_params=_cparams(), name=name)(dy, x, g)


def _rope(a1, a2, cos, sin, *, name):
    s_n, w = a1.shape
    tm = _pick(s_n, 512, 8)

    def body(a1_ref, a2_ref, c_ref, s_ref, r1_ref, r2_ref):
        u, v, c_v, s_v = a1_ref[...], a2_ref[...], c_ref[...], s_ref[...]
        r1_ref[...] = u * c_v - v * s_v
        r2_ref[...] = u * s_v + v * c_v

    sd = jax.ShapeDtypeStruct((s_n, w), F32)
    return pl.pallas_call(body, grid=(s_n // tm,), in_specs=[_rows(tm, w)] * 4, out_specs=[_rows(tm, w)] * 2,
                          out_shape=[sd, sd], compiler_params=_cparams(), name=name)(a1, a2, cos, sin)


def _silu_bf16(x, *, name):
    def body(x_ref, o_ref):
        xv = x_ref[...]
        o_ref[...] = (xv * jax.nn.sigmoid(xv)).astype(o_ref.dtype)

    return pl.pallas_call(body, out_shape=jax.ShapeDtypeStruct(x.shape, BF16), name=name)(x)


def _loss(y, target, *, name):
    s_n, w = y.shape
    tm = _pick(s_n, 512, 8)

    def body(y_ref, t_ref, dy_ref, l_ref):
        @pl.when(pl.program_id(0) == 0)
        def _():
            l_ref[...] = jnp.zeros_like(l_ref)

        e = y_ref[...] - t_ref[...]
        dy_ref[...] = e * (1.0 / w)
        row = jnp.mean(e * e, axis=-1, keepdims=True)
        l_ref[...] += 0.5 * jnp.sum(row, axis=0, keepdims=True)

    return pl.pallas_call(
        body, grid=(s_n // tm,), in_specs=[_rows(tm, w), _rows(tm, w)],
        out_specs=[_rows(tm, w), pl.BlockSpec((1, 1), lambda i: (0, 0))],
        out_shape=[jax.ShapeDtypeStruct((s_n, w), F32), jax.ShapeDtypeStruct((1, 1), F32)],
        compiler_params=_cparams(), name=name)(y, target)


FFN_TM = 512


def _ffn_up(hn, w_gu, *, name):
    s_n, d = hn.shape
    f = w_gu.shape[-1]
    tm = _pick(s_n, FFN_TM, 8)

    def body(hn_ref, wg_ref, wu_ref, gu_ref, a_ref):
        xv = hn_ref[...]
        g = jnp.dot(xv, wg_ref[...], preferred_element_type=F32)
        u = jnp.dot(xv, wu_ref[...], preferred_element_type=F32)
        gu_ref[0] = g.astype(BF16)
        gu_ref[1] = u.astype(BF16)
        a_ref[...] = ((g * jax.nn.sigmoid(g)) * u).astype(BF16)

    w_blk = lambda t: pl.BlockSpec((None, None, d, f), lambda s, m: (s, t, 0, 0))
    return pl.pallas_call(
        body, grid=(N_SHARD, s_n // tm),
        in_specs=[pl.BlockSpec((tm, d), lambda s, m: (m, 0)), w_blk(0), w_blk(1)],
        out_specs=[pl.BlockSpec((None, 2, tm, f), lambda s, m: (s, 0, m, 0)),
                   pl.BlockSpec((None, tm, f), lambda s, m: (s, m, 0))],
        out_shape=[jax.ShapeDtypeStruct((N_SHARD, 2, s_n, f), BF16), jax.ShapeDtypeStruct((N_SHARD, s_n, f), BF16)],
        compiler_params=_cparams(), name=name)(hn, w_gu, w_gu)


def _ffn_down(a, w_dn, x, qg, gate, res_w, *, name):
    _, s_n, f = a.shape
    d = w_dn.shape[-1]
    tm = _pick(s_n, FFN_TM, 8)
    a = a.reshape(-1, 2, s_n, f)
    w_dn = w_dn.reshape(-1, 2, f, d)
    g_n = a.shape[0]

    def body(a_ref, w_ref, x_ref, qg_ref, gate_ref, f_ref, o_ref):
        g = pl.program_id(1)
        r = (jnp.dot(a_ref[0], w_ref[0], preferred_element_type=F32)
             + jnp.dot(a_ref[1], w_ref[1], preferred_element_type=F32))

        @pl.when(g == 0)
        def _():
            f_ref[...] = r

        @pl.when(g > 0)
        def _():
            f_ref[...] += r

        @pl.when(g == g_n - 1)
        def _():
            fv = f_ref[...]
            y = (fv * _rstd(fv)) * qg_ref[...]
            o_ref[...] = x_ref[...] + (res_w * gate_ref[...]) * y

    row = pl.BlockSpec((tm, d), lambda m, g: (m, 0))
    vec = pl.BlockSpec((1, d), lambda m, g: (0, 0))
    sd = jax.ShapeDtypeStruct((s_n, d), F32)
    return pl.pallas_call(
        body, grid=(s_n // tm, g_n),
        in_specs=[pl.BlockSpec((None, 2, tm, f), lambda m, g: (g, 0, m, 0)),
                  pl.BlockSpec((None, 2, f, d), lambda m, g: (g, 0, 0, 0)), row, vec, vec],
        out_specs=[row, row], out_shape=[sd, sd], compiler_params=_cparams(), name=name)(a, w_dn, x, qg, gate)


def _ffn_dhn(dgu, w_gu, x, dout, pg, sc, *, name):
    g_n, _, s_n, f = dgu.shape
    d = w_gu.shape[-2]
    tm = _pick(s_n, FFN_TM, 8)
    nt_dims = (((1,), (1,)), ((), ()))

    def body(a_ref, w_ref, x_ref, do_ref, pg_ref, sc_ref, dx_ref, dsh_ref, dsc_ref, dpg_ref, acc_ref):
        m, g = pl.program_id(0), pl.program_id(1)
        r = (lax.dot_general(a_ref[0], w_ref[0], nt_dims, preferred_element_type=F32)
             + lax.dot_general(a_ref[1], w_ref[1], nt_dims, preferred_element_type=F32))

        @pl.when(g == 0)
        def _():
            acc_ref[...] = r

        @pl.when(g > 0)
        def _():
            acc_ref[...] += r

        @pl.when((m == 0) & (g == 0))
        def _():
            dsh_ref[...] = jnp.zeros_like(dsh_ref)
            dsc_ref[...] = jnp.zeros_like(dsc_ref)
            dpg_ref[...] = jnp.zeros_like(dpg_ref)

        @pl.when(g == g_n - 1)
        def _():
            dhn_v = acc_ref[...]
            xv = x_ref[...]
            rs = _rstd(xv)
            xh = xv * rs
            pg_v = pg_ref[...]
            dsh_ref[...] += jnp.sum(dhn_v, axis=0, keepdims=True)
            dsc_ref[...] += jnp.sum(dhn_v * (xh * pg_v), axis=0, keepdims=True)
            dn = dhn_v * (1.0 + sc_ref[...])
            dpg_ref[...] += jnp.sum(dn * xh, axis=0, keepdims=True)
            dxh = dn * pg_v
            dx_ref[...] = do_ref[...] + rs * (dxh - xh * jnp.mean(dxh * xh, axis=-1, keepdims=True))

    row = pl.BlockSpec((tm, d), lambda m, g: (m, 0))
    vec = pl.BlockSpec((1, d), lambda m, g: (0, 0))
    vsd = jax.ShapeDtypeStruct((1, d), F32)
    return pl.pallas_call(
        body, grid=(s_n // tm, g_n),
        in_specs=[pl.BlockSpec((None, 2, tm, f), lambda m, g: (g, 0, m, 0)),
                  pl.BlockSpec((None, 2, d, f), lambda m, g: (g, 0, 0, 0)), row, row, vec, vec],
        out_specs=[row, vec, vec, vec], out_shape=[jax.ShapeDtypeStruct((s_n, d), F32), vsd, vsd, vsd],
        scratch_shapes=[pltpu.VMEM((tm, d), F32)], compiler_params=_cparams(), name=name)(dgu, w_gu, x, dout, pg, sc)


def _ffn_dgu(df, w_dn, gu, *, name):
    s_n, d = df.shape
    f = w_dn.shape[-2]
    tm = _pick(s_n, FFN_TM, 8)

    def body(df_ref, wd_ref, gu_ref, o_ref):
        da = lax.dot_general(df_ref[...], wd_ref[...], (((1,), (1,)), ((), ())), preferred_element_type=F32)
        g = gu_ref[0].astype(F32)
        u = gu_ref[1].astype(F32)
        sig = jax.nn.sigmoid(g)
        o_ref[0] = (da * u * (sig * (1.0 + g * (1.0 - sig)))).astype(BF16)
        o_ref[1] = (da * (g * sig)).astype(BF16)

    gu_blk = pl.BlockSpec((None, 2, tm, f), lambda s, m: (s, 0, m, 0))
    return pl.pallas_call(
        body, grid=(N_SHARD, s_n // tm),
        in_specs=[pl.BlockSpec((tm, d), lambda s, m: (m, 0)),
                  pl.BlockSpec((None, f, d), lambda s, m: (s, 0, 0)), gu_blk],
        out_specs=gu_blk, out_shape=jax.ShapeDtypeStruct((N_SHARD, 2, s_n, f), BF16),
        compiler_params=_cparams(), name=name)(df, w_dn, gu)


_NT = (((1,), (1,)), ((), ()))
_TN = (((0,), (0,)), ((), ()))
MLA_TQ = 256


def _causal_mask(i, tq, s_n):
    qpos = i * tq + lax.broadcasted_iota(jnp.int32, (tq, s_n), 0)
    kpos = lax.broadcasted_iota(jnp.int32, (tq, s_n), 1)
    return kpos <= qpos


def _mla_attn_fwd(q, k, v, *, name):
    h_n, s_n, dq = q.shape
    dv = v.shape[-1]
    tq = MLA_TQ
    scale = float(dq) ** -0.5

    def body(q_ref, k_ref, v_ref, o_ref, lse_ref):
        i = pl.program_id(1)
        for e in range(1, s_n // tq + 1):
            @pl.when(i == e - 1)
            def _(ext=e * tq):
                mask = _causal_mask(i, tq, ext)
                s = lax.dot_general(q_ref[...], k_ref[0:ext, :], _NT, preferred_element_type=F32) * scale
                s = jnp.where(mask, s, -jnp.inf)
                m = jnp.max(s, axis=-1, keepdims=True)
                p = jnp.exp(s - m)
                l = jnp.sum(p, axis=-1, keepdims=True)
                o = jnp.dot(p.astype(BF16), v_ref[0:ext, :], preferred_element_type=F32)
                o_ref[...] = o / l
                lse_ref[...] = m + jnp.log(l)

    return pl.pallas_call(
        body, grid=(h_n, s_n // tq),
        in_specs=[pl.BlockSpec((None, tq, dq), lambda h, i: (h, i, 0)),
                  pl.BlockSpec((None, s_n, dq), lambda h, i: (h, 0, 0)),
                  pl.BlockSpec((None, s_n, dv), lambda h, i: (h, 0, 0))],
        out_specs=[pl.BlockSpec((None, tq, dv), lambda h, i: (h, i, 0)),
                   pl.BlockSpec((None, tq, 1), lambda h, i: (h, i, 0))],
        out_shape=[jax.ShapeDtypeStruct((h_n, s_n, dv), F32), jax.ShapeDtypeStruct((h_n, s_n, 1), F32)],
        compiler_params=_cparams(), name=name)(q, k, v)


def _mla_attn_bwd(q, k, v, o, do, lse, *, name):
    h_n, s_n, dq = q.shape
    dv = v.shape[-1]
    tq = MLA_TQ
    scale = float(dq) ** -0.5

    def body(q_ref, k_ref, v_ref, o_ref, do_ref, lse_ref, dq_ref, dk_ref, dv_ref):
        i = pl.program_id(1)

        @pl.when(i == 0)
        def _():
            dk_ref[...] = jnp.zeros_like(dk_ref)
            dv_ref[...] = jnp.zeros_like(dv_ref)

        for e in range(1, s_n // tq + 1):
            @pl.when(i == e - 1)
            def _(ext=e * tq):
                mask = _causal_mask(i, tq, ext)
                qv, kv, vv = q_ref[...], k_ref[0:ext, :], v_ref[0:ext, :]
                do_v = do_ref[...]
                s = lax.dot_general(qv, kv, _NT, preferred_element_type=F32) * scale
                p = jnp.where(mask, jnp.exp(s - lse_ref[...]), 0.0)
                dob = do_v.astype(BF16)
                dv_ref[0:ext, :] += lax.dot_general(p.astype(BF16), dob, _TN, preferred_element_type=F32)
                dp = lax.dot_general(dob, vv, _NT, preferred_element_type=F32)
                delta = jnp.sum(do_v * o_ref[...], axis=-1, keepdims=True)
                dsb = (p * (dp - delta) * scale).astype(BF16)
                dq_ref[...] = jnp.dot(dsb, kv, preferred_element_type=F32)
                dk_ref[0:ext, :] += lax.dot_general(dsb, qv, _TN, preferred_element_type=F32)

    return pl.pallas_call(
        body, grid=(h_n, s_n // tq),
        in_specs=[pl.BlockSpec((None, tq, dq), lambda h, i: (h, i, 0)),
                  pl.BlockSpec((None, s_n, dq), lambda h, i: (h, 0, 0)),
                  pl.BlockSpec((None, s_n, dv), lambda h, i: (h, 0, 0)),
                  pl.BlockSpec((None, tq, dv), lambda h, i: (h, i, 0)),
                  pl.BlockSpec((None, tq, dv), lambda h, i: (h, i, 0)),
                  pl.BlockSpec((None, tq, 1), lambda h, i: (h, i, 0))],
        out_specs=[pl.BlockSpec((None, tq, dq), lambda h, i: (h, i, 0)),
                   pl.BlockSpec((None, s_n, dq), lambda h, i: (h, 0, 0)),
                   pl.BlockSpec((None, s_n, dv), lambda h, i: (h, 0, 0))],
        out_shape=[jax.ShapeDtypeStruct((h_n, s_n, dq), F32), jax.ShapeDtypeStruct((h_n, s_n, dq), F32),
                   jax.ShapeDtypeStruct((h_n, s_n, dv), F32)],
        compiler_params=_cparams(), name=name)(q, k, v, o, do, lse)


def _head_sum(x, *, name):
    h_n, s_n, w = x.shape
    tm = _pick(s_n, 512, 8)

    def body(x_ref, o_ref):
        o_ref[...] = jnp.sum(x_ref[...], axis=0)

    return pl.pallas_call(body, grid=(s_n // tm,), in_specs=[pl.BlockSpec((h_n, tm, w), lambda i: (0, i, 0))],
                          out_specs=_rows(tm, w), out_shape=jax.ShapeDtypeStruct((s_n, w), F32),
                          compiler_params=_cparams(), name=name)(x)


N_BLK = SEQ // DIL_BLOCK
DIL_SCALE = 64 ** -0.5


def _dil_masks():
    iq = lax.broadcasted_iota(jnp.int32, (DIL_BLOCK, 2 * DIL_BLOCK), 0)
    ik = lax.broadcasted_iota(jnp.int32, (DIL_BLOCK, 2 * DIL_BLOCK), 1)
    rel = DIL_BLOCK + iq - ik
    both = (rel >= 0) & (rel <= DIL_BLOCK)
    iq1 = lax.broadcasted_iota(jnp.int32, (DIL_BLOCK, DIL_BLOCK), 0)
    ik1 = lax.broadcasted_iota(jnp.int32, (DIL_BLOCK, DIL_BLOCK), 1)
    return both, ik1 <= iq1


def _dil_block(j, d):
    nb = SEQ // d // DIL_BLOCK
    r, n = divmod(j, nb)
    first = n == 0
    rows = lambda start, size: pl.ds(start, size) if d == 1 else pl.ds(start, size, stride=d)
    q_rows = rows(n * DIL_BLOCK * d + r, DIL_BLOCK)
    k_rows = q_rows if first else rows((n - 1) * DIL_BLOCK * d + r, 2 * DIL_BLOCK)
    return q_rows, k_rows, (DIL_BLOCK if first else 0), first


PAIR = 2 * 64
N_PAIR = HEADS // 2


def _dil_head_specs(s_n, g):
    return [pl.BlockSpec((None, s_n, PAIR), lambda hp, t=t: ((g * 3 + t) * N_PAIR + hp, 0, 0)) for t in range(3)]


def _pair_specs(s_n, w):
    return pl.BlockSpec((2, s_n, w), lambda hp: (hp, 0, 0))


_PAIR_BIAS = pl.BlockSpec((2, DIL_BLOCK, 2 * DIL_BLOCK), lambda hp: (hp, 0, 0))


def _dil_attn_fwd(heads, bias, g, d, *, name):
    _, s_n, _ = heads.shape
    e = PAIR // 2

    def body(q_ref, k_ref, v_ref, b_ref, o_ref, lse_ref):
        m_both, m_first = _dil_masks()
        for j in range(N_BLK):
            q_rows, k_rows, b_lo, first = _dil_block(j, d)
            q2 = q_ref[q_rows, :].astype(BF16)
            k2 = k_ref[k_rows, :].astype(BF16)
            v2 = v_ref[k_rows, :].astype(BF16)
            for hh in range(2):
                cols = slice(hh * e, (hh + 1) * e)
                s = (lax.dot_general(q2[:, cols], k2[:, cols], _NT, preferred_element_type=F32) * DIL_SCALE
                     + b_ref[hh, :, b_lo:])
                s = jnp.where(m_first if first else m_both, s, -jnp.inf)
                m = jnp.max(s, axis=-1, keepdims=True)
                lse = m + jnp.log(jnp.sum(jnp.exp(s - m), axis=-1, keepdims=True))
                p = jnp.exp(s - lse)
                o_ref[hh, q_rows, :] = jnp.dot(p.astype(BF16), v2[:, cols], preferred_element_type=F32)
                lse_ref[hh, q_rows, :] = lse

    return pl.pallas_call(
        body, grid=(N_PAIR,), in_specs=_dil_head_specs(s_n, g) + [_PAIR_BIAS],
        out_specs=[_pair_specs(s_n, e), _pair_specs(s_n, 1)],
        out_shape=[jax.ShapeDtypeStruct((HEADS, s_n, e), F32), jax.ShapeDtypeStruct((HEADS, s_n, 1), F32)],
        compiler_params=_cparams(), name=name)(heads, heads, heads, bias)


def _dil_attn_bwd(heads, bias, lse, do, dlt, g, d, *, name):
    _, s_n, _ = heads.shape
    e = PAIR // 2

    def body(q_ref, k_ref, v_ref, b_ref, lse_ref, do_ref, dlt_ref, dq_ref, dk_ref, dv_ref, db_ref):
        db_ref[...] = jnp.zeros_like(db_ref)
        m_both, m_first = _dil_masks()
        nb = s_n // d // DIL_BLOCK
        own_v = own_k = own_rows = None
        for j in range(N_BLK):
            q_rows, k_rows, b_lo, first = _dil_block(j, d)
            q2 = q_ref[q_rows, :].astype(BF16)
            k2 = k_ref[k_rows, :].astype(BF16)
            v2 = v_ref[k_rows, :].astype(BF16)
            dq_h, dv_h, dk_h = [], [], []
            for hh in range(2):
                cols = slice(hh * e, (hh + 1) * e)
                qj, kk, vv = q2[:, cols], k2[:, cols], v2[:, cols]
                s = lax.dot_general(qj, kk, _NT, preferred_element_type=F32) * DIL_SCALE + b_ref[hh, :, b_lo:]
                p = jnp.where(m_first if first else m_both, jnp.exp(s - lse_ref[hh, q_rows, :]), 0.0)
                dob = do_ref[hh, q_rows, :].astype(BF16)
                dv_h.append(lax.dot_general(p.astype(BF16), dob, _TN, preferred_element_type=F32))
                dp = lax.dot_general(dob, vv, _NT, preferred_element_type=F32)
                ds = p * (dp - dlt_ref[hh, q_rows, :])
                db_ref[hh, :, b_lo:] += ds
                dsb = (ds * DIL_SCALE).astype(BF16)
                dq_h.append(jnp.dot(dsb, kk, preferred_element_type=F32))
                dk_h.append(lax.dot_general(dsb, qj, _TN, preferred_element_type=F32))
            dq_ref[q_rows, :] = jnp.concatenate(dq_h, axis=1)
            dvv, dkk = jnp.concatenate(dv_h, axis=1), jnp.concatenate(dk_h, axis=1)
            if not first:
                dv_ref[own_rows, :] = own_v + dvv[:DIL_BLOCK]
                dk_ref[own_rows, :] = own_k + dkk[:DIL_BLOCK]
                dvv, dkk = dvv[DIL_BLOCK:], dkk[DIL_BLOCK:]
            own_v, own_k, own_rows = dvv, dkk, q_rows
            if j % nb == nb - 1:
                dv_ref[own_rows, :] = own_v
                dk_ref[own_rows, :] = own_k

    slab = pl.BlockSpec((None, s_n, PAIR), lambda hp: (hp, 0, 0))
    sd = jax.ShapeDtypeStruct((N_PAIR, s_n, PAIR), F32)
    return pl.pallas_call(
        body, grid=(N_PAIR,),
        in_specs=_dil_head_specs(s_n, g) + [_PAIR_BIAS, _pair_specs(s_n, 1), _pair_specs(s_n, e), _pair_specs(s_n, 1)],
        out_specs=[slab, slab, slab, _PAIR_BIAS],
        out_shape=[sd, sd, sd, jax.ShapeDtypeStruct((HEADS, DIL_BLOCK, 2 * DIL_BLOCK), F32)],
        compiler_params=_cparams(), name=name)(heads, heads, heads, bias, lse, do, dlt)


def _proj_heads(x, w, *, name):
    s_n, k = x.shape
    n = w.shape[-1]
    tm, tn, e = 512, 768, PAIR
    per_blk, n_blk = tn // e, n // tn

    def body(x_ref, w_ref, o_ref):
        r = jnp.dot(x_ref[...], w_ref[...], preferred_element_type=F32)
        for j in range(per_blk):
            o_ref[j] = r[:, e * j:e * (j + 1)]

    return pl.pallas_call(
        body, grid=(w.shape[0], n_blk, s_n // tm),
        in_specs=[pl.BlockSpec((tm, k), lambda s, b, m: (m, 0)), pl.BlockSpec((None, k, tn), lambda s, b, m: (s, 0, b))],
        out_specs=pl.BlockSpec((per_blk, tm, e), lambda s, b, m: (s * n_blk + b, m, 0)),
        out_shape=jax.ShapeDtypeStruct((w.shape[0] * n // e, s_n, e), F32), compiler_params=_cparams(),
        name=name)(x, w)


def _heads_cat(d_ref):
    return jnp.concatenate([d_ref[j] for j in range(d_ref.shape[0])], axis=1)


def _proj_heads_dw(x, dh, *, name):
    s_n, k = x.shape
    tn, e = 768, PAIR
    per_blk = tn // e
    n_blk = dh.shape[0] // N_SHARD // per_blk
    n = n_blk * tn

    def body(x_ref, d_ref, o_ref):
        o_ref[...] = lax.dot_general(x_ref[...], _heads_cat(d_ref), _TN, preferred_element_type=F32)

    return pl.pallas_call(
        body, grid=(N_SHARD, n_blk, 2),
        in_specs=[pl.BlockSpec((s_n, k // 2), lambda s, b, r: (0, r)),
                  pl.BlockSpec((per_blk, s_n, e), lambda s, b, r: (s * n_blk + b, 0, 0))],
        out_specs=pl.BlockSpec((None, None, k // 2, tn), lambda s, b, r: (r, s, 0, b)),
        out_shape=jax.ShapeDtypeStruct((2, N_SHARD, k // 2, n), F32), compiler_params=_cparams(), name=name)(x, dh)


def _proj_heads_dx(dh, w, *, name):
    k, n = w.shape[1:]
    s_n = dh.shape[1]
    tm, tn, e = 512, 768, PAIR
    per_blk, n_blk = tn // e, n // tn

    def body(d_ref, w_ref, o_ref):
        r = lax.dot_general(_heads_cat(d_ref), w_ref[...], _NT, preferred_element_type=F32)
        g = pl.program_id(1)

        @pl.when(g == 0)
        def _():
            o_ref[...] = r

        @pl.when(g > 0)
        def _():
            o_ref[...] += r

    return pl.pallas_call(
        body, grid=(s_n // tm, N_SHARD * n_blk),
        in_specs=[pl.BlockSpec((per_blk, tm, e), lambda m, g: (g, m, 0)),
                  pl.BlockSpec((None, k, tn), lambda m, g: (g // n_blk, 0, g % n_blk))],
        out_specs=pl.BlockSpec((tm, k), lambda m, g: (m, 0)),
        out_shape=jax.ShapeDtypeStruct((s_n, k), F32), compiler_params=_cparams(), name=name)(dh, w)


def _group_alpha(ls):
    m = jnp.maximum(jnp.maximum(ls[0], ls[1]), ls[2])
    es = [jnp.exp(l - m) for l in ls]
    tot = es[0] + es[1] + es[2]
    return [ex / tot for ex in es]


def _dil_mix_fwd(os_, ls_, *, name):
    h_n, s_n, e = os_[0].shape
    tm = 512

    def body(o0, o1, o2, l0, l1, l2, out_ref):
        for hh in range(2):
            al = _group_alpha([l[hh] for l in (l0, l1, l2)])
            mix = al[0] * o0[hh] + al[1] * o1[hh] + al[2] * o2[hh]
            out_ref[:, hh * e:(hh + 1) * e] = mix.astype(out_ref.dtype)

    blk = lambda w: pl.BlockSpec((2, tm, w), lambda h, i: (h, i, 0))
    return pl.pallas_call(body, grid=(h_n // 2, s_n // tm), in_specs=[blk(e)] * 3 + [blk(1)] * 3,
                          out_specs=pl.BlockSpec((tm, 2 * e), lambda h, i: (i, h)),
                          out_shape=jax.ShapeDtypeStruct((s_n, h_n * e), BF16), compiler_params=_cparams(),
                          name=name)(*os_, *ls_)


def _dil_mix_bwd(do_flat, os_, ls_, *, name):
    h_n, s_n, e = os_[0].shape
    tm = 512

    def body(do_ref, o0, o1, o2, l0, l1, l2, d0, d1, d2, t0, t1, t2):
        for hh in range(2):
            al = _group_alpha([l[hh] for l in (l0, l1, l2)])
            do_v = do_ref[:, hh * e:(hh + 1) * e]
            mix = al[0] * o0[hh] + al[1] * o1[hh] + al[2] * o2[hh]
            dbar = jnp.sum(do_v * mix, axis=-1, keepdims=True)
            for a_g, d_ref, t_ref in zip(al, (d0, d1, d2), (t0, t1, t2)):
                d_ref[hh] = a_g * do_v
                t_ref[hh] = a_g * dbar

    blk = lambda w: pl.BlockSpec((2, tm, w), lambda h, i: (h, i, 0))
    sd_e = jax.ShapeDtypeStruct((h_n, s_n, e), F32)
    sd_1 = jax.ShapeDtypeStruct((h_n, s_n, 1), F32)
    outs = pl.pallas_call(body, grid=(h_n // 2, s_n // tm),
                          in_specs=[pl.BlockSpec((tm, 2 * e), lambda h, i: (i, h))] + [blk(e)] * 3 + [blk(1)] * 3,
                          out_specs=[blk(e)] * 3 + [blk(1)] * 3, out_shape=[sd_e] * 3 + [sd_1] * 3,
                          compiler_params=_cparams(), name=name)(do_flat, *os_, *ls_)
    return outs[:3], outs[3:]


def _bias_grad(ds, bucket, *, name):
    h_n = ds.shape[0]

    def body(ds_ref, bk_ref, o_ref):
        ds_v = ds_ref[...]
        bk = bk_ref[...]
        lane = lax.broadcasted_iota(jnp.int32, (1, N_BUCKETS), 1)
        acc = jnp.zeros((1, N_BUCKETS), F32)
        for b in range(N_BUCKETS):
            tot = jnp.sum(jnp.sum(jnp.where(bk == b, ds_v, 0.0), axis=1, keepdims=True), axis=0, keepdims=True)
            acc = acc + jnp.where(lane == b, tot, 0.0)
        o_ref[...] = acc

    return pl.pallas_call(
        body, grid=(h_n,),
        in_specs=[pl.BlockSpec((None, DIL_BLOCK, 2 * DIL_BLOCK), lambda h: (h, 0, 0)),
                  pl.BlockSpec((DIL_BLOCK, 2 * DIL_BLOCK), lambda h: (0, 0))],
        out_specs=pl.BlockSpec((None, 1, N_BUCKETS), lambda h: (h, 0, 0)),
        out_shape=jax.ShapeDtypeStruct((h_n, 1, N_BUCKETS), F32), compiler_params=_cparams(), name=name)(ds, bucket)


def _bias_table(rb, bucket, *, name):
    h_n = rb.shape[0]

    def body(rb_ref, bk_ref, o_ref):
        bk = bk_ref[...]
        row = rb_ref[...]
        acc = jnp.zeros(bk.shape, F32)
        for b in range(N_BUCKETS):
            acc = jnp.where(bk == b, row[:, b:b + 1], acc)
        o_ref[...] = acc

    return pl.pallas_call(
        body, grid=(h_n,),
        in_specs=[pl.BlockSpec((None, 1, N_BUCKETS), lambda h: (h, 0, 0)),
                  pl.BlockSpec((DIL_BLOCK, 2 * DIL_BLOCK), lambda h: (0, 0))],
        out_specs=pl.BlockSpec((None, DIL_BLOCK, 2 * DIL_BLOCK), lambda h: (h, 0, 0)),
        out_shape=jax.ShapeDtypeStruct((h_n, DIL_BLOCK, 2 * DIL_BLOCK), F32), compiler_params=_cparams(),
        name=name)(rb, bucket)


def _row_tile(rows, cols, budget=2 << 20):
    if rows * cols * 4 <= budget or rows % 8:
        return rows
    best = 8
    for t in range(8, rows + 1, 8):
        if rows % t == 0 and t * cols * 4 <= budget:
            best = t
    return best


def _adamw(w, g, m, v, *, name):
    shape = w.shape
    cols = shape[-1]
    rows = math.prod(shape[:-1]) if len(shape) > 1 else 1
    to2 = lambda t: t.reshape(rows, cols)
    tr = _row_tile(rows, cols)
    c1 = 1.0 / (1.0 - ADAM_B1 ** ADAM_STEP)
    c2 = 1.0 / (1.0 - ADAM_B2 ** ADAM_STEP)

    def body(w_ref, g_ref, m_ref, v_ref, d_ref, nm_ref, nv_ref):
        g_v = g_ref[...]
        nm = ADAM_B1 * m_ref[...] + (1.0 - ADAM_B1) * g_v
        nv = ADAM_B2 * v_ref[...] + (1.0 - ADAM_B2) * (g_v * g_v)
        m_hat = nm * c1
        v_hat = nv * c2
        d_ref[...] = -ADAM_LR * (m_hat / (jnp.sqrt(v_hat) + ADAM_EPS) + ADAM_WD * w_ref[...])
        nm_ref[...] = nm
        nv_ref[...] = nv

    blk = pl.BlockSpec((tr, cols), lambda i: (i, 0))
    sd = jax.ShapeDtypeStruct((rows, cols), F32)
    outs = pl.pallas_call(body, grid=(rows // tr,), in_specs=[blk] * 4, out_specs=[blk] * 3, out_shape=[sd] * 3,
                          compiler_params=_cparams(), name=name)(to2(w), to2(g), to2(m), to2(v))
    return tuple(t.reshape(shape) for t in outs)


def _add_half(unit, got, half_idx, *, name):
    rest = unit.shape[2:]
    c = rest[-1]
    r = math.prod(rest[:-1])
    tr = _row_tile(r, c)

    def body(idx_ref, u_ref, g_ref, o_ref, w_ref):
        tot = u_ref[...] + g_ref[...].astype(F32)
        o_ref[...] = tot
        w_ref[...] = tot.astype(BF16)

    blk = pl.BlockSpec((None, tr, c), lambda s, i, idx: (s, i, 0))
    grid_spec = pltpu.PrefetchScalarGridSpec(
        num_scalar_prefetch=1, grid=(N_SHARD, r // tr),
        in_specs=[pl.BlockSpec((None, None, tr, c), lambda s, i, idx: (idx[0], s, i, 0)), blk],
        out_specs=[blk, blk])
    out, wire = pl.pallas_call(
        body, grid_spec=grid_spec,
        out_shape=[jax.ShapeDtypeStruct((N_SHARD, r, c), F32), jax.ShapeDtypeStruct((N_SHARD, r, c), BF16)],
        compiler_params=_cparams(), name=name)(half_idx, unit.reshape(2, N_SHARD, r, c), got.reshape(N_SHARD, r, c))
    return out.reshape((N_SHARD,) + rest), wire.reshape((N_SHARD,) + rest)


def _add_shards(part, got, shard_idx, *, name):
    rest = part.shape[1:]
    c = rest[-1]
    r = math.prod(rest[:-1])
    tr = _row_tile(r, c)

    def body(idx_ref, p_ref, g_ref, o_ref):
        acc = p_ref[...]
        for k in range(3):
            acc = acc + g_ref[k].astype(F32)
        o_ref[...] = acc

    grid_spec = pltpu.PrefetchScalarGridSpec(
        num_scalar_prefetch=1, grid=(r // tr,),
        in_specs=[pl.BlockSpec((None, tr, c), lambda i, idx: (idx[0], i, 0)),
                  pl.BlockSpec((3, tr, c), lambda i, idx: (0, i, 0))],
        out_specs=pl.BlockSpec((tr, c), lambda i, idx: (i, 0)))
    out = pl.pallas_call(body, grid_spec=grid_spec, out_shape=jax.ShapeDtypeStruct((r, c), F32),
                         compiler_params=_cparams(), name=name)(
        shard_idx, part.reshape(N_SHARD, r, c), got.reshape(3, r, c))
    return out.reshape(rest)


def _sum_devices(x, n_dev, *, name):
    rows = x.shape[0] // n_dev

    def body(x_ref, o_ref):
        acc = x_ref[0:rows, :]
        for d in range(1, n_dev):
            acc = acc + x_ref[d * rows:(d + 1) * rows, :]
        o_ref[...] = acc

    return pl.pallas_call(body, out_shape=jax.ShapeDtypeStruct((rows, x.shape[1]), F32), name=name)(x)


def _my_pos():
    return lax.axis_index("x"), lax.axis_index("y"), lax.axis_index("c")


def _all_gather(x_blk, *, name, in_vmem):
    m_per, n = x_blk.shape

    def body(x_ref, out_ref, send_sems, recv_sems, local_sem):
        x, y, c = _my_pos()
        me, sibling = (x, y, c), (x, y, 1 - c)
        chips = [(1 - x, y), (x, 1 - y), (1 - x, 1 - y)]

        def rows(px, py, pc):
            return out_ref.at[pl.ds((4 * px + 2 * py + pc) * m_per, m_per), :]

        def copy(k, block, to, src=None):
            return pltpu.make_async_remote_copy(
                src_ref=rows(*block) if src is None else src, dst_ref=rows(*block),
                send_sem=send_sems.at[k], recv_sem=recv_sems.at[k], device_id=to, device_id_type=MESH)

        mine = pltpu.make_async_copy(x_ref, rows(*me), local_sem)
        mine.start()
        first = [copy(0, me, sibling, src=x_ref)]
        first += [copy(1 + j, me, (*chip, c), src=x_ref) for j, chip in enumerate(chips)]
        for cp in first:
            cp.start()
        passed = [copy(4 + j, (*chip, c), sibling) for j, chip in enumerate(chips)]
        for j, chip in enumerate(chips):
            copy(1 + j, (*chip, c), me).wait_recv()
            passed[j].start()
        copy(0, sibling, me).wait_recv()
        for j, chip in enumerate(chips):
            copy(4 + j, (*chip, 1 - c), me).wait_recv()
        for cp in first + passed:
            cp.wait_send()
        mine.wait()

    space = pltpu.VMEM if in_vmem else pl.ANY
    return pl.pallas_call(
        body, out_shape=jax.ShapeDtypeStruct((8 * m_per, n), x_blk.dtype),
        in_specs=[pl.BlockSpec(memory_space=space)], out_specs=pl.BlockSpec(memory_space=space),
        scratch_shapes=[pltpu.SemaphoreType.DMA((7,)), pltpu.SemaphoreType.DMA((7,)), pltpu.SemaphoreType.DMA],
        name=name)(x_blk)


_HBM = pl.BlockSpec(memory_space=pl.ANY)


def _gather_weights(fams, *, name):
    n = len(fams)

    def body(*refs):
        ins, outs = refs[:n], refs[n:2 * n]
        send_sems, recv_sems = refs[2 * n:]
        x, y, c = _my_pos()
        me, sibling = (x, y, c), (x, y, 1 - c)
        chips = [(1 - x, y), (x, 1 - y), (1 - x, 1 - y)]

        def copy(f, k, block, to, src=None):
            px, py, pc = block
            dst = outs[f].at[2 * px + py, pc]
            return pltpu.make_async_remote_copy(
                src_ref=dst if src is None else src, dst_ref=dst, send_sem=send_sems.at[7 * f + k],
                recv_sem=recv_sems.at[7 * f + k], device_id=to, device_id_type=MESH)

        first, passed = [], []
        for f in range(n):
            src = ins[f].at[c]
            first.append(copy(f, 0, me, sibling, src=src))
            first += [copy(f, 1 + j, me, (*chip, c), src=src) for j, chip in enumerate(chips)]
        for cp in first:
            cp.start()
        for j, chip in enumerate(chips):
            for f in range(n):
                copy(f, 1 + j, (*chip, c), me).wait_recv()
                passed.append(copy(f, 4 + j, (*chip, c), sibling))
                passed[-1].start()
        for f in range(n):
            copy(f, 0, sibling, me).wait_recv()
        for j, chip in enumerate(chips):
            for f in range(n):
                copy(f, 4 + j, (*chip, 1 - c), me).wait_recv()
        for cp in first + passed:
            cp.wait_send()

    outs = pl.pallas_call(
        body, out_shape=[jax.ShapeDtypeStruct((N_SHARD,) + t.shape, t.dtype) for t in fams],
        in_specs=[_HBM] * n, out_specs=[_HBM] * n,
        scratch_shapes=[pltpu.SemaphoreType.DMA((7 * n,)), pltpu.SemaphoreType.DMA((7 * n,))], name=name)(*fams)
    return [_place_own(o, t) for o, t in zip(outs, fams)]


def _swap_halves(units, *, name):
    n = len(units)

    def body(*refs):
        ins, outs = refs[:n], refs[n:2 * n]
        send_sems, recv_sems = refs[2 * n:]
        x, y, c = _my_pos()
        cps = [pltpu.make_async_remote_copy(src_ref=ins[f].at[1 - c], dst_ref=outs[f], send_sem=send_sems.at[f],
                                            recv_sem=recv_sems.at[f], device_id=(x, y, 1 - c), device_id_type=MESH)
               for f in range(n)]
        for cp in cps:
            cp.start()
        for cp in cps:
            cp.wait()

    return pl.pallas_call(
        body, out_shape=[jax.ShapeDtypeStruct(t.shape[1:], t.dtype) for t in units],
        in_specs=[_HBM] * n, out_specs=[_HBM] * n,
        scratch_shapes=[pltpu.SemaphoreType.DMA((n,)), pltpu.SemaphoreType.DMA((n,))], name=name)(*units)


def _send_to_chips(parts, *, name):
    n = len(parts)

    def body(*refs):
        ins, outs = refs[:n], refs[n:2 * n]
        send_sems, recv_sems = refs[2 * n:]
        x, y, c = _my_pos()
        chips = [(1 - x, y), (x, 1 - y), (1 - x, 1 - y)]
        cps = [pltpu.make_async_remote_copy(src_ref=ins[f].at[2 * cx + cy], dst_ref=outs[f].at[k],
                                            send_sem=send_sems.at[3 * f + k], recv_sem=recv_sems.at[3 * f + k],
                                            device_id=(cx, cy, c), device_id_type=MESH)
               for f in range(n) for k, (cx, cy) in enumerate(chips)]
        for cp in cps:
            cp.start()
        for cp in cps:
            cp.wait()

    return pl.pallas_call(
        body, out_shape=[jax.ShapeDtypeStruct((3,) + t.shape[1:], t.dtype) for t in parts],
        in_specs=[_HBM] * n, out_specs=[_HBM] * n,
        scratch_shapes=[pltpu.SemaphoreType.DMA((3 * n,)), pltpu.SemaphoreType.DMA((3 * n,))], name=name)(*parts)


def _pair_gather(halves, *, name):
    n = len(halves)

    def body(*refs):
        ins, outs = refs[:n], refs[n:2 * n]
        send_sems, recv_sems = refs[2 * n:]
        x, y, c = _my_pos()
        cps = [pltpu.make_async_remote_copy(src_ref=ins[f], dst_ref=outs[f].at[c], send_sem=send_sems.at[f],
                                            recv_sem=recv_sems.at[f], device_id=(x, y, 1 - c), device_id_type=MESH)
               for f in range(n)]
        for cp in cps:
            cp.start()
        for f in range(n):
            pltpu.make_async_remote_copy(src_ref=ins[f], dst_ref=outs[f].at[1 - c], send_sem=send_sems.at[f],
                                         recv_sem=recv_sems.at[f], device_id=(x, y, 1 - c),
                                         device_id_type=MESH).wait_recv()
        for cp in cps:
            cp.wait_send()

    outs = pl.pallas_call(
        body, out_shape=[jax.ShapeDtypeStruct((2,) + t.shape, t.dtype) for t in halves],
        in_specs=[_HBM] * n, out_specs=[_HBM] * n,
        scratch_shapes=[pltpu.SemaphoreType.DMA((n,)), pltpu.SemaphoreType.DMA((n,))], name=name)(*halves)
    c = lax.axis_index("c")
    return [lax.dynamic_update_index_in_dim(o, t, c, 0) for o, t in zip(outs, halves)]


_HBM_ONLY = pl.BlockSpec(memory_space=pltpu.HBM)
_SEMS = pl.BlockSpec(memory_space=pltpu.SEMAPHORE)
_EFFECT = pltpu.SideEffectType.DATAFLOW_SIDE_EFFECTING


def _copies_start(srcs, lands, plan, n_copies, *, name):
    n, m = len(srcs), len(lands)

    def body(*refs):
        src_refs, land_refs = refs[:n], refs[n:n + m]
        send_sems, recv_sems, token = refs[n + m], refs[n + m + 1], refs[-1]
        for k, (src, dst, peer) in enumerate(plan(src_refs, land_refs)):
            pltpu.make_async_remote_copy(src_ref=src, dst_ref=dst, send_sem=send_sems.at[k], recv_sem=recv_sems.at[k],
                                         device_id=peer, device_id_type=MESH).start()
        token[...] = jnp.zeros_like(token)

    bufs = [pltpu.with_memory_space_constraint(t, pltpu.HBM) for t in (*srcs, *lands)]
    outs = pl.pallas_call(
        body, name=name,
        out_shape=(pltpu.SemaphoreType.DMA((n_copies,)), pltpu.SemaphoreType.DMA((n_copies,)),
                   *[pltpu.HBM(t.shape, t.dtype) for t in bufs], jax.ShapeDtypeStruct((8, 128), F32)),
        in_specs=[_HBM_ONLY] * (n + m),
        out_specs=(_SEMS, _SEMS, *[_HBM_ONLY] * (n + m), pl.BlockSpec(memory_space=pltpu.VMEM)),
        input_output_aliases={k: 2 + k for k in range(n + m)},
        compiler_params=pltpu.CompilerParams(has_side_effects=_EFFECT))(*bufs)
    return outs[0], outs[1], list(outs[2:2 + n + m]), outs[-1]


def _copies_wait(send_sems, recv_sems, thru, n_src, plan, after, *, name):
    nm = len(thru)

    def body(*refs):
        t_refs, send, recv = refs[:nm], refs[nm], refs[nm + 1]
        for k, (src, dst, peer) in enumerate(plan(t_refs[:n_src], t_refs[n_src:])):
            cp = pltpu.make_async_remote_copy(src_ref=src, dst_ref=dst, send_sem=send.at[k], recv_sem=recv.at[k],
                                              device_id=peer, device_id_type=MESH)
            cp.wait_send()
            cp.wait_recv()

    outs = pl.pallas_call(
        body, name=name, out_shape=tuple(pltpu.HBM(t.shape, t.dtype) for t in thru),
        in_specs=[_HBM_ONLY] * nm + [_SEMS, _SEMS, pl.BlockSpec(memory_space=pl.ANY)],
        out_specs=tuple([_HBM_ONLY] * nm), input_output_aliases={k: k for k in range(nm)},
        compiler_params=pltpu.CompilerParams(has_side_effects=_EFFECT))(*thru, send_sems, recv_sems, after)
    return list(outs)


_RELATIONS = [(dx, dy, dc) for dx in (0, 1) for dy in (0, 1) for dc in (0, 1)][1:]


def _gather_plan(src_refs, land_refs):
    x, y, c = _my_pos()
    flip = lambda v, d: 1 - v if d else v
    return [(s_ref.at[c], l_ref.at[2 * x + y, c], (flip(x, dx), flip(y, dy), flip(c, dc)))
            for s_ref, l_ref in zip(src_refs, land_refs) for dx, dy, dc in _RELATIONS]


def _gather_chips_plan(src_refs, land_refs):
    x, y, c = _my_pos()
    peers = [(x, y, 1 - c), (1 - x, y, c), (x, 1 - y, c), (1 - x, 1 - y, c)]
    return [(s_ref.at[c], l_ref.at[2 * x + y, c], peer) for s_ref, l_ref in zip(src_refs, land_refs) for peer in peers]


def _gather_pass_plan(src_refs, land_refs):
    x, y, c = _my_pos()
    chips = [(1 - x, y), (x, 1 - y), (1 - x, 1 - y)]
    return [(l_ref.at[2 * cx + cy, c], l_ref.at[2 * cx + cy, c], (x, y, 1 - c))
            for l_ref in land_refs for cx, cy in chips]


def _sibling_plan(src_refs, land_refs):
    x, y, c = _my_pos()
    return [(s_ref.at[1 - c], l_ref, (x, y, 1 - c)) for s_ref, l_ref in zip(src_refs, land_refs)]


def _chips_plan(src_refs, land_refs):
    x, y, c = _my_pos()
    chips = [(1 - x, y), (x, 1 - y), (1 - x, 1 - y)]
    return [(s_ref.at[2 * cx + cy], l_ref.at[k], (cx, cy, c))
            for s_ref, l_ref in zip(src_refs, land_refs) for k, (cx, cy) in enumerate(chips)]


def _place_own(gathered, fam):
    x, y, c = _my_pos()
    own = lax.dynamic_index_in_dim(fam, c, 0, keepdims=True)[None]
    return lax.dynamic_update_slice(gathered, own, (2 * x + y, c) + (0,) * (fam.ndim - 1))


def _to_heads(t, width):
    return t.reshape(t.shape[0], HEADS, width).transpose(1, 0, 2)


def _from_heads(t):
    return t.transpose(1, 0, 2).reshape(t.shape[1], -1)


def _t5_bucket(dist):
    max_exact = N_BUCKETS // 2
    d = jnp.maximum(dist, 1).astype(F32)
    large = max_exact + (jnp.log(d / max_exact) / math.log(MAX_DISTANCE / max_exact)
                         * (N_BUCKETS - max_exact)).astype(jnp.int32)
    large = jnp.minimum(large, N_BUCKETS - 1)
    return jnp.where(dist < max_exact, dist, large)


def _bucket_map(dilation):
    iq = jnp.arange(DIL_BLOCK)[:, None]
    ik = jnp.arange(2 * DIL_BLOCK)[None, :]
    rel = DIL_BLOCK + iq - ik
    return _t5_bucket(jnp.maximum(rel, 0) * dilation).astype(jnp.int32)


def _q_perm(w):
    w3 = w.reshape(w.shape[0], HEADS, QK_NOPE + QK_ROPE)
    return jnp.concatenate([w3[:, :, :QK_NOPE].reshape(w.shape[0], -1),
                            w3[:, :, QK_NOPE:QK_NOPE + HALF_ROPE].reshape(w.shape[0], -1),
                            w3[:, :, QK_NOPE + HALF_ROPE:].reshape(w.shape[0], -1)], axis=1)


def _q_unperm(w):
    n0, n1 = HEADS * QK_NOPE, HEADS * HALF_ROPE
    r = w.shape[0]
    return jnp.concatenate([w[:, :n0].reshape(r, HEADS, QK_NOPE), w[:, n0:n0 + n1].reshape(r, HEADS, HALF_ROPE),
                            w[:, n0 + n1:].reshape(r, HEADS, HALF_ROPE)], axis=2).reshape(r, -1)


def _kv_perm(w):
    w3 = w.reshape(w.shape[0], HEADS, QK_NOPE + V_HEAD)
    return jnp.concatenate([w3[:, :, :QK_NOPE].reshape(w.shape[0], -1), w3[:, :, QK_NOPE:].reshape(w.shape[0], -1)],
                           axis=1)


def _kv_unperm(w):
    n0 = HEADS * QK_NOPE
    r = w.shape[0]
    return jnp.concatenate([w[:, :n0].reshape(r, HEADS, QK_NOPE), w[:, n0:].reshape(r, HEADS, V_HEAD)],
                           axis=2).reshape(r, -1)


def _row(v):
    return v.reshape(1, -1)


def kernel(x, c, norm_pre, norm_post, w_mod, b_mod, ffn_w_gate, ffn_w_up, ffn_w_down, mla_w_in, mla_q_norm, mla_w_q_up, mla_kv_norm, mla_w_kv_up, mla_w_o, dil_w_in, dil_w_o, rel_bias, loss_target, m_norm_pre, m_norm_post, m_w_mod, m_b_mod, m_ffn_w_gate, m_ffn_w_up, m_ffn_w_down, m_mla_w_in, m_mla_q_norm, m_mla_w_q_up, m_mla_kv_norm, m_mla_w_kv_up, m_mla_w_o, m_dil_w_in, m_dil_w_o, m_rel_bias, v_norm_pre, v_norm_post, v_w_mod, v_b_mod, v_ffn_w_gate, v_ffn_w_up, v_ffn_w_down, v_mla_w_in, v_mla_q_norm, v_mla_w_q_up, v_mla_kv_norm, v_mla_w_kv_up, v_mla_w_o, v_dil_w_in, v_dil_w_o, v_rel_bias):
    given = dict(locals())
    ix, iy, ic = _my_pos()
    shard_id = 2 * ix + iy
    dev_id = 4 * ix + 2 * iy + ic
    x2 = x[0]
    target = loss_target[0]
    half_idx = jnp.reshape(ic, (1,)).astype(jnp.int32)
    shard_idx = jnp.reshape(shard_id, (1,)).astype(jnp.int32)

    blk = jnp.zeros((8, D_MODEL), F32)
    blk = blk.at[0].set(c[0])
    blk = blk.at[1:3].set(jnp.pad(norm_pre.reshape(-1), (0, 512)).reshape(2, D_MODEL))
    blk = blk.at[3:5].set(jnp.pad(norm_post.reshape(-1), (0, 512)).reshape(2, D_MODEL))
    got = _all_gather(blk, name="ag_c_norms", in_vmem=True).reshape(N_SHARD, 2, 8, D_MODEL)
    c_all = got[:, :, 0, :].reshape(8, D_MODEL)

    def full_norm(lo):
        t = got[:, 0, lo:lo + 2, :].reshape(N_SHARD, 2 * D_MODEL)[:, :1536].reshape(N_SHARD, 2, 3, 256)
        return t.transpose(1, 2, 0, 3).reshape(2, 3, D_MODEL)

    pre_full, post_full = full_norm(1), full_norm(3)

    silu_c = _silu_bf16(c_all, name="silu_c")
    b_cols = lax.dynamic_slice_in_dim(b_mod, shard_id * 2304, 2304, axis=1).reshape(2, 1, 2304)
    mod_part = _mm(silu_c, w_mod, bias=b_cols, name="mod_mm", tn_cap=768)
    mod_all = _all_gather(mod_part.reshape(16, 2304), name="ag_mod", in_vmem=True)
    mod_all = mod_all.reshape(N_SHARD, 2, 2, 8, 2304)[:, 0]
    mod_mine = lax.dynamic_index_in_dim(mod_all, dev_id, axis=2, keepdims=False)
    mod = mod_mine.transpose(1, 0, 2).reshape(2, 9, D_MODEL)

    bf = lambda t: t.astype(BF16)
    ffn_fam = lambda i, h: [bf(jnp.stack([ffn_w_gate[i, h], ffn_w_up[i, h]])),
                            bf(ffn_w_down[i, h].reshape(2, F_SHARD // 2, D_MODEL))]
    mla_fam = [bf(mla_w_in.reshape(2, 128, -1)), bf(mla_w_q_up.reshape(2, 192, -1)),
               bf(mla_w_kv_up.reshape(2, 128, -1)), bf(mla_w_o.reshape(2, 128, D_MODEL))]
    dil_fam = [bf(dil_w_in.reshape(2, 512, -1)), bf(dil_w_o.reshape(2, 128, D_MODEL))]
    later_fams = [ffn_fam(0, 1), ffn_fam(1, 0) + dil_fam, ffn_fam(1, 1)]
    full, later_fams, mod = lax.optimization_barrier(
        (_gather_weights(ffn_fam(0, 0) + mla_fam, name="ag_weights_first"), later_fams, mod))

    def gather_later(fams, tag):
        lands = [lax.empty((N_SHARD,) + t.shape, t.dtype) for t in fams]
        send, recv, thru, token = _copies_start(fams, lands, _gather_plan, 7 * len(fams), name=f"ag_start_{tag}")
        return dict(send=send, recv=recv, thru=thru, token=token, n=len(fams), tag=tag)

    def arrive(st, after):
        thru = _copies_wait(st['send'], st['recv'], st['thru'], st['n'], _gather_plan, after,
                            name=f"ag_wait_{st['tag']}")
        return [_place_own(o, t) for t, o in zip(thru[:st['n']], thru[st['n']:])]

    def gather_chips(fams, tag):
        lands = [lax.empty((N_SHARD,) + t.shape, t.dtype) for t in fams]
        send, recv, thru, token = _copies_start(fams, lands, _gather_chips_plan, 4 * len(fams), name=f"ag_start_{tag}")
        return dict(send=send, recv=recv, thru=thru, token=token, n=len(fams), tag=tag)

    def pass_on(st, after):
        n, tag = st['n'], st['tag']
        thru = _copies_wait(st['send'], st['recv'], st['thru'], n, _gather_chips_plan, after, name=f"ag_mid_{tag}")
        send, recv, lands, token = _copies_start([], thru[n:], _gather_pass_plan, 3 * n, name=f"ag_pass_{tag}")
        return dict(send=send, recv=recv, thru=lands, fams=thru[:n], tag=tag), token[0, 0]

    def arrive_passed(st, after):
        lands = _copies_wait(st['send'], st['recv'], st['thru'], 0, _gather_pass_plan, after, name=f"ag_wait_{st['tag']}")
        return [_place_own(o, t) for t, o in zip(st['fams'], lands)]

    flight_a = gather_later(later_fams[0], "l0s2")
    _, next_fams = lax.optimization_barrier((flight_a['token'], later_fams[1]))
    flight_b = gather_chips(next_fams, "l1s01")
    as_ffn = lambda w_gu, w_dn: (w_gu, w_dn.reshape(N_SHARD, F_SHARD, D_MODEL))
    ffn_w = {(0, 0): as_ffn(full[0], full[1])}
    w_in = full[2].reshape(D_MODEL, -1)
    wq_p = _q_perm(full[3].reshape(N_SHARD, Q_LORA, -1).transpose(1, 0, 2).reshape(Q_LORA, -1))
    wkv_p = _kv_perm(full[4].reshape(N_SHARD, KV_LORA, -1).transpose(1, 0, 2).reshape(KV_LORA, -1))
    w_mo = full[5].reshape(D_MODEL, D_MODEL)
    dil_w = {}

    pos = jnp.arange(SEQ, dtype=F32)
    freqs = ROPE_THETA ** (-jnp.arange(HALF_ROPE, dtype=F32) / HALF_ROPE)
    ang = pos[:, None] * freqs[None, :]
    cos_k, sin_k = jnp.cos(ang), jnp.sin(ang)
    cos_q, sin_q = jnp.tile(cos_k, (1, HEADS)), jnp.tile(sin_k, (1, HEADS))

    buckets = [_bucket_map(d) for _, d in DIL_GROUPS]
    biases = [_bias_table(rel_bias[:, g * HEADS:(g + 1) * HEADS].T.reshape(HEADS, 1, N_BUCKETS), bk,
                          name=f"dil_bias_table_g{g}") for g, bk in enumerate(buckets)]

    def sub_params(i, sub):
        return dict(pg=_row(pre_full[i, sub]), qg=_row(post_full[i, sub]), sh=_row(mod[i, 3 * sub]),
                    sc=_row(mod[i, 3 * sub + 1]), gate=_row(mod[i, 3 * sub + 2]))

    def ffn_fwd(xin, i, h, sub, tie=None, mid=None):
        p = sub_params(i, sub)
        if tie is not None:
            p['sh'] = p['sh'] + tie
        tag = f"l{i}s{sub}"
        w_gu, w_dn = ffn_w[i, h]
        hn = _pre_fwd(xin, p['pg'], p['sc'], p['sh'], name=f"pre_fwd_{tag}")
        gu, a = _ffn_up(hn, w_gu, name=f"ffn_up_{tag}")
        if mid is not None:
            p['qg'] = p['qg'] + mid(a)
        f, out = _ffn_down(a, w_dn, xin, p['qg'], p['gate'], FFN_RES, name=f"ffn_down_{tag}")
        return out, dict(x=xin, hn=hn, gu=gu, a=a, f=f, p=p, i=i, h=h, tag=tag)

    def mla_fwd(xin, i, sub):
        p = sub_params(i, sub)
        tag = f"l{i}s{sub}"
        hn = _pre_fwd(xin, p['pg'], p['sc'], p['sh'], name=f"pre_fwd_{tag}")
        lat = _mm(hn, w_in, name="mla_lat")
        cq, ckv = lat[:, :Q_LORA], lat[:, Q_LORA:Q_LORA + KV_LORA]
        k1, k2 = lat[:, Q_LORA + KV_LORA:Q_LORA + KV_LORA + HALF_ROPE], lat[:, Q_LORA + KV_LORA + HALF_ROPE:]
        cqn = _rms_fwd(cq, mla_q_norm, name="mla_qnorm")
        ckvn = _rms_fwd(ckv, mla_kv_norm, name="mla_kvnorm")
        qp = _mm(cqn, wq_p, name="mla_q_up")
        kvp = _mm(ckvn, wkv_p, name="mla_kv_up")
        n0, n1 = HEADS * QK_NOPE, HEADS * HALF_ROPE
        qr1, qr2 = _rope(qp[:, n0:n0 + n1], qp[:, n0 + n1:], cos_q, sin_q, name="rope_q")
        kr1, kr2 = _rope(k1, k2, cos_k, sin_k, name="rope_k")
        q = jnp.concatenate([qp[:, :n0].reshape(SEQ, HEADS, QK_NOPE), qr1.reshape(SEQ, HEADS, HALF_ROPE),
                             qr2.reshape(SEQ, HEADS, HALF_ROPE)], axis=2).transpose(1, 0, 2).astype(BF16)
        kr = jnp.broadcast_to(jnp.concatenate([kr1, kr2], axis=1)[:, None, :], (SEQ, HEADS, QK_ROPE))
        k = jnp.concatenate([kvp[:, :n0].reshape(SEQ, HEADS, QK_NOPE), kr], axis=2).transpose(1, 0, 2).astype(BF16)
        v = _to_heads(kvp[:, n0:], V_HEAD).astype(BF16)
        o, lse = _mla_attn_fwd(q, k, v, name="mla_attn_fwd")
        o_flat = _from_heads(o).astype(BF16)
        f = _mm(o_flat, w_mo, name="mla_out")
        out = _post_fwd(f, xin, p['qg'], p['gate'], 1.0, name=f"post_fwd_{tag}")
        return out, dict(x=xin, hn=hn, cq=cq, ckv=ckv, cqn=cqn, ckvn=ckvn, q=q, k=k, v=v, o=o, lse=lse,
                         o_flat=o_flat, f=f, p=p, tag=tag)

    def dil_fwd(xin, i, sub):
        p = sub_params(i, sub)
        tag = f"l{i}s{sub}"
        hn = _pre_fwd(xin, p['pg'], p['sc'], p['sh'], name=f"pre_fwd_{tag}")
        heads = _proj_heads(hn, dil_w['in'], name="dil_proj")
        outs, lses = [], []
        for g, (window, d) in enumerate(DIL_GROUPS):
            o, lse = _dil_attn_fwd(heads, biases[g], g, d, name=f"dil_attn_fwd_g{g}")
            outs.append(o)
            lses.append(lse)
        o_flat = _dil_mix_fwd(outs, lses, name="dil_mix_fwd")
        f = _mm(o_flat, dil_w['out'], name="dil_out")
        out = _post_fwd(f, xin, p['qg'], p['gate'], 1.0, name=f"post_fwd_{tag}")
        return out, dict(x=xin, hn=hn, heads=heads, outs=outs, lses=lses, o_flat=o_flat, f=f, p=p, tag=tag)

    saved = [None] * 6
    xs, saved[0] = ffn_fwd(x2, 0, 0, 0, tie=flight_a['token'][0, 0] + flight_b['token'][0, 0])
    xs, saved[1] = mla_fwd(xs, 0, 1)
    ffn_w[0, 1] = as_ffn(*arrive(flight_a, xs))
    passed = {}

    def second_step(after):
        passed['st'], tok = pass_on(flight_b, after)
        return tok

    xs, saved[2] = ffn_fwd(xs, 0, 1, 2, mid=second_step)
    got, last_fams = lax.optimization_barrier((arrive_passed(passed['st'], xs), later_fams[2]))
    ffn_w[1, 0] = as_ffn(got[0], got[1])
    dil_w['in'], dil_w['out'] = got[2].reshape(N_SHARD, D_MODEL, -1), got[3].reshape(D_MODEL, D_MODEL)
    in_flight = gather_later(last_fams, "l1s2")
    xs, saved[3] = ffn_fwd(xs, 1, 0, 0, tie=in_flight['token'][0, 0])
    xs, saved[4] = dil_fwd(xs, 1, 1)
    ffn_w[1, 1] = as_ffn(*arrive(in_flight, xs))
    xs, saved[5] = ffn_fwd(xs, 1, 1, 2)

    dx, loss_part = _loss(xs, target, name="loss")

    dmod = [[None] * 9 for _ in range(2)]
    dpre = [[None] * 3 for _ in range(2)]
    dpost = [[None] * 3 for _ in range(2)]
    ffn_units = {}
    row_unit = lambda g, r, j: ((r % 2, r // 2), 0, j)

    def close_sub(dhn, dout, sv, i, sub, res_dgate, res_dqg):
        p = sv['p']
        dxs, dsh, dsc, dpg = _pre_bwd(dhn, sv['x'], dout, p['pg'], p['sc'], name=f"pre_bwd_{sv['tag']}")
        dmod[i][3 * sub], dmod[i][3 * sub + 1], dmod[i][3 * sub + 2] = dsh, dsc, res_dgate
        dpre[i][sub], dpost[i][sub] = dpg, res_dqg
        return dxs

    def ffn_bwd(dout, sv, sub, tie=0.0, mid=None):
        i, h, p, tag = sv['i'], sv['h'], sv['p'], sv['tag']
        w_gu, w_dn = ffn_w[i, h]
        df, dgate, dqg = _post_bwd(dout, sv['f'], p['qg'] + tie, p['gate'], FFN_RES, name=f"post_bwd_{tag}")
        u_dn = _mm(sv['a'], df, ta=True, tn_cap=D_MODEL // 2, out_shape=(2, N_SHARD, F_SHARD, D_MODEL // 2),
                   out_sel=lambda g, r, j: ((j, g), r, 0), name=f"ffn_dwd_{tag}")
        dgu = _ffn_dgu(df, w_dn, sv['gu'], name=f"ffn_dgu_{tag}")
        if mid is not None:
            p = dict(p, pg=p['pg'] + mid(dgu))
        u_gu = _mm(dgu.reshape(2 * N_SHARD, SEQ, F_SHARD), sv['hn'], ta=True,
                   out_shape=(2, N_SHARD, F_SHARD, D_MODEL), out_sel=lambda g, r, j: ((g % 2, g // 2), r, j),
                   name=f"ffn_dwgu_{tag}")
        ffn_units[i, h] = [u_gu, u_dn]
        dxs, dsh, dsc, dpg = _ffn_dhn(dgu, w_gu, sv['x'], dout, p['pg'], p['sc'], name=f"ffn_dhn_{tag}")
        dmod[i][3 * sub], dmod[i][3 * sub + 1], dmod[i][3 * sub + 2] = dsh, dsc, dgate
        dpre[i][sub], dpost[i][sub] = dpg, dqg
        return dxs

    def mla_bwd(dout, sv, i, sub, tie=0.0):
        p, tag = sv['p'], sv['tag']
        df, dgate, dqg = _post_bwd(dout, sv['f'], p['qg'] + tie, p['gate'], 1.0, name=f"post_bwd_{tag}")
        u_wo = _mm(sv['o_flat'], df, ta=True, tm_cap=128, out_shape=(2, N_SHARD, 128, D_MODEL), out_sel=row_unit,
                   name="mla_dwo")
        do_flat = _mm(df, w_mo, tb=True, name="mla_do")
        do = _to_heads(do_flat, V_HEAD)
        dq, dk, dv = _mla_attn_bwd(sv['q'], sv['k'], sv['v'], sv['o'], do, sv['lse'], name="mla_attn_bwd")
        dq_t = dq.transpose(1, 0, 2)
        dqr1, dqr2 = _rope(dq_t[:, :, QK_NOPE:QK_NOPE + HALF_ROPE].reshape(SEQ, -1),
                           dq_t[:, :, QK_NOPE + HALF_ROPE:].reshape(SEQ, -1), cos_q, -sin_q, name="rope_q_bwd")
        dqp = jnp.concatenate([dq_t[:, :, :QK_NOPE].reshape(SEQ, -1), dqr1, dqr2], axis=1).astype(BF16)
        dkr = _head_sum(dk[:, :, QK_NOPE:], name="mla_dkr_sum")
        dk1, dk2 = _rope(dkr[:, :HALF_ROPE], dkr[:, HALF_ROPE:], cos_k, -sin_k, name="rope_k_bwd")
        dkvp = jnp.concatenate([_from_heads(dk[:, :, :QK_NOPE]), _from_heads(dv)], axis=1).astype(BF16)
        g_wq = _q_unperm(_mm(sv['cqn'], dqp, ta=True, name="mla_dwq"))
        g_wkv = _kv_unperm(_mm(sv['ckvn'], dkvp, ta=True, name="mla_dwkv"))
        dcqn = _mm(dqp, wq_p, tb=True, name="mla_dcqn")
        dckvn = _mm(dkvp, wkv_p, tb=True, name="mla_dckvn")
        dcq, g_qn = _rms_bwd(dcqn, sv['cq'], mla_q_norm, name="mla_qnorm_bwd")
        dckv, g_kvn = _rms_bwd(dckvn, sv['ckv'], mla_kv_norm, name="mla_kvnorm_bwd")
        dlat = jnp.concatenate([dcq, dckv, dk1, dk2], axis=1).astype(BF16)
        u_win = _mm(sv['hn'], dlat, ta=True, tm_cap=128, out_shape=(2, N_SHARD, 128, dlat.shape[1]),
                    out_sel=row_unit, name="mla_dwin")
        dhn = _mm(dlat, w_in, tb=True, name="mla_dhn")
        col_unit = lambda t: (t.reshape(t.shape[0], N_SHARD, -1).transpose(1, 0, 2)
                              .reshape(N_SHARD, 2, t.shape[0] // 2, -1).transpose(1, 0, 2, 3))
        grads = dict(units=[u_win, col_unit(g_wq), col_unit(g_wkv), u_wo], q_norm=g_qn, kv_norm=g_kvn)
        return close_sub(dhn, dout, sv, i, sub, dgate, dqg), grads

    def dil_bwd(dout, sv, i, sub):
        p, tag = sv['p'], sv['tag']
        df, dgate, dqg = _post_bwd(dout, sv['f'], p['qg'], p['gate'], 1.0, name=f"post_bwd_{tag}")
        u_wo = _mm(sv['o_flat'], df, ta=True, tm_cap=128, out_shape=(2, N_SHARD, 128, D_MODEL), out_sel=row_unit,
                   name="dil_dwo")
        dos, dlts = _dil_mix_bwd(_mm(df, dil_w['out'], tb=True, name="dil_do"), sv['outs'], sv['lses'],
                                 name="dil_mix_bwd")
        pieces = []
        bias_rows = []
        for g, (window, d) in enumerate(DIL_GROUPS):
            dq, dk, dv, dbias = _dil_attn_bwd(sv['heads'], biases[g], sv['lses'][g], dos[g], dlts[g], g, d,
                                              name=f"dil_attn_bwd_g{g}")
            pieces += [dq, dk, dv]
            bias_rows.append(_bias_grad(dbias, buckets[g], name=f"dil_bias_grad_g{g}")[:, 0, :])
        dheads = jnp.concatenate(pieces).astype(BF16)
        u_win = _proj_heads_dw(sv['hn'], dheads, name="dil_dwin")
        dhn = _proj_heads_dx(dheads, dil_w['in'], name="dil_dhn")
        g_bias = jnp.concatenate(bias_rows, axis=0).T
        grads = dict(units=[u_win, u_wo], rel_bias=g_bias)
        return close_sub(dhn, dout, sv, i, sub, dgate, dqg), grads

    def to_sibling(units, tag):
        n = len(units)
        send, recv, thru, token = _copies_start(units, [lax.empty(u.shape[1:], F32) for u in units], _sibling_plan, n,
                                                name=f"rs{tag}_sibling_start")
        return dict(send=send, recv=recv, thru=thru, n=n, tag=tag), token[0, 0]

    def from_sibling(st, after):
        n, tag = st['n'], st['tag']
        thru = _copies_wait(st['send'], st['recv'], st['thru'], n, _sibling_plan, after, name=f"rs{tag}_sibling_wait")
        return [_add_half(u, g, half_idx, name=f"rs{tag}_add_half_{k}") for k, (u, g) in enumerate(zip(thru[:n], thru[n:]))]

    def to_chips(parts, tag):
        n = len(parts)
        send, recv, thru, token = _copies_start([w for _, w in parts],
                                                [lax.empty((3,) + w.shape[1:], BF16) for _, w in parts], _chips_plan,
                                                3 * n, name=f"rs{tag}_chips_start")
        return dict(send=send, recv=recv, thru=thru, n=n, tag=tag, parts=parts), token[0, 0]

    def from_chips(st, after):
        n, tag = st['n'], st['tag']
        thru = _copies_wait(st['send'], st['recv'], st['thru'], n, _chips_plan, after, name=f"rs{tag}_chips_wait")
        return [_add_shards(p, g, shard_idx, name=f"rs{tag}_add_shards_{k}")
                for k, ((p, _), g) in enumerate(zip(st['parts'], thru[n:]))]

    dx = ffn_bwd(dx, saved[5], 2)
    dx, dil_g = dil_bwd(dx, saved[4], 1, 1)
    dx = ffn_bwd(dx, saved[3], 0)
    st1, tok = to_sibling([*ffn_units[1, 1], *dil_g['units'], *ffn_units[1, 0]], "1")
    dx = ffn_bwd(dx, saved[2], 2, tie=tok)
    st1, tok1 = to_chips(from_sibling(st1, dx), "1")
    st2, tok2 = to_sibling(ffn_units[0, 1], "2")
    dx, mla_g = mla_bwd(dx, saved[1], 0, 1, tie=tok1 + tok2)
    reds1 = from_chips(st1, dx)
    st2, tok = to_chips(from_sibling(st2, dx), "2")
    st3, tok3 = to_sibling(mla_g['units'], "3")
    onward = {}

    def mixer_to_chips(after):
        onward['st'], t = to_chips(from_sibling(st3, after), "3")
        return t

    dx = ffn_bwd(dx, saved[0], 0, tie=tok + tok3, mid=mixer_to_chips)
    reds2 = from_chips(st2, dx)
    reds3 = from_chips(onward['st'], dx)
    grad_x = dx[None]

    pad_row = lambda v: jnp.pad(v.reshape(-1), (0, (-v.size) % D_MODEL)).reshape(-1, D_MODEL)
    small = jnp.concatenate(
        [jnp.concatenate([dmod[i][r] for i in range(2) for r in range(9)], axis=0),
         jnp.concatenate([dpre[i][s] for i in range(2) for s in range(3)], axis=0),
         jnp.concatenate([dpost[i][s] for i in range(2) for s in range(3)], axis=0),
         pad_row(mla_g['q_norm']), pad_row(mla_g['kv_norm']), pad_row(dil_g['rel_bias']), pad_row(loss_part)], axis=0)
    small = jnp.pad(small, ((0, SMALL_ROWS - small.shape[0]), (0, 0)))
    small_all = _all_gather(small, name="ag_small_grads", in_vmem=True)
    small_sum = _sum_devices(small_all, 8, name="sum_small_grads")
    g_b_mod = small_sum[0:18].reshape(2, 9 * D_MODEL)
    my_cols = lambda t: lax.dynamic_slice_in_dim(t, shard_id * 256, 256, axis=2)
    g_norm_pre = my_cols(small_sum[18:24].reshape(2, 3, D_MODEL))
    g_norm_post = my_cols(small_sum[24:30].reshape(2, 3, D_MODEL))
    g_q_norm = small_sum[30, :Q_LORA].reshape(1, Q_LORA)
    g_kv_norm = small_sum[31, :KV_LORA].reshape(1, KV_LORA)
    g_rel_bias = small_sum[32:34].reshape(-1)[:N_BUCKETS * 48].reshape(N_BUCKETS, 48)
    loss = small_sum[34, 0]
    dmod_all = small_all.reshape(8, SMALL_ROWS, D_MODEL)[:, 0:18].reshape(8, 2, 9 * D_MODEL)
    dmod_cols = lax.dynamic_slice_in_dim(dmod_all, shard_id * 2304, 2304, axis=2).transpose(1, 0, 2)
    g_w_mod = _mm(silu_c, dmod_cols.astype(BF16), ta=True, tn_cap=768, name="w_mod_grad")

    units0 = ffn_units[0, 0]
    got_a = _swap_halves(units0, name="rs0_sibling")
    parts0 = [_add_half(u, g, half_idx, name=f"rs0_add_half_{k}") for k, (u, g) in enumerate(zip(units0, got_a))]
    got_b = _send_to_chips([w for _, w in parts0], name="rs0_chips")
    reds0 = [_add_shards(p, g, shard_idx, name=f"rs0_add_shards_{k}")
             for k, ((p, _), g) in enumerate(zip(parts0, got_b))]
    fin = _pair_gather(reds1 + reds2 + reds3 + reds0, name="rs_pair_gather")
    ffn_fin = {(1, 1): fin[0:2], (1, 0): fin[4:6], (0, 1): fin[6:8], (0, 0): fin[12:14]}
    swap = lambda t: jnp.swapaxes(t, 2, 3)
    per_ffn = lambda pick: jnp.stack([jnp.stack([pick(*ffn_fin[i, h]) for h in range(2)]) for i in range(2)])
    reduced = dict(ffn_w_gate=swap(per_ffn(lambda gu, dn: gu[0])), ffn_w_up=swap(per_ffn(lambda gu, dn: gu[1])),
                   ffn_w_down=per_ffn(lambda gu, dn: jnp.concatenate([dn[0], dn[1]], axis=1)))
    for n, t in zip(['dil_w_in', 'dil_w_o', 'mla_w_in', 'mla_w_q_up', 'mla_w_kv_up', 'mla_w_o'], fin[2:4] + fin[8:12]):
        reduced[n] = t.reshape(given[n].shape)

    grads = dict(norm_pre=g_norm_pre, norm_post=g_norm_post, w_mod=g_w_mod, b_mod=g_b_mod, mla_q_norm=g_q_norm,
                 mla_kv_norm=g_kv_norm, rel_bias=g_rel_bias, **reduced)

    deltas, new_m, new_v = {}, {}, {}
    for n in WEIGHTS:
        view = swap if n in ('ffn_w_gate', 'ffn_w_up') else (lambda t: t)
        outs = _adamw(view(given[n]), view(grads[n]), view(given["m_" + n]), view(given["v_" + n]), name=f"adamw_{n}")
        deltas[n], new_m[n], new_v[n] = (view(t) for t in outs)
    return (loss, grad_x, *[grads[n] for n in WEIGHTS], *[deltas[n] for n in WEIGHTS],
            *[new_m[n] for n in WEIGHTS], *[new_v[n] for n in WEIGHTS])
```

```python
import math

import jax
import jax.numpy as jnp
from jax import lax
from jax.experimental import pallas as pl
from jax.experimental.pallas import tpu as pltpu

F32 = jnp.float32
BF16 = jnp.bfloat16
MESH = pl.DeviceIdType.MESH

SEQ = 2048
D_MODEL = 1024
D_FF = 2816
N_SHARD = 4
F_SHARD = D_FF // N_SHARD
EPS = 1e-6
FFN_RES = 0.5
HEADS = 16
Q_LORA, KV_LORA, QK_NOPE, QK_ROPE, V_HEAD = 384, 256, 64, 32, 64
HALF_ROPE = QK_ROPE // 2
ROPE_THETA = 10000.0
DIL_GROUPS = ((128, 1), (512, 4), (2048, 16))
DIL_BLOCK = 128
N_BUCKETS = 32
MAX_DISTANCE = 2048
ADAM_LR, ADAM_B1, ADAM_B2, ADAM_EPS, ADAM_WD, ADAM_STEP = 0.001, 0.9, 0.999, 1e-08, 0.01, 10

VMEM_LIMIT = 48 * 1024 * 1024
SMALL_ROWS = 40

WEIGHTS = ['norm_pre', 'norm_post', 'w_mod', 'b_mod', 'ffn_w_gate', 'ffn_w_up', 'ffn_w_down', 'mla_w_in',
           'mla_q_norm', 'mla_w_q_up', 'mla_kv_norm', 'mla_w_kv_up', 'mla_w_o', 'dil_w_in', 'dil_w_o', 'rel_bias']


def _cparams(**kw):
    return pltpu.CompilerParams(vmem_limit_bytes=VMEM_LIMIT, **kw)


def _pick(n, cap, mult=128):
    if n <= cap:
        return n
    best = n
    for t in range(mult, cap + 1, mult):
        if n % t == 0:
            best = t
    return best


def _mm(a, b, *, name, ta=False, tb=False, reduce_g=False, bias=None, out_dtype=F32, tm_cap=512, tn_cap=1024,
        g_n=None, b_sel=None, out_shape=None, out_sel=None, out_buf=None):
    a3 = a if a.ndim == 3 else a[None]
    ga = a3.shape[0]
    if b_sel is None:
        b_n = b if b.ndim == 3 else b[None]
        gb = b_n.shape[0]
        b_sel = (lambda g: (g,)) if gb > 1 else (lambda g: (0,))
        g_n = max(ga, gb)
    else:
        b_n = b
    k_dim, m_dim = (a3.shape[1], a3.shape[2]) if ta else (a3.shape[2], a3.shape[1])
    k2, n_dim = (b_n.shape[-1], b_n.shape[-2]) if tb else (b_n.shape[-2], b_n.shape[-1])
    assert k_dim == k2, (a.shape, b.shape)
    tm = _pick(m_dim, tm_cap, 128 if ta else 8)
    tn = _pick(n_dim, tn_cap, 128)
    mt, nt = m_dim // tm, n_dim // tn
    dims = (((0 if ta else 1,), (1 if tb else 0,)), ((), ()))

    if reduce_g:
        grid = (mt, nt, g_n)
        ids = lambda i, j, g: (g, i, j)
    else:
        grid = (g_n, mt, nt)
        ids = lambda g, i, j: (g, i, j)

    def a_map(*p):
        g, i, j = ids(*p)
        g = g if ga > 1 else 0
        return (g, 0, i) if ta else (g, i, 0)

    def b_map(*p):
        g, i, j = ids(*p)
        return (*b_sel(g), j, 0) if tb else (*b_sel(g), 0, j)

    b_lead = (None,) * (b_n.ndim - 2)
    a_spec = pl.BlockSpec((None, k_dim, tm) if ta else (None, tm, k_dim), a_map)
    b_spec = pl.BlockSpec(b_lead + ((tn, k_dim) if tb else (k_dim, tn)), b_map)
    in_specs = [a_spec, b_spec]
    operands = [a3, b_n]
    if bias is not None:
        assert not reduce_g and bias.shape == (g_n, 1, n_dim)
        in_specs.append(pl.BlockSpec((None, 1, tn), lambda g, i, j: (g, 0, j)))
        operands.append(bias)
    aliases = {}
    if out_buf is not None:
        assert tuple(out_buf.shape) == tuple(out_shape) and out_buf.dtype == out_dtype
        in_specs.append(pl.BlockSpec(memory_space=pl.ANY))
        operands.append(out_buf)
        aliases = {len(operands) - 1: 0}

    if reduce_g:
        out_spec = pl.BlockSpec((tm, tn), lambda i, j, g: (i, j))
        out_sds = jax.ShapeDtypeStruct((m_dim, n_dim), F32)
    elif out_shape is not None:
        def o_map(g, i, j):
            lead, rb, cb = out_sel(g, i, j)
            return (*lead, rb, cb)

        out_spec = pl.BlockSpec((None,) * (len(out_shape) - 2) + (tm, tn), o_map)
        out_sds = jax.ShapeDtypeStruct(tuple(out_shape), out_dtype)
    else:
        out_spec = pl.BlockSpec((None, tm, tn), lambda g, i, j: (g, i, j))
        out_sds = jax.ShapeDtypeStruct((g_n, m_dim, n_dim), out_dtype)

    def body(a_ref, b_ref, *rest):
        o_ref = rest[-1]
        r = lax.dot_general(a_ref[...].astype(BF16), b_ref[...].astype(BF16), dims, preferred_element_type=F32)
        if bias is not None:
            r = r + rest[0][...]
        if reduce_g:
            g = pl.program_id(2)

            @pl.when(g == 0)
            def _():
                o_ref[...] = r

            @pl.when(g > 0)
            def _():
                o_ref[...] += r
        else:
            o_ref[...] = r.astype(o_ref.dtype)

    out = pl.pallas_call(body, grid=grid, in_specs=in_specs, out_specs=out_spec, out_shape=out_sds,
                         input_output_aliases=aliases, compiler_params=_cparams(), name=name)(*operands)
    if not reduce_g and out_shape is None and a.ndim == 2 and b.ndim == 2:
        out = out[0]
    return out


def _rows(tm, w):
    return pl.BlockSpec((tm, w), lambda i: (i, 0))


def _vec(w):
    return pl.BlockSpec((1, w), lambda i: (0, 0))


def _rstd(v):
    return lax.rsqrt(jnp.mean(v * v, axis=-1, keepdims=True) + EPS)


def _pre_fwd(x, pg, sc, sh, *, name):
    s_n, w = x.shape
    tm = _pick(s_n, 512, 8)

    def body(x_ref, pg_ref, sc_ref, sh_ref, o_ref):
        xv = x_ref[...]
        n = (xv * _rstd(xv)) * pg_ref[...]
        o_ref[...] = (n * (1.0 + sc_ref[...]) + sh_ref[...]).astype(o_ref.dtype)

    return pl.pallas_call(body, grid=(s_n // tm,), in_specs=[_rows(tm, w), _vec(w), _vec(w), _vec(w)],
                          out_specs=_rows(tm, w), out_shape=jax.ShapeDtypeStruct((s_n, w), BF16),
                          compiler_params=_cparams(), name=name)(x, pg, sc, sh)


def _post_fwd(f, x, qg, gate, res_w, *, name):
    s_n, w = x.shape
    tm = _pick(s_n, 512, 8)

    def body(f_ref, x_ref, qg_ref, gate_ref, o_ref):
        fv = f_ref[...]
        y = (fv * _rstd(fv)) * qg_ref[...]
        o_ref[...] = x_ref[...] + (res_w * gate_ref[...]) * y

    return pl.pallas_call(body, grid=(s_n // tm,), in_specs=[_rows(tm, w), _rows(tm, w), _vec(w), _vec(w)],
                          out_specs=_rows(tm, w), out_shape=jax.ShapeDtypeStruct((s_n, w), F32),
                          compiler_params=_cparams(), name=name)(f, x, qg, gate)


def _post_bwd(dout, f, qg, gate, res_w, *, name):
    s_n, w = f.shape
    tm = _pick(s_n, 512, 8)

    def body(do_ref, f_ref, qg_ref, gate_ref, df_ref, dgate_ref, dqg_ref):
        @pl.when(pl.program_id(0) == 0)
        def _():
            dgate_ref[...] = jnp.zeros_like(dgate_ref)
            dqg_ref[...] = jnp.zeros_like(dqg_ref)

        do = do_ref[...]
        fv = f_ref[...]
        r = _rstd(fv)
        fh = fv * r
        qg_v = qg_ref[...]
        dgate_ref[...] += res_w * jnp.sum(do * (fh * qg_v), axis=0, keepdims=True)
        dy = do * (res_w * gate_ref[...])
        dqg_ref[...] += jnp.sum(dy * fh, axis=0, keepdims=True)
        dfh = dy * qg_v
        df = r * (dfh - fh * jnp.mean(dfh * fh, axis=-1, keepdims=True))
        df_ref[...] = df.astype(df_ref.dtype)

    return pl.pallas_call(
        body, grid=(s_n // tm,), in_specs=[_rows(tm, w), _rows(tm, w), _vec(w), _vec(w)],
        out_specs=[_rows(tm, w), _vec(w), _vec(w)],
        out_shape=[jax.ShapeDtypeStruct((s_n, w), BF16), jax.ShapeDtypeStruct((1, w), F32),
                   jax.ShapeDtypeStruct((1, w), F32)],
        compiler_params=_cparams(), name=name)(dout, f, qg, gate)


def _pre_bwd(dhn, x, dout, pg, sc, *, name):
    s_n, w = x.shape
    tm = _pick(s_n, 512, 8)

    def body(dhn_ref, x_ref, do_ref, pg_ref, sc_ref, dx_ref, dsh_ref, dsc_ref, dpg_ref):
        @pl.when(pl.program_id(0) == 0)
        def _():
            dsh_ref[...] = jnp.zeros_like(dsh_ref)
            dsc_ref[...] = jnp.zeros_like(dsc_ref)
            dpg_ref[...] = jnp.zeros_like(dpg_ref)

        dhn_v = dhn_ref[...]
        xv = x_ref[...]
        r = _rstd(xv)
        xh = xv * r
        pg_v = pg_ref[...]
        dsh_ref[...] += jnp.sum(dhn_v, axis=0, keepdims=True)
        dsc_ref[...] += jnp.sum(dhn_v * (xh * pg_v), axis=0, keepdims=True)
        dn = dhn_v * (1.0 + sc_ref[...])
        dpg_ref[...] += jnp.sum(dn * xh, axis=0, keepdims=True)
        dxh = dn * pg_v
        dx_ref[...] = do_ref[...] + r * (dxh - xh * jnp.mean(dxh * xh, axis=-1, keepdims=True))

    vec = jax.ShapeDtypeStruct((1, w), F32)
    return pl.pallas_call(
        body, grid=(s_n // tm,), in_specs=[_rows(tm, w), _rows(tm, w), _rows(tm, w), _vec(w), _vec(w)],
        out_specs=[_rows(tm, w), _vec(w), _vec(w), _vec(w)],
        out_shape=[jax.ShapeDtypeStruct((s_n, w), F32), vec, vec, vec],
        compiler_params=_cparams(), name=name)(dhn, x, dout, pg, sc)


def _rms_fwd(x, g, *, name):
    s_n, w = x.shape
    tm = _pick(s_n, 512, 8)

    def body(x_ref, g_ref, o_ref):
        xv = x_ref[...]
        o_ref[...] = ((xv * _rstd(xv)) * g_ref[...]).astype(o_ref.dtype)

    return pl.pallas_call(body, grid=(s_n // tm,), in_specs=[_rows(tm, w), _vec(w)], out_specs=_rows(tm, w),
                          out_shape=jax.ShapeDtypeStruct((s_n, w), BF16), compiler_params=_cparams(),
                          name=name)(x, g)


def _rms_bwd(dy, x, g, *, name):
    s_n, w = x.shape
    tm = _pick(s_n, 512, 8)

    def body(dy_ref, x_ref, g_ref, dx_ref, dg_ref):
        @pl.when(pl.program_id(0) == 0)
        def _():
            dg_ref[...] = jnp.zeros_like(dg_ref)

        dy_v = dy_ref[...]
        xv = x_ref[...]
        r = _rstd(xv)
        xh = xv * r
        dg_ref[...] += jnp.sum(dy_v * xh, axis=0, keepdims=True)
        dxh = dy_v * g_ref[...]
        dx_ref[...] = r * (dxh - xh * jnp.mean(dxh * xh, axis=-1, keepdims=True))

    return pl.pallas_call(
        body, grid=(s_n // tm,), in_specs=[_rows(tm, w), _rows(tm, w), _vec(w)],
        out_specs=[_rows(tm, w), _vec(w)],
        out_shape=[jax.ShapeDtypeStruct((s_n, w), F32), jax.ShapeDtypeStruct((1, w), F32)],
        compiler_params=_cparams(), name=name)(dy, x, g)


def _rope(a1, a2, cos, sin, *, name):
    s_n, w = a1.shape
    tm = _pick(s_n, 512, 8)

    def body(a1_ref, a2_ref, c_ref, s_ref, r1_ref, r2_ref):
        u, v, c_v, s_v = a1_ref[...], a2_ref[...], c_ref[...], s_ref[...]
        r1_ref[...] = u * c_v - v * s_v
        r2_ref[...] = u * s_v + v * c_v

    sd = jax.ShapeDtypeStruct((s_n, w), F32)
    return pl.pallas_call(body, grid=(s_n // tm,), in_specs=[_rows(tm, w)] * 4, out_specs=[_rows(tm, w)] * 2,
                          out_shape=[sd, sd], compiler_params=_cparams(), name=name)(a1, a2, cos, sin)


def _silu_bf16(x, *, name):
    def body(x_ref, o_ref):
        xv = x_ref[...]
        o_ref[...] = (xv * jax.nn.sigmoid(xv)).astype(o_ref.dtype)

    return pl.pallas_call(body, out_shape=jax.ShapeDtypeStruct(x.shape, BF16), name=name)(x)


def _loss(y, target, *, name):
    s_n, w = y.shape
    tm = _pick(s_n, 512, 8)

    def body(y_ref, t_ref, dy_ref, l_ref):
        @pl.when(pl.program_id(0) == 0)
        def _():
            l_ref[...] = jnp.zeros_like(l_ref)

        e = y_ref[...] - t_ref[...]
        dy_ref[...] = e * (1.0 / w)
        row = jnp.mean(e * e, axis=-1, keepdims=True)
        l_ref[...] += 0.5 * jnp.sum(row, axis=0, keepdims=True)

    return pl.pallas_call(
        body, grid=(s_n // tm,), in_specs=[_rows(tm, w), _rows(tm, w)],
        out_specs=[_rows(tm, w), pl.BlockSpec((1, 1), lambda i: (0, 0))],
        out_shape=[jax.ShapeDtypeStruct((s_n, w), F32), jax.ShapeDtypeStruct((1, 1), F32)],
        compiler_params=_cparams(), name=name)(y, target)


FFN_TM = 512


def _ffn_up(hn, w_gu, *, name):
    s_n, d = hn.shape
    f = w_gu.shape[-1]
    tm = _pick(s_n, FFN_TM, 8)

    def body(hn_ref, wg_ref, wu_ref, gu_ref, a_ref):
        xv = hn_ref[...]
        g = jnp.dot(xv, wg_ref[...], preferred_element_type=F32)
        u = jnp.dot(xv, wu_ref[...], preferred_element_type=F32)
        gu_ref[0] = g.astype(BF16)
        gu_ref[1] = u.astype(BF16)
        a_ref[...] = ((g * jax.nn.sigmoid(g)) * u).astype(BF16)

    w_blk = lambda t: pl.BlockSpec((None, None, d, f), lambda s, m: (s, t, 0, 0))
    return pl.pallas_call(
        body, grid=(N_SHARD, s_n // tm),
        in_specs=[pl.BlockSpec((tm, d), lambda s, m: (m, 0)), w_blk(0), w_blk(1)],
        out_specs=[pl.BlockSpec((None, 2, tm, f), lambda s, m: (s, 0, m, 0)),
                   pl.BlockSpec((None, tm, f), lambda s, m: (s, m, 0))],
        out_shape=[jax.ShapeDtypeStruct((N_SHARD, 2, s_n, f), BF16), jax.ShapeDtypeStruct((N_SHARD, s_n, f), BF16)],
        compiler_params=_cparams(), name=name)(hn, w_gu, w_gu)


def _ffn_down(a, w_dn, x, qg, gate, res_w, *, name):
    _, s_n, f = a.shape
    d = w_dn.shape[-1]
    tm = _pick(s_n, FFN_TM, 8)
    a = a.reshape(-1, 2, s_n, f)
    w_dn = w_dn.reshape(-1, 2, f, d)
    g_n = a.shape[0]

    def body(a_ref, w_ref, x_ref, qg_ref, gate_ref, f_ref, o_ref):
        g = pl.program_id(1)
        r = (jnp.dot(a_ref[0], w_ref[0], preferred_element_type=F32)
             + jnp.dot(a_ref[1], w_ref[1], preferred_element_type=F32))

        @pl.when(g == 0)
        def _():
            f_ref[...] = r

        @pl.when(g > 0)
        def _():
            f_ref[...] += r

        @pl.when(g == g_n - 1)
        def _():
            fv = f_ref[...]
            y = (fv * _rstd(fv)) * qg_ref[...]
            o_ref[...] = x_ref[...] + (res_w * gate_ref[...]) * y

    row = pl.BlockSpec((tm, d), lambda m, g: (m, 0))
    vec = pl.BlockSpec((1, d), lambda m, g: (0, 0))
    sd = jax.ShapeDtypeStruct((s_n, d), F32)
    return pl.pallas_call(
        body, grid=(s_n // tm, g_n),
        in_specs=[pl.BlockSpec((None, 2, tm, f), lambda m, g: (g, 0, m, 0)),
                  pl.BlockSpec((None, 2, f, d), lambda m, g: (g, 0, 0, 0)), row, vec, vec],
        out_specs=[row, row], out_shape=[sd, sd], compiler_params=_cparams(), name=name)(a, w_dn, x, qg, gate)


def _ffn_dhn(dgu, w_gu, x, dout, pg, sc, *, name):
    g_n, _, s_n, f = dgu.shape
    d = w_gu.shape[-2]
    tm = _pick(s_n, FFN_TM, 8)
    nt_dims = (((1,), (1,)), ((), ()))

    def body(a_ref, w_ref, x_ref, do_ref, pg_ref, sc_ref, dx_ref, dsh_ref, dsc_ref, dpg_ref, acc_ref):
        m, g = pl.program_id(0), pl.program_id(1)
        r = (lax.dot_general(a_ref[0], w_ref[0], nt_dims, preferred_element_type=F32)
             + lax.dot_general(a_ref[1], w_ref[1], nt_dims, preferred_element_type=F32))

        @pl.when(g == 0)
        def _():
            acc_ref[...] = r

        @pl.when(g > 0)
        def _():
            acc_ref[...] += r

        @pl.when((m == 0) & (g == 0))
        def _():
            dsh_ref[...] = jnp.zeros_like(dsh_ref)
            dsc_ref[...] = jnp.zeros_like(dsc_ref)
            dpg_ref[...] = jnp.zeros_like(dpg_ref)

        @pl.when(g == g_n - 1)
        def _():
            dhn_v = acc_ref[...]
            xv = x_ref[...]
            rs = _rstd(xv)
            xh = xv * rs
            pg_v = pg_ref[...]
            dsh_ref[...] += jnp.sum(dhn_v, axis=0, keepdims=True)
            dsc_ref[...] += jnp.sum(dhn_v * (xh * pg_v), axis=0, keepdims=True)
            dn = dhn_v * (1.0 + sc_ref[...])
            dpg_ref[...] += jnp.sum(dn * xh, axis=0, keepdims=True)
            dxh = dn * pg_v
            dx_ref[...] = do_ref[...] + rs * (dxh - xh * jnp.mean(dxh * xh, axis=-1, keepdims=True))

    row = pl.BlockSpec((tm, d), lambda m, g: (m, 0))
    vec = pl.BlockSpec((1, d), lambda m, g: (0, 0))
    vsd = jax.ShapeDtypeStruct((1, d), F32)
    return pl.pallas_call(
        body, grid=(s_n // tm, g_n),
        in_specs=[pl.BlockSpec((None, 2, tm, f), lambda m, g: (g, 0, m, 0)),
                  pl.BlockSpec((None, 2, d, f), lambda m, g: (g, 0, 0, 0)), row, row, vec, vec],
        out_specs=[row, vec, vec, vec], out_shape=[jax.ShapeDtypeStruct((s_n, d), F32), vsd, vsd, vsd],
        scratch_shapes=[pltpu.VMEM((tm, d), F32)], compiler_params=_cparams(), name=name)(dgu, w_gu, x, dout, pg, sc)


def _ffn_dgu(df, w_dn, gu, *, name):
    s_n, d = df.shape
    f = w_dn.shape[-2]
    tm = _pick(s_n, FFN_TM, 8)

    def body(df_ref, wd_ref, gu_ref, o_ref):
        da = lax.dot_general(df_ref[...], wd_ref[...], (((1,), (1,)), ((), ())), preferred_element_type=F32)
        g = gu_ref[0].astype(F32)
        u = gu_ref[1].astype(F32)
        sig = jax.nn.sigmoid(g)
        o_ref[0] = (da * u * (sig * (1.0 + g * (1.0 - sig)))).astype(BF16)
        o_ref[1] = (da * (g * sig)).astype(BF16)

    gu_blk = pl.BlockSpec((None, 2, tm, f), lambda s, m: (s, 0, m, 0))
    return pl.pallas_call(
        body, grid=(N_SHARD, s_n // tm),
        in_specs=[pl.BlockSpec((tm, d), lambda s, m: (m, 0)),
                  pl.BlockSpec((None, f, d), lambda s, m: (s, 0, 0)), gu_blk],
        out_specs=gu_blk, out_shape=jax.ShapeDtypeStruct((N_SHARD, 2, s_n, f), BF16),
        compiler_params=_cparams(), name=name)(df, w_dn, gu)


_NT = (((1,), (1,)), ((), ()))
_TN = (((0,), (0,)), ((), ()))
MLA_TQ = 256


def _causal_mask(i, tq, s_n):
    qpos = i * tq + lax.broadcasted_iota(jnp.int32, (tq, s_n), 0)
    kpos = lax.broadcasted_iota(jnp.int32, (tq, s_n), 1)
    return kpos <= qpos


def _mla_attn_fwd(q, k, v, *, name):
    h_n, s_n, dq = q.shape
    dv = v.shape[-1]
    tq = MLA_TQ
    scale = float(dq) ** -0.5

    def body(q_ref, k_ref, v_ref, o_ref, lse_ref):
        i = pl.program_id(1)
        for e in range(1, s_n // tq + 1):
            @pl.when(i == e - 1)
            def _(ext=e * tq):
                mask = _causal_mask(i, tq, ext)
                s = lax.dot_general(q_ref[...], k_ref[0:ext, :], _NT, preferred_element_type=F32) * scale
                s = jnp.where(mask, s, -jnp.inf)
                m = jnp.max(s, axis=-1, keepdims=True)
                p = jnp.exp(s - m)
                l = jnp.sum(p, axis=-1, keepdims=True)
                o = jnp.dot(p.astype(BF16), v_ref[0:ext, :], preferred_element_type=F32)
                o_ref[...] = o / l
                lse_ref[...] = m + jnp.log(l)

    return pl.pallas_call(
        body, grid=(h_n, s_n // tq),
        in_specs=[pl.BlockSpec((None, tq, dq), lambda h, i: (h, i, 0)),
                  pl.BlockSpec((None, s_n, dq), lambda h, i: (h, 0, 0)),
                  pl.BlockSpec((None, s_n, dv), lambda h, i: (h, 0, 0))],
        out_specs=[pl.BlockSpec((None, tq, dv), lambda h, i: (h, i, 0)),
                   pl.BlockSpec((None, tq, 1), lambda h, i: (h, i, 0))],
        out_shape=[jax.ShapeDtypeStruct((h_n, s_n, dv), F32), jax.ShapeDtypeStruct((h_n, s_n, 1), F32)],
        compiler_params=_cparams(), name=name)(q, k, v)


def _mla_attn_bwd(q, k, v, o, do, lse, *, name):
    h_n, s_n, dq = q.shape
    dv = v.shape[-1]
    tq = MLA_TQ
    scale = float(dq) ** -0.5

    def body(q_ref, k_ref, v_ref, o_ref, do_ref, lse_ref, dq_ref, dk_ref, dv_ref):
        i = pl.program_id(1)

        @pl.when(i == 0)
        def _():
            dk_ref[...] = jnp.zeros_like(dk_ref)
            dv_ref[...] = jnp.zeros_like(dv_ref)

        for e in range(1, s_n // tq + 1):
            @pl.when(i == e - 1)
            def _(ext=e * tq):
                mask = _causal_mask(i, tq, ext)
                qv, kv, vv = q_ref[...], k_ref[0:ext, :], v_ref[0:ext, :]
                do_v = do_ref[...]
                s = lax.dot_general(qv, kv, _NT, preferred_element_type=F32) * scale
                p = jnp.where(mask, jnp.exp(s - lse_ref[...]), 0.0)
                dob = do_v.astype(BF16)
                dv_ref[0:ext, :] += lax.dot_general(p.astype(BF16), dob, _TN, preferred_element_type=F32)
                dp = lax.dot_general(dob, vv, _NT, preferred_element_type=F32)
                delta = jnp.sum(do_v * o_ref[...], axis=-1, keepdims=True)
                dsb = (p * (dp - delta) * scale).astype(BF16)
                dq_ref[...] = jnp.dot(dsb, kv, preferred_element_type=F32)
                dk_ref[0:ext, :] += lax.dot_general(dsb, qv, _TN, preferred_element_type=F32)

    return pl.pallas_call(
        body, grid=(h_n, s_n // tq),
        in_specs=[pl.BlockSpec((None, tq, dq), lambda h, i: (h, i, 0)),
                  pl.BlockSpec((None, s_n, dq), lambda h, i: (h, 0, 0)),
                  pl.BlockSpec((None, s_n, dv), lambda h, i: (h, 0, 0)),
                  pl.BlockSpec((None, tq, dv), lambda h, i: (h, i, 0)),
                  pl.BlockSpec((None, tq, dv), lambda h, i: (h, i, 0)),
                  pl.BlockSpec((None, tq, 1), lambda h, i: (h, i, 0))],
        out_specs=[pl.BlockSpec((None, tq, dq), lambda h, i: (h, i, 0)),
                   pl.BlockSpec((None, s_n, dq), lambda h, i: (h, 0, 0)),
                   pl.BlockSpec((None, s_n, dv), lambda h, i: (h, 0, 0))],
        out_shape=[jax.ShapeDtypeStruct((h_n, s_n, dq), F32), jax.ShapeDtypeStruct((h_n, s_n, dq), F32),
                   jax.ShapeDtypeStruct((h_n, s_n, dv), F32)],
        compiler_params=_cparams(), name=name)(q, k, v, o, do, lse)


def _head_sum(x, *, name):
    h_n, s_n, w = x.shape
    tm = _pick(s_n, 512, 8)

    def body(x_ref, o_ref):
        o_ref[...] = jnp.sum(x_ref[...], axis=0)

    return pl.pallas_call(body, grid=(s_n // tm,), in_specs=[pl.BlockSpec((h_n, tm, w), lambda i: (0, i, 0))],
                          out_specs=_rows(tm, w), out_shape=jax.ShapeDtypeStruct((s_n, w), F32),
                          compiler_params=_cparams(), name=name)(x)


N_BLK = SEQ // DIL_BLOCK
DIL_SCALE = 64 ** -0.5


def _dil_masks():
    iq = lax.broadcasted_iota(jnp.int32, (DIL_BLOCK, 2 * DIL_BLOCK), 0)
    ik = lax.broadcasted_iota(jnp.int32, (DIL_BLOCK, 2 * DIL_BLOCK), 1)
    rel = DIL_BLOCK + iq - ik
    both = (rel >= 0) & (rel <= DIL_BLOCK)
    iq1 = lax.broadcasted_iota(jnp.int32, (DIL_BLOCK, DIL_BLOCK), 0)
    ik1 = lax.broadcasted_iota(jnp.int32, (DIL_BLOCK, DIL_BLOCK), 1)
    return both, ik1 <= iq1


def _dil_block(j, d):
    nb = SEQ // d // DIL_BLOCK
    r, n = divmod(j, nb)
    first = n == 0
    rows = lambda start, size: pl.ds(start, size) if d == 1 else pl.ds(start, size, stride=d)
    q_rows = rows(n * DIL_BLOCK * d + r, DIL_BLOCK)
    k_rows = q_rows if first else rows((n - 1) * DIL_BLOCK * d + r, 2 * DIL_BLOCK)
    return q_rows, k_rows, (DIL_BLOCK if first else 0), first


PAIR = 2 * 64
N_PAIR = HEADS // 2


def _dil_head_specs(s_n, g):
    return [pl.BlockSpec((None, s_n, PAIR), lambda hp, t=t: ((g * 3 + t) * N_PAIR + hp, 0, 0)) for t in range(3)]


def _pair_specs(s_n, w):
    return pl.BlockSpec((2, s_n, w), lambda hp: (hp, 0, 0))


_PAIR_BIAS = pl.BlockSpec((2, DIL_BLOCK, 2 * DIL_BLOCK), lambda hp: (hp, 0, 0))


def _dil_attn_fwd(heads, bias, g, d, *, name):
    _, s_n, _ = heads.shape
    e = PAIR // 2

    def body(q_ref, k_ref, v_ref, b_ref, o_ref, lse_ref):
        m_both, m_first = _dil_masks()
        for j in range(N_BLK):
            q_rows, k_rows, b_lo, first = _dil_block(j, d)
            q2 = q_ref[q_rows, :].astype(BF16)
            k2 = k_ref[k_rows, :].astype(BF16)
            v2 = v_ref[k_rows, :].astype(BF16)
            for hh in range(2):
                cols = slice(hh * e, (hh + 1) * e)
                s = (lax.dot_general(q2[:, cols], k2[:, cols], _NT, preferred_element_type=F32) * DIL_SCALE
                     + b_ref[hh, :, b_lo:])
                s = jnp.where(m_first if first else m_both, s, -jnp.inf)
                m = jnp.max(s, axis=-1, keepdims=True)
                lse = m + jnp.log(jnp.sum(jnp.exp(s - m), axis=-1, keepdims=True))
                p = jnp.exp(s - lse)
                o_ref[hh, q_rows, :] = jnp.dot(p.astype(BF16), v2[:, cols], preferred_element_type=F32)
                lse_ref[hh, q_rows, :] = lse

    return pl.pallas_call(
        body, grid=(N_PAIR,), in_specs=_dil_head_specs(s_n, g) + [_PAIR_BIAS],
        out_specs=[_pair_specs(s_n, e), _pair_specs(s_n, 1)],
        out_shape=[jax.ShapeDtypeStruct((HEADS, s_n, e), F32), jax.ShapeDtypeStruct((HEADS, s_n, 1), F32)],
        compiler_params=_cparams(), name=name)(heads, heads, heads, bias)


def _dil_attn_bwd(heads, bias, lse, do, dlt, g, d, *, name):
    _, s_n, _ = heads.shape
    e = PAIR // 2

    def body(q_ref, k_ref, v_ref, b_ref, lse_ref, do_ref, dlt_ref, dq_ref, dk_ref, dv_ref, db_ref):
        db_ref[...] = jnp.zeros_like(db_ref)
        m_both, m_first = _dil_masks()
        nb = s_n // d // DIL_BLOCK
        own_v = own_k = own_rows = None
        for j in range(N_BLK):
            q_rows, k_rows, b_lo, first = _dil_block(j, d)
            q2 = q_ref[q_rows, :].astype(BF16)
            k2 = k_ref[k_rows, :].astype(BF16)
            v2 = v_ref[k_rows, :].astype(BF16)
            dq_h, dv_h, dk_h = [], [], []
            for hh in range(2):
                cols = slice(hh * e, (hh + 1) * e)
                qj, kk, vv = q2[:, cols], k2[:, cols], v2[:, cols]
                s = lax.dot_general(qj, kk, _NT, preferred_element_type=F32) * DIL_SCALE + b_ref[hh, :, b_lo:]
                p = jnp.where(m_first if first else m_both, jnp.exp(s - lse_ref[hh, q_rows, :]), 0.0)
                dob = do_ref[hh, q_rows, :].astype(BF16)
                dv_h.append(lax.dot_general(p.astype(BF16), dob, _TN, preferred_element_type=F32))
                dp = lax.dot_general(dob, vv, _NT, preferred_element_type=F32)
                ds = p * (dp - dlt_ref[hh, q_rows, :])
                db_ref[hh, :, b_lo:] += ds
                dsb = (ds * DIL_SCALE).astype(BF16)
                dq_h.append(jnp.dot(dsb, kk, preferred_element_type=F32))
                dk_h.append(lax.dot_general(dsb, qj, _TN, preferred_element_type=F32))
            dq_ref[q_rows, :] = jnp.concatenate(dq_h, axis=1)
            dvv, dkk = jnp.concatenate(dv_h, axis=1), jnp.concatenate(dk_h, axis=1)
            if not first:
                dv_ref[own_rows, :] = own_v + dvv[:DIL_BLOCK]
                dk_ref[own_rows, :] = own_k + dkk[:DIL_BLOCK]
                dvv, dkk = dvv[DIL_BLOCK:], dkk[DIL_BLOCK:]
            own_v, own_k, own_rows = dvv, dkk, q_rows
            if j % nb == nb - 1:
                dv_ref[own_rows, :] = own_v
                dk_ref[own_rows, :] = own_k

    slab = pl.BlockSpec((None, s_n, PAIR), lambda hp: (hp, 0, 0))
    sd = jax.ShapeDtypeStruct((N_PAIR, s_n, PAIR), F32)
    return pl.pallas_call(
        body, grid=(N_PAIR,),
        in_specs=_dil_head_specs(s_n, g) + [_PAIR_BIAS, _pair_specs(s_n, 1), _pair_specs(s_n, e), _pair_specs(s_n, 1)],
        out_specs=[slab, slab, slab, _PAIR_BIAS],
        out_shape=[sd, sd, sd, jax.ShapeDtypeStruct((HEADS, DIL_BLOCK, 2 * DIL_BLOCK), F32)],
        compiler_params=_cparams(), name=name)(heads, heads, heads, bias, lse, do, dlt)


def _proj_heads(x, w, *, name):
    s_n, k = x.shape
    n = w.shape[-1]
    tm, tn, e = 512, 768, PAIR
    per_blk, n_blk = tn // e, n // tn

    def body(x_ref, w_ref, o_ref):
        r = jnp.dot(x_ref[...], w_ref[...], preferred_element_type=F32)
        for j in range(per_blk):
            o_ref[j] = r[:, e * j:e * (j + 1)]

    return pl.pallas_call(
        body, grid=(w.shape[0], n_blk, s_n // tm),
        in_specs=[pl.BlockSpec((tm, k), lambda s, b, m: (m, 0)), pl.BlockSpec((None, k, tn), lambda s, b, m: (s, 0, b))],
        out_specs=pl.BlockSpec((per_blk, tm, e), lambda s, b, m: (s * n_blk + b, m, 0)),
        out_shape=jax.ShapeDtypeStruct((w.shape[0] * n // e, s_n, e), F32), compiler_params=_cparams(),
        name=name)(x, w)


def _heads_cat(d_ref):
    return jnp.concatenate([d_ref[j] for j in range(d_ref.shape[0])], axis=1)


def _proj_heads_dw(x, dh, *, name):
    s_n, k = x.shape
    tn, e = 768, PAIR
    per_blk = tn // e
    n_blk = dh.shape[0] // N_SHARD // per_blk
    n = n_blk * tn

    def body(x_ref, d_ref, o_ref):
        o_ref[...] = lax.dot_general(x_ref[...], _heads_cat(d_ref), _TN, preferred_element_type=F32)

    return pl.pallas_call(
        body, grid=(N_SHARD, n_blk, 2),
        in_specs=[pl.BlockSpec((s_n, k // 2), lambda s, b, r: (0, r)),
                  pl.BlockSpec((per_blk, s_n, e), lambda s, b, r: (s * n_blk + b, 0, 0))],
        out_specs=pl.BlockSpec((None, None, k // 2, tn), lambda s, b, r: (r, s, 0, b)),
        out_shape=jax.ShapeDtypeStruct((2, N_SHARD, k // 2, n), F32), compiler_params=_cparams(), name=name)(x, dh)


def _proj_heads_dx(dh, w, *, name):
    k, n = w.shape[1:]
    s_n = dh.shape[1]
    tm, tn, e = 512, 768, PAIR
    per_blk, n_blk = tn // e, n // tn

    def body(d_ref, w_ref, o_ref):
        r = lax.dot_general(_heads_cat(d_ref), w_ref[...], _NT, preferred_element_type=F32)
        g = pl.program_id(1)

        @pl.when(g == 0)
        def _():
            o_ref[...] = r

        @pl.when(g > 0)
        def _():
            o_ref[...] += r

    return pl.pallas_call(
        body, grid=(s_n // tm, N_SHARD * n_blk),
        in_specs=[pl.BlockSpec((per_blk, tm, e), lambda m, g: (g, m, 0)),
                  pl.BlockSpec((None, k, tn), lambda m, g: (g // n_blk, 0, g % n_blk))],
        out_specs=pl.BlockSpec((tm, k), lambda m, g: (m, 0)),
        out_shape=jax.ShapeDtypeStruct((s_n, k), F32), compiler_params=_cparams(), name=name)(dh, w)


def _group_alpha(ls):
    m = jnp.maximum(jnp.maximum(ls[0], ls[1]), ls[2])
    es = [jnp.exp(l - m) for l in ls]
    tot = es[0] + es[1] + es[2]
    return [ex / tot for ex in es]


def _dil_mix_fwd(os_, ls_, *, name):
    h_n, s_n, e = os_[0].shape
    tm = 512

    def body(o0, o1, o2, l0, l1, l2, out_ref):
        for hh in range(2):
            al = _group_alpha([l[hh] for l in (l0, l1, l2)])
            mix = al[0] * o0[hh] + al[1] * o1[hh] + al[2] * o2[hh]
            out_ref[:, hh * e:(hh + 1) * e] = mix.astype(out_ref.dtype)

    blk = lambda w: pl.BlockSpec((2, tm, w), lambda h, i: (h, i, 0))
    return pl.pallas_call(body, grid=(h_n // 2, s_n // tm), in_specs=[blk(e)] * 3 + [blk(1)] * 3,
                          out_specs=pl.BlockSpec((tm, 2 * e), lambda h, i: (i, h)),
                          out_shape=jax.ShapeDtypeStruct((s_n, h_n * e), BF16), compiler_params=_cparams(),
                          name=name)(*os_, *ls_)


def _dil_mix_bwd(do_flat, os_, ls_, *, name):
    h_n, s_n, e = os_[0].shape
    tm = 512

    def body(do_ref, o0, o1, o2, l0, l1, l2, d0, d1, d2, t0, t1, t2):
        for hh in range(2):
            al = _group_alpha([l[hh] for l in (l0, l1, l2)])
            do_v = do_ref[:, hh * e:(hh + 1) * e]
            mix = al[0] * o0[hh] + al[1] * o1[hh] + al[2] * o2[hh]
            dbar = jnp.sum(do_v * mix, axis=-1, keepdims=True)
            for a_g, d_ref, t_ref in zip(al, (d0, d1, d2), (t0, t1, t2)):
                d_ref[hh] = a_g * do_v
                t_ref[hh] = a_g * dbar

    blk = lambda w: pl.BlockSpec((2, tm, w), lambda h, i: (h, i, 0))
    sd_e = jax.ShapeDtypeStruct((h_n, s_n, e), F32)
    sd_1 = jax.ShapeDtypeStruct((h_n, s_n, 1), F32)
    outs = pl.pallas_call(body, grid=(h_n // 2, s_n // tm),
                          in_specs=[pl.BlockSpec((tm, 2 * e), lambda h, i: (i, h))] + [blk(e)] * 3 + [blk(1)] * 3,
                          out_specs=[blk(e)] * 3 + [blk(1)] * 3, out_shape=[sd_e] * 3 + [sd_1] * 3,
                          compiler_params=_cparams(), name=name)(do_flat, *os_, *ls_)
    return outs[:3], outs[3:]


def _bias_grad(ds, bucket, *, name):
    h_n = ds.shape[0]

    def body(ds_ref, bk_ref, o_ref):
        ds_v = ds_ref[...]
        bk = bk_ref[...]
        lane = lax.broadcasted_iota(jnp.int32, (1, N_BUCKETS), 1)
        acc = jnp.zeros((1, N_BUCKETS), F32)
        for b in range(N_BUCKETS):
            tot = jnp.sum(jnp.sum(jnp.where(bk == b, ds_v, 0.0), axis=1, keepdims=True), axis=0, keepdims=True)
            acc = acc + jnp.where(lane == b, tot, 0.0)
        o_ref[...] = acc

    return pl.pallas_call(
        body, grid=(h_n,),
        in_specs=[pl.BlockSpec((None, DIL_BLOCK, 2 * DIL_BLOCK), lambda h: (h, 0, 0)),
                  pl.BlockSpec((DIL_BLOCK, 2 * DIL_BLOCK), lambda h: (0, 0))],
        out_specs=pl.BlockSpec((None, 1, N_BUCKETS), lambda h: (h, 0, 0)),
        out_shape=jax.ShapeDtypeStruct((h_n, 1, N_BUCKETS), F32), compiler_params=_cparams(), name=name)(ds, bucket)


def _bias_table(rb, bucket, *, name):
    h_n = rb.shape[0]

    def body(rb_ref, bk_ref, o_ref):
        bk = bk_ref[...]
        row = rb_ref[...]
        acc = jnp.zeros(bk.shape, F32)
        for b in range(N_BUCKETS):
            acc = jnp.where(bk == b, row[:, b:b + 1], acc)
        o_ref[...] = acc

    return pl.pallas_call(
        body, grid=(h_n,),
        in_specs=[pl.BlockSpec((None, 1, N_BUCKETS), lambda h: (h, 0, 0)),
                  pl.BlockSpec((DIL_BLOCK, 2 * DIL_BLOCK), lambda h: (0, 0))],
        out_specs=pl.BlockSpec((None, DIL_BLOCK, 2 * DIL_BLOCK), lambda h: (h, 0, 0)),
        out_shape=jax.ShapeDtypeStruct((h_n, DIL_BLOCK, 2 * DIL_BLOCK), F32), compiler_params=_cparams(),
        name=name)(rb, bucket)


def _row_tile(rows, cols, budget=2 << 20):
    if rows * cols * 4 <= budget or rows % 8:
        return rows
    best = 8
    for t in range(8, rows + 1, 8):
        if rows % t == 0 and t * cols * 4 <= budget:
            best = t
    return best


def _adamw(w, g, m, v, *, name):
    shape = w.shape
    cols = shape[-1]
    rows = math.prod(shape[:-1]) if len(shape) > 1 else 1
    to2 = lambda t: t.reshape(rows, cols)
    tr = _row_tile(rows, cols)
    c1 = 1.0 / (1.0 - ADAM_B1 ** ADAM_STEP)
    c2 = 1.0 / (1.0 - ADAM_B2 ** ADAM_STEP)

    def body(w_ref, g_ref, m_ref, v_ref, d_ref, nm_ref, nv_ref):
        g_v = g_ref[...]
        nm = ADAM_B1 * m_ref[...] + (1.0 - ADAM_B1) * g_v
        nv = ADAM_B2 * v_ref[...] + (1.0 - ADAM_B2) * (g_v * g_v)
        m_hat = nm * c1
        v_hat = nv * c2
        d_ref[...] = -ADAM_LR * (m_hat / (jnp.sqrt(v_hat) + ADAM_EPS) + ADAM_WD * w_ref[...])
        nm_ref[...] = nm
        nv_ref[...] = nv

    blk = pl.BlockSpec((tr, cols), lambda i: (i, 0))
    sd = jax.ShapeDtypeStruct((rows, cols), F32)
    outs = pl.pallas_call(body, grid=(rows // tr,), in_specs=[blk] * 4, out_specs=[blk] * 3, out_shape=[sd] * 3,
                          compiler_params=_cparams(), name=name)(to2(w), to2(g), to2(m), to2(v))
    return tuple(t.reshape(shape) for t in outs)


def _add_half(unit, got, half_idx, *, name):
    rest = unit.shape[2:]
    c = rest[-1]
    r = math.prod(rest[:-1])
    tr = _row_tile(r, c)

    def body(idx_ref, u_ref, g_ref, o_ref, w_ref):
        tot = u_ref[...] + g_ref[...].astype(F32)
        o_ref[...] = tot
        w_ref[...] = tot.astype(BF16)

    blk = pl.BlockSpec((None, tr, c), lambda s, i, idx: (s, i, 0))
    grid_spec = pltpu.PrefetchScalarGridSpec(
        num_scalar_prefetch=1, grid=(N_SHARD, r // tr),
        in_specs=[pl.BlockSpec((None, None, tr, c), lambda s, i, idx: (idx[0], s, i, 0)), blk],
        out_specs=[blk, blk])
    out, wire = pl.pallas_call(
        body, grid_spec=grid_spec,
        out_shape=[jax.ShapeDtypeStruct((N_SHARD, r, c), F32), jax.ShapeDtypeStruct((N_SHARD, r, c), BF16)],
        compiler_params=_cparams(), name=name)(half_idx, unit.reshape(2, N_SHARD, r, c), got.reshape(N_SHARD, r, c))
    return out.reshape((N_SHARD,) + rest), wire.reshape((N_SHARD,) + rest)


def _add_shards(part, got, shard_idx, *, name):
    rest = part.shape[1:]
    c = rest[-1]
    r = math.prod(rest[:-1])
    tr = _row_tile(r, c)

    def body(idx_ref, p_ref, g_ref, o_ref):
        acc = p_ref[...]
        for k in range(3):
            acc = acc + g_ref[k].astype(F32)
        o_ref[...] = acc

    grid_spec = pltpu.PrefetchScalarGridSpec(
        num_scalar_prefetch=1, grid=(r // tr,),
        in_specs=[pl.BlockSpec((None, tr, c), lambda i, idx: (idx[0], i, 0)),
                  pl.BlockSpec((3, tr, c), lambda i, idx: (0, i, 0))],
        out_specs=pl.BlockSpec((tr, c), lambda i, idx: (i, 0)))
    out = pl.pallas_call(body, grid_spec=grid_spec, out_shape=jax.ShapeDtypeStruct((r, c), F32),
                         compiler_params=_cparams(), name=name)(
        shard_idx, part.reshape(N_SHARD, r, c), got.reshape(3, r, c))
    return out.reshape(rest)


def _sum_devices(x, n_dev, *, name):
    rows = x.shape[0] // n_dev

    def body(x_ref, o_ref):
        acc = x_ref[0:rows, :]
        for d in range(1, n_dev):
            acc = acc + x_ref[d * rows:(d + 1) * rows, :]
        o_ref[...] = acc

    return pl.pallas_call(body, out_shape=jax.ShapeDtypeStruct((rows, x.shape[1]), F32), name=name)(x)


def _my_pos():
    return lax.axis_index("x"), lax.axis_index("y"), lax.axis_index("c")


def _all_gather(x_blk, *, name, in_vmem):
    m_per, n = x_blk.shape

    def body(x_ref, out_ref, send_sems, recv_sems, local_sem):
        x, y, c = _my_pos()
        me, sibling = (x, y, c), (x, y, 1 - c)
        chips = [(1 - x, y), (x, 1 - y), (1 - x, 1 - y)]

        def rows(px, py, pc):
            return out_ref.at[pl.ds((4 * px + 2 * py + pc) * m_per, m_per), :]

        def copy(k, block, to, src=None):
            return pltpu.make_async_remote_copy(
                src_ref=rows(*block) if src is None else src, dst_ref=rows(*block),
                send_sem=send_sems.at[k], recv_sem=recv_sems.at[k], device_id=to, device_id_type=MESH)

        mine = pltpu.make_async_copy(x_ref, rows(*me), local_sem)
        mine.start()
        first = [copy(0, me, sibling, src=x_ref)]
        first += [copy(1 + j, me, (*chip, c), src=x_ref) for j, chip in enumerate(chips)]
        for cp in first:
            cp.start()
        passed = [copy(4 + j, (*chip, c), sibling) for j, chip in enumerate(chips)]
        for j, chip in enumerate(chips):
            copy(1 + j, (*chip, c), me).wait_recv()
            passed[j].start()
        copy(0, sibling, me).wait_recv()
        for j, chip in enumerate(chips):
            copy(4 + j, (*chip, 1 - c), me).wait_recv()
        for cp in first + passed:
            cp.wait_send()
        mine.wait()

    space = pltpu.VMEM if in_vmem else pl.ANY
    return pl.pallas_call(
        body, out_shape=jax.ShapeDtypeStruct((8 * m_per, n), x_blk.dtype),
        in_specs=[pl.BlockSpec(memory_space=space)], out_specs=pl.BlockSpec(memory_space=space),
        scratch_shapes=[pltpu.SemaphoreType.DMA((7,)), pltpu.SemaphoreType.DMA((7,)), pltpu.SemaphoreType.DMA],
        name=name)(x_blk)


_HBM = pl.BlockSpec(memory_space=pl.ANY)


def _gather_weights(fams, *, name):
    n = len(fams)

    def body(*refs):
        ins, outs = refs[:n], refs[n:2 * n]
        send_sems, recv_sems = refs[2 * n:]
        x, y, c = _my_pos()
        me, sibling = (x, y, c), (x, y, 1 - c)
        chips = [(1 - x, y), (x, 1 - y), (1 - x, 1 - y)]

        def copy(f, k, block, to, src=None):
            px, py, pc = block
            dst = outs[f].at[2 * px + py, pc]
            return pltpu.make_async_remote_copy(
                src_ref=dst if src is None else src, dst_ref=dst, send_sem=send_sems.at[7 * f + k],
                recv_sem=recv_sems.at[7 * f + k], device_id=to, device_id_type=MESH)

        first, passed = [], []
        for f in range(n):
            src = ins[f].at[c]
            first.append(copy(f, 0, me, sibling, src=src))
            first += [copy(f, 1 + j, me, (*chip, c), src=src) for j, chip in enumerate(chips)]
        for cp in first:
            cp.start()
        for j, chip in enumerate(chips):
            for f in range(n):
                copy(f, 1 + j, (*chip, c), me).wait_recv()
                passed.append(copy(f, 4 + j, (*chip, c), sibling))
                passed[-1].start()
        for f in range(n):
            copy(f, 0, sibling, me).wait_recv()
        for j, chip in enumerate(chips):
            for f in range(n):
                copy(f, 4 + j, (*chip, 1 - c), me).wait_recv()
        for cp in first + passed:
            cp.wait_send()

    outs = pl.pallas_call(
        body, out_shape=[jax.ShapeDtypeStruct((N_SHARD,) + t.shape, t.dtype) for t in fams],
        in_specs=[_HBM] * n, out_specs=[_HBM] * n,
        scratch_shapes=[pltpu.SemaphoreType.DMA((7 * n,)), pltpu.SemaphoreType.DMA((7 * n,))], name=name)(*fams)
    return [_place_own(o, t) for o, t in zip(outs, fams)]


def _pair_gather(halves, *, name):
    n = len(halves)

    def body(*refs):
        ins, outs = refs[:n], refs[n:2 * n]
        send_sems, recv_sems = refs[2 * n:]
        x, y, c = _my_pos()
        cps = [pltpu.make_async_remote_copy(src_ref=ins[f], dst_ref=outs[f].at[c], send_sem=send_sems.at[f],
                                            recv_sem=recv_sems.at[f], device_id=(x, y, 1 - c), device_id_type=MESH)
               for f in range(n)]
        for cp in cps:
            cp.start()
        for f in range(n):
            pltpu.make_async_remote_copy(src_ref=ins[f], dst_ref=outs[f].at[1 - c], send_sem=send_sems.at[f],
                                         recv_sem=recv_sems.at[f], device_id=(x, y, 1 - c),
                                         device_id_type=MESH).wait_recv()
        for cp in cps:
            cp.wait_send()

    outs = pl.pallas_call(
        body, out_shape=[jax.ShapeDtypeStruct((2,) + t.shape, t.dtype) for t in halves],
        in_specs=[_HBM] * n, out_specs=[_HBM] * n,
        scratch_shapes=[pltpu.SemaphoreType.DMA((n,)), pltpu.SemaphoreType.DMA((n,))], name=name)(*halves)
    c = lax.axis_index("c")
    return [lax.dynamic_update_index_in_dim(o, t, c, 0) for o, t in zip(outs, halves)]


_HBM_ONLY = pl.BlockSpec(memory_space=pltpu.HBM)
_SEMS = pl.BlockSpec(memory_space=pltpu.SEMAPHORE)
_EFFECT = pltpu.SideEffectType.DATAFLOW_SIDE_EFFECTING


def _copies_start(srcs, lands, plan, n_copies, *, name):
    n, m = len(srcs), len(lands)

    def body(*refs):
        src_refs, land_refs = refs[:n], refs[n:n + m]
        send_sems, recv_sems, token = refs[n + m], refs[n + m + 1], refs[-1]
        for k, (src, dst, peer) in enumerate(plan(src_refs, land_refs)):
            pltpu.make_async_remote_copy(src_ref=src, dst_ref=dst, send_sem=send_sems.at[k], recv_sem=recv_sems.at[k],
                                         device_id=peer, device_id_type=MESH).start()
        token[...] = jnp.zeros_like(token)

    bufs = [pltpu.with_memory_space_constraint(t, pltpu.HBM) for t in (*srcs, *lands)]
    outs = pl.pallas_call(
        body, name=name,
        out_shape=(pltpu.SemaphoreType.DMA((n_copies,)), pltpu.SemaphoreType.DMA((n_copies,)),
                   *[pltpu.HBM(t.shape, t.dtype) for t in bufs], jax.ShapeDtypeStruct((8, 128), F32)),
        in_specs=[_HBM_ONLY] * (n + m),
        out_specs=(_SEMS, _SEMS, *[_HBM_ONLY] * (n + m), pl.BlockSpec(memory_space=pltpu.VMEM)),
        input_output_aliases={k: 2 + k for k in range(n + m)},
        compiler_params=pltpu.CompilerParams(has_side_effects=_EFFECT))(*bufs)
    return outs[0], outs[1], list(outs[2:2 + n + m]), outs[-1]


def _copies_wait(send_sems, recv_sems, thru, n_src, plan, after, *, name):
    nm = len(thru)

    def body(*refs):
        t_refs, send, recv = refs[:nm], refs[nm], refs[nm + 1]
        for k, (src, dst, peer) in enumerate(plan(t_refs[:n_src], t_refs[n_src:])):
            cp = pltpu.make_async_remote_copy(src_ref=src, dst_ref=dst, send_sem=send.at[k], recv_sem=recv.at[k],
                                              device_id=peer, device_id_type=MESH)
            cp.wait_send()
            cp.wait_recv()

    outs = pl.pallas_call(
        body, name=name, out_shape=tuple(pltpu.HBM(t.shape, t.dtype) for t in thru),
        in_specs=[_HBM_ONLY] * nm + [_SEMS, _SEMS, pl.BlockSpec(memory_space=pl.ANY)],
        out_specs=tuple([_HBM_ONLY] * nm), input_output_aliases={k: k for k in range(nm)},
        compiler_params=pltpu.CompilerParams(has_side_effects=_EFFECT))(*thru, send_sems, recv_sems, after)
    return list(outs)


_RELATIONS = [(dx, dy, dc) for dx in (0, 1) for dy in (0, 1) for dc in (0, 1)][1:]


def _gather_plan(src_refs, land_refs):
    x, y, c = _my_pos()
    flip = lambda v, d: 1 - v if d else v
    return [(s_ref.at[c], l_ref.at[2 * x + y, c], (flip(x, dx), flip(y, dy), flip(c, dc)))
            for s_ref, l_ref in zip(src_refs, land_refs) for dx, dy, dc in _RELATIONS]


def _gather_chips_plan(src_refs, land_refs):
    x, y, c = _my_pos()
    peers = [(x, y, 1 - c), (1 - x, y, c), (x, 1 - y, c), (1 - x, 1 - y, c)]
    return [(s_ref.at[c], l_ref.at[2 * x + y, c], peer) for s_ref, l_ref in zip(src_refs, land_refs) for peer in peers]


def _gather_pass_plan(src_refs, land_refs):
    x, y, c = _my_pos()
    chips = [(1 - x, y), (x, 1 - y), (1 - x, 1 - y)]
    return [(l_ref.at[2 * cx + cy, c], l_ref.at[2 * cx + cy, c], (x, y, 1 - c))
            for l_ref in land_refs for cx, cy in chips]


def _sibling_plan(src_refs, land_refs):
    x, y, c = _my_pos()
    return [(s_ref.at[1 - c], l_ref, (x, y, 1 - c)) for s_ref, l_ref in zip(src_refs, land_refs)]


def _chips_plan(src_refs, land_refs):
    x, y, c = _my_pos()
    chips = [(1 - x, y), (x, 1 - y), (1 - x, 1 - y)]
    return [(s_ref.at[2 * cx + cy], l_ref.at[k], (cx, cy, c))
            for s_ref, l_ref in zip(src_refs, land_refs) for k, (cx, cy) in enumerate(chips)]


def _place_own(gathered, fam):
    x, y, c = _my_pos()
    own = lax.dynamic_index_in_dim(fam, c, 0, keepdims=True)[None]
    return lax.dynamic_update_slice(gathered, own, (2 * x + y, c) + (0,) * (fam.ndim - 1))


def _to_heads(t, width):
    return t.reshape(t.shape[0], HEADS, width).transpose(1, 0, 2)


def _from_heads(t):
    return t.transpose(1, 0, 2).reshape(t.shape[1], -1)


def _t5_bucket(dist):
    max_exact = N_BUCKETS // 2
    d = jnp.maximum(dist, 1).astype(F32)
    large = max_exact + (jnp.log(d / max_exact) / math.log(MAX_DISTANCE / max_exact)
                         * (N_BUCKETS - max_exact)).astype(jnp.int32)
    large = jnp.minimum(large, N_BUCKETS - 1)
    return jnp.where(dist < max_exact, dist, large)


def _bucket_map(dilation):
    iq = jnp.arange(DIL_BLOCK)[:, None]
    ik = jnp.arange(2 * DIL_BLOCK)[None, :]
    rel = DIL_BLOCK + iq - ik
    return _t5_bucket(jnp.maximum(rel, 0) * dilation).astype(jnp.int32)


def _q_perm(w):
    w3 = w.reshape(w.shape[0], HEADS, QK_NOPE + QK_ROPE)
    return jnp.concatenate([w3[:, :, :QK_NOPE].reshape(w.shape[0], -1),
                            w3[:, :, QK_NOPE:QK_NOPE + HALF_ROPE].reshape(w.shape[0], -1),
                            w3[:, :, QK_NOPE + HALF_ROPE:].reshape(w.shape[0], -1)], axis=1)


def _q_unperm(w):
    n0, n1 = HEADS * QK_NOPE, HEADS * HALF_ROPE
    r = w.shape[0]
    return jnp.concatenate([w[:, :n0].reshape(r, HEADS, QK_NOPE), w[:, n0:n0 + n1].reshape(r, HEADS, HALF_ROPE),
                            w[:, n0 + n1:].reshape(r, HEADS, HALF_ROPE)], axis=2).reshape(r, -1)


def _kv_perm(w):
    w3 = w.reshape(w.shape[0], HEADS, QK_NOPE + V_HEAD)
    return jnp.concatenate([w3[:, :, :QK_NOPE].reshape(w.shape[0], -1), w3[:, :, QK_NOPE:].reshape(w.shape[0], -1)],
                           axis=1)


def _kv_unperm(w):
    n0 = HEADS * QK_NOPE
    r = w.shape[0]
    return jnp.concatenate([w[:, :n0].reshape(r, HEADS, QK_NOPE), w[:, n0:].reshape(r, HEADS, V_HEAD)],
                           axis=2).reshape(r, -1)


def _row(v):
    return v.reshape(1, -1)


def kernel(x, c, norm_pre, norm_post, w_mod, b_mod, ffn_w_gate, ffn_w_up, ffn_w_down, mla_w_in, mla_q_norm, mla_w_q_up, mla_kv_norm, mla_w_kv_up, mla_w_o, dil_w_in, dil_w_o, rel_bias, loss_target, m_norm_pre, m_norm_post, m_w_mod, m_b_mod, m_ffn_w_gate, m_ffn_w_up, m_ffn_w_down, m_mla_w_in, m_mla_q_norm, m_mla_w_q_up, m_mla_kv_norm, m_mla_w_kv_up, m_mla_w_o, m_dil_w_in, m_dil_w_o, m_rel_bias, v_norm_pre, v_norm_post, v_w_mod, v_b_mod, v_ffn_w_gate, v_ffn_w_up, v_ffn_w_down, v_mla_w_in, v_mla_q_norm, v_mla_w_q_up, v_mla_kv_norm, v_mla_w_kv_up, v_mla_w_o, v_dil_w_in, v_dil_w_o, v_rel_bias):
    given = dict(locals())
    ix, iy, ic = _my_pos()
    shard_id = 2 * ix + iy
    dev_id = 4 * ix + 2 * iy + ic
    x2 = x[0]
    target = loss_target[0]
    half_idx = jnp.reshape(ic, (1,)).astype(jnp.int32)
    shard_idx = jnp.reshape(shard_id, (1,)).astype(jnp.int32)

    blk = jnp.zeros((8, D_MODEL), F32)
    blk = blk.at[0].set(c[0])
    blk = blk.at[1:3].set(jnp.pad(norm_pre.reshape(-1), (0, 512)).reshape(2, D_MODEL))
    blk = blk.at[3:5].set(jnp.pad(norm_post.reshape(-1), (0, 512)).reshape(2, D_MODEL))
    got = _all_gather(blk, name="ag_c_norms", in_vmem=True).reshape(N_SHARD, 2, 8, D_MODEL)
    c_all = got[:, :, 0, :].reshape(8, D_MODEL)

    def full_norm(lo):
        t = got[:, 0, lo:lo + 2, :].reshape(N_SHARD, 2 * D_MODEL)[:, :1536].reshape(N_SHARD, 2, 3, 256)
        return t.transpose(1, 2, 0, 3).reshape(2, 3, D_MODEL)

    pre_full, post_full = full_norm(1), full_norm(3)

    silu_c = _silu_bf16(c_all, name="silu_c")
    b_cols = lax.dynamic_slice_in_dim(b_mod, shard_id * 2304, 2304, axis=1).reshape(2, 1, 2304)
    mod_part = _mm(silu_c, w_mod, bias=b_cols, name="mod_mm", tn_cap=768)
    mod_all = _all_gather(mod_part.reshape(16, 2304), name="ag_mod", in_vmem=True)
    mod_all = mod_all.reshape(N_SHARD, 2, 2, 8, 2304)[:, 0]
    mod_mine = lax.dynamic_index_in_dim(mod_all, dev_id, axis=2, keepdims=False)
    mod = mod_mine.transpose(1, 0, 2).reshape(2, 9, D_MODEL)

    bf = lambda t: t.astype(BF16)
    ffn_fam = lambda i, h: [bf(jnp.stack([ffn_w_gate[i, h], ffn_w_up[i, h]])),
                            bf(ffn_w_down[i, h].reshape(2, F_SHARD // 2, D_MODEL))]
    mla_fam = [bf(mla_w_in.reshape(2, 128, -1)), bf(mla_w_q_up.reshape(2, 192, -1)),
               bf(mla_w_kv_up.reshape(2, 128, -1)), bf(mla_w_o.reshape(2, 128, D_MODEL))]
    dil_fam = [bf(dil_w_in.reshape(2, 512, -1)), bf(dil_w_o.reshape(2, 128, D_MODEL))]
    later_fams = [ffn_fam(0, 1), ffn_fam(1, 0) + dil_fam, ffn_fam(1, 1)]
    full, later_fams, mod = lax.optimization_barrier(
        (_gather_weights(ffn_fam(0, 0) + mla_fam, name="ag_weights_first"), later_fams, mod))

    def gather_later(fams, tag):
        lands = [lax.empty((N_SHARD,) + t.shape, t.dtype) for t in fams]
        send, recv, thru, token = _copies_start(fams, lands, _gather_plan, 7 * len(fams), name=f"ag_start_{tag}")
        return dict(send=send, recv=recv, thru=thru, token=token, n=len(fams), tag=tag)

    def arrive(st, after):
        thru = _copies_wait(st['send'], st['recv'], st['thru'], st['n'], _gather_plan, after,
                            name=f"ag_wait_{st['tag']}")
        return [_place_own(o, t) for t, o in zip(thru[:st['n']], thru[st['n']:])]

    def gather_chips(fams, tag):
        lands = [lax.empty((N_SHARD,) + t.shape, t.dtype) for t in fams]
        send, recv, thru, token = _copies_start(fams, lands, _gather_chips_plan, 4 * len(fams), name=f"ag_start_{tag}")
        return dict(send=send, recv=recv, thru=thru, token=token, n=len(fams), tag=tag)

    def pass_on(st, after):
        n, tag = st['n'], st['tag']
        thru = _copies_wait(st['send'], st['recv'], st['thru'], n, _gather_chips_plan, after, name=f"ag_mid_{tag}")
        send, recv, lands, token = _copies_start([], thru[n:], _gather_pass_plan, 3 * n, name=f"ag_pass_{tag}")
        return dict(send=send, recv=recv, thru=lands, fams=thru[:n], tag=tag), token[0, 0]

    def arrive_passed(st, after):
        lands = _copies_wait(st['send'], st['recv'], st['thru'], 0, _gather_pass_plan, after, name=f"ag_wait_{st['tag']}")
        return [_place_own(o, t) for t, o in zip(st['fams'], lands)]

    flight_a = gather_later(later_fams[0], "l0s2")
    _, next_fams = lax.optimization_barrier((flight_a['token'], later_fams[1]))
    flight_b = gather_chips(next_fams, "l1s01")
    as_ffn = lambda w_gu, w_dn: (w_gu, w_dn.reshape(N_SHARD, F_SHARD, D_MODEL))
    ffn_w = {(0, 0): as_ffn(full[0], full[1])}
    w_in = full[2].reshape(D_MODEL, -1)
    wq_p = _q_perm(full[3].reshape(N_SHARD, Q_LORA, -1).transpose(1, 0, 2).reshape(Q_LORA, -1))
    wkv_p = _kv_perm(full[4].reshape(N_SHARD, KV_LORA, -1).transpose(1, 0, 2).reshape(KV_LORA, -1))
    w_mo = full[5].reshape(D_MODEL, D_MODEL)
    dil_w = {}

    pos = jnp.arange(SEQ, dtype=F32)
    freqs = ROPE_THETA ** (-jnp.arange(HALF_ROPE, dtype=F32) / HALF_ROPE)
    ang = pos[:, None] * freqs[None, :]
    cos_k, sin_k = jnp.cos(ang), jnp.sin(ang)
    cos_q, sin_q = jnp.tile(cos_k, (1, HEADS)), jnp.tile(sin_k, (1, HEADS))

    buckets = [_bucket_map(d) for _, d in DIL_GROUPS]
    biases = [_bias_table(rel_bias[:, g * HEADS:(g + 1) * HEADS].T.reshape(HEADS, 1, N_BUCKETS), bk,
                          name=f"dil_bias_table_g{g}") for g, bk in enumerate(buckets)]

    def sub_params(i, sub):
        return dict(pg=_row(pre_full[i, sub]), qg=_row(post_full[i, sub]), sh=_row(mod[i, 3 * sub]),
                    sc=_row(mod[i, 3 * sub + 1]), gate=_row(mod[i, 3 * sub + 2]))

    def ffn_fwd(xin, i, h, sub, tie=None, mid=None):
        p = sub_params(i, sub)
        if tie is not None:
            p['sh'] = p['sh'] + tie
        tag = f"l{i}s{sub}"
        w_gu, w_dn = ffn_w[i, h]
        hn = _pre_fwd(xin, p['pg'], p['sc'], p['sh'], name=f"pre_fwd_{tag}")
        gu, a = _ffn_up(hn, w_gu, name=f"ffn_up_{tag}")
        if mid is not None:
            p['qg'] = p['qg'] + mid(a)
        f, out = _ffn_down(a, w_dn, xin, p['qg'], p['gate'], FFN_RES, name=f"ffn_down_{tag}")
        return out, dict(x=xin, hn=hn, gu=gu, a=a, f=f, p=p, i=i, h=h, tag=tag)

    def mla_fwd(xin, i, sub):
        p = sub_params(i, sub)
        tag = f"l{i}s{sub}"
        hn = _pre_fwd(xin, p['pg'], p['sc'], p['sh'], name=f"pre_fwd_{tag}")
        lat = _mm(hn, w_in, name="mla_lat")
        cq, ckv = lat[:, :Q_LORA], lat[:, Q_LORA:Q_LORA + KV_LORA]
        k1, k2 = lat[:, Q_LORA + KV_LORA:Q_LORA + KV_LORA + HALF_ROPE], lat[:, Q_LORA + KV_LORA + HALF_ROPE:]
        cqn = _rms_fwd(cq, mla_q_norm, name="mla_qnorm")
        ckvn = _rms_fwd(ckv, mla_kv_norm, name="mla_kvnorm")
        qp = _mm(cqn, wq_p, name="mla_q_up")
        kvp = _mm(ckvn, wkv_p, name="mla_kv_up")
        n0, n1 = HEADS * QK_NOPE, HEADS * HALF_ROPE
        qr1, qr2 = _rope(qp[:, n0:n0 + n1], qp[:, n0 + n1:], cos_q, sin_q, name="rope_q")
        kr1, kr2 = _rope(k1, k2, cos_k, sin_k, name="rope_k")
        q = jnp.concatenate([qp[:, :n0].reshape(SEQ, HEADS, QK_NOPE), qr1.reshape(SEQ, HEADS, HALF_ROPE),
                             qr2.reshape(SEQ, HEADS, HALF_ROPE)], axis=2).transpose(1, 0, 2).astype(BF16)
        kr = jnp.broadcast_to(jnp.concatenate([kr1, kr2], axis=1)[:, None, :], (SEQ, HEADS, QK_ROPE))
        k = jnp.concatenate([kvp[:, :n0].reshape(SEQ, HEADS, QK_NOPE), kr], axis=2).transpose(1, 0, 2).astype(BF16)
        v = _to_heads(kvp[:, n0:], V_HEAD).astype(BF16)
        o, lse = _mla_attn_fwd(q, k, v, name="mla_attn_fwd")
        o_flat = _from_heads(o).astype(BF16)
        f = _mm(o_flat, w_mo, name="mla_out")
        out = _post_fwd(f, xin, p['qg'], p['gate'], 1.0, name=f"post_fwd_{tag}")
        return out, dict(x=xin, hn=hn, cq=cq, ckv=ckv, cqn=cqn, ckvn=ckvn, q=q, k=k, v=v, o=o, lse=lse,
                         o_flat=o_flat, f=f, p=p, tag=tag)

    def dil_fwd(xin, i, sub):
        p = sub_params(i, sub)
        tag = f"l{i}s{sub}"
        hn = _pre_fwd(xin, p['pg'], p['sc'], p['sh'], name=f"pre_fwd_{tag}")
        heads = _proj_heads(hn, dil_w['in'], name="dil_proj")
        outs, lses = [], []
        for g, (window, d) in enumerate(DIL_GROUPS):
            o, lse = _dil_attn_fwd(heads, biases[g], g, d, name=f"dil_attn_fwd_g{g}")
            outs.append(o)
            lses.append(lse)
        o_flat = _dil_mix_fwd(outs, lses, name="dil_mix_fwd")
        f = _mm(o_flat, dil_w['out'], name="dil_out")
        out = _post_fwd(f, xin, p['qg'], p['gate'], 1.0, name=f"post_fwd_{tag}")
        return out, dict(x=xin, hn=hn, heads=heads, outs=outs, lses=lses, o_flat=o_flat, f=f, p=p, tag=tag)

    saved = [None] * 6
    xs, saved[0] = ffn_fwd(x2, 0, 0, 0, tie=flight_a['token'][0, 0] + flight_b['token'][0, 0])
    xs, saved[1] = mla_fwd(xs, 0, 1)
    ffn_w[0, 1] = as_ffn(*arrive(flight_a, xs))
    passed = {}

    def second_step(after):
        passed['st'], tok = pass_on(flight_b, after)
        return tok

    xs, saved[2] = ffn_fwd(xs, 0, 1, 2, mid=second_step)
    got, last_fams = lax.optimization_barrier((arrive_passed(passed['st'], xs), later_fams[2]))
    ffn_w[1, 0] = as_ffn(got[0], got[1])
    dil_w['in'], dil_w['out'] = got[2].reshape(N_SHARD, D_MODEL, -1), got[3].reshape(D_MODEL, D_MODEL)
    in_flight = gather_later(last_fams, "l1s2")
    xs, saved[3] = ffn_fwd(xs, 1, 0, 0, tie=in_flight['token'][0, 0])
    xs, saved[4] = dil_fwd(xs, 1, 1)
    ffn_w[1, 1] = as_ffn(*arrive(in_flight, xs))
    xs, saved[5] = ffn_fwd(xs, 1, 1, 2)

    dx, loss_part = _loss(xs, target, name="loss")

    dmod = [[None] * 9 for _ in range(2)]
    dpre = [[None] * 3 for _ in range(2)]
    dpost = [[None] * 3 for _ in range(2)]
    ffn_units = {}
    row_unit = lambda g, r, j: ((r % 2, r // 2), 0, j)

    def close_sub(dhn, dout, sv, i, sub, res_dgate, res_dqg):
        p = sv['p']
        dxs, dsh, dsc, dpg = _pre_bwd(dhn, sv['x'], dout, p['pg'], p['sc'], name=f"pre_bwd_{sv['tag']}")
        dmod[i][3 * sub], dmod[i][3 * sub + 1], dmod[i][3 * sub + 2] = dsh, dsc, res_dgate
        dpre[i][sub], dpost[i][sub] = dpg, res_dqg
        return dxs

    def ffn_bwd(dout, sv, sub, tie=0.0, mid=None):
        i, h, p, tag = sv['i'], sv['h'], sv['p'], sv['tag']
        w_gu, w_dn = ffn_w[i, h]
        df, dgate, dqg = _post_bwd(dout, sv['f'], p['qg'] + tie, p['gate'], FFN_RES, name=f"post_bwd_{tag}")
        u_dn = _mm(sv['a'], df, ta=True, tn_cap=D_MODEL // 2, out_shape=(2, N_SHARD, F_SHARD, D_MODEL // 2),
                   out_sel=lambda g, r, j: ((j, g), r, 0), name=f"ffn_dwd_{tag}")
        dgu = _ffn_dgu(df, w_dn, sv['gu'], name=f"ffn_dgu_{tag}")
        if mid is not None:
            p = dict(p, pg=p['pg'] + mid(dgu))
        u_gu = _mm(dgu.reshape(2 * N_SHARD, SEQ, F_SHARD), sv['hn'], ta=True,
                   out_shape=(2, N_SHARD, F_SHARD, D_MODEL), out_sel=lambda g, r, j: ((g % 2, g // 2), r, j),
                   name=f"ffn_dwgu_{tag}")
        ffn_units[i, h] = [u_gu, u_dn]
        dxs, dsh, dsc, dpg = _ffn_dhn(dgu, w_gu, sv['x'], dout, p['pg'], p['sc'], name=f"ffn_dhn_{tag}")
        dmod[i][3 * sub], dmod[i][3 * sub + 1], dmod[i][3 * sub + 2] = dsh, dsc, dgate
        dpre[i][sub], dpost[i][sub] = dpg, dqg
        return dxs

    def mla_bwd(dout, sv, i, sub, tie=0.0):
        p, tag = sv['p'], sv['tag']
        df, dgate, dqg = _post_bwd(dout, sv['f'], p['qg'] + tie, p['gate'], 1.0, name=f"post_bwd_{tag}")
        u_wo = _mm(sv['o_flat'], df, ta=True, tm_cap=128, out_shape=(2, N_SHARD, 128, D_MODEL), out_sel=row_unit,
                   name="mla_dwo")
        do_flat = _mm(df, w_mo, tb=True, name="mla_do")
        do = _to_heads(do_flat, V_HEAD)
        dq, dk, dv = _mla_attn_bwd(sv['q'], sv['k'], sv['v'], sv['o'], do, sv['lse'], name="mla_attn_bwd")
        dq_t = dq.transpose(1, 0, 2)
        dqr1, dqr2 = _rope(dq_t[:, :, QK_NOPE:QK_NOPE + HALF_ROPE].reshape(SEQ, -1),
                           dq_t[:, :, QK_NOPE + HALF_ROPE:].reshape(SEQ, -1), cos_q, -sin_q, name="rope_q_bwd")
        dqp = jnp.concatenate([dq_t[:, :, :QK_NOPE].reshape(SEQ, -1), dqr1, dqr2], axis=1).astype(BF16)
        dkr = _head_sum(dk[:, :, QK_NOPE:], name="mla_dkr_sum")
        dk1, dk2 = _rope(dkr[:, :HALF_ROPE], dkr[:, HALF_ROPE:], cos_k, -sin_k, name="rope_k_bwd")
        dkvp = jnp.concatenate([_from_heads(dk[:, :, :QK_NOPE]), _from_heads(dv)], axis=1).astype(BF16)
        g_wq = _q_unperm(_mm(sv['cqn'], dqp, ta=True, name="mla_dwq"))
        g_wkv = _kv_unperm(_mm(sv['ckvn'], dkvp, ta=True, name="mla_dwkv"))
        dcqn = _mm(dqp, wq_p, tb=True, name="mla_dcqn")
        dckvn = _mm(dkvp, wkv_p, tb=True, name="mla_dckvn")
        dcq, g_qn = _rms_bwd(dcqn, sv['cq'], mla_q_norm, name="mla_qnorm_bwd")
        dckv, g_kvn = _rms_bwd(dckvn, sv['ckv'], mla_kv_norm, name="mla_kvnorm_bwd")
        dlat = jnp.concatenate([dcq, dckv, dk1, dk2], axis=1).astype(BF16)
        u_win = _mm(sv['hn'], dlat, ta=True, tm_cap=128, out_shape=(2, N_SHARD, 128, dlat.shape[1]),
                    out_sel=row_unit, name="mla_dwin")
        dhn = _mm(dlat, w_in, tb=True, name="mla_dhn")
        col_unit = lambda t: (t.reshape(t.shape[0], N_SHARD, -1).transpose(1, 0, 2)
                              .reshape(N_SHARD, 2, t.shape[0] // 2, -1).transpose(1, 0, 2, 3))
        grads = dict(units=[u_win, col_unit(g_wq), col_unit(g_wkv), u_wo], q_norm=g_qn, kv_norm=g_kvn)
        return close_sub(dhn, dout, sv, i, sub, dgate, dqg), grads

    def dil_bwd(dout, sv, i, sub):
        p, tag = sv['p'], sv['tag']
        df, dgate, dqg = _post_bwd(dout, sv['f'], p['qg'], p['gate'], 1.0, name=f"post_bwd_{tag}")
        u_wo = _mm(sv['o_flat'], df, ta=True, tm_cap=128, out_shape=(2, N_SHARD, 128, D_MODEL), out_sel=row_unit,
                   name="dil_dwo")
        dos, dlts = _dil_mix_bwd(_mm(df, dil_w['out'], tb=True, name="dil_do"), sv['outs'], sv['lses'],
                                 name="dil_mix_bwd")
        pieces = []
        bias_rows = []
        for g, (window, d) in enumerate(DIL_GROUPS):
            dq, dk, dv, dbias = _dil_attn_bwd(sv['heads'], biases[g], sv['lses'][g], dos[g], dlts[g], g, d,
                                              name=f"dil_attn_bwd_g{g}")
            pieces += [dq, dk, dv]
            bias_rows.append(_bias_grad(dbias, buckets[g], name=f"dil_bias_grad_g{g}")[:, 0, :])
        dheads = jnp.concatenate(pieces).astype(BF16)
        u_win = _proj_heads_dw(sv['hn'], dheads, name="dil_dwin")
        dhn = _proj_heads_dx(dheads, dil_w['in'], name="dil_dhn")
        g_bias = jnp.concatenate(bias_rows, axis=0).T
        grads = dict(units=[u_win, u_wo], rel_bias=g_bias)
        return close_sub(dhn, dout, sv, i, sub, dgate, dqg), grads

    def to_sibling(units, tag):
        n = len(units)
        send, recv, thru, token = _copies_start(units, [lax.empty(u.shape[1:], F32) for u in units], _sibling_plan, n,
                                                name=f"rs{tag}_sibling_start")
        return dict(send=send, recv=recv, thru=thru, n=n, tag=tag), token[0, 0]

    def from_sibling(st, after):
        n, tag = st['n'], st['tag']
        thru = _copies_wait(st['send'], st['recv'], st['thru'], n, _sibling_plan, after, name=f"rs{tag}_sibling_wait")
        return [_add_half(u, g, half_idx, name=f"rs{tag}_add_half_{k}") for k, (u, g) in enumerate(zip(thru[:n], thru[n:]))]

    def to_chips(parts, tag):
        n = len(parts)
        send, recv, thru, token = _copies_start([w for _, w in parts],
                                                [lax.empty((3,) + w.shape[1:], BF16) for _, w in parts], _chips_plan,
                                                3 * n, name=f"rs{tag}_chips_start")
        return dict(send=send, recv=recv, thru=thru, n=n, tag=tag, parts=parts), token[0, 0]

    def from_chips(st, after):
        n, tag = st['n'], st['tag']
        thru = _copies_wait(st['send'], st['recv'], st['thru'], n, _chips_plan, after, name=f"rs{tag}_chips_wait")
        return [_add_shards(p, g, shard_idx, name=f"rs{tag}_add_shards_{k}")
                for k, ((p, _), g) in enumerate(zip(st['parts'], thru[n:]))]

    dx = ffn_bwd(dx, saved[5], 2)
    dx, dil_g = dil_bwd(dx, saved[4], 1, 1)
    dx = ffn_bwd(dx, saved[3], 0)
    st1, tok = to_sibling([*ffn_units[1, 1], *dil_g['units'], *ffn_units[1, 0]], "1")
    dx = ffn_bwd(dx, saved[2], 2, tie=tok)
    st1, tok1 = to_chips(from_sibling(st1, dx), "1")
    st2, tok2 = to_sibling(ffn_units[0, 1], "2")
    dx, mla_g = mla_bwd(dx, saved[1], 0, 1, tie=tok1 + tok2)
    reds1 = from_chips(st1, dx)
    st2, tok = to_chips(from_sibling(st2, dx), "2")
    st3, tok3 = to_sibling(mla_g['units'], "3")
    onward = {}

    def mixer_to_chips(after):
        onward['st'], t = to_chips(from_sibling(st3, after), "3")
        return t

    dx = ffn_bwd(dx, saved[0], 0, tie=tok + tok3, mid=mixer_to_chips)
    reds2 = from_chips(st2, dx)
    reds3 = from_chips(onward['st'], dx)
    grad_x = dx[None]

    pad_row = lambda v: jnp.pad(v.reshape(-1), (0, (-v.size) % D_MODEL)).reshape(-1, D_MODEL)
    small = jnp.concatenate(
        [jnp.concatenate([dmod[i][r] for i in range(2) for r in range(9)], axis=0),
         jnp.concatenate([dpre[i][s] for i in range(2) for s in range(3)], axis=0),
         jnp.concatenate([dpost[i][s] for i in range(2) for s in range(3)], axis=0),
         pad_row(mla_g['q_norm']), pad_row(mla_g['kv_norm']), pad_row(dil_g['rel_bias']), pad_row(loss_part)], axis=0)
    small = jnp.pad(small, ((0, SMALL_ROWS - small.shape[0]), (0, 0)))
    small_all = _all_gather(small, name="ag_small_grads", in_vmem=True)
    small_sum = _sum_devices(small_all, 8, name="sum_small_grads")
    g_b_mod = small_sum[0:18].reshape(2, 9 * D_MODEL)
    my_cols = lambda t: lax.dynamic_slice_in_dim(t, shard_id * 256, 256, axis=2)
    g_norm_pre = my_cols(small_sum[18:24].reshape(2, 3, D_MODEL))
    g_norm_post = my_cols(small_sum[24:30].reshape(2, 3, D_MODEL))
    g_q_norm = small_sum[30, :Q_LORA].reshape(1, Q_LORA)
    g_kv_norm = small_sum[31, :KV_LORA].reshape(1, KV_LORA)
    g_rel_bias = small_sum[32:34].reshape(-1)[:N_BUCKETS * 48].reshape(N_BUCKETS, 48)
    loss = small_sum[34, 0]
    dmod_all = small_all.reshape(8, SMALL_ROWS, D_MODEL)[:, 0:18].reshape(8, 2, 9 * D_MODEL)
    dmod_cols = lax.dynamic_slice_in_dim(dmod_all, shard_id * 2304, 2304, axis=2).transpose(1, 0, 2)

    swap = lambda t: jnp.swapaxes(t, 2, 3)
    grads = dict(norm_pre=g_norm_pre, norm_post=g_norm_post, b_mod=g_b_mod, mla_q_norm=g_q_norm,
                 mla_kv_norm=g_kv_norm, rel_bias=g_rel_bias)
    deltas, new_m, new_v = {}, {}, {}

    def adamw(names):
        for n in names:
            view = swap if n in ('ffn_w_gate', 'ffn_w_up') else (lambda t: t)
            outs = _adamw(view(given[n]), view(grads[n]), view(given["m_" + n]), view(given["v_" + n]),
                          name=f"adamw_{n}")
            deltas[n], new_m[n], new_v[n] = (view(t) for t in outs)

    st0, tok = to_sibling(ffn_units[0, 0], "0")
    grads['w_mod'] = _mm(silu_c, (dmod_cols + tok).astype(BF16), ta=True, tn_cap=768, name="w_mod_grad")
    adamw(['w_mod'])
    st0, tok = to_chips(from_sibling(st0, deltas['w_mod']), "0")
    grads['b_mod'] = grads['b_mod'] + tok
    fin = _pair_gather(reds1 + reds2 + reds3, name="rs_pair_gather")
    for n, t in zip(['dil_w_in', 'dil_w_o', 'mla_w_in', 'mla_w_q_up', 'mla_w_kv_up', 'mla_w_o'], fin[2:4] + fin[8:12]):
        grads[n] = t.reshape(given[n].shape)
    adamw(['b_mod', 'dil_w_in', 'dil_w_o', 'mla_w_in', 'mla_w_q_up', 'mla_w_kv_up', 'mla_w_o', 'norm_pre', 'norm_post',
           'mla_q_norm', 'mla_kv_norm', 'rel_bias'])
    reds0 = from_chips(st0, deltas['dil_w_in'])
    fin0 = _pair_gather(reds0, name="rs_pair_gather_last")
    ffn_fin = {(1, 1): fin[0:2], (1, 0): fin[4:6], (0, 1): fin[6:8], (0, 0): fin0}
    per_ffn = lambda pick: jnp.stack([jnp.stack([pick(*ffn_fin[i, h]) for h in range(2)]) for i in range(2)])
    grads.update(ffn_w_gate=swap(per_ffn(lambda gu, dn: gu[0])), ffn_w_up=swap(per_ffn(lambda gu, dn: gu[1])),
                 ffn_w_down=per_ffn(lambda gu, dn: jnp.concatenate([dn[0], dn[1]], axis=1)))
    adamw(['ffn_w_gate', 'ffn_w_up', 'ffn_w_down'])
    return (loss, grad_x, *[grads[n] for n in WEIGHTS], *[deltas[n] for n in WEIGHTS],
            *[new_m[n] for n in WEIGHTS], *[new_v[n] for n in WEIGHTS])
```

```python
import math

import jax
import jax.numpy as jnp
from jax import lax
from jax.experimental import pallas as pl
from jax.experimental.pallas import tpu as pltpu

F32 = jnp.float32
BF16 = jnp.bfloat16
MESH = pl.DeviceIdType.MESH

SEQ = 2048
D_MODEL = 1024
D_FF = 2816
N_SHARD = 4
F_SHARD = D_FF // N_SHARD
EPS = 1e-6
FFN_RES = 0.5
HEADS = 16
Q_LORA, KV_LORA, QK_NOPE, QK_ROPE, V_HEAD = 384, 256, 64, 32, 64
HALF_ROPE = QK_ROPE // 2
ROPE_THETA = 10000.0
DIL_GROUPS = ((128, 1), (512, 4), (2048, 16))
DIL_BLOCK = 128
N_BUCKETS = 32
MAX_DISTANCE = 2048
ADAM_LR, ADAM_B1, ADAM_B2, ADAM_EPS, ADAM_WD, ADAM_STEP = 0.001, 0.9, 0.999, 1e-08, 0.01, 10

VMEM_LIMIT = 48 * 1024 * 1024
SMALL_ROWS = 40

WEIGHTS = ['norm_pre', 'norm_post', 'w_mod', 'b_mod', 'ffn_w_gate', 'ffn_w_up', 'ffn_w_down', 'mla_w_in',
           'mla_q_norm', 'mla_w_q_up', 'mla_kv_norm', 'mla_w_kv_up', 'mla_w_o', 'dil_w_in', 'dil_w_o', 'rel_bias']


def _cparams(**kw):
    return pltpu.CompilerParams(vmem_limit_bytes=VMEM_LIMIT, **kw)


def _pick(n, cap, mult=128):
    if n <= cap:
        return n
    best = n
    for t in range(mult, cap + 1, mult):
        if n % t == 0:
            best = t
    return best


def _mm(a, b, *, name, ta=False, tb=False, reduce_g=False, bias=None, out_dtype=F32, tm_cap=512, tn_cap=1024,
        g_n=None, b_sel=None, out_shape=None, out_sel=None, out_buf=None):
    a3 = a if a.ndim == 3 else a[None]
    ga = a3.shape[0]
    if b_sel is None:
        b_n = b if b.ndim == 3 else b[None]
        gb = b_n.shape[0]
        b_sel = (lambda g: (g,)) if gb > 1 else (lambda g: (0,))
        g_n = max(ga, gb)
    else:
        b_n = b
    k_dim, m_dim = (a3.shape[1], a3.shape[2]) if ta else (a3.shape[2], a3.shape[1])
    k2, n_dim = (b_n.shape[-1], b_n.shape[-2]) if tb else (b_n.shape[-2], b_n.shape[-1])
    assert k_dim == k2, (a.shape, b.shape)
    tm = _pick(m_dim, tm_cap, 128 if ta else 8)
    tn = _pick(n_dim, tn_cap, 128)
    mt, nt = m_dim // tm, n_dim // tn
    dims = (((0 if ta else 1,), (1 if tb else 0,)), ((), ()))

    if reduce_g:
        grid = (mt, nt, g_n)
        ids = lambda i, j, g: (g, i, j)
    else:
        grid = (g_n, mt, nt)
        ids = lambda g, i, j: (g, i, j)

    def a_map(*p):
        g, i, j = ids(*p)
        g = g if ga > 1 else 0
        return (g, 0, i) if ta else (g, i, 0)

    def b_map(*p):
        g, i, j = ids(*p)
        return (*b_sel(g), j, 0) if tb else (*b_sel(g), 0, j)

    b_lead = (None,) * (b_n.ndim - 2)
    a_spec = pl.BlockSpec((None, k_dim, tm) if ta else (None, tm, k_dim), a_map)
    b_spec = pl.BlockSpec(b_lead + ((tn, k_dim) if tb else (k_dim, tn)), b_map)
    in_specs = [a_spec, b_spec]
    operands = [a3, b_n]
    if bias is not None:
        assert not reduce_g and bias.shape == (g_n, 1, n_dim)
        in_specs.append(pl.BlockSpec((None, 1, tn), lambda g, i, j: (g, 0, j)))
        operands.append(bias)
    aliases = {}
    if out_buf is not None:
        assert tuple(out_buf.shape) == tuple(out_shape) and out_buf.dtype == out_dtype
        in_specs.append(pl.BlockSpec(memory_space=pl.ANY))
        operands.append(out_buf)
        aliases = {len(operands) - 1: 0}

    if reduce_g:
        out_spec = pl.BlockSpec((tm, tn), lambda i, j, g: (i, j))
        out_sds = jax.ShapeDtypeStruct((m_dim, n_dim), F32)
    elif out_shape is not None:
        def o_map(g, i, j):
            lead, rb, cb = out_sel(g, i, j)
            return (*lead, rb, cb)

        out_spec = pl.BlockSpec((None,) * (len(out_shape) - 2) + (tm, tn), o_map)
        out_sds = jax.ShapeDtypeStruct(tuple(out_shape), out_dtype)
    else:
        out_spec = pl.BlockSpec((None, tm, tn), lambda g, i, j: (g, i, j))
        out_sds = jax.ShapeDtypeStruct((g_n, m_dim, n_dim), out_dtype)

    def body(a_ref, b_ref, *rest):
        o_ref = rest[-1]
        r = lax.dot_general(a_ref[...].astype(BF16), b_ref[...].astype(BF16), dims, preferred_element_type=F32)
        if bias is not None:
            r = r + rest[0][...]
        if reduce_g:
            g = pl.program_id(2)

            @pl.when(g == 0)
            def _():
                o_ref[...] = r

            @pl.when(g > 0)
            def _():
                o_ref[...] += r
        else:
            o_ref[...] = r.astype(o_ref.dtype)

    out = pl.pallas_call(body, grid=grid, in_specs=in_specs, out_specs=out_spec, out_shape=out_sds,
                         input_output_aliases=aliases, compiler_params=_cparams(), name=name)(*operands)
    if not reduce_g and out_shape is None and a.ndim == 2 and b.ndim == 2:
        out = out[0]
    return out


def _rows(tm, w):
    return pl.BlockSpec((tm, w), lambda i: (i, 0))


def _vec(w):
    return pl.BlockSpec((1, w), lambda i: (0, 0))


def _rstd(v):
    return lax.rsqrt(jnp.mean(v * v, axis=-1, keepdims=True) + EPS)


def _pre_fwd(x, pg, sc, sh, *, name):
    s_n, w = x.shape
    tm = _pick(s_n, 512, 8)

    def body(x_ref, pg_ref, sc_ref, sh_ref, o_ref):
        xv = x_ref[...]
        n = (xv * _rstd(xv)) * pg_ref[...]
        o_ref[...] = (n * (1.0 + sc_ref[...]) + sh_ref[...]).astype(o_ref.dtype)

    return pl.pallas_call(body, grid=(s_n // tm,), in_specs=[_rows(tm, w), _vec(w), _vec(w), _vec(w)],
                          out_specs=_rows(tm, w), out_shape=jax.ShapeDtypeStruct((s_n, w), BF16),
                          compiler_params=_cparams(), name=name)(x, pg, sc, sh)


def _post_fwd(f, x, qg, gate, res_w, *, name):
    s_n, w = x.shape
    tm = _pick(s_n, 512, 8)

    def body(f_ref, x_ref, qg_ref, gate_ref, o_ref):
        fv = f_ref[...]
        y = (fv * _rstd(fv)) * qg_ref[...]
        o_ref[...] = x_ref[...] + (res_w * gate_ref[...]) * y

    return pl.pallas_call(body, grid=(s_n // tm,), in_specs=[_rows(tm, w), _rows(tm, w), _vec(w), _vec(w)],
                          out_specs=_rows(tm, w), out_shape=jax.ShapeDtypeStruct((s_n, w), F32),
                          compiler_params=_cparams(), name=name)(f, x, qg, gate)


def _post_bwd(dout, f, qg, gate, res_w, *, name):
    s_n, w = f.shape
    tm = _pick(s_n, 512, 8)

    def body(do_ref, f_ref, qg_ref, gate_ref, df_ref, dgate_ref, dqg_ref):
        @pl.when(pl.program_id(0) == 0)
        def _():
            dgate_ref[...] = jnp.zeros_like(dgate_ref)
            dqg_ref[...] = jnp.zeros_like(dqg_ref)

        do = do_ref[...]
        fv = f_ref[...]
        r = _rstd(fv)
        fh = fv * r
        qg_v = qg_ref[...]
        dgate_ref[...] += res_w * jnp.sum(do * (fh * qg_v), axis=0, keepdims=True)
        dy = do * (res_w * gate_ref[...])
        dqg_ref[...] += jnp.sum(dy * fh, axis=0, keepdims=True)
        dfh = dy * qg_v
        df = r * (dfh - fh * jnp.mean(dfh * fh, axis=-1, keepdims=True))
        df_ref[...] = df.astype(df_ref.dtype)

    return pl.pallas_call(
        body, grid=(s_n // tm,), in_specs=[_rows(tm, w), _rows(tm, w), _vec(w), _vec(w)],
        out_specs=[_rows(tm, w), _vec(w), _vec(w)],
        out_shape=[jax.ShapeDtypeStruct((s_n, w), BF16), jax.ShapeDtypeStruct((1, w), F32),
                   jax.ShapeDtypeStruct((1, w), F32)],
        compiler_params=_cparams(), name=name)(dout, f, qg, gate)


def _pre_bwd(dhn, x, dout, pg, sc, *, name):
    s_n, w = x.shape
    tm = _pick(s_n, 512, 8)

    def body(dhn_ref, x_ref, do_ref, pg_ref, sc_ref, dx_ref, dsh_ref, dsc_ref, dpg_ref):
        @pl.when(pl.program_id(0) == 0)
        def _():
            dsh_ref[...] = jnp.zeros_like(dsh_ref)
            dsc_ref[...] = jnp.zeros_like(dsc_ref)
            dpg_ref[...] = jnp.zeros_like(dpg_ref)

        dhn_v = dhn_ref[...]
        xv = x_ref[...]
        r = _rstd(xv)
        xh = xv * r
        pg_v = pg_ref[...]
        dsh_ref[...] += jnp.sum(dhn_v, axis=0, keepdims=True)
        dsc_ref[...] += jnp.sum(dhn_v * (xh * pg_v), axis=0, keepdims=True)
        dn = dhn_v * (1.0 + sc_ref[...])
        dpg_ref[...] += jnp.sum(dn * xh, axis=0, keepdims=True)
        dxh = dn * pg_v
        dx_ref[...] = do_ref[...] + r * (dxh - xh * jnp.mean(dxh * xh, axis=-1, keepdims=True))

    vec = jax.ShapeDtypeStruct((1, w), F32)
    return pl.pallas_call(
        body, grid=(s_n // tm,), in_specs=[_rows(tm, w), _rows(tm, w), _rows(tm, w), _vec(w), _vec(w)],
        out_specs=[_rows(tm, w), _vec(w), _vec(w), _vec(w)],
        out_shape=[jax.ShapeDtypeStruct((s_n, w), F32), vec, vec, vec],
        compiler_params=_cparams(), name=name)(dhn, x, dout, pg, sc)


def _rms_fwd(x, g, *, name):
    s_n, w = x.shape
    tm = _pick(s_n, 512, 8)

    def body(x_ref, g_ref, o_ref):
        xv = x_ref[...]
        o_ref[...] = ((xv * _rstd(xv)) * g_ref[...]).astype(o_ref.dtype)

    return pl.pallas_call(body, grid=(s_n // tm,), in_specs=[_rows(tm, w), _vec(w)], out_specs=_rows(tm, w),
                          out_shape=jax.ShapeDtypeStruct((s_n, w), BF16), compiler_params=_cparams(),
                          name=name)(x, g)


def _rms_bwd(dy, x, g, *, name):
    s_n, w = x.shape
    tm = _pick(s_n, 512, 8)

    def body(dy_ref, x_ref, g_ref, dx_ref, dg_ref):
        @pl.when(pl.program_id(0) == 0)
        def _():
            dg_ref[...] = jnp.zeros_like(dg_ref)

        dy_v = dy_ref[...]
        xv = x_ref[...]
        r = _rstd(xv)
        xh = xv * r
        dg_ref[...] += jnp.sum(dy_v * xh, axis=0, keepdims=True)
        dxh = dy_v * g_ref[...]
        dx_ref[...] = r * (dxh - xh * jnp.mean(dxh * xh, axis=-1, keepdims=True))

    return pl.pallas_call(
        body, grid=(s_n // tm,), in_specs=[_rows(tm, w), _rows(tm, w), _vec(w)],
        out_specs=[_rows(tm, w), _vec(w)],
        out_shape=[jax.ShapeDtypeStruct((s_n, w), F32), jax.ShapeDtypeStruct((1, w), F32)],
        compiler_params=_cparams(), name=name)(dy, x, g)


def _rope(a1, a2, cos, sin, *, name):
    s_n, w = a1.shape
    tm = _pick(s_n, 512, 8)

    def body(a1_ref, a2_ref, c_ref, s_ref, r1_ref, r2_ref):
        u, v, c_v, s_v = a1_ref[...], a2_ref[...], c_ref[...], s_ref[...]
        r1_ref[...] = u * c_v - v * s_v
        r2_ref[...] = u * s_v + v * c_v

    sd = jax.ShapeDtypeStruct((s_n, w), F32)
    return pl.pallas_call(body, grid=(s_n // tm,), in_specs=[_rows(tm, w)] * 4, out_specs=[_rows(tm, w)] * 2,
                          out_shape=[sd, sd], compiler_params=_cparams(), name=name)(a1, a2, cos, sin)


def _silu_bf16(x, *, name):
    def body(x_ref, o_ref):
        xv = x_ref[...]
        o_ref[...] = (xv * jax.nn.sigmoid(xv)).astype(o_ref.dtype)

    return pl.pallas_call(body, out_shape=jax.ShapeDtypeStruct(x.shape, BF16), name=name)(x)


def _loss(y, target, *, name):
    s_n, w = y.shape
    tm = _pick(s_n, 512, 8)

    def body(y_ref, t_ref, dy_ref, l_ref):
        @pl.when(pl.program_id(0) == 0)
        def _():
            l_ref[...] = jnp.zeros_like(l_ref)

        e = y_ref[...] - t_ref[...]
        dy_ref[...] = e * (1.0 / w)
        row = jnp.mean(e * e, axis=-1, keepdims=True)
        l_ref[...] += 0.5 * jnp.sum(row, axis=0, keepdims=True)

    return pl.pallas_call(
        body, grid=(s_n // tm,), in_specs=[_rows(tm, w), _rows(tm, w)],
        out_specs=[_rows(tm, w), pl.BlockSpec((1, 1), lambda i: (0, 0))],
        out_shape=[jax.ShapeDtypeStruct((s_n, w), F32), jax.ShapeDtypeStruct((1, 1), F32)],
        compiler_params=_cparams(), name=name)(y, target)


FFN_TM = 512


def _ffn_up(hn, w_gu, *, name):
    s_n, d = hn.shape
    f = w_gu.shape[-1]
    tm = _pick(s_n, FFN_TM, 8)

    def body(hn_ref, wg_ref, wu_ref, gu_ref, a_ref):
        xv = hn_ref[...]
        g = jnp.dot(xv, wg_ref[...], preferred_element_type=F32)
        u = jnp.dot(xv, wu_ref[...], preferred_element_type=F32)
        gu_ref[0] = g.astype(BF16)
        gu_ref[1] = u.astype(BF16)
        a_ref[...] = ((g * jax.nn.sigmoid(g)) * u).astype(BF16)

    w_blk = lambda t: pl.BlockSpec((None, None, d, f), lambda s, m: (s, t, 0, 0))
    return pl.pallas_call(
        body, grid=(N_SHARD, s_n // tm),
        in_specs=[pl.BlockSpec((tm, d), lambda s, m: (m, 0)), w_blk(0), w_blk(1)],
        out_specs=[pl.BlockSpec((None, 2, tm, f), lambda s, m: (s, 0, m, 0)),
                   pl.BlockSpec((None, tm, f), lambda s, m: (s, m, 0))],
        out_shape=[jax.ShapeDtypeStruct((N_SHARD, 2, s_n, f), BF16), jax.ShapeDtypeStruct((N_SHARD, s_n, f), BF16)],
        compiler_params=_cparams(), name=name)(hn, w_gu, w_gu)


def _ffn_down(a, w_dn, x, qg, gate, res_w, *, name):
    _, s_n, f = a.shape
    d = w_dn.shape[-1]
    tm = _pick(s_n, FFN_TM, 8)
    a = a.reshape(-1, 2, s_n, f)
    w_dn = w_dn.reshape(-1, 2, f, d)
    g_n = a.shape[0]

    def body(a_ref, w_ref, x_ref, qg_ref, gate_ref, f_ref, o_ref):
        g = pl.program_id(1)
        r = (jnp.dot(a_ref[0], w_ref[0], preferred_element_type=F32)
             + jnp.dot(a_ref[1], w_ref[1], preferred_element_type=F32))

        @pl.when(g == 0)
        def _():
            f_ref[...] = r

        @pl.when(g > 0)
        def _():
            f_ref[...] += r

        @pl.when(g == g_n - 1)
        def _():
            fv = f_ref[...]
            y = (fv * _rstd(fv)) * qg_ref[...]
            o_ref[...] = x_ref[...] + (res_w * gate_ref[...]) * y

    row = pl.BlockSpec((tm, d), lambda m, g: (m, 0))
    vec = pl.BlockSpec((1, d), lambda m, g: (0, 0))
    sd = jax.ShapeDtypeStruct((s_n, d), F32)
    return pl.pallas_call(
        body, grid=(s_n // tm, g_n),
        in_specs=[pl.BlockSpec((None, 2, tm, f), lambda m, g: (g, 0, m, 0)),
                  pl.BlockSpec((None, 2, f, d), lambda m, g: (g, 0, 0, 0)), row, vec, vec],
        out_specs=[row, row], out_shape=[sd, sd], compiler_params=_cparams(), name=name)(a, w_dn, x, qg, gate)


def _ffn_dhn(dgu, w_gu, x, dout, pg, sc, *, name):
    g_n, _, s_n, f = dgu.shape
    d = w_gu.shape[-2]
    tm = _pick(s_n, FFN_TM, 8)
    nt_dims = (((1,), (1,)), ((), ()))

    def body(a_ref, w_ref, x_ref, do_ref, pg_ref, sc_ref, dx_ref, dsh_ref, dsc_ref, dpg_ref, acc_ref):
        m, g = pl.program_id(0), pl.program_id(1)
        r = (lax.dot_general(a_ref[0], w_ref[0], nt_dims, preferred_element_type=F32)
             + lax.dot_general(a_ref[1], w_ref[1], nt_dims, preferred_element_type=F32))

        @pl.when(g == 0)
        def _():
            acc_ref[...] = r

        @pl.when(g > 0)
        def _():
            acc_ref[...] += r

        @pl.when((m == 0) & (g == 0))
        def _():
            dsh_ref[...] = jnp.zeros_like(dsh_ref)
            dsc_ref[...] = jnp.zeros_like(dsc_ref)
            dpg_ref[...] = jnp.zeros_like(dpg_ref)

        @pl.when(g == g_n - 1)
        def _():
            dhn_v = acc_ref[...]
            xv = x_ref[...]
            rs = _rstd(xv)
            xh = xv * rs
            pg_v = pg_ref[...]
            dsh_ref[...] += jnp.sum(dhn_v, axis=0, keepdims=True)
            dsc_ref[...] += jnp.sum(dhn_v * (xh * pg_v), axis=0, keepdims=True)
            dn = dhn_v * (1.0 + sc_ref[...])
            dpg_ref[...] += jnp.sum(dn * xh, axis=0, keepdims=True)
            dxh = dn * pg_v
            dx_ref[...] = do_ref[...] + rs * (dxh - xh * jnp.mean(dxh * xh, axis=-1, keepdims=True))

    row = pl.BlockSpec((tm, d), lambda m, g: (m, 0))
    vec = pl.BlockSpec((1, d), lambda m, g: (0, 0))
    vsd = jax.ShapeDtypeStruct((1, d), F32)
    return pl.pallas_call(
        body, grid=(s_n // tm, g_n),
        in_specs=[pl.BlockSpec((None, 2, tm, f), lambda m, g: (g, 0, m, 0)),
                  pl.BlockSpec((None, 2, d, f), lambda m, g: (g, 0, 0, 0)), row, row, vec, vec],
        out_specs=[row, vec, vec, vec], out_shape=[jax.ShapeDtypeStruct((s_n, d), F32), vsd, vsd, vsd],
        scratch_shapes=[pltpu.VMEM((tm, d), F32)], compiler_params=_cparams(), name=name)(dgu, w_gu, x, dout, pg, sc)


def _ffn_dgu(df, w_dn, gu, *, name):
    s_n, d = df.shape
    f = w_dn.shape[-2]
    tm = _pick(s_n, FFN_TM, 8)

    def body(df_ref, wd_ref, gu_ref, o_ref):
        da = lax.dot_general(df_ref[...], wd_ref[...], (((1,), (1,)), ((), ())), preferred_element_type=F32)
        g = gu_ref[0].astype(F32)
        u = gu_ref[1].astype(F32)
        sig = jax.nn.sigmoid(g)
        o_ref[0] = (da * u * (sig * (1.0 + g * (1.0 - sig)))).astype(BF16)
        o_ref[1] = (da * (g * sig)).astype(BF16)

    gu_blk = pl.BlockSpec((None, 2, tm, f), lambda s, m: (s, 0, m, 0))
    return pl.pallas_call(
        body, grid=(N_SHARD, s_n // tm),
        in_specs=[pl.BlockSpec((tm, d), lambda s, m: (m, 0)),
                  pl.BlockSpec((None, f, d), lambda s, m: (s, 0, 0)), gu_blk],
        out_specs=gu_blk, out_shape=jax.ShapeDtypeStruct((N_SHARD, 2, s_n, f), BF16),
        compiler_params=_cparams(), name=name)(df, w_dn, gu)


_NT = (((1,), (1,)), ((), ()))
_TN = (((0,), (0,)), ((), ()))
MLA_TQ = 256


def _causal_mask(i, tq, s_n):
    qpos = i * tq + lax.broadcasted_iota(jnp.int32, (tq, s_n), 0)
    kpos = lax.broadcasted_iota(jnp.int32, (tq, s_n), 1)
    return kpos <= qpos


def _mla_attn_fwd(q, k, v, *, name):
    h_n, s_n, dq = q.shape
    dv = v.shape[-1]
    tq = MLA_TQ
    scale = float(dq) ** -0.5

    def body(q_ref, k_ref, v_ref, o_ref, lse_ref):
        i = pl.program_id(1)
        for e in range(1, s_n // tq + 1):
            @pl.when(i == e - 1)
            def _(ext=e * tq):
                mask = _causal_mask(i, tq, ext)
                s = lax.dot_general(q_ref[...], k_ref[0:ext, :], _NT, preferred_element_type=F32) * scale
                s = jnp.where(mask, s, -jnp.inf)
                m = jnp.max(s, axis=-1, keepdims=True)
                p = jnp.exp(s - m)
                l = jnp.sum(p, axis=-1, keepdims=True)
                o = jnp.dot(p.astype(BF16), v_ref[0:ext, :], preferred_element_type=F32)
                o_ref[...] = o / l
                lse_ref[...] = m + jnp.log(l)

    return pl.pallas_call(
        body, grid=(h_n, s_n // tq),
        in_specs=[pl.BlockSpec((None, tq, dq), lambda h, i: (h, i, 0)),
                  pl.BlockSpec((None, s_n, dq), lambda h, i: (h, 0, 0)),
                  pl.BlockSpec((None, s_n, dv), lambda h, i: (h, 0, 0))],
        out_specs=[pl.BlockSpec((None, tq, dv), lambda h, i: (h, i, 0)),
                   pl.BlockSpec((None, tq, 1), lambda h, i: (h, i, 0))],
        out_shape=[jax.ShapeDtypeStruct((h_n, s_n, dv), F32), jax.ShapeDtypeStruct((h_n, s_n, 1), F32)],
        compiler_params=_cparams(), name=name)(q, k, v)


def _mla_attn_bwd(q, k, v, o, do, lse, *, name):
    h_n, s_n, dq = q.shape
    dv = v.shape[-1]
    tq = MLA_TQ
    scale = float(dq) ** -0.5

    def body(q_ref, k_ref, v_ref, o_ref, do_ref, lse_ref, dq_ref, dk_ref, dv_ref):
        i = pl.program_id(1)

        @pl.when(i == 0)
        def _():
            dk_ref[...] = jnp.zeros_like(dk_ref)
            dv_ref[...] = jnp.zeros_like(dv_ref)

        for e in range(1, s_n // tq + 1):
            @pl.when(i == e - 1)
            def _(ext=e * tq):
                mask = _causal_mask(i, tq, ext)
                qv, kv, vv = q_ref[...], k_ref[0:ext, :], v_ref[0:ext, :]
                do_v = do_ref[...]
                s = lax.dot_general(qv, kv, _NT, preferred_element_type=F32) * scale
                p = jnp.where(mask, jnp.exp(s - lse_ref[...]), 0.0)
                dob = do_v.astype(BF16)
                dv_ref[0:ext, :] += lax.dot_general(p.astype(BF16), dob, _TN, preferred_element_type=F32)
                dp = lax.dot_general(dob, vv, _NT, preferred_element_type=F32)
                delta = jnp.sum(do_v * o_ref[...], axis=-1, keepdims=True)
                dsb = (p * (dp - delta) * scale).astype(BF16)
                dq_ref[...] = jnp.dot(dsb, kv, preferred_element_type=F32)
                dk_ref[0:ext, :] += lax.dot_general(dsb, qv, _TN, preferred_element_type=F32)

    return pl.pallas_call(
        body, grid=(h_n, s_n // tq),
        in_specs=[pl.BlockSpec((None, tq, dq), lambda h, i: (h, i, 0)),
                  pl.BlockSpec((None, s_n, dq), lambda h, i: (h, 0, 0)),
                  pl.BlockSpec((None, s_n, dv), lambda h, i: (h, 0, 0)),
                  pl.BlockSpec((None, tq, dv), lambda h, i: (h, i, 0)),
                  pl.BlockSpec((None, tq, dv), lambda h, i: (h, i, 0)),
                  pl.BlockSpec((None, tq, 1), lambda h, i: (h, i, 0))],
        out_specs=[pl.BlockSpec((None, tq, dq), lambda h, i: (h, i, 0)),
                   pl.BlockSpec((None, s_n, dq), lambda h, i: (h, 0, 0)),
                   pl.BlockSpec((None, s_n, dv), lambda h, i: (h, 0, 0))],
        out_shape=[jax.ShapeDtypeStruct((h_n, s_n, dq), F32), jax.ShapeDtypeStruct((h_n, s_n, dq), F32),
                   jax.ShapeDtypeStruct((h_n, s_n, dv), F32)],
        compiler_params=_cparams(), name=name)(q, k, v, o, do, lse)


def _head_sum(x, *, name):
    h_n, s_n, w = x.shape
    tm = _pick(s_n, 512, 8)

    def body(x_ref, o_ref):
        o_ref[...] = jnp.sum(x_ref[...], axis=0)

    return pl.pallas_call(body, grid=(s_n // tm,), in_specs=[pl.BlockSpec((h_n, tm, w), lambda i: (0, i, 0))],
                          out_specs=_rows(tm, w), out_shape=jax.ShapeDtypeStruct((s_n, w), F32),
                          compiler_params=_cparams(), name=name)(x)


N_BLK = SEQ // DIL_BLOCK
DIL_SCALE = 64 ** -0.5


def _dil_masks():
    iq = lax.broadcasted_iota(jnp.int32, (DIL_BLOCK, 2 * DIL_BLOCK), 0)
    ik = lax.broadcasted_iota(jnp.int32, (DIL_BLOCK, 2 * DIL_BLOCK), 1)
    rel = DIL_BLOCK + iq - ik
    both = (rel >= 0) & (rel <= DIL_BLOCK)
    iq1 = lax.broadcasted_iota(jnp.int32, (DIL_BLOCK, DIL_BLOCK), 0)
    ik1 = lax.broadcasted_iota(jnp.int32, (DIL_BLOCK, DIL_BLOCK), 1)
    return both, ik1 <= iq1


def _dil_block(j, d):
    nb = SEQ // d // DIL_BLOCK
    r, n = divmod(j, nb)
    first = n == 0
    rows = lambda start, size: pl.ds(start, size) if d == 1 else pl.ds(start, size, stride=d)
    q_rows = rows(n * DIL_BLOCK * d + r, DIL_BLOCK)
    k_rows = q_rows if first else rows((n - 1) * DIL_BLOCK * d + r, 2 * DIL_BLOCK)
    return q_rows, k_rows, (DIL_BLOCK if first else 0), first


PAIR = 2 * 64
N_PAIR = HEADS // 2


def _dil_head_specs(s_n, g):
    return [pl.BlockSpec((None, s_n, PAIR), lambda hp, t=t: ((g * 3 + t) * N_PAIR + hp, 0, 0)) for t in range(3)]


def _pair_specs(s_n, w):
    return pl.BlockSpec((2, s_n, w), lambda hp: (hp, 0, 0))


_PAIR_BIAS = pl.BlockSpec((2, DIL_BLOCK, 2 * DIL_BLOCK), lambda hp: (hp, 0, 0))


def _dil_attn_fwd(heads, bias, g, d, *, name):
    _, s_n, _ = heads.shape
    e = PAIR // 2

    def body(q_ref, k_ref, v_ref, b_ref, o_ref, lse_ref):
        m_both, m_first = _dil_masks()
        for j in range(N_BLK):
            q_rows, k_rows, b_lo, first = _dil_block(j, d)
            q2 = q_ref[q_rows, :].astype(BF16)
            k2 = k_ref[k_rows, :].astype(BF16)
            v2 = v_ref[k_rows, :].astype(BF16)
            for hh in range(2):
                cols = slice(hh * e, (hh + 1) * e)
                s = (lax.dot_general(q2[:, cols], k2[:, cols], _NT, preferred_element_type=F32) * DIL_SCALE
                     + b_ref[hh, :, b_lo:])
                s = jnp.where(m_first if first else m_both, s, -jnp.inf)
                m = jnp.max(s, axis=-1, keepdims=True)
                lse = m + jnp.log(jnp.sum(jnp.exp(s - m), axis=-1, keepdims=True))
                p = jnp.exp(s - lse)
                o_ref[hh, q_rows, :] = jnp.dot(p.astype(BF16), v2[:, cols], preferred_element_type=F32)
                lse_ref[hh, q_rows, :] = lse

    return pl.pallas_call(
        body, grid=(N_PAIR,), in_specs=_dil_head_specs(s_n, g) + [_PAIR_BIAS],
        out_specs=[_pair_specs(s_n, e), _pair_specs(s_n, 1)],
        out_shape=[jax.ShapeDtypeStruct((HEADS, s_n, e), F32), jax.ShapeDtypeStruct((HEADS, s_n, 1), F32)],
        compiler_params=_cparams(), name=name)(heads, heads, heads, bias)


def _dil_attn_bwd(heads, bias, lse, do, dlt, g, d, *, name):
    _, s_n, _ = heads.shape
    e = PAIR // 2

    def body(q_ref, k_ref, v_ref, b_ref, lse_ref, do_ref, dlt_ref, dq_ref, dk_ref, dv_ref, db_ref):
        db_ref[...] = jnp.zeros_like(db_ref)
        m_both, m_first = _dil_masks()
        nb = s_n // d // DIL_BLOCK
        own_v = own_k = own_rows = None
        for j in range(N_BLK):
            q_rows, k_rows, b_lo, first = _dil_block(j, d)
            q2 = q_ref[q_rows, :].astype(BF16)
            k2 = k_ref[k_rows, :].astype(BF16)
            v2 = v_ref[k_rows, :].astype(BF16)
            dq_h, dv_h, dk_h = [], [], []
            for hh in range(2):
                cols = slice(hh * e, (hh + 1) * e)
                qj, kk, vv = q2[:, cols], k2[:, cols], v2[:, cols]
                s = lax.dot_general(qj, kk, _NT, preferred_element_type=F32) * DIL_SCALE + b_ref[hh, :, b_lo:]
                p = jnp.where(m_first if first else m_both, jnp.exp(s - lse_ref[hh, q_rows, :]), 0.0)
                dob = do_ref[hh, q_rows, :].astype(BF16)
                dv_h.append(lax.dot_general(p.astype(BF16), dob, _TN, preferred_element_type=F32))
                dp = lax.dot_general(dob, vv, _NT, preferred_element_type=F32)
                ds = p * (dp - dlt_ref[hh, q_rows, :])
                db_ref[hh, :, b_lo:] += ds
                dsb = (ds * DIL_SCALE).astype(BF16)
                dq_h.append(jnp.dot(dsb, kk, preferred_element_type=F32))
                dk_h.append(lax.dot_general(dsb, qj, _TN, preferred_element_type=F32))
            dq_ref[q_rows, :] = jnp.concatenate(dq_h, axis=1)
            dvv, dkk = jnp.concatenate(dv_h, axis=1), jnp.concatenate(dk_h, axis=1)
            if not first:
                dv_ref[own_rows, :] = own_v + dvv[:DIL_BLOCK]
                dk_ref[own_rows, :] = own_k + dkk[:DIL_BLOCK]
                dvv, dkk = dvv[DIL_BLOCK:], dkk[DIL_BLOCK:]
            own_v, own_k, own_rows = dvv, dkk, q_rows
            if j % nb == nb - 1:
                dv_ref[own_rows, :] = own_v
                dk_ref[own_rows, :] = own_k

    slab = pl.BlockSpec((None, s_n, PAIR), lambda hp: (hp, 0, 0))
    sd = jax.ShapeDtypeStruct((N_PAIR, s_n, PAIR), F32)
    return pl.pallas_call(
        body, grid=(N_PAIR,),
        in_specs=_dil_head_specs(s_n, g) + [_PAIR_BIAS, _pair_specs(s_n, 1), _pair_specs(s_n, e), _pair_specs(s_n, 1)],
        out_specs=[slab, slab, slab, _PAIR_BIAS],
        out_shape=[sd, sd, sd, jax.ShapeDtypeStruct((HEADS, DIL_BLOCK, 2 * DIL_BLOCK), F32)],
        compiler_params=_cparams(), name=name)(heads, heads, heads, bias, lse, do, dlt)


def _proj_heads(x, w, *, name):
    s_n, k = x.shape
    n = w.shape[-1]
    tm, tn, e = 512, 768, PAIR
    per_blk, n_blk = tn // e, n // tn

    def body(x_ref, w_ref, o_ref):
        r = jnp.dot(x_ref[...], w_ref[...], preferred_element_type=F32)
        for j in range(per_blk):
            o_ref[j] = r[:, e * j:e * (j + 1)]

    return pl.pallas_call(
        body, grid=(w.shape[0], n_blk, s_n // tm),
        in_specs=[pl.BlockSpec((tm, k), lambda s, b, m: (m, 0)), pl.BlockSpec((None, k, tn), lambda s, b, m: (s, 0, b))],
        out_specs=pl.BlockSpec((per_blk, tm, e), lambda s, b, m: (s * n_blk + b, m, 0)),
        out_shape=jax.ShapeDtypeStruct((w.shape[0] * n // e, s_n, e), F32), compiler_params=_cparams(),
        name=name)(x, w)


def _heads_cat(d_ref):
    return jnp.concatenate([d_ref[j] for j in range(d_ref.shape[0])], axis=1)


def _proj_heads_dw(x, dh, *, name):
    s_n, k = x.shape
    tn, e = 768, PAIR
    per_blk = tn // e
    n_blk = dh.shape[0] // N_SHARD // per_blk
    n = n_blk * tn

    def body(x_ref, d_ref, o_ref):
        o_ref[...] = lax.dot_general(x_ref[...], _heads_cat(d_ref), _TN, preferred_element_type=F32)

    return pl.pallas_call(
        body, grid=(N_SHARD, n_blk, 2),
        in_specs=[pl.BlockSpec((s_n, k // 2), lambda s, b, r: (0, r)),
                  pl.BlockSpec((per_blk, s_n, e), lambda s, b, r: (s * n_blk + b, 0, 0))],
        out_specs=pl.BlockSpec((None, None, k // 2, tn), lambda s, b, r: (r, s, 0, b)),
        out_shape=jax.ShapeDtypeStruct((2, N_SHARD, k // 2, n), F32), compiler_params=_cparams(), name=name)(x, dh)


def _proj_heads_dx(dh, w, *, name):
    k, n = w.shape[1:]
    s_n = dh.shape[1]
    tm, tn, e = 512, 1152, PAIR
    per_blk, n_blk = tn // e, n // tn

    def body(d_ref, w_ref, o_ref):
        r = lax.dot_general(_heads_cat(d_ref), w_ref[...], _NT, preferred_element_type=F32)
        g = pl.program_id(1)

        @pl.when(g == 0)
        def _():
            o_ref[...] = r

        @pl.when(g > 0)
        def _():
            o_ref[...] += r

    return pl.pallas_call(
        body, grid=(s_n // tm, N_SHARD * n_blk),
        in_specs=[pl.BlockSpec((per_blk, tm, e), lambda m, g: (g, m, 0)),
                  pl.BlockSpec((None, k, tn), lambda m, g: (g // n_blk, 0, g % n_blk))],
        out_specs=pl.BlockSpec((tm, k), lambda m, g: (m, 0)),
        out_shape=jax.ShapeDtypeStruct((s_n, k), F32), compiler_params=_cparams(), name=name)(dh, w)


def _group_alpha(ls):
    m = jnp.maximum(jnp.maximum(ls[0], ls[1]), ls[2])
    es = [jnp.exp(l - m) for l in ls]
    tot = es[0] + es[1] + es[2]
    return [ex / tot for ex in es]


def _dil_mix_fwd(os_, ls_, *, name):
    h_n, s_n, e = os_[0].shape
    tm = 512

    def body(o0, o1, o2, l0, l1, l2, out_ref):
        for hh in range(2):
            al = _group_alpha([l[hh] for l in (l0, l1, l2)])
            mix = al[0] * o0[hh] + al[1] * o1[hh] + al[2] * o2[hh]
            out_ref[:, hh * e:(hh + 1) * e] = mix.astype(out_ref.dtype)

    blk = lambda w: pl.BlockSpec((2, tm, w), lambda h, i: (h, i, 0))
    return pl.pallas_call(body, grid=(h_n // 2, s_n // tm), in_specs=[blk(e)] * 3 + [blk(1)] * 3,
                          out_specs=pl.BlockSpec((tm, 2 * e), lambda h, i: (i, h)),
                          out_shape=jax.ShapeDtypeStruct((s_n, h_n * e), BF16), compiler_params=_cparams(),
                          name=name)(*os_, *ls_)


def _dil_mix_bwd(do_flat, os_, ls_, *, name):
    h_n, s_n, e = os_[0].shape
    tm = 512

    def body(do_ref, o0, o1, o2, l0, l1, l2, d0, d1, d2, t0, t1, t2):
        for hh in range(2):
            al = _group_alpha([l[hh] for l in (l0, l1, l2)])
            do_v = do_ref[:, hh * e:(hh + 1) * e]
            mix = al[0] * o0[hh] + al[1] * o1[hh] + al[2] * o2[hh]
            dbar = jnp.sum(do_v * mix, axis=-1, keepdims=True)
            for a_g, d_ref, t_ref in zip(al, (d0, d1, d2), (t0, t1, t2)):
                d_ref[hh] = a_g * do_v
                t_ref[hh] = a_g * dbar

    blk = lambda w: pl.BlockSpec((2, tm, w), lambda h, i: (h, i, 0))
    sd_e = jax.ShapeDtypeStruct((h_n, s_n, e), F32)
    sd_1 = jax.ShapeDtypeStruct((h_n, s_n, 1), F32)
    outs = pl.pallas_call(body, grid=(h_n // 2, s_n // tm),
                          in_specs=[pl.BlockSpec((tm, 2 * e), lambda h, i: (i, h))] + [blk(e)] * 3 + [blk(1)] * 3,
                          out_specs=[blk(e)] * 3 + [blk(1)] * 3, out_shape=[sd_e] * 3 + [sd_1] * 3,
                          compiler_params=_cparams(), name=name)(do_flat, *os_, *ls_)
    return outs[:3], outs[3:]


def _bias_grad(ds, bucket, *, name):
    h_n = ds.shape[0]

    def body(ds_ref, bk_ref, o_ref):
        ds_v = ds_ref[...]
        bk = bk_ref[...]
        lane = lax.broadcasted_iota(jnp.int32, (1, N_BUCKETS), 1)
        acc = jnp.zeros((1, N_BUCKETS), F32)
        for b in range(N_BUCKETS):
            tot = jnp.sum(jnp.sum(jnp.where(bk == b, ds_v, 0.0), axis=1, keepdims=True), axis=0, keepdims=True)
            acc = acc + jnp.where(lane == b, tot, 0.0)
        o_ref[...] = acc

    return pl.pallas_call(
        body, grid=(h_n,),
        in_specs=[pl.BlockSpec((None, DIL_BLOCK, 2 * DIL_BLOCK), lambda h: (h, 0, 0)),
                  pl.BlockSpec((DIL_BLOCK, 2 * DIL_BLOCK), lambda h: (0, 0))],
        out_specs=pl.BlockSpec((None, 1, N_BUCKETS), lambda h: (h, 0, 0)),
        out_shape=jax.ShapeDtypeStruct((h_n, 1, N_BUCKETS), F32), compiler_params=_cparams(), name=name)(ds, bucket)


def _bias_table(rb, bucket, *, name):
    h_n = rb.shape[0]

    def body(rb_ref, bk_ref, o_ref):
        bk = bk_ref[...]
        row = rb_ref[...]
        acc = jnp.zeros(bk.shape, F32)
        for b in range(N_BUCKETS):
            acc = jnp.where(bk == b, row[:, b:b + 1], acc)
        o_ref[...] = acc

    return pl.pallas_call(
        body, grid=(h_n,),
        in_specs=[pl.BlockSpec((None, 1, N_BUCKETS), lambda h: (h, 0, 0)),
                  pl.BlockSpec((DIL_BLOCK, 2 * DIL_BLOCK), lambda h: (0, 0))],
        out_specs=pl.BlockSpec((None, DIL_BLOCK, 2 * DIL_BLOCK), lambda h: (h, 0, 0)),
        out_shape=jax.ShapeDtypeStruct((h_n, DIL_BLOCK, 2 * DIL_BLOCK), F32), compiler_params=_cparams(),
        name=name)(rb, bucket)


def _row_tile(rows, cols, budget=2 << 20):
    if rows * cols * 4 <= budget or rows % 8:
        return rows
    best = 8
    for t in range(8, rows + 1, 8):
        if rows % t == 0 and t * cols * 4 <= budget:
            best = t
    return best


def _adamw(w, g, m, v, *, name):
    shape = w.shape
    cols = shape[-1]
    rows = math.prod(shape[:-1]) if len(shape) > 1 else 1
    to2 = lambda t: t.reshape(rows, cols)
    tr = _row_tile(rows, cols)
    c1 = 1.0 / (1.0 - ADAM_B1 ** ADAM_STEP)
    c2 = 1.0 / (1.0 - ADAM_B2 ** ADAM_STEP)

    def body(w_ref, g_ref, m_ref, v_ref, d_ref, nm_ref, nv_ref):
        g_v = g_ref[...]
        nm = ADAM_B1 * m_ref[...] + (1.0 - ADAM_B1) * g_v
        nv = ADAM_B2 * v_ref[...] + (1.0 - ADAM_B2) * (g_v * g_v)
        m_hat = nm * c1
        v_hat = nv * c2
        d_ref[...] = -ADAM_LR * (m_hat / (jnp.sqrt(v_hat) + ADAM_EPS) + ADAM_WD * w_ref[...])
        nm_ref[...] = nm
        nv_ref[...] = nv

    blk = pl.BlockSpec((tr, cols), lambda i: (i, 0))
    sd = jax.ShapeDtypeStruct((rows, cols), F32)
    outs = pl.pallas_call(body, grid=(rows // tr,), in_specs=[blk] * 4, out_specs=[blk] * 3, out_shape=[sd] * 3,
                          compiler_params=_cparams(), name=name)(to2(w), to2(g), to2(m), to2(v))
    return tuple(t.reshape(shape) for t in outs)


def _add_half(unit, got, half_idx, *, name):
    rest = unit.shape[2:]
    c = rest[-1]
    r = math.prod(rest[:-1])
    tr = _row_tile(r, c)

    def body(idx_ref, u_ref, g_ref, o_ref, w_ref):
        tot = u_ref[...] + g_ref[...].astype(F32)
        o_ref[...] = tot
        w_ref[...] = tot.astype(BF16)

    blk = pl.BlockSpec((None, tr, c), lambda s, i, idx: (s, i, 0))
    grid_spec = pltpu.PrefetchScalarGridSpec(
        num_scalar_prefetch=1, grid=(N_SHARD, r // tr),
        in_specs=[pl.BlockSpec((None, None, tr, c), lambda s, i, idx: (idx[0], s, i, 0)), blk],
        out_specs=[blk, blk])
    out, wire = pl.pallas_call(
        body, grid_spec=grid_spec,
        out_shape=[jax.ShapeDtypeStruct((N_SHARD, r, c), F32), jax.ShapeDtypeStruct((N_SHARD, r, c), BF16)],
        compiler_params=_cparams(), name=name)(half_idx, unit.reshape(2, N_SHARD, r, c), got.reshape(N_SHARD, r, c))
    return out.reshape((N_SHARD,) + rest), wire.reshape((N_SHARD,) + rest)


def _add_shards(part, got, shard_idx, *, name):
    rest = part.shape[1:]
    c = rest[-1]
    r = math.prod(rest[:-1])
    tr = _row_tile(r, c)

    def body(idx_ref, p_ref, g_ref, o_ref):
        acc = p_ref[...]
        for k in range(3):
            acc = acc + g_ref[k].astype(F32)
        o_ref[...] = acc

    grid_spec = pltpu.PrefetchScalarGridSpec(
        num_scalar_prefetch=1, grid=(r // tr,),
        in_specs=[pl.BlockSpec((None, tr, c), lambda i, idx: (idx[0], i, 0)),
                  pl.BlockSpec((3, tr, c), lambda i, idx: (0, i, 0))],
        out_specs=pl.BlockSpec((tr, c), lambda i, idx: (i, 0)))
    out = pl.pallas_call(body, grid_spec=grid_spec, out_shape=jax.ShapeDtypeStruct((r, c), F32),
                         compiler_params=_cparams(), name=name)(
        shard_idx, part.reshape(N_SHARD, r, c), got.reshape(3, r, c))
    return out.reshape(rest)


def _sum_devices(x, n_dev, *, name):
    rows = x.shape[0] // n_dev

    def body(x_ref, o_ref):
        acc = x_ref[0:rows, :]
        for d in range(1, n_dev):
            acc = acc + x_ref[d * rows:(d + 1) * rows, :]
        o_ref[...] = acc

    return pl.pallas_call(body, out_shape=jax.ShapeDtypeStruct((rows, x.shape[1]), F32), name=name)(x)


def _my_pos():
    return lax.axis_index("x"), lax.axis_index("y"), lax.axis_index("c")


def _all_gather(x_blk, *, name, in_vmem):
    m_per, n = x_blk.shape

    def body(x_ref, out_ref, send_sems, recv_sems, local_sem):
        x, y, c = _my_pos()
        me, sibling = (x, y, c), (x, y, 1 - c)
        chips = [(1 - x, y), (x, 1 - y), (1 - x, 1 - y)]

        def rows(px, py, pc):
            return out_ref.at[pl.ds((4 * px + 2 * py + pc) * m_per, m_per), :]

        def copy(k, block, to, src=None):
            return pltpu.make_async_remote_copy(
                src_ref=rows(*block) if src is None else src, dst_ref=rows(*block),
                send_sem=send_sems.at[k], recv_sem=recv_sems.at[k], device_id=to, device_id_type=MESH)

        mine = pltpu.make_async_copy(x_ref, rows(*me), local_sem)
        mine.start()
        first = [copy(0, me, sibling, src=x_ref)]
        first += [copy(1 + j, me, (*chip, c), src=x_ref) for j, chip in enumerate(chips)]
        for cp in first:
            cp.start()
        passed = [copy(4 + j, (*chip, c), sibling) for j, chip in enumerate(chips)]
        for j, chip in enumerate(chips):
            copy(1 + j, (*chip, c), me).wait_recv()
            passed[j].start()
        copy(0, sibling, me).wait_recv()
        for j, chip in enumerate(chips):
            copy(4 + j, (*chip, 1 - c), me).wait_recv()
        for cp in first + passed:
            cp.wait_send()
        mine.wait()

    space = pltpu.VMEM if in_vmem else pl.ANY
    return pl.pallas_call(
        body, out_shape=jax.ShapeDtypeStruct((8 * m_per, n), x_blk.dtype),
        in_specs=[pl.BlockSpec(memory_space=space)], out_specs=pl.BlockSpec(memory_space=space),
        scratch_shapes=[pltpu.SemaphoreType.DMA((7,)), pltpu.SemaphoreType.DMA((7,)), pltpu.SemaphoreType.DMA],
        name=name)(x_blk)


_HBM = pl.BlockSpec(memory_space=pl.ANY)


def _gather_weights(fams, *, name):
    n = len(fams)

    def body(*refs):
        ins, outs = refs[:n], refs[n:2 * n]
        send_sems, recv_sems = refs[2 * n:]
        x, y, c = _my_pos()
        me, sibling = (x, y, c), (x, y, 1 - c)
        chips = [(1 - x, y), (x, 1 - y), (1 - x, 1 - y)]

        def copy(f, k, block, to, src=None):
            px, py, pc = block
            dst = outs[f].at[2 * px + py, pc]
            return pltpu.make_async_remote_copy(
                src_ref=dst if src is None else src, dst_ref=dst, send_sem=send_sems.at[7 * f + k],
                recv_sem=recv_sems.at[7 * f + k], device_id=to, device_id_type=MESH)

        first, passed = [], []
        for f in range(n):
            src = ins[f].at[c]
            first.append(copy(f, 0, me, sibling, src=src))
            first += [copy(f, 1 + j, me, (*chip, c), src=src) for j, chip in enumerate(chips)]
        for cp in first:
            cp.start()
        for j, chip in enumerate(chips):
            for f in range(n):
                copy(f, 1 + j, (*chip, c), me).wait_recv()
                passed.append(copy(f, 4 + j, (*chip, c), sibling))
                passed[-1].start()
        for f in range(n):
            copy(f, 0, sibling, me).wait_recv()
        for j, chip in enumerate(chips):
            for f in range(n):
                copy(f, 4 + j, (*chip, 1 - c), me).wait_recv()
        for cp in first + passed:
            cp.wait_send()

    outs = pl.pallas_call(
        body, out_shape=[jax.ShapeDtypeStruct((N_SHARD,) + t.shape, t.dtype) for t in fams],
        in_specs=[_HBM] * n, out_specs=[_HBM] * n,
        scratch_shapes=[pltpu.SemaphoreType.DMA((7 * n,)), pltpu.SemaphoreType.DMA((7 * n,))], name=name)(*fams)
    return [_place_own(o, t) for o, t in zip(outs, fams)]


def _pair_gather(halves, *, name):
    n = len(halves)

    def body(*refs):
        ins, outs = refs[:n], refs[n:2 * n]
        send_sems, recv_sems = refs[2 * n:]
        x, y, c = _my_pos()
        cps = [pltpu.make_async_remote_copy(src_ref=ins[f], dst_ref=outs[f].at[c], send_sem=send_sems.at[f],
                                            recv_sem=recv_sems.at[f], device_id=(x, y, 1 - c), device_id_type=MESH)
               for f in range(n)]
        for cp in cps:
            cp.start()
        for f in range(n):
            pltpu.make_async_remote_copy(src_ref=ins[f], dst_ref=outs[f].at[1 - c], send_sem=send_sems.at[f],
                                         recv_sem=recv_sems.at[f], device_id=(x, y, 1 - c),
                                         device_id_type=MESH).wait_recv()
        for cp in cps:
            cp.wait_send()

    outs = pl.pallas_call(
        body, out_shape=[jax.ShapeDtypeStruct((2,) + t.shape, t.dtype) for t in halves],
        in_specs=[_HBM] * n, out_specs=[_HBM] * n,
        scratch_shapes=[pltpu.SemaphoreType.DMA((n,)), pltpu.SemaphoreType.DMA((n,))], name=name)(*halves)
    c = lax.axis_index("c")
    return [lax.dynamic_update_index_in_dim(o, t, c, 0) for o, t in zip(outs, halves)]


_HBM_ONLY = pl.BlockSpec(memory_space=pltpu.HBM)
_SEMS = pl.BlockSpec(memory_space=pltpu.SEMAPHORE)
_EFFECT = pltpu.SideEffectType.DATAFLOW_SIDE_EFFECTING


def _copies_start(srcs, lands, plan, n_copies, *, name):
    n, m = len(srcs), len(lands)

    def body(*refs):
        src_refs, land_refs = refs[:n], refs[n:n + m]
        send_sems, recv_sems, token = refs[n + m], refs[n + m + 1], refs[-1]
        for k, (src, dst, peer) in enumerate(plan(src_refs, land_refs)):
            pltpu.make_async_remote_copy(src_ref=src, dst_ref=dst, send_sem=send_sems.at[k], recv_sem=recv_sems.at[k],
                                         device_id=peer, device_id_type=MESH).start()
        token[...] = jnp.zeros_like(token)

    bufs = [pltpu.with_memory_space_constraint(t, pltpu.HBM) for t in (*srcs, *lands)]
    outs = pl.pallas_call(
        body, name=name,
        out_shape=(pltpu.SemaphoreType.DMA((n_copies,)), pltpu.SemaphoreType.DMA((n_copies,)),
                   *[pltpu.HBM(t.shape, t.dtype) for t in bufs], jax.ShapeDtypeStruct((8, 128), F32)),
        in_specs=[_HBM_ONLY] * (n + m),
        out_specs=(_SEMS, _SEMS, *[_HBM_ONLY] * (n + m), pl.BlockSpec(memory_space=pltpu.VMEM)),
        input_output_aliases={k: 2 + k for k in range(n + m)},
        compiler_params=pltpu.CompilerParams(has_side_effects=_EFFECT))(*bufs)
    return outs[0], outs[1], list(outs[2:2 + n + m]), outs[-1]


def _copies_wait(send_sems, recv_sems, thru, n_src, plan, after, *, name):
    nm = len(thru)

    def body(*refs):
        t_refs, send, recv = refs[:nm], refs[nm], refs[nm + 1]
        for k, (src, dst, peer) in enumerate(plan(t_refs[:n_src], t_refs[n_src:])):
            cp = pltpu.make_async_remote_copy(src_ref=src, dst_ref=dst, send_sem=send.at[k], recv_sem=recv.at[k],
                                              device_id=peer, device_id_type=MESH)
            cp.wait_send()
            cp.wait_recv()

    outs = pl.pallas_call(
        body, name=name, out_shape=tuple(pltpu.HBM(t.shape, t.dtype) for t in thru),
        in_specs=[_HBM_ONLY] * nm + [_SEMS, _SEMS, pl.BlockSpec(memory_space=pl.ANY)],
        out_specs=tuple([_HBM_ONLY] * nm), input_output_aliases={k: k for k in range(nm)},
        compiler_params=pltpu.CompilerParams(has_side_effects=_EFFECT))(*thru, send_sems, recv_sems, after)
    return list(outs)


_RELATIONS = [(dx, dy, dc) for dx in (0, 1) for dy in (0, 1) for dc in (0, 1)][1:]


def _gather_plan(src_refs, land_refs):
    x, y, c = _my_pos()
    flip = lambda v, d: 1 - v if d else v
    return [(s_ref.at[c], l_ref.at[2 * x + y, c], (flip(x, dx), flip(y, dy), flip(c, dc)))
            for s_ref, l_ref in zip(src_refs, land_refs) for dx, dy, dc in _RELATIONS]


def _gather_chips_plan(src_refs, land_refs):
    x, y, c = _my_pos()
    peers = [(x, y, 1 - c), (1 - x, y, c), (x, 1 - y, c), (1 - x, 1 - y, c)]
    return [(s_ref.at[c], l_ref.at[2 * x + y, c], peer) for s_ref, l_ref in zip(src_refs, land_refs) for peer in peers]


def _gather_pass_plan(src_refs, land_refs):
    x, y, c = _my_pos()
    chips = [(1 - x, y), (x, 1 - y), (1 - x, 1 - y)]
    return [(l_ref.at[2 * cx + cy, c], l_ref.at[2 * cx + cy, c], (x, y, 1 - c))
            for l_ref in land_refs for cx, cy in chips]


def _sibling_plan(src_refs, land_refs):
    x, y, c = _my_pos()
    return [(s_ref.at[1 - c], l_ref, (x, y, 1 - c)) for s_ref, l_ref in zip(src_refs, land_refs)]


def _chips_plan(src_refs, land_refs):
    x, y, c = _my_pos()
    chips = [(1 - x, y), (x, 1 - y), (1 - x, 1 - y)]
    return [(s_ref.at[2 * cx + cy], l_ref.at[k], (cx, cy, c))
            for s_ref, l_ref in zip(src_refs, land_refs) for k, (cx, cy) in enumerate(chips)]


def _place_own(gathered, fam):
    x, y, c = _my_pos()
    own = lax.dynamic_index_in_dim(fam, c, 0, keepdims=True)[None]
    return lax.dynamic_update_slice(gathered, own, (2 * x + y, c) + (0,) * (fam.ndim - 1))


def _to_heads(t, width):
    return t.reshape(t.shape[0], HEADS, width).transpose(1, 0, 2)


def _from_heads(t):
    return t.transpose(1, 0, 2).reshape(t.shape[1], -1)


def _t5_bucket(dist):
    max_exact = N_BUCKETS // 2
    d = jnp.maximum(dist, 1).astype(F32)
    large = max_exact + (jnp.log(d / max_exact) / math.log(MAX_DISTANCE / max_exact)
                         * (N_BUCKETS - max_exact)).astype(jnp.int32)
    large = jnp.minimum(large, N_BUCKETS - 1)
    return jnp.where(dist < max_exact, dist, large)


def _bucket_map(dilation):
    iq = jnp.arange(DIL_BLOCK)[:, None]
    ik = jnp.arange(2 * DIL_BLOCK)[None, :]
    rel = DIL_BLOCK + iq - ik
    return _t5_bucket(jnp.maximum(rel, 0) * dilation).astype(jnp.int32)


def _q_perm(w):
    w3 = w.reshape(w.shape[0], HEADS, QK_NOPE + QK_ROPE)
    return jnp.concatenate([w3[:, :, :QK_NOPE].reshape(w.shape[0], -1),
                            w3[:, :, QK_NOPE:QK_NOPE + HALF_ROPE].reshape(w.shape[0], -1),
                            w3[:, :, QK_NOPE + HALF_ROPE:].reshape(w.shape[0], -1)], axis=1)


def _q_unperm(w):
    n0, n1 = HEADS * QK_NOPE, HEADS * HALF_ROPE
    r = w.shape[0]
    return jnp.concatenate([w[:, :n0].reshape(r, HEADS, QK_NOPE), w[:, n0:n0 + n1].reshape(r, HEADS, HALF_ROPE),
                            w[:, n0 + n1:].reshape(r, HEADS, HALF_ROPE)], axis=2).reshape(r, -1)


def _kv_perm(w):
    w3 = w.reshape(w.shape[0], HEADS, QK_NOPE + V_HEAD)
    return jnp.concatenate([w3[:, :, :QK_NOPE].reshape(w.shape[0], -1), w3[:, :, QK_NOPE:].reshape(w.shape[0], -1)],
                           axis=1)


def _kv_unperm(w):
    n0 = HEADS * QK_NOPE
    r = w.shape[0]
    return jnp.concatenate([w[:, :n0].reshape(r, HEADS, QK_NOPE), w[:, n0:].reshape(r, HEADS, V_HEAD)],
                           axis=2).reshape(r, -1)


def _row(v):
    return v.reshape(1, -1)


def kernel(x, c, norm_pre, norm_post, w_mod, b_mod, ffn_w_gate, ffn_w_up, ffn_w_down, mla_w_in, mla_q_norm, mla_w_q_up, mla_kv_norm, mla_w_kv_up, mla_w_o, dil_w_in, dil_w_o, rel_bias, loss_target, m_norm_pre, m_norm_post, m_w_mod, m_b_mod, m_ffn_w_gate, m_ffn_w_up, m_ffn_w_down, m_mla_w_in, m_mla_q_norm, m_mla_w_q_up, m_mla_kv_norm, m_mla_w_kv_up, m_mla_w_o, m_dil_w_in, m_dil_w_o, m_rel_bias, v_norm_pre, v_norm_post, v_w_mod, v_b_mod, v_ffn_w_gate, v_ffn_w_up, v_ffn_w_down, v_mla_w_in, v_mla_q_norm, v_mla_w_q_up, v_mla_kv_norm, v_mla_w_kv_up, v_mla_w_o, v_dil_w_in, v_dil_w_o, v_rel_bias):
    given = dict(locals())
    ix, iy, ic = _my_pos()
    shard_id = 2 * ix + iy
    dev_id = 4 * ix + 2 * iy + ic
    x2 = x[0]
    target = loss_target[0]
    half_idx = jnp.reshape(ic, (1,)).astype(jnp.int32)
    shard_idx = jnp.reshape(shard_id, (1,)).astype(jnp.int32)

    blk = jnp.zeros((8, D_MODEL), F32)
    blk = blk.at[0].set(c[0])
    blk = blk.at[1:3].set(jnp.pad(norm_pre.reshape(-1), (0, 512)).reshape(2, D_MODEL))
    blk = blk.at[3:5].set(jnp.pad(norm_post.reshape(-1), (0, 512)).reshape(2, D_MODEL))
    got = _all_gather(blk, name="ag_c_norms", in_vmem=True).reshape(N_SHARD, 2, 8, D_MODEL)
    c_all = got[:, :, 0, :].reshape(8, D_MODEL)

    def full_norm(lo):
        t = got[:, 0, lo:lo + 2, :].reshape(N_SHARD, 2 * D_MODEL)[:, :1536].reshape(N_SHARD, 2, 3, 256)
        return t.transpose(1, 2, 0, 3).reshape(2, 3, D_MODEL)

    pre_full, post_full = full_norm(1), full_norm(3)

    silu_c = _silu_bf16(c_all, name="silu_c")
    b_cols = lax.dynamic_slice_in_dim(b_mod, shard_id * 2304, 2304, axis=1).reshape(2, 1, 2304)
    mod_part = _mm(silu_c, w_mod, bias=b_cols, name="mod_mm", tn_cap=768)
    mod_all = _all_gather(mod_part.reshape(16, 2304), name="ag_mod", in_vmem=True)
    mod_all = mod_all.reshape(N_SHARD, 2, 2, 8, 2304)[:, 0]
    mod_mine = lax.dynamic_index_in_dim(mod_all, dev_id, axis=2, keepdims=False)
    mod = mod_mine.transpose(1, 0, 2).reshape(2, 9, D_MODEL)

    bf = lambda t: t.astype(BF16)
    ffn_fam = lambda i, h: [bf(jnp.stack([ffn_w_gate[i, h], ffn_w_up[i, h]])),
                            bf(ffn_w_down[i, h].reshape(2, F_SHARD // 2, D_MODEL))]
    mla_fam = [bf(mla_w_in.reshape(2, 128, -1)), bf(mla_w_q_up.reshape(2, 192, -1)),
               bf(mla_w_kv_up.reshape(2, 128, -1)), bf(mla_w_o.reshape(2, 128, D_MODEL))]
    dil_fam = [bf(dil_w_in.reshape(2, 512, -1)), bf(dil_w_o.reshape(2, 128, D_MODEL))]
    later_fams = [ffn_fam(0, 1), ffn_fam(1, 0) + dil_fam, ffn_fam(1, 1)]
    full, later_fams, mod = lax.optimization_barrier(
        (_gather_weights(ffn_fam(0, 0) + mla_fam, name="ag_weights_first"), later_fams, mod))

    def gather_later(fams, tag):
        lands = [lax.empty((N_SHARD,) + t.shape, t.dtype) for t in fams]
        send, recv, thru, token = _copies_start(fams, lands, _gather_plan, 7 * len(fams), name=f"ag_start_{tag}")
        return dict(send=send, recv=recv, thru=thru, token=token, n=len(fams), tag=tag)

    def arrive(st, after):
        thru = _copies_wait(st['send'], st['recv'], st['thru'], st['n'], _gather_plan, after,
                            name=f"ag_wait_{st['tag']}")
        return [_place_own(o, t) for t, o in zip(thru[:st['n']], thru[st['n']:])]

    def gather_chips(fams, tag):
        lands = [lax.empty((N_SHARD,) + t.shape, t.dtype) for t in fams]
        send, recv, thru, token = _copies_start(fams, lands, _gather_chips_plan, 4 * len(fams), name=f"ag_start_{tag}")
        return dict(send=send, recv=recv, thru=thru, token=token, n=len(fams), tag=tag)

    def pass_on(st, after):
        n, tag = st['n'], st['tag']
        thru = _copies_wait(st['send'], st['recv'], st['thru'], n, _gather_chips_plan, after, name=f"ag_mid_{tag}")
        send, recv, lands, token = _copies_start([], thru[n:], _gather_pass_plan, 3 * n, name=f"ag_pass_{tag}")
        return dict(send=send, recv=recv, thru=lands, fams=thru[:n], tag=tag), token[0, 0]

    def arrive_passed(st, after):
        lands = _copies_wait(st['send'], st['recv'], st['thru'], 0, _gather_pass_plan, after, name=f"ag_wait_{st['tag']}")
        return [_place_own(o, t) for t, o in zip(st['fams'], lands)]

    flight_a = gather_later(later_fams[0], "l0s2")
    _, next_fams = lax.optimization_barrier((flight_a['token'], later_fams[1]))
    flight_b = gather_chips(next_fams, "l1s01")
    as_ffn = lambda w_gu, w_dn: (w_gu, w_dn.reshape(N_SHARD, F_SHARD, D_MODEL))
    ffn_w = {(0, 0): as_ffn(full[0], full[1])}
    w_in = full[2].reshape(D_MODEL, -1)
    wq_p = _q_perm(full[3].reshape(N_SHARD, Q_LORA, -1).transpose(1, 0, 2).reshape(Q_LORA, -1))
    wkv_p = _kv_perm(full[4].reshape(N_SHARD, KV_LORA, -1).transpose(1, 0, 2).reshape(KV_LORA, -1))
    w_mo = full[5].reshape(D_MODEL, D_MODEL)
    dil_w = {}

    pos = jnp.arange(SEQ, dtype=F32)
    freqs = ROPE_THETA ** (-jnp.arange(HALF_ROPE, dtype=F32) / HALF_ROPE)
    ang = pos[:, None] * freqs[None, :]
    cos_k, sin_k = jnp.cos(ang), jnp.sin(ang)
    cos_q, sin_q = jnp.tile(cos_k, (1, HEADS)), jnp.tile(sin_k, (1, HEADS))

    buckets = [_bucket_map(d) for _, d in DIL_GROUPS]
    biases = [_bias_table(rel_bias[:, g * HEADS:(g + 1) * HEADS].T.reshape(HEADS, 1, N_BUCKETS), bk,
                          name=f"dil_bias_table_g{g}") for g, bk in enumerate(buckets)]

    def sub_params(i, sub):
        return dict(pg=_row(pre_full[i, sub]), qg=_row(post_full[i, sub]), sh=_row(mod[i, 3 * sub]),
                    sc=_row(mod[i, 3 * sub + 1]), gate=_row(mod[i, 3 * sub + 2]))

    def ffn_fwd(xin, i, h, sub, tie=None, mid=None):
        p = sub_params(i, sub)
        if tie is not None:
            p['sh'] = p['sh'] + tie
        tag = f"l{i}s{sub}"
        w_gu, w_dn = ffn_w[i, h]
        hn = _pre_fwd(xin, p['pg'], p['sc'], p['sh'], name=f"pre_fwd_{tag}")
        gu, a = _ffn_up(hn, w_gu, name=f"ffn_up_{tag}")
        if mid is not None:
            p['qg'] = p['qg'] + mid(a)
        f, out = _ffn_down(a, w_dn, xin, p['qg'], p['gate'], FFN_RES, name=f"ffn_down_{tag}")
        return out, dict(x=xin, hn=hn, gu=gu, a=a, f=f, p=p, i=i, h=h, tag=tag)

    def mla_fwd(xin, i, sub):
        p = sub_params(i, sub)
        tag = f"l{i}s{sub}"
        hn = _pre_fwd(xin, p['pg'], p['sc'], p['sh'], name=f"pre_fwd_{tag}")
        lat = _mm(hn, w_in, name="mla_lat")
        cq, ckv = lat[:, :Q_LORA], lat[:, Q_LORA:Q_LORA + KV_LORA]
        k1, k2 = lat[:, Q_LORA + KV_LORA:Q_LORA + KV_LORA + HALF_ROPE], lat[:, Q_LORA + KV_LORA + HALF_ROPE:]
        cqn = _rms_fwd(cq, mla_q_norm, name="mla_qnorm")
        ckvn = _rms_fwd(ckv, mla_kv_norm, name="mla_kvnorm")
        qp = _mm(cqn, wq_p, name="mla_q_up")
        kvp = _mm(ckvn, wkv_p, name="mla_kv_up")
        n0, n1 = HEADS * QK_NOPE, HEADS * HALF_ROPE
        qr1, qr2 = _rope(qp[:, n0:n0 + n1], qp[:, n0 + n1:], cos_q, sin_q, name="rope_q")
        kr1, kr2 = _rope(k1, k2, cos_k, sin_k, name="rope_k")
        q = jnp.concatenate([qp[:, :n0].reshape(SEQ, HEADS, QK_NOPE), qr1.reshape(SEQ, HEADS, HALF_ROPE),
                             qr2.reshape(SEQ, HEADS, HALF_ROPE)], axis=2).transpose(1, 0, 2).astype(BF16)
        kr = jnp.broadcast_to(jnp.concatenate([kr1, kr2], axis=1)[:, None, :], (SEQ, HEADS, QK_ROPE))
        k = jnp.concatenate([kvp[:, :n0].reshape(SEQ, HEADS, QK_NOPE), kr], axis=2).transpose(1, 0, 2).astype(BF16)
        v = _to_heads(kvp[:, n0:], V_HEAD).astype(BF16)
        o, lse = _mla_attn_fwd(q, k, v, name="mla_attn_fwd")
        o_flat = _from_heads(o).astype(BF16)
        f = _mm(o_flat, w_mo, name="mla_out")
        out = _post_fwd(f, xin, p['qg'], p['gate'], 1.0, name=f"post_fwd_{tag}")
        return out, dict(x=xin, hn=hn, cq=cq, ckv=ckv, cqn=cqn, ckvn=ckvn, q=q, k=k, v=v, o=o, lse=lse,
                         o_flat=o_flat, f=f, p=p, tag=tag)

    def dil_fwd(xin, i, sub):
        p = sub_params(i, sub)
        tag = f"l{i}s{sub}"
        hn = _pre_fwd(xin, p['pg'], p['sc'], p['sh'], name=f"pre_fwd_{tag}")
        heads = _proj_heads(hn, dil_w['in'], name="dil_proj")
        outs, lses = [], []
        for g, (window, d) in enumerate(DIL_GROUPS):
            o, lse = _dil_attn_fwd(heads, biases[g], g, d, name=f"dil_attn_fwd_g{g}")
            outs.append(o)
            lses.append(lse)
        o_flat = _dil_mix_fwd(outs, lses, name="dil_mix_fwd")
        f = _mm(o_flat, dil_w['out'], name="dil_out")
        out = _post_fwd(f, xin, p['qg'], p['gate'], 1.0, name=f"post_fwd_{tag}")
        return out, dict(x=xin, hn=hn, heads=heads, outs=outs, lses=lses, o_flat=o_flat, f=f, p=p, tag=tag)

    saved = [None] * 6
    xs, saved[0] = ffn_fwd(x2, 0, 0, 0, tie=flight_a['token'][0, 0] + flight_b['token'][0, 0])
    xs, saved[1] = mla_fwd(xs, 0, 1)
    ffn_w[0, 1] = as_ffn(*arrive(flight_a, xs))
    passed = {}

    def second_step(after):
        passed['st'], tok = pass_on(flight_b, after)
        return tok

    xs, saved[2] = ffn_fwd(xs, 0, 1, 2, mid=second_step)
    got, last_fams = lax.optimization_barrier((arrive_passed(passed['st'], xs), later_fams[2]))
    ffn_w[1, 0] = as_ffn(got[0], got[1])
    dil_w['in'], dil_w['out'] = got[2].reshape(N_SHARD, D_MODEL, -1), got[3].reshape(D_MODEL, D_MODEL)
    in_flight = gather_later(last_fams, "l1s2")
    xs, saved[3] = ffn_fwd(xs, 1, 0, 0, tie=in_flight['token'][0, 0])
    xs, saved[4] = dil_fwd(xs, 1, 1)
    ffn_w[1, 1] = as_ffn(*arrive(in_flight, xs))
    xs, saved[5] = ffn_fwd(xs, 1, 1, 2)

    dx, loss_part = _loss(xs, target, name="loss")

    dmod = [[None] * 9 for _ in range(2)]
    dpre = [[None] * 3 for _ in range(2)]
    dpost = [[None] * 3 for _ in range(2)]
    ffn_units = {}
    row_unit = lambda g, r, j: ((r % 2, r // 2), 0, j)

    def close_sub(dhn, dout, sv, i, sub, res_dgate, res_dqg):
        p = sv['p']
        dxs, dsh, dsc, dpg = _pre_bwd(dhn, sv['x'], dout, p['pg'], p['sc'], name=f"pre_bwd_{sv['tag']}")
        dmod[i][3 * sub], dmod[i][3 * sub + 1], dmod[i][3 * sub + 2] = dsh, dsc, res_dgate
        dpre[i][sub], dpost[i][sub] = dpg, res_dqg
        return dxs

    def ffn_bwd(dout, sv, sub, tie=0.0, mid=None):
        i, h, p, tag = sv['i'], sv['h'], sv['p'], sv['tag']
        w_gu, w_dn = ffn_w[i, h]
        df, dgate, dqg = _post_bwd(dout, sv['f'], p['qg'] + tie, p['gate'], FFN_RES, name=f"post_bwd_{tag}")
        u_dn = _mm(sv['a'], df, ta=True, tn_cap=D_MODEL // 2, out_shape=(2, N_SHARD, F_SHARD, D_MODEL // 2),
                   out_sel=lambda g, r, j: ((j, g), r, 0), name=f"ffn_dwd_{tag}")
        dgu = _ffn_dgu(df, w_dn, sv['gu'], name=f"ffn_dgu_{tag}")
        if mid is not None:
            p = dict(p, pg=p['pg'] + mid(dgu))
        u_gu = _mm(dgu.reshape(2 * N_SHARD, SEQ, F_SHARD), sv['hn'], ta=True,
                   out_shape=(2, N_SHARD, F_SHARD, D_MODEL), out_sel=lambda g, r, j: ((g % 2, g // 2), r, j),
                   name=f"ffn_dwgu_{tag}")
        ffn_units[i, h] = [u_gu, u_dn]
        dxs, dsh, dsc, dpg = _ffn_dhn(dgu, w_gu, sv['x'], dout, p['pg'], p['sc'], name=f"ffn_dhn_{tag}")
        dmod[i][3 * sub], dmod[i][3 * sub + 1], dmod[i][3 * sub + 2] = dsh, dsc, dgate
        dpre[i][sub], dpost[i][sub] = dpg, dqg
        return dxs

    def mla_bwd(dout, sv, i, sub, tie=0.0):
        p, tag = sv['p'], sv['tag']
        df, dgate, dqg = _post_bwd(dout, sv['f'], p['qg'] + tie, p['gate'], 1.0, name=f"post_bwd_{tag}")
        u_wo = _mm(sv['o_flat'], df, ta=True, tm_cap=128, out_shape=(2, N_SHARD, 128, D_MODEL), out_sel=row_unit,
                   name="mla_dwo")
        do_flat = _mm(df, w_mo, tb=True, name="mla_do")
        do = _to_heads(do_flat, V_HEAD)
        dq, dk, dv = _mla_attn_bwd(sv['q'], sv['k'], sv['v'], sv['o'], do, sv['lse'], name="mla_attn_bwd")
        dq_t = dq.transpose(1, 0, 2)
        dqr1, dqr2 = _rope(dq_t[:, :, QK_NOPE:QK_NOPE + HALF_ROPE].reshape(SEQ, -1),
                           dq_t[:, :, QK_NOPE + HALF_ROPE:].reshape(SEQ, -1), cos_q, -sin_q, name="rope_q_bwd")
        dqp = jnp.concatenate([dq_t[:, :, :QK_NOPE].reshape(SEQ, -1), dqr1, dqr2], axis=1).astype(BF16)
        dkr = _head_sum(dk[:, :, QK_NOPE:], name="mla_dkr_sum")
        dk1, dk2 = _rope(dkr[:, :HALF_ROPE], dkr[:, HALF_ROPE:], cos_k, -sin_k, name="rope_k_bwd")
        dkvp = jnp.concatenate([_from_heads(dk[:, :, :QK_NOPE]), _from_heads(dv)], axis=1).astype(BF16)
        g_wq = _q_unperm(_mm(sv['cqn'], dqp, ta=True, name="mla_dwq"))
        g_wkv = _kv_unperm(_mm(sv['ckvn'], dkvp, ta=True, name="mla_dwkv"))
        dcqn = _mm(dqp, wq_p, tb=True, name="mla_dcqn")
        dckvn = _mm(dkvp, wkv_p, tb=True, name="mla_dckvn")
        dcq, g_qn = _rms_bwd(dcqn, sv['cq'], mla_q_norm, name="mla_qnorm_bwd")
        dckv, g_kvn = _rms_bwd(dckvn, sv['ckv'], mla_kv_norm, name="mla_kvnorm_bwd")
        dlat = jnp.concatenate([dcq, dckv, dk1, dk2], axis=1).astype(BF16)
        u_win = _mm(sv['hn'], dlat, ta=True, tm_cap=128, out_shape=(2, N_SHARD, 128, dlat.shape[1]),
                    out_sel=row_unit, name="mla_dwin")
        dhn = _mm(dlat, w_in, tb=True, name="mla_dhn")
        col_unit = lambda t: (t.reshape(t.shape[0], N_SHARD, -1).transpose(1, 0, 2)
                              .reshape(N_SHARD, 2, t.shape[0] // 2, -1).transpose(1, 0, 2, 3))
        grads = dict(units=[u_win, col_unit(g_wq), col_unit(g_wkv), u_wo], q_norm=g_qn, kv_norm=g_kvn)
        return close_sub(dhn, dout, sv, i, sub, dgate, dqg), grads

    def dil_bwd(dout, sv, i, sub):
        p, tag = sv['p'], sv['tag']
        df, dgate, dqg = _post_bwd(dout, sv['f'], p['qg'], p['gate'], 1.0, name=f"post_bwd_{tag}")
        u_wo = _mm(sv['o_flat'], df, ta=True, tm_cap=128, out_shape=(2, N_SHARD, 128, D_MODEL), out_sel=row_unit,
                   name="dil_dwo")
        dos, dlts = _dil_mix_bwd(_mm(df, dil_w['out'], tb=True, name="dil_do"), sv['outs'], sv['lses'],
                                 name="dil_mix_bwd")
        pieces = []
        bias_rows = []
        for g, (window, d) in enumerate(DIL_GROUPS):
            dq, dk, dv, dbias = _dil_attn_bwd(sv['heads'], biases[g], sv['lses'][g], dos[g], dlts[g], g, d,
                                              name=f"dil_attn_bwd_g{g}")
            pieces += [dq, dk, dv]
            bias_rows.append(_bias_grad(dbias, buckets[g], name=f"dil_bias_grad_g{g}")[:, 0, :])
        dheads = jnp.concatenate(pieces).astype(BF16)
        u_win = _proj_heads_dw(sv['hn'], dheads, name="dil_dwin")
        dhn = _proj_heads_dx(dheads, dil_w['in'], name="dil_dhn")
        g_bias = jnp.concatenate(bias_rows, axis=0).T
        grads = dict(units=[u_win, u_wo], rel_bias=g_bias)
        return close_sub(dhn, dout, sv, i, sub, dgate, dqg), grads

    def to_sibling(units, tag):
        n = len(units)
        send, recv, thru, token = _copies_start(units, [lax.empty(u.shape[1:], F32) for u in units], _sibling_plan, n,
                                                name=f"rs{tag}_sibling_start")
        return dict(send=send, recv=recv, thru=thru, n=n, tag=tag), token[0, 0]

    def from_sibling(st, after):
        n, tag = st['n'], st['tag']
        thru = _copies_wait(st['send'], st['recv'], st['thru'], n, _sibling_plan, after, name=f"rs{tag}_sibling_wait")
        return [_add_half(u, g, half_idx, name=f"rs{tag}_add_half_{k}") for k, (u, g) in enumerate(zip(thru[:n], thru[n:]))]

    def to_chips(parts, tag):
        n = len(parts)
        send, recv, thru, token = _copies_start([w for _, w in parts],
                                                [lax.empty((3,) + w.shape[1:], BF16) for _, w in parts], _chips_plan,
                                                3 * n, name=f"rs{tag}_chips_start")
        return dict(send=send, recv=recv, thru=thru, n=n, tag=tag, parts=parts), token[0, 0]

    def from_chips(st, after):
        n, tag = st['n'], st['tag']
        thru = _copies_wait(st['send'], st['recv'], st['thru'], n, _chips_plan, after, name=f"rs{tag}_chips_wait")
        return [_add_shards(p, g, shard_idx, name=f"rs{tag}_add_shards_{k}")
                for k, ((p, _), g) in enumerate(zip(st['parts'], thru[n:]))]

    dx = ffn_bwd(dx, saved[5], 2)
    dx, dil_g = dil_bwd(dx, saved[4], 1, 1)
    dx = ffn_bwd(dx, saved[3], 0)
    st1, tok = to_sibling([*ffn_units[1, 1], *dil_g['units'], *ffn_units[1, 0]], "1")
    dx = ffn_bwd(dx, saved[2], 2, tie=tok)
    st1, tok1 = to_chips(from_sibling(st1, dx), "1")
    st2, tok2 = to_sibling(ffn_units[0, 1], "2")
    dx, mla_g = mla_bwd(dx, saved[1], 0, 1, tie=tok1 + tok2)
    reds1 = from_chips(st1, dx)
    st2, tok = to_chips(from_sibling(st2, dx), "2")
    st3, tok3 = to_sibling(mla_g['units'], "3")
    onward = {}

    def mixer_to_chips(after):
        onward['st'], t = to_chips(from_sibling(st3, after), "3")
        return t

    dx = ffn_bwd(dx, saved[0], 0, tie=tok + tok3, mid=mixer_to_chips)
    reds2 = from_chips(st2, dx)
    reds3 = from_chips(onward['st'], dx)
    grad_x = dx[None]

    pad_row = lambda v: jnp.pad(v.reshape(-1), (0, (-v.size) % D_MODEL)).reshape(-1, D_MODEL)
    small = jnp.concatenate(
        [jnp.concatenate([dmod[i][r] for i in range(2) for r in range(9)], axis=0),
         jnp.concatenate([dpre[i][s] for i in range(2) for s in range(3)], axis=0),
         jnp.concatenate([dpost[i][s] for i in range(2) for s in range(3)], axis=0),
         pad_row(mla_g['q_norm']), pad_row(mla_g['kv_norm']), pad_row(dil_g['rel_bias']), pad_row(loss_part)], axis=0)
    small = jnp.pad(small, ((0, SMALL_ROWS - small.shape[0]), (0, 0)))
    small_all = _all_gather(small, name="ag_small_grads", in_vmem=True)
    small_sum = _sum_devices(small_all, 8, name="sum_small_grads")
    g_b_mod = small_sum[0:18].reshape(2, 9 * D_MODEL)
    my_cols = lambda t: lax.dynamic_slice_in_dim(t, shard_id * 256, 256, axis=2)
    g_norm_pre = my_cols(small_sum[18:24].reshape(2, 3, D_MODEL))
    g_norm_post = my_cols(small_sum[24:30].reshape(2, 3, D_MODEL))
    g_q_norm = small_sum[30, :Q_LORA].reshape(1, Q_LORA)
    g_kv_norm = small_sum[31, :KV_LORA].reshape(1, KV_LORA)
    g_rel_bias = small_sum[32:34].reshape(-1)[:N_BUCKETS * 48].reshape(N_BUCKETS, 48)
    loss = small_sum[34, 0]
    dmod_all = small_all.reshape(8, SMALL_ROWS, D_MODEL)[:, 0:18].reshape(8, 2, 9 * D_MODEL)
    dmod_cols = lax.dynamic_slice_in_dim(dmod_all, shard_id * 2304, 2304, axis=2).transpose(1, 0, 2)

    swap = lambda t: jnp.swapaxes(t, 2, 3)
    grads = dict(norm_pre=g_norm_pre, norm_post=g_norm_post, b_mod=g_b_mod, mla_q_norm=g_q_norm,
                 mla_kv_norm=g_kv_norm, rel_bias=g_rel_bias)
    deltas, new_m, new_v = {}, {}, {}

    def adamw(names):
        for n in names:
            view = swap if n in ('ffn_w_gate', 'ffn_w_up') else (lambda t: t)
            outs = _adamw(view(given[n]), view(grads[n]), view(given["m_" + n]), view(given["v_" + n]),
                          name=f"adamw_{n}")
            deltas[n], new_m[n], new_v[n] = (view(t) for t in outs)

    st0, tok = to_sibling(ffn_units[0, 0], "0")
    grads['w_mod'] = _mm(silu_c, (dmod_cols + tok).astype(BF16), ta=True, tn_cap=768, name="w_mod_grad")
    adamw(['w_mod'])
    st0, tok = to_chips(from_sibling(st0, deltas['w_mod']), "0")
    grads['b_mod'] = grads['b_mod'] + tok
    fin = _pair_gather(reds1 + reds2 + reds3, name="rs_pair_gather")
    for n, t in zip(['dil_w_in', 'dil_w_o', 'mla_w_in', 'mla_w_q_up', 'mla_w_kv_up', 'mla_w_o'], fin[2:4] + fin[8:12]):
        grads[n] = t.reshape(given[n].shape)
    adamw(['b_mod', 'dil_w_in', 'dil_w_o', 'mla_w_in', 'mla_w_q_up', 'mla_w_kv_up', 'mla_w_o', 'norm_pre', 'norm_post',
           'mla_q_norm', 'mla_kv_norm', 'rel_bias'])
    done = jnp.stack([deltas[n][(0,) * deltas[n].ndim] for n in ('dil_w_in', 'dil_w_o', 'mla_w_in', 'mla_w_q_up',
                                                                 'mla_w_kv_up', 'mla_w_o', 'b_mod')])
    reds0 = from_chips(st0, done)
    fin0 = _pair_gather(reds0, name="rs_pair_gather_last")
    ffn_fin = {(1, 1): fin[0:2], (1, 0): fin[4:6], (0, 1): fin[6:8], (0, 0): fin0}
    per_ffn = lambda pick: jnp.stack([jnp.stack([pick(*ffn_fin[i, h]) for h in range(2)]) for i in range(2)])
    grads.update(ffn_w_gate=swap(per_ffn(lambda gu, dn: gu[0])), ffn_w_up=swap(per_ffn(lambda gu, dn: gu[1])),
                 ffn_w_down=per_ffn(lambda gu, dn: jnp.concatenate([dn[0], dn[1]], axis=1)))
    adamw(['ffn_w_gate', 'ffn_w_up', 'ffn_w_down'])
    return (loss, grad_x, *[grads[n] for n in WEIGHTS], *[deltas[n] for n in WEIGHTS],
            *[new_m[n] for n in WEIGHTS], *[new_v[n] for n in WEIGHTS])
```

```python
import math

import jax
import jax.numpy as jnp
from jax import lax
from jax.experimental import pallas as pl
from jax.experimental.pallas import tpu as pltpu

F32 = jnp.float32
BF16 = jnp.bfloat16
MESH = pl.DeviceIdType.MESH

SEQ = 2048
D_MODEL = 1024
D_FF = 2816
N_SHARD = 4
F_SHARD = D_FF // N_SHARD
EPS = 1e-6
FFN_RES = 0.5
HEADS = 16
Q_LORA, KV_LORA, QK_NOPE, QK_ROPE, V_HEAD = 384, 256, 64, 32, 64
HALF_ROPE = QK_ROPE // 2
ROPE_THETA = 10000.0
DIL_GROUPS = ((128, 1), (512, 4), (2048, 16))
DIL_BLOCK = 128
N_BUCKETS = 32
MAX_DISTANCE = 2048
ADAM_LR, ADAM_B1, ADAM_B2, ADAM_EPS, ADAM_WD, ADAM_STEP = 0.001, 0.9, 0.999, 1e-08, 0.01, 10

VMEM_LIMIT = 48 * 1024 * 1024
SMALL_ROWS = 40

WEIGHTS = ['norm_pre', 'norm_post', 'w_mod', 'b_mod', 'ffn_w_gate', 'ffn_w_up', 'ffn_w_down', 'mla_w_in',
           'mla_q_norm', 'mla_w_q_up', 'mla_kv_norm', 'mla_w_kv_up', 'mla_w_o', 'dil_w_in', 'dil_w_o', 'rel_bias']


def _cparams(**kw):
    return pltpu.CompilerParams(vmem_limit_bytes=VMEM_LIMIT, **kw)


def _pick(n, cap, mult=128):
    if n <= cap:
        return n
    best = n
    for t in range(mult, cap + 1, mult):
        if n % t == 0:
            best = t
    return best


def _mm(a, b, *, name, ta=False, tb=False, reduce_g=False, bias=None, out_dtype=F32, tm_cap=512, tn_cap=1024,
        g_n=None, b_sel=None, out_shape=None, out_sel=None, out_buf=None):
    a3 = a if a.ndim == 3 else a[None]
    ga = a3.shape[0]
    if b_sel is None:
        b_n = b if b.ndim == 3 else b[None]
        gb = b_n.shape[0]
        b_sel = (lambda g: (g,)) if gb > 1 else (lambda g: (0,))
        g_n = max(ga, gb)
    else:
        b_n = b
    k_dim, m_dim = (a3.shape[1], a3.shape[2]) if ta else (a3.shape[2], a3.shape[1])
    k2, n_dim = (b_n.shape[-1], b_n.shape[-2]) if tb else (b_n.shape[-2], b_n.shape[-1])
    assert k_dim == k2, (a.shape, b.shape)
    tm = _pick(m_dim, tm_cap, 128 if ta else 8)
    tn = _pick(n_dim, tn_cap, 128)
    mt, nt = m_dim // tm, n_dim // tn
    dims = (((0 if ta else 1,), (1 if tb else 0,)), ((), ()))

    if reduce_g:
        grid = (mt, nt, g_n)
        ids = lambda i, j, g: (g, i, j)
    else:
        grid = (g_n, mt, nt)
        ids = lambda g, i, j: (g, i, j)

    def a_map(*p):
        g, i, j = ids(*p)
        g = g if ga > 1 else 0
        return (g, 0, i) if ta else (g, i, 0)

    def b_map(*p):
        g, i, j = ids(*p)
        return (*b_sel(g), j, 0) if tb else (*b_sel(g), 0, j)

    b_lead = (None,) * (b_n.ndim - 2)
    a_spec = pl.BlockSpec((None, k_dim, tm) if ta else (None, tm, k_dim), a_map)
    b_spec = pl.BlockSpec(b_lead + ((tn, k_dim) if tb else (k_dim, tn)), b_map)
    in_specs = [a_spec, b_spec]
    operands = [a3, b_n]
    if bias is not None:
        assert not reduce_g and bias.shape == (g_n, 1, n_dim)
        in_specs.append(pl.BlockSpec((None, 1, tn), lambda g, i, j: (g, 0, j)))
        operands.append(bias)
    aliases = {}
    if out_buf is not None:
        assert tuple(out_buf.shape) == tuple(out_shape) and out_buf.dtype == out_dtype
        in_specs.append(pl.BlockSpec(memory_space=pl.ANY))
        operands.append(out_buf)
        aliases = {len(operands) - 1: 0}

    if reduce_g:
        out_spec = pl.BlockSpec((tm, tn), lambda i, j, g: (i, j))
        out_sds = jax.ShapeDtypeStruct((m_dim, n_dim), F32)
    elif out_shape is not None:
        def o_map(g, i, j):
            lead, rb, cb = out_sel(g, i, j)
            return (*lead, rb, cb)

        out_spec = pl.BlockSpec((None,) * (len(out_shape) - 2) + (tm, tn), o_map)
        out_sds = jax.ShapeDtypeStruct(tuple(out_shape), out_dtype)
    else:
        out_spec = pl.BlockSpec((None, tm, tn), lambda g, i, j: (g, i, j))
        out_sds = jax.ShapeDtypeStruct((g_n, m_dim, n_dim), out_dtype)

    def body(a_ref, b_ref, *rest):
        o_ref = rest[-1]
        r = lax.dot_general(a_ref[...].astype(BF16), b_ref[...].astype(BF16), dims, preferred_element_type=F32)
        if bias is not None:
            r = r + rest[0][...]
        if reduce_g:
            g = pl.program_id(2)

            @pl.when(g == 0)
            def _():
                o_ref[...] = r

            @pl.when(g > 0)
            def _():
                o_ref[...] += r
        else:
            o_ref[...] = r.astype(o_ref.dtype)

    out = pl.pallas_call(body, grid=grid, in_specs=in_specs, out_specs=out_spec, out_shape=out_sds,
                         input_output_aliases=aliases, compiler_params=_cparams(), name=name)(*operands)
    if not reduce_g and out_shape is None and a.ndim == 2 and b.ndim == 2:
        out = out[0]
    return out


def _rows(tm, w):
    return pl.BlockSpec((tm, w), lambda i: (i, 0))


def _vec(w):
    return pl.BlockSpec((1, w), lambda i: (0, 0))


def _rstd(v):
    return lax.rsqrt(jnp.mean(v * v, axis=-1, keepdims=True) + EPS)


V_PG, V_QG, V_SH, V_SC, V_GATE = range(5)


def _vrow(v_ref, k):
    return v_ref[k:k + 1, :]


def _vecs(w):
    return pl.BlockSpec((8, w), lambda *_: (0, 0))


def _pre_fwd(x, vp, *, name):
    s_n, w = x.shape
    tm = _pick(s_n, 512, 8)

    def body(x_ref, v_ref, o_ref):
        xv = x_ref[...]
        n = (xv * _rstd(xv)) * _vrow(v_ref, V_PG)
        o_ref[...] = (n * (1.0 + _vrow(v_ref, V_SC)) + _vrow(v_ref, V_SH)).astype(o_ref.dtype)

    return pl.pallas_call(body, grid=(s_n // tm,), in_specs=[_rows(tm, w), _vecs(w)],
                          out_specs=_rows(tm, w), out_shape=jax.ShapeDtypeStruct((s_n, w), BF16),
                          compiler_params=_cparams(), name=name)(x, vp)


def _post_fwd(f, x, vp, res_w, *, name):
    s_n, w = x.shape
    tm = _pick(s_n, 512, 8)

    def body(f_ref, x_ref, v_ref, o_ref):
        fv = f_ref[...]
        y = (fv * _rstd(fv)) * _vrow(v_ref, V_QG)
        o_ref[...] = x_ref[...] + (res_w * _vrow(v_ref, V_GATE)) * y

    return pl.pallas_call(body, grid=(s_n // tm,), in_specs=[_rows(tm, w), _rows(tm, w), _vecs(w)],
                          out_specs=_rows(tm, w), out_shape=jax.ShapeDtypeStruct((s_n, w), F32),
                          compiler_params=_cparams(), name=name)(f, x, vp)


def _post_bwd(dout, f, vp, res_w, *, name):
    s_n, w = f.shape
    tm = _pick(s_n, 512, 8)

    def body(do_ref, f_ref, v_ref, df_ref, dgate_ref, dqg_ref):
        @pl.when(pl.program_id(0) == 0)
        def _():
            dgate_ref[...] = jnp.zeros_like(dgate_ref)
            dqg_ref[...] = jnp.zeros_like(dqg_ref)

        do = do_ref[...]
        fv = f_ref[...]
        r = _rstd(fv)
        fh = fv * r
        qg_v = _vrow(v_ref, V_QG)
        dgate_ref[...] += res_w * jnp.sum(do * (fh * qg_v), axis=0, keepdims=True)
        dy = do * (res_w * _vrow(v_ref, V_GATE))
        dqg_ref[...] += jnp.sum(dy * fh, axis=0, keepdims=True)
        dfh = dy * qg_v
        df = r * (dfh - fh * jnp.mean(dfh * fh, axis=-1, keepdims=True))
        df_ref[...] = df.astype(df_ref.dtype)

    return pl.pallas_call(
        body, grid=(s_n // tm,), in_specs=[_rows(tm, w), _rows(tm, w), _vecs(w)],
        out_specs=[_rows(tm, w), _vec(w), _vec(w)],
        out_shape=[jax.ShapeDtypeStruct((s_n, w), BF16), jax.ShapeDtypeStruct((1, w), F32),
                   jax.ShapeDtypeStruct((1, w), F32)],
        compiler_params=_cparams(), name=name)(dout, f, vp)


def _pre_bwd(dhn, x, dout, vp, *, name):
    s_n, w = x.shape
    tm = _pick(s_n, 512, 8)

    def body(dhn_ref, x_ref, do_ref, v_ref, dx_ref, dsh_ref, dsc_ref, dpg_ref):
        @pl.when(pl.program_id(0) == 0)
        def _():
            dsh_ref[...] = jnp.zeros_like(dsh_ref)
            dsc_ref[...] = jnp.zeros_like(dsc_ref)
            dpg_ref[...] = jnp.zeros_like(dpg_ref)

        dhn_v = dhn_ref[...]
        xv = x_ref[...]
        r = _rstd(xv)
        xh = xv * r
        pg_v = _vrow(v_ref, V_PG)
        dsh_ref[...] += jnp.sum(dhn_v, axis=0, keepdims=True)
        dsc_ref[...] += jnp.sum(dhn_v * (xh * pg_v), axis=0, keepdims=True)
        dn = dhn_v * (1.0 + _vrow(v_ref, V_SC))
        dpg_ref[...] += jnp.sum(dn * xh, axis=0, keepdims=True)
        dxh = dn * pg_v
        dx_ref[...] = do_ref[...] + r * (dxh - xh * jnp.mean(dxh * xh, axis=-1, keepdims=True))

    vec = jax.ShapeDtypeStruct((1, w), F32)
    return pl.pallas_call(
        body, grid=(s_n // tm,), in_specs=[_rows(tm, w), _rows(tm, w), _rows(tm, w), _vecs(w)],
        out_specs=[_rows(tm, w), _vec(w), _vec(w), _vec(w)],
        out_shape=[jax.ShapeDtypeStruct((s_n, w), F32), vec, vec, vec],
        compiler_params=_cparams(), name=name)(dhn, x, dout, vp)


def _rms_fwd(x, g, *, name):
    s_n, w = x.shape
    tm = _pick(s_n, 512, 8)

    def body(x_ref, g_ref, o_ref):
        xv = x_ref[...]
        o_ref[...] = ((xv * _rstd(xv)) * g_ref[...]).astype(o_ref.dtype)

    return pl.pallas_call(body, grid=(s_n // tm,), in_specs=[_rows(tm, w), _vec(w)], out_specs=_rows(tm, w),
                          out_shape=jax.ShapeDtypeStruct((s_n, w), BF16), compiler_params=_cparams(),
                          name=name)(x, g)


def _rms_bwd(dy, x, g, *, name):
    s_n, w = x.shape
    tm = _pick(s_n, 512, 8)

    def body(dy_ref, x_ref, g_ref, dx_ref, dg_ref):
        @pl.when(pl.program_id(0) == 0)
        def _():
            dg_ref[...] = jnp.zeros_like(dg_ref)

        dy_v = dy_ref[...]
        xv = x_ref[...]
        r = _rstd(xv)
        xh = xv * r
        dg_ref[...] += jnp.sum(dy_v * xh, axis=0, keepdims=True)
        dxh = dy_v * g_ref[...]
        dx_ref[...] = r * (dxh - xh * jnp.mean(dxh * xh, axis=-1, keepdims=True))

    return pl.pallas_call(
        body, grid=(s_n // tm,), in_specs=[_rows(tm, w), _rows(tm, w), _vec(w)],
        out_specs=[_rows(tm, w), _vec(w)],
        out_shape=[jax.ShapeDtypeStruct((s_n, w), F32), jax.ShapeDtypeStruct((1, w), F32)],
        compiler_params=_cparams(), name=name)(dy, x, g)


def _rope(a1, a2, cos, sin, *, name):
    s_n, w = a1.shape
    tm = _pick(s_n, 512, 8)

    def body(a1_ref, a2_ref, c_ref, s_ref, r1_ref, r2_ref):
        u, v, c_v, s_v = a1_ref[...], a2_ref[...], c_ref[...], s_ref[...]
        r1_ref[...] = u * c_v - v * s_v
        r2_ref[...] = u * s_v + v * c_v

    sd = jax.ShapeDtypeStruct((s_n, w), F32)
    return pl.pallas_call(body, grid=(s_n // tm,), in_specs=[_rows(tm, w)] * 4, out_specs=[_rows(tm, w)] * 2,
                          out_shape=[sd, sd], compiler_params=_cparams(), name=name)(a1, a2, cos, sin)


def _silu_bf16(x, *, name):
    def body(x_ref, o_ref):
        xv = x_ref[...]
        o_ref[...] = (xv * jax.nn.sigmoid(xv)).astype(o_ref.dtype)

    return pl.pallas_call(body, out_shape=jax.ShapeDtypeStruct(x.shape, BF16), name=name)(x)


def _loss(y, target, *, name):
    s_n, w = y.shape
    tm = _pick(s_n, 512, 8)

    def body(y_ref, t_ref, dy_ref, l_ref):
        @pl.when(pl.program_id(0) == 0)
        def _():
            l_ref[...] = jnp.zeros_like(l_ref)

        e = y_ref[...] - t_ref[...]
        dy_ref[...] = e * (1.0 / w)
        row = jnp.mean(e * e, axis=-1, keepdims=True)
        l_ref[...] += 0.5 * jnp.sum(row, axis=0, keepdims=True)

    return pl.pallas_call(
        body, grid=(s_n // tm,), in_specs=[_rows(tm, w), _rows(tm, w)],
        out_specs=[_rows(tm, w), pl.BlockSpec((1, 1), lambda i: (0, 0))],
        out_shape=[jax.ShapeDtypeStruct((s_n, w), F32), jax.ShapeDtypeStruct((1, 1), F32)],
        compiler_params=_cparams(), name=name)(y, target)


FFN_TM = 512


def _ffn_up(hn, w_gu, *, name):
    s_n, d = hn.shape
    f = w_gu.shape[-1]
    tm = _pick(s_n, FFN_TM, 8)

    def body(hn_ref, wg_ref, wu_ref, gu_ref, a_ref):
        xv = hn_ref[...]
        g = jnp.dot(xv, wg_ref[...], preferred_element_type=F32)
        u = jnp.dot(xv, wu_ref[...], preferred_element_type=F32)
        gu_ref[0] = g.astype(BF16)
        gu_ref[1] = u.astype(BF16)
        a_ref[...] = ((g * jax.nn.sigmoid(g)) * u).astype(BF16)

    w_blk = lambda t: pl.BlockSpec((None, None, d, f), lambda s, m: (s, t, 0, 0))
    return pl.pallas_call(
        body, grid=(N_SHARD, s_n // tm),
        in_specs=[pl.BlockSpec((tm, d), lambda s, m: (m, 0)), w_blk(0), w_blk(1)],
        out_specs=[pl.BlockSpec((None, 2, tm, f), lambda s, m: (s, 0, m, 0)),
                   pl.BlockSpec((None, tm, f), lambda s, m: (s, m, 0))],
        out_shape=[jax.ShapeDtypeStruct((N_SHARD, 2, s_n, f), BF16), jax.ShapeDtypeStruct((N_SHARD, s_n, f), BF16)],
        compiler_params=_cparams(), name=name)(hn, w_gu, w_gu)


def _ffn_down(a, w_dn, x, vp, res_w, *, name):
    _, s_n, f = a.shape
    d = w_dn.shape[-1]
    tm = _pick(s_n, FFN_TM, 8)
    a = a.reshape(-1, 2, s_n, f)
    w_dn = w_dn.reshape(-1, 2, f, d)
    g_n = a.shape[0]

    def body(a_ref, w_ref, x_ref, v_ref, f_ref, o_ref):
        g = pl.program_id(1)
        r = (jnp.dot(a_ref[0], w_ref[0], preferred_element_type=F32)
             + jnp.dot(a_ref[1], w_ref[1], preferred_element_type=F32))

        @pl.when(g == 0)
        def _():
            f_ref[...] = r

        @pl.when(g > 0)
        def _():
            f_ref[...] += r

        @pl.when(g == g_n - 1)
        def _():
            fv = f_ref[...]
            y = (fv * _rstd(fv)) * _vrow(v_ref, V_QG)
            o_ref[...] = x_ref[...] + (res_w * _vrow(v_ref, V_GATE)) * y

    row = pl.BlockSpec((tm, d), lambda m, g: (m, 0))
    sd = jax.ShapeDtypeStruct((s_n, d), F32)
    return pl.pallas_call(
        body, grid=(s_n // tm, g_n),
        in_specs=[pl.BlockSpec((None, 2, tm, f), lambda m, g: (g, 0, m, 0)),
                  pl.BlockSpec((None, 2, f, d), lambda m, g: (g, 0, 0, 0)), row, _vecs(d)],
        out_specs=[row, row], out_shape=[sd, sd], compiler_params=_cparams(), name=name)(a, w_dn, x, vp)


def _ffn_dhn(dgu, w_gu, x, dout, vp, *, name):
    g_n, _, s_n, f = dgu.shape
    d = w_gu.shape[-2]
    tm = _pick(s_n, FFN_TM, 8)
    nt_dims = (((1,), (1,)), ((), ()))

    def body(a_ref, w_ref, x_ref, do_ref, v_ref, dx_ref, dsh_ref, dsc_ref, dpg_ref, acc_ref):
        m, g = pl.program_id(0), pl.program_id(1)
        r = (lax.dot_general(a_ref[0], w_ref[0], nt_dims, preferred_element_type=F32)
             + lax.dot_general(a_ref[1], w_ref[1], nt_dims, preferred_element_type=F32))

        @pl.when(g == 0)
        def _():
            acc_ref[...] = r

        @pl.when(g > 0)
        def _():
            acc_ref[...] += r

        @pl.when((m == 0) & (g == 0))
        def _():
            dsh_ref[...] = jnp.zeros_like(dsh_ref)
            dsc_ref[...] = jnp.zeros_like(dsc_ref)
            dpg_ref[...] = jnp.zeros_like(dpg_ref)

        @pl.when(g == g_n - 1)
        def _():
            dhn_v = acc_ref[...]
            xv = x_ref[...]
            rs = _rstd(xv)
            xh = xv * rs
            pg_v = _vrow(v_ref, V_PG)
            dsh_ref[...] += jnp.sum(dhn_v, axis=0, keepdims=True)
            dsc_ref[...] += jnp.sum(dhn_v * (xh * pg_v), axis=0, keepdims=True)
            dn = dhn_v * (1.0 + _vrow(v_ref, V_SC))
            dpg_ref[...] += jnp.sum(dn * xh, axis=0, keepdims=True)
            dxh = dn * pg_v
            dx_ref[...] = do_ref[...] + rs * (dxh - xh * jnp.mean(dxh * xh, axis=-1, keepdims=True))

    row = pl.BlockSpec((tm, d), lambda m, g: (m, 0))
    vec = pl.BlockSpec((1, d), lambda m, g: (0, 0))
    vsd = jax.ShapeDtypeStruct((1, d), F32)
    return pl.pallas_call(
        body, grid=(s_n // tm, g_n),
        in_specs=[pl.BlockSpec((None, 2, tm, f), lambda m, g: (g, 0, m, 0)),
                  pl.BlockSpec((None, 2, d, f), lambda m, g: (g, 0, 0, 0)), row, row, _vecs(d)],
        out_specs=[row, vec, vec, vec], out_shape=[jax.ShapeDtypeStruct((s_n, d), F32), vsd, vsd, vsd],
        scratch_shapes=[pltpu.VMEM((tm, d), F32)], compiler_params=_cparams(), name=name)(dgu, w_gu, x, dout, vp)


def _ffn_dgu(df, w_dn, gu, *, name):
    s_n, d = df.shape
    f = w_dn.shape[-2]
    tm = _pick(s_n, FFN_TM, 8)

    def body(df_ref, wd_ref, gu_ref, o_ref):
        da = lax.dot_general(df_ref[...], wd_ref[...], (((1,), (1,)), ((), ())), preferred_element_type=F32)
        g = gu_ref[0].astype(F32)
        u = gu_ref[1].astype(F32)
        sig = jax.nn.sigmoid(g)
        o_ref[0] = (da * u * (sig * (1.0 + g * (1.0 - sig)))).astype(BF16)
        o_ref[1] = (da * (g * sig)).astype(BF16)

    gu_blk = pl.BlockSpec((None, 2, tm, f), lambda s, m: (s, 0, m, 0))
    return pl.pallas_call(
        body, grid=(N_SHARD, s_n // tm),
        in_specs=[pl.BlockSpec((tm, d), lambda s, m: (m, 0)),
                  pl.BlockSpec((None, f, d), lambda s, m: (s, 0, 0)), gu_blk],
        out_specs=gu_blk, out_shape=jax.ShapeDtypeStruct((N_SHARD, 2, s_n, f), BF16),
        compiler_params=_cparams(), name=name)(df, w_dn, gu)


_NT = (((1,), (1,)), ((), ()))
_TN = (((0,), (0,)), ((), ()))
MLA_TQ = 256


def _causal_mask(i, tq, s_n):
    qpos = i * tq + lax.broadcasted_iota(jnp.int32, (tq, s_n), 0)
    kpos = lax.broadcasted_iota(jnp.int32, (tq, s_n), 1)
    return kpos <= qpos


def _mla_attn_fwd(q, k, v, *, name):
    h_n, s_n, dq = q.shape
    dv = v.shape[-1]
    tq = MLA_TQ
    scale = float(dq) ** -0.5

    def body(q_ref, k_ref, v_ref, o_ref, lse_ref):
        i = pl.program_id(1)
        for e in range(1, s_n // tq + 1):
            @pl.when(i == e - 1)
            def _(ext=e * tq):
                mask = _causal_mask(i, tq, ext)
                s = lax.dot_general(q_ref[...], k_ref[0:ext, :], _NT, preferred_element_type=F32) * scale
                s = jnp.where(mask, s, -jnp.inf)
                m = jnp.max(s, axis=-1, keepdims=True)
                p = jnp.exp(s - m)
                l = jnp.sum(p, axis=-1, keepdims=True)
                o = jnp.dot(p.astype(BF16), v_ref[0:ext, :], preferred_element_type=F32)
                o_ref[...] = o / l
                lse_ref[...] = m + jnp.log(l)

    return pl.pallas_call(
        body, grid=(h_n, s_n // tq),
        in_specs=[pl.BlockSpec((None, tq, dq), lambda h, i: (h, i, 0)),
                  pl.BlockSpec((None, s_n, dq), lambda h, i: (h, 0, 0)),
                  pl.BlockSpec((None, s_n, dv), lambda h, i: (h, 0, 0))],
        out_specs=[pl.BlockSpec((None, tq, dv), lambda h, i: (h, i, 0)),
                   pl.BlockSpec((None, tq, 1), lambda h, i: (h, i, 0))],
        out_shape=[jax.ShapeDtypeStruct((h_n, s_n, dv), F32), jax.ShapeDtypeStruct((h_n, s_n, 1), F32)],
        compiler_params=_cparams(), name=name)(q, k, v)


def _mla_attn_bwd(q, k, v, o, do, lse, *, name):
    h_n, s_n, dq = q.shape
    dv = v.shape[-1]
    tq = MLA_TQ
    scale = float(dq) ** -0.5

    def body(q_ref, k_ref, v_ref, o_ref, do_ref, lse_ref, dq_ref, dk_ref, dv_ref):
        i = pl.program_id(1)

        @pl.when(i == 0)
        def _():
            dk_ref[...] = jnp.zeros_like(dk_ref)
            dv_ref[...] = jnp.zeros_like(dv_ref)

        for e in range(1, s_n // tq + 1):
            @pl.when(i == e - 1)
            def _(ext=e * tq):
                mask = _causal_mask(i, tq, ext)
                qv, kv, vv = q_ref[...], k_ref[0:ext, :], v_ref[0:ext, :]
                do_v = do_ref[...]
                s = lax.dot_general(qv, kv, _NT, preferred_element_type=F32) * scale
                p = jnp.where(mask, jnp.exp(s - lse_ref[...]), 0.0)
                dob = do_v.astype(BF16)
                dv_ref[0:ext, :] += lax.dot_general(p.astype(BF16), dob, _TN, preferred_element_type=F32)
                dp = lax.dot_general(dob, vv, _NT, preferred_element_type=F32)
                delta = jnp.sum(do_v * o_ref[...], axis=-1, keepdims=True)
                dsb = (p * (dp - delta) * scale).astype(BF16)
                dq_ref[...] = jnp.dot(dsb, kv, preferred_element_type=F32)
                dk_ref[0:ext, :] += lax.dot_general(dsb, qv, _TN, preferred_element_type=F32)

    return pl.pallas_call(
        body, grid=(h_n, s_n // tq),
        in_specs=[pl.BlockSpec((None, tq, dq), lambda h, i: (h, i, 0)),
                  pl.BlockSpec((None, s_n, dq), lambda h, i: (h, 0, 0)),
                  pl.BlockSpec((None, s_n, dv), lambda h, i: (h, 0, 0)),
                  pl.BlockSpec((None, tq, dv), lambda h, i: (h, i, 0)),
                  pl.BlockSpec((None, tq, dv), lambda h, i: (h, i, 0)),
                  pl.BlockSpec((None, tq, 1), lambda h, i: (h, i, 0))],
        out_specs=[pl.BlockSpec((None, tq, dq), lambda h, i: (h, i, 0)),
                   pl.BlockSpec((None, s_n, dq), lambda h, i: (h, 0, 0)),
                   pl.BlockSpec((None, s_n, dv), lambda h, i: (h, 0, 0))],
        out_shape=[jax.ShapeDtypeStruct((h_n, s_n, dq), F32), jax.ShapeDtypeStruct((h_n, s_n, dq), F32),
                   jax.ShapeDtypeStruct((h_n, s_n, dv), F32)],
        compiler_params=_cparams(), name=name)(q, k, v, o, do, lse)


def _head_sum(x, *, name):
    h_n, s_n, w = x.shape
    tm = _pick(s_n, 512, 8)

    def body(x_ref, o_ref):
        o_ref[...] = jnp.sum(x_ref[...], axis=0)

    return pl.pallas_call(body, grid=(s_n // tm,), in_specs=[pl.BlockSpec((h_n, tm, w), lambda i: (0, i, 0))],
                          out_specs=_rows(tm, w), out_shape=jax.ShapeDtypeStruct((s_n, w), F32),
                          compiler_params=_cparams(), name=name)(x)


N_BLK = SEQ // DIL_BLOCK
DIL_SCALE = 64 ** -0.5


def _dil_masks():
    iq = lax.broadcasted_iota(jnp.int32, (DIL_BLOCK, 2 * DIL_BLOCK), 0)
    ik = lax.broadcasted_iota(jnp.int32, (DIL_BLOCK, 2 * DIL_BLOCK), 1)
    rel = DIL_BLOCK + iq - ik
    both = (rel >= 0) & (rel <= DIL_BLOCK)
    iq1 = lax.broadcasted_iota(jnp.int32, (DIL_BLOCK, DIL_BLOCK), 0)
    ik1 = lax.broadcasted_iota(jnp.int32, (DIL_BLOCK, DIL_BLOCK), 1)
    return both, ik1 <= iq1


def _dil_block(j, d):
    nb = SEQ // d // DIL_BLOCK
    r, n = divmod(j, nb)
    first = n == 0
    rows = lambda start, size: pl.ds(start, size) if d == 1 else pl.ds(start, size, stride=d)
    q_rows = rows(n * DIL_BLOCK * d + r, DIL_BLOCK)
    k_rows = q_rows if first else rows((n - 1) * DIL_BLOCK * d + r, 2 * DIL_BLOCK)
    return q_rows, k_rows, (DIL_BLOCK if first else 0), first


PAIR = 2 * 64
N_PAIR = HEADS // 2


def _dil_head_specs(s_n, g):
    return [pl.BlockSpec((None, s_n, PAIR), lambda hp, t=t: ((g * 3 + t) * N_PAIR + hp, 0, 0)) for t in range(3)]


def _pair_specs(s_n, w):
    return pl.BlockSpec((2, s_n, w), lambda hp: (hp, 0, 0))


_PAIR_BIAS = pl.BlockSpec((2, DIL_BLOCK, 2 * DIL_BLOCK), lambda hp: (hp, 0, 0))


def _dil_attn_fwd(heads, bias, g, d, *, name):
    _, s_n, _ = heads.shape
    e = PAIR // 2

    def body(q_ref, k_ref, v_ref, b_ref, o_ref, lse_ref):
        m_both, m_first = _dil_masks()
        for j in range(N_BLK):
            q_rows, k_rows, b_lo, first = _dil_block(j, d)
            q2 = q_ref[q_rows, :].astype(BF16)
            k2 = k_ref[k_rows, :].astype(BF16)
            v2 = v_ref[k_rows, :].astype(BF16)
            for hh in range(2):
                cols = slice(hh * e, (hh + 1) * e)
                s = (lax.dot_general(q2[:, cols], k2[:, cols], _NT, preferred_element_type=F32) * DIL_SCALE
                     + b_ref[hh, :, b_lo:])
                s = jnp.where(m_first if first else m_both, s, -jnp.inf)
                m = jnp.max(s, axis=-1, keepdims=True)
                lse = m + jnp.log(jnp.sum(jnp.exp(s - m), axis=-1, keepdims=True))
                p = jnp.exp(s - lse)
                o_ref[hh, q_rows, :] = jnp.dot(p.astype(BF16), v2[:, cols], preferred_element_type=F32)
                lse_ref[hh, q_rows, :] = lse

    return pl.pallas_call(
        body, grid=(N_PAIR,), in_specs=_dil_head_specs(s_n, g) + [_PAIR_BIAS],
        out_specs=[_pair_specs(s_n, e), _pair_specs(s_n, 1)],
        out_shape=[jax.ShapeDtypeStruct((HEADS, s_n, e), F32), jax.ShapeDtypeStruct((HEADS, s_n, 1), F32)],
        compiler_params=_cparams(), name=name)(heads, heads, heads, bias)


def _dil_attn_bwd(heads, bias, lse, do, dlt, g, d, *, name):
    _, s_n, _ = heads.shape
    e = PAIR // 2

    def body(q_ref, k_ref, v_ref, b_ref, lse_ref, do_ref, dlt_ref, dq_ref, dk_ref, dv_ref, db_ref):
        db_ref[...] = jnp.zeros_like(db_ref)
        m_both, m_first = _dil_masks()
        nb = s_n // d // DIL_BLOCK
        own_v = own_k = own_rows = None
        for j in range(N_BLK):
            q_rows, k_rows, b_lo, first = _dil_block(j, d)
            q2 = q_ref[q_rows, :].astype(BF16)
            k2 = k_ref[k_rows, :].astype(BF16)
            v2 = v_ref[k_rows, :].astype(BF16)
            dq_h, dv_h, dk_h = [], [], []
            for hh in range(2):
                cols = slice(hh * e, (hh + 1) * e)
                qj, kk, vv = q2[:, cols], k2[:, cols], v2[:, cols]
                s = lax.dot_general(qj, kk, _NT, preferred_element_type=F32) * DIL_SCALE + b_ref[hh, :, b_lo:]
                p = jnp.where(m_first if first else m_both, jnp.exp(s - lse_ref[hh, q_rows, :]), 0.0)
                dob = do_ref[hh, q_rows, :].astype(BF16)
                dv_h.append(lax.dot_general(p.astype(BF16), dob, _TN, preferred_element_type=F32))
                dp = lax.dot_general(dob, vv, _NT, preferred_element_type=F32)
                ds = p * (dp - dlt_ref[hh, q_rows, :])
                db_ref[hh, :, b_lo:] += ds
                dsb = (ds * DIL_SCALE).astype(BF16)
                dq_h.append(jnp.dot(dsb, kk, preferred_element_type=F32))
                dk_h.append(lax.dot_general(dsb, qj, _TN, preferred_element_type=F32))
            dq_ref[q_rows, :] = jnp.concatenate(dq_h, axis=1)
            dvv, dkk = jnp.concatenate(dv_h, axis=1), jnp.concatenate(dk_h, axis=1)
            if not first:
                dv_ref[own_rows, :] = own_v + dvv[:DIL_BLOCK]
                dk_ref[own_rows, :] = own_k + dkk[:DIL_BLOCK]
                dvv, dkk = dvv[DIL_BLOCK:], dkk[DIL_BLOCK:]
            own_v, own_k, own_rows = dvv, dkk, q_rows
            if j % nb == nb - 1:
                dv_ref[own_rows, :] = own_v
                dk_ref[own_rows, :] = own_k

    slab = pl.BlockSpec((None, s_n, PAIR), lambda hp: (hp, 0, 0))
    sd = jax.ShapeDtypeStruct((N_PAIR, s_n, PAIR), F32)
    return pl.pallas_call(
        body, grid=(N_PAIR,),
        in_specs=_dil_head_specs(s_n, g) + [_PAIR_BIAS, _pair_specs(s_n, 1), _pair_specs(s_n, e), _pair_specs(s_n, 1)],
        out_specs=[slab, slab, slab, _PAIR_BIAS],
        out_shape=[sd, sd, sd, jax.ShapeDtypeStruct((HEADS, DIL_BLOCK, 2 * DIL_BLOCK), F32)],
        compiler_params=_cparams(), name=name)(heads, heads, heads, bias, lse, do, dlt)


def _proj_heads(x, w, *, name):
    s_n, k = x.shape
    n = w.shape[-1]
    tm, tn, e = 512, 768, PAIR
    per_blk, n_blk = tn // e, n // tn

    def body(x_ref, w_ref, o_ref):
        r = jnp.dot(x_ref[...], w_ref[...], preferred_element_type=F32)
        for j in range(per_blk):
            o_ref[j] = r[:, e * j:e * (j + 1)]

    return pl.pallas_call(
        body, grid=(w.shape[0], n_blk, s_n // tm),
        in_specs=[pl.BlockSpec((tm, k), lambda s, b, m: (m, 0)), pl.BlockSpec((None, k, tn), lambda s, b, m: (s, 0, b))],
        out_specs=pl.BlockSpec((per_blk, tm, e), lambda s, b, m: (s * n_blk + b, m, 0)),
        out_shape=jax.ShapeDtypeStruct((w.shape[0] * n // e, s_n, e), F32), compiler_params=_cparams(),
        name=name)(x, w)


def _heads_cat(d_ref):
    return jnp.concatenate([d_ref[j] for j in range(d_ref.shape[0])], axis=1)


def _proj_heads_dw(x, dh, *, name):
    s_n, k = x.shape
    tn, e = 768, PAIR
    per_blk = tn // e
    n_blk = dh.shape[0] // N_SHARD // per_blk
    n = n_blk * tn

    def body(x_ref, d_ref, o_ref):
        o_ref[...] = lax.dot_general(x_ref[...], _heads_cat(d_ref), _TN, preferred_element_type=F32)

    return pl.pallas_call(
        body, grid=(N_SHARD, n_blk, 2),
        in_specs=[pl.BlockSpec((s_n, k // 2), lambda s, b, r: (0, r)),
                  pl.BlockSpec((per_blk, s_n, e), lambda s, b, r: (s * n_blk + b, 0, 0))],
        out_specs=pl.BlockSpec((None, None, k // 2, tn), lambda s, b, r: (r, s, 0, b)),
        out_shape=jax.ShapeDtypeStruct((2, N_SHARD, k // 2, n), F32), compiler_params=_cparams(), name=name)(x, dh)


def _proj_heads_dx(dh, w, *, name):
    k, n = w.shape[1:]
    s_n = dh.shape[1]
    tm, tn, e = 512, 768, PAIR
    per_blk, n_blk = tn // e, n // tn

    def body(d_ref, w_ref, o_ref):
        r = lax.dot_general(_heads_cat(d_ref), w_ref[...], _NT, preferred_element_type=F32)
        g = pl.program_id(1)

        @pl.when(g == 0)
        def _():
            o_ref[...] = r

        @pl.when(g > 0)
        def _():
            o_ref[...] += r

    return pl.pallas_call(
        body, grid=(s_n // tm, N_SHARD * n_blk),
        in_specs=[pl.BlockSpec((per_blk, tm, e), lambda m, g: (g, m, 0)),
                  pl.BlockSpec((None, k, tn), lambda m, g: (g // n_blk, 0, g % n_blk))],
        out_specs=pl.BlockSpec((tm, k), lambda m, g: (m, 0)),
        out_shape=jax.ShapeDtypeStruct((s_n, k), F32), compiler_params=_cparams(), name=name)(dh, w)


def _group_alpha(ls):
    m = jnp.maximum(jnp.maximum(ls[0], ls[1]), ls[2])
    es = [jnp.exp(l - m) for l in ls]
    tot = es[0] + es[1] + es[2]
    return [ex / tot for ex in es]


def _dil_mix_fwd(os_, ls_, *, name):
    h_n, s_n, e = os_[0].shape
    tm = 512

    def body(o0, o1, o2, l0, l1, l2, out_ref):
        for hh in range(2):
            al = _group_alpha([l[hh] for l in (l0, l1, l2)])
            mix = al[0] * o0[hh] + al[1] * o1[hh] + al[2] * o2[hh]
            out_ref[:, hh * e:(hh + 1) * e] = mix.astype(out_ref.dtype)

    blk = lambda w: pl.BlockSpec((2, tm, w), lambda h, i: (h, i, 0))
    return pl.pallas_call(body, grid=(h_n // 2, s_n // tm), in_specs=[blk(e)] * 3 + [blk(1)] * 3,
                          out_specs=pl.BlockSpec((tm, 2 * e), lambda h, i: (i, h)),
                          out_shape=jax.ShapeDtypeStruct((s_n, h_n * e), BF16), compiler_params=_cparams(),
                          name=name)(*os_, *ls_)


def _dil_mix_bwd(do_flat, os_, ls_, *, name):
    h_n, s_n, e = os_[0].shape
    tm = 512

    def body(do_ref, o0, o1, o2, l0, l1, l2, d0, d1, d2, t0, t1, t2):
        for hh in range(2):
            al = _group_alpha([l[hh] for l in (l0, l1, l2)])
            do_v = do_ref[:, hh * e:(hh + 1) * e]
            mix = al[0] * o0[hh] + al[1] * o1[hh] + al[2] * o2[hh]
            dbar = jnp.sum(do_v * mix, axis=-1, keepdims=True)
            for a_g, d_ref, t_ref in zip(al, (d0, d1, d2), (t0, t1, t2)):
                d_ref[hh] = a_g * do_v
                t_ref[hh] = a_g * dbar

    blk = lambda w: pl.BlockSpec((2, tm, w), lambda h, i: (h, i, 0))
    sd_e = jax.ShapeDtypeStruct((h_n, s_n, e), F32)
    sd_1 = jax.ShapeDtypeStruct((h_n, s_n, 1), F32)
    outs = pl.pallas_call(body, grid=(h_n // 2, s_n // tm),
                          in_specs=[pl.BlockSpec((tm, 2 * e), lambda h, i: (i, h))] + [blk(e)] * 3 + [blk(1)] * 3,
                          out_specs=[blk(e)] * 3 + [blk(1)] * 3, out_shape=[sd_e] * 3 + [sd_1] * 3,
                          compiler_params=_cparams(), name=name)(do_flat, *os_, *ls_)
    return outs[:3], outs[3:]


def _bias_grad(ds, bucket, *, name):
    h_n = ds.shape[0]

    def body(ds_ref, bk_ref, o_ref):
        ds_v = ds_ref[...]
        bk = bk_ref[...]
        lane = lax.broadcasted_iota(jnp.int32, (1, N_BUCKETS), 1)
        acc = jnp.zeros((1, N_BUCKETS), F32)
        for b in range(N_BUCKETS):
            tot = jnp.sum(jnp.sum(jnp.where(bk == b, ds_v, 0.0), axis=1, keepdims=True), axis=0, keepdims=True)
            acc = acc + jnp.where(lane == b, tot, 0.0)
        o_ref[...] = acc

    return pl.pallas_call(
        body, grid=(h_n,),
        in_specs=[pl.BlockSpec((None, DIL_BLOCK, 2 * DIL_BLOCK), lambda h: (h, 0, 0)),
                  pl.BlockSpec((DIL_BLOCK, 2 * DIL_BLOCK), lambda h: (0, 0))],
        out_specs=pl.BlockSpec((None, 1, N_BUCKETS), lambda h: (h, 0, 0)),
        out_shape=jax.ShapeDtypeStruct((h_n, 1, N_BUCKETS), F32), compiler_params=_cparams(), name=name)(ds, bucket)


def _bias_table(rb, bucket, *, name):
    h_n = rb.shape[0]

    def body(rb_ref, bk_ref, o_ref):
        bk = bk_ref[...]
        row = rb_ref[...]
        acc = jnp.zeros(bk.shape, F32)
        for b in range(N_BUCKETS):
            acc = jnp.where(bk == b, row[:, b:b + 1], acc)
        o_ref[...] = acc

    return pl.pallas_call(
        body, grid=(h_n,),
        in_specs=[pl.BlockSpec((None, 1, N_BUCKETS), lambda h: (h, 0, 0)),
                  pl.BlockSpec((DIL_BLOCK, 2 * DIL_BLOCK), lambda h: (0, 0))],
        out_specs=pl.BlockSpec((None, DIL_BLOCK, 2 * DIL_BLOCK), lambda h: (h, 0, 0)),
        out_shape=jax.ShapeDtypeStruct((h_n, DIL_BLOCK, 2 * DIL_BLOCK), F32), compiler_params=_cparams(),
        name=name)(rb, bucket)


def _row_tile(rows, cols, budget=2 << 20):
    if rows * cols * 4 <= budget or rows % 8:
        return rows
    best = 8
    for t in range(8, rows + 1, 8):
        if rows % t == 0 and t * cols * 4 <= budget:
            best = t
    return best


def _adamw(w, g, m, v, *, name):
    shape = w.shape
    cols = shape[-1]
    rows = math.prod(shape[:-1]) if len(shape) > 1 else 1
    to2 = lambda t: t.reshape(rows, cols)
    tr = _row_tile(rows, cols)
    c1 = 1.0 / (1.0 - ADAM_B1 ** ADAM_STEP)
    c2 = 1.0 / (1.0 - ADAM_B2 ** ADAM_STEP)

    def body(w_ref, g_ref, m_ref, v_ref, d_ref, nm_ref, nv_ref):
        g_v = g_ref[...]
        nm = ADAM_B1 * m_ref[...] + (1.0 - ADAM_B1) * g_v
        nv = ADAM_B2 * v_ref[...] + (1.0 - ADAM_B2) * (g_v * g_v)
        m_hat = nm * c1
        v_hat = nv * c2
        d_ref[...] = -ADAM_LR * (m_hat / (jnp.sqrt(v_hat) + ADAM_EPS) + ADAM_WD * w_ref[...])
        nm_ref[...] = nm
        nv_ref[...] = nv

    blk = pl.BlockSpec((tr, cols), lambda i: (i, 0))
    sd = jax.ShapeDtypeStruct((rows, cols), F32)
    outs = pl.pallas_call(body, grid=(rows // tr,), in_specs=[blk] * 4, out_specs=[blk] * 3, out_shape=[sd] * 3,
                          compiler_params=_cparams(), name=name)(to2(w), to2(g), to2(m), to2(v))
    return tuple(t.reshape(shape) for t in outs)


def _add_half(unit, got, half_idx, *, name):
    rest = unit.shape[2:]
    c = rest[-1]
    r = math.prod(rest[:-1])
    tr = _row_tile(r, c)

    def body(idx_ref, u_ref, g_ref, o_ref, w_ref):
        tot = u_ref[...] + g_ref[...].astype(F32)
        o_ref[...] = tot
        w_ref[...] = tot.astype(BF16)

    blk = pl.BlockSpec((None, tr, c), lambda s, i, idx: (s, i, 0))
    grid_spec = pltpu.PrefetchScalarGridSpec(
        num_scalar_prefetch=1, grid=(N_SHARD, r // tr),
        in_specs=[pl.BlockSpec((None, None, tr, c), lambda s, i, idx: (idx[0], s, i, 0)), blk],
        out_specs=[blk, blk])
    out, wire = pl.pallas_call(
        body, grid_spec=grid_spec,
        out_shape=[jax.ShapeDtypeStruct((N_SHARD, r, c), F32), jax.ShapeDtypeStruct((N_SHARD, r, c), BF16)],
        compiler_params=_cparams(), name=name)(half_idx, unit.reshape(2, N_SHARD, r, c), got.reshape(N_SHARD, r, c))
    return out.reshape((N_SHARD,) + rest), wire.reshape((N_SHARD,) + rest)


def _add_shards(part, got, shard_idx, *, name):
    rest = part.shape[1:]
    c = rest[-1]
    r = math.prod(rest[:-1])
    tr = _row_tile(r, c)

    def body(idx_ref, p_ref, g_ref, o_ref):
        acc = p_ref[...]
        for k in range(3):
            acc = acc + g_ref[k].astype(F32)
        o_ref[...] = acc

    grid_spec = pltpu.PrefetchScalarGridSpec(
        num_scalar_prefetch=1, grid=(r // tr,),
        in_specs=[pl.BlockSpec((None, tr, c), lambda i, idx: (idx[0], i, 0)),
                  pl.BlockSpec((3, tr, c), lambda i, idx: (0, i, 0))],
        out_specs=pl.BlockSpec((tr, c), lambda i, idx: (i, 0)))
    out = pl.pallas_call(body, grid_spec=grid_spec, out_shape=jax.ShapeDtypeStruct((r, c), F32),
                         compiler_params=_cparams(), name=name)(
        shard_idx, part.reshape(N_SHARD, r, c), got.reshape(3, r, c))
    return out.reshape(rest)


def _sum_devices(x, n_dev, *, name):
    rows = x.shape[0] // n_dev

    def body(x_ref, o_ref):
        acc = x_ref[0:rows, :]
        for d in range(1, n_dev):
            acc = acc + x_ref[d * rows:(d + 1) * rows, :]
        o_ref[...] = acc

    return pl.pallas_call(body, out_shape=jax.ShapeDtypeStruct((rows, x.shape[1]), F32), name=name)(x)


def _my_pos():
    return lax.axis_index("x"), lax.axis_index("y"), lax.axis_index("c")


def _all_gather(x_blk, *, name, in_vmem):
    m_per, n = x_blk.shape

    def body(x_ref, out_ref, send_sems, recv_sems, local_sem):
        x, y, c = _my_pos()
        me, sibling = (x, y, c), (x, y, 1 - c)
        chips = [(1 - x, y), (x, 1 - y), (1 - x, 1 - y)]

        def rows(px, py, pc):
            return out_ref.at[pl.ds((4 * px + 2 * py + pc) * m_per, m_per), :]

        def copy(k, block, to, src=None):
            return pltpu.make_async_remote_copy(
                src_ref=rows(*block) if src is None else src, dst_ref=rows(*block),
                send_sem=send_sems.at[k], recv_sem=recv_sems.at[k], device_id=to, device_id_type=MESH)

        mine = pltpu.make_async_copy(x_ref, rows(*me), local_sem)
        mine.start()
        first = [copy(0, me, sibling, src=x_ref)]
        first += [copy(1 + j, me, (*chip, c), src=x_ref) for j, chip in enumerate(chips)]
        for cp in first:
            cp.start()
        passed = [copy(4 + j, (*chip, c), sibling) for j, chip in enumerate(chips)]
        for j, chip in enumerate(chips):
            copy(1 + j, (*chip, c), me).wait_recv()
            passed[j].start()
        copy(0, sibling, me).wait_recv()
        for j, chip in enumerate(chips):
            copy(4 + j, (*chip, 1 - c), me).wait_recv()
        for cp in first + passed:
            cp.wait_send()
        mine.wait()

    space = pltpu.VMEM if in_vmem else pl.ANY
    return pl.pallas_call(
        body, out_shape=jax.ShapeDtypeStruct((8 * m_per, n), x_blk.dtype),
        in_specs=[pl.BlockSpec(memory_space=space)], out_specs=pl.BlockSpec(memory_space=space),
        scratch_shapes=[pltpu.SemaphoreType.DMA((7,)), pltpu.SemaphoreType.DMA((7,)), pltpu.SemaphoreType.DMA],
        name=name)(x_blk)


_HBM = pl.BlockSpec(memory_space=pl.ANY)


def _gather_weights(fams, *, name):
    n = len(fams)

    def body(*refs):
        ins, outs = refs[:n], refs[n:2 * n]
        send_sems, recv_sems = refs[2 * n:]
        x, y, c = _my_pos()
        me, sibling = (x, y, c), (x, y, 1 - c)
        chips = [(1 - x, y), (x, 1 - y), (1 - x, 1 - y)]

        def copy(f, k, block, to, src=None):
            px, py, pc = block
            dst = outs[f].at[2 * px + py, pc]
            return pltpu.make_async_remote_copy(
                src_ref=dst if src is None else src, dst_ref=dst, send_sem=send_sems.at[7 * f + k],
                recv_sem=recv_sems.at[7 * f + k], device_id=to, device_id_type=MESH)

        first, passed = [], []
        for f in range(n):
            src = ins[f].at[c]
            first.append(copy(f, 0, me, sibling, src=src))
            first += [copy(f, 1 + j, me, (*chip, c), src=src) for j, chip in enumerate(chips)]
        for cp in first:
            cp.start()
        for j, chip in enumerate(chips):
            for f in range(n):
                copy(f, 1 + j, (*chip, c), me).wait_recv()
                passed.append(copy(f, 4 + j, (*chip, c), sibling))
                passed[-1].start()
        for f in range(n):
            copy(f, 0, sibling, me).wait_recv()
        for j, chip in enumerate(chips):
            for f in range(n):
                copy(f, 4 + j, (*chip, 1 - c), me).wait_recv()
        for cp in first + passed:
            cp.wait_send()

    outs = pl.pallas_call(
        body, out_shape=[jax.ShapeDtypeStruct((N_SHARD,) + t.shape, t.dtype) for t in fams],
        in_specs=[_HBM] * n, out_specs=[_HBM] * n,
        scratch_shapes=[pltpu.SemaphoreType.DMA((7 * n,)), pltpu.SemaphoreType.DMA((7 * n,))], name=name)(*fams)
    return [_place_own(o, t) for o, t in zip(outs, fams)]


def _pair_gather(halves, *, name):
    n = len(halves)

    def body(*refs):
        ins, outs = refs[:n], refs[n:2 * n]
        send_sems, recv_sems = refs[2 * n:]
        x, y, c = _my_pos()
        cps = [pltpu.make_async_remote_copy(src_ref=ins[f], dst_ref=outs[f].at[c], send_sem=send_sems.at[f],
                                            recv_sem=recv_sems.at[f], device_id=(x, y, 1 - c), device_id_type=MESH)
               for f in range(n)]
        for cp in cps:
            cp.start()
        for f in range(n):
            pltpu.make_async_remote_copy(src_ref=ins[f], dst_ref=outs[f].at[1 - c], send_sem=send_sems.at[f],
                                         recv_sem=recv_sems.at[f], device_id=(x, y, 1 - c),
                                         device_id_type=MESH).wait_recv()
        for cp in cps:
            cp.wait_send()

    outs = pl.pallas_call(
        body, out_shape=[jax.ShapeDtypeStruct((2,) + t.shape, t.dtype) for t in halves],
        in_specs=[_HBM] * n, out_specs=[_HBM] * n,
        scratch_shapes=[pltpu.SemaphoreType.DMA((n,)), pltpu.SemaphoreType.DMA((n,))], name=name)(*halves)
    c = lax.axis_index("c")
    return [lax.dynamic_update_index_in_dim(o, t, c, 0) for o, t in zip(outs, halves)]


_HBM_ONLY = pl.BlockSpec(memory_space=pltpu.HBM)
_SEMS = pl.BlockSpec(memory_space=pltpu.SEMAPHORE)
_EFFECT = pltpu.SideEffectType.DATAFLOW_SIDE_EFFECTING


def _copies_start(srcs, lands, plan, n_copies, *, name):
    n, m = len(srcs), len(lands)

    def body(*refs):
        src_refs, land_refs = refs[:n], refs[n:n + m]
        send_sems, recv_sems, token = refs[n + m], refs[n + m + 1], refs[-1]
        for k, (src, dst, peer) in enumerate(plan(src_refs, land_refs)):
            pltpu.make_async_remote_copy(src_ref=src, dst_ref=dst, send_sem=send_sems.at[k], recv_sem=recv_sems.at[k],
                                         device_id=peer, device_id_type=MESH).start()
        token[...] = jnp.zeros_like(token)

    bufs = [pltpu.with_memory_space_constraint(t, pltpu.HBM) for t in (*srcs, *lands)]
    outs = pl.pallas_call(
        body, name=name,
        out_shape=(pltpu.SemaphoreType.DMA((n_copies,)), pltpu.SemaphoreType.DMA((n_copies,)),
                   *[pltpu.HBM(t.shape, t.dtype) for t in bufs], jax.ShapeDtypeStruct((8, 128), F32)),
        in_specs=[_HBM_ONLY] * (n + m),
        out_specs=(_SEMS, _SEMS, *[_HBM_ONLY] * (n + m), pl.BlockSpec(memory_space=pltpu.VMEM)),
        input_output_aliases={k: 2 + k for k in range(n + m)},
        compiler_params=pltpu.CompilerParams(has_side_effects=_EFFECT))(*bufs)
    return outs[0], outs[1], list(outs[2:2 + n + m]), outs[-1]


def _copies_wait(send_sems, recv_sems, thru, n_src, plan, after, *, name):
    nm = len(thru)

    def body(*refs):
        t_refs, send, recv = refs[:nm], refs[nm], refs[nm + 1]
        for k, (src, dst, peer) in enumerate(plan(t_refs[:n_src], t_refs[n_src:])):
            cp = pltpu.make_async_remote_copy(src_ref=src, dst_ref=dst, send_sem=send.at[k], recv_sem=recv.at[k],
                                              device_id=peer, device_id_type=MESH)
            cp.wait_send()
            cp.wait_recv()

    outs = pl.pallas_call(
        body, name=name, out_shape=tuple(pltpu.HBM(t.shape, t.dtype) for t in thru),
        in_specs=[_HBM_ONLY] * nm + [_SEMS, _SEMS, pl.BlockSpec(memory_space=pl.ANY)],
        out_specs=tuple([_HBM_ONLY] * nm), input_output_aliases={k: k for k in range(nm)},
        compiler_params=pltpu.CompilerParams(has_side_effects=_EFFECT))(*thru, send_sems, recv_sems, after)
    return list(outs)


_RELATIONS = [(dx, dy, dc) for dx in (0, 1) for dy in (0, 1) for dc in (0, 1)][1:]


def _gather_plan(src_refs, land_refs):
    x, y, c = _my_pos()
    flip = lambda v, d: 1 - v if d else v
    return [(s_ref.at[c], l_ref.at[2 * x + y, c], (flip(x, dx), flip(y, dy), flip(c, dc)))
            for s_ref, l_ref in zip(src_refs, land_refs) for dx, dy, dc in _RELATIONS]


def _gather_chips_plan(src_refs, land_refs):
    x, y, c = _my_pos()
    peers = [(x, y, 1 - c), (1 - x, y, c), (x, 1 - y, c), (1 - x, 1 - y, c)]
    return [(s_ref.at[c], l_ref.at[2 * x + y, c], peer) for s_ref, l_ref in zip(src_refs, land_refs) for peer in peers]


def _gather_pass_plan(src_refs, land_refs):
    x, y, c = _my_pos()
    chips = [(1 - x, y), (x, 1 - y), (1 - x, 1 - y)]
    return [(l_ref.at[2 * cx + cy, c], l_ref.at[2 * cx + cy, c], (x, y, 1 - c))
            for l_ref in land_refs for cx, cy in chips]


def _sibling_plan(src_refs, land_refs):
    x, y, c = _my_pos()
    return [(s_ref.at[1 - c], l_ref, (x, y, 1 - c)) for s_ref, l_ref in zip(src_refs, land_refs)]


def _chips_plan(src_refs, land_refs):
    x, y, c = _my_pos()
    chips = [(1 - x, y), (x, 1 - y), (1 - x, 1 - y)]
    return [(s_ref.at[2 * cx + cy], l_ref.at[k], (cx, cy, c))
            for s_ref, l_ref in zip(src_refs, land_refs) for k, (cx, cy) in enumerate(chips)]


def _place_own(gathered, fam):
    x, y, c = _my_pos()
    own = lax.dynamic_index_in_dim(fam, c, 0, keepdims=True)[None]
    return lax.dynamic_update_slice(gathered, own, (2 * x + y, c) + (0,) * (fam.ndim - 1))


def _to_heads(t, width):
    return t.reshape(t.shape[0], HEADS, width).transpose(1, 0, 2)


def _from_heads(t):
    return t.transpose(1, 0, 2).reshape(t.shape[1], -1)


def _t5_bucket(dist):
    max_exact = N_BUCKETS // 2
    d = jnp.maximum(dist, 1).astype(F32)
    large = max_exact + (jnp.log(d / max_exact) / math.log(MAX_DISTANCE / max_exact)
                         * (N_BUCKETS - max_exact)).astype(jnp.int32)
    large = jnp.minimum(large, N_BUCKETS - 1)
    return jnp.where(dist < max_exact, dist, large)


def _bucket_map(dilation):
    iq = jnp.arange(DIL_BLOCK)[:, None]
    ik = jnp.arange(2 * DIL_BLOCK)[None, :]
    rel = DIL_BLOCK + iq - ik
    return _t5_bucket(jnp.maximum(rel, 0) * dilation).astype(jnp.int32)


def _q_perm(w):
    w3 = w.reshape(w.shape[0], HEADS, QK_NOPE + QK_ROPE)
    return jnp.concatenate([w3[:, :, :QK_NOPE].reshape(w.shape[0], -1),
                            w3[:, :, QK_NOPE:QK_NOPE + HALF_ROPE].reshape(w.shape[0], -1),
                            w3[:, :, QK_NOPE + HALF_ROPE:].reshape(w.shape[0], -1)], axis=1)


def _q_unperm(w):
    n0, n1 = HEADS * QK_NOPE, HEADS * HALF_ROPE
    r = w.shape[0]
    return jnp.concatenate([w[:, :n0].reshape(r, HEADS, QK_NOPE), w[:, n0:n0 + n1].reshape(r, HEADS, HALF_ROPE),
                            w[:, n0 + n1:].reshape(r, HEADS, HALF_ROPE)], axis=2).reshape(r, -1)


def _kv_perm(w):
    w3 = w.reshape(w.shape[0], HEADS, QK_NOPE + V_HEAD)
    return jnp.concatenate([w3[:, :, :QK_NOPE].reshape(w.shape[0], -1), w3[:, :, QK_NOPE:].reshape(w.shape[0], -1)],
                           axis=1)


def _kv_unperm(w):
    n0 = HEADS * QK_NOPE
    r = w.shape[0]
    return jnp.concatenate([w[:, :n0].reshape(r, HEADS, QK_NOPE), w[:, n0:].reshape(r, HEADS, V_HEAD)],
                           axis=2).reshape(r, -1)


def _row(v):
    return v.reshape(1, -1)


def kernel(x, c, norm_pre, norm_post, w_mod, b_mod, ffn_w_gate, ffn_w_up, ffn_w_down, mla_w_in, mla_q_norm, mla_w_q_up, mla_kv_norm, mla_w_kv_up, mla_w_o, dil_w_in, dil_w_o, rel_bias, loss_target, m_norm_pre, m_norm_post, m_w_mod, m_b_mod, m_ffn_w_gate, m_ffn_w_up, m_ffn_w_down, m_mla_w_in, m_mla_q_norm, m_mla_w_q_up, m_mla_kv_norm, m_mla_w_kv_up, m_mla_w_o, m_dil_w_in, m_dil_w_o, m_rel_bias, v_norm_pre, v_norm_post, v_w_mod, v_b_mod, v_ffn_w_gate, v_ffn_w_up, v_ffn_w_down, v_mla_w_in, v_mla_q_norm, v_mla_w_q_up, v_mla_kv_norm, v_mla_w_kv_up, v_mla_w_o, v_dil_w_in, v_dil_w_o, v_rel_bias):
    given = dict(locals())
    ix, iy, ic = _my_pos()
    shard_id = 2 * ix + iy
    dev_id = 4 * ix + 2 * iy + ic
    x2 = x[0]
    target = loss_target[0]
    half_idx = jnp.reshape(ic, (1,)).astype(jnp.int32)
    shard_idx = jnp.reshape(shard_id, (1,)).astype(jnp.int32)

    blk = jnp.zeros((8, D_MODEL), F32)
    blk = blk.at[0].set(c[0])
    blk = blk.at[1:3].set(jnp.pad(norm_pre.reshape(-1), (0, 512)).reshape(2, D_MODEL))
    blk = blk.at[3:5].set(jnp.pad(norm_post.reshape(-1), (0, 512)).reshape(2, D_MODEL))
    got = _all_gather(blk, name="ag_c_norms", in_vmem=True).reshape(N_SHARD, 2, 8, D_MODEL)
    c_all = got[:, :, 0, :].reshape(8, D_MODEL)

    def full_norm(lo):
        t = got[:, 0, lo:lo + 2, :].reshape(N_SHARD, 2 * D_MODEL)[:, :1536].reshape(N_SHARD, 2, 3, 256)
        return t.transpose(1, 2, 0, 3).reshape(2, 3, D_MODEL)

    pre_full, post_full = full_norm(1), full_norm(3)

    silu_c = _silu_bf16(c_all, name="silu_c")
    b_cols = lax.dynamic_slice_in_dim(b_mod, shard_id * 2304, 2304, axis=1).reshape(2, 1, 2304)
    mod_part = _mm(silu_c, w_mod, bias=b_cols, name="mod_mm", tn_cap=768)
    mod_all = _all_gather(mod_part.reshape(16, 2304), name="ag_mod", in_vmem=True)
    mod_all = mod_all.reshape(N_SHARD, 2, 2, 8, 2304)[:, 0]
    mod_mine = lax.dynamic_index_in_dim(mod_all, dev_id, axis=2, keepdims=False)
    mod = mod_mine.transpose(1, 0, 2).reshape(2, 9, D_MODEL)

    bf = lambda t: t.astype(BF16)
    ffn_fam = lambda i, h: [bf(jnp.stack([ffn_w_gate[i, h], ffn_w_up[i, h]])),
                            bf(ffn_w_down[i, h].reshape(2, F_SHARD // 2, D_MODEL))]
    mla_fam = [bf(mla_w_in.reshape(2, 128, -1)), bf(mla_w_q_up.reshape(2, 192, -1)),
               bf(mla_w_kv_up.reshape(2, 128, -1)), bf(mla_w_o.reshape(2, 128, D_MODEL))]
    dil_fam = [bf(dil_w_in.reshape(2, 512, -1)), bf(dil_w_o.reshape(2, 128, D_MODEL))]
    later_fams = [ffn_fam(0, 1), ffn_fam(1, 0) + dil_fam, ffn_fam(1, 1)]
    full, later_fams, mod = lax.optimization_barrier(
        (_gather_weights(ffn_fam(0, 0) + mla_fam, name="ag_weights_first"), later_fams, mod))

    def gather_later(fams, tag):
        lands = [lax.empty((N_SHARD,) + t.shape, t.dtype) for t in fams]
        send, recv, thru, token = _copies_start(fams, lands, _gather_plan, 7 * len(fams), name=f"ag_start_{tag}")
        return dict(send=send, recv=recv, thru=thru, token=token, n=len(fams), tag=tag)

    def arrive(st, after):
        thru = _copies_wait(st['send'], st['recv'], st['thru'], st['n'], _gather_plan, after,
                            name=f"ag_wait_{st['tag']}")
        return [_place_own(o, t) for t, o in zip(thru[:st['n']], thru[st['n']:])]

    def gather_chips(fams, tag):
        lands = [lax.empty((N_SHARD,) + t.shape, t.dtype) for t in fams]
        send, recv, thru, token = _copies_start(fams, lands, _gather_chips_plan, 4 * len(fams), name=f"ag_start_{tag}")
        return dict(send=send, recv=recv, thru=thru, token=token, n=len(fams), tag=tag)

    def pass_on(st, after):
        n, tag = st['n'], st['tag']
        thru = _copies_wait(st['send'], st['recv'], st['thru'], n, _gather_chips_plan, after, name=f"ag_mid_{tag}")
        send, recv, lands, token = _copies_start([], thru[n:], _gather_pass_plan, 3 * n, name=f"ag_pass_{tag}")
        return dict(send=send, recv=recv, thru=lands, fams=thru[:n], tag=tag), token[0, 0]

    def arrive_passed(st, after):
        lands = _copies_wait(st['send'], st['recv'], st['thru'], 0, _gather_pass_plan, after, name=f"ag_wait_{st['tag']}")
        return [_place_own(o, t) for t, o in zip(st['fams'], lands)]

    flight_a = gather_later(later_fams[0], "l0s2")
    _, next_fams = lax.optimization_barrier((flight_a['token'], later_fams[1]))
    flight_b = gather_chips(next_fams, "l1s01")
    as_ffn = lambda w_gu, w_dn: (w_gu, w_dn.reshape(N_SHARD, F_SHARD, D_MODEL))
    ffn_w = {(0, 0): as_ffn(full[0], full[1])}
    w_in = full[2].reshape(D_MODEL, -1)
    wq_p = _q_perm(full[3].reshape(N_SHARD, Q_LORA, -1).transpose(1, 0, 2).reshape(Q_LORA, -1))
    wkv_p = _kv_perm(full[4].reshape(N_SHARD, KV_LORA, -1).transpose(1, 0, 2).reshape(KV_LORA, -1))
    w_mo = full[5].reshape(D_MODEL, D_MODEL)
    dil_w = {}

    pos = jnp.arange(SEQ, dtype=F32)
    freqs = ROPE_THETA ** (-jnp.arange(HALF_ROPE, dtype=F32) / HALF_ROPE)
    ang = pos[:, None] * freqs[None, :]
    cos_k, sin_k = jnp.cos(ang), jnp.sin(ang)
    cos_q, sin_q = jnp.tile(cos_k, (1, HEADS)), jnp.tile(sin_k, (1, HEADS))

    buckets = [_bucket_map(d) for _, d in DIL_GROUPS]
    biases = [_bias_table(rel_bias[:, g * HEADS:(g + 1) * HEADS].T.reshape(HEADS, 1, N_BUCKETS), bk,
                          name=f"dil_bias_table_g{g}") for g, bk in enumerate(buckets)]

    vpacks = jnp.concatenate([pre_full[:, :, None], post_full[:, :, None], mod.reshape(2, 3, 3, D_MODEL),
                              jnp.zeros((2, 3, 3, D_MODEL), F32)], axis=2)
    sub_params = lambda i, sub: vpacks[i, sub]

    def ffn_fwd(xin, i, h, sub, tie=None, mid=None):
        p = sub_params(i, sub)
        if tie is not None:
            p = p + tie
        tag = f"l{i}s{sub}"
        w_gu, w_dn = ffn_w[i, h]
        hn = _pre_fwd(xin, p, name=f"pre_fwd_{tag}")
        gu, a = _ffn_up(hn, w_gu, name=f"ffn_up_{tag}")
        if mid is not None:
            p = p + mid(a)
        f, out = _ffn_down(a, w_dn, xin, p, FFN_RES, name=f"ffn_down_{tag}")
        return out, dict(x=xin, hn=hn, gu=gu, a=a, f=f, p=p, i=i, h=h, tag=tag)

    def mla_fwd(xin, i, sub):
        p = sub_params(i, sub)
        tag = f"l{i}s{sub}"
        hn = _pre_fwd(xin, p, name=f"pre_fwd_{tag}")
        lat = _mm(hn, w_in, name="mla_lat")
        cq, ckv = lat[:, :Q_LORA], lat[:, Q_LORA:Q_LORA + KV_LORA]
        k1, k2 = lat[:, Q_LORA + KV_LORA:Q_LORA + KV_LORA + HALF_ROPE], lat[:, Q_LORA + KV_LORA + HALF_ROPE:]
        cqn = _rms_fwd(cq, mla_q_norm, name="mla_qnorm")
        ckvn = _rms_fwd(ckv, mla_kv_norm, name="mla_kvnorm")
        qp = _mm(cqn, wq_p, name="mla_q_up")
        kvp = _mm(ckvn, wkv_p, name="mla_kv_up")
        n0, n1 = HEADS * QK_NOPE, HEADS * HALF_ROPE
        qr1, qr2 = _rope(qp[:, n0:n0 + n1], qp[:, n0 + n1:], cos_q, sin_q, name="rope_q")
        kr1, kr2 = _rope(k1, k2, cos_k, sin_k, name="rope_k")
        q = jnp.concatenate([qp[:, :n0].reshape(SEQ, HEADS, QK_NOPE), qr1.reshape(SEQ, HEADS, HALF_ROPE),
                             qr2.reshape(SEQ, HEADS, HALF_ROPE)], axis=2).transpose(1, 0, 2).astype(BF16)
        kr = jnp.broadcast_to(jnp.concatenate([kr1, kr2], axis=1)[:, None, :], (SEQ, HEADS, QK_ROPE))
        k = jnp.concatenate([kvp[:, :n0].reshape(SEQ, HEADS, QK_NOPE), kr], axis=2).transpose(1, 0, 2).astype(BF16)
        v = _to_heads(kvp[:, n0:], V_HEAD).astype(BF16)
        o, lse = _mla_attn_fwd(q, k, v, name="mla_attn_fwd")
        o_flat = _from_heads(o).astype(BF16)
        f = _mm(o_flat, w_mo, name="mla_out")
        out = _post_fwd(f, xin, p, 1.0, name=f"post_fwd_{tag}")
        return out, dict(x=xin, hn=hn, cq=cq, ckv=ckv, cqn=cqn, ckvn=ckvn, q=q, k=k, v=v, o=o, lse=lse,
                         o_flat=o_flat, f=f, p=p, tag=tag)

    def dil_fwd(xin, i, sub):
        p = sub_params(i, sub)
        tag = f"l{i}s{sub}"
        hn = _pre_fwd(xin, p, name=f"pre_fwd_{tag}")
        heads = _proj_heads(hn, dil_w['in'], name="dil_proj")
        outs, lses = [], []
        for g, (window, d) in enumerate(DIL_GROUPS):
            o, lse = _dil_attn_fwd(heads, biases[g], g, d, name=f"dil_attn_fwd_g{g}")
            outs.append(o)
            lses.append(lse)
        o_flat = _dil_mix_fwd(outs, lses, name="dil_mix_fwd")
        f = _mm(o_flat, dil_w['out'], name="dil_out")
        out = _post_fwd(f, xin, p, 1.0, name=f"post_fwd_{tag}")
        return out, dict(x=xin, hn=hn, heads=heads, outs=outs, lses=lses, o_flat=o_flat, f=f, p=p, tag=tag)

    saved = [None] * 6
    xs, saved[0] = ffn_fwd(x2, 0, 0, 0, tie=flight_a['token'][0, 0] + flight_b['token'][0, 0])
    xs, saved[1] = mla_fwd(xs, 0, 1)
    ffn_w[0, 1] = as_ffn(*arrive(flight_a, xs))
    passed = {}

    def second_step(after):
        passed['st'], tok = pass_on(flight_b, after)
        return tok

    xs, saved[2] = ffn_fwd(xs, 0, 1, 2, mid=second_step)
    got, last_fams = lax.optimization_barrier((arrive_passed(passed['st'], xs), later_fams[2]))
    ffn_w[1, 0] = as_ffn(got[0], got[1])
    dil_w['in'], dil_w['out'] = got[2].reshape(N_SHARD, D_MODEL, -1), got[3].reshape(D_MODEL, D_MODEL)
    in_flight = gather_later(last_fams, "l1s2")
    xs, saved[3] = ffn_fwd(xs, 1, 0, 0, tie=in_flight['token'][0, 0])
    xs, saved[4] = dil_fwd(xs, 1, 1)
    ffn_w[1, 1] = as_ffn(*arrive(in_flight, xs))
    xs, saved[5] = ffn_fwd(xs, 1, 1, 2)

    dx, loss_part = _loss(xs, target, name="loss")

    dmod = [[None] * 9 for _ in range(2)]
    dpre = [[None] * 3 for _ in range(2)]
    dpost = [[None] * 3 for _ in range(2)]
    ffn_units = {}
    row_unit = lambda g, r, j: ((r % 2, r // 2), 0, j)

    def close_sub(dhn, dout, sv, i, sub, res_dgate, res_dqg):
        p = sv['p']
        dxs, dsh, dsc, dpg = _pre_bwd(dhn, sv['x'], dout, p, name=f"pre_bwd_{sv['tag']}")
        dmod[i][3 * sub], dmod[i][3 * sub + 1], dmod[i][3 * sub + 2] = dsh, dsc, res_dgate
        dpre[i][sub], dpost[i][sub] = dpg, res_dqg
        return dxs

    def ffn_bwd(dout, sv, sub, tie=0.0, mid=None):
        i, h, p, tag = sv['i'], sv['h'], sv['p'], sv['tag']
        w_gu, w_dn = ffn_w[i, h]
        df, dgate, dqg = _post_bwd(dout, sv['f'], p + tie, FFN_RES, name=f"post_bwd_{tag}")
        u_dn = _mm(sv['a'], df, ta=True, tn_cap=D_MODEL // 2, out_shape=(2, N_SHARD, F_SHARD, D_MODEL // 2),
                   out_sel=lambda g, r, j: ((j, g), r, 0), name=f"ffn_dwd_{tag}")
        dgu = _ffn_dgu(df, w_dn, sv['gu'], name=f"ffn_dgu_{tag}")
        if mid is not None:
            p = p + mid(dgu)
        u_gu = _mm(dgu.reshape(2 * N_SHARD, SEQ, F_SHARD), sv['hn'], ta=True,
                   out_shape=(2, N_SHARD, F_SHARD, D_MODEL), out_sel=lambda g, r, j: ((g % 2, g // 2), r, j),
                   name=f"ffn_dwgu_{tag}")
        ffn_units[i, h] = [u_gu, u_dn]
        dxs, dsh, dsc, dpg = _ffn_dhn(dgu, w_gu, sv['x'], dout, p, name=f"ffn_dhn_{tag}")
        dmod[i][3 * sub], dmod[i][3 * sub + 1], dmod[i][3 * sub + 2] = dsh, dsc, dgate
        dpre[i][sub], dpost[i][sub] = dpg, dqg
        return dxs

    def mla_bwd(dout, sv, i, sub, tie=0.0):
        p, tag = sv['p'], sv['tag']
        df, dgate, dqg = _post_bwd(dout, sv['f'], p + tie, 1.0, name=f"post_bwd_{tag}")
        u_wo = _mm(sv['o_flat'], df, ta=True, tm_cap=128, out_shape=(2, N_SHARD, 128, D_MODEL), out_sel=row_unit,
                   name="mla_dwo")
        do_flat = _mm(df, w_mo, tb=True, name="mla_do")
        do = _to_heads(do_flat, V_HEAD)
        dq, dk, dv = _mla_attn_bwd(sv['q'], sv['k'], sv['v'], sv['o'], do, sv['lse'], name="mla_attn_bwd")
        dq_t = dq.transpose(1, 0, 2)
        dqr1, dqr2 = _rope(dq_t[:, :, QK_NOPE:QK_NOPE + HALF_ROPE].reshape(SEQ, -1),
                           dq_t[:, :, QK_NOPE + HALF_ROPE:].reshape(SEQ, -1), cos_q, -sin_q, name="rope_q_bwd")
        dqp = jnp.concatenate([dq_t[:, :, :QK_NOPE].reshape(SEQ, -1), dqr1, dqr2], axis=1).astype(BF16)
        dkr = _head_sum(dk[:, :, QK_NOPE:], name="mla_dkr_sum")
        dk1, dk2 = _rope(dkr[:, :HALF_ROPE], dkr[:, HALF_ROPE:], cos_k, -sin_k, name="rope_k_bwd")
        dkvp = jnp.concatenate([_from_heads(dk[:, :, :QK_NOPE]), _from_heads(dv)], axis=1).astype(BF16)
        g_wq = _q_unperm(_mm(sv['cqn'], dqp, ta=True, name="mla_dwq"))
        g_wkv = _kv_unperm(_mm(sv['ckvn'], dkvp, ta=True, name="mla_dwkv"))
        dcqn = _mm(dqp, wq_p, tb=True, name="mla_dcqn")
        dckvn = _mm(dkvp, wkv_p, tb=True, name="mla_dckvn")
        dcq, g_qn = _rms_bwd(dcqn, sv['cq'], mla_q_norm, name="mla_qnorm_bwd")
        dckv, g_kvn = _rms_bwd(dckvn, sv['ckv'], mla_kv_norm, name="mla_kvnorm_bwd")
        dlat = jnp.concatenate([dcq, dckv, dk1, dk2], axis=1).astype(BF16)
        u_win = _mm(sv['hn'], dlat, ta=True, tm_cap=128, out_shape=(2, N_SHARD, 128, dlat.shape[1]),
                    out_sel=row_unit, name="mla_dwin")
        dhn = _mm(dlat, w_in, tb=True, name="mla_dhn")
        col_unit = lambda t: (t.reshape(t.shape[0], N_SHARD, -1).transpose(1, 0, 2)
                              .reshape(N_SHARD, 2, t.shape[0] // 2, -1).transpose(1, 0, 2, 3))
        grads = dict(units=[u_win, col_unit(g_wq), col_unit(g_wkv), u_wo], q_norm=g_qn, kv_norm=g_kvn)
        return close_sub(dhn, dout, sv, i, sub, dgate, dqg), grads

    def dil_bwd(dout, sv, i, sub):
        p, tag = sv['p'], sv['tag']
        df, dgate, dqg = _post_bwd(dout, sv['f'], p, 1.0, name=f"post_bwd_{tag}")
        u_wo = _mm(sv['o_flat'], df, ta=True, tm_cap=128, out_shape=(2, N_SHARD, 128, D_MODEL), out_sel=row_unit,
                   name="dil_dwo")
        dos, dlts = _dil_mix_bwd(_mm(df, dil_w['out'], tb=True, name="dil_do"), sv['outs'], sv['lses'],
                                 name="dil_mix_bwd")
        pieces = []
        bias_rows = []
        for g, (window, d) in enumerate(DIL_GROUPS):
            dq, dk, dv, dbias = _dil_attn_bwd(sv['heads'], biases[g], sv['lses'][g], dos[g], dlts[g], g, d,
                                              name=f"dil_attn_bwd_g{g}")
            pieces += [dq, dk, dv]
            bias_rows.append(_bias_grad(dbias, buckets[g], name=f"dil_bias_grad_g{g}")[:, 0, :])
        dheads = jnp.concatenate(pieces).astype(BF16)
        u_win = _proj_heads_dw(sv['hn'], dheads, name="dil_dwin")
        dhn = _proj_heads_dx(dheads, dil_w['in'], name="dil_dhn")
        g_bias = jnp.concatenate(bias_rows, axis=0).T
        grads = dict(units=[u_win, u_wo], rel_bias=g_bias)
        return close_sub(dhn, dout, sv, i, sub, dgate, dqg), grads

    def to_sibling(units, tag):
        n = len(units)
        send, recv, thru, token = _copies_start(units, [lax.empty(u.shape[1:], F32) for u in units], _sibling_plan, n,
                                                name=f"rs{tag}_sibling_start")
        return dict(send=send, recv=recv, thru=thru, n=n, tag=tag), token[0, 0]

    def from_sibling(st, after):
        n, tag = st['n'], st['tag']
        thru = _copies_wait(st['send'], st['recv'], st['thru'], n, _sibling_plan, after, name=f"rs{tag}_sibling_wait")
        return [_add_half(u, g, half_idx, name=f"rs{tag}_add_half_{k}") for k, (u, g) in enumerate(zip(thru[:n], thru[n:]))]

    def to_chips(parts, tag):
        n = len(parts)
        send, recv, thru, token = _copies_start([w for _, w in parts],
                                                [lax.empty((3,) + w.shape[1:], BF16) for _, w in parts], _chips_plan,
                                                3 * n, name=f"rs{tag}_chips_start")
        return dict(send=send, recv=recv, thru=thru, n=n, tag=tag, parts=parts), token[0, 0]

    def from_chips(st, after):
        n, tag = st['n'], st['tag']
        thru = _copies_wait(st['send'], st['recv'], st['thru'], n, _chips_plan, after, name=f"rs{tag}_chips_wait")
        return [_add_shards(p, g, shard_idx, name=f"rs{tag}_add_shards_{k}")
                for k, ((p, _), g) in enumerate(zip(st['parts'], thru[n:]))]

    dx = ffn_bwd(dx, saved[5], 2)
    dx, dil_g = dil_bwd(dx, saved[4], 1, 1)
    dx = ffn_bwd(dx, saved[3], 0)
    st1, tok = to_sibling([*ffn_units[1, 1], *dil_g['units'], *ffn_units[1, 0]], "1")
    dx = ffn_bwd(dx, saved[2], 2, tie=tok)
    st1, tok1 = to_chips(from_sibling(st1, dx), "1")
    st2, tok2 = to_sibling(ffn_units[0, 1], "2")
    dx, mla_g = mla_bwd(dx, saved[1], 0, 1, tie=tok1 + tok2)
    reds1 = from_chips(st1, dx)
    st2, tok = to_chips(from_sibling(st2, dx), "2")
    st3, tok3 = to_sibling(mla_g['units'], "3")
    onward = {}

    def mixer_to_chips(after):
        onward['st'], t = to_chips(from_sibling(st3, after), "3")
        return t

    dx = ffn_bwd(dx, saved[0], 0, tie=tok + tok3, mid=mixer_to_chips)
    reds2 = from_chips(st2, dx)
    reds3 = from_chips(onward['st'], dx)
    grad_x = dx[None]

    pad_row = lambda v: jnp.pad(v.reshape(-1), (0, (-v.size) % D_MODEL)).reshape(-1, D_MODEL)
    small = jnp.concatenate(
        [jnp.concatenate([dmod[i][r] for i in range(2) for r in range(9)], axis=0),
         jnp.concatenate([dpre[i][s] for i in range(2) for s in range(3)], axis=0),
         jnp.concatenate([dpost[i][s] for i in range(2) for s in range(3)], axis=0),
         pad_row(mla_g['q_norm']), pad_row(mla_g['kv_norm']), pad_row(dil_g['rel_bias']), pad_row(loss_part)], axis=0)
    small = jnp.pad(small, ((0, SMALL_ROWS - small.shape[0]), (0, 0)))
    small_all = _all_gather(small, name="ag_small_grads", in_vmem=True)
    small_sum = _sum_devices(small_all, 8, name="sum_small_grads")
    g_b_mod = small_sum[0:18].reshape(2, 9 * D_MODEL)
    my_cols = lambda t: lax.dynamic_slice_in_dim(t, shard_id * 256, 256, axis=2)
    g_norm_pre = my_cols(small_sum[18:24].reshape(2, 3, D_MODEL))
    g_norm_post = my_cols(small_sum[24:30].reshape(2, 3, D_MODEL))
    g_q_norm = small_sum[30, :Q_LORA].reshape(1, Q_LORA)
    g_kv_norm = small_sum[31, :KV_LORA].reshape(1, KV_LORA)
    g_rel_bias = small_sum[32:34].reshape(-1)[:N_BUCKETS * 48].reshape(N_BUCKETS, 48)
    loss = small_sum[34, 0]
    dmod_all = small_all.reshape(8, SMALL_ROWS, D_MODEL)[:, 0:18].reshape(8, 2, 9 * D_MODEL)
    dmod_cols = lax.dynamic_slice_in_dim(dmod_all, shard_id * 2304, 2304, axis=2).transpose(1, 0, 2)

    swap = lambda t: jnp.swapaxes(t, 2, 3)
    grads = dict(norm_pre=g_norm_pre, norm_post=g_norm_post, b_mod=g_b_mod, mla_q_norm=g_q_norm,
                 mla_kv_norm=g_kv_norm, rel_bias=g_rel_bias)
    deltas, new_m, new_v = {}, {}, {}

    def adamw(names):
        for n in names:
            view = swap if n in ('ffn_w_gate', 'ffn_w_up') else (lambda t: t)
            outs = _adamw(view(given[n]), view(grads[n]), view(given["m_" + n]), view(given["v_" + n]),
                          name=f"adamw_{n}")
            deltas[n], new_m[n], new_v[n] = (view(t) for t in outs)

    st0, tok = to_sibling(ffn_units[0, 0], "0")
    grads['w_mod'] = _mm(silu_c, (dmod_cols + tok).astype(BF16), ta=True, tn_cap=768, name="w_mod_grad")
    adamw(['w_mod'])
    st0, tok = to_chips(from_sibling(st0, deltas['w_mod']), "0")
    grads['b_mod'] = grads['b_mod'] + tok
    fin = _pair_gather(reds1 + reds2 + reds3, name="rs_pair_gather")
    for n, t in zip(['dil_w_in', 'dil_w_o', 'mla_w_in', 'mla_w_q_up', 'mla_w_kv_up', 'mla_w_o'], fin[2:4] + fin[8:12]):
        grads[n] = t.reshape(given[n].shape)
    adamw(['b_mod', 'dil_w_in', 'dil_w_o', 'mla_w_in', 'mla_w_q_up', 'mla_w_kv_up', 'mla_w_o', 'norm_pre', 'norm_post',
           'mla_q_norm', 'mla_kv_norm', 'rel_bias'])
    reds0 = from_chips(st0, deltas['dil_w_in'])
    fin0 = _pair_gather(reds0, name="rs_pair_gather_last")
    ffn_fin = {(1, 1): fin[0:2], (1, 0): fin[4:6], (0, 1): fin[6:8], (0, 0): fin0}
    per_ffn = lambda pick: jnp.stack([jnp.stack([pick(*ffn_fin[i, h]) for h in range(2)]) for i in range(2)])
    grads.update(ffn_w_gate=swap(per_ffn(lambda gu, dn: gu[0])), ffn_w_up=swap(per_ffn(lambda gu, dn: gu[1])),
                 ffn_w_down=per_ffn(lambda gu, dn: jnp.concatenate([dn[0], dn[1]], axis=1)))
    adamw(['ffn_w_gate', 'ffn_w_up', 'ffn_w_down'])
    return (loss, grad_x, *[grads[n] for n in WEIGHTS], *[deltas[n] for n in WEIGHTS],
            *[new_m[n] for n in WEIGHTS], *[new_v[n] for n in WEIGHTS])
```

```python
import math

import jax
import jax.numpy as jnp
from jax import lax
from jax.experimental import pallas as pl
from jax.experimental.pallas import tpu as pltpu

F32 = jnp.float32
BF16 = jnp.bfloat16
MESH = pl.DeviceIdType.MESH

SEQ = 2048
D_MODEL = 1024
D_FF = 2816
N_SHARD = 4
F_SHARD = D_FF // N_SHARD
EPS = 1e-6
FFN_RES = 0.5
HEADS = 16
Q_LORA, KV_LORA, QK_NOPE, QK_ROPE, V_HEAD = 384, 256, 64, 32, 64
HALF_ROPE = QK_ROPE // 2
ROPE_THETA = 10000.0
DIL_GROUPS = ((128, 1), (512, 4), (2048, 16))
DIL_BLOCK = 128
N_BUCKETS = 32
MAX_DISTANCE = 2048
ADAM_LR, ADAM_B1, ADAM_B2, ADAM_EPS, ADAM_WD, ADAM_STEP = 0.001, 0.9, 0.999, 1e-08, 0.01, 10

VMEM_LIMIT = 48 * 1024 * 1024
SMALL_ROWS = 40

WEIGHTS = ['norm_pre', 'norm_post', 'w_mod', 'b_mod', 'ffn_w_gate', 'ffn_w_up', 'ffn_w_down', 'mla_w_in',
           'mla_q_norm', 'mla_w_q_up', 'mla_kv_norm', 'mla_w_kv_up', 'mla_w_o', 'dil_w_in', 'dil_w_o', 'rel_bias']


def _cparams(**kw):
    return pltpu.CompilerParams(vmem_limit_bytes=VMEM_LIMIT, **kw)


def _pick(n, cap, mult=128):
    if n <= cap:
        return n
    best = n
    for t in range(mult, cap + 1, mult):
        if n % t == 0:
            best = t
    return best


def _mm(a, b, *, name, ta=False, tb=False, reduce_g=False, bias=None, out_dtype=F32, tm_cap=512, tn_cap=1024,
        g_n=None, b_sel=None, out_shape=None, out_sel=None, out_buf=None):
    a3 = a if a.ndim == 3 else a[None]
    ga = a3.shape[0]
    if b_sel is None:
        b_n = b if b.ndim == 3 else b[None]
        gb = b_n.shape[0]
        b_sel = (lambda g: (g,)) if gb > 1 else (lambda g: (0,))
        g_n = max(ga, gb)
    else:
        b_n = b
    k_dim, m_dim = (a3.shape[1], a3.shape[2]) if ta else (a3.shape[2], a3.shape[1])
    k2, n_dim = (b_n.shape[-1], b_n.shape[-2]) if tb else (b_n.shape[-2], b_n.shape[-1])
    assert k_dim == k2, (a.shape, b.shape)
    tm = _pick(m_dim, tm_cap, 128 if ta else 8)
    tn = _pick(n_dim, tn_cap, 128)
    mt, nt = m_dim // tm, n_dim // tn
    dims = (((0 if ta else 1,), (1 if tb else 0,)), ((), ()))

    if reduce_g:
        grid = (mt, nt, g_n)
        ids = lambda i, j, g: (g, i, j)
    else:
        grid = (g_n, mt, nt)
        ids = lambda g, i, j: (g, i, j)

    def a_map(*p):
        g, i, j = ids(*p)
        g = g if ga > 1 else 0
        return (g, 0, i) if ta else (g, i, 0)

    def b_map(*p):
        g, i, j = ids(*p)
        return (*b_sel(g), j, 0) if tb else (*b_sel(g), 0, j)

    b_lead = (None,) * (b_n.ndim - 2)
    a_spec = pl.BlockSpec((None, k_dim, tm) if ta else (None, tm, k_dim), a_map)
    b_spec = pl.BlockSpec(b_lead + ((tn, k_dim) if tb else (k_dim, tn)), b_map)
    in_specs = [a_spec, b_spec]
    operands = [a3, b_n]
    if bias is not None:
        assert not reduce_g and bias.shape == (g_n, 1, n_dim)
        in_specs.append(pl.BlockSpec((None, 1, tn), lambda g, i, j: (g, 0, j)))
        operands.append(bias)
    aliases = {}
    if out_buf is not None:
        assert tuple(out_buf.shape) == tuple(out_shape) and out_buf.dtype == out_dtype
        in_specs.append(pl.BlockSpec(memory_space=pl.ANY))
        operands.append(out_buf)
        aliases = {len(operands) - 1: 0}

    if reduce_g:
        out_spec = pl.BlockSpec((tm, tn), lambda i, j, g: (i, j))
        out_sds = jax.ShapeDtypeStruct((m_dim, n_dim), F32)
    elif out_shape is not None:
        def o_map(g, i, j):
            lead, rb, cb = out_sel(g, i, j)
            return (*lead, rb, cb)

        out_spec = pl.BlockSpec((None,) * (len(out_shape) - 2) + (tm, tn), o_map)
        out_sds = jax.ShapeDtypeStruct(tuple(out_shape), out_dtype)
    else:
        out_spec = pl.BlockSpec((None, tm, tn), lambda g, i, j: (g, i, j))
        out_sds = jax.ShapeDtypeStruct((g_n, m_dim, n_dim), out_dtype)

    def body(a_ref, b_ref, *rest):
        o_ref = rest[-1]
        r = lax.dot_general(a_ref[...].astype(BF16), b_ref[...].astype(BF16), dims, preferred_element_type=F32)
        if bias is not None:
            r = r + rest[0][...]
        if reduce_g:
            g = pl.program_id(2)

            @pl.when(g == 0)
            def _():
                o_ref[...] = r

            @pl.when(g > 0)
            def _():
                o_ref[...] += r
        else:
            o_ref[...] = r.astype(o_ref.dtype)

    out = pl.pallas_call(body, grid=grid, in_specs=in_specs, out_specs=out_spec, out_shape=out_sds,
                         input_output_aliases=aliases, compiler_params=_cparams(), name=name)(*operands)
    if not reduce_g and out_shape is None and a.ndim == 2 and b.ndim == 2:
        out = out[0]
    return out


def _rows(tm, w):
    return pl.BlockSpec((tm, w), lambda i: (i, 0))


def _vec(w):
    return pl.BlockSpec((1, w), lambda i: (0, 0))


def _rstd(v):
    return lax.rsqrt(jnp.mean(v * v, axis=-1, keepdims=True) + EPS)


V_PG, V_QG, V_SH, V_SC, V_GATE = range(5)


def _vrow(v_ref, k):
    return v_ref[k:k + 1, :]


def _vecs(w):
    return pl.BlockSpec((8, w), lambda *_: (0, 0))


def _pre_fwd(x, vp, *, name):
    s_n, w = x.shape
    tm = _pick(s_n, 512, 8)

    def body(x_ref, v_ref, o_ref):
        xv = x_ref[...]
        n = (xv * _rstd(xv)) * _vrow(v_ref, V_PG)
        o_ref[...] = (n * (1.0 + _vrow(v_ref, V_SC)) + _vrow(v_ref, V_SH)).astype(o_ref.dtype)

    return pl.pallas_call(body, grid=(s_n // tm,), in_specs=[_rows(tm, w), _vecs(w)],
                          out_specs=_rows(tm, w), out_shape=jax.ShapeDtypeStruct((s_n, w), BF16),
                          compiler_params=_cparams(), name=name)(x, vp)


def _post_fwd(f, x, vp, res_w, *, name):
    s_n, w = x.shape
    tm = _pick(s_n, 512, 8)

    def body(f_ref, x_ref, v_ref, o_ref):
        fv = f_ref[...]
        y = (fv * _rstd(fv)) * _vrow(v_ref, V_QG)
        o_ref[...] = x_ref[...] + (res_w * _vrow(v_ref, V_GATE)) * y

    return pl.pallas_call(body, grid=(s_n // tm,), in_specs=[_rows(tm, w), _rows(tm, w), _vecs(w)],
                          out_specs=_rows(tm, w), out_shape=jax.ShapeDtypeStruct((s_n, w), F32),
                          compiler_params=_cparams(), name=name)(f, x, vp)


def _post_bwd(dout, f, vp, res_w, *, name):
    s_n, w = f.shape
    tm = _pick(s_n, 512, 8)

    def body(do_ref, f_ref, v_ref, df_ref, dgate_ref, dqg_ref):
        @pl.when(pl.program_id(0) == 0)
        def _():
            dgate_ref[...] = jnp.zeros_like(dgate_ref)
            dqg_ref[...] = jnp.zeros_like(dqg_ref)

        do = do_ref[...]
        fv = f_ref[...]
        r = _rstd(fv)
        fh = fv * r
        qg_v = _vrow(v_ref, V_QG)
        dgate_ref[...] += res_w * jnp.sum(do * (fh * qg_v), axis=0, keepdims=True)
        dy = do * (res_w * _vrow(v_ref, V_GATE))
        dqg_ref[...] += jnp.sum(dy * fh, axis=0, keepdims=True)
        dfh = dy * qg_v
        df = r * (dfh - fh * jnp.mean(dfh * fh, axis=-1, keepdims=True))
        df_ref[...] = df.astype(df_ref.dtype)

    return pl.pallas_call(
        body, grid=(s_n // tm,), in_specs=[_rows(tm, w), _rows(tm, w), _vecs(w)],
        out_specs=[_rows(tm, w), _vec(w), _vec(w)],
        out_shape=[jax.ShapeDtypeStruct((s_n, w), BF16), jax.ShapeDtypeStruct((1, w), F32),
                   jax.ShapeDtypeStruct((1, w), F32)],
        compiler_params=_cparams(), name=name)(dout, f, vp)


def _pre_bwd(dhn, x, dout, vp, *, name):
    s_n, w = x.shape
    tm = _pick(s_n, 512, 8)

    def body(dhn_ref, x_ref, do_ref, v_ref, dx_ref, dsh_ref, dsc_ref, dpg_ref):
        @pl.when(pl.program_id(0) == 0)
        def _():
            dsh_ref[...] = jnp.zeros_like(dsh_ref)
            dsc_ref[...] = jnp.zeros_like(dsc_ref)
            dpg_ref[...] = jnp.zeros_like(dpg_ref)

        dhn_v = dhn_ref[...]
        xv = x_ref[...]
        r = _rstd(xv)
        xh = xv * r
        pg_v = _vrow(v_ref, V_PG)
        dsh_ref[...] += jnp.sum(dhn_v, axis=0, keepdims=True)
        dsc_ref[...] += jnp.sum(dhn_v * (xh * pg_v), axis=0, keepdims=True)
        dn = dhn_v * (1.0 + _vrow(v_ref, V_SC))
        dpg_ref[...] += jnp.sum(dn * xh, axis=0, keepdims=True)
        dxh = dn * pg_v
        dx_ref[...] = do_ref[...] + r * (dxh - xh * jnp.mean(dxh * xh, axis=-1, keepdims=True))

    vec = jax.ShapeDtypeStruct((1, w), F32)
    return pl.pallas_call(
        body, grid=(s_n // tm,), in_specs=[_rows(tm, w), _rows(tm, w), _rows(tm, w), _vecs(w)],
        out_specs=[_rows(tm, w), _vec(w), _vec(w), _vec(w)],
        out_shape=[jax.ShapeDtypeStruct((s_n, w), F32), vec, vec, vec],
        compiler_params=_cparams(), name=name)(dhn, x, dout, vp)


def _rms_fwd(x, g, *, name):
    s_n, w = x.shape
    tm = _pick(s_n, 512, 8)

    def body(x_ref, g_ref, o_ref):
        xv = x_ref[...]
        o_ref[...] = ((xv * _rstd(xv)) * g_ref[...]).astype(o_ref.dtype)

    return pl.pallas_call(body, grid=(s_n // tm,), in_specs=[_rows(tm, w), _vec(w)], out_specs=_rows(tm, w),
                          out_shape=jax.ShapeDtypeStruct((s_n, w), BF16), compiler_params=_cparams(),
                          name=name)(x, g)


def _rms_bwd(dy, x, g, *, name):
    s_n, w = x.shape
    tm = _pick(s_n, 512, 8)

    def body(dy_ref, x_ref, g_ref, dx_ref, dg_ref):
        @pl.when(pl.program_id(0) == 0)
        def _():
            dg_ref[...] = jnp.zeros_like(dg_ref)

        dy_v = dy_ref[...]
        xv = x_ref[...]
        r = _rstd(xv)
        xh = xv * r
        dg_ref[...] += jnp.sum(dy_v * xh, axis=0, keepdims=True)
        dxh = dy_v * g_ref[...]
        dx_ref[...] = r * (dxh - xh * jnp.mean(dxh * xh, axis=-1, keepdims=True))

    return pl.pallas_call(
        body, grid=(s_n // tm,), in_specs=[_rows(tm, w), _rows(tm, w), _vec(w)],
        out_specs=[_rows(tm, w), _vec(w)],
        out_shape=[jax.ShapeDtypeStruct((s_n, w), F32), jax.ShapeDtypeStruct((1, w), F32)],
        compiler_params=_cparams(), name=name)(dy, x, g)


def _rope(a1, a2, cos, sin, *, name):
    s_n, w = a1.shape
    tm = _pick(s_n, 512, 8)

    def body(a1_ref, a2_ref, c_ref, s_ref, r1_ref, r2_ref):
        u, v, c_v, s_v = a1_ref[...], a2_ref[...], c_ref[...], s_ref[...]
        r1_ref[...] = u * c_v - v * s_v
        r2_ref[...] = u * s_v + v * c_v

    sd = jax.ShapeDtypeStruct((s_n, w), F32)
    return pl.pallas_call(body, grid=(s_n // tm,), in_specs=[_rows(tm, w)] * 4, out_specs=[_rows(tm, w)] * 2,
                          out_shape=[sd, sd], compiler_params=_cparams(), name=name)(a1, a2, cos, sin)


def _silu_bf16(x, *, name):
    def body(x_ref, o_ref):
        xv = x_ref[...]
        o_ref[...] = (xv * jax.nn.sigmoid(xv)).astype(o_ref.dtype)

    return pl.pallas_call(body, out_shape=jax.ShapeDtypeStruct(x.shape, BF16), name=name)(x)


def _loss(y, target, *, name):
    s_n, w = y.shape
    tm = _pick(s_n, 512, 8)

    def body(y_ref, t_ref, dy_ref, l_ref):
        @pl.when(pl.program_id(0) == 0)
        def _():
            l_ref[...] = jnp.zeros_like(l_ref)

        e = y_ref[...] - t_ref[...]
        dy_ref[...] = e * (1.0 / w)
        row = jnp.mean(e * e, axis=-1, keepdims=True)
        l_ref[...] += 0.5 * jnp.sum(row, axis=0, keepdims=True)

    return pl.pallas_call(
        body, grid=(s_n // tm,), in_specs=[_rows(tm, w), _rows(tm, w)],
        out_specs=[_rows(tm, w), pl.BlockSpec((1, 1), lambda i: (0, 0))],
        out_shape=[jax.ShapeDtypeStruct((s_n, w), F32), jax.ShapeDtypeStruct((1, 1), F32)],
        compiler_params=_cparams(), name=name)(y, target)


FFN_TM = 512
FFN_TM_WIDE = 1024


def _ffn_up(hn, w_gu, *, name):
    s_n, d = hn.shape
    f = w_gu.shape[-1]
    tm = _pick(s_n, FFN_TM_WIDE, 8)

    def body(hn_ref, wg_ref, wu_ref, gu_ref, a_ref):
        xv = hn_ref[...]
        g = jnp.dot(xv, wg_ref[...], preferred_element_type=F32)
        u = jnp.dot(xv, wu_ref[...], preferred_element_type=F32)
        gu_ref[0] = g.astype(BF16)
        gu_ref[1] = u.astype(BF16)
        a_ref[...] = ((g * jax.nn.sigmoid(g)) * u).astype(BF16)

    w_blk = lambda t: pl.BlockSpec((None, None, d, f), lambda s, m: (s, t, 0, 0))
    return pl.pallas_call(
        body, grid=(N_SHARD, s_n // tm),
        in_specs=[pl.BlockSpec((tm, d), lambda s, m: (m, 0)), w_blk(0), w_blk(1)],
        out_specs=[pl.BlockSpec((None, 2, tm, f), lambda s, m: (s, 0, m, 0)),
                   pl.BlockSpec((None, tm, f), lambda s, m: (s, m, 0))],
        out_shape=[jax.ShapeDtypeStruct((N_SHARD, 2, s_n, f), BF16), jax.ShapeDtypeStruct((N_SHARD, s_n, f), BF16)],
        compiler_params=_cparams(), name=name)(hn, w_gu, w_gu)


def _ffn_down(a, w_dn, x, vp, res_w, *, name):
    _, s_n, f = a.shape
    d = w_dn.shape[-1]
    tm = _pick(s_n, FFN_TM, 8)
    a = a.reshape(-1, 2, s_n, f)
    w_dn = w_dn.reshape(-1, 2, f, d)
    g_n = a.shape[0]

    def body(a_ref, w_ref, x_ref, v_ref, f_ref, o_ref):
        g = pl.program_id(1)
        r = (jnp.dot(a_ref[0], w_ref[0], preferred_element_type=F32)
             + jnp.dot(a_ref[1], w_ref[1], preferred_element_type=F32))

        @pl.when(g == 0)
        def _():
            f_ref[...] = r

        @pl.when(g > 0)
        def _():
            f_ref[...] += r

        @pl.when(g == g_n - 1)
        def _():
            fv = f_ref[...]
            y = (fv * _rstd(fv)) * _vrow(v_ref, V_QG)
            o_ref[...] = x_ref[...] + (res_w * _vrow(v_ref, V_GATE)) * y

    row = pl.BlockSpec((tm, d), lambda m, g: (m, 0))
    sd = jax.ShapeDtypeStruct((s_n, d), F32)
    return pl.pallas_call(
        body, grid=(s_n // tm, g_n),
        in_specs=[pl.BlockSpec((None, 2, tm, f), lambda m, g: (g, 0, m, 0)),
                  pl.BlockSpec((None, 2, f, d), lambda m, g: (g, 0, 0, 0)), row, _vecs(d)],
        out_specs=[row, row], out_shape=[sd, sd], compiler_params=_cparams(), name=name)(a, w_dn, x, vp)


def _ffn_dhn(dgu, w_gu, x, dout, vp, *, name):
    g_n, _, s_n, f = dgu.shape
    d = w_gu.shape[-2]
    tm = _pick(s_n, FFN_TM, 8)
    nt_dims = (((1,), (1,)), ((), ()))

    def body(a_ref, w_ref, x_ref, do_ref, v_ref, dx_ref, dsh_ref, dsc_ref, dpg_ref, acc_ref):
        m, g = pl.program_id(0), pl.program_id(1)
        r = (lax.dot_general(a_ref[0], w_ref[0], nt_dims, preferred_element_type=F32)
             + lax.dot_general(a_ref[1], w_ref[1], nt_dims, preferred_element_type=F32))

        @pl.when(g == 0)
        def _():
            acc_ref[...] = r

        @pl.when(g > 0)
        def _():
            acc_ref[...] += r

        @pl.when((m == 0) & (g == 0))
        def _():
            dsh_ref[...] = jnp.zeros_like(dsh_ref)
            dsc_ref[...] = jnp.zeros_like(dsc_ref)
            dpg_ref[...] = jnp.zeros_like(dpg_ref)

        @pl.when(g == g_n - 1)
        def _():
            dhn_v = acc_ref[...]
            xv = x_ref[...]
            rs = _rstd(xv)
            xh = xv * rs
            pg_v = _vrow(v_ref, V_PG)
            dsh_ref[...] += jnp.sum(dhn_v, axis=0, keepdims=True)
            dsc_ref[...] += jnp.sum(dhn_v * (xh * pg_v), axis=0, keepdims=True)
            dn = dhn_v * (1.0 + _vrow(v_ref, V_SC))
            dpg_ref[...] += jnp.sum(dn * xh, axis=0, keepdims=True)
            dxh = dn * pg_v
            dx_ref[...] = do_ref[...] + rs * (dxh - xh * jnp.mean(dxh * xh, axis=-1, keepdims=True))

    row = pl.BlockSpec((tm, d), lambda m, g: (m, 0))
    vec = pl.BlockSpec((1, d), lambda m, g: (0, 0))
    vsd = jax.ShapeDtypeStruct((1, d), F32)
    return pl.pallas_call(
        body, grid=(s_n // tm, g_n),
        in_specs=[pl.BlockSpec((None, 2, tm, f), lambda m, g: (g, 0, m, 0)),
                  pl.BlockSpec((None, 2, d, f), lambda m, g: (g, 0, 0, 0)), row, row, _vecs(d)],
        out_specs=[row, vec, vec, vec], out_shape=[jax.ShapeDtypeStruct((s_n, d), F32), vsd, vsd, vsd],
        scratch_shapes=[pltpu.VMEM((tm, d), F32)], compiler_params=_cparams(), name=name)(dgu, w_gu, x, dout, vp)


def _ffn_dgu(df, w_dn, gu, *, name):
    s_n, d = df.shape
    f = w_dn.shape[-2]
    tm = _pick(s_n, FFN_TM_WIDE, 8)

    def body(df_ref, wd_ref, gu_ref, o_ref):
        da = lax.dot_general(df_ref[...], wd_ref[...], (((1,), (1,)), ((), ())), preferred_element_type=F32)
        g = gu_ref[0].astype(F32)
        u = gu_ref[1].astype(F32)
        sig = jax.nn.sigmoid(g)
        o_ref[0] = (da * u * (sig * (1.0 + g * (1.0 - sig)))).astype(BF16)
        o_ref[1] = (da * (g * sig)).astype(BF16)

    gu_blk = pl.BlockSpec((None, 2, tm, f), lambda s, m: (s, 0, m, 0))
    return pl.pallas_call(
        body, grid=(N_SHARD, s_n // tm),
        in_specs=[pl.BlockSpec((tm, d), lambda s, m: (m, 0)),
                  pl.BlockSpec((None, f, d), lambda s, m: (s, 0, 0)), gu_blk],
        out_specs=gu_blk, out_shape=jax.ShapeDtypeStruct((N_SHARD, 2, s_n, f), BF16),
        compiler_params=_cparams(), name=name)(df, w_dn, gu)


_NT = (((1,), (1,)), ((), ()))
_TN = (((0,), (0,)), ((), ()))
MLA_TQ = 256


def _causal_mask(i, tq, s_n):
    qpos = i * tq + lax.broadcasted_iota(jnp.int32, (tq, s_n), 0)
    kpos = lax.broadcasted_iota(jnp.int32, (tq, s_n), 1)
    return kpos <= qpos


def _mla_attn_fwd(q, k, v, *, name):
    h_n, s_n, dq = q.shape
    dv = v.shape[-1]
    tq = MLA_TQ
    scale = float(dq) ** -0.5

    def body(q_ref, k_ref, v_ref, o_ref, lse_ref):
        i = pl.program_id(1)
        for e in range(1, s_n // tq + 1):
            @pl.when(i == e - 1)
            def _(ext=e * tq):
                mask = _causal_mask(i, tq, ext)
                s = lax.dot_general(q_ref[...], k_ref[0:ext, :], _NT, preferred_element_type=F32) * scale
                s = jnp.where(mask, s, -jnp.inf)
                m = jnp.max(s, axis=-1, keepdims=True)
                p = jnp.exp(s - m)
                l = jnp.sum(p, axis=-1, keepdims=True)
                o = jnp.dot(p.astype(BF16), v_ref[0:ext, :], preferred_element_type=F32)
                o_ref[...] = o / l
                lse_ref[...] = m + jnp.log(l)

    return pl.pallas_call(
        body, grid=(h_n, s_n // tq),
        in_specs=[pl.BlockSpec((None, tq, dq), lambda h, i: (h, i, 0)),
                  pl.BlockSpec((None, s_n, dq), lambda h, i: (h, 0, 0)),
                  pl.BlockSpec((None, s_n, dv), lambda h, i: (h, 0, 0))],
        out_specs=[pl.BlockSpec((None, tq, dv), lambda h, i: (h, i, 0)),
                   pl.BlockSpec((None, tq, 1), lambda h, i: (h, i, 0))],
        out_shape=[jax.ShapeDtypeStruct((h_n, s_n, dv), F32), jax.ShapeDtypeStruct((h_n, s_n, 1), F32)],
        compiler_params=_cparams(), name=name)(q, k, v)


def _mla_attn_bwd(q, k, v, o, do, lse, *, name):
    h_n, s_n, dq = q.shape
    dv = v.shape[-1]
    tq = MLA_TQ
    scale = float(dq) ** -0.5

    def body(q_ref, k_ref, v_ref, o_ref, do_ref, lse_ref, dq_ref, dk_ref, dv_ref):
        i = pl.program_id(1)

        @pl.when(i == 0)
        def _():
            dk_ref[...] = jnp.zeros_like(dk_ref)
            dv_ref[...] = jnp.zeros_like(dv_ref)

        for e in range(1, s_n // tq + 1):
            @pl.when(i == e - 1)
            def _(ext=e * tq):
                mask = _causal_mask(i, tq, ext)
                qv, kv, vv = q_ref[...], k_ref[0:ext, :], v_ref[0:ext, :]
                do_v = do_ref[...]
                s = lax.dot_general(qv, kv, _NT, preferred_element_type=F32) * scale
                p = jnp.where(mask, jnp.exp(s - lse_ref[...]), 0.0)
                dob = do_v.astype(BF16)
                dv_ref[0:ext, :] += lax.dot_general(p.astype(BF16), dob, _TN, preferred_element_type=F32)
                dp = lax.dot_general(dob, vv, _NT, preferred_element_type=F32)
                delta = jnp.sum(do_v * o_ref[...], axis=-1, keepdims=True)
                dsb = (p * (dp - delta) * scale).astype(BF16)
                dq_ref[...] = jnp.dot(dsb, kv, preferred_element_type=F32)
                dk_ref[0:ext, :] += lax.dot_general(dsb, qv, _TN, preferred_element_type=F32)

    return pl.pallas_call(
        body, grid=(h_n, s_n // tq),
        in_specs=[pl.BlockSpec((None, tq, dq), lambda h, i: (h, i, 0)),
                  pl.BlockSpec((None, s_n, dq), lambda h, i: (h, 0, 0)),
                  pl.BlockSpec((None, s_n, dv), lambda h, i: (h, 0, 0)),
                  pl.BlockSpec((None, tq, dv), lambda h, i: (h, i, 0)),
                  pl.BlockSpec((None, tq, dv), lambda h, i: (h, i, 0)),
                  pl.BlockSpec((None, tq, 1), lambda h, i: (h, i, 0))],
        out_specs=[pl.BlockSpec((None, tq, dq), lambda h, i: (h, i, 0)),
                   pl.BlockSpec((None, s_n, dq), lambda h, i: (h, 0, 0)),
                   pl.BlockSpec((None, s_n, dv), lambda h, i: (h, 0, 0))],
        out_shape=[jax.ShapeDtypeStruct((h_n, s_n, dq), F32), jax.ShapeDtypeStruct((h_n, s_n, dq), F32),
                   jax.ShapeDtypeStruct((h_n, s_n, dv), F32)],
        compiler_params=_cparams(), name=name)(q, k, v, o, do, lse)


def _head_sum(x, *, name):
    h_n, s_n, w = x.shape
    tm = _pick(s_n, 512, 8)

    def body(x_ref, o_ref):
        o_ref[...] = jnp.sum(x_ref[...], axis=0)

    return pl.pallas_call(body, grid=(s_n // tm,), in_specs=[pl.BlockSpec((h_n, tm, w), lambda i: (0, i, 0))],
                          out_specs=_rows(tm, w), out_shape=jax.ShapeDtypeStruct((s_n, w), F32),
                          compiler_params=_cparams(), name=name)(x)


N_BLK = SEQ // DIL_BLOCK
DIL_SCALE = 64 ** -0.5


def _dil_masks():
    iq = lax.broadcasted_iota(jnp.int32, (DIL_BLOCK, 2 * DIL_BLOCK), 0)
    ik = lax.broadcasted_iota(jnp.int32, (DIL_BLOCK, 2 * DIL_BLOCK), 1)
    rel = DIL_BLOCK + iq - ik
    both = (rel >= 0) & (rel <= DIL_BLOCK)
    iq1 = lax.broadcasted_iota(jnp.int32, (DIL_BLOCK, DIL_BLOCK), 0)
    ik1 = lax.broadcasted_iota(jnp.int32, (DIL_BLOCK, DIL_BLOCK), 1)
    return both, ik1 <= iq1


def _dil_block(j, d):
    nb = SEQ // d // DIL_BLOCK
    r, n = divmod(j, nb)
    first = n == 0
    rows = lambda start, size: pl.ds(start, size) if d == 1 else pl.ds(start, size, stride=d)
    q_rows = rows(n * DIL_BLOCK * d + r, DIL_BLOCK)
    k_rows = q_rows if first else rows((n - 1) * DIL_BLOCK * d + r, 2 * DIL_BLOCK)
    return q_rows, k_rows, (DIL_BLOCK if first else 0), first


PAIR = 2 * 64
N_PAIR = HEADS // 2


def _dil_head_specs(s_n, g):
    return [pl.BlockSpec((None, s_n, PAIR), lambda hp, t=t: ((g * 3 + t) * N_PAIR + hp, 0, 0)) for t in range(3)]


def _pair_specs(s_n, w):
    return pl.BlockSpec((2, s_n, w), lambda hp: (hp, 0, 0))


_PAIR_BIAS = pl.BlockSpec((2, DIL_BLOCK, 2 * DIL_BLOCK), lambda hp: (hp, 0, 0))


def _dil_attn_fwd(heads, bias, g, d, *, name):
    _, s_n, _ = heads.shape
    e = PAIR // 2

    def body(q_ref, k_ref, v_ref, b_ref, o_ref, lse_ref):
        m_both, m_first = _dil_masks()
        for j in range(N_BLK):
            q_rows, k_rows, b_lo, first = _dil_block(j, d)
            q2 = q_ref[q_rows, :].astype(BF16)
            k2 = k_ref[k_rows, :].astype(BF16)
            v2 = v_ref[k_rows, :].astype(BF16)
            for hh in range(2):
                cols = slice(hh * e, (hh + 1) * e)
                s = (lax.dot_general(q2[:, cols], k2[:, cols], _NT, preferred_element_type=F32) * DIL_SCALE
                     + b_ref[hh, :, b_lo:])
                s = jnp.where(m_first if first else m_both, s, -jnp.inf)
                m = jnp.max(s, axis=-1, keepdims=True)
                lse = m + jnp.log(jnp.sum(jnp.exp(s - m), axis=-1, keepdims=True))
                p = jnp.exp(s - lse)
                o_ref[hh, q_rows, :] = jnp.dot(p.astype(BF16), v2[:, cols], preferred_element_type=F32)
                lse_ref[hh, q_rows, :] = lse

    return pl.pallas_call(
        body, grid=(N_PAIR,), in_specs=_dil_head_specs(s_n, g) + [_PAIR_BIAS],
        out_specs=[_pair_specs(s_n, e), _pair_specs(s_n, 1)],
        out_shape=[jax.ShapeDtypeStruct((HEADS, s_n, e), F32), jax.ShapeDtypeStruct((HEADS, s_n, 1), F32)],
        compiler_params=_cparams(), name=name)(heads, heads, heads, bias)


def _dil_attn_bwd(heads, bias, lse, do, dlt, g, d, *, name):
    _, s_n, _ = heads.shape
    e = PAIR // 2

    def body(q_ref, k_ref, v_ref, b_ref, lse_ref, do_ref, dlt_ref, dq_ref, dk_ref, dv_ref, db_ref):
        db_ref[...] = jnp.zeros_like(db_ref)
        m_both, m_first = _dil_masks()
        nb = s_n // d // DIL_BLOCK
        own_v = own_k = own_rows = None
        for j in range(N_BLK):
            q_rows, k_rows, b_lo, first = _dil_block(j, d)
            q2 = q_ref[q_rows, :].astype(BF16)
            k2 = k_ref[k_rows, :].astype(BF16)
            v2 = v_ref[k_rows, :].astype(BF16)
            dq_h, dv_h, dk_h = [], [], []
            for hh in range(2):
                cols = slice(hh * e, (hh + 1) * e)
                qj, kk, vv = q2[:, cols], k2[:, cols], v2[:, cols]
                s = lax.dot_general(qj, kk, _NT, preferred_element_type=F32) * DIL_SCALE + b_ref[hh, :, b_lo:]
                p = jnp.where(m_first if first else m_both, jnp.exp(s - lse_ref[hh, q_rows, :]), 0.0)
                dob = do_ref[hh, q_rows, :].astype(BF16)
                dv_h.append(lax.dot_general(p.astype(BF16), dob, _TN, preferred_element_type=F32))
                dp = lax.dot_general(dob, vv, _NT, preferred_element_type=F32)
                ds = p * (dp - dlt_ref[hh, q_rows, :])
                db_ref[hh, :, b_lo:] += ds
                dsb = (ds * DIL_SCALE).astype(BF16)
                dq_h.append(jnp.dot(dsb, kk, preferred_element_type=F32))
                dk_h.append(lax.dot_general(dsb, qj, _TN, preferred_element_type=F32))
            dq_ref[q_rows, :] = jnp.concatenate(dq_h, axis=1)
            dvv, dkk = jnp.concatenate(dv_h, axis=1), jnp.concatenate(dk_h, axis=1)
            if not first:
                dv_ref[own_rows, :] = own_v + dvv[:DIL_BLOCK]
                dk_ref[own_rows, :] = own_k + dkk[:DIL_BLOCK]
                dvv, dkk = dvv[DIL_BLOCK:], dkk[DIL_BLOCK:]
            own_v, own_k, own_rows = dvv, dkk, q_rows
            if j % nb == nb - 1:
                dv_ref[own_rows, :] = own_v
                dk_ref[own_rows, :] = own_k

    slab = pl.BlockSpec((None, s_n, PAIR), lambda hp: (hp, 0, 0))
    sd = jax.ShapeDtypeStruct((N_PAIR, s_n, PAIR), F32)
    return pl.pallas_call(
        body, grid=(N_PAIR,),
        in_specs=_dil_head_specs(s_n, g) + [_PAIR_BIAS, _pair_specs(s_n, 1), _pair_specs(s_n, e), _pair_specs(s_n, 1)],
        out_specs=[slab, slab, slab, _PAIR_BIAS],
        out_shape=[sd, sd, sd, jax.ShapeDtypeStruct((HEADS, DIL_BLOCK, 2 * DIL_BLOCK), F32)],
        compiler_params=_cparams(), name=name)(heads, heads, heads, bias, lse, do, dlt)


def _proj_heads(x, w, *, name):
    s_n, k = x.shape
    n = w.shape[-1]
    tm, tn, e = 512, 768, PAIR
    per_blk, n_blk = tn // e, n // tn

    def body(x_ref, w_ref, o_ref):
        r = jnp.dot(x_ref[...], w_ref[...], preferred_element_type=F32)
        for j in range(per_blk):
            o_ref[j] = r[:, e * j:e * (j + 1)]

    return pl.pallas_call(
        body, grid=(w.shape[0], n_blk, s_n // tm),
        in_specs=[pl.BlockSpec((tm, k), lambda s, b, m: (m, 0)), pl.BlockSpec((None, k, tn), lambda s, b, m: (s, 0, b))],
        out_specs=pl.BlockSpec((per_blk, tm, e), lambda s, b, m: (s * n_blk + b, m, 0)),
        out_shape=jax.ShapeDtypeStruct((w.shape[0] * n // e, s_n, e), F32), compiler_params=_cparams(),
        name=name)(x, w)


def _heads_cat(d_ref):
    return jnp.concatenate([d_ref[j] for j in range(d_ref.shape[0])], axis=1)


def _proj_heads_dw(x, dh, *, name):
    s_n, k = x.shape
    tn, e = 768, PAIR
    per_blk = tn // e
    n_blk = dh.shape[0] // N_SHARD // per_blk
    n = n_blk * tn

    def body(x_ref, d_ref, o_ref):
        o_ref[...] = lax.dot_general(x_ref[...], _heads_cat(d_ref), _TN, preferred_element_type=F32)

    return pl.pallas_call(
        body, grid=(N_SHARD, n_blk, 2),
        in_specs=[pl.BlockSpec((s_n, k // 2), lambda s, b, r: (0, r)),
                  pl.BlockSpec((per_blk, s_n, e), lambda s, b, r: (s * n_blk + b, 0, 0))],
        out_specs=pl.BlockSpec((None, None, k // 2, tn), lambda s, b, r: (r, s, 0, b)),
        out_shape=jax.ShapeDtypeStruct((2, N_SHARD, k // 2, n), F32), compiler_params=_cparams(), name=name)(x, dh)


def _proj_heads_dx(dh, w, *, name):
    k, n = w.shape[1:]
    s_n = dh.shape[1]
    tm, tn, e = 512, 1152, PAIR
    per_blk, n_blk = tn // e, n // tn

    def body(d_ref, w_ref, o_ref):
        r = lax.dot_general(_heads_cat(d_ref), w_ref[...], _NT, preferred_element_type=F32)
        g = pl.program_id(1)

        @pl.when(g == 0)
        def _():
            o_ref[...] = r

        @pl.when(g > 0)
        def _():
            o_ref[...] += r

    return pl.pallas_call(
        body, grid=(s_n // tm, N_SHARD * n_blk),
        in_specs=[pl.BlockSpec((per_blk, tm, e), lambda m, g: (g, m, 0)),
                  pl.BlockSpec((None, k, tn), lambda m, g: (g // n_blk, 0, g % n_blk))],
        out_specs=pl.BlockSpec((tm, k), lambda m, g: (m, 0)),
        out_shape=jax.ShapeDtypeStruct((s_n, k), F32), compiler_params=_cparams(), name=name)(dh, w)


def _group_alpha(ls):
    m = jnp.maximum(jnp.maximum(ls[0], ls[1]), ls[2])
    es = [jnp.exp(l - m) for l in ls]
    tot = es[0] + es[1] + es[2]
    return [ex / tot for ex in es]


def _dil_mix_fwd(os_, ls_, *, name):
    h_n, s_n, e = os_[0].shape
    tm = 512

    def body(o0, o1, o2, l0, l1, l2, out_ref):
        for hh in range(2):
            al = _group_alpha([l[hh] for l in (l0, l1, l2)])
            mix = al[0] * o0[hh] + al[1] * o1[hh] + al[2] * o2[hh]
            out_ref[:, hh * e:(hh + 1) * e] = mix.astype(out_ref.dtype)

    blk = lambda w: pl.BlockSpec((2, tm, w), lambda h, i: (h, i, 0))
    return pl.pallas_call(body, grid=(h_n // 2, s_n // tm), in_specs=[blk(e)] * 3 + [blk(1)] * 3,
                          out_specs=pl.BlockSpec((tm, 2 * e), lambda h, i: (i, h)),
                          out_shape=jax.ShapeDtypeStruct((s_n, h_n * e), BF16), compiler_params=_cparams(),
                          name=name)(*os_, *ls_)


def _dil_mix_bwd(do_flat, os_, ls_, *, name):
    h_n, s_n, e = os_[0].shape
    tm = 512

    def body(do_ref, o0, o1, o2, l0, l1, l2, d0, d1, d2, t0, t1, t2):
        for hh in range(2):
            al = _group_alpha([l[hh] for l in (l0, l1, l2)])
            do_v = do_ref[:, hh * e:(hh + 1) * e]
            mix = al[0] * o0[hh] + al[1] * o1[hh] + al[2] * o2[hh]
            dbar = jnp.sum(do_v * mix, axis=-1, keepdims=True)
            for a_g, d_ref, t_ref in zip(al, (d0, d1, d2), (t0, t1, t2)):
                d_ref[hh] = a_g * do_v
                t_ref[hh] = a_g * dbar

    blk = lambda w: pl.BlockSpec((2, tm, w), lambda h, i: (h, i, 0))
    sd_e = jax.ShapeDtypeStruct((h_n, s_n, e), F32)
    sd_1 = jax.ShapeDtypeStruct((h_n, s_n, 1), F32)
    outs = pl.pallas_call(body, grid=(h_n // 2, s_n // tm),
                          in_specs=[pl.BlockSpec((tm, 2 * e), lambda h, i: (i, h))] + [blk(e)] * 3 + [blk(1)] * 3,
                          out_specs=[blk(e)] * 3 + [blk(1)] * 3, out_shape=[sd_e] * 3 + [sd_1] * 3,
                          compiler_params=_cparams(), name=name)(do_flat, *os_, *ls_)
    return outs[:3], outs[3:]


def _bias_grad(ds, bucket, *, name):
    h_n = ds.shape[0]

    def body(ds_ref, bk_ref, o_ref):
        ds_v = ds_ref[...]
        bk = bk_ref[...]
        lane = lax.broadcasted_iota(jnp.int32, (1, N_BUCKETS), 1)
        acc = jnp.zeros((1, N_BUCKETS), F32)
        for b in range(N_BUCKETS):
            tot = jnp.sum(jnp.sum(jnp.where(bk == b, ds_v, 0.0), axis=1, keepdims=True), axis=0, keepdims=True)
            acc = acc + jnp.where(lane == b, tot, 0.0)
        o_ref[...] = acc

    return pl.pallas_call(
        body, grid=(h_n,),
        in_specs=[pl.BlockSpec((None, DIL_BLOCK, 2 * DIL_BLOCK), lambda h: (h, 0, 0)),
                  pl.BlockSpec((DIL_BLOCK, 2 * DIL_BLOCK), lambda h: (0, 0))],
        out_specs=pl.BlockSpec((None, 1, N_BUCKETS), lambda h: (h, 0, 0)),
        out_shape=jax.ShapeDtypeStruct((h_n, 1, N_BUCKETS), F32), compiler_params=_cparams(), name=name)(ds, bucket)


def _bias_table(rb, bucket, *, name):
    h_n = rb.shape[0]

    def body(rb_ref, bk_ref, o_ref):
        bk = bk_ref[...]
        row = rb_ref[...]
        acc = jnp.zeros(bk.shape, F32)
        for b in range(N_BUCKETS):
            acc = jnp.where(bk == b, row[:, b:b + 1], acc)
        o_ref[...] = acc

    return pl.pallas_call(
        body, grid=(h_n,),
        in_specs=[pl.BlockSpec((None, 1, N_BUCKETS), lambda h: (h, 0, 0)),
                  pl.BlockSpec((DIL_BLOCK, 2 * DIL_BLOCK), lambda h: (0, 0))],
        out_specs=pl.BlockSpec((None, DIL_BLOCK, 2 * DIL_BLOCK), lambda h: (h, 0, 0)),
        out_shape=jax.ShapeDtypeStruct((h_n, DIL_BLOCK, 2 * DIL_BLOCK), F32), compiler_params=_cparams(),
        name=name)(rb, bucket)


def _row_tile(rows, cols, budget=2 << 20):
    if rows * cols * 4 <= budget or rows % 8:
        return rows
    best = 8
    for t in range(8, rows + 1, 8):
        if rows % t == 0 and t * cols * 4 <= budget:
            best = t
    return best


def _adamw(w, g, m, v, *, name):
    shape = w.shape
    cols = shape[-1]
    rows = math.prod(shape[:-1]) if len(shape) > 1 else 1
    to2 = lambda t: t.reshape(rows, cols)
    tr = _row_tile(rows, cols)
    c1 = 1.0 / (1.0 - ADAM_B1 ** ADAM_STEP)
    c2 = 1.0 / (1.0 - ADAM_B2 ** ADAM_STEP)

    def body(w_ref, g_ref, m_ref, v_ref, d_ref, nm_ref, nv_ref):
        g_v = g_ref[...]
        nm = ADAM_B1 * m_ref[...] + (1.0 - ADAM_B1) * g_v
        nv = ADAM_B2 * v_ref[...] + (1.0 - ADAM_B2) * (g_v * g_v)
        m_hat = nm * c1
        v_hat = nv * c2
        d_ref[...] = -ADAM_LR * (m_hat / (jnp.sqrt(v_hat) + ADAM_EPS) + ADAM_WD * w_ref[...])
        nm_ref[...] = nm
        nv_ref[...] = nv

    blk = pl.BlockSpec((tr, cols), lambda i: (i, 0))
    sd = jax.ShapeDtypeStruct((rows, cols), F32)
    outs = pl.pallas_call(body, grid=(rows // tr,), in_specs=[blk] * 4, out_specs=[blk] * 3, out_shape=[sd] * 3,
                          compiler_params=_cparams(), name=name)(to2(w), to2(g), to2(m), to2(v))
    return tuple(t.reshape(shape) for t in outs)


def _add_half(unit, got, half_idx, *, name):
    rest = unit.shape[2:]
    c = rest[-1]
    r = math.prod(rest[:-1])
    tr = _row_tile(r, c)

    def body(idx_ref, u_ref, g_ref, o_ref, w_ref):
        tot = u_ref[...] + g_ref[...].astype(F32)
        o_ref[...] = tot
        w_ref[...] = tot.astype(BF16)

    blk = pl.BlockSpec((None, tr, c), lambda s, i, idx: (s, i, 0))
    grid_spec = pltpu.PrefetchScalarGridSpec(
        num_scalar_prefetch=1, grid=(N_SHARD, r // tr),
        in_specs=[pl.BlockSpec((None, None, tr, c), lambda s, i, idx: (idx[0], s, i, 0)), blk],
        out_specs=[blk, blk])
    out, wire = pl.pallas_call(
        body, grid_spec=grid_spec,
        out_shape=[jax.ShapeDtypeStruct((N_SHARD, r, c), F32), jax.ShapeDtypeStruct((N_SHARD, r, c), BF16)],
        compiler_params=_cparams(), name=name)(half_idx, unit.reshape(2, N_SHARD, r, c), got.reshape(N_SHARD, r, c))
    return out.reshape((N_SHARD,) + rest), wire.reshape((N_SHARD,) + rest)


def _add_shards(part, got, shard_idx, *, name):
    rest = part.shape[1:]
    c = rest[-1]
    r = math.prod(rest[:-1])
    tr = _row_tile(r, c)

    def body(idx_ref, p_ref, g_ref, o_ref):
        acc = p_ref[...]
        for k in range(3):
            acc = acc + g_ref[k].astype(F32)
        o_ref[...] = acc

    grid_spec = pltpu.PrefetchScalarGridSpec(
        num_scalar_prefetch=1, grid=(r // tr,),
        in_specs=[pl.BlockSpec((None, tr, c), lambda i, idx: (idx[0], i, 0)),
                  pl.BlockSpec((3, tr, c), lambda i, idx: (0, i, 0))],
        out_specs=pl.BlockSpec((tr, c), lambda i, idx: (i, 0)))
    out = pl.pallas_call(body, grid_spec=grid_spec, out_shape=jax.ShapeDtypeStruct((r, c), F32),
                         compiler_params=_cparams(), name=name)(
        shard_idx, part.reshape(N_SHARD, r, c), got.reshape(3, r, c))
    return out.reshape(rest)


def _sum_devices(x, n_dev, *, name):
    rows = x.shape[0] // n_dev

    def body(x_ref, o_ref):
        acc = x_ref[0:rows, :]
        for d in range(1, n_dev):
            acc = acc + x_ref[d * rows:(d + 1) * rows, :]
        o_ref[...] = acc

    return pl.pallas_call(body, out_shape=jax.ShapeDtypeStruct((rows, x.shape[1]), F32), name=name)(x)


def _my_pos():
    return lax.axis_index("x"), lax.axis_index("y"), lax.axis_index("c")


def _all_gather(x_blk, *, name, in_vmem):
    m_per, n = x_blk.shape

    def body(x_ref, out_ref, send_sems, recv_sems, local_sem):
        x, y, c = _my_pos()
        me, sibling = (x, y, c), (x, y, 1 - c)
        chips = [(1 - x, y), (x, 1 - y), (1 - x, 1 - y)]

        def rows(px, py, pc):
            return out_ref.at[pl.ds((4 * px + 2 * py + pc) * m_per, m_per), :]

        def copy(k, block, to, src=None):
            return pltpu.make_async_remote_copy(
                src_ref=rows(*block) if src is None else src, dst_ref=rows(*block),
                send_sem=send_sems.at[k], recv_sem=recv_sems.at[k], device_id=to, device_id_type=MESH)

        mine = pltpu.make_async_copy(x_ref, rows(*me), local_sem)
        mine.start()
        first = [copy(0, me, sibling, src=x_ref)]
        first += [copy(1 + j, me, (*chip, c), src=x_ref) for j, chip in enumerate(chips)]
        for cp in first:
            cp.start()
        passed = [copy(4 + j, (*chip, c), sibling) for j, chip in enumerate(chips)]
        for j, chip in enumerate(chips):
            copy(1 + j, (*chip, c), me).wait_recv()
            passed[j].start()
        copy(0, sibling, me).wait_recv()
        for j, chip in enumerate(chips):
            copy(4 + j, (*chip, 1 - c), me).wait_recv()
        for cp in first + passed:
            cp.wait_send()
        mine.wait()

    space = pltpu.VMEM if in_vmem else pl.ANY
    return pl.pallas_call(
        body, out_shape=jax.ShapeDtypeStruct((8 * m_per, n), x_blk.dtype),
        in_specs=[pl.BlockSpec(memory_space=space)], out_specs=pl.BlockSpec(memory_space=space),
        scratch_shapes=[pltpu.SemaphoreType.DMA((7,)), pltpu.SemaphoreType.DMA((7,)), pltpu.SemaphoreType.DMA],
        name=name)(x_blk)


_HBM = pl.BlockSpec(memory_space=pl.ANY)


def _gather_weights(fams, *, name):
    n = len(fams)

    def body(*refs):
        ins, outs = refs[:n], refs[n:2 * n]
        send_sems, recv_sems = refs[2 * n:]
        x, y, c = _my_pos()
        me, sibling = (x, y, c), (x, y, 1 - c)
        chips = [(1 - x, y), (x, 1 - y), (1 - x, 1 - y)]

        def copy(f, k, block, to, src=None):
            px, py, pc = block
            dst = outs[f].at[2 * px + py, pc]
            return pltpu.make_async_remote_copy(
                src_ref=dst if src is None else src, dst_ref=dst, send_sem=send_sems.at[7 * f + k],
                recv_sem=recv_sems.at[7 * f + k], device_id=to, device_id_type=MESH)

        first, passed = [], []
        for f in range(n):
            src = ins[f].at[c]
            first.append(copy(f, 0, me, sibling, src=src))
            first += [copy(f, 1 + j, me, (*chip, c), src=src) for j, chip in enumerate(chips)]
        for cp in first:
            cp.start()
        for j, chip in enumerate(chips):
            for f in range(n):
                copy(f, 1 + j, (*chip, c), me).wait_recv()
                passed.append(copy(f, 4 + j, (*chip, c), sibling))
                passed[-1].start()
        for f in range(n):
            copy(f, 0, sibling, me).wait_recv()
        for j, chip in enumerate(chips):
            for f in range(n):
                copy(f, 4 + j, (*chip, 1 - c), me).wait_recv()
        for cp in first + passed:
            cp.wait_send()

    outs = pl.pallas_call(
        body, out_shape=[jax.ShapeDtypeStruct((N_SHARD,) + t.shape, t.dtype) for t in fams],
        in_specs=[_HBM] * n, out_specs=[_HBM] * n,
        scratch_shapes=[pltpu.SemaphoreType.DMA((7 * n,)), pltpu.SemaphoreType.DMA((7 * n,))], name=name)(*fams)
    return [_place_own(o, t) for o, t in zip(outs, fams)]


def _pair_gather(halves, *, name):
    n = len(halves)

    def body(*refs):
        ins, outs = refs[:n], refs[n:2 * n]
        send_sems, recv_sems = refs[2 * n:]
        x, y, c = _my_pos()
        cps = [pltpu.make_async_remote_copy(src_ref=ins[f], dst_ref=outs[f].at[c], send_sem=send_sems.at[f],
                                            recv_sem=recv_sems.at[f], device_id=(x, y, 1 - c), device_id_type=MESH)
               for f in range(n)]
        for cp in cps:
            cp.start()
        for f in range(n):
            pltpu.make_async_remote_copy(src_ref=ins[f], dst_ref=outs[f].at[1 - c], send_sem=send_sems.at[f],
                                         recv_sem=recv_sems.at[f], device_id=(x, y, 1 - c),
                                         device_id_type=MESH).wait_recv()
        for cp in cps:
            cp.wait_send()

    outs = pl.pallas_call(
        body, out_shape=[jax.ShapeDtypeStruct((2,) + t.shape, t.dtype) for t in halves],
        in_specs=[_HBM] * n, out_specs=[_HBM] * n,
        scratch_shapes=[pltpu.SemaphoreType.DMA((n,)), pltpu.SemaphoreType.DMA((n,))], name=name)(*halves)
    c = lax.axis_index("c")
    return [lax.dynamic_update_index_in_dim(o, t, c, 0) for o, t in zip(outs, halves)]


_HBM_ONLY = pl.BlockSpec(memory_space=pltpu.HBM)
_SEMS = pl.BlockSpec(memory_space=pltpu.SEMAPHORE)
_EFFECT = pltpu.SideEffectType.DATAFLOW_SIDE_EFFECTING


def _copies_start(srcs, lands, plan, n_copies, *, name):
    n, m = len(srcs), len(lands)

    def body(*refs):
        src_refs, land_refs = refs[:n], refs[n:n + m]
        send_sems, recv_sems, token = refs[n + m], refs[n + m + 1], refs[-1]
        for k, (src, dst, peer) in enumerate(plan(src_refs, land_refs)):
            pltpu.make_async_remote_copy(src_ref=src, dst_ref=dst, send_sem=send_sems.at[k], recv_sem=recv_sems.at[k],
                                         device_id=peer, device_id_type=MESH).start()
        token[...] = jnp.zeros_like(token)

    bufs = [pltpu.with_memory_space_constraint(t, pltpu.HBM) for t in (*srcs, *lands)]
    outs = pl.pallas_call(
        body, name=name,
        out_shape=(pltpu.SemaphoreType.DMA((n_copies,)), pltpu.SemaphoreType.DMA((n_copies,)),
                   *[pltpu.HBM(t.shape, t.dtype) for t in bufs], jax.ShapeDtypeStruct((8, 128), F32)),
        in_specs=[_HBM_ONLY] * (n + m),
        out_specs=(_SEMS, _SEMS, *[_HBM_ONLY] * (n + m), pl.BlockSpec(memory_space=pltpu.VMEM)),
        input_output_aliases={k: 2 + k for k in range(n + m)},
        compiler_params=pltpu.CompilerParams(has_side_effects=_EFFECT))(*bufs)
    return outs[0], outs[1], list(outs[2:2 + n + m]), outs[-1]


def _copies_wait(send_sems, recv_sems, thru, n_src, plan, after, *, name):
    nm = len(thru)

    def body(*refs):
        t_refs, send, recv = refs[:nm], refs[nm], refs[nm + 1]
        for k, (src, dst, peer) in enumerate(plan(t_refs[:n_src], t_refs[n_src:])):
            cp = pltpu.make_async_remote_copy(src_ref=src, dst_ref=dst, send_sem=send.at[k], recv_sem=recv.at[k],
                                              device_id=peer, device_id_type=MESH)
            cp.wait_send()
            cp.wait_recv()

    outs = pl.pallas_call(
        body, name=name, out_shape=tuple(pltpu.HBM(t.shape, t.dtype) for t in thru),
        in_specs=[_HBM_ONLY] * nm + [_SEMS, _SEMS, pl.BlockSpec(memory_space=pl.ANY)],
        out_specs=tuple([_HBM_ONLY] * nm), input_output_aliases={k: k for k in range(nm)},
        compiler_params=pltpu.CompilerParams(has_side_effects=_EFFECT))(*thru, send_sems, recv_sems, after)
    return list(outs)


_RELATIONS = [(dx, dy, dc) for dx in (0, 1) for dy in (0, 1) for dc in (0, 1)][1:]


def _gather_plan(src_refs, land_refs):
    x, y, c = _my_pos()
    flip = lambda v, d: 1 - v if d else v
    return [(s_ref.at[c], l_ref.at[2 * x + y, c], (flip(x, dx), flip(y, dy), flip(c, dc)))
            for s_ref, l_ref in zip(src_refs, land_refs) for dx, dy, dc in _RELATIONS]


def _gather_chips_plan(src_refs, land_refs):
    x, y, c = _my_pos()
    peers = [(x, y, 1 - c), (1 - x, y, c), (x, 1 - y, c), (1 - x, 1 - y, c)]
    return [(s_ref.at[c], l_ref.at[2 * x + y, c], peer) for s_ref, l_ref in zip(src_refs, land_refs) for peer in peers]


def _gather_pass_plan(src_refs, land_refs):
    x, y, c = _my_pos()
    chips = [(1 - x, y), (x, 1 - y), (1 - x, 1 - y)]
    return [(l_ref.at[2 * cx + cy, c], l_ref.at[2 * cx + cy, c], (x, y, 1 - c))
            for l_ref in land_refs for cx, cy in chips]


def _sibling_plan(src_refs, land_refs):
    x, y, c = _my_pos()
    return [(s_ref.at[1 - c], l_ref, (x, y, 1 - c)) for s_ref, l_ref in zip(src_refs, land_refs)]


def _chips_plan(src_refs, land_refs):
    x, y, c = _my_pos()
    chips = [(1 - x, y), (x, 1 - y), (1 - x, 1 - y)]
    return [(s_ref.at[2 * cx + cy], l_ref.at[k], (cx, cy, c))
            for s_ref, l_ref in zip(src_refs, land_refs) for k, (cx, cy) in enumerate(chips)]


def _place_own(gathered, fam):
    x, y, c = _my_pos()
    own = lax.dynamic_index_in_dim(fam, c, 0, keepdims=True)[None]
    return lax.dynamic_update_slice(gathered, own, (2 * x + y, c) + (0,) * (fam.ndim - 1))


def _to_heads(t, width):
    return t.reshape(t.shape[0], HEADS, width).transpose(1, 0, 2)


def _from_heads(t):
    return t.transpose(1, 0, 2).reshape(t.shape[1], -1)


def _t5_bucket(dist):
    max_exact = N_BUCKETS // 2
    d = jnp.maximum(dist, 1).astype(F32)
    large = max_exact + (jnp.log(d / max_exact) / math.log(MAX_DISTANCE / max_exact)
                         * (N_BUCKETS - max_exact)).astype(jnp.int32)
    large = jnp.minimum(large, N_BUCKETS - 1)
    return jnp.where(dist < max_exact, dist, large)


def _bucket_map(dilation):
    iq = jnp.arange(DIL_BLOCK)[:, None]
    ik = jnp.arange(2 * DIL_BLOCK)[None, :]
    rel = DIL_BLOCK + iq - ik
    return _t5_bucket(jnp.maximum(rel, 0) * dilation).astype(jnp.int32)


def _q_perm(w):
    w3 = w.reshape(w.shape[0], HEADS, QK_NOPE + QK_ROPE)
    return jnp.concatenate([w3[:, :, :QK_NOPE].reshape(w.shape[0], -1),
                            w3[:, :, QK_NOPE:QK_NOPE + HALF_ROPE].reshape(w.shape[0], -1),
                            w3[:, :, QK_NOPE + HALF_ROPE:].reshape(w.shape[0], -1)], axis=1)


def _q_unperm(w):
    n0, n1 = HEADS * QK_NOPE, HEADS * HALF_ROPE
    r = w.shape[0]
    return jnp.concatenate([w[:, :n0].reshape(r, HEADS, QK_NOPE), w[:, n0:n0 + n1].reshape(r, HEADS, HALF_ROPE),
                            w[:, n0 + n1:].reshape(r, HEADS, HALF_ROPE)], axis=2).reshape(r, -1)


def _kv_perm(w):
    w3 = w.reshape(w.shape[0], HEADS, QK_NOPE + V_HEAD)
    return jnp.concatenate([w3[:, :, :QK_NOPE].reshape(w.shape[0], -1), w3[:, :, QK_NOPE:].reshape(w.shape[0], -1)],
                           axis=1)


def _kv_unperm(w):
    n0 = HEADS * QK_NOPE
    r = w.shape[0]
    return jnp.concatenate([w[:, :n0].reshape(r, HEADS, QK_NOPE), w[:, n0:].reshape(r, HEADS, V_HEAD)],
                           axis=2).reshape(r, -1)


def _row(v):
    return v.reshape(1, -1)


def kernel(x, c, norm_pre, norm_post, w_mod, b_mod, ffn_w_gate, ffn_w_up, ffn_w_down, mla_w_in, mla_q_norm, mla_w_q_up, mla_kv_norm, mla_w_kv_up, mla_w_o, dil_w_in, dil_w_o, rel_bias, loss_target, m_norm_pre, m_norm_post, m_w_mod, m_b_mod, m_ffn_w_gate, m_ffn_w_up, m_ffn_w_down, m_mla_w_in, m_mla_q_norm, m_mla_w_q_up, m_mla_kv_norm, m_mla_w_kv_up, m_mla_w_o, m_dil_w_in, m_dil_w_o, m_rel_bias, v_norm_pre, v_norm_post, v_w_mod, v_b_mod, v_ffn_w_gate, v_ffn_w_up, v_ffn_w_down, v_mla_w_in, v_mla_q_norm, v_mla_w_q_up, v_mla_kv_norm, v_mla_w_kv_up, v_mla_w_o, v_dil_w_in, v_dil_w_o, v_rel_bias):
    given = dict(locals())
    ix, iy, ic = _my_pos()
    shard_id = 2 * ix + iy
    dev_id = 4 * ix + 2 * iy + ic
    x2 = x[0]
    target = loss_target[0]
    half_idx = jnp.reshape(ic, (1,)).astype(jnp.int32)
    shard_idx = jnp.reshape(shard_id, (1,)).astype(jnp.int32)

    blk = jnp.zeros((8, D_MODEL), F32)
    blk = blk.at[0].set(c[0])
    blk = blk.at[1:3].set(jnp.pad(norm_pre.reshape(-1), (0, 512)).reshape(2, D_MODEL))
    blk = blk.at[3:5].set(jnp.pad(norm_post.reshape(-1), (0, 512)).reshape(2, D_MODEL))
    got = _all_gather(blk, name="ag_c_norms", in_vmem=True).reshape(N_SHARD, 2, 8, D_MODEL)
    c_all = got[:, :, 0, :].reshape(8, D_MODEL)

    def full_norm(lo):
        t = got[:, 0, lo:lo + 2, :].reshape(N_SHARD, 2 * D_MODEL)[:, :1536].reshape(N_SHARD, 2, 3, 256)
        return t.transpose(1, 2, 0, 3).reshape(2, 3, D_MODEL)

    pre_full, post_full = full_norm(1), full_norm(3)

    silu_c = _silu_bf16(c_all, name="silu_c")
    b_cols = lax.dynamic_slice_in_dim(b_mod, shard_id * 2304, 2304, axis=1).reshape(2, 1, 2304)
    mod_part = _mm(silu_c, w_mod, bias=b_cols, name="mod_mm", tn_cap=768)
    mod_all = _all_gather(mod_part.reshape(16, 2304), name="ag_mod", in_vmem=True)
    mod_all = mod_all.reshape(N_SHARD, 2, 2, 8, 2304)[:, 0]
    mod_mine = lax.dynamic_index_in_dim(mod_all, dev_id, axis=2, keepdims=False)
    mod = mod_mine.transpose(1, 0, 2).reshape(2, 9, D_MODEL)

    bf = lambda t: t.astype(BF16)
    ffn_fam = lambda i, h: [bf(jnp.stack([ffn_w_gate[i, h], ffn_w_up[i, h]])),
                            bf(ffn_w_down[i, h].reshape(2, F_SHARD // 2, D_MODEL))]
    mla_fam = [bf(mla_w_in.reshape(2, 128, -1)), bf(mla_w_q_up.reshape(2, 192, -1)),
               bf(mla_w_kv_up.reshape(2, 128, -1)), bf(mla_w_o.reshape(2, 128, D_MODEL))]
    dil_fam = [bf(dil_w_in.reshape(2, 512, -1)), bf(dil_w_o.reshape(2, 128, D_MODEL))]
    later_fams = [ffn_fam(0, 1), ffn_fam(1, 0) + dil_fam, ffn_fam(1, 1)]
    full, later_fams, mod = lax.optimization_barrier(
        (_gather_weights(ffn_fam(0, 0) + mla_fam, name="ag_weights_first"), later_fams, mod))

    def gather_later(fams, tag):
        lands = [lax.empty((N_SHARD,) + t.shape, t.dtype) for t in fams]
        send, recv, thru, token = _copies_start(fams, lands, _gather_plan, 7 * len(fams), name=f"ag_start_{tag}")
        return dict(send=send, recv=recv, thru=thru, token=token, n=len(fams), tag=tag)

    def arrive(st, after):
        thru = _copies_wait(st['send'], st['recv'], st['thru'], st['n'], _gather_plan, after,
                            name=f"ag_wait_{st['tag']}")
        return [_place_own(o, t) for t, o in zip(thru[:st['n']], thru[st['n']:])]

    def gather_chips(fams, tag):
        lands = [lax.empty((N_SHARD,) + t.shape, t.dtype) for t in fams]
        send, recv, thru, token = _copies_start(fams, lands, _gather_chips_plan, 4 * len(fams), name=f"ag_start_{tag}")
        return dict(send=send, recv=recv, thru=thru, token=token, n=len(fams), tag=tag)

    def pass_on(st, after):
        n, tag = st['n'], st['tag']
        thru = _copies_wait(st['send'], st['recv'], st['thru'], n, _gather_chips_plan, after, name=f"ag_mid_{tag}")
        send, recv, lands, token = _copies_start([], thru[n:], _gather_pass_plan, 3 * n, name=f"ag_pass_{tag}")
        return dict(send=send, recv=recv, thru=lands, fams=thru[:n], tag=tag), token[0, 0]

    def arrive_passed(st, after):
        lands = _copies_wait(st['send'], st['recv'], st['thru'], 0, _gather_pass_plan, after, name=f"ag_wait_{st['tag']}")
        return [_place_own(o, t) for t, o in zip(st['fams'], lands)]

    flight_a = gather_later(later_fams[0], "l0s2")
    _, next_fams = lax.optimization_barrier((flight_a['token'], later_fams[1]))
    flight_b = gather_chips(next_fams, "l1s01")
    as_ffn = lambda w_gu, w_dn: (w_gu, w_dn.reshape(N_SHARD, F_SHARD, D_MODEL))
    ffn_w = {(0, 0): as_ffn(full[0], full[1])}
    w_in = full[2].reshape(D_MODEL, -1)
    wq_p = _q_perm(full[3].reshape(N_SHARD, Q_LORA, -1).transpose(1, 0, 2).reshape(Q_LORA, -1))
    wkv_p = _kv_perm(full[4].reshape(N_SHARD, KV_LORA, -1).transpose(1, 0, 2).reshape(KV_LORA, -1))
    w_mo = full[5].reshape(D_MODEL, D_MODEL)
    dil_w = {}

    pos = jnp.arange(SEQ, dtype=F32)
    freqs = ROPE_THETA ** (-jnp.arange(HALF_ROPE, dtype=F32) / HALF_ROPE)
    ang = pos[:, None] * freqs[None, :]
    cos_k, sin_k = jnp.cos(ang), jnp.sin(ang)
    cos_q, sin_q = jnp.tile(cos_k, (1, HEADS)), jnp.tile(sin_k, (1, HEADS))

    buckets = [_bucket_map(d) for _, d in DIL_GROUPS]
    biases = [_bias_table(rel_bias[:, g * HEADS:(g + 1) * HEADS].T.reshape(HEADS, 1, N_BUCKETS), bk,
                          name=f"dil_bias_table_g{g}") for g, bk in enumerate(buckets)]

    vpacks = jnp.concatenate([pre_full[:, :, None], post_full[:, :, None], mod.reshape(2, 3, 3, D_MODEL),
                              jnp.zeros((2, 3, 3, D_MODEL), F32)], axis=2)
    sub_params = lambda i, sub: vpacks[i, sub]

    def ffn_fwd(xin, i, h, sub, tie=None, mid=None):
        p = sub_params(i, sub)
        if tie is not None:
            p = p + tie
        tag = f"l{i}s{sub}"
        w_gu, w_dn = ffn_w[i, h]
        hn = _pre_fwd(xin, p, name=f"pre_fwd_{tag}")
        gu, a = _ffn_up(hn, w_gu, name=f"ffn_up_{tag}")
        if mid is not None:
            p = p + mid(a)
        f, out = _ffn_down(a, w_dn, xin, p, FFN_RES, name=f"ffn_down_{tag}")
        return out, dict(x=xin, hn=hn, gu=gu, a=a, f=f, p=p, i=i, h=h, tag=tag)

    def mla_fwd(xin, i, sub):
        p = sub_params(i, sub)
        tag = f"l{i}s{sub}"
        hn = _pre_fwd(xin, p, name=f"pre_fwd_{tag}")
        lat = _mm(hn, w_in, name="mla_lat")
        cq, ckv = lat[:, :Q_LORA], lat[:, Q_LORA:Q_LORA + KV_LORA]
        k1, k2 = lat[:, Q_LORA + KV_LORA:Q_LORA + KV_LORA + HALF_ROPE], lat[:, Q_LORA + KV_LORA + HALF_ROPE:]
        cqn = _rms_fwd(cq, mla_q_norm, name="mla_qnorm")
        ckvn = _rms_fwd(ckv, mla_kv_norm, name="mla_kvnorm")
        qp = _mm(cqn, wq_p, name="mla_q_up")
        kvp = _mm(ckvn, wkv_p, name="mla_kv_up")
        n0, n1 = HEADS * QK_NOPE, HEADS * HALF_ROPE
        qr1, qr2 = _rope(qp[:, n0:n0 + n1], qp[:, n0 + n1:], cos_q, sin_q, name="rope_q")
        kr1, kr2 = _rope(k1, k2, cos_k, sin_k, name="rope_k")
        q = jnp.concatenate([qp[:, :n0].reshape(SEQ, HEADS, QK_NOPE), qr1.reshape(SEQ, HEADS, HALF_ROPE),
                             qr2.reshape(SEQ, HEADS, HALF_ROPE)], axis=2).transpose(1, 0, 2).astype(BF16)
        kr = jnp.broadcast_to(jnp.concatenate([kr1, kr2], axis=1)[:, None, :], (SEQ, HEADS, QK_ROPE))
        k = jnp.concatenate([kvp[:, :n0].reshape(SEQ, HEADS, QK_NOPE), kr], axis=2).transpose(1, 0, 2).astype(BF16)
        v = _to_heads(kvp[:, n0:], V_HEAD).astype(BF16)
        o, lse = _mla_attn_fwd(q, k, v, name="mla_attn_fwd")
        o_flat = _from_heads(o).astype(BF16)
        f = _mm(o_flat, w_mo, name="mla_out")
        out = _post_fwd(f, xin, p, 1.0, name=f"post_fwd_{tag}")
        return out, dict(x=xin, hn=hn, cq=cq, ckv=ckv, cqn=cqn, ckvn=ckvn, q=q, k=k, v=v, o=o, lse=lse,
                         o_flat=o_flat, f=f, p=p, tag=tag)

    def dil_fwd(xin, i, sub):
        p = sub_params(i, sub)
        tag = f"l{i}s{sub}"
        hn = _pre_fwd(xin, p, name=f"pre_fwd_{tag}")
        heads = _proj_heads(hn, dil_w['in'], name="dil_proj")
        outs, lses = [], []
        for g, (window, d) in enumerate(DIL_GROUPS):
            o, lse = _dil_attn_fwd(heads, biases[g], g, d, name=f"dil_attn_fwd_g{g}")
            outs.append(o)
            lses.append(lse)
        o_flat = _dil_mix_fwd(outs, lses, name="dil_mix_fwd")
        f = _mm(o_flat, dil_w['out'], name="dil_out")
        out = _post_fwd(f, xin, p, 1.0, name=f"post_fwd_{tag}")
        return out, dict(x=xin, hn=hn, heads=heads, outs=outs, lses=lses, o_flat=o_flat, f=f, p=p, tag=tag)

    saved = [None] * 6
    xs, saved[0] = ffn_fwd(x2, 0, 0, 0, tie=flight_a['token'][0, 0] + flight_b['token'][0, 0])
    xs, saved[1] = mla_fwd(xs, 0, 1)
    ffn_w[0, 1] = as_ffn(*arrive(flight_a, xs))
    passed = {}

    def second_step(after):
        passed['st'], tok = pass_on(flight_b, after)
        return tok

    xs, saved[2] = ffn_fwd(xs, 0, 1, 2, mid=second_step)
    got, last_fams = lax.optimization_barrier((arrive_passed(passed['st'], xs), later_fams[2]))
    ffn_w[1, 0] = as_ffn(got[0], got[1])
    dil_w['in'], dil_w['out'] = got[2].reshape(N_SHARD, D_MODEL, -1), got[3].reshape(D_MODEL, D_MODEL)
    in_flight = gather_later(last_fams, "l1s2")
    xs, saved[3] = ffn_fwd(xs, 1, 0, 0, tie=in_flight['token'][0, 0])
    xs, saved[4] = dil_fwd(xs, 1, 1)
    ffn_w[1, 1] = as_ffn(*arrive(in_flight, xs))
    xs, saved[5] = ffn_fwd(xs, 1, 1, 2)

    dx, loss_part = _loss(xs, target, name="loss")

    dmod = [[None] * 9 for _ in range(2)]
    dpre = [[None] * 3 for _ in range(2)]
    dpost = [[None] * 3 for _ in range(2)]
    ffn_units = {}
    row_unit = lambda g, r, j: ((r % 2, r // 2), 0, j)

    def close_sub(dhn, dout, sv, i, sub, res_dgate, res_dqg):
        p = sv['p']
        dxs, dsh, dsc, dpg = _pre_bwd(dhn, sv['x'], dout, p, name=f"pre_bwd_{sv['tag']}")
        dmod[i][3 * sub], dmod[i][3 * sub + 1], dmod[i][3 * sub + 2] = dsh, dsc, res_dgate
        dpre[i][sub], dpost[i][sub] = dpg, res_dqg
        return dxs

    def ffn_bwd(dout, sv, sub, tie=0.0, mid=None):
        i, h, p, tag = sv['i'], sv['h'], sv['p'], sv['tag']
        w_gu, w_dn = ffn_w[i, h]
        df, dgate, dqg = _post_bwd(dout, sv['f'], p + tie, FFN_RES, name=f"post_bwd_{tag}")
        u_dn = _mm(sv['a'], df, ta=True, tn_cap=D_MODEL // 2, out_shape=(2, N_SHARD, F_SHARD, D_MODEL // 2),
                   out_sel=lambda g, r, j: ((j, g), r, 0), name=f"ffn_dwd_{tag}")
        dgu = _ffn_dgu(df, w_dn, sv['gu'], name=f"ffn_dgu_{tag}")
        if mid is not None:
            p = p + mid(dgu)
        u_gu = _mm(dgu.reshape(2 * N_SHARD, SEQ, F_SHARD), sv['hn'], ta=True,
                   out_shape=(2, N_SHARD, F_SHARD, D_MODEL), out_sel=lambda g, r, j: ((g % 2, g // 2), r, j),
                   name=f"ffn_dwgu_{tag}")
        ffn_units[i, h] = [u_gu, u_dn]
        dxs, dsh, dsc, dpg = _ffn_dhn(dgu, w_gu, sv['x'], dout, p, name=f"ffn_dhn_{tag}")
        dmod[i][3 * sub], dmod[i][3 * sub + 1], dmod[i][3 * sub + 2] = dsh, dsc, dgate
        dpre[i][sub], dpost[i][sub] = dpg, dqg
        return dxs

    def mla_bwd(dout, sv, i, sub, tie=0.0):
        p, tag = sv['p'], sv['tag']
        df, dgate, dqg = _post_bwd(dout, sv['f'], p + tie, 1.0, name=f"post_bwd_{tag}")
        u_wo = _mm(sv['o_flat'], df, ta=True, tm_cap=128, out_shape=(2, N_SHARD, 128, D_MODEL), out_sel=row_unit,
                   name="mla_dwo")
        do_flat = _mm(df, w_mo, tb=True, name="mla_do")
        do = _to_heads(do_flat, V_HEAD)
        dq, dk, dv = _mla_attn_bwd(sv['q'], sv['k'], sv['v'], sv['o'], do, sv['lse'], name="mla_attn_bwd")
        dq_t = dq.transpose(1, 0, 2)
        dqr1, dqr2 = _rope(dq_t[:, :, QK_NOPE:QK_NOPE + HALF_ROPE].reshape(SEQ, -1),
                           dq_t[:, :, QK_NOPE + HALF_ROPE:].reshape(SEQ, -1), cos_q, -sin_q, name="rope_q_bwd")
        dqp = jnp.concatenate([dq_t[:, :, :QK_NOPE].reshape(SEQ, -1), dqr1, dqr2], axis=1).astype(BF16)
        dkr = _head_sum(dk[:, :, QK_NOPE:], name="mla_dkr_sum")
        dk1, dk2 = _rope(dkr[:, :HALF_ROPE], dkr[:, HALF_ROPE:], cos_k, -sin_k, name="rope_k_bwd")
        dkvp = jnp.concatenate([_from_heads(dk[:, :, :QK_NOPE]), _from_heads(dv)], axis=1).astype(BF16)
        g_wq = _q_unperm(_mm(sv['cqn'], dqp, ta=True, name="mla_dwq"))
        g_wkv = _kv_unperm(_mm(sv['ckvn'], dkvp, ta=True, name="mla_dwkv"))
        dcqn = _mm(dqp, wq_p, tb=True, name="mla_dcqn")
        dckvn = _mm(dkvp, wkv_p, tb=True, name="mla_dckvn")
        dcq, g_qn = _rms_bwd(dcqn, sv['cq'], mla_q_norm, name="mla_qnorm_bwd")
        dckv, g_kvn = _rms_bwd(dckvn, sv['ckv'], mla_kv_norm, name="mla_kvnorm_bwd")
        dlat = jnp.concatenate([dcq, dckv, dk1, dk2], axis=1).astype(BF16)
        u_win = _mm(sv['hn'], dlat, ta=True, tm_cap=128, out_shape=(2, N_SHARD, 128, dlat.shape[1]),
                    out_sel=row_unit, name="mla_dwin")
        dhn = _mm(dlat, w_in, tb=True, name="mla_dhn")
        col_unit = lambda t: (t.reshape(t.shape[0], N_SHARD, -1).transpose(1, 0, 2)
                              .reshape(N_SHARD, 2, t.shape[0] // 2, -1).transpose(1, 0, 2, 3))
        grads = dict(units=[u_win, col_unit(g_wq), col_unit(g_wkv), u_wo], q_norm=g_qn, kv_norm=g_kvn)
        return close_sub(dhn, dout, sv, i, sub, dgate, dqg), grads

    def dil_bwd(dout, sv, i, sub):
        p, tag = sv['p'], sv['tag']
        df, dgate, dqg = _post_bwd(dout, sv['f'], p, 1.0, name=f"post_bwd_{tag}")
        u_wo = _mm(sv['o_flat'], df, ta=True, tm_cap=128, out_shape=(2, N_SHARD, 128, D_MODEL), out_sel=row_unit,
                   name="dil_dwo")
        dos, dlts = _dil_mix_bwd(_mm(df, dil_w['out'], tb=True, name="dil_do"), sv['outs'], sv['lses'],
                                 name="dil_mix_bwd")
        pieces = []
        bias_rows = []
        for g, (window, d) in enumerate(DIL_GROUPS):
            dq, dk, dv, dbias = _dil_attn_bwd(sv['heads'], biases[g], sv['lses'][g], dos[g], dlts[g], g, d,
                                              name=f"dil_attn_bwd_g{g}")
            pieces += [dq, dk, dv]
            bias_rows.append(_bias_grad(dbias, buckets[g], name=f"dil_bias_grad_g{g}")[:, 0, :])
        dheads = jnp.concatenate(pieces).astype(BF16)
        u_win = _proj_heads_dw(sv['hn'], dheads, name="dil_dwin")
        dhn = _proj_heads_dx(dheads, dil_w['in'], name="dil_dhn")
        g_bias = jnp.concatenate(bias_rows, axis=0).T
        grads = dict(units=[u_win, u_wo], rel_bias=g_bias)
        return close_sub(dhn, dout, sv, i, sub, dgate, dqg), grads

    def to_sibling(units, tag):
        n = len(units)
        send, recv, thru, token = _copies_start(units, [lax.empty(u.shape[1:], F32) for u in units], _sibling_plan, n,
                                                name=f"rs{tag}_sibling_start")
        return dict(send=send, recv=recv, thru=thru, n=n, tag=tag), token[0, 0]

    def from_sibling(st, after):
        n, tag = st['n'], st['tag']
        thru = _copies_wait(st['send'], st['recv'], st['thru'], n, _sibling_plan, after, name=f"rs{tag}_sibling_wait")
        return [_add_half(u, g, half_idx, name=f"rs{tag}_add_half_{k}") for k, (u, g) in enumerate(zip(thru[:n], thru[n:]))]

    def to_chips(parts, tag):
        n = len(parts)
        send, recv, thru, token = _copies_start([w for _, w in parts],
                                                [lax.empty((3,) + w.shape[1:], BF16) for _, w in parts], _chips_plan,
                                                3 * n, name=f"rs{tag}_chips_start")
        return dict(send=send, recv=recv, thru=thru, n=n, tag=tag, parts=parts), token[0, 0]

    def from_chips(st, after):
        n, tag = st['n'], st['tag']
        thru = _copies_wait(st['send'], st['recv'], st['thru'], n, _chips_plan, after, name=f"rs{tag}_chips_wait")
        return [_add_shards(p, g, shard_idx, name=f"rs{tag}_add_shards_{k}")
                for k, ((p, _), g) in enumerate(zip(st['parts'], thru[n:]))]

    dx = ffn_bwd(dx, saved[5], 2)
    dx, dil_g = dil_bwd(dx, saved[4], 1, 1)
    dx = ffn_bwd(dx, saved[3], 0)
    st1, tok = to_sibling([*ffn_units[1, 1], *dil_g['units'], *ffn_units[1, 0]], "1")
    dx = ffn_bwd(dx, saved[2], 2, tie=tok)
    st1, tok1 = to_chips(from_sibling(st1, dx), "1")
    st2, tok2 = to_sibling(ffn_units[0, 1], "2")
    dx, mla_g = mla_bwd(dx, saved[1], 0, 1, tie=tok1 + tok2)
    reds1 = from_chips(st1, dx)
    st2, tok = to_chips(from_sibling(st2, dx), "2")
    st3, tok3 = to_sibling(mla_g['units'], "3")
    onward = {}

    def mixer_to_chips(after):
        onward['st'], t = to_chips(from_sibling(st3, after), "3")
        return t

    dx = ffn_bwd(dx, saved[0], 0, tie=tok + tok3, mid=mixer_to_chips)
    reds2 = from_chips(st2, dx)
    reds3 = from_chips(onward['st'], dx)
    grad_x = dx[None]

    pad_row = lambda v: jnp.pad(v.reshape(-1), (0, (-v.size) % D_MODEL)).reshape(-1, D_MODEL)
    small = jnp.concatenate(
        [jnp.concatenate([dmod[i][r] for i in range(2) for r in range(9)], axis=0),
         jnp.concatenate([dpre[i][s] for i in range(2) for s in range(3)], axis=0),
         jnp.concatenate([dpost[i][s] for i in range(2) for s in range(3)], axis=0),
         pad_row(mla_g['q_norm']), pad_row(mla_g['kv_norm']), pad_row(dil_g['rel_bias']), pad_row(loss_part)], axis=0)
    small = jnp.pad(small, ((0, SMALL_ROWS - small.shape[0]), (0, 0)))
    small_all = _all_gather(small, name="ag_small_grads", in_vmem=True)
    small_sum = _sum_devices(small_all, 8, name="sum_small_grads")
    g_b_mod = small_sum[0:18].reshape(2, 9 * D_MODEL)
    my_cols = lambda t: lax.dynamic_slice_in_dim(t, shard_id * 256, 256, axis=2)
    g_norm_pre = my_cols(small_sum[18:24].reshape(2, 3, D_MODEL))
    g_norm_post = my_cols(small_sum[24:30].reshape(2, 3, D_MODEL))
    g_q_norm = small_sum[30, :Q_LORA].reshape(1, Q_LORA)
    g_kv_norm = small_sum[31, :KV_LORA].reshape(1, KV_LORA)
    g_rel_bias = small_sum[32:34].reshape(-1)[:N_BUCKETS * 48].reshape(N_BUCKETS, 48)
    loss = small_sum[34, 0]
    dmod_all = small_all.reshape(8, SMALL_ROWS, D_MODEL)[:, 0:18].reshape(8, 2, 9 * D_MODEL)
    dmod_cols = lax.dynamic_slice_in_dim(dmod_all, shard_id * 2304, 2304, axis=2).transpose(1, 0, 2)

    swap = lambda t: jnp.swapaxes(t, 2, 3)
    grads = dict(norm_pre=g_norm_pre, norm_post=g_norm_post, b_mod=g_b_mod, mla_q_norm=g_q_norm,
                 mla_kv_norm=g_kv_norm, rel_bias=g_rel_bias)
    deltas, new_m, new_v = {}, {}, {}

    def adamw(names):
        for n in names:
            view = swap if n in ('ffn_w_gate', 'ffn_w_up') else (lambda t: t)
            outs = _adamw(view(given[n]), view(grads[n]), view(given["m_" + n]), view(given["v_" + n]),
                          name=f"adamw_{n}")
            deltas[n], new_m[n], new_v[n] = (view(t) for t in outs)

    st0, tok = to_sibling(ffn_units[0, 0], "0")
    grads['w_mod'] = _mm(silu_c, (dmod_cols + tok).astype(BF16), ta=True, tn_cap=768, name="w_mod_grad")
    adamw(['w_mod'])
    st0, tok = to_chips(from_sibling(st0, deltas['w_mod']), "0")
    grads['b_mod'] = grads['b_mod'] + tok
    fin = _pair_gather(reds1 + reds2 + reds3, name="rs_pair_gather")
    for n, t in zip(['dil_w_in', 'dil_w_o', 'mla_w_in', 'mla_w_q_up', 'mla_w_kv_up', 'mla_w_o'], fin[2:4] + fin[8:12]):
        grads[n] = t.reshape(given[n].shape)
    adamw(['b_mod', 'dil_w_in', 'dil_w_o', 'mla_w_in', 'mla_w_q_up', 'mla_w_kv_up', 'mla_w_o', 'norm_pre', 'norm_post',
           'mla_q_norm', 'mla_kv_norm', 'rel_bias'])
    reds0 = from_chips(st0, deltas['dil_w_in'])
    fin0 = _pair_gather(reds0, name="rs_pair_gather_last")
    ffn_fin = {(1, 1): fin[0:2], (1, 0): fin[4:6], (0, 1): fin[6:8], (0, 0): fin0}
    per_ffn = lambda pick: jnp.stack([jnp.stack([pick(*ffn_fin[i, h]) for h in range(2)]) for i in range(2)])
    grads.update(ffn_w_gate=swap(per_ffn(lambda gu, dn: gu[0])), ffn_w_up=swap(per_ffn(lambda gu, dn: gu[1])),
                 ffn_w_down=per_ffn(lambda gu, dn: jnp.concatenate([dn[0], dn[1]], axis=1)))
    adamw(['ffn_w_gate', 'ffn_w_up', 'ffn_w_down'])
    return (loss, grad_x, *[grads[n] for n in WEIGHTS], *[deltas[n] for n in WEIGHTS],
            *[new_m[n] for n in WEIGHTS], *[new_v[n] for n in WEIGHTS])
```

```python
import math

import jax
import jax.numpy as jnp
from jax import lax
from jax.experimental import pallas as pl
from jax.experimental.pallas import tpu as pltpu

F32 = jnp.float32
BF16 = jnp.bfloat16
MESH = pl.DeviceIdType.MESH

SEQ = 2048
D_MODEL = 1024
D_FF = 2816
N_SHARD = 4
F_SHARD = D_FF // N_SHARD
EPS = 1e-6
FFN_RES = 0.5
HEADS = 16
Q_LORA, KV_LORA, QK_NOPE, QK_ROPE, V_HEAD = 384, 256, 64, 32, 64
HALF_ROPE = QK_ROPE // 2
ROPE_THETA = 10000.0
DIL_GROUPS = ((128, 1), (512, 4), (2048, 16))
DIL_BLOCK = 128
N_BUCKETS = 32
MAX_DISTANCE = 2048
ADAM_LR, ADAM_B1, ADAM_B2, ADAM_EPS, ADAM_WD, ADAM_STEP = 0.001, 0.9, 0.999, 1e-08, 0.01, 10

VMEM_LIMIT = 48 * 1024 * 1024
SMALL_ROWS = 40

WEIGHTS = ['norm_pre', 'norm_post', 'w_mod', 'b_mod', 'ffn_w_gate', 'ffn_w_up', 'ffn_w_down', 'mla_w_in',
           'mla_q_norm', 'mla_w_q_up', 'mla_kv_norm', 'mla_w_kv_up', 'mla_w_o', 'dil_w_in', 'dil_w_o', 'rel_bias']


def _cparams(**kw):
    return pltpu.CompilerParams(vmem_limit_bytes=VMEM_LIMIT, **kw)


def _pick(n, cap, mult=128):
    if n <= cap:
        return n
    best = n
    for t in range(mult, cap + 1, mult):
        if n % t == 0:
            best = t
    return best


def _mm(a, b, *, name, ta=False, tb=False, reduce_g=False, bias=None, out_dtype=F32, tm_cap=1024, tn_cap=1024,
        g_n=None, b_sel=None, out_shape=None, out_sel=None, out_buf=None):
    a3 = a if a.ndim == 3 else a[None]
    ga = a3.shape[0]
    if b_sel is None:
        b_n = b if b.ndim == 3 else b[None]
        gb = b_n.shape[0]
        b_sel = (lambda g: (g,)) if gb > 1 else (lambda g: (0,))
        g_n = max(ga, gb)
    else:
        b_n = b
    k_dim, m_dim = (a3.shape[1], a3.shape[2]) if ta else (a3.shape[2], a3.shape[1])
    k2, n_dim = (b_n.shape[-1], b_n.shape[-2]) if tb else (b_n.shape[-2], b_n.shape[-1])
    assert k_dim == k2, (a.shape, b.shape)
    tm = _pick(m_dim, tm_cap, 128 if ta else 8)
    tn = _pick(n_dim, tn_cap, 128)
    mt, nt = m_dim // tm, n_dim // tn
    dims = (((0 if ta else 1,), (1 if tb else 0,)), ((), ()))

    if reduce_g:
        grid = (mt, nt, g_n)
        ids = lambda i, j, g: (g, i, j)
    else:
        grid = (g_n, mt, nt)
        ids = lambda g, i, j: (g, i, j)

    def a_map(*p):
        g, i, j = ids(*p)
        g = g if ga > 1 else 0
        return (g, 0, i) if ta else (g, i, 0)

    def b_map(*p):
        g, i, j = ids(*p)
        return (*b_sel(g), j, 0) if tb else (*b_sel(g), 0, j)

    b_lead = (None,) * (b_n.ndim - 2)
    a_spec = pl.BlockSpec((None, k_dim, tm) if ta else (None, tm, k_dim), a_map)
    b_spec = pl.BlockSpec(b_lead + ((tn, k_dim) if tb else (k_dim, tn)), b_map)
    in_specs = [a_spec, b_spec]
    operands = [a3, b_n]
    if bias is not None:
        assert not reduce_g and bias.shape == (g_n, 1, n_dim)
        in_specs.append(pl.BlockSpec((None, 1, tn), lambda g, i, j: (g, 0, j)))
        operands.append(bias)
    aliases = {}
    if out_buf is not None:
        assert tuple(out_buf.shape) == tuple(out_shape) and out_buf.dtype == out_dtype
        in_specs.append(pl.BlockSpec(memory_space=pl.ANY))
        operands.append(out_buf)
        aliases = {len(operands) - 1: 0}

    if reduce_g:
        out_spec = pl.BlockSpec((tm, tn), lambda i, j, g: (i, j))
        out_sds = jax.ShapeDtypeStruct((m_dim, n_dim), F32)
    elif out_shape is not None:
        def o_map(g, i, j):
            lead, rb, cb = out_sel(g, i, j)
            return (*lead, rb, cb)

        out_spec = pl.BlockSpec((None,) * (len(out_shape) - 2) + (tm, tn), o_map)
        out_sds = jax.ShapeDtypeStruct(tuple(out_shape), out_dtype)
    else:
        out_spec = pl.BlockSpec((None, tm, tn), lambda g, i, j: (g, i, j))
        out_sds = jax.ShapeDtypeStruct((g_n, m_dim, n_dim), out_dtype)

    def body(a_ref, b_ref, *rest):
        o_ref = rest[-1]
        r = lax.dot_general(a_ref[...].astype(BF16), b_ref[...].astype(BF16), dims, preferred_element_type=F32)
        if bias is not None:
            r = r + rest[0][...]
        if reduce_g:
            g = pl.program_id(2)

            @pl.when(g == 0)
            def _():
                o_ref[...] = r

            @pl.when(g > 0)
            def _():
                o_ref[...] += r
        else:
            o_ref[...] = r.astype(o_ref.dtype)

    out = pl.pallas_call(body, grid=grid, in_specs=in_specs, out_specs=out_spec, out_shape=out_sds,
                         input_output_aliases=aliases, compiler_params=_cparams(), name=name)(*operands)
    if not reduce_g and out_shape is None and a.ndim == 2 and b.ndim == 2:
        out = out[0]
    return out


def _rows(tm, w):
    return pl.BlockSpec((tm, w), lambda i: (i, 0))


def _vec(w):
    return pl.BlockSpec((1, w), lambda i: (0, 0))


def _rstd(v):
    return lax.rsqrt(jnp.mean(v * v, axis=-1, keepdims=True) + EPS)


V_PG, V_QG, V_SH, V_SC, V_GATE = range(5)


def _vrow(v_ref, k):
    return v_ref[k:k + 1, :]


def _vecs(w):
    return pl.BlockSpec((8, w), lambda *_: (0, 0))


def _pre_fwd(x, vp, *, name):
    s_n, w = x.shape
    tm = _pick(s_n, 512, 8)

    def body(x_ref, v_ref, o_ref):
        xv = x_ref[...]
        n = (xv * _rstd(xv)) * _vrow(v_ref, V_PG)
        o_ref[...] = (n * (1.0 + _vrow(v_ref, V_SC)) + _vrow(v_ref, V_SH)).astype(o_ref.dtype)

    return pl.pallas_call(body, grid=(s_n // tm,), in_specs=[_rows(tm, w), _vecs(w)],
                          out_specs=_rows(tm, w), out_shape=jax.ShapeDtypeStruct((s_n, w), BF16),
                          compiler_params=_cparams(), name=name)(x, vp)


def _post_fwd(f, x, vp, res_w, *, name):
    s_n, w = x.shape
    tm = _pick(s_n, 512, 8)

    def body(f_ref, x_ref, v_ref, o_ref):
        fv = f_ref[...]
        y = (fv * _rstd(fv)) * _vrow(v_ref, V_QG)
        o_ref[...] = x_ref[...] + (res_w * _vrow(v_ref, V_GATE)) * y

    return pl.pallas_call(body, grid=(s_n // tm,), in_specs=[_rows(tm, w), _rows(tm, w), _vecs(w)],
                          out_specs=_rows(tm, w), out_shape=jax.ShapeDtypeStruct((s_n, w), F32),
                          compiler_params=_cparams(), name=name)(f, x, vp)


def _post_bwd(dout, f, vp, res_w, *, name):
    s_n, w = f.shape
    tm = _pick(s_n, 512, 8)

    def body(do_ref, f_ref, v_ref, df_ref, dgate_ref, dqg_ref):
        @pl.when(pl.program_id(0) == 0)
        def _():
            dgate_ref[...] = jnp.zeros_like(dgate_ref)
            dqg_ref[...] = jnp.zeros_like(dqg_ref)

        do = do_ref[...]
        fv = f_ref[...]
        r = _rstd(fv)
        fh = fv * r
        qg_v = _vrow(v_ref, V_QG)
        dgate_ref[...] += res_w * jnp.sum(do * (fh * qg_v), axis=0, keepdims=True)
        dy = do * (res_w * _vrow(v_ref, V_GATE))
        dqg_ref[...] += jnp.sum(dy * fh, axis=0, keepdims=True)
        dfh = dy * qg_v
        df = r * (dfh - fh * jnp.mean(dfh * fh, axis=-1, keepdims=True))
        df_ref[...] = df.astype(df_ref.dtype)

    return pl.pallas_call(
        body, grid=(s_n // tm,), in_specs=[_rows(tm, w), _rows(tm, w), _vecs(w)],
        out_specs=[_rows(tm, w), _vec(w), _vec(w)],
        out_shape=[jax.ShapeDtypeStruct((s_n, w), BF16), jax.ShapeDtypeStruct((1, w), F32),
                   jax.ShapeDtypeStruct((1, w), F32)],
        compiler_params=_cparams(), name=name)(dout, f, vp)


def _pre_bwd(dhn, x, dout, vp, *, name):
    s_n, w = x.shape
    tm = _pick(s_n, 512, 8)

    def body(dhn_ref, x_ref, do_ref, v_ref, dx_ref, dsh_ref, dsc_ref, dpg_ref):
        @pl.when(pl.program_id(0) == 0)
        def _():
            dsh_ref[...] = jnp.zeros_like(dsh_ref)
            dsc_ref[...] = jnp.zeros_like(dsc_ref)
            dpg_ref[...] = jnp.zeros_like(dpg_ref)

        dhn_v = dhn_ref[...]
        xv = x_ref[...]
        r = _rstd(xv)
        xh = xv * r
        pg_v = _vrow(v_ref, V_PG)
        dsh_ref[...] += jnp.sum(dhn_v, axis=0, keepdims=True)
        dsc_ref[...] += jnp.sum(dhn_v * (xh * pg_v), axis=0, keepdims=True)
        dn = dhn_v * (1.0 + _vrow(v_ref, V_SC))
        dpg_ref[...] += jnp.sum(dn * xh, axis=0, keepdims=True)
        dxh = dn * pg_v
        dx_ref[...] = do_ref[...] + r * (dxh - xh * jnp.mean(dxh * xh, axis=-1, keepdims=True))

    vec = jax.ShapeDtypeStruct((1, w), F32)
    return pl.pallas_call(
        body, grid=(s_n // tm,), in_specs=[_rows(tm, w), _rows(tm, w), _rows(tm, w), _vecs(w)],
        out_specs=[_rows(tm, w), _vec(w), _vec(w), _vec(w)],
        out_shape=[jax.ShapeDtypeStruct((s_n, w), F32), vec, vec, vec],
        compiler_params=_cparams(), name=name)(dhn, x, dout, vp)


def _rms_fwd(x, g, *, name):
    s_n, w = x.shape
    tm = _pick(s_n, 512, 8)

    def body(x_ref, g_ref, o_ref):
        xv = x_ref[...]
        o_ref[...] = ((xv * _rstd(xv)) * g_ref[...]).astype(o_ref.dtype)

    return pl.pallas_call(body, grid=(s_n // tm,), in_specs=[_rows(tm, w), _vec(w)], out_specs=_rows(tm, w),
                          out_shape=jax.ShapeDtypeStruct((s_n, w), BF16), compiler_params=_cparams(),
                          name=name)(x, g)


def _rms_bwd(dy, x, g, *, name):
    s_n, w = x.shape
    tm = _pick(s_n, 512, 8)

    def body(dy_ref, x_ref, g_ref, dx_ref, dg_ref):
        @pl.when(pl.program_id(0) == 0)
        def _():
            dg_ref[...] = jnp.zeros_like(dg_ref)

        dy_v = dy_ref[...]
        xv = x_ref[...]
        r = _rstd(xv)
        xh = xv * r
        dg_ref[...] += jnp.sum(dy_v * xh, axis=0, keepdims=True)
        dxh = dy_v * g_ref[...]
        dx_ref[...] = r * (dxh - xh * jnp.mean(dxh * xh, axis=-1, keepdims=True))

    return pl.pallas_call(
        body, grid=(s_n // tm,), in_specs=[_rows(tm, w), _rows(tm, w), _vec(w)],
        out_specs=[_rows(tm, w), _vec(w)],
        out_shape=[jax.ShapeDtypeStruct((s_n, w), F32), jax.ShapeDtypeStruct((1, w), F32)],
        compiler_params=_cparams(), name=name)(dy, x, g)


def _rope(a1, a2, cos, sin, *, name):
    s_n, w = a1.shape
    tm = _pick(s_n, 512, 8)

    def body(a1_ref, a2_ref, c_ref, s_ref, r1_ref, r2_ref):
        u, v, c_v, s_v = a1_ref[...], a2_ref[...], c_ref[...], s_ref[...]
        r1_ref[...] = u * c_v - v * s_v
        r2_ref[...] = u * s_v + v * c_v

    sd = jax.ShapeDtypeStruct((s_n, w), F32)
    return pl.pallas_call(body, grid=(s_n // tm,), in_specs=[_rows(tm, w)] * 4, out_specs=[_rows(tm, w)] * 2,
                          out_shape=[sd, sd], compiler_params=_cparams(), name=name)(a1, a2, cos, sin)


def _silu_bf16(x, *, name):
    def body(x_ref, o_ref):
        xv = x_ref[...]
        o_ref[...] = (xv * jax.nn.sigmoid(xv)).astype(o_ref.dtype)

    return pl.pallas_call(body, out_shape=jax.ShapeDtypeStruct(x.shape, BF16), name=name)(x)


def _loss(y, target, *, name):
    s_n, w = y.shape
    tm = _pick(s_n, 512, 8)

    def body(y_ref, t_ref, dy_ref, l_ref):
        @pl.when(pl.program_id(0) == 0)
        def _():
            l_ref[...] = jnp.zeros_like(l_ref)

        e = y_ref[...] - t_ref[...]
        dy_ref[...] = e * (1.0 / w)
        row = jnp.mean(e * e, axis=-1, keepdims=True)
        l_ref[...] += 0.5 * jnp.sum(row, axis=0, keepdims=True)

    return pl.pallas_call(
        body, grid=(s_n // tm,), in_specs=[_rows(tm, w), _rows(tm, w)],
        out_specs=[_rows(tm, w), pl.BlockSpec((1, 1), lambda i: (0, 0))],
        out_shape=[jax.ShapeDtypeStruct((s_n, w), F32), jax.ShapeDtypeStruct((1, 1), F32)],
        compiler_params=_cparams(), name=name)(y, target)


FFN_TM = 512
FFN_TM_WIDE = 1024


def _ffn_up(hn, w_gu, *, name):
    s_n, d = hn.shape
    f = w_gu.shape[-1]
    tm = _pick(s_n, FFN_TM_WIDE, 8)

    def body(hn_ref, wg_ref, wu_ref, gu_ref, a_ref):
        xv = hn_ref[...]
        g = jnp.dot(xv, wg_ref[...], preferred_element_type=F32)
        u = jnp.dot(xv, wu_ref[...], preferred_element_type=F32)
        gu_ref[0] = g.astype(BF16)
        gu_ref[1] = u.astype(BF16)
        a_ref[...] = ((g * jax.nn.sigmoid(g)) * u).astype(BF16)

    w_blk = lambda t: pl.BlockSpec((None, None, d, f), lambda s, m: (s, t, 0, 0))
    return pl.pallas_call(
        body, grid=(N_SHARD, s_n // tm),
        in_specs=[pl.BlockSpec((tm, d), lambda s, m: (m, 0)), w_blk(0), w_blk(1)],
        out_specs=[pl.BlockSpec((None, 2, tm, f), lambda s, m: (s, 0, m, 0)),
                   pl.BlockSpec((None, tm, f), lambda s, m: (s, m, 0))],
        out_shape=[jax.ShapeDtypeStruct((N_SHARD, 2, s_n, f), BF16), jax.ShapeDtypeStruct((N_SHARD, s_n, f), BF16)],
        compiler_params=_cparams(), name=name)(hn, w_gu, w_gu)


def _ffn_down(a, w_dn, x, vp, res_w, *, name):
    _, s_n, f = a.shape
    d = w_dn.shape[-1]
    tm = _pick(s_n, FFN_TM, 8)
    a = a.reshape(-1, 2, s_n, f)
    w_dn = w_dn.reshape(-1, 2, f, d)
    g_n = a.shape[0]

    def body(a_ref, w_ref, x_ref, v_ref, f_ref, o_ref):
        g = pl.program_id(1)
        r = (jnp.dot(a_ref[0], w_ref[0], preferred_element_type=F32)
             + jnp.dot(a_ref[1], w_ref[1], preferred_element_type=F32))

        @pl.when(g == 0)
        def _():
            f_ref[...] = r

        @pl.when(g > 0)
        def _():
            f_ref[...] += r

        @pl.when(g == g_n - 1)
        def _():
            fv = f_ref[...]
            y = (fv * _rstd(fv)) * _vrow(v_ref, V_QG)
            o_ref[...] = x_ref[...] + (res_w * _vrow(v_ref, V_GATE)) * y

    row = pl.BlockSpec((tm, d), lambda m, g: (m, 0))
    sd = jax.ShapeDtypeStruct((s_n, d), F32)
    return pl.pallas_call(
        body, grid=(s_n // tm, g_n),
        in_specs=[pl.BlockSpec((None, 2, tm, f), lambda m, g: (g, 0, m, 0)),
                  pl.BlockSpec((None, 2, f, d), lambda m, g: (g, 0, 0, 0)), row, _vecs(d)],
        out_specs=[row, row], out_shape=[sd, sd], compiler_params=_cparams(), name=name)(a, w_dn, x, vp)


def _ffn_dhn(dgu, w_gu, x, dout, vp, *, name):
    g_n, _, s_n, f = dgu.shape
    d = w_gu.shape[-2]
    tm = _pick(s_n, FFN_TM, 8)
    nt_dims = (((1,), (1,)), ((), ()))

    def body(a_ref, w_ref, x_ref, do_ref, v_ref, dx_ref, dsh_ref, dsc_ref, dpg_ref, acc_ref):
        m, g = pl.program_id(0), pl.program_id(1)
        r = (lax.dot_general(a_ref[0], w_ref[0], nt_dims, preferred_element_type=F32)
             + lax.dot_general(a_ref[1], w_ref[1], nt_dims, preferred_element_type=F32))

        @pl.when(g == 0)
        def _():
            acc_ref[...] = r

        @pl.when(g > 0)
        def _():
            acc_ref[...] += r

        @pl.when((m == 0) & (g == 0))
        def _():
            dsh_ref[...] = jnp.zeros_like(dsh_ref)
            dsc_ref[...] = jnp.zeros_like(dsc_ref)
            dpg_ref[...] = jnp.zeros_like(dpg_ref)

        @pl.when(g == g_n - 1)
        def _():
            dhn_v = acc_ref[...]
            xv = x_ref[...]
            rs = _rstd(xv)
            xh = xv * rs
            pg_v = _vrow(v_ref, V_PG)
            dsh_ref[...] += jnp.sum(dhn_v, axis=0, keepdims=True)
            dsc_ref[...] += jnp.sum(dhn_v * (xh * pg_v), axis=0, keepdims=True)
            dn = dhn_v * (1.0 + _vrow(v_ref, V_SC))
            dpg_ref[...] += jnp.sum(dn * xh, axis=0, keepdims=True)
            dxh = dn * pg_v
            dx_ref[...] = do_ref[...] + rs * (dxh - xh * jnp.mean(dxh * xh, axis=-1, keepdims=True))

    row = pl.BlockSpec((tm, d), lambda m, g: (m, 0))
    vec = pl.BlockSpec((1, d), lambda m, g: (0, 0))
    vsd = jax.ShapeDtypeStruct((1, d), F32)
    return pl.pallas_call(
        body, grid=(s_n // tm, g_n),
        in_specs=[pl.BlockSpec((None, 2, tm, f), lambda m, g: (g, 0, m, 0)),
                  pl.BlockSpec((None, 2, d, f), lambda m, g: (g, 0, 0, 0)), row, row, _vecs(d)],
        out_specs=[row, vec, vec, vec], out_shape=[jax.ShapeDtypeStruct((s_n, d), F32), vsd, vsd, vsd],
        scratch_shapes=[pltpu.VMEM((tm, d), F32)], compiler_params=_cparams(), name=name)(dgu, w_gu, x, dout, vp)


def _ffn_dgu(df, w_dn, gu, *, name):
    s_n, d = df.shape
    f = w_dn.shape[-2]
    tm = _pick(s_n, FFN_TM_WIDE, 8)

    def body(df_ref, wd_ref, gu_ref, o_ref):
        da = lax.dot_general(df_ref[...], wd_ref[...], (((1,), (1,)), ((), ())), preferred_element_type=F32)
        g = gu_ref[0].astype(F32)
        u = gu_ref[1].astype(F32)
        sig = jax.nn.sigmoid(g)
        o_ref[0] = (da * u * (sig * (1.0 + g * (1.0 - sig)))).astype(BF16)
        o_ref[1] = (da * (g * sig)).astype(BF16)

    gu_blk = pl.BlockSpec((None, 2, tm, f), lambda s, m: (s, 0, m, 0))
    return pl.pallas_call(
        body, grid=(N_SHARD, s_n // tm),
        in_specs=[pl.BlockSpec((tm, d), lambda s, m: (m, 0)),
                  pl.BlockSpec((None, f, d), lambda s, m: (s, 0, 0)), gu_blk],
        out_specs=gu_blk, out_shape=jax.ShapeDtypeStruct((N_SHARD, 2, s_n, f), BF16),
        compiler_params=_cparams(), name=name)(df, w_dn, gu)


_NT = (((1,), (1,)), ((), ()))
_TN = (((0,), (0,)), ((), ()))
MLA_TQ = 256


def _causal_mask(i, tq, s_n):
    qpos = i * tq + lax.broadcasted_iota(jnp.int32, (tq, s_n), 0)
    kpos = lax.broadcasted_iota(jnp.int32, (tq, s_n), 1)
    return kpos <= qpos


def _mla_attn_fwd(q, k, v, *, name):
    h_n, s_n, dq = q.shape
    dv = v.shape[-1]
    tq = MLA_TQ
    scale = float(dq) ** -0.5

    def body(q_ref, k_ref, v_ref, o_ref, lse_ref):
        i = pl.program_id(1)
        for e in range(1, s_n // tq + 1):
            @pl.when(i == e - 1)
            def _(ext=e * tq):
                mask = _causal_mask(i, tq, ext)
                s = lax.dot_general(q_ref[...], k_ref[0:ext, :], _NT, preferred_element_type=F32) * scale
                s = jnp.where(mask, s, -jnp.inf)
                m = jnp.max(s, axis=-1, keepdims=True)
                p = jnp.exp(s - m)
                l = jnp.sum(p, axis=-1, keepdims=True)
                o = jnp.dot(p.astype(BF16), v_ref[0:ext, :], preferred_element_type=F32)
                o_ref[...] = o / l
                lse_ref[...] = m + jnp.log(l)

    return pl.pallas_call(
        body, grid=(h_n, s_n // tq),
        in_specs=[pl.BlockSpec((None, tq, dq), lambda h, i: (h, i, 0)),
                  pl.BlockSpec((None, s_n, dq), lambda h, i: (h, 0, 0)),
                  pl.BlockSpec((None, s_n, dv), lambda h, i: (h, 0, 0))],
        out_specs=[pl.BlockSpec((None, tq, dv), lambda h, i: (h, i, 0)),
                   pl.BlockSpec((None, tq, 1), lambda h, i: (h, i, 0))],
        out_shape=[jax.ShapeDtypeStruct((h_n, s_n, dv), F32), jax.ShapeDtypeStruct((h_n, s_n, 1), F32)],
        compiler_params=_cparams(), name=name)(q, k, v)


def _mla_attn_bwd(q, k, v, o, do, lse, *, name):
    h_n, s_n, dq = q.shape
    dv = v.shape[-1]
    tq = MLA_TQ
    scale = float(dq) ** -0.5

    def body(q_ref, k_ref, v_ref, o_ref, do_ref, lse_ref, dq_ref, dk_ref, dv_ref):
        i = pl.program_id(1)

        @pl.when(i == 0)
        def _():
            dk_ref[...] = jnp.zeros_like(dk_ref)
            dv_ref[...] = jnp.zeros_like(dv_ref)

        for e in range(1, s_n // tq + 1):
            @pl.when(i == e - 1)
            def _(ext=e * tq):
                mask = _causal_mask(i, tq, ext)
                qv, kv, vv = q_ref[...], k_ref[0:ext, :], v_ref[0:ext, :]
                do_v = do_ref[...]
                s = lax.dot_general(qv, kv, _NT, preferred_element_type=F32) * scale
                p = jnp.where(mask, jnp.exp(s - lse_ref[...]), 0.0)
                dob = do_v.astype(BF16)
                dv_ref[0:ext, :] += lax.dot_general(p.astype(BF16), dob, _TN, preferred_element_type=F32)
                dp = lax.dot_general(dob, vv, _NT, preferred_element_type=F32)
                delta = jnp.sum(do_v * o_ref[...], axis=-1, keepdims=True)
                dsb = (p * (dp - delta) * scale).astype(BF16)
                dq_ref[...] = jnp.dot(dsb, kv, preferred_element_type=F32)
                dk_ref[0:ext, :] += lax.dot_general(dsb, qv, _TN, preferred_element_type=F32)

    return pl.pallas_call(
        body, grid=(h_n, s_n // tq),
        in_specs=[pl.BlockSpec((None, tq, dq), lambda h, i: (h, i, 0)),
                  pl.BlockSpec((None, s_n, dq), lambda h, i: (h, 0, 0)),
                  pl.BlockSpec((None, s_n, dv), lambda h, i: (h, 0, 0)),
                  pl.BlockSpec((None, tq, dv), lambda h, i: (h, i, 0)),
                  pl.BlockSpec((None, tq, dv), lambda h, i: (h, i, 0)),
                  pl.BlockSpec((None, tq, 1), lambda h, i: (h, i, 0))],
        out_specs=[pl.BlockSpec((None, tq, dq), lambda h, i: (h, i, 0)),
                   pl.BlockSpec((None, s_n, dq), lambda h, i: (h, 0, 0)),
                   pl.BlockSpec((None, s_n, dv), lambda h, i: (h, 0, 0))],
        out_shape=[jax.ShapeDtypeStruct((h_n, s_n, dq), F32), jax.ShapeDtypeStruct((h_n, s_n, dq), F32),
                   jax.ShapeDtypeStruct((h_n, s_n, dv), F32)],
        compiler_params=_cparams(), name=name)(q, k, v, o, do, lse)


def _head_sum(x, *, name):
    h_n, s_n, w = x.shape
    tm = _pick(s_n, 512, 8)

    def body(x_ref, o_ref):
        o_ref[...] = jnp.sum(x_ref[...], axis=0)

    return pl.pallas_call(body, grid=(s_n // tm,), in_specs=[pl.BlockSpec((h_n, tm, w), lambda i: (0, i, 0))],
                          out_specs=_rows(tm, w), out_shape=jax.ShapeDtypeStruct((s_n, w), F32),
                          compiler_params=_cparams(), name=name)(x)


N_BLK = SEQ // DIL_BLOCK
DIL_SCALE = 64 ** -0.5


def _dil_masks():
    iq = lax.broadcasted_iota(jnp.int32, (DIL_BLOCK, 2 * DIL_BLOCK), 0)
    ik = lax.broadcasted_iota(jnp.int32, (DIL_BLOCK, 2 * DIL_BLOCK), 1)
    rel = DIL_BLOCK + iq - ik
    both = (rel >= 0) & (rel <= DIL_BLOCK)
    iq1 = lax.broadcasted_iota(jnp.int32, (DIL_BLOCK, DIL_BLOCK), 0)
    ik1 = lax.broadcasted_iota(jnp.int32, (DIL_BLOCK, DIL_BLOCK), 1)
    return both, ik1 <= iq1


def _dil_block(j, d):
    nb = SEQ // d // DIL_BLOCK
    r, n = divmod(j, nb)
    first = n == 0
    rows = lambda start, size: pl.ds(start, size) if d == 1 else pl.ds(start, size, stride=d)
    q_rows = rows(n * DIL_BLOCK * d + r, DIL_BLOCK)
    k_rows = q_rows if first else rows((n - 1) * DIL_BLOCK * d + r, 2 * DIL_BLOCK)
    return q_rows, k_rows, (DIL_BLOCK if first else 0), first


PAIR = 2 * 64
N_PAIR = HEADS // 2


def _dil_head_specs(s_n, g):
    return [pl.BlockSpec((None, s_n, PAIR), lambda hp, t=t: ((g * 3 + t) * N_PAIR + hp, 0, 0)) for t in range(3)]


def _pair_specs(s_n, w):
    return pl.BlockSpec((2, s_n, w), lambda hp: (hp, 0, 0))


_PAIR_BIAS = pl.BlockSpec((2, DIL_BLOCK, 2 * DIL_BLOCK), lambda hp: (hp, 0, 0))


def _dil_attn_fwd(heads, bias, g, d, *, name):
    _, s_n, _ = heads.shape
    e = PAIR // 2

    def body(q_ref, k_ref, v_ref, b_ref, o_ref, lse_ref):
        m_both, m_first = _dil_masks()
        for j in range(N_BLK):
            q_rows, k_rows, b_lo, first = _dil_block(j, d)
            q2 = q_ref[q_rows, :].astype(BF16)
            k2 = k_ref[k_rows, :].astype(BF16)
            v2 = v_ref[k_rows, :].astype(BF16)
            for hh in range(2):
                cols = slice(hh * e, (hh + 1) * e)
                s = (lax.dot_general(q2[:, cols], k2[:, cols], _NT, preferred_element_type=F32) * DIL_SCALE
                     + b_ref[hh, :, b_lo:])
                s = jnp.where(m_first if first else m_both, s, -jnp.inf)
                m = jnp.max(s, axis=-1, keepdims=True)
                lse = m + jnp.log(jnp.sum(jnp.exp(s - m), axis=-1, keepdims=True))
                p = jnp.exp(s - lse)
                o_ref[hh, q_rows, :] = jnp.dot(p.astype(BF16), v2[:, cols], preferred_element_type=F32)
                lse_ref[hh, q_rows, :] = lse

    return pl.pallas_call(
        body, grid=(N_PAIR,), in_specs=_dil_head_specs(s_n, g) + [_PAIR_BIAS],
        out_specs=[_pair_specs(s_n, e), _pair_specs(s_n, 1)],
        out_shape=[jax.ShapeDtypeStruct((HEADS, s_n, e), F32), jax.ShapeDtypeStruct((HEADS, s_n, 1), F32)],
        compiler_params=_cparams(), name=name)(heads, heads, heads, bias)


def _dil_attn_bwd(heads, bias, lse, do, dlt, g, d, *, name):
    _, s_n, _ = heads.shape
    e = PAIR // 2

    def body(q_ref, k_ref, v_ref, b_ref, lse_ref, do_ref, dlt_ref, dq_ref, dk_ref, dv_ref, db_ref):
        db_ref[...] = jnp.zeros_like(db_ref)
        m_both, m_first = _dil_masks()
        nb = s_n // d // DIL_BLOCK
        own_v = own_k = own_rows = None
        for j in range(N_BLK):
            q_rows, k_rows, b_lo, first = _dil_block(j, d)
            q2 = q_ref[q_rows, :].astype(BF16)
            k2 = k_ref[k_rows, :].astype(BF16)
            v2 = v_ref[k_rows, :].astype(BF16)
            dq_h, dv_h, dk_h = [], [], []
            for hh in range(2):
                cols = slice(hh * e, (hh + 1) * e)
                qj, kk, vv = q2[:, cols], k2[:, cols], v2[:, cols]
                s = lax.dot_general(qj, kk, _NT, preferred_element_type=F32) * DIL_SCALE + b_ref[hh, :, b_lo:]
                p = jnp.where(m_first if first else m_both, jnp.exp(s - lse_ref[hh, q_rows, :]), 0.0)
                dob = do_ref[hh, q_rows, :].astype(BF16)
                dv_h.append(lax.dot_general(p.astype(BF16), dob, _TN, preferred_element_type=F32))
                dp = lax.dot_general(dob, vv, _NT, preferred_element_type=F32)
                ds = p * (dp - dlt_ref[hh, q_rows, :])
                db_ref[hh, :, b_lo:] += ds
                dsb = (ds * DIL_SCALE).astype(BF16)
                dq_h.append(jnp.dot(dsb, kk, preferred_element_type=F32))
                dk_h.append(lax.dot_general(dsb, qj, _TN, preferred_element_type=F32))
            dq_ref[q_rows, :] = jnp.concatenate(dq_h, axis=1)
            dvv, dkk = jnp.concatenate(dv_h, axis=1), jnp.concatenate(dk_h, axis=1)
            if not first:
                dv_ref[own_rows, :] = own_v + dvv[:DIL_BLOCK]
                dk_ref[own_rows, :] = own_k + dkk[:DIL_BLOCK]
                dvv, dkk = dvv[DIL_BLOCK:], dkk[DIL_BLOCK:]
            own_v, own_k, own_rows = dvv, dkk, q_rows
            if j % nb == nb - 1:
                dv_ref[own_rows, :] = own_v
                dk_ref[own_rows, :] = own_k

    slab = pl.BlockSpec((None, s_n, PAIR), lambda hp: (hp, 0, 0))
    sd = jax.ShapeDtypeStruct((N_PAIR, s_n, PAIR), F32)
    return pl.pallas_call(
        body, grid=(N_PAIR,),
        in_specs=_dil_head_specs(s_n, g) + [_PAIR_BIAS, _pair_specs(s_n, 1), _pair_specs(s_n, e), _pair_specs(s_n, 1)],
        out_specs=[slab, slab, slab, _PAIR_BIAS],
        out_shape=[sd, sd, sd, jax.ShapeDtypeStruct((HEADS, DIL_BLOCK, 2 * DIL_BLOCK), F32)],
        compiler_params=_cparams(), name=name)(heads, heads, heads, bias, lse, do, dlt)


def _proj_heads(x, w, *, name):
    s_n, k = x.shape
    n = w.shape[-1]
    tm, tn, e = 1024, 768, PAIR
    per_blk, n_blk = tn // e, n // tn

    def body(x_ref, w_ref, o_ref):
        r = jnp.dot(x_ref[...], w_ref[...], preferred_element_type=F32)
        for j in range(per_blk):
            o_ref[j] = r[:, e * j:e * (j + 1)]

    return pl.pallas_call(
        body, grid=(w.shape[0], n_blk, s_n // tm),
        in_specs=[pl.BlockSpec((tm, k), lambda s, b, m: (m, 0)), pl.BlockSpec((None, k, tn), lambda s, b, m: (s, 0, b))],
        out_specs=pl.BlockSpec((per_blk, tm, e), lambda s, b, m: (s * n_blk + b, m, 0)),
        out_shape=jax.ShapeDtypeStruct((w.shape[0] * n // e, s_n, e), F32), compiler_params=_cparams(),
        name=name)(x, w)


def _heads_cat(d_ref):
    return jnp.concatenate([d_ref[j] for j in range(d_ref.shape[0])], axis=1)


def _proj_heads_dw(x, dh, *, name):
    s_n, k = x.shape
    tn, e = 768, PAIR
    per_blk = tn // e
    n_blk = dh.shape[0] // N_SHARD // per_blk
    n = n_blk * tn

    def body(x_ref, d_ref, o_ref):
        o_ref[...] = lax.dot_general(x_ref[...], _heads_cat(d_ref), _TN, preferred_element_type=F32)

    return pl.pallas_call(
        body, grid=(N_SHARD, n_blk, 2),
        in_specs=[pl.BlockSpec((s_n, k // 2), lambda s, b, r: (0, r)),
                  pl.BlockSpec((per_blk, s_n, e), lambda s, b, r: (s * n_blk + b, 0, 0))],
        out_specs=pl.BlockSpec((None, None, k // 2, tn), lambda s, b, r: (r, s, 0, b)),
        out_shape=jax.ShapeDtypeStruct((2, N_SHARD, k // 2, n), F32), compiler_params=_cparams(), name=name)(x, dh)


def _proj_heads_dx(dh, w, *, name):
    k, n = w.shape[1:]
    s_n = dh.shape[1]
    tm, tn, e = 512, 1152, PAIR
    per_blk, n_blk = tn // e, n // tn

    def body(d_ref, w_ref, o_ref):
        r = lax.dot_general(_heads_cat(d_ref), w_ref[...], _NT, preferred_element_type=F32)
        g = pl.program_id(1)

        @pl.when(g == 0)
        def _():
            o_ref[...] = r

        @pl.when(g > 0)
        def _():
            o_ref[...] += r

    return pl.pallas_call(
        body, grid=(s_n // tm, N_SHARD * n_blk),
        in_specs=[pl.BlockSpec((per_blk, tm, e), lambda m, g: (g, m, 0)),
                  pl.BlockSpec((None, k, tn), lambda m, g: (g // n_blk, 0, g % n_blk))],
        out_specs=pl.BlockSpec((tm, k), lambda m, g: (m, 0)),
        out_shape=jax.ShapeDtypeStruct((s_n, k), F32), compiler_params=_cparams(), name=name)(dh, w)


def _group_alpha(ls):
    m = jnp.maximum(jnp.maximum(ls[0], ls[1]), ls[2])
    es = [jnp.exp(l - m) for l in ls]
    tot = es[0] + es[1] + es[2]
    return [ex / tot for ex in es]


def _dil_mix_fwd(os_, ls_, *, name):
    h_n, s_n, e = os_[0].shape
    tm = 1024

    def body(o0, o1, o2, l0, l1, l2, out_ref):
        for hh in range(2):
            al = _group_alpha([l[hh] for l in (l0, l1, l2)])
            mix = al[0] * o0[hh] + al[1] * o1[hh] + al[2] * o2[hh]
            out_ref[:, hh * e:(hh + 1) * e] = mix.astype(out_ref.dtype)

    blk = lambda w: pl.BlockSpec((2, tm, w), lambda h, i: (h, i, 0))
    return pl.pallas_call(body, grid=(h_n // 2, s_n // tm), in_specs=[blk(e)] * 3 + [blk(1)] * 3,
                          out_specs=pl.BlockSpec((tm, 2 * e), lambda h, i: (i, h)),
                          out_shape=jax.ShapeDtypeStruct((s_n, h_n * e), BF16), compiler_params=_cparams(),
                          name=name)(*os_, *ls_)


def _dil_mix_bwd(do_flat, os_, ls_, *, name):
    h_n, s_n, e = os_[0].shape
    tm = 1024

    def body(do_ref, o0, o1, o2, l0, l1, l2, d0, d1, d2, t0, t1, t2):
        for hh in range(2):
            al = _group_alpha([l[hh] for l in (l0, l1, l2)])
            do_v = do_ref[:, hh * e:(hh + 1) * e]
            mix = al[0] * o0[hh] + al[1] * o1[hh] + al[2] * o2[hh]
            dbar = jnp.sum(do_v * mix, axis=-1, keepdims=True)
            for a_g, d_ref, t_ref in zip(al, (d0, d1, d2), (t0, t1, t2)):
                d_ref[hh] = a_g * do_v
                t_ref[hh] = a_g * dbar

    blk = lambda w: pl.BlockSpec((2, tm, w), lambda h, i: (h, i, 0))
    sd_e = jax.ShapeDtypeStruct((h_n, s_n, e), F32)
    sd_1 = jax.ShapeDtypeStruct((h_n, s_n, 1), F32)
    outs = pl.pallas_call(body, grid=(h_n // 2, s_n // tm),
                          in_specs=[pl.BlockSpec((tm, 2 * e), lambda h, i: (i, h))] + [blk(e)] * 3 + [blk(1)] * 3,
                          out_specs=[blk(e)] * 3 + [blk(1)] * 3, out_shape=[sd_e] * 3 + [sd_1] * 3,
                          compiler_params=_cparams(), name=name)(do_flat, *os_, *ls_)
    return outs[:3], outs[3:]


def _bias_grad(ds, bucket, *, name):
    h_n = ds.shape[0]

    def body(ds_ref, bk_ref, o_ref):
        ds_v = ds_ref[...]
        bk = bk_ref[...]
        lane = lax.broadcasted_iota(jnp.int32, (1, N_BUCKETS), 1)
        acc = jnp.zeros((1, N_BUCKETS), F32)
        for b in range(N_BUCKETS):
            tot = jnp.sum(jnp.sum(jnp.where(bk == b, ds_v, 0.0), axis=1, keepdims=True), axis=0, keepdims=True)
            acc = acc + jnp.where(lane == b, tot, 0.0)
        o_ref[...] = acc

    return pl.pallas_call(
        body, grid=(h_n,),
        in_specs=[pl.BlockSpec((None, DIL_BLOCK, 2 * DIL_BLOCK), lambda h: (h, 0, 0)),
                  pl.BlockSpec((DIL_BLOCK, 2 * DIL_BLOCK), lambda h: (0, 0))],
        out_specs=pl.BlockSpec((None, 1, N_BUCKETS), lambda h: (h, 0, 0)),
        out_shape=jax.ShapeDtypeStruct((h_n, 1, N_BUCKETS), F32), compiler_params=_cparams(), name=name)(ds, bucket)


def _bias_table(rb, bucket, *, name):
    h_n = rb.shape[0]

    def body(rb_ref, bk_ref, o_ref):
        bk = bk_ref[...]
        row = rb_ref[...]
        acc = jnp.zeros(bk.shape, F32)
        for b in range(N_BUCKETS):
            acc = jnp.where(bk == b, row[:, b:b + 1], acc)
        o_ref[...] = acc

    return pl.pallas_call(
        body, grid=(h_n,),
        in_specs=[pl.BlockSpec((None, 1, N_BUCKETS), lambda h: (h, 0, 0)),
                  pl.BlockSpec((DIL_BLOCK, 2 * DIL_BLOCK), lambda h: (0, 0))],
        out_specs=pl.BlockSpec((None, DIL_BLOCK, 2 * DIL_BLOCK), lambda h: (h, 0, 0)),
        out_shape=jax.ShapeDtypeStruct((h_n, DIL_BLOCK, 2 * DIL_BLOCK), F32), compiler_params=_cparams(),
        name=name)(rb, bucket)


def _row_tile(rows, cols, budget=2 << 20):
    if rows * cols * 4 <= budget or rows % 8:
        return rows
    best = 8
    for t in range(8, rows + 1, 8):
        if rows % t == 0 and t * cols * 4 <= budget:
            best = t
    return best


def _adamw(w, g, m, v, *, name):
    shape = w.shape
    cols = shape[-1]
    rows = math.prod(shape[:-1]) if len(shape) > 1 else 1
    to2 = lambda t: t.reshape(rows, cols)
    tr = _row_tile(rows, cols)
    c1 = 1.0 / (1.0 - ADAM_B1 ** ADAM_STEP)
    c2 = 1.0 / (1.0 - ADAM_B2 ** ADAM_STEP)

    def body(w_ref, g_ref, m_ref, v_ref, d_ref, nm_ref, nv_ref):
        g_v = g_ref[...]
        nm = ADAM_B1 * m_ref[...] + (1.0 - ADAM_B1) * g_v
        nv = ADAM_B2 * v_ref[...] + (1.0 - ADAM_B2) * (g_v * g_v)
        m_hat = nm * c1
        v_hat = nv * c2
        d_ref[...] = -ADAM_LR * (m_hat / (jnp.sqrt(v_hat) + ADAM_EPS) + ADAM_WD * w_ref[...])
        nm_ref[...] = nm
        nv_ref[...] = nv

    blk = pl.BlockSpec((tr, cols), lambda i: (i, 0))
    sd = jax.ShapeDtypeStruct((rows, cols), F32)
    outs = pl.pallas_call(body, grid=(rows // tr,), in_specs=[blk] * 4, out_specs=[blk] * 3, out_shape=[sd] * 3,
                          compiler_params=_cparams(), name=name)(to2(w), to2(g), to2(m), to2(v))
    return tuple(t.reshape(shape) for t in outs)


def _add_half(unit, got, half_idx, *, name):
    rest = unit.shape[2:]
    c = rest[-1]
    r = math.prod(rest[:-1])
    tr = _row_tile(r, c)

    def body(idx_ref, u_ref, g_ref, o_ref, w_ref):
        tot = u_ref[...] + g_ref[...].astype(F32)
        o_ref[...] = tot
        w_ref[...] = tot.astype(BF16)

    blk = pl.BlockSpec((None, tr, c), lambda s, i, idx: (s, i, 0))
    grid_spec = pltpu.PrefetchScalarGridSpec(
        num_scalar_prefetch=1, grid=(N_SHARD, r // tr),
        in_specs=[pl.BlockSpec((None, None, tr, c), lambda s, i, idx: (idx[0], s, i, 0)), blk],
        out_specs=[blk, blk])
    out, wire = pl.pallas_call(
        body, grid_spec=grid_spec,
        out_shape=[jax.ShapeDtypeStruct((N_SHARD, r, c), F32), jax.ShapeDtypeStruct((N_SHARD, r, c), BF16)],
        compiler_params=_cparams(), name=name)(half_idx, unit.reshape(2, N_SHARD, r, c), got.reshape(N_SHARD, r, c))
    return out.reshape((N_SHARD,) + rest), wire.reshape((N_SHARD,) + rest)


def _add_shards(part, got, shard_idx, *, name):
    rest = part.shape[1:]
    c = rest[-1]
    r = math.prod(rest[:-1])
    tr = _row_tile(r, c)

    def body(idx_ref, p_ref, g_ref, o_ref):
        acc = p_ref[...]
        for k in range(3):
            acc = acc + g_ref[k].astype(F32)
        o_ref[...] = acc

    grid_spec = pltpu.PrefetchScalarGridSpec(
        num_scalar_prefetch=1, grid=(r // tr,),
        in_specs=[pl.BlockSpec((None, tr, c), lambda i, idx: (idx[0], i, 0)),
                  pl.BlockSpec((3, tr, c), lambda i, idx: (0, i, 0))],
        out_specs=pl.BlockSpec((tr, c), lambda i, idx: (i, 0)))
    out = pl.pallas_call(body, grid_spec=grid_spec, out_shape=jax.ShapeDtypeStruct((r, c), F32),
                         compiler_params=_cparams(), name=name)(
        shard_idx, part.reshape(N_SHARD, r, c), got.reshape(3, r, c))
    return out.reshape(rest)


def _sum_devices(x, n_dev, *, name):
    rows = x.shape[0] // n_dev

    def body(x_ref, o_ref):
        acc = x_ref[0:rows, :]
        for d in range(1, n_dev):
            acc = acc + x_ref[d * rows:(d + 1) * rows, :]
        o_ref[...] = acc

    return pl.pallas_call(body, out_shape=jax.ShapeDtypeStruct((rows, x.shape[1]), F32), name=name)(x)


def _my_pos():
    return lax.axis_index("x"), lax.axis_index("y"), lax.axis_index("c")


def _all_gather(x_blk, *, name, in_vmem):
    m_per, n = x_blk.shape

    def body(x_ref, out_ref, send_sems, recv_sems, local_sem):
        x, y, c = _my_pos()
        me, sibling = (x, y, c), (x, y, 1 - c)
        chips = [(1 - x, y), (x, 1 - y), (1 - x, 1 - y)]

        def rows(px, py, pc):
            return out_ref.at[pl.ds((4 * px + 2 * py + pc) * m_per, m_per), :]

        def copy(k, block, to, src=None):
            return pltpu.make_async_remote_copy(
                src_ref=rows(*block) if src is None else src, dst_ref=rows(*block),
                send_sem=send_sems.at[k], recv_sem=recv_sems.at[k], device_id=to, device_id_type=MESH)

        mine = pltpu.make_async_copy(x_ref, rows(*me), local_sem)
        mine.start()
        first = [copy(0, me, sibling, src=x_ref)]
        first += [copy(1 + j, me, (*chip, c), src=x_ref) for j, chip in enumerate(chips)]
        for cp in first:
            cp.start()
        passed = [copy(4 + j, (*chip, c), sibling) for j, chip in enumerate(chips)]
        for j, chip in enumerate(chips):
            copy(1 + j, (*chip, c), me).wait_recv()
            passed[j].start()
        copy(0, sibling, me).wait_recv()
        for j, chip in enumerate(chips):
            copy(4 + j, (*chip, 1 - c), me).wait_recv()
        for cp in first + passed:
            cp.wait_send()
        mine.wait()

    space = pltpu.VMEM if in_vmem else pl.ANY
    return pl.pallas_call(
        body, out_shape=jax.ShapeDtypeStruct((8 * m_per, n), x_blk.dtype),
        in_specs=[pl.BlockSpec(memory_space=space)], out_specs=pl.BlockSpec(memory_space=space),
        scratch_shapes=[pltpu.SemaphoreType.DMA((7,)), pltpu.SemaphoreType.DMA((7,)), pltpu.SemaphoreType.DMA],
        name=name)(x_blk)


_HBM = pl.BlockSpec(memory_space=pl.ANY)


def _gather_weights(fams, *, name):
    n = len(fams)

    def body(*refs):
        ins, outs = refs[:n], refs[n:2 * n]
        send_sems, recv_sems = refs[2 * n:]
        x, y, c = _my_pos()
        me, sibling = (x, y, c), (x, y, 1 - c)
        chips = [(1 - x, y), (x, 1 - y), (1 - x, 1 - y)]

        def copy(f, k, block, to, src=None):
            px, py, pc = block
            dst = outs[f].at[2 * px + py, pc]
            return pltpu.make_async_remote_copy(
                src_ref=dst if src is None else src, dst_ref=dst, send_sem=send_sems.at[7 * f + k],
                recv_sem=recv_sems.at[7 * f + k], device_id=to, device_id_type=MESH)

        first, passed = [], []
        for f in range(n):
            src = ins[f].at[c]
            first.append(copy(f, 0, me, sibling, src=src))
            first += [copy(f, 1 + j, me, (*chip, c), src=src) for j, chip in enumerate(chips)]
        for cp in first:
            cp.start()
        for j, chip in enumerate(chips):
            for f in range(n):
                copy(f, 1 + j, (*chip, c), me).wait_recv()
                passed.append(copy(f, 4 + j, (*chip, c), sibling))
                passed[-1].start()
        for f in range(n):
            copy(f, 0, sibling, me).wait_recv()
        for j, chip in enumerate(chips):
            for f in range(n):
                copy(f, 4 + j, (*chip, 1 - c), me).wait_recv()
        for cp in first + passed:
            cp.wait_send()

    outs = pl.pallas_call(
        body, out_shape=[jax.ShapeDtypeStruct((N_SHARD,) + t.shape, t.dtype) for t in fams],
        in_specs=[_HBM] * n, out_specs=[_HBM] * n,
        scratch_shapes=[pltpu.SemaphoreType.DMA((7 * n,)), pltpu.SemaphoreType.DMA((7 * n,))], name=name)(*fams)
    return [_place_own(o, t) for o, t in zip(outs, fams)]


def _pair_gather(halves, *, name):
    n = len(halves)

    def body(*refs):
        ins, outs = refs[:n], refs[n:2 * n]
        send_sems, recv_sems = refs[2 * n:]
        x, y, c = _my_pos()
        cps = [pltpu.make_async_remote_copy(src_ref=ins[f], dst_ref=outs[f].at[c], send_sem=send_sems.at[f],
                                            recv_sem=recv_sems.at[f], device_id=(x, y, 1 - c), device_id_type=MESH)
               for f in range(n)]
        for cp in cps:
            cp.start()
        for f in range(n):
            pltpu.make_async_remote_copy(src_ref=ins[f], dst_ref=outs[f].at[1 - c], send_sem=send_sems.at[f],
                                         recv_sem=recv_sems.at[f], device_id=(x, y, 1 - c),
                                         device_id_type=MESH).wait_recv()
        for cp in cps:
            cp.wait_send()

    outs = pl.pallas_call(
        body, out_shape=[jax.ShapeDtypeStruct((2,) + t.shape, t.dtype) for t in halves],
        in_specs=[_HBM] * n, out_specs=[_HBM] * n,
        scratch_shapes=[pltpu.SemaphoreType.DMA((n,)), pltpu.SemaphoreType.DMA((n,))], name=name)(*halves)
    c = lax.axis_index("c")
    return [lax.dynamic_update_index_in_dim(o, t, c, 0) for o, t in zip(outs, halves)]


_HBM_ONLY = pl.BlockSpec(memory_space=pltpu.HBM)
_SEMS = pl.BlockSpec(memory_space=pltpu.SEMAPHORE)
_EFFECT = pltpu.SideEffectType.DATAFLOW_SIDE_EFFECTING


def _copies_start(srcs, lands, plan, n_copies, *, name):
    n, m = len(srcs), len(lands)

    def body(*refs):
        src_refs, land_refs = refs[:n], refs[n:n + m]
        send_sems, recv_sems, token = refs[n + m], refs[n + m + 1], refs[-1]
        for k, (src, dst, peer) in enumerate(plan(src_refs, land_refs)):
            pltpu.make_async_remote_copy(src_ref=src, dst_ref=dst, send_sem=send_sems.at[k], recv_sem=recv_sems.at[k],
                                         device_id=peer, device_id_type=MESH).start()
        token[...] = jnp.zeros_like(token)

    bufs = [pltpu.with_memory_space_constraint(t, pltpu.HBM) for t in (*srcs, *lands)]
    outs = pl.pallas_call(
        body, name=name,
        out_shape=(pltpu.SemaphoreType.DMA((n_copies,)), pltpu.SemaphoreType.DMA((n_copies,)),
                   *[pltpu.HBM(t.shape, t.dtype) for t in bufs], jax.ShapeDtypeStruct((8, 128), F32)),
        in_specs=[_HBM_ONLY] * (n + m),
        out_specs=(_SEMS, _SEMS, *[_HBM_ONLY] * (n + m), pl.BlockSpec(memory_space=pltpu.VMEM)),
        input_output_aliases={k: 2 + k for k in range(n + m)},
        compiler_params=pltpu.CompilerParams(has_side_effects=_EFFECT))(*bufs)
    return outs[0], outs[1], list(outs[2:2 + n + m]), outs[-1]


def _copies_wait(send_sems, recv_sems, thru, n_src, plan, after, *, name):
    nm = len(thru)

    def body(*refs):
        t_refs, send, recv = refs[:nm], refs[nm], refs[nm + 1]
        for k, (src, dst, peer) in enumerate(plan(t_refs[:n_src], t_refs[n_src:])):
            cp = pltpu.make_async_remote_copy(src_ref=src, dst_ref=dst, send_sem=send.at[k], recv_sem=recv.at[k],
                                              device_id=peer, device_id_type=MESH)
            cp.wait_send()
            cp.wait_recv()

    outs = pl.pallas_call(
        body, name=name, out_shape=tuple(pltpu.HBM(t.shape, t.dtype) for t in thru),
        in_specs=[_HBM_ONLY] * nm + [_SEMS, _SEMS, pl.BlockSpec(memory_space=pl.ANY)],
        out_specs=tuple([_HBM_ONLY] * nm), input_output_aliases={k: k for k in range(nm)},
        compiler_params=pltpu.CompilerParams(has_side_effects=_EFFECT))(*thru, send_sems, recv_sems, after)
    return list(outs)


_RELATIONS = [(dx, dy, dc) for dx in (0, 1) for dy in (0, 1) for dc in (0, 1)][1:]


def _gather_plan(src_refs, land_refs):
    x, y, c = _my_pos()
    flip = lambda v, d: 1 - v if d else v
    return [(s_ref.at[c], l_ref.at[2 * x + y, c], (flip(x, dx), flip(y, dy), flip(c, dc)))
            for s_ref, l_ref in zip(src_refs, land_refs) for dx, dy, dc in _RELATIONS]


def _gather_chips_plan(src_refs, land_refs):
    x, y, c = _my_pos()
    peers = [(x, y, 1 - c), (1 - x, y, c), (x, 1 - y, c), (1 - x, 1 - y, c)]
    return [(s_ref.at[c], l_ref.at[2 * x + y, c], peer) for s_ref, l_ref in zip(src_refs, land_refs) for peer in peers]


def _gather_pass_plan(src_refs, land_refs):
    x, y, c = _my_pos()
    chips = [(1 - x, y), (x, 1 - y), (1 - x, 1 - y)]
    return [(l_ref.at[2 * cx + cy, c], l_ref.at[2 * cx + cy, c], (x, y, 1 - c))
            for l_ref in land_refs for cx, cy in chips]


def _sibling_plan(src_refs, land_refs):
    x, y, c = _my_pos()
    return [(s_ref.at[1 - c], l_ref, (x, y, 1 - c)) for s_ref, l_ref in zip(src_refs, land_refs)]


def _chips_plan(src_refs, land_refs):
    x, y, c = _my_pos()
    chips = [(1 - x, y), (x, 1 - y), (1 - x, 1 - y)]
    return [(s_ref.at[2 * cx + cy], l_ref.at[k], (cx, cy, c))
            for s_ref, l_ref in zip(src_refs, land_refs) for k, (cx, cy) in enumerate(chips)]


def _place_own(gathered, fam):
    x, y, c = _my_pos()
    own = lax.dynamic_index_in_dim(fam, c, 0, keepdims=True)[None]
    return lax.dynamic_update_slice(gathered, own, (2 * x + y, c) + (0,) * (fam.ndim - 1))


def _to_heads(t, width):
    return t.reshape(t.shape[0], HEADS, width).transpose(1, 0, 2)


def _from_heads(t):
    return t.transpose(1, 0, 2).reshape(t.shape[1], -1)


def _t5_bucket(dist):
    max_exact = N_BUCKETS // 2
    d = jnp.maximum(dist, 1).astype(F32)
    large = max_exact + (jnp.log(d / max_exact) / math.log(MAX_DISTANCE / max_exact)
                         * (N_BUCKETS - max_exact)).astype(jnp.int32)
    large = jnp.minimum(large, N_BUCKETS - 1)
    return jnp.where(dist < max_exact, dist, large)


def _bucket_map(dilation):
    iq = jnp.arange(DIL_BLOCK)[:, None]
    ik = jnp.arange(2 * DIL_BLOCK)[None, :]
    rel = DIL_BLOCK + iq - ik
    return _t5_bucket(jnp.maximum(rel, 0) * dilation).astype(jnp.int32)


def _q_perm(w):
    w3 = w.reshape(w.shape[0], HEADS, QK_NOPE + QK_ROPE)
    return jnp.concatenate([w3[:, :, :QK_NOPE].reshape(w.shape[0], -1),
                            w3[:, :, QK_NOPE:QK_NOPE + HALF_ROPE].reshape(w.shape[0], -1),
                            w3[:, :, QK_NOPE + HALF_ROPE:].reshape(w.shape[0], -1)], axis=1)


def _q_unperm(w):
    n0, n1 = HEADS * QK_NOPE, HEADS * HALF_ROPE
    r = w.shape[0]
    return jnp.concatenate([w[:, :n0].reshape(r, HEADS, QK_NOPE), w[:, n0:n0 + n1].reshape(r, HEADS, HALF_ROPE),
                            w[:, n0 + n1:].reshape(r, HEADS, HALF_ROPE)], axis=2).reshape(r, -1)


def _kv_perm(w):
    w3 = w.reshape(w.shape[0], HEADS, QK_NOPE + V_HEAD)
    return jnp.concatenate([w3[:, :, :QK_NOPE].reshape(w.shape[0], -1), w3[:, :, QK_NOPE:].reshape(w.shape[0], -1)],
                           axis=1)


def _kv_unperm(w):
    n0 = HEADS * QK_NOPE
    r = w.shape[0]
    return jnp.concatenate([w[:, :n0].reshape(r, HEADS, QK_NOPE), w[:, n0:].reshape(r, HEADS, V_HEAD)],
                           axis=2).reshape(r, -1)


def _row(v):
    return v.reshape(1, -1)


def kernel(x, c, norm_pre, norm_post, w_mod, b_mod, ffn_w_gate, ffn_w_up, ffn_w_down, mla_w_in, mla_q_norm, mla_w_q_up, mla_kv_norm, mla_w_kv_up, mla_w_o, dil_w_in, dil_w_o, rel_bias, loss_target, m_norm_pre, m_norm_post, m_w_mod, m_b_mod, m_ffn_w_gate, m_ffn_w_up, m_ffn_w_down, m_mla_w_in, m_mla_q_norm, m_mla_w_q_up, m_mla_kv_norm, m_mla_w_kv_up, m_mla_w_o, m_dil_w_in, m_dil_w_o, m_rel_bias, v_norm_pre, v_norm_post, v_w_mod, v_b_mod, v_ffn_w_gate, v_ffn_w_up, v_ffn_w_down, v_mla_w_in, v_mla_q_norm, v_mla_w_q_up, v_mla_kv_norm, v_mla_w_kv_up, v_mla_w_o, v_dil_w_in, v_dil_w_o, v_rel_bias):
    given = dict(locals())
    ix, iy, ic = _my_pos()
    shard_id = 2 * ix + iy
    dev_id = 4 * ix + 2 * iy + ic
    x2 = x[0]
    target = loss_target[0]
    half_idx = jnp.reshape(ic, (1,)).astype(jnp.int32)
    shard_idx = jnp.reshape(shard_id, (1,)).astype(jnp.int32)

    blk = jnp.zeros((8, D_MODEL), F32)
    blk = blk.at[0].set(c[0])
    blk = blk.at[1:3].set(jnp.pad(norm_pre.reshape(-1), (0, 512)).reshape(2, D_MODEL))
    blk = blk.at[3:5].set(jnp.pad(norm_post.reshape(-1), (0, 512)).reshape(2, D_MODEL))
    got = _all_gather(blk, name="ag_c_norms", in_vmem=True).reshape(N_SHARD, 2, 8, D_MODEL)
    c_all = got[:, :, 0, :].reshape(8, D_MODEL)

    def full_norm(lo):
        t = got[:, 0, lo:lo + 2, :].reshape(N_SHARD, 2 * D_MODEL)[:, :1536].reshape(N_SHARD, 2, 3, 256)
        return t.transpose(1, 2, 0, 3).reshape(2, 3, D_MODEL)

    pre_full, post_full = full_norm(1), full_norm(3)

    silu_c = _silu_bf16(c_all, name="silu_c")
    b_cols = lax.dynamic_slice_in_dim(b_mod, shard_id * 2304, 2304, axis=1).reshape(2, 1, 2304)
    mod_part = _mm(silu_c, w_mod, bias=b_cols, name="mod_mm", tn_cap=768)
    mod_all = _all_gather(mod_part.reshape(16, 2304), name="ag_mod", in_vmem=True)
    mod_all = mod_all.reshape(N_SHARD, 2, 2, 8, 2304)[:, 0]
    mod_mine = lax.dynamic_index_in_dim(mod_all, dev_id, axis=2, keepdims=False)
    mod = mod_mine.transpose(1, 0, 2).reshape(2, 9, D_MODEL)

    bf = lambda t: t.astype(BF16)
    ffn_fam = lambda i, h: [bf(jnp.stack([ffn_w_gate[i, h], ffn_w_up[i, h]])),
                            bf(ffn_w_down[i, h].reshape(2, F_SHARD // 2, D_MODEL))]
    mla_fam = [bf(mla_w_in.reshape(2, 128, -1)), bf(mla_w_q_up.reshape(2, 192, -1)),
               bf(mla_w_kv_up.reshape(2, 128, -1)), bf(mla_w_o.reshape(2, 128, D_MODEL))]
    dil_fam = [bf(dil_w_in.reshape(2, 512, -1)), bf(dil_w_o.reshape(2, 128, D_MODEL))]
    later_fams = [ffn_fam(0, 1), ffn_fam(1, 0) + dil_fam, ffn_fam(1, 1)]
    full, later_fams, mod = lax.optimization_barrier(
        (_gather_weights(ffn_fam(0, 0) + mla_fam, name="ag_weights_first"), later_fams, mod))

    def gather_later(fams, tag):
        lands = [lax.empty((N_SHARD,) + t.shape, t.dtype) for t in fams]
        send, recv, thru, token = _copies_start(fams, lands, _gather_plan, 7 * len(fams), name=f"ag_start_{tag}")
        return dict(send=send, recv=recv, thru=thru, token=token, n=len(fams), tag=tag)

    def arrive(st, after):
        thru = _copies_wait(st['send'], st['recv'], st['thru'], st['n'], _gather_plan, after,
                            name=f"ag_wait_{st['tag']}")
        return [_place_own(o, t) for t, o in zip(thru[:st['n']], thru[st['n']:])]

    def gather_chips(fams, tag):
        lands = [lax.empty((N_SHARD,) + t.shape, t.dtype) for t in fams]
        send, recv, thru, token = _copies_start(fams, lands, _gather_chips_plan, 4 * len(fams), name=f"ag_start_{tag}")
        return dict(send=send, recv=recv, thru=thru, token=token, n=len(fams), tag=tag)

    def pass_on(st, after):
        n, tag = st['n'], st['tag']
        thru = _copies_wait(st['send'], st['recv'], st['thru'], n, _gather_chips_plan, after, name=f"ag_mid_{tag}")
        send, recv, lands, token = _copies_start([], thru[n:], _gather_pass_plan, 3 * n, name=f"ag_pass_{tag}")
        return dict(send=send, recv=recv, thru=lands, fams=thru[:n], tag=tag), token[0, 0]

    def arrive_passed(st, after):
        lands = _copies_wait(st['send'], st['recv'], st['thru'], 0, _gather_pass_plan, after, name=f"ag_wait_{st['tag']}")
        return [_place_own(o, t) for t, o in zip(st['fams'], lands)]

    flight_a = gather_later(later_fams[0], "l0s2")
    _, next_fams = lax.optimization_barrier((flight_a['token'], later_fams[1]))
    flight_b = gather_chips(next_fams, "l1s01")
    as_ffn = lambda w_gu, w_dn: (w_gu, w_dn.reshape(N_SHARD, F_SHARD, D_MODEL))
    ffn_w = {(0, 0): as_ffn(full[0], full[1])}
    w_in = full[2].reshape(D_MODEL, -1)
    wq_p = _q_perm(full[3].reshape(N_SHARD, Q_LORA, -1).transpose(1, 0, 2).reshape(Q_LORA, -1))
    wkv_p = _kv_perm(full[4].reshape(N_SHARD, KV_LORA, -1).transpose(1, 0, 2).reshape(KV_LORA, -1))
    w_mo = full[5].reshape(D_MODEL, D_MODEL)
    dil_w = {}

    pos = jnp.arange(SEQ, dtype=F32)
    freqs = ROPE_THETA ** (-jnp.arange(HALF_ROPE, dtype=F32) / HALF_ROPE)
    ang = pos[:, None] * freqs[None, :]
    cos_k, sin_k = jnp.cos(ang), jnp.sin(ang)
    cos_q, sin_q = jnp.tile(cos_k, (1, HEADS)), jnp.tile(sin_k, (1, HEADS))

    buckets = [_bucket_map(d) for _, d in DIL_GROUPS]
    biases = [_bias_table(rel_bias[:, g * HEADS:(g + 1) * HEADS].T.reshape(HEADS, 1, N_BUCKETS), bk,
                          name=f"dil_bias_table_g{g}") for g, bk in enumerate(buckets)]

    vpacks = jnp.concatenate([pre_full[:, :, None], post_full[:, :, None], mod.reshape(2, 3, 3, D_MODEL),
                              jnp.zeros((2, 3, 3, D_MODEL), F32)], axis=2)
    sub_params = lambda i, sub: vpacks[i, sub]

    def ffn_fwd(xin, i, h, sub, tie=None, mid=None):
        p = sub_params(i, sub)
        if tie is not None:
            p = p + tie
        tag = f"l{i}s{sub}"
        w_gu, w_dn = ffn_w[i, h]
        hn = _pre_fwd(xin, p, name=f"pre_fwd_{tag}")
        gu, a = _ffn_up(hn, w_gu, name=f"ffn_up_{tag}")
        if mid is not None:
            p = p + mid(a)
        f, out = _ffn_down(a, w_dn, xin, p, FFN_RES, name=f"ffn_down_{tag}")
        return out, dict(x=xin, hn=hn, gu=gu, a=a, f=f, p=p, i=i, h=h, tag=tag)

    def mla_fwd(xin, i, sub):
        p = sub_params(i, sub)
        tag = f"l{i}s{sub}"
        hn = _pre_fwd(xin, p, name=f"pre_fwd_{tag}")
        lat = _mm(hn, w_in, name="mla_lat")
        cq, ckv = lat[:, :Q_LORA], lat[:, Q_LORA:Q_LORA + KV_LORA]
        k1, k2 = lat[:, Q_LORA + KV_LORA:Q_LORA + KV_LORA + HALF_ROPE], lat[:, Q_LORA + KV_LORA + HALF_ROPE:]
        cqn = _rms_fwd(cq, mla_q_norm, name="mla_qnorm")
        ckvn = _rms_fwd(ckv, mla_kv_norm, name="mla_kvnorm")
        qp = _mm(cqn, wq_p, name="mla_q_up")
        kvp = _mm(ckvn, wkv_p, name="mla_kv_up")
        n0, n1 = HEADS * QK_NOPE, HEADS * HALF_ROPE
        qr1, qr2 = _rope(qp[:, n0:n0 + n1], qp[:, n0 + n1:], cos_q, sin_q, name="rope_q")
        kr1, kr2 = _rope(k1, k2, cos_k, sin_k, name="rope_k")
        q = jnp.concatenate([qp[:, :n0].reshape(SEQ, HEADS, QK_NOPE), qr1.reshape(SEQ, HEADS, HALF_ROPE),
                             qr2.reshape(SEQ, HEADS, HALF_ROPE)], axis=2).transpose(1, 0, 2).astype(BF16)
        kr = jnp.broadcast_to(jnp.concatenate([kr1, kr2], axis=1)[:, None, :], (SEQ, HEADS, QK_ROPE))
        k = jnp.concatenate([kvp[:, :n0].reshape(SEQ, HEADS, QK_NOPE), kr], axis=2).transpose(1, 0, 2).astype(BF16)
        v = _to_heads(kvp[:, n0:], V_HEAD).astype(BF16)
        o, lse = _mla_attn_fwd(q, k, v, name="mla_attn_fwd")
        o_flat = _from_heads(o).astype(BF16)
        f = _mm(o_flat, w_mo, name="mla_out")
        out = _post_fwd(f, xin, p, 1.0, name=f"post_fwd_{tag}")
        return out, dict(x=xin, hn=hn, cq=cq, ckv=ckv, cqn=cqn, ckvn=ckvn, q=q, k=k, v=v, o=o, lse=lse,
                         o_flat=o_flat, f=f, p=p, tag=tag)

    def dil_fwd(xin, i, sub):
        p = sub_params(i, sub)
        tag = f"l{i}s{sub}"
        hn = _pre_fwd(xin, p, name=f"pre_fwd_{tag}")
        heads = _proj_heads(hn, dil_w['in'], name="dil_proj")
        outs, lses = [], []
        for g, (window, d) in enumerate(DIL_GROUPS):
            o, lse = _dil_attn_fwd(heads, biases[g], g, d, name=f"dil_attn_fwd_g{g}")
            outs.append(o)
            lses.append(lse)
        o_flat = _dil_mix_fwd(outs, lses, name="dil_mix_fwd")
        f = _mm(o_flat, dil_w['out'], name="dil_out")
        out = _post_fwd(f, xin, p, 1.0, name=f"post_fwd_{tag}")
        return out, dict(x=xin, hn=hn, heads=heads, outs=outs, lses=lses, o_flat=o_flat, f=f, p=p, tag=tag)

    saved = [None] * 6
    xs, saved[0] = ffn_fwd(x2, 0, 0, 0, tie=flight_a['token'][0, 0] + flight_b['token'][0, 0])
    xs, saved[1] = mla_fwd(xs, 0, 1)
    ffn_w[0, 1] = as_ffn(*arrive(flight_a, xs))
    passed = {}

    def second_step(after):
        passed['st'], tok = pass_on(flight_b, after)
        return tok

    xs, saved[2] = ffn_fwd(xs, 0, 1, 2, mid=second_step)
    got, last_fams = lax.optimization_barrier((arrive_passed(passed['st'], xs), later_fams[2]))
    ffn_w[1, 0] = as_ffn(got[0], got[1])
    dil_w['in'], dil_w['out'] = got[2].reshape(N_SHARD, D_MODEL, -1), got[3].reshape(D_MODEL, D_MODEL)
    in_flight = gather_later(last_fams, "l1s2")
    xs, saved[3] = ffn_fwd(xs, 1, 0, 0, tie=in_flight['token'][0, 0])
    xs, saved[4] = dil_fwd(xs, 1, 1)
    ffn_w[1, 1] = as_ffn(*arrive(in_flight, xs))
    xs, saved[5] = ffn_fwd(xs, 1, 1, 2)

    dx, loss_part = _loss(xs, target, name="loss")

    dmod = [[None] * 9 for _ in range(2)]
    dpre = [[None] * 3 for _ in range(2)]
    dpost = [[None] * 3 for _ in range(2)]
    ffn_units = {}
    row_unit = lambda g, r, j: ((r % 2, r // 2), 0, j)

    def close_sub(dhn, dout, sv, i, sub, res_dgate, res_dqg):
        p = sv['p']
        dxs, dsh, dsc, dpg = _pre_bwd(dhn, sv['x'], dout, p, name=f"pre_bwd_{sv['tag']}")
        dmod[i][3 * sub], dmod[i][3 * sub + 1], dmod[i][3 * sub + 2] = dsh, dsc, res_dgate
        dpre[i][sub], dpost[i][sub] = dpg, res_dqg
        return dxs

    def ffn_bwd(dout, sv, sub, tie=0.0, mid=None):
        i, h, p, tag = sv['i'], sv['h'], sv['p'], sv['tag']
        w_gu, w_dn = ffn_w[i, h]
        df, dgate, dqg = _post_bwd(dout, sv['f'], p + tie, FFN_RES, name=f"post_bwd_{tag}")
        u_dn = _mm(sv['a'], df, ta=True, tn_cap=D_MODEL // 2, out_shape=(2, N_SHARD, F_SHARD, D_MODEL // 2),
                   out_sel=lambda g, r, j: ((j, g), r, 0), name=f"ffn_dwd_{tag}")
        dgu = _ffn_dgu(df, w_dn, sv['gu'], name=f"ffn_dgu_{tag}")
        if mid is not None:
            p = p + mid(dgu)
        u_gu = _mm(dgu.reshape(2 * N_SHARD, SEQ, F_SHARD), sv['hn'], ta=True,
                   out_shape=(2, N_SHARD, F_SHARD, D_MODEL), out_sel=lambda g, r, j: ((g % 2, g // 2), r, j),
                   name=f"ffn_dwgu_{tag}")
        ffn_units[i, h] = [u_gu, u_dn]
        dxs, dsh, dsc, dpg = _ffn_dhn(dgu, w_gu, sv['x'], dout, p, name=f"ffn_dhn_{tag}")
        dmod[i][3 * sub], dmod[i][3 * sub + 1], dmod[i][3 * sub + 2] = dsh, dsc, dgate
        dpre[i][sub], dpost[i][sub] = dpg, dqg
        return dxs

    def mla_bwd(dout, sv, i, sub, tie=0.0):
        p, tag = sv['p'], sv['tag']
        df, dgate, dqg = _post_bwd(dout, sv['f'], p + tie, 1.0, name=f"post_bwd_{tag}")
        u_wo = _mm(sv['o_flat'], df, ta=True, tm_cap=128, out_shape=(2, N_SHARD, 128, D_MODEL), out_sel=row_unit,
                   name="mla_dwo")
        do_flat = _mm(df, w_mo, tb=True, name="mla_do")
        do = _to_heads(do_flat, V_HEAD)
        dq, dk, dv = _mla_attn_bwd(sv['q'], sv['k'], sv['v'], sv['o'], do, sv['lse'], name="mla_attn_bwd")
        dq_t = dq.transpose(1, 0, 2)
        dqr1, dqr2 = _rope(dq_t[:, :, QK_NOPE:QK_NOPE + HALF_ROPE].reshape(SEQ, -1),
                           dq_t[:, :, QK_NOPE + HALF_ROPE:].reshape(SEQ, -1), cos_q, -sin_q, name="rope_q_bwd")
        dqp = jnp.concatenate([dq_t[:, :, :QK_NOPE].reshape(SEQ, -1), dqr1, dqr2], axis=1).astype(BF16)
        dkr = _head_sum(dk[:, :, QK_NOPE:], name="mla_dkr_sum")
        dk1, dk2 = _rope(dkr[:, :HALF_ROPE], dkr[:, HALF_ROPE:], cos_k, -sin_k, name="rope_k_bwd")
        dkvp = jnp.concatenate([_from_heads(dk[:, :, :QK_NOPE]), _from_heads(dv)], axis=1).astype(BF16)
        g_wq = _q_unperm(_mm(sv['cqn'], dqp, ta=True, name="mla_dwq"))
        g_wkv = _kv_unperm(_mm(sv['ckvn'], dkvp, ta=True, name="mla_dwkv"))
        dcqn = _mm(dqp, wq_p, tb=True, name="mla_dcqn")
        dckvn = _mm(dkvp, wkv_p, tb=True, name="mla_dckvn")
        dcq, g_qn = _rms_bwd(dcqn, sv['cq'], mla_q_norm, name="mla_qnorm_bwd")
        dckv, g_kvn = _rms_bwd(dckvn, sv['ckv'], mla_kv_norm, name="mla_kvnorm_bwd")
        dlat = jnp.concatenate([dcq, dckv, dk1, dk2], axis=1).astype(BF16)
        u_win = _mm(sv['hn'], dlat, ta=True, tm_cap=128, out_shape=(2, N_SHARD, 128, dlat.shape[1]),
                    out_sel=row_unit, name="mla_dwin")
        dhn = _mm(dlat, w_in, tb=True, name="mla_dhn")
        col_unit = lambda t: (t.reshape(t.shape[0], N_SHARD, -1).transpose(1, 0, 2)
                              .reshape(N_SHARD, 2, t.shape[0] // 2, -1).transpose(1, 0, 2, 3))
        grads = dict(units=[u_win, col_unit(g_wq), col_unit(g_wkv), u_wo], q_norm=g_qn, kv_norm=g_kvn)
        return close_sub(dhn, dout, sv, i, sub, dgate, dqg), grads

    def dil_bwd(dout, sv, i, sub):
        p, tag = sv['p'], sv['tag']
        df, dgate, dqg = _post_bwd(dout, sv['f'], p, 1.0, name=f"post_bwd_{tag}")
        u_wo = _mm(sv['o_flat'], df, ta=True, tm_cap=128, out_shape=(2, N_SHARD, 128, D_MODEL), out_sel=row_unit,
                   name="dil_dwo")
        dos, dlts = _dil_mix_bwd(_mm(df, dil_w['out'], tb=True, name="dil_do"), sv['outs'], sv['lses'],
                                 name="dil_mix_bwd")
        pieces = []
        bias_rows = []
        for g, (window, d) in enumerate(DIL_GROUPS):
            dq, dk, dv, dbias = _dil_attn_bwd(sv['heads'], biases[g], sv['lses'][g], dos[g], dlts[g], g, d,
                                              name=f"dil_attn_bwd_g{g}")
            pieces += [dq, dk, dv]
            bias_rows.append(_bias_grad(dbias, buckets[g], name=f"dil_bias_grad_g{g}")[:, 0, :])
        dheads = jnp.concatenate(pieces).astype(BF16)
        u_win = _proj_heads_dw(sv['hn'], dheads, name="dil_dwin")
        dhn = _proj_heads_dx(dheads, dil_w['in'], name="dil_dhn")
        g_bias = jnp.concatenate(bias_rows, axis=0).T
        grads = dict(units=[u_win, u_wo], rel_bias=g_bias)
        return close_sub(dhn, dout, sv, i, sub, dgate, dqg), grads

    def to_sibling(units, tag):
        n = len(units)
        send, recv, thru, token = _copies_start(units, [lax.empty(u.shape[1:], F32) for u in units], _sibling_plan, n,
                                                name=f"rs{tag}_sibling_start")
        return dict(send=send, recv=recv, thru=thru, n=n, tag=tag), token[0, 0]

    def from_sibling(st, after):
        n, tag = st['n'], st['tag']
        thru = _copies_wait(st['send'], st['recv'], st['thru'], n, _sibling_plan, after, name=f"rs{tag}_sibling_wait")
        return [_add_half(u, g, half_idx, name=f"rs{tag}_add_half_{k}") for k, (u, g) in enumerate(zip(thru[:n], thru[n:]))]

    def to_chips(parts, tag):
        n = len(parts)
        send, recv, thru, token = _copies_start([w for _, w in parts],
                                                [lax.empty((3,) + w.shape[1:], BF16) for _, w in parts], _chips_plan,
                                                3 * n, name=f"rs{tag}_chips_start")
        return dict(send=send, recv=recv, thru=thru, n=n, tag=tag, parts=parts), token[0, 0]

    def from_chips(st, after):
        n, tag = st['n'], st['tag']
        thru = _copies_wait(st['send'], st['recv'], st['thru'], n, _chips_plan, after, name=f"rs{tag}_chips_wait")
        return [_add_shards(p, g, shard_idx, name=f"rs{tag}_add_shards_{k}")
                for k, ((p, _), g) in enumerate(zip(st['parts'], thru[n:]))]

    dx = ffn_bwd(dx, saved[5], 2)
    dx, dil_g = dil_bwd(dx, saved[4], 1, 1)
    dx = ffn_bwd(dx, saved[3], 0)
    st1, tok = to_sibling([*ffn_units[1, 1], *dil_g['units'], *ffn_units[1, 0]], "1")
    dx = ffn_bwd(dx, saved[2], 2, tie=tok)
    st1, tok1 = to_chips(from_sibling(st1, dx), "1")
    st2, tok2 = to_sibling(ffn_units[0, 1], "2")
    dx, mla_g = mla_bwd(dx, saved[1], 0, 1, tie=tok1 + tok2)
    reds1 = from_chips(st1, dx)
    st2, tok = to_chips(from_sibling(st2, dx), "2")
    st3, tok3 = to_sibling(mla_g['units'], "3")
    onward = {}

    def mixer_to_chips(after):
        onward['st'], t = to_chips(from_sibling(st3, after), "3")
        return t

    dx = ffn_bwd(dx, saved[0], 0, tie=tok + tok3, mid=mixer_to_chips)
    reds2 = from_chips(st2, dx)
    reds3 = from_chips(onward['st'], dx)
    grad_x = dx[None]

    pad_row = lambda v: jnp.pad(v.reshape(-1), (0, (-v.size) % D_MODEL)).reshape(-1, D_MODEL)
    small = jnp.concatenate(
        [jnp.concatenate([dmod[i][r] for i in range(2) for r in range(9)], axis=0),
         jnp.concatenate([dpre[i][s] for i in range(2) for s in range(3)], axis=0),
         jnp.concatenate([dpost[i][s] for i in range(2) for s in range(3)], axis=0),
         pad_row(mla_g['q_norm']), pad_row(mla_g['kv_norm']), pad_row(dil_g['rel_bias']), pad_row(loss_part)], axis=0)
    small = jnp.pad(small, ((0, SMALL_ROWS - small.shape[0]), (0, 0)))
    small_all = _all_gather(small, name="ag_small_grads", in_vmem=True)
    small_sum = _sum_devices(small_all, 8, name="sum_small_grads")
    g_b_mod = small_sum[0:18].reshape(2, 9 * D_MODEL)
    my_cols = lambda t: lax.dynamic_slice_in_dim(t, shard_id * 256, 256, axis=2)
    g_norm_pre = my_cols(small_sum[18:24].reshape(2, 3, D_MODEL))
    g_norm_post = my_cols(small_sum[24:30].reshape(2, 3, D_MODEL))
    g_q_norm = small_sum[30, :Q_LORA].reshape(1, Q_LORA)
    g_kv_norm = small_sum[31, :KV_LORA].reshape(1, KV_LORA)
    g_rel_bias = small_sum[32:34].reshape(-1)[:N_BUCKETS * 48].reshape(N_BUCKETS, 48)
    loss = small_sum[34, 0]
    dmod_all = small_all.reshape(8, SMALL_ROWS, D_MODEL)[:, 0:18].reshape(8, 2, 9 * D_MODEL)
    dmod_cols = lax.dynamic_slice_in_dim(dmod_all, shard_id * 2304, 2304, axis=2).transpose(1, 0, 2)

    swap = lambda t: jnp.swapaxes(t, 2, 3)
    grads = dict(norm_pre=g_norm_pre, norm_post=g_norm_post, b_mod=g_b_mod, mla_q_norm=g_q_norm,
                 mla_kv_norm=g_kv_norm, rel_bias=g_rel_bias)
    deltas, new_m, new_v = {}, {}, {}

    def adamw(names):
        for n in names:
            view = swap if n in ('ffn_w_gate', 'ffn_w_up') else (lambda t: t)
            outs = _adamw(view(given[n]), view(grads[n]), view(given["m_" + n]), view(given["v_" + n]),
                          name=f"adamw_{n}")
            deltas[n], new_m[n], new_v[n] = (view(t) for t in outs)

    st0, tok = to_sibling(ffn_units[0, 0], "0")
    grads['w_mod'] = _mm(silu_c, (dmod_cols + tok).astype(BF16), ta=True, tn_cap=768, name="w_mod_grad")
    adamw(['w_mod'])
    st0, tok = to_chips(from_sibling(st0, deltas['w_mod']), "0")
    grads['b_mod'] = grads['b_mod'] + tok
    fin = _pair_gather(reds1 + reds2 + reds3, name="rs_pair_gather")
    for n, t in zip(['dil_w_in', 'dil_w_o', 'mla_w_in', 'mla_w_q_up', 'mla_w_kv_up', 'mla_w_o'], fin[2:4] + fin[8:12]):
        grads[n] = t.reshape(given[n].shape)
    adamw(['b_mod', 'dil_w_in', 'dil_w_o', 'mla_w_in', 'mla_w_q_up', 'mla_w_kv_up', 'mla_w_o', 'norm_pre', 'norm_post',
           'mla_q_norm', 'mla_kv_norm', 'rel_bias'])
    reds0 = from_chips(st0, deltas['dil_w_in'])
    fin0 = _pair_gather(reds0, name="rs_pair_gather_last")
    ffn_fin = {(1, 1): fin[0:2], (1, 0): fin[4:6], (0, 1): fin[6:8], (0, 0): fin0}
    per_ffn = lambda pick: jnp.stack([jnp.stack([pick(*ffn_fin[i, h]) for h in range(2)]) for i in range(2)])
    grads.update(ffn_w_gate=swap(per_ffn(lambda gu, dn: gu[0])), ffn_w_up=swap(per_ffn(lambda gu, dn: gu[1])),
                 ffn_w_down=per_ffn(lambda gu, dn: jnp.concatenate([dn[0], dn[1]], axis=1)))
    adamw(['ffn_w_gate', 'ffn_w_up', 'ffn_w_down'])
    return (loss, grad_x, *[grads[n] for n in WEIGHTS], *[deltas[n] for n in WEIGHTS],
            *[new_m[n] for n in WEIGHTS], *[new_v[n] for n in WEIGHTS])
```

```python
import math

import jax
import jax.numpy as jnp
from jax import lax
from jax.experimental import pallas as pl
from jax.experimental.pallas import tpu as pltpu

F32 = jnp.float32
BF16 = jnp.bfloat16
MESH = pl.DeviceIdType.MESH

SEQ = 2048
D_MODEL = 1024
D_FF = 2816
N_SHARD = 4
F_SHARD = D_FF // N_SHARD
EPS = 1e-6
FFN_RES = 0.5
HEADS = 16
Q_LORA, KV_LORA, QK_NOPE, QK_ROPE, V_HEAD = 384, 256, 64, 32, 64
HALF_ROPE = QK_ROPE // 2
ROPE_THETA = 10000.0
DIL_GROUPS = ((128, 1), (512, 4), (2048, 16))
DIL_BLOCK = 128
N_BUCKETS = 32
MAX_DISTANCE = 2048
ADAM_LR, ADAM_B1, ADAM_B2, ADAM_EPS, ADAM_WD, ADAM_STEP = 0.001, 0.9, 0.999, 1e-08, 0.01, 10

VMEM_LIMIT = 48 * 1024 * 1024
SMALL_ROWS = 40

WEIGHTS = ['norm_pre', 'norm_post', 'w_mod', 'b_mod', 'ffn_w_gate', 'ffn_w_up', 'ffn_w_down', 'mla_w_in',
           'mla_q_norm', 'mla_w_q_up', 'mla_kv_norm', 'mla_w_kv_up', 'mla_w_o', 'dil_w_in', 'dil_w_o', 'rel_bias']


def _cparams(**kw):
    return pltpu.CompilerParams(vmem_limit_bytes=VMEM_LIMIT, **kw)


def _pick(n, cap, mult=128):
    if n <= cap:
        return n
    best = n
    for t in range(mult, cap + 1, mult):
        if n % t == 0:
            best = t
    return best


def _mm(a, b, *, name, ta=False, tb=False, reduce_g=False, bias=None, out_dtype=F32, tm_cap=1024, tn_cap=1024,
        g_n=None, b_sel=None, out_shape=None, out_sel=None, out_buf=None):
    a3 = a if a.ndim == 3 else a[None]
    ga = a3.shape[0]
    if b_sel is None:
        b_n = b if b.ndim == 3 else b[None]
        gb = b_n.shape[0]
        b_sel = (lambda g: (g,)) if gb > 1 else (lambda g: (0,))
        g_n = max(ga, gb)
    else:
        b_n = b
    k_dim, m_dim = (a3.shape[1], a3.shape[2]) if ta else (a3.shape[2], a3.shape[1])
    k2, n_dim = (b_n.shape[-1], b_n.shape[-2]) if tb else (b_n.shape[-2], b_n.shape[-1])
    assert k_dim == k2, (a.shape, b.shape)
    tm = _pick(m_dim, tm_cap, 128 if ta else 8)
    tn = _pick(n_dim, tn_cap, 128)
    mt, nt = m_dim // tm, n_dim // tn
    dims = (((0 if ta else 1,), (1 if tb else 0,)), ((), ()))

    if reduce_g:
        grid = (mt, nt, g_n)
        ids = lambda i, j, g: (g, i, j)
    else:
        grid = (g_n, mt, nt)
        ids = lambda g, i, j: (g, i, j)

    def a_map(*p):
        g, i, j = ids(*p)
        g = g if ga > 1 else 0
        return (g, 0, i) if ta else (g, i, 0)

    def b_map(*p):
        g, i, j = ids(*p)
        return (*b_sel(g), j, 0) if tb else (*b_sel(g), 0, j)

    b_lead = (None,) * (b_n.ndim - 2)
    a_spec = pl.BlockSpec((None, k_dim, tm) if ta else (None, tm, k_dim), a_map)
    b_spec = pl.BlockSpec(b_lead + ((tn, k_dim) if tb else (k_dim, tn)), b_map)
    in_specs = [a_spec, b_spec]
    operands = [a3, b_n]
    if bias is not None:
        assert not reduce_g and bias.shape == (g_n, 1, n_dim)
        in_specs.append(pl.BlockSpec((None, 1, tn), lambda g, i, j: (g, 0, j)))
        operands.append(bias)
    aliases = {}
    if out_buf is not None:
        assert tuple(out_buf.shape) == tuple(out_shape) and out_buf.dtype == out_dtype
        in_specs.append(pl.BlockSpec(memory_space=pl.ANY))
        operands.append(out_buf)
        aliases = {len(operands) - 1: 0}

    if reduce_g:
        out_spec = pl.BlockSpec((tm, tn), lambda i, j, g: (i, j))
        out_sds = jax.ShapeDtypeStruct((m_dim, n_dim), F32)
    elif out_shape is not None:
        def o_map(g, i, j):
            lead, rb, cb = out_sel(g, i, j)
            return (*lead, rb, cb)

        out_spec = pl.BlockSpec((None,) * (len(out_shape) - 2) + (tm, tn), o_map)
        out_sds = jax.ShapeDtypeStruct(tuple(out_shape), out_dtype)
    else:
        out_spec = pl.BlockSpec((None, tm, tn), lambda g, i, j: (g, i, j))
        out_sds = jax.ShapeDtypeStruct((g_n, m_dim, n_dim), out_dtype)

    def body(a_ref, b_ref, *rest):
        o_ref = rest[-1]
        r = lax.dot_general(a_ref[...].astype(BF16), b_ref[...].astype(BF16), dims, preferred_element_type=F32)
        if bias is not None:
            r = r + rest[0][...]
        if reduce_g:
            g = pl.program_id(2)

            @pl.when(g == 0)
            def _():
                o_ref[...] = r

            @pl.when(g > 0)
            def _():
                o_ref[...] += r
        else:
            o_ref[...] = r.astype(o_ref.dtype)

    out = pl.pallas_call(body, grid=grid, in_specs=in_specs, out_specs=out_spec, out_shape=out_sds,
                         input_output_aliases=aliases, compiler_params=_cparams(), name=name)(*operands)
    if not reduce_g and out_shape is None and a.ndim == 2 and b.ndim == 2:
        out = out[0]
    return out


def _rows(tm, w):
    return pl.BlockSpec((tm, w), lambda i: (i, 0))


def _vec(w):
    return pl.BlockSpec((1, w), lambda i: (0, 0))


def _rstd(v):
    return lax.rsqrt(jnp.mean(v * v, axis=-1, keepdims=True) + EPS)


V_PG, V_QG, V_SH, V_SC, V_GATE = range(5)


def _vrow(v_ref, k):
    return v_ref[k:k + 1, :]


def _vecs(w):
    return pl.BlockSpec((8, w), lambda *_: (0, 0))


def _pre_fwd(x, vp, *, name):
    s_n, w = x.shape
    tm = _pick(s_n, 512, 8)

    def body(x_ref, v_ref, o_ref):
        xv = x_ref[...]
        n = (xv * _rstd(xv)) * _vrow(v_ref, V_PG)
        o_ref[...] = (n * (1.0 + _vrow(v_ref, V_SC)) + _vrow(v_ref, V_SH)).astype(o_ref.dtype)

    return pl.pallas_call(body, grid=(s_n // tm,), in_specs=[_rows(tm, w), _vecs(w)],
                          out_specs=_rows(tm, w), out_shape=jax.ShapeDtypeStruct((s_n, w), BF16),
                          compiler_params=_cparams(), name=name)(x, vp)


def _post_fwd(f, x, vp, res_w, *, name):
    s_n, w = x.shape
    tm = _pick(s_n, 512, 8)

    def body(f_ref, x_ref, v_ref, o_ref):
        fv = f_ref[...]
        y = (fv * _rstd(fv)) * _vrow(v_ref, V_QG)
        o_ref[...] = x_ref[...] + (res_w * _vrow(v_ref, V_GATE)) * y

    return pl.pallas_call(body, grid=(s_n // tm,), in_specs=[_rows(tm, w), _rows(tm, w), _vecs(w)],
                          out_specs=_rows(tm, w), out_shape=jax.ShapeDtypeStruct((s_n, w), F32),
                          compiler_params=_cparams(), name=name)(f, x, vp)


def _post_bwd(dout, f, vp, res_w, *, name):
    s_n, w = f.shape
    tm = _pick(s_n, 512, 8)

    def body(do_ref, f_ref, v_ref, df_ref, dgate_ref, dqg_ref):
        @pl.when(pl.program_id(0) == 0)
        def _():
            dgate_ref[...] = jnp.zeros_like(dgate_ref)
            dqg_ref[...] = jnp.zeros_like(dqg_ref)

        do = do_ref[...]
        fv = f_ref[...]
        r = _rstd(fv)
        fh = fv * r
        qg_v = _vrow(v_ref, V_QG)
        dgate_ref[...] += res_w * jnp.sum(do * (fh * qg_v), axis=0, keepdims=True)
        dy = do * (res_w * _vrow(v_ref, V_GATE))
        dqg_ref[...] += jnp.sum(dy * fh, axis=0, keepdims=True)
        dfh = dy * qg_v
        df = r * (dfh - fh * jnp.mean(dfh * fh, axis=-1, keepdims=True))
        df_ref[...] = df.astype(df_ref.dtype)

    return pl.pallas_call(
        body, grid=(s_n // tm,), in_specs=[_rows(tm, w), _rows(tm, w), _vecs(w)],
        out_specs=[_rows(tm, w), _vec(w), _vec(w)],
        out_shape=[jax.ShapeDtypeStruct((s_n, w), BF16), jax.ShapeDtypeStruct((1, w), F32),
                   jax.ShapeDtypeStruct((1, w), F32)],
        compiler_params=_cparams(), name=name)(dout, f, vp)


def _pre_bwd(dhn, x, dout, vp, *, name):
    s_n, w = x.shape
    tm = _pick(s_n, 512, 8)

    def body(dhn_ref, x_ref, do_ref, v_ref, dx_ref, dsh_ref, dsc_ref, dpg_ref):
        @pl.when(pl.program_id(0) == 0)
        def _():
            dsh_ref[...] = jnp.zeros_like(dsh_ref)
            dsc_ref[...] = jnp.zeros_like(dsc_ref)
            dpg_ref[...] = jnp.zeros_like(dpg_ref)

        dhn_v = dhn_ref[...]
        xv = x_ref[...]
        r = _rstd(xv)
        xh = xv * r
        pg_v = _vrow(v_ref, V_PG)
        dsh_ref[...] += jnp.sum(dhn_v, axis=0, keepdims=True)
        dsc_ref[...] += jnp.sum(dhn_v * (xh * pg_v), axis=0, keepdims=True)
        dn = dhn_v * (1.0 + _vrow(v_ref, V_SC))
        dpg_ref[...] += jnp.sum(dn * xh, axis=0, keepdims=True)
        dxh = dn * pg_v
        dx_ref[...] = do_ref[...] + r * (dxh - xh * jnp.mean(dxh * xh, axis=-1, keepdims=True))

    vec = jax.ShapeDtypeStruct((1, w), F32)
    return pl.pallas_call(
        body, grid=(s_n // tm,), in_specs=[_rows(tm, w), _rows(tm, w), _rows(tm, w), _vecs(w)],
        out_specs=[_rows(tm, w), _vec(w), _vec(w), _vec(w)],
        out_shape=[jax.ShapeDtypeStruct((s_n, w), F32), vec, vec, vec],
        compiler_params=_cparams(), name=name)(dhn, x, dout, vp)


def _rms_fwd(x, g, *, name):
    s_n, w = x.shape
    tm = _pick(s_n, 512, 8)

    def body(x_ref, g_ref, o_ref):
        xv = x_ref[...]
        o_ref[...] = ((xv * _rstd(xv)) * g_ref[...]).astype(o_ref.dtype)

    return pl.pallas_call(body, grid=(s_n // tm,), in_specs=[_rows(tm, w), _vec(w)], out_specs=_rows(tm, w),
                          out_shape=jax.ShapeDtypeStruct((s_n, w), BF16), compiler_params=_cparams(),
                          name=name)(x, g)


def _rms_bwd(dy, x, g, *, name):
    s_n, w = x.shape
    tm = _pick(s_n, 512, 8)

    def body(dy_ref, x_ref, g_ref, dx_ref, dg_ref):
        @pl.when(pl.program_id(0) == 0)
        def _():
            dg_ref[...] = jnp.zeros_like(dg_ref)

        dy_v = dy_ref[...]
        xv = x_ref[...]
        r = _rstd(xv)
        xh = xv * r
        dg_ref[...] += jnp.sum(dy_v * xh, axis=0, keepdims=True)
        dxh = dy_v * g_ref[...]
        dx_ref[...] = r * (dxh - xh * jnp.mean(dxh * xh, axis=-1, keepdims=True))

    return pl.pallas_call(
        body, grid=(s_n // tm,), in_specs=[_rows(tm, w), _rows(tm, w), _vec(w)],
        out_specs=[_rows(tm, w), _vec(w)],
        out_shape=[jax.ShapeDtypeStruct((s_n, w), F32), jax.ShapeDtypeStruct((1, w), F32)],
        compiler_params=_cparams(), name=name)(dy, x, g)


def _rope(a1, a2, cos, sin, *, name):
    s_n, w = a1.shape
    tm = _pick(s_n, 512, 8)

    def body(a1_ref, a2_ref, c_ref, s_ref, r1_ref, r2_ref):
        u, v, c_v, s_v = a1_ref[...], a2_ref[...], c_ref[...], s_ref[...]
        r1_ref[...] = u * c_v - v * s_v
        r2_ref[...] = u * s_v + v * c_v

    sd = jax.ShapeDtypeStruct((s_n, w), F32)
    return pl.pallas_call(body, grid=(s_n // tm,), in_specs=[_rows(tm, w)] * 4, out_specs=[_rows(tm, w)] * 2,
                          out_shape=[sd, sd], compiler_params=_cparams(), name=name)(a1, a2, cos, sin)


def _silu_bf16(x, *, name):
    def body(x_ref, o_ref):
        xv = x_ref[...]
        o_ref[...] = (xv * jax.nn.sigmoid(xv)).astype(o_ref.dtype)

    return pl.pallas_call(body, out_shape=jax.ShapeDtypeStruct(x.shape, BF16), name=name)(x)


def _loss(y, target, *, name):
    s_n, w = y.shape
    tm = _pick(s_n, 512, 8)

    def body(y_ref, t_ref, dy_ref, l_ref):
        @pl.when(pl.program_id(0) == 0)
        def _():
            l_ref[...] = jnp.zeros_like(l_ref)

        e = y_ref[...] - t_ref[...]
        dy_ref[...] = e * (1.0 / w)
        row = jnp.mean(e * e, axis=-1, keepdims=True)
        l_ref[...] += 0.5 * jnp.sum(row, axis=0, keepdims=True)

    return pl.pallas_call(
        body, grid=(s_n // tm,), in_specs=[_rows(tm, w), _rows(tm, w)],
        out_specs=[_rows(tm, w), pl.BlockSpec((1, 1), lambda i: (0, 0))],
        out_shape=[jax.ShapeDtypeStruct((s_n, w), F32), jax.ShapeDtypeStruct((1, 1), F32)],
        compiler_params=_cparams(), name=name)(y, target)


FFN_TM = 512
FFN_TM_WIDE = 1024


def _ffn_up(hn, w_gu, *, name):
    s_n, d = hn.shape
    f = w_gu.shape[-1]
    tm = _pick(s_n, FFN_TM_WIDE, 8)

    def body(hn_ref, wg_ref, wu_ref, gu_ref, a_ref):
        xv = hn_ref[...]
        g = jnp.dot(xv, wg_ref[...], preferred_element_type=F32)
        u = jnp.dot(xv, wu_ref[...], preferred_element_type=F32)
        gu_ref[0] = g.astype(BF16)
        gu_ref[1] = u.astype(BF16)
        a_ref[...] = ((g * jax.nn.sigmoid(g)) * u).astype(BF16)

    w_blk = lambda t: pl.BlockSpec((None, None, d, f), lambda s, m: (s, t, 0, 0))
    return pl.pallas_call(
        body, grid=(N_SHARD, s_n // tm),
        in_specs=[pl.BlockSpec((tm, d), lambda s, m: (m, 0)), w_blk(0), w_blk(1)],
        out_specs=[pl.BlockSpec((None, 2, tm, f), lambda s, m: (s, 0, m, 0)),
                   pl.BlockSpec((None, tm, f), lambda s, m: (s, m, 0))],
        out_shape=[jax.ShapeDtypeStruct((N_SHARD, 2, s_n, f), BF16), jax.ShapeDtypeStruct((N_SHARD, s_n, f), BF16)],
        compiler_params=_cparams(), name=name)(hn, w_gu, w_gu)


def _ffn_down(a, w_dn, x, vp, res_w, *, name):
    _, s_n, f = a.shape
    d = w_dn.shape[-1]
    tm = _pick(s_n, FFN_TM, 8)
    a = a.reshape(-1, 2, s_n, f)
    w_dn = w_dn.reshape(-1, 2, f, d)
    g_n = a.shape[0]

    def body(a_ref, w_ref, x_ref, v_ref, f_ref, o_ref):
        g = pl.program_id(1)
        r = (jnp.dot(a_ref[0], w_ref[0], preferred_element_type=F32)
             + jnp.dot(a_ref[1], w_ref[1], preferred_element_type=F32))

        @pl.when(g == 0)
        def _():
            f_ref[...] = r

        @pl.when(g > 0)
        def _():
            f_ref[...] += r

        @pl.when(g == g_n - 1)
        def _():
            fv = f_ref[...]
            y = (fv * _rstd(fv)) * _vrow(v_ref, V_QG)
            o_ref[...] = x_ref[...] + (res_w * _vrow(v_ref, V_GATE)) * y

    row = pl.BlockSpec((tm, d), lambda m, g: (m, 0))
    sd = jax.ShapeDtypeStruct((s_n, d), F32)
    return pl.pallas_call(
        body, grid=(s_n // tm, g_n),
        in_specs=[pl.BlockSpec((None, 2, tm, f), lambda m, g: (g, 0, m, 0)),
                  pl.BlockSpec((None, 2, f, d), lambda m, g: (g, 0, 0, 0)), row, _vecs(d)],
        out_specs=[row, row], out_shape=[sd, sd], compiler_params=_cparams(), name=name)(a, w_dn, x, vp)


def _ffn_dhn(dgu, w_gu, x, dout, vp, *, name):
    g_n, _, s_n, f = dgu.shape
    d = w_gu.shape[-2]
    tm = _pick(s_n, FFN_TM, 8)
    nt_dims = (((1,), (1,)), ((), ()))

    def body(a_ref, w_ref, x_ref, do_ref, v_ref, dx_ref, dsh_ref, dsc_ref, dpg_ref, acc_ref):
        m, g = pl.program_id(0), pl.program_id(1)
        r = (lax.dot_general(a_ref[0], w_ref[0], nt_dims, preferred_element_type=F32)
             + lax.dot_general(a_ref[1], w_ref[1], nt_dims, preferred_element_type=F32))

        @pl.when(g == 0)
        def _():
            acc_ref[...] = r

        @pl.when(g > 0)
        def _():
            acc_ref[...] += r

        @pl.when((m == 0) & (g == 0))
        def _():
            dsh_ref[...] = jnp.zeros_like(dsh_ref)
            dsc_ref[...] = jnp.zeros_like(dsc_ref)
            dpg_ref[...] = jnp.zeros_like(dpg_ref)

        @pl.when(g == g_n - 1)
        def _():
            dhn_v = acc_ref[...]
            xv = x_ref[...]
            rs = _rstd(xv)
            xh = xv * rs
            pg_v = _vrow(v_ref, V_PG)
            dsh_ref[...] += jnp.sum(dhn_v, axis=0, keepdims=True)
            dsc_ref[...] += jnp.sum(dhn_v * (xh * pg_v), axis=0, keepdims=True)
            dn = dhn_v * (1.0 + _vrow(v_ref, V_SC))
            dpg_ref[...] += jnp.sum(dn * xh, axis=0, keepdims=True)
            dxh = dn * pg_v
            dx_ref[...] = do_ref[...] + rs * (dxh - xh * jnp.mean(dxh * xh, axis=-1, keepdims=True))

    row = pl.BlockSpec((tm, d), lambda m, g: (m, 0))
    vec = pl.BlockSpec((1, d), lambda m, g: (0, 0))
    vsd = jax.ShapeDtypeStruct((1, d), F32)
    return pl.pallas_call(
        body, grid=(s_n // tm, g_n),
        in_specs=[pl.BlockSpec((None, 2, tm, f), lambda m, g: (g, 0, m, 0)),
                  pl.BlockSpec((None, 2, d, f), lambda m, g: (g, 0, 0, 0)), row, row, _vecs(d)],
        out_specs=[row, vec, vec, vec], out_shape=[jax.ShapeDtypeStruct((s_n, d), F32), vsd, vsd, vsd],
        scratch_shapes=[pltpu.VMEM((tm, d), F32)], compiler_params=_cparams(), name=name)(dgu, w_gu, x, dout, vp)


def _ffn_dgu(df, w_dn, gu, *, name):
    s_n, d = df.shape
    f = w_dn.shape[-2]
    tm = _pick(s_n, FFN_TM_WIDE, 8)

    def body(df_ref, wd_ref, gu_ref, o_ref):
        da = lax.dot_general(df_ref[...], wd_ref[...], (((1,), (1,)), ((), ())), preferred_element_type=F32)
        g = gu_ref[0].astype(F32)
        u = gu_ref[1].astype(F32)
        sig = jax.nn.sigmoid(g)
        o_ref[0] = (da * u * (sig * (1.0 + g * (1.0 - sig)))).astype(BF16)
        o_ref[1] = (da * (g * sig)).astype(BF16)

    gu_blk = pl.BlockSpec((None, 2, tm, f), lambda s, m: (s, 0, m, 0))
    return pl.pallas_call(
        body, grid=(N_SHARD, s_n // tm),
        in_specs=[pl.BlockSpec((tm, d), lambda s, m: (m, 0)),
                  pl.BlockSpec((None, f, d), lambda s, m: (s, 0, 0)), gu_blk],
        out_specs=gu_blk, out_shape=jax.ShapeDtypeStruct((N_SHARD, 2, s_n, f), BF16),
        compiler_params=_cparams(), name=name)(df, w_dn, gu)


_NT = (((1,), (1,)), ((), ()))
_TN = (((0,), (0,)), ((), ()))
MLA_TQ = 512


def _causal_mask(i, tq, s_n):
    qpos = i * tq + lax.broadcasted_iota(jnp.int32, (tq, s_n), 0)
    kpos = lax.broadcasted_iota(jnp.int32, (tq, s_n), 1)
    return kpos <= qpos


def _mla_attn_fwd(q, k, v, *, name):
    h_n, s_n, dq = q.shape
    dv = v.shape[-1]
    tq = MLA_TQ
    scale = float(dq) ** -0.5

    def body(q_ref, k_ref, v_ref, o_ref, lse_ref):
        i = pl.program_id(1)
        for e in range(1, s_n // tq + 1):
            @pl.when(i == e - 1)
            def _(ext=e * tq):
                mask = _causal_mask(i, tq, ext)
                s = lax.dot_general(q_ref[...], k_ref[0:ext, :], _NT, preferred_element_type=F32) * scale
                s = jnp.where(mask, s, -jnp.inf)
                m = jnp.max(s, axis=-1, keepdims=True)
                p = jnp.exp(s - m)
                l = jnp.sum(p, axis=-1, keepdims=True)
                o = jnp.dot(p.astype(BF16), v_ref[0:ext, :], preferred_element_type=F32)
                o_ref[...] = o / l
                lse_ref[...] = m + jnp.log(l)

    return pl.pallas_call(
        body, grid=(h_n, s_n // tq),
        in_specs=[pl.BlockSpec((None, tq, dq), lambda h, i: (h, i, 0)),
                  pl.BlockSpec((None, s_n, dq), lambda h, i: (h, 0, 0)),
                  pl.BlockSpec((None, s_n, dv), lambda h, i: (h, 0, 0))],
        out_specs=[pl.BlockSpec((None, tq, dv), lambda h, i: (h, i, 0)),
                   pl.BlockSpec((None, tq, 1), lambda h, i: (h, i, 0))],
        out_shape=[jax.ShapeDtypeStruct((h_n, s_n, dv), F32), jax.ShapeDtypeStruct((h_n, s_n, 1), F32)],
        compiler_params=_cparams(), name=name)(q, k, v)


def _mla_attn_bwd(q, k, v, o, do, lse, *, name):
    h_n, s_n, dq = q.shape
    dv = v.shape[-1]
    tq = MLA_TQ
    scale = float(dq) ** -0.5

    def body(q_ref, k_ref, v_ref, o_ref, do_ref, lse_ref, dq_ref, dk_ref, dv_ref):
        i = pl.program_id(1)

        @pl.when(i == 0)
        def _():
            dk_ref[...] = jnp.zeros_like(dk_ref)
            dv_ref[...] = jnp.zeros_like(dv_ref)

        for e in range(1, s_n // tq + 1):
            @pl.when(i == e - 1)
            def _(ext=e * tq):
                mask = _causal_mask(i, tq, ext)
                qv, kv, vv = q_ref[...], k_ref[0:ext, :], v_ref[0:ext, :]
                do_v = do_ref[...]
                s = lax.dot_general(qv, kv, _NT, preferred_element_type=F32) * scale
                p = jnp.where(mask, jnp.exp(s - lse_ref[...]), 0.0)
                dob = do_v.astype(BF16)
                dv_ref[0:ext, :] += lax.dot_general(p.astype(BF16), dob, _TN, preferred_element_type=F32)
                dp = lax.dot_general(dob, vv, _NT, preferred_element_type=F32)
                delta = jnp.sum(do_v * o_ref[...], axis=-1, keepdims=True)
                dsb = (p * (dp - delta) * scale).astype(BF16)
                dq_ref[...] = jnp.dot(dsb, kv, preferred_element_type=F32)
                dk_ref[0:ext, :] += lax.dot_general(dsb, qv, _TN, preferred_element_type=F32)

    return pl.pallas_call(
        body, grid=(h_n, s_n // tq),
        in_specs=[pl.BlockSpec((None, tq, dq), lambda h, i: (h, i, 0)),
                  pl.BlockSpec((None, s_n, dq), lambda h, i: (h, 0, 0)),
                  pl.BlockSpec((None, s_n, dv), lambda h, i: (h, 0, 0)),
                  pl.BlockSpec((None, tq, dv), lambda h, i: (h, i, 0)),
                  pl.BlockSpec((None, tq, dv), lambda h, i: (h, i, 0)),
                  pl.BlockSpec((None, tq, 1), lambda h, i: (h, i, 0))],
        out_specs=[pl.BlockSpec((None, tq, dq), lambda h, i: (h, i, 0)),
                   pl.BlockSpec((None, s_n, dq), lambda h, i: (h, 0, 0)),
                   pl.BlockSpec((None, s_n, dv), lambda h, i: (h, 0, 0))],
        out_shape=[jax.ShapeDtypeStruct((h_n, s_n, dq), F32), jax.ShapeDtypeStruct((h_n, s_n, dq), F32),
                   jax.ShapeDtypeStruct((h_n, s_n, dv), F32)],
        compiler_params=_cparams(), name=name)(q, k, v, o, do, lse)


def _head_sum(x, *, name):
    h_n, s_n, w = x.shape
    tm = _pick(s_n, 512, 8)

    def body(x_ref, o_ref):
        o_ref[...] = jnp.sum(x_ref[...], axis=0)

    return pl.pallas_call(body, grid=(s_n // tm,), in_specs=[pl.BlockSpec((h_n, tm, w), lambda i: (0, i, 0))],
                          out_specs=_rows(tm, w), out_shape=jax.ShapeDtypeStruct((s_n, w), F32),
                          compiler_params=_cparams(), name=name)(x)


N_BLK = SEQ // DIL_BLOCK
DIL_SCALE = 64 ** -0.5


def _dil_masks():
    iq = lax.broadcasted_iota(jnp.int32, (DIL_BLOCK, 2 * DIL_BLOCK), 0)
    ik = lax.broadcasted_iota(jnp.int32, (DIL_BLOCK, 2 * DIL_BLOCK), 1)
    rel = DIL_BLOCK + iq - ik
    both = (rel >= 0) & (rel <= DIL_BLOCK)
    iq1 = lax.broadcasted_iota(jnp.int32, (DIL_BLOCK, DIL_BLOCK), 0)
    ik1 = lax.broadcasted_iota(jnp.int32, (DIL_BLOCK, DIL_BLOCK), 1)
    return both, ik1 <= iq1


def _dil_block(j, d):
    nb = SEQ // d // DIL_BLOCK
    r, n = divmod(j, nb)
    first = n == 0
    rows = lambda start, size: pl.ds(start, size) if d == 1 else pl.ds(start, size, stride=d)
    q_rows = rows(n * DIL_BLOCK * d + r, DIL_BLOCK)
    k_rows = q_rows if first else rows((n - 1) * DIL_BLOCK * d + r, 2 * DIL_BLOCK)
    return q_rows, k_rows, (DIL_BLOCK if first else 0), first


PAIR = 2 * 64
N_PAIR = HEADS // 2


def _dil_head_specs(s_n, g):
    return [pl.BlockSpec((None, s_n, PAIR), lambda hp, t=t: ((g * 3 + t) * N_PAIR + hp, 0, 0)) for t in range(3)]


def _pair_specs(s_n, w):
    return pl.BlockSpec((2, s_n, w), lambda hp: (hp, 0, 0))


_PAIR_BIAS = pl.BlockSpec((2, DIL_BLOCK, 2 * DIL_BLOCK), lambda hp: (hp, 0, 0))


def _dil_attn_fwd(heads, bias, g, d, *, name):
    _, s_n, _ = heads.shape
    e = PAIR // 2

    def body(q_ref, k_ref, v_ref, b_ref, o_ref, lse_ref):
        m_both, m_first = _dil_masks()
        for j in range(N_BLK):
            q_rows, k_rows, b_lo, first = _dil_block(j, d)
            q2 = q_ref[q_rows, :].astype(BF16)
            k2 = k_ref[k_rows, :].astype(BF16)
            v2 = v_ref[k_rows, :].astype(BF16)
            for hh in range(2):
                cols = slice(hh * e, (hh + 1) * e)
                s = (lax.dot_general(q2[:, cols], k2[:, cols], _NT, preferred_element_type=F32) * DIL_SCALE
                     + b_ref[hh, :, b_lo:])
                s = jnp.where(m_first if first else m_both, s, -jnp.inf)
                m = jnp.max(s, axis=-1, keepdims=True)
                lse = m + jnp.log(jnp.sum(jnp.exp(s - m), axis=-1, keepdims=True))
                p = jnp.exp(s - lse)
                o_ref[hh, q_rows, :] = jnp.dot(p.astype(BF16), v2[:, cols], preferred_element_type=F32)
                lse_ref[hh, q_rows, :] = lse

    return pl.pallas_call(
        body, grid=(N_PAIR,), in_specs=_dil_head_specs(s_n, g) + [_PAIR_BIAS],
        out_specs=[_pair_specs(s_n, e), _pair_specs(s_n, 1)],
        out_shape=[jax.ShapeDtypeStruct((HEADS, s_n, e), F32), jax.ShapeDtypeStruct((HEADS, s_n, 1), F32)],
        compiler_params=_cparams(), name=name)(heads, heads, heads, bias)


def _dil_attn_bwd(heads, bias, lse, do, dlt, g, d, *, name):
    _, s_n, _ = heads.shape
    e = PAIR // 2

    def body(q_ref, k_ref, v_ref, b_ref, lse_ref, do_ref, dlt_ref, dq_ref, dk_ref, dv_ref, db_ref):
        db_ref[...] = jnp.zeros_like(db_ref)
        m_both, m_first = _dil_masks()
        nb = s_n // d // DIL_BLOCK
        own_v = own_k = own_rows = None
        for j in range(N_BLK):
            q_rows, k_rows, b_lo, first = _dil_block(j, d)
            q2 = q_ref[q_rows, :].astype(BF16)
            k2 = k_ref[k_rows, :].astype(BF16)
            v2 = v_ref[k_rows, :].astype(BF16)
            dq_h, dv_h, dk_h = [], [], []
            for hh in range(2):
                cols = slice(hh * e, (hh + 1) * e)
                qj, kk, vv = q2[:, cols], k2[:, cols], v2[:, cols]
                s = lax.dot_general(qj, kk, _NT, preferred_element_type=F32) * DIL_SCALE + b_ref[hh, :, b_lo:]
                p = jnp.where(m_first if first else m_both, jnp.exp(s - lse_ref[hh, q_rows, :]), 0.0)
                dob = do_ref[hh, q_rows, :].astype(BF16)
                dv_h.append(lax.dot_general(p.astype(BF16), dob, _TN, preferred_element_type=F32))
                dp = lax.dot_general(dob, vv, _NT, preferred_element_type=F32)
                ds = p * (dp - dlt_ref[hh, q_rows, :])
                db_ref[hh, :, b_lo:] += ds
                dsb = (ds * DIL_SCALE).astype(BF16)
                dq_h.append(jnp.dot(dsb, kk, preferred_element_type=F32))
                dk_h.append(lax.dot_general(dsb, qj, _TN, preferred_element_type=F32))
            dq_ref[q_rows, :] = jnp.concatenate(dq_h, axis=1)
            dvv, dkk = jnp.concatenate(dv_h, axis=1), jnp.concatenate(dk_h, axis=1)
            if not first:
                dv_ref[own_rows, :] = own_v + dvv[:DIL_BLOCK]
                dk_ref[own_rows, :] = own_k + dkk[:DIL_BLOCK]
                dvv, dkk = dvv[DIL_BLOCK:], dkk[DIL_BLOCK:]
            own_v, own_k, own_rows = dvv, dkk, q_rows
            if j % nb == nb - 1:
                dv_ref[own_rows, :] = own_v
                dk_ref[own_rows, :] = own_k

    slab = pl.BlockSpec((None, s_n, PAIR), lambda hp: (hp, 0, 0))
    sd = jax.ShapeDtypeStruct((N_PAIR, s_n, PAIR), F32)
    return pl.pallas_call(
        body, grid=(N_PAIR,),
        in_specs=_dil_head_specs(s_n, g) + [_PAIR_BIAS, _pair_specs(s_n, 1), _pair_specs(s_n, e), _pair_specs(s_n, 1)],
        out_specs=[slab, slab, slab, _PAIR_BIAS],
        out_shape=[sd, sd, sd, jax.ShapeDtypeStruct((HEADS, DIL_BLOCK, 2 * DIL_BLOCK), F32)],
        compiler_params=_cparams(), name=name)(heads, heads, heads, bias, lse, do, dlt)


def _proj_heads(x, w, *, name):
    s_n, k = x.shape
    n = w.shape[-1]
    tm, tn, e = 1024, 768, PAIR
    per_blk, n_blk = tn // e, n // tn

    def body(x_ref, w_ref, o_ref):
        r = jnp.dot(x_ref[...], w_ref[...], preferred_element_type=F32)
        for j in range(per_blk):
            o_ref[j] = r[:, e * j:e * (j + 1)]

    return pl.pallas_call(
        body, grid=(w.shape[0], n_blk, s_n // tm),
        in_specs=[pl.BlockSpec((tm, k), lambda s, b, m: (m, 0)), pl.BlockSpec((None, k, tn), lambda s, b, m: (s, 0, b))],
        out_specs=pl.BlockSpec((per_blk, tm, e), lambda s, b, m: (s * n_blk + b, m, 0)),
        out_shape=jax.ShapeDtypeStruct((w.shape[0] * n // e, s_n, e), F32), compiler_params=_cparams(),
        name=name)(x, w)


def _heads_cat(d_ref):
    return jnp.concatenate([d_ref[j] for j in range(d_ref.shape[0])], axis=1)


def _proj_heads_dw(x, dh, *, name):
    s_n, k = x.shape
    tn, e = 768, PAIR
    per_blk = tn // e
    n_blk = dh.shape[0] // N_SHARD // per_blk
    n = n_blk * tn

    def body(x_ref, d_ref, o_ref):
        o_ref[...] = lax.dot_general(x_ref[...], _heads_cat(d_ref), _TN, preferred_element_type=F32)

    return pl.pallas_call(
        body, grid=(N_SHARD, n_blk, 2),
        in_specs=[pl.BlockSpec((s_n, k // 2), lambda s, b, r: (0, r)),
                  pl.BlockSpec((per_blk, s_n, e), lambda s, b, r: (s * n_blk + b, 0, 0))],
        out_specs=pl.BlockSpec((None, None, k // 2, tn), lambda s, b, r: (r, s, 0, b)),
        out_shape=jax.ShapeDtypeStruct((2, N_SHARD, k // 2, n), F32), compiler_params=_cparams(), name=name)(x, dh)


def _proj_heads_dx(dh, w, *, name):
    k, n = w.shape[1:]
    s_n = dh.shape[1]
    tm, tn, e = 512, 1152, PAIR
    per_blk, n_blk = tn // e, n // tn

    def body(d_ref, w_ref, o_ref):
        r = lax.dot_general(_heads_cat(d_ref), w_ref[...], _NT, preferred_element_type=F32)
        g = pl.program_id(1)

        @pl.when(g == 0)
        def _():
            o_ref[...] = r

        @pl.when(g > 0)
        def _():
            o_ref[...] += r

    return pl.pallas_call(
        body, grid=(s_n // tm, N_SHARD * n_blk),
        in_specs=[pl.BlockSpec((per_blk, tm, e), lambda m, g: (g, m, 0)),
                  pl.BlockSpec((None, k, tn), lambda m, g: (g // n_blk, 0, g % n_blk))],
        out_specs=pl.BlockSpec((tm, k), lambda m, g: (m, 0)),
        out_shape=jax.ShapeDtypeStruct((s_n, k), F32), compiler_params=_cparams(), name=name)(dh, w)


def _group_alpha(ls):
    m = jnp.maximum(jnp.maximum(ls[0], ls[1]), ls[2])
    es = [jnp.exp(l - m) for l in ls]
    tot = es[0] + es[1] + es[2]
    return [ex / tot for ex in es]


def _dil_mix_fwd(os_, ls_, *, name):
    h_n, s_n, e = os_[0].shape
    tm = 1024

    def body(o0, o1, o2, l0, l1, l2, out_ref):
        for hh in range(2):
            al = _group_alpha([l[hh] for l in (l0, l1, l2)])
            mix = al[0] * o0[hh] + al[1] * o1[hh] + al[2] * o2[hh]
            out_ref[:, hh * e:(hh + 1) * e] = mix.astype(out_ref.dtype)

    blk = lambda w: pl.BlockSpec((2, tm, w), lambda h, i: (h, i, 0))
    return pl.pallas_call(body, grid=(h_n // 2, s_n // tm), in_specs=[blk(e)] * 3 + [blk(1)] * 3,
                          out_specs=pl.BlockSpec((tm, 2 * e), lambda h, i: (i, h)),
                          out_shape=jax.ShapeDtypeStruct((s_n, h_n * e), BF16), compiler_params=_cparams(),
                          name=name)(*os_, *ls_)


def _dil_mix_bwd(do_flat, os_, ls_, *, name):
    h_n, s_n, e = os_[0].shape
    tm = 1024

    def body(do_ref, o0, o1, o2, l0, l1, l2, d0, d1, d2, t0, t1, t2):
        for hh in range(2):
            al = _group_alpha([l[hh] for l in (l0, l1, l2)])
            do_v = do_ref[:, hh * e:(hh + 1) * e]
            mix = al[0] * o0[hh] + al[1] * o1[hh] + al[2] * o2[hh]
            dbar = jnp.sum(do_v * mix, axis=-1, keepdims=True)
            for a_g, d_ref, t_ref in zip(al, (d0, d1, d2), (t0, t1, t2)):
                d_ref[hh] = a_g * do_v
                t_ref[hh] = a_g * dbar

    blk = lambda w: pl.BlockSpec((2, tm, w), lambda h, i: (h, i, 0))
    sd_e = jax.ShapeDtypeStruct((h_n, s_n, e), F32)
    sd_1 = jax.ShapeDtypeStruct((h_n, s_n, 1), F32)
    outs = pl.pallas_call(body, grid=(h_n // 2, s_n // tm),
                          in_specs=[pl.BlockSpec((tm, 2 * e), lambda h, i: (i, h))] + [blk(e)] * 3 + [blk(1)] * 3,
                          out_specs=[blk(e)] * 3 + [blk(1)] * 3, out_shape=[sd_e] * 3 + [sd_1] * 3,
                          compiler_params=_cparams(), name=name)(do_flat, *os_, *ls_)
    return outs[:3], outs[3:]


def _bias_grad(ds, bucket, *, name):
    h_n = ds.shape[0]

    def body(ds_ref, bk_ref, o_ref):
        ds_v = ds_ref[...]
        bk = bk_ref[...]
        lane = lax.broadcasted_iota(jnp.int32, (1, N_BUCKETS), 1)
        acc = jnp.zeros((1, N_BUCKETS), F32)
        for b in range(N_BUCKETS):
            tot = jnp.sum(jnp.sum(jnp.where(bk == b, ds_v, 0.0), axis=1, keepdims=True), axis=0, keepdims=True)
            acc = acc + jnp.where(lane == b, tot, 0.0)
        o_ref[...] = acc

    return pl.pallas_call(
        body, grid=(h_n,),
        in_specs=[pl.BlockSpec((None, DIL_BLOCK, 2 * DIL_BLOCK), lambda h: (h, 0, 0)),
                  pl.BlockSpec((DIL_BLOCK, 2 * DIL_BLOCK), lambda h: (0, 0))],
        out_specs=pl.BlockSpec((None, 1, N_BUCKETS), lambda h: (h, 0, 0)),
        out_shape=jax.ShapeDtypeStruct((h_n, 1, N_BUCKETS), F32), compiler_params=_cparams(), name=name)(ds, bucket)


def _bias_table(rb, bucket, *, name):
    h_n = rb.shape[0]

    def body(rb_ref, bk_ref, o_ref):
        bk = bk_ref[...]
        row = rb_ref[...]
        acc = jnp.zeros(bk.shape, F32)
        for b in range(N_BUCKETS):
            acc = jnp.where(bk == b, row[:, b:b + 1], acc)
        o_ref[...] = acc

    return pl.pallas_call(
        body, grid=(h_n,),
        in_specs=[pl.BlockSpec((None, 1, N_BUCKETS), lambda h: (h, 0, 0)),
                  pl.BlockSpec((DIL_BLOCK, 2 * DIL_BLOCK), lambda h: (0, 0))],
        out_specs=pl.BlockSpec((None, DIL_BLOCK, 2 * DIL_BLOCK), lambda h: (h, 0, 0)),
        out_shape=jax.ShapeDtypeStruct((h_n, DIL_BLOCK, 2 * DIL_BLOCK), F32), compiler_params=_cparams(),
        name=name)(rb, bucket)


def _row_tile(rows, cols, budget=2 << 20):
    if rows * cols * 4 <= budget or rows % 8:
        return rows
    best = 8
    for t in range(8, rows + 1, 8):
        if rows % t == 0 and t * cols * 4 <= budget:
            best = t
    return best


def _adamw(w, g, m, v, *, name):
    shape = w.shape
    cols = shape[-1]
    rows = math.prod(shape[:-1]) if len(shape) > 1 else 1
    to2 = lambda t: t.reshape(rows, cols)
    tr = _row_tile(rows, cols)
    c1 = 1.0 / (1.0 - ADAM_B1 ** ADAM_STEP)
    c2 = 1.0 / (1.0 - ADAM_B2 ** ADAM_STEP)

    def body(w_ref, g_ref, m_ref, v_ref, d_ref, nm_ref, nv_ref):
        g_v = g_ref[...]
        nm = ADAM_B1 * m_ref[...] + (1.0 - ADAM_B1) * g_v
        nv = ADAM_B2 * v_ref[...] + (1.0 - ADAM_B2) * (g_v * g_v)
        m_hat = nm * c1
        v_hat = nv * c2
        d_ref[...] = -ADAM_LR * (m_hat / (jnp.sqrt(v_hat) + ADAM_EPS) + ADAM_WD * w_ref[...])
        nm_ref[...] = nm
        nv_ref[...] = nv

    blk = pl.BlockSpec((tr, cols), lambda i: (i, 0))
    sd = jax.ShapeDtypeStruct((rows, cols), F32)
    outs = pl.pallas_call(body, grid=(rows // tr,), in_specs=[blk] * 4, out_specs=[blk] * 3, out_shape=[sd] * 3,
                          compiler_params=_cparams(), name=name)(to2(w), to2(g), to2(m), to2(v))
    return tuple(t.reshape(shape) for t in outs)


def _add_half(unit, got, half_idx, *, name):
    rest = unit.shape[2:]
    c = rest[-1]
    r = math.prod(rest[:-1])
    tr = _row_tile(r, c, budget=4 << 20)

    def body(idx_ref, u_ref, g_ref, o_ref, w_ref):
        tot = u_ref[...] + g_ref[...].astype(F32)
        o_ref[...] = tot
        w_ref[...] = tot.astype(BF16)

    blk = pl.BlockSpec((None, tr, c), lambda s, i, idx: (s, i, 0))
    grid_spec = pltpu.PrefetchScalarGridSpec(
        num_scalar_prefetch=1, grid=(N_SHARD, r // tr),
        in_specs=[pl.BlockSpec((None, None, tr, c), lambda s, i, idx: (idx[0], s, i, 0)), blk],
        out_specs=[blk, blk])
    out, wire = pl.pallas_call(
        body, grid_spec=grid_spec,
        out_shape=[jax.ShapeDtypeStruct((N_SHARD, r, c), F32), jax.ShapeDtypeStruct((N_SHARD, r, c), BF16)],
        compiler_params=_cparams(), name=name)(half_idx, unit.reshape(2, N_SHARD, r, c), got.reshape(N_SHARD, r, c))
    return out.reshape((N_SHARD,) + rest), wire.reshape((N_SHARD,) + rest)


def _add_shards(part, got, shard_idx, *, name):
    rest = part.shape[1:]
    c = rest[-1]
    r = math.prod(rest[:-1])
    tr = _row_tile(r, c, budget=4 << 20)

    def body(idx_ref, p_ref, g_ref, o_ref):
        acc = p_ref[...]
        for k in range(3):
            acc = acc + g_ref[k].astype(F32)
        o_ref[...] = acc

    grid_spec = pltpu.PrefetchScalarGridSpec(
        num_scalar_prefetch=1, grid=(r // tr,),
        in_specs=[pl.BlockSpec((None, tr, c), lambda i, idx: (idx[0], i, 0)),
                  pl.BlockSpec((3, tr, c), lambda i, idx: (0, i, 0))],
        out_specs=pl.BlockSpec((tr, c), lambda i, idx: (i, 0)))
    out = pl.pallas_call(body, grid_spec=grid_spec, out_shape=jax.ShapeDtypeStruct((r, c), F32),
                         compiler_params=_cparams(), name=name)(
        shard_idx, part.reshape(N_SHARD, r, c), got.reshape(3, r, c))
    return out.reshape(rest)


def _sum_devices(x, n_dev, *, name):
    rows = x.shape[0] // n_dev

    def body(x_ref, o_ref):
        acc = x_ref[0:rows, :]
        for d in range(1, n_dev):
            acc = acc + x_ref[d * rows:(d + 1) * rows, :]
        o_ref[...] = acc

    return pl.pallas_call(body, out_shape=jax.ShapeDtypeStruct((rows, x.shape[1]), F32), name=name)(x)


def _my_pos():
    return lax.axis_index("x"), lax.axis_index("y"), lax.axis_index("c")


def _all_gather(x_blk, *, name, in_vmem):
    m_per, n = x_blk.shape

    def body(x_ref, out_ref, send_sems, recv_sems, local_sem):
        x, y, c = _my_pos()
        me, sibling = (x, y, c), (x, y, 1 - c)
        chips = [(1 - x, y), (x, 1 - y), (1 - x, 1 - y)]

        def rows(px, py, pc):
            return out_ref.at[pl.ds((4 * px + 2 * py + pc) * m_per, m_per), :]

        def copy(k, block, to, src=None):
            return pltpu.make_async_remote_copy(
                src_ref=rows(*block) if src is None else src, dst_ref=rows(*block),
                send_sem=send_sems.at[k], recv_sem=recv_sems.at[k], device_id=to, device_id_type=MESH)

        mine = pltpu.make_async_copy(x_ref, rows(*me), local_sem)
        mine.start()
        first = [copy(0, me, sibling, src=x_ref)]
        first += [copy(1 + j, me, (*chip, c), src=x_ref) for j, chip in enumerate(chips)]
        for cp in first:
            cp.start()
        passed = [copy(4 + j, (*chip, c), sibling) for j, chip in enumerate(chips)]
        for j, chip in enumerate(chips):
            copy(1 + j, (*chip, c), me).wait_recv()
            passed[j].start()
        copy(0, sibling, me).wait_recv()
        for j, chip in enumerate(chips):
            copy(4 + j, (*chip, 1 - c), me).wait_recv()
        for cp in first + passed:
            cp.wait_send()
        mine.wait()

    space = pltpu.VMEM if in_vmem else pl.ANY
    return pl.pallas_call(
        body, out_shape=jax.ShapeDtypeStruct((8 * m_per, n), x_blk.dtype),
        in_specs=[pl.BlockSpec(memory_space=space)], out_specs=pl.BlockSpec(memory_space=space),
        scratch_shapes=[pltpu.SemaphoreType.DMA((7,)), pltpu.SemaphoreType.DMA((7,)), pltpu.SemaphoreType.DMA],
        name=name)(x_blk)


_HBM = pl.BlockSpec(memory_space=pl.ANY)


def _gather_weights(fams, *, name):
    n = len(fams)

    def body(*refs):
        ins, outs = refs[:n], refs[n:2 * n]
        send_sems, recv_sems = refs[2 * n:]
        x, y, c = _my_pos()
        me, sibling = (x, y, c), (x, y, 1 - c)
        chips = [(1 - x, y), (x, 1 - y), (1 - x, 1 - y)]

        def copy(f, k, block, to, src=None):
            px, py, pc = block
            dst = outs[f].at[2 * px + py, pc]
            return pltpu.make_async_remote_copy(
                src_ref=dst if src is None else src, dst_ref=dst, send_sem=send_sems.at[7 * f + k],
                recv_sem=recv_sems.at[7 * f + k], device_id=to, device_id_type=MESH)

        first, passed = [], []
        for f in range(n):
            src = ins[f].at[c]
            first.append(copy(f, 0, me, sibling, src=src))
            first += [copy(f, 1 + j, me, (*chip, c), src=src) for j, chip in enumerate(chips)]
        for cp in first:
            cp.start()
        for j, chip in enumerate(chips):
            for f in range(n):
                copy(f, 1 + j, (*chip, c), me).wait_recv()
                passed.append(copy(f, 4 + j, (*chip, c), sibling))
                passed[-1].start()
        for f in range(n):
            copy(f, 0, sibling, me).wait_recv()
        for j, chip in enumerate(chips):
            for f in range(n):
                copy(f, 4 + j, (*chip, 1 - c), me).wait_recv()
        for cp in first + passed:
            cp.wait_send()

    outs = pl.pallas_call(
        body, out_shape=[jax.ShapeDtypeStruct((N_SHARD,) + t.shape, t.dtype) for t in fams],
        in_specs=[_HBM] * n, out_specs=[_HBM] * n,
        scratch_shapes=[pltpu.SemaphoreType.DMA((7 * n,)), pltpu.SemaphoreType.DMA((7 * n,))], name=name)(*fams)
    return [_place_own(o, t) for o, t in zip(outs, fams)]


def _pair_gather(halves, *, name):
    n = len(halves)

    def body(*refs):
        ins, outs = refs[:n], refs[n:2 * n]
        send_sems, recv_sems = refs[2 * n:]
        x, y, c = _my_pos()
        cps = [pltpu.make_async_remote_copy(src_ref=ins[f], dst_ref=outs[f].at[c], send_sem=send_sems.at[f],
                                            recv_sem=recv_sems.at[f], device_id=(x, y, 1 - c), device_id_type=MESH)
               for f in range(n)]
        for cp in cps:
            cp.start()
        for f in range(n):
            pltpu.make_async_remote_copy(src_ref=ins[f], dst_ref=outs[f].at[1 - c], send_sem=send_sems.at[f],
                                         recv_sem=recv_sems.at[f], device_id=(x, y, 1 - c),
                                         device_id_type=MESH).wait_recv()
        for cp in cps:
            cp.wait_send()

    outs = pl.pallas_call(
        body, out_shape=[jax.ShapeDtypeStruct((2,) + t.shape, t.dtype) for t in halves],
        in_specs=[_HBM] * n, out_specs=[_HBM] * n,
        scratch_shapes=[pltpu.SemaphoreType.DMA((n,)), pltpu.SemaphoreType.DMA((n,))], name=name)(*halves)
    c = lax.axis_index("c")
    return [lax.dynamic_update_index_in_dim(o, t, c, 0) for o, t in zip(outs, halves)]


_HBM_ONLY = pl.BlockSpec(memory_space=pltpu.HBM)
_SEMS = pl.BlockSpec(memory_space=pltpu.SEMAPHORE)
_EFFECT = pltpu.SideEffectType.DATAFLOW_SIDE_EFFECTING


def _copies_start(srcs, lands, plan, n_copies, *, name):
    n, m = len(srcs), len(lands)

    def body(*refs):
        src_refs, land_refs = refs[:n], refs[n:n + m]
        send_sems, recv_sems, token = refs[n + m], refs[n + m + 1], refs[-1]
        for k, (src, dst, peer) in enumerate(plan(src_refs, land_refs)):
            pltpu.make_async_remote_copy(src_ref=src, dst_ref=dst, send_sem=send_sems.at[k], recv_sem=recv_sems.at[k],
                                         device_id=peer, device_id_type=MESH).start()
        token[...] = jnp.zeros_like(token)

    bufs = [pltpu.with_memory_space_constraint(t, pltpu.HBM) for t in (*srcs, *lands)]
    outs = pl.pallas_call(
        body, name=name,
        out_shape=(pltpu.SemaphoreType.DMA((n_copies,)), pltpu.SemaphoreType.DMA((n_copies,)),
                   *[pltpu.HBM(t.shape, t.dtype) for t in bufs], jax.ShapeDtypeStruct((8, 128), F32)),
        in_specs=[_HBM_ONLY] * (n + m),
        out_specs=(_SEMS, _SEMS, *[_HBM_ONLY] * (n + m), pl.BlockSpec(memory_space=pltpu.VMEM)),
        input_output_aliases={k: 2 + k for k in range(n + m)},
        compiler_params=pltpu.CompilerParams(has_side_effects=_EFFECT))(*bufs)
    return outs[0], outs[1], list(outs[2:2 + n + m]), outs[-1]


def _copies_wait(send_sems, recv_sems, thru, n_src, plan, after, *, name):
    nm = len(thru)

    def body(*refs):
        t_refs, send, recv = refs[:nm], refs[nm], refs[nm + 1]
        for k, (src, dst, peer) in enumerate(plan(t_refs[:n_src], t_refs[n_src:])):
            cp = pltpu.make_async_remote_copy(src_ref=src, dst_ref=dst, send_sem=send.at[k], recv_sem=recv.at[k],
                                              device_id=peer, device_id_type=MESH)
            cp.wait_send()
            cp.wait_recv()

    outs = pl.pallas_call(
        body, name=name, out_shape=tuple(pltpu.HBM(t.shape, t.dtype) for t in thru),
        in_specs=[_HBM_ONLY] * nm + [_SEMS, _SEMS, pl.BlockSpec(memory_space=pl.ANY)],
        out_specs=tuple([_HBM_ONLY] * nm), input_output_aliases={k: k for k in range(nm)},
        compiler_params=pltpu.CompilerParams(has_side_effects=_EFFECT))(*thru, send_sems, recv_sems, after)
    return list(outs)


_RELATIONS = [(dx, dy, dc) for dx in (0, 1) for dy in (0, 1) for dc in (0, 1)][1:]


def _gather_plan(src_refs, land_refs):
    x, y, c = _my_pos()
    flip = lambda v, d: 1 - v if d else v
    return [(s_ref.at[c], l_ref.at[2 * x + y, c], (flip(x, dx), flip(y, dy), flip(c, dc)))
            for s_ref, l_ref in zip(src_refs, land_refs) for dx, dy, dc in _RELATIONS]


def _gather_chips_plan(src_refs, land_refs):
    x, y, c = _my_pos()
    peers = [(x, y, 1 - c), (1 - x, y, c), (x, 1 - y, c), (1 - x, 1 - y, c)]
    return [(s_ref.at[c], l_ref.at[2 * x + y, c], peer) for s_ref, l_ref in zip(src_refs, land_refs) for peer in peers]


def _gather_pass_plan(src_refs, land_refs):
    x, y, c = _my_pos()
    chips = [(1 - x, y), (x, 1 - y), (1 - x, 1 - y)]
    return [(l_ref.at[2 * cx + cy, c], l_ref.at[2 * cx + cy, c], (x, y, 1 - c))
            for l_ref in land_refs for cx, cy in chips]


def _sibling_plan(src_refs, land_refs):
    x, y, c = _my_pos()
    return [(s_ref.at[1 - c], l_ref, (x, y, 1 - c)) for s_ref, l_ref in zip(src_refs, land_refs)]


def _chips_plan(src_refs, land_refs):
    x, y, c = _my_pos()
    chips = [(1 - x, y), (x, 1 - y), (1 - x, 1 - y)]
    return [(s_ref.at[2 * cx + cy], l_ref.at[k], (cx, cy, c))
            for s_ref, l_ref in zip(src_refs, land_refs) for k, (cx, cy) in enumerate(chips)]


def _place_own(gathered, fam):
    x, y, c = _my_pos()
    own = lax.dynamic_index_in_dim(fam, c, 0, keepdims=True)[None]
    return lax.dynamic_update_slice(gathered, own, (2 * x + y, c) + (0,) * (fam.ndim - 1))


def _to_heads(t, width):
    return t.reshape(t.shape[0], HEADS, width).transpose(1, 0, 2)


def _from_heads(t):
    return t.transpose(1, 0, 2).reshape(t.shape[1], -1)


def _t5_bucket(dist):
    max_exact = N_BUCKETS // 2
    d = jnp.maximum(dist, 1).astype(F32)
    large = max_exact + (jnp.log(d / max_exact) / math.log(MAX_DISTANCE / max_exact)
                         * (N_BUCKETS - max_exact)).astype(jnp.int32)
    large = jnp.minimum(large, N_BUCKETS - 1)
    return jnp.where(dist < max_exact, dist, large)


def _bucket_map(dilation):
    iq = jnp.arange(DIL_BLOCK)[:, None]
    ik = jnp.arange(2 * DIL_BLOCK)[None, :]
    rel = DIL_BLOCK + iq - ik
    return _t5_bucket(jnp.maximum(rel, 0) * dilation).astype(jnp.int32)


def _q_perm(w):
    w3 = w.reshape(w.shape[0], HEADS, QK_NOPE + QK_ROPE)
    return jnp.concatenate([w3[:, :, :QK_NOPE].reshape(w.shape[0], -1),
                            w3[:, :, QK_NOPE:QK_NOPE + HALF_ROPE].reshape(w.shape[0], -1),
                            w3[:, :, QK_NOPE + HALF_ROPE:].reshape(w.shape[0], -1)], axis=1)


def _q_unperm(w):
    n0, n1 = HEADS * QK_NOPE, HEADS * HALF_ROPE
    r = w.shape[0]
    return jnp.concatenate([w[:, :n0].reshape(r, HEADS, QK_NOPE), w[:, n0:n0 + n1].reshape(r, HEADS, HALF_ROPE),
                            w[:, n0 + n1:].reshape(r, HEADS, HALF_ROPE)], axis=2).reshape(r, -1)


def _kv_perm(w):
    w3 = w.reshape(w.shape[0], HEADS, QK_NOPE + V_HEAD)
    return jnp.concatenate([w3[:, :, :QK_NOPE].reshape(w.shape[0], -1), w3[:, :, QK_NOPE:].reshape(w.shape[0], -1)],
                           axis=1)


def _kv_unperm(w):
    n0 = HEADS * QK_NOPE
    r = w.shape[0]
    return jnp.concatenate([w[:, :n0].reshape(r, HEADS, QK_NOPE), w[:, n0:].reshape(r, HEADS, V_HEAD)],
                           axis=2).reshape(r, -1)


def _row(v):
    return v.reshape(1, -1)


def kernel(x, c, norm_pre, norm_post, w_mod, b_mod, ffn_w_gate, ffn_w_up, ffn_w_down, mla_w_in, mla_q_norm, mla_w_q_up, mla_kv_norm, mla_w_kv_up, mla_w_o, dil_w_in, dil_w_o, rel_bias, loss_target, m_norm_pre, m_norm_post, m_w_mod, m_b_mod, m_ffn_w_gate, m_ffn_w_up, m_ffn_w_down, m_mla_w_in, m_mla_q_norm, m_mla_w_q_up, m_mla_kv_norm, m_mla_w_kv_up, m_mla_w_o, m_dil_w_in, m_dil_w_o, m_rel_bias, v_norm_pre, v_norm_post, v_w_mod, v_b_mod, v_ffn_w_gate, v_ffn_w_up, v_ffn_w_down, v_mla_w_in, v_mla_q_norm, v_mla_w_q_up, v_mla_kv_norm, v_mla_w_kv_up, v_mla_w_o, v_dil_w_in, v_dil_w_o, v_rel_bias):
    given = dict(locals())
    ix, iy, ic = _my_pos()
    shard_id = 2 * ix + iy
    dev_id = 4 * ix + 2 * iy + ic
    x2 = x[0]
    target = loss_target[0]
    half_idx = jnp.reshape(ic, (1,)).astype(jnp.int32)
    shard_idx = jnp.reshape(shard_id, (1,)).astype(jnp.int32)

    blk = jnp.zeros((8, D_MODEL), F32)
    blk = blk.at[0].set(c[0])
    blk = blk.at[1:3].set(jnp.pad(norm_pre.reshape(-1), (0, 512)).reshape(2, D_MODEL))
    blk = blk.at[3:5].set(jnp.pad(norm_post.reshape(-1), (0, 512)).reshape(2, D_MODEL))
    got = _all_gather(blk, name="ag_c_norms", in_vmem=True).reshape(N_SHARD, 2, 8, D_MODEL)
    c_all = got[:, :, 0, :].reshape(8, D_MODEL)

    def full_norm(lo):
        t = got[:, 0, lo:lo + 2, :].reshape(N_SHARD, 2 * D_MODEL)[:, :1536].reshape(N_SHARD, 2, 3, 256)
        return t.transpose(1, 2, 0, 3).reshape(2, 3, D_MODEL)

    pre_full, post_full = full_norm(1), full_norm(3)

    silu_c = _silu_bf16(c_all, name="silu_c")
    b_cols = lax.dynamic_slice_in_dim(b_mod, shard_id * 2304, 2304, axis=1).reshape(2, 1, 2304)
    mod_part = _mm(silu_c, w_mod, bias=b_cols, name="mod_mm", tn_cap=768)
    mod_all = _all_gather(mod_part.reshape(16, 2304), name="ag_mod", in_vmem=True)
    mod_all = mod_all.reshape(N_SHARD, 2, 2, 8, 2304)[:, 0]
    mod_mine = lax.dynamic_index_in_dim(mod_all, dev_id, axis=2, keepdims=False)
    mod = mod_mine.transpose(1, 0, 2).reshape(2, 9, D_MODEL)

    bf = lambda t: t.astype(BF16)
    ffn_fam = lambda i, h: [bf(jnp.stack([ffn_w_gate[i, h], ffn_w_up[i, h]])),
                            bf(ffn_w_down[i, h].reshape(2, F_SHARD // 2, D_MODEL))]
    mla_fam = [bf(mla_w_in.reshape(2, 128, -1)), bf(mla_w_q_up.reshape(2, 192, -1)),
               bf(mla_w_kv_up.reshape(2, 128, -1)), bf(mla_w_o.reshape(2, 128, D_MODEL))]
    dil_fam = [bf(dil_w_in.reshape(2, 512, -1)), bf(dil_w_o.reshape(2, 128, D_MODEL))]
    later_fams = [ffn_fam(0, 1), ffn_fam(1, 0) + dil_fam, ffn_fam(1, 1)]
    full, later_fams, mod = lax.optimization_barrier(
        (_gather_weights(ffn_fam(0, 0) + mla_fam, name="ag_weights_first"), later_fams, mod))

    def gather_later(fams, tag):
        lands = [lax.empty((N_SHARD,) + t.shape, t.dtype) for t in fams]
        send, recv, thru, token = _copies_start(fams, lands, _gather_plan, 7 * len(fams), name=f"ag_start_{tag}")
        return dict(send=send, recv=recv, thru=thru, token=token, n=len(fams), tag=tag)

    def arrive(st, after):
        thru = _copies_wait(st['send'], st['recv'], st['thru'], st['n'], _gather_plan, after,
                            name=f"ag_wait_{st['tag']}")
        return [_place_own(o, t) for t, o in zip(thru[:st['n']], thru[st['n']:])]

    def gather_chips(fams, tag):
        lands = [lax.empty((N_SHARD,) + t.shape, t.dtype) for t in fams]
        send, recv, thru, token = _copies_start(fams, lands, _gather_chips_plan, 4 * len(fams), name=f"ag_start_{tag}")
        return dict(send=send, recv=recv, thru=thru, token=token, n=len(fams), tag=tag)

    def pass_on(st, after):
        n, tag = st['n'], st['tag']
        thru = _copies_wait(st['send'], st['recv'], st['thru'], n, _gather_chips_plan, after, name=f"ag_mid_{tag}")
        send, recv, lands, token = _copies_start([], thru[n:], _gather_pass_plan, 3 * n, name=f"ag_pass_{tag}")
        return dict(send=send, recv=recv, thru=lands, fams=thru[:n], tag=tag), token[0, 0]

    def arrive_passed(st, after):
        lands = _copies_wait(st['send'], st['recv'], st['thru'], 0, _gather_pass_plan, after, name=f"ag_wait_{st['tag']}")
        return [_place_own(o, t) for t, o in zip(st['fams'], lands)]

    flight_a = gather_later(later_fams[0], "l0s2")
    _, next_fams = lax.optimization_barrier((flight_a['token'], later_fams[1]))
    flight_b = gather_chips(next_fams, "l1s01")
    as_ffn = lambda w_gu, w_dn: (w_gu, w_dn.reshape(N_SHARD, F_SHARD, D_MODEL))
    ffn_w = {(0, 0): as_ffn(full[0], full[1])}
    w_in = full[2].reshape(D_MODEL, -1)
    wq_p = _q_perm(full[3].reshape(N_SHARD, Q_LORA, -1).transpose(1, 0, 2).reshape(Q_LORA, -1))
    wkv_p = _kv_perm(full[4].reshape(N_SHARD, KV_LORA, -1).transpose(1, 0, 2).reshape(KV_LORA, -1))
    w_mo = full[5].reshape(D_MODEL, D_MODEL)
    dil_w = {}

    pos = jnp.arange(SEQ, dtype=F32)
    freqs = ROPE_THETA ** (-jnp.arange(HALF_ROPE, dtype=F32) / HALF_ROPE)
    ang = pos[:, None] * freqs[None, :]
    cos_k, sin_k = jnp.cos(ang), jnp.sin(ang)
    cos_q, sin_q = jnp.tile(cos_k, (1, HEADS)), jnp.tile(sin_k, (1, HEADS))

    buckets = [_bucket_map(d) for _, d in DIL_GROUPS]
    biases = [_bias_table(rel_bias[:, g * HEADS:(g + 1) * HEADS].T.reshape(HEADS, 1, N_BUCKETS), bk,
                          name=f"dil_bias_table_g{g}") for g, bk in enumerate(buckets)]

    vpacks = jnp.concatenate([pre_full[:, :, None], post_full[:, :, None], mod.reshape(2, 3, 3, D_MODEL),
                              jnp.zeros((2, 3, 3, D_MODEL), F32)], axis=2)
    sub_params = lambda i, sub: vpacks[i, sub]

    def ffn_fwd(xin, i, h, sub, tie=None, mid=None):
        p = sub_params(i, sub)
        if tie is not None:
            p = p + tie
        tag = f"l{i}s{sub}"
        w_gu, w_dn = ffn_w[i, h]
        hn = _pre_fwd(xin, p, name=f"pre_fwd_{tag}")
        gu, a = _ffn_up(hn, w_gu, name=f"ffn_up_{tag}")
        if mid is not None:
            p = p + mid(a)
        f, out = _ffn_down(a, w_dn, xin, p, FFN_RES, name=f"ffn_down_{tag}")
        return out, dict(x=xin, hn=hn, gu=gu, a=a, f=f, p=p, i=i, h=h, tag=tag)

    def mla_fwd(xin, i, sub):
        p = sub_params(i, sub)
        tag = f"l{i}s{sub}"
        hn = _pre_fwd(xin, p, name=f"pre_fwd_{tag}")
        lat = _mm(hn, w_in, name="mla_lat")
        cq, ckv = lat[:, :Q_LORA], lat[:, Q_LORA:Q_LORA + KV_LORA]
        k1, k2 = lat[:, Q_LORA + KV_LORA:Q_LORA + KV_LORA + HALF_ROPE], lat[:, Q_LORA + KV_LORA + HALF_ROPE:]
        cqn = _rms_fwd(cq, mla_q_norm, name="mla_qnorm")
        ckvn = _rms_fwd(ckv, mla_kv_norm, name="mla_kvnorm")
        qp = _mm(cqn, wq_p, name="mla_q_up")
        kvp = _mm(ckvn, wkv_p, name="mla_kv_up")
        n0, n1 = HEADS * QK_NOPE, HEADS * HALF_ROPE
        qr1, qr2 = _rope(qp[:, n0:n0 + n1], qp[:, n0 + n1:], cos_q, sin_q, name="rope_q")
        kr1, kr2 = _rope(k1, k2, cos_k, sin_k, name="rope_k")
        q = jnp.concatenate([qp[:, :n0].reshape(SEQ, HEADS, QK_NOPE), qr1.reshape(SEQ, HEADS, HALF_ROPE),
                             qr2.reshape(SEQ, HEADS, HALF_ROPE)], axis=2).transpose(1, 0, 2).astype(BF16)
        kr = jnp.broadcast_to(jnp.concatenate([kr1, kr2], axis=1)[:, None, :], (SEQ, HEADS, QK_ROPE))
        k = jnp.concatenate([kvp[:, :n0].reshape(SEQ, HEADS, QK_NOPE), kr], axis=2).transpose(1, 0, 2).astype(BF16)
        v = _to_heads(kvp[:, n0:], V_HEAD).astype(BF16)
        o, lse = _mla_attn_fwd(q, k, v, name="mla_attn_fwd")
        o_flat = _from_heads(o).astype(BF16)
        f = _mm(o_flat, w_mo, name="mla_out")
        out = _post_fwd(f, xin, p, 1.0, name=f"post_fwd_{tag}")
        return out, dict(x=xin, hn=hn, cq=cq, ckv=ckv, cqn=cqn, ckvn=ckvn, q=q, k=k, v=v, o=o, lse=lse,
                         o_flat=o_flat, f=f, p=p, tag=tag)

    def dil_fwd(xin, i, sub):
        p = sub_params(i, sub)
        tag = f"l{i}s{sub}"
        hn = _pre_fwd(xin, p, name=f"pre_fwd_{tag}")
        heads = _proj_heads(hn, dil_w['in'], name="dil_proj")
        outs, lses = [], []
        for g, (window, d) in enumerate(DIL_GROUPS):
            o, lse = _dil_attn_fwd(heads, biases[g], g, d, name=f"dil_attn_fwd_g{g}")
            outs.append(o)
            lses.append(lse)
        o_flat = _dil_mix_fwd(outs, lses, name="dil_mix_fwd")
        f = _mm(o_flat, dil_w['out'], name="dil_out")
        out = _post_fwd(f, xin, p, 1.0, name=f"post_fwd_{tag}")
        return out, dict(x=xin, hn=hn, heads=heads, outs=outs, lses=lses, o_flat=o_flat, f=f, p=p, tag=tag)

    saved = [None] * 6
    xs, saved[0] = ffn_fwd(x2, 0, 0, 0, tie=flight_a['token'][0, 0] + flight_b['token'][0, 0])
    xs, saved[1] = mla_fwd(xs, 0, 1)
    ffn_w[0, 1] = as_ffn(*arrive(flight_a, xs))
    passed = {}

    def second_step(after):
        passed['st'], tok = pass_on(flight_b, after)
        return tok

    xs, saved[2] = ffn_fwd(xs, 0, 1, 2, mid=second_step)
    got, last_fams = lax.optimization_barrier((arrive_passed(passed['st'], xs), later_fams[2]))
    ffn_w[1, 0] = as_ffn(got[0], got[1])
    dil_w['in'], dil_w['out'] = got[2].reshape(N_SHARD, D_MODEL, -1), got[3].reshape(D_MODEL, D_MODEL)
    in_flight = gather_later(last_fams, "l1s2")
    xs, saved[3] = ffn_fwd(xs, 1, 0, 0, tie=in_flight['token'][0, 0])
    xs, saved[4] = dil_fwd(xs, 1, 1)
    ffn_w[1, 1] = as_ffn(*arrive(in_flight, xs))
    xs, saved[5] = ffn_fwd(xs, 1, 1, 2)

    dx, loss_part = _loss(xs, target, name="loss")

    dmod = [[None] * 9 for _ in range(2)]
    dpre = [[None] * 3 for _ in range(2)]
    dpost = [[None] * 3 for _ in range(2)]
    ffn_units = {}
    row_unit = lambda g, r, j: ((r % 2, r // 2), 0, j)

    def close_sub(dhn, dout, sv, i, sub, res_dgate, res_dqg):
        p = sv['p']
        dxs, dsh, dsc, dpg = _pre_bwd(dhn, sv['x'], dout, p, name=f"pre_bwd_{sv['tag']}")
        dmod[i][3 * sub], dmod[i][3 * sub + 1], dmod[i][3 * sub + 2] = dsh, dsc, res_dgate
        dpre[i][sub], dpost[i][sub] = dpg, res_dqg
        return dxs

    def ffn_bwd(dout, sv, sub, tie=0.0, mid=None):
        i, h, p, tag = sv['i'], sv['h'], sv['p'], sv['tag']
        w_gu, w_dn = ffn_w[i, h]
        df, dgate, dqg = _post_bwd(dout, sv['f'], p + tie, FFN_RES, name=f"post_bwd_{tag}")
        u_dn = _mm(sv['a'], df, ta=True, tn_cap=D_MODEL // 2, out_shape=(2, N_SHARD, F_SHARD, D_MODEL // 2),
                   out_sel=lambda g, r, j: ((j, g), r, 0), name=f"ffn_dwd_{tag}")
        dgu = _ffn_dgu(df, w_dn, sv['gu'], name=f"ffn_dgu_{tag}")
        if mid is not None:
            p = p + mid(dgu)
        u_gu = _mm(dgu.reshape(2 * N_SHARD, SEQ, F_SHARD), sv['hn'], ta=True,
                   out_shape=(2, N_SHARD, F_SHARD, D_MODEL), out_sel=lambda g, r, j: ((g % 2, g // 2), r, j),
                   name=f"ffn_dwgu_{tag}")
        ffn_units[i, h] = [u_gu, u_dn]
        dxs, dsh, dsc, dpg = _ffn_dhn(dgu, w_gu, sv['x'], dout, p, name=f"ffn_dhn_{tag}")
        dmod[i][3 * sub], dmod[i][3 * sub + 1], dmod[i][3 * sub + 2] = dsh, dsc, dgate
        dpre[i][sub], dpost[i][sub] = dpg, dqg
        return dxs

    def mla_bwd(dout, sv, i, sub, tie=0.0):
        p, tag = sv['p'], sv['tag']
        df, dgate, dqg = _post_bwd(dout, sv['f'], p + tie, 1.0, name=f"post_bwd_{tag}")
        u_wo = _mm(sv['o_flat'], df, ta=True, tm_cap=128, out_shape=(2, N_SHARD, 128, D_MODEL), out_sel=row_unit,
                   name="mla_dwo")
        do_flat = _mm(df, w_mo, tb=True, name="mla_do")
        do = _to_heads(do_flat, V_HEAD)
        dq, dk, dv = _mla_attn_bwd(sv['q'], sv['k'], sv['v'], sv['o'], do, sv['lse'], name="mla_attn_bwd")
        dq_t = dq.transpose(1, 0, 2)
        dqr1, dqr2 = _rope(dq_t[:, :, QK_NOPE:QK_NOPE + HALF_ROPE].reshape(SEQ, -1),
                           dq_t[:, :, QK_NOPE + HALF_ROPE:].reshape(SEQ, -1), cos_q, -sin_q, name="rope_q_bwd")
        dqp = jnp.concatenate([dq_t[:, :, :QK_NOPE].reshape(SEQ, -1), dqr1, dqr2], axis=1).astype(BF16)
        dkr = _head_sum(dk[:, :, QK_NOPE:], name="mla_dkr_sum")
        dk1, dk2 = _rope(dkr[:, :HALF_ROPE], dkr[:, HALF_ROPE:], cos_k, -sin_k, name="rope_k_bwd")
        dkvp = jnp.concatenate([_from_heads(dk[:, :, :QK_NOPE]), _from_heads(dv)], axis=1).astype(BF16)
        g_wq = _q_unperm(_mm(sv['cqn'], dqp, ta=True, name="mla_dwq"))
        g_wkv = _kv_unperm(_mm(sv['ckvn'], dkvp, ta=True, name="mla_dwkv"))
        dcqn = _mm(dqp, wq_p, tb=True, name="mla_dcqn")
        dckvn = _mm(dkvp, wkv_p, tb=True, name="mla_dckvn")
        dcq, g_qn = _rms_bwd(dcqn, sv['cq'], mla_q_norm, name="mla_qnorm_bwd")
        dckv, g_kvn = _rms_bwd(dckvn, sv['ckv'], mla_kv_norm, name="mla_kvnorm_bwd")
        dlat = jnp.concatenate([dcq, dckv, dk1, dk2], axis=1).astype(BF16)
        u_win = _mm(sv['hn'], dlat, ta=True, tm_cap=128, out_shape=(2, N_SHARD, 128, dlat.shape[1]),
                    out_sel=row_unit, name="mla_dwin")
        dhn = _mm(dlat, w_in, tb=True, name="mla_dhn")
        col_unit = lambda t: (t.reshape(t.shape[0], N_SHARD, -1).transpose(1, 0, 2)
                              .reshape(N_SHARD, 2, t.shape[0] // 2, -1).transpose(1, 0, 2, 3))
        grads = dict(units=[u_win, col_unit(g_wq), col_unit(g_wkv), u_wo], q_norm=g_qn, kv_norm=g_kvn)
        return close_sub(dhn, dout, sv, i, sub, dgate, dqg), grads

    def dil_bwd(dout, sv, i, sub):
        p, tag = sv['p'], sv['tag']
        df, dgate, dqg = _post_bwd(dout, sv['f'], p, 1.0, name=f"post_bwd_{tag}")
        u_wo = _mm(sv['o_flat'], df, ta=True, tm_cap=128, out_shape=(2, N_SHARD, 128, D_MODEL), out_sel=row_unit,
                   name="dil_dwo")
        dos, dlts = _dil_mix_bwd(_mm(df, dil_w['out'], tb=True, name="dil_do"), sv['outs'], sv['lses'],
                                 name="dil_mix_bwd")
        pieces = []
        bias_rows = []
        for g, (window, d) in enumerate(DIL_GROUPS):
            dq, dk, dv, dbias = _dil_attn_bwd(sv['heads'], biases[g], sv['lses'][g], dos[g], dlts[g], g, d,
                                              name=f"dil_attn_bwd_g{g}")
            pieces += [dq, dk, dv]
            bias_rows.append(_bias_grad(dbias, buckets[g], name=f"dil_bias_grad_g{g}")[:, 0, :])
        dheads = jnp.concatenate(pieces).astype(BF16)
        u_win = _proj_heads_dw(sv['hn'], dheads, name="dil_dwin")
        dhn = _proj_heads_dx(dheads, dil_w['in'], name="dil_dhn")
        g_bias = jnp.concatenate(bias_rows, axis=0).T
        grads = dict(units=[u_win, u_wo], rel_bias=g_bias)
        return close_sub(dhn, dout, sv, i, sub, dgate, dqg), grads

    def to_sibling(units, tag):
        n = len(units)
        send, recv, thru, token = _copies_start(units, [lax.empty(u.shape[1:], F32) for u in units], _sibling_plan, n,
                                                name=f"rs{tag}_sibling_start")
        return dict(send=send, recv=recv, thru=thru, n=n, tag=tag), token[0, 0]

    def from_sibling(st, after):
        n, tag = st['n'], st['tag']
        thru = _copies_wait(st['send'], st['recv'], st['thru'], n, _sibling_plan, after, name=f"rs{tag}_sibling_wait")
        return [_add_half(u, g, half_idx, name=f"rs{tag}_add_half_{k}") for k, (u, g) in enumerate(zip(thru[:n], thru[n:]))]

    def to_chips(parts, tag):
        n = len(parts)
        send, recv, thru, token = _copies_start([w for _, w in parts],
                                                [lax.empty((3,) + w.shape[1:], BF16) for _, w in parts], _chips_plan,
                                                3 * n, name=f"rs{tag}_chips_start")
        return dict(send=send, recv=recv, thru=thru, n=n, tag=tag, parts=parts), token[0, 0]

    def from_chips(st, after):
        n, tag = st['n'], st['tag']
        thru = _copies_wait(st['send'], st['recv'], st['thru'], n, _chips_plan, after, name=f"rs{tag}_chips_wait")
        return [_add_shards(p, g, shard_idx, name=f"rs{tag}_add_shards_{k}")
                for k, ((p, _), g) in enumerate(zip(st['parts'], thru[n:]))]

    dx = ffn_bwd(dx, saved[5], 2)
    dx, dil_g = dil_bwd(dx, saved[4], 1, 1)
    dx = ffn_bwd(dx, saved[3], 0)
    st1, tok = to_sibling([*ffn_units[1, 1], *dil_g['units'], *ffn_units[1, 0]], "1")
    dx = ffn_bwd(dx, saved[2], 2, tie=tok)
    st1, tok1 = to_chips(from_sibling(st1, dx), "1")
    st2, tok2 = to_sibling(ffn_units[0, 1], "2")
    dx, mla_g = mla_bwd(dx, saved[1], 0, 1, tie=tok1 + tok2)
    reds1 = from_chips(st1, dx)
    st2, tok = to_chips(from_sibling(st2, dx), "2")
    st3, tok3 = to_sibling(mla_g['units'], "3")
    onward = {}

    def mixer_to_chips(after):
        onward['st'], t = to_chips(from_sibling(st3, after), "3")
        return t

    dx = ffn_bwd(dx, saved[0], 0, tie=tok + tok3, mid=mixer_to_chips)
    reds2 = from_chips(st2, dx)
    reds3 = from_chips(onward['st'], dx)
    grad_x = dx[None]

    pad_row = lambda v: jnp.pad(v.reshape(-1), (0, (-v.size) % D_MODEL)).reshape(-1, D_MODEL)
    small = jnp.concatenate(
        [jnp.concatenate([dmod[i][r] for i in range(2) for r in range(9)], axis=0),
         jnp.concatenate([dpre[i][s] for i in range(2) for s in range(3)], axis=0),
         jnp.concatenate([dpost[i][s] for i in range(2) for s in range(3)], axis=0),
         pad_row(mla_g['q_norm']), pad_row(mla_g['kv_norm']), pad_row(dil_g['rel_bias']), pad_row(loss_part)], axis=0)
    small = jnp.pad(small, ((0, SMALL_ROWS - small.shape[0]), (0, 0)))
    small_all = _all_gather(small, name="ag_small_grads", in_vmem=True)
    small_sum = _sum_devices(small_all, 8, name="sum_small_grads")
    g_b_mod = small_sum[0:18].reshape(2, 9 * D_MODEL)
    my_cols = lambda t: lax.dynamic_slice_in_dim(t, shard_id * 256, 256, axis=2)
    g_norm_pre = my_cols(small_sum[18:24].reshape(2, 3, D_MODEL))
    g_norm_post = my_cols(small_sum[24:30].reshape(2, 3, D_MODEL))
    g_q_norm = small_sum[30, :Q_LORA].reshape(1, Q_LORA)
    g_kv_norm = small_sum[31, :KV_LORA].reshape(1, KV_LORA)
    g_rel_bias = small_sum[32:34].reshape(-1)[:N_BUCKETS * 48].reshape(N_BUCKETS, 48)
    loss = small_sum[34, 0]
    dmod_all = small_all.reshape(8, SMALL_ROWS, D_MODEL)[:, 0:18].reshape(8, 2, 9 * D_MODEL)
    dmod_cols = lax.dynamic_slice_in_dim(dmod_all, shard_id * 2304, 2304, axis=2).transpose(1, 0, 2)

    swap = lambda t: jnp.swapaxes(t, 2, 3)
    grads = dict(norm_pre=g_norm_pre, norm_post=g_norm_post, b_mod=g_b_mod, mla_q_norm=g_q_norm,
                 mla_kv_norm=g_kv_norm, rel_bias=g_rel_bias)
    deltas, new_m, new_v = {}, {}, {}

    def adamw(names):
        for n in names:
            view = swap if n in ('ffn_w_gate', 'ffn_w_up') else (lambda t: t)
            outs = _adamw(view(given[n]), view(grads[n]), view(given["m_" + n]), view(given["v_" + n]),
                          name=f"adamw_{n}")
            deltas[n], new_m[n], new_v[n] = (view(t) for t in outs)

    st0, tok = to_sibling(ffn_units[0, 0], "0")
    grads['w_mod'] = _mm(silu_c, (dmod_cols + tok).astype(BF16), ta=True, tn_cap=768, name="w_mod_grad")
    adamw(['w_mod'])
    st0, tok = to_chips(from_sibling(st0, deltas['w_mod']), "0")
    grads['b_mod'] = grads['b_mod'] + tok
    fin = _pair_gather(reds1 + reds2 + reds3, name="rs_pair_gather")
    for n, t in zip(['dil_w_in', 'dil_w_o', 'mla_w_in', 'mla_w_q_up', 'mla_w_kv_up', 'mla_w_o'], fin[2:4] + fin[8:12]):
        grads[n] = t.reshape(given[n].shape)
    adamw(['b_mod', 'dil_w_in', 'dil_w_o', 'mla_w_in', 'mla_w_q_up', 'mla_w_kv_up', 'mla_w_o', 'norm_pre', 'norm_post',
           'mla_q_norm', 'mla_kv_norm', 'rel_bias'])
    reds0 = from_chips(st0, deltas['dil_w_in'])
    fin0 = _pair_gather(reds0, name="rs_pair_gather_last")
    ffn_fin = {(1, 1): fin[0:2], (1, 0): fin[4:6], (0, 1): fin[6:8], (0, 0): fin0}
    per_ffn = lambda pick: jnp.stack([jnp.stack([pick(*ffn_fin[i, h]) for h in range(2)]) for i in range(2)])
    grads.update(ffn_w_gate=swap(per_ffn(lambda gu, dn: gu[0])), ffn_w_up=swap(per_ffn(lambda gu, dn: gu[1])),
                 ffn_w_down=per_ffn(lambda gu, dn: jnp.concatenate([dn[0], dn[1]], axis=1)))
    adamw(['ffn_w_gate', 'ffn_w_up', 'ffn_w_down'])
    return (loss, grad_x, *[grads[n] for n in WEIGHTS], *[deltas[n] for n in WEIGHTS],
            *[new_m[n] for n in WEIGHTS], *[new_v[n] for n in WEIGHTS])
```

```python
import math

import jax
import jax.numpy as jnp
from jax import lax
from jax.experimental import pallas as pl
from jax.experimental.pallas import tpu as pltpu

F32 = jnp.float32
BF16 = jnp.bfloat16
MESH = pl.DeviceIdType.MESH

SEQ = 2048
D_MODEL = 1024
D_FF = 2816
N_SHARD = 4
F_SHARD = D_FF // N_SHARD
EPS = 1e-6
FFN_RES = 0.5
HEADS = 16
Q_LORA, KV_LORA, QK_NOPE, QK_ROPE, V_HEAD = 384, 256, 64, 32, 64
HALF_ROPE = QK_ROPE // 2
ROPE_THETA = 10000.0
DIL_GROUPS = ((128, 1), (512, 4), (2048, 16))
DIL_BLOCK = 128
N_BUCKETS = 32
MAX_DISTANCE = 2048
ADAM_LR, ADAM_B1, ADAM_B2, ADAM_EPS, ADAM_WD, ADAM_STEP = 0.001, 0.9, 0.999, 1e-08, 0.01, 10

VMEM_LIMIT = 48 * 1024 * 1024
SMALL_ROWS = 40

WEIGHTS = ['norm_pre', 'norm_post', 'w_mod', 'b_mod', 'ffn_w_gate', 'ffn_w_up', 'ffn_w_down', 'mla_w_in',
           'mla_q_norm', 'mla_w_q_up', 'mla_kv_norm', 'mla_w_kv_up', 'mla_w_o', 'dil_w_in', 'dil_w_o', 'rel_bias']


def _cparams(**kw):
    return pltpu.CompilerParams(vmem_limit_bytes=VMEM_LIMIT, **kw)


def _pick(n, cap, mult=128):
    if n <= cap:
        return n
    best = n
    for t in range(mult, cap + 1, mult):
        if n % t == 0:
            best = t
    return best


def _mm(a, b, *, name, ta=False, tb=False, reduce_g=False, bias=None, out_dtype=F32, tm_cap=1024, tn_cap=1024,
        g_n=None, b_sel=None, out_shape=None, out_sel=None, out_buf=None):
    a3 = a if a.ndim == 3 else a[None]
    ga = a3.shape[0]
    if b_sel is None:
        b_n = b if b.ndim == 3 else b[None]
        gb = b_n.shape[0]
        b_sel = (lambda g: (g,)) if gb > 1 else (lambda g: (0,))
        g_n = max(ga, gb)
    else:
        b_n = b
    k_dim, m_dim = (a3.shape[1], a3.shape[2]) if ta else (a3.shape[2], a3.shape[1])
    k2, n_dim = (b_n.shape[-1], b_n.shape[-2]) if tb else (b_n.shape[-2], b_n.shape[-1])
    assert k_dim == k2, (a.shape, b.shape)
    tm = _pick(m_dim, tm_cap, 128 if ta else 8)
    tn = _pick(n_dim, tn_cap, 128)
    mt, nt = m_dim // tm, n_dim // tn
    dims = (((0 if ta else 1,), (1 if tb else 0,)), ((), ()))

    if reduce_g:
        grid = (mt, nt, g_n)
        ids = lambda i, j, g: (g, i, j)
    else:
        grid = (g_n, mt, nt)
        ids = lambda g, i, j: (g, i, j)

    def a_map(*p):
        g, i, j = ids(*p)
        g = g if ga > 1 else 0
        return (g, 0, i) if ta else (g, i, 0)

    def b_map(*p):
        g, i, j = ids(*p)
        return (*b_sel(g), j, 0) if tb else (*b_sel(g), 0, j)

    b_lead = (None,) * (b_n.ndim - 2)
    a_spec = pl.BlockSpec((None, k_dim, tm) if ta else (None, tm, k_dim), a_map)
    b_spec = pl.BlockSpec(b_lead + ((tn, k_dim) if tb else (k_dim, tn)), b_map)
    in_specs = [a_spec, b_spec]
    operands = [a3, b_n]
    if bias is not None:
        assert not reduce_g and bias.shape == (g_n, 1, n_dim)
        in_specs.append(pl.BlockSpec((None, 1, tn), lambda g, i, j: (g, 0, j)))
        operands.append(bias)
    aliases = {}
    if out_buf is not None:
        assert tuple(out_buf.shape) == tuple(out_shape) and out_buf.dtype == out_dtype
        in_specs.append(pl.BlockSpec(memory_space=pl.ANY))
        operands.append(out_buf)
        aliases = {len(operands) - 1: 0}

    if reduce_g:
        out_spec = pl.BlockSpec((tm, tn), lambda i, j, g: (i, j))
        out_sds = jax.ShapeDtypeStruct((m_dim, n_dim), F32)
    elif out_shape is not None:
        def o_map(g, i, j):
            lead, rb, cb = out_sel(g, i, j)
            return (*lead, rb, cb)

        out_spec = pl.BlockSpec((None,) * (len(out_shape) - 2) + (tm, tn), o_map)
        out_sds = jax.ShapeDtypeStruct(tuple(out_shape), out_dtype)
    else:
        out_spec = pl.BlockSpec((None, tm, tn), lambda g, i, j: (g, i, j))
        out_sds = jax.ShapeDtypeStruct((g_n, m_dim, n_dim), out_dtype)

    def body(a_ref, b_ref, *rest):
        o_ref = rest[-1]
        r = lax.dot_general(a_ref[...].astype(BF16), b_ref[...].astype(BF16), dims, preferred_element_type=F32)
        if bias is not None:
            r = r + rest[0][...]
        if reduce_g:
            g = pl.program_id(2)

            @pl.when(g == 0)
            def _():
                o_ref[...] = r

            @pl.when(g > 0)
            def _():
                o_ref[...] += r
        else:
            o_ref[...] = r.astype(o_ref.dtype)

    out = pl.pallas_call(body, grid=grid, in_specs=in_specs, out_specs=out_spec, out_shape=out_sds,
                         input_output_aliases=aliases, compiler_params=_cparams(), name=name)(*operands)
    if not reduce_g and out_shape is None and a.ndim == 2 and b.ndim == 2:
        out = out[0]
    return out


def _rows(tm, w):
    return pl.BlockSpec((tm, w), lambda i: (i, 0))


def _vec(w):
    return pl.BlockSpec((1, w), lambda i: (0, 0))


def _rstd(v):
    return lax.rsqrt(jnp.mean(v * v, axis=-1, keepdims=True) + EPS)


V_PG, V_QG, V_SH, V_SC, V_GATE = range(5)


def _vrow(v_ref, k):
    return v_ref[k:k + 1, :]


def _vecs(w):
    return pl.BlockSpec((8, w), lambda *_: (0, 0))


def _pre_fwd(x, vp, *, name):
    s_n, w = x.shape
    tm = _pick(s_n, 512, 8)

    def body(x_ref, v_ref, o_ref):
        xv = x_ref[...]
        n = (xv * _rstd(xv)) * _vrow(v_ref, V_PG)
        o_ref[...] = (n * (1.0 + _vrow(v_ref, V_SC)) + _vrow(v_ref, V_SH)).astype(o_ref.dtype)

    return pl.pallas_call(body, grid=(s_n // tm,), in_specs=[_rows(tm, w), _vecs(w)],
                          out_specs=_rows(tm, w), out_shape=jax.ShapeDtypeStruct((s_n, w), BF16),
                          compiler_params=_cparams(), name=name)(x, vp)


def _post_fwd(f, x, vp, res_w, *, name):
    s_n, w = x.shape
    tm = _pick(s_n, 512, 8)

    def body(f_ref, x_ref, v_ref, o_ref):
        fv = f_ref[...]
        y = (fv * _rstd(fv)) * _vrow(v_ref, V_QG)
        o_ref[...] = x_ref[...] + (res_w * _vrow(v_ref, V_GATE)) * y

    return pl.pallas_call(body, grid=(s_n // tm,), in_specs=[_rows(tm, w), _rows(tm, w), _vecs(w)],
                          out_specs=_rows(tm, w), out_shape=jax.ShapeDtypeStruct((s_n, w), F32),
                          compiler_params=_cparams(), name=name)(f, x, vp)


def _post_bwd(dout, f, vp, res_w, *, name):
    s_n, w = f.shape
    tm = _pick(s_n, 512, 8)

    def body(do_ref, f_ref, v_ref, df_ref, dgate_ref, dqg_ref):
        @pl.when(pl.program_id(0) == 0)
        def _():
            dgate_ref[...] = jnp.zeros_like(dgate_ref)
            dqg_ref[...] = jnp.zeros_like(dqg_ref)

        do = do_ref[...]
        fv = f_ref[...]
        r = _rstd(fv)
        fh = fv * r
        qg_v = _vrow(v_ref, V_QG)
        dgate_ref[...] += res_w * jnp.sum(do * (fh * qg_v), axis=0, keepdims=True)
        dy = do * (res_w * _vrow(v_ref, V_GATE))
        dqg_ref[...] += jnp.sum(dy * fh, axis=0, keepdims=True)
        dfh = dy * qg_v
        df = r * (dfh - fh * jnp.mean(dfh * fh, axis=-1, keepdims=True))
        df_ref[...] = df.astype(df_ref.dtype)

    return pl.pallas_call(
        body, grid=(s_n // tm,), in_specs=[_rows(tm, w), _rows(tm, w), _vecs(w)],
        out_specs=[_rows(tm, w), _vec(w), _vec(w)],
        out_shape=[jax.ShapeDtypeStruct((s_n, w), BF16), jax.ShapeDtypeStruct((1, w), F32),
                   jax.ShapeDtypeStruct((1, w), F32)],
        compiler_params=_cparams(), name=name)(dout, f, vp)


def _pre_bwd(dhn, x, dout, vp, *, name):
    s_n, w = x.shape
    tm = _pick(s_n, 512, 8)

    def body(dhn_ref, x_ref, do_ref, v_ref, dx_ref, dsh_ref, dsc_ref, dpg_ref):
        @pl.when(pl.program_id(0) == 0)
        def _():
            dsh_ref[...] = jnp.zeros_like(dsh_ref)
            dsc_ref[...] = jnp.zeros_like(dsc_ref)
            dpg_ref[...] = jnp.zeros_like(dpg_ref)

        dhn_v = dhn_ref[...]
        xv = x_ref[...]
        r = _rstd(xv)
        xh = xv * r
        pg_v = _vrow(v_ref, V_PG)
        dsh_ref[...] += jnp.sum(dhn_v, axis=0, keepdims=True)
        dsc_ref[...] += jnp.sum(dhn_v * (xh * pg_v), axis=0, keepdims=True)
        dn = dhn_v * (1.0 + _vrow(v_ref, V_SC))
        dpg_ref[...] += jnp.sum(dn * xh, axis=0, keepdims=True)
        dxh = dn * pg_v
        dx_ref[...] = do_ref[...] + r * (dxh - xh * jnp.mean(dxh * xh, axis=-1, keepdims=True))

    vec = jax.ShapeDtypeStruct((1, w), F32)
    return pl.pallas_call(
        body, grid=(s_n // tm,), in_specs=[_rows(tm, w), _rows(tm, w), _rows(tm, w), _vecs(w)],
        out_specs=[_rows(tm, w), _vec(w), _vec(w), _vec(w)],
        out_shape=[jax.ShapeDtypeStruct((s_n, w), F32), vec, vec, vec],
        compiler_params=_cparams(), name=name)(dhn, x, dout, vp)


def _rms_fwd(x, g, *, name):
    s_n, w = x.shape
    tm = _pick(s_n, 512, 8)

    def body(x_ref, g_ref, o_ref):
        xv = x_ref[...]
        o_ref[...] = ((xv * _rstd(xv)) * g_ref[...]).astype(o_ref.dtype)

    return pl.pallas_call(body, grid=(s_n // tm,), in_specs=[_rows(tm, w), _vec(w)], out_specs=_rows(tm, w),
                          out_shape=jax.ShapeDtypeStruct((s_n, w), BF16), compiler_params=_cparams(),
                          name=name)(x, g)


def _rms_bwd(dy, x, g, *, name):
    s_n, w = x.shape
    tm = _pick(s_n, 512, 8)

    def body(dy_ref, x_ref, g_ref, dx_ref, dg_ref):
        @pl.when(pl.program_id(0) == 0)
        def _():
            dg_ref[...] = jnp.zeros_like(dg_ref)

        dy_v = dy_ref[...]
        xv = x_ref[...]
        r = _rstd(xv)
        xh = xv * r
        dg_ref[...] += jnp.sum(dy_v * xh, axis=0, keepdims=True)
        dxh = dy_v * g_ref[...]
        dx_ref[...] = r * (dxh - xh * jnp.mean(dxh * xh, axis=-1, keepdims=True))

    return pl.pallas_call(
        body, grid=(s_n // tm,), in_specs=[_rows(tm, w), _rows(tm, w), _vec(w)],
        out_specs=[_rows(tm, w), _vec(w)],
        out_shape=[jax.ShapeDtypeStruct((s_n, w), F32), jax.ShapeDtypeStruct((1, w), F32)],
        compiler_params=_cparams(), name=name)(dy, x, g)


def _rope(a1, a2, cos, sin, *, name):
    s_n, w = a1.shape
    tm = _pick(s_n, 512, 8)

    def body(a1_ref, a2_ref, c_ref, s_ref, r1_ref, r2_ref):
        u, v, c_v, s_v = a1_ref[...], a2_ref[...], c_ref[...], s_ref[...]
        r1_ref[...] = u * c_v - v * s_v
        r2_ref[...] = u * s_v + v * c_v

    sd = jax.ShapeDtypeStruct((s_n, w), F32)
    return pl.pallas_call(body, grid=(s_n // tm,), in_specs=[_rows(tm, w)] * 4, out_specs=[_rows(tm, w)] * 2,
                          out_shape=[sd, sd], compiler_params=_cparams(), name=name)(a1, a2, cos, sin)


def _silu_bf16(x, *, name):
    def body(x_ref, o_ref):
        xv = x_ref[...]
        o_ref[...] = (xv * jax.nn.sigmoid(xv)).astype(o_ref.dtype)

    return pl.pallas_call(body, out_shape=jax.ShapeDtypeStruct(x.shape, BF16), name=name)(x)


def _loss(y, target, *, name):
    s_n, w = y.shape
    tm = _pick(s_n, 512, 8)

    def body(y_ref, t_ref, dy_ref, l_ref):
        @pl.when(pl.program_id(0) == 0)
        def _():
            l_ref[...] = jnp.zeros_like(l_ref)

        e = y_ref[...] - t_ref[...]
        dy_ref[...] = e * (1.0 / w)
        row = jnp.mean(e * e, axis=-1, keepdims=True)
        l_ref[...] += 0.5 * jnp.sum(row, axis=0, keepdims=True)

    return pl.pallas_call(
        body, grid=(s_n // tm,), in_specs=[_rows(tm, w), _rows(tm, w)],
        out_specs=[_rows(tm, w), pl.BlockSpec((1, 1), lambda i: (0, 0))],
        out_shape=[jax.ShapeDtypeStruct((s_n, w), F32), jax.ShapeDtypeStruct((1, 1), F32)],
        compiler_params=_cparams(), name=name)(y, target)


FFN_TM = 512
FFN_TM_WIDE = 1024


def _ffn_up(hn, w_gu, *, name):
    s_n, d = hn.shape
    f = w_gu.shape[-1]
    tm = _pick(s_n, FFN_TM_WIDE, 8)

    def body(hn_ref, wg_ref, wu_ref, gu_ref, a_ref):
        xv = hn_ref[...]
        g = jnp.dot(xv, wg_ref[...], preferred_element_type=F32)
        u = jnp.dot(xv, wu_ref[...], preferred_element_type=F32)
        gu_ref[0] = g.astype(BF16)
        gu_ref[1] = u.astype(BF16)
        a_ref[...] = ((g * jax.nn.sigmoid(g)) * u).astype(BF16)

    w_blk = lambda t: pl.BlockSpec((None, None, d, f), lambda s, m: (s, t, 0, 0))
    return pl.pallas_call(
        body, grid=(N_SHARD, s_n // tm),
        in_specs=[pl.BlockSpec((tm, d), lambda s, m: (m, 0)), w_blk(0), w_blk(1)],
        out_specs=[pl.BlockSpec((None, 2, tm, f), lambda s, m: (s, 0, m, 0)),
                   pl.BlockSpec((None, tm, f), lambda s, m: (s, m, 0))],
        out_shape=[jax.ShapeDtypeStruct((N_SHARD, 2, s_n, f), BF16), jax.ShapeDtypeStruct((N_SHARD, s_n, f), BF16)],
        compiler_params=_cparams(), name=name)(hn, w_gu, w_gu)


def _ffn_down(a, w_dn, x, vp, res_w, *, name):
    _, s_n, f = a.shape
    d = w_dn.shape[-1]
    tm = _pick(s_n, FFN_TM, 8)
    a = a.reshape(-1, 2, s_n, f)
    w_dn = w_dn.reshape(-1, 2, f, d)
    g_n = a.shape[0]

    def body(a_ref, w_ref, x_ref, v_ref, f_ref, o_ref):
        g = pl.program_id(1)
        r = (jnp.dot(a_ref[0], w_ref[0], preferred_element_type=F32)
             + jnp.dot(a_ref[1], w_ref[1], preferred_element_type=F32))

        @pl.when(g == 0)
        def _():
            f_ref[...] = r

        @pl.when(g > 0)
        def _():
            f_ref[...] += r

        @pl.when(g == g_n - 1)
        def _():
            fv = f_ref[...]
            y = (fv * _rstd(fv)) * _vrow(v_ref, V_QG)
            o_ref[...] = x_ref[...] + (res_w * _vrow(v_ref, V_GATE)) * y

    row = pl.BlockSpec((tm, d), lambda m, g: (m, 0))
    sd = jax.ShapeDtypeStruct((s_n, d), F32)
    return pl.pallas_call(
        body, grid=(s_n // tm, g_n),
        in_specs=[pl.BlockSpec((None, 2, tm, f), lambda m, g: (g, 0, m, 0)),
                  pl.BlockSpec((None, 2, f, d), lambda m, g: (g, 0, 0, 0)), row, _vecs(d)],
        out_specs=[row, row], out_shape=[sd, sd], compiler_params=_cparams(), name=name)(a, w_dn, x, vp)


def _ffn_dhn(dgu, w_gu, x, dout, vp, *, name):
    g_n, _, s_n, f = dgu.shape
    d = w_gu.shape[-2]
    tm = _pick(s_n, FFN_TM, 8)
    nt_dims = (((1,), (1,)), ((), ()))

    def body(a_ref, w_ref, x_ref, do_ref, v_ref, dx_ref, dsh_ref, dsc_ref, dpg_ref, acc_ref):
        m, g = pl.program_id(0), pl.program_id(1)
        r = (lax.dot_general(a_ref[0], w_ref[0], nt_dims, preferred_element_type=F32)
             + lax.dot_general(a_ref[1], w_ref[1], nt_dims, preferred_element_type=F32))

        @pl.when(g == 0)
        def _():
            acc_ref[...] = r

        @pl.when(g > 0)
        def _():
            acc_ref[...] += r

        @pl.when((m == 0) & (g == 0))
        def _():
            dsh_ref[...] = jnp.zeros_like(dsh_ref)
            dsc_ref[...] = jnp.zeros_like(dsc_ref)
            dpg_ref[...] = jnp.zeros_like(dpg_ref)

        @pl.when(g == g_n - 1)
        def _():
            dhn_v = acc_ref[...]
            xv = x_ref[...]
            rs = _rstd(xv)
            xh = xv * rs
            pg_v = _vrow(v_ref, V_PG)
            dsh_ref[...] += jnp.sum(dhn_v, axis=0, keepdims=True)
            dsc_ref[...] += jnp.sum(dhn_v * (xh * pg_v), axis=0, keepdims=True)
            dn = dhn_v * (1.0 + _vrow(v_ref, V_SC))
            dpg_ref[...] += jnp.sum(dn * xh, axis=0, keepdims=True)
            dxh = dn * pg_v
            dx_ref[...] = do_ref[...] + rs * (dxh - xh * jnp.mean(dxh * xh, axis=-1, keepdims=True))

    row = pl.BlockSpec((tm, d), lambda m, g: (m, 0))
    vec = pl.BlockSpec((1, d), lambda m, g: (0, 0))
    vsd = jax.ShapeDtypeStruct((1, d), F32)
    return pl.pallas_call(
        body, grid=(s_n // tm, g_n),
        in_specs=[pl.BlockSpec((None, 2, tm, f), lambda m, g: (g, 0, m, 0)),
                  pl.BlockSpec((None, 2, d, f), lambda m, g: (g, 0, 0, 0)), row, row, _vecs(d)],
        out_specs=[row, vec, vec, vec], out_shape=[jax.ShapeDtypeStruct((s_n, d), F32), vsd, vsd, vsd],
        scratch_shapes=[pltpu.VMEM((tm, d), F32)], compiler_params=_cparams(), name=name)(dgu, w_gu, x, dout, vp)


def _ffn_dgu(df, w_dn, gu, *, name):
    s_n, d = df.shape
    f = w_dn.shape[-2]
    tm = _pick(s_n, FFN_TM_WIDE, 8)

    def body(df_ref, wd_ref, gu_ref, o_ref):
        da = lax.dot_general(df_ref[...], wd_ref[...], (((1,), (1,)), ((), ())), preferred_element_type=F32)
        g = gu_ref[0].astype(F32)
        u = gu_ref[1].astype(F32)
        sig = jax.nn.sigmoid(g)
        o_ref[0] = (da * u * (sig * (1.0 + g * (1.0 - sig)))).astype(BF16)
        o_ref[1] = (da * (g * sig)).astype(BF16)

    gu_blk = pl.BlockSpec((None, 2, tm, f), lambda s, m: (s, 0, m, 0))
    return pl.pallas_call(
        body, grid=(N_SHARD, s_n // tm),
        in_specs=[pl.BlockSpec((tm, d), lambda s, m: (m, 0)),
                  pl.BlockSpec((None, f, d), lambda s, m: (s, 0, 0)), gu_blk],
        out_specs=gu_blk, out_shape=jax.ShapeDtypeStruct((N_SHARD, 2, s_n, f), BF16),
        compiler_params=_cparams(), name=name)(df, w_dn, gu)


_NT = (((1,), (1,)), ((), ()))
_TN = (((0,), (0,)), ((), ()))
MLA_TQ = 512


def _lower_triangle(tq):
    return lax.broadcasted_iota(jnp.int32, (tq, tq), 1) <= lax.broadcasted_iota(jnp.int32, (tq, tq), 0)


def _mla_attn_fwd(q, k, v, *, name):
    h_n, s_n, dq = q.shape
    dv = v.shape[-1]
    tq = MLA_TQ
    scale = float(dq) ** -0.5

    def body(q_ref, k_ref, v_ref, o_ref, lse_ref):
        i = pl.program_id(1)
        for e in range(1, s_n // tq + 1):
            @pl.when(i == e - 1)
            def _(lo=(e - 1) * tq, ext=e * tq):
                qv = (q_ref[...] * scale).astype(BF16)
                s2 = lax.dot_general(qv, k_ref[lo:ext, :], _NT, preferred_element_type=F32)
                s2 = jnp.where(_lower_triangle(tq), s2, -jnp.inf)
                m = jnp.max(s2, axis=-1, keepdims=True)
                if lo:
                    s1 = lax.dot_general(qv, k_ref[0:lo, :], _NT, preferred_element_type=F32)
                    m = jnp.maximum(m, jnp.max(s1, axis=-1, keepdims=True))
                p2 = jnp.exp(s2 - m)
                l = jnp.sum(p2, axis=-1, keepdims=True)
                o = jnp.dot(p2.astype(BF16), v_ref[lo:ext, :], preferred_element_type=F32)
                if lo:
                    p1 = jnp.exp(s1 - m)
                    l = l + jnp.sum(p1, axis=-1, keepdims=True)
                    o = o + jnp.dot(p1.astype(BF16), v_ref[0:lo, :], preferred_element_type=F32)
                o_ref[...] = o / l
                lse_ref[...] = m + jnp.log(l)

    return pl.pallas_call(
        body, grid=(h_n, s_n // tq),
        in_specs=[pl.BlockSpec((None, tq, dq), lambda h, i: (h, i, 0)),
                  pl.BlockSpec((None, s_n, dq), lambda h, i: (h, 0, 0)),
                  pl.BlockSpec((None, s_n, dv), lambda h, i: (h, 0, 0))],
        out_specs=[pl.BlockSpec((None, tq, dv), lambda h, i: (h, i, 0)),
                   pl.BlockSpec((None, tq, 1), lambda h, i: (h, i, 0))],
        out_shape=[jax.ShapeDtypeStruct((h_n, s_n, dv), F32), jax.ShapeDtypeStruct((h_n, s_n, 1), F32)],
        compiler_params=_cparams(), name=name)(q, k, v)


def _mla_attn_bwd(q, k, v, o, do, lse, *, name):
    h_n, s_n, dq = q.shape
    dv = v.shape[-1]
    tq = MLA_TQ
    scale = float(dq) ** -0.5

    def body(q_ref, k_ref, v_ref, o_ref, do_ref, lse_ref, dq_ref, dk_ref, dv_ref):
        i = pl.program_id(1)

        @pl.when(i == 0)
        def _():
            dk_ref[...] = jnp.zeros_like(dk_ref)
            dv_ref[...] = jnp.zeros_like(dv_ref)

        for e in range(1, s_n // tq + 1):
            @pl.when(i == e - 1)
            def _(lo=(e - 1) * tq, ext=e * tq):
                qv = (q_ref[...] * scale).astype(BF16)
                do_v = do_ref[...]
                dob = do_v.astype(BF16)
                lse_v = lse_ref[...]
                delta = jnp.sum(do_v * o_ref[...], axis=-1, keepdims=True)
                dq_acc = None
                for k_lo, k_hi, masked in ((lo, ext, True),) + (((0, lo, False),) if lo else ()):
                    kv, vv = k_ref[k_lo:k_hi, :], v_ref[k_lo:k_hi, :]
                    p = jnp.exp(lax.dot_general(qv, kv, _NT, preferred_element_type=F32) - lse_v)
                    if masked:
                        p = jnp.where(_lower_triangle(tq), p, 0.0)
                    dv_ref[k_lo:k_hi, :] += lax.dot_general(p.astype(BF16), dob, _TN, preferred_element_type=F32)
                    dp = lax.dot_general(dob, vv, _NT, preferred_element_type=F32)
                    dsb = (p * (dp - delta)).astype(BF16)
                    part = jnp.dot(dsb, kv, preferred_element_type=F32)
                    dq_acc = part if dq_acc is None else dq_acc + part
                    dk_ref[k_lo:k_hi, :] += lax.dot_general(dsb, qv, _TN, preferred_element_type=F32)
                dq_ref[...] = dq_acc * scale

    return pl.pallas_call(
        body, grid=(h_n, s_n // tq),
        in_specs=[pl.BlockSpec((None, tq, dq), lambda h, i: (h, i, 0)),
                  pl.BlockSpec((None, s_n, dq), lambda h, i: (h, 0, 0)),
                  pl.BlockSpec((None, s_n, dv), lambda h, i: (h, 0, 0)),
                  pl.BlockSpec((None, tq, dv), lambda h, i: (h, i, 0)),
                  pl.BlockSpec((None, tq, dv), lambda h, i: (h, i, 0)),
                  pl.BlockSpec((None, tq, 1), lambda h, i: (h, i, 0))],
        out_specs=[pl.BlockSpec((None, tq, dq), lambda h, i: (h, i, 0)),
                   pl.BlockSpec((None, s_n, dq), lambda h, i: (h, 0, 0)),
                   pl.BlockSpec((None, s_n, dv), lambda h, i: (h, 0, 0))],
        out_shape=[jax.ShapeDtypeStruct((h_n, s_n, dq), F32), jax.ShapeDtypeStruct((h_n, s_n, dq), F32),
                   jax.ShapeDtypeStruct((h_n, s_n, dv), F32)],
        compiler_params=_cparams(), name=name)(q, k, v, o, do, lse)


def _head_sum(x, *, name):
    h_n, s_n, w = x.shape
    tm = _pick(s_n, 512, 8)

    def body(x_ref, o_ref):
        o_ref[...] = jnp.sum(x_ref[...], axis=0)

    return pl.pallas_call(body, grid=(s_n // tm,), in_specs=[pl.BlockSpec((h_n, tm, w), lambda i: (0, i, 0))],
                          out_specs=_rows(tm, w), out_shape=jax.ShapeDtypeStruct((s_n, w), F32),
                          compiler_params=_cparams(), name=name)(x)


N_BLK = SEQ // DIL_BLOCK
DIL_SCALE = 64 ** -0.5


def _dil_masks():
    iq = lax.broadcasted_iota(jnp.int32, (DIL_BLOCK, 2 * DIL_BLOCK), 0)
    ik = lax.broadcasted_iota(jnp.int32, (DIL_BLOCK, 2 * DIL_BLOCK), 1)
    rel = DIL_BLOCK + iq - ik
    both = (rel >= 0) & (rel <= DIL_BLOCK)
    iq1 = lax.broadcasted_iota(jnp.int32, (DIL_BLOCK, DIL_BLOCK), 0)
    ik1 = lax.broadcasted_iota(jnp.int32, (DIL_BLOCK, DIL_BLOCK), 1)
    return both, ik1 <= iq1


def _dil_block(j, d):
    nb = SEQ // d // DIL_BLOCK
    r, n = divmod(j, nb)
    first = n == 0
    rows = lambda start, size: pl.ds(start, size) if d == 1 else pl.ds(start, size, stride=d)
    q_rows = rows(n * DIL_BLOCK * d + r, DIL_BLOCK)
    k_rows = q_rows if first else rows((n - 1) * DIL_BLOCK * d + r, 2 * DIL_BLOCK)
    return q_rows, k_rows, (DIL_BLOCK if first else 0), first


PAIR = 2 * 64
N_PAIR = HEADS // 2


def _dil_head_specs(s_n, g):
    return [pl.BlockSpec((None, s_n, PAIR), lambda hp, t=t: ((g * 3 + t) * N_PAIR + hp, 0, 0)) for t in range(3)]


def _pair_specs(s_n, w):
    return pl.BlockSpec((2, s_n, w), lambda hp: (hp, 0, 0))


_PAIR_BIAS = pl.BlockSpec((2, DIL_BLOCK, 2 * DIL_BLOCK), lambda hp: (hp, 0, 0))


def _dil_attn_fwd(heads, bias, g, d, *, name):
    _, s_n, _ = heads.shape
    e = PAIR // 2

    def body(q_ref, k_ref, v_ref, b_ref, o_ref, lse_ref):
        m_both, m_first = _dil_masks()
        for j in range(N_BLK):
            q_rows, k_rows, b_lo, first = _dil_block(j, d)
            q2 = q_ref[q_rows, :].astype(BF16)
            k2 = k_ref[k_rows, :].astype(BF16)
            v2 = v_ref[k_rows, :].astype(BF16)
            for hh in range(2):
                cols = slice(hh * e, (hh + 1) * e)
                s = (lax.dot_general(q2[:, cols], k2[:, cols], _NT, preferred_element_type=F32) * DIL_SCALE
                     + b_ref[hh, :, b_lo:])
                s = jnp.where(m_first if first else m_both, s, -jnp.inf)
                m = jnp.max(s, axis=-1, keepdims=True)
                lse = m + jnp.log(jnp.sum(jnp.exp(s - m), axis=-1, keepdims=True))
                p = jnp.exp(s - lse)
                o_ref[hh, q_rows, :] = jnp.dot(p.astype(BF16), v2[:, cols], preferred_element_type=F32)
                lse_ref[hh, q_rows, :] = lse

    return pl.pallas_call(
        body, grid=(N_PAIR,), in_specs=_dil_head_specs(s_n, g) + [_PAIR_BIAS],
        out_specs=[_pair_specs(s_n, e), _pair_specs(s_n, 1)],
        out_shape=[jax.ShapeDtypeStruct((HEADS, s_n, e), F32), jax.ShapeDtypeStruct((HEADS, s_n, 1), F32)],
        compiler_params=_cparams(), name=name)(heads, heads, heads, bias)


def _dil_attn_bwd(heads, bias, lse, do, dlt, g, d, *, name):
    _, s_n, _ = heads.shape
    e = PAIR // 2

    def body(q_ref, k_ref, v_ref, b_ref, lse_ref, do_ref, dlt_ref, dq_ref, dk_ref, dv_ref, db_ref):
        db_ref[...] = jnp.zeros_like(db_ref)
        m_both, m_first = _dil_masks()
        nb = s_n // d // DIL_BLOCK
        own_v = own_k = own_rows = None
        for j in range(N_BLK):
            q_rows, k_rows, b_lo, first = _dil_block(j, d)
            q2 = q_ref[q_rows, :].astype(BF16)
            k2 = k_ref[k_rows, :].astype(BF16)
            v2 = v_ref[k_rows, :].astype(BF16)
            dq_h, dv_h, dk_h = [], [], []
            for hh in range(2):
                cols = slice(hh * e, (hh + 1) * e)
                qj, kk, vv = q2[:, cols], k2[:, cols], v2[:, cols]
                s = lax.dot_general(qj, kk, _NT, preferred_element_type=F32) * DIL_SCALE + b_ref[hh, :, b_lo:]
                p = jnp.where(m_first if first else m_both, jnp.exp(s - lse_ref[hh, q_rows, :]), 0.0)
                dob = do_ref[hh, q_rows, :].astype(BF16)
                dv_h.append(lax.dot_general(p.astype(BF16), dob, _TN, preferred_element_type=F32))
                dp = lax.dot_general(dob, vv, _NT, preferred_element_type=F32)
                ds = p * (dp - dlt_ref[hh, q_rows, :])
                db_ref[hh, :, b_lo:] += ds
                dsb = (ds * DIL_SCALE).astype(BF16)
                dq_h.append(jnp.dot(dsb, kk, preferred_element_type=F32))
                dk_h.append(lax.dot_general(dsb, qj, _TN, preferred_element_type=F32))
            dq_ref[q_rows, :] = jnp.concatenate(dq_h, axis=1)
            dvv, dkk = jnp.concatenate(dv_h, axis=1), jnp.concatenate(dk_h, axis=1)
            if not first:
                dv_ref[own_rows, :] = own_v + dvv[:DIL_BLOCK]
                dk_ref[own_rows, :] = own_k + dkk[:DIL_BLOCK]
                dvv, dkk = dvv[DIL_BLOCK:], dkk[DIL_BLOCK:]
            own_v, own_k, own_rows = dvv, dkk, q_rows
            if j % nb == nb - 1:
                dv_ref[own_rows, :] = own_v
                dk_ref[own_rows, :] = own_k

    slab = pl.BlockSpec((None, s_n, PAIR), lambda hp: (hp, 0, 0))
    sd = jax.ShapeDtypeStruct((N_PAIR, s_n, PAIR), F32)
    return pl.pallas_call(
        body, grid=(N_PAIR,),
        in_specs=_dil_head_specs(s_n, g) + [_PAIR_BIAS, _pair_specs(s_n, 1), _pair_specs(s_n, e), _pair_specs(s_n, 1)],
        out_specs=[slab, slab, slab, _PAIR_BIAS],
        out_shape=[sd, sd, sd, jax.ShapeDtypeStruct((HEADS, DIL_BLOCK, 2 * DIL_BLOCK), F32)],
        compiler_params=_cparams(), name=name)(heads, heads, heads, bias, lse, do, dlt)


def _proj_heads(x, w, *, name):
    s_n, k = x.shape
    n = w.shape[-1]
    tm, tn, e = 1024, 768, PAIR
    per_blk, n_blk = tn // e, n // tn

    def body(x_ref, w_ref, o_ref):
        r = jnp.dot(x_ref[...], w_ref[...], preferred_element_type=F32)
        for j in range(per_blk):
            o_ref[j] = r[:, e * j:e * (j + 1)]

    return pl.pallas_call(
        body, grid=(w.shape[0], n_blk, s_n // tm),
        in_specs=[pl.BlockSpec((tm, k), lambda s, b, m: (m, 0)), pl.BlockSpec((None, k, tn), lambda s, b, m: (s, 0, b))],
        out_specs=pl.BlockSpec((per_blk, tm, e), lambda s, b, m: (s * n_blk + b, m, 0)),
        out_shape=jax.ShapeDtypeStruct((w.shape[0] * n // e, s_n, e), F32), compiler_params=_cparams(),
        name=name)(x, w)


def _heads_cat(d_ref):
    return jnp.concatenate([d_ref[j] for j in range(d_ref.shape[0])], axis=1)


def _proj_heads_dw(x, dh, *, name):
    s_n, k = x.shape
    tn, e = 768, PAIR
    per_blk = tn // e
    n_blk = dh.shape[0] // N_SHARD // per_blk
    n = n_blk * tn

    def body(x_ref, d_ref, o_ref):
        o_ref[...] = lax.dot_general(x_ref[...], _heads_cat(d_ref), _TN, preferred_element_type=F32)

    return pl.pallas_call(
        body, grid=(N_SHARD, n_blk, 2),
        in_specs=[pl.BlockSpec((s_n, k // 2), lambda s, b, r: (0, r)),
                  pl.BlockSpec((per_blk, s_n, e), lambda s, b, r: (s * n_blk + b, 0, 0))],
        out_specs=pl.BlockSpec((None, None, k // 2, tn), lambda s, b, r: (r, s, 0, b)),
        out_shape=jax.ShapeDtypeStruct((2, N_SHARD, k // 2, n), F32), compiler_params=_cparams(), name=name)(x, dh)


def _proj_heads_dx(dh, w, *, name):
    k, n = w.shape[1:]
    s_n = dh.shape[1]
    tm, tn, e = 512, 1152, PAIR
    per_blk, n_blk = tn // e, n // tn

    def body(d_ref, w_ref, o_ref):
        r = lax.dot_general(_heads_cat(d_ref), w_ref[...], _NT, preferred_element_type=F32)
        g = pl.program_id(1)

        @pl.when(g == 0)
        def _():
            o_ref[...] = r

        @pl.when(g > 0)
        def _():
            o_ref[...] += r

    return pl.pallas_call(
        body, grid=(s_n // tm, N_SHARD * n_blk),
        in_specs=[pl.BlockSpec((per_blk, tm, e), lambda m, g: (g, m, 0)),
                  pl.BlockSpec((None, k, tn), lambda m, g: (g // n_blk, 0, g % n_blk))],
        out_specs=pl.BlockSpec((tm, k), lambda m, g: (m, 0)),
        out_shape=jax.ShapeDtypeStruct((s_n, k), F32), compiler_params=_cparams(), name=name)(dh, w)


def _group_alpha(ls):
    m = jnp.maximum(jnp.maximum(ls[0], ls[1]), ls[2])
    es = [jnp.exp(l - m) for l in ls]
    tot = es[0] + es[1] + es[2]
    return [ex / tot for ex in es]


def _dil_mix_fwd(os_, ls_, *, name):
    h_n, s_n, e = os_[0].shape
    tm = 1024

    def body(o0, o1, o2, l0, l1, l2, out_ref):
        for hh in range(2):
            al = _group_alpha([l[hh] for l in (l0, l1, l2)])
            mix = al[0] * o0[hh] + al[1] * o1[hh] + al[2] * o2[hh]
            out_ref[:, hh * e:(hh + 1) * e] = mix.astype(out_ref.dtype)

    blk = lambda w: pl.BlockSpec((2, tm, w), lambda h, i: (h, i, 0))
    return pl.pallas_call(body, grid=(h_n // 2, s_n // tm), in_specs=[blk(e)] * 3 + [blk(1)] * 3,
                          out_specs=pl.BlockSpec((tm, 2 * e), lambda h, i: (i, h)),
                          out_shape=jax.ShapeDtypeStruct((s_n, h_n * e), BF16), compiler_params=_cparams(),
                          name=name)(*os_, *ls_)


def _dil_mix_bwd(do_flat, os_, ls_, *, name):
    h_n, s_n, e = os_[0].shape
    tm = 1024

    def body(do_ref, o0, o1, o2, l0, l1, l2, d0, d1, d2, t0, t1, t2):
        for hh in range(2):
            al = _group_alpha([l[hh] for l in (l0, l1, l2)])
            do_v = do_ref[:, hh * e:(hh + 1) * e]
            mix = al[0] * o0[hh] + al[1] * o1[hh] + al[2] * o2[hh]
            dbar = jnp.sum(do_v * mix, axis=-1, keepdims=True)
            for a_g, d_ref, t_ref in zip(al, (d0, d1, d2), (t0, t1, t2)):
                d_ref[hh] = a_g * do_v
                t_ref[hh] = a_g * dbar

    blk = lambda w: pl.BlockSpec((2, tm, w), lambda h, i: (h, i, 0))
    sd_e = jax.ShapeDtypeStruct((h_n, s_n, e), F32)
    sd_1 = jax.ShapeDtypeStruct((h_n, s_n, 1), F32)
    outs = pl.pallas_call(body, grid=(h_n // 2, s_n // tm),
                          in_specs=[pl.BlockSpec((tm, 2 * e), lambda h, i: (i, h))] + [blk(e)] * 3 + [blk(1)] * 3,
                          out_specs=[blk(e)] * 3 + [blk(1)] * 3, out_shape=[sd_e] * 3 + [sd_1] * 3,
                          compiler_params=_cparams(), name=name)(do_flat, *os_, *ls_)
    return outs[:3], outs[3:]


def _bias_grad(ds, bucket, *, name):
    h_n = ds.shape[0]

    def body(ds_ref, bk_ref, o_ref):
        ds_v = ds_ref[...]
        bk = bk_ref[...]
        lane = lax.broadcasted_iota(jnp.int32, (1, N_BUCKETS), 1)
        acc = jnp.zeros((1, N_BUCKETS), F32)
        for b in range(N_BUCKETS):
            tot = jnp.sum(jnp.sum(jnp.where(bk == b, ds_v, 0.0), axis=1, keepdims=True), axis=0, keepdims=True)
            acc = acc + jnp.where(lane == b, tot, 0.0)
        o_ref[...] = acc

    return pl.pallas_call(
        body, grid=(h_n,),
        in_specs=[pl.BlockSpec((None, DIL_BLOCK, 2 * DIL_BLOCK), lambda h: (h, 0, 0)),
                  pl.BlockSpec((DIL_BLOCK, 2 * DIL_BLOCK), lambda h: (0, 0))],
        out_specs=pl.BlockSpec((None, 1, N_BUCKETS), lambda h: (h, 0, 0)),
        out_shape=jax.ShapeDtypeStruct((h_n, 1, N_BUCKETS), F32), compiler_params=_cparams(), name=name)(ds, bucket)


def _bias_table(rb, bucket, *, name):
    h_n = rb.shape[0]

    def body(rb_ref, bk_ref, o_ref):
        bk = bk_ref[...]
        row = rb_ref[...]
        acc = jnp.zeros(bk.shape, F32)
        for b in range(N_BUCKETS):
            acc = jnp.where(bk == b, row[:, b:b + 1], acc)
        o_ref[...] = acc

    return pl.pallas_call(
        body, grid=(h_n,),
        in_specs=[pl.BlockSpec((None, 1, N_BUCKETS), lambda h: (h, 0, 0)),
                  pl.BlockSpec((DIL_BLOCK, 2 * DIL_BLOCK), lambda h: (0, 0))],
        out_specs=pl.BlockSpec((None, DIL_BLOCK, 2 * DIL_BLOCK), lambda h: (h, 0, 0)),
        out_shape=jax.ShapeDtypeStruct((h_n, DIL_BLOCK, 2 * DIL_BLOCK), F32), compiler_params=_cparams(),
        name=name)(rb, bucket)


def _row_tile(rows, cols, budget=2 << 20):
    if rows * cols * 4 <= budget or rows % 8:
        return rows
    best = 8
    for t in range(8, rows + 1, 8):
        if rows % t == 0 and t * cols * 4 <= budget:
            best = t
    return best


def _adamw(w, g, m, v, *, name):
    shape = w.shape
    cols = shape[-1]
    rows = math.prod(shape[:-1]) if len(shape) > 1 else 1
    to2 = lambda t: t.reshape(rows, cols)
    tr = _row_tile(rows, cols)
    c1 = 1.0 / (1.0 - ADAM_B1 ** ADAM_STEP)
    c2 = 1.0 / (1.0 - ADAM_B2 ** ADAM_STEP)

    def body(w_ref, g_ref, m_ref, v_ref, d_ref, nm_ref, nv_ref):
        g_v = g_ref[...]
        nm = ADAM_B1 * m_ref[...] + (1.0 - ADAM_B1) * g_v
        nv = ADAM_B2 * v_ref[...] + (1.0 - ADAM_B2) * (g_v * g_v)
        m_hat = nm * c1
        v_hat = nv * c2
        d_ref[...] = -ADAM_LR * (m_hat / (jnp.sqrt(v_hat) + ADAM_EPS) + ADAM_WD * w_ref[...])
        nm_ref[...] = nm
        nv_ref[...] = nv

    blk = pl.BlockSpec((tr, cols), lambda i: (i, 0))
    sd = jax.ShapeDtypeStruct((rows, cols), F32)
    outs = pl.pallas_call(body, grid=(rows // tr,), in_specs=[blk] * 4, out_specs=[blk] * 3, out_shape=[sd] * 3,
                          compiler_params=_cparams(), name=name)(to2(w), to2(g), to2(m), to2(v))
    return tuple(t.reshape(shape) for t in outs)


def _add_half(unit, got, half_idx, *, name):
    rest = unit.shape[2:]
    c = rest[-1]
    r = math.prod(rest[:-1])
    tr = _row_tile(r, c, budget=4 << 20)

    def body(idx_ref, u_ref, g_ref, o_ref, w_ref):
        tot = u_ref[...] + g_ref[...].astype(F32)
        o_ref[...] = tot
        w_ref[...] = tot.astype(BF16)

    blk = pl.BlockSpec((None, tr, c), lambda s, i, idx: (s, i, 0))
    grid_spec = pltpu.PrefetchScalarGridSpec(
        num_scalar_prefetch=1, grid=(N_SHARD, r // tr),
        in_specs=[pl.BlockSpec((None, None, tr, c), lambda s, i, idx: (idx[0], s, i, 0)), blk],
        out_specs=[blk, blk])
    out, wire = pl.pallas_call(
        body, grid_spec=grid_spec,
        out_shape=[jax.ShapeDtypeStruct((N_SHARD, r, c), F32), jax.ShapeDtypeStruct((N_SHARD, r, c), BF16)],
        compiler_params=_cparams(), name=name)(half_idx, unit.reshape(2, N_SHARD, r, c), got.reshape(N_SHARD, r, c))
    return out.reshape((N_SHARD,) + rest), wire.reshape((N_SHARD,) + rest)


def _add_shards(part, got, shard_idx, *, name):
    rest = part.shape[1:]
    c = rest[-1]
    r = math.prod(rest[:-1])
    tr = _row_tile(r, c, budget=4 << 20)

    def body(idx_ref, p_ref, g_ref, o_ref):
        acc = p_ref[...]
        for k in range(3):
            acc = acc + g_ref[k].astype(F32)
        o_ref[...] = acc

    grid_spec = pltpu.PrefetchScalarGridSpec(
        num_scalar_prefetch=1, grid=(r // tr,),
        in_specs=[pl.BlockSpec((None, tr, c), lambda i, idx: (idx[0], i, 0)),
                  pl.BlockSpec((3, tr, c), lambda i, idx: (0, i, 0))],
        out_specs=pl.BlockSpec((tr, c), lambda i, idx: (i, 0)))
    out = pl.pallas_call(body, grid_spec=grid_spec, out_shape=jax.ShapeDtypeStruct((r, c), F32),
                         compiler_params=_cparams(), name=name)(
        shard_idx, part.reshape(N_SHARD, r, c), got.reshape(3, r, c))
    return out.reshape(rest)


def _sum_devices(x, n_dev, *, name):
    rows = x.shape[0] // n_dev

    def body(x_ref, o_ref):
        acc = x_ref[0:rows, :]
        for d in range(1, n_dev):
            acc = acc + x_ref[d * rows:(d + 1) * rows, :]
        o_ref[...] = acc

    return pl.pallas_call(body, out_shape=jax.ShapeDtypeStruct((rows, x.shape[1]), F32), name=name)(x)


def _my_pos():
    return lax.axis_index("x"), lax.axis_index("y"), lax.axis_index("c")


def _all_gather(x_blk, *, name, in_vmem):
    m_per, n = x_blk.shape

    def body(x_ref, out_ref, send_sems, recv_sems, local_sem):
        x, y, c = _my_pos()
        me, sibling = (x, y, c), (x, y, 1 - c)
        chips = [(1 - x, y), (x, 1 - y), (1 - x, 1 - y)]

        def rows(px, py, pc):
            return out_ref.at[pl.ds((4 * px + 2 * py + pc) * m_per, m_per), :]

        def copy(k, block, to, src=None):
            return pltpu.make_async_remote_copy(
                src_ref=rows(*block) if src is None else src, dst_ref=rows(*block),
                send_sem=send_sems.at[k], recv_sem=recv_sems.at[k], device_id=to, device_id_type=MESH)

        mine = pltpu.make_async_copy(x_ref, rows(*me), local_sem)
        mine.start()
        first = [copy(0, me, sibling, src=x_ref)]
        first += [copy(1 + j, me, (*chip, c), src=x_ref) for j, chip in enumerate(chips)]
        for cp in first:
            cp.start()
        passed = [copy(4 + j, (*chip, c), sibling) for j, chip in enumerate(chips)]
        for j, chip in enumerate(chips):
            copy(1 + j, (*chip, c), me).wait_recv()
            passed[j].start()
        copy(0, sibling, me).wait_recv()
        for j, chip in enumerate(chips):
            copy(4 + j, (*chip, 1 - c), me).wait_recv()
        for cp in first + passed:
            cp.wait_send()
        mine.wait()

    space = pltpu.VMEM if in_vmem else pl.ANY
    return pl.pallas_call(
        body, out_shape=jax.ShapeDtypeStruct((8 * m_per, n), x_blk.dtype),
        in_specs=[pl.BlockSpec(memory_space=space)], out_specs=pl.BlockSpec(memory_space=space),
        scratch_shapes=[pltpu.SemaphoreType.DMA((7,)), pltpu.SemaphoreType.DMA((7,)), pltpu.SemaphoreType.DMA],
        name=name)(x_blk)


_HBM = pl.BlockSpec(memory_space=pl.ANY)


def _gather_weights(fams, *, name):
    n = len(fams)

    def body(*refs):
        ins, outs = refs[:n], refs[n:2 * n]
        send_sems, recv_sems = refs[2 * n:]
        x, y, c = _my_pos()
        me, sibling = (x, y, c), (x, y, 1 - c)
        chips = [(1 - x, y), (x, 1 - y), (1 - x, 1 - y)]

        def copy(f, k, block, to, src=None):
            px, py, pc = block
            dst = outs[f].at[2 * px + py, pc]
            return pltpu.make_async_remote_copy(
                src_ref=dst if src is None else src, dst_ref=dst, send_sem=send_sems.at[7 * f + k],
                recv_sem=recv_sems.at[7 * f + k], device_id=to, device_id_type=MESH)

        first, passed = [], []
        for f in range(n):
            src = ins[f].at[c]
            first.append(copy(f, 0, me, sibling, src=src))
            first += [copy(f, 1 + j, me, (*chip, c), src=src) for j, chip in enumerate(chips)]
        for cp in first:
            cp.start()
        for j, chip in enumerate(chips):
            for f in range(n):
                copy(f, 1 + j, (*chip, c), me).wait_recv()
                passed.append(copy(f, 4 + j, (*chip, c), sibling))
                passed[-1].start()
        for f in range(n):
            copy(f, 0, sibling, me).wait_recv()
        for j, chip in enumerate(chips):
            for f in range(n):
                copy(f, 4 + j, (*chip, 1 - c), me).wait_recv()
        for cp in first + passed:
            cp.wait_send()

    outs = pl.pallas_call(
        body, out_shape=[jax.ShapeDtypeStruct((N_SHARD,) + t.shape, t.dtype) for t in fams],
        in_specs=[_HBM] * n, out_specs=[_HBM] * n,
        scratch_shapes=[pltpu.SemaphoreType.DMA((7 * n,)), pltpu.SemaphoreType.DMA((7 * n,))], name=name)(*fams)
    return [_place_own(o, t) for o, t in zip(outs, fams)]


def _pair_gather(halves, *, name):
    n = len(halves)

    def body(*refs):
        ins, outs = refs[:n], refs[n:2 * n]
        send_sems, recv_sems = refs[2 * n:]
        x, y, c = _my_pos()
        cps = [pltpu.make_async_remote_copy(src_ref=ins[f], dst_ref=outs[f].at[c], send_sem=send_sems.at[f],
                                            recv_sem=recv_sems.at[f], device_id=(x, y, 1 - c), device_id_type=MESH)
               for f in range(n)]
        for cp in cps:
            cp.start()
        for f in range(n):
            pltpu.make_async_remote_copy(src_ref=ins[f], dst_ref=outs[f].at[1 - c], send_sem=send_sems.at[f],
                                         recv_sem=recv_sems.at[f], device_id=(x, y, 1 - c),
                                         device_id_type=MESH).wait_recv()
        for cp in cps:
            cp.wait_send()

    outs = pl.pallas_call(
        body, out_shape=[jax.ShapeDtypeStruct((2,) + t.shape, t.dtype) for t in halves],
        in_specs=[_HBM] * n, out_specs=[_HBM] * n,
        scratch_shapes=[pltpu.SemaphoreType.DMA((n,)), pltpu.SemaphoreType.DMA((n,))], name=name)(*halves)
    c = lax.axis_index("c")
    return [lax.dynamic_update_index_in_dim(o, t, c, 0) for o, t in zip(outs, halves)]


_HBM_ONLY = pl.BlockSpec(memory_space=pltpu.HBM)
_SEMS = pl.BlockSpec(memory_space=pltpu.SEMAPHORE)
_EFFECT = pltpu.SideEffectType.DATAFLOW_SIDE_EFFECTING


def _copies_start(srcs, lands, plan, n_copies, *, name):
    n, m = len(srcs), len(lands)

    def body(*refs):
        src_refs, land_refs = refs[:n], refs[n:n + m]
        send_sems, recv_sems, token = refs[n + m], refs[n + m + 1], refs[-1]
        for k, (src, dst, peer) in enumerate(plan(src_refs, land_refs)):
            pltpu.make_async_remote_copy(src_ref=src, dst_ref=dst, send_sem=send_sems.at[k], recv_sem=recv_sems.at[k],
                                         device_id=peer, device_id_type=MESH).start()
        token[...] = jnp.zeros_like(token)

    bufs = [pltpu.with_memory_space_constraint(t, pltpu.HBM) for t in (*srcs, *lands)]
    outs = pl.pallas_call(
        body, name=name,
        out_shape=(pltpu.SemaphoreType.DMA((n_copies,)), pltpu.SemaphoreType.DMA((n_copies,)),
                   *[pltpu.HBM(t.shape, t.dtype) for t in bufs], jax.ShapeDtypeStruct((8, 128), F32)),
        in_specs=[_HBM_ONLY] * (n + m),
        out_specs=(_SEMS, _SEMS, *[_HBM_ONLY] * (n + m), pl.BlockSpec(memory_space=pltpu.VMEM)),
        input_output_aliases={k: 2 + k for k in range(n + m)},
        compiler_params=pltpu.CompilerParams(has_side_effects=_EFFECT))(*bufs)
    return outs[0], outs[1], list(outs[2:2 + n + m]), outs[-1]


def _copies_wait(send_sems, recv_sems, thru, n_src, plan, after, *, name):
    nm = len(thru)

    def body(*refs):
        t_refs, send, recv = refs[:nm], refs[nm], refs[nm + 1]
        for k, (src, dst, peer) in enumerate(plan(t_refs[:n_src], t_refs[n_src:])):
            cp = pltpu.make_async_remote_copy(src_ref=src, dst_ref=dst, send_sem=send.at[k], recv_sem=recv.at[k],
                                              device_id=peer, device_id_type=MESH)
            cp.wait_send()
            cp.wait_recv()

    outs = pl.pallas_call(
        body, name=name, out_shape=tuple(pltpu.HBM(t.shape, t.dtype) for t in thru),
        in_specs=[_HBM_ONLY] * nm + [_SEMS, _SEMS, pl.BlockSpec(memory_space=pl.ANY)],
        out_specs=tuple([_HBM_ONLY] * nm), input_output_aliases={k: k for k in range(nm)},
        compiler_params=pltpu.CompilerParams(has_side_effects=_EFFECT))(*thru, send_sems, recv_sems, after)
    return list(outs)


_RELATIONS = [(dx, dy, dc) for dx in (0, 1) for dy in (0, 1) for dc in (0, 1)][1:]


def _gather_plan(src_refs, land_refs):
    x, y, c = _my_pos()
    flip = lambda v, d: 1 - v if d else v
    return [(s_ref.at[c], l_ref.at[2 * x + y, c], (flip(x, dx), flip(y, dy), flip(c, dc)))
            for s_ref, l_ref in zip(src_refs, land_refs) for dx, dy, dc in _RELATIONS]


def _gather_chips_plan(src_refs, land_refs):
    x, y, c = _my_pos()
    peers = [(x, y, 1 - c), (1 - x, y, c), (x, 1 - y, c), (1 - x, 1 - y, c)]
    return [(s_ref.at[c], l_ref.at[2 * x + y, c], peer) for s_ref, l_ref in zip(src_refs, land_refs) for peer in peers]


def _gather_pass_plan(src_refs, land_refs):
    x, y, c = _my_pos()
    chips = [(1 - x, y), (x, 1 - y), (1 - x, 1 - y)]
    return [(l_ref.at[2 * cx + cy, c], l_ref.at[2 * cx + cy, c], (x, y, 1 - c))
            for l_ref in land_refs for cx, cy in chips]


def _sibling_plan(src_refs, land_refs):
    x, y, c = _my_pos()
    return [(s_ref.at[1 - c], l_ref, (x, y, 1 - c)) for s_ref, l_ref in zip(src_refs, land_refs)]


def _chips_plan(src_refs, land_refs):
    x, y, c = _my_pos()
    chips = [(1 - x, y), (x, 1 - y), (1 - x, 1 - y)]
    return [(s_ref.at[2 * cx + cy], l_ref.at[k], (cx, cy, c))
            for s_ref, l_ref in zip(src_refs, land_refs) for k, (cx, cy) in enumerate(chips)]


def _place_own(gathered, fam):
    x, y, c = _my_pos()
    own = lax.dynamic_index_in_dim(fam, c, 0, keepdims=True)[None]
    return lax.dynamic_update_slice(gathered, own, (2 * x + y, c) + (0,) * (fam.ndim - 1))


def _to_heads(t, width):
    return t.reshape(t.shape[0], HEADS, width).transpose(1, 0, 2)


def _from_heads(t):
    return t.transpose(1, 0, 2).reshape(t.shape[1], -1)


def _t5_bucket(dist):
    max_exact = N_BUCKETS // 2
    d = jnp.maximum(dist, 1).astype(F32)
    large = max_exact + (jnp.log(d / max_exact) / math.log(MAX_DISTANCE / max_exact)
                         * (N_BUCKETS - max_exact)).astype(jnp.int32)
    large = jnp.minimum(large, N_BUCKETS - 1)
    return jnp.where(dist < max_exact, dist, large)


def _bucket_map(dilation):
    iq = jnp.arange(DIL_BLOCK)[:, None]
    ik = jnp.arange(2 * DIL_BLOCK)[None, :]
    rel = DIL_BLOCK + iq - ik
    return _t5_bucket(jnp.maximum(rel, 0) * dilation).astype(jnp.int32)


def _q_perm(w):
    w3 = w.reshape(w.shape[0], HEADS, QK_NOPE + QK_ROPE)
    return jnp.concatenate([w3[:, :, :QK_NOPE].reshape(w.shape[0], -1),
                            w3[:, :, QK_NOPE:QK_NOPE + HALF_ROPE].reshape(w.shape[0], -1),
                            w3[:, :, QK_NOPE + HALF_ROPE:].reshape(w.shape[0], -1)], axis=1)


def _q_unperm(w):
    n0, n1 = HEADS * QK_NOPE, HEADS * HALF_ROPE
    r = w.shape[0]
    return jnp.concatenate([w[:, :n0].reshape(r, HEADS, QK_NOPE), w[:, n0:n0 + n1].reshape(r, HEADS, HALF_ROPE),
                            w[:, n0 + n1:].reshape(r, HEADS, HALF_ROPE)], axis=2).reshape(r, -1)


def _kv_perm(w):
    w3 = w.reshape(w.shape[0], HEADS, QK_NOPE + V_HEAD)
    return jnp.concatenate([w3[:, :, :QK_NOPE].reshape(w.shape[0], -1), w3[:, :, QK_NOPE:].reshape(w.shape[0], -1)],
                           axis=1)


def _kv_unperm(w):
    n0 = HEADS * QK_NOPE
    r = w.shape[0]
    return jnp.concatenate([w[:, :n0].reshape(r, HEADS, QK_NOPE), w[:, n0:].reshape(r, HEADS, V_HEAD)],
                           axis=2).reshape(r, -1)


def _row(v):
    return v.reshape(1, -1)


def kernel(x, c, norm_pre, norm_post, w_mod, b_mod, ffn_w_gate, ffn_w_up, ffn_w_down, mla_w_in, mla_q_norm, mla_w_q_up, mla_kv_norm, mla_w_kv_up, mla_w_o, dil_w_in, dil_w_o, rel_bias, loss_target, m_norm_pre, m_norm_post, m_w_mod, m_b_mod, m_ffn_w_gate, m_ffn_w_up, m_ffn_w_down, m_mla_w_in, m_mla_q_norm, m_mla_w_q_up, m_mla_kv_norm, m_mla_w_kv_up, m_mla_w_o, m_dil_w_in, m_dil_w_o, m_rel_bias, v_norm_pre, v_norm_post, v_w_mod, v_b_mod, v_ffn_w_gate, v_ffn_w_up, v_ffn_w_down, v_mla_w_in, v_mla_q_norm, v_mla_w_q_up, v_mla_kv_norm, v_mla_w_kv_up, v_mla_w_o, v_dil_w_in, v_dil_w_o, v_rel_bias):
    given = dict(locals())
    ix, iy, ic = _my_pos()
    shard_id = 2 * ix + iy
    dev_id = 4 * ix + 2 * iy + ic
    x2 = x[0]
    target = loss_target[0]
    half_idx = jnp.reshape(ic, (1,)).astype(jnp.int32)
    shard_idx = jnp.reshape(shard_id, (1,)).astype(jnp.int32)

    blk = jnp.zeros((8, D_MODEL), F32)
    blk = blk.at[0].set(c[0])
    blk = blk.at[1:3].set(jnp.pad(norm_pre.reshape(-1), (0, 512)).reshape(2, D_MODEL))
    blk = blk.at[3:5].set(jnp.pad(norm_post.reshape(-1), (0, 512)).reshape(2, D_MODEL))
    got = _all_gather(blk, name="ag_c_norms", in_vmem=True).reshape(N_SHARD, 2, 8, D_MODEL)
    c_all = got[:, :, 0, :].reshape(8, D_MODEL)

    def full_norm(lo):
        t = got[:, 0, lo:lo + 2, :].reshape(N_SHARD, 2 * D_MODEL)[:, :1536].reshape(N_SHARD, 2, 3, 256)
        return t.transpose(1, 2, 0, 3).reshape(2, 3, D_MODEL)

    pre_full, post_full = full_norm(1), full_norm(3)

    silu_c = _silu_bf16(c_all, name="silu_c")
    b_cols = lax.dynamic_slice_in_dim(b_mod, shard_id * 2304, 2304, axis=1).reshape(2, 1, 2304)
    mod_part = _mm(silu_c, w_mod, bias=b_cols, name="mod_mm", tn_cap=768)
    mod_all = _all_gather(mod_part.reshape(16, 2304), name="ag_mod", in_vmem=True)
    mod_all = mod_all.reshape(N_SHARD, 2, 2, 8, 2304)[:, 0]
    mod_mine = lax.dynamic_index_in_dim(mod_all, dev_id, axis=2, keepdims=False)
    mod = mod_mine.transpose(1, 0, 2).reshape(2, 9, D_MODEL)

    bf = lambda t: t.astype(BF16)
    ffn_fam = lambda i, h: [bf(jnp.stack([ffn_w_gate[i, h], ffn_w_up[i, h]])),
                            bf(ffn_w_down[i, h].reshape(2, F_SHARD // 2, D_MODEL))]
    mla_fam = [bf(mla_w_in.reshape(2, 128, -1)), bf(mla_w_q_up.reshape(2, 192, -1)),
               bf(mla_w_kv_up.reshape(2, 128, -1)), bf(mla_w_o.reshape(2, 128, D_MODEL))]
    dil_fam = [bf(dil_w_in.reshape(2, 512, -1)), bf(dil_w_o.reshape(2, 128, D_MODEL))]
    later_fams = [ffn_fam(0, 1), ffn_fam(1, 0) + dil_fam, ffn_fam(1, 1)]
    full, later_fams, mod = lax.optimization_barrier(
        (_gather_weights(ffn_fam(0, 0) + mla_fam, name="ag_weights_first"), later_fams, mod))

    def gather_later(fams, tag):
        lands = [lax.empty((N_SHARD,) + t.shape, t.dtype) for t in fams]
        send, recv, thru, token = _copies_start(fams, lands, _gather_plan, 7 * len(fams), name=f"ag_start_{tag}")
        return dict(send=send, recv=recv, thru=thru, token=token, n=len(fams), tag=tag)

    def arrive(st, after):
        thru = _copies_wait(st['send'], st['recv'], st['thru'], st['n'], _gather_plan, after,
                            name=f"ag_wait_{st['tag']}")
        return [_place_own(o, t) for t, o in zip(thru[:st['n']], thru[st['n']:])]

    def gather_chips(fams, tag):
        lands = [lax.empty((N_SHARD,) + t.shape, t.dtype) for t in fams]
        send, recv, thru, token = _copies_start(fams, lands, _gather_chips_plan, 4 * len(fams), name=f"ag_start_{tag}")
        return dict(send=send, recv=recv, thru=thru, token=token, n=len(fams), tag=tag)

    def pass_on(st, after):
        n, tag = st['n'], st['tag']
        thru = _copies_wait(st['send'], st['recv'], st['thru'], n, _gather_chips_plan, after, name=f"ag_mid_{tag}")
        send, recv, lands, token = _copies_start([], thru[n:], _gather_pass_plan, 3 * n, name=f"ag_pass_{tag}")
        return dict(send=send, recv=recv, thru=lands, fams=thru[:n], tag=tag), token[0, 0]

    def arrive_passed(st, after):
        lands = _copies_wait(st['send'], st['recv'], st['thru'], 0, _gather_pass_plan, after, name=f"ag_wait_{st['tag']}")
        return [_place_own(o, t) for t, o in zip(st['fams'], lands)]

    flight_a = gather_later(later_fams[0], "l0s2")
    _, next_fams = lax.optimization_barrier((flight_a['token'], later_fams[1]))
    flight_b = gather_chips(next_fams, "l1s01")
    as_ffn = lambda w_gu, w_dn: (w_gu, w_dn.reshape(N_SHARD, F_SHARD, D_MODEL))
    ffn_w = {(0, 0): as_ffn(full[0], full[1])}
    w_in = full[2].reshape(D_MODEL, -1)
    wq_p = _q_perm(full[3].reshape(N_SHARD, Q_LORA, -1).transpose(1, 0, 2).reshape(Q_LORA, -1))
    wkv_p = _kv_perm(full[4].reshape(N_SHARD, KV_LORA, -1).transpose(1, 0, 2).reshape(KV_LORA, -1))
    w_mo = full[5].reshape(D_MODEL, D_MODEL)
    dil_w = {}

    pos = jnp.arange(SEQ, dtype=F32)
    freqs = ROPE_THETA ** (-jnp.arange(HALF_ROPE, dtype=F32) / HALF_ROPE)
    ang = pos[:, None] * freqs[None, :]
    cos_k, sin_k = jnp.cos(ang), jnp.sin(ang)
    cos_q, sin_q = jnp.tile(cos_k, (1, HEADS)), jnp.tile(sin_k, (1, HEADS))

    buckets = [_bucket_map(d) for _, d in DIL_GROUPS]
    biases = [_bias_table(rel_bias[:, g * HEADS:(g + 1) * HEADS].T.reshape(HEADS, 1, N_BUCKETS), bk,
                          name=f"dil_bias_table_g{g}") for g, bk in enumerate(buckets)]

    vpacks = jnp.concatenate([pre_full[:, :, None], post_full[:, :, None], mod.reshape(2, 3, 3, D_MODEL),
                              jnp.zeros((2, 3, 3, D_MODEL), F32)], axis=2)
    sub_params = lambda i, sub: vpacks[i, sub]

    def ffn_fwd(xin, i, h, sub, tie=None, mid=None):
        p = sub_params(i, sub)
        if tie is not None:
            p = p + tie
        tag = f"l{i}s{sub}"
        w_gu, w_dn = ffn_w[i, h]
        hn = _pre_fwd(xin, p, name=f"pre_fwd_{tag}")
        gu, a = _ffn_up(hn, w_gu, name=f"ffn_up_{tag}")
        if mid is not None:
            p = p + mid(a)
        f, out = _ffn_down(a, w_dn, xin, p, FFN_RES, name=f"ffn_down_{tag}")
        return out, dict(x=xin, hn=hn, gu=gu, a=a, f=f, p=p, i=i, h=h, tag=tag)

    def mla_fwd(xin, i, sub):
        p = sub_params(i, sub)
        tag = f"l{i}s{sub}"
        hn = _pre_fwd(xin, p, name=f"pre_fwd_{tag}")
        lat = _mm(hn, w_in, name="mla_lat")
        cq, ckv = lat[:, :Q_LORA], lat[:, Q_LORA:Q_LORA + KV_LORA]
        k1, k2 = lat[:, Q_LORA + KV_LORA:Q_LORA + KV_LORA + HALF_ROPE], lat[:, Q_LORA + KV_LORA + HALF_ROPE:]
        cqn = _rms_fwd(cq, mla_q_norm, name="mla_qnorm")
        ckvn = _rms_fwd(ckv, mla_kv_norm, name="mla_kvnorm")
        qp = _mm(cqn, wq_p, name="mla_q_up")
        kvp = _mm(ckvn, wkv_p, name="mla_kv_up")
        n0, n1 = HEADS * QK_NOPE, HEADS * HALF_ROPE
        qr1, qr2 = _rope(qp[:, n0:n0 + n1], qp[:, n0 + n1:], cos_q, sin_q, name="rope_q")
        kr1, kr2 = _rope(k1, k2, cos_k, sin_k, name="rope_k")
        q = jnp.concatenate([qp[:, :n0].reshape(SEQ, HEADS, QK_NOPE), qr1.reshape(SEQ, HEADS, HALF_ROPE),
                             qr2.reshape(SEQ, HEADS, HALF_ROPE)], axis=2).transpose(1, 0, 2)
        kr = jnp.broadcast_to(jnp.concatenate([kr1, kr2], axis=1)[:, None, :], (SEQ, HEADS, QK_ROPE))
        k = jnp.concatenate([kvp[:, :n0].reshape(SEQ, HEADS, QK_NOPE), kr], axis=2).transpose(1, 0, 2).astype(BF16)
        v = _to_heads(kvp[:, n0:], V_HEAD).astype(BF16)
        o, lse = _mla_attn_fwd(q, k, v, name="mla_attn_fwd")
        o_flat = _from_heads(o).astype(BF16)
        f = _mm(o_flat, w_mo, name="mla_out")
        out = _post_fwd(f, xin, p, 1.0, name=f"post_fwd_{tag}")
        return out, dict(x=xin, hn=hn, cq=cq, ckv=ckv, cqn=cqn, ckvn=ckvn, q=q, k=k, v=v, o=o, lse=lse,
                         o_flat=o_flat, f=f, p=p, tag=tag)

    def dil_fwd(xin, i, sub):
        p = sub_params(i, sub)
        tag = f"l{i}s{sub}"
        hn = _pre_fwd(xin, p, name=f"pre_fwd_{tag}")
        heads = _proj_heads(hn, dil_w['in'], name="dil_proj")
        outs, lses = [], []
        for g, (window, d) in enumerate(DIL_GROUPS):
            o, lse = _dil_attn_fwd(heads, biases[g], g, d, name=f"dil_attn_fwd_g{g}")
            outs.append(o)
            lses.append(lse)
        o_flat = _dil_mix_fwd(outs, lses, name="dil_mix_fwd")
        f = _mm(o_flat, dil_w['out'], name="dil_out")
        out = _post_fwd(f, xin, p, 1.0, name=f"post_fwd_{tag}")
        return out, dict(x=xin, hn=hn, heads=heads, outs=outs, lses=lses, o_flat=o_flat, f=f, p=p, tag=tag)

    saved = [None] * 6
    xs, saved[0] = ffn_fwd(x2, 0, 0, 0, tie=flight_a['token'][0, 0] + flight_b['token'][0, 0])
    xs, saved[1] = mla_fwd(xs, 0, 1)
    ffn_w[0, 1] = as_ffn(*arrive(flight_a, xs))
    passed = {}

    def second_step(after):
        passed['st'], tok = pass_on(flight_b, after)
        return tok

    xs, saved[2] = ffn_fwd(xs, 0, 1, 2, mid=second_step)
    got, last_fams = lax.optimization_barrier((arrive_passed(passed['st'], xs), later_fams[2]))
    ffn_w[1, 0] = as_ffn(got[0], got[1])
    dil_w['in'], dil_w['out'] = got[2].reshape(N_SHARD, D_MODEL, -1), got[3].reshape(D_MODEL, D_MODEL)
    in_flight = gather_later(last_fams, "l1s2")
    xs, saved[3] = ffn_fwd(xs, 1, 0, 0, tie=in_flight['token'][0, 0])
    xs, saved[4] = dil_fwd(xs, 1, 1)
    ffn_w[1, 1] = as_ffn(*arrive(in_flight, xs))
    xs, saved[5] = ffn_fwd(xs, 1, 1, 2)

    dx, loss_part = _loss(xs, target, name="loss")

    dmod = [[None] * 9 for _ in range(2)]
    dpre = [[None] * 3 for _ in range(2)]
    dpost = [[None] * 3 for _ in range(2)]
    ffn_units = {}
    row_unit = lambda g, r, j: ((r % 2, r // 2), 0, j)

    def close_sub(dhn, dout, sv, i, sub, res_dgate, res_dqg):
        p = sv['p']
        dxs, dsh, dsc, dpg = _pre_bwd(dhn, sv['x'], dout, p, name=f"pre_bwd_{sv['tag']}")
        dmod[i][3 * sub], dmod[i][3 * sub + 1], dmod[i][3 * sub + 2] = dsh, dsc, res_dgate
        dpre[i][sub], dpost[i][sub] = dpg, res_dqg
        return dxs

    def ffn_bwd(dout, sv, sub, tie=0.0, mid=None):
        i, h, p, tag = sv['i'], sv['h'], sv['p'], sv['tag']
        w_gu, w_dn = ffn_w[i, h]
        df, dgate, dqg = _post_bwd(dout, sv['f'], p + tie, FFN_RES, name=f"post_bwd_{tag}")
        u_dn = _mm(sv['a'], df, ta=True, tn_cap=D_MODEL // 2, out_shape=(2, N_SHARD, F_SHARD, D_MODEL // 2),
                   out_sel=lambda g, r, j: ((j, g), r, 0), name=f"ffn_dwd_{tag}")
        dgu = _ffn_dgu(df, w_dn, sv['gu'], name=f"ffn_dgu_{tag}")
        if mid is not None:
            p = p + mid(dgu)
        u_gu = _mm(dgu.reshape(2 * N_SHARD, SEQ, F_SHARD), sv['hn'], ta=True,
                   out_shape=(2, N_SHARD, F_SHARD, D_MODEL), out_sel=lambda g, r, j: ((g % 2, g // 2), r, j),
                   name=f"ffn_dwgu_{tag}")
        ffn_units[i, h] = [u_gu, u_dn]
        dxs, dsh, dsc, dpg = _ffn_dhn(dgu, w_gu, sv['x'], dout, p, name=f"ffn_dhn_{tag}")
        dmod[i][3 * sub], dmod[i][3 * sub + 1], dmod[i][3 * sub + 2] = dsh, dsc, dgate
        dpre[i][sub], dpost[i][sub] = dpg, dqg
        return dxs

    def mla_bwd(dout, sv, i, sub, tie=0.0):
        p, tag = sv['p'], sv['tag']
        df, dgate, dqg = _post_bwd(dout, sv['f'], p + tie, 1.0, name=f"post_bwd_{tag}")
        u_wo = _mm(sv['o_flat'], df, ta=True, tm_cap=128, out_shape=(2, N_SHARD, 128, D_MODEL), out_sel=row_unit,
                   name="mla_dwo")
        do_flat = _mm(df, w_mo, tb=True, name="mla_do")
        do = _to_heads(do_flat, V_HEAD)
        dq, dk, dv = _mla_attn_bwd(sv['q'], sv['k'], sv['v'], sv['o'], do, sv['lse'], name="mla_attn_bwd")
        dq_t = dq.transpose(1, 0, 2)
        dqr1, dqr2 = _rope(dq_t[:, :, QK_NOPE:QK_NOPE + HALF_ROPE].reshape(SEQ, -1),
                           dq_t[:, :, QK_NOPE + HALF_ROPE:].reshape(SEQ, -1), cos_q, -sin_q, name="rope_q_bwd")
        dqp = jnp.concatenate([dq_t[:, :, :QK_NOPE].reshape(SEQ, -1), dqr1, dqr2], axis=1).astype(BF16)
        dkr = _head_sum(dk[:, :, QK_NOPE:], name="mla_dkr_sum")
        dk1, dk2 = _rope(dkr[:, :HALF_ROPE], dkr[:, HALF_ROPE:], cos_k, -sin_k, name="rope_k_bwd")
        dkvp = jnp.concatenate([_from_heads(dk[:, :, :QK_NOPE]), _from_heads(dv)], axis=1).astype(BF16)
        g_wq = _q_unperm(_mm(sv['cqn'], dqp, ta=True, name="mla_dwq"))
        g_wkv = _kv_unperm(_mm(sv['ckvn'], dkvp, ta=True, name="mla_dwkv"))
        dcqn = _mm(dqp, wq_p, tb=True, name="mla_dcqn")
        dckvn = _mm(dkvp, wkv_p, tb=True, name="mla_dckvn")
        dcq, g_qn = _rms_bwd(dcqn, sv['cq'], mla_q_norm, name="mla_qnorm_bwd")
        dckv, g_kvn = _rms_bwd(dckvn, sv['ckv'], mla_kv_norm, name="mla_kvnorm_bwd")
        dlat = jnp.concatenate([dcq, dckv, dk1, dk2], axis=1).astype(BF16)
        u_win = _mm(sv['hn'], dlat, ta=True, tm_cap=128, out_shape=(2, N_SHARD, 128, dlat.shape[1]),
                    out_sel=row_unit, name="mla_dwin")
        dhn = _mm(dlat, w_in, tb=True, name="mla_dhn")
        col_unit = lambda t: (t.reshape(t.shape[0], N_SHARD, -1).transpose(1, 0, 2)
                              .reshape(N_SHARD, 2, t.shape[0] // 2, -1).transpose(1, 0, 2, 3))
        grads = dict(units=[u_win, col_unit(g_wq), col_unit(g_wkv), u_wo], q_norm=g_qn, kv_norm=g_kvn)
        return close_sub(dhn, dout, sv, i, sub, dgate, dqg), grads

    def dil_bwd(dout, sv, i, sub):
        p, tag = sv['p'], sv['tag']
        df, dgate, dqg = _post_bwd(dout, sv['f'], p, 1.0, name=f"post_bwd_{tag}")
        u_wo = _mm(sv['o_flat'], df, ta=True, tm_cap=128, out_shape=(2, N_SHARD, 128, D_MODEL), out_sel=row_unit,
                   name="dil_dwo")
        dos, dlts = _dil_mix_bwd(_mm(df, dil_w['out'], tb=True, name="dil_do"), sv['outs'], sv['lses'],
                                 name="dil_mix_bwd")
        pieces = []
        bias_rows = []
        for g, (window, d) in enumerate(DIL_GROUPS):
            dq, dk, dv, dbias = _dil_attn_bwd(sv['heads'], biases[g], sv['lses'][g], dos[g], dlts[g], g, d,
                                              name=f"dil_attn_bwd_g{g}")
            pieces += [dq, dk, dv]
            bias_rows.append(_bias_grad(dbias, buckets[g], name=f"dil_bias_grad_g{g}")[:, 0, :])
        dheads = jnp.concatenate(pieces).astype(BF16)
        u_win = _proj_heads_dw(sv['hn'], dheads, name="dil_dwin")
        dhn = _proj_heads_dx(dheads, dil_w['in'], name="dil_dhn")
        g_bias = jnp.concatenate(bias_rows, axis=0).T
        grads = dict(units=[u_win, u_wo], rel_bias=g_bias)
        return close_sub(dhn, dout, sv, i, sub, dgate, dqg), grads

    def to_sibling(units, tag):
        n = len(units)
        send, recv, thru, token = _copies_start(units, [lax.empty(u.shape[1:], F32) for u in units], _sibling_plan, n,
                                                name=f"rs{tag}_sibling_start")
        return dict(send=send, recv=recv, thru=thru, n=n, tag=tag), token[0, 0]

    def from_sibling(st, after):
        n, tag = st['n'], st['tag']
        thru = _copies_wait(st['send'], st['recv'], st['thru'], n, _sibling_plan, after, name=f"rs{tag}_sibling_wait")
        return [_add_half(u, g, half_idx, name=f"rs{tag}_add_half_{k}") for k, (u, g) in enumerate(zip(thru[:n], thru[n:]))]

    def to_chips(parts, tag):
        n = len(parts)
        send, recv, thru, token = _copies_start([w for _, w in parts],
                                                [lax.empty((3,) + w.shape[1:], BF16) for _, w in parts], _chips_plan,
                                                3 * n, name=f"rs{tag}_chips_start")
        return dict(send=send, recv=recv, thru=thru, n=n, tag=tag, parts=parts), token[0, 0]

    def from_chips(st, after):
        n, tag = st['n'], st['tag']
        thru = _copies_wait(st['send'], st['recv'], st['thru'], n, _chips_plan, after, name=f"rs{tag}_chips_wait")
        return [_add_shards(p, g, shard_idx, name=f"rs{tag}_add_shards_{k}")
                for k, ((p, _), g) in enumerate(zip(st['parts'], thru[n:]))]

    dx = ffn_bwd(dx, saved[5], 2)
    dx, dil_g = dil_bwd(dx, saved[4], 1, 1)
    dx = ffn_bwd(dx, saved[3], 0)
    st1, tok = to_sibling([*ffn_units[1, 1], *dil_g['units'], *ffn_units[1, 0]], "1")
    dx = ffn_bwd(dx, saved[2], 2, tie=tok)
    st1, tok1 = to_chips(from_sibling(st1, dx), "1")
    st2, tok2 = to_sibling(ffn_units[0, 1], "2")
    dx, mla_g = mla_bwd(dx, saved[1], 0, 1, tie=tok1 + tok2)
    reds1 = from_chips(st1, dx)
    st2, tok = to_chips(from_sibling(st2, dx), "2")
    st3, tok3 = to_sibling(mla_g['units'], "3")
    onward = {}

    def mixer_to_chips(after):
        onward['st'], t = to_chips(from_sibling(st3, after), "3")
        return t

    dx = ffn_bwd(dx, saved[0], 0, tie=tok + tok3, mid=mixer_to_chips)
    reds2 = from_chips(st2, dx)
    reds3 = from_chips(onward['st'], dx)
    grad_x = dx[None]

    pad_row = lambda v: jnp.pad(v.reshape(-1), (0, (-v.size) % D_MODEL)).reshape(-1, D_MODEL)
    small = jnp.concatenate(
        [jnp.concatenate([dmod[i][r] for i in range(2) for r in range(9)], axis=0),
         jnp.concatenate([dpre[i][s] for i in range(2) for s in range(3)], axis=0),
         jnp.concatenate([dpost[i][s] for i in range(2) for s in range(3)], axis=0),
         pad_row(mla_g['q_norm']), pad_row(mla_g['kv_norm']), pad_row(dil_g['rel_bias']), pad_row(loss_part)], axis=0)
    small = jnp.pad(small, ((0, SMALL_ROWS - small.shape[0]), (0, 0)))
    small_all = _all_gather(small, name="ag_small_grads", in_vmem=True)
    small_sum = _sum_devices(small_all, 8, name="sum_small_grads")
    g_b_mod = small_sum[0:18].reshape(2, 9 * D_MODEL)
    my_cols = lambda t: lax.dynamic_slice_in_dim(t, shard_id * 256, 256, axis=2)
    g_norm_pre = my_cols(small_sum[18:24].reshape(2, 3, D_MODEL))
    g_norm_post = my_cols(small_sum[24:30].reshape(2, 3, D_MODEL))
    g_q_norm = small_sum[30, :Q_LORA].reshape(1, Q_LORA)
    g_kv_norm = small_sum[31, :KV_LORA].reshape(1, KV_LORA)
    g_rel_bias = small_sum[32:34].reshape(-1)[:N_BUCKETS * 48].reshape(N_BUCKETS, 48)
    loss = small_sum[34, 0]
    dmod_all = small_all.reshape(8, SMALL_ROWS, D_MODEL)[:, 0:18].reshape(8, 2, 9 * D_MODEL)
    dmod_cols = lax.dynamic_slice_in_dim(dmod_all, shard_id * 2304, 2304, axis=2).transpose(1, 0, 2)

    swap = lambda t: jnp.swapaxes(t, 2, 3)
    grads = dict(norm_pre=g_norm_pre, norm_post=g_norm_post, b_mod=g_b_mod, mla_q_norm=g_q_norm,
                 mla_kv_norm=g_kv_norm, rel_bias=g_rel_bias)
    deltas, new_m, new_v = {}, {}, {}

    def adamw(names):
        for n in names:
            view = swap if n in ('ffn_w_gate', 'ffn_w_up') else (lambda t: t)
            outs = _adamw(view(given[n]), view(grads[n]), view(given["m_" + n]), view(given["v_" + n]),
                          name=f"adamw_{n}")
            deltas[n], new_m[n], new_v[n] = (view(t) for t in outs)

    st0, tok = to_sibling(ffn_units[0, 0], "0")
    grads['w_mod'] = _mm(silu_c, (dmod_cols + tok).astype(BF16), ta=True, tn_cap=768, name="w_mod_grad")
    adamw(['w_mod'])
    st0, tok = to_chips(from_sibling(st0, deltas['w_mod']), "0")
    grads['b_mod'] = grads['b_mod'] + tok
    fin = _pair_gather(reds1 + reds2 + reds3, name="rs_pair_gather")
    for n, t in zip(['dil_w_in', 'dil_w_o', 'mla_w_in', 'mla_w_q_up', 'mla_w_kv_up', 'mla_w_o'], fin[2:4] + fin[8:12]):
        grads[n] = t.reshape(given[n].shape)
    adamw(['b_mod', 'dil_w_in', 'dil_w_o', 'mla_w_in', 'mla_w_q_up', 'mla_w_kv_up', 'mla_w_o', 'norm_pre', 'norm_post',
           'mla_q_norm', 'mla_kv_norm', 'rel_bias'])
    reds0 = from_chips(st0, deltas['dil_w_in'])
    fin0 = _pair_gather(reds0, name="rs_pair_gather_last")
    ffn_fin = {(1, 1): fin[0:2], (1, 0): fin[4:6], (0, 1): fin[6:8], (0, 0): fin0}
    per_ffn = lambda pick: jnp.stack([jnp.stack([pick(*ffn_fin[i, h]) for h in range(2)]) for i in range(2)])
    grads.update(ffn_w_gate=swap(per_ffn(lambda gu, dn: gu[0])), ffn_w_up=swap(per_ffn(lambda gu, dn: gu[1])),
                 ffn_w_down=per_ffn(lambda gu, dn: jnp.concatenate([dn[0], dn[1]], axis=1)))
    adamw(['ffn_w_gate', 'ffn_w_up', 'ffn_w_down'])
    return (loss, grad_x, *[grads[n] for n in WEIGHTS], *[deltas[n] for n in WEIGHTS],
            *[new_m[n] for n in WEIGHTS], *[new_v[n] for n in WEIGHTS])
```

```python
import math

import jax
import jax.numpy as jnp
from jax import lax
from jax.experimental import pallas as pl
from jax.experimental.pallas import tpu as pltpu

F32 = jnp.float32
BF16 = jnp.bfloat16
MESH = pl.DeviceIdType.MESH

SEQ = 2048
D_MODEL = 1024
D_FF = 2816
N_SHARD = 4
F_SHARD = D_FF // N_SHARD
EPS = 1e-6
FFN_RES = 0.5
HEADS = 16
Q_LORA, KV_LORA, QK_NOPE, QK_ROPE, V_HEAD = 384, 256, 64, 32, 64
HALF_ROPE = QK_ROPE // 2
ROPE_THETA = 10000.0
DIL_GROUPS = ((128, 1), (512, 4), (2048, 16))
DIL_BLOCK = 128
N_BUCKETS = 32
MAX_DISTANCE = 2048
ADAM_LR, ADAM_B1, ADAM_B2, ADAM_EPS, ADAM_WD, ADAM_STEP = 0.001, 0.9, 0.999, 1e-08, 0.01, 10

VMEM_LIMIT = 56 * 1024 * 1024
SMALL_ROWS = 40

WEIGHTS = ['norm_pre', 'norm_post', 'w_mod', 'b_mod', 'ffn_w_gate', 'ffn_w_up', 'ffn_w_down', 'mla_w_in',
           'mla_q_norm', 'mla_w_q_up', 'mla_kv_norm', 'mla_w_kv_up', 'mla_w_o', 'dil_w_in', 'dil_w_o', 'rel_bias']


def _cparams(**kw):
    return pltpu.CompilerParams(vmem_limit_bytes=VMEM_LIMIT, **kw)


def _pick(n, cap, mult=128):
    if n <= cap:
        return n
    best = n
    for t in range(mult, cap + 1, mult):
        if n % t == 0:
            best = t
    return best


def _mm(a, b, *, name, ta=False, tb=False, reduce_g=False, bias=None, out_dtype=F32, tm_cap=1024, tn_cap=1024,
        g_n=None, b_sel=None, out_shape=None, out_sel=None, out_buf=None):
    a3 = a if a.ndim == 3 else a[None]
    ga = a3.shape[0]
    if b_sel is None:
        b_n = b if b.ndim == 3 else b[None]
        gb = b_n.shape[0]
        b_sel = (lambda g: (g,)) if gb > 1 else (lambda g: (0,))
        g_n = max(ga, gb)
    else:
        b_n = b
    k_dim, m_dim = (a3.shape[1], a3.shape[2]) if ta else (a3.shape[2], a3.shape[1])
    k2, n_dim = (b_n.shape[-1], b_n.shape[-2]) if tb else (b_n.shape[-2], b_n.shape[-1])
    assert k_dim == k2, (a.shape, b.shape)
    tm = _pick(m_dim, tm_cap, 128 if ta else 8)
    tn = _pick(n_dim, tn_cap, 128)
    mt, nt = m_dim // tm, n_dim // tn
    dims = (((0 if ta else 1,), (1 if tb else 0,)), ((), ()))

    if reduce_g:
        grid = (mt, nt, g_n)
        ids = lambda i, j, g: (g, i, j)
    else:
        grid = (g_n, mt, nt)
        ids = lambda g, i, j: (g, i, j)

    def a_map(*p):
        g, i, j = ids(*p)
        g = g if ga > 1 else 0
        return (g, 0, i) if ta else (g, i, 0)

    def b_map(*p):
        g, i, j = ids(*p)
        return (*b_sel(g), j, 0) if tb else (*b_sel(g), 0, j)

    b_lead = (None,) * (b_n.ndim - 2)
    a_spec = pl.BlockSpec((None, k_dim, tm) if ta else (None, tm, k_dim), a_map)
    b_spec = pl.BlockSpec(b_lead + ((tn, k_dim) if tb else (k_dim, tn)), b_map)
    in_specs = [a_spec, b_spec]
    operands = [a3, b_n]
    if bias is not None:
        assert not reduce_g and bias.shape == (g_n, 1, n_dim)
        in_specs.append(pl.BlockSpec((None, 1, tn), lambda g, i, j: (g, 0, j)))
        operands.append(bias)
    aliases = {}
    if out_buf is not None:
        assert tuple(out_buf.shape) == tuple(out_shape) and out_buf.dtype == out_dtype
        in_specs.append(pl.BlockSpec(memory_space=pl.ANY))
        operands.append(out_buf)
        aliases = {len(operands) - 1: 0}

    if reduce_g:
        out_spec = pl.BlockSpec((tm, tn), lambda i, j, g: (i, j))
        out_sds = jax.ShapeDtypeStruct((m_dim, n_dim), F32)
    elif out_shape is not None:
        def o_map(g, i, j):
            lead, rb, cb = out_sel(g, i, j)
            return (*lead, rb, cb)

        out_spec = pl.BlockSpec((None,) * (len(out_shape) - 2) + (tm, tn), o_map)
        out_sds = jax.ShapeDtypeStruct(tuple(out_shape), out_dtype)
    else:
        out_spec = pl.BlockSpec((None, tm, tn), lambda g, i, j: (g, i, j))
        out_sds = jax.ShapeDtypeStruct((g_n, m_dim, n_dim), out_dtype)

    def body(a_ref, b_ref, *rest):
        o_ref = rest[-1]
        r = lax.dot_general(a_ref[...].astype(BF16), b_ref[...].astype(BF16), dims, preferred_element_type=F32)
        if bias is not None:
            r = r + rest[0][...]
        if reduce_g:
            g = pl.program_id(2)

            @pl.when(g == 0)
            def _():
                o_ref[...] = r

            @pl.when(g > 0)
            def _():
                o_ref[...] += r
        else:
            o_ref[...] = r.astype(o_ref.dtype)

    out = pl.pallas_call(body, grid=grid, in_specs=in_specs, out_specs=out_spec, out_shape=out_sds,
                         input_output_aliases=aliases, compiler_params=_cparams(), name=name)(*operands)
    if not reduce_g and out_shape is None and a.ndim == 2 and b.ndim == 2:
        out = out[0]
    return out


def _rows(tm, w):
    return pl.BlockSpec((tm, w), lambda i: (i, 0))


def _vec(w):
    return pl.BlockSpec((1, w), lambda i: (0, 0))


def _rstd(v):
    return lax.rsqrt(jnp.mean(v * v, axis=-1, keepdims=True) + EPS)


V_PG, V_QG, V_SH, V_SC, V_GATE = range(5)


def _vrow(v_ref, k):
    return v_ref[k:k + 1, :]


def _vecs(w):
    return pl.BlockSpec((8, w), lambda *_: (0, 0))


def _pre_fwd(x, vp, *, name):
    s_n, w = x.shape
    tm = _pick(s_n, 512, 8)

    def body(x_ref, v_ref, o_ref):
        xv = x_ref[...]
        n = (xv * _rstd(xv)) * _vrow(v_ref, V_PG)
        o_ref[...] = (n * (1.0 + _vrow(v_ref, V_SC)) + _vrow(v_ref, V_SH)).astype(o_ref.dtype)

    return pl.pallas_call(body, grid=(s_n // tm,), in_specs=[_rows(tm, w), _vecs(w)],
                          out_specs=_rows(tm, w), out_shape=jax.ShapeDtypeStruct((s_n, w), BF16),
                          compiler_params=_cparams(), name=name)(x, vp)


def _post_fwd(f, x, vp, res_w, *, name):
    s_n, w = x.shape
    tm = _pick(s_n, 512, 8)

    def body(f_ref, x_ref, v_ref, o_ref):
        fv = f_ref[...]
        y = (fv * _rstd(fv)) * _vrow(v_ref, V_QG)
        o_ref[...] = x_ref[...] + (res_w * _vrow(v_ref, V_GATE)) * y

    return pl.pallas_call(body, grid=(s_n // tm,), in_specs=[_rows(tm, w), _rows(tm, w), _vecs(w)],
                          out_specs=_rows(tm, w), out_shape=jax.ShapeDtypeStruct((s_n, w), F32),
                          compiler_params=_cparams(), name=name)(f, x, vp)


def _post_bwd(dout, f, vp, res_w, *, name):
    s_n, w = f.shape
    tm = _pick(s_n, 512, 8)

    def body(do_ref, f_ref, v_ref, df_ref, dgate_ref, dqg_ref):
        @pl.when(pl.program_id(0) == 0)
        def _():
            dgate_ref[...] = jnp.zeros_like(dgate_ref)
            dqg_ref[...] = jnp.zeros_like(dqg_ref)

        do = do_ref[...]
        fv = f_ref[...]
        r = _rstd(fv)
        fh = fv * r
        qg_v = _vrow(v_ref, V_QG)
        dgate_ref[...] += res_w * jnp.sum(do * (fh * qg_v), axis=0, keepdims=True)
        dy = do * (res_w * _vrow(v_ref, V_GATE))
        dqg_ref[...] += jnp.sum(dy * fh, axis=0, keepdims=True)
        dfh = dy * qg_v
        df = r * (dfh - fh * jnp.mean(dfh * fh, axis=-1, keepdims=True))
        df_ref[...] = df.astype(df_ref.dtype)

    return pl.pallas_call(
        body, grid=(s_n // tm,), in_specs=[_rows(tm, w), _rows(tm, w), _vecs(w)],
        out_specs=[_rows(tm, w), _vec(w), _vec(w)],
        out_shape=[jax.ShapeDtypeStruct((s_n, w), BF16), jax.ShapeDtypeStruct((1, w), F32),
                   jax.ShapeDtypeStruct((1, w), F32)],
        compiler_params=_cparams(), name=name)(dout, f, vp)


def _pre_bwd(dhn, x, dout, vp, *, name):
    s_n, w = x.shape
    tm = _pick(s_n, 512, 8)

    def body(dhn_ref, x_ref, do_ref, v_ref, dx_ref, dsh_ref, dsc_ref, dpg_ref):
        @pl.when(pl.program_id(0) == 0)
        def _():
            dsh_ref[...] = jnp.zeros_like(dsh_ref)
            dsc_ref[...] = jnp.zeros_like(dsc_ref)
            dpg_ref[...] = jnp.zeros_like(dpg_ref)

        dhn_v = dhn_ref[...]
        xv = x_ref[...]
        r = _rstd(xv)
        xh = xv * r
        pg_v = _vrow(v_ref, V_PG)
        dsh_ref[...] += jnp.sum(dhn_v, axis=0, keepdims=True)
        dsc_ref[...] += jnp.sum(dhn_v * (xh * pg_v), axis=0, keepdims=True)
        dn = dhn_v * (1.0 + _vrow(v_ref, V_SC))
        dpg_ref[...] += jnp.sum(dn * xh, axis=0, keepdims=True)
        dxh = dn * pg_v
        dx_ref[...] = do_ref[...] + r * (dxh - xh * jnp.mean(dxh * xh, axis=-1, keepdims=True))

    vec = jax.ShapeDtypeStruct((1, w), F32)
    return pl.pallas_call(
        body, grid=(s_n // tm,), in_specs=[_rows(tm, w), _rows(tm, w), _rows(tm, w), _vecs(w)],
        out_specs=[_rows(tm, w), _vec(w), _vec(w), _vec(w)],
        out_shape=[jax.ShapeDtypeStruct((s_n, w), F32), vec, vec, vec],
        compiler_params=_cparams(), name=name)(dhn, x, dout, vp)


def _rms_fwd(x, g, *, name):
    s_n, w = x.shape
    tm = _pick(s_n, 512, 8)

    def body(x_ref, g_ref, o_ref):
        xv = x_ref[...]
        o_ref[...] = ((xv * _rstd(xv)) * g_ref[...]).astype(o_ref.dtype)

    return pl.pallas_call(body, grid=(s_n // tm,), in_specs=[_rows(tm, w), _vec(w)], out_specs=_rows(tm, w),
                          out_shape=jax.ShapeDtypeStruct((s_n, w), BF16), compiler_params=_cparams(),
                          name=name)(x, g)


def _rms_bwd(dy, x, g, *, name):
    s_n, w = x.shape
    tm = _pick(s_n, 512, 8)

    def body(dy_ref, x_ref, g_ref, dx_ref, dg_ref):
        @pl.when(pl.program_id(0) == 0)
        def _():
            dg_ref[...] = jnp.zeros_like(dg_ref)

        dy_v = dy_ref[...]
        xv = x_ref[...]
        r = _rstd(xv)
        xh = xv * r
        dg_ref[...] += jnp.sum(dy_v * xh, axis=0, keepdims=True)
        dxh = dy_v * g_ref[...]
        dx_ref[...] = r * (dxh - xh * jnp.mean(dxh * xh, axis=-1, keepdims=True))

    return pl.pallas_call(
        body, grid=(s_n // tm,), in_specs=[_rows(tm, w), _rows(tm, w), _vec(w)],
        out_specs=[_rows(tm, w), _vec(w)],
        out_shape=[jax.ShapeDtypeStruct((s_n, w), F32), jax.ShapeDtypeStruct((1, w), F32)],
        compiler_params=_cparams(), name=name)(dy, x, g)


def _rope(a1, a2, cos, sin, *, name):
    s_n, w = a1.shape
    tm = _pick(s_n, 512, 8)

    def body(a1_ref, a2_ref, c_ref, s_ref, r1_ref, r2_ref):
        u, v, c_v, s_v = a1_ref[...], a2_ref[...], c_ref[...], s_ref[...]
        r1_ref[...] = u * c_v - v * s_v
        r2_ref[...] = u * s_v + v * c_v

    sd = jax.ShapeDtypeStruct((s_n, w), F32)
    return pl.pallas_call(body, grid=(s_n // tm,), in_specs=[_rows(tm, w)] * 4, out_specs=[_rows(tm, w)] * 2,
                          out_shape=[sd, sd], compiler_params=_cparams(), name=name)(a1, a2, cos, sin)


def _silu_bf16(x, *, name):
    def body(x_ref, o_ref):
        xv = x_ref[...]
        o_ref[...] = (xv * jax.nn.sigmoid(xv)).astype(o_ref.dtype)

    return pl.pallas_call(body, out_shape=jax.ShapeDtypeStruct(x.shape, BF16), name=name)(x)


def _loss(y, target, *, name):
    s_n, w = y.shape
    tm = _pick(s_n, 512, 8)

    def body(y_ref, t_ref, dy_ref, l_ref):
        @pl.when(pl.program_id(0) == 0)
        def _():
            l_ref[...] = jnp.zeros_like(l_ref)

        e = y_ref[...] - t_ref[...]
        dy_ref[...] = e * (1.0 / w)
        row = jnp.mean(e * e, axis=-1, keepdims=True)
        l_ref[...] += 0.5 * jnp.sum(row, axis=0, keepdims=True)

    return pl.pallas_call(
        body, grid=(s_n // tm,), in_specs=[_rows(tm, w), _rows(tm, w)],
        out_specs=[_rows(tm, w), pl.BlockSpec((1, 1), lambda i: (0, 0))],
        out_shape=[jax.ShapeDtypeStruct((s_n, w), F32), jax.ShapeDtypeStruct((1, 1), F32)],
        compiler_params=_cparams(), name=name)(y, target)


FFN_TM = 1024
FFN_TM_WIDE = 1024


def _ffn_up(hn, w_gu, *, name):
    s_n, d = hn.shape
    f = w_gu.shape[-1]
    tm = _pick(s_n, FFN_TM_WIDE, 8)

    def body(hn_ref, wg_ref, wu_ref, gu_ref, a_ref):
        xv = hn_ref[...]
        g = jnp.dot(xv, wg_ref[...], preferred_element_type=F32)
        u = jnp.dot(xv, wu_ref[...], preferred_element_type=F32)
        gu_ref[0] = g.astype(BF16)
        gu_ref[1] = u.astype(BF16)
        a_ref[...] = ((g * jax.nn.sigmoid(g)) * u).astype(BF16)

    w_blk = lambda t: pl.BlockSpec((None, None, d, f), lambda s, m: (s, t, 0, 0))
    return pl.pallas_call(
        body, grid=(N_SHARD, s_n // tm),
        in_specs=[pl.BlockSpec((tm, d), lambda s, m: (m, 0)), w_blk(0), w_blk(1)],
        out_specs=[pl.BlockSpec((None, 2, tm, f), lambda s, m: (s, 0, m, 0)),
                   pl.BlockSpec((None, tm, f), lambda s, m: (s, m, 0))],
        out_shape=[jax.ShapeDtypeStruct((N_SHARD, 2, s_n, f), BF16), jax.ShapeDtypeStruct((N_SHARD, s_n, f), BF16)],
        compiler_params=_cparams(), name=name)(hn, w_gu, w_gu)


def _ffn_down(a, w_dn, x, vp, res_w, *, name):
    _, s_n, f = a.shape
    d = w_dn.shape[-1]
    tm = _pick(s_n, FFN_TM, 8)
    a = a.reshape(-1, 2, s_n, f)
    w_dn = w_dn.reshape(-1, 2, f, d)
    g_n = a.shape[0]

    def body(a_ref, w_ref, x_ref, v_ref, f_ref, o_ref):
        g = pl.program_id(1)
        r = (jnp.dot(a_ref[0], w_ref[0], preferred_element_type=F32)
             + jnp.dot(a_ref[1], w_ref[1], preferred_element_type=F32))

        @pl.when(g == 0)
        def _():
            f_ref[...] = r

        @pl.when(g > 0)
        def _():
            f_ref[...] += r

        @pl.when(g == g_n - 1)
        def _():
            fv = f_ref[...]
            y = (fv * _rstd(fv)) * _vrow(v_ref, V_QG)
            o_ref[...] = x_ref[...] + (res_w * _vrow(v_ref, V_GATE)) * y

    row = pl.BlockSpec((tm, d), lambda m, g: (m, 0))
    sd = jax.ShapeDtypeStruct((s_n, d), F32)
    return pl.pallas_call(
        body, grid=(s_n // tm, g_n),
        in_specs=[pl.BlockSpec((None, 2, tm, f), lambda m, g: (g, 0, m, 0)),
                  pl.BlockSpec((None, 2, f, d), lambda m, g: (g, 0, 0, 0)), row, _vecs(d)],
        out_specs=[row, row], out_shape=[sd, sd], compiler_params=_cparams(), name=name)(a, w_dn, x, vp)


def _ffn_dhn(dgu, w_gu, x, dout, vp, *, name):
    g_n, _, s_n, f = dgu.shape
    d = w_gu.shape[-2]
    tm = _pick(s_n, FFN_TM, 8)
    nt_dims = (((1,), (1,)), ((), ()))

    def body(a_ref, w_ref, x_ref, do_ref, v_ref, dx_ref, dsh_ref, dsc_ref, dpg_ref, acc_ref):
        m, g = pl.program_id(0), pl.program_id(1)
        r = (lax.dot_general(a_ref[0], w_ref[0], nt_dims, preferred_element_type=F32)
             + lax.dot_general(a_ref[1], w_ref[1], nt_dims, preferred_element_type=F32))

        @pl.when(g == 0)
        def _():
            acc_ref[...] = r

        @pl.when(g > 0)
        def _():
            acc_ref[...] += r

        @pl.when((m == 0) & (g == 0))
        def _():
            dsh_ref[...] = jnp.zeros_like(dsh_ref)
            dsc_ref[...] = jnp.zeros_like(dsc_ref)
            dpg_ref[...] = jnp.zeros_like(dpg_ref)

        @pl.when(g == g_n - 1)
        def _():
            dhn_v = acc_ref[...]
            xv = x_ref[...]
            rs = _rstd(xv)
            xh = xv * rs
            pg_v = _vrow(v_ref, V_PG)
            dsh_ref[...] += jnp.sum(dhn_v, axis=0, keepdims=True)
            dsc_ref[...] += jnp.sum(dhn_v * (xh * pg_v), axis=0, keepdims=True)
            dn = dhn_v * (1.0 + _vrow(v_ref, V_SC))
            dpg_ref[...] += jnp.sum(dn * xh, axis=0, keepdims=True)
            dxh = dn * pg_v
            dx_ref[...] = do_ref[...] + rs * (dxh - xh * jnp.mean(dxh * xh, axis=-1, keepdims=True))

    row = pl.BlockSpec((tm, d), lambda m, g: (m, 0))
    vec = pl.BlockSpec((1, d), lambda m, g: (0, 0))
    vsd = jax.ShapeDtypeStruct((1, d), F32)
    return pl.pallas_call(
        body, grid=(s_n // tm, g_n),
        in_specs=[pl.BlockSpec((None, 2, tm, f), lambda m, g: (g, 0, m, 0)),
                  pl.BlockSpec((None, 2, d, f), lambda m, g: (g, 0, 0, 0)), row, row, _vecs(d)],
        out_specs=[row, vec, vec, vec], out_shape=[jax.ShapeDtypeStruct((s_n, d), F32), vsd, vsd, vsd],
        scratch_shapes=[pltpu.VMEM((tm, d), F32)], compiler_params=_cparams(), name=name)(dgu, w_gu, x, dout, vp)


def _ffn_dgu(df, w_dn, gu, *, name):
    s_n, d = df.shape
    f = w_dn.shape[-2]
    tm = _pick(s_n, FFN_TM_WIDE, 8)

    def body(df_ref, wd_ref, gu_ref, o_ref):
        da = lax.dot_general(df_ref[...], wd_ref[...], (((1,), (1,)), ((), ())), preferred_element_type=F32)
        g = gu_ref[0].astype(F32)
        u = gu_ref[1].astype(F32)
        sig = jax.nn.sigmoid(g)
        o_ref[0] = (da * u * (sig * (1.0 + g * (1.0 - sig)))).astype(BF16)
        o_ref[1] = (da * (g * sig)).astype(BF16)

    gu_blk = pl.BlockSpec((None, 2, tm, f), lambda s, m: (s, 0, m, 0))
    return pl.pallas_call(
        body, grid=(N_SHARD, s_n // tm),
        in_specs=[pl.BlockSpec((tm, d), lambda s, m: (m, 0)),
                  pl.BlockSpec((None, f, d), lambda s, m: (s, 0, 0)), gu_blk],
        out_specs=gu_blk, out_shape=jax.ShapeDtypeStruct((N_SHARD, 2, s_n, f), BF16),
        compiler_params=_cparams(), name=name)(df, w_dn, gu)


_NT = (((1,), (1,)), ((), ()))
_TN = (((0,), (0,)), ((), ()))
MLA_TQ = 512


def _causal_mask(i, tq, s_n):
    qpos = i * tq + lax.broadcasted_iota(jnp.int32, (tq, s_n), 0)
    kpos = lax.broadcasted_iota(jnp.int32, (tq, s_n), 1)
    return kpos <= qpos


def _mla_attn_fwd(q, k, v, *, name):
    h_n, s_n, dq = q.shape
    dv = v.shape[-1]
    tq = MLA_TQ
    scale = float(dq) ** -0.5

    def body(q_ref, k_ref, v_ref, o_ref, lse_ref):
        i = pl.program_id(1)
        for e in range(1, s_n // tq + 1):
            @pl.when(i == e - 1)
            def _(ext=e * tq):
                mask = _causal_mask(i, tq, ext)
                s = lax.dot_general(q_ref[...], k_ref[0:ext, :], _NT, preferred_element_type=F32) * scale
                s = jnp.where(mask, s, -jnp.inf)
                m = jnp.max(s, axis=-1, keepdims=True)
                p = jnp.exp(s - m)
                l = jnp.sum(p, axis=-1, keepdims=True)
                o = jnp.dot(p.astype(BF16), v_ref[0:ext, :], preferred_element_type=F32)
                o_ref[...] = o / l
                lse_ref[...] = m + jnp.log(l)

    return pl.pallas_call(
        body, grid=(h_n, s_n // tq),
        in_specs=[pl.BlockSpec((None, tq, dq), lambda h, i: (h, i, 0)),
                  pl.BlockSpec((None, s_n, dq), lambda h, i: (h, 0, 0)),
                  pl.BlockSpec((None, s_n, dv), lambda h, i: (h, 0, 0))],
        out_specs=[pl.BlockSpec((None, tq, dv), lambda h, i: (h, i, 0)),
                   pl.BlockSpec((None, tq, 1), lambda h, i: (h, i, 0))],
        out_shape=[jax.ShapeDtypeStruct((h_n, s_n, dv), F32), jax.ShapeDtypeStruct((h_n, s_n, 1), F32)],
        compiler_params=_cparams(), name=name)(q, k, v)


def _mla_attn_bwd(q, k, v, o, do, lse, *, name):
    h_n, s_n, dq = q.shape
    dv = v.shape[-1]
    tq = MLA_TQ
    scale = float(dq) ** -0.5

    def body(q_ref, k_ref, v_ref, o_ref, do_ref, lse_ref, dq_ref, dk_ref, dv_ref):
        i = pl.program_id(1)

        @pl.when(i == 0)
        def _():
            dk_ref[...] = jnp.zeros_like(dk_ref)
            dv_ref[...] = jnp.zeros_like(dv_ref)

        for e in range(1, s_n // tq + 1):
            @pl.when(i == e - 1)
            def _(ext=e * tq):
                mask = _causal_mask(i, tq, ext)
                qv, kv, vv = q_ref[...], k_ref[0:ext, :], v_ref[0:ext, :]
                do_v = do_ref[...]
                s = lax.dot_general(qv, kv, _NT, preferred_element_type=F32) * scale
                p = jnp.where(mask, jnp.exp(s - lse_ref[...]), 0.0)
                dob = do_v.astype(BF16)
                dv_ref[0:ext, :] += lax.dot_general(p.astype(BF16), dob, _TN, preferred_element_type=F32)
                dp = lax.dot_general(dob, vv, _NT, preferred_element_type=F32)
                delta = jnp.sum(do_v * o_ref[...], axis=-1, keepdims=True)
                dsb = (p * (dp - delta) * scale).astype(BF16)
                dq_ref[...] = jnp.dot(dsb, kv, preferred_element_type=F32)
                dk_ref[0:ext, :] += lax.dot_general(dsb, qv, _TN, preferred_element_type=F32)

    return pl.pallas_call(
        body, grid=(h_n, s_n // tq),
        in_specs=[pl.BlockSpec((None, tq, dq), lambda h, i: (h, i, 0)),
                  pl.BlockSpec((None, s_n, dq), lambda h, i: (h, 0, 0)),
                  pl.BlockSpec((None, s_n, dv), lambda h, i: (h, 0, 0)),
                  pl.BlockSpec((None, tq, dv), lambda h, i: (h, i, 0)),
                  pl.BlockSpec((None, tq, dv), lambda h, i: (h, i, 0)),
                  pl.BlockSpec((None, tq, 1), lambda h, i: (h, i, 0))],
        out_specs=[pl.BlockSpec((None, tq, dq), lambda h, i: (h, i, 0)),
                   pl.BlockSpec((None, s_n, dq), lambda h, i: (h, 0, 0)),
                   pl.BlockSpec((None, s_n, dv), lambda h, i: (h, 0, 0))],
        out_shape=[jax.ShapeDtypeStruct((h_n, s_n, dq), F32), jax.ShapeDtypeStruct((h_n, s_n, dq), F32),
                   jax.ShapeDtypeStruct((h_n, s_n, dv), F32)],
        compiler_params=_cparams(), name=name)(q, k, v, o, do, lse)


def _head_sum(x, *, name):
    h_n, s_n, w = x.shape
    tm = _pick(s_n, 512, 8)

    def body(x_ref, o_ref):
        o_ref[...] = jnp.sum(x_ref[...], axis=0)

    return pl.pallas_call(body, grid=(s_n // tm,), in_specs=[pl.BlockSpec((h_n, tm, w), lambda i: (0, i, 0))],
                          out_specs=_rows(tm, w), out_shape=jax.ShapeDtypeStruct((s_n, w), F32),
                          compiler_params=_cparams(), name=name)(x)


N_BLK = SEQ // DIL_BLOCK
DIL_SCALE = 64 ** -0.5


def _dil_masks():
    iq = lax.broadcasted_iota(jnp.int32, (DIL_BLOCK, 2 * DIL_BLOCK), 0)
    ik = lax.broadcasted_iota(jnp.int32, (DIL_BLOCK, 2 * DIL_BLOCK), 1)
    rel = DIL_BLOCK + iq - ik
    both = (rel >= 0) & (rel <= DIL_BLOCK)
    iq1 = lax.broadcasted_iota(jnp.int32, (DIL_BLOCK, DIL_BLOCK), 0)
    ik1 = lax.broadcasted_iota(jnp.int32, (DIL_BLOCK, DIL_BLOCK), 1)
    return both, ik1 <= iq1


def _dil_block(j, d):
    nb = SEQ // d // DIL_BLOCK
    r, n = divmod(j, nb)
    first = n == 0
    rows = lambda start, size: pl.ds(start, size) if d == 1 else pl.ds(start, size, stride=d)
    q_rows = rows(n * DIL_BLOCK * d + r, DIL_BLOCK)
    k_rows = q_rows if first else rows((n - 1) * DIL_BLOCK * d + r, 2 * DIL_BLOCK)
    return q_rows, k_rows, (DIL_BLOCK if first else 0), first


PAIR = 2 * 64
N_PAIR = HEADS // 2


def _dil_head_specs(s_n, g):
    return [pl.BlockSpec((None, s_n, PAIR), lambda hp, t=t: ((g * 3 + t) * N_PAIR + hp, 0, 0)) for t in range(3)]


def _pair_specs(s_n, w):
    return pl.BlockSpec((2, s_n, w), lambda hp: (hp, 0, 0))


_PAIR_BIAS = pl.BlockSpec((2, DIL_BLOCK, 2 * DIL_BLOCK), lambda hp: (hp, 0, 0))


def _dil_attn_fwd(heads, bias, g, d, *, name):
    _, s_n, _ = heads.shape
    e = PAIR // 2

    def body(q_ref, k_ref, v_ref, b_ref, o_ref, lse_ref):
        m_both, m_first = _dil_masks()
        for j in range(N_BLK):
            q_rows, k_rows, b_lo, first = _dil_block(j, d)
            q2 = q_ref[q_rows, :].astype(BF16)
            k2 = k_ref[k_rows, :].astype(BF16)
            v2 = v_ref[k_rows, :].astype(BF16)
            for hh in range(2):
                cols = slice(hh * e, (hh + 1) * e)
                s = (lax.dot_general(q2[:, cols], k2[:, cols], _NT, preferred_element_type=F32) * DIL_SCALE
                     + b_ref[hh, :, b_lo:])
                s = jnp.where(m_first if first else m_both, s, -jnp.inf)
                m = jnp.max(s, axis=-1, keepdims=True)
                lse = m + jnp.log(jnp.sum(jnp.exp(s - m), axis=-1, keepdims=True))
                p = jnp.exp(s - lse)
                o_ref[hh, q_rows, :] = jnp.dot(p.astype(BF16), v2[:, cols], preferred_element_type=F32)
                lse_ref[hh, q_rows, :] = lse

    return pl.pallas_call(
        body, grid=(N_PAIR,), in_specs=_dil_head_specs(s_n, g) + [_PAIR_BIAS],
        out_specs=[_pair_specs(s_n, e), _pair_specs(s_n, 1)],
        out_shape=[jax.ShapeDtypeStruct((HEADS, s_n, e), F32), jax.ShapeDtypeStruct((HEADS, s_n, 1), F32)],
        compiler_params=_cparams(), name=name)(heads, heads, heads, bias)


def _dil_attn_bwd(heads, bias, lse, do, dlt, g, d, *, name):
    _, s_n, _ = heads.shape
    e = PAIR // 2

    def body(q_ref, k_ref, v_ref, b_ref, lse_ref, do_ref, dlt_ref, dq_ref, dk_ref, dv_ref, db_ref):
        db_ref[...] = jnp.zeros_like(db_ref)
        m_both, m_first = _dil_masks()
        nb = s_n // d // DIL_BLOCK
        own_v = own_k = own_rows = None
        for j in range(N_BLK):
            q_rows, k_rows, b_lo, first = _dil_block(j, d)
            q2 = q_ref[q_rows, :].astype(BF16)
            k2 = k_ref[k_rows, :].astype(BF16)
            v2 = v_ref[k_rows, :].astype(BF16)
            dq_h, dv_h, dk_h = [], [], []
            for hh in range(2):
                cols = slice(hh * e, (hh + 1) * e)
                qj, kk, vv = q2[:, cols], k2[:, cols], v2[:, cols]
                s = lax.dot_general(qj, kk, _NT, preferred_element_type=F32) * DIL_SCALE + b_ref[hh, :, b_lo:]
                p = jnp.where(m_first if first else m_both, jnp.exp(s - lse_ref[hh, q_rows, :]), 0.0)
                dob = do_ref[hh, q_rows, :].astype(BF16)
                dv_h.append(lax.dot_general(p.astype(BF16), dob, _TN, preferred_element_type=F32))
                dp = lax.dot_general(dob, vv, _NT, preferred_element_type=F32)
                ds = p * (dp - dlt_ref[hh, q_rows, :])
                db_ref[hh, :, b_lo:] += ds
                dsb = (ds * DIL_SCALE).astype(BF16)
                dq_h.append(jnp.dot(dsb, kk, preferred_element_type=F32))
                dk_h.append(lax.dot_general(dsb, qj, _TN, preferred_element_type=F32))
            dq_ref[q_rows, :] = jnp.concatenate(dq_h, axis=1)
            dvv, dkk = jnp.concatenate(dv_h, axis=1), jnp.concatenate(dk_h, axis=1)
            if not first:
                dv_ref[own_rows, :] = own_v + dvv[:DIL_BLOCK]
                dk_ref[own_rows, :] = own_k + dkk[:DIL_BLOCK]
                dvv, dkk = dvv[DIL_BLOCK:], dkk[DIL_BLOCK:]
            own_v, own_k, own_rows = dvv, dkk, q_rows
            if j % nb == nb - 1:
                dv_ref[own_rows, :] = own_v
                dk_ref[own_rows, :] = own_k

    slab = pl.BlockSpec((None, s_n, PAIR), lambda hp: (hp, 0, 0))
    sd = jax.ShapeDtypeStruct((N_PAIR, s_n, PAIR), F32)
    return pl.pallas_call(
        body, grid=(N_PAIR,),
        in_specs=_dil_head_specs(s_n, g) + [_PAIR_BIAS, _pair_specs(s_n, 1), _pair_specs(s_n, e), _pair_specs(s_n, 1)],
        out_specs=[slab, slab, slab, _PAIR_BIAS],
        out_shape=[sd, sd, sd, jax.ShapeDtypeStruct((HEADS, DIL_BLOCK, 2 * DIL_BLOCK), F32)],
        compiler_params=_cparams(), name=name)(heads, heads, heads, bias, lse, do, dlt)


def _proj_heads(x, w, *, name):
    s_n, k = x.shape
    n = w.shape[-1]
    tm, tn, e = 1024, 768, PAIR
    per_blk, n_blk = tn // e, n // tn

    def body(x_ref, w_ref, o_ref):
        r = jnp.dot(x_ref[...], w_ref[...], preferred_element_type=F32)
        for j in range(per_blk):
            o_ref[j] = r[:, e * j:e * (j + 1)]

    return pl.pallas_call(
        body, grid=(w.shape[0], n_blk, s_n // tm),
        in_specs=[pl.BlockSpec((tm, k), lambda s, b, m: (m, 0)), pl.BlockSpec((None, k, tn), lambda s, b, m: (s, 0, b))],
        out_specs=pl.BlockSpec((per_blk, tm, e), lambda s, b, m: (s * n_blk + b, m, 0)),
        out_shape=jax.ShapeDtypeStruct((w.shape[0] * n // e, s_n, e), F32), compiler_params=_cparams(),
        name=name)(x, w)


def _heads_cat(d_ref):
    return jnp.concatenate([d_ref[j] for j in range(d_ref.shape[0])], axis=1)


def _proj_heads_dw(x, dh, *, name):
    s_n, k = x.shape
    tn, e = 768, PAIR
    per_blk = tn // e
    n_blk = dh.shape[0] // N_SHARD // per_blk
    n = n_blk * tn

    def body(x_ref, d_ref, o_ref):
        o_ref[...] = lax.dot_general(x_ref[...], _heads_cat(d_ref), _TN, preferred_element_type=F32)

    return pl.pallas_call(
        body, grid=(N_SHARD, n_blk, 2),
        in_specs=[pl.BlockSpec((s_n, k // 2), lambda s, b, r: (0, r)),
                  pl.BlockSpec((per_blk, s_n, e), lambda s, b, r: (s * n_blk + b, 0, 0))],
        out_specs=pl.BlockSpec((None, None, k // 2, tn), lambda s, b, r: (r, s, 0, b)),
        out_shape=jax.ShapeDtypeStruct((2, N_SHARD, k // 2, n), F32), compiler_params=_cparams(), name=name)(x, dh)


def _proj_heads_dx(dh, w, *, name):
    k, n = w.shape[1:]
    s_n = dh.shape[1]
    tm, tn, e = 512, 1152, PAIR
    per_blk, n_blk = tn // e, n // tn

    def body(d_ref, w_ref, o_ref):
        r = lax.dot_general(_heads_cat(d_ref), w_ref[...], _NT, preferred_element_type=F32)
        g = pl.program_id(1)

        @pl.when(g == 0)
        def _():
            o_ref[...] = r

        @pl.when(g > 0)
        def _():
            o_ref[...] += r

    return pl.pallas_call(
        body, grid=(s_n // tm, N_SHARD * n_blk),
        in_specs=[pl.BlockSpec((per_blk, tm, e), lambda m, g: (g, m, 0)),
                  pl.BlockSpec((None, k, tn), lambda m, g: (g // n_blk, 0, g % n_blk))],
        out_specs=pl.BlockSpec((tm, k), lambda m, g: (m, 0)),
        out_shape=jax.ShapeDtypeStruct((s_n, k), F32), compiler_params=_cparams(), name=name)(dh, w)


def _group_alpha(ls):
    m = jnp.maximum(jnp.maximum(ls[0], ls[1]), ls[2])
    es = [jnp.exp(l - m) for l in ls]
    tot = es[0] + es[1] + es[2]
    return [ex / tot for ex in es]


def _dil_mix_fwd(os_, ls_, *, name):
    h_n, s_n, e = os_[0].shape
    tm = 1024

    def body(o0, o1, o2, l0, l1, l2, out_ref):
        for hh in range(2):
            al = _group_alpha([l[hh] for l in (l0, l1, l2)])
            mix = al[0] * o0[hh] + al[1] * o1[hh] + al[2] * o2[hh]
            out_ref[:, hh * e:(hh + 1) * e] = mix.astype(out_ref.dtype)

    blk = lambda w: pl.BlockSpec((2, tm, w), lambda h, i: (h, i, 0))
    return pl.pallas_call(body, grid=(h_n // 2, s_n // tm), in_specs=[blk(e)] * 3 + [blk(1)] * 3,
                          out_specs=pl.BlockSpec((tm, 2 * e), lambda h, i: (i, h)),
                          out_shape=jax.ShapeDtypeStruct((s_n, h_n * e), BF16), compiler_params=_cparams(),
                          name=name)(*os_, *ls_)


def _dil_mix_bwd(do_flat, os_, ls_, *, name):
    h_n, s_n, e = os_[0].shape
    tm = 1024

    def body(do_ref, o0, o1, o2, l0, l1, l2, d0, d1, d2, t0, t1, t2):
        for hh in range(2):
            al = _group_alpha([l[hh] for l in (l0, l1, l2)])
            do_v = do_ref[:, hh * e:(hh + 1) * e]
            mix = al[0] * o0[hh] + al[1] * o1[hh] + al[2] * o2[hh]
            dbar = jnp.sum(do_v * mix, axis=-1, keepdims=True)
            for a_g, d_ref, t_ref in zip(al, (d0, d1, d2), (t0, t1, t2)):
                d_ref[hh] = a_g * do_v
                t_ref[hh] = a_g * dbar

    blk = lambda w: pl.BlockSpec((2, tm, w), lambda h, i: (h, i, 0))
    sd_e = jax.ShapeDtypeStruct((h_n, s_n, e), F32)
    sd_1 = jax.ShapeDtypeStruct((h_n, s_n, 1), F32)
    outs = pl.pallas_call(body, grid=(h_n // 2, s_n // tm),
                          in_specs=[pl.BlockSpec((tm, 2 * e), lambda h, i: (i, h))] + [blk(e)] * 3 + [blk(1)] * 3,
                          out_specs=[blk(e)] * 3 + [blk(1)] * 3, out_shape=[sd_e] * 3 + [sd_1] * 3,
                          compiler_params=_cparams(), name=name)(do_flat, *os_, *ls_)
    return outs[:3], outs[3:]


def _bias_grad(ds, bucket, *, name):
    h_n = ds.shape[0]

    def body(ds_ref, bk_ref, o_ref):
        ds_v = ds_ref[...]
        bk = bk_ref[...]
        lane = lax.broadcasted_iota(jnp.int32, (1, N_BUCKETS), 1)
        acc = jnp.zeros((1, N_BUCKETS), F32)
        for b in range(N_BUCKETS):
            tot = jnp.sum(jnp.sum(jnp.where(bk == b, ds_v, 0.0), axis=1, keepdims=True), axis=0, keepdims=True)
            acc = acc + jnp.where(lane == b, tot, 0.0)
        o_ref[...] = acc

    return pl.pallas_call(
        body, grid=(h_n,),
        in_specs=[pl.BlockSpec((None, DIL_BLOCK, 2 * DIL_BLOCK), lambda h: (h, 0, 0)),
                  pl.BlockSpec((DIL_BLOCK, 2 * DIL_BLOCK), lambda h: (0, 0))],
        out_specs=pl.BlockSpec((None, 1, N_BUCKETS), lambda h: (h, 0, 0)),
        out_shape=jax.ShapeDtypeStruct((h_n, 1, N_BUCKETS), F32), compiler_params=_cparams(), name=name)(ds, bucket)


def _bias_table(rb, bucket, *, name):
    h_n = rb.shape[0]

    def body(rb_ref, bk_ref, o_ref):
        bk = bk_ref[...]
        row = rb_ref[...]
        acc = jnp.zeros(bk.shape, F32)
        for b in range(N_BUCKETS):
            acc = jnp.where(bk == b, row[:, b:b + 1], acc)
        o_ref[...] = acc

    return pl.pallas_call(
        body, grid=(h_n,),
        in_specs=[pl.BlockSpec((None, 1, N_BUCKETS), lambda h: (h, 0, 0)),
                  pl.BlockSpec((DIL_BLOCK, 2 * DIL_BLOCK), lambda h: (0, 0))],
        out_specs=pl.BlockSpec((None, DIL_BLOCK, 2 * DIL_BLOCK), lambda h: (h, 0, 0)),
        out_shape=jax.ShapeDtypeStruct((h_n, DIL_BLOCK, 2 * DIL_BLOCK), F32), compiler_params=_cparams(),
        name=name)(rb, bucket)


def _row_tile(rows, cols, budget=2 << 20):
    if rows * cols * 4 <= budget or rows % 8:
        return rows
    best = 8
    for t in range(8, rows + 1, 8):
        if rows % t == 0 and t * cols * 4 <= budget:
            best = t
    return best


def _adamw(w, g, m, v, *, name):
    shape = w.shape
    cols = shape[-1]
    rows = math.prod(shape[:-1]) if len(shape) > 1 else 1
    to2 = lambda t: t.reshape(rows, cols)
    tr = _row_tile(rows, cols)
    c1 = 1.0 / (1.0 - ADAM_B1 ** ADAM_STEP)
    c2 = 1.0 / (1.0 - ADAM_B2 ** ADAM_STEP)

    def body(w_ref, g_ref, m_ref, v_ref, d_ref, nm_ref, nv_ref):
        g_v = g_ref[...]
        nm = ADAM_B1 * m_ref[...] + (1.0 - ADAM_B1) * g_v
        nv = ADAM_B2 * v_ref[...] + (1.0 - ADAM_B2) * (g_v * g_v)
        m_hat = nm * c1
        v_hat = nv * c2
        d_ref[...] = -ADAM_LR * (m_hat / (jnp.sqrt(v_hat) + ADAM_EPS) + ADAM_WD * w_ref[...])
        nm_ref[...] = nm
        nv_ref[...] = nv

    blk = pl.BlockSpec((tr, cols), lambda i: (i, 0))
    sd = jax.ShapeDtypeStruct((rows, cols), F32)
    outs = pl.pallas_call(body, grid=(rows // tr,), in_specs=[blk] * 4, out_specs=[blk] * 3, out_shape=[sd] * 3,
                          compiler_params=_cparams(), name=name)(to2(w), to2(g), to2(m), to2(v))
    return tuple(t.reshape(shape) for t in outs)


def _add_half(unit, got, half_idx, *, name):
    rest = unit.shape[2:]
    c = rest[-1]
    r = math.prod(rest[:-1])
    tr = _row_tile(r, c, budget=4 << 20)

    def body(idx_ref, u_ref, g_ref, o_ref, w_ref):
        tot = u_ref[...] + g_ref[...].astype(F32)
        o_ref[...] = tot
        w_ref[...] = tot.astype(BF16)

    blk = pl.BlockSpec((None, tr, c), lambda s, i, idx: (s, i, 0))
    grid_spec = pltpu.PrefetchScalarGridSpec(
        num_scalar_prefetch=1, grid=(N_SHARD, r // tr),
        in_specs=[pl.BlockSpec((None, None, tr, c), lambda s, i, idx: (idx[0], s, i, 0)), blk],
        out_specs=[blk, blk])
    out, wire = pl.pallas_call(
        body, grid_spec=grid_spec,
        out_shape=[jax.ShapeDtypeStruct((N_SHARD, r, c), F32), jax.ShapeDtypeStruct((N_SHARD, r, c), BF16)],
        compiler_params=_cparams(), name=name)(half_idx, unit.reshape(2, N_SHARD, r, c), got.reshape(N_SHARD, r, c))
    return out.reshape((N_SHARD,) + rest), wire.reshape((N_SHARD,) + rest)


def _add_shards(part, got, shard_idx, *, name):
    rest = part.shape[1:]
    c = rest[-1]
    r = math.prod(rest[:-1])
    tr = _row_tile(r, c, budget=4 << 20)

    def body(idx_ref, p_ref, g_ref, o_ref):
        acc = p_ref[...]
        for k in range(3):
            acc = acc + g_ref[k].astype(F32)
        o_ref[...] = acc

    grid_spec = pltpu.PrefetchScalarGridSpec(
        num_scalar_prefetch=1, grid=(r // tr,),
        in_specs=[pl.BlockSpec((None, tr, c), lambda i, idx: (idx[0], i, 0)),
                  pl.BlockSpec((3, tr, c), lambda i, idx: (0, i, 0))],
        out_specs=pl.BlockSpec((tr, c), lambda i, idx: (i, 0)))
    out = pl.pallas_call(body, grid_spec=grid_spec, out_shape=jax.ShapeDtypeStruct((r, c), F32),
                         compiler_params=_cparams(), name=name)(
        shard_idx, part.reshape(N_SHARD, r, c), got.reshape(3, r, c))
    return out.reshape(rest)


def _sum_devices(x, n_dev, *, name):
    rows = x.shape[0] // n_dev

    def body(x_ref, o_ref):
        acc = x_ref[0:rows, :]
        for d in range(1, n_dev):
            acc = acc + x_ref[d * rows:(d + 1) * rows, :]
        o_ref[...] = acc

    return pl.pallas_call(body, out_shape=jax.ShapeDtypeStruct((rows, x.shape[1]), F32), name=name)(x)


def _my_pos():
    return lax.axis_index("x"), lax.axis_index("y"), lax.axis_index("c")


def _all_gather(x_blk, *, name, in_vmem):
    m_per, n = x_blk.shape

    def body(x_ref, out_ref, send_sems, recv_sems, local_sem):
        x, y, c = _my_pos()
        me, sibling = (x, y, c), (x, y, 1 - c)
        chips = [(1 - x, y), (x, 1 - y), (1 - x, 1 - y)]

        def rows(px, py, pc):
            return out_ref.at[pl.ds((4 * px + 2 * py + pc) * m_per, m_per), :]

        def copy(k, block, to, src=None):
            return pltpu.make_async_remote_copy(
                src_ref=rows(*block) if src is None else src, dst_ref=rows(*block),
                send_sem=send_sems.at[k], recv_sem=recv_sems.at[k], device_id=to, device_id_type=MESH)

        mine = pltpu.make_async_copy(x_ref, rows(*me), local_sem)
        mine.start()
        first = [copy(0, me, sibling, src=x_ref)]
        first += [copy(1 + j, me, (*chip, c), src=x_ref) for j, chip in enumerate(chips)]
        for cp in first:
            cp.start()
        passed = [copy(4 + j, (*chip, c), sibling) for j, chip in enumerate(chips)]
        for j, chip in enumerate(chips):
            copy(1 + j, (*chip, c), me).wait_recv()
            passed[j].start()
        copy(0, sibling, me).wait_recv()
        for j, chip in enumerate(chips):
            copy(4 + j, (*chip, 1 - c), me).wait_recv()
        for cp in first + passed:
            cp.wait_send()
        mine.wait()

    space = pltpu.VMEM if in_vmem else pl.ANY
    return pl.pallas_call(
        body, out_shape=jax.ShapeDtypeStruct((8 * m_per, n), x_blk.dtype),
        in_specs=[pl.BlockSpec(memory_space=space)], out_specs=pl.BlockSpec(memory_space=space),
        scratch_shapes=[pltpu.SemaphoreType.DMA((7,)), pltpu.SemaphoreType.DMA((7,)), pltpu.SemaphoreType.DMA],
        name=name)(x_blk)


_HBM = pl.BlockSpec(memory_space=pl.ANY)


def _gather_weights(fams, *, name):
    n = len(fams)

    def body(*refs):
        ins, outs = refs[:n], refs[n:2 * n]
        send_sems, recv_sems = refs[2 * n:]
        x, y, c = _my_pos()
        me, sibling = (x, y, c), (x, y, 1 - c)
        chips = [(1 - x, y), (x, 1 - y), (1 - x, 1 - y)]

        def copy(f, k, block, to, src=None):
            px, py, pc = block
            dst = outs[f].at[2 * px + py, pc]
            return pltpu.make_async_remote_copy(
                src_ref=dst if src is None else src, dst_ref=dst, send_sem=send_sems.at[7 * f + k],
                recv_sem=recv_sems.at[7 * f + k], device_id=to, device_id_type=MESH)

        first, passed = [], []
        for f in range(n):
            src = ins[f].at[c]
            first.append(copy(f, 0, me, sibling, src=src))
            first += [copy(f, 1 + j, me, (*chip, c), src=src) for j, chip in enumerate(chips)]
        for cp in first:
            cp.start()
        for j, chip in enumerate(chips):
            for f in range(n):
                copy(f, 1 + j, (*chip, c), me).wait_recv()
                passed.append(copy(f, 4 + j, (*chip, c), sibling))
                passed[-1].start()
        for f in range(n):
            copy(f, 0, sibling, me).wait_recv()
        for j, chip in enumerate(chips):
            for f in range(n):
                copy(f, 4 + j, (*chip, 1 - c), me).wait_recv()
        for cp in first + passed:
            cp.wait_send()

    outs = pl.pallas_call(
        body, out_shape=[jax.ShapeDtypeStruct((N_SHARD,) + t.shape, t.dtype) for t in fams],
        in_specs=[_HBM] * n, out_specs=[_HBM] * n,
        scratch_shapes=[pltpu.SemaphoreType.DMA((7 * n,)), pltpu.SemaphoreType.DMA((7 * n,))], name=name)(*fams)
    return [_place_own(o, t) for o, t in zip(outs, fams)]


def _pair_gather(halves, *, name):
    n = len(halves)

    def body(*refs):
        ins, outs = refs[:n], refs[n:2 * n]
        send_sems, recv_sems = refs[2 * n:]
        x, y, c = _my_pos()
        cps = [pltpu.make_async_remote_copy(src_ref=ins[f], dst_ref=outs[f].at[c], send_sem=send_sems.at[f],
                                            recv_sem=recv_sems.at[f], device_id=(x, y, 1 - c), device_id_type=MESH)
               for f in range(n)]
        for cp in cps:
            cp.start()
        for f in range(n):
            pltpu.make_async_remote_copy(src_ref=ins[f], dst_ref=outs[f].at[1 - c], send_sem=send_sems.at[f],
                                         recv_sem=recv_sems.at[f], device_id=(x, y, 1 - c),
                                         device_id_type=MESH).wait_recv()
        for cp in cps:
            cp.wait_send()

    outs = pl.pallas_call(
        body, out_shape=[jax.ShapeDtypeStruct((2,) + t.shape, t.dtype) for t in halves],
        in_specs=[_HBM] * n, out_specs=[_HBM] * n,
        scratch_shapes=[pltpu.SemaphoreType.DMA((n,)), pltpu.SemaphoreType.DMA((n,))], name=name)(*halves)
    c = lax.axis_index("c")
    return [lax.dynamic_update_index_in_dim(o, t, c, 0) for o, t in zip(outs, halves)]


_HBM_ONLY = pl.BlockSpec(memory_space=pltpu.HBM)
_SEMS = pl.BlockSpec(memory_space=pltpu.SEMAPHORE)
_EFFECT = pltpu.SideEffectType.DATAFLOW_SIDE_EFFECTING


def _copies_start(srcs, lands, plan, n_copies, *, name):
    n, m = len(srcs), len(lands)

    def body(*refs):
        src_refs, land_refs = refs[:n], refs[n:n + m]
        send_sems, recv_sems, token = refs[n + m], refs[n + m + 1], refs[-1]
        for k, (src, dst, peer) in enumerate(plan(src_refs, land_refs)):
            pltpu.make_async_remote_copy(src_ref=src, dst_ref=dst, send_sem=send_sems.at[k], recv_sem=recv_sems.at[k],
                                         device_id=peer, device_id_type=MESH).start()
        token[...] = jnp.zeros_like(token)

    bufs = [pltpu.with_memory_space_constraint(t, pltpu.HBM) for t in (*srcs, *lands)]
    outs = pl.pallas_call(
        body, name=name,
        out_shape=(pltpu.SemaphoreType.DMA((n_copies,)), pltpu.SemaphoreType.DMA((n_copies,)),
                   *[pltpu.HBM(t.shape, t.dtype) for t in bufs], jax.ShapeDtypeStruct((8, 128), F32)),
        in_specs=[_HBM_ONLY] * (n + m),
        out_specs=(_SEMS, _SEMS, *[_HBM_ONLY] * (n + m), pl.BlockSpec(memory_space=pltpu.VMEM)),
        input_output_aliases={k: 2 + k for k in range(n + m)},
        compiler_params=pltpu.CompilerParams(has_side_effects=_EFFECT))(*bufs)
    return outs[0], outs[1], list(outs[2:2 + n + m]), outs[-1]


def _copies_wait(send_sems, recv_sems, thru, n_src, plan, after, *, name):
    nm = len(thru)

    def body(*refs):
        t_refs, send, recv = refs[:nm], refs[nm], refs[nm + 1]
        for k, (src, dst, peer) in enumerate(plan(t_refs[:n_src], t_refs[n_src:])):
            cp = pltpu.make_async_remote_copy(src_ref=src, dst_ref=dst, send_sem=send.at[k], recv_sem=recv.at[k],
                                              device_id=peer, device_id_type=MESH)
            cp.wait_send()
            cp.wait_recv()

    outs = pl.pallas_call(
        body, name=name, out_shape=tuple(pltpu.HBM(t.shape, t.dtype) for t in thru),
        in_specs=[_HBM_ONLY] * nm + [_SEMS, _SEMS, pl.BlockSpec(memory_space=pl.ANY)],
        out_specs=tuple([_HBM_ONLY] * nm), input_output_aliases={k: k for k in range(nm)},
        compiler_params=pltpu.CompilerParams(has_side_effects=_EFFECT))(*thru, send_sems, recv_sems, after)
    return list(outs)


_RELATIONS = [(dx, dy, dc) for dx in (0, 1) for dy in (0, 1) for dc in (0, 1)][1:]


def _gather_plan(src_refs, land_refs):
    x, y, c = _my_pos()
    flip = lambda v, d: 1 - v if d else v
    return [(s_ref.at[c], l_ref.at[2 * x + y, c], (flip(x, dx), flip(y, dy), flip(c, dc)))
            for s_ref, l_ref in zip(src_refs, land_refs) for dx, dy, dc in _RELATIONS]


def _gather_chips_plan(src_refs, land_refs):
    x, y, c = _my_pos()
    peers = [(x, y, 1 - c), (1 - x, y, c), (x, 1 - y, c), (1 - x, 1 - y, c)]
    return [(s_ref.at[c], l_ref.at[2 * x + y, c], peer) for s_ref, l_ref in zip(src_refs, land_refs) for peer in peers]


def _gather_pass_plan(src_refs, land_refs):
    x, y, c = _my_pos()
    chips = [(1 - x, y), (x, 1 - y), (1 - x, 1 - y)]
    return [(l_ref.at[2 * cx + cy, c], l_ref.at[2 * cx + cy, c], (x, y, 1 - c))
            for l_ref in land_refs for cx, cy in chips]


def _sibling_plan(src_refs, land_refs):
    x, y, c = _my_pos()
    return [(s_ref.at[1 - c], l_ref, (x, y, 1 - c)) for s_ref, l_ref in zip(src_refs, land_refs)]


def _chips_plan(src_refs, land_refs):
    x, y, c = _my_pos()
    chips = [(1 - x, y), (x, 1 - y), (1 - x, 1 - y)]
    return [(s_ref.at[2 * cx + cy], l_ref.at[k], (cx, cy, c))
            for s_ref, l_ref in zip(src_refs, land_refs) for k, (cx, cy) in enumerate(chips)]


def _place_own(gathered, fam):
    x, y, c = _my_pos()
    own = lax.dynamic_index_in_dim(fam, c, 0, keepdims=True)[None]
    return lax.dynamic_update_slice(gathered, own, (2 * x + y, c) + (0,) * (fam.ndim - 1))


def _to_heads(t, width):
    return t.reshape(t.shape[0], HEADS, width).transpose(1, 0, 2)


def _from_heads(t):
    return t.transpose(1, 0, 2).reshape(t.shape[1], -1)


def _t5_bucket(dist):
    max_exact = N_BUCKETS // 2
    d = jnp.maximum(dist, 1).astype(F32)
    large = max_exact + (jnp.log(d / max_exact) / math.log(MAX_DISTANCE / max_exact)
                         * (N_BUCKETS - max_exact)).astype(jnp.int32)
    large = jnp.minimum(large, N_BUCKETS - 1)
    return jnp.where(dist < max_exact, dist, large)


def _bucket_map(dilation):
    iq = jnp.arange(DIL_BLOCK)[:, None]
    ik = jnp.arange(2 * DIL_BLOCK)[None, :]
    rel = DIL_BLOCK + iq - ik
    return _t5_bucket(jnp.maximum(rel, 0) * dilation).astype(jnp.int32)


def _q_perm(w):
    w3 = w.reshape(w.shape[0], HEADS, QK_NOPE + QK_ROPE)
    return jnp.concatenate([w3[:, :, :QK_NOPE].reshape(w.shape[0], -1),
                            w3[:, :, QK_NOPE:QK_NOPE + HALF_ROPE].reshape(w.shape[0], -1),
                            w3[:, :, QK_NOPE + HALF_ROPE:].reshape(w.shape[0], -1)], axis=1)


def _q_unperm(w):
    n0, n1 = HEADS * QK_NOPE, HEADS * HALF_ROPE
    r = w.shape[0]
    return jnp.concatenate([w[:, :n0].reshape(r, HEADS, QK_NOPE), w[:, n0:n0 + n1].reshape(r, HEADS, HALF_ROPE),
                            w[:, n0 + n1:].reshape(r, HEADS, HALF_ROPE)], axis=2).reshape(r, -1)


def _kv_perm(w):
    w3 = w.reshape(w.shape[0], HEADS, QK_NOPE + V_HEAD)
    return jnp.concatenate([w3[:, :, :QK_NOPE].reshape(w.shape[0], -1), w3[:, :, QK_NOPE:].reshape(w.shape[0], -1)],
                           axis=1)


def _kv_unperm(w):
    n0 = HEADS * QK_NOPE
    r = w.shape[0]
    return jnp.concatenate([w[:, :n0].reshape(r, HEADS, QK_NOPE), w[:, n0:].reshape(r, HEADS, V_HEAD)],
                           axis=2).reshape(r, -1)


def _row(v):
    return v.reshape(1, -1)


def kernel(x, c, norm_pre, norm_post, w_mod, b_mod, ffn_w_gate, ffn_w_up, ffn_w_down, mla_w_in, mla_q_norm, mla_w_q_up, mla_kv_norm, mla_w_kv_up, mla_w_o, dil_w_in, dil_w_o, rel_bias, loss_target, m_norm_pre, m_norm_post, m_w_mod, m_b_mod, m_ffn_w_gate, m_ffn_w_up, m_ffn_w_down, m_mla_w_in, m_mla_q_norm, m_mla_w_q_up, m_mla_kv_norm, m_mla_w_kv_up, m_mla_w_o, m_dil_w_in, m_dil_w_o, m_rel_bias, v_norm_pre, v_norm_post, v_w_mod, v_b_mod, v_ffn_w_gate, v_ffn_w_up, v_ffn_w_down, v_mla_w_in, v_mla_q_norm, v_mla_w_q_up, v_mla_kv_norm, v_mla_w_kv_up, v_mla_w_o, v_dil_w_in, v_dil_w_o, v_rel_bias):
    given = dict(locals())
    ix, iy, ic = _my_pos()
    shard_id = 2 * ix + iy
    dev_id = 4 * ix + 2 * iy + ic
    x2 = x[0]
    target = loss_target[0]
    half_idx = jnp.reshape(ic, (1,)).astype(jnp.int32)
    shard_idx = jnp.reshape(shard_id, (1,)).astype(jnp.int32)

    blk = jnp.zeros((8, D_MODEL), F32)
    blk = blk.at[0].set(c[0])
    blk = blk.at[1:3].set(jnp.pad(norm_pre.reshape(-1), (0, 512)).reshape(2, D_MODEL))
    blk = blk.at[3:5].set(jnp.pad(norm_post.reshape(-1), (0, 512)).reshape(2, D_MODEL))
    got = _all_gather(blk, name="ag_c_norms", in_vmem=True).reshape(N_SHARD, 2, 8, D_MODEL)
    c_all = got[:, :, 0, :].reshape(8, D_MODEL)

    def full_norm(lo):
        t = got[:, 0, lo:lo + 2, :].reshape(N_SHARD, 2 * D_MODEL)[:, :1536].reshape(N_SHARD, 2, 3, 256)
        return t.transpose(1, 2, 0, 3).reshape(2, 3, D_MODEL)

    pre_full, post_full = full_norm(1), full_norm(3)

    silu_c = _silu_bf16(c_all, name="silu_c")
    b_cols = lax.dynamic_slice_in_dim(b_mod, shard_id * 2304, 2304, axis=1).reshape(2, 1, 2304)
    mod_part = _mm(silu_c, w_mod, bias=b_cols, name="mod_mm", tn_cap=768)
    mod_all = _all_gather(mod_part.reshape(16, 2304), name="ag_mod", in_vmem=True)
    mod_all = mod_all.reshape(N_SHARD, 2, 2, 8, 2304)[:, 0]
    mod_mine = lax.dynamic_index_in_dim(mod_all, dev_id, axis=2, keepdims=False)
    mod = mod_mine.transpose(1, 0, 2).reshape(2, 9, D_MODEL)

    bf = lambda t: t.astype(BF16)
    ffn_fam = lambda i, h: [bf(jnp.stack([ffn_w_gate[i, h], ffn_w_up[i, h]])),
                            bf(ffn_w_down[i, h].reshape(2, F_SHARD // 2, D_MODEL))]
    mla_fam = [bf(mla_w_in.reshape(2, 128, -1)), bf(mla_w_q_up.reshape(2, 192, -1)),
               bf(mla_w_kv_up.reshape(2, 128, -1)), bf(mla_w_o.reshape(2, 128, D_MODEL))]
    dil_fam = [bf(dil_w_in.reshape(2, 512, -1)), bf(dil_w_o.reshape(2, 128, D_MODEL))]
    later_fams = [ffn_fam(0, 1), ffn_fam(1, 0) + dil_fam, ffn_fam(1, 1)]
    full, later_fams, mod = lax.optimization_barrier(
        (_gather_weights(ffn_fam(0, 0) + mla_fam, name="ag_weights_first"), later_fams, mod))

    def gather_later(fams, tag):
        lands = [lax.empty((N_SHARD,) + t.shape, t.dtype) for t in fams]
        send, recv, thru, token = _copies_start(fams, lands, _gather_plan, 7 * len(fams), name=f"ag_start_{tag}")
        return dict(send=send, recv=recv, thru=thru, token=token, n=len(fams), tag=tag)

    def arrive(st, after):
        thru = _copies_wait(st['send'], st['recv'], st['thru'], st['n'], _gather_plan, after,
                            name=f"ag_wait_{st['tag']}")
        return [_place_own(o, t) for t, o in zip(thru[:st['n']], thru[st['n']:])]

    def gather_chips(fams, tag):
        lands = [lax.empty((N_SHARD,) + t.shape, t.dtype) for t in fams]
        send, recv, thru, token = _copies_start(fams, lands, _gather_chips_plan, 4 * len(fams), name=f"ag_start_{tag}")
        return dict(send=send, recv=recv, thru=thru, token=token, n=len(fams), tag=tag)

    def pass_on(st, after):
        n, tag = st['n'], st['tag']
        thru = _copies_wait(st['send'], st['recv'], st['thru'], n, _gather_chips_plan, after, name=f"ag_mid_{tag}")
        send, recv, lands, token = _copies_start([], thru[n:], _gather_pass_plan, 3 * n, name=f"ag_pass_{tag}")
        return dict(send=send, recv=recv, thru=lands, fams=thru[:n], tag=tag), token[0, 0]

    def arrive_passed(st, after):
        lands = _copies_wait(st['send'], st['recv'], st['thru'], 0, _gather_pass_plan, after, name=f"ag_wait_{st['tag']}")
        return [_place_own(o, t) for t, o in zip(st['fams'], lands)]

    flight_a = gather_later(later_fams[0], "l0s2")
    _, next_fams = lax.optimization_barrier((flight_a['token'], later_fams[1]))
    flight_b = gather_chips(next_fams, "l1s01")
    as_ffn = lambda w_gu, w_dn: (w_gu, w_dn.reshape(N_SHARD, F_SHARD, D_MODEL))
    ffn_w = {(0, 0): as_ffn(full[0], full[1])}
    w_in = full[2].reshape(D_MODEL, -1)
    wq_p = _q_perm(full[3].reshape(N_SHARD, Q_LORA, -1).transpose(1, 0, 2).reshape(Q_LORA, -1))
    wkv_p = _kv_perm(full[4].reshape(N_SHARD, KV_LORA, -1).transpose(1, 0, 2).reshape(KV_LORA, -1))
    w_mo = full[5].reshape(D_MODEL, D_MODEL)
    dil_w = {}

    pos = jnp.arange(SEQ, dtype=F32)
    freqs = ROPE_THETA ** (-jnp.arange(HALF_ROPE, dtype=F32) / HALF_ROPE)
    ang = pos[:, None] * freqs[None, :]
    cos_k, sin_k = jnp.cos(ang), jnp.sin(ang)
    cos_q, sin_q = jnp.tile(cos_k, (1, HEADS)), jnp.tile(sin_k, (1, HEADS))

    buckets = [_bucket_map(d) for _, d in DIL_GROUPS]
    biases = [_bias_table(rel_bias[:, g * HEADS:(g + 1) * HEADS].T.reshape(HEADS, 1, N_BUCKETS), bk,
                          name=f"dil_bias_table_g{g}") for g, bk in enumerate(buckets)]

    vpacks = jnp.concatenate([pre_full[:, :, None], post_full[:, :, None], mod.reshape(2, 3, 3, D_MODEL),
                              jnp.zeros((2, 3, 3, D_MODEL), F32)], axis=2)
    sub_params = lambda i, sub: vpacks[i, sub]

    def ffn_fwd(xin, i, h, sub, tie=None, mid=None):
        p = sub_params(i, sub)
        if tie is not None:
            p = p + tie
        tag = f"l{i}s{sub}"
        w_gu, w_dn = ffn_w[i, h]
        hn = _pre_fwd(xin, p, name=f"pre_fwd_{tag}")
        gu, a = _ffn_up(hn, w_gu, name=f"ffn_up_{tag}")
        if mid is not None:
            p = p + mid(a)
        f, out = _ffn_down(a, w_dn, xin, p, FFN_RES, name=f"ffn_down_{tag}")
        return out, dict(x=xin, hn=hn, gu=gu, a=a, f=f, p=p, i=i, h=h, tag=tag)

    def mla_fwd(xin, i, sub):
        p = sub_params(i, sub)
        tag = f"l{i}s{sub}"
        hn = _pre_fwd(xin, p, name=f"pre_fwd_{tag}")
        lat = _mm(hn, w_in, name="mla_lat")
        cq, ckv = lat[:, :Q_LORA], lat[:, Q_LORA:Q_LORA + KV_LORA]
        k1, k2 = lat[:, Q_LORA + KV_LORA:Q_LORA + KV_LORA + HALF_ROPE], lat[:, Q_LORA + KV_LORA + HALF_ROPE:]
        cqn = _rms_fwd(cq, mla_q_norm, name="mla_qnorm")
        ckvn = _rms_fwd(ckv, mla_kv_norm, name="mla_kvnorm")
        qp = _mm(cqn, wq_p, name="mla_q_up")
        kvp = _mm(ckvn, wkv_p, name="mla_kv_up")
        n0, n1 = HEADS * QK_NOPE, HEADS * HALF_ROPE
        qr1, qr2 = _rope(qp[:, n0:n0 + n1], qp[:, n0 + n1:], cos_q, sin_q, name="rope_q")
        kr1, kr2 = _rope(k1, k2, cos_k, sin_k, name="rope_k")
        q = jnp.concatenate([qp[:, :n0].reshape(SEQ, HEADS, QK_NOPE), qr1.reshape(SEQ, HEADS, HALF_ROPE),
                             qr2.reshape(SEQ, HEADS, HALF_ROPE)], axis=2).transpose(1, 0, 2).astype(BF16)
        kr = jnp.broadcast_to(jnp.concatenate([kr1, kr2], axis=1)[:, None, :], (SEQ, HEADS, QK_ROPE))
        k = jnp.concatenate([kvp[:, :n0].reshape(SEQ, HEADS, QK_NOPE), kr], axis=2).transpose(1, 0, 2).astype(BF16)
        v = _to_heads(kvp[:, n0:], V_HEAD).astype(BF16)
        o, lse = _mla_attn_fwd(q, k, v, name="mla_attn_fwd")
        o_flat = _from_heads(o).astype(BF16)
        f = _mm(o_flat, w_mo, name="mla_out")
        out = _post_fwd(f, xin, p, 1.0, name=f"post_fwd_{tag}")
        return out, dict(x=xin, hn=hn, cq=cq, ckv=ckv, cqn=cqn, ckvn=ckvn, q=q, k=k, v=v, o=o, lse=lse,
                         o_flat=o_flat, f=f, p=p, tag=tag)

    def dil_fwd(xin, i, sub):
        p = sub_params(i, sub)
        tag = f"l{i}s{sub}"
        hn = _pre_fwd(xin, p, name=f"pre_fwd_{tag}")
        heads = _proj_heads(hn, dil_w['in'], name="dil_proj")
        outs, lses = [], []
        for g, (window, d) in enumerate(DIL_GROUPS):
            o, lse = _dil_attn_fwd(heads, biases[g], g, d, name=f"dil_attn_fwd_g{g}")
            outs.append(o)
            lses.append(lse)
        o_flat = _dil_mix_fwd(outs, lses, name="dil_mix_fwd")
        f = _mm(o_flat, dil_w['out'], name="dil_out")
        out = _post_fwd(f, xin, p, 1.0, name=f"post_fwd_{tag}")
        return out, dict(x=xin, hn=hn, heads=heads, outs=outs, lses=lses, o_flat=o_flat, f=f, p=p, tag=tag)

    saved = [None] * 6
    xs, saved[0] = ffn_fwd(x2, 0, 0, 0, tie=flight_a['token'][0, 0] + flight_b['token'][0, 0])
    xs, saved[1] = mla_fwd(xs, 0, 1)
    ffn_w[0, 1] = as_ffn(*arrive(flight_a, xs))
    passed = {}

    def second_step(after):
        passed['st'], tok = pass_on(flight_b, after)
        return tok

    xs, saved[2] = ffn_fwd(xs, 0, 1, 2, mid=second_step)
    got, last_fams = lax.optimization_barrier((arrive_passed(passed['st'], xs), later_fams[2]))
    ffn_w[1, 0] = as_ffn(got[0], got[1])
    dil_w['in'], dil_w['out'] = got[2].reshape(N_SHARD, D_MODEL, -1), got[3].reshape(D_MODEL, D_MODEL)
    in_flight = gather_later(last_fams, "l1s2")
    xs, saved[3] = ffn_fwd(xs, 1, 0, 0, tie=in_flight['token'][0, 0])
    xs, saved[4] = dil_fwd(xs, 1, 1)
    ffn_w[1, 1] = as_ffn(*arrive(in_flight, xs))
    xs, saved[5] = ffn_fwd(xs, 1, 1, 2)

    dx, loss_part = _loss(xs, target, name="loss")

    dmod = [[None] * 9 for _ in range(2)]
    dpre = [[None] * 3 for _ in range(2)]
    dpost = [[None] * 3 for _ in range(2)]
    ffn_units = {}
    row_unit = lambda g, r, j: ((r % 2, r // 2), 0, j)

    def close_sub(dhn, dout, sv, i, sub, res_dgate, res_dqg):
        p = sv['p']
        dxs, dsh, dsc, dpg = _pre_bwd(dhn, sv['x'], dout, p, name=f"pre_bwd_{sv['tag']}")
        dmod[i][3 * sub], dmod[i][3 * sub + 1], dmod[i][3 * sub + 2] = dsh, dsc, res_dgate
        dpre[i][sub], dpost[i][sub] = dpg, res_dqg
        return dxs

    def ffn_bwd(dout, sv, sub, tie=0.0, mid=None):
        i, h, p, tag = sv['i'], sv['h'], sv['p'], sv['tag']
        w_gu, w_dn = ffn_w[i, h]
        df, dgate, dqg = _post_bwd(dout, sv['f'], p + tie, FFN_RES, name=f"post_bwd_{tag}")
        u_dn = _mm(sv['a'], df, ta=True, tn_cap=D_MODEL // 2, out_shape=(2, N_SHARD, F_SHARD, D_MODEL // 2),
                   out_sel=lambda g, r, j: ((j, g), r, 0), name=f"ffn_dwd_{tag}")
        dgu = _ffn_dgu(df, w_dn, sv['gu'], name=f"ffn_dgu_{tag}")
        if mid is not None:
            p = p + mid(dgu)
        u_gu = _mm(dgu.reshape(2 * N_SHARD, SEQ, F_SHARD), sv['hn'], ta=True,
                   out_shape=(2, N_SHARD, F_SHARD, D_MODEL), out_sel=lambda g, r, j: ((g % 2, g // 2), r, j),
                   name=f"ffn_dwgu_{tag}")
        ffn_units[i, h] = [u_gu, u_dn]
        dxs, dsh, dsc, dpg = _ffn_dhn(dgu, w_gu, sv['x'], dout, p, name=f"ffn_dhn_{tag}")
        dmod[i][3 * sub], dmod[i][3 * sub + 1], dmod[i][3 * sub + 2] = dsh, dsc, dgate
        dpre[i][sub], dpost[i][sub] = dpg, dqg
        return dxs

    def mla_bwd(dout, sv, i, sub, tie=0.0):
        p, tag = sv['p'], sv['tag']
        df, dgate, dqg = _post_bwd(dout, sv['f'], p + tie, 1.0, name=f"post_bwd_{tag}")
        u_wo = _mm(sv['o_flat'], df, ta=True, tm_cap=128, out_shape=(2, N_SHARD, 128, D_MODEL), out_sel=row_unit,
                   name="mla_dwo")
        do_flat = _mm(df, w_mo, tb=True, name="mla_do")
        do = _to_heads(do_flat, V_HEAD)
        dq, dk, dv = _mla_attn_bwd(sv['q'], sv['k'], sv['v'], sv['o'], do, sv['lse'], name="mla_attn_bwd")
        dq_t = dq.transpose(1, 0, 2)
        dqr1, dqr2 = _rope(dq_t[:, :, QK_NOPE:QK_NOPE + HALF_ROPE].reshape(SEQ, -1),
                           dq_t[:, :, QK_NOPE + HALF_ROPE:].reshape(SEQ, -1), cos_q, -sin_q, name="rope_q_bwd")
        dqp = jnp.concatenate([dq_t[:, :, :QK_NOPE].reshape(SEQ, -1), dqr1, dqr2], axis=1).astype(BF16)
        dkr = _head_sum(dk[:, :, QK_NOPE:], name="mla_dkr_sum")
        dk1, dk2 = _rope(dkr[:, :HALF_ROPE], dkr[:, HALF_ROPE:], cos_k, -sin_k, name="rope_k_bwd")
        dkvp = jnp.concatenate([_from_heads(dk[:, :, :QK_NOPE]), _from_heads(dv)], axis=1).astype(BF16)
        g_wq = _q_unperm(_mm(sv['cqn'], dqp, ta=True, name="mla_dwq"))
        g_wkv = _kv_unperm(_mm(sv['ckvn'], dkvp, ta=True, name="mla_dwkv"))
        dcqn = _mm(dqp, wq_p, tb=True, name="mla_dcqn")
        dckvn = _mm(dkvp, wkv_p, tb=True, name="mla_dckvn")
        dcq, g_qn = _rms_bwd(dcqn, sv['cq'], mla_q_norm, name="mla_qnorm_bwd")
        dckv, g_kvn = _rms_bwd(dckvn, sv['ckv'], mla_kv_norm, name="mla_kvnorm_bwd")
        dlat = jnp.concatenate([dcq, dckv, dk1, dk2], axis=1).astype(BF16)
        u_win = _mm(sv['hn'], dlat, ta=True, tm_cap=128, out_shape=(2, N_SHARD, 128, dlat.shape[1]),
                    out_sel=row_unit, name="mla_dwin")
        dhn = _mm(dlat, w_in, tb=True, name="mla_dhn")
        col_unit = lambda t: (t.reshape(t.shape[0], N_SHARD, -1).transpose(1, 0, 2)
                              .reshape(N_SHARD, 2, t.shape[0] // 2, -1).transpose(1, 0, 2, 3))
        grads = dict(units=[u_win, col_unit(g_wq), col_unit(g_wkv), u_wo], q_norm=g_qn, kv_norm=g_kvn)
        return close_sub(dhn, dout, sv, i, sub, dgate, dqg), grads

    def dil_bwd(dout, sv, i, sub):
        p, tag = sv['p'], sv['tag']
        df, dgate, dqg = _post_bwd(dout, sv['f'], p, 1.0, name=f"post_bwd_{tag}")
        u_wo = _mm(sv['o_flat'], df, ta=True, tm_cap=128, out_shape=(2, N_SHARD, 128, D_MODEL), out_sel=row_unit,
                   name="dil_dwo")
        dos, dlts = _dil_mix_bwd(_mm(df, dil_w['out'], tb=True, name="dil_do"), sv['outs'], sv['lses'],
                                 name="dil_mix_bwd")
        pieces = []
        bias_rows = []
        for g, (window, d) in enumerate(DIL_GROUPS):
            dq, dk, dv, dbias = _dil_attn_bwd(sv['heads'], biases[g], sv['lses'][g], dos[g], dlts[g], g, d,
                                              name=f"dil_attn_bwd_g{g}")
            pieces += [dq, dk, dv]
            bias_rows.append(_bias_grad(dbias, buckets[g], name=f"dil_bias_grad_g{g}")[:, 0, :])
        dheads = jnp.concatenate(pieces).astype(BF16)
        u_win = _proj_heads_dw(sv['hn'], dheads, name="dil_dwin")
        dhn = _proj_heads_dx(dheads, dil_w['in'], name="dil_dhn")
        g_bias = jnp.concatenate(bias_rows, axis=0).T
        grads = dict(units=[u_win, u_wo], rel_bias=g_bias)
        return close_sub(dhn, dout, sv, i, sub, dgate, dqg), grads

    def to_sibling(units, tag):
        n = len(units)
        send, recv, thru, token = _copies_start(units, [lax.empty(u.shape[1:], F32) for u in units], _sibling_plan, n,
                                                name=f"rs{tag}_sibling_start")
        return dict(send=send, recv=recv, thru=thru, n=n, tag=tag), token[0, 0]

    def from_sibling(st, after):
        n, tag = st['n'], st['tag']
        thru = _copies_wait(st['send'], st['recv'], st['thru'], n, _sibling_plan, after, name=f"rs{tag}_sibling_wait")
        return [_add_half(u, g, half_idx, name=f"rs{tag}_add_half_{k}") for k, (u, g) in enumerate(zip(thru[:n], thru[n:]))]

    def to_chips(parts, tag):
        n = len(parts)
        send, recv, thru, token = _copies_start([w for _, w in parts],
                                                [lax.empty((3,) + w.shape[1:], BF16) for _, w in parts], _chips_plan,
                                                3 * n, name=f"rs{tag}_chips_start")
        return dict(send=send, recv=recv, thru=thru, n=n, tag=tag, parts=parts), token[0, 0]

    def from_chips(st, after):
        n, tag = st['n'], st['tag']
        thru = _copies_wait(st['send'], st['recv'], st['thru'], n, _chips_plan, after, name=f"rs{tag}_chips_wait")
        return [_add_shards(p, g, shard_idx, name=f"rs{tag}_add_shards_{k}")
                for k, ((p, _), g) in enumerate(zip(st['parts'], thru[n:]))]

    dx = ffn_bwd(dx, saved[5], 2)
    dx, dil_g = dil_bwd(dx, saved[4], 1, 1)
    dx = ffn_bwd(dx, saved[3], 0)
    st1, tok = to_sibling([*ffn_units[1, 1], *dil_g['units'], *ffn_units[1, 0]], "1")
    dx = ffn_bwd(dx, saved[2], 2, tie=tok)
    st1, tok1 = to_chips(from_sibling(st1, dx), "1")
    st2, tok2 = to_sibling(ffn_units[0, 1], "2")
    dx, mla_g = mla_bwd(dx, saved[1], 0, 1, tie=tok1 + tok2)
    reds1 = from_chips(st1, dx)
    st2, tok = to_chips(from_sibling(st2, dx), "2")
    st3, tok3 = to_sibling(mla_g['units'], "3")
    onward = {}

    def mixer_to_chips(after):
        onward['st'], t = to_chips(from_sibling(st3, after), "3")
        return t

    dx = ffn_bwd(dx, saved[0], 0, tie=tok + tok3, mid=mixer_to_chips)
    reds2 = from_chips(st2, dx)
    reds3 = from_chips(onward['st'], dx)
    grad_x = dx[None]

    pad_row = lambda v: jnp.pad(v.reshape(-1), (0, (-v.size) % D_MODEL)).reshape(-1, D_MODEL)
    small = jnp.concatenate(
        [jnp.concatenate([dmod[i][r] for i in range(2) for r in range(9)], axis=0),
         jnp.concatenate([dpre[i][s] for i in range(2) for s in range(3)], axis=0),
         jnp.concatenate([dpost[i][s] for i in range(2) for s in range(3)], axis=0),
         pad_row(mla_g['q_norm']), pad_row(mla_g['kv_norm']), pad_row(dil_g['rel_bias']), pad_row(loss_part)], axis=0)
    small = jnp.pad(small, ((0, SMALL_ROWS - small.shape[0]), (0, 0)))
    small_all = _all_gather(small, name="ag_small_grads", in_vmem=True)
    small_sum = _sum_devices(small_all, 8, name="sum_small_grads")
    g_b_mod = small_sum[0:18].reshape(2, 9 * D_MODEL)
    my_cols = lambda t: lax.dynamic_slice_in_dim(t, shard_id * 256, 256, axis=2)
    g_norm_pre = my_cols(small_sum[18:24].reshape(2, 3, D_MODEL))
    g_norm_post = my_cols(small_sum[24:30].reshape(2, 3, D_MODEL))
    g_q_norm = small_sum[30, :Q_LORA].reshape(1, Q_LORA)
    g_kv_norm = small_sum[31, :KV_LORA].reshape(1, KV_LORA)
    g_rel_bias = small_sum[32:34].reshape(-1)[:N_BUCKETS * 48].reshape(N_BUCKETS, 48)
    loss = small_sum[34, 0]
    dmod_all = small_all.reshape(8, SMALL_ROWS, D_MODEL)[:, 0:18].reshape(8, 2, 9 * D_MODEL)
    dmod_cols = lax.dynamic_slice_in_dim(dmod_all, shard_id * 2304, 2304, axis=2).transpose(1, 0, 2)

    swap = lambda t: jnp.swapaxes(t, 2, 3)
    grads = dict(norm_pre=g_norm_pre, norm_post=g_norm_post, b_mod=g_b_mod, mla_q_norm=g_q_norm,
                 mla_kv_norm=g_kv_norm, rel_bias=g_rel_bias)
    deltas, new_m, new_v = {}, {}, {}

    def adamw(names):
        for n in names:
            view = swap if n in ('ffn_w_gate', 'ffn_w_up') else (lambda t: t)
            outs = _adamw(view(given[n]), view(grads[n]), view(given["m_" + n]), view(given["v_" + n]),
                          name=f"adamw_{n}")
            deltas[n], new_m[n], new_v[n] = (view(t) for t in outs)

    st0, tok = to_sibling(ffn_units[0, 0], "0")
    grads['w_mod'] = _mm(silu_c, (dmod_cols + tok).astype(BF16), ta=True, tn_cap=768, name="w_mod_grad")
    adamw(['w_mod'])
    st0, tok = to_chips(from_sibling(st0, deltas['w_mod']), "0")
    grads['b_mod'] = grads['b_mod'] + tok
    fin = _pair_gather(reds1 + reds2 + reds3, name="rs_pair_gather")
    for n, t in zip(['dil_w_in', 'dil_w_o', 'mla_w_in', 'mla_w_q_up', 'mla_w_kv_up', 'mla_w_o'], fin[2:4] + fin[8:12]):
        grads[n] = t.reshape(given[n].shape)
    adamw(['b_mod', 'dil_w_in', 'dil_w_o', 'mla_w_in', 'mla_w_q_up', 'mla_w_kv_up', 'mla_w_o', 'norm_pre', 'norm_post',
           'mla_q_norm', 'mla_kv_norm', 'rel_bias'])
    reds0 = from_chips(st0, deltas['dil_w_in'])
    fin0 = _pair_gather(reds0, name="rs_pair_gather_last")
    ffn_fin = {(1, 1): fin[0:2], (1, 0): fin[4:6], (0, 1): fin[6:8], (0, 0): fin0}
    per_ffn = lambda pick: jnp.stack([jnp.stack([pick(*ffn_fin[i, h]) for h in range(2)]) for i in range(2)])
    grads.update(ffn_w_gate=swap(per_ffn(lambda gu, dn: gu[0])), ffn_w_up=swap(per_ffn(lambda gu, dn: gu[1])),
                 ffn_w_down=per_ffn(lambda gu, dn: jnp.concatenate([dn[0], dn[1]], axis=1)))
    adamw(['ffn_w_gate', 'ffn_w_up', 'ffn_w_down'])
    return (loss, grad_x, *[grads[n] for n in WEIGHTS], *[deltas[n] for n in WEIGHTS],
            *[new_m[n] for n in WEIGHTS], *[new_v[n] for n in WEIGHTS])
```
